```python
import math
import jax, jax.numpy as jnp
from jax import lax
import numpy as np

D_MODEL = 1024
BATCH = 8
SEQ = 4096
DEPTH = 1

EPS = 1e-6
D_FF = 2816
CONV_CH = D_MODEL
CONV_WIDTH = 31
HEAD_DIM = 64
N_Q_HEADS = 16
N_KV_HEADS = 4
GROUP = N_Q_HEADS // N_KV_HEADS
WINDOW = 128
BLOCK = WINDOW
N_BUCKETS = 32
MAX_DISTANCE = 128
Q_W = N_Q_HEADS * HEAD_DIM
KV_W = N_KV_HEADS * HEAD_DIM
SPLITS = (2 * CONV_CH, Q_W, KV_W, KV_W, D_MODEL, D_MODEL)
IN_W = sum(SPLITS)

kernel_name = "hybrid_conformer_conv_swa_sink_macaron"


def rmsnorm(x, g):
    xf = x.astype(jnp.float32)
    y = xf * lax.rsqrt(jnp.mean(xf * xf, axis=-1, keepdims=True) + EPS)
    return (y * g.astype(jnp.float32)).astype(x.dtype)


def layernorm(x, g, b):
    xf = x.astype(jnp.float32)
    mu = jnp.mean(xf, axis=-1, keepdims=True)
    var = jnp.mean(jnp.square(xf - mu), axis=-1, keepdims=True)
    y = (xf - mu) * lax.rsqrt(var + EPS)
    return (y * g.astype(jnp.float32) + b.astype(jnp.float32)).astype(x.dtype)


def swiglu_ffn(x, w_in, w_out):
    a, b = jnp.split(x @ w_in, 2, axis=-1)
    return (jax.nn.silu(a) * b) @ w_out


def t5_causal_bucket(dist):
    max_exact = N_BUCKETS // 2
    d = jnp.maximum(dist, 1).astype(jnp.float32)
    large = max_exact + (jnp.log(d / max_exact) / math.log(MAX_DISTANCE / max_exact)
                         * (N_BUCKETS - max_exact)).astype(jnp.int32)
    large = jnp.minimum(large, N_BUCKETS - 1)
    return jnp.where(dist < max_exact, dist, large)


def conformer_conv(u, dw_kernel, dw_bias, ln_g, ln_b, w_proj):
    a, g = jnp.split(u, 2, axis=-1)
    z = a * jax.nn.sigmoid(g)
    z = lax.conv_general_dilated(
        z, dw_kernel[:, None, :].astype(z.dtype),
        window_strides=(1,), padding=((CONV_WIDTH - 1, 0),),
        dimension_numbers=("NWC", "WIO", "NWC"),
        feature_group_count=CONV_CH) + dw_bias
    z = jax.nn.silu(layernorm(z, ln_g, ln_b))
    return z @ w_proj


def sliding_window_gqa(q, k, v, q_norm_g, k_norm_g, sinks, rel_bias, w_o):
    B, S = q.shape[0], q.shape[1]
    nb = S // BLOCK
    q = rmsnorm(q, q_norm_g)
    k = rmsnorm(k, k_norm_g)
    qb = q.reshape(B, nb, BLOCK, N_KV_HEADS, GROUP, HEAD_DIM)
    kb = k.reshape(B, nb, BLOCK, N_KV_HEADS, HEAD_DIM)
    vb = v.reshape(B, nb, BLOCK, N_KV_HEADS, HEAD_DIM)
    kpad = jnp.zeros_like(kb[:, :1])
    vpad = jnp.zeros_like(vb[:, :1])
    kw = jnp.concatenate([jnp.concatenate([kpad, kb[:, :-1]], 1), kb], axis=2)
    vw = jnp.concatenate([jnp.concatenate([vpad, vb[:, :-1]], 1), vb], axis=2)

    s = jnp.einsum("bnqhgd,bnkhd->bnhgqk", qb, kw).astype(jnp.float32)
    s = s * (1.0 / math.sqrt(HEAD_DIM))

    qi = jnp.arange(BLOCK, dtype=jnp.int32)[:, None]
    kj = jnp.arange(2 * BLOCK, dtype=jnp.int32)[None, :]
    dist = qi + BLOCK - kj
    in_win = (dist >= 0) & (dist < WINDOW)
    bucket = t5_causal_bucket(jnp.maximum(dist, 0))
    bias = rel_bias.astype(jnp.float32)[bucket]
    bias = jnp.transpose(bias, (2, 0, 1)).reshape(N_KV_HEADS, GROUP, BLOCK, 2 * BLOCK)
    key_pos = jnp.arange(nb, dtype=jnp.int32)[:, None] * BLOCK - BLOCK + kj
    valid = in_win[None] & (key_pos >= 0)[:, None, :]

    s = s + bias[None, None]
    s = jnp.where(valid[None, :, None, None], s, jnp.finfo(jnp.float32).min)
    sink = sinks.astype(jnp.float32).reshape(N_KV_HEADS, GROUP)[None, None, :, :, None, None]
    m = jnp.maximum(jnp.max(s, axis=-1, keepdims=True), sink)
    p = jnp.exp(s - m)
    p = p / (jnp.sum(p, axis=-1, keepdims=True) + jnp.exp(sink - m))
    o = jnp.einsum("bnhgqk,bnkhd->bnqhgd", p.astype(vw.dtype), vw)
    return o.reshape(B, S, Q_W) @ w_o


def _fwd_setup_inputs(seed: int = 0) -> dict:
    key = jax.random.key(seed)
    ks = jax.random.split(key, 24)
    f32 = jnp.float32

    def w(k, shape, fan_in):
        return jax.random.normal(k, shape, f32) * (fan_in ** -0.5)

    def gain(k, n):
        return jnp.ones((n,), f32) + 0.01 * jax.random.normal(k, (n,), f32)

    return {
        "x": jax.random.normal(ks[0], (BATCH, SEQ, D_MODEL), f32),
        "ffn1_norm": gain(ks[1], D_MODEL),
        "ffn1_w_in": w(ks[2], (D_MODEL, 2 * D_FF), D_MODEL),
        "ffn1_w_out": w(ks[3], (D_FF, D_MODEL), D_FF),
        "mix_norm": gain(ks[4], D_MODEL),
        "w_in": w(ks[5], (D_MODEL, IN_W), D_MODEL),
        "conv_dw_kernel": w(ks[6], (CONV_WIDTH, CONV_CH), CONV_WIDTH),
        "conv_dw_bias": 0.01 * jax.random.normal(ks[7], (CONV_CH,), f32),
        "conv_ln_g": gain(ks[8], CONV_CH),
        "conv_ln_b": 0.01 * jax.random.normal(ks[9], (CONV_CH,), f32),
        "conv_w_proj": w(ks[10], (CONV_CH, D_MODEL), CONV_CH),
        "q_norm": gain(ks[11], HEAD_DIM),
        "k_norm": gain(ks[12], HEAD_DIM),
        "attn_sinks": jax.random.normal(ks[13], (N_Q_HEADS,), f32),
        "rel_bias": 0.1 * jax.random.normal(ks[14], (N_BUCKETS, N_Q_HEADS), f32),
        "attn_w_o": w(ks[15], (Q_W, D_MODEL), Q_W),
        "w_out": w(ks[16], (D_MODEL, D_MODEL), D_MODEL),
        "ffn2_norm": gain(ks[17], D_MODEL),
        "ffn2_w_in": w(ks[18], (D_MODEL, 2 * D_FF), D_MODEL),
        "ffn2_w_out": w(ks[19], (D_FF, D_MODEL), D_FF),
    }


def _fwd_reference(x, ffn1_norm, ffn1_w_in, ffn1_w_out, mix_norm, w_in,
              conv_dw_kernel, conv_dw_bias, conv_ln_g, conv_ln_b, conv_w_proj,
              q_norm, k_norm, attn_sinks, rel_bias, attn_w_o, w_out,
              ffn2_norm, ffn2_w_in, ffn2_w_out):
    B, S = x.shape[0], x.shape[1]
    for _ in range(DEPTH):
        x = x + 0.5 * swiglu_ffn(rmsnorm(x, ffn1_norm), ffn1_w_in, ffn1_w_out)

        h = rmsnorm(x, mix_norm)
        idx = np.cumsum(SPLITS)[:-1].tolist()
        u_conv, q, k, v, g_conv, g_attn = jnp.split(h @ w_in, idx, axis=-1)
        a = conformer_conv(u_conv, conv_dw_kernel, conv_dw_bias, conv_ln_g, conv_ln_b, conv_w_proj)
        b = sliding_window_gqa(q.reshape(B, S, N_Q_HEADS, HEAD_DIM),
                               k.reshape(B, S, N_KV_HEADS, HEAD_DIM),
                               v.reshape(B, S, N_KV_HEADS, HEAD_DIM),
                               q_norm, k_norm, attn_sinks, rel_bias, attn_w_o)
        merged = jax.nn.sigmoid(g_conv) * a + jax.nn.sigmoid(g_attn) * b
        x = x + merged @ w_out

        x = x + 0.5 * swiglu_ffn(rmsnorm(x, ffn2_norm), ffn2_w_in, ffn2_w_out)
    return x


import jax as _jax
import jax.numpy as _jnp

TWIN_FORMAT = 'train_step'
FWD_PARAMS = ['x', 'ffn1_norm', 'ffn1_w_in', 'ffn1_w_out', 'mix_norm', 'w_in', 'conv_dw_kernel', 'conv_dw_bias', 'conv_ln_g', 'conv_ln_b', 'conv_w_proj', 'q_norm', 'k_norm', 'attn_sinks', 'rel_bias', 'attn_w_o', 'w_out', 'ffn2_norm', 'ffn2_w_in', 'ffn2_w_out']
TWIN_WEIGHTS = ['ffn1_norm', 'ffn1_w_in', 'ffn1_w_out', 'mix_norm', 'w_in', 'conv_dw_kernel', 'conv_dw_bias', 'conv_ln_g', 'conv_ln_b', 'conv_w_proj', 'q_norm', 'k_norm', 'attn_sinks', 'rel_bias', 'attn_w_o', 'w_out', 'ffn2_norm', 'ffn2_w_in', 'ffn2_w_out']
TWIN_DIFF_INPUT = 'x'
TWIN_INPUTS = ['x', 'ffn1_norm', 'ffn1_w_in', 'ffn1_w_out', 'mix_norm', 'w_in', 'conv_dw_kernel', 'conv_dw_bias', 'conv_ln_g', 'conv_ln_b', 'conv_w_proj', 'q_norm', 'k_norm', 'attn_sinks', 'rel_bias', 'attn_w_o', 'w_out', 'ffn2_norm', 'ffn2_w_in', 'ffn2_w_out', 'loss_target', 'm_ffn1_norm', 'm_ffn1_w_in', 'm_ffn1_w_out', 'm_mix_norm', 'm_w_in', 'm_conv_dw_kernel', 'm_conv_dw_bias', 'm_conv_ln_g', 'm_conv_ln_b', 'm_conv_w_proj', 'm_q_norm', 'm_k_norm', 'm_attn_sinks', 'm_rel_bias', 'm_attn_w_o', 'm_w_out', 'm_ffn2_norm', 'm_ffn2_w_in', 'm_ffn2_w_out', 'v_ffn1_norm', 'v_ffn1_w_in', 'v_ffn1_w_out', 'v_mix_norm', 'v_w_in', 'v_conv_dw_kernel', 'v_conv_dw_bias', 'v_conv_ln_g', 'v_conv_ln_b', 'v_conv_w_proj', 'v_q_norm', 'v_k_norm', 'v_attn_sinks', 'v_rel_bias', 'v_attn_w_o', 'v_w_out', 'v_ffn2_norm', 'v_ffn2_w_in', 'v_ffn2_w_out']
TWIN_OUTPUTS = ['loss', 'grad_x', 'grad_ffn1_norm', 'grad_ffn1_w_in', 'grad_ffn1_w_out', 'grad_mix_norm', 'grad_w_in', 'grad_conv_dw_kernel', 'grad_conv_dw_bias', 'grad_conv_ln_g', 'grad_conv_ln_b', 'grad_conv_w_proj', 'grad_q_norm', 'grad_k_norm', 'grad_attn_sinks', 'grad_rel_bias', 'grad_attn_w_o', 'grad_w_out', 'grad_ffn2_norm', 'grad_ffn2_w_in', 'grad_ffn2_w_out', 'delta_ffn1_norm', 'delta_ffn1_w_in', 'delta_ffn1_w_out', 'delta_mix_norm', 'delta_w_in', 'delta_conv_dw_kernel', 'delta_conv_dw_bias', 'delta_conv_ln_g', 'delta_conv_ln_b', 'delta_conv_w_proj', 'delta_q_norm', 'delta_k_norm', 'delta_attn_sinks', 'delta_rel_bias', 'delta_attn_w_o', 'delta_w_out', 'delta_ffn2_norm', 'delta_ffn2_w_in', 'delta_ffn2_w_out', 'new_m_ffn1_norm', 'new_m_ffn1_w_in', 'new_m_ffn1_w_out', 'new_m_mix_norm', 'new_m_w_in', 'new_m_conv_dw_kernel', 'new_m_conv_dw_bias', 'new_m_conv_ln_g', 'new_m_conv_ln_b', 'new_m_conv_w_proj', 'new_m_q_norm', 'new_m_k_norm', 'new_m_attn_sinks', 'new_m_rel_bias', 'new_m_attn_w_o', 'new_m_w_out', 'new_m_ffn2_norm', 'new_m_ffn2_w_in', 'new_m_ffn2_w_out', 'new_v_ffn1_norm', 'new_v_ffn1_w_in', 'new_v_ffn1_w_out', 'new_v_mix_norm', 'new_v_w_in', 'new_v_conv_dw_kernel', 'new_v_conv_dw_bias', 'new_v_conv_ln_g', 'new_v_conv_ln_b', 'new_v_conv_w_proj', 'new_v_q_norm', 'new_v_k_norm', 'new_v_attn_sinks', 'new_v_rel_bias', 'new_v_attn_w_o', 'new_v_w_out', 'new_v_ffn2_norm', 'new_v_ffn2_w_in', 'new_v_ffn2_w_out']
TWIN_LEAF_KINDS = {'loss': 'loss', 'grad_x': 'grad_x', 'grad_ffn1_norm': 'grad_w', 'grad_ffn1_w_in': 'grad_w', 'grad_ffn1_w_out': 'grad_w', 'grad_mix_norm': 'grad_w', 'grad_w_in': 'grad_w', 'grad_conv_dw_kernel': 'grad_w', 'grad_conv_dw_bias': 'grad_w', 'grad_conv_ln_g': 'grad_w', 'grad_conv_ln_b': 'grad_w', 'grad_conv_w_proj': 'grad_w', 'grad_q_norm': 'grad_w', 'grad_k_norm': 'grad_w', 'grad_attn_sinks': 'grad_w', 'grad_rel_bias': 'grad_w', 'grad_attn_w_o': 'grad_w', 'grad_w_out': 'grad_w', 'grad_ffn2_norm': 'grad_w', 'grad_ffn2_w_in': 'grad_w', 'grad_ffn2_w_out': 'grad_w', 'delta_ffn1_norm': 'delta_w', 'delta_ffn1_w_in': 'delta_w', 'delta_ffn1_w_out': 'delta_w', 'delta_mix_norm': 'delta_w', 'delta_w_in': 'delta_w', 'delta_conv_dw_kernel': 'delta_w', 'delta_conv_dw_bias': 'delta_w', 'delta_conv_ln_g': 'delta_w', 'delta_conv_ln_b': 'delta_w', 'delta_conv_w_proj': 'delta_w', 'delta_q_norm': 'delta_w', 'delta_k_norm': 'delta_w', 'delta_attn_sinks': 'delta_w', 'delta_rel_bias': 'delta_w', 'delta_attn_w_o': 'delta_w', 'delta_w_out': 'delta_w', 'delta_ffn2_norm': 'delta_w', 'delta_ffn2_w_in': 'delta_w', 'delta_ffn2_w_out': 'delta_w', 'new_m_ffn1_norm': 'new_m', 'new_m_ffn1_w_in': 'new_m', 'new_m_ffn1_w_out': 'new_m', 'new_m_mix_norm': 'new_m', 'new_m_w_in': 'new_m', 'new_m_conv_dw_kernel': 'new_m', 'new_m_conv_dw_bias': 'new_m', 'new_m_conv_ln_g': 'new_m', 'new_m_conv_ln_b': 'new_m', 'new_m_conv_w_proj': 'new_m', 'new_m_q_norm': 'new_m', 'new_m_k_norm': 'new_m', 'new_m_attn_sinks': 'new_m', 'new_m_rel_bias': 'new_m', 'new_m_attn_w_o': 'new_m', 'new_m_w_out': 'new_m', 'new_m_ffn2_norm': 'new_m', 'new_m_ffn2_w_in': 'new_m', 'new_m_ffn2_w_out': 'new_m', 'new_v_ffn1_norm': 'new_v', 'new_v_ffn1_w_in': 'new_v', 'new_v_ffn1_w_out': 'new_v', 'new_v_mix_norm': 'new_v', 'new_v_w_in': 'new_v', 'new_v_conv_dw_kernel': 'new_v', 'new_v_conv_dw_bias': 'new_v', 'new_v_conv_ln_g': 'new_v', 'new_v_conv_ln_b': 'new_v', 'new_v_conv_w_proj': 'new_v', 'new_v_q_norm': 'new_v', 'new_v_k_norm': 'new_v', 'new_v_attn_sinks': 'new_v', 'new_v_rel_bias': 'new_v', 'new_v_attn_w_o': 'new_v', 'new_v_w_out': 'new_v', 'new_v_ffn2_norm': 'new_v', 'new_v_ffn2_w_in': 'new_v', 'new_v_ffn2_w_out': 'new_v'}


def _forward(args):
    return _fwd_reference(*[args[k] for k in FWD_PARAMS])


def _output_shape():
    out = _jax.eval_shape(lambda: _forward(_fwd_setup_inputs(0)))
    return out.shape, out.dtype

N_MICROBATCH = 1
ADAM_LR = 0.001
ADAM_B1 = 0.9
ADAM_B2 = 0.999
ADAM_EPS = 1e-08
ADAM_WD = 0.01
ADAM_STEP = 10
PER_EXAMPLE_BATCH_AXIS = {'x': 0, 'loss_target': 0}
SHARED_INPUTS = []
_WEIGHT_DTYPES = {'ffn1_norm': _jnp.float32, 'ffn1_w_in': _jnp.float32, 'ffn1_w_out': _jnp.float32, 'mix_norm': _jnp.float32, 'w_in': _jnp.float32, 'conv_dw_kernel': _jnp.float32, 'conv_dw_bias': _jnp.float32, 'conv_ln_g': _jnp.float32, 'conv_ln_b': _jnp.float32, 'conv_w_proj': _jnp.float32, 'q_norm': _jnp.float32, 'k_norm': _jnp.float32, 'attn_sinks': _jnp.float32, 'rel_bias': _jnp.float32, 'attn_w_o': _jnp.float32, 'w_out': _jnp.float32, 'ffn2_norm': _jnp.float32, 'ffn2_w_in': _jnp.float32, 'ffn2_w_out': _jnp.float32}
MOMENT_SCALE = {'ffn1_norm': 6.103824e+00, 'ffn1_w_in': 6.683505e-02, 'ffn1_w_out': 1.134921e-01, 'mix_norm': 3.940150e-01, 'w_in': 5.620276e-02, 'conv_dw_kernel': 1.400386e-01, 'conv_dw_bias': 2.087161e+00, 'conv_ln_g': 4.010857e+00, 'conv_ln_b': 2.746408e+00, 'conv_w_proj': 4.083576e-01, 'q_norm': 1.782725e+00, 'k_norm': 1.779693e+00, 'attn_sinks': 2.684736e-01, 'rel_bias': 4.341654e-02, 'attn_w_o': 3.792490e-02, 'w_out': 2.625969e-01, 'ffn2_norm': 6.091488e+00, 'ffn2_w_in': 7.011818e-02, 'ffn2_w_out': 1.110944e-01}


def _to_microbatches(a, axis):
    t = _jnp.moveaxis(a, axis, 0)
    t = t.reshape((N_MICROBATCH, t.shape[0] // N_MICROBATCH) + t.shape[1:])
    return _jnp.moveaxis(t, 1, axis + 1)


def setup_inputs(seed: int = 0) -> dict:
    inp = _fwd_setup_inputs(seed)
    key = _jax.random.fold_in(_jax.random.key(seed), 7919)
    shape, _ = _output_shape()
    out = dict(inp)
    out["loss_target"] = _jax.random.normal(_jax.random.fold_in(key, 0), shape, _jnp.float32)
    for i, name in enumerate(TWIN_WEIGHTS):
        w = inp[name].astype(_jnp.float32)
        if MOMENT_SCALE is None:
            s = _jnp.sqrt(_jnp.mean(_jnp.square(w)) + 1e-30)
        else:
            s = MOMENT_SCALE[name]
        km, kv = _jax.random.split(_jax.random.fold_in(key, i + 1))
        out[name] = w
        out["m_" + name] = s * _jax.random.normal(km, w.shape, _jnp.float32)
        out["v_" + name] = (s * s) * _jax.random.uniform(kv, w.shape, _jnp.float32, 0.5, 1.5)
    if N_MICROBATCH > 1:
        for name, axis in PER_EXAMPLE_BATCH_AXIS.items():
            out[name] = _to_microbatches(out[name], axis)
    return {'x': out['x'], 'ffn1_norm': out['ffn1_norm'], 'ffn1_w_in': out['ffn1_w_in'], 'ffn1_w_out': out['ffn1_w_out'], 'mix_norm': out['mix_norm'], 'w_in': out['w_in'], 'conv_dw_kernel': out['conv_dw_kernel'], 'conv_dw_bias': out['conv_dw_bias'], 'conv_ln_g': out['conv_ln_g'], 'conv_ln_b': out['conv_ln_b'], 'conv_w_proj': out['conv_w_proj'], 'q_norm': out['q_norm'], 'k_norm': out['k_norm'], 'attn_sinks': out['attn_sinks'], 'rel_bias': out['rel_bias'], 'attn_w_o': out['attn_w_o'], 'w_out': out['w_out'], 'ffn2_norm': out['ffn2_norm'], 'ffn2_w_in': out['ffn2_w_in'], 'ffn2_w_out': out['ffn2_w_out'], 'loss_target': out['loss_target'], 'm_ffn1_norm': out['m_ffn1_norm'], 'm_ffn1_w_in': out['m_ffn1_w_in'], 'm_ffn1_w_out': out['m_ffn1_w_out'], 'm_mix_norm': out['m_mix_norm'], 'm_w_in': out['m_w_in'], 'm_conv_dw_kernel': out['m_conv_dw_kernel'], 'm_conv_dw_bias': out['m_conv_dw_bias'], 'm_conv_ln_g': out['m_conv_ln_g'], 'm_conv_ln_b': out['m_conv_ln_b'], 'm_conv_w_proj': out['m_conv_w_proj'], 'm_q_norm': out['m_q_norm'], 'm_k_norm': out['m_k_norm'], 'm_attn_sinks': out['m_attn_sinks'], 'm_rel_bias': out['m_rel_bias'], 'm_attn_w_o': out['m_attn_w_o'], 'm_w_out': out['m_w_out'], 'm_ffn2_norm': out['m_ffn2_norm'], 'm_ffn2_w_in': out['m_ffn2_w_in'], 'm_ffn2_w_out': out['m_ffn2_w_out'], 'v_ffn1_norm': out['v_ffn1_norm'], 'v_ffn1_w_in': out['v_ffn1_w_in'], 'v_ffn1_w_out': out['v_ffn1_w_out'], 'v_mix_norm': out['v_mix_norm'], 'v_w_in': out['v_w_in'], 'v_conv_dw_kernel': out['v_conv_dw_kernel'], 'v_conv_dw_bias': out['v_conv_dw_bias'], 'v_conv_ln_g': out['v_conv_ln_g'], 'v_conv_ln_b': out['v_conv_ln_b'], 'v_conv_w_proj': out['v_conv_w_proj'], 'v_q_norm': out['v_q_norm'], 'v_k_norm': out['v_k_norm'], 'v_attn_sinks': out['v_attn_sinks'], 'v_rel_bias': out['v_rel_bias'], 'v_attn_w_o': out['v_attn_w_o'], 'v_w_out': out['v_w_out'], 'v_ffn2_norm': out['v_ffn2_norm'], 'v_ffn2_w_in': out['v_ffn2_w_in'], 'v_ffn2_w_out': out['v_ffn2_w_out']}


def _loss(weights, diff, rest, loss_target):
    with _jax.named_scope("forward"):
        args = {**rest, TWIN_DIFF_INPUT: diff, **{k: w.astype(_WEIGHT_DTYPES[k]) for k, w in weights.items()}}
        y = _forward(args)
    with _jax.named_scope("loss_head"):
        err = _jnp.square(y.astype(_jnp.float32) - loss_target)
        return 0.5 * _jnp.sum(_jnp.mean(err, axis=-1)) if err.ndim else 0.5 * err


def _adamw(w, g, m, v):
    m = ADAM_B1 * m + (1.0 - ADAM_B1) * g
    v = ADAM_B2 * v + (1.0 - ADAM_B2) * _jnp.square(g)
    m_hat = m / (1.0 - ADAM_B1 ** ADAM_STEP)
    v_hat = v / (1.0 - ADAM_B2 ** ADAM_STEP)
    delta = -ADAM_LR * (m_hat / (_jnp.sqrt(v_hat) + ADAM_EPS) + ADAM_WD * w)
    return delta, m, v


def reference(x, ffn1_norm, ffn1_w_in, ffn1_w_out, mix_norm, w_in, conv_dw_kernel, conv_dw_bias, conv_ln_g, conv_ln_b, conv_w_proj, q_norm, k_norm, attn_sinks, rel_bias, attn_w_o, w_out, ffn2_norm, ffn2_w_in, ffn2_w_out, loss_target, m_ffn1_norm, m_ffn1_w_in, m_ffn1_w_out, m_mix_norm, m_w_in, m_conv_dw_kernel, m_conv_dw_bias, m_conv_ln_g, m_conv_ln_b, m_conv_w_proj, m_q_norm, m_k_norm, m_attn_sinks, m_rel_bias, m_attn_w_o, m_w_out, m_ffn2_norm, m_ffn2_w_in, m_ffn2_w_out, v_ffn1_norm, v_ffn1_w_in, v_ffn1_w_out, v_mix_norm, v_w_in, v_conv_dw_kernel, v_conv_dw_bias, v_conv_ln_g, v_conv_ln_b, v_conv_w_proj, v_q_norm, v_k_norm, v_attn_sinks, v_rel_bias, v_attn_w_o, v_w_out, v_ffn2_norm, v_ffn2_w_in, v_ffn2_w_out):
    given = dict(x=x, ffn1_norm=ffn1_norm, ffn1_w_in=ffn1_w_in, ffn1_w_out=ffn1_w_out, mix_norm=mix_norm, w_in=w_in, conv_dw_kernel=conv_dw_kernel, conv_dw_bias=conv_dw_bias, conv_ln_g=conv_ln_g, conv_ln_b=conv_ln_b, conv_w_proj=conv_w_proj, q_norm=q_norm, k_norm=k_norm, attn_sinks=attn_sinks, rel_bias=rel_bias, attn_w_o=attn_w_o, w_out=w_out, ffn2_norm=ffn2_norm, ffn2_w_in=ffn2_w_in, ffn2_w_out=ffn2_w_out, loss_target=loss_target, m_ffn1_norm=m_ffn1_norm, m_ffn1_w_in=m_ffn1_w_in, m_ffn1_w_out=m_ffn1_w_out, m_mix_norm=m_mix_norm, m_w_in=m_w_in, m_conv_dw_kernel=m_conv_dw_kernel, m_conv_dw_bias=m_conv_dw_bias, m_conv_ln_g=m_conv_ln_g, m_conv_ln_b=m_conv_ln_b, m_conv_w_proj=m_conv_w_proj, m_q_norm=m_q_norm, m_k_norm=m_k_norm, m_attn_sinks=m_attn_sinks, m_rel_bias=m_rel_bias, m_attn_w_o=m_attn_w_o, m_w_out=m_w_out, m_ffn2_norm=m_ffn2_norm, m_ffn2_w_in=m_ffn2_w_in, m_ffn2_w_out=m_ffn2_w_out, v_ffn1_norm=v_ffn1_norm, v_ffn1_w_in=v_ffn1_w_in, v_ffn1_w_out=v_ffn1_w_out, v_mix_norm=v_mix_norm, v_w_in=v_w_in, v_conv_dw_kernel=v_conv_dw_kernel, v_conv_dw_bias=v_conv_dw_bias, v_conv_ln_g=v_conv_ln_g, v_conv_ln_b=v_conv_ln_b, v_conv_w_proj=v_conv_w_proj, v_q_norm=v_q_norm, v_k_norm=v_k_norm, v_attn_sinks=v_attn_sinks, v_rel_bias=v_rel_bias, v_attn_w_o=v_attn_w_o, v_w_out=v_w_out, v_ffn2_norm=v_ffn2_norm, v_ffn2_w_in=v_ffn2_w_in, v_ffn2_w_out=v_ffn2_w_out)
    weights = {n: given[n] for n in TWIN_WEIGHTS}
    shared = {n: given[n] for n in SHARED_INPUTS}
    per_example = {n: given[n] for n in ['x']}
    grad_fn = _jax.value_and_grad(_loss, argnums=(0, 1))

    def one_microbatch(ex, loss_target):
        ex = dict(ex)
        diff = ex.pop(TWIN_DIFF_INPUT)
        return grad_fn(weights, diff, {**shared, **ex}, loss_target)

    if N_MICROBATCH == 1:
        loss, (grad_w, grad_x) = one_microbatch(per_example, given["loss_target"])
    else:
        def body(carry, xs):
            loss_sum, grad_sum = carry
            l_k, (gw_k, gx_k) = one_microbatch(xs[0], xs[1])
            with _jax.named_scope("update"):
                return (loss_sum + l_k, _jax.tree.map(_jnp.add, grad_sum, gw_k)), gx_k

        init = (_jnp.zeros((), _jnp.float32), _jax.tree.map(_jnp.zeros_like, weights))
        (loss, grad_w), grad_x = _jax.lax.scan(body, init, (per_example, given["loss_target"]))
    with _jax.named_scope("update"):
        delta_w, new_m, new_v = {}, {}, {}
        for n in TWIN_WEIGHTS:
            delta_w[n], new_m[n], new_v[n] = _adamw(weights[n], grad_w[n], given["m_" + n], given["v_" + n])
    return (loss, grad_x, *[grad_w[n] for n in TWIN_WEIGHTS], *[delta_w[n] for n in TWIN_WEIGHTS],
            *[new_m[n] for n in TWIN_WEIGHTS], *[new_v[n] for n in TWIN_WEIGHTS])
```

```python
import functools
import math

import numpy as np
import jax
import jax.numpy as jnp
from jax import lax
from jax.experimental import pallas as pl
from jax.experimental.pallas import tpu as pltpu

F32 = jnp.float32
BF = jnp.bfloat16

D = 1024
F = 2816
INW = 5632
CW = 31
CWP = 32
HD = 64
NQ = 16
NKV = 4
GRP = NQ // NKV
BLK = 128
NBUCKET = 32
EPS = 1e-6
NEG = float(jnp.finfo(jnp.float32).min)
QK_SCALE = 1.0 / math.sqrt(HD)
R_CONV = (0, 2048)
R_QKV = (2048, 3584)
R_Q = (2048, 3072)
R_KV = (3072, 3584)
R_GATE = (3584, 5632)

N_DEV = 8
VMEM_LIMIT_V7X = 56 * 1024 * 1024

ADAM_LR = 0.001
ADAM_B1 = 0.9
ADAM_B2 = 0.999
ADAM_EPS = 1e-08
ADAM_WD = 0.01
ADAM_STEP = 10

NT_DIMS = (((1,), (1,)), ((), ()))
TN_DIMS = (((0,), (0,)), ((), ()))


def _dot(a, b):
    return jnp.dot(a, b, preferred_element_type=F32)


def _dot_nt(a, b):
    return lax.dot_general(a, b, NT_DIMS, preferred_element_type=F32)


def _dot_tn(a, b):
    return lax.dot_general(a, b, TN_DIMS, preferred_element_type=F32)


def _sig(x):
    return 1.0 / (1.0 + jnp.exp(-x))


def _params(n_axes):
    return pltpu.CompilerParams(dimension_semantics=("arbitrary",) * n_axes, vmem_limit_bytes=VMEM_LIMIT_V7X)


def _resident(shape):
    zeros = (0,) * len(shape)
    return pl.BlockSpec(shape, lambda *_: zeros, pipeline_mode=pl.Buffered(1))


def _row_tile(rows, cols):
    return pl.BlockSpec((rows, cols), lambda i: (i, 0))


def _rms_stats(x):
    r = lax.rsqrt(jnp.mean(x * x, axis=-1, keepdims=True) + EPS)
    return r, x * r


def _rms_bwd(dn, x, g):
    r, xh = _rms_stats(x)
    dxh = dn * g
    dx = r * (dxh - xh * jnp.mean(dxh * xh, axis=-1, keepdims=True))
    return dx, jnp.sum(dn * xh, axis=0, keepdims=True)


def _ffn_fwd(x, g, w_in_t, w_out, name, target=None):
    t = x.shape[0]
    tm = min(256, t)
    with_loss = target is not None

    def body(*refs):
        if with_loss:
            x_ref, g_ref, w_ref, wo_ref, t_ref, n_ref, u_ref, dy_ref, sq_ref = refs
        else:
            x_ref, g_ref, w_ref, wo_ref, n_ref, u_ref, xo_ref = refs
        x = x_ref[...]
        r, xh = _rms_stats(x)
        n = (xh * g_ref[...]).astype(BF)
        n_ref[...] = n
        u = _dot_nt(n, w_ref[...])
        u_ref[...] = u.astype(BF)
        a = u[:, :F]
        b = u[:, F:]
        h = (a * _sig(a) * b).astype(BF)
        xo = x + 0.5 * _dot(h, wo_ref[...])
        if with_loss:
            err = xo - t_ref[...]
            dy_ref[...] = err * (1.0 / D)

            @pl.when(pl.program_id(0) == 0)
            def _():
                sq_ref[...] = jnp.zeros_like(sq_ref)

            sq_ref[...] += jnp.sum(err * err, axis=0, keepdims=True)
        else:
            xo_ref[...] = xo

    in_specs = [_row_tile(tm, D), _resident((1, D)), _resident((INW, D)), _resident((F, D))]
    args = [x, g, w_in_t, w_out]
    out_specs = [_row_tile(tm, D), _row_tile(tm, INW), _row_tile(tm, D)]
    out_shape = [jax.ShapeDtypeStruct((t, D), BF), jax.ShapeDtypeStruct((t, INW), BF), jax.ShapeDtypeStruct((t, D), F32)]
    if with_loss:
        in_specs.append(_row_tile(tm, D))
        args.append(target)
        out_specs.append(pl.BlockSpec((1, D), lambda i: (0, 0)))
        out_shape.append(jax.ShapeDtypeStruct((1, D), F32))
    return pl.pallas_call(body, grid=(t // tm,), in_specs=in_specs, out_specs=out_specs, out_shape=out_shape,
                          compiler_params=_params(1), name=name)(*args)


def _ffn_bwd(dxo, x, g, u, w_in_t, w_out, name):
    t = x.shape[0]
    tm = min(256, t)

    def body(dxo_ref, x_ref, g_ref, u_ref, w_ref, wo_ref, dx_ref, du_ref, h_ref, dy_ref, dg_ref):
        dxo = dxo_ref[...]
        dy = (0.5 * dxo).astype(BF)
        dy_ref[...] = dy
        dh = _dot_nt(dy, wo_ref[...])
        a = u_ref[:, :F].astype(F32)
        b = u_ref[:, F:].astype(F32)
        s = _sig(a)
        sa = a * s
        h_ref[...] = (sa * b).astype(BF)
        du_ref[:, :F] = (dh * b * (s * (1.0 + a * (1.0 - s)))).astype(BF)
        du_ref[:, F:] = (dh * sa).astype(BF)
        dn = _dot(du_ref[...], w_ref[...])
        dx, dg = _rms_bwd(dn, x_ref[...], g_ref[...])
        dx_ref[...] = dxo + dx

        @pl.when(pl.program_id(0) == 0)
        def _():
            dg_ref[...] = jnp.zeros_like(dg_ref)

        dg_ref[...] += dg

    return pl.pallas_call(
        body, grid=(t // tm,),
        in_specs=[_row_tile(tm, D), _row_tile(tm, D), _resident((1, D)), _row_tile(tm, INW), _resident((INW, D)),
                  _resident((F, D))],
        out_specs=[_row_tile(tm, D), _row_tile(tm, INW), _row_tile(tm, F), _row_tile(tm, D),
                   pl.BlockSpec((1, D), lambda i: (0, 0))],
        out_shape=[jax.ShapeDtypeStruct((t, D), F32), jax.ShapeDtypeStruct((t, INW), BF), jax.ShapeDtypeStruct((t, F), BF),
                   jax.ShapeDtypeStruct((t, D), BF), jax.ShapeDtypeStruct((1, D), F32)],
        compiler_params=_params(1), name=name)(dxo, x, g, u, w_in_t, w_out)


def _wgrad(lhs, rhs, name, *, lhs_is_transposed, chunk):
    t = rhs.shape[0]
    n = lhs.shape[0] if lhs_is_transposed else lhs.shape[1]
    tm = min(512, t)
    c = min(chunk, n)
    n_tok = t // tm

    def body(l_ref, r_ref, o_ref, acc_ref):
        i = pl.program_id(1)

        @pl.when(i == 0)
        def _():
            acc_ref[...] = jnp.zeros_like(acc_ref)

        lhs_tile = l_ref[...].astype(BF)
        rhs_tile = r_ref[...].astype(BF)
        if lhs_is_transposed:
            acc_ref[...] += _dot(lhs_tile, rhs_tile)
        else:
            acc_ref[...] += _dot_tn(lhs_tile, rhs_tile)

        @pl.when(i == n_tok - 1)
        def _():
            o_ref[...] = acc_ref[...].astype(o_ref.dtype)

    if lhs_is_transposed:
        lhs_spec = pl.BlockSpec((c, tm), lambda j, i: (j, i))
    else:
        lhs_spec = pl.BlockSpec((tm, c), lambda j, i: (i, j))
    return pl.pallas_call(
        body, grid=(n // c, n_tok),
        in_specs=[lhs_spec, pl.BlockSpec((tm, D), lambda j, i: (i, 0))],
        out_specs=pl.BlockSpec((c, D), lambda j, i: (j, 0)),
        out_shape=jax.ShapeDtypeStruct((n, D), BF),
        scratch_shapes=[pltpu.VMEM((c, D), F32)],
        compiler_params=_params(2), name=name)(lhs, rhs)


def _mix_proj(x, g, w_t):
    t = x.shape[0]
    tm = min(256, t)

    def body(x_ref, g_ref, w_ref, hm_ref, uc_ref, gp_ref, qkv_ref):
        r, xh = _rms_stats(x_ref[...])
        hm = (xh * g_ref[...]).astype(BF)
        hm_ref[...] = hm
        uc_ref[...] = _dot_nt(hm, w_ref[R_CONV[0]:R_CONV[1], :]).astype(BF)
        gp_ref[...] = _dot_nt(hm, w_ref[R_GATE[0]:R_GATE[1], :]).astype(BF)
        qkv_ref[...] = _dot_nt(w_ref[R_QKV[0]:R_QKV[1], :], hm).astype(BF)

    return pl.pallas_call(
        body, grid=(t // tm,),
        in_specs=[_row_tile(tm, D), _resident((1, D)), _resident((INW, D))],
        out_specs=[_row_tile(tm, D), _row_tile(tm, 2 * D), _row_tile(tm, 2 * D), pl.BlockSpec((1536, tm), lambda i: (0, i))],
        out_shape=[jax.ShapeDtypeStruct((t, D), BF), jax.ShapeDtypeStruct((t, 2 * D), BF),
                   jax.ShapeDtypeStruct((t, 2 * D), BF), jax.ShapeDtypeStruct((1536, t), BF)],
        compiler_params=_params(1), name="mix_proj")(x, g, w_t)


CONV_HALO = 32
CONV_LEAD = CONV_HALO - (CW - 1)


def _glu(uc):
    uc = uc.astype(F32)
    return uc[:, :D] * _sig(uc[:, D:])


def _ln_stats(zc):
    mu = jnp.mean(zc, axis=-1, keepdims=True)
    zm = zc - mu
    r = lax.rsqrt(jnp.mean(zm * zm, axis=-1, keepdims=True) + EPS)
    return r, zm * r


def _conv_fwd(uc, dwk, dwb, lng, lnb):
    t = uc.shape[0]
    tm = min(512, t)
    per = tm // CONV_HALO

    def body(cur_ref, prev_ref, k_ref, kb_ref, g_ref, b_ref, o_ref, z_scr):
        i = pl.program_id(0)
        z_scr[0:CONV_HALO, :] = _glu(prev_ref[...]) * (i > 0).astype(F32)
        z_scr[CONV_HALO:, :] = _glu(cur_ref[...])
        acc = jnp.zeros((tm, D), F32) + kb_ref[...]
        for w in range(CW):
            acc = acc + k_ref[w:w + 1, :] * z_scr[pl.ds(CONV_LEAD + w, tm), :]
        r, xh = _ln_stats(acc)
        y = xh * g_ref[...] + b_ref[...]
        o_ref[...] = (y * _sig(y)).astype(BF)

    return pl.pallas_call(
        body, grid=(t // tm,),
        in_specs=[_row_tile(tm, 2 * D),
                  pl.BlockSpec((CONV_HALO, 2 * D), lambda i: (jnp.maximum(i * per - 1, 0), 0)),
                  _resident((CWP, D)), _resident((1, D)), _resident((1, D)), _resident((1, D))],
        out_specs=_row_tile(tm, D),
        out_shape=jax.ShapeDtypeStruct((t, D), BF),
        scratch_shapes=[pltpu.VMEM((tm + CONV_HALO, D), F32)],
        compiler_params=_params(1), name="conv_fwd")(uc, uc, dwk, dwb, lng, lnb)


def _conv_bwd(uc, dzs, dwk, dwb, lng, lnb):
    t = uc.shape[0]
    tm = min(512, t)
    per = tm // CONV_HALO
    n_tiles = t // tm
    ext = tm + CONV_HALO

    def body(cur_ref, prev_ref, next_ref, dz_ref, dzn_ref, k_ref, kb_ref, g_ref, b_ref,
             duc_ref, dk_ref, dkb_ref, dg_ref, db_ref, z_scr, d_scr):
        i = pl.program_id(0)

        @pl.when(i == 0)
        def _():
            dk_ref[...] = jnp.zeros_like(dk_ref)
            dkb_ref[...] = jnp.zeros_like(dkb_ref)
            dg_ref[...] = jnp.zeros_like(dg_ref)
            db_ref[...] = jnp.zeros_like(db_ref)

        has_next = (i < n_tiles - 1).astype(F32)
        z_scr[0:CONV_HALO, :] = _glu(prev_ref[...]) * (i > 0).astype(F32)
        z_scr[CONV_HALO:CONV_HALO + tm, :] = _glu(cur_ref[...])
        z_scr[CONV_HALO + tm:, :] = _glu(next_ref[...])
        acc = jnp.zeros((ext, D), F32) + kb_ref[...]
        for w in range(CW):
            acc = acc + k_ref[w:w + 1, :] * z_scr[pl.ds(CONV_LEAD + w, ext), :]
        r, xh = _ln_stats(acc)
        gain = g_ref[...]
        y = xh * gain + b_ref[...]
        sy = _sig(y)
        row = lax.broadcasted_iota(jnp.int32, (ext, 1), 0)
        own = (row < tm).astype(F32)
        live = jnp.maximum(own, has_next)
        dzs = jnp.concatenate([dz_ref[...], dzn_ref[...]], axis=0)
        dy = dzs * (sy * (1.0 + y * (1.0 - sy))) * live
        dxh = dy * gain
        dzc = r * (dxh - jnp.mean(dxh, axis=-1, keepdims=True) - xh * jnp.mean(dxh * xh, axis=-1, keepdims=True))
        d_scr[...] = dzc
        dg_ref[...] += jnp.sum(dy * xh * own, axis=0, keepdims=True)
        db_ref[...] += jnp.sum(dy * own, axis=0, keepdims=True)
        dzc_own = dzc[0:tm, :]
        dkb_ref[...] += jnp.sum(dzc_own, axis=0, keepdims=True)
        dz = jnp.zeros((tm, D), F32)
        for w in range(CW):
            dk_ref[w:w + 1, :] += jnp.sum(dzc_own * z_scr[pl.ds(CONV_LEAD + w, tm), :], axis=0, keepdims=True)
            dz = dz + k_ref[w:w + 1, :] * d_scr[pl.ds(CW - 1 - w, tm), :]
        ucc = cur_ref[...].astype(F32)
        a = ucc[:, :D]
        sg = _sig(ucc[:, D:])
        duc_ref[:, :D] = (dz * sg).astype(BF)
        duc_ref[:, D:] = (dz * a * sg * (1.0 - sg)).astype(BF)

    vec = pl.BlockSpec((1, D), lambda i: (0, 0))
    return pl.pallas_call(
        body, grid=(n_tiles,),
        in_specs=[_row_tile(tm, 2 * D),
                  pl.BlockSpec((CONV_HALO, 2 * D), lambda i: (jnp.maximum(i * per - 1, 0), 0)),
                  pl.BlockSpec((CONV_HALO, 2 * D), lambda i: (jnp.minimum((i + 1) * per, t // CONV_HALO - 1), 0)),
                  _row_tile(tm, D),
                  pl.BlockSpec((CONV_HALO, D), lambda i: (jnp.minimum((i + 1) * per, t // CONV_HALO - 1), 0)),
                  _resident((CWP, D)), _resident((1, D)), _resident((1, D)), _resident((1, D))],
        out_specs=[_row_tile(tm, 2 * D), pl.BlockSpec((CWP, D), lambda i: (0, 0)), vec, vec, vec],
        out_shape=[jax.ShapeDtypeStruct((t, 2 * D), BF), jax.ShapeDtypeStruct((CWP, D), F32),
                   jax.ShapeDtypeStruct((1, D), F32), jax.ShapeDtypeStruct((1, D), F32), jax.ShapeDtypeStruct((1, D), F32)],
        scratch_shapes=[pltpu.VMEM((tm + 2 * CONV_HALO, D), F32), pltpu.VMEM((ext, D), F32)],
        compiler_params=_params(1), name="conv_bwd")(uc, uc, uc, dzs, dzs, dwk, dwb, lng, lnb)


def _norm_rows(xt, g):
    r = lax.rsqrt(jnp.mean(xt * xt, axis=0, keepdims=True) + EPS)
    xh = xt * r
    return xh * g, r, xh


ATT_TQ = 512


def _attn_specs(t, tq):
    per = tq // BLK
    return [pl.BlockSpec((1536, tq), lambda i: (0, i)),
            pl.BlockSpec((512, BLK), lambda i: (2, jnp.maximum(i * per - 1, 0))),
            _resident((HD, 1)), _resident((HD, 1)), _resident((NKV, 1, GRP * BLK)),
            _resident((NKV, 2 * BLK, GRP * BLK)), _resident((2, 2 * BLK, GRP * BLK))]


def _attn_window(hk, sb, qkv_ref, halo_ref, kn_cur, kn_halo):
    v0 = D + NKV * HD + hk * HD
    if sb == 0:
        k_prev = kn_halo[hk]
        v_prev = halo_ref[NKV * HD + hk * HD:NKV * HD + (hk + 1) * HD, :]
    else:
        k_prev = kn_cur[hk][:, (sb - 1) * BLK:sb * BLK]
        v_prev = qkv_ref[v0:v0 + HD, (sb - 1) * BLK:sb * BLK]
    kw = jnp.concatenate([k_prev, kn_cur[hk][:, sb * BLK:(sb + 1) * BLK]], axis=1).astype(BF)
    vw = jnp.concatenate([v_prev, qkv_ref[v0:v0 + HD, sb * BLK:(sb + 1) * BLK]], axis=1)
    return kw, vw


def _attn_probs(kw, qc, bias, mask, sink):
    st = _dot_tn(kw, qc) * QK_SCALE + bias
    st = jnp.where(mask > 0.5, st, NEG)
    m = jnp.maximum(jnp.max(st, axis=0, keepdims=True), sink)
    p = jnp.exp(st - m)
    e_sink = jnp.exp(sink - m)
    inv = 1.0 / (jnp.sum(p, axis=0, keepdims=True) + e_sink)
    return p * inv, e_sink * inv


def _attn_fwd(qkv_t, qg, kg, sink_rows, bias_t, mask_t):
    t = qkv_t.shape[1]
    tq = min(ATT_TQ, t)
    n_sub = tq // BLK

    def body(qkv_ref, halo_ref, qg_ref, kg_ref, sink_ref, bias_ref, mask_ref, o_ref):
        i = pl.program_id(0)
        first = (i == 0).astype(jnp.int32)
        kgain = kg_ref[...]
        qgain = qg_ref[...]
        kn_cur = [_norm_rows(qkv_ref[D + h * HD:D + (h + 1) * HD, :].astype(F32), kgain)[0] for h in range(NKV)]
        kn_halo = [_norm_rows(halo_ref[h * HD:(h + 1) * HD, :].astype(F32), kgain)[0] for h in range(NKV)]
        for hk in range(NKV):
            for sb in range(n_sub):
                cols = slice(sb * BLK, (sb + 1) * BLK)
                kw, vw = _attn_window(hk, sb, qkv_ref, halo_ref, kn_cur, kn_halo)
                qc = jnp.concatenate(
                    [_norm_rows(qkv_ref[(GRP * hk + g) * HD:(GRP * hk + g + 1) * HD, cols].astype(F32), qgain)[0]
                     for g in range(GRP)], axis=1).astype(BF)
                mask = mask_ref[first] if sb == 0 else mask_ref[0]
                p, _ = _attn_probs(kw, qc, bias_ref[hk], mask, sink_ref[hk])
                o = _dot(vw, p.astype(BF))
                for g in range(GRP):
                    head = GRP * hk + g
                    o_ref[head * HD:(head + 1) * HD, cols] = o[:, g * BLK:(g + 1) * BLK].astype(BF)

    return pl.pallas_call(
        body, grid=(t // tq,),
        in_specs=_attn_specs(t, tq),
        out_specs=pl.BlockSpec((D, tq), lambda i: (0, i)),
        out_shape=jax.ShapeDtypeStruct((D, t), BF),
        compiler_params=_params(1), name="attn_fwd")(qkv_t, qkv_t, qg, kg, sink_rows, bias_t, mask_t)


def _attn_bwd(qkv_t, do_t, qg, kg, sink_rows, bias_t, mask_t):
    t = qkv_t.shape[1]
    tq = min(ATT_TQ, t)
    n_sub = tq // BLK
    n_tiles = t // tq

    def body(qkv_ref, halo_ref, do_ref, qg_ref, kg_ref, sink_ref, bias_ref, mask_ref,
             dq_ref, ckv_ref, dqg_ref, dsink_ref, dsacc_ref, qg_scr):
        i = pl.program_id(0)

        @pl.when(i == 0)
        def _():
            qg_scr[...] = jnp.zeros_like(qg_scr)
            dsink_ref[...] = jnp.zeros_like(dsink_ref)
            dsacc_ref[...] = jnp.zeros_like(dsacc_ref)

        first = (i == 0).astype(jnp.int32)
        kgain = kg_ref[...]
        qgain = qg_ref[...]
        kn_cur = [_norm_rows(qkv_ref[D + h * HD:D + (h + 1) * HD, :].astype(F32), kgain)[0] for h in range(NKV)]
        kn_halo = [_norm_rows(halo_ref[h * HD:(h + 1) * HD, :].astype(F32), kgain)[0] for h in range(NKV)]
        dqg = jnp.zeros((HD, BLK), F32)
        for hk in range(NKV):
            for sb in range(n_sub):
                cols = slice(sb * BLK, (sb + 1) * BLK)
                kw, vw = _attn_window(hk, sb, qkv_ref, halo_ref, kn_cur, kn_halo)
                qn, qr, qh = [], [], []
                for g in range(GRP):
                    head = GRP * hk + g
                    n_, r_, h_ = _norm_rows(qkv_ref[head * HD:(head + 1) * HD, cols].astype(F32), qgain)
                    qn.append(n_)
                    qr.append(r_)
                    qh.append(h_)
                qc = jnp.concatenate(qn, axis=1).astype(BF)
                mask = mask_ref[first] if sb == 0 else mask_ref[0]
                p, p_sink = _attn_probs(kw, qc, bias_ref[hk], mask, sink_ref[hk])
                doc = jnp.concatenate([do_ref[(GRP * hk + g) * HD:(GRP * hk + g + 1) * HD, cols] for g in range(GRP)], axis=1)
                dp = _dot_tn(vw, doc)
                delta = jnp.sum(p * dp, axis=0, keepdims=True)
                ds = p * (dp - delta)
                dsink_ref[hk] += -(p_sink * delta)
                dsacc_ref[hk] += ds
                dsb = ds.astype(BF)
                dqc = _dot(kw, dsb) * QK_SCALE
                ckv_ref[sb, hk * HD:(hk + 1) * HD, :] = _dot_nt(qc, dsb) * QK_SCALE
                ckv_ref[sb, NKV * HD + hk * HD:NKV * HD + (hk + 1) * HD, :] = _dot_nt(doc, p.astype(BF))
                for g in range(GRP):
                    head = GRP * hk + g
                    dqn = dqc[:, g * BLK:(g + 1) * BLK]
                    dqh = dqn * qgain
                    dq = qr[g] * (dqh - qh[g] * jnp.mean(dqh * qh[g], axis=0, keepdims=True))
                    dq_ref[head * HD:(head + 1) * HD, cols] = dq.astype(BF)
                    dqg = dqg + dqn * qh[g]
        qg_scr[...] += dqg

        @pl.when(i == n_tiles - 1)
        def _():
            dqg_ref[...] = jnp.sum(qg_scr[...], axis=1, keepdims=True)

    return pl.pallas_call(
        body, grid=(n_tiles,),
        in_specs=_attn_specs(t, tq)[:2] + [pl.BlockSpec((D, tq), lambda i: (0, i))] + _attn_specs(t, tq)[2:],
        out_specs=[pl.BlockSpec((D, tq), lambda i: (0, i)),
                   pl.BlockSpec((n_sub, 2 * NKV * HD, 2 * BLK), lambda i: (i, 0, 0)),
                   pl.BlockSpec((HD, 1), lambda i: (0, 0)),
                   pl.BlockSpec((NKV, 1, GRP * BLK), lambda i: (0, 0, 0)),
                   pl.BlockSpec((NKV, 2 * BLK, GRP * BLK), lambda i: (0, 0, 0))],
        out_shape=[jax.ShapeDtypeStruct((D, t), BF),
                   jax.ShapeDtypeStruct((t // BLK, 2 * NKV * HD, 2 * BLK), F32),
                   jax.ShapeDtypeStruct((HD, 1), F32),
                   jax.ShapeDtypeStruct((NKV, 1, GRP * BLK), F32),
                   jax.ShapeDtypeStruct((NKV, 2 * BLK, GRP * BLK), F32)],
        scratch_shapes=[pltpu.VMEM((HD, BLK), F32)],
        compiler_params=_params(1), name="attn_bwd")(qkv_t, qkv_t, do_t, qg, kg, sink_rows, bias_t, mask_t)


def _kv_combine(ckv, qkv_t, kg):
    nb = ckv.shape[0]
    t = nb * BLK
    rows = NKV * HD

    def body(c_ref, cn_ref, k_ref, kg_ref, o_ref, dkg_ref, kg_scr):
        n = pl.program_id(0)

        @pl.when(n == 0)
        def _():
            kg_scr[...] = jnp.zeros_like(kg_scr)

        has_next = (n < nb - 1).astype(F32)
        d = c_ref[0, :, BLK:] + cn_ref[0, :, :BLK] * has_next
        o_ref[rows:, :] = d[rows:, :].astype(BF)
        kgain = kg_ref[...]
        dkg = jnp.zeros((HD, BLK), F32)
        for h in range(NKV):
            _, r, kh = _norm_rows(k_ref[h * HD:(h + 1) * HD, :].astype(F32), kgain)
            dkn = d[h * HD:(h + 1) * HD, :]
            dkh = dkn * kgain
            o_ref[h * HD:(h + 1) * HD, :] = (r * (dkh - kh * jnp.mean(dkh * kh, axis=0, keepdims=True))).astype(BF)
            dkg = dkg + dkn * kh
        kg_scr[...] += dkg

        @pl.when(n == nb - 1)
        def _():
            dkg_ref[...] = jnp.sum(kg_scr[...], axis=1, keepdims=True)

    return pl.pallas_call(
        body, grid=(nb,),
        in_specs=[pl.BlockSpec((1, 2 * rows, 2 * BLK), lambda n: (n, 0, 0)),
                  pl.BlockSpec((1, 2 * rows, 2 * BLK), lambda n: (jnp.minimum(n + 1, nb - 1), 0, 0)),
                  pl.BlockSpec((rows, BLK), lambda n: (D // rows, n)),
                  _resident((HD, 1))],
        out_specs=[pl.BlockSpec((2 * rows, BLK), lambda n: (0, n)), pl.BlockSpec((HD, 1), lambda n: (0, 0))],
        out_shape=[jax.ShapeDtypeStruct((2 * rows, t), BF), jax.ShapeDtypeStruct((HD, 1), F32)],
        scratch_shapes=[pltpu.VMEM((HD, BLK), F32)],
        compiler_params=_params(1), name="kv_combine")(ckv, ckv, qkv_t, kg)


def _group_lane_sums(v):
    lane_group = lax.broadcasted_iota(jnp.int32, (1, GRP * BLK), 1) // BLK
    col = lax.broadcasted_iota(jnp.int32, (1, BLK), 1)
    out = jnp.zeros((NKV, BLK), F32)
    for g in range(GRP):
        s = jnp.sum(jnp.where(lane_group == g, v, 0.0), axis=1, keepdims=True)
        out = jnp.where(col == g, s, out)
    return out


def _bias_grad(dsacc, onehot_t):
    def body(ds_ref, oh_ref, o_ref):
        oh = jnp.concatenate([oh_ref[0]] * GRP, axis=1)
        o_ref[0] = _group_lane_sums(jnp.sum(ds_ref[...] * oh[None], axis=1))

    return pl.pallas_call(
        body, grid=(NBUCKET,),
        in_specs=[_resident((NKV, 2 * BLK, GRP * BLK)), pl.BlockSpec((1, 2 * BLK, BLK), lambda b: (b, 0, 0))],
        out_specs=pl.BlockSpec((1, NKV, BLK), lambda b: (b, 0, 0)),
        out_shape=jax.ShapeDtypeStruct((NBUCKET, NKV, BLK), F32),
        compiler_params=_params(1), name="bias_grad")(dsacc, onehot_t)


def _sink_grad(dsink_rows):
    def body(d_ref, o_ref):
        o_ref[...] = _group_lane_sums(d_ref[:, 0, :])

    return pl.pallas_call(body, out_shape=jax.ShapeDtypeStruct((NKV, BLK), F32), name="sink_grad")(dsink_rows)


def _mix_out(zs, o_t, gp, x, w_cp, w_o, w_out):
    t = x.shape[0]
    tm = min(256, t)

    def body(zs_ref, ot_ref, gp_ref, x_ref, wcp_ref, wo_ref, wout_ref, xo_ref, a_ref, b_ref, m_ref):
        a = _dot(zs_ref[...], wcp_ref[...])
        b = _dot_tn(ot_ref[...], wo_ref[...])
        a_ref[...] = a.astype(BF)
        b_ref[...] = b.astype(BF)
        merged = (_sig(gp_ref[:, :D].astype(F32)) * a + _sig(gp_ref[:, D:].astype(F32)) * b).astype(BF)
        m_ref[...] = merged
        xo_ref[...] = x_ref[...] + _dot(merged, wout_ref[...])

    return pl.pallas_call(
        body, grid=(t // tm,),
        in_specs=[_row_tile(tm, D), pl.BlockSpec((D, tm), lambda i: (0, i)), _row_tile(tm, 2 * D), _row_tile(tm, D),
                  _resident((D, D)), _resident((D, D)), _resident((D, D))],
        out_specs=[_row_tile(tm, D)] * 4,
        out_shape=[jax.ShapeDtypeStruct((t, D), F32)] + [jax.ShapeDtypeStruct((t, D), BF)] * 3,
        compiler_params=_params(1), name="mix_out")(zs, o_t, gp, x, w_cp, w_o, w_out)


def _mix_out_bwd(dx, a, b, gp, w_cp, w_o, w_out):
    t = dx.shape[0]
    tm = min(256, t)

    def body(dx_ref, a_ref, b_ref, gp_ref, wcp_ref, wo_ref, wout_ref, dzs_ref, dot_ref, dgp_ref, da_ref, db_ref, dxb_ref):
        dxb = dx_ref[...].astype(BF)
        dxb_ref[...] = dxb
        dm = _dot_nt(dxb, wout_ref[...])
        gc = _sig(gp_ref[:, :D].astype(F32))
        ga = _sig(gp_ref[:, D:].astype(F32))
        da = (dm * gc).astype(BF)
        db = (dm * ga).astype(BF)
        da_ref[...] = da
        db_ref[...] = db
        dgp_ref[:, :D] = (dm * a_ref[...].astype(F32) * gc * (1.0 - gc)).astype(BF)
        dgp_ref[:, D:] = (dm * b_ref[...].astype(F32) * ga * (1.0 - ga)).astype(BF)
        dzs_ref[...] = _dot_nt(da, wcp_ref[...])
        dot_ref[...] = _dot_nt(wo_ref[...], db).astype(BF)

    return pl.pallas_call(
        body, grid=(t // tm,),
        in_specs=[_row_tile(tm, D), _row_tile(tm, D), _row_tile(tm, D), _row_tile(tm, 2 * D),
                  _resident((D, D)), _resident((D, D)), _resident((D, D))],
        out_specs=[_row_tile(tm, D), pl.BlockSpec((D, tm), lambda i: (0, i)), _row_tile(tm, 2 * D),
                   _row_tile(tm, D), _row_tile(tm, D), _row_tile(tm, D)],
        out_shape=[jax.ShapeDtypeStruct((t, D), F32), jax.ShapeDtypeStruct((D, t), BF), jax.ShapeDtypeStruct((t, 2 * D), BF),
                   jax.ShapeDtypeStruct((t, D), BF), jax.ShapeDtypeStruct((t, D), BF), jax.ShapeDtypeStruct((t, D), BF)],
        compiler_params=_params(1), name="mix_out_bwd")(dx, a, b, gp, w_cp, w_o, w_out)


def _mix_proj_bwd(dxo, duc, dq_t, dkv_t, dgp, x, g, w_t):
    t = x.shape[0]
    tm = min(256, t)

    def body(dxo_ref, duc_ref, dq_ref, dkv_ref, dgp_ref, x_ref, g_ref, w_ref, dx_ref, dg_ref):
        dn = _dot(duc_ref[...], w_ref[R_CONV[0]:R_CONV[1], :])
        dn = dn + _dot(dgp_ref[...], w_ref[R_GATE[0]:R_GATE[1], :])
        dn = dn + _dot_tn(dq_ref[...], w_ref[R_Q[0]:R_Q[1], :])
        dn = dn + _dot_tn(dkv_ref[...], w_ref[R_KV[0]:R_KV[1], :])
        dx, dg = _rms_bwd(dn, x_ref[...], g_ref[...])
        dx_ref[...] = dxo_ref[...] + dx

        @pl.when(pl.program_id(0) == 0)
        def _():
            dg_ref[...] = jnp.zeros_like(dg_ref)

        dg_ref[...] += dg

    return pl.pallas_call(
        body, grid=(t // tm,),
        in_specs=[_row_tile(tm, D), _row_tile(tm, 2 * D), pl.BlockSpec((D, tm), lambda i: (0, i)),
                  pl.BlockSpec((2 * NKV * HD, tm), lambda i: (0, i)), _row_tile(tm, 2 * D), _row_tile(tm, D),
                  _resident((1, D)), _resident((INW, D))],
        out_specs=[_row_tile(tm, D), pl.BlockSpec((1, D), lambda i: (0, 0))],
        out_shape=[jax.ShapeDtypeStruct((t, D), F32), jax.ShapeDtypeStruct((1, D), F32)],
        compiler_params=_params(1), name="mix_proj_bwd")(dxo, duc, dq_t, dkv_t, dgp, x, g, w_t)


def _attention_tables():
    kj = np.arange(2 * BLK)[:, None]
    qi = np.arange(BLK)[None, :]
    dist = qi + BLK - kj
    in_win = (dist >= 0) & (dist < BLK)
    dpos = np.maximum(dist, 0)
    max_exact = NBUCKET // 2
    dfl = np.maximum(dpos, 1).astype(np.float32)
    large = max_exact + (np.log(dfl / np.float32(max_exact)) / np.float32(math.log(BLK / max_exact))
                         * np.float32(NBUCKET - max_exact)).astype(np.int32)
    large = np.minimum(large, NBUCKET - 1)
    bucket = np.where(dpos < max_exact, dpos, large)
    onehot = (bucket[None] == np.arange(NBUCKET)[:, None, None]).astype(np.float32)
    mask = in_win.astype(np.float32)
    mask_first = mask * (kj >= BLK)
    masks = np.stack([np.tile(mask, (1, GRP)), np.tile(mask_first, (1, GRP))])
    return onehot, masks


def _bias_table(rel_bias, onehot):
    tab = jnp.einsum("bkq,bh->hkq", onehot, rel_bias, precision=lax.Precision.HIGHEST)
    tab = tab.reshape(NKV, GRP, 2 * BLK, BLK)
    return jnp.transpose(tab, (0, 2, 1, 3)).reshape(NKV, 2 * BLK, GRP * BLK)


def _local_step(x, target, vec, mats):
    onehot_np, masks_np = _attention_tables()
    onehot = jnp.asarray(onehot_np)
    masks = jnp.asarray(masks_np)
    bias_t = _bias_table(vec["rel_bias"], onehot)
    sink_rows = jnp.repeat(vec["attn_sinks"].reshape(NKV, 1, GRP), BLK, axis=2)
    qg = vec["q_norm"].reshape(HD, 1)
    kg = vec["k_norm"].reshape(HD, 1)
    g1 = vec["ffn1_norm"].reshape(1, D)
    gm = vec["mix_norm"].reshape(1, D)
    g2 = vec["ffn2_norm"].reshape(1, D)
    dwk = jnp.pad(vec["conv_dw_kernel"], ((0, CWP - CW), (0, 0)))
    dwb = vec["conv_dw_bias"].reshape(1, D)
    lng = vec["conv_ln_g"].reshape(1, D)
    lnb = vec["conv_ln_b"].reshape(1, D)

    n1, u1, x1 = _ffn_fwd(x, g1, mats["ffn1_w_in"], mats["ffn1_w_out"], "ffn1_fwd")
    hm, uc, gp, qkv_t = _mix_proj(x1, gm, mats["w_in"])
    zs = _conv_fwd(uc, dwk, dwb, lng, lnb)
    o_t = _attn_fwd(qkv_t, qg, kg, sink_rows, bias_t, masks)
    x2, a, b, merged = _mix_out(zs, o_t, gp, x1, mats["conv_w_proj"], mats["attn_w_o"], mats["w_out"])
    n2, u2, dx3, sq = _ffn_fwd(x2, g2, mats["ffn2_w_in"], mats["ffn2_w_out"], "ffn2_fwd", target=target)

    gv, gm_ = {}, {}
    dx2, du2, h2, dy2, gv["ffn2_norm"] = _ffn_bwd(dx3, x2, g2, u2, mats["ffn2_w_in"], mats["ffn2_w_out"], "ffn2_bwd")
    gm_["ffn2_w_in"] = _wgrad(du2, n2, "ffn2_dw_in", lhs_is_transposed=False, chunk=1408)
    gm_["ffn2_w_out"] = _wgrad(h2, dy2, "ffn2_dw_out", lhs_is_transposed=False, chunk=1408)

    dzs, do_t, dgp, da, db, dx2b = _mix_out_bwd(dx2, a, b, gp, mats["conv_w_proj"], mats["attn_w_o"], mats["w_out"])
    gm_["w_out"] = _wgrad(merged, dx2b, "mix_dw_out", lhs_is_transposed=False, chunk=1024)
    gm_["conv_w_proj"] = _wgrad(zs, da, "mix_dw_cp", lhs_is_transposed=False, chunk=1024)
    gm_["attn_w_o"] = _wgrad(o_t, db, "mix_dw_o", lhs_is_transposed=True, chunk=1024)

    dq_t, ckv, dqg, dsink_rows, dsacc = _attn_bwd(qkv_t, do_t, qg, kg, sink_rows, bias_t, masks)
    dkv_t, dkg = _kv_combine(ckv, qkv_t, kg)
    gv["q_norm"] = dqg.reshape(HD)
    gv["k_norm"] = dkg.reshape(HD)
    gv["attn_sinks"] = _sink_grad(dsink_rows)[:, :GRP].reshape(NQ)
    gv["rel_bias"] = _bias_grad(dsacc, onehot)[:, :, :GRP].reshape(NBUCKET, NQ)

    duc, dk_conv, gv["conv_dw_bias"], gv["conv_ln_g"], gv["conv_ln_b"] = _conv_bwd(uc, dzs, dwk, dwb, lng, lnb)
    gv["conv_dw_kernel"] = dk_conv[:CW]

    dx1, gv["mix_norm"] = _mix_proj_bwd(dx2, duc, dq_t, dkv_t, dgp, x1, gm, mats["w_in"])
    gm_["w_in"] = jnp.concatenate([
        _wgrad(duc, hm, "mix_dw_conv", lhs_is_transposed=False, chunk=1024),
        _wgrad(dq_t, hm, "mix_dw_q", lhs_is_transposed=True, chunk=1024),
        _wgrad(dkv_t, hm, "mix_dw_kv", lhs_is_transposed=True, chunk=512),
        _wgrad(dgp, hm, "mix_dw_gate", lhs_is_transposed=False, chunk=1024)], axis=0)

    dx0, du1, h1, dy1, gv["ffn1_norm"] = _ffn_bwd(dx1, x, g1, u1, mats["ffn1_w_in"], mats["ffn1_w_out"], "ffn1_bwd")
    gm_["ffn1_w_in"] = _wgrad(du1, n1, "ffn1_dw_in", lhs_is_transposed=False, chunk=1408)
    gm_["ffn1_w_out"] = _wgrad(h1, dy1, "ffn1_dw_out", lhs_is_transposed=False, chunk=1408)
    for k in ("ffn1_norm", "mix_norm", "ffn2_norm", "conv_dw_bias", "conv_ln_g", "conv_ln_b"):
        gv[k] = gv[k].reshape(D)
    return sq, dx0, gv, gm_


MESH_ID = pl.DeviceIdType.MESH
ANY = pl.BlockSpec(memory_space=pl.ANY)


def _position():
    return lax.axis_index("x"), lax.axis_index("y"), lax.axis_index("c")


def _shard_rows(ref, index, rows):
    return ref.at[pl.ds(pl.multiple_of(index * rows, 16), rows), :]


def _prep(weights, transposed):
    n = len(weights)

    def body(*refs):
        for k in range(n):
            w = refs[k][...]
            refs[n + k][...] = (w.T if transposed[k] else w).astype(BF)

    out_shape = [jax.ShapeDtypeStruct(w.shape[::-1] if tr else w.shape, BF) for w, tr in zip(weights, transposed)]
    return pl.pallas_call(body, out_shape=out_shape, compiler_params=pltpu.CompilerParams(vmem_limit_bytes=VMEM_LIMIT_V7X),
                          name="prep")(*weights)


def _all_gather(shards):
    n = len(shards)
    rows = [s.shape[0] for s in shards]

    def body(*refs):
        ins, outs = refs[:n], refs[n:2 * n]
        send_sems, recv_sems, local_sems = refs[2 * n:]
        x, y, c = _position()
        me, sibling = (x, y, c), (x, y, 1 - c)
        chips = [(1 - x, y), (x, 1 - y), (1 - x, 1 - y)]

        def block(k, dev):
            return _shard_rows(outs[k], 4 * dev[0] + 2 * dev[1] + dev[2], rows[k])

        def copy(k, j, dev, to, src=None):
            return pltpu.make_async_remote_copy(
                src_ref=block(k, dev) if src is None else src, dst_ref=block(k, dev),
                send_sem=send_sems.at[k, j], recv_sem=recv_sems.at[k, j], device_id=to, device_id_type=MESH_ID)

        mine = [pltpu.make_async_copy(ins[k], block(k, me), local_sems.at[k]) for k in range(n)]
        for cp in mine:
            cp.start()
        first = []
        for k in range(n):
            for j, chip in enumerate(chips):
                first.append(copy(k, 1 + j, me, (*chip, c), src=ins[k]))
            first.append(copy(k, 0, me, sibling, src=ins[k]))
        for cp in first:
            cp.start()
        passed = []
        for j, chip in enumerate(chips):
            for k in range(n):
                copy(k, 1 + j, (*chip, c), me).wait_recv()
                cp = copy(k, 4 + j, (*chip, c), sibling)
                cp.start()
                passed.append(cp)
        for k in range(n):
            copy(k, 0, sibling, me).wait_recv()
        for j, chip in enumerate(chips):
            for k in range(n):
                copy(k, 4 + j, (*chip, 1 - c), me).wait_recv()
        for cp in first + passed:
            cp.wait_send()
        for cp in mine:
            cp.wait()

    return pl.pallas_call(
        body,
        out_shape=[jax.ShapeDtypeStruct((N_DEV * s.shape[0], s.shape[1]), s.dtype) for s in shards],
        in_specs=[ANY] * n, out_specs=[ANY] * n,
        scratch_shapes=[pltpu.SemaphoreType.DMA((n, 7)), pltpu.SemaphoreType.DMA((n, 7)), pltpu.SemaphoreType.DMA((n,))],
        name="all_gather")(*shards)


def _rs_pair(grads):
    n = len(grads)
    rows = [g.shape[0] // N_DEV for g in grads]

    def body(*refs):
        ins, outs = refs[:n], refs[n:2 * n]
        send_sems, recv_sems = refs[2 * n:]
        x, y, c = _position()
        copies = []
        for k in range(n):
            for q in range(4):
                copies.append(pltpu.make_async_remote_copy(
                    src_ref=_shard_rows(ins[k], 2 * q + 1 - c, rows[k]), dst_ref=_shard_rows(outs[k], q, rows[k]),
                    send_sem=send_sems.at[k, q], recv_sem=recv_sems.at[k, q], device_id=(x, y, 1 - c),
                    device_id_type=MESH_ID))
        for cp in copies:
            cp.start()
        for cp in copies:
            cp.wait()

    return pl.pallas_call(
        body, out_shape=[jax.ShapeDtypeStruct((4 * r, g.shape[1]), g.dtype) for g, r in zip(grads, rows)],
        in_specs=[ANY] * n, out_specs=[ANY] * n,
        scratch_shapes=[pltpu.SemaphoreType.DMA((n, 4)), pltpu.SemaphoreType.DMA((n, 4))],
        name="rs_pair")(*grads)


def _pair_add(grad, received, core):
    r = received.shape[0] // 4
    tr = 352 if r % 352 == 0 else r
    per = r // tr

    def body(c_ref, g_ref, r_ref, o_ref):
        o_ref[...] = (g_ref[...].astype(F32) + r_ref[...].astype(F32)).astype(BF)

    return pl.pallas_call(
        body,
        grid_spec=pltpu.PrefetchScalarGridSpec(
            num_scalar_prefetch=1, grid=(4, per),
            in_specs=[pl.BlockSpec((tr, D), lambda q, i, c_ref: ((2 * q + c_ref[0]) * per + i, 0)),
                      pl.BlockSpec((tr, D), lambda q, i, c_ref: (q * per + i, 0))],
            out_specs=pl.BlockSpec((tr, D), lambda q, i, c_ref: (q * per + i, 0))),
        out_shape=jax.ShapeDtypeStruct(received.shape, BF),
        compiler_params=_params(2), name=f"pair_add_{r}")(core, grad, received)


def _rs_chips(partials):
    n = len(partials)
    rows = [p.shape[0] // 4 for p in partials]

    def body(*refs):
        ins, outs = refs[:n], refs[n:2 * n]
        send_sems, recv_sems, local_sems = refs[2 * n:]
        x, y, c = _position()
        my_chip = 2 * x + y
        chips = [(1 - x, y), (x, 1 - y), (1 - x, 1 - y)]
        mine = [pltpu.make_async_copy(_shard_rows(ins[k], my_chip, rows[k]), _shard_rows(outs[k], my_chip, rows[k]),
                                      local_sems.at[k]) for k in range(n)]
        for cp in mine:
            cp.start()
        sends, recvs = [], []
        for k in range(n):
            for j, chip in enumerate(chips):
                their = 2 * chip[0] + chip[1]
                sends.append(pltpu.make_async_remote_copy(
                    src_ref=_shard_rows(ins[k], their, rows[k]), dst_ref=_shard_rows(outs[k], my_chip, rows[k]),
                    send_sem=send_sems.at[k, j], recv_sem=recv_sems.at[k, j], device_id=(*chip, c), device_id_type=MESH_ID))
                recvs.append(pltpu.make_async_remote_copy(
                    src_ref=_shard_rows(ins[k], their, rows[k]), dst_ref=_shard_rows(outs[k], their, rows[k]),
                    send_sem=send_sems.at[k, j], recv_sem=recv_sems.at[k, j], device_id=(*chip, c), device_id_type=MESH_ID))
        for cp in sends:
            cp.start()
        for cp in recvs:
            cp.wait_recv()
        for cp in sends:
            cp.wait_send()
        for cp in mine:
            cp.wait()

    return pl.pallas_call(
        body, out_shape=[jax.ShapeDtypeStruct(p.shape, p.dtype) for p in partials],
        in_specs=[ANY] * n, out_specs=[ANY] * n,
        scratch_shapes=[pltpu.SemaphoreType.DMA((n, 3)), pltpu.SemaphoreType.DMA((n, 3)), pltpu.SemaphoreType.DMA((n,))],
        name="rs_chips")(*partials)


def _all_reduce_small(payload):
    r = payload.shape[0]

    def body(in_ref, out_ref, land_ref, send_sems, recv_sems):
        x, y, c = _position()
        me = 4 * x + 2 * y + c
        land_ref[me] = in_ref[...]
        copies = []
        for k in range(1, N_DEV):
            peer = (x ^ (k >> 2), y ^ ((k >> 1) & 1), c ^ (k & 1))
            copies.append(pltpu.make_async_remote_copy(
                src_ref=in_ref, dst_ref=land_ref.at[me], send_sem=send_sems.at[k - 1], recv_sem=recv_sems.at[k - 1],
                device_id=peer, device_id_type=MESH_ID))
        for cp in copies:
            cp.start()
        for k in range(1, N_DEV):
            peer_index = me ^ k
            pltpu.make_async_remote_copy(
                src_ref=in_ref, dst_ref=land_ref.at[peer_index], send_sem=send_sems.at[k - 1], recv_sem=recv_sems.at[k - 1],
                device_id=(x, y, c), device_id_type=MESH_ID).wait_recv()
        for cp in copies:
            cp.wait_send()
        acc = land_ref[0]
        for d in range(1, N_DEV):
            acc = acc + land_ref[d]
        out_ref[...] = acc

    return pl.pallas_call(
        body, out_shape=jax.ShapeDtypeStruct((r, D), F32),
        in_specs=[pl.BlockSpec(memory_space=pltpu.VMEM)], out_specs=pl.BlockSpec(memory_space=pltpu.VMEM),
        scratch_shapes=[pltpu.VMEM((N_DEV, r, D), F32), pltpu.SemaphoreType.DMA((N_DEV - 1,)),
                        pltpu.SemaphoreType.DMA((N_DEV - 1,))],
        name="all_reduce_small")(payload)


def _adamw_math(w, g, m, v):
    m = ADAM_B1 * m + (1.0 - ADAM_B1) * g
    v = ADAM_B2 * v + (1.0 - ADAM_B2) * (g * g)
    m_hat = m / (1.0 - ADAM_B1 ** ADAM_STEP)
    v_hat = v / (1.0 - ADAM_B2 ** ADAM_STEP)
    delta = -ADAM_LR * (m_hat / (jnp.sqrt(v_hat) + ADAM_EPS) + ADAM_WD * w)
    return delta, m, v


def _reduce_adamw(received, w, m, v, name, transposed):
    r = received.shape[0] // 4

    def body(r_ref, w_ref, m_ref, v_ref, g_ref, d_ref, nm_ref, nv_ref):
        g = r_ref[0:r, :].astype(F32)
        for q in range(1, 4):
            g = g + r_ref[q * r:(q + 1) * r, :].astype(F32)
        if transposed:
            g = g.T
        g_ref[...] = g
        d_ref[...], nm_ref[...], nv_ref[...] = _adamw_math(w_ref[...], g, m_ref[...], v_ref[...])

    return pl.pallas_call(body, out_shape=[jax.ShapeDtypeStruct(w.shape, F32)] * 4,
                          compiler_params=pltpu.CompilerParams(vmem_limit_bytes=VMEM_LIMIT_V7X), name=name)(received, w, m, v)


def _adamw_small(w, g, m, v, name):
    def body(w_ref, g_ref, m_ref, v_ref, d_ref, nm_ref, nv_ref):
        d_ref[...], nm_ref[...], nv_ref[...] = _adamw_math(w_ref[...], g_ref[...], m_ref[...], v_ref[...])

    return pl.pallas_call(body, out_shape=[jax.ShapeDtypeStruct(w.shape, F32)] * 3, name=name)(w, g, m, v)


WEIGHTS = ("ffn1_norm", "ffn1_w_in", "ffn1_w_out", "mix_norm", "w_in", "conv_dw_kernel", "conv_dw_bias", "conv_ln_g",
           "conv_ln_b", "conv_w_proj", "q_norm", "k_norm", "attn_sinks", "rel_bias", "attn_w_o", "w_out", "ffn2_norm",
           "ffn2_w_in", "ffn2_w_out")
MATRICES = ("ffn1_w_in", "ffn1_w_out", "w_in", "conv_w_proj", "attn_w_o", "w_out", "ffn2_w_in", "ffn2_w_out")
COLUMN_SHARDED = ("ffn1_w_in", "w_in", "ffn2_w_in")
ROW_VECTORS = ("ffn1_norm", "mix_norm", "conv_dw_bias", "conv_ln_g", "conv_ln_b", "ffn2_norm")
PACKED = (("q_norm", HD), ("k_norm", HD), ("attn_sinks", NQ), ("rel_bias", NBUCKET * NQ))
ROW_PACKED = len(ROW_VECTORS)
ROW_LOSS = ROW_PACKED + 1
ROW_TAPS = 8
PAYLOAD_ROWS = ROW_TAPS + CWP


def _pack_small(values, last_row):
    packed = jnp.concatenate([values[k].reshape(-1) for k, _ in PACKED])
    packed = jnp.pad(packed, (0, D - packed.shape[0])).reshape(1, D)
    return jnp.concatenate([values[k].reshape(1, D) for k in ROW_VECTORS] + [packed, last_row], axis=0)


def _unpack_small(rows):
    out = {k: rows[i] for i, k in enumerate(ROW_VECTORS)}
    at = 0
    for k, size in PACKED:
        out[k] = rows[ROW_PACKED, at:at + size]
        at += size
    out["rel_bias"] = out["rel_bias"].reshape(NBUCKET, NQ)
    return out


def kernel(x, ffn1_norm, ffn1_w_in, ffn1_w_out, mix_norm, w_in, conv_dw_kernel, conv_dw_bias, conv_ln_g, conv_ln_b, conv_w_proj, q_norm, k_norm, attn_sinks, rel_bias, attn_w_o, w_out, ffn2_norm, ffn2_w_in, ffn2_w_out, loss_target, m_ffn1_norm, m_ffn1_w_in, m_ffn1_w_out, m_mix_norm, m_w_in, m_conv_dw_kernel, m_conv_dw_bias, m_conv_ln_g, m_conv_ln_b, m_conv_w_proj, m_q_norm, m_k_norm, m_attn_sinks, m_rel_bias, m_attn_w_o, m_w_out, m_ffn2_norm, m_ffn2_w_in, m_ffn2_w_out, v_ffn1_norm, v_ffn1_w_in, v_ffn1_w_out, v_mix_norm, v_w_in, v_conv_dw_kernel, v_conv_dw_bias, v_conv_ln_g, v_conv_ln_b, v_conv_w_proj, v_q_norm, v_k_norm, v_attn_sinks, v_rel_bias, v_attn_w_o, v_w_out, v_ffn2_norm, v_ffn2_w_in, v_ffn2_w_out):
    w = dict(ffn1_norm=ffn1_norm, ffn1_w_in=ffn1_w_in, ffn1_w_out=ffn1_w_out, mix_norm=mix_norm, w_in=w_in,
             conv_dw_kernel=conv_dw_kernel, conv_dw_bias=conv_dw_bias, conv_ln_g=conv_ln_g, conv_ln_b=conv_ln_b,
             conv_w_proj=conv_w_proj, q_norm=q_norm, k_norm=k_norm, attn_sinks=attn_sinks, rel_bias=rel_bias,
             attn_w_o=attn_w_o, w_out=w_out, ffn2_norm=ffn2_norm, ffn2_w_in=ffn2_w_in, ffn2_w_out=ffn2_w_out)
    m = dict(ffn1_norm=m_ffn1_norm, ffn1_w_in=m_ffn1_w_in, ffn1_w_out=m_ffn1_w_out, mix_norm=m_mix_norm, w_in=m_w_in,
             conv_dw_kernel=m_conv_dw_kernel, conv_dw_bias=m_conv_dw_bias, conv_ln_g=m_conv_ln_g, conv_ln_b=m_conv_ln_b,
             conv_w_proj=m_conv_w_proj, q_norm=m_q_norm, k_norm=m_k_norm, attn_sinks=m_attn_sinks, rel_bias=m_rel_bias,
             attn_w_o=m_attn_w_o, w_out=m_w_out, ffn2_norm=m_ffn2_norm, ffn2_w_in=m_ffn2_w_in, ffn2_w_out=m_ffn2_w_out)
    v = dict(ffn1_norm=v_ffn1_norm, ffn1_w_in=v_ffn1_w_in, ffn1_w_out=v_ffn1_w_out, mix_norm=v_mix_norm, w_in=v_w_in,
             conv_dw_kernel=v_conv_dw_kernel, conv_dw_bias=v_conv_dw_bias, conv_ln_g=v_conv_ln_g, conv_ln_b=v_conv_ln_b,
             conv_w_proj=v_conv_w_proj, q_norm=v_q_norm, k_norm=v_k_norm, attn_sinks=v_attn_sinks, rel_bias=v_rel_bias,
             attn_w_o=v_attn_w_o, w_out=v_w_out, ffn2_norm=v_ffn2_norm, ffn2_w_in=v_ffn2_w_in, ffn2_w_out=v_ffn2_w_out)
    me = 4 * lax.axis_index("x") + 2 * lax.axis_index("y") + lax.axis_index("c")
    core = lax.axis_index("c").astype(jnp.int32).reshape(1)

    shards = _prep([w[k] for k in MATRICES], [k in COLUMN_SHARDED for k in MATRICES])
    taps_shard = jnp.pad(conv_dw_kernel, ((0, CWP - CW), (0, 0)))
    gathered = _all_gather(list(shards) + [taps_shard])
    mats = dict(zip(MATRICES, gathered[:-1]))
    taps = jnp.transpose(gathered[-1].reshape(N_DEV, CWP, BLK), (1, 0, 2)).reshape(CWP, D)[:CW]

    vec = {k: w[k] for k in WEIGHTS if k not in MATRICES}
    vec["conv_dw_kernel"] = taps
    sq, dx0, gv, gmat = _local_step(x[0], loss_target[0], vec, mats)

    from_sibling = _rs_pair([gmat[k] for k in MATRICES])
    partials = [_pair_add(gmat[k], r, core) for k, r in zip(MATRICES, from_sibling)]
    from_chips = dict(zip(MATRICES, _rs_chips(partials)))

    payload = jnp.concatenate([_pack_small(gv, sq), jnp.pad(gv["conv_dw_kernel"], ((0, CWP - CW), (0, 0)))], axis=0)
    total = _all_reduce_small(payload)
    loss = (0.5 / D) * jnp.sum(total[ROW_LOSS])

    grads, delta, new_m, new_v = {}, {}, {}, {}
    for k in MATRICES:
        grads[k], delta[k], new_m[k], new_v[k] = _reduce_adamw(from_chips[k], w[k], m[k], v[k], "adamw_" + k,
                                                               transposed=k in COLUMN_SHARDED)
    zero_row = jnp.zeros((1, D), F32)
    d8, m8, v8 = _adamw_small(_pack_small(w, zero_row), total[:ROW_TAPS], _pack_small(m, zero_row),
                              _pack_small(v, zero_row), "adamw_small")
    grads.update(_unpack_small(total[:ROW_TAPS]))
    delta.update(_unpack_small(d8))
    new_m.update(_unpack_small(m8))
    new_v.update(_unpack_small(v8))
    k = "conv_dw_kernel"
    grads[k] = lax.dynamic_slice_in_dim(total[ROW_TAPS:ROW_TAPS + CW], me * BLK, BLK, axis=1)
    delta[k], new_m[k], new_v[k] = _adamw_small(w[k], grads[k], m[k], v[k], "adamw_taps")

    return (loss, dx0[None], *[grads[k] for k in WEIGHTS], *[delta[k] for k in WEIGHTS],
            *[new_m[k] for k in WEIGHTS], *[new_v[k] for k in WEIGHTS])
```

```python
import functools
import math

import numpy as np
import jax
import jax.numpy as jnp
from jax import lax
from jax.experimental import pallas as pl
from jax.experimental.pallas import tpu as pltpu

F32 = jnp.float32
BF = jnp.bfloat16

D = 1024
F = 2816
INW = 5632
CW = 31
CWP = 32
HD = 64
NQ = 16
NKV = 4
GRP = NQ // NKV
BLK = 128
NBUCKET = 32
EPS = 1e-6
NEG = float(jnp.finfo(jnp.float32).min)
QK_SCALE = 1.0 / math.sqrt(HD)
R_CONV = (0, 2048)
R_QKV = (2048, 3584)
R_Q = (2048, 3072)
R_KV = (3072, 3584)
R_GATE = (3584, 5632)

N_DEV = 8
VMEM_LIMIT_V7X = 56 * 1024 * 1024

ADAM_LR = 0.001
ADAM_B1 = 0.9
ADAM_B2 = 0.999
ADAM_EPS = 1e-08
ADAM_WD = 0.01
ADAM_STEP = 10

NT_DIMS = (((1,), (1,)), ((), ()))
TN_DIMS = (((0,), (0,)), ((), ()))


def _dot(a, b):
    return jnp.dot(a, b, preferred_element_type=F32)


def _dot_nt(a, b):
    return lax.dot_general(a, b, NT_DIMS, preferred_element_type=F32)


def _dot_tn(a, b):
    return lax.dot_general(a, b, TN_DIMS, preferred_element_type=F32)


def _sig(x):
    return 1.0 / (1.0 + jnp.exp(-x))


def _params(n_axes):
    return pltpu.CompilerParams(dimension_semantics=("arbitrary",) * n_axes, vmem_limit_bytes=VMEM_LIMIT_V7X)


def _resident(shape):
    zeros = (0,) * len(shape)
    return pl.BlockSpec(shape, lambda *_: zeros, pipeline_mode=pl.Buffered(1))


def _row_tile(rows, cols):
    return pl.BlockSpec((rows, cols), lambda i: (i, 0))


def _rms_stats(x):
    r = lax.rsqrt(jnp.mean(x * x, axis=-1, keepdims=True) + EPS)
    return r, x * r


def _rms_bwd(dn, x, g):
    r, xh = _rms_stats(x)
    dxh = dn * g
    dx = r * (dxh - xh * jnp.mean(dxh * xh, axis=-1, keepdims=True))
    return dx, jnp.sum(dn * xh, axis=0, keepdims=True)


def _ffn_fwd(x, g, w_in_t, w_out, name, target=None):
    t = x.shape[0]
    tm = min(256, t)
    with_loss = target is not None

    def body(*refs):
        if with_loss:
            x_ref, g_ref, w_ref, wo_ref, t_ref, n_ref, u_ref, dy_ref, sq_ref = refs
        else:
            x_ref, g_ref, w_ref, wo_ref, n_ref, u_ref, xo_ref = refs
        x = x_ref[...]
        r, xh = _rms_stats(x)
        n = (xh * g_ref[...]).astype(BF)
        n_ref[...] = n
        u = _dot_nt(n, w_ref[...])
        u_ref[...] = u.astype(BF)
        a = u[:, :F]
        b = u[:, F:]
        h = (a * _sig(a) * b).astype(BF)
        xo = x + 0.5 * _dot(h, wo_ref[...])
        if with_loss:
            err = xo - t_ref[...]
            dy_ref[...] = err * (1.0 / D)

            @pl.when(pl.program_id(0) == 0)
            def _():
                sq_ref[...] = jnp.zeros_like(sq_ref)

            sq_ref[...] += jnp.sum(err * err, axis=0, keepdims=True)
        else:
            xo_ref[...] = xo

    in_specs = [_row_tile(tm, D), _resident((1, D)), _resident((INW, D)), _resident((F, D))]
    args = [x, g, w_in_t, w_out]
    out_specs = [_row_tile(tm, D), _row_tile(tm, INW), _row_tile(tm, D)]
    out_shape = [jax.ShapeDtypeStruct((t, D), BF), jax.ShapeDtypeStruct((t, INW), BF), jax.ShapeDtypeStruct((t, D), F32)]
    if with_loss:
        in_specs.append(_row_tile(tm, D))
        args.append(target)
        out_specs.append(pl.BlockSpec((1, D), lambda i: (0, 0)))
        out_shape.append(jax.ShapeDtypeStruct((1, D), F32))
    return pl.pallas_call(body, grid=(t // tm,), in_specs=in_specs, out_specs=out_specs, out_shape=out_shape,
                          compiler_params=_params(1), name=name)(*args)


def _ffn_bwd(dxo, x, g, u, w_in_t, w_out, name):
    t = x.shape[0]
    tm = min(256, t)

    def body(dxo_ref, x_ref, g_ref, u_ref, w_ref, wo_ref, dx_ref, du_ref, h_ref, dy_ref, dg_ref):
        dxo = dxo_ref[...]
        dy = (0.5 * dxo).astype(BF)
        dy_ref[...] = dy
        dh = _dot_nt(dy, wo_ref[...])
        a = u_ref[:, :F].astype(F32)
        b = u_ref[:, F:].astype(F32)
        s = _sig(a)
        sa = a * s
        h_ref[...] = (sa * b).astype(BF)
        du_ref[:, :F] = (dh * b * (s * (1.0 + a * (1.0 - s)))).astype(BF)
        du_ref[:, F:] = (dh * sa).astype(BF)
        dn = _dot(du_ref[...], w_ref[...])
        dx, dg = _rms_bwd(dn, x_ref[...], g_ref[...])
        dx_ref[...] = dxo + dx

        @pl.when(pl.program_id(0) == 0)
        def _():
            dg_ref[...] = jnp.zeros_like(dg_ref)

        dg_ref[...] += dg

    return pl.pallas_call(
        body, grid=(t // tm,),
        in_specs=[_row_tile(tm, D), _row_tile(tm, D), _resident((1, D)), _row_tile(tm, INW), _resident((INW, D)),
                  _resident((F, D))],
        out_specs=[_row_tile(tm, D), _row_tile(tm, INW), _row_tile(tm, F), _row_tile(tm, D),
                   pl.BlockSpec((1, D), lambda i: (0, 0))],
        out_shape=[jax.ShapeDtypeStruct((t, D), F32), jax.ShapeDtypeStruct((t, INW), BF), jax.ShapeDtypeStruct((t, F), BF),
                   jax.ShapeDtypeStruct((t, D), BF), jax.ShapeDtypeStruct((1, D), F32)],
        compiler_params=_params(1), name=name)(dxo, x, g, u, w_in_t, w_out)


def _wgrad(lhs, rhs, name, *, lhs_is_transposed, chunk):
    t = rhs.shape[0]
    n = lhs.shape[0] if lhs_is_transposed else lhs.shape[1]
    tm = min(512, t)
    c = min(chunk, n)
    n_tok = t // tm

    def body(l_ref, r_ref, o_ref, acc_ref):
        i = pl.program_id(1)

        @pl.when(i == 0)
        def _():
            acc_ref[...] = jnp.zeros_like(acc_ref)

        lhs_tile = l_ref[...].astype(BF)
        rhs_tile = r_ref[...].astype(BF)
        if lhs_is_transposed:
            acc_ref[...] += _dot(lhs_tile, rhs_tile)
        else:
            acc_ref[...] += _dot_tn(lhs_tile, rhs_tile)

        @pl.when(i == n_tok - 1)
        def _():
            o_ref[...] = acc_ref[...].astype(o_ref.dtype)

    if lhs_is_transposed:
        lhs_spec = pl.BlockSpec((c, tm), lambda j, i: (j, i))
    else:
        lhs_spec = pl.BlockSpec((tm, c), lambda j, i: (i, j))
    return pl.pallas_call(
        body, grid=(n // c, n_tok),
        in_specs=[lhs_spec, pl.BlockSpec((tm, D), lambda j, i: (i, 0))],
        out_specs=pl.BlockSpec((c, D), lambda j, i: (j, 0)),
        out_shape=jax.ShapeDtypeStruct((n, D), BF),
        scratch_shapes=[pltpu.VMEM((c, D), F32)],
        compiler_params=_params(2), name=name)(lhs, rhs)


def _mix_proj(x, g, w_t):
    t = x.shape[0]
    tm = min(256, t)

    def body(x_ref, g_ref, w_ref, hm_ref, uc_ref, gp_ref, qkv_ref):
        r, xh = _rms_stats(x_ref[...])
        hm = (xh * g_ref[...]).astype(BF)
        hm_ref[...] = hm
        uc_ref[...] = _dot_nt(hm, w_ref[R_CONV[0]:R_CONV[1], :]).astype(BF)
        gp_ref[...] = _dot_nt(hm, w_ref[R_GATE[0]:R_GATE[1], :]).astype(BF)
        qkv_ref[...] = _dot_nt(w_ref[R_QKV[0]:R_QKV[1], :], hm).astype(BF)

    return pl.pallas_call(
        body, grid=(t // tm,),
        in_specs=[_row_tile(tm, D), _resident((1, D)), _resident((INW, D))],
        out_specs=[_row_tile(tm, D), _row_tile(tm, 2 * D), _row_tile(tm, 2 * D), pl.BlockSpec((1536, tm), lambda i: (0, i))],
        out_shape=[jax.ShapeDtypeStruct((t, D), BF), jax.ShapeDtypeStruct((t, 2 * D), BF),
                   jax.ShapeDtypeStruct((t, 2 * D), BF), jax.ShapeDtypeStruct((1536, t), BF)],
        compiler_params=_params(1), name="mix_proj")(x, g, w_t)


CONV_HALO = 32
CONV_LEAD = CONV_HALO - (CW - 1)


def _glu(uc):
    uc = uc.astype(F32)
    return uc[:, :D] * _sig(uc[:, D:])


def _ln_stats(zc):
    mu = jnp.mean(zc, axis=-1, keepdims=True)
    zm = zc - mu
    r = lax.rsqrt(jnp.mean(zm * zm, axis=-1, keepdims=True) + EPS)
    return r, zm * r


CONV_SHIFTS = 8
CONV_CHUNK = 32


def _store_shifted(buf, rows):
    for b in range(1, CONV_SHIFTS):
        buf[b, 0:rows - 8, :] = buf[0, pl.ds(b, rows - 8), :]


def _conv_fwd(uc, dwk, dwb, lng, lnb):
    t = uc.shape[0]
    tm = min(512, t)
    per = tm // CONV_HALO
    ext = tm + CONV_HALO

    def body(cur_ref, prev_ref, k_ref, kb_ref, g_ref, b_ref, o_ref, zc_ref, zsh):
        i = pl.program_id(0)
        zsh[0, 0:CONV_HALO, :] = _glu(prev_ref[...]) * (i > 0).astype(F32)
        zsh[0, CONV_HALO:, :] = _glu(cur_ref[...])
        _store_shifted(zsh, ext)

        def chunk(ci, carry):
            r0 = pl.multiple_of(ci * CONV_CHUNK, CONV_CHUNK)
            acc = jnp.zeros((CONV_CHUNK, D), F32) + kb_ref[...]
            for w in range(CW):
                a, b = divmod(CONV_LEAD + w, 8)
                acc = acc + k_ref[w:w + 1, :] * zsh[b, pl.ds(r0 + 8 * a, CONV_CHUNK), :]
            zc_ref[pl.ds(r0, CONV_CHUNK), :] = acc
            r, xh = _ln_stats(acc)
            y = xh * g_ref[...] + b_ref[...]
            o_ref[pl.ds(r0, CONV_CHUNK), :] = (y * _sig(y)).astype(BF)
            return carry

        lax.fori_loop(0, tm // CONV_CHUNK, chunk, 0)

    return pl.pallas_call(
        body, grid=(t // tm,),
        in_specs=[_row_tile(tm, 2 * D),
                  pl.BlockSpec((CONV_HALO, 2 * D), lambda i: (jnp.maximum(i * per - 1, 0), 0)),
                  _resident((CWP, D)), _resident((1, D)), _resident((1, D)), _resident((1, D))],
        out_specs=[_row_tile(tm, D), _row_tile(tm, D)],
        out_shape=[jax.ShapeDtypeStruct((t, D), BF), jax.ShapeDtypeStruct((t, D), F32)],
        scratch_shapes=[pltpu.VMEM((CONV_SHIFTS, ext, D), F32)],
        compiler_params=_params(1), name="conv_fwd")(uc, uc, dwk, dwb, lng, lnb)


def _conv_bwd(uc, zc, dzs, dwk, lng, lnb):
    t = uc.shape[0]
    tm = min(256, t)
    per = tm // CONV_HALO
    n_tiles = t // tm
    ext = tm + CONV_HALO
    last_block = t // CONV_HALO - 1

    def body(cur_ref, prev_ref, zc_ref, zcn_ref, dz_ref, dzn_ref, k_ref, g_ref, b_ref,
             duc_ref, dk_ref, dkb_ref, dg_ref, db_ref, zsh, dsh, dk8):
        i = pl.program_id(0)

        @pl.when(i == 0)
        def _():
            dk8[...] = jnp.zeros_like(dk8)
            dkb_ref[...] = jnp.zeros_like(dkb_ref)
            dg_ref[...] = jnp.zeros_like(dg_ref)
            db_ref[...] = jnp.zeros_like(db_ref)

        has_next = (i < n_tiles - 1).astype(F32)
        zsh[0, 0:CONV_HALO, :] = _glu(prev_ref[...]) * (i > 0).astype(F32)
        zsh[0, CONV_HALO:, :] = _glu(cur_ref[...])
        _store_shifted(zsh, ext)
        gain = g_ref[...]

        def ln_silu_bwd(zc, dzs, live):
            r, xh = _ln_stats(zc)
            y = xh * gain + b_ref[...]
            sy = _sig(y)
            dy = dzs * (sy * (1.0 + y * (1.0 - sy))) * live
            dxh = dy * gain
            dzc = r * (dxh - jnp.mean(dxh, axis=-1, keepdims=True) - xh * jnp.mean(dxh * xh, axis=-1, keepdims=True))
            return dzc, dy, xh

        dzc, dy, xh = ln_silu_bwd(zc_ref[...], dz_ref[...], 1.0)
        dsh[0, 0:tm, :] = dzc
        dg_ref[...] += jnp.sum(dy * xh, axis=0, keepdims=True)
        db_ref[...] += jnp.sum(dy, axis=0, keepdims=True)
        dkb_ref[...] += jnp.sum(dzc, axis=0, keepdims=True)
        dsh[0, tm:, :] = ln_silu_bwd(zcn_ref[...], dzn_ref[...], has_next)[0]
        _store_shifted(dsh, ext)

        def chunk(ci, carry):
            r0 = pl.multiple_of(ci * CONV_CHUNK, CONV_CHUNK)
            dzc_c = dsh[0, pl.ds(r0, CONV_CHUNK), :]
            dz = jnp.zeros((CONV_CHUNK, D), F32)
            for w in range(CW):
                a, b = divmod(CW - 1 - w, 8)
                dz = dz + k_ref[w:w + 1, :] * dsh[b, pl.ds(r0 + 8 * a, CONV_CHUNK), :]
                a, b = divmod(CONV_LEAD + w, 8)
                prod = dzc_c * zsh[b, pl.ds(r0 + 8 * a, CONV_CHUNK), :]
                part = prod[0:8, :]
                for j in range(1, CONV_CHUNK // 8):
                    part = part + prod[8 * j:8 * j + 8, :]
                dk8[w] += part
            ucc = cur_ref[pl.ds(r0, CONV_CHUNK), :].astype(F32)
            sg = _sig(ucc[:, D:])
            duc_ref[pl.ds(r0, CONV_CHUNK), 0:D] = (dz * sg).astype(BF)
            duc_ref[pl.ds(r0, CONV_CHUNK), D:2 * D] = (dz * ucc[:, :D] * sg * (1.0 - sg)).astype(BF)
            return carry

        lax.fori_loop(0, tm // CONV_CHUNK, chunk, 0)

        @pl.when(i == n_tiles - 1)
        def _():
            dk_ref[...] = jnp.sum(dk8[...], axis=1)

    vec = pl.BlockSpec((1, D), lambda i: (0, 0))
    next_halo = pl.BlockSpec((CONV_HALO, D), lambda i: (jnp.minimum((i + 1) * per, last_block), 0))
    return pl.pallas_call(
        body, grid=(n_tiles,),
        in_specs=[_row_tile(tm, 2 * D),
                  pl.BlockSpec((CONV_HALO, 2 * D), lambda i: (jnp.maximum(i * per - 1, 0), 0)),
                  _row_tile(tm, D), next_halo, _row_tile(tm, D), next_halo,
                  _resident((CWP, D)), _resident((1, D)), _resident((1, D))],
        out_specs=[_row_tile(tm, 2 * D), pl.BlockSpec((CWP, D), lambda i: (0, 0)), vec, vec, vec],
        out_shape=[jax.ShapeDtypeStruct((t, 2 * D), BF), jax.ShapeDtypeStruct((CWP, D), F32),
                   jax.ShapeDtypeStruct((1, D), F32), jax.ShapeDtypeStruct((1, D), F32), jax.ShapeDtypeStruct((1, D), F32)],
        scratch_shapes=[pltpu.VMEM((CONV_SHIFTS, ext, D), F32), pltpu.VMEM((CONV_SHIFTS, ext, D), F32),
                        pltpu.VMEM((CWP, 8, D), F32)],
        compiler_params=_params(1), name="conv_bwd")(uc, uc, zc, zc, dzs, dzs, dwk, lng, lnb)


def _norm_rows(xt, g):
    r = lax.rsqrt(jnp.mean(xt * xt, axis=0, keepdims=True) + EPS)
    xh = xt * r
    return xh * g, r, xh


ATT_TQ = 512


def _attn_specs(t, tq):
    per = tq // BLK
    return [pl.BlockSpec((1536, tq), lambda i: (0, i)),
            pl.BlockSpec((512, BLK), lambda i: (2, jnp.maximum(i * per - 1, 0))),
            _resident((HD, 1)), _resident((HD, 1)), _resident((NKV, 1, GRP * BLK)),
            _resident((NKV, 2 * BLK, GRP * BLK)), _resident((2, 2 * BLK, GRP * BLK))]


def _attn_window(hk, sb, qkv_ref, halo_ref, kn_cur, kn_halo):
    v0 = D + NKV * HD + hk * HD
    if sb == 0:
        k_prev = kn_halo[hk]
        v_prev = halo_ref[NKV * HD + hk * HD:NKV * HD + (hk + 1) * HD, :]
    else:
        k_prev = kn_cur[hk][:, (sb - 1) * BLK:sb * BLK]
        v_prev = qkv_ref[v0:v0 + HD, (sb - 1) * BLK:sb * BLK]
    kw = jnp.concatenate([k_prev, kn_cur[hk][:, sb * BLK:(sb + 1) * BLK]], axis=1).astype(BF)
    vw = jnp.concatenate([v_prev, qkv_ref[v0:v0 + HD, sb * BLK:(sb + 1) * BLK]], axis=1)
    return kw, vw


def _attn_probs(kw, qc, bias, mask, sink):
    st = _dot_tn(kw, qc) * QK_SCALE + bias
    st = jnp.where(mask > 0.5, st, NEG)
    m = jnp.maximum(jnp.max(st, axis=0, keepdims=True), sink)
    p = jnp.exp(st - m)
    e_sink = jnp.exp(sink - m)
    inv = 1.0 / (jnp.sum(p, axis=0, keepdims=True) + e_sink)
    return p * inv, e_sink * inv


def _attn_fwd(qkv_t, qg, kg, sink_rows, bias_t, mask_t):
    t = qkv_t.shape[1]
    tq = min(ATT_TQ, t)
    n_sub = tq // BLK

    def body(qkv_ref, halo_ref, qg_ref, kg_ref, sink_ref, bias_ref, mask_ref, o_ref):
        i = pl.program_id(0)
        first = (i == 0).astype(jnp.int32)
        kgain = kg_ref[...]
        qgain = qg_ref[...]
        kn_cur = [_norm_rows(qkv_ref[D + h * HD:D + (h + 1) * HD, :].astype(F32), kgain)[0] for h in range(NKV)]
        kn_halo = [_norm_rows(halo_ref[h * HD:(h + 1) * HD, :].astype(F32), kgain)[0] for h in range(NKV)]
        for hk in range(NKV):
            for sb in range(n_sub):
                cols = slice(sb * BLK, (sb + 1) * BLK)
                kw, vw = _attn_window(hk, sb, qkv_ref, halo_ref, kn_cur, kn_halo)
                qc = jnp.concatenate(
                    [_norm_rows(qkv_ref[(GRP * hk + g) * HD:(GRP * hk + g + 1) * HD, cols].astype(F32), qgain)[0]
                     for g in range(GRP)], axis=1).astype(BF)
                mask = mask_ref[first] if sb == 0 else mask_ref[0]
                p, _ = _attn_probs(kw, qc, bias_ref[hk], mask, sink_ref[hk])
                o = _dot(vw, p.astype(BF))
                for g in range(GRP):
                    head = GRP * hk + g
                    o_ref[head * HD:(head + 1) * HD, cols] = o[:, g * BLK:(g + 1) * BLK].astype(BF)

    return pl.pallas_call(
        body, grid=(t // tq,),
        in_specs=_attn_specs(t, tq),
        out_specs=pl.BlockSpec((D, tq), lambda i: (0, i)),
        out_shape=jax.ShapeDtypeStruct((D, t), BF),
        compiler_params=_params(1), name="attn_fwd")(qkv_t, qkv_t, qg, kg, sink_rows, bias_t, mask_t)


def _attn_bwd(qkv_t, do_t, qg, kg, sink_rows, bias_t, mask_t):
    t = qkv_t.shape[1]
    tq = min(ATT_TQ, t)
    n_sub = tq // BLK
    n_tiles = t // tq

    def body(qkv_ref, halo_ref, do_ref, qg_ref, kg_ref, sink_ref, bias_ref, mask_ref,
             dq_ref, ckv_ref, dqg_ref, dsink_ref, dsacc_ref, qg_scr):
        i = pl.program_id(0)

        @pl.when(i == 0)
        def _():
            qg_scr[...] = jnp.zeros_like(qg_scr)
            dsink_ref[...] = jnp.zeros_like(dsink_ref)
            dsacc_ref[...] = jnp.zeros_like(dsacc_ref)

        first = (i == 0).astype(jnp.int32)
        kgain = kg_ref[...]
        qgain = qg_ref[...]
        kn_cur = [_norm_rows(qkv_ref[D + h * HD:D + (h + 1) * HD, :].astype(F32), kgain)[0] for h in range(NKV)]
        kn_halo = [_norm_rows(halo_ref[h * HD:(h + 1) * HD, :].astype(F32), kgain)[0] for h in range(NKV)]
        dqg = jnp.zeros((HD, BLK), F32)
        for hk in range(NKV):
            for sb in range(n_sub):
                cols = slice(sb * BLK, (sb + 1) * BLK)
                kw, vw = _attn_window(hk, sb, qkv_ref, halo_ref, kn_cur, kn_halo)
                qn, qr, qh = [], [], []
                for g in range(GRP):
                    head = GRP * hk + g
                    n_, r_, h_ = _norm_rows(qkv_ref[head * HD:(head + 1) * HD, cols].astype(F32), qgain)
                    qn.append(n_)
                    qr.append(r_)
                    qh.append(h_)
                qc = jnp.concatenate(qn, axis=1).astype(BF)
                mask = mask_ref[first] if sb == 0 else mask_ref[0]
                p, p_sink = _attn_probs(kw, qc, bias_ref[hk], mask, sink_ref[hk])
                doc = jnp.concatenate([do_ref[(GRP * hk + g) * HD:(GRP * hk + g + 1) * HD, cols] for g in range(GRP)], axis=1)
                dp = _dot_tn(vw, doc)
                delta = jnp.sum(p * dp, axis=0, keepdims=True)
                ds = p * (dp - delta)
                dsink_ref[hk] += -(p_sink * delta)
                dsacc_ref[hk] += ds
                dsb = ds.astype(BF)
                dqc = _dot(kw, dsb) * QK_SCALE
                ckv_ref[sb, hk * HD:(hk + 1) * HD, :] = _dot_nt(qc, dsb) * QK_SCALE
                ckv_ref[sb, NKV * HD + hk * HD:NKV * HD + (hk + 1) * HD, :] = _dot_nt(doc, p.astype(BF))
                for g in range(GRP):
                    head = GRP * hk + g
                    dqn = dqc[:, g * BLK:(g + 1) * BLK]
                    dqh = dqn * qgain
                    dq = qr[g] * (dqh - qh[g] * jnp.mean(dqh * qh[g], axis=0, keepdims=True))
                    dq_ref[head * HD:(head + 1) * HD, cols] = dq.astype(BF)
                    dqg = dqg + dqn * qh[g]
        qg_scr[...] += dqg

        @pl.when(i == n_tiles - 1)
        def _():
            dqg_ref[...] = jnp.sum(qg_scr[...], axis=1, keepdims=True)

    return pl.pallas_call(
        body, grid=(n_tiles,),
        in_specs=_attn_specs(t, tq)[:2] + [pl.BlockSpec((D, tq), lambda i: (0, i))] + _attn_specs(t, tq)[2:],
        out_specs=[pl.BlockSpec((D, tq), lambda i: (0, i)),
                   pl.BlockSpec((n_sub, 2 * NKV * HD, 2 * BLK), lambda i: (i, 0, 0)),
                   pl.BlockSpec((HD, 1), lambda i: (0, 0)),
                   pl.BlockSpec((NKV, 1, GRP * BLK), lambda i: (0, 0, 0)),
                   pl.BlockSpec((NKV, 2 * BLK, GRP * BLK), lambda i: (0, 0, 0))],
        out_shape=[jax.ShapeDtypeStruct((D, t), BF),
                   jax.ShapeDtypeStruct((t // BLK, 2 * NKV * HD, 2 * BLK), F32),
                   jax.ShapeDtypeStruct((HD, 1), F32),
                   jax.ShapeDtypeStruct((NKV, 1, GRP * BLK), F32),
                   jax.ShapeDtypeStruct((NKV, 2 * BLK, GRP * BLK), F32)],
        scratch_shapes=[pltpu.VMEM((HD, BLK), F32)],
        compiler_params=_params(1), name="attn_bwd")(qkv_t, qkv_t, do_t, qg, kg, sink_rows, bias_t, mask_t)


def _kv_combine(ckv, qkv_t, kg):
    nb = ckv.shape[0]
    t = nb * BLK
    rows = NKV * HD

    def body(c_ref, cn_ref, k_ref, kg_ref, o_ref, dkg_ref, kg_scr):
        n = pl.program_id(0)

        @pl.when(n == 0)
        def _():
            kg_scr[...] = jnp.zeros_like(kg_scr)

        has_next = (n < nb - 1).astype(F32)
        d = c_ref[0, :, BLK:] + cn_ref[0, :, :BLK] * has_next
        o_ref[rows:, :] = d[rows:, :].astype(BF)
        kgain = kg_ref[...]
        dkg = jnp.zeros((HD, BLK), F32)
        for h in range(NKV):
            _, r, kh = _norm_rows(k_ref[h * HD:(h + 1) * HD, :].astype(F32), kgain)
            dkn = d[h * HD:(h + 1) * HD, :]
            dkh = dkn * kgain
            o_ref[h * HD:(h + 1) * HD, :] = (r * (dkh - kh * jnp.mean(dkh * kh, axis=0, keepdims=True))).astype(BF)
            dkg = dkg + dkn * kh
        kg_scr[...] += dkg

        @pl.when(n == nb - 1)
        def _():
            dkg_ref[...] = jnp.sum(kg_scr[...], axis=1, keepdims=True)

    return pl.pallas_call(
        body, grid=(nb,),
        in_specs=[pl.BlockSpec((1, 2 * rows, 2 * BLK), lambda n: (n, 0, 0)),
                  pl.BlockSpec((1, 2 * rows, 2 * BLK), lambda n: (jnp.minimum(n + 1, nb - 1), 0, 0)),
                  pl.BlockSpec((rows, BLK), lambda n: (D // rows, n)),
                  _resident((HD, 1))],
        out_specs=[pl.BlockSpec((2 * rows, BLK), lambda n: (0, n)), pl.BlockSpec((HD, 1), lambda n: (0, 0))],
        out_shape=[jax.ShapeDtypeStruct((2 * rows, t), BF), jax.ShapeDtypeStruct((HD, 1), F32)],
        scratch_shapes=[pltpu.VMEM((HD, BLK), F32)],
        compiler_params=_params(1), name="kv_combine")(ckv, ckv, qkv_t, kg)


def _group_lane_sums(v):
    lane_group = lax.broadcasted_iota(jnp.int32, (1, GRP * BLK), 1) // BLK
    col = lax.broadcasted_iota(jnp.int32, (1, BLK), 1)
    out = jnp.zeros((NKV, BLK), F32)
    for g in range(GRP):
        s = jnp.sum(jnp.where(lane_group == g, v, 0.0), axis=1, keepdims=True)
        out = jnp.where(col == g, s, out)
    return out


def _bias_grad(dsacc, onehot_t):
    def body(ds_ref, oh_ref, o_ref):
        oh = jnp.concatenate([oh_ref[0]] * GRP, axis=1)
        o_ref[0] = _group_lane_sums(jnp.sum(ds_ref[...] * oh[None], axis=1))

    return pl.pallas_call(
        body, grid=(NBUCKET,),
        in_specs=[_resident((NKV, 2 * BLK, GRP * BLK)), pl.BlockSpec((1, 2 * BLK, BLK), lambda b: (b, 0, 0))],
        out_specs=pl.BlockSpec((1, NKV, BLK), lambda b: (b, 0, 0)),
        out_shape=jax.ShapeDtypeStruct((NBUCKET, NKV, BLK), F32),
        compiler_params=_params(1), name="bias_grad")(dsacc, onehot_t)


def _sink_grad(dsink_rows):
    def body(d_ref, o_ref):
        o_ref[...] = _group_lane_sums(d_ref[:, 0, :])

    return pl.pallas_call(body, out_shape=jax.ShapeDtypeStruct((NKV, BLK), F32), name="sink_grad")(dsink_rows)


def _mix_out(zs, o_t, gp, x, w_cp, w_o, w_out):
    t = x.shape[0]
    tm = min(256, t)

    def body(zs_ref, ot_ref, gp_ref, x_ref, wcp_ref, wo_ref, wout_ref, xo_ref, a_ref, b_ref, m_ref):
        a = _dot(zs_ref[...], wcp_ref[...])
        b = _dot_tn(ot_ref[...], wo_ref[...])
        a_ref[...] = a.astype(BF)
        b_ref[...] = b.astype(BF)
        merged = (_sig(gp_ref[:, :D].astype(F32)) * a + _sig(gp_ref[:, D:].astype(F32)) * b).astype(BF)
        m_ref[...] = merged
        xo_ref[...] = x_ref[...] + _dot(merged, wout_ref[...])

    return pl.pallas_call(
        body, grid=(t // tm,),
        in_specs=[_row_tile(tm, D), pl.BlockSpec((D, tm), lambda i: (0, i)), _row_tile(tm, 2 * D), _row_tile(tm, D),
                  _resident((D, D)), _resident((D, D)), _resident((D, D))],
        out_specs=[_row_tile(tm, D)] * 4,
        out_shape=[jax.ShapeDtypeStruct((t, D), F32)] + [jax.ShapeDtypeStruct((t, D), BF)] * 3,
        compiler_params=_params(1), name="mix_out")(zs, o_t, gp, x, w_cp, w_o, w_out)


def _mix_out_bwd(dx, a, b, gp, w_cp, w_o, w_out):
    t = dx.shape[0]
    tm = min(256, t)

    def body(dx_ref, a_ref, b_ref, gp_ref, wcp_ref, wo_ref, wout_ref, dzs_ref, dot_ref, dgp_ref, da_ref, db_ref, dxb_ref):
        dxb = dx_ref[...].astype(BF)
        dxb_ref[...] = dxb
        dm = _dot_nt(dxb, wout_ref[...])
        gc = _sig(gp_ref[:, :D].astype(F32))
        ga = _sig(gp_ref[:, D:].astype(F32))
        da = (dm * gc).astype(BF)
        db = (dm * ga).astype(BF)
        da_ref[...] = da
        db_ref[...] = db
        dgp_ref[:, :D] = (dm * a_ref[...].astype(F32) * gc * (1.0 - gc)).astype(BF)
        dgp_ref[:, D:] = (dm * b_ref[...].astype(F32) * ga * (1.0 - ga)).astype(BF)
        dzs_ref[...] = _dot_nt(da, wcp_ref[...])
        dot_ref[...] = _dot_nt(wo_ref[...], db).astype(BF)

    return pl.pallas_call(
        body, grid=(t // tm,),
        in_specs=[_row_tile(tm, D), _row_tile(tm, D), _row_tile(tm, D), _row_tile(tm, 2 * D),
                  _resident((D, D)), _resident((D, D)), _resident((D, D))],
        out_specs=[_row_tile(tm, D), pl.BlockSpec((D, tm), lambda i: (0, i)), _row_tile(tm, 2 * D),
                   _row_tile(tm, D), _row_tile(tm, D), _row_tile(tm, D)],
        out_shape=[jax.ShapeDtypeStruct((t, D), F32), jax.ShapeDtypeStruct((D, t), BF), jax.ShapeDtypeStruct((t, 2 * D), BF),
                   jax.ShapeDtypeStruct((t, D), BF), jax.ShapeDtypeStruct((t, D), BF), jax.ShapeDtypeStruct((t, D), BF)],
        compiler_params=_params(1), name="mix_out_bwd")(dx, a, b, gp, w_cp, w_o, w_out)


def _mix_proj_bwd(dxo, duc, dq_t, dkv_t, dgp, x, g, w_t):
    t = x.shape[0]
    tm = min(256, t)

    def body(dxo_ref, duc_ref, dq_ref, dkv_ref, dgp_ref, x_ref, g_ref, w_ref, dx_ref, dg_ref):
        dn = _dot(duc_ref[...], w_ref[R_CONV[0]:R_CONV[1], :])
        dn = dn + _dot(dgp_ref[...], w_ref[R_GATE[0]:R_GATE[1], :])
        dn = dn + _dot_tn(dq_ref[...], w_ref[R_Q[0]:R_Q[1], :])
        dn = dn + _dot_tn(dkv_ref[...], w_ref[R_KV[0]:R_KV[1], :])
        dx, dg = _rms_bwd(dn, x_ref[...], g_ref[...])
        dx_ref[...] = dxo_ref[...] + dx

        @pl.when(pl.program_id(0) == 0)
        def _():
            dg_ref[...] = jnp.zeros_like(dg_ref)

        dg_ref[...] += dg

    return pl.pallas_call(
        body, grid=(t // tm,),
        in_specs=[_row_tile(tm, D), _row_tile(tm, 2 * D), pl.BlockSpec((D, tm), lambda i: (0, i)),
                  pl.BlockSpec((2 * NKV * HD, tm), lambda i: (0, i)), _row_tile(tm, 2 * D), _row_tile(tm, D),
                  _resident((1, D)), _resident((INW, D))],
        out_specs=[_row_tile(tm, D), pl.BlockSpec((1, D), lambda i: (0, 0))],
        out_shape=[jax.ShapeDtypeStruct((t, D), F32), jax.ShapeDtypeStruct((1, D), F32)],
        compiler_params=_params(1), name="mix_proj_bwd")(dxo, duc, dq_t, dkv_t, dgp, x, g, w_t)


def _attention_tables():
    kj = np.arange(2 * BLK)[:, None]
    qi = np.arange(BLK)[None, :]
    dist = qi + BLK - kj
    in_win = (dist >= 0) & (dist < BLK)
    dpos = np.maximum(dist, 0)
    max_exact = NBUCKET // 2
    dfl = np.maximum(dpos, 1).astype(np.float32)
    large = max_exact + (np.log(dfl / np.float32(max_exact)) / np.float32(math.log(BLK / max_exact))
                         * np.float32(NBUCKET - max_exact)).astype(np.int32)
    large = np.minimum(large, NBUCKET - 1)
    bucket = np.where(dpos < max_exact, dpos, large)
    onehot = (bucket[None] == np.arange(NBUCKET)[:, None, None]).astype(np.float32)
    mask = in_win.astype(np.float32)
    mask_first = mask * (kj >= BLK)
    masks = np.stack([np.tile(mask, (1, GRP)), np.tile(mask_first, (1, GRP))])
    return onehot, masks


def _bias_table(rel_bias, onehot):
    tab = jnp.einsum("bkq,bh->hkq", onehot, rel_bias, precision=lax.Precision.HIGHEST)
    tab = tab.reshape(NKV, GRP, 2 * BLK, BLK)
    return jnp.transpose(tab, (0, 2, 1, 3)).reshape(NKV, 2 * BLK, GRP * BLK)


def _local_step(x, target, vec, mats):
    onehot_np, masks_np = _attention_tables()
    onehot = jnp.asarray(onehot_np)
    masks = jnp.asarray(masks_np)
    bias_t = _bias_table(vec["rel_bias"], onehot)
    sink_rows = jnp.repeat(vec["attn_sinks"].reshape(NKV, 1, GRP), BLK, axis=2)
    qg = vec["q_norm"].reshape(HD, 1)
    kg = vec["k_norm"].reshape(HD, 1)
    g1 = vec["ffn1_norm"].reshape(1, D)
    gm = vec["mix_norm"].reshape(1, D)
    g2 = vec["ffn2_norm"].reshape(1, D)
    dwk = jnp.pad(vec["conv_dw_kernel"], ((0, CWP - CW), (0, 0)))
    dwb = vec["conv_dw_bias"].reshape(1, D)
    lng = vec["conv_ln_g"].reshape(1, D)
    lnb = vec["conv_ln_b"].reshape(1, D)

    n1, u1, x1 = _ffn_fwd(x, g1, mats["ffn1_w_in"], mats["ffn1_w_out"], "ffn1_fwd")
    hm, uc, gp, qkv_t = _mix_proj(x1, gm, mats["w_in"])
    zs, zc = _conv_fwd(uc, dwk, dwb, lng, lnb)
    o_t = _attn_fwd(qkv_t, qg, kg, sink_rows, bias_t, masks)
    x2, a, b, merged = _mix_out(zs, o_t, gp, x1, mats["conv_w_proj"], mats["attn_w_o"], mats["w_out"])
    n2, u2, dx3, sq = _ffn_fwd(x2, g2, mats["ffn2_w_in"], mats["ffn2_w_out"], "ffn2_fwd", target=target)

    gv, gm_ = {}, {}
    dx2, du2, h2, dy2, gv["ffn2_norm"] = _ffn_bwd(dx3, x2, g2, u2, mats["ffn2_w_in"], mats["ffn2_w_out"], "ffn2_bwd")
    gm_["ffn2_w_in"] = _wgrad(du2, n2, "ffn2_dw_in", lhs_is_transposed=False, chunk=1408)
    gm_["ffn2_w_out"] = _wgrad(h2, dy2, "ffn2_dw_out", lhs_is_transposed=False, chunk=1408)

    dzs, do_t, dgp, da, db, dx2b = _mix_out_bwd(dx2, a, b, gp, mats["conv_w_proj"], mats["attn_w_o"], mats["w_out"])
    gm_["w_out"] = _wgrad(merged, dx2b, "mix_dw_out", lhs_is_transposed=False, chunk=1024)
    gm_["conv_w_proj"] = _wgrad(zs, da, "mix_dw_cp", lhs_is_transposed=False, chunk=1024)
    gm_["attn_w_o"] = _wgrad(o_t, db, "mix_dw_o", lhs_is_transposed=True, chunk=1024)

    dq_t, ckv, dqg, dsink_rows, dsacc = _attn_bwd(qkv_t, do_t, qg, kg, sink_rows, bias_t, masks)
    dkv_t, dkg = _kv_combine(ckv, qkv_t, kg)
    gv["q_norm"] = dqg.reshape(HD)
    gv["k_norm"] = dkg.reshape(HD)
    gv["attn_sinks"] = _sink_grad(dsink_rows)[:, :GRP].reshape(NQ)
    gv["rel_bias"] = _bias_grad(dsacc, onehot)[:, :, :GRP].reshape(NBUCKET, NQ)

    duc, dk_conv, gv["conv_dw_bias"], gv["conv_ln_g"], gv["conv_ln_b"] = _conv_bwd(uc, zc, dzs, dwk, lng, lnb)
    gv["conv_dw_kernel"] = dk_conv[:CW]

    dx1, gv["mix_norm"] = _mix_proj_bwd(dx2, duc, dq_t, dkv_t, dgp, x1, gm, mats["w_in"])
    gm_["w_in"] = jnp.concatenate([
        _wgrad(duc, hm, "mix_dw_conv", lhs_is_transposed=False, chunk=1024),
        _wgrad(dq_t, hm, "mix_dw_q", lhs_is_transposed=True, chunk=1024),
        _wgrad(dkv_t, hm, "mix_dw_kv", lhs_is_transposed=True, chunk=512),
        _wgrad(dgp, hm, "mix_dw_gate", lhs_is_transposed=False, chunk=1024)], axis=0)

    dx0, du1, h1, dy1, gv["ffn1_norm"] = _ffn_bwd(dx1, x, g1, u1, mats["ffn1_w_in"], mats["ffn1_w_out"], "ffn1_bwd")
    gm_["ffn1_w_in"] = _wgrad(du1, n1, "ffn1_dw_in", lhs_is_transposed=False, chunk=1408)
    gm_["ffn1_w_out"] = _wgrad(h1, dy1, "ffn1_dw_out", lhs_is_transposed=False, chunk=1408)
    for k in ("ffn1_norm", "mix_norm", "ffn2_norm", "conv_dw_bias", "conv_ln_g", "conv_ln_b"):
        gv[k] = gv[k].reshape(D)
    return sq, dx0, gv, gm_


MESH_ID = pl.DeviceIdType.MESH
ANY = pl.BlockSpec(memory_space=pl.ANY)


def _position():
    return lax.axis_index("x"), lax.axis_index("y"), lax.axis_index("c")


def _shard_rows(ref, index, rows):
    return ref.at[pl.ds(pl.multiple_of(index * rows, 16), rows), :]


def _prep(weights, transposed):
    n = len(weights)

    def body(*refs):
        for k in range(n):
            w = refs[k][...]
            refs[n + k][...] = (w.T if transposed[k] else w).astype(BF)

    out_shape = [jax.ShapeDtypeStruct(w.shape[::-1] if tr else w.shape, BF) for w, tr in zip(weights, transposed)]
    return pl.pallas_call(body, out_shape=out_shape, compiler_params=pltpu.CompilerParams(vmem_limit_bytes=VMEM_LIMIT_V7X),
                          name="prep")(*weights)


def _all_gather(shards):
    n = len(shards)
    rows = [s.shape[0] for s in shards]

    def body(*refs):
        ins, outs = refs[:n], refs[n:2 * n]
        send_sems, recv_sems, local_sems = refs[2 * n:]
        x, y, c = _position()
        me, sibling = (x, y, c), (x, y, 1 - c)
        chips = [(1 - x, y), (x, 1 - y), (1 - x, 1 - y)]

        def block(k, dev):
            return _shard_rows(outs[k], 4 * dev[0] + 2 * dev[1] + dev[2], rows[k])

        def copy(k, j, dev, to, src=None):
            return pltpu.make_async_remote_copy(
                src_ref=block(k, dev) if src is None else src, dst_ref=block(k, dev),
                send_sem=send_sems.at[k, j], recv_sem=recv_sems.at[k, j], device_id=to, device_id_type=MESH_ID)

        mine = [pltpu.make_async_copy(ins[k], block(k, me), local_sems.at[k]) for k in range(n)]
        for cp in mine:
            cp.start()
        first = []
        for k in range(n):
            for j, chip in enumerate(chips):
                first.append(copy(k, 1 + j, me, (*chip, c), src=ins[k]))
            first.append(copy(k, 0, me, sibling, src=ins[k]))
        for cp in first:
            cp.start()
        passed = []
        for j, chip in enumerate(chips):
            for k in range(n):
                copy(k, 1 + j, (*chip, c), me).wait_recv()
                cp = copy(k, 4 + j, (*chip, c), sibling)
                cp.start()
                passed.append(cp)
        for k in range(n):
            copy(k, 0, sibling, me).wait_recv()
        for j, chip in enumerate(chips):
            for k in range(n):
                copy(k, 4 + j, (*chip, 1 - c), me).wait_recv()
        for cp in first + passed:
            cp.wait_send()
        for cp in mine:
            cp.wait()

    return pl.pallas_call(
        body,
        out_shape=[jax.ShapeDtypeStruct((N_DEV * s.shape[0], s.shape[1]), s.dtype) for s in shards],
        in_specs=[ANY] * n, out_specs=[ANY] * n,
        scratch_shapes=[pltpu.SemaphoreType.DMA((n, 7)), pltpu.SemaphoreType.DMA((n, 7)), pltpu.SemaphoreType.DMA((n,))],
        name="all_gather")(*shards)


def _rs_pair(grads):
    n = len(grads)
    rows = [g.shape[0] // N_DEV for g in grads]

    def body(*refs):
        ins, outs = refs[:n], refs[n:2 * n]
        send_sems, recv_sems = refs[2 * n:]
        x, y, c = _position()
        copies = []
        for k in range(n):
            for q in range(4):
                copies.append(pltpu.make_async_remote_copy(
                    src_ref=_shard_rows(ins[k], 2 * q + 1 - c, rows[k]), dst_ref=_shard_rows(outs[k], q, rows[k]),
                    send_sem=send_sems.at[k, q], recv_sem=recv_sems.at[k, q], device_id=(x, y, 1 - c),
                    device_id_type=MESH_ID))
        for cp in copies:
            cp.start()
        for cp in copies:
            cp.wait()

    return pl.pallas_call(
        body, out_shape=[jax.ShapeDtypeStruct((4 * r, g.shape[1]), g.dtype) for g, r in zip(grads, rows)],
        in_specs=[ANY] * n, out_specs=[ANY] * n,
        scratch_shapes=[pltpu.SemaphoreType.DMA((n, 4)), pltpu.SemaphoreType.DMA((n, 4))],
        name="rs_pair")(*grads)


def _pair_add(grad, received, core):
    r = received.shape[0] // 4
    tr = 352 if r % 352 == 0 else r
    per = r // tr

    def body(c_ref, g_ref, r_ref, o_ref):
        o_ref[...] = (g_ref[...].astype(F32) + r_ref[...].astype(F32)).astype(BF)

    return pl.pallas_call(
        body,
        grid_spec=pltpu.PrefetchScalarGridSpec(
            num_scalar_prefetch=1, grid=(4, per),
            in_specs=[pl.BlockSpec((tr, D), lambda q, i, c_ref: ((2 * q + c_ref[0]) * per + i, 0)),
                      pl.BlockSpec((tr, D), lambda q, i, c_ref: (q * per + i, 0))],
            out_specs=pl.BlockSpec((tr, D), lambda q, i, c_ref: (q * per + i, 0))),
        out_shape=jax.ShapeDtypeStruct(received.shape, BF),
        compiler_params=_params(2), name=f"pair_add_{r}")(core, grad, received)


def _rs_chips(partials):
    n = len(partials)
    rows = [p.shape[0] // 4 for p in partials]

    def body(*refs):
        ins, outs = refs[:n], refs[n:2 * n]
        send_sems, recv_sems, local_sems = refs[2 * n:]
        x, y, c = _position()
        my_chip = 2 * x + y
        chips = [(1 - x, y), (x, 1 - y), (1 - x, 1 - y)]
        mine = [pltpu.make_async_copy(_shard_rows(ins[k], my_chip, rows[k]), _shard_rows(outs[k], my_chip, rows[k]),
                                      local_sems.at[k]) for k in range(n)]
        for cp in mine:
            cp.start()
        sends, recvs = [], []
        for k in range(n):
            for j, chip in enumerate(chips):
                their = 2 * chip[0] + chip[1]
                sends.append(pltpu.make_async_remote_copy(
                    src_ref=_shard_rows(ins[k], their, rows[k]), dst_ref=_shard_rows(outs[k], my_chip, rows[k]),
                    send_sem=send_sems.at[k, j], recv_sem=recv_sems.at[k, j], device_id=(*chip, c), device_id_type=MESH_ID))
                recvs.append(pltpu.make_async_remote_copy(
                    src_ref=_shard_rows(ins[k], their, rows[k]), dst_ref=_shard_rows(outs[k], their, rows[k]),
                    send_sem=send_sems.at[k, j], recv_sem=recv_sems.at[k, j], device_id=(*chip, c), device_id_type=MESH_ID))
        for cp in sends:
            cp.start()
        for cp in recvs:
            cp.wait_recv()
        for cp in sends:
            cp.wait_send()
        for cp in mine:
            cp.wait()

    return pl.pallas_call(
        body, out_shape=[jax.ShapeDtypeStruct(p.shape, p.dtype) for p in partials],
        in_specs=[ANY] * n, out_specs=[ANY] * n,
        scratch_shapes=[pltpu.SemaphoreType.DMA((n, 3)), pltpu.SemaphoreType.DMA((n, 3)), pltpu.SemaphoreType.DMA((n,))],
        name="rs_chips")(*partials)


def _all_reduce_small(payload):
    r = payload.shape[0]

    def body(in_ref, out_ref, land_ref, send_sems, recv_sems):
        x, y, c = _position()
        me = 4 * x + 2 * y + c
        land_ref[me] = in_ref[...]
        copies = []
        for k in range(1, N_DEV):
            peer = (x ^ (k >> 2), y ^ ((k >> 1) & 1), c ^ (k & 1))
            copies.append(pltpu.make_async_remote_copy(
                src_ref=in_ref, dst_ref=land_ref.at[me], send_sem=send_sems.at[k - 1], recv_sem=recv_sems.at[k - 1],
                device_id=peer, device_id_type=MESH_ID))
        for cp in copies:
            cp.start()
        for k in range(1, N_DEV):
            peer_index = me ^ k
            pltpu.make_async_remote_copy(
                src_ref=in_ref, dst_ref=land_ref.at[peer_index], send_sem=send_sems.at[k - 1], recv_sem=recv_sems.at[k - 1],
                device_id=(x, y, c), device_id_type=MESH_ID).wait_recv()
        for cp in copies:
            cp.wait_send()
        acc = land_ref[0]
        for d in range(1, N_DEV):
            acc = acc + land_ref[d]
        out_ref[...] = acc

    return pl.pallas_call(
        body, out_shape=jax.ShapeDtypeStruct((r, D), F32),
        in_specs=[pl.BlockSpec(memory_space=pltpu.VMEM)], out_specs=pl.BlockSpec(memory_space=pltpu.VMEM),
        scratch_shapes=[pltpu.VMEM((N_DEV, r, D), F32), pltpu.SemaphoreType.DMA((N_DEV - 1,)),
                        pltpu.SemaphoreType.DMA((N_DEV - 1,))],
        name="all_reduce_small")(payload)


def _adamw_math(w, g, m, v):
    m = ADAM_B1 * m + (1.0 - ADAM_B1) * g
    v = ADAM_B2 * v + (1.0 - ADAM_B2) * (g * g)
    m_hat = m / (1.0 - ADAM_B1 ** ADAM_STEP)
    v_hat = v / (1.0 - ADAM_B2 ** ADAM_STEP)
    delta = -ADAM_LR * (m_hat / (jnp.sqrt(v_hat) + ADAM_EPS) + ADAM_WD * w)
    return delta, m, v


def _reduce_adamw(received, w, m, v, name, transposed):
    r = received.shape[0] // 4

    def body(r_ref, w_ref, m_ref, v_ref, g_ref, d_ref, nm_ref, nv_ref):
        g = r_ref[0:r, :].astype(F32)
        for q in range(1, 4):
            g = g + r_ref[q * r:(q + 1) * r, :].astype(F32)
        if transposed:
            g = g.T
        g_ref[...] = g
        d_ref[...], nm_ref[...], nv_ref[...] = _adamw_math(w_ref[...], g, m_ref[...], v_ref[...])

    return pl.pallas_call(body, out_shape=[jax.ShapeDtypeStruct(w.shape, F32)] * 4,
                          compiler_params=pltpu.CompilerParams(vmem_limit_bytes=VMEM_LIMIT_V7X), name=name)(received, w, m, v)


def _adamw_small(w, g, m, v, name):
    def body(w_ref, g_ref, m_ref, v_ref, d_ref, nm_ref, nv_ref):
        d_ref[...], nm_ref[...], nv_ref[...] = _adamw_math(w_ref[...], g_ref[...], m_ref[...], v_ref[...])

    return pl.pallas_call(body, out_shape=[jax.ShapeDtypeStruct(w.shape, F32)] * 3, name=name)(w, g, m, v)


WEIGHTS = ("ffn1_norm", "ffn1_w_in", "ffn1_w_out", "mix_norm", "w_in", "conv_dw_kernel", "conv_dw_bias", "conv_ln_g",
           "conv_ln_b", "conv_w_proj", "q_norm", "k_norm", "attn_sinks", "rel_bias", "attn_w_o", "w_out", "ffn2_norm",
           "ffn2_w_in", "ffn2_w_out")
MATRICES = ("ffn1_w_in", "ffn1_w_out", "w_in", "conv_w_proj", "attn_w_o", "w_out", "ffn2_w_in", "ffn2_w_out")
COLUMN_SHARDED = ("ffn1_w_in", "w_in", "ffn2_w_in")
ROW_VECTORS = ("ffn1_norm", "mix_norm", "conv_dw_bias", "conv_ln_g", "conv_ln_b", "ffn2_norm")
PACKED = (("q_norm", HD), ("k_norm", HD), ("attn_sinks", NQ), ("rel_bias", NBUCKET * NQ))
ROW_PACKED = len(ROW_VECTORS)
ROW_LOSS = ROW_PACKED + 1
ROW_TAPS = 8
PAYLOAD_ROWS = ROW_TAPS + CWP


def _pack_small(values, last_row):
    packed = jnp.concatenate([values[k].reshape(-1) for k, _ in PACKED])
    packed = jnp.pad(packed, (0, D - packed.shape[0])).reshape(1, D)
    return jnp.concatenate([values[k].reshape(1, D) for k in ROW_VECTORS] + [packed, last_row], axis=0)


def _unpack_small(rows):
    out = {k: rows[i] for i, k in enumerate(ROW_VECTORS)}
    at = 0
    for k, size in PACKED:
        out[k] = rows[ROW_PACKED, at:at + size]
        at += size
    out["rel_bias"] = out["rel_bias"].reshape(NBUCKET, NQ)
    return out


def kernel(x, ffn1_norm, ffn1_w_in, ffn1_w_out, mix_norm, w_in, conv_dw_kernel, conv_dw_bias, conv_ln_g, conv_ln_b, conv_w_proj, q_norm, k_norm, attn_sinks, rel_bias, attn_w_o, w_out, ffn2_norm, ffn2_w_in, ffn2_w_out, loss_target, m_ffn1_norm, m_ffn1_w_in, m_ffn1_w_out, m_mix_norm, m_w_in, m_conv_dw_kernel, m_conv_dw_bias, m_conv_ln_g, m_conv_ln_b, m_conv_w_proj, m_q_norm, m_k_norm, m_attn_sinks, m_rel_bias, m_attn_w_o, m_w_out, m_ffn2_norm, m_ffn2_w_in, m_ffn2_w_out, v_ffn1_norm, v_ffn1_w_in, v_ffn1_w_out, v_mix_norm, v_w_in, v_conv_dw_kernel, v_conv_dw_bias, v_conv_ln_g, v_conv_ln_b, v_conv_w_proj, v_q_norm, v_k_norm, v_attn_sinks, v_rel_bias, v_attn_w_o, v_w_out, v_ffn2_norm, v_ffn2_w_in, v_ffn2_w_out):
    w = dict(ffn1_norm=ffn1_norm, ffn1_w_in=ffn1_w_in, ffn1_w_out=ffn1_w_out, mix_norm=mix_norm, w_in=w_in,
             conv_dw_kernel=conv_dw_kernel, conv_dw_bias=conv_dw_bias, conv_ln_g=conv_ln_g, conv_ln_b=conv_ln_b,
             conv_w_proj=conv_w_proj, q_norm=q_norm, k_norm=k_norm, attn_sinks=attn_sinks, rel_bias=rel_bias,
             attn_w_o=attn_w_o, w_out=w_out, ffn2_norm=ffn2_norm, ffn2_w_in=ffn2_w_in, ffn2_w_out=ffn2_w_out)
    m = dict(ffn1_norm=m_ffn1_norm, ffn1_w_in=m_ffn1_w_in, ffn1_w_out=m_ffn1_w_out, mix_norm=m_mix_norm, w_in=m_w_in,
             conv_dw_kernel=m_conv_dw_kernel, conv_dw_bias=m_conv_dw_bias, conv_ln_g=m_conv_ln_g, conv_ln_b=m_conv_ln_b,
             conv_w_proj=m_conv_w_proj, q_norm=m_q_norm, k_norm=m_k_norm, attn_sinks=m_attn_sinks, rel_bias=m_rel_bias,
             attn_w_o=m_attn_w_o, w_out=m_w_out, ffn2_norm=m_ffn2_norm, ffn2_w_in=m_ffn2_w_in, ffn2_w_out=m_ffn2_w_out)
    v = dict(ffn1_norm=v_ffn1_norm, ffn1_w_in=v_ffn1_w_in, ffn1_w_out=v_ffn1_w_out, mix_norm=v_mix_norm, w_in=v_w_in,
             conv_dw_kernel=v_conv_dw_kernel, conv_dw_bias=v_conv_dw_bias, conv_ln_g=v_conv_ln_g, conv_ln_b=v_conv_ln_b,
             conv_w_proj=v_conv_w_proj, q_norm=v_q_norm, k_norm=v_k_norm, attn_sinks=v_attn_sinks, rel_bias=v_rel_bias,
             attn_w_o=v_attn_w_o, w_out=v_w_out, ffn2_norm=v_ffn2_norm, ffn2_w_in=v_ffn2_w_in, ffn2_w_out=v_ffn2_w_out)
    me = 4 * lax.axis_index("x") + 2 * lax.axis_index("y") + lax.axis_index("c")
    core = lax.axis_index("c").astype(jnp.int32).reshape(1)

    shards = _prep([w[k] for k in MATRICES], [k in COLUMN_SHARDED for k in MATRICES])
    taps_shard = jnp.pad(conv_dw_kernel, ((0, CWP - CW), (0, 0)))
    gathered = _all_gather(list(shards) + [taps_shard])
    mats = dict(zip(MATRICES, gathered[:-1]))
    taps = jnp.transpose(gathered[-1].reshape(N_DEV, CWP, BLK), (1, 0, 2)).reshape(CWP, D)[:CW]

    vec = {k: w[k] for k in WEIGHTS if k not in MATRICES}
    vec["conv_dw_kernel"] = taps
    sq, dx0, gv, gmat = _local_step(x[0], loss_target[0], vec, mats)

    from_sibling = _rs_pair([gmat[k] for k in MATRICES])
    partials = [_pair_add(gmat[k], r, core) for k, r in zip(MATRICES, from_sibling)]
    from_chips = dict(zip(MATRICES, _rs_chips(partials)))

    payload = jnp.concatenate([_pack_small(gv, sq), jnp.pad(gv["conv_dw_kernel"], ((0, CWP - CW), (0, 0)))], axis=0)
    total = _all_reduce_small(payload)
    loss = (0.5 / D) * jnp.sum(total[ROW_LOSS])

    grads, delta, new_m, new_v = {}, {}, {}, {}
    for k in MATRICES:
        grads[k], delta[k], new_m[k], new_v[k] = _reduce_adamw(from_chips[k], w[k], m[k], v[k], "adamw_" + k,
                                                               transposed=k in COLUMN_SHARDED)
    zero_row = jnp.zeros((1, D), F32)
    d8, m8, v8 = _adamw_small(_pack_small(w, zero_row), total[:ROW_TAPS], _pack_small(m, zero_row),
                              _pack_small(v, zero_row), "adamw_small")
    grads.update(_unpack_small(total[:ROW_TAPS]))
    delta.update(_unpack_small(d8))
    new_m.update(_unpack_small(m8))
    new_v.update(_unpack_small(v8))
    k = "conv_dw_kernel"
    grads[k] = lax.dynamic_slice_in_dim(total[ROW_TAPS:ROW_TAPS + CW], me * BLK, BLK, axis=1)
    delta[k], new_m[k], new_v[k] = _adamw_small(w[k], grads[k], m[k], v[k], "adamw_taps")

    return (loss, dx0[None], *[grads[k] for k in WEIGHTS], *[delta[k] for k in WEIGHTS],
            *[new_m[k] for k in WEIGHTS], *[new_v[k] for k in WEIGHTS])
```

```python
import functools
import math

import numpy as np
import jax
import jax.numpy as jnp
from jax import lax
from jax.experimental import pallas as pl
from jax.experimental.pallas import tpu as pltpu

F32 = jnp.float32
BF = jnp.bfloat16

D = 1024
F = 2816
INW = 5632
CW = 31
CWP = 32
HD = 64
NQ = 16
NKV = 4
GRP = NQ // NKV
BLK = 128
NBUCKET = 32
EPS = 1e-6
NEG = float(jnp.finfo(jnp.float32).min)
QK_SCALE = 1.0 / math.sqrt(HD)
R_CONV = (0, 2048)
R_QKV = (2048, 3584)
R_Q = (2048, 3072)
R_KV = (3072, 3584)
R_GATE = (3584, 5632)

N_DEV = 8
VMEM_LIMIT_V7X = 56 * 1024 * 1024

ADAM_LR = 0.001
ADAM_B1 = 0.9
ADAM_B2 = 0.999
ADAM_EPS = 1e-08
ADAM_WD = 0.01
ADAM_STEP = 10

NT_DIMS = (((1,), (1,)), ((), ()))
TN_DIMS = (((0,), (0,)), ((), ()))


def _dot(a, b):
    return jnp.dot(a, b, preferred_element_type=F32)


def _dot_nt(a, b):
    return lax.dot_general(a, b, NT_DIMS, preferred_element_type=F32)


def _dot_tn(a, b):
    return lax.dot_general(a, b, TN_DIMS, preferred_element_type=F32)


def _sig(x):
    return 1.0 / (1.0 + jnp.exp(-x))


ANY = pl.BlockSpec(memory_space=pl.ANY)


def _call(body, deps, args, **kw):
    n = len(deps)
    if n:
        kw["in_specs"] = [ANY] * n + list(kw["in_specs"])
        return pl.pallas_call(lambda *refs: body(*refs[n:]), **kw)(*deps, *args)
    return pl.pallas_call(body, **kw)(*args)


def _params(n_axes):
    return pltpu.CompilerParams(dimension_semantics=("arbitrary",) * n_axes, vmem_limit_bytes=VMEM_LIMIT_V7X)


def _resident(shape):
    zeros = (0,) * len(shape)
    return pl.BlockSpec(shape, lambda *_: zeros, pipeline_mode=pl.Buffered(1))


def _row_tile(rows, cols):
    return pl.BlockSpec((rows, cols), lambda i: (i, 0))


def _rms_stats(x):
    r = lax.rsqrt(jnp.mean(x * x, axis=-1, keepdims=True) + EPS)
    return r, x * r


def _rms_bwd(dn, x, g):
    r, xh = _rms_stats(x)
    dxh = dn * g
    dx = r * (dxh - xh * jnp.mean(dxh * xh, axis=-1, keepdims=True))
    return dx, jnp.sum(dn * xh, axis=0, keepdims=True)


def _ffn_fwd(x, g, w_in_t, w_out, name, target=None):
    t = x.shape[0]
    tm = min(256, t)
    with_loss = target is not None

    def body(*refs):
        if with_loss:
            x_ref, g_ref, w_ref, wo_ref, t_ref, n_ref, u_ref, dy_ref, sq_ref = refs
        else:
            x_ref, g_ref, w_ref, wo_ref, n_ref, u_ref, xo_ref = refs
        x = x_ref[...]
        r, xh = _rms_stats(x)
        n = (xh * g_ref[...]).astype(BF)
        n_ref[...] = n
        u = _dot_nt(n, w_ref[...])
        u_ref[...] = u.astype(BF)
        a = u[:, :F]
        b = u[:, F:]
        h = (a * _sig(a) * b).astype(BF)
        xo = x + 0.5 * _dot(h, wo_ref[...])
        if with_loss:
            err = xo - t_ref[...]
            dy_ref[...] = err * (1.0 / D)

            @pl.when(pl.program_id(0) == 0)
            def _():
                sq_ref[...] = jnp.zeros_like(sq_ref)

            sq_ref[...] += jnp.sum(err * err, axis=0, keepdims=True)
        else:
            xo_ref[...] = xo

    in_specs = [_row_tile(tm, D), _resident((1, D)), _resident((INW, D)), _resident((F, D))]
    args = [x, g, w_in_t, w_out]
    out_specs = [_row_tile(tm, D), _row_tile(tm, INW), _row_tile(tm, D)]
    out_shape = [jax.ShapeDtypeStruct((t, D), BF), jax.ShapeDtypeStruct((t, INW), BF), jax.ShapeDtypeStruct((t, D), F32)]
    if with_loss:
        in_specs.append(_row_tile(tm, D))
        args.append(target)
        out_specs.append(pl.BlockSpec((1, D), lambda i: (0, 0)))
        out_shape.append(jax.ShapeDtypeStruct((1, D), F32))
    return pl.pallas_call(body, grid=(t // tm,), in_specs=in_specs, out_specs=out_specs, out_shape=out_shape,
                          compiler_params=_params(1), name=name)(*args)


def _ffn_bwd(dxo, x, g, u, w_in_t, w_out, name, deps=()):
    t = x.shape[0]
    tm = min(256, t)

    def body(dxo_ref, x_ref, g_ref, u_ref, w_ref, wo_ref, dx_ref, du_ref, h_ref, dy_ref, dg_ref):
        dxo = dxo_ref[...]
        dy = (0.5 * dxo).astype(BF)
        dy_ref[...] = dy
        dh = _dot_nt(dy, wo_ref[...])
        a = u_ref[:, :F].astype(F32)
        b = u_ref[:, F:].astype(F32)
        s = _sig(a)
        sa = a * s
        h_ref[...] = (sa * b).astype(BF)
        du_ref[:, :F] = (dh * b * (s * (1.0 + a * (1.0 - s)))).astype(BF)
        du_ref[:, F:] = (dh * sa).astype(BF)
        dn = _dot(du_ref[...], w_ref[...])
        dx, dg = _rms_bwd(dn, x_ref[...], g_ref[...])
        dx_ref[...] = dxo + dx

        @pl.when(pl.program_id(0) == 0)
        def _():
            dg_ref[...] = jnp.zeros_like(dg_ref)

        dg_ref[...] += dg

    return _call(
        body, deps, (dxo, x, g, u, w_in_t, w_out), grid=(t // tm,),
        in_specs=[_row_tile(tm, D), _row_tile(tm, D), _resident((1, D)), _row_tile(tm, INW), _resident((INW, D)),
                  _resident((F, D))],
        out_specs=[_row_tile(tm, D), _row_tile(tm, INW), _row_tile(tm, F), _row_tile(tm, D),
                   pl.BlockSpec((1, D), lambda i: (0, 0))],
        out_shape=[jax.ShapeDtypeStruct((t, D), F32), jax.ShapeDtypeStruct((t, INW), BF), jax.ShapeDtypeStruct((t, F), BF),
                   jax.ShapeDtypeStruct((t, D), BF), jax.ShapeDtypeStruct((1, D), F32)],
        compiler_params=_params(1), name=name)


def _wgrad(lhs, rhs, name, *, lhs_is_transposed, chunk):
    t = rhs.shape[0]
    n = lhs.shape[0] if lhs_is_transposed else lhs.shape[1]
    tm = min(512, t)
    c = min(chunk, n)
    n_tok = t // tm

    def body(l_ref, r_ref, o_ref, acc_ref):
        i = pl.program_id(1)

        @pl.when(i == 0)
        def _():
            acc_ref[...] = jnp.zeros_like(acc_ref)

        lhs_tile = l_ref[...].astype(BF)
        rhs_tile = r_ref[...].astype(BF)
        if lhs_is_transposed:
            acc_ref[...] += _dot(lhs_tile, rhs_tile)
        else:
            acc_ref[...] += _dot_tn(lhs_tile, rhs_tile)

        @pl.when(i == n_tok - 1)
        def _():
            o_ref[...] = acc_ref[...].astype(o_ref.dtype)

    if lhs_is_transposed:
        lhs_spec = pl.BlockSpec((c, tm), lambda j, i: (j, i))
    else:
        lhs_spec = pl.BlockSpec((tm, c), lambda j, i: (i, j))
    return pl.pallas_call(
        body, grid=(n // c, n_tok),
        in_specs=[lhs_spec, pl.BlockSpec((tm, D), lambda j, i: (i, 0))],
        out_specs=pl.BlockSpec((c, D), lambda j, i: (j, 0)),
        out_shape=jax.ShapeDtypeStruct((n, D), BF),
        scratch_shapes=[pltpu.VMEM((c, D), F32)],
        compiler_params=_params(2), name=name)(lhs, rhs)


def _wgrad_mix(duc, dq_t, dkv_t, dgp, hm):
    t = hm.shape[0]
    tm = min(1024, t)
    c = 512
    n_tok = t // tm
    first_q, first_kv, first_gate = R_Q[0] // c, R_KV[0] // c, R_GATE[0] // c

    def body(uc_ref, q_ref, kv_ref, gp_ref, h_ref, o_ref, acc_ref):
        j = pl.program_id(0)
        i = pl.program_id(1)

        @pl.when(i == 0)
        def _():
            acc_ref[...] = jnp.zeros_like(acc_ref)

        @pl.when(j < first_q)
        def _():
            acc_ref[...] += _dot_tn(uc_ref[...], h_ref[...])

        @pl.when((j >= first_q) & (j < first_kv))
        def _():
            acc_ref[...] += _dot(q_ref[...], h_ref[...])

        @pl.when(j == first_kv)
        def _():
            acc_ref[...] += _dot(kv_ref[...], h_ref[...])

        @pl.when(j >= first_gate)
        def _():
            acc_ref[...] += _dot_tn(gp_ref[...], h_ref[...])

        @pl.when(i == n_tok - 1)
        def _():
            o_ref[...] = acc_ref[...].astype(BF)

    def tokens(active, i):
        return jnp.where(active, i, 0)

    return pl.pallas_call(
        body, grid=(INW // c, n_tok),
        in_specs=[pl.BlockSpec((tm, c), lambda j, i: (tokens(j < first_q, i), jnp.clip(j, 0, first_q - 1))),
                  pl.BlockSpec((c, tm), lambda j, i: (jnp.clip(j - first_q, 0, first_kv - first_q - 1),
                                                      tokens((j >= first_q) & (j < first_kv), i))),
                  pl.BlockSpec((c, tm), lambda j, i: (0, tokens(j == first_kv, i))),
                  pl.BlockSpec((tm, c), lambda j, i: (tokens(j >= first_gate, i),
                                                      jnp.clip(j - first_gate, 0, INW // c - first_gate - 1))),
                  pl.BlockSpec((tm, D), lambda j, i: (i, 0))],
        out_specs=pl.BlockSpec((c, D), lambda j, i: (j, 0)),
        out_shape=jax.ShapeDtypeStruct((INW, D), BF),
        scratch_shapes=[pltpu.VMEM((c, D), F32)],
        compiler_params=_params(2), name="mix_dw_in")(duc, dq_t, dkv_t, dgp, hm)


def _mix_proj(x, g, w_t):
    t = x.shape[0]
    tm = min(256, t)

    def body(x_ref, g_ref, w_ref, hm_ref, uc_ref, gp_ref, qkv_ref):
        r, xh = _rms_stats(x_ref[...])
        hm = (xh * g_ref[...]).astype(BF)
        hm_ref[...] = hm
        uc_ref[...] = _dot_nt(hm, w_ref[R_CONV[0]:R_CONV[1], :]).astype(BF)
        gp_ref[...] = _dot_nt(hm, w_ref[R_GATE[0]:R_GATE[1], :]).astype(BF)
        qkv_ref[...] = _dot_nt(w_ref[R_QKV[0]:R_QKV[1], :], hm).astype(BF)

    return pl.pallas_call(
        body, grid=(t // tm,),
        in_specs=[_row_tile(tm, D), _resident((1, D)), _resident((INW, D))],
        out_specs=[_row_tile(tm, D), _row_tile(tm, 2 * D), _row_tile(tm, 2 * D), pl.BlockSpec((1536, tm), lambda i: (0, i))],
        out_shape=[jax.ShapeDtypeStruct((t, D), BF), jax.ShapeDtypeStruct((t, 2 * D), BF),
                   jax.ShapeDtypeStruct((t, 2 * D), BF), jax.ShapeDtypeStruct((1536, t), BF)],
        compiler_params=_params(1), name="mix_proj")(x, g, w_t)


CONV_HALO = 32
CONV_LEAD = CONV_HALO - (CW - 1)


def _glu(uc):
    uc = uc.astype(F32)
    return uc[:, :D] * _sig(uc[:, D:])


def _ln_stats(zc):
    mu = jnp.mean(zc, axis=-1, keepdims=True)
    zm = zc - mu
    r = lax.rsqrt(jnp.mean(zm * zm, axis=-1, keepdims=True) + EPS)
    return r, zm * r


CONV_SHIFTS = 8
CONV_CHUNK = 32


def _store_shifted(buf, rows):
    for b in range(1, CONV_SHIFTS):
        buf[b, 0:rows - 8, :] = buf[0, pl.ds(b, rows - 8), :]


def _conv_fwd(uc, dwk, dwb, lng, lnb):
    t = uc.shape[0]
    tm = min(512, t)
    per = tm // CONV_HALO
    ext = tm + CONV_HALO

    def body(cur_ref, prev_ref, k_ref, kb_ref, g_ref, b_ref, o_ref, zc_ref, zsh):
        i = pl.program_id(0)
        zsh[0, 0:CONV_HALO, :] = _glu(prev_ref[...]) * (i > 0).astype(F32)
        zsh[0, CONV_HALO:, :] = _glu(cur_ref[...])
        _store_shifted(zsh, ext)

        def chunk(ci, carry):
            r0 = pl.multiple_of(ci * CONV_CHUNK, CONV_CHUNK)
            acc = jnp.zeros((CONV_CHUNK, D), F32) + kb_ref[...]
            for w in range(CW):
                a, b = divmod(CONV_LEAD + w, 8)
                acc = acc + k_ref[w:w + 1, :] * zsh[b, pl.ds(r0 + 8 * a, CONV_CHUNK), :]
            zc_ref[pl.ds(r0, CONV_CHUNK), :] = acc
            r, xh = _ln_stats(acc)
            y = xh * g_ref[...] + b_ref[...]
            o_ref[pl.ds(r0, CONV_CHUNK), :] = (y * _sig(y)).astype(BF)
            return carry

        lax.fori_loop(0, tm // CONV_CHUNK, chunk, 0)

    return pl.pallas_call(
        body, grid=(t // tm,),
        in_specs=[_row_tile(tm, 2 * D),
                  pl.BlockSpec((CONV_HALO, 2 * D), lambda i: (jnp.maximum(i * per - 1, 0), 0)),
                  _resident((CWP, D)), _resident((1, D)), _resident((1, D)), _resident((1, D))],
        out_specs=[_row_tile(tm, D), _row_tile(tm, D)],
        out_shape=[jax.ShapeDtypeStruct((t, D), BF), jax.ShapeDtypeStruct((t, D), F32)],
        scratch_shapes=[pltpu.VMEM((CONV_SHIFTS, ext, D), F32)],
        compiler_params=_params(1), name="conv_fwd")(uc, uc, dwk, dwb, lng, lnb)


def _conv_bwd(uc, zc, dzs, dwk, lng, lnb):
    t = uc.shape[0]
    tm = min(256, t)
    per = tm // CONV_HALO
    n_tiles = t // tm
    ext = tm + CONV_HALO
    last_block = t // CONV_HALO - 1

    def body(cur_ref, prev_ref, zc_ref, zcn_ref, dz_ref, dzn_ref, k_ref, g_ref, b_ref,
             duc_ref, dk_ref, dkb_ref, dg_ref, db_ref, zsh, dsh, dk8):
        i = pl.program_id(0)

        @pl.when(i == 0)
        def _():
            dk8[...] = jnp.zeros_like(dk8)
            dkb_ref[...] = jnp.zeros_like(dkb_ref)
            dg_ref[...] = jnp.zeros_like(dg_ref)
            db_ref[...] = jnp.zeros_like(db_ref)

        has_next = (i < n_tiles - 1).astype(F32)
        zsh[0, 0:CONV_HALO, :] = _glu(prev_ref[...]) * (i > 0).astype(F32)
        zsh[0, CONV_HALO:, :] = _glu(cur_ref[...])
        _store_shifted(zsh, ext)
        gain = g_ref[...]

        def ln_silu_bwd(zc, dzs, live):
            r, xh = _ln_stats(zc)
            y = xh * gain + b_ref[...]
            sy = _sig(y)
            dy = dzs * (sy * (1.0 + y * (1.0 - sy))) * live
            dxh = dy * gain
            dzc = r * (dxh - jnp.mean(dxh, axis=-1, keepdims=True) - xh * jnp.mean(dxh * xh, axis=-1, keepdims=True))
            return dzc, dy, xh

        dzc, dy, xh = ln_silu_bwd(zc_ref[...], dz_ref[...], 1.0)
        dsh[0, 0:tm, :] = dzc
        dg_ref[...] += jnp.sum(dy * xh, axis=0, keepdims=True)
        db_ref[...] += jnp.sum(dy, axis=0, keepdims=True)
        dkb_ref[...] += jnp.sum(dzc, axis=0, keepdims=True)
        dsh[0, tm:, :] = ln_silu_bwd(zcn_ref[...], dzn_ref[...], has_next)[0]
        _store_shifted(dsh, ext)

        def chunk(ci, carry):
            r0 = pl.multiple_of(ci * CONV_CHUNK, CONV_CHUNK)
            dzc_c = dsh[0, pl.ds(r0, CONV_CHUNK), :]
            dz = jnp.zeros((CONV_CHUNK, D), F32)
            for w in range(CW):
                a, b = divmod(CW - 1 - w, 8)
                dz = dz + k_ref[w:w + 1, :] * dsh[b, pl.ds(r0 + 8 * a, CONV_CHUNK), :]
                a, b = divmod(CONV_LEAD + w, 8)
                prod = dzc_c * zsh[b, pl.ds(r0 + 8 * a, CONV_CHUNK), :]
                part = prod[0:8, :]
                for j in range(1, CONV_CHUNK // 8):
                    part = part + prod[8 * j:8 * j + 8, :]
                dk8[w] += part
            ucc = cur_ref[pl.ds(r0, CONV_CHUNK), :].astype(F32)
            sg = _sig(ucc[:, D:])
            duc_ref[pl.ds(r0, CONV_CHUNK), 0:D] = (dz * sg).astype(BF)
            duc_ref[pl.ds(r0, CONV_CHUNK), D:2 * D] = (dz * ucc[:, :D] * sg * (1.0 - sg)).astype(BF)
            return carry

        lax.fori_loop(0, tm // CONV_CHUNK, chunk, 0)

        @pl.when(i == n_tiles - 1)
        def _():
            dk_ref[...] = jnp.sum(dk8[...], axis=1)

    vec = pl.BlockSpec((1, D), lambda i: (0, 0))
    next_halo = pl.BlockSpec((CONV_HALO, D), lambda i: (jnp.minimum((i + 1) * per, last_block), 0))
    return pl.pallas_call(
        body, grid=(n_tiles,),
        in_specs=[_row_tile(tm, 2 * D),
                  pl.BlockSpec((CONV_HALO, 2 * D), lambda i: (jnp.maximum(i * per - 1, 0), 0)),
                  _row_tile(tm, D), next_halo, _row_tile(tm, D), next_halo,
                  _resident((CWP, D)), _resident((1, D)), _resident((1, D))],
        out_specs=[_row_tile(tm, 2 * D), pl.BlockSpec((CWP, D), lambda i: (0, 0)), vec, vec, vec],
        out_shape=[jax.ShapeDtypeStruct((t, 2 * D), BF), jax.ShapeDtypeStruct((CWP, D), F32),
                   jax.ShapeDtypeStruct((1, D), F32), jax.ShapeDtypeStruct((1, D), F32), jax.ShapeDtypeStruct((1, D), F32)],
        scratch_shapes=[pltpu.VMEM((CONV_SHIFTS, ext, D), F32), pltpu.VMEM((CONV_SHIFTS, ext, D), F32),
                        pltpu.VMEM((CWP, 8, D), F32)],
        compiler_params=_params(1), name="conv_bwd")(uc, uc, zc, zc, dzs, dzs, dwk, lng, lnb)


def _norm_rows(xt, g):
    r = lax.rsqrt(jnp.mean(xt * xt, axis=0, keepdims=True) + EPS)
    xh = xt * r
    return xh * g, r, xh


ATT_TQ = 512


def _attn_specs(t, tq):
    per = tq // BLK
    return [pl.BlockSpec((1536, tq), lambda i: (0, i)),
            pl.BlockSpec((512, BLK), lambda i: (2, jnp.maximum(i * per - 1, 0))),
            _resident((HD, 1)), _resident((HD, 1)), _resident((NKV, 1, GRP * BLK)),
            _resident((NKV, 2 * BLK, GRP * BLK)), _resident((2, 2 * BLK, GRP * BLK))]


def _attn_window(hk, sb, qkv_ref, halo_ref, kn_cur, kn_halo):
    v0 = D + NKV * HD + hk * HD
    if sb == 0:
        k_prev = kn_halo[hk]
        v_prev = halo_ref[NKV * HD + hk * HD:NKV * HD + (hk + 1) * HD, :]
    else:
        k_prev = kn_cur[hk][:, (sb - 1) * BLK:sb * BLK]
        v_prev = qkv_ref[v0:v0 + HD, (sb - 1) * BLK:sb * BLK]
    kw = jnp.concatenate([k_prev, kn_cur[hk][:, sb * BLK:(sb + 1) * BLK]], axis=1).astype(BF)
    vw = jnp.concatenate([v_prev, qkv_ref[v0:v0 + HD, sb * BLK:(sb + 1) * BLK]], axis=1)
    return kw, vw


def _attn_probs(kw, qc, bias, mask, sink):
    st = _dot_tn(kw, qc) * QK_SCALE + bias
    st = jnp.where(mask > 0.5, st, NEG)
    m = jnp.maximum(jnp.max(st, axis=0, keepdims=True), sink)
    p = jnp.exp(st - m)
    e_sink = jnp.exp(sink - m)
    inv = 1.0 / (jnp.sum(p, axis=0, keepdims=True) + e_sink)
    return p * inv, e_sink * inv


def _attn_fwd(qkv_t, qg, kg, sink_rows, bias_t, mask_t):
    t = qkv_t.shape[1]
    tq = min(ATT_TQ, t)
    n_sub = tq // BLK

    def body(qkv_ref, halo_ref, qg_ref, kg_ref, sink_ref, bias_ref, mask_ref, o_ref):
        i = pl.program_id(0)
        first = (i == 0).astype(jnp.int32)
        kgain = kg_ref[...]
        qgain = qg_ref[...]
        kn_cur = [_norm_rows(qkv_ref[D + h * HD:D + (h + 1) * HD, :].astype(F32), kgain)[0] for h in range(NKV)]
        kn_halo = [_norm_rows(halo_ref[h * HD:(h + 1) * HD, :].astype(F32), kgain)[0] for h in range(NKV)]
        for hk in range(NKV):
            for sb in range(n_sub):
                cols = slice(sb * BLK, (sb + 1) * BLK)
                kw, vw = _attn_window(hk, sb, qkv_ref, halo_ref, kn_cur, kn_halo)
                qc = jnp.concatenate(
                    [_norm_rows(qkv_ref[(GRP * hk + g) * HD:(GRP * hk + g + 1) * HD, cols].astype(F32), qgain)[0]
                     for g in range(GRP)], axis=1).astype(BF)
                mask = mask_ref[first] if sb == 0 else mask_ref[0]
                p, _ = _attn_probs(kw, qc, bias_ref[hk], mask, sink_ref[hk])
                o = _dot(vw, p.astype(BF))
                for g in range(GRP):
                    head = GRP * hk + g
                    o_ref[head * HD:(head + 1) * HD, cols] = o[:, g * BLK:(g + 1) * BLK].astype(BF)

    return pl.pallas_call(
        body, grid=(t // tq,),
        in_specs=_attn_specs(t, tq),
        out_specs=pl.BlockSpec((D, tq), lambda i: (0, i)),
        out_shape=jax.ShapeDtypeStruct((D, t), BF),
        compiler_params=_params(1), name="attn_fwd")(qkv_t, qkv_t, qg, kg, sink_rows, bias_t, mask_t)


def _attn_bwd(qkv_t, do_t, qg, kg, sink_rows, bias_t, mask_t, deps=()):
    t = qkv_t.shape[1]
    tq = min(ATT_TQ, t)
    n_sub = tq // BLK
    n_tiles = t // tq

    def body(qkv_ref, halo_ref, do_ref, qg_ref, kg_ref, sink_ref, bias_ref, mask_ref,
             dq_ref, ckv_ref, dqg_ref, dsink_ref, dsacc_ref, qg_scr):
        i = pl.program_id(0)

        @pl.when(i == 0)
        def _():
            qg_scr[...] = jnp.zeros_like(qg_scr)
            dsink_ref[...] = jnp.zeros_like(dsink_ref)
            dsacc_ref[...] = jnp.zeros_like(dsacc_ref)

        first = (i == 0).astype(jnp.int32)
        kgain = kg_ref[...]
        qgain = qg_ref[...]
        kn_cur = [_norm_rows(qkv_ref[D + h * HD:D + (h + 1) * HD, :].astype(F32), kgain)[0] for h in range(NKV)]
        kn_halo = [_norm_rows(halo_ref[h * HD:(h + 1) * HD, :].astype(F32), kgain)[0] for h in range(NKV)]
        dqg = jnp.zeros((HD, BLK), F32)
        for hk in range(NKV):
            for sb in range(n_sub):
                cols = slice(sb * BLK, (sb + 1) * BLK)
                kw, vw = _attn_window(hk, sb, qkv_ref, halo_ref, kn_cur, kn_halo)
                qn, qr, qh = [], [], []
                for g in range(GRP):
                    head = GRP * hk + g
                    n_, r_, h_ = _norm_rows(qkv_ref[head * HD:(head + 1) * HD, cols].astype(F32), qgain)
                    qn.append(n_)
                    qr.append(r_)
                    qh.append(h_)
                qc = jnp.concatenate(qn, axis=1).astype(BF)
                mask = mask_ref[first] if sb == 0 else mask_ref[0]
                p, p_sink = _attn_probs(kw, qc, bias_ref[hk], mask, sink_ref[hk])
                doc = jnp.concatenate([do_ref[(GRP * hk + g) * HD:(GRP * hk + g + 1) * HD, cols] for g in range(GRP)], axis=1)
                dp = _dot_tn(vw, doc)
                delta = jnp.sum(p * dp, axis=0, keepdims=True)
                ds = p * (dp - delta)
                dsink_ref[hk] += -(p_sink * delta)
                dsacc_ref[hk] += ds
                dsb = ds.astype(BF)
                dqc = _dot(kw, dsb) * QK_SCALE
                ckv_ref[sb, hk * HD:(hk + 1) * HD, :] = _dot_nt(qc, dsb) * QK_SCALE
                ckv_ref[sb, NKV * HD + hk * HD:NKV * HD + (hk + 1) * HD, :] = _dot_nt(doc, p.astype(BF))
                for g in range(GRP):
                    head = GRP * hk + g
                    dqn = dqc[:, g * BLK:(g + 1) * BLK]
                    dqh = dqn * qgain
                    dq = qr[g] * (dqh - qh[g] * jnp.mean(dqh * qh[g], axis=0, keepdims=True))
                    dq_ref[head * HD:(head + 1) * HD, cols] = dq.astype(BF)
                    dqg = dqg + dqn * qh[g]
        qg_scr[...] += dqg

        @pl.when(i == n_tiles - 1)
        def _():
            dqg_ref[...] = jnp.sum(qg_scr[...], axis=1, keepdims=True)

    return _call(
        body, deps, (qkv_t, qkv_t, do_t, qg, kg, sink_rows, bias_t, mask_t), grid=(n_tiles,),
        in_specs=_attn_specs(t, tq)[:2] + [pl.BlockSpec((D, tq), lambda i: (0, i))] + _attn_specs(t, tq)[2:],
        out_specs=[pl.BlockSpec((D, tq), lambda i: (0, i)),
                   pl.BlockSpec((n_sub, 2 * NKV * HD, 2 * BLK), lambda i: (i, 0, 0)),
                   pl.BlockSpec((HD, 1), lambda i: (0, 0)),
                   pl.BlockSpec((NKV, 1, GRP * BLK), lambda i: (0, 0, 0)),
                   pl.BlockSpec((NKV, 2 * BLK, GRP * BLK), lambda i: (0, 0, 0))],
        out_shape=[jax.ShapeDtypeStruct((D, t), BF),
                   jax.ShapeDtypeStruct((t // BLK, 2 * NKV * HD, 2 * BLK), F32),
                   jax.ShapeDtypeStruct((HD, 1), F32),
                   jax.ShapeDtypeStruct((NKV, 1, GRP * BLK), F32),
                   jax.ShapeDtypeStruct((NKV, 2 * BLK, GRP * BLK), F32)],
        scratch_shapes=[pltpu.VMEM((HD, BLK), F32)],
        compiler_params=_params(1), name="attn_bwd")


def _kv_combine(ckv, qkv_t, kg):
    nb = ckv.shape[0]
    t = nb * BLK
    rows = NKV * HD

    def body(c_ref, cn_ref, k_ref, kg_ref, o_ref, dkg_ref, kg_scr):
        n = pl.program_id(0)

        @pl.when(n == 0)
        def _():
            kg_scr[...] = jnp.zeros_like(kg_scr)

        has_next = (n < nb - 1).astype(F32)
        d = c_ref[0, :, BLK:] + cn_ref[0, :, :BLK] * has_next
        o_ref[rows:, :] = d[rows:, :].astype(BF)
        kgain = kg_ref[...]
        dkg = jnp.zeros((HD, BLK), F32)
        for h in range(NKV):
            _, r, kh = _norm_rows(k_ref[h * HD:(h + 1) * HD, :].astype(F32), kgain)
            dkn = d[h * HD:(h + 1) * HD, :]
            dkh = dkn * kgain
            o_ref[h * HD:(h + 1) * HD, :] = (r * (dkh - kh * jnp.mean(dkh * kh, axis=0, keepdims=True))).astype(BF)
            dkg = dkg + dkn * kh
        kg_scr[...] += dkg

        @pl.when(n == nb - 1)
        def _():
            dkg_ref[...] = jnp.sum(kg_scr[...], axis=1, keepdims=True)

    return pl.pallas_call(
        body, grid=(nb,),
        in_specs=[pl.BlockSpec((1, 2 * rows, 2 * BLK), lambda n: (n, 0, 0)),
                  pl.BlockSpec((1, 2 * rows, 2 * BLK), lambda n: (jnp.minimum(n + 1, nb - 1), 0, 0)),
                  pl.BlockSpec((rows, BLK), lambda n: (D // rows, n)),
                  _resident((HD, 1))],
        out_specs=[pl.BlockSpec((2 * rows, BLK), lambda n: (0, n)), pl.BlockSpec((HD, 1), lambda n: (0, 0))],
        out_shape=[jax.ShapeDtypeStruct((2 * rows, t), BF), jax.ShapeDtypeStruct((HD, 1), F32)],
        scratch_shapes=[pltpu.VMEM((HD, BLK), F32)],
        compiler_params=_params(1), name="kv_combine")(ckv, ckv, qkv_t, kg)


def _group_lane_sums(v):
    lane_group = lax.broadcasted_iota(jnp.int32, (1, GRP * BLK), 1) // BLK
    col = lax.broadcasted_iota(jnp.int32, (1, BLK), 1)
    out = jnp.zeros((NKV, BLK), F32)
    for g in range(GRP):
        s = jnp.sum(jnp.where(lane_group == g, v, 0.0), axis=1, keepdims=True)
        out = jnp.where(col == g, s, out)
    return out


def _bias_grad(dsacc, onehot_t):
    def body(ds_ref, oh_ref, o_ref):
        oh = jnp.concatenate([oh_ref[0]] * GRP, axis=1)
        o_ref[0] = _group_lane_sums(jnp.sum(ds_ref[...] * oh[None], axis=1))

    return pl.pallas_call(
        body, grid=(NBUCKET,),
        in_specs=[_resident((NKV, 2 * BLK, GRP * BLK)), pl.BlockSpec((1, 2 * BLK, BLK), lambda b: (b, 0, 0))],
        out_specs=pl.BlockSpec((1, NKV, BLK), lambda b: (b, 0, 0)),
        out_shape=jax.ShapeDtypeStruct((NBUCKET, NKV, BLK), F32),
        compiler_params=_params(1), name="bias_grad")(dsacc, onehot_t)


def _sink_grad(dsink_rows):
    def body(d_ref, o_ref):
        o_ref[...] = _group_lane_sums(d_ref[:, 0, :])

    return pl.pallas_call(body, out_shape=jax.ShapeDtypeStruct((NKV, BLK), F32), name="sink_grad")(dsink_rows)


def _mix_out(zs, o_t, gp, x, w_cp, w_o, w_out):
    t = x.shape[0]
    tm = min(256, t)

    def body(zs_ref, ot_ref, gp_ref, x_ref, wcp_ref, wo_ref, wout_ref, xo_ref, a_ref, b_ref, m_ref):
        a = _dot(zs_ref[...], wcp_ref[...])
        b = _dot_tn(ot_ref[...], wo_ref[...])
        a_ref[...] = a.astype(BF)
        b_ref[...] = b.astype(BF)
        merged = (_sig(gp_ref[:, :D].astype(F32)) * a + _sig(gp_ref[:, D:].astype(F32)) * b).astype(BF)
        m_ref[...] = merged
        xo_ref[...] = x_ref[...] + _dot(merged, wout_ref[...])

    return pl.pallas_call(
        body, grid=(t // tm,),
        in_specs=[_row_tile(tm, D), pl.BlockSpec((D, tm), lambda i: (0, i)), _row_tile(tm, 2 * D), _row_tile(tm, D),
                  _resident((D, D)), _resident((D, D)), _resident((D, D))],
        out_specs=[_row_tile(tm, D)] * 4,
        out_shape=[jax.ShapeDtypeStruct((t, D), F32)] + [jax.ShapeDtypeStruct((t, D), BF)] * 3,
        compiler_params=_params(1), name="mix_out")(zs, o_t, gp, x, w_cp, w_o, w_out)


def _mix_out_bwd(dx, a, b, gp, w_cp, w_o, w_out, deps=()):
    t = dx.shape[0]
    tm = min(256, t)

    def body(dx_ref, a_ref, b_ref, gp_ref, wcp_ref, wo_ref, wout_ref, dzs_ref, dot_ref, dgp_ref, da_ref, db_ref, dxb_ref):
        dxb = dx_ref[...].astype(BF)
        dxb_ref[...] = dxb
        dm = _dot_nt(dxb, wout_ref[...])
        gc = _sig(gp_ref[:, :D].astype(F32))
        ga = _sig(gp_ref[:, D:].astype(F32))
        da = (dm * gc).astype(BF)
        db = (dm * ga).astype(BF)
        da_ref[...] = da
        db_ref[...] = db
        dgp_ref[:, :D] = (dm * a_ref[...].astype(F32) * gc * (1.0 - gc)).astype(BF)
        dgp_ref[:, D:] = (dm * b_ref[...].astype(F32) * ga * (1.0 - ga)).astype(BF)
        dzs_ref[...] = _dot_nt(da, wcp_ref[...])
        dot_ref[...] = _dot_nt(wo_ref[...], db).astype(BF)

    return _call(
        body, deps, (dx, a, b, gp, w_cp, w_o, w_out), grid=(t // tm,),
        in_specs=[_row_tile(tm, D), _row_tile(tm, D), _row_tile(tm, D), _row_tile(tm, 2 * D),
                  _resident((D, D)), _resident((D, D)), _resident((D, D))],
        out_specs=[_row_tile(tm, D), pl.BlockSpec((D, tm), lambda i: (0, i)), _row_tile(tm, 2 * D),
                   _row_tile(tm, D), _row_tile(tm, D), _row_tile(tm, D)],
        out_shape=[jax.ShapeDtypeStruct((t, D), F32), jax.ShapeDtypeStruct((D, t), BF), jax.ShapeDtypeStruct((t, 2 * D), BF),
                   jax.ShapeDtypeStruct((t, D), BF), jax.ShapeDtypeStruct((t, D), BF), jax.ShapeDtypeStruct((t, D), BF)],
        compiler_params=_params(1), name="mix_out_bwd")


def _mix_proj_bwd(dxo, duc, dq_t, dkv_t, dgp, x, g, w_t):
    t = x.shape[0]
    tm = min(256, t)

    def body(dxo_ref, duc_ref, dq_ref, dkv_ref, dgp_ref, x_ref, g_ref, w_ref, dx_ref, dg_ref):
        dn = _dot(duc_ref[...], w_ref[R_CONV[0]:R_CONV[1], :])
        dn = dn + _dot(dgp_ref[...], w_ref[R_GATE[0]:R_GATE[1], :])
        dn = dn + _dot_tn(dq_ref[...], w_ref[R_Q[0]:R_Q[1], :])
        dn = dn + _dot_tn(dkv_ref[...], w_ref[R_KV[0]:R_KV[1], :])
        dx, dg = _rms_bwd(dn, x_ref[...], g_ref[...])
        dx_ref[...] = dxo_ref[...] + dx

        @pl.when(pl.program_id(0) == 0)
        def _():
            dg_ref[...] = jnp.zeros_like(dg_ref)

        dg_ref[...] += dg

    return pl.pallas_call(
        body, grid=(t // tm,),
        in_specs=[_row_tile(tm, D), _row_tile(tm, 2 * D), pl.BlockSpec((D, tm), lambda i: (0, i)),
                  pl.BlockSpec((2 * NKV * HD, tm), lambda i: (0, i)), _row_tile(tm, 2 * D), _row_tile(tm, D),
                  _resident((1, D)), _resident((INW, D))],
        out_specs=[_row_tile(tm, D), pl.BlockSpec((1, D), lambda i: (0, 0))],
        out_shape=[jax.ShapeDtypeStruct((t, D), F32), jax.ShapeDtypeStruct((1, D), F32)],
        compiler_params=_params(1), name="mix_proj_bwd")(dxo, duc, dq_t, dkv_t, dgp, x, g, w_t)


def _attention_tables():
    kj = np.arange(2 * BLK)[:, None]
    qi = np.arange(BLK)[None, :]
    dist = qi + BLK - kj
    in_win = (dist >= 0) & (dist < BLK)
    dpos = np.maximum(dist, 0)
    max_exact = NBUCKET // 2
    dfl = np.maximum(dpos, 1).astype(np.float32)
    large = max_exact + (np.log(dfl / np.float32(max_exact)) / np.float32(math.log(BLK / max_exact))
                         * np.float32(NBUCKET - max_exact)).astype(np.int32)
    large = np.minimum(large, NBUCKET - 1)
    bucket = np.where(dpos < max_exact, dpos, large)
    onehot = (bucket[None] == np.arange(NBUCKET)[:, None, None]).astype(np.float32)
    mask = in_win.astype(np.float32)
    mask_first = mask * (kj >= BLK)
    masks = np.stack([np.tile(mask, (1, GRP)), np.tile(mask_first, (1, GRP))])
    return onehot, masks


def _bias_table(rel_bias, onehot):
    tab = jnp.einsum("bkq,bh->hkq", onehot, rel_bias, precision=lax.Precision.HIGHEST)
    tab = tab.reshape(NKV, GRP, 2 * BLK, BLK)
    return jnp.transpose(tab, (0, 2, 1, 3)).reshape(NKV, 2 * BLK, GRP * BLK)


def _local_step(x, target, vec, weights_of, grads_done):
    onehot_np, masks_np = _attention_tables()
    onehot = jnp.asarray(onehot_np)
    masks = jnp.asarray(masks_np)
    bias_t = _bias_table(vec["rel_bias"], onehot)
    sink_rows = jnp.repeat(vec["attn_sinks"].reshape(NKV, 1, GRP), BLK, axis=2)
    qg = vec["q_norm"].reshape(HD, 1)
    kg = vec["k_norm"].reshape(HD, 1)
    g1 = vec["ffn1_norm"].reshape(1, D)
    gm = vec["mix_norm"].reshape(1, D)
    g2 = vec["ffn2_norm"].reshape(1, D)
    dwb = vec["conv_dw_bias"].reshape(1, D)
    lng = vec["conv_ln_g"].reshape(1, D)
    lnb = vec["conv_ln_b"].reshape(1, D)

    w1 = weights_of("ffn1", x)
    n1, u1, x1 = _ffn_fwd(x, g1, w1["ffn1_w_in"], w1["ffn1_w_out"], "ffn1_fwd")
    wm = weights_of("mix", x1)
    dwk = jnp.pad(wm["conv_dw_kernel"], ((0, CWP - CW), (0, 0)))
    hm, uc, gp, qkv_t = _mix_proj(x1, gm, wm["w_in"])
    zs, zc = _conv_fwd(uc, dwk, dwb, lng, lnb)
    o_t = _attn_fwd(qkv_t, qg, kg, sink_rows, bias_t, masks)
    x2, a, b, merged = _mix_out(zs, o_t, gp, x1, wm["conv_w_proj"], wm["attn_w_o"], wm["w_out"])
    w2 = weights_of("ffn2", x2)
    n2, u2, dx3, sq = _ffn_fwd(x2, g2, w2["ffn2_w_in"], w2["ffn2_w_out"], "ffn2_fwd", target=target)

    gv = {}
    dx2, du2, h2, dy2, gv["ffn2_norm"] = _ffn_bwd(dx3, x2, g2, u2, w2["ffn2_w_in"], w2["ffn2_w_out"], "ffn2_bwd")
    deps = grads_done("ffn2", {"ffn2_w_in": _wgrad(du2, n2, "ffn2_dw_in", lhs_is_transposed=False, chunk=1408),
                               "ffn2_w_out": _wgrad(h2, dy2, "ffn2_dw_out", lhs_is_transposed=False, chunk=1408)})

    dzs, do_t, dgp, da, db, dx2b = _mix_out_bwd(dx2, a, b, gp, wm["conv_w_proj"], wm["attn_w_o"], wm["w_out"], deps=deps)
    deps = grads_done("mix_out", {"w_out": _wgrad(merged, dx2b, "mix_dw_out", lhs_is_transposed=False, chunk=1024),
                                  "conv_w_proj": _wgrad(zs, da, "mix_dw_cp", lhs_is_transposed=False, chunk=1024),
                                  "attn_w_o": _wgrad(o_t, db, "mix_dw_o", lhs_is_transposed=True, chunk=1024)})

    dq_t, ckv, dqg, dsink_rows, dsacc = _attn_bwd(qkv_t, do_t, qg, kg, sink_rows, bias_t, masks, deps=deps)
    dkv_t, dkg = _kv_combine(ckv, qkv_t, kg)
    gv["q_norm"] = dqg.reshape(HD)
    gv["k_norm"] = dkg.reshape(HD)
    gv["attn_sinks"] = _sink_grad(dsink_rows)[:, :GRP].reshape(NQ)
    gv["rel_bias"] = _bias_grad(dsacc, onehot)[:, :, :GRP].reshape(NBUCKET, NQ)

    duc, dk_conv, gv["conv_dw_bias"], gv["conv_ln_g"], gv["conv_ln_b"] = _conv_bwd(uc, zc, dzs, dwk, lng, lnb)
    gv["conv_dw_kernel"] = dk_conv[:CW]

    dx1, gv["mix_norm"] = _mix_proj_bwd(dx2, duc, dq_t, dkv_t, dgp, x1, gm, wm["w_in"])
    deps = grads_done("mix_in", {"w_in": _wgrad_mix(duc, dq_t, dkv_t, dgp, hm)})

    dx0, du1, h1, dy1, gv["ffn1_norm"] = _ffn_bwd(dx1, x, g1, u1, w1["ffn1_w_in"], w1["ffn1_w_out"], "ffn1_bwd", deps=deps)
    grads_done("ffn1", {"ffn1_w_in": _wgrad(du1, n1, "ffn1_dw_in", lhs_is_transposed=False, chunk=1408),
                        "ffn1_w_out": _wgrad(h1, dy1, "ffn1_dw_out", lhs_is_transposed=False, chunk=1408)})
    for k in ("ffn1_norm", "mix_norm", "ffn2_norm", "conv_dw_bias", "conv_ln_g", "conv_ln_b"):
        gv[k] = gv[k].reshape(D)
    return sq, dx0, gv


MESH_ID = pl.DeviceIdType.MESH


def _position():
    return lax.axis_index("x"), lax.axis_index("y"), lax.axis_index("c")


def _shard_rows(ref, index, rows):
    return ref.at[pl.ds(pl.multiple_of(index * rows, 16), rows), :]


def _prep(weights, transposed):
    n = len(weights)

    def body(*refs):
        for k in range(n):
            w = refs[k][...]
            refs[n + k][...] = (w.T if transposed[k] else w).astype(BF)

    out_shape = [jax.ShapeDtypeStruct(w.shape[::-1] if tr else w.shape, BF) for w, tr in zip(weights, transposed)]
    return pl.pallas_call(body, out_shape=out_shape, compiler_params=pltpu.CompilerParams(vmem_limit_bytes=VMEM_LIMIT_V7X),
                          name="prep")(*weights)


HBM = pl.BlockSpec(memory_space=pltpu.HBM)
SEM = pl.BlockSpec(memory_space=pltpu.SEMAPHORE)
DATAFLOW = pltpu.SideEffectType.DATAFLOW_SIDE_EFFECTING
TOKEN = jax.ShapeDtypeStruct((8, 128), F32)


def _in_hbm(x):
    return pltpu.with_memory_space_constraint(x, pltpu.HBM)


def _hbm_like(arrays):
    return [pltpu.HBM(a.shape, a.dtype) for a in arrays]


def _other_chips(x, y):
    return [(1 - x, y), (x, 1 - y), (1 - x, 1 - y)]


def _ici_copies_start(sets, sources, landings, src_index, dst_index, name):
    n = len(sources)
    n_sets = len(sets)

    def body(*refs):
        src, land = refs[:n], refs[n:2 * n]
        sems = refs[2 * n:2 * n + 2 * n_sets]
        token = refs[-1]
        x, y, c = _position()
        for s, members in enumerate(sets):
            for slot, (k, rows) in enumerate(members):
                for j, chip in enumerate(_other_chips(x, y)):
                    pltpu.make_async_remote_copy(
                        src_ref=src[k] if src_index is None else _shard_rows(src[k], src_index(chip), rows),
                        dst_ref=_shard_rows(land[k], dst_index(x, y, c), rows),
                        send_sem=sems[2 * s].at[3 * slot + j], recv_sem=sems[2 * s + 1].at[3 * slot + j],
                        device_id=(*chip, c), device_id_type=MESH_ID).start()
        token[...] = jnp.zeros_like(token)

    sem_shapes = []
    for members in sets:
        sem_shapes += [pltpu.SemaphoreType.DMA((3 * len(members),))] * 2
    out = pl.pallas_call(
        body, name=name,
        out_shape=sem_shapes + _hbm_like(sources) + _hbm_like(landings) + [TOKEN],
        in_specs=[HBM] * (2 * n), out_specs=[SEM] * (2 * n_sets) + [HBM] * (2 * n) + [pl.BlockSpec(memory_space=pltpu.VMEM)],
        input_output_aliases={i: 2 * n_sets + i for i in range(2 * n)},
        compiler_params=pltpu.CompilerParams(has_side_effects=DATAFLOW),
    )(*[_in_hbm(a) for a in sources], *[_in_hbm(a) for a in landings])
    sems = [(out[2 * s], out[2 * s + 1]) for s in range(n_sets)]
    return sems, list(out[2 * n_sets:2 * n_sets + n]), list(out[2 * n_sets + n:2 * n_sets + 2 * n]), out[-1]


def _ici_copies_wait(sems, members, sources, landings, src_index, recv_index, after, name):
    n = len(sources)

    def body(*refs):
        src, land = refs[:n], refs[n:2 * n]
        send_sems, recv_sems = refs[2 * n], refs[2 * n + 1]
        x, y, c = _position()
        for slot, rows in enumerate(members):
            for j, chip in enumerate(_other_chips(x, y)):
                cp = pltpu.make_async_remote_copy(
                    src_ref=src[slot] if src_index is None else _shard_rows(src[slot], src_index(chip), rows),
                    dst_ref=_shard_rows(land[slot], recv_index(chip, c), rows),
                    send_sem=send_sems.at[3 * slot + j], recv_sem=recv_sems.at[3 * slot + j],
                    device_id=(*chip, c), device_id_type=MESH_ID)
                cp.wait_send()
                cp.wait_recv()

    out = pl.pallas_call(
        body, name=name, out_shape=_hbm_like(sources) + _hbm_like(landings),
        in_specs=[HBM] * (2 * n) + [SEM, SEM, ANY], out_specs=[HBM] * (2 * n),
        input_output_aliases={i: i for i in range(2 * n)},
        compiler_params=pltpu.CompilerParams(has_side_effects=DATAFLOW),
    )(*sources, *landings, sems[0], sems[1], after)
    return list(out[:n]), list(out[n:])


def _device_index(chip, c):
    return 4 * chip[0] + 2 * chip[1] + c


def _chip_index(chip):
    return 2 * chip[0] + chip[1]


def _d2d_gather(shards, landings, name):
    n = len(shards)
    rows = [s.shape[0] for s in shards]

    def body(*refs):
        src, land = refs[:n], refs[2 * n:3 * n]
        send_sems, recv_sems, local_sems = refs[3 * n:]
        x, y, c = _position()
        chips = _other_chips(x, y)
        sibling = (x, y, 1 - c)

        def block(k, index):
            return _shard_rows(land[k], index, rows[k])

        def copy(k, j, index, source=None):
            return pltpu.make_async_remote_copy(
                src_ref=block(k, index) if source is None else source, dst_ref=block(k, index),
                send_sem=send_sems.at[k, j], recv_sem=recv_sems.at[k, j], device_id=sibling, device_id_type=MESH_ID)

        me = _device_index((x, y), c)
        mine = [pltpu.make_async_copy(src[k], block(k, me), local_sems.at[k]) for k in range(n)]
        sends, recvs = [], []
        for k in range(n):
            sends.append(copy(k, 0, me, source=src[k]))
            recvs.append(copy(k, 0, _device_index((x, y), 1 - c)))
            for j, chip in enumerate(chips):
                sends.append(copy(k, 1 + j, _device_index(chip, c)))
                recvs.append(copy(k, 1 + j, _device_index(chip, 1 - c)))
        for cp in mine + sends:
            cp.start()
        for cp in recvs:
            cp.wait_recv()
        for cp in sends:
            cp.wait_send()
        for cp in mine:
            cp.wait()

    return pl.pallas_call(
        body, name=name, out_shape=[jax.ShapeDtypeStruct(a.shape, a.dtype) for a in landings],
        in_specs=[ANY] * (2 * n), out_specs=[ANY] * n, input_output_aliases={n + i: i for i in range(n)},
        scratch_shapes=[pltpu.SemaphoreType.DMA((n, 4)), pltpu.SemaphoreType.DMA((n, 4)), pltpu.SemaphoreType.DMA((n,))],
    )(*shards, *landings)


def _rs_pair(grads, name):
    n = len(grads)
    rows = [g.shape[0] // N_DEV for g in grads]

    def body(*refs):
        ins, outs = refs[:n], refs[n:2 * n]
        send_sems, recv_sems = refs[2 * n:]
        x, y, c = _position()
        copies = []
        for k in range(n):
            for q in range(4):
                copies.append(pltpu.make_async_remote_copy(
                    src_ref=_shard_rows(ins[k], 2 * q + 1 - c, rows[k]), dst_ref=_shard_rows(outs[k], q, rows[k]),
                    send_sem=send_sems.at[k, q], recv_sem=recv_sems.at[k, q], device_id=(x, y, 1 - c),
                    device_id_type=MESH_ID))
        for cp in copies:
            cp.start()
        for cp in copies:
            cp.wait()

    return pl.pallas_call(
        body, out_shape=[jax.ShapeDtypeStruct((4 * r, g.shape[1]), g.dtype) for g, r in zip(grads, rows)],
        in_specs=[ANY] * n, out_specs=[ANY] * n,
        scratch_shapes=[pltpu.SemaphoreType.DMA((n, 4)), pltpu.SemaphoreType.DMA((n, 4))],
        name=name)(*grads)


def _pair_add(grad, received, place, name):
    r = received.shape[0] // 4
    tr = 352 if r % 352 == 0 else r
    per = r // tr

    def body(place_ref, g_ref, r_ref, o_ref, land_ref):
        total = (g_ref[...].astype(F32) + r_ref[...].astype(F32)).astype(BF)
        o_ref[...] = total

        @pl.when(pl.program_id(1) == place_ref[1])
        def _():
            land_ref[...] = total

    return pl.pallas_call(
        body,
        grid_spec=pltpu.PrefetchScalarGridSpec(
            num_scalar_prefetch=1, grid=(per, 4),
            in_specs=[pl.BlockSpec((tr, D), lambda i, q, p: ((2 * q + p[0]) * per + i, 0)),
                      pl.BlockSpec((tr, D), lambda i, q, p: (q * per + i, 0))],
            out_specs=[pl.BlockSpec((tr, D), lambda i, q, p: (q * per + i, 0)),
                       pl.BlockSpec((tr, D), lambda i, q, p: (p[1] * per + i, 0))]),
        out_shape=[jax.ShapeDtypeStruct(received.shape, BF)] * 2,
        compiler_params=_params(2), name=name)(place, grad, received)


def _all_reduce_small(payload):
    r = payload.shape[0]

    def body(in_ref, out_ref, land_ref, send_sems, recv_sems):
        x, y, c = _position()
        me = 4 * x + 2 * y + c
        land_ref[me] = in_ref[...]
        copies = []
        for k in range(1, N_DEV):
            peer = (x ^ (k >> 2), y ^ ((k >> 1) & 1), c ^ (k & 1))
            copies.append(pltpu.make_async_remote_copy(
                src_ref=in_ref, dst_ref=land_ref.at[me], send_sem=send_sems.at[k - 1], recv_sem=recv_sems.at[k - 1],
                device_id=peer, device_id_type=MESH_ID))
        for cp in copies:
            cp.start()
        for k in range(1, N_DEV):
            peer_index = me ^ k
            pltpu.make_async_remote_copy(
                src_ref=in_ref, dst_ref=land_ref.at[peer_index], send_sem=send_sems.at[k - 1], recv_sem=recv_sems.at[k - 1],
                device_id=(x, y, c), device_id_type=MESH_ID).wait_recv()
        for cp in copies:
            cp.wait_send()
        acc = land_ref[0]
        for d in range(1, N_DEV):
            acc = acc + land_ref[d]
        out_ref[...] = acc

    return pl.pallas_call(
        body, out_shape=jax.ShapeDtypeStruct((r, D), F32),
        in_specs=[pl.BlockSpec(memory_space=pltpu.VMEM)], out_specs=pl.BlockSpec(memory_space=pltpu.VMEM),
        scratch_shapes=[pltpu.VMEM((N_DEV, r, D), F32), pltpu.SemaphoreType.DMA((N_DEV - 1,)),
                        pltpu.SemaphoreType.DMA((N_DEV - 1,))],
        name="all_reduce_small")(payload)


def _adamw_math(w, g, m, v):
    m = ADAM_B1 * m + (1.0 - ADAM_B1) * g
    v = ADAM_B2 * v + (1.0 - ADAM_B2) * (g * g)
    m_hat = m / (1.0 - ADAM_B1 ** ADAM_STEP)
    v_hat = v / (1.0 - ADAM_B2 ** ADAM_STEP)
    delta = -ADAM_LR * (m_hat / (jnp.sqrt(v_hat) + ADAM_EPS) + ADAM_WD * w)
    return delta, m, v


def _reduce_adamw(received, w, m, v, name, transposed):
    r = received.shape[0] // 4

    def body(r_ref, w_ref, m_ref, v_ref, g_ref, d_ref, nm_ref, nv_ref):
        g = r_ref[0:r, :].astype(F32)
        for q in range(1, 4):
            g = g + r_ref[q * r:(q + 1) * r, :].astype(F32)
        if transposed:
            g = g.T
        g_ref[...] = g
        d_ref[...], nm_ref[...], nv_ref[...] = _adamw_math(w_ref[...], g, m_ref[...], v_ref[...])

    return pl.pallas_call(body, out_shape=[jax.ShapeDtypeStruct(w.shape, F32)] * 4,
                          compiler_params=pltpu.CompilerParams(vmem_limit_bytes=VMEM_LIMIT_V7X), name=name)(received, w, m, v)


def _adamw_small(w, g, m, v, name):
    def body(w_ref, g_ref, m_ref, v_ref, d_ref, nm_ref, nv_ref):
        d_ref[...], nm_ref[...], nv_ref[...] = _adamw_math(w_ref[...], g_ref[...], m_ref[...], v_ref[...])

    return pl.pallas_call(body, out_shape=[jax.ShapeDtypeStruct(w.shape, F32)] * 3, name=name)(w, g, m, v)


WEIGHTS = ("ffn1_norm", "ffn1_w_in", "ffn1_w_out", "mix_norm", "w_in", "conv_dw_kernel", "conv_dw_bias", "conv_ln_g",
           "conv_ln_b", "conv_w_proj", "q_norm", "k_norm", "attn_sinks", "rel_bias", "attn_w_o", "w_out", "ffn2_norm",
           "ffn2_w_in", "ffn2_w_out")
MATRICES = ("ffn1_w_in", "ffn1_w_out", "w_in", "conv_w_proj", "attn_w_o", "w_out", "ffn2_w_in", "ffn2_w_out")
COLUMN_SHARDED = ("ffn1_w_in", "w_in", "ffn2_w_in")
ROW_VECTORS = ("ffn1_norm", "mix_norm", "conv_dw_bias", "conv_ln_g", "conv_ln_b", "ffn2_norm")
PACKED = (("q_norm", HD), ("k_norm", HD), ("attn_sinks", NQ), ("rel_bias", NBUCKET * NQ))
GATHER_STAGES = ("ffn1", "mix", "ffn2")
STAGE_MEMBERS = {"ffn1": ("ffn1_w_in", "ffn1_w_out"), "mix": ("w_in", "conv_w_proj", "attn_w_o", "w_out", "taps"),
                 "ffn2": ("ffn2_w_in", "ffn2_w_out")}
ROW_PACKED = len(ROW_VECTORS)
ROW_LOSS = ROW_PACKED + 1
ROW_TAPS = 8
PAYLOAD_ROWS = ROW_TAPS + CWP


def _pack_small(values, last_row):
    packed = jnp.concatenate([values[k].reshape(-1) for k, _ in PACKED])
    packed = jnp.pad(packed, (0, D - packed.shape[0])).reshape(1, D)
    return jnp.concatenate([values[k].reshape(1, D) for k in ROW_VECTORS] + [packed, last_row], axis=0)


def _unpack_small(rows):
    out = {k: rows[i] for i, k in enumerate(ROW_VECTORS)}
    at = 0
    for k, size in PACKED:
        out[k] = rows[ROW_PACKED, at:at + size]
        at += size
    out["rel_bias"] = out["rel_bias"].reshape(NBUCKET, NQ)
    return out


def kernel(x, ffn1_norm, ffn1_w_in, ffn1_w_out, mix_norm, w_in, conv_dw_kernel, conv_dw_bias, conv_ln_g, conv_ln_b, conv_w_proj, q_norm, k_norm, attn_sinks, rel_bias, attn_w_o, w_out, ffn2_norm, ffn2_w_in, ffn2_w_out, loss_target, m_ffn1_norm, m_ffn1_w_in, m_ffn1_w_out, m_mix_norm, m_w_in, m_conv_dw_kernel, m_conv_dw_bias, m_conv_ln_g, m_conv_ln_b, m_conv_w_proj, m_q_norm, m_k_norm, m_attn_sinks, m_rel_bias, m_attn_w_o, m_w_out, m_ffn2_norm, m_ffn2_w_in, m_ffn2_w_out, v_ffn1_norm, v_ffn1_w_in, v_ffn1_w_out, v_mix_norm, v_w_in, v_conv_dw_kernel, v_conv_dw_bias, v_conv_ln_g, v_conv_ln_b, v_conv_w_proj, v_q_norm, v_k_norm, v_attn_sinks, v_rel_bias, v_attn_w_o, v_w_out, v_ffn2_norm, v_ffn2_w_in, v_ffn2_w_out):
    w = dict(ffn1_norm=ffn1_norm, ffn1_w_in=ffn1_w_in, ffn1_w_out=ffn1_w_out, mix_norm=mix_norm, w_in=w_in,
             conv_dw_kernel=conv_dw_kernel, conv_dw_bias=conv_dw_bias, conv_ln_g=conv_ln_g, conv_ln_b=conv_ln_b,
             conv_w_proj=conv_w_proj, q_norm=q_norm, k_norm=k_norm, attn_sinks=attn_sinks, rel_bias=rel_bias,
             attn_w_o=attn_w_o, w_out=w_out, ffn2_norm=ffn2_norm, ffn2_w_in=ffn2_w_in, ffn2_w_out=ffn2_w_out)
    m = dict(ffn1_norm=m_ffn1_norm, ffn1_w_in=m_ffn1_w_in, ffn1_w_out=m_ffn1_w_out, mix_norm=m_mix_norm, w_in=m_w_in,
             conv_dw_kernel=m_conv_dw_kernel, conv_dw_bias=m_conv_dw_bias, conv_ln_g=m_conv_ln_g, conv_ln_b=m_conv_ln_b,
             conv_w_proj=m_conv_w_proj, q_norm=m_q_norm, k_norm=m_k_norm, attn_sinks=m_attn_sinks, rel_bias=m_rel_bias,
             attn_w_o=m_attn_w_o, w_out=m_w_out, ffn2_norm=m_ffn2_norm, ffn2_w_in=m_ffn2_w_in, ffn2_w_out=m_ffn2_w_out)
    v = dict(ffn1_norm=v_ffn1_norm, ffn1_w_in=v_ffn1_w_in, ffn1_w_out=v_ffn1_w_out, mix_norm=v_mix_norm, w_in=v_w_in,
             conv_dw_kernel=v_conv_dw_kernel, conv_dw_bias=v_conv_dw_bias, conv_ln_g=v_conv_ln_g, conv_ln_b=v_conv_ln_b,
             conv_w_proj=v_conv_w_proj, q_norm=v_q_norm, k_norm=v_k_norm, attn_sinks=v_attn_sinks, rel_bias=v_rel_bias,
             attn_w_o=v_attn_w_o, w_out=v_w_out, ffn2_norm=v_ffn2_norm, ffn2_w_in=v_ffn2_w_in, ffn2_w_out=v_ffn2_w_out)
    px, py, pc = _position()
    me = 4 * px + 2 * py + pc
    place = jnp.stack([pc, 2 * px + py]).astype(jnp.int32)

    shards = dict(zip(MATRICES, _prep([w[k] for k in MATRICES], [k in COLUMN_SHARDED for k in MATRICES])))
    shards["taps"] = jnp.pad(conv_dw_kernel, ((0, CWP - CW), (0, 0)))
    sources, sets = [], []
    for stage in GATHER_STAGES:
        sets.append([(len(sources) + i, shards[k].shape[0]) for i, k in enumerate(STAGE_MEMBERS[stage])])
        sources += [shards[k] for k in STAGE_MEMBERS[stage]]
    landings = [lax.empty((N_DEV * a.shape[0], a.shape[1]), a.dtype) for a in sources]
    sems, src_thru, land_thru, _ = _ici_copies_start(sets, sources, landings, None,
                                                     lambda x, y, c: _device_index((x, y), c), "gather_start")

    def weights_of(stage, after):
        s = GATHER_STAGES.index(stage)
        ks = [k for k, _ in sets[s]]
        src, land = _ici_copies_wait(sems[s], [r for _, r in sets[s]], [src_thru[k] for k in ks],
                                     [land_thru[k] for k in ks], None, _device_index, after, "gather_wait_" + stage)
        out = dict(zip(STAGE_MEMBERS[stage], _d2d_gather(src, land, "gather_d2d_" + stage)))
        if "taps" in out:
            taps = out.pop("taps")
            out["conv_dw_kernel"] = jnp.transpose(taps.reshape(N_DEV, CWP, BLK), (1, 0, 2)).reshape(CWP, D)[:CW]
        return out

    in_flight = []

    def grads_done(stage, grads):
        names = list(grads)
        received = _rs_pair([grads[k] for k in names], "rs_pair_" + stage)
        added = [_pair_add(grads[k], r, place, "pair_add_" + k) for k, r in zip(names, received)]
        partials = [p for p, _ in added]
        members = [(i, p.shape[0] // 4) for i, p in enumerate(partials)]
        sem, p_thru, l_thru, token = _ici_copies_start([members], partials, [l for _, l in added], _chip_index,
                                                       lambda x, y, c: _chip_index((x, y)), "scatter_start_" + stage)
        in_flight.append((stage, names, sem[0], p_thru, l_thru))
        return [token]

    vec = {k: w[k] for k in WEIGHTS if k not in MATRICES and k != "conv_dw_kernel"}
    sq, dx0, gv = _local_step(x[0], loss_target[0], vec, weights_of, grads_done)

    payload = jnp.concatenate([_pack_small(gv, sq), jnp.pad(gv["conv_dw_kernel"], ((0, CWP - CW), (0, 0)))], axis=0)
    total = _all_reduce_small(payload)
    loss = (0.5 / D) * jnp.sum(total[ROW_LOSS])

    from_chips = {}
    for stage, names, sem, p_thru, l_thru in in_flight:
        _, landed = _ici_copies_wait(sem, [p.shape[0] // 4 for p in p_thru], p_thru, l_thru, _chip_index,
                                     lambda chip, c: _chip_index(chip), total, "scatter_wait_" + stage)
        from_chips.update(zip(names, landed))

    grads, delta, new_m, new_v = {}, {}, {}, {}
    for k in MATRICES:
        grads[k], delta[k], new_m[k], new_v[k] = _reduce_adamw(from_chips[k], w[k], m[k], v[k], "adamw_" + k,
                                                               transposed=k in COLUMN_SHARDED)
    zero_row = jnp.zeros((1, D), F32)
    d8, m8, v8 = _adamw_small(_pack_small(w, zero_row), total[:ROW_TAPS], _pack_small(m, zero_row),
                              _pack_small(v, zero_row), "adamw_small")
    grads.update(_unpack_small(total[:ROW_TAPS]))
    delta.update(_unpack_small(d8))
    new_m.update(_unpack_small(m8))
    new_v.update(_unpack_small(v8))
    k = "conv_dw_kernel"
    grads[k] = lax.dynamic_slice_in_dim(total[ROW_TAPS:ROW_TAPS + CW], me * BLK, BLK, axis=1)
    delta[k], new_m[k], new_v[k] = _adamw_small(w[k], grads[k], m[k], v[k], "adamw_taps")

    return (loss, dx0[None], *[grads[k] for k in WEIGHTS], *[delta[k] for k in WEIGHTS],
            *[new_m[k] for k in WEIGHTS], *[new_v[k] for k in WEIGHTS])
```

```python
import functools
import math

import numpy as np
import jax
import jax.numpy as jnp
from jax import lax
from jax.experimental import pallas as pl
from jax.experimental.pallas import tpu as pltpu

F32 = jnp.float32
BF = jnp.bfloat16

D = 1024
F = 2816
INW = 5632
CW = 31
CWP = 32
HD = 64
NQ = 16
NKV = 4
GRP = NQ // NKV
BLK = 128
NBUCKET = 32
EPS = 1e-6
NEG = float(jnp.finfo(jnp.float32).min)
QK_SCALE = 1.0 / math.sqrt(HD)
R_CONV = (0, 2048)
R_QKV = (2048, 3584)
R_Q = (2048, 3072)
R_KV = (3072, 3584)
R_GATE = (3584, 5632)

N_DEV = 8
VMEM_LIMIT_V7X = 56 * 1024 * 1024

ADAM_LR = 0.001
ADAM_B1 = 0.9
ADAM_B2 = 0.999
ADAM_EPS = 1e-08
ADAM_WD = 0.01
ADAM_STEP = 10

NT_DIMS = (((1,), (1,)), ((), ()))
TN_DIMS = (((0,), (0,)), ((), ()))


def _dot(a, b):
    return jnp.dot(a, b, preferred_element_type=F32)


def _dot_nt(a, b):
    return lax.dot_general(a, b, NT_DIMS, preferred_element_type=F32)


def _dot_tn(a, b):
    return lax.dot_general(a, b, TN_DIMS, preferred_element_type=F32)


def _sig(x):
    return 1.0 / (1.0 + jnp.exp(-x))


ANY = pl.BlockSpec(memory_space=pl.ANY)


def _call(body, deps, args, **kw):
    n = len(deps)
    if n:
        kw["in_specs"] = [ANY] * n + list(kw["in_specs"])
        return pl.pallas_call(lambda *refs: body(*refs[n:]), **kw)(*deps, *args)
    return pl.pallas_call(body, **kw)(*args)


def _params(n_axes):
    return pltpu.CompilerParams(dimension_semantics=("arbitrary",) * n_axes, vmem_limit_bytes=VMEM_LIMIT_V7X)


def _resident(shape):
    zeros = (0,) * len(shape)
    return pl.BlockSpec(shape, lambda *_: zeros, pipeline_mode=pl.Buffered(1))


def _row_tile(rows, cols):
    return pl.BlockSpec((rows, cols), lambda i: (i, 0))


def _rms_stats(x):
    r = lax.rsqrt(jnp.mean(x * x, axis=-1, keepdims=True) + EPS)
    return r, x * r


def _rms_bwd(dn, x, g):
    r, xh = _rms_stats(x)
    dxh = dn * g
    dx = r * (dxh - xh * jnp.mean(dxh * xh, axis=-1, keepdims=True))
    return dx, jnp.sum(dn * xh, axis=0, keepdims=True)


def _ffn_fwd(x, g, w_in_t, w_out, name, target=None):
    t = x.shape[0]
    tm = min(256, t)
    with_loss = target is not None

    def body(*refs):
        if with_loss:
            x_ref, g_ref, w_ref, wo_ref, t_ref, n_ref, u_ref, dy_ref, sq_ref = refs
        else:
            x_ref, g_ref, w_ref, wo_ref, n_ref, u_ref, xo_ref = refs
        x = x_ref[...]
        r, xh = _rms_stats(x)
        n = (xh * g_ref[...]).astype(BF)
        n_ref[...] = n
        u = _dot_nt(n, w_ref[...])
        u_ref[...] = u.astype(BF)
        a = u[:, :F]
        b = u[:, F:]
        h = (a * _sig(a) * b).astype(BF)
        xo = x + 0.5 * _dot(h, wo_ref[...])
        if with_loss:
            err = xo - t_ref[...]
            dy_ref[...] = err * (1.0 / D)

            @pl.when(pl.program_id(0) == 0)
            def _():
                sq_ref[...] = jnp.zeros_like(sq_ref)

            sq_ref[...] += jnp.sum(err * err, axis=0, keepdims=True)
        else:
            xo_ref[...] = xo

    in_specs = [_row_tile(tm, D), _resident((1, D)), _resident((INW, D)), _resident((F, D))]
    args = [x, g, w_in_t, w_out]
    out_specs = [_row_tile(tm, D), _row_tile(tm, INW), _row_tile(tm, D)]
    out_shape = [jax.ShapeDtypeStruct((t, D), BF), jax.ShapeDtypeStruct((t, INW), BF), jax.ShapeDtypeStruct((t, D), F32)]
    if with_loss:
        in_specs.append(_row_tile(tm, D))
        args.append(target)
        out_specs.append(pl.BlockSpec((1, D), lambda i: (0, 0)))
        out_shape.append(jax.ShapeDtypeStruct((1, D), F32))
    return pl.pallas_call(body, grid=(t // tm,), in_specs=in_specs, out_specs=out_specs, out_shape=out_shape,
                          compiler_params=_params(1), name=name)(*args)


def _ffn_bwd(dxo, x, g, u, w_in_t, w_out, name, deps=()):
    t = x.shape[0]
    tm = min(256, t)

    def body(dxo_ref, x_ref, g_ref, u_ref, w_ref, wo_ref, dx_ref, du_ref, h_ref, dy_ref, dg_ref):
        dxo = dxo_ref[...]
        dy = (0.5 * dxo).astype(BF)
        dy_ref[...] = dy
        dh = _dot_nt(dy, wo_ref[...])
        a = u_ref[:, :F].astype(F32)
        b = u_ref[:, F:].astype(F32)
        s = _sig(a)
        sa = a * s
        h_ref[...] = (sa * b).astype(BF)
        du_ref[:, :F] = (dh * b * (s * (1.0 + a * (1.0 - s)))).astype(BF)
        du_ref[:, F:] = (dh * sa).astype(BF)
        dn = _dot(du_ref[...], w_ref[...])
        dx, dg = _rms_bwd(dn, x_ref[...], g_ref[...])
        dx_ref[...] = dxo + dx

        @pl.when(pl.program_id(0) == 0)
        def _():
            dg_ref[...] = jnp.zeros_like(dg_ref)

        dg_ref[...] += dg

    return _call(
        body, deps, (dxo, x, g, u, w_in_t, w_out), grid=(t // tm,),
        in_specs=[_row_tile(tm, D), _row_tile(tm, D), _resident((1, D)), _row_tile(tm, INW), _resident((INW, D)),
                  _resident((F, D))],
        out_specs=[_row_tile(tm, D), _row_tile(tm, INW), _row_tile(tm, F), _row_tile(tm, D),
                   pl.BlockSpec((1, D), lambda i: (0, 0))],
        out_shape=[jax.ShapeDtypeStruct((t, D), F32), jax.ShapeDtypeStruct((t, INW), BF), jax.ShapeDtypeStruct((t, F), BF),
                   jax.ShapeDtypeStruct((t, D), BF), jax.ShapeDtypeStruct((1, D), F32)],
        compiler_params=_params(1), name=name)


def _wgrad(lhs, rhs, name, *, lhs_is_transposed, chunk, deps=()):
    t = rhs.shape[0]
    n = lhs.shape[0] if lhs_is_transposed else lhs.shape[1]
    tm = min(512, t)
    c = min(chunk, n)
    n_tok = t // tm

    def body(l_ref, r_ref, o_ref, acc_ref):
        i = pl.program_id(1)

        @pl.when(i == 0)
        def _():
            acc_ref[...] = jnp.zeros_like(acc_ref)

        lhs_tile = l_ref[...].astype(BF)
        rhs_tile = r_ref[...].astype(BF)
        if lhs_is_transposed:
            acc_ref[...] += _dot(lhs_tile, rhs_tile)
        else:
            acc_ref[...] += _dot_tn(lhs_tile, rhs_tile)

        @pl.when(i == n_tok - 1)
        def _():
            o_ref[...] = acc_ref[...].astype(o_ref.dtype)

    if lhs_is_transposed:
        lhs_spec = pl.BlockSpec((c, tm), lambda j, i: (j, i))
    else:
        lhs_spec = pl.BlockSpec((tm, c), lambda j, i: (i, j))
    return _call(
        body, deps, (lhs, rhs), grid=(n // c, n_tok),
        in_specs=[lhs_spec, pl.BlockSpec((tm, D), lambda j, i: (i, 0))],
        out_specs=pl.BlockSpec((c, D), lambda j, i: (j, 0)),
        out_shape=jax.ShapeDtypeStruct((n, D), BF),
        scratch_shapes=[pltpu.VMEM((c, D), F32)],
        compiler_params=_params(2), name=name)


def _wgrad_mix(duc, dq_t, dkv_t, dgp, hm):
    t = hm.shape[0]
    tm = min(1024, t)
    c = 512
    n_tok = t // tm
    first_q, first_kv, first_gate = R_Q[0] // c, R_KV[0] // c, R_GATE[0] // c

    def body(uc_ref, q_ref, kv_ref, gp_ref, h_ref, o_ref, acc_ref):
        j = pl.program_id(0)
        i = pl.program_id(1)

        @pl.when(i == 0)
        def _():
            acc_ref[...] = jnp.zeros_like(acc_ref)

        @pl.when(j < first_q)
        def _():
            acc_ref[...] += _dot_tn(uc_ref[...], h_ref[...])

        @pl.when((j >= first_q) & (j < first_kv))
        def _():
            acc_ref[...] += _dot(q_ref[...], h_ref[...])

        @pl.when(j == first_kv)
        def _():
            acc_ref[...] += _dot(kv_ref[...], h_ref[...])

        @pl.when(j >= first_gate)
        def _():
            acc_ref[...] += _dot_tn(gp_ref[...], h_ref[...])

        @pl.when(i == n_tok - 1)
        def _():
            o_ref[...] = acc_ref[...].astype(BF)

    def tokens(active, i):
        return jnp.where(active, i, 0)

    return pl.pallas_call(
        body, grid=(INW // c, n_tok),
        in_specs=[pl.BlockSpec((tm, c), lambda j, i: (tokens(j < first_q, i), jnp.clip(j, 0, first_q - 1))),
                  pl.BlockSpec((c, tm), lambda j, i: (jnp.clip(j - first_q, 0, first_kv - first_q - 1),
                                                      tokens((j >= first_q) & (j < first_kv), i))),
                  pl.BlockSpec((c, tm), lambda j, i: (0, tokens(j == first_kv, i))),
                  pl.BlockSpec((tm, c), lambda j, i: (tokens(j >= first_gate, i),
                                                      jnp.clip(j - first_gate, 0, INW // c - first_gate - 1))),
                  pl.BlockSpec((tm, D), lambda j, i: (i, 0))],
        out_specs=pl.BlockSpec((c, D), lambda j, i: (j, 0)),
        out_shape=jax.ShapeDtypeStruct((INW, D), BF),
        scratch_shapes=[pltpu.VMEM((c, D), F32)],
        compiler_params=_params(2), name="mix_dw_in")(duc, dq_t, dkv_t, dgp, hm)


def _mix_proj(x, g, w_t):
    t = x.shape[0]
    tm = min(256, t)

    def body(x_ref, g_ref, w_ref, hm_ref, uc_ref, gp_ref, qkv_ref):
        r, xh = _rms_stats(x_ref[...])
        hm = (xh * g_ref[...]).astype(BF)
        hm_ref[...] = hm
        uc_ref[...] = _dot_nt(hm, w_ref[R_CONV[0]:R_CONV[1], :]).astype(BF)
        gp_ref[...] = _dot_nt(hm, w_ref[R_GATE[0]:R_GATE[1], :]).astype(BF)
        qkv_ref[...] = _dot_nt(w_ref[R_QKV[0]:R_QKV[1], :], hm).astype(BF)

    return pl.pallas_call(
        body, grid=(t // tm,),
        in_specs=[_row_tile(tm, D), _resident((1, D)), _resident((INW, D))],
        out_specs=[_row_tile(tm, D), _row_tile(tm, 2 * D), _row_tile(tm, 2 * D), pl.BlockSpec((1536, tm), lambda i: (0, i))],
        out_shape=[jax.ShapeDtypeStruct((t, D), BF), jax.ShapeDtypeStruct((t, 2 * D), BF),
                   jax.ShapeDtypeStruct((t, 2 * D), BF), jax.ShapeDtypeStruct((1536, t), BF)],
        compiler_params=_params(1), name="mix_proj")(x, g, w_t)


CONV_HALO = 32
CONV_LEAD = CONV_HALO - (CW - 1)


def _glu(uc):
    uc = uc.astype(F32)
    return uc[:, :D] * _sig(uc[:, D:])


def _ln_stats(zc):
    mu = jnp.mean(zc, axis=-1, keepdims=True)
    zm = zc - mu
    r = lax.rsqrt(jnp.mean(zm * zm, axis=-1, keepdims=True) + EPS)
    return r, zm * r


CONV_SHIFTS = 8
CONV_CHUNK = 32


def _store_shifted(buf, rows):
    for b in range(1, CONV_SHIFTS):
        buf[b, 0:rows - 8, :] = buf[0, pl.ds(b, rows - 8), :]


def _conv_fwd(uc, dwk, dwb, lng, lnb):
    t = uc.shape[0]
    tm = min(512, t)
    per = tm // CONV_HALO
    ext = tm + CONV_HALO

    def body(cur_ref, prev_ref, k_ref, kb_ref, g_ref, b_ref, o_ref, zc_ref, zsh):
        i = pl.program_id(0)
        zsh[0, 0:CONV_HALO, :] = _glu(prev_ref[...]) * (i > 0).astype(F32)
        zsh[0, CONV_HALO:, :] = _glu(cur_ref[...])
        _store_shifted(zsh, ext)

        def chunk(ci, carry):
            r0 = pl.multiple_of(ci * CONV_CHUNK, CONV_CHUNK)
            acc = jnp.zeros((CONV_CHUNK, D), F32) + kb_ref[...]
            for w in range(CW):
                a, b = divmod(CONV_LEAD + w, 8)
                acc = acc + k_ref[w:w + 1, :] * zsh[b, pl.ds(r0 + 8 * a, CONV_CHUNK), :]
            zc_ref[pl.ds(r0, CONV_CHUNK), :] = acc
            r, xh = _ln_stats(acc)
            y = xh * g_ref[...] + b_ref[...]
            o_ref[pl.ds(r0, CONV_CHUNK), :] = (y * _sig(y)).astype(BF)
            return carry

        lax.fori_loop(0, tm // CONV_CHUNK, chunk, 0)

    return pl.pallas_call(
        body, grid=(t // tm,),
        in_specs=[_row_tile(tm, 2 * D),
                  pl.BlockSpec((CONV_HALO, 2 * D), lambda i: (jnp.maximum(i * per - 1, 0), 0)),
                  _resident((CWP, D)), _resident((1, D)), _resident((1, D)), _resident((1, D))],
        out_specs=[_row_tile(tm, D), _row_tile(tm, D)],
        out_shape=[jax.ShapeDtypeStruct((t, D), BF), jax.ShapeDtypeStruct((t, D), F32)],
        scratch_shapes=[pltpu.VMEM((CONV_SHIFTS, ext, D), F32)],
        compiler_params=_params(1), name="conv_fwd")(uc, uc, dwk, dwb, lng, lnb)


def _conv_bwd(uc, zc, dzs, dwk, lng, lnb):
    t = uc.shape[0]
    tm = min(256, t)
    per = tm // CONV_HALO
    n_tiles = t // tm
    ext = tm + CONV_HALO
    last_block = t // CONV_HALO - 1

    def body(cur_ref, prev_ref, zc_ref, zcn_ref, dz_ref, dzn_ref, k_ref, g_ref, b_ref,
             duc_ref, dk_ref, dkb_ref, dg_ref, db_ref, zsh, dsh, dk8):
        i = pl.program_id(0)

        @pl.when(i == 0)
        def _():
            dk8[...] = jnp.zeros_like(dk8)
            dkb_ref[...] = jnp.zeros_like(dkb_ref)
            dg_ref[...] = jnp.zeros_like(dg_ref)
            db_ref[...] = jnp.zeros_like(db_ref)

        has_next = (i < n_tiles - 1).astype(F32)
        zsh[0, 0:CONV_HALO, :] = _glu(prev_ref[...]) * (i > 0).astype(F32)
        zsh[0, CONV_HALO:, :] = _glu(cur_ref[...])
        _store_shifted(zsh, ext)
        gain = g_ref[...]

        def ln_silu_bwd(zc, dzs, live):
            r, xh = _ln_stats(zc)
            y = xh * gain + b_ref[...]
            sy = _sig(y)
            dy = dzs * (sy * (1.0 + y * (1.0 - sy))) * live
            dxh = dy * gain
            dzc = r * (dxh - jnp.mean(dxh, axis=-1, keepdims=True) - xh * jnp.mean(dxh * xh, axis=-1, keepdims=True))
            return dzc, dy, xh

        dzc, dy, xh = ln_silu_bwd(zc_ref[...], dz_ref[...], 1.0)
        dsh[0, 0:tm, :] = dzc
        dg_ref[...] += jnp.sum(dy * xh, axis=0, keepdims=True)
        db_ref[...] += jnp.sum(dy, axis=0, keepdims=True)
        dkb_ref[...] += jnp.sum(dzc, axis=0, keepdims=True)
        dsh[0, tm:, :] = ln_silu_bwd(zcn_ref[...], dzn_ref[...], has_next)[0]
        _store_shifted(dsh, ext)

        def chunk(ci, carry):
            r0 = pl.multiple_of(ci * CONV_CHUNK, CONV_CHUNK)
            dzc_c = dsh[0, pl.ds(r0, CONV_CHUNK), :]
            dz = jnp.zeros((CONV_CHUNK, D), F32)
            for w in range(CW):
                a, b = divmod(CW - 1 - w, 8)
                dz = dz + k_ref[w:w + 1, :] * dsh[b, pl.ds(r0 + 8 * a, CONV_CHUNK), :]
                a, b = divmod(CONV_LEAD + w, 8)
                prod = dzc_c * zsh[b, pl.ds(r0 + 8 * a, CONV_CHUNK), :]
                part = prod[0:8, :]
                for j in range(1, CONV_CHUNK // 8):
                    part = part + prod[8 * j:8 * j + 8, :]
                dk8[w] += part
            ucc = cur_ref[pl.ds(r0, CONV_CHUNK), :].astype(F32)
            sg = _sig(ucc[:, D:])
            duc_ref[pl.ds(r0, CONV_CHUNK), 0:D] = (dz * sg).astype(BF)
            duc_ref[pl.ds(r0, CONV_CHUNK), D:2 * D] = (dz * ucc[:, :D] * sg * (1.0 - sg)).astype(BF)
            return carry

        lax.fori_loop(0, tm // CONV_CHUNK, chunk, 0)

        @pl.when(i == n_tiles - 1)
        def _():
            dk_ref[...] = jnp.sum(dk8[...], axis=1)

    vec = pl.BlockSpec((1, D), lambda i: (0, 0))
    next_halo = pl.BlockSpec((CONV_HALO, D), lambda i: (jnp.minimum((i + 1) * per, last_block), 0))
    return pl.pallas_call(
        body, grid=(n_tiles,),
        in_specs=[_row_tile(tm, 2 * D),
                  pl.BlockSpec((CONV_HALO, 2 * D), lambda i: (jnp.maximum(i * per - 1, 0), 0)),
                  _row_tile(tm, D), next_halo, _row_tile(tm, D), next_halo,
                  _resident((CWP, D)), _resident((1, D)), _resident((1, D))],
        out_specs=[_row_tile(tm, 2 * D), pl.BlockSpec((CWP, D), lambda i: (0, 0)), vec, vec, vec],
        out_shape=[jax.ShapeDtypeStruct((t, 2 * D), BF), jax.ShapeDtypeStruct((CWP, D), F32),
                   jax.ShapeDtypeStruct((1, D), F32), jax.ShapeDtypeStruct((1, D), F32), jax.ShapeDtypeStruct((1, D), F32)],
        scratch_shapes=[pltpu.VMEM((CONV_SHIFTS, ext, D), F32), pltpu.VMEM((CONV_SHIFTS, ext, D), F32),
                        pltpu.VMEM((CWP, 8, D), F32)],
        compiler_params=_params(1), name="conv_bwd")(uc, uc, zc, zc, dzs, dzs, dwk, lng, lnb)


def _norm_rows(xt, g):
    r = lax.rsqrt(jnp.mean(xt * xt, axis=0, keepdims=True) + EPS)
    xh = xt * r
    return xh * g, r, xh


ATT_TQ = 512


def _attn_specs(t, tq):
    per = tq // BLK
    return [pl.BlockSpec((1536, tq), lambda i: (0, i)),
            pl.BlockSpec((512, BLK), lambda i: (2, jnp.maximum(i * per - 1, 0))),
            _resident((HD, 1)), _resident((HD, 1)), _resident((NKV, 1, GRP * BLK)),
            _resident((NKV, 2 * BLK, GRP * BLK)), _resident((2, 2 * BLK, GRP * BLK))]


def _attn_window(hk, sb, qkv_ref, halo_ref, kn_cur, kn_halo):
    v0 = D + NKV * HD + hk * HD
    if sb == 0:
        k_prev = kn_halo[hk]
        v_prev = halo_ref[NKV * HD + hk * HD:NKV * HD + (hk + 1) * HD, :]
    else:
        k_prev = kn_cur[hk][:, (sb - 1) * BLK:sb * BLK]
        v_prev = qkv_ref[v0:v0 + HD, (sb - 1) * BLK:sb * BLK]
    kw = jnp.concatenate([k_prev, kn_cur[hk][:, sb * BLK:(sb + 1) * BLK]], axis=1).astype(BF)
    vw = jnp.concatenate([v_prev, qkv_ref[v0:v0 + HD, sb * BLK:(sb + 1) * BLK]], axis=1)
    return kw, vw


def _attn_probs(kw, qc, bias, mask, sink):
    st = _dot_tn(kw, qc) * QK_SCALE + bias
    st = jnp.where(mask > 0.5, st, NEG)
    m = jnp.maximum(jnp.max(st, axis=0, keepdims=True), sink)
    p = jnp.exp(st - m)
    e_sink = jnp.exp(sink - m)
    inv = 1.0 / (jnp.sum(p, axis=0, keepdims=True) + e_sink)
    return p * inv, e_sink * inv


def _attn_fwd(qkv_t, qg, kg, sink_rows, bias_t, mask_t):
    t = qkv_t.shape[1]
    tq = min(ATT_TQ, t)
    n_sub = tq // BLK

    def body(qkv_ref, halo_ref, qg_ref, kg_ref, sink_ref, bias_ref, mask_ref, o_ref):
        i = pl.program_id(0)
        first = (i == 0).astype(jnp.int32)
        kgain = kg_ref[...]
        qgain = qg_ref[...]
        kn_cur = [_norm_rows(qkv_ref[D + h * HD:D + (h + 1) * HD, :].astype(F32), kgain)[0] for h in range(NKV)]
        kn_halo = [_norm_rows(halo_ref[h * HD:(h + 1) * HD, :].astype(F32), kgain)[0] for h in range(NKV)]
        for hk in range(NKV):
            for sb in range(n_sub):
                cols = slice(sb * BLK, (sb + 1) * BLK)
                kw, vw = _attn_window(hk, sb, qkv_ref, halo_ref, kn_cur, kn_halo)
                qc = jnp.concatenate(
                    [_norm_rows(qkv_ref[(GRP * hk + g) * HD:(GRP * hk + g + 1) * HD, cols].astype(F32), qgain)[0]
                     for g in range(GRP)], axis=1).astype(BF)
                mask = mask_ref[first] if sb == 0 else mask_ref[0]
                p, _ = _attn_probs(kw, qc, bias_ref[hk], mask, sink_ref[hk])
                o = _dot(vw, p.astype(BF))
                for g in range(GRP):
                    head = GRP * hk + g
                    o_ref[head * HD:(head + 1) * HD, cols] = o[:, g * BLK:(g + 1) * BLK].astype(BF)

    return pl.pallas_call(
        body, grid=(t // tq,),
        in_specs=_attn_specs(t, tq),
        out_specs=pl.BlockSpec((D, tq), lambda i: (0, i)),
        out_shape=jax.ShapeDtypeStruct((D, t), BF),
        compiler_params=_params(1), name="attn_fwd")(qkv_t, qkv_t, qg, kg, sink_rows, bias_t, mask_t)


def _attn_bwd(qkv_t, do_t, qg, kg, sink_rows, bias_t, mask_t, deps=()):
    t = qkv_t.shape[1]
    tq = min(ATT_TQ, t)
    n_sub = tq // BLK
    n_tiles = t // tq

    def body(qkv_ref, halo_ref, do_ref, qg_ref, kg_ref, sink_ref, bias_ref, mask_ref,
             dq_ref, ckv_ref, dqg_ref, dsink_ref, dsacc_ref, qg_scr):
        i = pl.program_id(0)

        @pl.when(i == 0)
        def _():
            qg_scr[...] = jnp.zeros_like(qg_scr)
            dsink_ref[...] = jnp.zeros_like(dsink_ref)
            dsacc_ref[...] = jnp.zeros_like(dsacc_ref)

        first = (i == 0).astype(jnp.int32)
        kgain = kg_ref[...]
        qgain = qg_ref[...]
        kn_cur = [_norm_rows(qkv_ref[D + h * HD:D + (h + 1) * HD, :].astype(F32), kgain)[0] for h in range(NKV)]
        kn_halo = [_norm_rows(halo_ref[h * HD:(h + 1) * HD, :].astype(F32), kgain)[0] for h in range(NKV)]
        dqg = jnp.zeros((HD, BLK), F32)
        for hk in range(NKV):
            for sb in range(n_sub):
                cols = slice(sb * BLK, (sb + 1) * BLK)
                kw, vw = _attn_window(hk, sb, qkv_ref, halo_ref, kn_cur, kn_halo)
                qn, qr, qh = [], [], []
                for g in range(GRP):
                    head = GRP * hk + g
                    n_, r_, h_ = _norm_rows(qkv_ref[head * HD:(head + 1) * HD, cols].astype(F32), qgain)
                    qn.append(n_)
                    qr.append(r_)
                    qh.append(h_)
                qc = jnp.concatenate(qn, axis=1).astype(BF)
                mask = mask_ref[first] if sb == 0 else mask_ref[0]
                p, p_sink = _attn_probs(kw, qc, bias_ref[hk], mask, sink_ref[hk])
                doc = jnp.concatenate([do_ref[(GRP * hk + g) * HD:(GRP * hk + g + 1) * HD, cols] for g in range(GRP)], axis=1)
                dp = _dot_tn(vw, doc)
                delta = jnp.sum(p * dp, axis=0, keepdims=True)
                ds = p * (dp - delta)
                dsink_ref[hk] += -(p_sink * delta)
                dsacc_ref[hk] += ds
                dsb = ds.astype(BF)
                dqc = _dot(kw, dsb) * QK_SCALE
                ckv_ref[sb, hk * HD:(hk + 1) * HD, :] = _dot_nt(qc, dsb) * QK_SCALE
                ckv_ref[sb, NKV * HD + hk * HD:NKV * HD + (hk + 1) * HD, :] = _dot_nt(doc, p.astype(BF))
                for g in range(GRP):
                    head = GRP * hk + g
                    dqn = dqc[:, g * BLK:(g + 1) * BLK]
                    dqh = dqn * qgain
                    dq = qr[g] * (dqh - qh[g] * jnp.mean(dqh * qh[g], axis=0, keepdims=True))
                    dq_ref[head * HD:(head + 1) * HD, cols] = dq.astype(BF)
                    dqg = dqg + dqn * qh[g]
        qg_scr[...] += dqg

        @pl.when(i == n_tiles - 1)
        def _():
            dqg_ref[...] = jnp.sum(qg_scr[...], axis=1, keepdims=True)

    return _call(
        body, deps, (qkv_t, qkv_t, do_t, qg, kg, sink_rows, bias_t, mask_t), grid=(n_tiles,),
        in_specs=_attn_specs(t, tq)[:2] + [pl.BlockSpec((D, tq), lambda i: (0, i))] + _attn_specs(t, tq)[2:],
        out_specs=[pl.BlockSpec((D, tq), lambda i: (0, i)),
                   pl.BlockSpec((n_sub, 2 * NKV * HD, 2 * BLK), lambda i: (i, 0, 0)),
                   pl.BlockSpec((HD, 1), lambda i: (0, 0)),
                   pl.BlockSpec((NKV, 1, GRP * BLK), lambda i: (0, 0, 0)),
                   pl.BlockSpec((NKV, 2 * BLK, GRP * BLK), lambda i: (0, 0, 0))],
        out_shape=[jax.ShapeDtypeStruct((D, t), BF),
                   jax.ShapeDtypeStruct((t // BLK, 2 * NKV * HD, 2 * BLK), F32),
                   jax.ShapeDtypeStruct((HD, 1), F32),
                   jax.ShapeDtypeStruct((NKV, 1, GRP * BLK), F32),
                   jax.ShapeDtypeStruct((NKV, 2 * BLK, GRP * BLK), F32)],
        scratch_shapes=[pltpu.VMEM((HD, BLK), F32)],
        compiler_params=_params(1), name="attn_bwd")


def _kv_combine(ckv, qkv_t, kg):
    nb = ckv.shape[0]
    t = nb * BLK
    rows = NKV * HD

    def body(c_ref, cn_ref, k_ref, kg_ref, o_ref, dkg_ref, kg_scr):
        n = pl.program_id(0)

        @pl.when(n == 0)
        def _():
            kg_scr[...] = jnp.zeros_like(kg_scr)

        has_next = (n < nb - 1).astype(F32)
        d = c_ref[0, :, BLK:] + cn_ref[0, :, :BLK] * has_next
        o_ref[rows:, :] = d[rows:, :].astype(BF)
        kgain = kg_ref[...]
        dkg = jnp.zeros((HD, BLK), F32)
        for h in range(NKV):
            _, r, kh = _norm_rows(k_ref[h * HD:(h + 1) * HD, :].astype(F32), kgain)
            dkn = d[h * HD:(h + 1) * HD, :]
            dkh = dkn * kgain
            o_ref[h * HD:(h + 1) * HD, :] = (r * (dkh - kh * jnp.mean(dkh * kh, axis=0, keepdims=True))).astype(BF)
            dkg = dkg + dkn * kh
        kg_scr[...] += dkg

        @pl.when(n == nb - 1)
        def _():
            dkg_ref[...] = jnp.sum(kg_scr[...], axis=1, keepdims=True)

    return pl.pallas_call(
        body, grid=(nb,),
        in_specs=[pl.BlockSpec((1, 2 * rows, 2 * BLK), lambda n: (n, 0, 0)),
                  pl.BlockSpec((1, 2 * rows, 2 * BLK), lambda n: (jnp.minimum(n + 1, nb - 1), 0, 0)),
                  pl.BlockSpec((rows, BLK), lambda n: (D // rows, n)),
                  _resident((HD, 1))],
        out_specs=[pl.BlockSpec((2 * rows, BLK), lambda n: (0, n)), pl.BlockSpec((HD, 1), lambda n: (0, 0))],
        out_shape=[jax.ShapeDtypeStruct((2 * rows, t), BF), jax.ShapeDtypeStruct((HD, 1), F32)],
        scratch_shapes=[pltpu.VMEM((HD, BLK), F32)],
        compiler_params=_params(1), name="kv_combine")(ckv, ckv, qkv_t, kg)


def _group_lane_sums(v):
    lane_group = lax.broadcasted_iota(jnp.int32, (1, GRP * BLK), 1) // BLK
    col = lax.broadcasted_iota(jnp.int32, (1, BLK), 1)
    out = jnp.zeros((NKV, BLK), F32)
    for g in range(GRP):
        s = jnp.sum(jnp.where(lane_group == g, v, 0.0), axis=1, keepdims=True)
        out = jnp.where(col == g, s, out)
    return out


def _bias_grad(dsacc, onehot_t):
    def body(ds_ref, oh_ref, o_ref):
        oh = jnp.concatenate([oh_ref[0]] * GRP, axis=1)
        o_ref[0] = _group_lane_sums(jnp.sum(ds_ref[...] * oh[None], axis=1))

    return pl.pallas_call(
        body, grid=(NBUCKET,),
        in_specs=[_resident((NKV, 2 * BLK, GRP * BLK)), pl.BlockSpec((1, 2 * BLK, BLK), lambda b: (b, 0, 0))],
        out_specs=pl.BlockSpec((1, NKV, BLK), lambda b: (b, 0, 0)),
        out_shape=jax.ShapeDtypeStruct((NBUCKET, NKV, BLK), F32),
        compiler_params=_params(1), name="bias_grad")(dsacc, onehot_t)


def _sink_grad(dsink_rows):
    def body(d_ref, o_ref):
        o_ref[...] = _group_lane_sums(d_ref[:, 0, :])

    return pl.pallas_call(body, out_shape=jax.ShapeDtypeStruct((NKV, BLK), F32), name="sink_grad")(dsink_rows)


def _mix_out(zs, o_t, gp, x, w_cp, w_o, w_out):
    t = x.shape[0]
    tm = min(256, t)

    def body(zs_ref, ot_ref, gp_ref, x_ref, wcp_ref, wo_ref, wout_ref, xo_ref, a_ref, b_ref, m_ref):
        a = _dot(zs_ref[...], wcp_ref[...])
        b = _dot_tn(ot_ref[...], wo_ref[...])
        a_ref[...] = a.astype(BF)
        b_ref[...] = b.astype(BF)
        merged = (_sig(gp_ref[:, :D].astype(F32)) * a + _sig(gp_ref[:, D:].astype(F32)) * b).astype(BF)
        m_ref[...] = merged
        xo_ref[...] = x_ref[...] + _dot(merged, wout_ref[...])

    return pl.pallas_call(
        body, grid=(t // tm,),
        in_specs=[_row_tile(tm, D), pl.BlockSpec((D, tm), lambda i: (0, i)), _row_tile(tm, 2 * D), _row_tile(tm, D),
                  _resident((D, D)), _resident((D, D)), _resident((D, D))],
        out_specs=[_row_tile(tm, D)] * 4,
        out_shape=[jax.ShapeDtypeStruct((t, D), F32)] + [jax.ShapeDtypeStruct((t, D), BF)] * 3,
        compiler_params=_params(1), name="mix_out")(zs, o_t, gp, x, w_cp, w_o, w_out)


def _mix_out_bwd(dx, a, b, gp, w_cp, w_o, w_out, deps=()):
    t = dx.shape[0]
    tm = min(256, t)

    def body(dx_ref, a_ref, b_ref, gp_ref, wcp_ref, wo_ref, wout_ref, dzs_ref, dot_ref, dgp_ref, da_ref, db_ref, dxb_ref):
        dxb = dx_ref[...].astype(BF)
        dxb_ref[...] = dxb
        dm = _dot_nt(dxb, wout_ref[...])
        gc = _sig(gp_ref[:, :D].astype(F32))
        ga = _sig(gp_ref[:, D:].astype(F32))
        da = (dm * gc).astype(BF)
        db = (dm * ga).astype(BF)
        da_ref[...] = da
        db_ref[...] = db
        dgp_ref[:, :D] = (dm * a_ref[...].astype(F32) * gc * (1.0 - gc)).astype(BF)
        dgp_ref[:, D:] = (dm * b_ref[...].astype(F32) * ga * (1.0 - ga)).astype(BF)
        dzs_ref[...] = _dot_nt(da, wcp_ref[...])
        dot_ref[...] = _dot_nt(wo_ref[...], db).astype(BF)

    return _call(
        body, deps, (dx, a, b, gp, w_cp, w_o, w_out), grid=(t // tm,),
        in_specs=[_row_tile(tm, D), _row_tile(tm, D), _row_tile(tm, D), _row_tile(tm, 2 * D),
                  _resident((D, D)), _resident((D, D)), _resident((D, D))],
        out_specs=[_row_tile(tm, D), pl.BlockSpec((D, tm), lambda i: (0, i)), _row_tile(tm, 2 * D),
                   _row_tile(tm, D), _row_tile(tm, D), _row_tile(tm, D)],
        out_shape=[jax.ShapeDtypeStruct((t, D), F32), jax.ShapeDtypeStruct((D, t), BF), jax.ShapeDtypeStruct((t, 2 * D), BF),
                   jax.ShapeDtypeStruct((t, D), BF), jax.ShapeDtypeStruct((t, D), BF), jax.ShapeDtypeStruct((t, D), BF)],
        compiler_params=_params(1), name="mix_out_bwd")


def _mix_proj_bwd(dxo, duc, dq_t, dkv_t, dgp, x, g, w_t):
    t = x.shape[0]
    tm = min(256, t)

    def body(dxo_ref, duc_ref, dq_ref, dkv_ref, dgp_ref, x_ref, g_ref, w_ref, dx_ref, dg_ref):
        dn = _dot(duc_ref[...], w_ref[R_CONV[0]:R_CONV[1], :])
        dn = dn + _dot(dgp_ref[...], w_ref[R_GATE[0]:R_GATE[1], :])
        dn = dn + _dot_tn(dq_ref[...], w_ref[R_Q[0]:R_Q[1], :])
        dn = dn + _dot_tn(dkv_ref[...], w_ref[R_KV[0]:R_KV[1], :])
        dx, dg = _rms_bwd(dn, x_ref[...], g_ref[...])
        dx_ref[...] = dxo_ref[...] + dx

        @pl.when(pl.program_id(0) == 0)
        def _():
            dg_ref[...] = jnp.zeros_like(dg_ref)

        dg_ref[...] += dg

    return pl.pallas_call(
        body, grid=(t // tm,),
        in_specs=[_row_tile(tm, D), _row_tile(tm, 2 * D), pl.BlockSpec((D, tm), lambda i: (0, i)),
                  pl.BlockSpec((2 * NKV * HD, tm), lambda i: (0, i)), _row_tile(tm, 2 * D), _row_tile(tm, D),
                  _resident((1, D)), _resident((INW, D))],
        out_specs=[_row_tile(tm, D), pl.BlockSpec((1, D), lambda i: (0, 0))],
        out_shape=[jax.ShapeDtypeStruct((t, D), F32), jax.ShapeDtypeStruct((1, D), F32)],
        compiler_params=_params(1), name="mix_proj_bwd")(dxo, duc, dq_t, dkv_t, dgp, x, g, w_t)


def _attention_tables():
    kj = np.arange(2 * BLK)[:, None]
    qi = np.arange(BLK)[None, :]
    dist = qi + BLK - kj
    in_win = (dist >= 0) & (dist < BLK)
    dpos = np.maximum(dist, 0)
    max_exact = NBUCKET // 2
    dfl = np.maximum(dpos, 1).astype(np.float32)
    large = max_exact + (np.log(dfl / np.float32(max_exact)) / np.float32(math.log(BLK / max_exact))
                         * np.float32(NBUCKET - max_exact)).astype(np.int32)
    large = np.minimum(large, NBUCKET - 1)
    bucket = np.where(dpos < max_exact, dpos, large)
    onehot = (bucket[None] == np.arange(NBUCKET)[:, None, None]).astype(np.float32)
    mask = in_win.astype(np.float32)
    mask_first = mask * (kj >= BLK)
    masks = np.stack([np.tile(mask, (1, GRP)), np.tile(mask_first, (1, GRP))])
    return onehot, masks


def _bias_table(rel_bias, onehot):
    tab = jnp.einsum("bkq,bh->hkq", onehot, rel_bias, precision=lax.Precision.HIGHEST)
    tab = tab.reshape(NKV, GRP, 2 * BLK, BLK)
    return jnp.transpose(tab, (0, 2, 1, 3)).reshape(NKV, 2 * BLK, GRP * BLK)


def _local_step(x, target, vec, weights_of, grads_done):
    onehot_np, masks_np = _attention_tables()
    onehot = jnp.asarray(onehot_np)
    masks = jnp.asarray(masks_np)
    bias_t = _bias_table(vec["rel_bias"], onehot)
    sink_rows = jnp.repeat(vec["attn_sinks"].reshape(NKV, 1, GRP), BLK, axis=2)
    qg = vec["q_norm"].reshape(HD, 1)
    kg = vec["k_norm"].reshape(HD, 1)
    g1 = vec["ffn1_norm"].reshape(1, D)
    gm = vec["mix_norm"].reshape(1, D)
    g2 = vec["ffn2_norm"].reshape(1, D)
    dwb = vec["conv_dw_bias"].reshape(1, D)
    lng = vec["conv_ln_g"].reshape(1, D)
    lnb = vec["conv_ln_b"].reshape(1, D)

    w1 = weights_of("ffn1", x)
    n1, u1, x1 = _ffn_fwd(x, g1, w1["ffn1_w_in"], w1["ffn1_w_out"], "ffn1_fwd")
    wm = weights_of("mix", x1)
    dwk = jnp.pad(wm["conv_dw_kernel"], ((0, CWP - CW), (0, 0)))
    hm, uc, gp, qkv_t = _mix_proj(x1, gm, wm["w_in"])
    zs, zc = _conv_fwd(uc, dwk, dwb, lng, lnb)
    o_t = _attn_fwd(qkv_t, qg, kg, sink_rows, bias_t, masks)
    x2, a, b, merged = _mix_out(zs, o_t, gp, x1, wm["conv_w_proj"], wm["attn_w_o"], wm["w_out"])
    w2 = weights_of("ffn2", x2)
    n2, u2, dx3, sq = _ffn_fwd(x2, g2, w2["ffn2_w_in"], w2["ffn2_w_out"], "ffn2_fwd", target=target)

    gv = {}
    dx2, du2, h2, dy2, gv["ffn2_norm"] = _ffn_bwd(dx3, x2, g2, u2, w2["ffn2_w_in"], w2["ffn2_w_out"], "ffn2_bwd")
    deps = grads_done("ffn2", {"ffn2_w_in": _wgrad(du2, n2, "ffn2_dw_in", lhs_is_transposed=False, chunk=1408),
                               "ffn2_w_out": _wgrad(h2, dy2, "ffn2_dw_out", lhs_is_transposed=False, chunk=1408)})

    dzs, do_t, dgp, da, db, dx2b = _mix_out_bwd(dx2, a, b, gp, wm["conv_w_proj"], wm["attn_w_o"], wm["w_out"], deps=deps)
    deps = grads_done("mix_out", {"w_out": _wgrad(merged, dx2b, "mix_dw_out", lhs_is_transposed=False, chunk=1024),
                                  "conv_w_proj": _wgrad(zs, da, "mix_dw_cp", lhs_is_transposed=False, chunk=1024),
                                  "attn_w_o": _wgrad(o_t, db, "mix_dw_o", lhs_is_transposed=True, chunk=1024)})

    dq_t, ckv, dqg, dsink_rows, dsacc = _attn_bwd(qkv_t, do_t, qg, kg, sink_rows, bias_t, masks, deps=deps)
    dkv_t, dkg = _kv_combine(ckv, qkv_t, kg)
    gv["q_norm"] = dqg.reshape(HD)
    gv["k_norm"] = dkg.reshape(HD)
    gv["attn_sinks"] = _sink_grad(dsink_rows)[:, :GRP].reshape(NQ)
    gv["rel_bias"] = _bias_grad(dsacc, onehot)[:, :, :GRP].reshape(NBUCKET, NQ)

    duc, dk_conv, gv["conv_dw_bias"], gv["conv_ln_g"], gv["conv_ln_b"] = _conv_bwd(uc, zc, dzs, dwk, lng, lnb)
    gv["conv_dw_kernel"] = dk_conv[:CW]

    dx1, gv["mix_norm"] = _mix_proj_bwd(dx2, duc, dq_t, dkv_t, dgp, x1, gm, wm["w_in"])
    deps = grads_done("mix_in", {"w_in": _wgrad_mix(duc, dq_t, dkv_t, dgp, hm)})

    dx0, du1, h1, dy1, gv["ffn1_norm"] = _ffn_bwd(dx1, x, g1, u1, w1["ffn1_w_in"], w1["ffn1_w_out"], "ffn1_bwd", deps=deps)
    deps = grads_done("ffn1_out", {"ffn1_w_out": _wgrad(h1, dy1, "ffn1_dw_out", lhs_is_transposed=False, chunk=1408)})
    grads_done("ffn1_in", {"ffn1_w_in": _wgrad(du1, n1, "ffn1_dw_in", lhs_is_transposed=False, chunk=1408, deps=deps)})
    for k in ("ffn1_norm", "mix_norm", "ffn2_norm", "conv_dw_bias", "conv_ln_g", "conv_ln_b"):
        gv[k] = gv[k].reshape(D)
    return sq, dx0, gv


MESH_ID = pl.DeviceIdType.MESH


def _position():
    return lax.axis_index("x"), lax.axis_index("y"), lax.axis_index("c")


def _shard_rows(ref, index, rows):
    return ref.at[pl.ds(pl.multiple_of(index * rows, 16), rows), :]


def _prep(weights, transposed, taps, me):
    n = len(weights)

    def body(me_ref, *refs):
        for k in range(n):
            w = refs[k][...]
            refs[n + 1 + k][...] = (w.T if transposed[k] else w).astype(BF)
        refs[2 * n + 1][0:CW, :] = refs[n][...]
        refs[2 * n + 1][CW:, :] = jnp.zeros((CWP - CW, BLK), F32)

    shard_shapes = [w.shape[::-1] if tr else w.shape for w, tr in zip(weights, transposed)] + [(CWP, BLK)]
    dtypes = [BF] * n + [F32]
    ins = list(weights) + [taps]
    return pl.pallas_call(
        body,
        grid_spec=pltpu.PrefetchScalarGridSpec(
            num_scalar_prefetch=1, grid=(1,),
            in_specs=[pl.BlockSpec(a.shape, lambda i, m: (0, 0), pipeline_mode=pl.Buffered(1)) for a in ins],
            out_specs=[pl.BlockSpec(s, lambda i, m: (m[0], 0)) for s in shard_shapes]),
        out_shape=[jax.ShapeDtypeStruct((N_DEV * s[0], s[1]), d) for s, d in zip(shard_shapes, dtypes)],
        compiler_params=_params(1), name="prep")(me, *ins)


HBM = pl.BlockSpec(memory_space=pltpu.HBM)
SEM = pl.BlockSpec(memory_space=pltpu.SEMAPHORE)
DATAFLOW = pltpu.SideEffectType.DATAFLOW_SIDE_EFFECTING
TOKEN = jax.ShapeDtypeStruct((8, 128), F32)


def _in_hbm(x):
    return pltpu.with_memory_space_constraint(x, pltpu.HBM)


def _hbm_like(arrays):
    return [pltpu.HBM(a.shape, a.dtype) for a in arrays]


def _other_chips(x, y):
    return [(1 - x, y), (x, 1 - y), (1 - x, 1 - y)]


def _device_index(chip, c):
    return 4 * chip[0] + 2 * chip[1] + c


def _chip_index(chip):
    return 2 * chip[0] + chip[1]


class _Exchange:
    def __init__(self, gather):
        self.gather = gather

    def sent(self, x, y, c, chip):
        return _device_index((x, y), c) if self.gather else _chip_index(chip)

    def lands_at(self, x, y, c):
        return _device_index((x, y), c) if self.gather else _chip_index((x, y))

    def arrives_at(self, chip, c):
        return _device_index(chip, c) if self.gather else _chip_index(chip)


def _ici_copies_start(sets, sources, landings, exchange, name, deps=()):
    n = len(landings)
    arrays = (list(sources) if sources is not None else []) + list(landings)
    first_land = len(arrays) - n
    n_sets = len(sets)
    n_deps = len(deps)

    def body(*refs):
        refs = refs[n_deps:]
        src, land = refs[:n], refs[first_land:first_land + n]
        sems = refs[len(arrays):len(arrays) + 2 * n_sets]
        token = refs[-1]
        x, y, c = _position()
        for s, members in enumerate(sets):
            for slot, (k, rows) in enumerate(members):
                for j, chip in enumerate(_other_chips(x, y)):
                    pltpu.make_async_remote_copy(
                        src_ref=_shard_rows(src[k], exchange.sent(x, y, c, chip), rows),
                        dst_ref=_shard_rows(land[k], exchange.lands_at(x, y, c), rows),
                        send_sem=sems[2 * s].at[3 * slot + j], recv_sem=sems[2 * s + 1].at[3 * slot + j],
                        device_id=(*chip, c), device_id_type=MESH_ID).start()
        token[...] = jnp.zeros_like(token)

    sem_shapes = []
    for members in sets:
        sem_shapes += [pltpu.SemaphoreType.DMA((3 * len(members),))] * 2
    out = pl.pallas_call(
        body, name=name,
        out_shape=sem_shapes + _hbm_like(arrays) + [TOKEN],
        in_specs=[ANY] * n_deps + [HBM] * len(arrays),
        out_specs=[SEM] * (2 * n_sets) + [HBM] * len(arrays) + [pl.BlockSpec(memory_space=pltpu.VMEM)],
        input_output_aliases={n_deps + i: 2 * n_sets + i for i in range(len(arrays))},
        compiler_params=pltpu.CompilerParams(has_side_effects=DATAFLOW),
    )(*deps, *[_in_hbm(a) for a in arrays])
    sems = [(out[2 * s], out[2 * s + 1]) for s in range(n_sets)]
    thru = list(out[2 * n_sets:2 * n_sets + len(arrays)])
    return sems, (thru[:first_land] if sources is not None else None), thru[first_land:], out[-1]


def _ici_copies_wait(sems, members, sources, landings, exchange, after, name):
    n = len(landings)
    arrays = (list(sources) if sources is not None else []) + list(landings)
    first_land = len(arrays) - n

    def body(*refs):
        src, land = refs[:n], refs[first_land:first_land + n]
        send_sems, recv_sems = refs[len(arrays)], refs[len(arrays) + 1]
        x, y, c = _position()
        for slot, rows in enumerate(members):
            for j, chip in enumerate(_other_chips(x, y)):
                cp = pltpu.make_async_remote_copy(
                    src_ref=_shard_rows(src[slot], exchange.sent(x, y, c, chip), rows),
                    dst_ref=_shard_rows(land[slot], exchange.arrives_at(chip, c), rows),
                    send_sem=send_sems.at[3 * slot + j], recv_sem=recv_sems.at[3 * slot + j],
                    device_id=(*chip, c), device_id_type=MESH_ID)
                cp.wait_send()
                cp.wait_recv()

    out = pl.pallas_call(
        body, name=name, out_shape=_hbm_like(arrays),
        in_specs=[HBM] * len(arrays) + [SEM, SEM, ANY], out_specs=[HBM] * len(arrays),
        input_output_aliases={i: i for i in range(len(arrays))},
        compiler_params=pltpu.CompilerParams(has_side_effects=DATAFLOW),
    )(*arrays, sems[0], sems[1], after)
    return list(out[first_land:])


def _d2d_gather(buffers, rows, name):
    n = len(buffers)

    def body(*refs):
        land = refs[n:2 * n]
        send_sems, recv_sems = refs[2 * n:]
        x, y, c = _position()
        chips = [(x, y)] + _other_chips(x, y)
        sends, recvs = [], []
        for k in range(n):
            for j, chip in enumerate(chips):
                for copies, core in ((sends, c), (recvs, 1 - c)):
                    block = _shard_rows(land[k], _device_index(chip, core), rows[k])
                    copies.append(pltpu.make_async_remote_copy(
                        src_ref=block, dst_ref=block, send_sem=send_sems.at[k, j], recv_sem=recv_sems.at[k, j],
                        device_id=(x, y, 1 - c), device_id_type=MESH_ID))
        for cp in sends:
            cp.start()
        for cp in recvs:
            cp.wait_recv()
        for cp in sends:
            cp.wait_send()

    return pl.pallas_call(
        body, name=name, out_shape=[jax.ShapeDtypeStruct(a.shape, a.dtype) for a in buffers],
        in_specs=[ANY] * n, out_specs=[ANY] * n, input_output_aliases={i: i for i in range(n)},
        scratch_shapes=[pltpu.SemaphoreType.DMA((n, 4)), pltpu.SemaphoreType.DMA((n, 4))],
    )(*buffers)


def _rs_pair(grads, name):
    n = len(grads)
    rows = [g.shape[0] // N_DEV for g in grads]

    def body(*refs):
        ins, outs = refs[:n], refs[n:2 * n]
        send_sems, recv_sems = refs[2 * n:]
        x, y, c = _position()
        copies = []
        for k in range(n):
            for q in range(4):
                copies.append(pltpu.make_async_remote_copy(
                    src_ref=_shard_rows(ins[k], 2 * q + 1 - c, rows[k]), dst_ref=_shard_rows(outs[k], q, rows[k]),
                    send_sem=send_sems.at[k, q], recv_sem=recv_sems.at[k, q], device_id=(x, y, 1 - c),
                    device_id_type=MESH_ID))
        for cp in copies:
            cp.start()
        for cp in copies:
            cp.wait()

    return pl.pallas_call(
        body, out_shape=[jax.ShapeDtypeStruct((4 * r, g.shape[1]), g.dtype) for g, r in zip(grads, rows)],
        in_specs=[ANY] * n, out_specs=[ANY] * n,
        scratch_shapes=[pltpu.SemaphoreType.DMA((n, 4)), pltpu.SemaphoreType.DMA((n, 4))],
        name=name)(*grads)


def _pair_add(grad, received, place, name):
    r = received.shape[0] // 4
    tr = 352 if r % 352 == 0 else r
    per = r // tr

    def body(place_ref, g_ref, r_ref, o_ref, land_ref):
        total = (g_ref[...].astype(F32) + r_ref[...].astype(F32)).astype(BF)
        o_ref[...] = total

        @pl.when(pl.program_id(1) == place_ref[1])
        def _():
            land_ref[...] = total

    return pl.pallas_call(
        body,
        grid_spec=pltpu.PrefetchScalarGridSpec(
            num_scalar_prefetch=1, grid=(per, 4),
            in_specs=[pl.BlockSpec((tr, D), lambda i, q, p: ((2 * q + p[0]) * per + i, 0)),
                      pl.BlockSpec((tr, D), lambda i, q, p: (q * per + i, 0))],
            out_specs=[pl.BlockSpec((tr, D), lambda i, q, p: (q * per + i, 0)),
                       pl.BlockSpec((tr, D), lambda i, q, p: (p[1] * per + i, 0))]),
        out_shape=[jax.ShapeDtypeStruct(received.shape, BF)] * 2,
        compiler_params=_params(2), name=name)(place, grad, received)


def _all_reduce_small(payload, deps=()):
    r = payload.shape[0]

    def body(in_ref, out_ref, land_ref, send_sems, recv_sems):
        x, y, c = _position()
        me = 4 * x + 2 * y + c
        land_ref[me] = in_ref[...]
        copies = []
        for k in range(1, N_DEV):
            peer = (x ^ (k >> 2), y ^ ((k >> 1) & 1), c ^ (k & 1))
            copies.append(pltpu.make_async_remote_copy(
                src_ref=in_ref, dst_ref=land_ref.at[me], send_sem=send_sems.at[k - 1], recv_sem=recv_sems.at[k - 1],
                device_id=peer, device_id_type=MESH_ID))
        for cp in copies:
            cp.start()
        for k in range(1, N_DEV):
            peer_index = me ^ k
            pltpu.make_async_remote_copy(
                src_ref=in_ref, dst_ref=land_ref.at[peer_index], send_sem=send_sems.at[k - 1], recv_sem=recv_sems.at[k - 1],
                device_id=(x, y, c), device_id_type=MESH_ID).wait_recv()
        for cp in copies:
            cp.wait_send()
        acc = land_ref[0]
        for d in range(1, N_DEV):
            acc = acc + land_ref[d]
        out_ref[...] = acc

    return _call(
        body, deps, (payload,), out_shape=jax.ShapeDtypeStruct((r, D), F32),
        in_specs=[pl.BlockSpec(memory_space=pltpu.VMEM)], out_specs=pl.BlockSpec(memory_space=pltpu.VMEM),
        scratch_shapes=[pltpu.VMEM((N_DEV, r, D), F32), pltpu.SemaphoreType.DMA((N_DEV - 1,)),
                        pltpu.SemaphoreType.DMA((N_DEV - 1,))],
        name="all_reduce_small")


def _adamw_math(w, g, m, v):
    m = ADAM_B1 * m + (1.0 - ADAM_B1) * g
    v = ADAM_B2 * v + (1.0 - ADAM_B2) * (g * g)
    m_hat = m / (1.0 - ADAM_B1 ** ADAM_STEP)
    v_hat = v / (1.0 - ADAM_B2 ** ADAM_STEP)
    delta = -ADAM_LR * (m_hat / (jnp.sqrt(v_hat) + ADAM_EPS) + ADAM_WD * w)
    return delta, m, v


def _sum_partials(blocks):
    g = blocks[0].astype(F32)
    for blk in blocks[1:]:
        g = g + blk.astype(F32)
    return g


def _reduce_adamw(landed, w, m, v, name):
    r = w.shape[0]
    tr = 176 if r % 176 == 0 else r
    per = r // tr

    def body(r0, r1, r2, r3, w_ref, m_ref, v_ref, g_ref, d_ref, nm_ref, nv_ref):
        g = _sum_partials([r0[...], r1[...], r2[...], r3[...]])
        g_ref[...] = g
        d_ref[...], nm_ref[...], nv_ref[...] = _adamw_math(w_ref[...], g, m_ref[...], v_ref[...])

    tile = _row_tile(tr, D)
    return pl.pallas_call(
        body, grid=(per,),
        in_specs=[pl.BlockSpec((tr, D), lambda i, q=q: (q * per + i, 0)) for q in range(4)] + [tile] * 3,
        out_specs=[tile] * 4, out_shape=[jax.ShapeDtypeStruct(w.shape, F32)] * 4,
        compiler_params=_params(1), name=name)(landed, landed, landed, landed, w, m, v)


def _reduce_adamw_t(landed, w, m, v, name):
    r = w.shape[1]

    def body(land_hbm, w_hbm, m_hbm, v_hbm, g_hbm, d_hbm, nm_hbm, nv_hbm, land, w_v, m_v, v_v, g_v, d_v, nm_v, nv_v, sems):
        loads = [pltpu.make_async_copy(s, d, sems.at[i])
                 for i, (s, d) in enumerate(((land_hbm, land), (w_hbm, w_v), (m_hbm, m_v), (v_hbm, v_v)))]
        for cp in loads:
            cp.start()
        for cp in loads:
            cp.wait()
        g = _sum_partials([land[q * r:(q + 1) * r, :] for q in range(4)]).T
        g_v[...] = g
        d_v[...], nm_v[...], nv_v[...] = _adamw_math(w_v[...], g, m_v[...], v_v[...])
        stores = [pltpu.make_async_copy(s, d, sems.at[4 + i])
                  for i, (s, d) in enumerate(((g_v, g_hbm), (d_v, d_hbm), (nm_v, nm_hbm), (nv_v, nv_hbm)))]
        for cp in stores:
            cp.start()
        for cp in stores:
            cp.wait()

    return pl.pallas_call(
        body, in_specs=[ANY] * 4, out_specs=[ANY] * 4, out_shape=[jax.ShapeDtypeStruct(w.shape, F32)] * 4,
        scratch_shapes=[pltpu.VMEM(landed.shape, BF)] + [pltpu.VMEM(w.shape, F32)] * 7 + [pltpu.SemaphoreType.DMA((8,))],
        compiler_params=pltpu.CompilerParams(vmem_limit_bytes=VMEM_LIMIT_V7X), name=name)(landed, w, m, v)


def _adamw_small(w, g, m, v, name):
    def body(w_ref, g_ref, m_ref, v_ref, d_ref, nm_ref, nv_ref):
        d_ref[...], nm_ref[...], nv_ref[...] = _adamw_math(w_ref[...], g_ref[...], m_ref[...], v_ref[...])

    return pl.pallas_call(body, out_shape=[jax.ShapeDtypeStruct(w.shape, F32)] * 3, name=name)(w, g, m, v)


WEIGHTS = ("ffn1_norm", "ffn1_w_in", "ffn1_w_out", "mix_norm", "w_in", "conv_dw_kernel", "conv_dw_bias", "conv_ln_g",
           "conv_ln_b", "conv_w_proj", "q_norm", "k_norm", "attn_sinks", "rel_bias", "attn_w_o", "w_out", "ffn2_norm",
           "ffn2_w_in", "ffn2_w_out")
MATRICES = ("ffn1_w_in", "ffn1_w_out", "w_in", "conv_w_proj", "attn_w_o", "w_out", "ffn2_w_in", "ffn2_w_out")
COLUMN_SHARDED = ("ffn1_w_in", "w_in", "ffn2_w_in")
ROW_VECTORS = ("ffn1_norm", "mix_norm", "conv_dw_bias", "conv_ln_g", "conv_ln_b", "ffn2_norm")
PACKED = (("q_norm", HD), ("k_norm", HD), ("attn_sinks", NQ), ("rel_bias", NBUCKET * NQ))
GATHER = _Exchange(gather=True)
SCATTER = _Exchange(gather=False)
GATHER_STAGES = ("ffn1", "mix", "ffn2")
STAGE_MEMBERS = {"ffn1": ("ffn1_w_in", "ffn1_w_out"), "mix": ("w_in", "conv_w_proj", "attn_w_o", "w_out", "taps"),
                 "ffn2": ("ffn2_w_in", "ffn2_w_out")}
ROW_PACKED = len(ROW_VECTORS)
ROW_LOSS = ROW_PACKED + 1
ROW_TAPS = 8
PAYLOAD_ROWS = ROW_TAPS + CWP


def _pack_small(values, last_row):
    packed = jnp.concatenate([values[k].reshape(-1) for k, _ in PACKED])
    packed = jnp.pad(packed, (0, D - packed.shape[0])).reshape(1, D)
    return jnp.concatenate([values[k].reshape(1, D) for k in ROW_VECTORS] + [packed, last_row], axis=0)


def _unpack_small(rows):
    out = {k: rows[i] for i, k in enumerate(ROW_VECTORS)}
    at = 0
    for k, size in PACKED:
        out[k] = rows[ROW_PACKED, at:at + size]
        at += size
    out["rel_bias"] = out["rel_bias"].reshape(NBUCKET, NQ)
    return out


def kernel(x, ffn1_norm, ffn1_w_in, ffn1_w_out, mix_norm, w_in, conv_dw_kernel, conv_dw_bias, conv_ln_g, conv_ln_b, conv_w_proj, q_norm, k_norm, attn_sinks, rel_bias, attn_w_o, w_out, ffn2_norm, ffn2_w_in, ffn2_w_out, loss_target, m_ffn1_norm, m_ffn1_w_in, m_ffn1_w_out, m_mix_norm, m_w_in, m_conv_dw_kernel, m_conv_dw_bias, m_conv_ln_g, m_conv_ln_b, m_conv_w_proj, m_q_norm, m_k_norm, m_attn_sinks, m_rel_bias, m_attn_w_o, m_w_out, m_ffn2_norm, m_ffn2_w_in, m_ffn2_w_out, v_ffn1_norm, v_ffn1_w_in, v_ffn1_w_out, v_mix_norm, v_w_in, v_conv_dw_kernel, v_conv_dw_bias, v_conv_ln_g, v_conv_ln_b, v_conv_w_proj, v_q_norm, v_k_norm, v_attn_sinks, v_rel_bias, v_attn_w_o, v_w_out, v_ffn2_norm, v_ffn2_w_in, v_ffn2_w_out):
    w = dict(ffn1_norm=ffn1_norm, ffn1_w_in=ffn1_w_in, ffn1_w_out=ffn1_w_out, mix_norm=mix_norm, w_in=w_in,
             conv_dw_kernel=conv_dw_kernel, conv_dw_bias=conv_dw_bias, conv_ln_g=conv_ln_g, conv_ln_b=conv_ln_b,
             conv_w_proj=conv_w_proj, q_norm=q_norm, k_norm=k_norm, attn_sinks=attn_sinks, rel_bias=rel_bias,
             attn_w_o=attn_w_o, w_out=w_out, ffn2_norm=ffn2_norm, ffn2_w_in=ffn2_w_in, ffn2_w_out=ffn2_w_out)
    m = dict(ffn1_norm=m_ffn1_norm, ffn1_w_in=m_ffn1_w_in, ffn1_w_out=m_ffn1_w_out, mix_norm=m_mix_norm, w_in=m_w_in,
             conv_dw_kernel=m_conv_dw_kernel, conv_dw_bias=m_conv_dw_bias, conv_ln_g=m_conv_ln_g, conv_ln_b=m_conv_ln_b,
             conv_w_proj=m_conv_w_proj, q_norm=m_q_norm, k_norm=m_k_norm, attn_sinks=m_attn_sinks, rel_bias=m_rel_bias,
             attn_w_o=m_attn_w_o, w_out=m_w_out, ffn2_norm=m_ffn2_norm, ffn2_w_in=m_ffn2_w_in, ffn2_w_out=m_ffn2_w_out)
    v = dict(ffn1_norm=v_ffn1_norm, ffn1_w_in=v_ffn1_w_in, ffn1_w_out=v_ffn1_w_out, mix_norm=v_mix_norm, w_in=v_w_in,
             conv_dw_kernel=v_conv_dw_kernel, conv_dw_bias=v_conv_dw_bias, conv_ln_g=v_conv_ln_g, conv_ln_b=v_conv_ln_b,
             conv_w_proj=v_conv_w_proj, q_norm=v_q_norm, k_norm=v_k_norm, attn_sinks=v_attn_sinks, rel_bias=v_rel_bias,
             attn_w_o=v_attn_w_o, w_out=v_w_out, ffn2_norm=v_ffn2_norm, ffn2_w_in=v_ffn2_w_in, ffn2_w_out=v_ffn2_w_out)
    px, py, pc = _position()
    me = 4 * px + 2 * py + pc
    place = jnp.stack([pc, 2 * px + py]).astype(jnp.int32)

    buffers = dict(zip(MATRICES + ("taps",), _prep([w[k] for k in MATRICES], [k in COLUMN_SHARDED for k in MATRICES],
                                                   conv_dw_kernel, me.astype(jnp.int32).reshape(1))))
    landings, sets = [], []
    for stage in GATHER_STAGES:
        sets.append([(len(landings) + i, buffers[k].shape[0] // N_DEV) for i, k in enumerate(STAGE_MEMBERS[stage])])
        landings += [buffers[k] for k in STAGE_MEMBERS[stage]]
    sems, _, land_thru, _ = _ici_copies_start(sets, None, landings, GATHER, "gather_start")

    def weights_of(stage, after):
        s = GATHER_STAGES.index(stage)
        rows = [r for _, r in sets[s]]
        landed = _ici_copies_wait(sems[s], rows, None, [land_thru[k] for k, _ in sets[s]], GATHER, after,
                                  "gather_wait_" + stage)
        out = dict(zip(STAGE_MEMBERS[stage], _d2d_gather(landed, rows, "gather_d2d_" + stage)))
        if "taps" in out:
            taps = out.pop("taps")
            out["conv_dw_kernel"] = jnp.transpose(taps.reshape(N_DEV, CWP, BLK), (1, 0, 2)).reshape(CWP, D)[:CW]
        return out

    in_flight = []

    def grads_done(stage, grads):
        names = list(grads)
        received = _rs_pair([grads[k] for k in names], "rs_pair_" + stage)
        added = [_pair_add(grads[k], r, place, "pair_add_" + k) for k, r in zip(names, received)]
        partials = [p for p, _ in added]
        members = [(i, p.shape[0] // 4) for i, p in enumerate(partials)]
        sem, p_thru, l_thru, token = _ici_copies_start([members], partials, [l for _, l in added], SCATTER,
                                                       "scatter_start_" + stage)
        in_flight.append((stage, names, sem[0], p_thru, l_thru, token))
        return [token]

    vec = {k: w[k] for k in WEIGHTS if k not in MATRICES and k != "conv_dw_kernel"}
    sq, dx0, gv = _local_step(x[0], loss_target[0], vec, weights_of, grads_done)

    payload = jnp.concatenate([_pack_small(gv, sq), jnp.pad(gv["conv_dw_kernel"], ((0, CWP - CW), (0, 0)))], axis=0)
    total = _all_reduce_small(payload, deps=[in_flight[-1][-1]])
    loss = (0.5 / D) * jnp.sum(total[ROW_LOSS])

    grads, delta, new_m, new_v = {}, {}, {}, {}
    after = total
    for stage, names, sem, p_thru, l_thru, _ in in_flight:
        landed = _ici_copies_wait(sem, [p.shape[0] // 4 for p in p_thru], p_thru, l_thru, SCATTER, after,
                                  "scatter_wait_" + stage)
        for k, buf in zip(names, landed):
            update = _reduce_adamw_t if k in COLUMN_SHARDED else _reduce_adamw
            grads[k], delta[k], new_m[k], new_v[k] = update(buf, w[k], m[k], v[k], "adamw_" + k)
            after = delta[k]
    zero_row = jnp.zeros((1, D), F32)
    d8, m8, v8 = _adamw_small(_pack_small(w, zero_row), total[:ROW_TAPS], _pack_small(m, zero_row),
                              _pack_small(v, zero_row), "adamw_small")
    grads.update(_unpack_small(total[:ROW_TAPS]))
    delta.update(_unpack_small(d8))
    new_m.update(_unpack_small(m8))
    new_v.update(_unpack_small(v8))
    k = "conv_dw_kernel"
    grads[k] = lax.dynamic_slice_in_dim(total[ROW_TAPS:ROW_TAPS + CW], me * BLK, BLK, axis=1)
    delta[k], new_m[k], new_v[k] = _adamw_small(w[k], grads[k], m[k], v[k], "adamw_taps")

    return (loss, dx0[None], *[grads[k] for k in WEIGHTS], *[delta[k] for k in WEIGHTS],
            *[new_m[k] for k in WEIGHTS], *[new_v[k] for k in WEIGHTS])
```

```python
import functools
import math

import numpy as np
import jax
import jax.numpy as jnp
from jax import lax
from jax.experimental import pallas as pl
from jax.experimental.pallas import tpu as pltpu

F32 = jnp.float32
BF = jnp.bfloat16

D = 1024
F = 2816
INW = 5632
CW = 31
CWP = 32
HD = 64
NQ = 16
NKV = 4
GRP = NQ // NKV
BLK = 128
NBUCKET = 32
EPS = 1e-6
NEG = float(jnp.finfo(jnp.float32).min)
QK_SCALE = 1.0 / math.sqrt(HD)
R_CONV = (0, 2048)
R_QKV = (2048, 3584)
R_Q = (2048, 3072)
R_KV = (3072, 3584)
R_GATE = (3584, 5632)

N_DEV = 8
VMEM_LIMIT_V7X = 56 * 1024 * 1024

ADAM_LR = 0.001
ADAM_B1 = 0.9
ADAM_B2 = 0.999
ADAM_EPS = 1e-08
ADAM_WD = 0.01
ADAM_STEP = 10

NT_DIMS = (((1,), (1,)), ((), ()))
TN_DIMS = (((0,), (0,)), ((), ()))


def _dot(a, b):
    return jnp.dot(a, b, preferred_element_type=F32)


def _dot_nt(a, b):
    return lax.dot_general(a, b, NT_DIMS, preferred_element_type=F32)


def _dot_tn(a, b):
    return lax.dot_general(a, b, TN_DIMS, preferred_element_type=F32)


def _sig(x):
    return 1.0 / (1.0 + jnp.exp(-x))


ANY = pl.BlockSpec(memory_space=pl.ANY)


def _call(body, deps, args, **kw):
    n = len(deps)
    if n:
        kw["in_specs"] = [ANY] * n + list(kw["in_specs"])
        return pl.pallas_call(lambda *refs: body(*refs[n:]), **kw)(*deps, *args)
    return pl.pallas_call(body, **kw)(*args)


def _params(n_axes):
    return pltpu.CompilerParams(dimension_semantics=("arbitrary",) * n_axes, vmem_limit_bytes=VMEM_LIMIT_V7X)


def _resident(shape):
    zeros = (0,) * len(shape)
    return pl.BlockSpec(shape, lambda *_: zeros, pipeline_mode=pl.Buffered(1))


def _row_tile(rows, cols):
    return pl.BlockSpec((rows, cols), lambda i: (i, 0))


def _rms_stats(x):
    r = lax.rsqrt(jnp.mean(x * x, axis=-1, keepdims=True) + EPS)
    return r, x * r


def _rms_bwd(dn, x, g):
    r, xh = _rms_stats(x)
    dxh = dn * g
    dx = r * (dxh - xh * jnp.mean(dxh * xh, axis=-1, keepdims=True))
    return dx, jnp.sum(dn * xh, axis=0, keepdims=True)


def _ffn_fwd(x, g, w_in_t, w_out, name, target=None):
    t = x.shape[0]
    tm = min(256, t)
    with_loss = target is not None

    def body(*refs):
        if with_loss:
            x_ref, g_ref, w_ref, wo_ref, t_ref, n_ref, u_ref, dy_ref, sq_ref = refs
        else:
            x_ref, g_ref, w_ref, wo_ref, n_ref, u_ref, xo_ref = refs
        x = x_ref[...]
        r, xh = _rms_stats(x)
        n = (xh * g_ref[...]).astype(BF)
        n_ref[...] = n
        u = _dot_nt(n, w_ref[...])
        u_ref[...] = u.astype(BF)
        a = u[:, :F]
        b = u[:, F:]
        h = (a * _sig(a) * b).astype(BF)
        xo = x + 0.5 * _dot(h, wo_ref[...])
        if with_loss:
            err = xo - t_ref[...]
            dy_ref[...] = err * (1.0 / D)

            @pl.when(pl.program_id(0) == 0)
            def _():
                sq_ref[...] = jnp.zeros_like(sq_ref)

            sq_ref[...] += jnp.sum(err * err, axis=0, keepdims=True)
        else:
            xo_ref[...] = xo

    in_specs = [_row_tile(tm, D), _resident((1, D)), _resident((INW, D)), _resident((F, D))]
    args = [x, g, w_in_t, w_out]
    out_specs = [_row_tile(tm, D), _row_tile(tm, INW), _row_tile(tm, D)]
    out_shape = [jax.ShapeDtypeStruct((t, D), BF), jax.ShapeDtypeStruct((t, INW), BF), jax.ShapeDtypeStruct((t, D), F32)]
    if with_loss:
        in_specs.append(_row_tile(tm, D))
        args.append(target)
        out_specs.append(pl.BlockSpec((1, D), lambda i: (0, 0)))
        out_shape.append(jax.ShapeDtypeStruct((1, D), F32))
    return pl.pallas_call(body, grid=(t // tm,), in_specs=in_specs, out_specs=out_specs, out_shape=out_shape,
                          compiler_params=_params(1), name=name)(*args)


def _ffn_bwd(dxo, x, g, u, w_in_t, w_out, name, deps=()):
    t = x.shape[0]
    tm = min(256, t)

    def body(dxo_ref, x_ref, g_ref, u_ref, w_ref, wo_ref, dx_ref, du_ref, h_ref, dy_ref, dg_ref):
        dxo = dxo_ref[...]
        dy = (0.5 * dxo).astype(BF)
        dy_ref[...] = dy
        dh = _dot_nt(dy, wo_ref[...])
        a = u_ref[:, :F].astype(F32)
        b = u_ref[:, F:].astype(F32)
        s = _sig(a)
        sa = a * s
        h_ref[...] = (sa * b).astype(BF)
        du_ref[:, :F] = (dh * b * (s * (1.0 + a * (1.0 - s)))).astype(BF)
        du_ref[:, F:] = (dh * sa).astype(BF)
        dn = _dot(du_ref[...], w_ref[...])
        dx, dg = _rms_bwd(dn, x_ref[...], g_ref[...])
        dx_ref[...] = dxo + dx

        @pl.when(pl.program_id(0) == 0)
        def _():
            dg_ref[...] = jnp.zeros_like(dg_ref)

        dg_ref[...] += dg

    return _call(
        body, deps, (dxo, x, g, u, w_in_t, w_out), grid=(t // tm,),
        in_specs=[_row_tile(tm, D), _row_tile(tm, D), _resident((1, D)), _row_tile(tm, INW), _resident((INW, D)),
                  _resident((F, D))],
        out_specs=[_row_tile(tm, D), _row_tile(tm, INW), _row_tile(tm, F), _row_tile(tm, D),
                   pl.BlockSpec((1, D), lambda i: (0, 0))],
        out_shape=[jax.ShapeDtypeStruct((t, D), F32), jax.ShapeDtypeStruct((t, INW), BF), jax.ShapeDtypeStruct((t, F), BF),
                   jax.ShapeDtypeStruct((t, D), BF), jax.ShapeDtypeStruct((1, D), F32)],
        compiler_params=_params(1), name=name)


def _wgrad(lhs, rhs, name, *, lhs_is_transposed, chunk, deps=()):
    t = rhs.shape[0]
    n = lhs.shape[0] if lhs_is_transposed else lhs.shape[1]
    tm = min(512, t)
    c = min(chunk, n)
    n_tok = t // tm

    def body(l_ref, r_ref, o_ref, acc_ref):
        i = pl.program_id(1)

        @pl.when(i == 0)
        def _():
            acc_ref[...] = jnp.zeros_like(acc_ref)

        lhs_tile = l_ref[...].astype(BF)
        rhs_tile = r_ref[...].astype(BF)
        if lhs_is_transposed:
            acc_ref[...] += _dot(lhs_tile, rhs_tile)
        else:
            acc_ref[...] += _dot_tn(lhs_tile, rhs_tile)

        @pl.when(i == n_tok - 1)
        def _():
            o_ref[...] = acc_ref[...].astype(o_ref.dtype)

    if lhs_is_transposed:
        lhs_spec = pl.BlockSpec((c, tm), lambda j, i: (j, i))
    else:
        lhs_spec = pl.BlockSpec((tm, c), lambda j, i: (i, j))
    return _call(
        body, deps, (lhs, rhs), grid=(n // c, n_tok),
        in_specs=[lhs_spec, pl.BlockSpec((tm, D), lambda j, i: (i, 0))],
        out_specs=pl.BlockSpec((c, D), lambda j, i: (j, 0)),
        out_shape=jax.ShapeDtypeStruct((n, D), BF),
        scratch_shapes=[pltpu.VMEM((c, D), F32)],
        compiler_params=_params(2), name=name)


def _wgrad_mix(duc, dq_t, dkv_t, dgp, hm):
    t = hm.shape[0]
    tm = min(1024, t)
    c = 512
    n_tok = t // tm
    first_q, first_kv, first_gate = R_Q[0] // c, R_KV[0] // c, R_GATE[0] // c

    def body(uc_ref, q_ref, kv_ref, gp_ref, h_ref, o_ref, acc_ref):
        j = pl.program_id(0)
        i = pl.program_id(1)

        @pl.when(i == 0)
        def _():
            acc_ref[...] = jnp.zeros_like(acc_ref)

        @pl.when(j < first_q)
        def _():
            acc_ref[...] += _dot_tn(uc_ref[...], h_ref[...])

        @pl.when((j >= first_q) & (j < first_kv))
        def _():
            acc_ref[...] += _dot(q_ref[...], h_ref[...])

        @pl.when(j == first_kv)
        def _():
            acc_ref[...] += _dot(kv_ref[...], h_ref[...])

        @pl.when(j >= first_gate)
        def _():
            acc_ref[...] += _dot_tn(gp_ref[...], h_ref[...])

        @pl.when(i == n_tok - 1)
        def _():
            o_ref[...] = acc_ref[...].astype(BF)

    def tokens(active, i):
        return jnp.where(active, i, 0)

    return pl.pallas_call(
        body, grid=(INW // c, n_tok),
        in_specs=[pl.BlockSpec((tm, c), lambda j, i: (tokens(j < first_q, i), jnp.clip(j, 0, first_q - 1))),
                  pl.BlockSpec((c, tm), lambda j, i: (jnp.clip(j - first_q, 0, first_kv - first_q - 1),
                                                      tokens((j >= first_q) & (j < first_kv), i))),
                  pl.BlockSpec((c, tm), lambda j, i: (0, tokens(j == first_kv, i))),
                  pl.BlockSpec((tm, c), lambda j, i: (tokens(j >= first_gate, i),
                                                      jnp.clip(j - first_gate, 0, INW // c - first_gate - 1))),
                  pl.BlockSpec((tm, D), lambda j, i: (i, 0))],
        out_specs=pl.BlockSpec((c, D), lambda j, i: (j, 0)),
        out_shape=jax.ShapeDtypeStruct((INW, D), BF),
        scratch_shapes=[pltpu.VMEM((c, D), F32)],
        compiler_params=_params(2), name="mix_dw_in")(duc, dq_t, dkv_t, dgp, hm)


def _mix_proj(x, g, w_t):
    t = x.shape[0]
    tm = min(256, t)

    def body(x_ref, g_ref, w_ref, hm_ref, uc_ref, gp_ref, qkv_ref):
        r, xh = _rms_stats(x_ref[...])
        hm = (xh * g_ref[...]).astype(BF)
        hm_ref[...] = hm
        uc_ref[...] = _dot_nt(hm, w_ref[R_CONV[0]:R_CONV[1], :]).astype(BF)
        gp_ref[...] = _dot_nt(hm, w_ref[R_GATE[0]:R_GATE[1], :]).astype(BF)
        qkv_ref[...] = _dot_nt(w_ref[R_QKV[0]:R_QKV[1], :], hm).astype(BF)

    return pl.pallas_call(
        body, grid=(t // tm,),
        in_specs=[_row_tile(tm, D), _resident((1, D)), _resident((INW, D))],
        out_specs=[_row_tile(tm, D), _row_tile(tm, 2 * D), _row_tile(tm, 2 * D), pl.BlockSpec((1536, tm), lambda i: (0, i))],
        out_shape=[jax.ShapeDtypeStruct((t, D), BF), jax.ShapeDtypeStruct((t, 2 * D), BF),
                   jax.ShapeDtypeStruct((t, 2 * D), BF), jax.ShapeDtypeStruct((1536, t), BF)],
        compiler_params=_params(1), name="mix_proj")(x, g, w_t)


CONV_HALO = 32
CONV_LEAD = CONV_HALO - (CW - 1)


def _glu(uc):
    uc = uc.astype(F32)
    return uc[:, :D] * _sig(uc[:, D:])


def _ln_stats(zc):
    mu = jnp.mean(zc, axis=-1, keepdims=True)
    zm = zc - mu
    r = lax.rsqrt(jnp.mean(zm * zm, axis=-1, keepdims=True) + EPS)
    return r, zm * r


CONV_SHIFTS = 8
CONV_CHUNK = 32


def _store_shifted(buf, rows):
    for b in range(1, CONV_SHIFTS):
        buf[b, 0:rows - 8, :] = buf[0, pl.ds(b, rows - 8), :]


def _conv_fwd(uc, dwk, dwb, lng, lnb):
    t = uc.shape[0]
    tm = min(512, t)
    per = tm // CONV_HALO
    ext = tm + CONV_HALO

    def body(cur_ref, prev_ref, k_ref, kb_ref, g_ref, b_ref, o_ref, zc_ref, zsh):
        i = pl.program_id(0)
        zsh[0, 0:CONV_HALO, :] = _glu(prev_ref[...]) * (i > 0).astype(F32)
        zsh[0, CONV_HALO:, :] = _glu(cur_ref[...])
        _store_shifted(zsh, ext)

        def chunk(ci, carry):
            r0 = pl.multiple_of(ci * CONV_CHUNK, CONV_CHUNK)
            acc = jnp.zeros((CONV_CHUNK, D), F32) + kb_ref[...]
            for w in range(CW):
                a, b = divmod(CONV_LEAD + w, 8)
                acc = acc + k_ref[w:w + 1, :] * zsh[b, pl.ds(r0 + 8 * a, CONV_CHUNK), :]
            zc_ref[pl.ds(r0, CONV_CHUNK), :] = acc
            r, xh = _ln_stats(acc)
            y = xh * g_ref[...] + b_ref[...]
            o_ref[pl.ds(r0, CONV_CHUNK), :] = (y * _sig(y)).astype(BF)
            return carry

        lax.fori_loop(0, tm // CONV_CHUNK, chunk, 0)

    return pl.pallas_call(
        body, grid=(t // tm,),
        in_specs=[_row_tile(tm, 2 * D),
                  pl.BlockSpec((CONV_HALO, 2 * D), lambda i: (jnp.maximum(i * per - 1, 0), 0)),
                  _resident((CWP, D)), _resident((1, D)), _resident((1, D)), _resident((1, D))],
        out_specs=[_row_tile(tm, D), _row_tile(tm, D)],
        out_shape=[jax.ShapeDtypeStruct((t, D), BF), jax.ShapeDtypeStruct((t, D), F32)],
        scratch_shapes=[pltpu.VMEM((CONV_SHIFTS, ext, D), F32)],
        compiler_params=_params(1), name="conv_fwd")(uc, uc, dwk, dwb, lng, lnb)


def _conv_bwd(uc, zc, dzs, dwk, lng, lnb):
    t = uc.shape[0]
    tm = min(256, t)
    per = tm // CONV_HALO
    n_tiles = t // tm
    ext = tm + CONV_HALO
    last_block = t // CONV_HALO - 1

    def body(cur_ref, prev_ref, zc_ref, zcn_ref, dz_ref, dzn_ref, k_ref, g_ref, b_ref,
             duc_ref, dk_ref, dkb_ref, dg_ref, db_ref, zsh, dsh, dk8):
        i = pl.program_id(0)

        @pl.when(i == 0)
        def _():
            dk8[...] = jnp.zeros_like(dk8)
            dkb_ref[...] = jnp.zeros_like(dkb_ref)
            dg_ref[...] = jnp.zeros_like(dg_ref)
            db_ref[...] = jnp.zeros_like(db_ref)

        has_next = (i < n_tiles - 1).astype(F32)
        zsh[0, 0:CONV_HALO, :] = _glu(prev_ref[...]) * (i > 0).astype(F32)
        zsh[0, CONV_HALO:, :] = _glu(cur_ref[...])
        _store_shifted(zsh, ext)
        gain = g_ref[...]

        def ln_silu_bwd(zc, dzs, live):
            r, xh = _ln_stats(zc)
            y = xh * gain + b_ref[...]
            sy = _sig(y)
            dy = dzs * (sy * (1.0 + y * (1.0 - sy))) * live
            dxh = dy * gain
            dzc = r * (dxh - jnp.mean(dxh, axis=-1, keepdims=True) - xh * jnp.mean(dxh * xh, axis=-1, keepdims=True))
            return dzc, dy, xh

        dzc, dy, xh = ln_silu_bwd(zc_ref[...], dz_ref[...], 1.0)
        dsh[0, 0:tm, :] = dzc
        dg_ref[...] += jnp.sum(dy * xh, axis=0, keepdims=True)
        db_ref[...] += jnp.sum(dy, axis=0, keepdims=True)
        dkb_ref[...] += jnp.sum(dzc, axis=0, keepdims=True)
        dsh[0, tm:, :] = ln_silu_bwd(zcn_ref[...], dzn_ref[...], has_next)[0]
        _store_shifted(dsh, ext)

        def chunk(ci, carry):
            r0 = pl.multiple_of(ci * CONV_CHUNK, CONV_CHUNK)
            dzc_c = dsh[0, pl.ds(r0, CONV_CHUNK), :]
            dz = jnp.zeros((CONV_CHUNK, D), F32)
            for w in range(CW):
                a, b = divmod(CW - 1 - w, 8)
                dz = dz + k_ref[w:w + 1, :] * dsh[b, pl.ds(r0 + 8 * a, CONV_CHUNK), :]
                a, b = divmod(CONV_LEAD + w, 8)
                prod = dzc_c * zsh[b, pl.ds(r0 + 8 * a, CONV_CHUNK), :]
                part = prod[0:8, :]
                for j in range(1, CONV_CHUNK // 8):
                    part = part + prod[8 * j:8 * j + 8, :]
                dk8[w] += part
            ucc = cur_ref[pl.ds(r0, CONV_CHUNK), :].astype(F32)
            sg = _sig(ucc[:, D:])
            duc_ref[pl.ds(r0, CONV_CHUNK), 0:D] = (dz * sg).astype(BF)
            duc_ref[pl.ds(r0, CONV_CHUNK), D:2 * D] = (dz * ucc[:, :D] * sg * (1.0 - sg)).astype(BF)
            return carry

        lax.fori_loop(0, tm // CONV_CHUNK, chunk, 0)

        @pl.when(i == n_tiles - 1)
        def _():
            dk_ref[...] = jnp.sum(dk8[...], axis=1)

    vec = pl.BlockSpec((1, D), lambda i: (0, 0))
    next_halo = pl.BlockSpec((CONV_HALO, D), lambda i: (jnp.minimum((i + 1) * per, last_block), 0))
    return pl.pallas_call(
        body, grid=(n_tiles,),
        in_specs=[_row_tile(tm, 2 * D),
                  pl.BlockSpec((CONV_HALO, 2 * D), lambda i: (jnp.maximum(i * per - 1, 0), 0)),
                  _row_tile(tm, D), next_halo, _row_tile(tm, D), next_halo,
                  _resident((CWP, D)), _resident((1, D)), _resident((1, D))],
        out_specs=[_row_tile(tm, 2 * D), pl.BlockSpec((CWP, D), lambda i: (0, 0)), vec, vec, vec],
        out_shape=[jax.ShapeDtypeStruct((t, 2 * D), BF), jax.ShapeDtypeStruct((CWP, D), F32),
                   jax.ShapeDtypeStruct((1, D), F32), jax.ShapeDtypeStruct((1, D), F32), jax.ShapeDtypeStruct((1, D), F32)],
        scratch_shapes=[pltpu.VMEM((CONV_SHIFTS, ext, D), F32), pltpu.VMEM((CONV_SHIFTS, ext, D), F32),
                        pltpu.VMEM((CWP, 8, D), F32)],
        compiler_params=_params(1), name="conv_bwd")(uc, uc, zc, zc, dzs, dzs, dwk, lng, lnb)


def _norm_rows(xt, g):
    r = lax.rsqrt(jnp.mean(xt * xt, axis=0, keepdims=True) + EPS)
    xh = xt * r
    return xh * g, r, xh


ATT_TQ = 512


def _attn_specs(t, tq):
    per = tq // BLK
    return [pl.BlockSpec((1536, tq), lambda i: (0, i)),
            pl.BlockSpec((512, BLK), lambda i: (2, jnp.maximum(i * per - 1, 0))),
            _resident((HD, 1)), _resident((HD, 1)), _resident((NKV, 1, GRP * BLK)),
            _resident((NKV, 2 * BLK, GRP * BLK)), _resident((2, 2 * BLK, GRP * BLK))]


def _attn_window(hk, sb, qkv_ref, halo_ref, kn_cur, kn_halo):
    v0 = D + NKV * HD + hk * HD
    if sb == 0:
        k_prev = kn_halo[hk]
        v_prev = halo_ref[NKV * HD + hk * HD:NKV * HD + (hk + 1) * HD, :]
    else:
        k_prev = kn_cur[hk][:, (sb - 1) * BLK:sb * BLK]
        v_prev = qkv_ref[v0:v0 + HD, (sb - 1) * BLK:sb * BLK]
    kw = jnp.concatenate([k_prev, kn_cur[hk][:, sb * BLK:(sb + 1) * BLK]], axis=1).astype(BF)
    vw = jnp.concatenate([v_prev, qkv_ref[v0:v0 + HD, sb * BLK:(sb + 1) * BLK]], axis=1)
    return kw, vw


def _attn_probs(kw, qc, bias, mask, sink):
    st = _dot_tn(kw, qc) * QK_SCALE + bias
    st = jnp.where(mask > 0.5, st, NEG)
    m = jnp.maximum(jnp.max(st, axis=0, keepdims=True), sink)
    p = jnp.exp(st - m)
    e_sink = jnp.exp(sink - m)
    inv = 1.0 / (jnp.sum(p, axis=0, keepdims=True) + e_sink)
    return p * inv, e_sink * inv


def _attn_fwd(qkv_t, qg, kg, sink_rows, bias_t, mask_t):
    t = qkv_t.shape[1]
    tq = min(ATT_TQ, t)
    n_sub = tq // BLK

    def body(qkv_ref, halo_ref, qg_ref, kg_ref, sink_ref, bias_ref, mask_ref, o_ref):
        i = pl.program_id(0)
        first = (i == 0).astype(jnp.int32)
        kgain = kg_ref[...]
        qgain = qg_ref[...]
        kn_cur = [_norm_rows(qkv_ref[D + h * HD:D + (h + 1) * HD, :].astype(F32), kgain)[0] for h in range(NKV)]
        kn_halo = [_norm_rows(halo_ref[h * HD:(h + 1) * HD, :].astype(F32), kgain)[0] for h in range(NKV)]
        for hk in range(NKV):
            for sb in range(n_sub):
                cols = slice(sb * BLK, (sb + 1) * BLK)
                kw, vw = _attn_window(hk, sb, qkv_ref, halo_ref, kn_cur, kn_halo)
                qc = jnp.concatenate(
                    [_norm_rows(qkv_ref[(GRP * hk + g) * HD:(GRP * hk + g + 1) * HD, cols].astype(F32), qgain)[0]
                     for g in range(GRP)], axis=1).astype(BF)
                mask = mask_ref[first] if sb == 0 else mask_ref[0]
                p, _ = _attn_probs(kw, qc, bias_ref[hk], mask, sink_ref[hk])
                o = _dot(vw, p.astype(BF))
                for g in range(GRP):
                    head = GRP * hk + g
                    o_ref[head * HD:(head + 1) * HD, cols] = o[:, g * BLK:(g + 1) * BLK].astype(BF)

    return pl.pallas_call(
        body, grid=(t // tq,),
        in_specs=_attn_specs(t, tq),
        out_specs=pl.BlockSpec((D, tq), lambda i: (0, i)),
        out_shape=jax.ShapeDtypeStruct((D, t), BF),
        compiler_params=_params(1), name="attn_fwd")(qkv_t, qkv_t, qg, kg, sink_rows, bias_t, mask_t)


def _attn_bwd(qkv_t, do_t, qg, kg, sink_rows, bias_t, mask_t, deps=()):
    t = qkv_t.shape[1]
    tq = min(ATT_TQ, t)
    n_sub = tq // BLK
    n_tiles = t // tq

    def body(qkv_ref, halo_ref, do_ref, qg_ref, kg_ref, sink_ref, bias_ref, mask_ref,
             dq_ref, ckv_ref, dqg_ref, dsink_ref, dsacc_ref, qg_scr):
        i = pl.program_id(0)

        @pl.when(i == 0)
        def _():
            qg_scr[...] = jnp.zeros_like(qg_scr)
            dsink_ref[...] = jnp.zeros_like(dsink_ref)
            dsacc_ref[...] = jnp.zeros_like(dsacc_ref)

        first = (i == 0).astype(jnp.int32)
        kgain = kg_ref[...]
        qgain = qg_ref[...]
        kn_cur = [_norm_rows(qkv_ref[D + h * HD:D + (h + 1) * HD, :].astype(F32), kgain)[0] for h in range(NKV)]
        kn_halo = [_norm_rows(halo_ref[h * HD:(h + 1) * HD, :].astype(F32), kgain)[0] for h in range(NKV)]
        dqg = jnp.zeros((HD, BLK), F32)
        for hk in range(NKV):
            for sb in range(n_sub):
                cols = slice(sb * BLK, (sb + 1) * BLK)
                kw, vw = _attn_window(hk, sb, qkv_ref, halo_ref, kn_cur, kn_halo)
                qn, qr, qh = [], [], []
                for g in range(GRP):
                    head = GRP * hk + g
                    n_, r_, h_ = _norm_rows(qkv_ref[head * HD:(head + 1) * HD, cols].astype(F32), qgain)
                    qn.append(n_)
                    qr.append(r_)
                    qh.append(h_)
                qc = jnp.concatenate(qn, axis=1).astype(BF)
                mask = mask_ref[first] if sb == 0 else mask_ref[0]
                p, p_sink = _attn_probs(kw, qc, bias_ref[hk], mask, sink_ref[hk])
                doc = jnp.concatenate([do_ref[(GRP * hk + g) * HD:(GRP * hk + g + 1) * HD, cols] for g in range(GRP)], axis=1)
                dp = _dot_tn(vw, doc)
                delta = jnp.sum(p * dp, axis=0, keepdims=True)
                ds = p * (dp - delta)
                dsink_ref[hk] += -(p_sink * delta)
                dsacc_ref[hk] += ds
                dsb = ds.astype(BF)
                dqc = _dot(kw, dsb) * QK_SCALE
                ckv_ref[sb, hk * HD:(hk + 1) * HD, :] = _dot_nt(qc, dsb) * QK_SCALE
                ckv_ref[sb, NKV * HD + hk * HD:NKV * HD + (hk + 1) * HD, :] = _dot_nt(doc, p.astype(BF))
                for g in range(GRP):
                    head = GRP * hk + g
                    dqn = dqc[:, g * BLK:(g + 1) * BLK]
                    dqh = dqn * qgain
                    dq = qr[g] * (dqh - qh[g] * jnp.mean(dqh * qh[g], axis=0, keepdims=True))
                    dq_ref[head * HD:(head + 1) * HD, cols] = dq.astype(BF)
                    dqg = dqg + dqn * qh[g]
        qg_scr[...] += dqg

        @pl.when(i == n_tiles - 1)
        def _():
            dqg_ref[...] = jnp.sum(qg_scr[...], axis=1, keepdims=True)

    return _call(
        body, deps, (qkv_t, qkv_t, do_t, qg, kg, sink_rows, bias_t, mask_t), grid=(n_tiles,),
        in_specs=_attn_specs(t, tq)[:2] + [pl.BlockSpec((D, tq), lambda i: (0, i))] + _attn_specs(t, tq)[2:],
        out_specs=[pl.BlockSpec((D, tq), lambda i: (0, i)),
                   pl.BlockSpec((n_sub, 2 * NKV * HD, 2 * BLK), lambda i: (i, 0, 0)),
                   pl.BlockSpec((HD, 1), lambda i: (0, 0)),
                   pl.BlockSpec((NKV, 1, GRP * BLK), lambda i: (0, 0, 0)),
                   pl.BlockSpec((NKV, 2 * BLK, GRP * BLK), lambda i: (0, 0, 0))],
        out_shape=[jax.ShapeDtypeStruct((D, t), BF),
                   jax.ShapeDtypeStruct((t // BLK, 2 * NKV * HD, 2 * BLK), F32),
                   jax.ShapeDtypeStruct((HD, 1), F32),
                   jax.ShapeDtypeStruct((NKV, 1, GRP * BLK), F32),
                   jax.ShapeDtypeStruct((NKV, 2 * BLK, GRP * BLK), F32)],
        scratch_shapes=[pltpu.VMEM((HD, BLK), F32)],
        compiler_params=_params(1), name="attn_bwd")


def _kv_combine(ckv, qkv_t, kg):
    nb = ckv.shape[0]
    t = nb * BLK
    rows = NKV * HD

    def body(c_ref, cn_ref, k_ref, kg_ref, o_ref, dkg_ref, kg_scr):
        n = pl.program_id(0)

        @pl.when(n == 0)
        def _():
            kg_scr[...] = jnp.zeros_like(kg_scr)

        has_next = (n < nb - 1).astype(F32)
        d = c_ref[0, :, BLK:] + cn_ref[0, :, :BLK] * has_next
        o_ref[rows:, :] = d[rows:, :].astype(BF)
        kgain = kg_ref[...]
        dkg = jnp.zeros((HD, BLK), F32)
        for h in range(NKV):
            _, r, kh = _norm_rows(k_ref[h * HD:(h + 1) * HD, :].astype(F32), kgain)
            dkn = d[h * HD:(h + 1) * HD, :]
            dkh = dkn * kgain
            o_ref[h * HD:(h + 1) * HD, :] = (r * (dkh - kh * jnp.mean(dkh * kh, axis=0, keepdims=True))).astype(BF)
            dkg = dkg + dkn * kh
        kg_scr[...] += dkg

        @pl.when(n == nb - 1)
        def _():
            dkg_ref[...] = jnp.sum(kg_scr[...], axis=1, keepdims=True)

    return pl.pallas_call(
        body, grid=(nb,),
        in_specs=[pl.BlockSpec((1, 2 * rows, 2 * BLK), lambda n: (n, 0, 0)),
                  pl.BlockSpec((1, 2 * rows, 2 * BLK), lambda n: (jnp.minimum(n + 1, nb - 1), 0, 0)),
                  pl.BlockSpec((rows, BLK), lambda n: (D // rows, n)),
                  _resident((HD, 1))],
        out_specs=[pl.BlockSpec((2 * rows, BLK), lambda n: (0, n)), pl.BlockSpec((HD, 1), lambda n: (0, 0))],
        out_shape=[jax.ShapeDtypeStruct((2 * rows, t), BF), jax.ShapeDtypeStruct((HD, 1), F32)],
        scratch_shapes=[pltpu.VMEM((HD, BLK), F32)],
        compiler_params=_params(1), name="kv_combine")(ckv, ckv, qkv_t, kg)


def _group_lane_sums(v):
    lane_group = lax.broadcasted_iota(jnp.int32, (1, GRP * BLK), 1) // BLK
    col = lax.broadcasted_iota(jnp.int32, (1, BLK), 1)
    out = jnp.zeros((NKV, BLK), F32)
    for g in range(GRP):
        s = jnp.sum(jnp.where(lane_group == g, v, 0.0), axis=1, keepdims=True)
        out = jnp.where(col == g, s, out)
    return out


def _bias_grad(dsacc, onehot_t):
    def body(ds_ref, oh_ref, o_ref):
        oh = jnp.concatenate([oh_ref[0]] * GRP, axis=1)
        o_ref[0] = _group_lane_sums(jnp.sum(ds_ref[...] * oh[None], axis=1))

    return pl.pallas_call(
        body, grid=(NBUCKET,),
        in_specs=[_resident((NKV, 2 * BLK, GRP * BLK)), pl.BlockSpec((1, 2 * BLK, BLK), lambda b: (b, 0, 0))],
        out_specs=pl.BlockSpec((1, NKV, BLK), lambda b: (b, 0, 0)),
        out_shape=jax.ShapeDtypeStruct((NBUCKET, NKV, BLK), F32),
        compiler_params=_params(1), name="bias_grad")(dsacc, onehot_t)


def _sink_grad(dsink_rows):
    def body(d_ref, o_ref):
        o_ref[...] = _group_lane_sums(d_ref[:, 0, :])

    return pl.pallas_call(body, out_shape=jax.ShapeDtypeStruct((NKV, BLK), F32), name="sink_grad")(dsink_rows)


def _mix_out(zs, o_t, gp, x, w_cp, w_o, w_out):
    t = x.shape[0]
    tm = min(256, t)

    def body(zs_ref, ot_ref, gp_ref, x_ref, wcp_ref, wo_ref, wout_ref, xo_ref, a_ref, b_ref, m_ref):
        a = _dot(zs_ref[...], wcp_ref[...])
        b = _dot_tn(ot_ref[...], wo_ref[...])
        a_ref[...] = a.astype(BF)
        b_ref[...] = b.astype(BF)
        merged = (_sig(gp_ref[:, :D].astype(F32)) * a + _sig(gp_ref[:, D:].astype(F32)) * b).astype(BF)
        m_ref[...] = merged
        xo_ref[...] = x_ref[...] + _dot(merged, wout_ref[...])

    return pl.pallas_call(
        body, grid=(t // tm,),
        in_specs=[_row_tile(tm, D), pl.BlockSpec((D, tm), lambda i: (0, i)), _row_tile(tm, 2 * D), _row_tile(tm, D),
                  _resident((D, D)), _resident((D, D)), _resident((D, D))],
        out_specs=[_row_tile(tm, D)] * 4,
        out_shape=[jax.ShapeDtypeStruct((t, D), F32)] + [jax.ShapeDtypeStruct((t, D), BF)] * 3,
        compiler_params=_params(1), name="mix_out")(zs, o_t, gp, x, w_cp, w_o, w_out)


def _mix_out_bwd(dx, a, b, gp, w_cp, w_o, w_out, deps=()):
    t = dx.shape[0]
    tm = min(256, t)

    def body(dx_ref, a_ref, b_ref, gp_ref, wcp_ref, wo_ref, wout_ref, dzs_ref, dot_ref, dgp_ref, da_ref, db_ref, dxb_ref):
        dxb = dx_ref[...].astype(BF)
        dxb_ref[...] = dxb
        dm = _dot_nt(dxb, wout_ref[...])
        gc = _sig(gp_ref[:, :D].astype(F32))
        ga = _sig(gp_ref[:, D:].astype(F32))
        da = (dm * gc).astype(BF)
        db = (dm * ga).astype(BF)
        da_ref[...] = da
        db_ref[...] = db
        dgp_ref[:, :D] = (dm * a_ref[...].astype(F32) * gc * (1.0 - gc)).astype(BF)
        dgp_ref[:, D:] = (dm * b_ref[...].astype(F32) * ga * (1.0 - ga)).astype(BF)
        dzs_ref[...] = _dot_nt(da, wcp_ref[...])
        dot_ref[...] = _dot_nt(wo_ref[...], db).astype(BF)

    return _call(
        body, deps, (dx, a, b, gp, w_cp, w_o, w_out), grid=(t // tm,),
        in_specs=[_row_tile(tm, D), _row_tile(tm, D), _row_tile(tm, D), _row_tile(tm, 2 * D),
                  _resident((D, D)), _resident((D, D)), _resident((D, D))],
        out_specs=[_row_tile(tm, D), pl.BlockSpec((D, tm), lambda i: (0, i)), _row_tile(tm, 2 * D),
                   _row_tile(tm, D), _row_tile(tm, D), _row_tile(tm, D)],
        out_shape=[jax.ShapeDtypeStruct((t, D), F32), jax.ShapeDtypeStruct((D, t), BF), jax.ShapeDtypeStruct((t, 2 * D), BF),
                   jax.ShapeDtypeStruct((t, D), BF), jax.ShapeDtypeStruct((t, D), BF), jax.ShapeDtypeStruct((t, D), BF)],
        compiler_params=_params(1), name="mix_out_bwd")


def _mix_proj_bwd(dxo, duc, dq_t, dkv_t, dgp, x, g, w_t):
    t = x.shape[0]
    tm = min(256, t)

    def body(dxo_ref, duc_ref, dq_ref, dkv_ref, dgp_ref, x_ref, g_ref, w_ref, dx_ref, dg_ref):
        dn = _dot(duc_ref[...], w_ref[R_CONV[0]:R_CONV[1], :])
        dn = dn + _dot(dgp_ref[...], w_ref[R_GATE[0]:R_GATE[1], :])
        dn = dn + _dot_tn(dq_ref[...], w_ref[R_Q[0]:R_Q[1], :])
        dn = dn + _dot_tn(dkv_ref[...], w_ref[R_KV[0]:R_KV[1], :])
        dx, dg = _rms_bwd(dn, x_ref[...], g_ref[...])
        dx_ref[...] = dxo_ref[...] + dx

        @pl.when(pl.program_id(0) == 0)
        def _():
            dg_ref[...] = jnp.zeros_like(dg_ref)

        dg_ref[...] += dg

    return pl.pallas_call(
        body, grid=(t // tm,),
        in_specs=[_row_tile(tm, D), _row_tile(tm, 2 * D), pl.BlockSpec((D, tm), lambda i: (0, i)),
                  pl.BlockSpec((2 * NKV * HD, tm), lambda i: (0, i)), _row_tile(tm, 2 * D), _row_tile(tm, D),
                  _resident((1, D)), _resident((INW, D))],
        out_specs=[_row_tile(tm, D), pl.BlockSpec((1, D), lambda i: (0, 0))],
        out_shape=[jax.ShapeDtypeStruct((t, D), F32), jax.ShapeDtypeStruct((1, D), F32)],
        compiler_params=_params(1), name="mix_proj_bwd")(dxo, duc, dq_t, dkv_t, dgp, x, g, w_t)


def _attention_tables():
    kj = np.arange(2 * BLK)[:, None]
    qi = np.arange(BLK)[None, :]
    dist = qi + BLK - kj
    in_win = (dist >= 0) & (dist < BLK)
    dpos = np.maximum(dist, 0)
    max_exact = NBUCKET // 2
    dfl = np.maximum(dpos, 1).astype(np.float32)
    large = max_exact + (np.log(dfl / np.float32(max_exact)) / np.float32(math.log(BLK / max_exact))
                         * np.float32(NBUCKET - max_exact)).astype(np.int32)
    large = np.minimum(large, NBUCKET - 1)
    bucket = np.where(dpos < max_exact, dpos, large)
    onehot = (bucket[None] == np.arange(NBUCKET)[:, None, None]).astype(np.float32)
    mask = in_win.astype(np.float32)
    mask_first = mask * (kj >= BLK)
    masks = np.stack([np.tile(mask, (1, GRP)), np.tile(mask_first, (1, GRP))])
    return onehot, masks


def _bias_table(rel_bias, onehot):
    tab = jnp.einsum("bkq,bh->hkq", onehot, rel_bias, precision=lax.Precision.HIGHEST)
    tab = tab.reshape(NKV, GRP, 2 * BLK, BLK)
    return jnp.transpose(tab, (0, 2, 1, 3)).reshape(NKV, 2 * BLK, GRP * BLK)


def _local_step(x, target, vec, weights_of, grads_done, small_done):
    onehot_np, masks_np = _attention_tables()
    onehot = jnp.asarray(onehot_np)
    masks = jnp.asarray(masks_np)
    bias_t = _bias_table(vec["rel_bias"], onehot)
    sink_rows = jnp.repeat(vec["attn_sinks"].reshape(NKV, 1, GRP), BLK, axis=2)
    qg = vec["q_norm"].reshape(HD, 1)
    kg = vec["k_norm"].reshape(HD, 1)
    g1 = vec["ffn1_norm"].reshape(1, D)
    gm = vec["mix_norm"].reshape(1, D)
    g2 = vec["ffn2_norm"].reshape(1, D)
    dwb = vec["conv_dw_bias"].reshape(1, D)
    lng = vec["conv_ln_g"].reshape(1, D)
    lnb = vec["conv_ln_b"].reshape(1, D)

    w1 = weights_of("ffn1", x)
    n1, u1, x1 = _ffn_fwd(x, g1, w1["ffn1_w_in"], w1["ffn1_w_out"], "ffn1_fwd")
    wm = weights_of("mix", x1)
    dwk = jnp.pad(wm["conv_dw_kernel"], ((0, CWP - CW), (0, 0)))
    hm, uc, gp, qkv_t = _mix_proj(x1, gm, wm["w_in"])
    zs, zc = _conv_fwd(uc, dwk, dwb, lng, lnb)
    o_t = _attn_fwd(qkv_t, qg, kg, sink_rows, bias_t, masks)
    x2, a, b, merged = _mix_out(zs, o_t, gp, x1, wm["conv_w_proj"], wm["attn_w_o"], wm["w_out"])
    w2 = weights_of("ffn2", x2)
    n2, u2, dx3, sq = _ffn_fwd(x2, g2, w2["ffn2_w_in"], w2["ffn2_w_out"], "ffn2_fwd", target=target)

    gv = {}
    dx2, du2, h2, dy2, gv["ffn2_norm"] = _ffn_bwd(dx3, x2, g2, u2, w2["ffn2_w_in"], w2["ffn2_w_out"], "ffn2_bwd")
    deps = grads_done("ffn2", {"ffn2_w_in": _wgrad(du2, n2, "ffn2_dw_in", lhs_is_transposed=False, chunk=1408),
                               "ffn2_w_out": _wgrad(h2, dy2, "ffn2_dw_out", lhs_is_transposed=False, chunk=1408)})

    dzs, do_t, dgp, da, db, dx2b = _mix_out_bwd(dx2, a, b, gp, wm["conv_w_proj"], wm["attn_w_o"], wm["w_out"], deps=deps)
    deps = grads_done("mix_out", {"w_out": _wgrad(merged, dx2b, "mix_dw_out", lhs_is_transposed=False, chunk=1024),
                                  "conv_w_proj": _wgrad(zs, da, "mix_dw_cp", lhs_is_transposed=False, chunk=1024),
                                  "attn_w_o": _wgrad(o_t, db, "mix_dw_o", lhs_is_transposed=True, chunk=1024)})

    dq_t, ckv, dqg, dsink_rows, dsacc = _attn_bwd(qkv_t, do_t, qg, kg, sink_rows, bias_t, masks, deps=deps)
    dkv_t, dkg = _kv_combine(ckv, qkv_t, kg)
    gv["q_norm"] = dqg.reshape(HD)
    gv["k_norm"] = dkg.reshape(HD)
    gv["attn_sinks"] = _sink_grad(dsink_rows)[:, :GRP].reshape(NQ)
    gv["rel_bias"] = _bias_grad(dsacc, onehot)[:, :, :GRP].reshape(NBUCKET, NQ)

    duc, dk_conv, gv["conv_dw_bias"], gv["conv_ln_g"], gv["conv_ln_b"] = _conv_bwd(uc, zc, dzs, dwk, lng, lnb)
    gv["conv_dw_kernel"] = dk_conv[:CW]

    dx1, gv["mix_norm"] = _mix_proj_bwd(dx2, duc, dq_t, dkv_t, dgp, x1, gm, wm["w_in"])
    deps = grads_done("mix_in", {"w_in": _wgrad_mix(duc, dq_t, dkv_t, dgp, hm)})

    dx0, du1, h1, dy1, gv["ffn1_norm"] = _ffn_bwd(dx1, x, g1, u1, w1["ffn1_w_in"], w1["ffn1_w_out"], "ffn1_bwd", deps=deps)
    for k in ("ffn1_norm", "mix_norm", "ffn2_norm", "conv_dw_bias", "conv_ln_g", "conv_ln_b"):
        gv[k] = gv[k].reshape(D)
    deps = small_done(gv, sq)
    deps = grads_done("ffn1_out", {"ffn1_w_out": _wgrad(h1, dy1, "ffn1_dw_out", lhs_is_transposed=False, chunk=1408,
                                                        deps=deps)})
    grads_done("ffn1_in", {"ffn1_w_in": _wgrad(du1, n1, "ffn1_dw_in", lhs_is_transposed=False, chunk=1408, deps=deps)})
    return dx0


MESH_ID = pl.DeviceIdType.MESH


def _position():
    return lax.axis_index("x"), lax.axis_index("y"), lax.axis_index("c")


def _shard_rows(ref, index, rows):
    return ref.at[pl.ds(pl.multiple_of(index * rows, 16), rows), :]


def _prep(weights, taps, me):
    n = len(weights)

    def body(me_ref, *refs):
        for k in range(n):
            refs[n + 1 + k][...] = refs[k][...].astype(BF)
        refs[2 * n + 1][0:CW, :] = refs[n][...]
        refs[2 * n + 1][CW:, :] = jnp.zeros((CWP - CW, BLK), F32)

    shard_shapes = [w.shape for w in weights] + [(CWP, BLK)]
    dtypes = [BF] * n + [F32]
    ins = list(weights) + [taps]
    return pl.pallas_call(
        body,
        grid_spec=pltpu.PrefetchScalarGridSpec(
            num_scalar_prefetch=1, grid=(1,),
            in_specs=[pl.BlockSpec(a.shape, lambda i, m: (0, 0), pipeline_mode=pl.Buffered(1)) for a in ins],
            out_specs=[pl.BlockSpec(s, lambda i, m: (m[0], 0)) for s in shard_shapes]),
        out_shape=[jax.ShapeDtypeStruct((N_DEV * s[0], s[1]), d) for s, d in zip(shard_shapes, dtypes)],
        compiler_params=_params(1), name="prep")(me, *ins)


HBM = pl.BlockSpec(memory_space=pltpu.HBM)
SEM = pl.BlockSpec(memory_space=pltpu.SEMAPHORE)
DATAFLOW = pltpu.SideEffectType.DATAFLOW_SIDE_EFFECTING
TOKEN = jax.ShapeDtypeStruct((8, 128), F32)


def _in_hbm(x):
    return pltpu.with_memory_space_constraint(x, pltpu.HBM)


def _hbm_like(arrays):
    return [pltpu.HBM(a.shape, a.dtype) for a in arrays]


def _other_chips(x, y):
    return [(1 - x, y), (x, 1 - y), (1 - x, 1 - y)]


def _device_index(chip, c):
    return 4 * chip[0] + 2 * chip[1] + c


def _chip_index(chip):
    return 2 * chip[0] + chip[1]


class _Exchange:
    def __init__(self, gather):
        self.gather = gather

    def sent(self, x, y, c, chip):
        return _device_index((x, y), c) if self.gather else _chip_index(chip)

    def lands_at(self, x, y, c):
        return _device_index((x, y), c) if self.gather else _chip_index((x, y))

    def arrives_at(self, chip, c):
        return _device_index(chip, c) if self.gather else _chip_index(chip)


def _ici_copies_start(sets, sources, landings, exchange, name, deps=()):
    n = len(landings)
    arrays = (list(sources) if sources is not None else []) + list(landings)
    first_land = len(arrays) - n
    n_sets = len(sets)
    n_deps = len(deps)

    def body(*refs):
        refs = refs[n_deps:]
        src, land = refs[:n], refs[first_land:first_land + n]
        sems = refs[len(arrays):len(arrays) + 2 * n_sets]
        token = refs[-1]
        x, y, c = _position()
        for s, members in enumerate(sets):
            for slot, (k, rows) in enumerate(members):
                for j, chip in enumerate(_other_chips(x, y)):
                    pltpu.make_async_remote_copy(
                        src_ref=_shard_rows(src[k], exchange.sent(x, y, c, chip), rows),
                        dst_ref=_shard_rows(land[k], exchange.lands_at(x, y, c), rows),
                        send_sem=sems[2 * s].at[3 * slot + j], recv_sem=sems[2 * s + 1].at[3 * slot + j],
                        device_id=(*chip, c), device_id_type=MESH_ID).start()
        token[...] = jnp.zeros_like(token)

    sem_shapes = []
    for members in sets:
        sem_shapes += [pltpu.SemaphoreType.DMA((3 * len(members),))] * 2
    out = pl.pallas_call(
        body, name=name,
        out_shape=sem_shapes + _hbm_like(arrays) + [TOKEN],
        in_specs=[ANY] * n_deps + [HBM] * len(arrays),
        out_specs=[SEM] * (2 * n_sets) + [HBM] * len(arrays) + [pl.BlockSpec(memory_space=pltpu.VMEM)],
        input_output_aliases={n_deps + i: 2 * n_sets + i for i in range(len(arrays))},
        compiler_params=pltpu.CompilerParams(has_side_effects=DATAFLOW),
    )(*deps, *[_in_hbm(a) for a in arrays])
    sems = [(out[2 * s], out[2 * s + 1]) for s in range(n_sets)]
    thru = list(out[2 * n_sets:2 * n_sets + len(arrays)])
    return sems, (thru[:first_land] if sources is not None else None), thru[first_land:], out[-1]


def _ici_copies_wait(sems, members, sources, landings, exchange, after, name):
    n = len(landings)
    arrays = (list(sources) if sources is not None else []) + list(landings)
    first_land = len(arrays) - n

    def body(*refs):
        src, land = refs[:n], refs[first_land:first_land + n]
        send_sems, recv_sems = refs[len(arrays)], refs[len(arrays) + 1]
        x, y, c = _position()
        for slot, rows in enumerate(members):
            for j, chip in enumerate(_other_chips(x, y)):
                cp = pltpu.make_async_remote_copy(
                    src_ref=_shard_rows(src[slot], exchange.sent(x, y, c, chip), rows),
                    dst_ref=_shard_rows(land[slot], exchange.arrives_at(chip, c), rows),
                    send_sem=send_sems.at[3 * slot + j], recv_sem=recv_sems.at[3 * slot + j],
                    device_id=(*chip, c), device_id_type=MESH_ID)
                cp.wait_send()
                cp.wait_recv()

    out = pl.pallas_call(
        body, name=name, out_shape=_hbm_like(arrays),
        in_specs=[HBM] * len(arrays) + [SEM, SEM, ANY], out_specs=[HBM] * len(arrays),
        input_output_aliases={i: i for i in range(len(arrays))},
        compiler_params=pltpu.CompilerParams(has_side_effects=DATAFLOW),
    )(*arrays, sems[0], sems[1], after)
    return list(out[first_land:])


def _d2d_gather(buffers, rows, name):
    n = len(buffers)

    def body(*refs):
        land = refs[n:2 * n]
        send_sems, recv_sems = refs[2 * n:]
        x, y, c = _position()
        chips = [(x, y)] + _other_chips(x, y)
        sends, recvs = [], []
        for k in range(n):
            for j, chip in enumerate(chips):
                for copies, core in ((sends, c), (recvs, 1 - c)):
                    block = _shard_rows(land[k], _device_index(chip, core), rows[k])
                    copies.append(pltpu.make_async_remote_copy(
                        src_ref=block, dst_ref=block, send_sem=send_sems.at[k, j], recv_sem=recv_sems.at[k, j],
                        device_id=(x, y, 1 - c), device_id_type=MESH_ID))
        for cp in sends:
            cp.start()
        for cp in recvs:
            cp.wait_recv()
        for cp in sends:
            cp.wait_send()

    return pl.pallas_call(
        body, name=name, out_shape=[jax.ShapeDtypeStruct(a.shape, a.dtype) for a in buffers],
        in_specs=[ANY] * n, out_specs=[ANY] * n, input_output_aliases={i: i for i in range(n)},
        scratch_shapes=[pltpu.SemaphoreType.DMA((n, 4)), pltpu.SemaphoreType.DMA((n, 4))],
    )(*buffers)


def _rs_pair(grads, name):
    n = len(grads)
    rows = [g.shape[0] // N_DEV for g in grads]

    def body(*refs):
        ins, outs = refs[:n], refs[n:2 * n]
        send_sems, recv_sems = refs[2 * n:]
        x, y, c = _position()
        copies = []
        for k in range(n):
            for q in range(4):
                copies.append(pltpu.make_async_remote_copy(
                    src_ref=_shard_rows(ins[k], 2 * q + 1 - c, rows[k]), dst_ref=_shard_rows(outs[k], q, rows[k]),
                    send_sem=send_sems.at[k, q], recv_sem=recv_sems.at[k, q], device_id=(x, y, 1 - c),
                    device_id_type=MESH_ID))
        for cp in copies:
            cp.start()
        for cp in copies:
            cp.wait()

    return pl.pallas_call(
        body, out_shape=[jax.ShapeDtypeStruct((4 * r, g.shape[1]), g.dtype) for g, r in zip(grads, rows)],
        in_specs=[ANY] * n, out_specs=[ANY] * n,
        scratch_shapes=[pltpu.SemaphoreType.DMA((n, 4)), pltpu.SemaphoreType.DMA((n, 4))],
        name=name)(*grads)


def _pair_add(grad, received, place, name):
    r = received.shape[0] // 4
    tr = 352 if r % 352 == 0 else r
    per = r // tr

    def body(place_ref, g_ref, r_ref, o_ref, land_ref):
        total = (g_ref[...].astype(F32) + r_ref[...].astype(F32)).astype(BF)
        o_ref[...] = total

        @pl.when(pl.program_id(1) == place_ref[1])
        def _():
            land_ref[...] = total

    return pl.pallas_call(
        body,
        grid_spec=pltpu.PrefetchScalarGridSpec(
            num_scalar_prefetch=1, grid=(per, 4),
            in_specs=[pl.BlockSpec((tr, D), lambda i, q, p: ((2 * q + p[0]) * per + i, 0)),
                      pl.BlockSpec((tr, D), lambda i, q, p: (q * per + i, 0))],
            out_specs=[pl.BlockSpec((tr, D), lambda i, q, p: (q * per + i, 0)),
                       pl.BlockSpec((tr, D), lambda i, q, p: (p[1] * per + i, 0))]),
        out_shape=[jax.ShapeDtypeStruct(received.shape, BF)] * 2,
        compiler_params=_params(2), name=name)(place, grad, received)


def _all_reduce_small(payload, deps=()):
    r = payload.shape[0]

    def body(in_ref, out_ref, land_ref, send_sems, recv_sems):
        x, y, c = _position()
        me = 4 * x + 2 * y + c
        land_ref[me] = in_ref[...]
        copies = []
        for k in range(1, N_DEV):
            peer = (x ^ (k >> 2), y ^ ((k >> 1) & 1), c ^ (k & 1))
            copies.append(pltpu.make_async_remote_copy(
                src_ref=in_ref, dst_ref=land_ref.at[me], send_sem=send_sems.at[k - 1], recv_sem=recv_sems.at[k - 1],
                device_id=peer, device_id_type=MESH_ID))
        for cp in copies:
            cp.start()
        for k in range(1, N_DEV):
            peer_index = me ^ k
            pltpu.make_async_remote_copy(
                src_ref=in_ref, dst_ref=land_ref.at[peer_index], send_sem=send_sems.at[k - 1], recv_sem=recv_sems.at[k - 1],
                device_id=(x, y, c), device_id_type=MESH_ID).wait_recv()
        for cp in copies:
            cp.wait_send()
        acc = land_ref[0]
        for d in range(1, N_DEV):
            acc = acc + land_ref[d]
        out_ref[...] = acc

    return _call(
        body, deps, (payload,), out_shape=jax.ShapeDtypeStruct((r, D), F32),
        in_specs=[pl.BlockSpec(memory_space=pltpu.VMEM)], out_specs=pl.BlockSpec(memory_space=pltpu.VMEM),
        scratch_shapes=[pltpu.VMEM((N_DEV, r, D), F32), pltpu.SemaphoreType.DMA((N_DEV - 1,)),
                        pltpu.SemaphoreType.DMA((N_DEV - 1,))],
        name="all_reduce_small")


def _adamw_math(w, g, m, v):
    m = ADAM_B1 * m + (1.0 - ADAM_B1) * g
    v = ADAM_B2 * v + (1.0 - ADAM_B2) * (g * g)
    m_hat = m / (1.0 - ADAM_B1 ** ADAM_STEP)
    v_hat = v / (1.0 - ADAM_B2 ** ADAM_STEP)
    delta = -ADAM_LR * (m_hat / (jnp.sqrt(v_hat) + ADAM_EPS) + ADAM_WD * w)
    return delta, m, v


def _sum_partials(blocks):
    g = blocks[0].astype(F32)
    for blk in blocks[1:]:
        g = g + blk.astype(F32)
    return g


def _reduce_adamw(landed, w, m, v, name):
    r = w.shape[0]
    tr = 176 if r % 176 == 0 else r
    per = r // tr

    def body(r0, r1, r2, r3, w_ref, m_ref, v_ref, g_ref, d_ref, nm_ref, nv_ref):
        g = _sum_partials([r0[...], r1[...], r2[...], r3[...]])
        g_ref[...] = g
        d_ref[...], nm_ref[...], nv_ref[...] = _adamw_math(w_ref[...], g, m_ref[...], v_ref[...])

    tile = _row_tile(tr, D)
    return pl.pallas_call(
        body, grid=(per,),
        in_specs=[pl.BlockSpec((tr, D), lambda i, q=q: (q * per + i, 0)) for q in range(4)] + [tile] * 3,
        out_specs=[tile] * 4, out_shape=[jax.ShapeDtypeStruct(w.shape, F32)] * 4,
        compiler_params=_params(1), name=name)(landed, landed, landed, landed, w, m, v)


def _adamw_small(w, g, m, v, name):
    def body(w_ref, g_ref, m_ref, v_ref, d_ref, nm_ref, nv_ref):
        d_ref[...], nm_ref[...], nv_ref[...] = _adamw_math(w_ref[...], g_ref[...], m_ref[...], v_ref[...])

    return pl.pallas_call(body, out_shape=[jax.ShapeDtypeStruct(w.shape, F32)] * 3, name=name)(w, g, m, v)


WEIGHTS = ("ffn1_norm", "ffn1_w_in", "ffn1_w_out", "mix_norm", "w_in", "conv_dw_kernel", "conv_dw_bias", "conv_ln_g",
           "conv_ln_b", "conv_w_proj", "q_norm", "k_norm", "attn_sinks", "rel_bias", "attn_w_o", "w_out", "ffn2_norm",
           "ffn2_w_in", "ffn2_w_out")
MATRICES = ("ffn1_w_in", "ffn1_w_out", "w_in", "conv_w_proj", "attn_w_o", "w_out", "ffn2_w_in", "ffn2_w_out")
COLUMN_SHARDED = ("ffn1_w_in", "w_in", "ffn2_w_in")
ROW_VECTORS = ("ffn1_norm", "mix_norm", "conv_dw_bias", "conv_ln_g", "conv_ln_b", "ffn2_norm")
PACKED = (("q_norm", HD), ("k_norm", HD), ("attn_sinks", NQ), ("rel_bias", NBUCKET * NQ))
GATHER = _Exchange(gather=True)
SCATTER = _Exchange(gather=False)
GATHER_STAGES = ("ffn1", "mix", "ffn2")
STAGE_MEMBERS = {"ffn1": ("ffn1_w_in", "ffn1_w_out"), "mix": ("w_in", "conv_w_proj", "attn_w_o", "w_out", "taps"),
                 "ffn2": ("ffn2_w_in", "ffn2_w_out")}
ROW_PACKED = len(ROW_VECTORS)
ROW_LOSS = ROW_PACKED + 1
ROW_TAPS = 8
PAYLOAD_ROWS = ROW_TAPS + CWP


def _pack_small(values, last_row):
    packed = jnp.concatenate([values[k].reshape(-1) for k, _ in PACKED])
    packed = jnp.pad(packed, (0, D - packed.shape[0])).reshape(1, D)
    return jnp.concatenate([values[k].reshape(1, D) for k in ROW_VECTORS] + [packed, last_row], axis=0)


def _unpack_small(rows):
    out = {k: rows[i] for i, k in enumerate(ROW_VECTORS)}
    at = 0
    for k, size in PACKED:
        out[k] = rows[ROW_PACKED, at:at + size]
        at += size
    out["rel_bias"] = out["rel_bias"].reshape(NBUCKET, NQ)
    return out


def kernel(x, ffn1_norm, ffn1_w_in, ffn1_w_out, mix_norm, w_in, conv_dw_kernel, conv_dw_bias, conv_ln_g, conv_ln_b, conv_w_proj, q_norm, k_norm, attn_sinks, rel_bias, attn_w_o, w_out, ffn2_norm, ffn2_w_in, ffn2_w_out, loss_target, m_ffn1_norm, m_ffn1_w_in, m_ffn1_w_out, m_mix_norm, m_w_in, m_conv_dw_kernel, m_conv_dw_bias, m_conv_ln_g, m_conv_ln_b, m_conv_w_proj, m_q_norm, m_k_norm, m_attn_sinks, m_rel_bias, m_attn_w_o, m_w_out, m_ffn2_norm, m_ffn2_w_in, m_ffn2_w_out, v_ffn1_norm, v_ffn1_w_in, v_ffn1_w_out, v_mix_norm, v_w_in, v_conv_dw_kernel, v_conv_dw_bias, v_conv_ln_g, v_conv_ln_b, v_conv_w_proj, v_q_norm, v_k_norm, v_attn_sinks, v_rel_bias, v_attn_w_o, v_w_out, v_ffn2_norm, v_ffn2_w_in, v_ffn2_w_out):
    w = dict(ffn1_norm=ffn1_norm, ffn1_w_in=ffn1_w_in, ffn1_w_out=ffn1_w_out, mix_norm=mix_norm, w_in=w_in,
             conv_dw_kernel=conv_dw_kernel, conv_dw_bias=conv_dw_bias, conv_ln_g=conv_ln_g, conv_ln_b=conv_ln_b,
             conv_w_proj=conv_w_proj, q_norm=q_norm, k_norm=k_norm, attn_sinks=attn_sinks, rel_bias=rel_bias,
             attn_w_o=attn_w_o, w_out=w_out, ffn2_norm=ffn2_norm, ffn2_w_in=ffn2_w_in, ffn2_w_out=ffn2_w_out)
    m = dict(ffn1_norm=m_ffn1_norm, ffn1_w_in=m_ffn1_w_in, ffn1_w_out=m_ffn1_w_out, mix_norm=m_mix_norm, w_in=m_w_in,
             conv_dw_kernel=m_conv_dw_kernel, conv_dw_bias=m_conv_dw_bias, conv_ln_g=m_conv_ln_g, conv_ln_b=m_conv_ln_b,
             conv_w_proj=m_conv_w_proj, q_norm=m_q_norm, k_norm=m_k_norm, attn_sinks=m_attn_sinks, rel_bias=m_rel_bias,
             attn_w_o=m_attn_w_o, w_out=m_w_out, ffn2_norm=m_ffn2_norm, ffn2_w_in=m_ffn2_w_in, ffn2_w_out=m_ffn2_w_out)
    v = dict(ffn1_norm=v_ffn1_norm, ffn1_w_in=v_ffn1_w_in, ffn1_w_out=v_ffn1_w_out, mix_norm=v_mix_norm, w_in=v_w_in,
             conv_dw_kernel=v_conv_dw_kernel, conv_dw_bias=v_conv_dw_bias, conv_ln_g=v_conv_ln_g, conv_ln_b=v_conv_ln_b,
             conv_w_proj=v_conv_w_proj, q_norm=v_q_norm, k_norm=v_k_norm, attn_sinks=v_attn_sinks, rel_bias=v_rel_bias,
             attn_w_o=v_attn_w_o, w_out=v_w_out, ffn2_norm=v_ffn2_norm, ffn2_w_in=v_ffn2_w_in, ffn2_w_out=v_ffn2_w_out)
    px, py, pc = _position()
    me = 4 * px + 2 * py + pc
    place = jnp.stack([pc, 2 * px + py]).astype(jnp.int32)

    rows_of = lambda k, a: a.T if k in COLUMN_SHARDED else a
    buffers = dict(zip(MATRICES + ("taps",), _prep([rows_of(k, w[k]) for k in MATRICES], conv_dw_kernel,
                                                   me.astype(jnp.int32).reshape(1))))
    landings, sets = [], []
    for stage in GATHER_STAGES:
        sets.append([(len(landings) + i, buffers[k].shape[0] // N_DEV) for i, k in enumerate(STAGE_MEMBERS[stage])])
        landings += [buffers[k] for k in STAGE_MEMBERS[stage]]
    sems, _, land_thru, _ = _ici_copies_start(sets, None, landings, GATHER, "gather_start")

    def weights_of(stage, after):
        s = GATHER_STAGES.index(stage)
        rows = [r for _, r in sets[s]]
        landed = _ici_copies_wait(sems[s], rows, None, [land_thru[k] for k, _ in sets[s]], GATHER, after,
                                  "gather_wait_" + stage)
        out = dict(zip(STAGE_MEMBERS[stage], _d2d_gather(landed, rows, "gather_d2d_" + stage)))
        if "taps" in out:
            taps = out.pop("taps")
            out["conv_dw_kernel"] = jnp.transpose(taps.reshape(N_DEV, CWP, BLK), (1, 0, 2)).reshape(CWP, D)[:CW]
        return out

    in_flight = []

    def grads_done(stage, grads):
        names = list(grads)
        received = _rs_pair([grads[k] for k in names], "rs_pair_" + stage)
        added = [_pair_add(grads[k], r, place, "pair_add_" + k) for k, r in zip(names, received)]
        partials = [p for p, _ in added]
        members = [(i, p.shape[0] // 4) for i, p in enumerate(partials)]
        sem, p_thru, l_thru, token = _ici_copies_start([members], partials, [l for _, l in added], SCATTER,
                                                       "scatter_start_" + stage)
        in_flight.append((stage, names, sem[0], p_thru, l_thru, token))
        return [token]

    reduced = []

    def small_done(gv, sq):
        payload = jnp.concatenate([_pack_small(gv, sq), jnp.pad(gv["conv_dw_kernel"], ((0, CWP - CW), (0, 0)))], axis=0)
        reduced.append(_all_reduce_small(payload))
        return reduced

    vec = {k: w[k] for k in WEIGHTS if k not in MATRICES and k != "conv_dw_kernel"}
    dx0 = _local_step(x[0], loss_target[0], vec, weights_of, grads_done, small_done)
    total = reduced[0]
    loss = (0.5 / D) * jnp.sum(total[ROW_LOSS])

    grads, delta, new_m, new_v = {}, {}, {}, {}
    after = in_flight[-1][-1]
    for stage, names, sem, p_thru, l_thru, _ in in_flight:
        landed = _ici_copies_wait(sem, [p.shape[0] // 4 for p in p_thru], p_thru, l_thru, SCATTER, after,
                                  "scatter_wait_" + stage)
        for k, buf in zip(names, landed):
            out = _reduce_adamw(buf, rows_of(k, w[k]), rows_of(k, m[k]), rows_of(k, v[k]), "adamw_" + k)
            grads[k], delta[k], new_m[k], new_v[k] = [rows_of(k, a) for a in out]
            after = out[1]
    zero_row = jnp.zeros((1, D), F32)
    d8, m8, v8 = _adamw_small(_pack_small(w, zero_row), total[:ROW_TAPS], _pack_small(m, zero_row),
                              _pack_small(v, zero_row), "adamw_small")
    grads.update(_unpack_small(total[:ROW_TAPS]))
    delta.update(_unpack_small(d8))
    new_m.update(_unpack_small(m8))
    new_v.update(_unpack_small(v8))
    k = "conv_dw_kernel"
    grads[k] = lax.dynamic_slice_in_dim(total[ROW_TAPS:ROW_TAPS + CW], me * BLK, BLK, axis=1)
    delta[k], new_m[k], new_v[k] = _adamw_small(w[k], grads[k], m[k], v[k], "adamw_taps")

    return (loss, dx0[None], *[grads[k] for k in WEIGHTS], *[delta[k] for k in WEIGHTS],
            *[new_m[k] for k in WEIGHTS], *[new_v[k] for k in WEIGHTS])
```

```python
import functools
import math

import numpy as np
import jax
import jax.numpy as jnp
from jax import lax
from jax.experimental import pallas as pl
from jax.experimental.pallas import tpu as pltpu

F32 = jnp.float32
BF = jnp.bfloat16

D = 1024
F = 2816
INW = 5632
CW = 31
CWP = 32
HD = 64
NQ = 16
NKV = 4
GRP = NQ // NKV
BLK = 128
NBUCKET = 32
EPS = 1e-6
NEG = float(jnp.finfo(jnp.float32).min)
QK_SCALE = 1.0 / math.sqrt(HD)
R_CONV = (0, 2048)
R_QKV = (2048, 3584)
R_Q = (2048, 3072)
R_KV = (3072, 3584)
R_GATE = (3584, 5632)

N_DEV = 8
VMEM_LIMIT_V7X = 56 * 1024 * 1024

ADAM_LR = 0.001
ADAM_B1 = 0.9
ADAM_B2 = 0.999
ADAM_EPS = 1e-08
ADAM_WD = 0.01
ADAM_STEP = 10

NT_DIMS = (((1,), (1,)), ((), ()))
TN_DIMS = (((0,), (0,)), ((), ()))


def _dot(a, b):
    return jnp.dot(a, b, preferred_element_type=F32)


def _dot_nt(a, b):
    return lax.dot_general(a, b, NT_DIMS, preferred_element_type=F32)


def _dot_tn(a, b):
    return lax.dot_general(a, b, TN_DIMS, preferred_element_type=F32)


def _sig(x):
    return 1.0 / (1.0 + jnp.exp(-x))


ANY = pl.BlockSpec(memory_space=pl.ANY)


def _call(body, deps, args, **kw):
    n = len(deps)
    if n:
        kw["in_specs"] = [ANY] * n + list(kw["in_specs"])
        return pl.pallas_call(lambda *refs: body(*refs[n:]), **kw)(*deps, *args)
    return pl.pallas_call(body, **kw)(*args)


def _params(n_axes):
    return pltpu.CompilerParams(dimension_semantics=("arbitrary",) * n_axes, vmem_limit_bytes=VMEM_LIMIT_V7X)


def _resident(shape):
    zeros = (0,) * len(shape)
    return pl.BlockSpec(shape, lambda *_: zeros, pipeline_mode=pl.Buffered(1))


def _row_tile(rows, cols):
    return pl.BlockSpec((rows, cols), lambda i: (i, 0))


def _rms_stats(x):
    r = lax.rsqrt(jnp.mean(x * x, axis=-1, keepdims=True) + EPS)
    return r, x * r


def _rms_bwd(dn, x, g):
    r, xh = _rms_stats(x)
    dxh = dn * g
    dx = r * (dxh - xh * jnp.mean(dxh * xh, axis=-1, keepdims=True))
    return dx, jnp.sum(dn * xh, axis=0, keepdims=True)


def _ffn_fwd(x, g, w_in_t, w_out, name, target=None):
    t = x.shape[0]
    tm = min(256, t)
    with_loss = target is not None

    def body(*refs):
        if with_loss:
            x_ref, g_ref, w_ref, wo_ref, t_ref, n_ref, u_ref, dy_ref, sq_ref = refs
        else:
            x_ref, g_ref, w_ref, wo_ref, n_ref, u_ref, xo_ref = refs
        x = x_ref[...]
        r, xh = _rms_stats(x)
        n = (xh * g_ref[...]).astype(BF)
        n_ref[...] = n
        u = _dot_nt(n, w_ref[...])
        u_ref[...] = u.astype(BF)
        a = u[:, :F]
        b = u[:, F:]
        h = (a * _sig(a) * b).astype(BF)
        xo = x + 0.5 * _dot(h, wo_ref[...])
        if with_loss:
            err = xo - t_ref[...]
            dy_ref[...] = err * (1.0 / D)

            @pl.when(pl.program_id(0) == 0)
            def _():
                sq_ref[...] = jnp.zeros_like(sq_ref)

            sq_ref[...] += jnp.sum(err * err, axis=0, keepdims=True)
        else:
            xo_ref[...] = xo

    in_specs = [_row_tile(tm, D), _resident((1, D)), _resident((INW, D)), _resident((F, D))]
    args = [x, g, w_in_t, w_out]
    out_specs = [_row_tile(tm, D), _row_tile(tm, INW), _row_tile(tm, D)]
    out_shape = [jax.ShapeDtypeStruct((t, D), BF), jax.ShapeDtypeStruct((t, INW), BF), jax.ShapeDtypeStruct((t, D), F32)]
    if with_loss:
        in_specs.append(_row_tile(tm, D))
        args.append(target)
        out_specs.append(pl.BlockSpec((1, D), lambda i: (0, 0)))
        out_shape.append(jax.ShapeDtypeStruct((1, D), F32))
    return pl.pallas_call(body, grid=(t // tm,), in_specs=in_specs, out_specs=out_specs, out_shape=out_shape,
                          compiler_params=_params(1), name=name)(*args)


def _ffn_bwd(dxo, x, g, u, w_in_t, w_out, name, deps=()):
    t = x.shape[0]
    tm = min(256, t)

    def body(dxo_ref, x_ref, g_ref, u_ref, w_ref, wo_ref, dx_ref, du_ref, h_ref, dy_ref, dg_ref):
        dxo = dxo_ref[...]
        dy = (0.5 * dxo).astype(BF)
        dy_ref[...] = dy
        dh = _dot_nt(dy, wo_ref[...])
        a = u_ref[:, :F].astype(F32)
        b = u_ref[:, F:].astype(F32)
        s = _sig(a)
        sa = a * s
        h_ref[...] = (sa * b).astype(BF)
        du_ref[:, :F] = (dh * b * (s * (1.0 + a * (1.0 - s)))).astype(BF)
        du_ref[:, F:] = (dh * sa).astype(BF)
        dn = _dot(du_ref[...], w_ref[...])
        dx, dg = _rms_bwd(dn, x_ref[...], g_ref[...])
        dx_ref[...] = dxo + dx

        @pl.when(pl.program_id(0) == 0)
        def _():
            dg_ref[...] = jnp.zeros_like(dg_ref)

        dg_ref[...] += dg

    return _call(
        body, deps, (dxo, x, g, u, w_in_t, w_out), grid=(t // tm,),
        in_specs=[_row_tile(tm, D), _row_tile(tm, D), _resident((1, D)), _row_tile(tm, INW), _resident((INW, D)),
                  _resident((F, D))],
        out_specs=[_row_tile(tm, D), _row_tile(tm, INW), _row_tile(tm, F), _row_tile(tm, D),
                   pl.BlockSpec((1, D), lambda i: (0, 0))],
        out_shape=[jax.ShapeDtypeStruct((t, D), F32), jax.ShapeDtypeStruct((t, INW), BF), jax.ShapeDtypeStruct((t, F), BF),
                   jax.ShapeDtypeStruct((t, D), BF), jax.ShapeDtypeStruct((1, D), F32)],
        compiler_params=_params(1), name=name)


def _wgrad(lhs, rhs, name, *, lhs_is_transposed, chunk, deps=()):
    t = rhs.shape[0]
    n = lhs.shape[0] if lhs_is_transposed else lhs.shape[1]
    c = min(chunk, n)

    def body(l_ref, r_ref, o_ref):
        if lhs_is_transposed:
            o_ref[...] = _dot(l_ref[...], r_ref[...]).astype(BF)
        else:
            o_ref[...] = _dot_tn(l_ref[...], r_ref[...]).astype(BF)

    lhs_spec = pl.BlockSpec((c, t), lambda j: (j, 0)) if lhs_is_transposed else pl.BlockSpec((t, c), lambda j: (0, j))
    return _call(
        body, deps, (lhs, rhs), grid=(n // c,),
        in_specs=[lhs_spec, _resident((t, D))],
        out_specs=pl.BlockSpec((c, D), lambda j: (j, 0)),
        out_shape=jax.ShapeDtypeStruct((n, D), BF),
        compiler_params=_params(1), name=name)


def _wgrad_mix(duc, dq_t, dkv_t, dgp, hm):
    t = hm.shape[0]
    c = 512
    first_q, first_kv, first_gate = R_Q[0] // c, R_KV[0] // c, R_GATE[0] // c

    def body(uc_ref, q_ref, kv_ref, gp_ref, h_ref, o_ref):
        j = pl.program_id(0)

        @pl.when(j < first_q)
        def _():
            o_ref[...] = _dot_tn(uc_ref[...], h_ref[...]).astype(BF)

        @pl.when((j >= first_q) & (j < first_kv))
        def _():
            o_ref[...] = _dot(q_ref[...], h_ref[...]).astype(BF)

        @pl.when((j >= first_kv) & (j < first_gate))
        def _():
            o_ref[...] = _dot(kv_ref[...], h_ref[...]).astype(BF)

        @pl.when(j >= first_gate)
        def _():
            o_ref[...] = _dot_tn(gp_ref[...], h_ref[...]).astype(BF)

    return pl.pallas_call(
        body, grid=(INW // c,),
        in_specs=[pl.BlockSpec((t, c), lambda j: (0, jnp.clip(j, 0, first_q - 1))),
                  pl.BlockSpec((c, t), lambda j: (jnp.clip(j - first_q, 0, first_kv - first_q - 1), 0)),
                  pl.BlockSpec((c, t), lambda j: (jnp.clip(j - first_kv, 0, first_gate - first_kv - 1), 0)),
                  pl.BlockSpec((t, c), lambda j: (0, jnp.clip(j - first_gate, 0, INW // c - first_gate - 1))),
                  _resident((t, D))],
        out_specs=pl.BlockSpec((c, D), lambda j: (j, 0)),
        out_shape=jax.ShapeDtypeStruct((INW, D), BF),
        compiler_params=_params(1), name="mix_dw_in")(duc, dq_t, dkv_t, dgp, hm)


def _mix_proj(x, g, w_t):
    t = x.shape[0]
    tm = min(256, t)

    def body(x_ref, g_ref, w_ref, hm_ref, uc_ref, gp_ref, qkv_ref):
        r, xh = _rms_stats(x_ref[...])
        hm = (xh * g_ref[...]).astype(BF)
        hm_ref[...] = hm
        uc_ref[...] = _dot_nt(hm, w_ref[R_CONV[0]:R_CONV[1], :]).astype(BF)
        gp_ref[...] = _dot_nt(hm, w_ref[R_GATE[0]:R_GATE[1], :]).astype(BF)
        qkv_ref[...] = _dot_nt(w_ref[R_QKV[0]:R_QKV[1], :], hm).astype(BF)

    return pl.pallas_call(
        body, grid=(t // tm,),
        in_specs=[_row_tile(tm, D), _resident((1, D)), _resident((INW, D))],
        out_specs=[_row_tile(tm, D), _row_tile(tm, 2 * D), _row_tile(tm, 2 * D), pl.BlockSpec((1536, tm), lambda i: (0, i))],
        out_shape=[jax.ShapeDtypeStruct((t, D), BF), jax.ShapeDtypeStruct((t, 2 * D), BF),
                   jax.ShapeDtypeStruct((t, 2 * D), BF), jax.ShapeDtypeStruct((1536, t), BF)],
        compiler_params=_params(1), name="mix_proj")(x, g, w_t)


CONV_HALO = 32
CONV_LEAD = CONV_HALO - (CW - 1)


def _glu(uc):
    uc = uc.astype(F32)
    return uc[:, :D] * _sig(uc[:, D:])


def _ln_stats(zc):
    mu = jnp.mean(zc, axis=-1, keepdims=True)
    zm = zc - mu
    r = lax.rsqrt(jnp.mean(zm * zm, axis=-1, keepdims=True) + EPS)
    return r, zm * r


CONV_SHIFTS = 8
CONV_CHUNK = 32


def _store_shifted(buf, rows):
    for b in range(1, CONV_SHIFTS):
        buf[b, 0:rows - 8, :] = buf[0, pl.ds(b, rows - 8), :]


def _conv_fwd(uc, dwk, dwb, lng, lnb):
    t = uc.shape[0]
    tm = min(512, t)
    per = tm // CONV_HALO
    ext = tm + CONV_HALO

    def body(cur_ref, prev_ref, k_ref, kb_ref, g_ref, b_ref, o_ref, zc_ref, zsh):
        i = pl.program_id(0)
        zsh[0, 0:CONV_HALO, :] = _glu(prev_ref[...]) * (i > 0).astype(F32)
        zsh[0, CONV_HALO:, :] = _glu(cur_ref[...])
        _store_shifted(zsh, ext)

        def chunk(ci, carry):
            r0 = pl.multiple_of(ci * CONV_CHUNK, CONV_CHUNK)
            acc = jnp.zeros((CONV_CHUNK, D), F32) + kb_ref[...]
            for w in range(CW):
                a, b = divmod(CONV_LEAD + w, 8)
                acc = acc + k_ref[w:w + 1, :] * zsh[b, pl.ds(r0 + 8 * a, CONV_CHUNK), :]
            zc_ref[pl.ds(r0, CONV_CHUNK), :] = acc
            r, xh = _ln_stats(acc)
            y = xh * g_ref[...] + b_ref[...]
            o_ref[pl.ds(r0, CONV_CHUNK), :] = (y * _sig(y)).astype(BF)
            return carry

        lax.fori_loop(0, tm // CONV_CHUNK, chunk, 0)

    return pl.pallas_call(
        body, grid=(t // tm,),
        in_specs=[_row_tile(tm, 2 * D),
                  pl.BlockSpec((CONV_HALO, 2 * D), lambda i: (jnp.maximum(i * per - 1, 0), 0)),
                  _resident((CWP, D)), _resident((1, D)), _resident((1, D)), _resident((1, D))],
        out_specs=[_row_tile(tm, D), _row_tile(tm, D)],
        out_shape=[jax.ShapeDtypeStruct((t, D), BF), jax.ShapeDtypeStruct((t, D), F32)],
        scratch_shapes=[pltpu.VMEM((CONV_SHIFTS, ext, D), F32)],
        compiler_params=_params(1), name="conv_fwd")(uc, uc, dwk, dwb, lng, lnb)


def _conv_bwd(uc, zc, dzs, dwk, lng, lnb):
    t = uc.shape[0]
    tm = min(256, t)
    per = tm // CONV_HALO
    n_tiles = t // tm
    ext = tm + CONV_HALO
    last_block = t // CONV_HALO - 1

    def body(cur_ref, prev_ref, zc_ref, zcn_ref, dz_ref, dzn_ref, k_ref, g_ref, b_ref,
             duc_ref, dk_ref, dkb_ref, dg_ref, db_ref, zsh, dsh, dk8):
        i = pl.program_id(0)

        @pl.when(i == 0)
        def _():
            dk8[...] = jnp.zeros_like(dk8)
            dkb_ref[...] = jnp.zeros_like(dkb_ref)
            dg_ref[...] = jnp.zeros_like(dg_ref)
            db_ref[...] = jnp.zeros_like(db_ref)

        has_next = (i < n_tiles - 1).astype(F32)
        zsh[0, 0:CONV_HALO, :] = _glu(prev_ref[...]) * (i > 0).astype(F32)
        zsh[0, CONV_HALO:, :] = _glu(cur_ref[...])
        _store_shifted(zsh, ext)
        gain = g_ref[...]

        def ln_silu_bwd(zc, dzs, live):
            r, xh = _ln_stats(zc)
            y = xh * gain + b_ref[...]
            sy = _sig(y)
            dy = dzs * (sy * (1.0 + y * (1.0 - sy))) * live
            dxh = dy * gain
            dzc = r * (dxh - jnp.mean(dxh, axis=-1, keepdims=True) - xh * jnp.mean(dxh * xh, axis=-1, keepdims=True))
            return dzc, dy, xh

        dzc, dy, xh = ln_silu_bwd(zc_ref[...], dz_ref[...], 1.0)
        dsh[0, 0:tm, :] = dzc
        dg_ref[...] += jnp.sum(dy * xh, axis=0, keepdims=True)
        db_ref[...] += jnp.sum(dy, axis=0, keepdims=True)
        dkb_ref[...] += jnp.sum(dzc, axis=0, keepdims=True)
        dsh[0, tm:, :] = ln_silu_bwd(zcn_ref[...], dzn_ref[...], has_next)[0]
        _store_shifted(dsh, ext)

        def chunk(ci, carry):
            r0 = pl.multiple_of(ci * CONV_CHUNK, CONV_CHUNK)
            dzc_c = dsh[0, pl.ds(r0, CONV_CHUNK), :]
            dz = jnp.zeros((CONV_CHUNK, D), F32)
            for w in range(CW):
                a, b = divmod(CW - 1 - w, 8)
                dz = dz + k_ref[w:w + 1, :] * dsh[b, pl.ds(r0 + 8 * a, CONV_CHUNK), :]
                a, b = divmod(CONV_LEAD + w, 8)
                prod = dzc_c * zsh[b, pl.ds(r0 + 8 * a, CONV_CHUNK), :]
                part = prod[0:8, :]
                for j in range(1, CONV_CHUNK // 8):
                    part = part + prod[8 * j:8 * j + 8, :]
                dk8[w] += part
            ucc = cur_ref[pl.ds(r0, CONV_CHUNK), :].astype(F32)
            sg = _sig(ucc[:, D:])
            duc_ref[pl.ds(r0, CONV_CHUNK), 0:D] = (dz * sg).astype(BF)
            duc_ref[pl.ds(r0, CONV_CHUNK), D:2 * D] = (dz * ucc[:, :D] * sg * (1.0 - sg)).astype(BF)
            return carry

        lax.fori_loop(0, tm // CONV_CHUNK, chunk, 0)

        @pl.when(i == n_tiles - 1)
        def _():
            dk_ref[...] = jnp.sum(dk8[...], axis=1)

    vec = pl.BlockSpec((1, D), lambda i: (0, 0))
    next_halo = pl.BlockSpec((CONV_HALO, D), lambda i: (jnp.minimum((i + 1) * per, last_block), 0))
    return pl.pallas_call(
        body, grid=(n_tiles,),
        in_specs=[_row_tile(tm, 2 * D),
                  pl.BlockSpec((CONV_HALO, 2 * D), lambda i: (jnp.maximum(i * per - 1, 0), 0)),
                  _row_tile(tm, D), next_halo, _row_tile(tm, D), next_halo,
                  _resident((CWP, D)), _resident((1, D)), _resident((1, D))],
        out_specs=[_row_tile(tm, 2 * D), pl.BlockSpec((CWP, D), lambda i: (0, 0)), vec, vec, vec],
        out_shape=[jax.ShapeDtypeStruct((t, 2 * D), BF), jax.ShapeDtypeStruct((CWP, D), F32),
                   jax.ShapeDtypeStruct((1, D), F32), jax.ShapeDtypeStruct((1, D), F32), jax.ShapeDtypeStruct((1, D), F32)],
        scratch_shapes=[pltpu.VMEM((CONV_SHIFTS, ext, D), F32), pltpu.VMEM((CONV_SHIFTS, ext, D), F32),
                        pltpu.VMEM((CWP, 8, D), F32)],
        compiler_params=_params(1), name="conv_bwd")(uc, uc, zc, zc, dzs, dzs, dwk, lng, lnb)


def _norm_rows(xt, g):
    r = lax.rsqrt(jnp.mean(xt * xt, axis=0, keepdims=True) + EPS)
    xh = xt * r
    return xh * g, r, xh


ATT_TQ = 512


def _attn_specs(t, tq):
    per = tq // BLK
    return [pl.BlockSpec((1536, tq), lambda i: (0, i)),
            pl.BlockSpec((512, BLK), lambda i: (2, jnp.maximum(i * per - 1, 0))),
            _resident((HD, 1)), _resident((HD, 1)), _resident((NKV, 1, GRP * BLK)),
            _resident((NKV, 2 * BLK, GRP * BLK)), _resident((2, 2 * BLK, GRP * BLK))]


def _attn_window(hk, sb, qkv_ref, halo_ref, kn_cur, kn_halo):
    v0 = D + NKV * HD + hk * HD
    if sb == 0:
        k_prev = kn_halo[hk]
        v_prev = halo_ref[NKV * HD + hk * HD:NKV * HD + (hk + 1) * HD, :]
    else:
        k_prev = kn_cur[hk][:, (sb - 1) * BLK:sb * BLK]
        v_prev = qkv_ref[v0:v0 + HD, (sb - 1) * BLK:sb * BLK]
    kw = jnp.concatenate([k_prev, kn_cur[hk][:, sb * BLK:(sb + 1) * BLK]], axis=1).astype(BF)
    vw = jnp.concatenate([v_prev, qkv_ref[v0:v0 + HD, sb * BLK:(sb + 1) * BLK]], axis=1)
    return kw, vw


def _attn_probs(kw, qc, bias, mask, sink):
    st = _dot_tn(kw, qc) * QK_SCALE + bias
    st = jnp.where(mask > 0.5, st, NEG)
    m = jnp.maximum(jnp.max(st, axis=0, keepdims=True), sink)
    p = jnp.exp(st - m)
    e_sink = jnp.exp(sink - m)
    inv = 1.0 / (jnp.sum(p, axis=0, keepdims=True) + e_sink)
    return p * inv, e_sink * inv


def _attn_fwd(qkv_t, qg, kg, sink_rows, bias_t, mask_t):
    t = qkv_t.shape[1]
    tq = min(ATT_TQ, t)
    n_sub = tq // BLK

    def body(qkv_ref, halo_ref, qg_ref, kg_ref, sink_ref, bias_ref, mask_ref, o_ref):
        i = pl.program_id(0)
        first = (i == 0).astype(jnp.int32)
        kgain = kg_ref[...]
        qgain = qg_ref[...]
        kn_cur = [_norm_rows(qkv_ref[D + h * HD:D + (h + 1) * HD, :].astype(F32), kgain)[0] for h in range(NKV)]
        kn_halo = [_norm_rows(halo_ref[h * HD:(h + 1) * HD, :].astype(F32), kgain)[0] for h in range(NKV)]
        for hk in range(NKV):
            for sb in range(n_sub):
                cols = slice(sb * BLK, (sb + 1) * BLK)
                kw, vw = _attn_window(hk, sb, qkv_ref, halo_ref, kn_cur, kn_halo)
                qc = jnp.concatenate(
                    [_norm_rows(qkv_ref[(GRP * hk + g) * HD:(GRP * hk + g + 1) * HD, cols].astype(F32), qgain)[0]
                     for g in range(GRP)], axis=1).astype(BF)
                mask = mask_ref[first] if sb == 0 else mask_ref[0]
                p, _ = _attn_probs(kw, qc, bias_ref[hk], mask, sink_ref[hk])
                o = _dot(vw, p.astype(BF))
                for g in range(GRP):
                    head = GRP * hk + g
                    o_ref[head * HD:(head + 1) * HD, cols] = o[:, g * BLK:(g + 1) * BLK].astype(BF)

    return pl.pallas_call(
        body, grid=(t // tq,),
        in_specs=_attn_specs(t, tq),
        out_specs=pl.BlockSpec((D, tq), lambda i: (0, i)),
        out_shape=jax.ShapeDtypeStruct((D, t), BF),
        compiler_params=_params(1), name="attn_fwd")(qkv_t, qkv_t, qg, kg, sink_rows, bias_t, mask_t)


def _attn_bwd(qkv_t, do_t, qg, kg, sink_rows, bias_t, mask_t, deps=()):
    t = qkv_t.shape[1]
    tq = min(ATT_TQ, t)
    n_sub = tq // BLK
    n_tiles = t // tq

    def body(qkv_ref, halo_ref, do_ref, qg_ref, kg_ref, sink_ref, bias_ref, mask_ref,
             dq_ref, ckv_ref, dqg_ref, dsink_ref, dsacc_ref, qg_scr):
        i = pl.program_id(0)

        @pl.when(i == 0)
        def _():
            qg_scr[...] = jnp.zeros_like(qg_scr)
            dsink_ref[...] = jnp.zeros_like(dsink_ref)
            dsacc_ref[...] = jnp.zeros_like(dsacc_ref)

        first = (i == 0).astype(jnp.int32)
        kgain = kg_ref[...]
        qgain = qg_ref[...]
        kn_cur = [_norm_rows(qkv_ref[D + h * HD:D + (h + 1) * HD, :].astype(F32), kgain)[0] for h in range(NKV)]
        kn_halo = [_norm_rows(halo_ref[h * HD:(h + 1) * HD, :].astype(F32), kgain)[0] for h in range(NKV)]
        dqg = jnp.zeros((HD, BLK), F32)
        for hk in range(NKV):
            for sb in range(n_sub):
                cols = slice(sb * BLK, (sb + 1) * BLK)
                kw, vw = _attn_window(hk, sb, qkv_ref, halo_ref, kn_cur, kn_halo)
                qn, qr, qh = [], [], []
                for g in range(GRP):
                    head = GRP * hk + g
                    n_, r_, h_ = _norm_rows(qkv_ref[head * HD:(head + 1) * HD, cols].astype(F32), qgain)
                    qn.append(n_)
                    qr.append(r_)
                    qh.append(h_)
                qc = jnp.concatenate(qn, axis=1).astype(BF)
                mask = mask_ref[first] if sb == 0 else mask_ref[0]
                p, p_sink = _attn_probs(kw, qc, bias_ref[hk], mask, sink_ref[hk])
                doc = jnp.concatenate([do_ref[(GRP * hk + g) * HD:(GRP * hk + g + 1) * HD, cols] for g in range(GRP)], axis=1)
                dp = _dot_tn(vw, doc)
                delta = jnp.sum(p * dp, axis=0, keepdims=True)
                ds = p * (dp - delta)
                dsink_ref[hk] += -(p_sink * delta)
                dsacc_ref[hk] += ds
                dsb = ds.astype(BF)
                dqc = _dot(kw, dsb) * QK_SCALE
                ckv_ref[sb, hk * HD:(hk + 1) * HD, :] = _dot_nt(qc, dsb) * QK_SCALE
                ckv_ref[sb, NKV * HD + hk * HD:NKV * HD + (hk + 1) * HD, :] = _dot_nt(doc, p.astype(BF))
                for g in range(GRP):
                    head = GRP * hk + g
                    dqn = dqc[:, g * BLK:(g + 1) * BLK]
                    dqh = dqn * qgain
                    dq = qr[g] * (dqh - qh[g] * jnp.mean(dqh * qh[g], axis=0, keepdims=True))
                    dq_ref[head * HD:(head + 1) * HD, cols] = dq.astype(BF)
                    dqg = dqg + dqn * qh[g]
        qg_scr[...] += dqg

        @pl.when(i == n_tiles - 1)
        def _():
            dqg_ref[...] = jnp.sum(qg_scr[...], axis=1, keepdims=True)

    return _call(
        body, deps, (qkv_t, qkv_t, do_t, qg, kg, sink_rows, bias_t, mask_t), grid=(n_tiles,),
        in_specs=_attn_specs(t, tq)[:2] + [pl.BlockSpec((D, tq), lambda i: (0, i))] + _attn_specs(t, tq)[2:],
        out_specs=[pl.BlockSpec((D, tq), lambda i: (0, i)),
                   pl.BlockSpec((n_sub, 2 * NKV * HD, 2 * BLK), lambda i: (i, 0, 0)),
                   pl.BlockSpec((HD, 1), lambda i: (0, 0)),
                   pl.BlockSpec((NKV, 1, GRP * BLK), lambda i: (0, 0, 0)),
                   pl.BlockSpec((NKV, 2 * BLK, GRP * BLK), lambda i: (0, 0, 0))],
        out_shape=[jax.ShapeDtypeStruct((D, t), BF),
                   jax.ShapeDtypeStruct((t // BLK, 2 * NKV * HD, 2 * BLK), F32),
                   jax.ShapeDtypeStruct((HD, 1), F32),
                   jax.ShapeDtypeStruct((NKV, 1, GRP * BLK), F32),
                   jax.ShapeDtypeStruct((NKV, 2 * BLK, GRP * BLK), F32)],
        scratch_shapes=[pltpu.VMEM((HD, BLK), F32)],
        compiler_params=_params(1), name="attn_bwd")


def _kv_combine(ckv, qkv_t, kg):
    nb = ckv.shape[0]
    t = nb * BLK
    rows = NKV * HD
    per = min(4, nb)
    steps = nb // per

    def body(c_ref, cn_ref, k_ref, kg_ref, o_ref, dkg_ref, kg_scr):
        n = pl.program_id(0)

        @pl.when(n == 0)
        def _():
            kg_scr[...] = jnp.zeros_like(kg_scr)

        has_next = (n < steps - 1).astype(F32)
        kgain = kg_ref[...]
        dkg = jnp.zeros((HD, BLK), F32)
        for s in range(per):
            cols = slice(s * BLK, (s + 1) * BLK)
            after = c_ref[s + 1, :, :BLK] if s + 1 < per else cn_ref[0, :, :BLK] * has_next
            d = c_ref[s, :, BLK:] + after
            o_ref[rows:, cols] = d[rows:, :].astype(BF)
            for h in range(NKV):
                _, r, kh = _norm_rows(k_ref[h * HD:(h + 1) * HD, cols].astype(F32), kgain)
                dkn = d[h * HD:(h + 1) * HD, :]
                dkh = dkn * kgain
                o_ref[h * HD:(h + 1) * HD, cols] = (r * (dkh - kh * jnp.mean(dkh * kh, axis=0, keepdims=True))).astype(BF)
                dkg = dkg + dkn * kh
        kg_scr[...] += dkg

        @pl.when(n == steps - 1)
        def _():
            dkg_ref[...] = jnp.sum(kg_scr[...], axis=1, keepdims=True)

    return pl.pallas_call(
        body, grid=(steps,),
        in_specs=[pl.BlockSpec((per, 2 * rows, 2 * BLK), lambda n: (n, 0, 0)),
                  pl.BlockSpec((1, 2 * rows, 2 * BLK), lambda n: (jnp.minimum((n + 1) * per, nb - 1), 0, 0)),
                  pl.BlockSpec((rows, per * BLK), lambda n: (D // rows, n)),
                  _resident((HD, 1))],
        out_specs=[pl.BlockSpec((2 * rows, per * BLK), lambda n: (0, n)), pl.BlockSpec((HD, 1), lambda n: (0, 0))],
        out_shape=[jax.ShapeDtypeStruct((2 * rows, t), BF), jax.ShapeDtypeStruct((HD, 1), F32)],
        scratch_shapes=[pltpu.VMEM((HD, BLK), F32)],
        compiler_params=_params(1), name="kv_combine")(ckv, ckv, qkv_t, kg)


def _group_lane_sums(v):
    lane_group = lax.broadcasted_iota(jnp.int32, (1, GRP * BLK), 1) // BLK
    col = lax.broadcasted_iota(jnp.int32, (1, BLK), 1)
    out = jnp.zeros((NKV, BLK), F32)
    for g in range(GRP):
        s = jnp.sum(jnp.where(lane_group == g, v, 0.0), axis=1, keepdims=True)
        out = jnp.where(col == g, s, out)
    return out


def _bias_grad(dsacc, onehot_t):
    per = 8

    def body(ds_ref, oh_ref, o_ref):
        for b in range(per):
            oh = jnp.concatenate([oh_ref[b]] * GRP, axis=1)
            o_ref[b] = _group_lane_sums(jnp.sum(ds_ref[...] * oh[None], axis=1))

    return pl.pallas_call(
        body, grid=(NBUCKET // per,),
        in_specs=[_resident((NKV, 2 * BLK, GRP * BLK)), pl.BlockSpec((per, 2 * BLK, BLK), lambda b: (b, 0, 0))],
        out_specs=pl.BlockSpec((per, NKV, BLK), lambda b: (b, 0, 0)),
        out_shape=jax.ShapeDtypeStruct((NBUCKET, NKV, BLK), F32),
        compiler_params=_params(1), name="bias_grad")(dsacc, onehot_t)


def _sink_grad(dsink_rows):
    def body(d_ref, o_ref):
        o_ref[...] = _group_lane_sums(d_ref[:, 0, :])

    return pl.pallas_call(body, out_shape=jax.ShapeDtypeStruct((NKV, BLK), F32), name="sink_grad")(dsink_rows)


def _mix_out(zs, o_t, gp, x, w_cp, w_o, w_out):
    t = x.shape[0]
    tm = min(256, t)

    def body(zs_ref, ot_ref, gp_ref, x_ref, wcp_ref, wo_ref, wout_ref, xo_ref, a_ref, b_ref, m_ref):
        a = _dot(zs_ref[...], wcp_ref[...])
        b = _dot_tn(ot_ref[...], wo_ref[...])
        a_ref[...] = a.astype(BF)
        b_ref[...] = b.astype(BF)
        merged = (_sig(gp_ref[:, :D].astype(F32)) * a + _sig(gp_ref[:, D:].astype(F32)) * b).astype(BF)
        m_ref[...] = merged
        xo_ref[...] = x_ref[...] + _dot(merged, wout_ref[...])

    return pl.pallas_call(
        body, grid=(t // tm,),
        in_specs=[_row_tile(tm, D), pl.BlockSpec((D, tm), lambda i: (0, i)), _row_tile(tm, 2 * D), _row_tile(tm, D),
                  _resident((D, D)), _resident((D, D)), _resident((D, D))],
        out_specs=[_row_tile(tm, D)] * 4,
        out_shape=[jax.ShapeDtypeStruct((t, D), F32)] + [jax.ShapeDtypeStruct((t, D), BF)] * 3,
        compiler_params=_params(1), name="mix_out")(zs, o_t, gp, x, w_cp, w_o, w_out)


def _mix_out_bwd(dx, a, b, gp, w_cp, w_o, w_out, deps=()):
    t = dx.shape[0]
    tm = min(256, t)

    def body(dx_ref, a_ref, b_ref, gp_ref, wcp_ref, wo_ref, wout_ref, dzs_ref, dot_ref, dgp_ref, da_ref, db_ref, dxb_ref):
        dxb = dx_ref[...].astype(BF)
        dxb_ref[...] = dxb
        dm = _dot_nt(dxb, wout_ref[...])
        gc = _sig(gp_ref[:, :D].astype(F32))
        ga = _sig(gp_ref[:, D:].astype(F32))
        da = (dm * gc).astype(BF)
        db = (dm * ga).astype(BF)
        da_ref[...] = da
        db_ref[...] = db
        dgp_ref[:, :D] = (dm * a_ref[...].astype(F32) * gc * (1.0 - gc)).astype(BF)
        dgp_ref[:, D:] = (dm * b_ref[...].astype(F32) * ga * (1.0 - ga)).astype(BF)
        dzs_ref[...] = _dot_nt(da, wcp_ref[...])
        dot_ref[...] = _dot_nt(wo_ref[...], db).astype(BF)

    return _call(
        body, deps, (dx, a, b, gp, w_cp, w_o, w_out), grid=(t // tm,),
        in_specs=[_row_tile(tm, D), _row_tile(tm, D), _row_tile(tm, D), _row_tile(tm, 2 * D),
                  _resident((D, D)), _resident((D, D)), _resident((D, D))],
        out_specs=[_row_tile(tm, D), pl.BlockSpec((D, tm), lambda i: (0, i)), _row_tile(tm, 2 * D),
                   _row_tile(tm, D), _row_tile(tm, D), _row_tile(tm, D)],
        out_shape=[jax.ShapeDtypeStruct((t, D), F32), jax.ShapeDtypeStruct((D, t), BF), jax.ShapeDtypeStruct((t, 2 * D), BF),
                   jax.ShapeDtypeStruct((t, D), BF), jax.ShapeDtypeStruct((t, D), BF), jax.ShapeDtypeStruct((t, D), BF)],
        compiler_params=_params(1), name="mix_out_bwd")


def _mix_proj_bwd(dxo, duc, dq_t, dkv_t, dgp, x, g, w_t):
    t = x.shape[0]
    tm = min(256, t)

    def body(dxo_ref, duc_ref, dq_ref, dkv_ref, dgp_ref, x_ref, g_ref, w_ref, dx_ref, dg_ref):
        dn = _dot(duc_ref[...], w_ref[R_CONV[0]:R_CONV[1], :])
        dn = dn + _dot(dgp_ref[...], w_ref[R_GATE[0]:R_GATE[1], :])
        dn = dn + _dot_tn(dq_ref[...], w_ref[R_Q[0]:R_Q[1], :])
        dn = dn + _dot_tn(dkv_ref[...], w_ref[R_KV[0]:R_KV[1], :])
        dx, dg = _rms_bwd(dn, x_ref[...], g_ref[...])
        dx_ref[...] = dxo_ref[...] + dx

        @pl.when(pl.program_id(0) == 0)
        def _():
            dg_ref[...] = jnp.zeros_like(dg_ref)

        dg_ref[...] += dg

    return pl.pallas_call(
        body, grid=(t // tm,),
        in_specs=[_row_tile(tm, D), _row_tile(tm, 2 * D), pl.BlockSpec((D, tm), lambda i: (0, i)),
                  pl.BlockSpec((2 * NKV * HD, tm), lambda i: (0, i)), _row_tile(tm, 2 * D), _row_tile(tm, D),
                  _resident((1, D)), _resident((INW, D))],
        out_specs=[_row_tile(tm, D), pl.BlockSpec((1, D), lambda i: (0, 0))],
        out_shape=[jax.ShapeDtypeStruct((t, D), F32), jax.ShapeDtypeStruct((1, D), F32)],
        compiler_params=_params(1), name="mix_proj_bwd")(dxo, duc, dq_t, dkv_t, dgp, x, g, w_t)


def _attention_tables():
    kj = np.arange(2 * BLK)[:, None]
    qi = np.arange(BLK)[None, :]
    dist = qi + BLK - kj
    in_win = (dist >= 0) & (dist < BLK)
    dpos = np.maximum(dist, 0)
    max_exact = NBUCKET // 2
    dfl = np.maximum(dpos, 1).astype(np.float32)
    large = max_exact + (np.log(dfl / np.float32(max_exact)) / np.float32(math.log(BLK / max_exact))
                         * np.float32(NBUCKET - max_exact)).astype(np.int32)
    large = np.minimum(large, NBUCKET - 1)
    bucket = np.where(dpos < max_exact, dpos, large)
    onehot = (bucket[None] == np.arange(NBUCKET)[:, None, None]).astype(np.float32)
    mask = in_win.astype(np.float32)
    mask_first = mask * (kj >= BLK)
    masks = np.stack([np.tile(mask, (1, GRP)), np.tile(mask_first, (1, GRP))])
    return onehot, masks


def _bias_table(rel_bias, onehot):
    tab = jnp.einsum("bkq,bh->hkq", onehot, rel_bias, precision=lax.Precision.HIGHEST)
    tab = tab.reshape(NKV, GRP, 2 * BLK, BLK)
    return jnp.transpose(tab, (0, 2, 1, 3)).reshape(NKV, 2 * BLK, GRP * BLK)


def _local_step(x, target, vec, weights_of, grads_done, small_done):
    onehot_np, masks_np = _attention_tables()
    onehot = jnp.asarray(onehot_np)
    masks = jnp.asarray(masks_np)
    bias_t = _bias_table(vec["rel_bias"], onehot)
    sink_rows = jnp.repeat(vec["attn_sinks"].reshape(NKV, 1, GRP), BLK, axis=2)
    qg = vec["q_norm"].reshape(HD, 1)
    kg = vec["k_norm"].reshape(HD, 1)
    g1 = vec["ffn1_norm"].reshape(1, D)
    gm = vec["mix_norm"].reshape(1, D)
    g2 = vec["ffn2_norm"].reshape(1, D)
    dwb = vec["conv_dw_bias"].reshape(1, D)
    lng = vec["conv_ln_g"].reshape(1, D)
    lnb = vec["conv_ln_b"].reshape(1, D)

    w1 = weights_of("ffn1", x)
    n1, u1, x1 = _ffn_fwd(x, g1, w1["ffn1_w_in"], w1["ffn1_w_out"], "ffn1_fwd")
    wm = weights_of("mix", x1)
    dwk = jnp.pad(wm["conv_dw_kernel"], ((0, CWP - CW), (0, 0)))
    hm, uc, gp, qkv_t = _mix_proj(x1, gm, wm["w_in"])
    zs, zc = _conv_fwd(uc, dwk, dwb, lng, lnb)
    o_t = _attn_fwd(qkv_t, qg, kg, sink_rows, bias_t, masks)
    x2, a, b, merged = _mix_out(zs, o_t, gp, x1, wm["conv_w_proj"], wm["attn_w_o"], wm["w_out"])
    w2 = weights_of("ffn2", x2)
    n2, u2, dx3, sq = _ffn_fwd(x2, g2, w2["ffn2_w_in"], w2["ffn2_w_out"], "ffn2_fwd", target=target)

    gv = {}
    dx2, du2, h2, dy2, gv["ffn2_norm"] = _ffn_bwd(dx3, x2, g2, u2, w2["ffn2_w_in"], w2["ffn2_w_out"], "ffn2_bwd")
    deps = grads_done("ffn2", {"ffn2_w_in": _wgrad(du2, n2, "ffn2_dw_in", lhs_is_transposed=False, chunk=1408),
                               "ffn2_w_out": _wgrad(h2, dy2, "ffn2_dw_out", lhs_is_transposed=False, chunk=1408)})

    dzs, do_t, dgp, da, db, dx2b = _mix_out_bwd(dx2, a, b, gp, wm["conv_w_proj"], wm["attn_w_o"], wm["w_out"], deps=deps)
    deps = grads_done("mix_out", {"w_out": _wgrad(merged, dx2b, "mix_dw_out", lhs_is_transposed=False, chunk=1024),
                                  "conv_w_proj": _wgrad(zs, da, "mix_dw_cp", lhs_is_transposed=False, chunk=1024),
                                  "attn_w_o": _wgrad(o_t, db, "mix_dw_o", lhs_is_transposed=True, chunk=1024)})

    dq_t, ckv, dqg, dsink_rows, dsacc = _attn_bwd(qkv_t, do_t, qg, kg, sink_rows, bias_t, masks, deps=deps)
    dkv_t, dkg = _kv_combine(ckv, qkv_t, kg)
    gv["q_norm"] = dqg.reshape(HD)
    gv["k_norm"] = dkg.reshape(HD)
    gv["attn_sinks"] = _sink_grad(dsink_rows)[:, :GRP].reshape(NQ)
    gv["rel_bias"] = _bias_grad(dsacc, onehot)[:, :, :GRP].reshape(NBUCKET, NQ)

    duc, dk_conv, gv["conv_dw_bias"], gv["conv_ln_g"], gv["conv_ln_b"] = _conv_bwd(uc, zc, dzs, dwk, lng, lnb)
    gv["conv_dw_kernel"] = dk_conv[:CW]

    dx1, gv["mix_norm"] = _mix_proj_bwd(dx2, duc, dq_t, dkv_t, dgp, x1, gm, wm["w_in"])
    deps = grads_done("mix_in", {"w_in": _wgrad_mix(duc, dq_t, dkv_t, dgp, hm)})

    dx0, du1, h1, dy1, gv["ffn1_norm"] = _ffn_bwd(dx1, x, g1, u1, w1["ffn1_w_in"], w1["ffn1_w_out"], "ffn1_bwd", deps=deps)
    for k in ("ffn1_norm", "mix_norm", "ffn2_norm", "conv_dw_bias", "conv_ln_g", "conv_ln_b"):
        gv[k] = gv[k].reshape(D)
    deps = small_done(gv, sq)
    deps = grads_done("ffn1_out", {"ffn1_w_out": _wgrad(h1, dy1, "ffn1_dw_out", lhs_is_transposed=False, chunk=1408,
                                                        deps=deps)})
    grads_done("ffn1_in", {"ffn1_w_in": _wgrad(du1, n1, "ffn1_dw_in", lhs_is_transposed=False, chunk=1408, deps=deps)})
    return dx0


MESH_ID = pl.DeviceIdType.MESH


def _position():
    return lax.axis_index("x"), lax.axis_index("y"), lax.axis_index("c")


def _shard_rows(ref, index, rows):
    return ref.at[pl.ds(pl.multiple_of(index * rows, 16), rows), :]


def _prep(weights, taps, me):
    n = len(weights)

    def body(me_ref, *refs):
        for k in range(n):
            refs[n + 1 + k][...] = refs[k][...].astype(BF)
        refs[2 * n + 1][0:CW, :] = refs[n][...]
        refs[2 * n + 1][CW:, :] = jnp.zeros((CWP - CW, BLK), F32)

    shard_shapes = [w.shape for w in weights] + [(CWP, BLK)]
    dtypes = [BF] * n + [F32]
    ins = list(weights) + [taps]
    return pl.pallas_call(
        body,
        grid_spec=pltpu.PrefetchScalarGridSpec(
            num_scalar_prefetch=1, grid=(1,),
            in_specs=[pl.BlockSpec(a.shape, lambda i, m: (0, 0), pipeline_mode=pl.Buffered(1)) for a in ins],
            out_specs=[pl.BlockSpec(s, lambda i, m: (m[0], 0)) for s in shard_shapes]),
        out_shape=[jax.ShapeDtypeStruct((N_DEV * s[0], s[1]), d) for s, d in zip(shard_shapes, dtypes)],
        compiler_params=_params(1), name="prep")(me, *ins)


HBM = pl.BlockSpec(memory_space=pltpu.HBM)
SEM = pl.BlockSpec(memory_space=pltpu.SEMAPHORE)
DATAFLOW = pltpu.SideEffectType.DATAFLOW_SIDE_EFFECTING
TOKEN = jax.ShapeDtypeStruct((8, 128), F32)


def _in_hbm(x):
    return pltpu.with_memory_space_constraint(x, pltpu.HBM)


def _hbm_like(arrays):
    return [pltpu.HBM(a.shape, a.dtype) for a in arrays]


def _other_chips(x, y):
    return [(1 - x, y), (x, 1 - y), (1 - x, 1 - y)]


def _device_index(chip, c):
    return 4 * chip[0] + 2 * chip[1] + c


def _chip_index(chip):
    return 2 * chip[0] + chip[1]


class _Exchange:
    def __init__(self, gather):
        self.gather = gather

    def sent(self, x, y, c, chip):
        return _device_index((x, y), c) if self.gather else _chip_index(chip)

    def lands_at(self, x, y, c):
        return _device_index((x, y), c) if self.gather else _chip_index((x, y))

    def arrives_at(self, chip, c):
        return _device_index(chip, c) if self.gather else _chip_index(chip)


def _ici_copies_start(sets, sources, landings, exchange, name, deps=()):
    n = len(landings)
    arrays = (list(sources) if sources is not None else []) + list(landings)
    first_land = len(arrays) - n
    n_sets = len(sets)
    n_deps = len(deps)

    def body(*refs):
        refs = refs[n_deps:]
        src, land = refs[:n], refs[first_land:first_land + n]
        sems = refs[len(arrays):len(arrays) + 2 * n_sets]
        token = refs[-1]
        x, y, c = _position()
        for s, members in enumerate(sets):
            for slot, (k, rows) in enumerate(members):
                for j, chip in enumerate(_other_chips(x, y)):
                    pltpu.make_async_remote_copy(
                        src_ref=_shard_rows(src[k], exchange.sent(x, y, c, chip), rows),
                        dst_ref=_shard_rows(land[k], exchange.lands_at(x, y, c), rows),
                        send_sem=sems[2 * s].at[3 * slot + j], recv_sem=sems[2 * s + 1].at[3 * slot + j],
                        device_id=(*chip, c), device_id_type=MESH_ID).start()
        token[...] = jnp.zeros_like(token)

    sem_shapes = []
    for members in sets:
        sem_shapes += [pltpu.SemaphoreType.DMA((3 * len(members),))] * 2
    out = pl.pallas_call(
        body, name=name,
        out_shape=sem_shapes + _hbm_like(arrays) + [TOKEN],
        in_specs=[ANY] * n_deps + [HBM] * len(arrays),
        out_specs=[SEM] * (2 * n_sets) + [HBM] * len(arrays) + [pl.BlockSpec(memory_space=pltpu.VMEM)],
        input_output_aliases={n_deps + i: 2 * n_sets + i for i in range(len(arrays))},
        compiler_params=pltpu.CompilerParams(has_side_effects=DATAFLOW),
    )(*deps, *[_in_hbm(a) for a in arrays])
    sems = [(out[2 * s], out[2 * s + 1]) for s in range(n_sets)]
    thru = list(out[2 * n_sets:2 * n_sets + len(arrays)])
    return sems, (thru[:first_land] if sources is not None else None), thru[first_land:], out[-1]


def _ici_copies_wait(sems, members, sources, landings, exchange, after, name):
    n = len(landings)
    arrays = (list(sources) if sources is not None else []) + list(landings)
    first_land = len(arrays) - n

    def body(*refs):
        src, land = refs[:n], refs[first_land:first_land + n]
        send_sems, recv_sems = refs[len(arrays)], refs[len(arrays) + 1]
        x, y, c = _position()
        for slot, rows in enumerate(members):
            for j, chip in enumerate(_other_chips(x, y)):
                cp = pltpu.make_async_remote_copy(
                    src_ref=_shard_rows(src[slot], exchange.sent(x, y, c, chip), rows),
                    dst_ref=_shard_rows(land[slot], exchange.arrives_at(chip, c), rows),
                    send_sem=send_sems.at[3 * slot + j], recv_sem=recv_sems.at[3 * slot + j],
                    device_id=(*chip, c), device_id_type=MESH_ID)
                cp.wait_send()
                cp.wait_recv()

    out = pl.pallas_call(
        body, name=name, out_shape=_hbm_like(arrays),
        in_specs=[HBM] * len(arrays) + [SEM, SEM] + [ANY] * len(after), out_specs=[HBM] * len(arrays),
        input_output_aliases={i: i for i in range(len(arrays))},
        compiler_params=pltpu.CompilerParams(has_side_effects=DATAFLOW),
    )(*arrays, sems[0], sems[1], *after)
    return list(out[first_land:])


def _d2d_gather(buffers, rows, name):
    n = len(buffers)

    def body(*refs):
        land = refs[n:2 * n]
        send_sems, recv_sems = refs[2 * n:]
        x, y, c = _position()
        chips = [(x, y)] + _other_chips(x, y)
        sends, recvs = [], []
        for k in range(n):
            for j, chip in enumerate(chips):
                for copies, core in ((sends, c), (recvs, 1 - c)):
                    block = _shard_rows(land[k], _device_index(chip, core), rows[k])
                    copies.append(pltpu.make_async_remote_copy(
                        src_ref=block, dst_ref=block, send_sem=send_sems.at[k, j], recv_sem=recv_sems.at[k, j],
                        device_id=(x, y, 1 - c), device_id_type=MESH_ID))
        for cp in sends:
            cp.start()
        for cp in recvs:
            cp.wait_recv()
        for cp in sends:
            cp.wait_send()

    return pl.pallas_call(
        body, name=name, out_shape=[jax.ShapeDtypeStruct(a.shape, a.dtype) for a in buffers],
        in_specs=[ANY] * n, out_specs=[ANY] * n, input_output_aliases={i: i for i in range(n)},
        scratch_shapes=[pltpu.SemaphoreType.DMA((n, 4)), pltpu.SemaphoreType.DMA((n, 4))],
    )(*buffers)


def _rs_pair(grads, name):
    n = len(grads)
    rows = [g.shape[0] // N_DEV for g in grads]

    def body(*refs):
        ins, outs = refs[:n], refs[n:2 * n]
        send_sems, recv_sems = refs[2 * n:]
        x, y, c = _position()
        copies = []
        for k in range(n):
            for q in range(4):
                copies.append(pltpu.make_async_remote_copy(
                    src_ref=_shard_rows(ins[k], 2 * q + 1 - c, rows[k]), dst_ref=_shard_rows(outs[k], q, rows[k]),
                    send_sem=send_sems.at[k, q], recv_sem=recv_sems.at[k, q], device_id=(x, y, 1 - c),
                    device_id_type=MESH_ID))
        for cp in copies:
            cp.start()
        for cp in copies:
            cp.wait()

    return pl.pallas_call(
        body, out_shape=[jax.ShapeDtypeStruct((4 * r, g.shape[1]), g.dtype) for g, r in zip(grads, rows)],
        in_specs=[ANY] * n, out_specs=[ANY] * n,
        scratch_shapes=[pltpu.SemaphoreType.DMA((n, 4)), pltpu.SemaphoreType.DMA((n, 4))],
        name=name)(*grads)


def _pair_add(grad, received, place, name):
    r = received.shape[0] // 4
    tr = 352 if r % 352 == 0 else r
    per = r // tr

    def body(place_ref, g_ref, r_ref, o_ref, land_ref):
        total = (g_ref[...].astype(F32) + r_ref[...].astype(F32)).astype(BF)
        o_ref[...] = total

        @pl.when(pl.program_id(1) == place_ref[1])
        def _():
            land_ref[...] = total

    return pl.pallas_call(
        body,
        grid_spec=pltpu.PrefetchScalarGridSpec(
            num_scalar_prefetch=1, grid=(per, 4),
            in_specs=[pl.BlockSpec((tr, D), lambda i, q, p: ((2 * q + p[0]) * per + i, 0)),
                      pl.BlockSpec((tr, D), lambda i, q, p: (q * per + i, 0))],
            out_specs=[pl.BlockSpec((tr, D), lambda i, q, p: (q * per + i, 0)),
                       pl.BlockSpec((tr, D), lambda i, q, p: (p[1] * per + i, 0))]),
        out_shape=[jax.ShapeDtypeStruct(received.shape, BF)] * 2,
        compiler_params=_params(2), name=name)(place, grad, received)


def _all_reduce_small(payload, deps=()):
    r = payload.shape[0]

    def body(in_ref, out_ref, land_ref, send_sems, recv_sems):
        x, y, c = _position()
        me = 4 * x + 2 * y + c
        land_ref[me] = in_ref[...]
        copies = []
        for k in range(1, N_DEV):
            peer = (x ^ (k >> 2), y ^ ((k >> 1) & 1), c ^ (k & 1))
            copies.append(pltpu.make_async_remote_copy(
                src_ref=in_ref, dst_ref=land_ref.at[me], send_sem=send_sems.at[k - 1], recv_sem=recv_sems.at[k - 1],
                device_id=peer, device_id_type=MESH_ID))
        for cp in copies:
            cp.start()
        for k in range(1, N_DEV):
            peer_index = me ^ k
            pltpu.make_async_remote_copy(
                src_ref=in_ref, dst_ref=land_ref.at[peer_index], send_sem=send_sems.at[k - 1], recv_sem=recv_sems.at[k - 1],
                device_id=(x, y, c), device_id_type=MESH_ID).wait_recv()
        for cp in copies:
            cp.wait_send()
        acc = land_ref[0]
        for d in range(1, N_DEV):
            acc = acc + land_ref[d]
        out_ref[...] = acc

    return _call(
        body, deps, (payload,), out_shape=jax.ShapeDtypeStruct((r, D), F32),
        in_specs=[pl.BlockSpec(memory_space=pltpu.VMEM)], out_specs=pl.BlockSpec(memory_space=pltpu.VMEM),
        scratch_shapes=[pltpu.VMEM((N_DEV, r, D), F32), pltpu.SemaphoreType.DMA((N_DEV - 1,)),
                        pltpu.SemaphoreType.DMA((N_DEV - 1,))],
        name="all_reduce_small")


def _adamw_math(w, g, m, v):
    m = ADAM_B1 * m + (1.0 - ADAM_B1) * g
    v = ADAM_B2 * v + (1.0 - ADAM_B2) * (g * g)
    m_hat = m / (1.0 - ADAM_B1 ** ADAM_STEP)
    v_hat = v / (1.0 - ADAM_B2 ** ADAM_STEP)
    delta = -ADAM_LR * (m_hat / (jnp.sqrt(v_hat) + ADAM_EPS) + ADAM_WD * w)
    return delta, m, v


def _sum_partials(blocks):
    g = blocks[0].astype(F32)
    for blk in blocks[1:]:
        g = g + blk.astype(F32)
    return g


def _reduce_adamw(landed, w, m, v, name):
    r = w.shape[0]
    tr = 176 if r % 176 == 0 else r
    per = r // tr

    def body(r0, r1, r2, r3, w_ref, m_ref, v_ref, g_ref, d_ref, nm_ref, nv_ref):
        g = _sum_partials([r0[...], r1[...], r2[...], r3[...]])
        g_ref[...] = g
        d_ref[...], nm_ref[...], nv_ref[...] = _adamw_math(w_ref[...], g, m_ref[...], v_ref[...])

    tile = _row_tile(tr, D)
    return pl.pallas_call(
        body, grid=(per,),
        in_specs=[pl.BlockSpec((tr, D), lambda i, q=q: (q * per + i, 0)) for q in range(4)] + [tile] * 3,
        out_specs=[tile] * 4, out_shape=[jax.ShapeDtypeStruct(w.shape, F32)] * 4,
        compiler_params=_params(1), name=name)(landed, landed, landed, landed, w, m, v)


def _adamw_small(w, g, m, v, name):
    def body(w_ref, g_ref, m_ref, v_ref, d_ref, nm_ref, nv_ref):
        d_ref[...], nm_ref[...], nv_ref[...] = _adamw_math(w_ref[...], g_ref[...], m_ref[...], v_ref[...])

    return pl.pallas_call(body, out_shape=[jax.ShapeDtypeStruct(w.shape, F32)] * 3, name=name)(w, g, m, v)


WEIGHTS = ("ffn1_norm", "ffn1_w_in", "ffn1_w_out", "mix_norm", "w_in", "conv_dw_kernel", "conv_dw_bias", "conv_ln_g",
           "conv_ln_b", "conv_w_proj", "q_norm", "k_norm", "attn_sinks", "rel_bias", "attn_w_o", "w_out", "ffn2_norm",
           "ffn2_w_in", "ffn2_w_out")
MATRICES = ("ffn1_w_in", "ffn1_w_out", "w_in", "conv_w_proj", "attn_w_o", "w_out", "ffn2_w_in", "ffn2_w_out")
COLUMN_SHARDED = ("ffn1_w_in", "w_in", "ffn2_w_in")
ROW_VECTORS = ("ffn1_norm", "mix_norm", "conv_dw_bias", "conv_ln_g", "conv_ln_b", "ffn2_norm")
PACKED = (("q_norm", HD), ("k_norm", HD), ("attn_sinks", NQ), ("rel_bias", NBUCKET * NQ))
GATHER = _Exchange(gather=True)
SCATTER = _Exchange(gather=False)
GATHER_STAGES = ("ffn1", "mix", "ffn2")
STAGE_MEMBERS = {"ffn1": ("ffn1_w_in", "ffn1_w_out"), "mix": ("w_in", "conv_w_proj", "attn_w_o", "w_out", "taps"),
                 "ffn2": ("ffn2_w_in", "ffn2_w_out")}
ROW_PACKED = len(ROW_VECTORS)
ROW_LOSS = ROW_PACKED + 1
ROW_TAPS = 8
PAYLOAD_ROWS = ROW_TAPS + CWP


def _pack_small(values, last_row):
    packed = jnp.concatenate([values[k].reshape(-1) for k, _ in PACKED])
    packed = jnp.pad(packed, (0, D - packed.shape[0])).reshape(1, D)
    return jnp.concatenate([values[k].reshape(1, D) for k in ROW_VECTORS] + [packed, last_row], axis=0)


def _unpack_small(rows):
    out = {k: rows[i] for i, k in enumerate(ROW_VECTORS)}
    at = 0
    for k, size in PACKED:
        out[k] = rows[ROW_PACKED, at:at + size]
        at += size
    out["rel_bias"] = out["rel_bias"].reshape(NBUCKET, NQ)
    return out


def kernel(x, ffn1_norm, ffn1_w_in, ffn1_w_out, mix_norm, w_in, conv_dw_kernel, conv_dw_bias, conv_ln_g, conv_ln_b, conv_w_proj, q_norm, k_norm, attn_sinks, rel_bias, attn_w_o, w_out, ffn2_norm, ffn2_w_in, ffn2_w_out, loss_target, m_ffn1_norm, m_ffn1_w_in, m_ffn1_w_out, m_mix_norm, m_w_in, m_conv_dw_kernel, m_conv_dw_bias, m_conv_ln_g, m_conv_ln_b, m_conv_w_proj, m_q_norm, m_k_norm, m_attn_sinks, m_rel_bias, m_attn_w_o, m_w_out, m_ffn2_norm, m_ffn2_w_in, m_ffn2_w_out, v_ffn1_norm, v_ffn1_w_in, v_ffn1_w_out, v_mix_norm, v_w_in, v_conv_dw_kernel, v_conv_dw_bias, v_conv_ln_g, v_conv_ln_b, v_conv_w_proj, v_q_norm, v_k_norm, v_attn_sinks, v_rel_bias, v_attn_w_o, v_w_out, v_ffn2_norm, v_ffn2_w_in, v_ffn2_w_out):
    w = dict(ffn1_norm=ffn1_norm, ffn1_w_in=ffn1_w_in, ffn1_w_out=ffn1_w_out, mix_norm=mix_norm, w_in=w_in,
             conv_dw_kernel=conv_dw_kernel, conv_dw_bias=conv_dw_bias, conv_ln_g=conv_ln_g, conv_ln_b=conv_ln_b,
             conv_w_proj=conv_w_proj, q_norm=q_norm, k_norm=k_norm, attn_sinks=attn_sinks, rel_bias=rel_bias,
             attn_w_o=attn_w_o, w_out=w_out, ffn2_norm=ffn2_norm, ffn2_w_in=ffn2_w_in, ffn2_w_out=ffn2_w_out)
    m = dict(ffn1_norm=m_ffn1_norm, ffn1_w_in=m_ffn1_w_in, ffn1_w_out=m_ffn1_w_out, mix_norm=m_mix_norm, w_in=m_w_in,
             conv_dw_kernel=m_conv_dw_kernel, conv_dw_bias=m_conv_dw_bias, conv_ln_g=m_conv_ln_g, conv_ln_b=m_conv_ln_b,
             conv_w_proj=m_conv_w_proj, q_norm=m_q_norm, k_norm=m_k_norm, attn_sinks=m_attn_sinks, rel_bias=m_rel_bias,
             attn_w_o=m_attn_w_o, w_out=m_w_out, ffn2_norm=m_ffn2_norm, ffn2_w_in=m_ffn2_w_in, ffn2_w_out=m_ffn2_w_out)
    v = dict(ffn1_norm=v_ffn1_norm, ffn1_w_in=v_ffn1_w_in, ffn1_w_out=v_ffn1_w_out, mix_norm=v_mix_norm, w_in=v_w_in,
             conv_dw_kernel=v_conv_dw_kernel, conv_dw_bias=v_conv_dw_bias, conv_ln_g=v_conv_ln_g, conv_ln_b=v_conv_ln_b,
             conv_w_proj=v_conv_w_proj, q_norm=v_q_norm, k_norm=v_k_norm, attn_sinks=v_attn_sinks, rel_bias=v_rel_bias,
             attn_w_o=v_attn_w_o, w_out=v_w_out, ffn2_norm=v_ffn2_norm, ffn2_w_in=v_ffn2_w_in, ffn2_w_out=v_ffn2_w_out)
    px, py, pc = _position()
    me = 4 * px + 2 * py + pc
    place = jnp.stack([pc, 2 * px + py]).astype(jnp.int32)

    rows_of = lambda k, a: a.T if k in COLUMN_SHARDED else a
    buffers = dict(zip(MATRICES + ("taps",), _prep([rows_of(k, w[k]) for k in MATRICES], conv_dw_kernel,
                                                   me.astype(jnp.int32).reshape(1))))
    landings, sets = [], []
    for stage in GATHER_STAGES:
        sets.append([(len(landings) + i, buffers[k].shape[0] // N_DEV) for i, k in enumerate(STAGE_MEMBERS[stage])])
        landings += [buffers[k] for k in STAGE_MEMBERS[stage]]
    sems, _, land_thru, _ = _ici_copies_start(sets, None, landings, GATHER, "gather_start")

    def weights_of(stage, after):
        s = GATHER_STAGES.index(stage)
        rows = [r for _, r in sets[s]]
        landed = _ici_copies_wait(sems[s], rows, None, [land_thru[k] for k, _ in sets[s]], GATHER, [after],
                                  "gather_wait_" + stage)
        out = dict(zip(STAGE_MEMBERS[stage], _d2d_gather(landed, rows, "gather_d2d_" + stage)))
        if "taps" in out:
            taps = out.pop("taps")
            out["conv_dw_kernel"] = jnp.transpose(taps.reshape(N_DEV, CWP, BLK), (1, 0, 2)).reshape(CWP, D)[:CW]
        return out

    in_flight = []

    def grads_done(stage, grads):
        names = list(grads)
        received = _rs_pair([grads[k] for k in names], "rs_pair_" + stage)
        added = [_pair_add(grads[k], r, place, "pair_add_" + k) for k, r in zip(names, received)]
        partials = [p for p, _ in added]
        members = [(i, p.shape[0] // 4) for i, p in enumerate(partials)]
        sem, p_thru, l_thru, token = _ici_copies_start([members], partials, [l for _, l in added], SCATTER,
                                                       "scatter_start_" + stage)
        in_flight.append((stage, names, sem[0], p_thru, l_thru, token))
        return [token]

    reduced = []

    def small_done(gv, sq):
        payload = jnp.concatenate([_pack_small(gv, sq), jnp.pad(gv["conv_dw_kernel"], ((0, CWP - CW), (0, 0)))], axis=0)
        reduced.append(_all_reduce_small(payload))
        return reduced

    vec = {k: w[k] for k in WEIGHTS if k not in MATRICES and k != "conv_dw_kernel"}
    dx0 = _local_step(x[0], loss_target[0], vec, weights_of, grads_done, small_done)
    total = reduced[0]
    loss = (0.5 / D) * jnp.sum(total[ROW_LOSS])

    grads, delta, new_m, new_v = {}, {}, {}, {}
    after = [in_flight[-1][-1]]
    for stage, names, sem, p_thru, l_thru, _ in in_flight:
        landed = _ici_copies_wait(sem, [p.shape[0] // 4 for p in p_thru], p_thru, l_thru, SCATTER, after,
                                  "scatter_wait_" + stage)
        after = []
        for k, buf in zip(names, landed):
            out = _reduce_adamw(buf, rows_of(k, w[k]), rows_of(k, m[k]), rows_of(k, v[k]), "adamw_" + k)
            grads[k], delta[k], new_m[k], new_v[k] = [rows_of(k, a) for a in out]
            after.append(out[1])
    zero_row = jnp.zeros((1, D), F32)
    d8, m8, v8 = _adamw_small(_pack_small(w, zero_row), total[:ROW_TAPS], _pack_small(m, zero_row),
                              _pack_small(v, zero_row), "adamw_small")
    grads.update(_unpack_small(total[:ROW_TAPS]))
    delta.update(_unpack_small(d8))
    new_m.update(_unpack_small(m8))
    new_v.update(_unpack_small(v8))
    k = "conv_dw_kernel"
    grads[k] = lax.dynamic_slice_in_dim(total[ROW_TAPS:ROW_TAPS + CW], me * BLK, BLK, axis=1)
    delta[k], new_m[k], new_v[k] = _adamw_small(w[k], grads[k], m[k], v[k], "adamw_taps")

    return (loss, dx0[None], *[grads[k] for k in WEIGHTS], *[delta[k] for k in WEIGHTS],
            *[new_m[k] for k in WEIGHTS], *[new_v[k] for k in WEIGHTS])
```

```python
import functools
import math

import numpy as np
import jax
import jax.numpy as jnp
from jax import lax
from jax.experimental import pallas as pl
from jax.experimental.pallas import tpu as pltpu

F32 = jnp.float32
BF = jnp.bfloat16

D = 1024
F = 2816
INW = 5632
CW = 31
CWP = 32
HD = 64
NQ = 16
NKV = 4
GRP = NQ // NKV
BLK = 128
NBUCKET = 32
EPS = 1e-6
NEG = float(jnp.finfo(jnp.float32).min)
QK_SCALE = 1.0 / math.sqrt(HD)
R_CONV = (0, 2048)
R_QKV = (2048, 3584)
R_Q = (2048, 3072)
R_KV = (3072, 3584)
R_GATE = (3584, 5632)

N_DEV = 8
VMEM_LIMIT_V7X = 56 * 1024 * 1024

ADAM_LR = 0.001
ADAM_B1 = 0.9
ADAM_B2 = 0.999
ADAM_EPS = 1e-08
ADAM_WD = 0.01
ADAM_STEP = 10

NT_DIMS = (((1,), (1,)), ((), ()))
TN_DIMS = (((0,), (0,)), ((), ()))


def _dot(a, b):
    return jnp.dot(a, b, preferred_element_type=F32)


def _dot_nt(a, b):
    return lax.dot_general(a, b, NT_DIMS, preferred_element_type=F32)


def _dot_tn(a, b):
    return lax.dot_general(a, b, TN_DIMS, preferred_element_type=F32)


def _sig(x):
    return 1.0 / (1.0 + jnp.exp(-x))


ANY = pl.BlockSpec(memory_space=pl.ANY)


def _call(body, deps, args, **kw):
    n = len(deps)
    if n:
        kw["in_specs"] = [ANY] * n + list(kw["in_specs"])
        return pl.pallas_call(lambda *refs: body(*refs[n:]), **kw)(*deps, *args)
    return pl.pallas_call(body, **kw)(*args)


def _params(n_axes):
    return pltpu.CompilerParams(dimension_semantics=("arbitrary",) * n_axes, vmem_limit_bytes=VMEM_LIMIT_V7X)


def _resident(shape):
    zeros = (0,) * len(shape)
    return pl.BlockSpec(shape, lambda *_: zeros, pipeline_mode=pl.Buffered(1))


def _row_tile(rows, cols):
    return pl.BlockSpec((rows, cols), lambda i: (i, 0))


def _rms_stats(x):
    r = lax.rsqrt(jnp.mean(x * x, axis=-1, keepdims=True) + EPS)
    return r, x * r


def _rms_bwd(dn, x, g):
    r, xh = _rms_stats(x)
    dxh = dn * g
    dx = r * (dxh - xh * jnp.mean(dxh * xh, axis=-1, keepdims=True))
    return dx, jnp.sum(dn * xh, axis=0, keepdims=True)


def _ffn_fwd(x, g, w_in_t, w_out, name, target=None):
    t = x.shape[0]
    tm = min(256, t)
    with_loss = target is not None

    def body(*refs):
        if with_loss:
            x_ref, g_ref, w_ref, wo_ref, t_ref, n_ref, u_ref, dy_ref, sq_ref = refs
        else:
            x_ref, g_ref, w_ref, wo_ref, n_ref, u_ref, xo_ref = refs
        x = x_ref[...]
        r, xh = _rms_stats(x)
        n = (xh * g_ref[...]).astype(BF)
        n_ref[...] = n
        u = _dot_nt(n, w_ref[...])
        u_ref[...] = u.astype(BF)
        a = u[:, :F]
        b = u[:, F:]
        h = (a * _sig(a) * b).astype(BF)
        xo = x + 0.5 * _dot(h, wo_ref[...])
        if with_loss:
            err = xo - t_ref[...]
            dy_ref[...] = err * (1.0 / D)

            @pl.when(pl.program_id(0) == 0)
            def _():
                sq_ref[...] = jnp.zeros_like(sq_ref)

            sq_ref[...] += jnp.sum(err * err, axis=0, keepdims=True)
        else:
            xo_ref[...] = xo

    in_specs = [_row_tile(tm, D), _resident((1, D)), _resident((INW, D)), _resident((F, D))]
    args = [x, g, w_in_t, w_out]
    out_specs = [_row_tile(tm, D), _row_tile(tm, INW), _row_tile(tm, D)]
    out_shape = [jax.ShapeDtypeStruct((t, D), BF), jax.ShapeDtypeStruct((t, INW), BF), jax.ShapeDtypeStruct((t, D), F32)]
    if with_loss:
        in_specs.append(_row_tile(tm, D))
        args.append(target)
        out_specs.append(pl.BlockSpec((1, D), lambda i: (0, 0)))
        out_shape.append(jax.ShapeDtypeStruct((1, D), F32))
    return pl.pallas_call(body, grid=(t // tm,), in_specs=in_specs, out_specs=out_specs, out_shape=out_shape,
                          compiler_params=_params(1), name=name)(*args)


def _ffn_bwd(dxo, x, g, u, w_in_t, w_out, name, deps=()):
    t = x.shape[0]
    tm = min(256, t)

    def body(dxo_ref, x_ref, g_ref, u_ref, w_ref, wo_ref, dx_ref, du_ref, h_ref, dy_ref, dg_ref):
        dxo = dxo_ref[...]
        dy = (0.5 * dxo).astype(BF)
        dy_ref[...] = dy
        dh = _dot_nt(dy, wo_ref[...])
        a = u_ref[:, :F].astype(F32)
        b = u_ref[:, F:].astype(F32)
        s = _sig(a)
        sa = a * s
        h_ref[...] = (sa * b).astype(BF)
        du_ref[:, :F] = (dh * b * (s * (1.0 + a * (1.0 - s)))).astype(BF)
        du_ref[:, F:] = (dh * sa).astype(BF)
        dn = _dot(du_ref[...], w_ref[...])
        dx, dg = _rms_bwd(dn, x_ref[...], g_ref[...])
        dx_ref[...] = dxo + dx

        @pl.when(pl.program_id(0) == 0)
        def _():
            dg_ref[...] = jnp.zeros_like(dg_ref)

        dg_ref[...] += dg

    return _call(
        body, deps, (dxo, x, g, u, w_in_t, w_out), grid=(t // tm,),
        in_specs=[_row_tile(tm, D), _row_tile(tm, D), _resident((1, D)), _row_tile(tm, INW), _resident((INW, D)),
                  _resident((F, D))],
        out_specs=[_row_tile(tm, D), _row_tile(tm, INW), _row_tile(tm, F), _row_tile(tm, D),
                   pl.BlockSpec((1, D), lambda i: (0, 0))],
        out_shape=[jax.ShapeDtypeStruct((t, D), F32), jax.ShapeDtypeStruct((t, INW), BF), jax.ShapeDtypeStruct((t, F), BF),
                   jax.ShapeDtypeStruct((t, D), BF), jax.ShapeDtypeStruct((1, D), F32)],
        compiler_params=_params(1), name=name)


def _wgrad(lhs, rhs, name, *, lhs_is_transposed, chunk, deps=()):
    t = rhs.shape[0]
    n = lhs.shape[0] if lhs_is_transposed else lhs.shape[1]
    c = min(chunk, n)

    def body(l_ref, r_ref, o_ref):
        if lhs_is_transposed:
            o_ref[...] = _dot(l_ref[...], r_ref[...]).astype(BF)
        else:
            o_ref[...] = _dot_tn(l_ref[...], r_ref[...]).astype(BF)

    lhs_spec = pl.BlockSpec((c, t), lambda j: (j, 0)) if lhs_is_transposed else pl.BlockSpec((t, c), lambda j: (0, j))
    return _call(
        body, deps, (lhs, rhs), grid=(n // c,),
        in_specs=[lhs_spec, _resident((t, D))],
        out_specs=pl.BlockSpec((c, D), lambda j: (j, 0)),
        out_shape=jax.ShapeDtypeStruct((n, D), BF),
        compiler_params=_params(1), name=name)


def _wgrad_mix(duc, dq_t, dkv_t, dgp, hm):
    t = hm.shape[0]
    c = 512
    first_q, first_kv, first_gate = R_Q[0] // c, R_KV[0] // c, R_GATE[0] // c

    def body(uc_ref, q_ref, kv_ref, gp_ref, h_ref, o_ref):
        j = pl.program_id(0)

        @pl.when(j < first_q)
        def _():
            o_ref[...] = _dot_tn(uc_ref[...], h_ref[...]).astype(BF)

        @pl.when((j >= first_q) & (j < first_kv))
        def _():
            o_ref[...] = _dot(q_ref[...], h_ref[...]).astype(BF)

        @pl.when((j >= first_kv) & (j < first_gate))
        def _():
            o_ref[...] = _dot(kv_ref[...], h_ref[...]).astype(BF)

        @pl.when(j >= first_gate)
        def _():
            o_ref[...] = _dot_tn(gp_ref[...], h_ref[...]).astype(BF)

    return pl.pallas_call(
        body, grid=(INW // c,),
        in_specs=[pl.BlockSpec((t, c), lambda j: (0, jnp.clip(j, 0, first_q - 1))),
                  pl.BlockSpec((c, t), lambda j: (jnp.clip(j - first_q, 0, first_kv - first_q - 1), 0)),
                  pl.BlockSpec((c, t), lambda j: (jnp.clip(j - first_kv, 0, first_gate - first_kv - 1), 0)),
                  pl.BlockSpec((t, c), lambda j: (0, jnp.clip(j - first_gate, 0, INW // c - first_gate - 1))),
                  _resident((t, D))],
        out_specs=pl.BlockSpec((c, D), lambda j: (j, 0)),
        out_shape=jax.ShapeDtypeStruct((INW, D), BF),
        compiler_params=_params(1), name="mix_dw_in")(duc, dq_t, dkv_t, dgp, hm)


def _mix_proj(x, g, w_t):
    t = x.shape[0]
    tm = min(256, t)

    def body(x_ref, g_ref, w_ref, hm_ref, uc_ref, gp_ref, qkv_ref):
        r, xh = _rms_stats(x_ref[...])
        hm = (xh * g_ref[...]).astype(BF)
        hm_ref[...] = hm
        uc_ref[...] = _dot_nt(hm, w_ref[R_CONV[0]:R_CONV[1], :]).astype(BF)
        gp_ref[...] = _dot_nt(hm, w_ref[R_GATE[0]:R_GATE[1], :]).astype(BF)
        qkv_ref[...] = _dot_nt(w_ref[R_QKV[0]:R_QKV[1], :], hm).astype(BF)

    return pl.pallas_call(
        body, grid=(t // tm,),
        in_specs=[_row_tile(tm, D), _resident((1, D)), _resident((INW, D))],
        out_specs=[_row_tile(tm, D), _row_tile(tm, 2 * D), _row_tile(tm, 2 * D), pl.BlockSpec((1536, tm), lambda i: (0, i))],
        out_shape=[jax.ShapeDtypeStruct((t, D), BF), jax.ShapeDtypeStruct((t, 2 * D), BF),
                   jax.ShapeDtypeStruct((t, 2 * D), BF), jax.ShapeDtypeStruct((1536, t), BF)],
        compiler_params=_params(1), name="mix_proj")(x, g, w_t)


CONV_HALO = 32
CONV_LEAD = CONV_HALO - (CW - 1)


def _glu(uc):
    uc = uc.astype(F32)
    return uc[:, :D] * _sig(uc[:, D:])


def _ln_stats(zc):
    mu = jnp.mean(zc, axis=-1, keepdims=True)
    zm = zc - mu
    r = lax.rsqrt(jnp.mean(zm * zm, axis=-1, keepdims=True) + EPS)
    return r, zm * r


CONV_SHIFTS = 8
CONV_CHUNK = 32


def _store_shifted(buf, rows):
    for b in range(1, CONV_SHIFTS):
        buf[b, 0:rows - 8, :] = buf[0, pl.ds(b, rows - 8), :]


def _conv_fwd(uc, dwk, dwb, lng, lnb):
    t = uc.shape[0]
    tm = min(512, t)
    per = tm // CONV_HALO
    ext = tm + CONV_HALO

    def body(cur_ref, prev_ref, k_ref, kb_ref, g_ref, b_ref, o_ref, zc_ref, zsh):
        i = pl.program_id(0)
        zsh[0, 0:CONV_HALO, :] = _glu(prev_ref[...]) * (i > 0).astype(F32)
        zsh[0, CONV_HALO:, :] = _glu(cur_ref[...])
        _store_shifted(zsh, ext)

        def chunk(ci, carry):
            r0 = pl.multiple_of(ci * CONV_CHUNK, CONV_CHUNK)
            acc = jnp.zeros((CONV_CHUNK, D), F32) + kb_ref[...]
            for w in range(CW):
                a, b = divmod(CONV_LEAD + w, 8)
                acc = acc + k_ref[w:w + 1, :] * zsh[b, pl.ds(r0 + 8 * a, CONV_CHUNK), :]
            zc_ref[pl.ds(r0, CONV_CHUNK), :] = acc
            r, xh = _ln_stats(acc)
            y = xh * g_ref[...] + b_ref[...]
            o_ref[pl.ds(r0, CONV_CHUNK), :] = (y * _sig(y)).astype(BF)
            return carry

        lax.fori_loop(0, tm // CONV_CHUNK, chunk, 0)

    return pl.pallas_call(
        body, grid=(t // tm,),
        in_specs=[_row_tile(tm, 2 * D),
                  pl.BlockSpec((CONV_HALO, 2 * D), lambda i: (jnp.maximum(i * per - 1, 0), 0)),
                  _resident((CWP, D)), _resident((1, D)), _resident((1, D)), _resident((1, D))],
        out_specs=[_row_tile(tm, D), _row_tile(tm, D)],
        out_shape=[jax.ShapeDtypeStruct((t, D), BF), jax.ShapeDtypeStruct((t, D), F32)],
        scratch_shapes=[pltpu.VMEM((CONV_SHIFTS, ext, D), F32)],
        compiler_params=_params(1), name="conv_fwd")(uc, uc, dwk, dwb, lng, lnb)


def _conv_bwd(uc, zc, dzs, dwk, lng, lnb):
    t = uc.shape[0]
    tm = min(256, t)
    per = tm // CONV_HALO
    n_tiles = t // tm
    ext = tm + CONV_HALO
    last_block = t // CONV_HALO - 1

    def body(cur_ref, prev_ref, zc_ref, zcn_ref, dz_ref, dzn_ref, k_ref, g_ref, b_ref,
             duc_ref, dk_ref, dkb_ref, dg_ref, db_ref, zsh, dsh, dk8):
        i = pl.program_id(0)

        @pl.when(i == 0)
        def _():
            dk8[...] = jnp.zeros_like(dk8)
            dkb_ref[...] = jnp.zeros_like(dkb_ref)
            dg_ref[...] = jnp.zeros_like(dg_ref)
            db_ref[...] = jnp.zeros_like(db_ref)

        has_next = (i < n_tiles - 1).astype(F32)
        zsh[0, 0:CONV_HALO, :] = _glu(prev_ref[...]) * (i > 0).astype(F32)
        zsh[0, CONV_HALO:, :] = _glu(cur_ref[...])
        _store_shifted(zsh, ext)
        gain = g_ref[...]

        def ln_silu_bwd(zc, dzs, live):
            r, xh = _ln_stats(zc)
            y = xh * gain + b_ref[...]
            sy = _sig(y)
            dy = dzs * (sy * (1.0 + y * (1.0 - sy))) * live
            dxh = dy * gain
            dzc = r * (dxh - jnp.mean(dxh, axis=-1, keepdims=True) - xh * jnp.mean(dxh * xh, axis=-1, keepdims=True))
            return dzc, dy, xh

        dzc, dy, xh = ln_silu_bwd(zc_ref[...], dz_ref[...], 1.0)
        dsh[0, 0:tm, :] = dzc
        dg_ref[...] += jnp.sum(dy * xh, axis=0, keepdims=True)
        db_ref[...] += jnp.sum(dy, axis=0, keepdims=True)
        dkb_ref[...] += jnp.sum(dzc, axis=0, keepdims=True)
        dsh[0, tm:, :] = ln_silu_bwd(zcn_ref[...], dzn_ref[...], has_next)[0]
        _store_shifted(dsh, ext)

        def chunk(ci, carry):
            r0 = pl.multiple_of(ci * CONV_CHUNK, CONV_CHUNK)
            dzc_c = dsh[0, pl.ds(r0, CONV_CHUNK), :]
            dz = jnp.zeros((CONV_CHUNK, D), F32)
            for w in range(CW):
                a, b = divmod(CW - 1 - w, 8)
                dz = dz + k_ref[w:w + 1, :] * dsh[b, pl.ds(r0 + 8 * a, CONV_CHUNK), :]
                a, b = divmod(CONV_LEAD + w, 8)
                prod = dzc_c * zsh[b, pl.ds(r0 + 8 * a, CONV_CHUNK), :]
                part = prod[0:8, :]
                for j in range(1, CONV_CHUNK // 8):
                    part = part + prod[8 * j:8 * j + 8, :]
                dk8[w] += part
            ucc = cur_ref[pl.ds(r0, CONV_CHUNK), :].astype(F32)
            sg = _sig(ucc[:, D:])
            duc_ref[pl.ds(r0, CONV_CHUNK), 0:D] = (dz * sg).astype(BF)
            duc_ref[pl.ds(r0, CONV_CHUNK), D:2 * D] = (dz * ucc[:, :D] * sg * (1.0 - sg)).astype(BF)
            return carry

        lax.fori_loop(0, tm // CONV_CHUNK, chunk, 0)

        @pl.when(i == n_tiles - 1)
        def _():
            dk_ref[...] = jnp.sum(dk8[...], axis=1)

    vec = pl.BlockSpec((1, D), lambda i: (0, 0))
    next_halo = pl.BlockSpec((CONV_HALO, D), lambda i: (jnp.minimum((i + 1) * per, last_block), 0))
    return pl.pallas_call(
        body, grid=(n_tiles,),
        in_specs=[_row_tile(tm, 2 * D),
                  pl.BlockSpec((CONV_HALO, 2 * D), lambda i: (jnp.maximum(i * per - 1, 0), 0)),
                  _row_tile(tm, D), next_halo, _row_tile(tm, D), next_halo,
                  _resident((CWP, D)), _resident((1, D)), _resident((1, D))],
        out_specs=[_row_tile(tm, 2 * D), pl.BlockSpec((CWP, D), lambda i: (0, 0)), vec, vec, vec],
        out_shape=[jax.ShapeDtypeStruct((t, 2 * D), BF), jax.ShapeDtypeStruct((CWP, D), F32),
                   jax.ShapeDtypeStruct((1, D), F32), jax.ShapeDtypeStruct((1, D), F32), jax.ShapeDtypeStruct((1, D), F32)],
        scratch_shapes=[pltpu.VMEM((CONV_SHIFTS, ext, D), F32), pltpu.VMEM((CONV_SHIFTS, ext, D), F32),
                        pltpu.VMEM((CWP, 8, D), F32)],
        compiler_params=_params(1), name="conv_bwd")(uc, uc, zc, zc, dzs, dzs, dwk, lng, lnb)


def _norm_rows(xt, g):
    r = lax.rsqrt(jnp.mean(xt * xt, axis=0, keepdims=True) + EPS)
    xh = xt * r
    return xh * g, r, xh


ATT_TQ = 512


def _attn_specs(t, tq):
    per = tq // BLK
    return [pl.BlockSpec((1536, tq), lambda i: (0, i)),
            pl.BlockSpec((512, BLK), lambda i: (2, jnp.maximum(i * per - 1, 0))),
            _resident((HD, 1)), _resident((HD, 1)), _resident((NKV, 1, GRP * BLK)),
            _resident((2, NKV, 2 * BLK, GRP * BLK))]


def _attn_window(hk, sb, qkv_ref, halo_ref, kn_cur, kn_halo):
    v0 = D + NKV * HD + hk * HD
    if sb == 0:
        k_prev = kn_halo[hk]
        v_prev = halo_ref[NKV * HD + hk * HD:NKV * HD + (hk + 1) * HD, :]
    else:
        k_prev = kn_cur[hk][:, (sb - 1) * BLK:sb * BLK]
        v_prev = qkv_ref[v0:v0 + HD, (sb - 1) * BLK:sb * BLK]
    kw = jnp.concatenate([k_prev, kn_cur[hk][:, sb * BLK:(sb + 1) * BLK]], axis=1).astype(BF)
    vw = jnp.concatenate([v_prev, qkv_ref[v0:v0 + HD, sb * BLK:(sb + 1) * BLK]], axis=1)
    return kw, vw


def _attn_probs(kw, qc, bias, sink):
    st = _dot_tn(kw, qc) + bias
    m = jnp.maximum(jnp.max(st, axis=0, keepdims=True), sink)
    p = jnp.exp(st - m)
    e_sink = jnp.exp(sink - m)
    inv = 1.0 / (jnp.sum(p, axis=0, keepdims=True) + e_sink)
    return p * inv, e_sink * inv


def _attn_fwd(qkv_t, qg, kg, sink_rows, bias_t):
    t = qkv_t.shape[1]
    tq = min(ATT_TQ, t)
    n_sub = tq // BLK

    def body(qkv_ref, halo_ref, qg_ref, kg_ref, sink_ref, bias_ref, o_ref):
        i = pl.program_id(0)
        first = (i == 0).astype(jnp.int32)
        kgain = kg_ref[...]
        qgain = qg_ref[...]
        kn_cur = [_norm_rows(qkv_ref[D + h * HD:D + (h + 1) * HD, :].astype(F32), kgain)[0] for h in range(NKV)]
        kn_halo = [_norm_rows(halo_ref[h * HD:(h + 1) * HD, :].astype(F32), kgain)[0] for h in range(NKV)]
        for hk in range(NKV):
            for sb in range(n_sub):
                cols = slice(sb * BLK, (sb + 1) * BLK)
                kw, vw = _attn_window(hk, sb, qkv_ref, halo_ref, kn_cur, kn_halo)
                qc = jnp.concatenate(
                    [_norm_rows(qkv_ref[(GRP * hk + g) * HD:(GRP * hk + g + 1) * HD, cols].astype(F32), qgain)[0] * QK_SCALE
                     for g in range(GRP)], axis=1).astype(BF)
                bias = bias_ref[first, hk] if sb == 0 else bias_ref[0, hk]
                p, _ = _attn_probs(kw, qc, bias, sink_ref[hk])
                o = _dot(vw, p.astype(BF))
                for g in range(GRP):
                    head = GRP * hk + g
                    o_ref[head * HD:(head + 1) * HD, cols] = o[:, g * BLK:(g + 1) * BLK].astype(BF)

    return pl.pallas_call(
        body, grid=(t // tq,),
        in_specs=_attn_specs(t, tq),
        out_specs=pl.BlockSpec((D, tq), lambda i: (0, i)),
        out_shape=jax.ShapeDtypeStruct((D, t), BF),
        compiler_params=_params(1), name="attn_fwd")(qkv_t, qkv_t, qg, kg, sink_rows, bias_t)


def _attn_bwd(qkv_t, do_t, qg, kg, sink_rows, bias_t, deps=()):
    t = qkv_t.shape[1]
    tq = min(ATT_TQ, t)
    n_sub = tq // BLK
    n_tiles = t // tq

    def body(qkv_ref, halo_ref, do_ref, qg_ref, kg_ref, sink_ref, bias_ref,
             dq_ref, ckv_ref, dqg_ref, dsink_ref, dsacc_ref, qg_scr):
        i = pl.program_id(0)

        @pl.when(i == 0)
        def _():
            qg_scr[...] = jnp.zeros_like(qg_scr)
            dsink_ref[...] = jnp.zeros_like(dsink_ref)
            dsacc_ref[...] = jnp.zeros_like(dsacc_ref)

        first = (i == 0).astype(jnp.int32)
        kgain = kg_ref[...]
        qgain = qg_ref[...]
        kn_cur = [_norm_rows(qkv_ref[D + h * HD:D + (h + 1) * HD, :].astype(F32), kgain)[0] for h in range(NKV)]
        kn_halo = [_norm_rows(halo_ref[h * HD:(h + 1) * HD, :].astype(F32), kgain)[0] for h in range(NKV)]
        dqg = jnp.zeros((HD, BLK), F32)
        for hk in range(NKV):
            for sb in range(n_sub):
                cols = slice(sb * BLK, (sb + 1) * BLK)
                kw, vw = _attn_window(hk, sb, qkv_ref, halo_ref, kn_cur, kn_halo)
                qn, qr, qh = [], [], []
                for g in range(GRP):
                    head = GRP * hk + g
                    n_, r_, h_ = _norm_rows(qkv_ref[head * HD:(head + 1) * HD, cols].astype(F32), qgain)
                    qn.append(n_)
                    qr.append(r_)
                    qh.append(h_)
                qc = (jnp.concatenate(qn, axis=1) * QK_SCALE).astype(BF)
                bias = bias_ref[first, hk] if sb == 0 else bias_ref[0, hk]
                p, p_sink = _attn_probs(kw, qc, bias, sink_ref[hk])
                doc = jnp.concatenate([do_ref[(GRP * hk + g) * HD:(GRP * hk + g + 1) * HD, cols] for g in range(GRP)], axis=1)
                dp = _dot_tn(vw, doc)
                delta = jnp.sum(p * dp, axis=0, keepdims=True)
                ds = p * (dp - delta)
                dsink_ref[hk] += -(p_sink * delta)
                dsacc_ref[hk] += ds
                dsb = ds.astype(BF)
                dqc = _dot(kw, dsb) * QK_SCALE
                ckv_ref[sb, hk * HD:(hk + 1) * HD, :] = _dot_nt(qc, dsb)
                ckv_ref[sb, NKV * HD + hk * HD:NKV * HD + (hk + 1) * HD, :] = _dot_nt(doc, p.astype(BF))
                for g in range(GRP):
                    head = GRP * hk + g
                    dqn = dqc[:, g * BLK:(g + 1) * BLK]
                    dqh = dqn * qgain
                    dq = qr[g] * (dqh - qh[g] * jnp.mean(dqh * qh[g], axis=0, keepdims=True))
                    dq_ref[head * HD:(head + 1) * HD, cols] = dq.astype(BF)
                    dqg = dqg + dqn * qh[g]
        qg_scr[...] += dqg

        @pl.when(i == n_tiles - 1)
        def _():
            dqg_ref[...] = jnp.sum(qg_scr[...], axis=1, keepdims=True)

    return _call(
        body, deps, (qkv_t, qkv_t, do_t, qg, kg, sink_rows, bias_t), grid=(n_tiles,),
        in_specs=_attn_specs(t, tq)[:2] + [pl.BlockSpec((D, tq), lambda i: (0, i))] + _attn_specs(t, tq)[2:],
        out_specs=[pl.BlockSpec((D, tq), lambda i: (0, i)),
                   pl.BlockSpec((n_sub, 2 * NKV * HD, 2 * BLK), lambda i: (i, 0, 0)),
                   pl.BlockSpec((HD, 1), lambda i: (0, 0)),
                   pl.BlockSpec((NKV, 1, GRP * BLK), lambda i: (0, 0, 0)),
                   pl.BlockSpec((NKV, 2 * BLK, GRP * BLK), lambda i: (0, 0, 0))],
        out_shape=[jax.ShapeDtypeStruct((D, t), BF),
                   jax.ShapeDtypeStruct((t // BLK, 2 * NKV * HD, 2 * BLK), F32),
                   jax.ShapeDtypeStruct((HD, 1), F32),
                   jax.ShapeDtypeStruct((NKV, 1, GRP * BLK), F32),
                   jax.ShapeDtypeStruct((NKV, 2 * BLK, GRP * BLK), F32)],
        scratch_shapes=[pltpu.VMEM((HD, BLK), F32)],
        compiler_params=_params(1), name="attn_bwd")


def _kv_combine(ckv, qkv_t, kg):
    nb = ckv.shape[0]
    t = nb * BLK
    rows = NKV * HD
    per = min(4, nb)
    steps = nb // per

    def body(c_ref, cn_ref, k_ref, kg_ref, o_ref, dkg_ref, kg_scr):
        n = pl.program_id(0)

        @pl.when(n == 0)
        def _():
            kg_scr[...] = jnp.zeros_like(kg_scr)

        has_next = (n < steps - 1).astype(F32)
        kgain = kg_ref[...]
        dkg = jnp.zeros((HD, BLK), F32)
        for s in range(per):
            cols = slice(s * BLK, (s + 1) * BLK)
            after = c_ref[s + 1, :, :BLK] if s + 1 < per else cn_ref[0, :, :BLK] * has_next
            d = c_ref[s, :, BLK:] + after
            o_ref[rows:, cols] = d[rows:, :].astype(BF)
            for h in range(NKV):
                _, r, kh = _norm_rows(k_ref[h * HD:(h + 1) * HD, cols].astype(F32), kgain)
                dkn = d[h * HD:(h + 1) * HD, :]
                dkh = dkn * kgain
                o_ref[h * HD:(h + 1) * HD, cols] = (r * (dkh - kh * jnp.mean(dkh * kh, axis=0, keepdims=True))).astype(BF)
                dkg = dkg + dkn * kh
        kg_scr[...] += dkg

        @pl.when(n == steps - 1)
        def _():
            dkg_ref[...] = jnp.sum(kg_scr[...], axis=1, keepdims=True)

    return pl.pallas_call(
        body, grid=(steps,),
        in_specs=[pl.BlockSpec((per, 2 * rows, 2 * BLK), lambda n: (n, 0, 0)),
                  pl.BlockSpec((1, 2 * rows, 2 * BLK), lambda n: (jnp.minimum((n + 1) * per, nb - 1), 0, 0)),
                  pl.BlockSpec((rows, per * BLK), lambda n: (D // rows, n)),
                  _resident((HD, 1))],
        out_specs=[pl.BlockSpec((2 * rows, per * BLK), lambda n: (0, n)), pl.BlockSpec((HD, 1), lambda n: (0, 0))],
        out_shape=[jax.ShapeDtypeStruct((2 * rows, t), BF), jax.ShapeDtypeStruct((HD, 1), F32)],
        scratch_shapes=[pltpu.VMEM((HD, BLK), F32)],
        compiler_params=_params(1), name="kv_combine")(ckv, ckv, qkv_t, kg)


def _group_lane_sums(v):
    lane_group = lax.broadcasted_iota(jnp.int32, (1, GRP * BLK), 1) // BLK
    col = lax.broadcasted_iota(jnp.int32, (1, BLK), 1)
    out = jnp.zeros((NKV, BLK), F32)
    for g in range(GRP):
        s = jnp.sum(jnp.where(lane_group == g, v, 0.0), axis=1, keepdims=True)
        out = jnp.where(col == g, s, out)
    return out


def _bias_grad(dsacc, onehot_t):
    per = 8

    def body(ds_ref, oh_ref, o_ref):
        for b in range(per):
            oh = jnp.concatenate([oh_ref[b]] * GRP, axis=1)
            o_ref[b] = _group_lane_sums(jnp.sum(ds_ref[...] * oh[None], axis=1))

    return pl.pallas_call(
        body, grid=(NBUCKET // per,),
        in_specs=[_resident((NKV, 2 * BLK, GRP * BLK)), pl.BlockSpec((per, 2 * BLK, BLK), lambda b: (b, 0, 0))],
        out_specs=pl.BlockSpec((per, NKV, BLK), lambda b: (b, 0, 0)),
        out_shape=jax.ShapeDtypeStruct((NBUCKET, NKV, BLK), F32),
        compiler_params=_params(1), name="bias_grad")(dsacc, onehot_t)


def _sink_grad(dsink_rows):
    def body(d_ref, o_ref):
        o_ref[...] = _group_lane_sums(d_ref[:, 0, :])

    return pl.pallas_call(body, out_shape=jax.ShapeDtypeStruct((NKV, BLK), F32), name="sink_grad")(dsink_rows)


def _mix_out(zs, o_t, gp, x, w_cp, w_o, w_out):
    t = x.shape[0]
    tm = min(256, t)

    def body(zs_ref, ot_ref, gp_ref, x_ref, wcp_ref, wo_ref, wout_ref, xo_ref, a_ref, b_ref, m_ref):
        a = _dot(zs_ref[...], wcp_ref[...])
        b = _dot_tn(ot_ref[...], wo_ref[...])
        a_ref[...] = a.astype(BF)
        b_ref[...] = b.astype(BF)
        merged = (_sig(gp_ref[:, :D].astype(F32)) * a + _sig(gp_ref[:, D:].astype(F32)) * b).astype(BF)
        m_ref[...] = merged
        xo_ref[...] = x_ref[...] + _dot(merged, wout_ref[...])

    return pl.pallas_call(
        body, grid=(t // tm,),
        in_specs=[_row_tile(tm, D), pl.BlockSpec((D, tm), lambda i: (0, i)), _row_tile(tm, 2 * D), _row_tile(tm, D),
                  _resident((D, D)), _resident((D, D)), _resident((D, D))],
        out_specs=[_row_tile(tm, D)] * 4,
        out_shape=[jax.ShapeDtypeStruct((t, D), F32)] + [jax.ShapeDtypeStruct((t, D), BF)] * 3,
        compiler_params=_params(1), name="mix_out")(zs, o_t, gp, x, w_cp, w_o, w_out)


def _mix_out_bwd(dx, a, b, gp, w_cp, w_o, w_out, deps=()):
    t = dx.shape[0]
    tm = min(256, t)

    def body(dx_ref, a_ref, b_ref, gp_ref, wcp_ref, wo_ref, wout_ref, dzs_ref, dot_ref, dgp_ref, da_ref, db_ref, dxb_ref):
        dxb = dx_ref[...].astype(BF)
        dxb_ref[...] = dxb
        dm = _dot_nt(dxb, wout_ref[...])
        gc = _sig(gp_ref[:, :D].astype(F32))
        ga = _sig(gp_ref[:, D:].astype(F32))
        da = (dm * gc).astype(BF)
        db = (dm * ga).astype(BF)
        da_ref[...] = da
        db_ref[...] = db
        dgp_ref[:, :D] = (dm * a_ref[...].astype(F32) * gc * (1.0 - gc)).astype(BF)
        dgp_ref[:, D:] = (dm * b_ref[...].astype(F32) * ga * (1.0 - ga)).astype(BF)
        dzs_ref[...] = _dot_nt(da, wcp_ref[...])
        dot_ref[...] = _dot_nt(wo_ref[...], db).astype(BF)

    return _call(
        body, deps, (dx, a, b, gp, w_cp, w_o, w_out), grid=(t // tm,),
        in_specs=[_row_tile(tm, D), _row_tile(tm, D), _row_tile(tm, D), _row_tile(tm, 2 * D),
                  _resident((D, D)), _resident((D, D)), _resident((D, D))],
        out_specs=[_row_tile(tm, D), pl.BlockSpec((D, tm), lambda i: (0, i)), _row_tile(tm, 2 * D),
                   _row_tile(tm, D), _row_tile(tm, D), _row_tile(tm, D)],
        out_shape=[jax.ShapeDtypeStruct((t, D), F32), jax.ShapeDtypeStruct((D, t), BF), jax.ShapeDtypeStruct((t, 2 * D), BF),
                   jax.ShapeDtypeStruct((t, D), BF), jax.ShapeDtypeStruct((t, D), BF), jax.ShapeDtypeStruct((t, D), BF)],
        compiler_params=_params(1), name="mix_out_bwd")


def _mix_proj_bwd(dxo, duc, dq_t, dkv_t, dgp, x, g, w_t):
    t = x.shape[0]
    tm = min(256, t)

    def body(dxo_ref, duc_ref, dq_ref, dkv_ref, dgp_ref, x_ref, g_ref, w_ref, dx_ref, dg_ref):
        dn = _dot(duc_ref[...], w_ref[R_CONV[0]:R_CONV[1], :])
        dn = dn + _dot(dgp_ref[...], w_ref[R_GATE[0]:R_GATE[1], :])
        dn = dn + _dot_tn(dq_ref[...], w_ref[R_Q[0]:R_Q[1], :])
        dn = dn + _dot_tn(dkv_ref[...], w_ref[R_KV[0]:R_KV[1], :])
        dx, dg = _rms_bwd(dn, x_ref[...], g_ref[...])
        dx_ref[...] = dxo_ref[...] + dx

        @pl.when(pl.program_id(0) == 0)
        def _():
            dg_ref[...] = jnp.zeros_like(dg_ref)

        dg_ref[...] += dg

    return pl.pallas_call(
        body, grid=(t // tm,),
        in_specs=[_row_tile(tm, D), _row_tile(tm, 2 * D), pl.BlockSpec((D, tm), lambda i: (0, i)),
                  pl.BlockSpec((2 * NKV * HD, tm), lambda i: (0, i)), _row_tile(tm, 2 * D), _row_tile(tm, D),
                  _resident((1, D)), _resident((INW, D))],
        out_specs=[_row_tile(tm, D), pl.BlockSpec((1, D), lambda i: (0, 0))],
        out_shape=[jax.ShapeDtypeStruct((t, D), F32), jax.ShapeDtypeStruct((1, D), F32)],
        compiler_params=_params(1), name="mix_proj_bwd")(dxo, duc, dq_t, dkv_t, dgp, x, g, w_t)


def _attention_tables():
    kj = np.arange(2 * BLK)[:, None]
    qi = np.arange(BLK)[None, :]
    dist = qi + BLK - kj
    in_win = (dist >= 0) & (dist < BLK)
    dpos = np.maximum(dist, 0)
    max_exact = NBUCKET // 2
    dfl = np.maximum(dpos, 1).astype(np.float32)
    large = max_exact + (np.log(dfl / np.float32(max_exact)) / np.float32(math.log(BLK / max_exact))
                         * np.float32(NBUCKET - max_exact)).astype(np.int32)
    large = np.minimum(large, NBUCKET - 1)
    bucket = np.where(dpos < max_exact, dpos, large)
    onehot = (bucket[None] == np.arange(NBUCKET)[:, None, None]).astype(np.float32)
    mask = in_win.astype(np.float32)
    mask_first = mask * (kj >= BLK)
    masks = np.stack([np.tile(mask, (1, GRP)), np.tile(mask_first, (1, GRP))])
    return onehot, masks


def _bias_table(rel_bias, onehot):
    tab = jnp.einsum("bkq,bh->hkq", onehot, rel_bias, precision=lax.Precision.HIGHEST)
    tab = tab.reshape(NKV, GRP, 2 * BLK, BLK)
    return jnp.transpose(tab, (0, 2, 1, 3)).reshape(NKV, 2 * BLK, GRP * BLK)


def _local_step(x, target, vec, weights_of, wgrad, grads_done, small_done):
    onehot_np, masks_np = _attention_tables()
    onehot = jnp.asarray(onehot_np)
    masks = jnp.asarray(masks_np)
    bias_t = jnp.where(masks[:, None] > 0.5, _bias_table(vec["rel_bias"], onehot)[None], NEG)
    sink_rows = jnp.repeat(vec["attn_sinks"].reshape(NKV, 1, GRP), BLK, axis=2)
    qg = vec["q_norm"].reshape(HD, 1)
    kg = vec["k_norm"].reshape(HD, 1)
    g1 = vec["ffn1_norm"].reshape(1, D)
    gm = vec["mix_norm"].reshape(1, D)
    g2 = vec["ffn2_norm"].reshape(1, D)
    dwb = vec["conv_dw_bias"].reshape(1, D)
    lng = vec["conv_ln_g"].reshape(1, D)
    lnb = vec["conv_ln_b"].reshape(1, D)

    w1 = weights_of("ffn1", x)
    n1, u1, x1 = _ffn_fwd(x, g1, w1["ffn1_w_in"], w1["ffn1_w_out"], "ffn1_fwd")
    wm = weights_of("mix", x1)
    dwk = jnp.pad(wm["conv_dw_kernel"], ((0, CWP - CW), (0, 0)))
    hm, uc, gp, qkv_t = _mix_proj(x1, gm, wm["w_in"])
    zs, zc = _conv_fwd(uc, dwk, dwb, lng, lnb)
    o_t = _attn_fwd(qkv_t, qg, kg, sink_rows, bias_t)
    x2, a, b, merged = _mix_out(zs, o_t, gp, x1, wm["conv_w_proj"], wm["attn_w_o"], wm["w_out"])
    w2 = weights_of("ffn2", x2)
    n2, u2, dx3, sq = _ffn_fwd(x2, g2, w2["ffn2_w_in"], w2["ffn2_w_out"], "ffn2_fwd", target=target)

    gv = {}
    dx2, du2, h2, dy2, gv["ffn2_norm"] = _ffn_bwd(dx3, x2, g2, u2, w2["ffn2_w_in"], w2["ffn2_w_out"], "ffn2_bwd")
    deps = grads_done("ffn2", {"ffn2_w_in": wgrad(du2, n2, "ffn2_dw_in", False),
                               "ffn2_w_out": wgrad(h2, dy2, "ffn2_dw_out", False)})

    dzs, do_t, dgp, da, db, dx2b = _mix_out_bwd(dx2, a, b, gp, wm["conv_w_proj"], wm["attn_w_o"], wm["w_out"], deps=deps)
    deps = grads_done("mix_out", {"w_out": wgrad(merged, dx2b, "mix_dw_out", False),
                                  "conv_w_proj": wgrad(zs, da, "mix_dw_cp", False),
                                  "attn_w_o": wgrad(o_t, db, "mix_dw_o", True)})

    dq_t, ckv, dqg, dsink_rows, dsacc = _attn_bwd(qkv_t, do_t, qg, kg, sink_rows, bias_t, deps=deps)
    dkv_t, dkg = _kv_combine(ckv, qkv_t, kg)
    gv["q_norm"] = dqg.reshape(HD)
    gv["k_norm"] = dkg.reshape(HD)
    gv["attn_sinks"] = _sink_grad(dsink_rows)[:, :GRP].reshape(NQ)
    gv["rel_bias"] = _bias_grad(dsacc, onehot)[:, :, :GRP].reshape(NBUCKET, NQ)

    duc, dk_conv, gv["conv_dw_bias"], gv["conv_ln_g"], gv["conv_ln_b"] = _conv_bwd(uc, zc, dzs, dwk, lng, lnb)
    gv["conv_dw_kernel"] = dk_conv[:CW]

    dx1, gv["mix_norm"] = _mix_proj_bwd(dx2, duc, dq_t, dkv_t, dgp, x1, gm, wm["w_in"])
    deps = grads_done("mix_in", {"w_in": _wgrad_mix(duc, dq_t, dkv_t, dgp, hm)})

    dx0, du1, h1, dy1, gv["ffn1_norm"] = _ffn_bwd(dx1, x, g1, u1, w1["ffn1_w_in"], w1["ffn1_w_out"], "ffn1_bwd", deps=deps)
    for k in ("ffn1_norm", "mix_norm", "ffn2_norm", "conv_dw_bias", "conv_ln_g", "conv_ln_b"):
        gv[k] = gv[k].reshape(D)
    deps = small_done(gv, sq)
    deps = grads_done("ffn1_out", {"ffn1_w_out": wgrad(h1, dy1, "ffn1_dw_out", False, deps)})
    grads_done("ffn1_in", {"ffn1_w_in": wgrad(du1, n1, "ffn1_dw_in", False, deps)})
    return dx0


MESH_ID = pl.DeviceIdType.MESH


def _position():
    return lax.axis_index("x"), lax.axis_index("y"), lax.axis_index("c")


def _shard_rows(ref, index, rows):
    return ref.at[pl.ds(pl.multiple_of(index * rows, 16), rows), :]


def _prep(weights, taps, me):
    n = len(weights)

    def body(me_ref, *refs):
        for k in range(n):
            refs[n + 1 + k][...] = refs[k][...].astype(BF)
        refs[2 * n + 1][0:CW, :] = refs[n][...]
        refs[2 * n + 1][CW:, :] = jnp.zeros((CWP - CW, BLK), F32)

    shard_shapes = [w.shape for w in weights] + [(CWP, BLK)]
    dtypes = [BF] * n + [F32]
    ins = list(weights) + [taps]
    return pl.pallas_call(
        body,
        grid_spec=pltpu.PrefetchScalarGridSpec(
            num_scalar_prefetch=1, grid=(1,),
            in_specs=[pl.BlockSpec(a.shape, lambda i, m: (0, 0), pipeline_mode=pl.Buffered(1)) for a in ins],
            out_specs=[pl.BlockSpec(s, lambda i, m: (m[0], 0)) for s in shard_shapes]),
        out_shape=[jax.ShapeDtypeStruct((N_DEV * s[0], s[1]), d) for s, d in zip(shard_shapes, dtypes)],
        compiler_params=_params(1), name="prep")(me, *ins)


HBM = pl.BlockSpec(memory_space=pltpu.HBM)
SEM = pl.BlockSpec(memory_space=pltpu.SEMAPHORE)
DATAFLOW = pltpu.SideEffectType.DATAFLOW_SIDE_EFFECTING
TOKEN = jax.ShapeDtypeStruct((8, 128), F32)


def _in_hbm(x):
    return pltpu.with_memory_space_constraint(x, pltpu.HBM)


def _hbm_like(arrays):
    return [pltpu.HBM(a.shape, a.dtype) for a in arrays]


def _other_chips(x, y):
    return [(1 - x, y), (x, 1 - y), (1 - x, 1 - y)]


def _device_index(chip, c):
    return 4 * chip[0] + 2 * chip[1] + c


def _chip_index(chip):
    return 2 * chip[0] + chip[1]


class _Exchange:
    def __init__(self, gather):
        self.gather = gather

    def sent(self, x, y, c, chip):
        return _device_index((x, y), c) if self.gather else _chip_index(chip)

    def lands_at(self, x, y, c):
        return _device_index((x, y), c) if self.gather else _chip_index((x, y))

    def arrives_at(self, chip, c):
        return _device_index(chip, c) if self.gather else _chip_index(chip)


def _ici_copies_start(sets, sources, landings, exchange, name, deps=()):
    n = len(landings)
    arrays = (list(sources) if sources is not None else []) + list(landings)
    first_land = len(arrays) - n
    n_sets = len(sets)
    n_deps = len(deps)

    def body(*refs):
        refs = refs[n_deps:]
        src, land = refs[:n], refs[first_land:first_land + n]
        sems = refs[len(arrays):len(arrays) + 2 * n_sets]
        token = refs[-1]
        x, y, c = _position()
        for s, members in enumerate(sets):
            for slot, (k, rows) in enumerate(members):
                for j, chip in enumerate(_other_chips(x, y)):
                    pltpu.make_async_remote_copy(
                        src_ref=_shard_rows(src[k], exchange.sent(x, y, c, chip), rows),
                        dst_ref=_shard_rows(land[k], exchange.lands_at(x, y, c), rows),
                        send_sem=sems[2 * s].at[3 * slot + j], recv_sem=sems[2 * s + 1].at[3 * slot + j],
                        device_id=(*chip, c), device_id_type=MESH_ID).start()
        token[...] = jnp.zeros_like(token)

    sem_shapes = []
    for members in sets:
        sem_shapes += [pltpu.SemaphoreType.DMA((3 * len(members),))] * 2
    out = pl.pallas_call(
        body, name=name,
        out_shape=sem_shapes + _hbm_like(arrays) + [TOKEN],
        in_specs=[ANY] * n_deps + [HBM] * len(arrays),
        out_specs=[SEM] * (2 * n_sets) + [HBM] * len(arrays) + [pl.BlockSpec(memory_space=pltpu.VMEM)],
        input_output_aliases={n_deps + i: 2 * n_sets + i for i in range(len(arrays))},
        compiler_params=pltpu.CompilerParams(has_side_effects=DATAFLOW),
    )(*deps, *[_in_hbm(a) for a in arrays])
    sems = [(out[2 * s], out[2 * s + 1]) for s in range(n_sets)]
    thru = list(out[2 * n_sets:2 * n_sets + len(arrays)])
    return sems, (thru[:first_land] if sources is not None else None), thru[first_land:], out[-1]


def _ici_copies_wait(sems, members, sources, landings, exchange, after, name):
    n = len(landings)
    arrays = (list(sources) if sources is not None else []) + list(landings)
    first_land = len(arrays) - n

    def body(*refs):
        src, land = refs[:n], refs[first_land:first_land + n]
        send_sems, recv_sems = refs[len(arrays)], refs[len(arrays) + 1]
        x, y, c = _position()
        for slot, rows in enumerate(members):
            for j, chip in enumerate(_other_chips(x, y)):
                cp = pltpu.make_async_remote_copy(
                    src_ref=_shard_rows(src[slot], exchange.sent(x, y, c, chip), rows),
                    dst_ref=_shard_rows(land[slot], exchange.arrives_at(chip, c), rows),
                    send_sem=send_sems.at[3 * slot + j], recv_sem=recv_sems.at[3 * slot + j],
                    device_id=(*chip, c), device_id_type=MESH_ID)
                cp.wait_send()
                cp.wait_recv()

    out = pl.pallas_call(
        body, name=name, out_shape=_hbm_like(arrays),
        in_specs=[HBM] * len(arrays) + [SEM, SEM] + [ANY] * len(after), out_specs=[HBM] * len(arrays),
        input_output_aliases={i: i for i in range(len(arrays))},
        compiler_params=pltpu.CompilerParams(has_side_effects=DATAFLOW),
    )(*arrays, sems[0], sems[1], *after)
    return list(out[first_land:])


def _d2d_gather(buffers, rows, name):
    n = len(buffers)

    def body(*refs):
        land = refs[n:2 * n]
        send_sems, recv_sems = refs[2 * n:]
        x, y, c = _position()
        chips = [(x, y)] + _other_chips(x, y)
        sends, recvs = [], []
        for k in range(n):
            for j, chip in enumerate(chips):
                for copies, core in ((sends, c), (recvs, 1 - c)):
                    block = _shard_rows(land[k], _device_index(chip, core), rows[k])
                    copies.append(pltpu.make_async_remote_copy(
                        src_ref=block, dst_ref=block, send_sem=send_sems.at[k, j], recv_sem=recv_sems.at[k, j],
                        device_id=(x, y, 1 - c), device_id_type=MESH_ID))
        for cp in sends:
            cp.start()
        for cp in recvs:
            cp.wait_recv()
        for cp in sends:
            cp.wait_send()

    return pl.pallas_call(
        body, name=name, out_shape=[jax.ShapeDtypeStruct(a.shape, a.dtype) for a in buffers],
        in_specs=[ANY] * n, out_specs=[ANY] * n, input_output_aliases={i: i for i in range(n)},
        scratch_shapes=[pltpu.SemaphoreType.DMA((n, 4)), pltpu.SemaphoreType.DMA((n, 4))],
    )(*buffers)


def _rs_pair(grads, name):
    n = len(grads)
    rows = [g.shape[0] // N_DEV for g in grads]

    def body(*refs):
        ins, outs = refs[:n], refs[n:2 * n]
        send_sems, recv_sems = refs[2 * n:]
        x, y, c = _position()
        copies = []
        for k in range(n):
            for q in range(4):
                copies.append(pltpu.make_async_remote_copy(
                    src_ref=_shard_rows(ins[k], 2 * q + 1 - c, rows[k]), dst_ref=_shard_rows(outs[k], q, rows[k]),
                    send_sem=send_sems.at[k, q], recv_sem=recv_sems.at[k, q], device_id=(x, y, 1 - c),
                    device_id_type=MESH_ID))
        for cp in copies:
            cp.start()
        for cp in copies:
            cp.wait()

    return pl.pallas_call(
        body, out_shape=[jax.ShapeDtypeStruct((4 * r, g.shape[1]), g.dtype) for g, r in zip(grads, rows)],
        in_specs=[ANY] * n, out_specs=[ANY] * n,
        scratch_shapes=[pltpu.SemaphoreType.DMA((n, 4)), pltpu.SemaphoreType.DMA((n, 4))],
        name=name)(*grads)


def _wgrad_pair(lhs, rhs, name, *, lhs_is_transposed, deps=()):
    t = rhs.shape[0]
    n = lhs.shape[0] if lhs_is_transposed else lhs.shape[1]
    r = n // N_DEV
    n_chips = N_DEV // 2
    per = 1 if (2 * r) % BLK == 0 else 2
    steps = n_chips // per

    def body(l_ref, r_ref, kept_ref, recv_ref, res, send_sems, recv_sems):
        q = pl.program_id(0)
        slot = q % 2
        x, y, c = _position()

        def send(step, buf, i):
            return pltpu.make_async_remote_copy(
                src_ref=res.at[buf, pl.ds(pl.multiple_of((2 * i + 1 - c) * r, 16), r), :],
                dst_ref=_shard_rows(recv_ref, step * per + i, r),
                send_sem=send_sems.at[buf, i], recv_sem=recv_sems.at[step * per + i],
                device_id=(x, y, 1 - c), device_id_type=MESH_ID)

        @pl.when(q >= 2)
        def _():
            for i in range(per):
                send(q - 2, slot, i).wait_send()

        if lhs_is_transposed:
            res[slot] = _dot(l_ref[...], r_ref[...]).astype(BF)
        else:
            res[slot] = _dot_tn(l_ref[...], r_ref[...]).astype(BF)
        for i in range(per):
            kept_ref[i * r:(i + 1) * r, :] = res[slot, pl.ds(pl.multiple_of((2 * i + c) * r, 16), r), :]
            send(q, slot, i).start()

        @pl.when(q == steps - 1)
        def _():
            for i in range(per):
                if steps > 1:
                    send(q - 1, 1 - slot, i).wait_send()
                send(q, slot, i).wait_send()
            for chip in range(n_chips):
                send(chip // per, 0, chip % per).wait_recv()

    width = 2 * r * per
    lhs_spec = pl.BlockSpec((width, t), lambda q: (q, 0)) if lhs_is_transposed else pl.BlockSpec((t, width), lambda q: (0, q))
    return _call(
        body, deps, (lhs, rhs), grid=(steps,),
        in_specs=[lhs_spec, _resident((t, D))],
        out_specs=[pl.BlockSpec((per * r, D), lambda q: (q, 0)), ANY],
        out_shape=[jax.ShapeDtypeStruct((n // 2, D), BF)] * 2,
        scratch_shapes=[pltpu.VMEM((2, width, D), BF), pltpu.SemaphoreType.DMA((2, per)),
                        pltpu.SemaphoreType.DMA((n_chips,))],
        compiler_params=_params(1), name=name)


def _pair_add(grad, received, place, name, kept_only=False):
    r = received.shape[0] // 4
    tr = 352 if r % 352 == 0 else r
    per = r // tr
    parity = 0 if kept_only else 1

    def body(place_ref, g_ref, r_ref, o_ref, land_ref):
        total = (g_ref[...].astype(F32) + r_ref[...].astype(F32)).astype(BF)
        o_ref[...] = total

        @pl.when(pl.program_id(1) == place_ref[1])
        def _():
            land_ref[...] = total

    return pl.pallas_call(
        body,
        grid_spec=pltpu.PrefetchScalarGridSpec(
            num_scalar_prefetch=1, grid=(per, 4),
            in_specs=[pl.BlockSpec((tr, D), lambda i, q, p: (((1 + parity) * q + parity * p[0]) * per + i, 0)),
                      pl.BlockSpec((tr, D), lambda i, q, p: (q * per + i, 0))],
            out_specs=[pl.BlockSpec((tr, D), lambda i, q, p: (q * per + i, 0)),
                       pl.BlockSpec((tr, D), lambda i, q, p: (p[1] * per + i, 0))]),
        out_shape=[jax.ShapeDtypeStruct(received.shape, BF)] * 2,
        compiler_params=_params(2), name=name)(place, grad, received)


def _all_reduce_small(payload, deps=()):
    r = payload.shape[0]

    def body(in_ref, out_ref, land_ref, send_sems, recv_sems):
        x, y, c = _position()
        me = 4 * x + 2 * y + c
        land_ref[me] = in_ref[...]
        copies = []
        for k in range(1, N_DEV):
            peer = (x ^ (k >> 2), y ^ ((k >> 1) & 1), c ^ (k & 1))
            copies.append(pltpu.make_async_remote_copy(
                src_ref=in_ref, dst_ref=land_ref.at[me], send_sem=send_sems.at[k - 1], recv_sem=recv_sems.at[k - 1],
                device_id=peer, device_id_type=MESH_ID))
        for cp in copies:
            cp.start()
        for k in range(1, N_DEV):
            peer_index = me ^ k
            pltpu.make_async_remote_copy(
                src_ref=in_ref, dst_ref=land_ref.at[peer_index], send_sem=send_sems.at[k - 1], recv_sem=recv_sems.at[k - 1],
                device_id=(x, y, c), device_id_type=MESH_ID).wait_recv()
        for cp in copies:
            cp.wait_send()
        acc = land_ref[0]
        for d in range(1, N_DEV):
            acc = acc + land_ref[d]
        out_ref[...] = acc

    return _call(
        body, deps, (payload,), out_shape=jax.ShapeDtypeStruct((r, D), F32),
        in_specs=[pl.BlockSpec(memory_space=pltpu.VMEM)], out_specs=pl.BlockSpec(memory_space=pltpu.VMEM),
        scratch_shapes=[pltpu.VMEM((N_DEV, r, D), F32), pltpu.SemaphoreType.DMA((N_DEV - 1,)),
                        pltpu.SemaphoreType.DMA((N_DEV - 1,))],
        name="all_reduce_small")


def _adamw_math(w, g, m, v):
    m = ADAM_B1 * m + (1.0 - ADAM_B1) * g
    v = ADAM_B2 * v + (1.0 - ADAM_B2) * (g * g)
    m_hat = m / (1.0 - ADAM_B1 ** ADAM_STEP)
    v_hat = v / (1.0 - ADAM_B2 ** ADAM_STEP)
    delta = -ADAM_LR * (m_hat / (jnp.sqrt(v_hat) + ADAM_EPS) + ADAM_WD * w)
    return delta, m, v


def _sum_partials(blocks):
    g = blocks[0].astype(F32)
    for blk in blocks[1:]:
        g = g + blk.astype(F32)
    return g


def _reduce_adamw(landed, w, m, v, name):
    r = w.shape[0]
    tr = 176 if r % 176 == 0 else r
    per = r // tr

    def body(r0, r1, r2, r3, w_ref, m_ref, v_ref, g_ref, d_ref, nm_ref, nv_ref):
        g = _sum_partials([r0[...], r1[...], r2[...], r3[...]])
        g_ref[...] = g
        d_ref[...], nm_ref[...], nv_ref[...] = _adamw_math(w_ref[...], g, m_ref[...], v_ref[...])

    tile = _row_tile(tr, D)
    return pl.pallas_call(
        body, grid=(per,),
        in_specs=[pl.BlockSpec((tr, D), lambda i, q=q: (q * per + i, 0)) for q in range(4)] + [tile] * 3,
        out_specs=[tile] * 4, out_shape=[jax.ShapeDtypeStruct(w.shape, F32)] * 4,
        compiler_params=_params(1), name=name)(landed, landed, landed, landed, w, m, v)


def _adamw_small(w, g, m, v, name):
    def body(w_ref, g_ref, m_ref, v_ref, d_ref, nm_ref, nv_ref):
        d_ref[...], nm_ref[...], nv_ref[...] = _adamw_math(w_ref[...], g_ref[...], m_ref[...], v_ref[...])

    return pl.pallas_call(body, out_shape=[jax.ShapeDtypeStruct(w.shape, F32)] * 3, name=name)(w, g, m, v)


WEIGHTS = ("ffn1_norm", "ffn1_w_in", "ffn1_w_out", "mix_norm", "w_in", "conv_dw_kernel", "conv_dw_bias", "conv_ln_g",
           "conv_ln_b", "conv_w_proj", "q_norm", "k_norm", "attn_sinks", "rel_bias", "attn_w_o", "w_out", "ffn2_norm",
           "ffn2_w_in", "ffn2_w_out")
MATRICES = ("ffn1_w_in", "ffn1_w_out", "w_in", "conv_w_proj", "attn_w_o", "w_out", "ffn2_w_in", "ffn2_w_out")
COLUMN_SHARDED = ("ffn1_w_in", "w_in", "ffn2_w_in")
ROW_VECTORS = ("ffn1_norm", "mix_norm", "conv_dw_bias", "conv_ln_g", "conv_ln_b", "ffn2_norm")
PACKED = (("q_norm", HD), ("k_norm", HD), ("attn_sinks", NQ), ("rel_bias", NBUCKET * NQ))
GATHER = _Exchange(gather=True)
SCATTER = _Exchange(gather=False)
GATHER_STAGES = ("ffn1", "mix", "ffn2")
STAGE_MEMBERS = {"ffn1": ("ffn1_w_in", "ffn1_w_out"), "mix": ("w_in", "conv_w_proj", "attn_w_o", "w_out", "taps"),
                 "ffn2": ("ffn2_w_in", "ffn2_w_out")}
ROW_PACKED = len(ROW_VECTORS)
ROW_LOSS = ROW_PACKED + 1
ROW_TAPS = 8
PAYLOAD_ROWS = ROW_TAPS + CWP


def _pack_small(values, last_row):
    packed = jnp.concatenate([values[k].reshape(-1) for k, _ in PACKED])
    packed = jnp.pad(packed, (0, D - packed.shape[0])).reshape(1, D)
    return jnp.concatenate([values[k].reshape(1, D) for k in ROW_VECTORS] + [packed, last_row], axis=0)


def _unpack_small(rows):
    out = {k: rows[i] for i, k in enumerate(ROW_VECTORS)}
    at = 0
    for k, size in PACKED:
        out[k] = rows[ROW_PACKED, at:at + size]
        at += size
    out["rel_bias"] = out["rel_bias"].reshape(NBUCKET, NQ)
    return out


def kernel(x, ffn1_norm, ffn1_w_in, ffn1_w_out, mix_norm, w_in, conv_dw_kernel, conv_dw_bias, conv_ln_g, conv_ln_b, conv_w_proj, q_norm, k_norm, attn_sinks, rel_bias, attn_w_o, w_out, ffn2_norm, ffn2_w_in, ffn2_w_out, loss_target, m_ffn1_norm, m_ffn1_w_in, m_ffn1_w_out, m_mix_norm, m_w_in, m_conv_dw_kernel, m_conv_dw_bias, m_conv_ln_g, m_conv_ln_b, m_conv_w_proj, m_q_norm, m_k_norm, m_attn_sinks, m_rel_bias, m_attn_w_o, m_w_out, m_ffn2_norm, m_ffn2_w_in, m_ffn2_w_out, v_ffn1_norm, v_ffn1_w_in, v_ffn1_w_out, v_mix_norm, v_w_in, v_conv_dw_kernel, v_conv_dw_bias, v_conv_ln_g, v_conv_ln_b, v_conv_w_proj, v_q_norm, v_k_norm, v_attn_sinks, v_rel_bias, v_attn_w_o, v_w_out, v_ffn2_norm, v_ffn2_w_in, v_ffn2_w_out):
    w = dict(ffn1_norm=ffn1_norm, ffn1_w_in=ffn1_w_in, ffn1_w_out=ffn1_w_out, mix_norm=mix_norm, w_in=w_in,
             conv_dw_kernel=conv_dw_kernel, conv_dw_bias=conv_dw_bias, conv_ln_g=conv_ln_g, conv_ln_b=conv_ln_b,
             conv_w_proj=conv_w_proj, q_norm=q_norm, k_norm=k_norm, attn_sinks=attn_sinks, rel_bias=rel_bias,
             attn_w_o=attn_w_o, w_out=w_out, ffn2_norm=ffn2_norm, ffn2_w_in=ffn2_w_in, ffn2_w_out=ffn2_w_out)
    m = dict(ffn1_norm=m_ffn1_norm, ffn1_w_in=m_ffn1_w_in, ffn1_w_out=m_ffn1_w_out, mix_norm=m_mix_norm, w_in=m_w_in,
             conv_dw_kernel=m_conv_dw_kernel, conv_dw_bias=m_conv_dw_bias, conv_ln_g=m_conv_ln_g, conv_ln_b=m_conv_ln_b,
             conv_w_proj=m_conv_w_proj, q_norm=m_q_norm, k_norm=m_k_norm, attn_sinks=m_attn_sinks, rel_bias=m_rel_bias,
             attn_w_o=m_attn_w_o, w_out=m_w_out, ffn2_norm=m_ffn2_norm, ffn2_w_in=m_ffn2_w_in, ffn2_w_out=m_ffn2_w_out)
    v = dict(ffn1_norm=v_ffn1_norm, ffn1_w_in=v_ffn1_w_in, ffn1_w_out=v_ffn1_w_out, mix_norm=v_mix_norm, w_in=v_w_in,
             conv_dw_kernel=v_conv_dw_kernel, conv_dw_bias=v_conv_dw_bias, conv_ln_g=v_conv_ln_g, conv_ln_b=v_conv_ln_b,
             conv_w_proj=v_conv_w_proj, q_norm=v_q_norm, k_norm=v_k_norm, attn_sinks=v_attn_sinks, rel_bias=v_rel_bias,
             attn_w_o=v_attn_w_o, w_out=v_w_out, ffn2_norm=v_ffn2_norm, ffn2_w_in=v_ffn2_w_in, ffn2_w_out=v_ffn2_w_out)
    px, py, pc = _position()
    me = 4 * px + 2 * py + pc
    place = jnp.stack([pc, 2 * px + py]).astype(jnp.int32)

    rows_of = lambda k, a: a.T if k in COLUMN_SHARDED else a
    buffers = dict(zip(MATRICES + ("taps",), _prep([rows_of(k, w[k]) for k in MATRICES], conv_dw_kernel,
                                                   me.astype(jnp.int32).reshape(1))))
    landings, sets = [], []
    for stage in GATHER_STAGES:
        sets.append([(len(landings) + i, buffers[k].shape[0] // N_DEV) for i, k in enumerate(STAGE_MEMBERS[stage])])
        landings += [buffers[k] for k in STAGE_MEMBERS[stage]]
    sems, _, land_thru, _ = _ici_copies_start(sets, None, landings, GATHER, "gather_start")

    def weights_of(stage, after):
        s = GATHER_STAGES.index(stage)
        rows = [r for _, r in sets[s]]
        landed = _ici_copies_wait(sems[s], rows, None, [land_thru[k] for k, _ in sets[s]], GATHER, [after],
                                  "gather_wait_" + stage)
        out = dict(zip(STAGE_MEMBERS[stage], _d2d_gather(landed, rows, "gather_d2d_" + stage)))
        if "taps" in out:
            taps = out.pop("taps")
            out["conv_dw_kernel"] = jnp.transpose(taps.reshape(N_DEV, CWP, BLK), (1, 0, 2)).reshape(CWP, D)[:CW]
        return out

    in_flight = []

    def wgrad(lhs, rhs, name, lhs_is_transposed, deps=()):
        return _wgrad_pair(lhs, rhs, name, lhs_is_transposed=lhs_is_transposed, deps=deps)

    def grads_done(stage, grads):
        names = list(grads)
        added = []
        for k in names:
            if isinstance(grads[k], (tuple, list)):
                kept, received = grads[k]
                added.append(_pair_add(kept, received, place, "pair_add_" + k, kept_only=True))
            else:
                received, = _rs_pair([grads[k]], "rs_pair_" + k)
                added.append(_pair_add(grads[k], received, place, "pair_add_" + k))
        partials = [p for p, _ in added]
        members = [(i, p.shape[0] // 4) for i, p in enumerate(partials)]
        sem, p_thru, l_thru, token = _ici_copies_start([members], partials, [l for _, l in added], SCATTER,
                                                       "scatter_start_" + stage)
        in_flight.append((stage, names, sem[0], p_thru, l_thru, token))
        return [token]

    reduced = []

    def small_done(gv, sq):
        payload = jnp.concatenate([_pack_small(gv, sq), jnp.pad(gv["conv_dw_kernel"], ((0, CWP - CW), (0, 0)))], axis=0)
        reduced.append(_all_reduce_small(payload))
        return reduced

    vec = {k: w[k] for k in WEIGHTS if k not in MATRICES and k != "conv_dw_kernel"}
    dx0 = _local_step(x[0], loss_target[0], vec, weights_of, wgrad, grads_done, small_done)
    total = reduced[0]
    loss = (0.5 / D) * jnp.sum(total[ROW_LOSS])

    grads, delta, new_m, new_v = {}, {}, {}, {}
    after = [in_flight[-1][-1]]
    for stage, names, sem, p_thru, l_thru, _ in in_flight:
        landed = _ici_copies_wait(sem, [p.shape[0] // 4 for p in p_thru], p_thru, l_thru, SCATTER, after,
                                  "scatter_wait_" + stage)
        after = []
        for k, buf in zip(names, landed):
            out = _reduce_adamw(buf, rows_of(k, w[k]), rows_of(k, m[k]), rows_of(k, v[k]), "adamw_" + k)
            grads[k], delta[k], new_m[k], new_v[k] = [rows_of(k, a) for a in out]
            after.append(out[1])
    zero_row = jnp.zeros((1, D), F32)
    d8, m8, v8 = _adamw_small(_pack_small(w, zero_row), total[:ROW_TAPS], _pack_small(m, zero_row),
                              _pack_small(v, zero_row), "adamw_small")
    grads.update(_unpack_small(total[:ROW_TAPS]))
    delta.update(_unpack_small(d8))
    new_m.update(_unpack_small(m8))
    new_v.update(_unpack_small(v8))
    k = "conv_dw_kernel"
    grads[k] = lax.dynamic_slice_in_dim(total[ROW_TAPS:ROW_TAPS + CW], me * BLK, BLK, axis=1)
    delta[k], new_m[k], new_v[k] = _adamw_small(w[k], grads[k], m[k], v[k], "adamw_taps")

    return (loss, dx0[None], *[grads[k] for k in WEIGHTS], *[delta[k] for k in WEIGHTS],
            *[new_m[k] for k in WEIGHTS], *[new_v[k] for k in WEIGHTS])
```

```python
import functools
import math

import numpy as np
import jax
import jax.numpy as jnp
from jax import lax
from jax.experimental import pallas as pl
from jax.experimental.pallas import tpu as pltpu

F32 = jnp.float32
BF = jnp.bfloat16

D = 1024
F = 2816
INW = 5632
CW = 31
CWP = 32
HD = 64
NQ = 16
NKV = 4
GRP = NQ // NKV
BLK = 128
NBUCKET = 32
EPS = 1e-6
NEG = float(jnp.finfo(jnp.float32).min)
QK_SCALE = 1.0 / math.sqrt(HD)
R_CONV = (0, 2048)
R_QKV = (2048, 3584)
R_Q = (2048, 3072)
R_KV = (3072, 3584)
R_GATE = (3584, 5632)

N_DEV = 8
VMEM_LIMIT_V7X = 56 * 1024 * 1024
ROW_TILE = 256
ROW_TILE_WIDE = 512

ADAM_LR = 0.001
ADAM_B1 = 0.9
ADAM_B2 = 0.999
ADAM_EPS = 1e-08
ADAM_WD = 0.01
ADAM_STEP = 10

NT_DIMS = (((1,), (1,)), ((), ()))
TN_DIMS = (((0,), (0,)), ((), ()))


def _dot(a, b):
    return jnp.dot(a, b, preferred_element_type=F32)


def _dot_nt(a, b):
    return lax.dot_general(a, b, NT_DIMS, preferred_element_type=F32)


def _dot_tn(a, b):
    return lax.dot_general(a, b, TN_DIMS, preferred_element_type=F32)


def _sig(x):
    return 0.5 * jnp.tanh(0.5 * x) + 0.5


ANY = pl.BlockSpec(memory_space=pl.ANY)


def _call(body, deps, args, **kw):
    n = len(deps)
    if n:
        kw["in_specs"] = [ANY] * n + list(kw["in_specs"])
        return pl.pallas_call(lambda *refs: body(*refs[n:]), **kw)(*deps, *args)
    return pl.pallas_call(body, **kw)(*args)


def _params(n_axes):
    return pltpu.CompilerParams(dimension_semantics=("arbitrary",) * n_axes, vmem_limit_bytes=VMEM_LIMIT_V7X)


def _resident(shape):
    zeros = (0,) * len(shape)
    return pl.BlockSpec(shape, lambda *_: zeros, pipeline_mode=pl.Buffered(1))


def _row_tile(rows, cols):
    return pl.BlockSpec((rows, cols), lambda i: (i, 0))


def _rms_stats(x):
    r = lax.rsqrt(jnp.mean(x * x, axis=-1, keepdims=True) + EPS)
    return r, x * r


def _rms_bwd(dn, x, g):
    r, xh = _rms_stats(x)
    dxh = dn * g
    dx = r * (dxh - xh * jnp.mean(dxh * xh, axis=-1, keepdims=True))
    return dx, jnp.sum(dn * xh, axis=0, keepdims=True)


def _ffn_fwd(x, g, w_in_t, w_out, name, target=None):
    t = x.shape[0]
    tm = min(ROW_TILE_WIDE, t)
    with_loss = target is not None

    def body(*refs):
        if with_loss:
            x_ref, g_ref, w_ref, wo_ref, t_ref, n_ref, u_ref, dy_ref, sq_ref = refs
        else:
            x_ref, g_ref, w_ref, wo_ref, n_ref, u_ref, xo_ref = refs
        x = x_ref[...]
        r, xh = _rms_stats(x)
        n = (xh * g_ref[...]).astype(BF)
        n_ref[...] = n
        u = _dot_nt(n, w_ref[...])
        u_ref[...] = u.astype(BF)
        a = u[:, :F]
        b = u[:, F:]
        h = (a * _sig(a) * b).astype(BF)
        xo = x + 0.5 * _dot(h, wo_ref[...])
        if with_loss:
            err = xo - t_ref[...]
            dy_ref[...] = err * (1.0 / D)

            @pl.when(pl.program_id(0) == 0)
            def _():
                sq_ref[...] = jnp.zeros_like(sq_ref)

            sq_ref[...] += jnp.sum(err * err, axis=0, keepdims=True)
        else:
            xo_ref[...] = xo

    in_specs = [_row_tile(tm, D), _resident((1, D)), _resident((INW, D)), _resident((F, D))]
    args = [x, g, w_in_t, w_out]
    out_specs = [_row_tile(tm, D), _row_tile(tm, INW), _row_tile(tm, D)]
    out_shape = [jax.ShapeDtypeStruct((t, D), BF), jax.ShapeDtypeStruct((t, INW), BF), jax.ShapeDtypeStruct((t, D), F32)]
    if with_loss:
        in_specs.append(_row_tile(tm, D))
        args.append(target)
        out_specs.append(pl.BlockSpec((1, D), lambda i: (0, 0)))
        out_shape.append(jax.ShapeDtypeStruct((1, D), F32))
    return pl.pallas_call(body, grid=(t // tm,), in_specs=in_specs, out_specs=out_specs, out_shape=out_shape,
                          compiler_params=_params(1), name=name)(*args)


def _ffn_bwd(dxo, x, g, u, w_in_t, w_out, name, deps=()):
    t = x.shape[0]
    tm = min(ROW_TILE, t)

    def body(dxo_ref, x_ref, g_ref, u_ref, w_ref, wo_ref, dx_ref, du_ref, h_ref, dy_ref, dg_ref):
        dxo = dxo_ref[...]
        dy = (0.5 * dxo).astype(BF)
        dy_ref[...] = dy
        dh = _dot_nt(dy, wo_ref[...])
        a = u_ref[:, :F].astype(F32)
        b = u_ref[:, F:].astype(F32)
        s = _sig(a)
        sa = a * s
        h_ref[...] = (sa * b).astype(BF)
        du_ref[:, :F] = (dh * b * (s * (1.0 + a * (1.0 - s)))).astype(BF)
        du_ref[:, F:] = (dh * sa).astype(BF)
        dn = _dot(du_ref[...], w_ref[...])
        dx, dg = _rms_bwd(dn, x_ref[...], g_ref[...])
        dx_ref[...] = dxo + dx

        @pl.when(pl.program_id(0) == 0)
        def _():
            dg_ref[...] = jnp.zeros_like(dg_ref)

        dg_ref[...] += dg

    return _call(
        body, deps, (dxo, x, g, u, w_in_t, w_out), grid=(t // tm,),
        in_specs=[_row_tile(tm, D), _row_tile(tm, D), _resident((1, D)), _row_tile(tm, INW), _resident((INW, D)),
                  _resident((F, D))],
        out_specs=[_row_tile(tm, D), _row_tile(tm, INW), _row_tile(tm, F), _row_tile(tm, D),
                   pl.BlockSpec((1, D), lambda i: (0, 0))],
        out_shape=[jax.ShapeDtypeStruct((t, D), F32), jax.ShapeDtypeStruct((t, INW), BF), jax.ShapeDtypeStruct((t, F), BF),
                   jax.ShapeDtypeStruct((t, D), BF), jax.ShapeDtypeStruct((1, D), F32)],
        compiler_params=_params(1), name=name)


def _wgrad(lhs, rhs, name, *, lhs_is_transposed, chunk, deps=()):
    t = rhs.shape[0]
    n = lhs.shape[0] if lhs_is_transposed else lhs.shape[1]
    c = min(chunk, n)

    def body(l_ref, r_ref, o_ref):
        if lhs_is_transposed:
            o_ref[...] = _dot(l_ref[...], r_ref[...]).astype(BF)
        else:
            o_ref[...] = _dot_tn(l_ref[...], r_ref[...]).astype(BF)

    lhs_spec = pl.BlockSpec((c, t), lambda j: (j, 0)) if lhs_is_transposed else pl.BlockSpec((t, c), lambda j: (0, j))
    return _call(
        body, deps, (lhs, rhs), grid=(n // c,),
        in_specs=[lhs_spec, _resident((t, D))],
        out_specs=pl.BlockSpec((c, D), lambda j: (j, 0)),
        out_shape=jax.ShapeDtypeStruct((n, D), BF),
        compiler_params=_params(1), name=name)


def _wgrad_mix(duc, dq_t, dkv_t, dgp, hm):
    t = hm.shape[0]
    c = 512
    first_q, first_kv, first_gate = R_Q[0] // c, R_KV[0] // c, R_GATE[0] // c

    def body(uc_ref, q_ref, kv_ref, gp_ref, h_ref, o_ref):
        j = pl.program_id(0)

        @pl.when(j < first_q)
        def _():
            o_ref[...] = _dot_tn(uc_ref[...], h_ref[...]).astype(BF)

        @pl.when((j >= first_q) & (j < first_kv))
        def _():
            o_ref[...] = _dot(q_ref[...], h_ref[...]).astype(BF)

        @pl.when((j >= first_kv) & (j < first_gate))
        def _():
            o_ref[...] = _dot(kv_ref[...], h_ref[...]).astype(BF)

        @pl.when(j >= first_gate)
        def _():
            o_ref[...] = _dot_tn(gp_ref[...], h_ref[...]).astype(BF)

    return pl.pallas_call(
        body, grid=(INW // c,),
        in_specs=[pl.BlockSpec((t, c), lambda j: (0, jnp.clip(j, 0, first_q - 1))),
                  pl.BlockSpec((c, t), lambda j: (jnp.clip(j - first_q, 0, first_kv - first_q - 1), 0)),
                  pl.BlockSpec((c, t), lambda j: (jnp.clip(j - first_kv, 0, first_gate - first_kv - 1), 0)),
                  pl.BlockSpec((t, c), lambda j: (0, jnp.clip(j - first_gate, 0, INW // c - first_gate - 1))),
                  _resident((t, D))],
        out_specs=pl.BlockSpec((c, D), lambda j: (j, 0)),
        out_shape=jax.ShapeDtypeStruct((INW, D), BF),
        compiler_params=_params(1), name="mix_dw_in")(duc, dq_t, dkv_t, dgp, hm)


def _mix_proj(x, g, w_t):
    t = x.shape[0]
    tm = min(ROW_TILE_WIDE, t)

    def body(x_ref, g_ref, w_ref, hm_ref, uc_ref, gp_ref, qkv_ref):
        r, xh = _rms_stats(x_ref[...])
        hm = (xh * g_ref[...]).astype(BF)
        hm_ref[...] = hm
        uc_ref[...] = _dot_nt(hm, w_ref[R_CONV[0]:R_CONV[1], :]).astype(BF)
        gp_ref[...] = _dot_nt(hm, w_ref[R_GATE[0]:R_GATE[1], :]).astype(BF)
        qkv_ref[...] = _dot_nt(w_ref[R_QKV[0]:R_QKV[1], :], hm).astype(BF)

    return pl.pallas_call(
        body, grid=(t // tm,),
        in_specs=[_row_tile(tm, D), _resident((1, D)), _resident((INW, D))],
        out_specs=[_row_tile(tm, D), _row_tile(tm, 2 * D), _row_tile(tm, 2 * D), pl.BlockSpec((1536, tm), lambda i: (0, i))],
        out_shape=[jax.ShapeDtypeStruct((t, D), BF), jax.ShapeDtypeStruct((t, 2 * D), BF),
                   jax.ShapeDtypeStruct((t, 2 * D), BF), jax.ShapeDtypeStruct((1536, t), BF)],
        compiler_params=_params(1), name="mix_proj")(x, g, w_t)


CONV_HALO = 32
CONV_LEAD = CONV_HALO - (CW - 1)


def _glu(uc):
    uc = uc.astype(F32)
    return uc[:, :D] * _sig(uc[:, D:])


def _ln_stats(zc):
    mu = jnp.mean(zc, axis=-1, keepdims=True)
    zm = zc - mu
    r = lax.rsqrt(jnp.mean(zm * zm, axis=-1, keepdims=True) + EPS)
    return r, zm * r


CONV_SHIFTS = 8
CONV_CHUNK = 32


def _store_shifted(buf, rows):
    for b in range(1, CONV_SHIFTS):
        buf[b, 0:rows - 8, :] = buf[0, pl.ds(b, rows - 8), :]


def _conv_fwd(uc, dwk, dwb, lng, lnb):
    t = uc.shape[0]
    tm = min(512, t)
    per = tm // CONV_HALO
    ext = tm + CONV_HALO

    def body(cur_ref, prev_ref, k_ref, kb_ref, g_ref, b_ref, o_ref, zc_ref, zsh):
        i = pl.program_id(0)
        zsh[0, 0:CONV_HALO, :] = _glu(prev_ref[...]) * (i > 0).astype(F32)
        zsh[0, CONV_HALO:, :] = _glu(cur_ref[...])
        _store_shifted(zsh, ext)

        def chunk(ci, carry):
            r0 = pl.multiple_of(ci * CONV_CHUNK, CONV_CHUNK)
            acc = jnp.zeros((CONV_CHUNK, D), F32) + kb_ref[...]
            for w in range(CW):
                a, b = divmod(CONV_LEAD + w, 8)
                acc = acc + k_ref[w:w + 1, :] * zsh[b, pl.ds(r0 + 8 * a, CONV_CHUNK), :]
            zc_ref[pl.ds(r0, CONV_CHUNK), :] = acc
            r, xh = _ln_stats(acc)
            y = xh * g_ref[...] + b_ref[...]
            o_ref[pl.ds(r0, CONV_CHUNK), :] = (y * _sig(y)).astype(BF)
            return carry

        lax.fori_loop(0, tm // CONV_CHUNK, chunk, 0)

    return pl.pallas_call(
        body, grid=(t // tm,),
        in_specs=[_row_tile(tm, 2 * D),
                  pl.BlockSpec((CONV_HALO, 2 * D), lambda i: (jnp.maximum(i * per - 1, 0), 0)),
                  _resident((CWP, D)), _resident((1, D)), _resident((1, D)), _resident((1, D))],
        out_specs=[_row_tile(tm, D), _row_tile(tm, D)],
        out_shape=[jax.ShapeDtypeStruct((t, D), BF), jax.ShapeDtypeStruct((t, D), F32)],
        scratch_shapes=[pltpu.VMEM((CONV_SHIFTS, ext, D), F32)],
        compiler_params=_params(1), name="conv_fwd")(uc, uc, dwk, dwb, lng, lnb)


def _conv_bwd(uc, zc, dzs, dwk, lng, lnb):
    t = uc.shape[0]
    tm = min(ROW_TILE, t)
    per = tm // CONV_HALO
    n_tiles = t // tm
    ext = tm + CONV_HALO
    last_block = t // CONV_HALO - 1

    def body(cur_ref, prev_ref, zc_ref, zcn_ref, dz_ref, dzn_ref, k_ref, g_ref, b_ref,
             duc_ref, dk_ref, dkb_ref, dg_ref, db_ref, zsh, dsh, dk8):
        i = pl.program_id(0)

        @pl.when(i == 0)
        def _():
            dk8[...] = jnp.zeros_like(dk8)
            dkb_ref[...] = jnp.zeros_like(dkb_ref)
            dg_ref[...] = jnp.zeros_like(dg_ref)
            db_ref[...] = jnp.zeros_like(db_ref)

        has_next = (i < n_tiles - 1).astype(F32)
        zsh[0, 0:CONV_HALO, :] = _glu(prev_ref[...]) * (i > 0).astype(F32)
        zsh[0, CONV_HALO:, :] = _glu(cur_ref[...])
        _store_shifted(zsh, ext)
        gain = g_ref[...]

        def ln_silu_bwd(zc, dzs, live):
            r, xh = _ln_stats(zc)
            y = xh * gain + b_ref[...]
            sy = _sig(y)
            dy = dzs * (sy * (1.0 + y * (1.0 - sy))) * live
            dxh = dy * gain
            dzc = r * (dxh - jnp.mean(dxh, axis=-1, keepdims=True) - xh * jnp.mean(dxh * xh, axis=-1, keepdims=True))
            return dzc, dy, xh

        dzc, dy, xh = ln_silu_bwd(zc_ref[...], dz_ref[...], 1.0)
        dsh[0, 0:tm, :] = dzc
        dg_ref[...] += jnp.sum(dy * xh, axis=0, keepdims=True)
        db_ref[...] += jnp.sum(dy, axis=0, keepdims=True)
        dkb_ref[...] += jnp.sum(dzc, axis=0, keepdims=True)
        dsh[0, tm:, :] = ln_silu_bwd(zcn_ref[...], dzn_ref[...], has_next)[0]
        _store_shifted(dsh, ext)

        def chunk(ci, carry):
            r0 = pl.multiple_of(ci * CONV_CHUNK, CONV_CHUNK)
            dzc_c = dsh[0, pl.ds(r0, CONV_CHUNK), :]
            dz = jnp.zeros((CONV_CHUNK, D), F32)
            for w in range(CW):
                a, b = divmod(CW - 1 - w, 8)
                dz = dz + k_ref[w:w + 1, :] * dsh[b, pl.ds(r0 + 8 * a, CONV_CHUNK), :]
                a, b = divmod(CONV_LEAD + w, 8)
                prod = dzc_c * zsh[b, pl.ds(r0 + 8 * a, CONV_CHUNK), :]
                part = prod[0:8, :]
                for j in range(1, CONV_CHUNK // 8):
                    part = part + prod[8 * j:8 * j + 8, :]
                dk8[w] += part
            ucc = cur_ref[pl.ds(r0, CONV_CHUNK), :].astype(F32)
            sg = _sig(ucc[:, D:])
            duc_ref[pl.ds(r0, CONV_CHUNK), 0:D] = (dz * sg).astype(BF)
            duc_ref[pl.ds(r0, CONV_CHUNK), D:2 * D] = (dz * ucc[:, :D] * sg * (1.0 - sg)).astype(BF)
            return carry

        lax.fori_loop(0, tm // CONV_CHUNK, chunk, 0)

        @pl.when(i == n_tiles - 1)
        def _():
            dk_ref[...] = jnp.sum(dk8[...], axis=1)

    vec = pl.BlockSpec((1, D), lambda i: (0, 0))
    next_halo = pl.BlockSpec((CONV_HALO, D), lambda i: (jnp.minimum((i + 1) * per, last_block), 0))
    return pl.pallas_call(
        body, grid=(n_tiles,),
        in_specs=[_row_tile(tm, 2 * D),
                  pl.BlockSpec((CONV_HALO, 2 * D), lambda i: (jnp.maximum(i * per - 1, 0), 0)),
                  _row_tile(tm, D), next_halo, _row_tile(tm, D), next_halo,
                  _resident((CWP, D)), _resident((1, D)), _resident((1, D))],
        out_specs=[_row_tile(tm, 2 * D), pl.BlockSpec((CWP, D), lambda i: (0, 0)), vec, vec, vec],
        out_shape=[jax.ShapeDtypeStruct((t, 2 * D), BF), jax.ShapeDtypeStruct((CWP, D), F32),
                   jax.ShapeDtypeStruct((1, D), F32), jax.ShapeDtypeStruct((1, D), F32), jax.ShapeDtypeStruct((1, D), F32)],
        scratch_shapes=[pltpu.VMEM((CONV_SHIFTS, ext, D), F32), pltpu.VMEM((CONV_SHIFTS, ext, D), F32),
                        pltpu.VMEM((CWP, 8, D), F32)],
        compiler_params=_params(1), name="conv_bwd")(uc, uc, zc, zc, dzs, dzs, dwk, lng, lnb)


def _norm_rows(xt, g):
    r = lax.rsqrt(jnp.mean(xt * xt, axis=0, keepdims=True) + EPS)
    xh = xt * r
    return xh * g, r, xh


ATT_TQ = 512


def _attn_specs(t, tq):
    per = tq // BLK
    return [pl.BlockSpec((1536, tq), lambda i: (0, i)),
            pl.BlockSpec((512, BLK), lambda i: (2, jnp.maximum(i * per - 1, 0))),
            _resident((HD, 1)), _resident((HD, 1)), _resident((NKV, 1, GRP * BLK)),
            _resident((2, NKV, 2 * BLK, GRP * BLK))]


def _attn_window(hk, sb, qkv_ref, halo_ref, kn_cur, kn_halo):
    v0 = D + NKV * HD + hk * HD
    if sb == 0:
        k_prev = kn_halo[hk]
        v_prev = halo_ref[NKV * HD + hk * HD:NKV * HD + (hk + 1) * HD, :]
    else:
        k_prev = kn_cur[hk][:, (sb - 1) * BLK:sb * BLK]
        v_prev = qkv_ref[v0:v0 + HD, (sb - 1) * BLK:sb * BLK]
    kw = jnp.concatenate([k_prev, kn_cur[hk][:, sb * BLK:(sb + 1) * BLK]], axis=1).astype(BF)
    vw = jnp.concatenate([v_prev, qkv_ref[v0:v0 + HD, sb * BLK:(sb + 1) * BLK]], axis=1)
    return kw, vw


def _attn_probs(kw, qc, bias, sink):
    st = _dot_tn(kw, qc) + bias
    m = jnp.maximum(jnp.max(st, axis=0, keepdims=True), sink)
    p = jnp.exp(st - m)
    e_sink = jnp.exp(sink - m)
    inv = 1.0 / (jnp.sum(p, axis=0, keepdims=True) + e_sink)
    return p * inv, e_sink * inv


def _attn_fwd(qkv_t, qg, kg, sink_rows, bias_t):
    t = qkv_t.shape[1]
    tq = min(ATT_TQ, t)
    n_sub = tq // BLK

    def body(qkv_ref, halo_ref, qg_ref, kg_ref, sink_ref, bias_ref, o_ref):
        i = pl.program_id(0)
        first = (i == 0).astype(jnp.int32)
        kgain = kg_ref[...]
        qgain = qg_ref[...]
        kn_cur = [_norm_rows(qkv_ref[D + h * HD:D + (h + 1) * HD, :].astype(F32), kgain)[0] for h in range(NKV)]
        kn_halo = [_norm_rows(halo_ref[h * HD:(h + 1) * HD, :].astype(F32), kgain)[0] for h in range(NKV)]
        for hk in range(NKV):
            for sb in range(n_sub):
                cols = slice(sb * BLK, (sb + 1) * BLK)
                kw, vw = _attn_window(hk, sb, qkv_ref, halo_ref, kn_cur, kn_halo)
                qc = jnp.concatenate(
                    [_norm_rows(qkv_ref[(GRP * hk + g) * HD:(GRP * hk + g + 1) * HD, cols].astype(F32), qgain)[0] * QK_SCALE
                     for g in range(GRP)], axis=1).astype(BF)
                bias = bias_ref[first, hk] if sb == 0 else bias_ref[0, hk]
                p, _ = _attn_probs(kw, qc, bias, sink_ref[hk])
                o = _dot(vw, p.astype(BF))
                for g in range(GRP):
                    head = GRP * hk + g
                    o_ref[head * HD:(head + 1) * HD, cols] = o[:, g * BLK:(g + 1) * BLK].astype(BF)

    return pl.pallas_call(
        body, grid=(t // tq,),
        in_specs=_attn_specs(t, tq),
        out_specs=pl.BlockSpec((D, tq), lambda i: (0, i)),
        out_shape=jax.ShapeDtypeStruct((D, t), BF),
        compiler_params=_params(1), name="attn_fwd")(qkv_t, qkv_t, qg, kg, sink_rows, bias_t)


def _attn_bwd(qkv_t, do_t, qg, kg, sink_rows, bias_t, deps=()):
    t = qkv_t.shape[1]
    tq = min(ATT_TQ, t)
    n_sub = tq // BLK
    n_tiles = t // tq

    def body(qkv_ref, halo_ref, do_ref, qg_ref, kg_ref, sink_ref, bias_ref,
             dq_ref, ckv_ref, dqg_ref, dsink_ref, dsacc_ref, qg_scr):
        i = pl.program_id(0)

        @pl.when(i == 0)
        def _():
            qg_scr[...] = jnp.zeros_like(qg_scr)
            dsink_ref[...] = jnp.zeros_like(dsink_ref)
            dsacc_ref[...] = jnp.zeros_like(dsacc_ref)

        first = (i == 0).astype(jnp.int32)
        kgain = kg_ref[...]
        qgain = qg_ref[...]
        kn_cur = [_norm_rows(qkv_ref[D + h * HD:D + (h + 1) * HD, :].astype(F32), kgain)[0] for h in range(NKV)]
        kn_halo = [_norm_rows(halo_ref[h * HD:(h + 1) * HD, :].astype(F32), kgain)[0] for h in range(NKV)]
        dqg = jnp.zeros((HD, BLK), F32)
        for hk in range(NKV):
            for sb in range(n_sub):
                cols = slice(sb * BLK, (sb + 1) * BLK)
                kw, vw = _attn_window(hk, sb, qkv_ref, halo_ref, kn_cur, kn_halo)
                qn, qr, qh = [], [], []
                for g in range(GRP):
                    head = GRP * hk + g
                    n_, r_, h_ = _norm_rows(qkv_ref[head * HD:(head + 1) * HD, cols].astype(F32), qgain)
                    qn.append(n_)
                    qr.append(r_)
                    qh.append(h_)
                qc = (jnp.concatenate(qn, axis=1) * QK_SCALE).astype(BF)
                bias = bias_ref[first, hk] if sb == 0 else bias_ref[0, hk]
                p, p_sink = _attn_probs(kw, qc, bias, sink_ref[hk])
                doc = jnp.concatenate([do_ref[(GRP * hk + g) * HD:(GRP * hk + g + 1) * HD, cols] for g in range(GRP)], axis=1)
                dp = _dot_tn(vw, doc)
                delta = jnp.sum(p * dp, axis=0, keepdims=True)
                ds = p * (dp - delta)
                dsink_ref[hk] += -(p_sink * delta)
                dsacc_ref[hk] += ds
                dsb = ds.astype(BF)
                dqc = _dot(kw, dsb) * QK_SCALE
                ckv_ref[sb, hk * HD:(hk + 1) * HD, :] = _dot_nt(qc, dsb)
                ckv_ref[sb, NKV * HD + hk * HD:NKV * HD + (hk + 1) * HD, :] = _dot_nt(doc, p.astype(BF))
                for g in range(GRP):
                    head = GRP * hk + g
                    dqn = dqc[:, g * BLK:(g + 1) * BLK]
                    dqh = dqn * qgain
                    dq = qr[g] * (dqh - qh[g] * jnp.mean(dqh * qh[g], axis=0, keepdims=True))
                    dq_ref[head * HD:(head + 1) * HD, cols] = dq.astype(BF)
                    dqg = dqg + dqn * qh[g]
        qg_scr[...] += dqg

        @pl.when(i == n_tiles - 1)
        def _():
            dqg_ref[...] = jnp.sum(qg_scr[...], axis=1, keepdims=True)

    return _call(
        body, deps, (qkv_t, qkv_t, do_t, qg, kg, sink_rows, bias_t), grid=(n_tiles,),
        in_specs=_attn_specs(t, tq)[:2] + [pl.BlockSpec((D, tq), lambda i: (0, i))] + _attn_specs(t, tq)[2:],
        out_specs=[pl.BlockSpec((D, tq), lambda i: (0, i)),
                   pl.BlockSpec((n_sub, 2 * NKV * HD, 2 * BLK), lambda i: (i, 0, 0)),
                   pl.BlockSpec((HD, 1), lambda i: (0, 0)),
                   pl.BlockSpec((NKV, 1, GRP * BLK), lambda i: (0, 0, 0)),
                   pl.BlockSpec((NKV, 2 * BLK, GRP * BLK), lambda i: (0, 0, 0))],
        out_shape=[jax.ShapeDtypeStruct((D, t), BF),
                   jax.ShapeDtypeStruct((t // BLK, 2 * NKV * HD, 2 * BLK), F32),
                   jax.ShapeDtypeStruct((HD, 1), F32),
                   jax.ShapeDtypeStruct((NKV, 1, GRP * BLK), F32),
                   jax.ShapeDtypeStruct((NKV, 2 * BLK, GRP * BLK), F32)],
        scratch_shapes=[pltpu.VMEM((HD, BLK), F32)],
        compiler_params=_params(1), name="attn_bwd")


def _kv_combine(ckv, qkv_t, kg):
    nb = ckv.shape[0]
    t = nb * BLK
    rows = NKV * HD
    per = min(4, nb)
    steps = nb // per

    def body(c_ref, cn_ref, k_ref, kg_ref, o_ref, dkg_ref, kg_scr):
        n = pl.program_id(0)

        @pl.when(n == 0)
        def _():
            kg_scr[...] = jnp.zeros_like(kg_scr)

        has_next = (n < steps - 1).astype(F32)
        kgain = kg_ref[...]
        dkg = jnp.zeros((HD, BLK), F32)
        for s in range(per):
            cols = slice(s * BLK, (s + 1) * BLK)
            after = c_ref[s + 1, :, :BLK] if s + 1 < per else cn_ref[0, :, :BLK] * has_next
            d = c_ref[s, :, BLK:] + after
            o_ref[rows:, cols] = d[rows:, :].astype(BF)
            for h in range(NKV):
                _, r, kh = _norm_rows(k_ref[h * HD:(h + 1) * HD, cols].astype(F32), kgain)
                dkn = d[h * HD:(h + 1) * HD, :]
                dkh = dkn * kgain
                o_ref[h * HD:(h + 1) * HD, cols] = (r * (dkh - kh * jnp.mean(dkh * kh, axis=0, keepdims=True))).astype(BF)
                dkg = dkg + dkn * kh
        kg_scr[...] += dkg

        @pl.when(n == steps - 1)
        def _():
            dkg_ref[...] = jnp.sum(kg_scr[...], axis=1, keepdims=True)

    return pl.pallas_call(
        body, grid=(steps,),
        in_specs=[pl.BlockSpec((per, 2 * rows, 2 * BLK), lambda n: (n, 0, 0)),
                  pl.BlockSpec((1, 2 * rows, 2 * BLK), lambda n: (jnp.minimum((n + 1) * per, nb - 1), 0, 0)),
                  pl.BlockSpec((rows, per * BLK), lambda n: (D // rows, n)),
                  _resident((HD, 1))],
        out_specs=[pl.BlockSpec((2 * rows, per * BLK), lambda n: (0, n)), pl.BlockSpec((HD, 1), lambda n: (0, 0))],
        out_shape=[jax.ShapeDtypeStruct((2 * rows, t), BF), jax.ShapeDtypeStruct((HD, 1), F32)],
        scratch_shapes=[pltpu.VMEM((HD, BLK), F32)],
        compiler_params=_params(1), name="kv_combine")(ckv, ckv, qkv_t, kg)


def _group_lane_sums(v):
    lane_group = lax.broadcasted_iota(jnp.int32, (1, GRP * BLK), 1) // BLK
    col = lax.broadcasted_iota(jnp.int32, (1, BLK), 1)
    out = jnp.zeros((NKV, BLK), F32)
    for g in range(GRP):
        s = jnp.sum(jnp.where(lane_group == g, v, 0.0), axis=1, keepdims=True)
        out = jnp.where(col == g, s, out)
    return out


def _bias_grad(dsacc, onehot_t):
    per = 8

    def body(ds_ref, oh_ref, o_ref):
        for b in range(per):
            oh = jnp.concatenate([oh_ref[b]] * GRP, axis=1)
            o_ref[b] = _group_lane_sums(jnp.sum(ds_ref[...] * oh[None], axis=1))

    return pl.pallas_call(
        body, grid=(NBUCKET // per,),
        in_specs=[_resident((NKV, 2 * BLK, GRP * BLK)), pl.BlockSpec((per, 2 * BLK, BLK), lambda b: (b, 0, 0))],
        out_specs=pl.BlockSpec((per, NKV, BLK), lambda b: (b, 0, 0)),
        out_shape=jax.ShapeDtypeStruct((NBUCKET, NKV, BLK), F32),
        compiler_params=_params(1), name="bias_grad")(dsacc, onehot_t)


def _sink_grad(dsink_rows):
    def body(d_ref, o_ref):
        o_ref[...] = _group_lane_sums(d_ref[:, 0, :])

    return pl.pallas_call(body, out_shape=jax.ShapeDtypeStruct((NKV, BLK), F32), name="sink_grad")(dsink_rows)


def _mix_out(zs, o_t, gp, x, w_cp, w_o, w_out):
    t = x.shape[0]
    tm = min(ROW_TILE_WIDE, t)

    def body(zs_ref, ot_ref, gp_ref, x_ref, wcp_ref, wo_ref, wout_ref, xo_ref, a_ref, b_ref, m_ref):
        a = _dot(zs_ref[...], wcp_ref[...])
        b = _dot_tn(ot_ref[...], wo_ref[...])
        a_ref[...] = a.astype(BF)
        b_ref[...] = b.astype(BF)
        merged = (_sig(gp_ref[:, :D].astype(F32)) * a + _sig(gp_ref[:, D:].astype(F32)) * b).astype(BF)
        m_ref[...] = merged
        xo_ref[...] = x_ref[...] + _dot(merged, wout_ref[...])

    return pl.pallas_call(
        body, grid=(t // tm,),
        in_specs=[_row_tile(tm, D), pl.BlockSpec((D, tm), lambda i: (0, i)), _row_tile(tm, 2 * D), _row_tile(tm, D),
                  _resident((D, D)), _resident((D, D)), _resident((D, D))],
        out_specs=[_row_tile(tm, D)] * 4,
        out_shape=[jax.ShapeDtypeStruct((t, D), F32)] + [jax.ShapeDtypeStruct((t, D), BF)] * 3,
        compiler_params=_params(1), name="mix_out")(zs, o_t, gp, x, w_cp, w_o, w_out)


def _mix_out_bwd(dx, a, b, gp, w_cp, w_o, w_out, deps=()):
    t = dx.shape[0]
    tm = min(ROW_TILE_WIDE, t)

    def body(dx_ref, a_ref, b_ref, gp_ref, wcp_ref, wo_ref, wout_ref, dzs_ref, dot_ref, dgp_ref, da_ref, db_ref, dxb_ref):
        dxb = dx_ref[...].astype(BF)
        dxb_ref[...] = dxb
        dm = _dot_nt(dxb, wout_ref[...])
        gc = _sig(gp_ref[:, :D].astype(F32))
        ga = _sig(gp_ref[:, D:].astype(F32))
        da = (dm * gc).astype(BF)
        db = (dm * ga).astype(BF)
        da_ref[...] = da
        db_ref[...] = db
        dgp_ref[:, :D] = (dm * a_ref[...].astype(F32) * gc * (1.0 - gc)).astype(BF)
        dgp_ref[:, D:] = (dm * b_ref[...].astype(F32) * ga * (1.0 - ga)).astype(BF)
        dzs_ref[...] = _dot_nt(da, wcp_ref[...])
        dot_ref[...] = _dot_nt(wo_ref[...], db).astype(BF)

    return _call(
        body, deps, (dx, a, b, gp, w_cp, w_o, w_out), grid=(t // tm,),
        in_specs=[_row_tile(tm, D), _row_tile(tm, D), _row_tile(tm, D), _row_tile(tm, 2 * D),
                  _resident((D, D)), _resident((D, D)), _resident((D, D))],
        out_specs=[_row_tile(tm, D), pl.BlockSpec((D, tm), lambda i: (0, i)), _row_tile(tm, 2 * D),
                   _row_tile(tm, D), _row_tile(tm, D), _row_tile(tm, D)],
        out_shape=[jax.ShapeDtypeStruct((t, D), F32), jax.ShapeDtypeStruct((D, t), BF), jax.ShapeDtypeStruct((t, 2 * D), BF),
                   jax.ShapeDtypeStruct((t, D), BF), jax.ShapeDtypeStruct((t, D), BF), jax.ShapeDtypeStruct((t, D), BF)],
        compiler_params=_params(1), name="mix_out_bwd")


def _mix_proj_bwd(dxo, duc, dq_t, dkv_t, dgp, x, g, w_t):
    t = x.shape[0]
    tm = min(ROW_TILE_WIDE, t)

    def body(dxo_ref, duc_ref, dq_ref, dkv_ref, dgp_ref, x_ref, g_ref, w_ref, dx_ref, dg_ref):
        dn = _dot(duc_ref[...], w_ref[R_CONV[0]:R_CONV[1], :])
        dn = dn + _dot(dgp_ref[...], w_ref[R_GATE[0]:R_GATE[1], :])
        dn = dn + _dot_tn(dq_ref[...], w_ref[R_Q[0]:R_Q[1], :])
        dn = dn + _dot_tn(dkv_ref[...], w_ref[R_KV[0]:R_KV[1], :])
        dx, dg = _rms_bwd(dn, x_ref[...], g_ref[...])
        dx_ref[...] = dxo_ref[...] + dx

        @pl.when(pl.program_id(0) == 0)
        def _():
            dg_ref[...] = jnp.zeros_like(dg_ref)

        dg_ref[...] += dg

    return pl.pallas_call(
        body, grid=(t // tm,),
        in_specs=[_row_tile(tm, D), _row_tile(tm, 2 * D), pl.BlockSpec((D, tm), lambda i: (0, i)),
                  pl.BlockSpec((2 * NKV * HD, tm), lambda i: (0, i)), _row_tile(tm, 2 * D), _row_tile(tm, D),
                  _resident((1, D)), _resident((INW, D))],
        out_specs=[_row_tile(tm, D), pl.BlockSpec((1, D), lambda i: (0, 0))],
        out_shape=[jax.ShapeDtypeStruct((t, D), F32), jax.ShapeDtypeStruct((1, D), F32)],
        compiler_params=_params(1), name="mix_proj_bwd")(dxo, duc, dq_t, dkv_t, dgp, x, g, w_t)


def _attention_tables():
    kj = np.arange(2 * BLK)[:, None]
    qi = np.arange(BLK)[None, :]
    dist = qi + BLK - kj
    in_win = (dist >= 0) & (dist < BLK)
    dpos = np.maximum(dist, 0)
    max_exact = NBUCKET // 2
    dfl = np.maximum(dpos, 1).astype(np.float32)
    large = max_exact + (np.log(dfl / np.float32(max_exact)) / np.float32(math.log(BLK / max_exact))
                         * np.float32(NBUCKET - max_exact)).astype(np.int32)
    large = np.minimum(large, NBUCKET - 1)
    bucket = np.where(dpos < max_exact, dpos, large)
    onehot = (bucket[None] == np.arange(NBUCKET)[:, None, None]).astype(np.float32)
    mask = in_win.astype(np.float32)
    mask_first = mask * (kj >= BLK)
    masks = np.stack([np.tile(mask, (1, GRP)), np.tile(mask_first, (1, GRP))])
    return onehot, masks


def _bias_table(rel_bias, onehot):
    tab = jnp.einsum("bkq,bh->hkq", onehot, rel_bias, precision=lax.Precision.HIGHEST)
    tab = tab.reshape(NKV, GRP, 2 * BLK, BLK)
    return jnp.transpose(tab, (0, 2, 1, 3)).reshape(NKV, 2 * BLK, GRP * BLK)


def _local_step(x, target, vec, weights_of, wgrad, grads_done, small_done):
    onehot_np, masks_np = _attention_tables()
    onehot = jnp.asarray(onehot_np)
    masks = jnp.asarray(masks_np)
    bias_t = jnp.where(masks[:, None] > 0.5, _bias_table(vec["rel_bias"], onehot)[None], NEG)
    sink_rows = jnp.repeat(vec["attn_sinks"].reshape(NKV, 1, GRP), BLK, axis=2)
    qg = vec["q_norm"].reshape(HD, 1)
    kg = vec["k_norm"].reshape(HD, 1)
    g1 = vec["ffn1_norm"].reshape(1, D)
    gm = vec["mix_norm"].reshape(1, D)
    g2 = vec["ffn2_norm"].reshape(1, D)
    dwb = vec["conv_dw_bias"].reshape(1, D)
    lng = vec["conv_ln_g"].reshape(1, D)
    lnb = vec["conv_ln_b"].reshape(1, D)

    w1 = weights_of("ffn1", (bias_t, sink_rows))
    n1, u1, x1 = _ffn_fwd(x, g1, w1["ffn1_w_in"], w1["ffn1_w_out"], "ffn1_fwd")
    wm = weights_of("mix", (x1,))
    dwk = jnp.pad(wm["conv_dw_kernel"], ((0, CWP - CW), (0, 0)))
    hm, uc, gp, qkv_t = _mix_proj(x1, gm, wm["w_in"])
    zs, zc = _conv_fwd(uc, dwk, dwb, lng, lnb)
    o_t = _attn_fwd(qkv_t, qg, kg, sink_rows, bias_t)
    x2, a, b, merged = _mix_out(zs, o_t, gp, x1, wm["conv_w_proj"], wm["attn_w_o"], wm["w_out"])
    w2 = weights_of("ffn2", (x2,))
    n2, u2, dx3, sq = _ffn_fwd(x2, g2, w2["ffn2_w_in"], w2["ffn2_w_out"], "ffn2_fwd", target=target)

    gv = {}
    dx2, du2, h2, dy2, gv["ffn2_norm"] = _ffn_bwd(dx3, x2, g2, u2, w2["ffn2_w_in"], w2["ffn2_w_out"], "ffn2_bwd")
    deps = grads_done("ffn2", {"ffn2_w_in": wgrad(du2, n2, "ffn2_dw_in", False),
                               "ffn2_w_out": wgrad(h2, dy2, "ffn2_dw_out", False)})

    dzs, do_t, dgp, da, db, dx2b = _mix_out_bwd(dx2, a, b, gp, wm["conv_w_proj"], wm["attn_w_o"], wm["w_out"], deps=deps)
    deps = grads_done("mix_out", {"w_out": wgrad(merged, dx2b, "mix_dw_out", False),
                                  "conv_w_proj": wgrad(zs, da, "mix_dw_cp", False),
                                  "attn_w_o": wgrad(o_t, db, "mix_dw_o", True)})

    dq_t, ckv, dqg, dsink_rows, dsacc = _attn_bwd(qkv_t, do_t, qg, kg, sink_rows, bias_t, deps=deps)
    dkv_t, dkg = _kv_combine(ckv, qkv_t, kg)
    gv["q_norm"] = dqg.reshape(HD)
    gv["k_norm"] = dkg.reshape(HD)
    gv["attn_sinks"] = _sink_grad(dsink_rows)[:, :GRP].reshape(NQ)
    gv["rel_bias"] = _bias_grad(dsacc, onehot)[:, :, :GRP].reshape(NBUCKET, NQ)

    duc, dk_conv, gv["conv_dw_bias"], gv["conv_ln_g"], gv["conv_ln_b"] = _conv_bwd(uc, zc, dzs, dwk, lng, lnb)
    gv["conv_dw_kernel"] = dk_conv[:CW]

    dx1, gv["mix_norm"] = _mix_proj_bwd(dx2, duc, dq_t, dkv_t, dgp, x1, gm, wm["w_in"])
    deps = grads_done("mix_in", {"w_in": _wgrad_mix(duc, dq_t, dkv_t, dgp, hm)})

    dx0, du1, h1, dy1, gv["ffn1_norm"] = _ffn_bwd(dx1, x, g1, u1, w1["ffn1_w_in"], w1["ffn1_w_out"], "ffn1_bwd", deps=deps)
    for k in ("ffn1_norm", "mix_norm", "ffn2_norm", "conv_dw_bias", "conv_ln_g", "conv_ln_b"):
        gv[k] = gv[k].reshape(D)
    deps = small_done(gv, sq)
    deps = grads_done("ffn1_out", {"ffn1_w_out": wgrad(h1, dy1, "ffn1_dw_out", False, deps)})
    grads_done("ffn1_in", {"ffn1_w_in": wgrad(du1, n1, "ffn1_dw_in", False, deps)})
    return dx0


MESH_ID = pl.DeviceIdType.MESH


def _position():
    return lax.axis_index("x"), lax.axis_index("y"), lax.axis_index("c")


def _shard_rows(ref, index, rows):
    return ref.at[pl.ds(pl.multiple_of(index * rows, 16), rows), :]


def _prep(weights, taps, me):
    n = len(weights)

    def body(me_ref, *refs):
        for k in range(n):
            refs[n + 1 + k][...] = refs[k][...].astype(BF)
        refs[2 * n + 1][0:CW, :] = refs[n][...]
        refs[2 * n + 1][CW:, :] = jnp.zeros((CWP - CW, BLK), F32)

    shard_shapes = [w.shape for w in weights] + [(CWP, BLK)]
    dtypes = [BF] * n + [F32]
    ins = list(weights) + [taps]
    return pl.pallas_call(
        body,
        grid_spec=pltpu.PrefetchScalarGridSpec(
            num_scalar_prefetch=1, grid=(1,),
            in_specs=[pl.BlockSpec(a.shape, lambda i, m: (0, 0), pipeline_mode=pl.Buffered(1)) for a in ins],
            out_specs=[pl.BlockSpec(s, lambda i, m: (m[0], 0)) for s in shard_shapes]),
        out_shape=[jax.ShapeDtypeStruct((N_DEV * s[0], s[1]), d) for s, d in zip(shard_shapes, dtypes)],
        compiler_params=_params(1), name="prep")(me, *ins)


HBM = pl.BlockSpec(memory_space=pltpu.HBM)
SEM = pl.BlockSpec(memory_space=pltpu.SEMAPHORE)
DATAFLOW = pltpu.SideEffectType.DATAFLOW_SIDE_EFFECTING
TOKEN = jax.ShapeDtypeStruct((8, 128), F32)


def _in_hbm(x):
    return pltpu.with_memory_space_constraint(x, pltpu.HBM)


def _hbm_like(arrays):
    return [pltpu.HBM(a.shape, a.dtype) for a in arrays]


def _other_chips(x, y):
    return [(1 - x, y), (x, 1 - y), (1 - x, 1 - y)]


def _device_index(chip, c):
    return 4 * chip[0] + 2 * chip[1] + c


def _chip_index(chip):
    return 2 * chip[0] + chip[1]


class _Exchange:
    def __init__(self, gather):
        self.gather = gather

    def sent(self, x, y, c, chip):
        return _device_index((x, y), c) if self.gather else _chip_index(chip)

    def lands_at(self, x, y, c):
        return _device_index((x, y), c) if self.gather else _chip_index((x, y))

    def arrives_at(self, chip, c):
        return _device_index(chip, c) if self.gather else _chip_index(chip)


def _ici_copies_start(sets, sources, landings, exchange, name, deps=()):
    n = len(landings)
    arrays = (list(sources) if sources is not None else []) + list(landings)
    first_land = len(arrays) - n
    n_sets = len(sets)
    n_deps = len(deps)

    def body(*refs):
        refs = refs[n_deps:]
        src, land = refs[:n], refs[first_land:first_land + n]
        sems = refs[len(arrays):len(arrays) + 2 * n_sets]
        token = refs[-1]
        x, y, c = _position()
        for s, members in enumerate(sets):
            for slot, (k, rows) in enumerate(members):
                for j, chip in enumerate(_other_chips(x, y)):
                    pltpu.make_async_remote_copy(
                        src_ref=_shard_rows(src[k], exchange.sent(x, y, c, chip), rows),
                        dst_ref=_shard_rows(land[k], exchange.lands_at(x, y, c), rows),
                        send_sem=sems[2 * s].at[3 * slot + j], recv_sem=sems[2 * s + 1].at[3 * slot + j],
                        device_id=(*chip, c), device_id_type=MESH_ID).start()
        token[...] = jnp.zeros_like(token)

    sem_shapes = []
    for members in sets:
        sem_shapes += [pltpu.SemaphoreType.DMA((3 * len(members),))] * 2
    out = pl.pallas_call(
        body, name=name,
        out_shape=sem_shapes + _hbm_like(arrays) + [TOKEN],
        in_specs=[ANY] * n_deps + [HBM] * len(arrays),
        out_specs=[SEM] * (2 * n_sets) + [HBM] * len(arrays) + [pl.BlockSpec(memory_space=pltpu.VMEM)],
        input_output_aliases={n_deps + i: 2 * n_sets + i for i in range(len(arrays))},
        compiler_params=pltpu.CompilerParams(has_side_effects=DATAFLOW),
    )(*deps, *[_in_hbm(a) for a in arrays])
    sems = [(out[2 * s], out[2 * s + 1]) for s in range(n_sets)]
    thru = list(out[2 * n_sets:2 * n_sets + len(arrays)])
    return sems, (thru[:first_land] if sources is not None else None), thru[first_land:], out[-1]


def _ici_copies_wait(sems, members, sources, landings, exchange, after, name):
    n = len(landings)
    arrays = (list(sources) if sources is not None else []) + list(landings)
    first_land = len(arrays) - n

    def body(*refs):
        src, land = refs[:n], refs[first_land:first_land + n]
        send_sems, recv_sems = refs[len(arrays)], refs[len(arrays) + 1]
        x, y, c = _position()
        for slot, rows in enumerate(members):
            for j, chip in enumerate(_other_chips(x, y)):
                cp = pltpu.make_async_remote_copy(
                    src_ref=_shard_rows(src[slot], exchange.sent(x, y, c, chip), rows),
                    dst_ref=_shard_rows(land[slot], exchange.arrives_at(chip, c), rows),
                    send_sem=send_sems.at[3 * slot + j], recv_sem=recv_sems.at[3 * slot + j],
                    device_id=(*chip, c), device_id_type=MESH_ID)
                cp.wait_send()
                cp.wait_recv()

    out = pl.pallas_call(
        body, name=name, out_shape=_hbm_like(arrays),
        in_specs=[HBM] * len(arrays) + [SEM, SEM] + [ANY] * len(after), out_specs=[HBM] * len(arrays),
        input_output_aliases={i: i for i in range(len(arrays))},
        compiler_params=pltpu.CompilerParams(has_side_effects=DATAFLOW),
    )(*arrays, sems[0], sems[1], *after)
    return list(out[first_land:])


def _d2d_gather(buffers, rows, name):
    n = len(buffers)

    def body(*refs):
        land = refs[n:2 * n]
        send_sems, recv_sems = refs[2 * n:]
        x, y, c = _position()
        chips = [(x, y)] + _other_chips(x, y)
        sends, recvs = [], []
        for k in range(n):
            for j, chip in enumerate(chips):
                for copies, core in ((sends, c), (recvs, 1 - c)):
                    block = _shard_rows(land[k], _device_index(chip, core), rows[k])
                    copies.append(pltpu.make_async_remote_copy(
                        src_ref=block, dst_ref=block, send_sem=send_sems.at[k, j], recv_sem=recv_sems.at[k, j],
                        device_id=(x, y, 1 - c), device_id_type=MESH_ID))
        for cp in sends:
            cp.start()
        for cp in recvs:
            cp.wait_recv()
        for cp in sends:
            cp.wait_send()

    return pl.pallas_call(
        body, name=name, out_shape=[jax.ShapeDtypeStruct(a.shape, a.dtype) for a in buffers],
        in_specs=[ANY] * n, out_specs=[ANY] * n, input_output_aliases={i: i for i in range(n)},
        scratch_shapes=[pltpu.SemaphoreType.DMA((n, 4)), pltpu.SemaphoreType.DMA((n, 4))],
    )(*buffers)


def _rs_pair(grads, name):
    n = len(grads)
    rows = [g.shape[0] // N_DEV for g in grads]

    def body(*refs):
        ins, outs = refs[:n], refs[n:2 * n]
        send_sems, recv_sems = refs[2 * n:]
        x, y, c = _position()
        copies = []
        for k in range(n):
            for q in range(4):
                copies.append(pltpu.make_async_remote_copy(
                    src_ref=_shard_rows(ins[k], 2 * q + 1 - c, rows[k]), dst_ref=_shard_rows(outs[k], q, rows[k]),
                    send_sem=send_sems.at[k, q], recv_sem=recv_sems.at[k, q], device_id=(x, y, 1 - c),
                    device_id_type=MESH_ID))
        for cp in copies:
            cp.start()
        for cp in copies:
            cp.wait()

    return pl.pallas_call(
        body, out_shape=[jax.ShapeDtypeStruct((4 * r, g.shape[1]), g.dtype) for g, r in zip(grads, rows)],
        in_specs=[ANY] * n, out_specs=[ANY] * n,
        scratch_shapes=[pltpu.SemaphoreType.DMA((n, 4)), pltpu.SemaphoreType.DMA((n, 4))],
        name=name)(*grads)


def _wgrad_pair(lhs, rhs, name, *, lhs_is_transposed, deps=()):
    t = rhs.shape[0]
    n = lhs.shape[0] if lhs_is_transposed else lhs.shape[1]
    r = n // N_DEV
    n_chips = N_DEV // 2
    per = 1 if (2 * r) % BLK == 0 else 2
    steps = n_chips // per

    def body(l_ref, r_ref, kept_ref, recv_ref, res, send_sems, recv_sems):
        q = pl.program_id(0)
        slot = q % 2
        x, y, c = _position()

        def send(step, buf, i):
            return pltpu.make_async_remote_copy(
                src_ref=res.at[buf, pl.ds(pl.multiple_of((2 * i + 1 - c) * r, 16), r), :],
                dst_ref=_shard_rows(recv_ref, step * per + i, r),
                send_sem=send_sems.at[buf, i], recv_sem=recv_sems.at[step * per + i],
                device_id=(x, y, 1 - c), device_id_type=MESH_ID)

        @pl.when(q >= 2)
        def _():
            for i in range(per):
                send(q - 2, slot, i).wait_send()

        if lhs_is_transposed:
            res[slot] = _dot(l_ref[...], r_ref[...]).astype(BF)
        else:
            res[slot] = _dot_tn(l_ref[...], r_ref[...]).astype(BF)
        for i in range(per):
            kept_ref[i * r:(i + 1) * r, :] = res[slot, pl.ds(pl.multiple_of((2 * i + c) * r, 16), r), :]
            send(q, slot, i).start()

        @pl.when(q == steps - 1)
        def _():
            for i in range(per):
                if steps > 1:
                    send(q - 1, 1 - slot, i).wait_send()
                send(q, slot, i).wait_send()
            for chip in range(n_chips):
                send(chip // per, 0, chip % per).wait_recv()

    width = 2 * r * per
    lhs_spec = pl.BlockSpec((width, t), lambda q: (q, 0)) if lhs_is_transposed else pl.BlockSpec((t, width), lambda q: (0, q))
    return _call(
        body, deps, (lhs, rhs), grid=(steps,),
        in_specs=[lhs_spec, _resident((t, D))],
        out_specs=[pl.BlockSpec((per * r, D), lambda q: (q, 0)), ANY],
        out_shape=[jax.ShapeDtypeStruct((n // 2, D), BF)] * 2,
        scratch_shapes=[pltpu.VMEM((2, width, D), BF), pltpu.SemaphoreType.DMA((2, per)),
                        pltpu.SemaphoreType.DMA((n_chips,))],
        compiler_params=_params(1), name=name)


def _pair_add(grad, received, place, name, kept_only=False):
    r = received.shape[0] // 4
    tr = 352 if r % 352 == 0 else r
    per = r // tr
    parity = 0 if kept_only else 1

    def body(place_ref, g_ref, r_ref, o_ref, land_ref):
        total = (g_ref[...].astype(F32) + r_ref[...].astype(F32)).astype(BF)
        o_ref[...] = total

        @pl.when(pl.program_id(1) == place_ref[1])
        def _():
            land_ref[...] = total

    return pl.pallas_call(
        body,
        grid_spec=pltpu.PrefetchScalarGridSpec(
            num_scalar_prefetch=1, grid=(per, 4),
            in_specs=[pl.BlockSpec((tr, D), lambda i, q, p: (((1 + parity) * q + parity * p[0]) * per + i, 0)),
                      pl.BlockSpec((tr, D), lambda i, q, p: (q * per + i, 0))],
            out_specs=[pl.BlockSpec((tr, D), lambda i, q, p: (q * per + i, 0)),
                       pl.BlockSpec((tr, D), lambda i, q, p: (p[1] * per + i, 0))]),
        out_shape=[jax.ShapeDtypeStruct(received.shape, BF)] * 2,
        compiler_params=_params(2), name=name)(place, grad, received)


def _all_reduce_small(payload, deps=()):
    r = payload.shape[0]

    def body(in_ref, out_ref, land_ref, send_sems, recv_sems):
        x, y, c = _position()
        me = 4 * x + 2 * y + c
        land_ref[me] = in_ref[...]
        copies = []
        for k in range(1, N_DEV):
            peer = (x ^ (k >> 2), y ^ ((k >> 1) & 1), c ^ (k & 1))
            copies.append(pltpu.make_async_remote_copy(
                src_ref=in_ref, dst_ref=land_ref.at[me], send_sem=send_sems.at[k - 1], recv_sem=recv_sems.at[k - 1],
                device_id=peer, device_id_type=MESH_ID))
        for cp in copies:
            cp.start()
        for k in range(1, N_DEV):
            peer_index = me ^ k
            pltpu.make_async_remote_copy(
                src_ref=in_ref, dst_ref=land_ref.at[peer_index], send_sem=send_sems.at[k - 1], recv_sem=recv_sems.at[k - 1],
                device_id=(x, y, c), device_id_type=MESH_ID).wait_recv()
        for cp in copies:
            cp.wait_send()
        acc = land_ref[0]
        for d in range(1, N_DEV):
            acc = acc + land_ref[d]
        out_ref[...] = acc

    return _call(
        body, deps, (payload,), out_shape=jax.ShapeDtypeStruct((r, D), F32),
        in_specs=[pl.BlockSpec(memory_space=pltpu.VMEM)], out_specs=pl.BlockSpec(memory_space=pltpu.VMEM),
        scratch_shapes=[pltpu.VMEM((N_DEV, r, D), F32), pltpu.SemaphoreType.DMA((N_DEV - 1,)),
                        pltpu.SemaphoreType.DMA((N_DEV - 1,))],
        name="all_reduce_small")


def _adamw_math(w, g, m, v):
    m = ADAM_B1 * m + (1.0 - ADAM_B1) * g
    v = ADAM_B2 * v + (1.0 - ADAM_B2) * (g * g)
    m_hat = m / (1.0 - ADAM_B1 ** ADAM_STEP)
    v_hat = v / (1.0 - ADAM_B2 ** ADAM_STEP)
    delta = -ADAM_LR * (m_hat / (jnp.sqrt(v_hat) + ADAM_EPS) + ADAM_WD * w)
    return delta, m, v


def _sum_partials(blocks):
    g = blocks[0].astype(F32)
    for blk in blocks[1:]:
        g = g + blk.astype(F32)
    return g


def _reduce_adamw(landed, w, m, v, name):
    r = w.shape[0]
    tr = 352 if r % 352 == 0 else r
    per = r // tr

    def body(r0, r1, r2, r3, w_ref, m_ref, v_ref, g_ref, d_ref, nm_ref, nv_ref):
        g = _sum_partials([r0[...], r1[...], r2[...], r3[...]])
        g_ref[...] = g
        d_ref[...], nm_ref[...], nv_ref[...] = _adamw_math(w_ref[...], g, m_ref[...], v_ref[...])

    tile = _row_tile(tr, D)
    return pl.pallas_call(
        body, grid=(per,),
        in_specs=[pl.BlockSpec((tr, D), lambda i, q=q: (q * per + i, 0)) for q in range(4)] + [tile] * 3,
        out_specs=[tile] * 4, out_shape=[jax.ShapeDtypeStruct(w.shape, F32)] * 4,
        compiler_params=_params(1), name=name)(landed, landed, landed, landed, w, m, v)


def _adamw_small(w, g, m, v, name):
    def body(w_ref, g_ref, m_ref, v_ref, d_ref, nm_ref, nv_ref):
        d_ref[...], nm_ref[...], nv_ref[...] = _adamw_math(w_ref[...], g_ref[...], m_ref[...], v_ref[...])

    return pl.pallas_call(body, out_shape=[jax.ShapeDtypeStruct(w.shape, F32)] * 3, name=name)(w, g, m, v)


WEIGHTS = ("ffn1_norm", "ffn1_w_in", "ffn1_w_out", "mix_norm", "w_in", "conv_dw_kernel", "conv_dw_bias", "conv_ln_g",
           "conv_ln_b", "conv_w_proj", "q_norm", "k_norm", "attn_sinks", "rel_bias", "attn_w_o", "w_out", "ffn2_norm",
           "ffn2_w_in", "ffn2_w_out")
MATRICES = ("ffn1_w_in", "ffn1_w_out", "w_in", "conv_w_proj", "attn_w_o", "w_out", "ffn2_w_in", "ffn2_w_out")
COLUMN_SHARDED = ("ffn1_w_in", "w_in", "ffn2_w_in")
ROW_VECTORS = ("ffn1_norm", "mix_norm", "conv_dw_bias", "conv_ln_g", "conv_ln_b", "ffn2_norm")
PACKED = (("q_norm", HD), ("k_norm", HD), ("attn_sinks", NQ), ("rel_bias", NBUCKET * NQ))
GATHER = _Exchange(gather=True)
SCATTER = _Exchange(gather=False)
GATHER_STAGES = ("ffn1", "mix", "ffn2")
STAGE_MEMBERS = {"ffn1": ("ffn1_w_in", "ffn1_w_out"), "mix": ("w_in", "conv_w_proj", "attn_w_o", "w_out", "taps"),
                 "ffn2": ("ffn2_w_in", "ffn2_w_out")}
ROW_PACKED = len(ROW_VECTORS)
ROW_LOSS = ROW_PACKED + 1
ROW_TAPS = 8
PAYLOAD_ROWS = ROW_TAPS + CWP


def _pack_small(values, last_row):
    packed = jnp.concatenate([values[k].reshape(-1) for k, _ in PACKED])
    packed = jnp.pad(packed, (0, D - packed.shape[0])).reshape(1, D)
    return jnp.concatenate([values[k].reshape(1, D) for k in ROW_VECTORS] + [packed, last_row], axis=0)


def _unpack_small(rows):
    out = {k: rows[i] for i, k in enumerate(ROW_VECTORS)}
    at = 0
    for k, size in PACKED:
        out[k] = rows[ROW_PACKED, at:at + size]
        at += size
    out["rel_bias"] = out["rel_bias"].reshape(NBUCKET, NQ)
    return out


def kernel(x, ffn1_norm, ffn1_w_in, ffn1_w_out, mix_norm, w_in, conv_dw_kernel, conv_dw_bias, conv_ln_g, conv_ln_b, conv_w_proj, q_norm, k_norm, attn_sinks, rel_bias, attn_w_o, w_out, ffn2_norm, ffn2_w_in, ffn2_w_out, loss_target, m_ffn1_norm, m_ffn1_w_in, m_ffn1_w_out, m_mix_norm, m_w_in, m_conv_dw_kernel, m_conv_dw_bias, m_conv_ln_g, m_conv_ln_b, m_conv_w_proj, m_q_norm, m_k_norm, m_attn_sinks, m_rel_bias, m_attn_w_o, m_w_out, m_ffn2_norm, m_ffn2_w_in, m_ffn2_w_out, v_ffn1_norm, v_ffn1_w_in, v_ffn1_w_out, v_mix_norm, v_w_in, v_conv_dw_kernel, v_conv_dw_bias, v_conv_ln_g, v_conv_ln_b, v_conv_w_proj, v_q_norm, v_k_norm, v_attn_sinks, v_rel_bias, v_attn_w_o, v_w_out, v_ffn2_norm, v_ffn2_w_in, v_ffn2_w_out):
    w = dict(ffn1_norm=ffn1_norm, ffn1_w_in=ffn1_w_in, ffn1_w_out=ffn1_w_out, mix_norm=mix_norm, w_in=w_in,
             conv_dw_kernel=conv_dw_kernel, conv_dw_bias=conv_dw_bias, conv_ln_g=conv_ln_g, conv_ln_b=conv_ln_b,
             conv_w_proj=conv_w_proj, q_norm=q_norm, k_norm=k_norm, attn_sinks=attn_sinks, rel_bias=rel_bias,
             attn_w_o=attn_w_o, w_out=w_out, ffn2_norm=ffn2_norm, ffn2_w_in=ffn2_w_in, ffn2_w_out=ffn2_w_out)
    m = dict(ffn1_norm=m_ffn1_norm, ffn1_w_in=m_ffn1_w_in, ffn1_w_out=m_ffn1_w_out, mix_norm=m_mix_norm, w_in=m_w_in,
             conv_dw_kernel=m_conv_dw_kernel, conv_dw_bias=m_conv_dw_bias, conv_ln_g=m_conv_ln_g, conv_ln_b=m_conv_ln_b,
             conv_w_proj=m_conv_w_proj, q_norm=m_q_norm, k_norm=m_k_norm, attn_sinks=m_attn_sinks, rel_bias=m_rel_bias,
             attn_w_o=m_attn_w_o, w_out=m_w_out, ffn2_norm=m_ffn2_norm, ffn2_w_in=m_ffn2_w_in, ffn2_w_out=m_ffn2_w_out)
    v = dict(ffn1_norm=v_ffn1_norm, ffn1_w_in=v_ffn1_w_in, ffn1_w_out=v_ffn1_w_out, mix_norm=v_mix_norm, w_in=v_w_in,
             conv_dw_kernel=v_conv_dw_kernel, conv_dw_bias=v_conv_dw_bias, conv_ln_g=v_conv_ln_g, conv_ln_b=v_conv_ln_b,
             conv_w_proj=v_conv_w_proj, q_norm=v_q_norm, k_norm=v_k_norm, attn_sinks=v_attn_sinks, rel_bias=v_rel_bias,
             attn_w_o=v_attn_w_o, w_out=v_w_out, ffn2_norm=v_ffn2_norm, ffn2_w_in=v_ffn2_w_in, ffn2_w_out=v_ffn2_w_out)
    px, py, pc = _position()
    me = 4 * px + 2 * py + pc
    place = jnp.stack([pc, 2 * px + py]).astype(jnp.int32)

    rows_of = lambda k, a: a.T if k in COLUMN_SHARDED else a
    buffers = dict(zip(MATRICES + ("taps",), _prep([rows_of(k, w[k]) for k in MATRICES], conv_dw_kernel,
                                                   me.astype(jnp.int32).reshape(1))))
    landings, sets = [], []
    for stage in GATHER_STAGES:
        sets.append([(len(landings) + i, buffers[k].shape[0] // N_DEV) for i, k in enumerate(STAGE_MEMBERS[stage])])
        landings += [buffers[k] for k in STAGE_MEMBERS[stage]]
    sems, _, land_thru, _ = _ici_copies_start(sets, None, landings, GATHER, "gather_start")

    def weights_of(stage, after):
        s = GATHER_STAGES.index(stage)
        rows = [r for _, r in sets[s]]
        landed = _ici_copies_wait(sems[s], rows, None, [land_thru[k] for k, _ in sets[s]], GATHER, list(after),
                                  "gather_wait_" + stage)
        out = dict(zip(STAGE_MEMBERS[stage], _d2d_gather(landed, rows, "gather_d2d_" + stage)))
        if "taps" in out:
            taps = out.pop("taps")
            out["conv_dw_kernel"] = jnp.transpose(taps.reshape(N_DEV, CWP, BLK), (1, 0, 2)).reshape(CWP, D)[:CW]
        return out

    in_flight = []

    def wgrad(lhs, rhs, name, lhs_is_transposed, deps=()):
        return _wgrad_pair(lhs, rhs, name, lhs_is_transposed=lhs_is_transposed, deps=deps)

    def grads_done(stage, grads):
        names = list(grads)
        added = []
        for k in names:
            if isinstance(grads[k], (tuple, list)):
                kept, received = grads[k]
                added.append(_pair_add(kept, received, place, "pair_add_" + k, kept_only=True))
            else:
                received, = _rs_pair([grads[k]], "rs_pair_" + k)
                added.append(_pair_add(grads[k], received, place, "pair_add_" + k))
        partials = [p for p, _ in added]
        members = [(i, p.shape[0] // 4) for i, p in enumerate(partials)]
        sem, p_thru, l_thru, token = _ici_copies_start([members], partials, [l for _, l in added], SCATTER,
                                                       "scatter_start_" + stage)
        in_flight.append((stage, names, sem[0], p_thru, l_thru, token))
        return [token]

    reduced = []

    def small_done(gv, sq):
        payload = jnp.concatenate([_pack_small(gv, sq), jnp.pad(gv["conv_dw_kernel"], ((0, CWP - CW), (0, 0)))], axis=0)
        reduced.append(_all_reduce_small(payload))
        return reduced

    vec = {k: w[k] for k in WEIGHTS if k not in MATRICES and k != "conv_dw_kernel"}
    dx0 = _local_step(x[0], loss_target[0], vec, weights_of, wgrad, grads_done, small_done)
    total = reduced[0]
    loss = (0.5 / D) * jnp.sum(total[ROW_LOSS])

    grads, delta, new_m, new_v = {}, {}, {}, {}
    after = [in_flight[-1][-1]]
    for stage, names, sem, p_thru, l_thru, _ in in_flight:
        landed = _ici_copies_wait(sem, [p.shape[0] // 4 for p in p_thru], p_thru, l_thru, SCATTER, after,
                                  "scatter_wait_" + stage)
        after = []
        for k, buf in zip(names, landed):
            out = _reduce_adamw(buf, rows_of(k, w[k]), rows_of(k, m[k]), rows_of(k, v[k]), "adamw_" + k)
            grads[k], delta[k], new_m[k], new_v[k] = [rows_of(k, a) for a in out]
            after.append(out[1])
    zero_row = jnp.zeros((1, D), F32)
    d8, m8, v8 = _adamw_small(_pack_small(w, zero_row), total[:ROW_TAPS], _pack_small(m, zero_row),
                              _pack_small(v, zero_row), "adamw_small")
    grads.update(_unpack_small(total[:ROW_TAPS]))
    delta.update(_unpack_small(d8))
    new_m.update(_unpack_small(m8))
    new_v.update(_unpack_small(v8))
    k = "conv_dw_kernel"
    grads[k] = lax.dynamic_slice_in_dim(total[ROW_TAPS:ROW_TAPS + CW], me * BLK, BLK, axis=1)
    delta[k], new_m[k], new_v[k] = _adamw_small(w[k], grads[k], m[k], v[k], "adamw_taps")

    return (loss, dx0[None], *[grads[k] for k in WEIGHTS], *[delta[k] for k in WEIGHTS],
            *[new_m[k] for k in WEIGHTS], *[new_v[k] for k in WEIGHTS])
```

```python
import functools
import math

import numpy as np
import jax
import jax.numpy as jnp
from jax import lax
from jax.experimental import pallas as pl
from jax.experimental.pallas import tpu as pltpu

F32 = jnp.float32
BF = jnp.bfloat16

D = 1024
F = 2816
INW = 5632
CW = 31
CWP = 32
HD = 64
NQ = 16
NKV = 4
GRP = NQ // NKV
BLK = 128
NBUCKET = 32
EPS = 1e-6
NEG = float(jnp.finfo(jnp.float32).min)
QK_SCALE = 1.0 / math.sqrt(HD)
R_CONV = (0, 2048)
R_QKV = (2048, 3584)
R_Q = (2048, 3072)
R_KV = (3072, 3584)
R_GATE = (3584, 5632)

N_DEV = 8
VMEM_LIMIT_V7X = 56 * 1024 * 1024
ROW_TILE = 256
ROW_TILE_WIDE = 512

ADAM_LR = 0.001
ADAM_B1 = 0.9
ADAM_B2 = 0.999
ADAM_EPS = 1e-08
ADAM_WD = 0.01
ADAM_STEP = 10

NT_DIMS = (((1,), (1,)), ((), ()))
TN_DIMS = (((0,), (0,)), ((), ()))


def _dot(a, b):
    return jnp.dot(a, b, preferred_element_type=F32)


def _dot_nt(a, b):
    return lax.dot_general(a, b, NT_DIMS, preferred_element_type=F32)


def _dot_tn(a, b):
    return lax.dot_general(a, b, TN_DIMS, preferred_element_type=F32)


def _sig(x):
    return 0.5 * jnp.tanh(0.5 * x) + 0.5


ANY = pl.BlockSpec(memory_space=pl.ANY)


def _call(body, deps, args, **kw):
    n = len(deps)
    if n:
        kw["in_specs"] = [ANY] * n + list(kw["in_specs"])
        return pl.pallas_call(lambda *refs: body(*refs[n:]), **kw)(*deps, *args)
    return pl.pallas_call(body, **kw)(*args)


def _params(n_axes):
    return pltpu.CompilerParams(dimension_semantics=("arbitrary",) * n_axes, vmem_limit_bytes=VMEM_LIMIT_V7X)


def _resident(shape):
    zeros = (0,) * len(shape)
    return pl.BlockSpec(shape, lambda *_: zeros, pipeline_mode=pl.Buffered(1))


def _row_tile(rows, cols):
    return pl.BlockSpec((rows, cols), lambda i: (i, 0))


def _rms_stats(x):
    r = lax.rsqrt(jnp.mean(x * x, axis=-1, keepdims=True) + EPS)
    return r, x * r


def _rms_bwd(dn, x, g):
    r, xh = _rms_stats(x)
    dxh = dn * g
    dx = r * (dxh - xh * jnp.mean(dxh * xh, axis=-1, keepdims=True))
    return dx, jnp.sum(dn * xh, axis=0, keepdims=True)


def _ffn_fwd(x, g, w_in_t, w_out, name, target=None):
    t = x.shape[0]
    tm = min(ROW_TILE_WIDE, t)
    with_loss = target is not None

    def body(*refs):
        if with_loss:
            x_ref, g_ref, w_ref, wo_ref, t_ref, n_ref, u_ref, dy_ref, sq_ref = refs
        else:
            x_ref, g_ref, w_ref, wo_ref, n_ref, u_ref, xo_ref = refs
        x = x_ref[...]
        r, xh = _rms_stats(x)
        n = (xh * g_ref[...]).astype(BF)
        n_ref[...] = n
        u = _dot_nt(n, w_ref[...])
        u_ref[...] = u.astype(BF)
        a = u[:, :F]
        b = u[:, F:]
        h = (a * _sig(a) * b).astype(BF)
        xo = x + 0.5 * _dot(h, wo_ref[...])
        if with_loss:
            err = xo - t_ref[...]
            dy_ref[...] = err * (1.0 / D)

            @pl.when(pl.program_id(0) == 0)
            def _():
                sq_ref[...] = jnp.zeros_like(sq_ref)

            sq_ref[...] += jnp.sum(err * err, axis=0, keepdims=True)
        else:
            xo_ref[...] = xo

    in_specs = [_row_tile(tm, D), _resident((1, D)), _resident((INW, D)), _resident((F, D))]
    args = [x, g, w_in_t, w_out]
    out_specs = [_row_tile(tm, D), _row_tile(tm, INW), _row_tile(tm, D)]
    out_shape = [jax.ShapeDtypeStruct((t, D), BF), jax.ShapeDtypeStruct((t, INW), BF), jax.ShapeDtypeStruct((t, D), F32)]
    if with_loss:
        in_specs.append(_row_tile(tm, D))
        args.append(target)
        out_specs.append(pl.BlockSpec((1, D), lambda i: (0, 0)))
        out_shape.append(jax.ShapeDtypeStruct((1, D), F32))
    return pl.pallas_call(body, grid=(t // tm,), in_specs=in_specs, out_specs=out_specs, out_shape=out_shape,
                          compiler_params=_params(1), name=name)(*args)


def _ffn_bwd(dxo, x, g, u, w_in_t, w_out, name, deps=()):
    t = x.shape[0]
    tm = min(ROW_TILE, t)

    def body(dxo_ref, x_ref, g_ref, u_ref, w_ref, wo_ref, dx_ref, du_ref, h_ref, dy_ref, dg_ref):
        dxo = dxo_ref[...]
        dy = (0.5 * dxo).astype(BF)
        dy_ref[...] = dy
        dh = _dot_nt(dy, wo_ref[...])
        a = u_ref[:, :F].astype(F32)
        b = u_ref[:, F:].astype(F32)
        s = _sig(a)
        sa = a * s
        h_ref[...] = (sa * b).astype(BF)
        du_ref[:, :F] = (dh * b * (s * (1.0 + a * (1.0 - s)))).astype(BF)
        du_ref[:, F:] = (dh * sa).astype(BF)
        dn = _dot(du_ref[...], w_ref[...])
        dx, dg = _rms_bwd(dn, x_ref[...], g_ref[...])
        dx_ref[...] = dxo + dx

        @pl.when(pl.program_id(0) == 0)
        def _():
            dg_ref[...] = jnp.zeros_like(dg_ref)

        dg_ref[...] += dg

    return _call(
        body, deps, (dxo, x, g, u, w_in_t, w_out), grid=(t // tm,),
        in_specs=[_row_tile(tm, D), _row_tile(tm, D), _resident((1, D)), _row_tile(tm, INW), _resident((INW, D)),
                  _resident((F, D))],
        out_specs=[_row_tile(tm, D), _row_tile(tm, INW), _row_tile(tm, F), _row_tile(tm, D),
                   pl.BlockSpec((1, D), lambda i: (0, 0))],
        out_shape=[jax.ShapeDtypeStruct((t, D), F32), jax.ShapeDtypeStruct((t, INW), BF), jax.ShapeDtypeStruct((t, F), BF),
                   jax.ShapeDtypeStruct((t, D), BF), jax.ShapeDtypeStruct((1, D), F32)],
        compiler_params=_params(1), name=name)


def _wgrad(lhs, rhs, name, *, lhs_is_transposed, chunk, deps=()):
    t = rhs.shape[0]
    n = lhs.shape[0] if lhs_is_transposed else lhs.shape[1]
    c = min(chunk, n)

    def body(l_ref, r_ref, o_ref):
        if lhs_is_transposed:
            o_ref[...] = _dot(l_ref[...], r_ref[...]).astype(BF)
        else:
            o_ref[...] = _dot_tn(l_ref[...], r_ref[...]).astype(BF)

    lhs_spec = pl.BlockSpec((c, t), lambda j: (j, 0)) if lhs_is_transposed else pl.BlockSpec((t, c), lambda j: (0, j))
    return _call(
        body, deps, (lhs, rhs), grid=(n // c,),
        in_specs=[lhs_spec, _resident((t, D))],
        out_specs=pl.BlockSpec((c, D), lambda j: (j, 0)),
        out_shape=jax.ShapeDtypeStruct((n, D), BF),
        compiler_params=_params(1), name=name)


def _wgrad_mix(duc, dq_t, dkv_t, dgp, hm):
    t = hm.shape[0]
    c = 512
    first_q, first_kv, first_gate = R_Q[0] // c, R_KV[0] // c, R_GATE[0] // c

    def body(uc_ref, q_ref, kv_ref, gp_ref, h_ref, o_ref):
        j = pl.program_id(0)

        @pl.when(j < first_q)
        def _():
            o_ref[...] = _dot_tn(uc_ref[...], h_ref[...]).astype(BF)

        @pl.when((j >= first_q) & (j < first_kv))
        def _():
            o_ref[...] = _dot(q_ref[...], h_ref[...]).astype(BF)

        @pl.when((j >= first_kv) & (j < first_gate))
        def _():
            o_ref[...] = _dot(kv_ref[...], h_ref[...]).astype(BF)

        @pl.when(j >= first_gate)
        def _():
            o_ref[...] = _dot_tn(gp_ref[...], h_ref[...]).astype(BF)

    return pl.pallas_call(
        body, grid=(INW // c,),
        in_specs=[pl.BlockSpec((t, c), lambda j: (0, jnp.clip(j, 0, first_q - 1))),
                  pl.BlockSpec((c, t), lambda j: (jnp.clip(j - first_q, 0, first_kv - first_q - 1), 0)),
                  pl.BlockSpec((c, t), lambda j: (jnp.clip(j - first_kv, 0, first_gate - first_kv - 1), 0)),
                  pl.BlockSpec((t, c), lambda j: (0, jnp.clip(j - first_gate, 0, INW // c - first_gate - 1))),
                  _resident((t, D))],
        out_specs=pl.BlockSpec((c, D), lambda j: (j, 0)),
        out_shape=jax.ShapeDtypeStruct((INW, D), BF),
        compiler_params=_params(1), name="mix_dw_in")(duc, dq_t, dkv_t, dgp, hm)


def _mix_proj(x, g, w_t):
    t = x.shape[0]
    tm = min(ROW_TILE_WIDE, t)

    def body(x_ref, g_ref, w_ref, hm_ref, uc_ref, gp_ref, qkv_ref):
        r, xh = _rms_stats(x_ref[...])
        hm = (xh * g_ref[...]).astype(BF)
        hm_ref[...] = hm
        uc_ref[...] = _dot_nt(hm, w_ref[R_CONV[0]:R_CONV[1], :]).astype(BF)
        gp_ref[...] = _dot_nt(hm, w_ref[R_GATE[0]:R_GATE[1], :]).astype(BF)
        qkv_ref[...] = _dot_nt(w_ref[R_QKV[0]:R_QKV[1], :], hm).astype(BF)

    return pl.pallas_call(
        body, grid=(t // tm,),
        in_specs=[_row_tile(tm, D), _resident((1, D)), _resident((INW, D))],
        out_specs=[_row_tile(tm, D), _row_tile(tm, 2 * D), _row_tile(tm, 2 * D), pl.BlockSpec((1536, tm), lambda i: (0, i))],
        out_shape=[jax.ShapeDtypeStruct((t, D), BF), jax.ShapeDtypeStruct((t, 2 * D), BF),
                   jax.ShapeDtypeStruct((t, 2 * D), BF), jax.ShapeDtypeStruct((1536, t), BF)],
        compiler_params=_params(1), name="mix_proj")(x, g, w_t)


CONV_HALO = 32
CONV_LEAD = CONV_HALO - (CW - 1)


def _glu(uc):
    uc = uc.astype(F32)
    return uc[:, :D] * _sig(uc[:, D:])


def _ln_stats(zc):
    mu = jnp.mean(zc, axis=-1, keepdims=True)
    zm = zc - mu
    r = lax.rsqrt(jnp.mean(zm * zm, axis=-1, keepdims=True) + EPS)
    return r, zm * r


CONV_SHIFTS = 8
CONV_CHUNK = 32


def _store_shifted(buf, rows):
    for b in range(1, CONV_SHIFTS):
        buf[b, 0:rows - 8, :] = buf[0, pl.ds(b, rows - 8), :]


def _conv_fwd(uc, dwk, dwb, lng, lnb):
    t = uc.shape[0]
    tm = min(512, t)
    per = tm // CONV_HALO
    ext = tm + CONV_HALO

    def body(cur_ref, prev_ref, k_ref, kb_ref, g_ref, b_ref, o_ref, zc_ref, zsh):
        i = pl.program_id(0)
        zsh[0, 0:CONV_HALO, :] = _glu(prev_ref[...]) * (i > 0).astype(F32)
        zsh[0, CONV_HALO:, :] = _glu(cur_ref[...])
        _store_shifted(zsh, ext)

        def chunk(ci, carry):
            r0 = pl.multiple_of(ci * CONV_CHUNK, CONV_CHUNK)
            acc = jnp.zeros((CONV_CHUNK, D), F32) + kb_ref[...]
            for w in range(CW):
                a, b = divmod(CONV_LEAD + w, 8)
                acc = acc + k_ref[w:w + 1, :] * zsh[b, pl.ds(r0 + 8 * a, CONV_CHUNK), :]
            zc_ref[pl.ds(r0, CONV_CHUNK), :] = acc
            return carry

        lax.fori_loop(0, tm // CONV_CHUNK, chunk, 0)
        r, xh = _ln_stats(zc_ref[...])
        y = xh * g_ref[...] + b_ref[...]
        o_ref[...] = (y * _sig(y)).astype(BF)

    return pl.pallas_call(
        body, grid=(t // tm,),
        in_specs=[_row_tile(tm, 2 * D),
                  pl.BlockSpec((CONV_HALO, 2 * D), lambda i: (jnp.maximum(i * per - 1, 0), 0)),
                  _resident((CWP, D)), _resident((1, D)), _resident((1, D)), _resident((1, D))],
        out_specs=[_row_tile(tm, D), _row_tile(tm, D)],
        out_shape=[jax.ShapeDtypeStruct((t, D), BF), jax.ShapeDtypeStruct((t, D), F32)],
        scratch_shapes=[pltpu.VMEM((CONV_SHIFTS, ext, D), F32)],
        compiler_params=_params(1), name="conv_fwd")(uc, uc, dwk, dwb, lng, lnb)


def _conv_bwd(uc, zc, dzs, dwk, lng, lnb):
    t = uc.shape[0]
    tm = min(ROW_TILE_WIDE, t)
    per = tm // CONV_HALO
    n_tiles = t // tm
    ext = tm + CONV_HALO
    last_block = t // CONV_HALO - 1

    def body(cur_ref, zc_ref, zcn_ref, dz_ref, dzn_ref, k_ref, g_ref, b_ref,
             duc_ref, dk_ref, dkb_ref, dg_ref, db_ref, dsh, dk8, z_scr):
        i = pl.program_id(0)

        @pl.when(i == 0)
        def _():
            dk8[...] = jnp.zeros_like(dk8)
            dkb_ref[...] = jnp.zeros_like(dkb_ref)
            dg_ref[...] = jnp.zeros_like(dg_ref)
            db_ref[...] = jnp.zeros_like(db_ref)

        has_next = (i < n_tiles - 1).astype(F32)
        z_scr[...] = _glu(cur_ref[...])
        gain = g_ref[...]

        def ln_silu_bwd(zc, dzs, live):
            r, xh = _ln_stats(zc)
            y = xh * gain + b_ref[...]
            sy = _sig(y)
            dy = dzs * (sy * (1.0 + y * (1.0 - sy))) * live
            dxh = dy * gain
            dzc = r * (dxh - jnp.mean(dxh, axis=-1, keepdims=True) - xh * jnp.mean(dxh * xh, axis=-1, keepdims=True))
            return dzc, dy, xh

        dzc, dy, xh = ln_silu_bwd(zc_ref[...], dz_ref[...], 1.0)
        dsh[0, 0:tm, :] = dzc
        dg_ref[...] += jnp.sum(dy * xh, axis=0, keepdims=True)
        db_ref[...] += jnp.sum(dy, axis=0, keepdims=True)
        dkb_ref[...] += jnp.sum(dzc, axis=0, keepdims=True)
        dsh[0, tm:, :] = ln_silu_bwd(zcn_ref[...], dzn_ref[...], has_next)[0]
        _store_shifted(dsh, ext)

        def chunk(ci, carry):
            r0 = pl.multiple_of(ci * CONV_CHUNK, CONV_CHUNK)
            z_c = z_scr[pl.ds(r0, CONV_CHUNK), :]
            dz = jnp.zeros((CONV_CHUNK, D), F32)
            for w in range(CW):
                a, b = divmod(CW - 1 - w, 8)
                window = dsh[b, pl.ds(r0 + 8 * a, CONV_CHUNK), :]
                dz = dz + k_ref[w:w + 1, :] * window
                prod = z_c * window
                part = prod[0:8, :]
                for j in range(1, CONV_CHUNK // 8):
                    part = part + prod[8 * j:8 * j + 8, :]
                dk8[w] += part
            ucc = cur_ref[pl.ds(r0, CONV_CHUNK), :].astype(F32)
            sg = _sig(ucc[:, D:])
            duc_ref[pl.ds(r0, CONV_CHUNK), 0:D] = (dz * sg).astype(BF)
            duc_ref[pl.ds(r0, CONV_CHUNK), D:2 * D] = (dz * ucc[:, :D] * sg * (1.0 - sg)).astype(BF)
            return carry

        lax.fori_loop(0, tm // CONV_CHUNK, chunk, 0)

        @pl.when(i == n_tiles - 1)
        def _():
            dk_ref[...] = jnp.sum(dk8[...], axis=1)

    vec = pl.BlockSpec((1, D), lambda i: (0, 0))
    next_halo = pl.BlockSpec((CONV_HALO, D), lambda i: (jnp.minimum((i + 1) * per, last_block), 0))
    return pl.pallas_call(
        body, grid=(n_tiles,),
        in_specs=[_row_tile(tm, 2 * D), _row_tile(tm, D), next_halo, _row_tile(tm, D), next_halo,
                  _resident((CWP, D)), _resident((1, D)), _resident((1, D))],
        out_specs=[_row_tile(tm, 2 * D), pl.BlockSpec((CWP, D), lambda i: (0, 0)), vec, vec, vec],
        out_shape=[jax.ShapeDtypeStruct((t, 2 * D), BF), jax.ShapeDtypeStruct((CWP, D), F32),
                   jax.ShapeDtypeStruct((1, D), F32), jax.ShapeDtypeStruct((1, D), F32), jax.ShapeDtypeStruct((1, D), F32)],
        scratch_shapes=[pltpu.VMEM((CONV_SHIFTS, ext, D), F32), pltpu.VMEM((CWP, 8, D), F32), pltpu.VMEM((tm, D), F32)],
        compiler_params=_params(1), name="conv_bwd")(uc, zc, zc, dzs, dzs, dwk, lng, lnb)


def _norm_rows(xt, g):
    r = lax.rsqrt(jnp.mean(xt * xt, axis=0, keepdims=True) + EPS)
    xh = xt * r
    return xh * g, r, xh


ATT_TQ = 512


def _attn_specs(t, tq):
    per = tq // BLK
    return [pl.BlockSpec((1536, tq), lambda i: (0, i)),
            pl.BlockSpec((512, BLK), lambda i: (2, jnp.maximum(i * per - 1, 0))),
            _resident((HD, 1)), _resident((HD, 1)), _resident((NKV, 1, GRP * BLK)),
            _resident((2, NKV, 2 * BLK, GRP * BLK))]


def _attn_window(hk, sb, qkv_ref, halo_ref, kn_cur, kn_halo):
    v0 = D + NKV * HD + hk * HD
    if sb == 0:
        k_prev = kn_halo[hk]
        v_prev = halo_ref[NKV * HD + hk * HD:NKV * HD + (hk + 1) * HD, :]
    else:
        k_prev = kn_cur[hk][:, (sb - 1) * BLK:sb * BLK]
        v_prev = qkv_ref[v0:v0 + HD, (sb - 1) * BLK:sb * BLK]
    kw = jnp.concatenate([k_prev, kn_cur[hk][:, sb * BLK:(sb + 1) * BLK]], axis=1).astype(BF)
    vw = jnp.concatenate([v_prev, qkv_ref[v0:v0 + HD, sb * BLK:(sb + 1) * BLK]], axis=1)
    return kw, vw


def _attn_probs(kw, qc, bias, sink):
    st = _dot_tn(kw, qc) + bias
    m = jnp.maximum(jnp.max(st, axis=0, keepdims=True), sink)
    p = jnp.exp(st - m)
    e_sink = jnp.exp(sink - m)
    inv = 1.0 / (jnp.sum(p, axis=0, keepdims=True) + e_sink)
    return p * inv, e_sink * inv


def _attn_fwd(qkv_t, qg, kg, sink_rows, bias_t):
    t = qkv_t.shape[1]
    tq = min(ATT_TQ, t)
    n_sub = tq // BLK

    def body(qkv_ref, halo_ref, qg_ref, kg_ref, sink_ref, bias_ref, o_ref):
        i = pl.program_id(0)
        first = (i == 0).astype(jnp.int32)
        kgain = kg_ref[...]
        qgain = qg_ref[...]
        kn_cur = [_norm_rows(qkv_ref[D + h * HD:D + (h + 1) * HD, :].astype(F32), kgain)[0] for h in range(NKV)]
        kn_halo = [_norm_rows(halo_ref[h * HD:(h + 1) * HD, :].astype(F32), kgain)[0] for h in range(NKV)]
        for hk in range(NKV):
            for sb in range(n_sub):
                cols = slice(sb * BLK, (sb + 1) * BLK)
                kw, vw = _attn_window(hk, sb, qkv_ref, halo_ref, kn_cur, kn_halo)
                qc = jnp.concatenate(
                    [_norm_rows(qkv_ref[(GRP * hk + g) * HD:(GRP * hk + g + 1) * HD, cols].astype(F32), qgain)[0] * QK_SCALE
                     for g in range(GRP)], axis=1).astype(BF)
                bias = bias_ref[first, hk] if sb == 0 else bias_ref[0, hk]
                p, _ = _attn_probs(kw, qc, bias, sink_ref[hk])
                o = _dot(vw, p.astype(BF))
                for g in range(GRP):
                    head = GRP * hk + g
                    o_ref[head * HD:(head + 1) * HD, cols] = o[:, g * BLK:(g + 1) * BLK].astype(BF)

    return pl.pallas_call(
        body, grid=(t // tq,),
        in_specs=_attn_specs(t, tq),
        out_specs=pl.BlockSpec((D, tq), lambda i: (0, i)),
        out_shape=jax.ShapeDtypeStruct((D, t), BF),
        compiler_params=_params(1), name="attn_fwd")(qkv_t, qkv_t, qg, kg, sink_rows, bias_t)


def _attn_bwd(qkv_t, do_t, qg, kg, sink_rows, bias_t, deps=()):
    t = qkv_t.shape[1]
    tq = min(ATT_TQ, t)
    n_sub = tq // BLK
    n_tiles = t // tq

    def body(qkv_ref, halo_ref, do_ref, qg_ref, kg_ref, sink_ref, bias_ref,
             dq_ref, ckv_ref, dqg_ref, dsink_ref, dsacc_ref, qg_scr):
        i = pl.program_id(0)

        @pl.when(i == 0)
        def _():
            qg_scr[...] = jnp.zeros_like(qg_scr)
            dsink_ref[...] = jnp.zeros_like(dsink_ref)
            dsacc_ref[...] = jnp.zeros_like(dsacc_ref)

        first = (i == 0).astype(jnp.int32)
        kgain = kg_ref[...]
        qgain = qg_ref[...]
        kn_cur = [_norm_rows(qkv_ref[D + h * HD:D + (h + 1) * HD, :].astype(F32), kgain)[0] for h in range(NKV)]
        kn_halo = [_norm_rows(halo_ref[h * HD:(h + 1) * HD, :].astype(F32), kgain)[0] for h in range(NKV)]
        dqg = jnp.zeros((HD, BLK), F32)
        for hk in range(NKV):
            for sb in range(n_sub):
                cols = slice(sb * BLK, (sb + 1) * BLK)
                kw, vw = _attn_window(hk, sb, qkv_ref, halo_ref, kn_cur, kn_halo)
                qn, qr, qh = [], [], []
                for g in range(GRP):
                    head = GRP * hk + g
                    n_, r_, h_ = _norm_rows(qkv_ref[head * HD:(head + 1) * HD, cols].astype(F32), qgain)
                    qn.append(n_)
                    qr.append(r_)
                    qh.append(h_)
                qc = (jnp.concatenate(qn, axis=1) * QK_SCALE).astype(BF)
                bias = bias_ref[first, hk] if sb == 0 else bias_ref[0, hk]
                p, p_sink = _attn_probs(kw, qc, bias, sink_ref[hk])
                doc = jnp.concatenate([do_ref[(GRP * hk + g) * HD:(GRP * hk + g + 1) * HD, cols] for g in range(GRP)], axis=1)
                dp = _dot_tn(vw, doc)
                delta = jnp.sum(p * dp, axis=0, keepdims=True)
                ds = p * (dp - delta)
                dsink_ref[hk] += -(p_sink * delta)
                dsacc_ref[hk] += ds
                dsb = ds.astype(BF)
                dqc = _dot(kw, dsb) * QK_SCALE
                ckv_ref[sb, hk * HD:(hk + 1) * HD, :] = _dot_nt(qc, dsb)
                ckv_ref[sb, NKV * HD + hk * HD:NKV * HD + (hk + 1) * HD, :] = _dot_nt(doc, p.astype(BF))
                for g in range(GRP):
                    head = GRP * hk + g
                    dqn = dqc[:, g * BLK:(g + 1) * BLK]
                    dqh = dqn * qgain
                    dq = qr[g] * (dqh - qh[g] * jnp.mean(dqh * qh[g], axis=0, keepdims=True))
                    dq_ref[head * HD:(head + 1) * HD, cols] = dq.astype(BF)
                    dqg = dqg + dqn * qh[g]
        qg_scr[...] += dqg

        @pl.when(i == n_tiles - 1)
        def _():
            dqg_ref[...] = jnp.sum(qg_scr[...], axis=1, keepdims=True)

    return _call(
        body, deps, (qkv_t, qkv_t, do_t, qg, kg, sink_rows, bias_t), grid=(n_tiles,),
        in_specs=_attn_specs(t, tq)[:2] + [pl.BlockSpec((D, tq), lambda i: (0, i))] + _attn_specs(t, tq)[2:],
        out_specs=[pl.BlockSpec((D, tq), lambda i: (0, i)),
                   pl.BlockSpec((n_sub, 2 * NKV * HD, 2 * BLK), lambda i: (i, 0, 0)),
                   pl.BlockSpec((HD, 1), lambda i: (0, 0)),
                   pl.BlockSpec((NKV, 1, GRP * BLK), lambda i: (0, 0, 0)),
                   pl.BlockSpec((NKV, 2 * BLK, GRP * BLK), lambda i: (0, 0, 0))],
        out_shape=[jax.ShapeDtypeStruct((D, t), BF),
                   jax.ShapeDtypeStruct((t // BLK, 2 * NKV * HD, 2 * BLK), F32),
                   jax.ShapeDtypeStruct((HD, 1), F32),
                   jax.ShapeDtypeStruct((NKV, 1, GRP * BLK), F32),
                   jax.ShapeDtypeStruct((NKV, 2 * BLK, GRP * BLK), F32)],
        scratch_shapes=[pltpu.VMEM((HD, BLK), F32)],
        compiler_params=_params(1), name="attn_bwd")


def _kv_combine(ckv, qkv_t, kg):
    nb = ckv.shape[0]
    t = nb * BLK
    rows = NKV * HD
    per = min(4, nb)
    steps = nb // per

    def body(c_ref, cn_ref, k_ref, kg_ref, o_ref, dkg_ref, kg_scr):
        n = pl.program_id(0)

        @pl.when(n == 0)
        def _():
            kg_scr[...] = jnp.zeros_like(kg_scr)

        has_next = (n < steps - 1).astype(F32)
        kgain = kg_ref[...]
        dkg = jnp.zeros((HD, BLK), F32)
        for s in range(per):
            cols = slice(s * BLK, (s + 1) * BLK)
            after = c_ref[s + 1, :, :BLK] if s + 1 < per else cn_ref[0, :, :BLK] * has_next
            d = c_ref[s, :, BLK:] + after
            o_ref[rows:, cols] = d[rows:, :].astype(BF)
            for h in range(NKV):
                _, r, kh = _norm_rows(k_ref[h * HD:(h + 1) * HD, cols].astype(F32), kgain)
                dkn = d[h * HD:(h + 1) * HD, :]
                dkh = dkn * kgain
                o_ref[h * HD:(h + 1) * HD, cols] = (r * (dkh - kh * jnp.mean(dkh * kh, axis=0, keepdims=True))).astype(BF)
                dkg = dkg + dkn * kh
        kg_scr[...] += dkg

        @pl.when(n == steps - 1)
        def _():
            dkg_ref[...] = jnp.sum(kg_scr[...], axis=1, keepdims=True)

    return pl.pallas_call(
        body, grid=(steps,),
        in_specs=[pl.BlockSpec((per, 2 * rows, 2 * BLK), lambda n: (n, 0, 0)),
                  pl.BlockSpec((1, 2 * rows, 2 * BLK), lambda n: (jnp.minimum((n + 1) * per, nb - 1), 0, 0)),
                  pl.BlockSpec((rows, per * BLK), lambda n: (D // rows, n)),
                  _resident((HD, 1))],
        out_specs=[pl.BlockSpec((2 * rows, per * BLK), lambda n: (0, n)), pl.BlockSpec((HD, 1), lambda n: (0, 0))],
        out_shape=[jax.ShapeDtypeStruct((2 * rows, t), BF), jax.ShapeDtypeStruct((HD, 1), F32)],
        scratch_shapes=[pltpu.VMEM((HD, BLK), F32)],
        compiler_params=_params(1), name="kv_combine")(ckv, ckv, qkv_t, kg)


def _group_lane_sums(v):
    lane_group = lax.broadcasted_iota(jnp.int32, (1, GRP * BLK), 1) // BLK
    col = lax.broadcasted_iota(jnp.int32, (1, BLK), 1)
    out = jnp.zeros((NKV, BLK), F32)
    for g in range(GRP):
        s = jnp.sum(jnp.where(lane_group == g, v, 0.0), axis=1, keepdims=True)
        out = jnp.where(col == g, s, out)
    return out


def _bias_grad(dsacc, onehot_t):
    per = 8

    def body(ds_ref, oh_ref, o_ref):
        for b in range(per):
            oh = jnp.concatenate([oh_ref[b]] * GRP, axis=1)
            o_ref[b] = _group_lane_sums(jnp.sum(ds_ref[...] * oh[None], axis=1))

    return pl.pallas_call(
        body, grid=(NBUCKET // per,),
        in_specs=[_resident((NKV, 2 * BLK, GRP * BLK)), pl.BlockSpec((per, 2 * BLK, BLK), lambda b: (b, 0, 0))],
        out_specs=pl.BlockSpec((per, NKV, BLK), lambda b: (b, 0, 0)),
        out_shape=jax.ShapeDtypeStruct((NBUCKET, NKV, BLK), F32),
        compiler_params=_params(1), name="bias_grad")(dsacc, onehot_t)


def _sink_grad(dsink_rows):
    def body(d_ref, o_ref):
        o_ref[...] = _group_lane_sums(d_ref[:, 0, :])

    return pl.pallas_call(body, out_shape=jax.ShapeDtypeStruct((NKV, BLK), F32), name="sink_grad")(dsink_rows)


def _mix_out(zs, o_t, gp, x, w_cp, w_o, w_out):
    t = x.shape[0]
    tm = min(ROW_TILE_WIDE, t)

    def body(zs_ref, ot_ref, gp_ref, x_ref, wcp_ref, wo_ref, wout_ref, xo_ref, a_ref, b_ref, m_ref):
        a = _dot(zs_ref[...], wcp_ref[...])
        b = _dot_tn(ot_ref[...], wo_ref[...])
        a_ref[...] = a.astype(BF)
        b_ref[...] = b.astype(BF)
        merged = (_sig(gp_ref[:, :D].astype(F32)) * a + _sig(gp_ref[:, D:].astype(F32)) * b).astype(BF)
        m_ref[...] = merged
        xo_ref[...] = x_ref[...] + _dot(merged, wout_ref[...])

    return pl.pallas_call(
        body, grid=(t // tm,),
        in_specs=[_row_tile(tm, D), pl.BlockSpec((D, tm), lambda i: (0, i)), _row_tile(tm, 2 * D), _row_tile(tm, D),
                  _resident((D, D)), _resident((D, D)), _resident((D, D))],
        out_specs=[_row_tile(tm, D)] * 4,
        out_shape=[jax.ShapeDtypeStruct((t, D), F32)] + [jax.ShapeDtypeStruct((t, D), BF)] * 3,
        compiler_params=_params(1), name="mix_out")(zs, o_t, gp, x, w_cp, w_o, w_out)


def _mix_out_bwd(dx, a, b, gp, w_cp, w_o, w_out, deps=()):
    t = dx.shape[0]
    tm = min(ROW_TILE_WIDE, t)

    def body(dx_ref, a_ref, b_ref, gp_ref, wcp_ref, wo_ref, wout_ref, dzs_ref, dot_ref, dgp_ref, da_ref, db_ref, dxb_ref):
        dxb = dx_ref[...].astype(BF)
        dxb_ref[...] = dxb
        dm = _dot_nt(dxb, wout_ref[...])
        gc = _sig(gp_ref[:, :D].astype(F32))
        ga = _sig(gp_ref[:, D:].astype(F32))
        da = (dm * gc).astype(BF)
        db = (dm * ga).astype(BF)
        da_ref[...] = da
        db_ref[...] = db
        dgp_ref[:, :D] = (dm * a_ref[...].astype(F32) * gc * (1.0 - gc)).astype(BF)
        dgp_ref[:, D:] = (dm * b_ref[...].astype(F32) * ga * (1.0 - ga)).astype(BF)
        dzs_ref[...] = _dot_nt(da, wcp_ref[...])
        dot_ref[...] = _dot_nt(wo_ref[...], db).astype(BF)

    return _call(
        body, deps, (dx, a, b, gp, w_cp, w_o, w_out), grid=(t // tm,),
        in_specs=[_row_tile(tm, D), _row_tile(tm, D), _row_tile(tm, D), _row_tile(tm, 2 * D),
                  _resident((D, D)), _resident((D, D)), _resident((D, D))],
        out_specs=[_row_tile(tm, D), pl.BlockSpec((D, tm), lambda i: (0, i)), _row_tile(tm, 2 * D),
                   _row_tile(tm, D), _row_tile(tm, D), _row_tile(tm, D)],
        out_shape=[jax.ShapeDtypeStruct((t, D), F32), jax.ShapeDtypeStruct((D, t), BF), jax.ShapeDtypeStruct((t, 2 * D), BF),
                   jax.ShapeDtypeStruct((t, D), BF), jax.ShapeDtypeStruct((t, D), BF), jax.ShapeDtypeStruct((t, D), BF)],
        compiler_params=_params(1), name="mix_out_bwd")


def _mix_proj_bwd(dxo, duc, dq_t, dkv_t, dgp, x, g, w_t):
    t = x.shape[0]
    tm = min(ROW_TILE_WIDE, t)

    def body(dxo_ref, duc_ref, dq_ref, dkv_ref, dgp_ref, x_ref, g_ref, w_ref, dx_ref, dg_ref):
        dn = _dot(duc_ref[...], w_ref[R_CONV[0]:R_CONV[1], :])
        dn = dn + _dot(dgp_ref[...], w_ref[R_GATE[0]:R_GATE[1], :])
        dn = dn + _dot_tn(dq_ref[...], w_ref[R_Q[0]:R_Q[1], :])
        dn = dn + _dot_tn(dkv_ref[...], w_ref[R_KV[0]:R_KV[1], :])
        dx, dg = _rms_bwd(dn, x_ref[...], g_ref[...])
        dx_ref[...] = dxo_ref[...] + dx

        @pl.when(pl.program_id(0) == 0)
        def _():
            dg_ref[...] = jnp.zeros_like(dg_ref)

        dg_ref[...] += dg

    return pl.pallas_call(
        body, grid=(t // tm,),
        in_specs=[_row_tile(tm, D), _row_tile(tm, 2 * D), pl.BlockSpec((D, tm), lambda i: (0, i)),
                  pl.BlockSpec((2 * NKV * HD, tm), lambda i: (0, i)), _row_tile(tm, 2 * D), _row_tile(tm, D),
                  _resident((1, D)), _resident((INW, D))],
        out_specs=[_row_tile(tm, D), pl.BlockSpec((1, D), lambda i: (0, 0))],
        out_shape=[jax.ShapeDtypeStruct((t, D), F32), jax.ShapeDtypeStruct((1, D), F32)],
        compiler_params=_params(1), name="mix_proj_bwd")(dxo, duc, dq_t, dkv_t, dgp, x, g, w_t)


def _attention_tables():
    kj = np.arange(2 * BLK)[:, None]
    qi = np.arange(BLK)[None, :]
    dist = qi + BLK - kj
    in_win = (dist >= 0) & (dist < BLK)
    dpos = np.maximum(dist, 0)
    max_exact = NBUCKET // 2
    dfl = np.maximum(dpos, 1).astype(np.float32)
    large = max_exact + (np.log(dfl / np.float32(max_exact)) / np.float32(math.log(BLK / max_exact))
                         * np.float32(NBUCKET - max_exact)).astype(np.int32)
    large = np.minimum(large, NBUCKET - 1)
    bucket = np.where(dpos < max_exact, dpos, large)
    onehot = (bucket[None] == np.arange(NBUCKET)[:, None, None]).astype(np.float32)
    mask = in_win.astype(np.float32)
    mask_first = mask * (kj >= BLK)
    masks = np.stack([np.tile(mask, (1, GRP)), np.tile(mask_first, (1, GRP))])
    return onehot, masks


def _bias_table(rel_bias, onehot):
    tab = jnp.einsum("bkq,bh->hkq", onehot, rel_bias, precision=lax.Precision.HIGHEST)
    tab = tab.reshape(NKV, GRP, 2 * BLK, BLK)
    return jnp.transpose(tab, (0, 2, 1, 3)).reshape(NKV, 2 * BLK, GRP * BLK)


def _local_step(x, target, vec, weights_of, wgrad, grads_done, small_done):
    onehot_np, masks_np = _attention_tables()
    onehot = jnp.asarray(onehot_np)
    masks = jnp.asarray(masks_np)
    bias_t = jnp.where(masks[:, None] > 0.5, _bias_table(vec["rel_bias"], onehot)[None], NEG)
    sink_rows = jnp.repeat(vec["attn_sinks"].reshape(NKV, 1, GRP), BLK, axis=2)
    qg = vec["q_norm"].reshape(HD, 1)
    kg = vec["k_norm"].reshape(HD, 1)
    g1 = vec["ffn1_norm"].reshape(1, D)
    gm = vec["mix_norm"].reshape(1, D)
    g2 = vec["ffn2_norm"].reshape(1, D)
    dwb = vec["conv_dw_bias"].reshape(1, D)
    lng = vec["conv_ln_g"].reshape(1, D)
    lnb = vec["conv_ln_b"].reshape(1, D)

    w1 = weights_of("ffn1", (bias_t, sink_rows))
    n1, u1, x1 = _ffn_fwd(x, g1, w1["ffn1_w_in"], w1["ffn1_w_out"], "ffn1_fwd")
    wm = weights_of("mix", (x1,))
    dwk = jnp.pad(wm["conv_dw_kernel"], ((0, CWP - CW), (0, 0)))
    hm, uc, gp, qkv_t = _mix_proj(x1, gm, wm["w_in"])
    zs, zc = _conv_fwd(uc, dwk, dwb, lng, lnb)
    o_t = _attn_fwd(qkv_t, qg, kg, sink_rows, bias_t)
    x2, a, b, merged = _mix_out(zs, o_t, gp, x1, wm["conv_w_proj"], wm["attn_w_o"], wm["w_out"])
    w2 = weights_of("ffn2", (x2,))
    n2, u2, dx3, sq = _ffn_fwd(x2, g2, w2["ffn2_w_in"], w2["ffn2_w_out"], "ffn2_fwd", target=target)

    gv = {}
    dx2, du2, h2, dy2, gv["ffn2_norm"] = _ffn_bwd(dx3, x2, g2, u2, w2["ffn2_w_in"], w2["ffn2_w_out"], "ffn2_bwd")
    deps = grads_done("ffn2", {"ffn2_w_in": wgrad(du2, n2, "ffn2_dw_in", False),
                               "ffn2_w_out": wgrad(h2, dy2, "ffn2_dw_out", False)})

    dzs, do_t, dgp, da, db, dx2b = _mix_out_bwd(dx2, a, b, gp, wm["conv_w_proj"], wm["attn_w_o"], wm["w_out"], deps=deps)
    deps = grads_done("mix_out", {"w_out": wgrad(merged, dx2b, "mix_dw_out", False),
                                  "conv_w_proj": wgrad(zs, da, "mix_dw_cp", False),
                                  "attn_w_o": wgrad(o_t, db, "mix_dw_o", True)})

    dq_t, ckv, dqg, dsink_rows, dsacc = _attn_bwd(qkv_t, do_t, qg, kg, sink_rows, bias_t, deps=deps)
    dkv_t, dkg = _kv_combine(ckv, qkv_t, kg)
    gv["q_norm"] = dqg.reshape(HD)
    gv["k_norm"] = dkg.reshape(HD)
    gv["attn_sinks"] = _sink_grad(dsink_rows)[:, :GRP].reshape(NQ)
    gv["rel_bias"] = _bias_grad(dsacc, onehot)[:, :, :GRP].reshape(NBUCKET, NQ)

    duc, dk_conv, gv["conv_dw_bias"], gv["conv_ln_g"], gv["conv_ln_b"] = _conv_bwd(uc, zc, dzs, dwk, lng, lnb)
    gv["conv_dw_kernel"] = dk_conv[:CW]

    dx1, gv["mix_norm"] = _mix_proj_bwd(dx2, duc, dq_t, dkv_t, dgp, x1, gm, wm["w_in"])
    deps = grads_done("mix_in", {"w_in": _wgrad_mix(duc, dq_t, dkv_t, dgp, hm)})

    dx0, du1, h1, dy1, gv["ffn1_norm"] = _ffn_bwd(dx1, x, g1, u1, w1["ffn1_w_in"], w1["ffn1_w_out"], "ffn1_bwd", deps=deps)
    for k in ("ffn1_norm", "mix_norm", "ffn2_norm", "conv_dw_bias", "conv_ln_g", "conv_ln_b"):
        gv[k] = gv[k].reshape(D)
    deps = small_done(gv, sq)
    deps = grads_done("ffn1_out", {"ffn1_w_out": wgrad(h1, dy1, "ffn1_dw_out", False, deps)})
    grads_done("ffn1_in", {"ffn1_w_in": wgrad(du1, n1, "ffn1_dw_in", False, deps)})
    return dx0


MESH_ID = pl.DeviceIdType.MESH


def _position():
    return lax.axis_index("x"), lax.axis_index("y"), lax.axis_index("c")


def _shard_rows(ref, index, rows):
    return ref.at[pl.ds(pl.multiple_of(index * rows, 16), rows), :]


def _prep(weights, taps, me):
    n = len(weights)

    def body(me_ref, *refs):
        for k in range(n):
            refs[n + 1 + k][...] = refs[k][...].astype(BF)
        refs[2 * n + 1][0:CW, :] = refs[n][...]
        refs[2 * n + 1][CW:, :] = jnp.zeros((CWP - CW, BLK), F32)

    shard_shapes = [w.shape for w in weights] + [(CWP, BLK)]
    dtypes = [BF] * n + [F32]
    ins = list(weights) + [taps]
    return pl.pallas_call(
        body,
        grid_spec=pltpu.PrefetchScalarGridSpec(
            num_scalar_prefetch=1, grid=(1,),
            in_specs=[pl.BlockSpec(a.shape, lambda i, m: (0, 0), pipeline_mode=pl.Buffered(1)) for a in ins],
            out_specs=[pl.BlockSpec(s, lambda i, m: (m[0], 0)) for s in shard_shapes]),
        out_shape=[jax.ShapeDtypeStruct((N_DEV * s[0], s[1]), d) for s, d in zip(shard_shapes, dtypes)],
        compiler_params=_params(1), name="prep")(me, *ins)


HBM = pl.BlockSpec(memory_space=pltpu.HBM)
SEM = pl.BlockSpec(memory_space=pltpu.SEMAPHORE)
DATAFLOW = pltpu.SideEffectType.DATAFLOW_SIDE_EFFECTING
TOKEN = jax.ShapeDtypeStruct((8, 128), F32)


def _in_hbm(x):
    return pltpu.with_memory_space_constraint(x, pltpu.HBM)


def _hbm_like(arrays):
    return [pltpu.HBM(a.shape, a.dtype) for a in arrays]


def _other_chips(x, y):
    return [(1 - x, y), (x, 1 - y), (1 - x, 1 - y)]


def _device_index(chip, c):
    return 4 * chip[0] + 2 * chip[1] + c


def _chip_index(chip):
    return 2 * chip[0] + chip[1]


class _Exchange:
    def __init__(self, gather, all_cores=False):
        self.gather = gather
        self.all_cores = all_cores
        self.n_peers = N_DEV - 1 if all_cores else 3

    def peers(self, x, y, c):
        if self.all_cores:
            return [(x ^ (k >> 2), y ^ ((k >> 1) & 1), c ^ (k & 1)) for k in range(1, N_DEV)]
        return [(*chip, c) for chip in _other_chips(x, y)]

    def sent(self, x, y, c, peer):
        return _device_index((x, y), c) if self.gather else _chip_index(peer[:2])

    def lands_at(self, x, y, c):
        return _device_index((x, y), c) if self.gather else _chip_index((x, y))

    def arrives_at(self, peer):
        return _device_index(peer[:2], peer[2]) if self.gather else _chip_index(peer[:2])


def _ici_copies_start(sets, sources, landings, exchanges, name, deps=()):
    n = len(landings)
    arrays = (list(sources) if sources is not None else []) + list(landings)
    first_land = len(arrays) - n
    n_sets = len(sets)
    n_deps = len(deps)

    def body(*refs):
        refs = refs[n_deps:]
        src, land = refs[:n], refs[first_land:first_land + n]
        sems = refs[len(arrays):len(arrays) + 2 * n_sets]
        token = refs[-1]
        x, y, c = _position()
        for s, (members, exchange) in enumerate(zip(sets, exchanges)):
            for slot, (k, rows) in enumerate(members):
                for j, peer in enumerate(exchange.peers(x, y, c)):
                    at = exchange.n_peers * slot + j
                    pltpu.make_async_remote_copy(
                        src_ref=_shard_rows(src[k], exchange.sent(x, y, c, peer), rows),
                        dst_ref=_shard_rows(land[k], exchange.lands_at(x, y, c), rows),
                        send_sem=sems[2 * s].at[at], recv_sem=sems[2 * s + 1].at[at],
                        device_id=peer, device_id_type=MESH_ID).start()
        token[...] = jnp.zeros_like(token)

    sem_shapes = []
    for members, exchange in zip(sets, exchanges):
        sem_shapes += [pltpu.SemaphoreType.DMA((exchange.n_peers * len(members),))] * 2
    out = pl.pallas_call(
        body, name=name,
        out_shape=sem_shapes + _hbm_like(arrays) + [TOKEN],
        in_specs=[ANY] * n_deps + [HBM] * len(arrays),
        out_specs=[SEM] * (2 * n_sets) + [HBM] * len(arrays) + [pl.BlockSpec(memory_space=pltpu.VMEM)],
        input_output_aliases={n_deps + i: 2 * n_sets + i for i in range(len(arrays))},
        compiler_params=pltpu.CompilerParams(has_side_effects=DATAFLOW),
    )(*deps, *[_in_hbm(a) for a in arrays])
    sems = [(out[2 * s], out[2 * s + 1]) for s in range(n_sets)]
    thru = list(out[2 * n_sets:2 * n_sets + len(arrays)])
    return sems, (thru[:first_land] if sources is not None else None), thru[first_land:], out[-1]


def _ici_copies_wait(sems, members, sources, landings, exchange, after, name):
    n = len(landings)
    arrays = (list(sources) if sources is not None else []) + list(landings)
    first_land = len(arrays) - n

    def body(*refs):
        src, land = refs[:n], refs[first_land:first_land + n]
        send_sems, recv_sems = refs[len(arrays)], refs[len(arrays) + 1]
        x, y, c = _position()
        for slot, rows in enumerate(members):
            for j, peer in enumerate(exchange.peers(x, y, c)):
                at = exchange.n_peers * slot + j
                cp = pltpu.make_async_remote_copy(
                    src_ref=_shard_rows(src[slot], exchange.sent(x, y, c, peer), rows),
                    dst_ref=_shard_rows(land[slot], exchange.arrives_at(peer), rows),
                    send_sem=send_sems.at[at], recv_sem=recv_sems.at[at], device_id=peer, device_id_type=MESH_ID)
                cp.wait_send()
                cp.wait_recv()

    out = pl.pallas_call(
        body, name=name, out_shape=_hbm_like(arrays),
        in_specs=[HBM] * len(arrays) + [SEM, SEM] + [ANY] * len(after), out_specs=[HBM] * len(arrays),
        input_output_aliases={i: i for i in range(len(arrays))},
        compiler_params=pltpu.CompilerParams(has_side_effects=DATAFLOW),
    )(*arrays, sems[0], sems[1], *after)
    return list(out[first_land:])


def _d2d_gather(buffers, rows, name):
    n = len(buffers)

    def body(*refs):
        land = refs[n:2 * n]
        send_sems, recv_sems = refs[2 * n:]
        x, y, c = _position()
        chips = [(x, y)] + _other_chips(x, y)
        sends, recvs = [], []
        for k in range(n):
            for j, chip in enumerate(chips):
                for copies, core in ((sends, c), (recvs, 1 - c)):
                    block = _shard_rows(land[k], _device_index(chip, core), rows[k])
                    copies.append(pltpu.make_async_remote_copy(
                        src_ref=block, dst_ref=block, send_sem=send_sems.at[k, j], recv_sem=recv_sems.at[k, j],
                        device_id=(x, y, 1 - c), device_id_type=MESH_ID))
        for cp in sends:
            cp.start()
        for cp in recvs:
            cp.wait_recv()
        for cp in sends:
            cp.wait_send()

    return pl.pallas_call(
        body, name=name, out_shape=[jax.ShapeDtypeStruct(a.shape, a.dtype) for a in buffers],
        in_specs=[ANY] * n, out_specs=[ANY] * n, input_output_aliases={i: i for i in range(n)},
        scratch_shapes=[pltpu.SemaphoreType.DMA((n, 4)), pltpu.SemaphoreType.DMA((n, 4))],
    )(*buffers)


def _rs_pair(grads, name):
    n = len(grads)
    rows = [g.shape[0] // N_DEV for g in grads]

    def body(*refs):
        ins, outs = refs[:n], refs[n:2 * n]
        send_sems, recv_sems = refs[2 * n:]
        x, y, c = _position()
        copies = []
        for k in range(n):
            for q in range(4):
                copies.append(pltpu.make_async_remote_copy(
                    src_ref=_shard_rows(ins[k], 2 * q + 1 - c, rows[k]), dst_ref=_shard_rows(outs[k], q, rows[k]),
                    send_sem=send_sems.at[k, q], recv_sem=recv_sems.at[k, q], device_id=(x, y, 1 - c),
                    device_id_type=MESH_ID))
        for cp in copies:
            cp.start()
        for cp in copies:
            cp.wait()

    return pl.pallas_call(
        body, out_shape=[jax.ShapeDtypeStruct((4 * r, g.shape[1]), g.dtype) for g, r in zip(grads, rows)],
        in_specs=[ANY] * n, out_specs=[ANY] * n,
        scratch_shapes=[pltpu.SemaphoreType.DMA((n, 4)), pltpu.SemaphoreType.DMA((n, 4))],
        name=name)(*grads)


def _wgrad_pair(lhs, rhs, name, *, lhs_is_transposed, deps=()):
    t = rhs.shape[0]
    n = lhs.shape[0] if lhs_is_transposed else lhs.shape[1]
    r = n // N_DEV
    n_chips = N_DEV // 2
    per = 1 if (2 * r) % BLK == 0 else 2
    steps = n_chips // per

    def body(l_ref, r_ref, kept_ref, recv_ref, res, send_sems, recv_sems):
        q = pl.program_id(0)
        slot = q % 2
        x, y, c = _position()

        def send(step, buf, i):
            return pltpu.make_async_remote_copy(
                src_ref=res.at[buf, pl.ds(pl.multiple_of((2 * i + 1 - c) * r, 16), r), :],
                dst_ref=_shard_rows(recv_ref, step * per + i, r),
                send_sem=send_sems.at[buf, i], recv_sem=recv_sems.at[step * per + i],
                device_id=(x, y, 1 - c), device_id_type=MESH_ID)

        @pl.when(q >= 2)
        def _():
            for i in range(per):
                send(q - 2, slot, i).wait_send()

        if lhs_is_transposed:
            res[slot] = _dot(l_ref[...], r_ref[...]).astype(BF)
        else:
            res[slot] = _dot_tn(l_ref[...], r_ref[...]).astype(BF)
        for i in range(per):
            kept_ref[i * r:(i + 1) * r, :] = res[slot, pl.ds(pl.multiple_of((2 * i + c) * r, 16), r), :]
            send(q, slot, i).start()

        @pl.when(q == steps - 1)
        def _():
            for i in range(per):
                if steps > 1:
                    send(q - 1, 1 - slot, i).wait_send()
                send(q, slot, i).wait_send()
            for chip in range(n_chips):
                send(chip // per, 0, chip % per).wait_recv()

    width = 2 * r * per
    lhs_spec = pl.BlockSpec((width, t), lambda q: (q, 0)) if lhs_is_transposed else pl.BlockSpec((t, width), lambda q: (0, q))
    return _call(
        body, deps, (lhs, rhs), grid=(steps,),
        in_specs=[lhs_spec, _resident((t, D))],
        out_specs=[pl.BlockSpec((per * r, D), lambda q: (q, 0)), ANY],
        out_shape=[jax.ShapeDtypeStruct((n // 2, D), BF)] * 2,
        scratch_shapes=[pltpu.VMEM((2, width, D), BF), pltpu.SemaphoreType.DMA((2, per)),
                        pltpu.SemaphoreType.DMA((n_chips,))],
        compiler_params=_params(1), name=name)


def _pair_add(grad, received, place, name, kept_only=False):
    r = received.shape[0] // 4
    tr = 352 if r % 352 == 0 else r
    per = r // tr
    parity = 0 if kept_only else 1

    def body(place_ref, g_ref, r_ref, o_ref, land_ref):
        total = (g_ref[...].astype(F32) + r_ref[...].astype(F32)).astype(BF)
        o_ref[...] = total

        @pl.when(pl.program_id(1) == place_ref[1])
        def _():
            land_ref[...] = total

    return pl.pallas_call(
        body,
        grid_spec=pltpu.PrefetchScalarGridSpec(
            num_scalar_prefetch=1, grid=(per, 4),
            in_specs=[pl.BlockSpec((tr, D), lambda i, q, p: (((1 + parity) * q + parity * p[0]) * per + i, 0)),
                      pl.BlockSpec((tr, D), lambda i, q, p: (q * per + i, 0))],
            out_specs=[pl.BlockSpec((tr, D), lambda i, q, p: (q * per + i, 0)),
                       pl.BlockSpec((tr, D), lambda i, q, p: (p[1] * per + i, 0))]),
        out_shape=[jax.ShapeDtypeStruct(received.shape, BF)] * 2,
        compiler_params=_params(2), name=name)(place, grad, received)


def _all_reduce_small(payload, deps=()):
    r = payload.shape[0]

    def body(in_ref, out_ref, land_ref, send_sems, recv_sems):
        x, y, c = _position()
        me = 4 * x + 2 * y + c
        land_ref[me] = in_ref[...]
        copies = []
        for k in range(1, N_DEV):
            peer = (x ^ (k >> 2), y ^ ((k >> 1) & 1), c ^ (k & 1))
            copies.append(pltpu.make_async_remote_copy(
                src_ref=in_ref, dst_ref=land_ref.at[me], send_sem=send_sems.at[k - 1], recv_sem=recv_sems.at[k - 1],
                device_id=peer, device_id_type=MESH_ID))
        for cp in copies:
            cp.start()
        for k in range(1, N_DEV):
            peer_index = me ^ k
            pltpu.make_async_remote_copy(
                src_ref=in_ref, dst_ref=land_ref.at[peer_index], send_sem=send_sems.at[k - 1], recv_sem=recv_sems.at[k - 1],
                device_id=(x, y, c), device_id_type=MESH_ID).wait_recv()
        for cp in copies:
            cp.wait_send()
        acc = land_ref[0]
        for d in range(1, N_DEV):
            acc = acc + land_ref[d]
        out_ref[...] = acc

    return _call(
        body, deps, (payload,), out_shape=jax.ShapeDtypeStruct((r, D), F32),
        in_specs=[pl.BlockSpec(memory_space=pltpu.VMEM)], out_specs=pl.BlockSpec(memory_space=pltpu.VMEM),
        scratch_shapes=[pltpu.VMEM((N_DEV, r, D), F32), pltpu.SemaphoreType.DMA((N_DEV - 1,)),
                        pltpu.SemaphoreType.DMA((N_DEV - 1,))],
        name="all_reduce_small")


def _adamw_math(w, g, m, v):
    m = ADAM_B1 * m + (1.0 - ADAM_B1) * g
    v = ADAM_B2 * v + (1.0 - ADAM_B2) * (g * g)
    m_hat = m / (1.0 - ADAM_B1 ** ADAM_STEP)
    v_hat = v / (1.0 - ADAM_B2 ** ADAM_STEP)
    delta = -ADAM_LR * (m_hat / (jnp.sqrt(v_hat) + ADAM_EPS) + ADAM_WD * w)
    return delta, m, v


def _sum_partials(blocks):
    g = blocks[0].astype(F32)
    for blk in blocks[1:]:
        g = g + blk.astype(F32)
    return g


def _reduce_adamw(landed, w, m, v, name):
    r = w.shape[0]
    tr = 352 if r % 352 == 0 else r
    per = r // tr

    def body(r0, r1, r2, r3, w_ref, m_ref, v_ref, g_ref, d_ref, nm_ref, nv_ref):
        g = _sum_partials([r0[...], r1[...], r2[...], r3[...]])
        g_ref[...] = g
        d_ref[...], nm_ref[...], nv_ref[...] = _adamw_math(w_ref[...], g, m_ref[...], v_ref[...])

    tile = _row_tile(tr, D)
    return pl.pallas_call(
        body, grid=(per,),
        in_specs=[pl.BlockSpec((tr, D), lambda i, q=q: (q * per + i, 0)) for q in range(4)] + [tile] * 3,
        out_specs=[tile] * 4, out_shape=[jax.ShapeDtypeStruct(w.shape, F32)] * 4,
        compiler_params=_params(1), name=name)(landed, landed, landed, landed, w, m, v)


def _adamw_small(w, g, m, v, name):
    def body(w_ref, g_ref, m_ref, v_ref, d_ref, nm_ref, nv_ref):
        d_ref[...], nm_ref[...], nv_ref[...] = _adamw_math(w_ref[...], g_ref[...], m_ref[...], v_ref[...])

    return pl.pallas_call(body, out_shape=[jax.ShapeDtypeStruct(w.shape, F32)] * 3, name=name)(w, g, m, v)


WEIGHTS = ("ffn1_norm", "ffn1_w_in", "ffn1_w_out", "mix_norm", "w_in", "conv_dw_kernel", "conv_dw_bias", "conv_ln_g",
           "conv_ln_b", "conv_w_proj", "q_norm", "k_norm", "attn_sinks", "rel_bias", "attn_w_o", "w_out", "ffn2_norm",
           "ffn2_w_in", "ffn2_w_out")
MATRICES = ("ffn1_w_in", "ffn1_w_out", "w_in", "conv_w_proj", "attn_w_o", "w_out", "ffn2_w_in", "ffn2_w_out")
COLUMN_SHARDED = ("ffn1_w_in", "w_in", "ffn2_w_in")
ROW_VECTORS = ("ffn1_norm", "mix_norm", "conv_dw_bias", "conv_ln_g", "conv_ln_b", "ffn2_norm")
PACKED = (("q_norm", HD), ("k_norm", HD), ("attn_sinks", NQ), ("rel_bias", NBUCKET * NQ))
GATHER = _Exchange(gather=True)
GATHER_ALL = _Exchange(gather=True, all_cores=True)
SCATTER = _Exchange(gather=False)
GATHER_STAGES = ("ffn1", "mix", "ffn2")
STAGE_GATHER = {"ffn1": GATHER, "mix": GATHER, "ffn2": GATHER_ALL}
STAGE_MEMBERS = {"ffn1": ("ffn1_w_in", "ffn1_w_out"), "mix": ("w_in", "conv_w_proj", "attn_w_o", "w_out", "taps"),
                 "ffn2": ("ffn2_w_in", "ffn2_w_out")}
ROW_PACKED = len(ROW_VECTORS)
ROW_LOSS = ROW_PACKED + 1
ROW_TAPS = 8
PAYLOAD_ROWS = ROW_TAPS + CWP


def _pack_small(values, last_row):
    packed = jnp.concatenate([values[k].reshape(-1) for k, _ in PACKED])
    packed = jnp.pad(packed, (0, D - packed.shape[0])).reshape(1, D)
    return jnp.concatenate([values[k].reshape(1, D) for k in ROW_VECTORS] + [packed, last_row], axis=0)


def _unpack_small(rows):
    out = {k: rows[i] for i, k in enumerate(ROW_VECTORS)}
    at = 0
    for k, size in PACKED:
        out[k] = rows[ROW_PACKED, at:at + size]
        at += size
    out["rel_bias"] = out["rel_bias"].reshape(NBUCKET, NQ)
    return out


def kernel(x, ffn1_norm, ffn1_w_in, ffn1_w_out, mix_norm, w_in, conv_dw_kernel, conv_dw_bias, conv_ln_g, conv_ln_b, conv_w_proj, q_norm, k_norm, attn_sinks, rel_bias, attn_w_o, w_out, ffn2_norm, ffn2_w_in, ffn2_w_out, loss_target, m_ffn1_norm, m_ffn1_w_in, m_ffn1_w_out, m_mix_norm, m_w_in, m_conv_dw_kernel, m_conv_dw_bias, m_conv_ln_g, m_conv_ln_b, m_conv_w_proj, m_q_norm, m_k_norm, m_attn_sinks, m_rel_bias, m_attn_w_o, m_w_out, m_ffn2_norm, m_ffn2_w_in, m_ffn2_w_out, v_ffn1_norm, v_ffn1_w_in, v_ffn1_w_out, v_mix_norm, v_w_in, v_conv_dw_kernel, v_conv_dw_bias, v_conv_ln_g, v_conv_ln_b, v_conv_w_proj, v_q_norm, v_k_norm, v_attn_sinks, v_rel_bias, v_attn_w_o, v_w_out, v_ffn2_norm, v_ffn2_w_in, v_ffn2_w_out):
    w = dict(ffn1_norm=ffn1_norm, ffn1_w_in=ffn1_w_in, ffn1_w_out=ffn1_w_out, mix_norm=mix_norm, w_in=w_in,
             conv_dw_kernel=conv_dw_kernel, conv_dw_bias=conv_dw_bias, conv_ln_g=conv_ln_g, conv_ln_b=conv_ln_b,
             conv_w_proj=conv_w_proj, q_norm=q_norm, k_norm=k_norm, attn_sinks=attn_sinks, rel_bias=rel_bias,
             attn_w_o=attn_w_o, w_out=w_out, ffn2_norm=ffn2_norm, ffn2_w_in=ffn2_w_in, ffn2_w_out=ffn2_w_out)
    m = dict(ffn1_norm=m_ffn1_norm, ffn1_w_in=m_ffn1_w_in, ffn1_w_out=m_ffn1_w_out, mix_norm=m_mix_norm, w_in=m_w_in,
             conv_dw_kernel=m_conv_dw_kernel, conv_dw_bias=m_conv_dw_bias, conv_ln_g=m_conv_ln_g, conv_ln_b=m_conv_ln_b,
             conv_w_proj=m_conv_w_proj, q_norm=m_q_norm, k_norm=m_k_norm, attn_sinks=m_attn_sinks, rel_bias=m_rel_bias,
             attn_w_o=m_attn_w_o, w_out=m_w_out, ffn2_norm=m_ffn2_norm, ffn2_w_in=m_ffn2_w_in, ffn2_w_out=m_ffn2_w_out)
    v = dict(ffn1_norm=v_ffn1_norm, ffn1_w_in=v_ffn1_w_in, ffn1_w_out=v_ffn1_w_out, mix_norm=v_mix_norm, w_in=v_w_in,
             conv_dw_kernel=v_conv_dw_kernel, conv_dw_bias=v_conv_dw_bias, conv_ln_g=v_conv_ln_g, conv_ln_b=v_conv_ln_b,
             conv_w_proj=v_conv_w_proj, q_norm=v_q_norm, k_norm=v_k_norm, attn_sinks=v_attn_sinks, rel_bias=v_rel_bias,
             attn_w_o=v_attn_w_o, w_out=v_w_out, ffn2_norm=v_ffn2_norm, ffn2_w_in=v_ffn2_w_in, ffn2_w_out=v_ffn2_w_out)
    px, py, pc = _position()
    me = 4 * px + 2 * py + pc
    place = jnp.stack([pc, 2 * px + py]).astype(jnp.int32)

    rows_of = lambda k, a: a.T if k in COLUMN_SHARDED else a
    buffers = dict(zip(MATRICES + ("taps",), _prep([rows_of(k, w[k]) for k in MATRICES], conv_dw_kernel,
                                                   me.astype(jnp.int32).reshape(1))))
    landings, sets = [], []
    for stage in GATHER_STAGES:
        sets.append([(len(landings) + i, buffers[k].shape[0] // N_DEV) for i, k in enumerate(STAGE_MEMBERS[stage])])
        landings += [buffers[k] for k in STAGE_MEMBERS[stage]]
    sems, _, land_thru, _ = _ici_copies_start(sets, None, landings, [STAGE_GATHER[s] for s in GATHER_STAGES],
                                              "gather_start")

    def weights_of(stage, after):
        s = GATHER_STAGES.index(stage)
        rows = [r for _, r in sets[s]]
        landed = _ici_copies_wait(sems[s], rows, None, [land_thru[k] for k, _ in sets[s]], STAGE_GATHER[stage],
                                  list(after), "gather_wait_" + stage)
        if not STAGE_GATHER[stage].all_cores:
            landed = _d2d_gather(landed, rows, "gather_d2d_" + stage)
        out = dict(zip(STAGE_MEMBERS[stage], landed))
        if "taps" in out:
            taps = out.pop("taps")
            out["conv_dw_kernel"] = jnp.transpose(taps.reshape(N_DEV, CWP, BLK), (1, 0, 2)).reshape(CWP, D)[:CW]
        return out

    in_flight = []

    def wgrad(lhs, rhs, name, lhs_is_transposed, deps=()):
        return _wgrad_pair(lhs, rhs, name, lhs_is_transposed=lhs_is_transposed, deps=deps)

    def grads_done(stage, grads):
        names = list(grads)
        added = []
        for k in names:
            if isinstance(grads[k], (tuple, list)):
                kept, received = grads[k]
                added.append(_pair_add(kept, received, place, "pair_add_" + k, kept_only=True))
            else:
                received, = _rs_pair([grads[k]], "rs_pair_" + k)
                added.append(_pair_add(grads[k], received, place, "pair_add_" + k))
        partials = [p for p, _ in added]
        members = [(i, p.shape[0] // 4) for i, p in enumerate(partials)]
        sem, p_thru, l_thru, token = _ici_copies_start([members], partials, [l for _, l in added], [SCATTER],
                                                       "scatter_start_" + stage)
        in_flight.append((stage, names, sem[0], p_thru, l_thru, token))
        return [token]

    reduced = []

    def small_done(gv, sq):
        payload = jnp.concatenate([_pack_small(gv, sq), jnp.pad(gv["conv_dw_kernel"], ((0, CWP - CW), (0, 0)))], axis=0)
        reduced.append(_all_reduce_small(payload))
        return reduced

    vec = {k: w[k] for k in WEIGHTS if k not in MATRICES and k != "conv_dw_kernel"}
    dx0 = _local_step(x[0], loss_target[0], vec, weights_of, wgrad, grads_done, small_done)
    total = reduced[0]
    loss = (0.5 / D) * jnp.sum(total[ROW_LOSS])

    grads, delta, new_m, new_v = {}, {}, {}, {}
    after = [in_flight[-1][-1]]
    for stage, names, sem, p_thru, l_thru, _ in in_flight:
        landed = _ici_copies_wait(sem, [p.shape[0] // 4 for p in p_thru], p_thru, l_thru, SCATTER, after,
                                  "scatter_wait_" + stage)
        after = []
        for k, buf in zip(names, landed):
            out = _reduce_adamw(buf, rows_of(k, w[k]), rows_of(k, m[k]), rows_of(k, v[k]), "adamw_" + k)
            grads[k], delta[k], new_m[k], new_v[k] = [rows_of(k, a) for a in out]
            after.append(out[1])
    zero_row = jnp.zeros((1, D), F32)
    d8, m8, v8 = _adamw_small(_pack_small(w, zero_row), total[:ROW_TAPS], _pack_small(m, zero_row),
                              _pack_small(v, zero_row), "adamw_small")
    grads.update(_unpack_small(total[:ROW_TAPS]))
    delta.update(_unpack_small(d8))
    new_m.update(_unpack_small(m8))
    new_v.update(_unpack_small(v8))
    k = "conv_dw_kernel"
    grads[k] = lax.dynamic_slice_in_dim(total[ROW_TAPS:ROW_TAPS + CW], me * BLK, BLK, axis=1)
    delta[k], new_m[k], new_v[k] = _adamw_small(w[k], grads[k], m[k], v[k], "adamw_taps")

    return (loss, dx0[None], *[grads[k] for k in WEIGHTS], *[delta[k] for k in WEIGHTS],
            *[new_m[k] for k in WEIGHTS], *[new_v[k] for k in WEIGHTS])
```

```python
import functools
import math

import numpy as np
import jax
import jax.numpy as jnp
from jax import lax
from jax.experimental import pallas as pl
from jax.experimental.pallas import tpu as pltpu

F32 = jnp.float32
BF = jnp.bfloat16

D = 1024
F = 2816
INW = 5632
CW = 31
CWP = 32
HD = 64
NQ = 16
NKV = 4
GRP = NQ // NKV
BLK = 128
NBUCKET = 32
EPS = 1e-6
NEG = float(jnp.finfo(jnp.float32).min)
QK_SCALE = 1.0 / math.sqrt(HD)
R_CONV = (0, 2048)
R_QKV = (2048, 3584)
R_Q = (2048, 3072)
R_KV = (3072, 3584)
R_GATE = (3584, 5632)

N_DEV = 8
VMEM_LIMIT_V7X = 56 * 1024 * 1024
ROW_TILE = 256
ROW_TILE_WIDE = 512

ADAM_LR = 0.001
ADAM_B1 = 0.9
ADAM_B2 = 0.999
ADAM_EPS = 1e-08
ADAM_WD = 0.01
ADAM_STEP = 10

NT_DIMS = (((1,), (1,)), ((), ()))
TN_DIMS = (((0,), (0,)), ((), ()))


def _dot(a, b):
    return jnp.dot(a, b, preferred_element_type=F32)


def _dot_nt(a, b):
    return lax.dot_general(a, b, NT_DIMS, preferred_element_type=F32)


def _dot_tn(a, b):
    return lax.dot_general(a, b, TN_DIMS, preferred_element_type=F32)


def _sig(x):
    return 0.5 * jnp.tanh(0.5 * x) + 0.5


ANY = pl.BlockSpec(memory_space=pl.ANY)


def _call(body, deps, args, **kw):
    n = len(deps)
    if n:
        kw["in_specs"] = [ANY] * n + list(kw["in_specs"])
        return pl.pallas_call(lambda *refs: body(*refs[n:]), **kw)(*deps, *args)
    return pl.pallas_call(body, **kw)(*args)


def _params(n_axes):
    return pltpu.CompilerParams(dimension_semantics=("arbitrary",) * n_axes, vmem_limit_bytes=VMEM_LIMIT_V7X)


def _resident(shape):
    zeros = (0,) * len(shape)
    return pl.BlockSpec(shape, lambda *_: zeros, pipeline_mode=pl.Buffered(1))


def _row_tile(rows, cols):
    return pl.BlockSpec((rows, cols), lambda i: (i, 0))


def _rms_stats(x):
    r = lax.rsqrt(jnp.mean(x * x, axis=-1, keepdims=True) + EPS)
    return r, x * r


def _rms_bwd(dn, x, g):
    r, xh = _rms_stats(x)
    dxh = dn * g
    dx = r * (dxh - xh * jnp.mean(dxh * xh, axis=-1, keepdims=True))
    return dx, jnp.sum(dn * xh, axis=0, keepdims=True)


def _ffn_fwd(x, g, w_in_t, w_out, name, target=None):
    t = x.shape[0]
    tm = min(ROW_TILE_WIDE, t)
    with_loss = target is not None

    def body(*refs):
        if with_loss:
            x_ref, g_ref, w_ref, wo_ref, t_ref, n_ref, u_ref, dy_ref, sq_ref = refs
        else:
            x_ref, g_ref, w_ref, wo_ref, n_ref, u_ref, xo_ref = refs
        x = x_ref[...]
        r, xh = _rms_stats(x)
        n = (xh * g_ref[...]).astype(BF)
        n_ref[...] = n
        u = _dot_nt(n, w_ref[...])
        u_ref[...] = u.astype(BF)
        a = u[:, :F]
        b = u[:, F:]
        h = (a * _sig(a) * b).astype(BF)
        xo = x + 0.5 * _dot(h, wo_ref[...])
        if with_loss:
            err = xo - t_ref[...]
            dy_ref[...] = err * (1.0 / D)

            @pl.when(pl.program_id(0) == 0)
            def _():
                sq_ref[...] = jnp.zeros_like(sq_ref)

            sq_ref[...] += jnp.sum(err * err, axis=0, keepdims=True)
        else:
            xo_ref[...] = xo

    in_specs = [_row_tile(tm, D), _resident((1, D)), _resident((INW, D)), _resident((F, D))]
    args = [x, g, w_in_t, w_out]
    out_specs = [_row_tile(tm, D), _row_tile(tm, INW), _row_tile(tm, D)]
    out_shape = [jax.ShapeDtypeStruct((t, D), BF), jax.ShapeDtypeStruct((t, INW), BF), jax.ShapeDtypeStruct((t, D), F32)]
    if with_loss:
        in_specs.append(_row_tile(tm, D))
        args.append(target)
        out_specs.append(pl.BlockSpec((1, D), lambda i: (0, 0)))
        out_shape.append(jax.ShapeDtypeStruct((1, D), F32))
    return pl.pallas_call(body, grid=(t // tm,), in_specs=in_specs, out_specs=out_specs, out_shape=out_shape,
                          compiler_params=_params(1), name=name)(*args)


def _ffn_up(x, g, w_in_t, name):
    t = x.shape[0]
    tm = min(ROW_TILE_WIDE, t)

    def body(x_ref, g_ref, w_ref, n_ref, u_ref):
        r, xh = _rms_stats(x_ref[...])
        n = (xh * g_ref[...]).astype(BF)
        n_ref[...] = n
        u_ref[...] = _dot_nt(n, w_ref[...]).astype(BF)

    return pl.pallas_call(
        body, grid=(t // tm,), in_specs=[_row_tile(tm, D), _resident((1, D)), _resident((INW, D))],
        out_specs=[_row_tile(tm, D), _row_tile(tm, INW)],
        out_shape=[jax.ShapeDtypeStruct((t, D), BF), jax.ShapeDtypeStruct((t, INW), BF)],
        compiler_params=_params(1), name=name)(x, g, w_in_t)


def _ffn_down(x, u, w_out, name):
    t = x.shape[0]
    tm = min(ROW_TILE_WIDE, t)

    def body(x_ref, u_ref, wo_ref, xo_ref):
        a = u_ref[:, :F].astype(F32)
        b = u_ref[:, F:].astype(F32)
        h = (a * _sig(a) * b).astype(BF)
        xo_ref[...] = x_ref[...] + 0.5 * _dot(h, wo_ref[...])

    return pl.pallas_call(
        body, grid=(t // tm,), in_specs=[_row_tile(tm, D), _row_tile(tm, INW), _resident((F, D))],
        out_specs=_row_tile(tm, D), out_shape=jax.ShapeDtypeStruct((t, D), F32),
        compiler_params=_params(1), name=name)(x, u, w_out)


def _ffn_bwd(dxo, x, g, u, w_in_t, w_out, name, deps=()):
    t = x.shape[0]
    tm = min(ROW_TILE, t)

    def body(dxo_ref, x_ref, g_ref, u_ref, w_ref, wo_ref, dx_ref, du_ref, h_ref, dy_ref, dg_ref):
        dxo = dxo_ref[...]
        dy = (0.5 * dxo).astype(BF)
        dy_ref[...] = dy
        dh = _dot_nt(dy, wo_ref[...])
        a = u_ref[:, :F].astype(F32)
        b = u_ref[:, F:].astype(F32)
        s = _sig(a)
        sa = a * s
        h_ref[...] = (sa * b).astype(BF)
        du_ref[:, :F] = (dh * b * (s * (1.0 + a * (1.0 - s)))).astype(BF)
        du_ref[:, F:] = (dh * sa).astype(BF)
        dn = _dot(du_ref[...], w_ref[...])
        dx, dg = _rms_bwd(dn, x_ref[...], g_ref[...])
        dx_ref[...] = dxo + dx

        @pl.when(pl.program_id(0) == 0)
        def _():
            dg_ref[...] = jnp.zeros_like(dg_ref)

        dg_ref[...] += dg

    return _call(
        body, deps, (dxo, x, g, u, w_in_t, w_out), grid=(t // tm,),
        in_specs=[_row_tile(tm, D), _row_tile(tm, D), _resident((1, D)), _row_tile(tm, INW), _resident((INW, D)),
                  _resident((F, D))],
        out_specs=[_row_tile(tm, D), _row_tile(tm, INW), _row_tile(tm, F), _row_tile(tm, D),
                   pl.BlockSpec((1, D), lambda i: (0, 0))],
        out_shape=[jax.ShapeDtypeStruct((t, D), F32), jax.ShapeDtypeStruct((t, INW), BF), jax.ShapeDtypeStruct((t, F), BF),
                   jax.ShapeDtypeStruct((t, D), BF), jax.ShapeDtypeStruct((1, D), F32)],
        compiler_params=_params(1), name=name)


def _wgrad(lhs, rhs, name, *, lhs_is_transposed, chunk, deps=()):
    t = rhs.shape[0]
    n = lhs.shape[0] if lhs_is_transposed else lhs.shape[1]
    c = min(chunk, n)

    def body(l_ref, r_ref, o_ref):
        if lhs_is_transposed:
            o_ref[...] = _dot(l_ref[...], r_ref[...]).astype(BF)
        else:
            o_ref[...] = _dot_tn(l_ref[...], r_ref[...]).astype(BF)

    lhs_spec = pl.BlockSpec((c, t), lambda j: (j, 0)) if lhs_is_transposed else pl.BlockSpec((t, c), lambda j: (0, j))
    return _call(
        body, deps, (lhs, rhs), grid=(n // c,),
        in_specs=[lhs_spec, _resident((t, D))],
        out_specs=pl.BlockSpec((c, D), lambda j: (j, 0)),
        out_shape=jax.ShapeDtypeStruct((n, D), BF),
        compiler_params=_params(1), name=name)


def _wgrad_mix(duc, dq_t, dkv_t, dgp, hm):
    t = hm.shape[0]
    c = 512
    first_q, first_kv, first_gate = R_Q[0] // c, R_KV[0] // c, R_GATE[0] // c

    def body(uc_ref, q_ref, kv_ref, gp_ref, h_ref, o_ref):
        j = pl.program_id(0)

        @pl.when(j < first_q)
        def _():
            o_ref[...] = _dot_tn(uc_ref[...], h_ref[...]).astype(BF)

        @pl.when((j >= first_q) & (j < first_kv))
        def _():
            o_ref[...] = _dot(q_ref[...], h_ref[...]).astype(BF)

        @pl.when((j >= first_kv) & (j < first_gate))
        def _():
            o_ref[...] = _dot(kv_ref[...], h_ref[...]).astype(BF)

        @pl.when(j >= first_gate)
        def _():
            o_ref[...] = _dot_tn(gp_ref[...], h_ref[...]).astype(BF)

    return pl.pallas_call(
        body, grid=(INW // c,),
        in_specs=[pl.BlockSpec((t, c), lambda j: (0, jnp.clip(j, 0, first_q - 1))),
                  pl.BlockSpec((c, t), lambda j: (jnp.clip(j - first_q, 0, first_kv - first_q - 1), 0)),
                  pl.BlockSpec((c, t), lambda j: (jnp.clip(j - first_kv, 0, first_gate - first_kv - 1), 0)),
                  pl.BlockSpec((t, c), lambda j: (0, jnp.clip(j - first_gate, 0, INW // c - first_gate - 1))),
                  _resident((t, D))],
        out_specs=pl.BlockSpec((c, D), lambda j: (j, 0)),
        out_shape=jax.ShapeDtypeStruct((INW, D), BF),
        compiler_params=_params(1), name="mix_dw_in")(duc, dq_t, dkv_t, dgp, hm)


def _mix_proj(x, g, w_t):
    t = x.shape[0]
    tm = min(ROW_TILE_WIDE, t)

    def body(x_ref, g_ref, w_ref, hm_ref, uc_ref, gp_ref, qkv_ref):
        r, xh = _rms_stats(x_ref[...])
        hm = (xh * g_ref[...]).astype(BF)
        hm_ref[...] = hm
        uc_ref[...] = _dot_nt(hm, w_ref[R_CONV[0]:R_CONV[1], :]).astype(BF)
        gp_ref[...] = _dot_nt(hm, w_ref[R_GATE[0]:R_GATE[1], :]).astype(BF)
        qkv_ref[...] = _dot_nt(w_ref[R_QKV[0]:R_QKV[1], :], hm).astype(BF)

    return pl.pallas_call(
        body, grid=(t // tm,),
        in_specs=[_row_tile(tm, D), _resident((1, D)), _resident((INW, D))],
        out_specs=[_row_tile(tm, D), _row_tile(tm, 2 * D), _row_tile(tm, 2 * D), pl.BlockSpec((1536, tm), lambda i: (0, i))],
        out_shape=[jax.ShapeDtypeStruct((t, D), BF), jax.ShapeDtypeStruct((t, 2 * D), BF),
                   jax.ShapeDtypeStruct((t, 2 * D), BF), jax.ShapeDtypeStruct((1536, t), BF)],
        compiler_params=_params(1), name="mix_proj")(x, g, w_t)


CONV_HALO = 32
CONV_LEAD = CONV_HALO - (CW - 1)


def _glu(uc):
    uc = uc.astype(F32)
    return uc[:, :D] * _sig(uc[:, D:])


def _ln_stats(zc):
    mu = jnp.mean(zc, axis=-1, keepdims=True)
    zm = zc - mu
    r = lax.rsqrt(jnp.mean(zm * zm, axis=-1, keepdims=True) + EPS)
    return r, zm * r


CONV_SHIFTS = 8
CONV_CHUNK = 32


def _store_shifted(buf, rows):
    for b in range(1, CONV_SHIFTS):
        buf[b, 0:rows - 8, :] = buf[0, pl.ds(b, rows - 8), :]


def _conv_fwd(uc, dwk, dwb, lng, lnb):
    t = uc.shape[0]
    tm = min(512, t)
    per = tm // CONV_HALO
    ext = tm + CONV_HALO

    def body(cur_ref, prev_ref, k_ref, kb_ref, g_ref, b_ref, o_ref, zc_ref, zsh):
        i = pl.program_id(0)
        zsh[0, 0:CONV_HALO, :] = _glu(prev_ref[...]) * (i > 0).astype(F32)
        zsh[0, CONV_HALO:, :] = _glu(cur_ref[...])
        _store_shifted(zsh, ext)

        def chunk(ci, carry):
            r0 = pl.multiple_of(ci * CONV_CHUNK, CONV_CHUNK)
            acc = jnp.zeros((CONV_CHUNK, D), F32) + kb_ref[...]
            for w in range(CW):
                a, b = divmod(CONV_LEAD + w, 8)
                acc = acc + k_ref[w:w + 1, :] * zsh[b, pl.ds(r0 + 8 * a, CONV_CHUNK), :]
            zc_ref[pl.ds(r0, CONV_CHUNK), :] = acc
            return carry

        lax.fori_loop(0, tm // CONV_CHUNK, chunk, 0)
        r, xh = _ln_stats(zc_ref[...])
        y = xh * g_ref[...] + b_ref[...]
        o_ref[...] = (y * _sig(y)).astype(BF)

    return pl.pallas_call(
        body, grid=(t // tm,),
        in_specs=[_row_tile(tm, 2 * D),
                  pl.BlockSpec((CONV_HALO, 2 * D), lambda i: (jnp.maximum(i * per - 1, 0), 0)),
                  _resident((CWP, D)), _resident((1, D)), _resident((1, D)), _resident((1, D))],
        out_specs=[_row_tile(tm, D), _row_tile(tm, D)],
        out_shape=[jax.ShapeDtypeStruct((t, D), BF), jax.ShapeDtypeStruct((t, D), F32)],
        scratch_shapes=[pltpu.VMEM((CONV_SHIFTS, ext, D), F32)],
        compiler_params=_params(1), name="conv_fwd")(uc, uc, dwk, dwb, lng, lnb)


def _conv_bwd(uc, zc, dzs, dwk, lng, lnb):
    t = uc.shape[0]
    tm = min(ROW_TILE_WIDE, t)
    per = tm // CONV_HALO
    n_tiles = t // tm
    ext = tm + CONV_HALO
    last_block = t // CONV_HALO - 1

    def body(cur_ref, zc_ref, zcn_ref, dz_ref, dzn_ref, k_ref, g_ref, b_ref,
             duc_ref, dk_ref, dkb_ref, dg_ref, db_ref, dsh, dk8, z_scr):
        i = pl.program_id(0)

        @pl.when(i == 0)
        def _():
            dk8[...] = jnp.zeros_like(dk8)
            dkb_ref[...] = jnp.zeros_like(dkb_ref)
            dg_ref[...] = jnp.zeros_like(dg_ref)
            db_ref[...] = jnp.zeros_like(db_ref)

        has_next = (i < n_tiles - 1).astype(F32)
        z_scr[...] = _glu(cur_ref[...])
        gain = g_ref[...]

        def ln_silu_bwd(zc, dzs, live):
            r, xh = _ln_stats(zc)
            y = xh * gain + b_ref[...]
            sy = _sig(y)
            dy = dzs * (sy * (1.0 + y * (1.0 - sy))) * live
            dxh = dy * gain
            dzc = r * (dxh - jnp.mean(dxh, axis=-1, keepdims=True) - xh * jnp.mean(dxh * xh, axis=-1, keepdims=True))
            return dzc, dy, xh

        dzc, dy, xh = ln_silu_bwd(zc_ref[...], dz_ref[...], 1.0)
        dsh[0, 0:tm, :] = dzc
        dg_ref[...] += jnp.sum(dy * xh, axis=0, keepdims=True)
        db_ref[...] += jnp.sum(dy, axis=0, keepdims=True)
        dkb_ref[...] += jnp.sum(dzc, axis=0, keepdims=True)
        dsh[0, tm:, :] = ln_silu_bwd(zcn_ref[...], dzn_ref[...], has_next)[0]
        _store_shifted(dsh, ext)

        def chunk(ci, carry):
            r0 = pl.multiple_of(ci * CONV_CHUNK, CONV_CHUNK)
            z_c = z_scr[pl.ds(r0, CONV_CHUNK), :]
            dz = jnp.zeros((CONV_CHUNK, D), F32)
            for w in range(CW):
                a, b = divmod(CW - 1 - w, 8)
                window = dsh[b, pl.ds(r0 + 8 * a, CONV_CHUNK), :]
                dz = dz + k_ref[w:w + 1, :] * window
                prod = z_c * window
                part = prod[0:8, :]
                for j in range(1, CONV_CHUNK // 8):
                    part = part + prod[8 * j:8 * j + 8, :]
                dk8[w] += part
            ucc = cur_ref[pl.ds(r0, CONV_CHUNK), :].astype(F32)
            sg = _sig(ucc[:, D:])
            duc_ref[pl.ds(r0, CONV_CHUNK), 0:D] = (dz * sg).astype(BF)
            duc_ref[pl.ds(r0, CONV_CHUNK), D:2 * D] = (dz * ucc[:, :D] * sg * (1.0 - sg)).astype(BF)
            return carry

        lax.fori_loop(0, tm // CONV_CHUNK, chunk, 0)

        @pl.when(i == n_tiles - 1)
        def _():
            dk_ref[...] = jnp.sum(dk8[...], axis=1)

    vec = pl.BlockSpec((1, D), lambda i: (0, 0))
    next_halo = pl.BlockSpec((CONV_HALO, D), lambda i: (jnp.minimum((i + 1) * per, last_block), 0))
    return pl.pallas_call(
        body, grid=(n_tiles,),
        in_specs=[_row_tile(tm, 2 * D), _row_tile(tm, D), next_halo, _row_tile(tm, D), next_halo,
                  _resident((CWP, D)), _resident((1, D)), _resident((1, D))],
        out_specs=[_row_tile(tm, 2 * D), pl.BlockSpec((CWP, D), lambda i: (0, 0)), vec, vec, vec],
        out_shape=[jax.ShapeDtypeStruct((t, 2 * D), BF), jax.ShapeDtypeStruct((CWP, D), F32),
                   jax.ShapeDtypeStruct((1, D), F32), jax.ShapeDtypeStruct((1, D), F32), jax.ShapeDtypeStruct((1, D), F32)],
        scratch_shapes=[pltpu.VMEM((CONV_SHIFTS, ext, D), F32), pltpu.VMEM((CWP, 8, D), F32), pltpu.VMEM((tm, D), F32)],
        compiler_params=_params(1), name="conv_bwd")(uc, zc, zc, dzs, dzs, dwk, lng, lnb)


def _norm_rows(xt, g):
    r = lax.rsqrt(jnp.mean(xt * xt, axis=0, keepdims=True) + EPS)
    xh = xt * r
    return xh * g, r, xh


ATT_TQ = 512


def _attn_specs(t, tq):
    per = tq // BLK
    return [pl.BlockSpec((1536, tq), lambda i: (0, i)),
            pl.BlockSpec((512, BLK), lambda i: (2, jnp.maximum(i * per - 1, 0))),
            _resident((HD, 1)), _resident((HD, 1)), _resident((NKV, 1, GRP * BLK)),
            _resident((2, NKV, 2 * BLK, GRP * BLK))]


def _attn_window(hk, sb, qkv_ref, halo_ref, kn_cur, kn_halo):
    v0 = D + NKV * HD + hk * HD
    if sb == 0:
        k_prev = kn_halo[hk]
        v_prev = halo_ref[NKV * HD + hk * HD:NKV * HD + (hk + 1) * HD, :]
    else:
        k_prev = kn_cur[hk][:, (sb - 1) * BLK:sb * BLK]
        v_prev = qkv_ref[v0:v0 + HD, (sb - 1) * BLK:sb * BLK]
    kw = jnp.concatenate([k_prev, kn_cur[hk][:, sb * BLK:(sb + 1) * BLK]], axis=1).astype(BF)
    vw = jnp.concatenate([v_prev, qkv_ref[v0:v0 + HD, sb * BLK:(sb + 1) * BLK]], axis=1)
    return kw, vw


def _attn_probs(kw, qc, bias, sink):
    st = _dot_tn(kw, qc) + bias
    m = jnp.maximum(jnp.max(st, axis=0, keepdims=True), sink)
    p = jnp.exp(st - m)
    e_sink = jnp.exp(sink - m)
    inv = 1.0 / (jnp.sum(p, axis=0, keepdims=True) + e_sink)
    return p * inv, e_sink * inv


def _attn_fwd(qkv_t, qg, kg, sink_rows, bias_t):
    t = qkv_t.shape[1]
    tq = min(ATT_TQ, t)
    n_sub = tq // BLK

    def body(qkv_ref, halo_ref, qg_ref, kg_ref, sink_ref, bias_ref, o_ref):
        i = pl.program_id(0)
        first = (i == 0).astype(jnp.int32)
        kgain = kg_ref[...]
        qgain = qg_ref[...]
        kn_cur = [_norm_rows(qkv_ref[D + h * HD:D + (h + 1) * HD, :].astype(F32), kgain)[0] for h in range(NKV)]
        kn_halo = [_norm_rows(halo_ref[h * HD:(h + 1) * HD, :].astype(F32), kgain)[0] for h in range(NKV)]
        for hk in range(NKV):
            for sb in range(n_sub):
                cols = slice(sb * BLK, (sb + 1) * BLK)
                kw, vw = _attn_window(hk, sb, qkv_ref, halo_ref, kn_cur, kn_halo)
                qc = jnp.concatenate(
                    [_norm_rows(qkv_ref[(GRP * hk + g) * HD:(GRP * hk + g + 1) * HD, cols].astype(F32), qgain)[0] * QK_SCALE
                     for g in range(GRP)], axis=1).astype(BF)
                bias = bias_ref[first, hk] if sb == 0 else bias_ref[0, hk]
                p, _ = _attn_probs(kw, qc, bias, sink_ref[hk])
                o = _dot(vw, p.astype(BF))
                for g in range(GRP):
                    head = GRP * hk + g
                    o_ref[head * HD:(head + 1) * HD, cols] = o[:, g * BLK:(g + 1) * BLK].astype(BF)

    return pl.pallas_call(
        body, grid=(t // tq,),
        in_specs=_attn_specs(t, tq),
        out_specs=pl.BlockSpec((D, tq), lambda i: (0, i)),
        out_shape=jax.ShapeDtypeStruct((D, t), BF),
        compiler_params=_params(1), name="attn_fwd")(qkv_t, qkv_t, qg, kg, sink_rows, bias_t)


def _attn_bwd(qkv_t, do_t, qg, kg, sink_rows, bias_t, deps=()):
    t = qkv_t.shape[1]
    tq = min(ATT_TQ, t)
    n_sub = tq // BLK
    n_tiles = t // tq

    def body(qkv_ref, halo_ref, do_ref, qg_ref, kg_ref, sink_ref, bias_ref,
             dq_ref, ckv_ref, dqg_ref, dsink_ref, dsacc_ref, qg_scr):
        i = pl.program_id(0)

        @pl.when(i == 0)
        def _():
            qg_scr[...] = jnp.zeros_like(qg_scr)
            dsink_ref[...] = jnp.zeros_like(dsink_ref)
            dsacc_ref[...] = jnp.zeros_like(dsacc_ref)

        first = (i == 0).astype(jnp.int32)
        kgain = kg_ref[...]
        qgain = qg_ref[...]
        kn_cur = [_norm_rows(qkv_ref[D + h * HD:D + (h + 1) * HD, :].astype(F32), kgain)[0] for h in range(NKV)]
        kn_halo = [_norm_rows(halo_ref[h * HD:(h + 1) * HD, :].astype(F32), kgain)[0] for h in range(NKV)]
        dqg = jnp.zeros((HD, BLK), F32)
        for hk in range(NKV):
            for sb in range(n_sub):
                cols = slice(sb * BLK, (sb + 1) * BLK)
                kw, vw = _attn_window(hk, sb, qkv_ref, halo_ref, kn_cur, kn_halo)
                qn, qr, qh = [], [], []
                for g in range(GRP):
                    head = GRP * hk + g
                    n_, r_, h_ = _norm_rows(qkv_ref[head * HD:(head + 1) * HD, cols].astype(F32), qgain)
                    qn.append(n_)
                    qr.append(r_)
                    qh.append(h_)
                qc = (jnp.concatenate(qn, axis=1) * QK_SCALE).astype(BF)
                bias = bias_ref[first, hk] if sb == 0 else bias_ref[0, hk]
                p, p_sink = _attn_probs(kw, qc, bias, sink_ref[hk])
                doc = jnp.concatenate([do_ref[(GRP * hk + g) * HD:(GRP * hk + g + 1) * HD, cols] for g in range(GRP)], axis=1)
                dp = _dot_tn(vw, doc)
                delta = jnp.sum(p * dp, axis=0, keepdims=True)
                ds = p * (dp - delta)
                dsink_ref[hk] += -(p_sink * delta)
                dsacc_ref[hk] += ds
                dsb = ds.astype(BF)
                dqc = _dot(kw, dsb) * QK_SCALE
                ckv_ref[sb, hk * HD:(hk + 1) * HD, :] = _dot_nt(qc, dsb)
                ckv_ref[sb, NKV * HD + hk * HD:NKV * HD + (hk + 1) * HD, :] = _dot_nt(doc, p.astype(BF))
                for g in range(GRP):
                    head = GRP * hk + g
                    dqn = dqc[:, g * BLK:(g + 1) * BLK]
                    dqh = dqn * qgain
                    dq = qr[g] * (dqh - qh[g] * jnp.mean(dqh * qh[g], axis=0, keepdims=True))
                    dq_ref[head * HD:(head + 1) * HD, cols] = dq.astype(BF)
                    dqg = dqg + dqn * qh[g]
        qg_scr[...] += dqg

        @pl.when(i == n_tiles - 1)
        def _():
            dqg_ref[...] = jnp.sum(qg_scr[...], axis=1, keepdims=True)

    return _call(
        body, deps, (qkv_t, qkv_t, do_t, qg, kg, sink_rows, bias_t), grid=(n_tiles,),
        in_specs=_attn_specs(t, tq)[:2] + [pl.BlockSpec((D, tq), lambda i: (0, i))] + _attn_specs(t, tq)[2:],
        out_specs=[pl.BlockSpec((D, tq), lambda i: (0, i)),
                   pl.BlockSpec((n_sub, 2 * NKV * HD, 2 * BLK), lambda i: (i, 0, 0)),
                   pl.BlockSpec((HD, 1), lambda i: (0, 0)),
                   pl.BlockSpec((NKV, 1, GRP * BLK), lambda i: (0, 0, 0)),
                   pl.BlockSpec((NKV, 2 * BLK, GRP * BLK), lambda i: (0, 0, 0))],
        out_shape=[jax.ShapeDtypeStruct((D, t), BF),
                   jax.ShapeDtypeStruct((t // BLK, 2 * NKV * HD, 2 * BLK), F32),
                   jax.ShapeDtypeStruct((HD, 1), F32),
                   jax.ShapeDtypeStruct((NKV, 1, GRP * BLK), F32),
                   jax.ShapeDtypeStruct((NKV, 2 * BLK, GRP * BLK), F32)],
        scratch_shapes=[pltpu.VMEM((HD, BLK), F32)],
        compiler_params=_params(1), name="attn_bwd")


def _kv_combine(ckv, qkv_t, kg):
    nb = ckv.shape[0]
    t = nb * BLK
    rows = NKV * HD
    per = min(4, nb)
    steps = nb // per

    def body(c_ref, cn_ref, k_ref, kg_ref, o_ref, dkg_ref, kg_scr):
        n = pl.program_id(0)

        @pl.when(n == 0)
        def _():
            kg_scr[...] = jnp.zeros_like(kg_scr)

        has_next = (n < steps - 1).astype(F32)
        kgain = kg_ref[...]
        dkg = jnp.zeros((HD, BLK), F32)
        for s in range(per):
            cols = slice(s * BLK, (s + 1) * BLK)
            after = c_ref[s + 1, :, :BLK] if s + 1 < per else cn_ref[0, :, :BLK] * has_next
            d = c_ref[s, :, BLK:] + after
            o_ref[rows:, cols] = d[rows:, :].astype(BF)
            for h in range(NKV):
                _, r, kh = _norm_rows(k_ref[h * HD:(h + 1) * HD, cols].astype(F32), kgain)
                dkn = d[h * HD:(h + 1) * HD, :]
                dkh = dkn * kgain
                o_ref[h * HD:(h + 1) * HD, cols] = (r * (dkh - kh * jnp.mean(dkh * kh, axis=0, keepdims=True))).astype(BF)
                dkg = dkg + dkn * kh
        kg_scr[...] += dkg

        @pl.when(n == steps - 1)
        def _():
            dkg_ref[...] = jnp.sum(kg_scr[...], axis=1, keepdims=True)

    return pl.pallas_call(
        body, grid=(steps,),
        in_specs=[pl.BlockSpec((per, 2 * rows, 2 * BLK), lambda n: (n, 0, 0)),
                  pl.BlockSpec((1, 2 * rows, 2 * BLK), lambda n: (jnp.minimum((n + 1) * per, nb - 1), 0, 0)),
                  pl.BlockSpec((rows, per * BLK), lambda n: (D // rows, n)),
                  _resident((HD, 1))],
        out_specs=[pl.BlockSpec((2 * rows, per * BLK), lambda n: (0, n)), pl.BlockSpec((HD, 1), lambda n: (0, 0))],
        out_shape=[jax.ShapeDtypeStruct((2 * rows, t), BF), jax.ShapeDtypeStruct((HD, 1), F32)],
        scratch_shapes=[pltpu.VMEM((HD, BLK), F32)],
        compiler_params=_params(1), name="kv_combine")(ckv, ckv, qkv_t, kg)


def _group_lane_sums(v):
    lane_group = lax.broadcasted_iota(jnp.int32, (1, GRP * BLK), 1) // BLK
    col = lax.broadcasted_iota(jnp.int32, (1, BLK), 1)
    out = jnp.zeros((NKV, BLK), F32)
    for g in range(GRP):
        s = jnp.sum(jnp.where(lane_group == g, v, 0.0), axis=1, keepdims=True)
        out = jnp.where(col == g, s, out)
    return out


def _bias_grad(dsacc, onehot_t):
    per = 8

    def body(ds_ref, oh_ref, o_ref):
        for b in range(per):
            oh = jnp.concatenate([oh_ref[b]] * GRP, axis=1)
            o_ref[b] = _group_lane_sums(jnp.sum(ds_ref[...] * oh[None], axis=1))

    return pl.pallas_call(
        body, grid=(NBUCKET // per,),
        in_specs=[_resident((NKV, 2 * BLK, GRP * BLK)), pl.BlockSpec((per, 2 * BLK, BLK), lambda b: (b, 0, 0))],
        out_specs=pl.BlockSpec((per, NKV, BLK), lambda b: (b, 0, 0)),
        out_shape=jax.ShapeDtypeStruct((NBUCKET, NKV, BLK), F32),
        compiler_params=_params(1), name="bias_grad")(dsacc, onehot_t)


def _sink_grad(dsink_rows):
    def body(d_ref, o_ref):
        o_ref[...] = _group_lane_sums(d_ref[:, 0, :])

    return pl.pallas_call(body, out_shape=jax.ShapeDtypeStruct((NKV, BLK), F32), name="sink_grad")(dsink_rows)


def _mix_out(zs, o_t, gp, x, w_cp, w_o, w_out):
    t = x.shape[0]
    tm = min(ROW_TILE_WIDE, t)

    def body(zs_ref, ot_ref, gp_ref, x_ref, wcp_ref, wo_ref, wout_ref, xo_ref, a_ref, b_ref, m_ref):
        a = _dot(zs_ref[...], wcp_ref[...])
        b = _dot_tn(ot_ref[...], wo_ref[...])
        a_ref[...] = a.astype(BF)
        b_ref[...] = b.astype(BF)
        merged = (_sig(gp_ref[:, :D].astype(F32)) * a + _sig(gp_ref[:, D:].astype(F32)) * b).astype(BF)
        m_ref[...] = merged
        xo_ref[...] = x_ref[...] + _dot(merged, wout_ref[...])

    return pl.pallas_call(
        body, grid=(t // tm,),
        in_specs=[_row_tile(tm, D), pl.BlockSpec((D, tm), lambda i: (0, i)), _row_tile(tm, 2 * D), _row_tile(tm, D),
                  _resident((D, D)), _resident((D, D)), _resident((D, D))],
        out_specs=[_row_tile(tm, D)] * 4,
        out_shape=[jax.ShapeDtypeStruct((t, D), F32)] + [jax.ShapeDtypeStruct((t, D), BF)] * 3,
        compiler_params=_params(1), name="mix_out")(zs, o_t, gp, x, w_cp, w_o, w_out)


def _mix_out_bwd(dx, a, b, gp, w_cp, w_o, w_out, deps=()):
    t = dx.shape[0]
    tm = min(ROW_TILE_WIDE, t)

    def body(dx_ref, a_ref, b_ref, gp_ref, wcp_ref, wo_ref, wout_ref, dzs_ref, dot_ref, dgp_ref, da_ref, db_ref, dxb_ref):
        dxb = dx_ref[...].astype(BF)
        dxb_ref[...] = dxb
        dm = _dot_nt(dxb, wout_ref[...])
        gc = _sig(gp_ref[:, :D].astype(F32))
        ga = _sig(gp_ref[:, D:].astype(F32))
        da = (dm * gc).astype(BF)
        db = (dm * ga).astype(BF)
        da_ref[...] = da
        db_ref[...] = db
        dgp_ref[:, :D] = (dm * a_ref[...].astype(F32) * gc * (1.0 - gc)).astype(BF)
        dgp_ref[:, D:] = (dm * b_ref[...].astype(F32) * ga * (1.0 - ga)).astype(BF)
        dzs_ref[...] = _dot_nt(da, wcp_ref[...])
        dot_ref[...] = _dot_nt(wo_ref[...], db).astype(BF)

    return _call(
        body, deps, (dx, a, b, gp, w_cp, w_o, w_out), grid=(t // tm,),
        in_specs=[_row_tile(tm, D), _row_tile(tm, D), _row_tile(tm, D), _row_tile(tm, 2 * D),
                  _resident((D, D)), _resident((D, D)), _resident((D, D))],
        out_specs=[_row_tile(tm, D), pl.BlockSpec((D, tm), lambda i: (0, i)), _row_tile(tm, 2 * D),
                   _row_tile(tm, D), _row_tile(tm, D), _row_tile(tm, D)],
        out_shape=[jax.ShapeDtypeStruct((t, D), F32), jax.ShapeDtypeStruct((D, t), BF), jax.ShapeDtypeStruct((t, 2 * D), BF),
                   jax.ShapeDtypeStruct((t, D), BF), jax.ShapeDtypeStruct((t, D), BF), jax.ShapeDtypeStruct((t, D), BF)],
        compiler_params=_params(1), name="mix_out_bwd")


def _mix_proj_bwd(dxo, duc, dq_t, dkv_t, dgp, x, g, w_t):
    t = x.shape[0]
    tm = min(ROW_TILE_WIDE, t)

    def body(dxo_ref, duc_ref, dq_ref, dkv_ref, dgp_ref, x_ref, g_ref, w_ref, dx_ref, dg_ref):
        dn = _dot(duc_ref[...], w_ref[R_CONV[0]:R_CONV[1], :])
        dn = dn + _dot(dgp_ref[...], w_ref[R_GATE[0]:R_GATE[1], :])
        dn = dn + _dot_tn(dq_ref[...], w_ref[R_Q[0]:R_Q[1], :])
        dn = dn + _dot_tn(dkv_ref[...], w_ref[R_KV[0]:R_KV[1], :])
        dx, dg = _rms_bwd(dn, x_ref[...], g_ref[...])
        dx_ref[...] = dxo_ref[...] + dx

        @pl.when(pl.program_id(0) == 0)
        def _():
            dg_ref[...] = jnp.zeros_like(dg_ref)

        dg_ref[...] += dg

    return pl.pallas_call(
        body, grid=(t // tm,),
        in_specs=[_row_tile(tm, D), _row_tile(tm, 2 * D), pl.BlockSpec((D, tm), lambda i: (0, i)),
                  pl.BlockSpec((2 * NKV * HD, tm), lambda i: (0, i)), _row_tile(tm, 2 * D), _row_tile(tm, D),
                  _resident((1, D)), _resident((INW, D))],
        out_specs=[_row_tile(tm, D), pl.BlockSpec((1, D), lambda i: (0, 0))],
        out_shape=[jax.ShapeDtypeStruct((t, D), F32), jax.ShapeDtypeStruct((1, D), F32)],
        compiler_params=_params(1), name="mix_proj_bwd")(dxo, duc, dq_t, dkv_t, dgp, x, g, w_t)


def _attention_tables():
    kj = np.arange(2 * BLK)[:, None]
    qi = np.arange(BLK)[None, :]
    dist = qi + BLK - kj
    in_win = (dist >= 0) & (dist < BLK)
    dpos = np.maximum(dist, 0)
    max_exact = NBUCKET // 2
    dfl = np.maximum(dpos, 1).astype(np.float32)
    large = max_exact + (np.log(dfl / np.float32(max_exact)) / np.float32(math.log(BLK / max_exact))
                         * np.float32(NBUCKET - max_exact)).astype(np.int32)
    large = np.minimum(large, NBUCKET - 1)
    bucket = np.where(dpos < max_exact, dpos, large)
    onehot = (bucket[None] == np.arange(NBUCKET)[:, None, None]).astype(np.float32)
    mask = in_win.astype(np.float32)
    mask_first = mask * (kj >= BLK)
    masks = np.stack([np.tile(mask, (1, GRP)), np.tile(mask_first, (1, GRP))])
    return onehot, masks


def _bias_table(rel_bias, onehot):
    tab = jnp.einsum("bkq,bh->hkq", onehot, rel_bias, precision=lax.Precision.HIGHEST)
    tab = tab.reshape(NKV, GRP, 2 * BLK, BLK)
    return jnp.transpose(tab, (0, 2, 1, 3)).reshape(NKV, 2 * BLK, GRP * BLK)


def _local_step(x, target, vec, weights_of, wgrad, grads_done, small_done):
    onehot_np, masks_np = _attention_tables()
    onehot = jnp.asarray(onehot_np)
    masks = jnp.asarray(masks_np)
    bias_t = jnp.where(masks[:, None] > 0.5, _bias_table(vec["rel_bias"], onehot)[None], NEG)
    sink_rows = jnp.repeat(vec["attn_sinks"].reshape(NKV, 1, GRP), BLK, axis=2)
    qg = vec["q_norm"].reshape(HD, 1)
    kg = vec["k_norm"].reshape(HD, 1)
    g1 = vec["ffn1_norm"].reshape(1, D)
    gm = vec["mix_norm"].reshape(1, D)
    g2 = vec["ffn2_norm"].reshape(1, D)
    dwb = vec["conv_dw_bias"].reshape(1, D)
    lng = vec["conv_ln_g"].reshape(1, D)
    lnb = vec["conv_ln_b"].reshape(1, D)

    w1 = weights_of("ffn1_in", (bias_t, sink_rows))
    n1, u1 = _ffn_up(x, g1, w1["ffn1_w_in"], "ffn1_up")
    w1.update(weights_of("ffn1_out", (u1,)))
    x1 = _ffn_down(x, u1, w1["ffn1_w_out"], "ffn1_down")
    wm = weights_of("mix", (x1,))
    dwk = jnp.pad(wm["conv_dw_kernel"], ((0, CWP - CW), (0, 0)))
    hm, uc, gp, qkv_t = _mix_proj(x1, gm, wm["w_in"])
    zs, zc = _conv_fwd(uc, dwk, dwb, lng, lnb)
    o_t = _attn_fwd(qkv_t, qg, kg, sink_rows, bias_t)
    x2, a, b, merged = _mix_out(zs, o_t, gp, x1, wm["conv_w_proj"], wm["attn_w_o"], wm["w_out"])
    w2 = weights_of("ffn2", (x2,))
    n2, u2, dx3, sq = _ffn_fwd(x2, g2, w2["ffn2_w_in"], w2["ffn2_w_out"], "ffn2_fwd", target=target)

    gv = {}
    dx2, du2, h2, dy2, gv["ffn2_norm"] = _ffn_bwd(dx3, x2, g2, u2, w2["ffn2_w_in"], w2["ffn2_w_out"], "ffn2_bwd")
    deps = grads_done("ffn2", {"ffn2_w_in": wgrad(du2, n2, "ffn2_dw_in", False),
                               "ffn2_w_out": wgrad(h2, dy2, "ffn2_dw_out", False)})

    dzs, do_t, dgp, da, db, dx2b = _mix_out_bwd(dx2, a, b, gp, wm["conv_w_proj"], wm["attn_w_o"], wm["w_out"], deps=deps)
    deps = grads_done("mix_out", {"w_out": wgrad(merged, dx2b, "mix_dw_out", False),
                                  "conv_w_proj": wgrad(zs, da, "mix_dw_cp", False),
                                  "attn_w_o": wgrad(o_t, db, "mix_dw_o", True)})

    dq_t, ckv, dqg, dsink_rows, dsacc = _attn_bwd(qkv_t, do_t, qg, kg, sink_rows, bias_t, deps=deps)
    dkv_t, dkg = _kv_combine(ckv, qkv_t, kg)
    gv["q_norm"] = dqg.reshape(HD)
    gv["k_norm"] = dkg.reshape(HD)
    gv["attn_sinks"] = _sink_grad(dsink_rows)[:, :GRP].reshape(NQ)
    gv["rel_bias"] = _bias_grad(dsacc, onehot)[:, :, :GRP].reshape(NBUCKET, NQ)

    duc, dk_conv, gv["conv_dw_bias"], gv["conv_ln_g"], gv["conv_ln_b"] = _conv_bwd(uc, zc, dzs, dwk, lng, lnb)
    gv["conv_dw_kernel"] = dk_conv[:CW]

    dx1, gv["mix_norm"] = _mix_proj_bwd(dx2, duc, dq_t, dkv_t, dgp, x1, gm, wm["w_in"])
    deps = grads_done("mix_in", {"w_in": _wgrad_mix(duc, dq_t, dkv_t, dgp, hm)})

    dx0, du1, h1, dy1, gv["ffn1_norm"] = _ffn_bwd(dx1, x, g1, u1, w1["ffn1_w_in"], w1["ffn1_w_out"], "ffn1_bwd", deps=deps)
    for k in ("ffn1_norm", "mix_norm", "ffn2_norm", "conv_dw_bias", "conv_ln_g", "conv_ln_b"):
        gv[k] = gv[k].reshape(D)
    deps = small_done(gv, sq)
    deps = grads_done("ffn1_in", {"ffn1_w_in": wgrad(du1, n1, "ffn1_dw_in", False, deps)})
    grads_done("ffn1_out", {"ffn1_w_out": wgrad(h1, dy1, "ffn1_dw_out", False, deps)})
    return dx0


MESH_ID = pl.DeviceIdType.MESH


def _position():
    return lax.axis_index("x"), lax.axis_index("y"), lax.axis_index("c")


def _shard_rows(ref, index, rows):
    return ref.at[pl.ds(pl.multiple_of(index * rows, 16), rows), :]


def _prep(weights, taps, me):
    n = len(weights)

    def body(me_ref, *refs):
        for k in range(n):
            refs[n + 1 + k][...] = refs[k][...].astype(BF)
        refs[2 * n + 1][0:CW, :] = refs[n][...]
        refs[2 * n + 1][CW:, :] = jnp.zeros((CWP - CW, BLK), F32)

    shard_shapes = [w.shape for w in weights] + [(CWP, BLK)]
    dtypes = [BF] * n + [F32]
    ins = list(weights) + [taps]
    return pl.pallas_call(
        body,
        grid_spec=pltpu.PrefetchScalarGridSpec(
            num_scalar_prefetch=1, grid=(1,),
            in_specs=[pl.BlockSpec(a.shape, lambda i, m: (0, 0), pipeline_mode=pl.Buffered(1)) for a in ins],
            out_specs=[pl.BlockSpec(s, lambda i, m: (m[0], 0)) for s in shard_shapes]),
        out_shape=[jax.ShapeDtypeStruct((N_DEV * s[0], s[1]), d) for s, d in zip(shard_shapes, dtypes)],
        compiler_params=_params(1), name="prep")(me, *ins)


HBM = pl.BlockSpec(memory_space=pltpu.HBM)
SEM = pl.BlockSpec(memory_space=pltpu.SEMAPHORE)
DATAFLOW = pltpu.SideEffectType.DATAFLOW_SIDE_EFFECTING
TOKEN = jax.ShapeDtypeStruct((8, 128), F32)


def _in_hbm(x):
    return pltpu.with_memory_space_constraint(x, pltpu.HBM)


def _hbm_like(arrays):
    return [pltpu.HBM(a.shape, a.dtype) for a in arrays]


def _other_chips(x, y):
    return [(1 - x, y), (x, 1 - y), (1 - x, 1 - y)]


def _device_index(chip, c):
    return 4 * chip[0] + 2 * chip[1] + c


def _chip_index(chip):
    return 2 * chip[0] + chip[1]


class _Exchange:
    def __init__(self, gather, all_cores=False):
        self.gather = gather
        self.all_cores = all_cores
        self.n_peers = N_DEV - 1 if all_cores else 3

    def peers(self, x, y, c):
        if self.all_cores:
            return [(x ^ (k >> 2), y ^ ((k >> 1) & 1), c ^ (k & 1)) for k in range(1, N_DEV)]
        return [(*chip, c) for chip in _other_chips(x, y)]

    def sent(self, x, y, c, peer):
        return _device_index((x, y), c) if self.gather else _chip_index(peer[:2])

    def lands_at(self, x, y, c):
        return _device_index((x, y), c) if self.gather else _chip_index((x, y))

    def arrives_at(self, peer):
        return _device_index(peer[:2], peer[2]) if self.gather else _chip_index(peer[:2])


def _ici_copies_start(sets, sources, landings, exchanges, name, deps=()):
    n = len(landings)
    arrays = (list(sources) if sources is not None else []) + list(landings)
    first_land = len(arrays) - n
    n_sets = len(sets)
    n_deps = len(deps)

    def body(*refs):
        refs = refs[n_deps:]
        src, land = refs[:n], refs[first_land:first_land + n]
        sems = refs[len(arrays):len(arrays) + 2 * n_sets]
        token = refs[-1]
        x, y, c = _position()
        for s, (members, exchange) in enumerate(zip(sets, exchanges)):
            for slot, (k, rows) in enumerate(members):
                for j, peer in enumerate(exchange.peers(x, y, c)):
                    at = exchange.n_peers * slot + j
                    pltpu.make_async_remote_copy(
                        src_ref=_shard_rows(src[k], exchange.sent(x, y, c, peer), rows),
                        dst_ref=_shard_rows(land[k], exchange.lands_at(x, y, c), rows),
                        send_sem=sems[2 * s].at[at], recv_sem=sems[2 * s + 1].at[at],
                        device_id=peer, device_id_type=MESH_ID).start()
        token[...] = jnp.zeros_like(token)

    sem_shapes = []
    for members, exchange in zip(sets, exchanges):
        sem_shapes += [pltpu.SemaphoreType.DMA((exchange.n_peers * len(members),))] * 2
    out = pl.pallas_call(
        body, name=name,
        out_shape=sem_shapes + _hbm_like(arrays) + [TOKEN],
        in_specs=[ANY] * n_deps + [HBM] * len(arrays),
        out_specs=[SEM] * (2 * n_sets) + [HBM] * len(arrays) + [pl.BlockSpec(memory_space=pltpu.VMEM)],
        input_output_aliases={n_deps + i: 2 * n_sets + i for i in range(len(arrays))},
        compiler_params=pltpu.CompilerParams(has_side_effects=DATAFLOW),
    )(*deps, *[_in_hbm(a) for a in arrays])
    sems = [(out[2 * s], out[2 * s + 1]) for s in range(n_sets)]
    thru = list(out[2 * n_sets:2 * n_sets + len(arrays)])
    return sems, (thru[:first_land] if sources is not None else None), thru[first_land:], out[-1]


def _ici_copies_wait(sems, members, sources, landings, exchange, after, name):
    n = len(landings)
    arrays = (list(sources) if sources is not None else []) + list(landings)
    first_land = len(arrays) - n

    def body(*refs):
        src, land = refs[:n], refs[first_land:first_land + n]
        send_sems, recv_sems = refs[len(arrays)], refs[len(arrays) + 1]
        x, y, c = _position()
        for slot, rows in enumerate(members):
            for j, peer in enumerate(exchange.peers(x, y, c)):
                at = exchange.n_peers * slot + j
                cp = pltpu.make_async_remote_copy(
                    src_ref=_shard_rows(src[slot], exchange.sent(x, y, c, peer), rows),
                    dst_ref=_shard_rows(land[slot], exchange.arrives_at(peer), rows),
                    send_sem=send_sems.at[at], recv_sem=recv_sems.at[at], device_id=peer, device_id_type=MESH_ID)
                cp.wait_send()
                cp.wait_recv()

    out = pl.pallas_call(
        body, name=name, out_shape=_hbm_like(arrays),
        in_specs=[HBM] * len(arrays) + [SEM, SEM] + [ANY] * len(after), out_specs=[HBM] * len(arrays),
        input_output_aliases={i: i for i in range(len(arrays))},
        compiler_params=pltpu.CompilerParams(has_side_effects=DATAFLOW),
    )(*arrays, sems[0], sems[1], *after)
    return list(out[first_land:])


def _d2d_gather(buffers, rows, name):
    n = len(buffers)

    def body(*refs):
        land = refs[n:2 * n]
        send_sems, recv_sems = refs[2 * n:]
        x, y, c = _position()
        chips = [(x, y)] + _other_chips(x, y)
        sends, recvs = [], []
        for k in range(n):
            for j, chip in enumerate(chips):
                for copies, core in ((sends, c), (recvs, 1 - c)):
                    block = _shard_rows(land[k], _device_index(chip, core), rows[k])
                    copies.append(pltpu.make_async_remote_copy(
                        src_ref=block, dst_ref=block, send_sem=send_sems.at[k, j], recv_sem=recv_sems.at[k, j],
                        device_id=(x, y, 1 - c), device_id_type=MESH_ID))
        for cp in sends:
            cp.start()
        for cp in recvs:
            cp.wait_recv()
        for cp in sends:
            cp.wait_send()

    return pl.pallas_call(
        body, name=name, out_shape=[jax.ShapeDtypeStruct(a.shape, a.dtype) for a in buffers],
        in_specs=[ANY] * n, out_specs=[ANY] * n, input_output_aliases={i: i for i in range(n)},
        scratch_shapes=[pltpu.SemaphoreType.DMA((n, 4)), pltpu.SemaphoreType.DMA((n, 4))],
    )(*buffers)


def _rs_pair(grads, name):
    n = len(grads)
    rows = [g.shape[0] // N_DEV for g in grads]

    def body(*refs):
        ins, outs = refs[:n], refs[n:2 * n]
        send_sems, recv_sems = refs[2 * n:]
        x, y, c = _position()
        copies = []
        for k in range(n):
            for q in range(4):
                copies.append(pltpu.make_async_remote_copy(
                    src_ref=_shard_rows(ins[k], 2 * q + 1 - c, rows[k]), dst_ref=_shard_rows(outs[k], q, rows[k]),
                    send_sem=send_sems.at[k, q], recv_sem=recv_sems.at[k, q], device_id=(x, y, 1 - c),
                    device_id_type=MESH_ID))
        for cp in copies:
            cp.start()
        for cp in copies:
            cp.wait()

    return pl.pallas_call(
        body, out_shape=[jax.ShapeDtypeStruct((4 * r, g.shape[1]), g.dtype) for g, r in zip(grads, rows)],
        in_specs=[ANY] * n, out_specs=[ANY] * n,
        scratch_shapes=[pltpu.SemaphoreType.DMA((n, 4)), pltpu.SemaphoreType.DMA((n, 4))],
        name=name)(*grads)


def _wgrad_pair(lhs, rhs, name, *, lhs_is_transposed, deps=()):
    t = rhs.shape[0]
    n = lhs.shape[0] if lhs_is_transposed else lhs.shape[1]
    r = n // N_DEV
    n_chips = N_DEV // 2
    per = 1 if (2 * r) % BLK == 0 else 2
    steps = n_chips // per

    def body(l_ref, r_ref, kept_ref, recv_ref, res, send_sems, recv_sems):
        q = pl.program_id(0)
        slot = q % 2
        x, y, c = _position()

        def send(step, buf, i):
            return pltpu.make_async_remote_copy(
                src_ref=res.at[buf, pl.ds(pl.multiple_of((2 * i + 1 - c) * r, 16), r), :],
                dst_ref=_shard_rows(recv_ref, step * per + i, r),
                send_sem=send_sems.at[buf, i], recv_sem=recv_sems.at[step * per + i],
                device_id=(x, y, 1 - c), device_id_type=MESH_ID)

        @pl.when(q >= 2)
        def _():
            for i in range(per):
                send(q - 2, slot, i).wait_send()

        if lhs_is_transposed:
            res[slot] = _dot(l_ref[...], r_ref[...]).astype(BF)
        else:
            res[slot] = _dot_tn(l_ref[...], r_ref[...]).astype(BF)
        for i in range(per):
            kept_ref[i * r:(i + 1) * r, :] = res[slot, pl.ds(pl.multiple_of((2 * i + c) * r, 16), r), :]
            send(q, slot, i).start()

        @pl.when(q == steps - 1)
        def _():
            for i in range(per):
                if steps > 1:
                    send(q - 1, 1 - slot, i).wait_send()
                send(q, slot, i).wait_send()
            for chip in range(n_chips):
                send(chip // per, 0, chip % per).wait_recv()

    width = 2 * r * per
    lhs_spec = pl.BlockSpec((width, t), lambda q: (q, 0)) if lhs_is_transposed else pl.BlockSpec((t, width), lambda q: (0, q))
    return _call(
        body, deps, (lhs, rhs), grid=(steps,),
        in_specs=[lhs_spec, _resident((t, D))],
        out_specs=[pl.BlockSpec((per * r, D), lambda q: (q, 0)), ANY],
        out_shape=[jax.ShapeDtypeStruct((n // 2, D), BF)] * 2,
        scratch_shapes=[pltpu.VMEM((2, width, D), BF), pltpu.SemaphoreType.DMA((2, per)),
                        pltpu.SemaphoreType.DMA((n_chips,))],
        compiler_params=_params(1), name=name)


def _pair_add(grad, received, place, name, kept_only=False):
    r = received.shape[0] // 4
    tr = 352 if r % 352 == 0 else r
    per = r // tr
    parity = 0 if kept_only else 1

    def body(place_ref, g_ref, r_ref, o_ref, land_ref):
        total = (g_ref[...].astype(F32) + r_ref[...].astype(F32)).astype(BF)
        o_ref[...] = total

        @pl.when(pl.program_id(1) == place_ref[1])
        def _():
            land_ref[...] = total

    return pl.pallas_call(
        body,
        grid_spec=pltpu.PrefetchScalarGridSpec(
            num_scalar_prefetch=1, grid=(per, 4),
            in_specs=[pl.BlockSpec((tr, D), lambda i, q, p: (((1 + parity) * q + parity * p[0]) * per + i, 0)),
                      pl.BlockSpec((tr, D), lambda i, q, p: (q * per + i, 0))],
            out_specs=[pl.BlockSpec((tr, D), lambda i, q, p: (q * per + i, 0)),
                       pl.BlockSpec((tr, D), lambda i, q, p: (p[1] * per + i, 0))]),
        out_shape=[jax.ShapeDtypeStruct(received.shape, BF)] * 2,
        compiler_params=_params(2), name=name)(place, grad, received)


def _all_reduce_small(payload, deps=()):
    r = payload.shape[0]

    def body(in_ref, out_ref, land_ref, send_sems, recv_sems):
        x, y, c = _position()
        me = 4 * x + 2 * y + c
        land_ref[me] = in_ref[...]
        copies = []
        for k in range(1, N_DEV):
            peer = (x ^ (k >> 2), y ^ ((k >> 1) & 1), c ^ (k & 1))
            copies.append(pltpu.make_async_remote_copy(
                src_ref=in_ref, dst_ref=land_ref.at[me], send_sem=send_sems.at[k - 1], recv_sem=recv_sems.at[k - 1],
                device_id=peer, device_id_type=MESH_ID))
        for cp in copies:
            cp.start()
        for k in range(1, N_DEV):
            peer_index = me ^ k
            pltpu.make_async_remote_copy(
                src_ref=in_ref, dst_ref=land_ref.at[peer_index], send_sem=send_sems.at[k - 1], recv_sem=recv_sems.at[k - 1],
                device_id=(x, y, c), device_id_type=MESH_ID).wait_recv()
        for cp in copies:
            cp.wait_send()
        acc = land_ref[0]
        for d in range(1, N_DEV):
            acc = acc + land_ref[d]
        out_ref[...] = acc

    return _call(
        body, deps, (payload,), out_shape=jax.ShapeDtypeStruct((r, D), F32),
        in_specs=[pl.BlockSpec(memory_space=pltpu.VMEM)], out_specs=pl.BlockSpec(memory_space=pltpu.VMEM),
        scratch_shapes=[pltpu.VMEM((N_DEV, r, D), F32), pltpu.SemaphoreType.DMA((N_DEV - 1,)),
                        pltpu.SemaphoreType.DMA((N_DEV - 1,))],
        name="all_reduce_small")


def _adamw_math(w, g, m, v):
    m = ADAM_B1 * m + (1.0 - ADAM_B1) * g
    v = ADAM_B2 * v + (1.0 - ADAM_B2) * (g * g)
    m_hat = m / (1.0 - ADAM_B1 ** ADAM_STEP)
    v_hat = v / (1.0 - ADAM_B2 ** ADAM_STEP)
    delta = -ADAM_LR * (m_hat / (jnp.sqrt(v_hat) + ADAM_EPS) + ADAM_WD * w)
    return delta, m, v


def _sum_partials(blocks):
    g = blocks[0].astype(F32)
    for blk in blocks[1:]:
        g = g + blk.astype(F32)
    return g


def _reduce_adamw(landed, w, m, v, name):
    r = w.shape[0]
    tr = 352 if r % 352 == 0 else r
    per = r // tr

    def body(r0, r1, r2, r3, w_ref, m_ref, v_ref, g_ref, d_ref, nm_ref, nv_ref):
        g = _sum_partials([r0[...], r1[...], r2[...], r3[...]])
        g_ref[...] = g
        d_ref[...], nm_ref[...], nv_ref[...] = _adamw_math(w_ref[...], g, m_ref[...], v_ref[...])

    tile = _row_tile(tr, D)
    return pl.pallas_call(
        body, grid=(per,),
        in_specs=[pl.BlockSpec((tr, D), lambda i, q=q: (q * per + i, 0)) for q in range(4)] + [tile] * 3,
        out_specs=[tile] * 4, out_shape=[jax.ShapeDtypeStruct(w.shape, F32)] * 4,
        compiler_params=_params(1), name=name)(landed, landed, landed, landed, w, m, v)


def _adamw_small(w, g, m, v, name):
    def body(w_ref, g_ref, m_ref, v_ref, d_ref, nm_ref, nv_ref):
        d_ref[...], nm_ref[...], nv_ref[...] = _adamw_math(w_ref[...], g_ref[...], m_ref[...], v_ref[...])

    return pl.pallas_call(body, out_shape=[jax.ShapeDtypeStruct(w.shape, F32)] * 3, name=name)(w, g, m, v)


WEIGHTS = ("ffn1_norm", "ffn1_w_in", "ffn1_w_out", "mix_norm", "w_in", "conv_dw_kernel", "conv_dw_bias", "conv_ln_g",
           "conv_ln_b", "conv_w_proj", "q_norm", "k_norm", "attn_sinks", "rel_bias", "attn_w_o", "w_out", "ffn2_norm",
           "ffn2_w_in", "ffn2_w_out")
MATRICES = ("ffn1_w_in", "ffn1_w_out", "w_in", "conv_w_proj", "attn_w_o", "w_out", "ffn2_w_in", "ffn2_w_out")
COLUMN_SHARDED = ("ffn1_w_in", "w_in", "ffn2_w_in")
ROW_VECTORS = ("ffn1_norm", "mix_norm", "conv_dw_bias", "conv_ln_g", "conv_ln_b", "ffn2_norm")
PACKED = (("q_norm", HD), ("k_norm", HD), ("attn_sinks", NQ), ("rel_bias", NBUCKET * NQ))
GATHER = _Exchange(gather=True)
GATHER_ALL = _Exchange(gather=True, all_cores=True)
SCATTER = _Exchange(gather=False)
GATHER_STAGES = ("ffn1_in", "ffn1_out", "mix", "ffn2")
STAGE_GATHER = {"ffn1_in": GATHER, "ffn1_out": GATHER, "mix": GATHER, "ffn2": GATHER_ALL}
STAGE_MEMBERS = {"ffn1_in": ("ffn1_w_in",), "ffn1_out": ("ffn1_w_out",),
                 "mix": ("w_in", "conv_w_proj", "attn_w_o", "w_out", "taps"), "ffn2": ("ffn2_w_in", "ffn2_w_out")}
ROW_PACKED = len(ROW_VECTORS)
ROW_LOSS = ROW_PACKED + 1
ROW_TAPS = 8
PAYLOAD_ROWS = ROW_TAPS + CWP


def _pack_small(values, last_row):
    packed = jnp.concatenate([values[k].reshape(-1) for k, _ in PACKED])
    packed = jnp.pad(packed, (0, D - packed.shape[0])).reshape(1, D)
    return jnp.concatenate([values[k].reshape(1, D) for k in ROW_VECTORS] + [packed, last_row], axis=0)


def _unpack_small(rows):
    out = {k: rows[i] for i, k in enumerate(ROW_VECTORS)}
    at = 0
    for k, size in PACKED:
        out[k] = rows[ROW_PACKED, at:at + size]
        at += size
    out["rel_bias"] = out["rel_bias"].reshape(NBUCKET, NQ)
    return out


def kernel(x, ffn1_norm, ffn1_w_in, ffn1_w_out, mix_norm, w_in, conv_dw_kernel, conv_dw_bias, conv_ln_g, conv_ln_b, conv_w_proj, q_norm, k_norm, attn_sinks, rel_bias, attn_w_o, w_out, ffn2_norm, ffn2_w_in, ffn2_w_out, loss_target, m_ffn1_norm, m_ffn1_w_in, m_ffn1_w_out, m_mix_norm, m_w_in, m_conv_dw_kernel, m_conv_dw_bias, m_conv_ln_g, m_conv_ln_b, m_conv_w_proj, m_q_norm, m_k_norm, m_attn_sinks, m_rel_bias, m_attn_w_o, m_w_out, m_ffn2_norm, m_ffn2_w_in, m_ffn2_w_out, v_ffn1_norm, v_ffn1_w_in, v_ffn1_w_out, v_mix_norm, v_w_in, v_conv_dw_kernel, v_conv_dw_bias, v_conv_ln_g, v_conv_ln_b, v_conv_w_proj, v_q_norm, v_k_norm, v_attn_sinks, v_rel_bias, v_attn_w_o, v_w_out, v_ffn2_norm, v_ffn2_w_in, v_ffn2_w_out):
    w = dict(ffn1_norm=ffn1_norm, ffn1_w_in=ffn1_w_in, ffn1_w_out=ffn1_w_out, mix_norm=mix_norm, w_in=w_in,
             conv_dw_kernel=conv_dw_kernel, conv_dw_bias=conv_dw_bias, conv_ln_g=conv_ln_g, conv_ln_b=conv_ln_b,
             conv_w_proj=conv_w_proj, q_norm=q_norm, k_norm=k_norm, attn_sinks=attn_sinks, rel_bias=rel_bias,
             attn_w_o=attn_w_o, w_out=w_out, ffn2_norm=ffn2_norm, ffn2_w_in=ffn2_w_in, ffn2_w_out=ffn2_w_out)
    m = dict(ffn1_norm=m_ffn1_norm, ffn1_w_in=m_ffn1_w_in, ffn1_w_out=m_ffn1_w_out, mix_norm=m_mix_norm, w_in=m_w_in,
             conv_dw_kernel=m_conv_dw_kernel, conv_dw_bias=m_conv_dw_bias, conv_ln_g=m_conv_ln_g, conv_ln_b=m_conv_ln_b,
             conv_w_proj=m_conv_w_proj, q_norm=m_q_norm, k_norm=m_k_norm, attn_sinks=m_attn_sinks, rel_bias=m_rel_bias,
             attn_w_o=m_attn_w_o, w_out=m_w_out, ffn2_norm=m_ffn2_norm, ffn2_w_in=m_ffn2_w_in, ffn2_w_out=m_ffn2_w_out)
    v = dict(ffn1_norm=v_ffn1_norm, ffn1_w_in=v_ffn1_w_in, ffn1_w_out=v_ffn1_w_out, mix_norm=v_mix_norm, w_in=v_w_in,
             conv_dw_kernel=v_conv_dw_kernel, conv_dw_bias=v_conv_dw_bias, conv_ln_g=v_conv_ln_g, conv_ln_b=v_conv_ln_b,
             conv_w_proj=v_conv_w_proj, q_norm=v_q_norm, k_norm=v_k_norm, attn_sinks=v_attn_sinks, rel_bias=v_rel_bias,
             attn_w_o=v_attn_w_o, w_out=v_w_out, ffn2_norm=v_ffn2_norm, ffn2_w_in=v_ffn2_w_in, ffn2_w_out=v_ffn2_w_out)
    px, py, pc = _position()
    me = 4 * px + 2 * py + pc
    place = jnp.stack([pc, 2 * px + py]).astype(jnp.int32)

    rows_of = lambda k, a: a.T if k in COLUMN_SHARDED else a
    buffers = dict(zip(MATRICES + ("taps",), _prep([rows_of(k, w[k]) for k in MATRICES], conv_dw_kernel,
                                                   me.astype(jnp.int32).reshape(1))))
    landings, sets = [], []
    for stage in GATHER_STAGES:
        sets.append([(len(landings) + i, buffers[k].shape[0] // N_DEV) for i, k in enumerate(STAGE_MEMBERS[stage])])
        landings += [buffers[k] for k in STAGE_MEMBERS[stage]]
    sems, _, land_thru, _ = _ici_copies_start(sets, None, landings, [STAGE_GATHER[s] for s in GATHER_STAGES],
                                              "gather_start")

    def weights_of(stage, after):
        s = GATHER_STAGES.index(stage)
        rows = [r for _, r in sets[s]]
        landed = _ici_copies_wait(sems[s], rows, None, [land_thru[k] for k, _ in sets[s]], STAGE_GATHER[stage],
                                  list(after), "gather_wait_" + stage)
        if not STAGE_GATHER[stage].all_cores:
            landed = _d2d_gather(landed, rows, "gather_d2d_" + stage)
        out = dict(zip(STAGE_MEMBERS[stage], landed))
        if "taps" in out:
            taps = out.pop("taps")
            out["conv_dw_kernel"] = jnp.transpose(taps.reshape(N_DEV, CWP, BLK), (1, 0, 2)).reshape(CWP, D)[:CW]
        return out

    in_flight = []

    def wgrad(lhs, rhs, name, lhs_is_transposed, deps=()):
        return _wgrad_pair(lhs, rhs, name, lhs_is_transposed=lhs_is_transposed, deps=deps)

    def grads_done(stage, grads):
        names = list(grads)
        added = []
        for k in names:
            if isinstance(grads[k], (tuple, list)):
                kept, received = grads[k]
                added.append(_pair_add(kept, received, place, "pair_add_" + k, kept_only=True))
            else:
                received, = _rs_pair([grads[k]], "rs_pair_" + k)
                added.append(_pair_add(grads[k], received, place, "pair_add_" + k))
        partials = [p for p, _ in added]
        members = [(i, p.shape[0] // 4) for i, p in enumerate(partials)]
        sem, p_thru, l_thru, token = _ici_copies_start([members], partials, [l for _, l in added], [SCATTER],
                                                       "scatter_start_" + stage)
        in_flight.append((stage, names, sem[0], p_thru, l_thru, token))
        return [token]

    reduced = []

    def small_done(gv, sq):
        payload = jnp.concatenate([_pack_small(gv, sq), jnp.pad(gv["conv_dw_kernel"], ((0, CWP - CW), (0, 0)))], axis=0)
        reduced.append(_all_reduce_small(payload))
        return reduced

    vec = {k: w[k] for k in WEIGHTS if k not in MATRICES and k != "conv_dw_kernel"}
    dx0 = _local_step(x[0], loss_target[0], vec, weights_of, wgrad, grads_done, small_done)
    total = reduced[0]
    loss = (0.5 / D) * jnp.sum(total[ROW_LOSS])

    grads, delta, new_m, new_v = {}, {}, {}, {}
    after = [in_flight[-1][-1]]
    for stage, names, sem, p_thru, l_thru, _ in in_flight:
        landed = _ici_copies_wait(sem, [p.shape[0] // 4 for p in p_thru], p_thru, l_thru, SCATTER, after,
                                  "scatter_wait_" + stage)
        after = []
        for k, buf in zip(names, landed):
            out = _reduce_adamw(buf, rows_of(k, w[k]), rows_of(k, m[k]), rows_of(k, v[k]), "adamw_" + k)
            grads[k], delta[k], new_m[k], new_v[k] = [rows_of(k, a) for a in out]
            after.append(out[1])
    zero_row = jnp.zeros((1, D), F32)
    d8, m8, v8 = _adamw_small(_pack_small(w, zero_row), total[:ROW_TAPS], _pack_small(m, zero_row),
                              _pack_small(v, zero_row), "adamw_small")
    grads.update(_unpack_small(total[:ROW_TAPS]))
    delta.update(_unpack_small(d8))
    new_m.update(_unpack_small(m8))
    new_v.update(_unpack_small(v8))
    k = "conv_dw_kernel"
    grads[k] = lax.dynamic_slice_in_dim(total[ROW_TAPS:ROW_TAPS + CW], me * BLK, BLK, axis=1)
    delta[k], new_m[k], new_v[k] = _adamw_small(w[k], grads[k], m[k], v[k], "adamw_taps")

    return (loss, dx0[None], *[grads[k] for k in WEIGHTS], *[delta[k] for k in WEIGHTS],
            *[new_m[k] for k in WEIGHTS], *[new_v[k] for k in WEIGHTS])
```

```python
import functools
import math

import numpy as np
import jax
import jax.numpy as jnp
from jax import lax
from jax.experimental import pallas as pl
from jax.experimental.pallas import tpu as pltpu

F32 = jnp.float32
BF = jnp.bfloat16

D = 1024
F = 2816
INW = 5632
CW = 31
CWP = 32
HD = 64
NQ = 16
NKV = 4
GRP = NQ // NKV
BLK = 128
NBUCKET = 32
EPS = 1e-6
NEG = float(jnp.finfo(jnp.float32).min)
QK_SCALE = 1.0 / math.sqrt(HD)
R_CONV = (0, 2048)
R_QKV = (2048, 3584)
R_Q = (2048, 3072)
R_KV = (3072, 3584)
R_GATE = (3584, 5632)

N_DEV = 8
VMEM_LIMIT_V7X = 56 * 1024 * 1024
ROW_TILE = 256
ROW_TILE_WIDE = 512

ADAM_LR = 0.001
ADAM_B1 = 0.9
ADAM_B2 = 0.999
ADAM_EPS = 1e-08
ADAM_WD = 0.01
ADAM_STEP = 10

NT_DIMS = (((1,), (1,)), ((), ()))
TN_DIMS = (((0,), (0,)), ((), ()))


def _dot(a, b):
    return jnp.dot(a, b, preferred_element_type=F32)


def _dot_nt(a, b):
    return lax.dot_general(a, b, NT_DIMS, preferred_element_type=F32)


def _dot_tn(a, b):
    return lax.dot_general(a, b, TN_DIMS, preferred_element_type=F32)


def _sig(x):
    return 0.5 * jnp.tanh(0.5 * x) + 0.5


ANY = pl.BlockSpec(memory_space=pl.ANY)


def _call(body, deps, args, **kw):
    n = len(deps)
    if n:
        kw["in_specs"] = [ANY] * n + list(kw["in_specs"])
        return pl.pallas_call(lambda *refs: body(*refs[n:]), **kw)(*deps, *args)
    return pl.pallas_call(body, **kw)(*args)


def _params(n_axes):
    return pltpu.CompilerParams(dimension_semantics=("arbitrary",) * n_axes, vmem_limit_bytes=VMEM_LIMIT_V7X)


def _resident(shape):
    zeros = (0,) * len(shape)
    return pl.BlockSpec(shape, lambda *_: zeros, pipeline_mode=pl.Buffered(1))


def _row_tile(rows, cols):
    return pl.BlockSpec((rows, cols), lambda i: (i, 0))


def _rms_stats(x):
    r = lax.rsqrt(jnp.mean(x * x, axis=-1, keepdims=True) + EPS)
    return r, x * r


def _rms_bwd(dn, x, g):
    r, xh = _rms_stats(x)
    dxh = dn * g
    dx = r * (dxh - xh * jnp.mean(dxh * xh, axis=-1, keepdims=True))
    return dx, jnp.sum(dn * xh, axis=0, keepdims=True)


def _ffn_fwd(x, g, w_in_t, w_out, name, target=None):
    t = x.shape[0]
    tm = min(ROW_TILE_WIDE, t)
    with_loss = target is not None

    def body(*refs):
        if with_loss:
            x_ref, g_ref, w_ref, wo_ref, t_ref, n_ref, u_ref, dy_ref, sq_ref = refs
        else:
            x_ref, g_ref, w_ref, wo_ref, n_ref, u_ref, xo_ref = refs
        x = x_ref[...]
        r, xh = _rms_stats(x)
        n = (xh * g_ref[...]).astype(BF)
        n_ref[...] = n
        u = _dot_nt(n, w_ref[...])
        u_ref[...] = u.astype(BF)
        a = u[:, :F]
        b = u[:, F:]
        h = (a * _sig(a) * b).astype(BF)
        xo = x + 0.5 * _dot(h, wo_ref[...])
        if with_loss:
            err = xo - t_ref[...]
            dy_ref[...] = err * (1.0 / D)

            @pl.when(pl.program_id(0) == 0)
            def _():
                sq_ref[...] = jnp.zeros_like(sq_ref)

            sq_ref[...] += jnp.sum(err * err, axis=0, keepdims=True)
        else:
            xo_ref[...] = xo

    in_specs = [_row_tile(tm, D), _resident((1, D)), _resident((INW, D)), _resident((F, D))]
    args = [x, g, w_in_t, w_out]
    out_specs = [_row_tile(tm, D), _row_tile(tm, INW), _row_tile(tm, D)]
    out_shape = [jax.ShapeDtypeStruct((t, D), BF), jax.ShapeDtypeStruct((t, INW), BF), jax.ShapeDtypeStruct((t, D), F32)]
    if with_loss:
        in_specs.append(_row_tile(tm, D))
        args.append(target)
        out_specs.append(pl.BlockSpec((1, D), lambda i: (0, 0)))
        out_shape.append(jax.ShapeDtypeStruct((1, D), F32))
    return pl.pallas_call(body, grid=(t // tm,), in_specs=in_specs, out_specs=out_specs, out_shape=out_shape,
                          compiler_params=_params(1), name=name)(*args)


def _ffn_up(x, g, w_in_t, name):
    t = x.shape[0]
    tm = min(ROW_TILE_WIDE, t)

    def body(x_ref, g_ref, w_ref, n_ref, u_ref):
        r, xh = _rms_stats(x_ref[...])
        n = (xh * g_ref[...]).astype(BF)
        n_ref[...] = n
        u_ref[...] = _dot_nt(n, w_ref[...]).astype(BF)

    return pl.pallas_call(
        body, grid=(t // tm,), in_specs=[_row_tile(tm, D), _resident((1, D)), _resident((INW, D))],
        out_specs=[_row_tile(tm, D), _row_tile(tm, INW)],
        out_shape=[jax.ShapeDtypeStruct((t, D), BF), jax.ShapeDtypeStruct((t, INW), BF)],
        compiler_params=_params(1), name=name)(x, g, w_in_t)


def _ffn_down(x, u, w_out, name):
    t = x.shape[0]
    tm = min(ROW_TILE_WIDE, t)

    def body(x_ref, u_ref, wo_ref, xo_ref):
        a = u_ref[:, :F].astype(F32)
        b = u_ref[:, F:].astype(F32)
        h = (a * _sig(a) * b).astype(BF)
        xo_ref[...] = x_ref[...] + 0.5 * _dot(h, wo_ref[...])

    return pl.pallas_call(
        body, grid=(t // tm,), in_specs=[_row_tile(tm, D), _row_tile(tm, INW), _resident((F, D))],
        out_specs=_row_tile(tm, D), out_shape=jax.ShapeDtypeStruct((t, D), F32),
        compiler_params=_params(1), name=name)(x, u, w_out)


def _ffn_bwd(dxo, x, g, u, w_in_t, w_out, name, deps=()):
    t = x.shape[0]
    tm = min(ROW_TILE, t)

    def body(dxo_ref, x_ref, g_ref, u_ref, w_ref, wo_ref, dx_ref, du_ref, h_ref, dy_ref, dg_ref):
        dxo = dxo_ref[...]
        dy = (0.5 * dxo).astype(BF)
        dy_ref[...] = dy
        dh = _dot_nt(dy, wo_ref[...])
        a = u_ref[:, :F].astype(F32)
        b = u_ref[:, F:].astype(F32)
        s = _sig(a)
        sa = a * s
        h_ref[...] = (sa * b).astype(BF)
        du_ref[:, :F] = (dh * b * (s * (1.0 + a * (1.0 - s)))).astype(BF)
        du_ref[:, F:] = (dh * sa).astype(BF)
        dn = _dot(du_ref[...], w_ref[...])
        dx, dg = _rms_bwd(dn, x_ref[...], g_ref[...])
        dx_ref[...] = dxo + dx

        @pl.when(pl.program_id(0) == 0)
        def _():
            dg_ref[...] = jnp.zeros_like(dg_ref)

        dg_ref[...] += dg

    return _call(
        body, deps, (dxo, x, g, u, w_in_t, w_out), grid=(t // tm,),
        in_specs=[_row_tile(tm, D), _row_tile(tm, D), _resident((1, D)), _row_tile(tm, INW), _resident((INW, D)),
                  _resident((F, D))],
        out_specs=[_row_tile(tm, D), _row_tile(tm, INW), _row_tile(tm, F), _row_tile(tm, D),
                   pl.BlockSpec((1, D), lambda i: (0, 0))],
        out_shape=[jax.ShapeDtypeStruct((t, D), F32), jax.ShapeDtypeStruct((t, INW), BF), jax.ShapeDtypeStruct((t, F), BF),
                   jax.ShapeDtypeStruct((t, D), BF), jax.ShapeDtypeStruct((1, D), F32)],
        compiler_params=_params(1), name=name)


def _wgrad(lhs, rhs, name, *, lhs_is_transposed, chunk, deps=()):
    t = rhs.shape[0]
    n = lhs.shape[0] if lhs_is_transposed else lhs.shape[1]
    c = min(chunk, n)

    def body(l_ref, r_ref, o_ref):
        if lhs_is_transposed:
            o_ref[...] = _dot(l_ref[...], r_ref[...]).astype(BF)
        else:
            o_ref[...] = _dot_tn(l_ref[...], r_ref[...]).astype(BF)

    lhs_spec = pl.BlockSpec((c, t), lambda j: (j, 0)) if lhs_is_transposed else pl.BlockSpec((t, c), lambda j: (0, j))
    return _call(
        body, deps, (lhs, rhs), grid=(n // c,),
        in_specs=[lhs_spec, _resident((t, D))],
        out_specs=pl.BlockSpec((c, D), lambda j: (j, 0)),
        out_shape=jax.ShapeDtypeStruct((n, D), BF),
        compiler_params=_params(1), name=name)


def _wgrad_mix(duc, dq_t, dkv_t, dgp, hm):
    t = hm.shape[0]
    c = 512
    first_q, first_kv, first_gate = R_Q[0] // c, R_KV[0] // c, R_GATE[0] // c

    def body(uc_ref, q_ref, kv_ref, gp_ref, h_ref, o_ref):
        j = pl.program_id(0)

        @pl.when(j < first_q)
        def _():
            o_ref[...] = _dot_tn(uc_ref[...], h_ref[...]).astype(BF)

        @pl.when((j >= first_q) & (j < first_kv))
        def _():
            o_ref[...] = _dot(q_ref[...], h_ref[...]).astype(BF)

        @pl.when((j >= first_kv) & (j < first_gate))
        def _():
            o_ref[...] = _dot(kv_ref[...], h_ref[...]).astype(BF)

        @pl.when(j >= first_gate)
        def _():
            o_ref[...] = _dot_tn(gp_ref[...], h_ref[...]).astype(BF)

    return pl.pallas_call(
        body, grid=(INW // c,),
        in_specs=[pl.BlockSpec((t, c), lambda j: (0, jnp.clip(j, 0, first_q - 1))),
                  pl.BlockSpec((c, t), lambda j: (jnp.clip(j - first_q, 0, first_kv - first_q - 1), 0)),
                  pl.BlockSpec((c, t), lambda j: (jnp.clip(j - first_kv, 0, first_gate - first_kv - 1), 0)),
                  pl.BlockSpec((t, c), lambda j: (0, jnp.clip(j - first_gate, 0, INW // c - first_gate - 1))),
                  _resident((t, D))],
        out_specs=pl.BlockSpec((c, D), lambda j: (j, 0)),
        out_shape=jax.ShapeDtypeStruct((INW, D), BF),
        compiler_params=_params(1), name="mix_dw_in")(duc, dq_t, dkv_t, dgp, hm)


def _mix_proj(x, g, w_t):
    t = x.shape[0]
    tm = min(ROW_TILE_WIDE, t)

    def body(x_ref, g_ref, w_ref, hm_ref, uc_ref, gp_ref, qkv_ref):
        r, xh = _rms_stats(x_ref[...])
        hm = (xh * g_ref[...]).astype(BF)
        hm_ref[...] = hm
        uc_ref[...] = _dot_nt(hm, w_ref[R_CONV[0]:R_CONV[1], :]).astype(BF)
        gp_ref[...] = _dot_nt(hm, w_ref[R_GATE[0]:R_GATE[1], :]).astype(BF)
        qkv_ref[...] = _dot_nt(w_ref[R_QKV[0]:R_QKV[1], :], hm).astype(BF)

    return pl.pallas_call(
        body, grid=(t // tm,),
        in_specs=[_row_tile(tm, D), _resident((1, D)), _resident((INW, D))],
        out_specs=[_row_tile(tm, D), _row_tile(tm, 2 * D), _row_tile(tm, 2 * D), pl.BlockSpec((1536, tm), lambda i: (0, i))],
        out_shape=[jax.ShapeDtypeStruct((t, D), BF), jax.ShapeDtypeStruct((t, 2 * D), BF),
                   jax.ShapeDtypeStruct((t, 2 * D), BF), jax.ShapeDtypeStruct((1536, t), BF)],
        compiler_params=_params(1), name="mix_proj")(x, g, w_t)


CONV_HALO = 32
CONV_LEAD = CONV_HALO - (CW - 1)


def _glu(uc):
    uc = uc.astype(F32)
    return uc[:, :D] * _sig(uc[:, D:])


def _ln_stats(zc):
    mu = jnp.mean(zc, axis=-1, keepdims=True)
    zm = zc - mu
    r = lax.rsqrt(jnp.mean(zm * zm, axis=-1, keepdims=True) + EPS)
    return r, zm * r


CONV_SHIFTS = 8
CONV_CHUNK = 32


def _store_shifted(buf, rows):
    for b in range(1, CONV_SHIFTS):
        buf[b, 0:rows - 8, :] = buf[0, pl.ds(b, rows - 8), :]


def _conv_fwd(uc, dwk, dwb, lng, lnb):
    t = uc.shape[0]
    tm = min(512, t)
    per = tm // CONV_HALO
    ext = tm + CONV_HALO

    def body(cur_ref, prev_ref, k_ref, kb_ref, g_ref, b_ref, o_ref, zc_ref, zsh):
        i = pl.program_id(0)
        zsh[0, 0:CONV_HALO, :] = _glu(prev_ref[...]) * (i > 0).astype(F32)
        zsh[0, CONV_HALO:, :] = _glu(cur_ref[...])
        _store_shifted(zsh, ext)

        def chunk(ci, carry):
            r0 = pl.multiple_of(ci * CONV_CHUNK, CONV_CHUNK)
            acc = jnp.zeros((CONV_CHUNK, D), F32) + kb_ref[...]
            for w in range(CW):
                a, b = divmod(CONV_LEAD + w, 8)
                acc = acc + k_ref[w:w + 1, :] * zsh[b, pl.ds(r0 + 8 * a, CONV_CHUNK), :]
            zc_ref[pl.ds(r0, CONV_CHUNK), :] = acc
            return carry

        lax.fori_loop(0, tm // CONV_CHUNK, chunk, 0)
        r, xh = _ln_stats(zc_ref[...])
        y = xh * g_ref[...] + b_ref[...]
        o_ref[...] = (y * _sig(y)).astype(BF)

    return pl.pallas_call(
        body, grid=(t // tm,),
        in_specs=[_row_tile(tm, 2 * D),
                  pl.BlockSpec((CONV_HALO, 2 * D), lambda i: (jnp.maximum(i * per - 1, 0), 0)),
                  _resident((CWP, D)), _resident((1, D)), _resident((1, D)), _resident((1, D))],
        out_specs=[_row_tile(tm, D), _row_tile(tm, D)],
        out_shape=[jax.ShapeDtypeStruct((t, D), BF), jax.ShapeDtypeStruct((t, D), F32)],
        scratch_shapes=[pltpu.VMEM((CONV_SHIFTS, ext, D), F32)],
        compiler_params=_params(1), name="conv_fwd")(uc, uc, dwk, dwb, lng, lnb)


def _conv_bwd(uc, zc, dzs, dwk, lng, lnb):
    t = uc.shape[0]
    tm = min(ROW_TILE_WIDE, t)
    per = tm // CONV_HALO
    n_tiles = t // tm
    ext = tm + CONV_HALO
    last_block = t // CONV_HALO - 1

    def body(cur_ref, zc_ref, zcn_ref, dz_ref, dzn_ref, k_ref, g_ref, b_ref,
             duc_ref, dk_ref, dkb_ref, dg_ref, db_ref, dsh, dk8, z_scr):
        i = pl.program_id(0)

        @pl.when(i == 0)
        def _():
            dk8[...] = jnp.zeros_like(dk8)
            dkb_ref[...] = jnp.zeros_like(dkb_ref)
            dg_ref[...] = jnp.zeros_like(dg_ref)
            db_ref[...] = jnp.zeros_like(db_ref)

        has_next = (i < n_tiles - 1).astype(F32)
        z_scr[...] = _glu(cur_ref[...])
        gain = g_ref[...]

        def ln_silu_bwd(zc, dzs, live):
            r, xh = _ln_stats(zc)
            y = xh * gain + b_ref[...]
            sy = _sig(y)
            dy = dzs * (sy * (1.0 + y * (1.0 - sy))) * live
            dxh = dy * gain
            dzc = r * (dxh - jnp.mean(dxh, axis=-1, keepdims=True) - xh * jnp.mean(dxh * xh, axis=-1, keepdims=True))
            return dzc, dy, xh

        dzc, dy, xh = ln_silu_bwd(zc_ref[...], dz_ref[...], 1.0)
        dsh[0, 0:tm, :] = dzc
        dg_ref[...] += jnp.sum(dy * xh, axis=0, keepdims=True)
        db_ref[...] += jnp.sum(dy, axis=0, keepdims=True)
        dkb_ref[...] += jnp.sum(dzc, axis=0, keepdims=True)
        dsh[0, tm:, :] = ln_silu_bwd(zcn_ref[...], dzn_ref[...], has_next)[0]
        _store_shifted(dsh, ext)

        def chunk(ci, carry):
            r0 = pl.multiple_of(ci * CONV_CHUNK, CONV_CHUNK)
            z_c = z_scr[pl.ds(r0, CONV_CHUNK), :]
            dz = jnp.zeros((CONV_CHUNK, D), F32)
            for w in range(CW):
                a, b = divmod(CW - 1 - w, 8)
                window = dsh[b, pl.ds(r0 + 8 * a, CONV_CHUNK), :]
                dz = dz + k_ref[w:w + 1, :] * window
                prod = z_c * window
                part = prod[0:8, :]
                for j in range(1, CONV_CHUNK // 8):
                    part = part + prod[8 * j:8 * j + 8, :]
                dk8[w] += part
            ucc = cur_ref[pl.ds(r0, CONV_CHUNK), :].astype(F32)
            sg = _sig(ucc[:, D:])
            duc_ref[pl.ds(r0, CONV_CHUNK), 0:D] = (dz * sg).astype(BF)
            duc_ref[pl.ds(r0, CONV_CHUNK), D:2 * D] = (dz * ucc[:, :D] * sg * (1.0 - sg)).astype(BF)
            return carry

        lax.fori_loop(0, tm // CONV_CHUNK, chunk, 0)

        @pl.when(i == n_tiles - 1)
        def _():
            dk_ref[...] = jnp.sum(dk8[...], axis=1)

    vec = pl.BlockSpec((1, D), lambda i: (0, 0))
    next_halo = pl.BlockSpec((CONV_HALO, D), lambda i: (jnp.minimum((i + 1) * per, last_block), 0))
    return pl.pallas_call(
        body, grid=(n_tiles,),
        in_specs=[_row_tile(tm, 2 * D), _row_tile(tm, D), next_halo, _row_tile(tm, D), next_halo,
                  _resident((CWP, D)), _resident((1, D)), _resident((1, D))],
        out_specs=[_row_tile(tm, 2 * D), pl.BlockSpec((CWP, D), lambda i: (0, 0)), vec, vec, vec],
        out_shape=[jax.ShapeDtypeStruct((t, 2 * D), BF), jax.ShapeDtypeStruct((CWP, D), F32),
                   jax.ShapeDtypeStruct((1, D), F32), jax.ShapeDtypeStruct((1, D), F32), jax.ShapeDtypeStruct((1, D), F32)],
        scratch_shapes=[pltpu.VMEM((CONV_SHIFTS, ext, D), F32), pltpu.VMEM((CWP, 8, D), F32), pltpu.VMEM((tm, D), F32)],
        compiler_params=_params(1), name="conv_bwd")(uc, zc, zc, dzs, dzs, dwk, lng, lnb)


def _norm_rows(xt, g):
    r = lax.rsqrt(jnp.mean(xt * xt, axis=0, keepdims=True) + EPS)
    xh = xt * r
    return xh * g, r, xh


ATT_TQ = 512


def _attn_specs(t, tq):
    per = tq // BLK
    return [pl.BlockSpec((1536, tq), lambda i: (0, i)),
            pl.BlockSpec((512, BLK), lambda i: (2, jnp.maximum(i * per - 1, 0))),
            _resident((HD, 1)), _resident((HD, 1)), _resident((NKV, 1, GRP * BLK)),
            _resident((2, NKV, 2 * BLK, GRP * BLK))]


def _attn_window(hk, sb, qkv_ref, halo_ref, kn_cur, kn_halo):
    v0 = D + NKV * HD + hk * HD
    if sb == 0:
        k_prev = kn_halo[hk]
        v_prev = halo_ref[NKV * HD + hk * HD:NKV * HD + (hk + 1) * HD, :]
    else:
        k_prev = kn_cur[hk][:, (sb - 1) * BLK:sb * BLK]
        v_prev = qkv_ref[v0:v0 + HD, (sb - 1) * BLK:sb * BLK]
    kw = jnp.concatenate([k_prev, kn_cur[hk][:, sb * BLK:(sb + 1) * BLK]], axis=1).astype(BF)
    vw = jnp.concatenate([v_prev, qkv_ref[v0:v0 + HD, sb * BLK:(sb + 1) * BLK]], axis=1)
    return kw, vw


def _attn_probs(kw, qc, bias, sink):
    st = _dot_tn(kw, qc) + bias
    m = jnp.maximum(jnp.max(st, axis=0, keepdims=True), sink)
    p = jnp.exp(st - m)
    e_sink = jnp.exp(sink - m)
    inv = 1.0 / (jnp.sum(p, axis=0, keepdims=True) + e_sink)
    return p * inv, e_sink * inv


def _attn_fwd(qkv_t, qg, kg, sink_rows, bias_t):
    t = qkv_t.shape[1]
    tq = min(ATT_TQ, t)
    n_sub = tq // BLK

    def body(qkv_ref, halo_ref, qg_ref, kg_ref, sink_ref, bias_ref, o_ref):
        i = pl.program_id(0)
        first = (i == 0).astype(jnp.int32)
        kgain = kg_ref[...]
        qgain = qg_ref[...]
        kn_cur = [_norm_rows(qkv_ref[D + h * HD:D + (h + 1) * HD, :].astype(F32), kgain)[0] for h in range(NKV)]
        kn_halo = [_norm_rows(halo_ref[h * HD:(h + 1) * HD, :].astype(F32), kgain)[0] for h in range(NKV)]
        for hk in range(NKV):
            for sb in range(n_sub):
                cols = slice(sb * BLK, (sb + 1) * BLK)
                kw, vw = _attn_window(hk, sb, qkv_ref, halo_ref, kn_cur, kn_halo)
                qc = jnp.concatenate(
                    [_norm_rows(qkv_ref[(GRP * hk + g) * HD:(GRP * hk + g + 1) * HD, cols].astype(F32), qgain)[0] * QK_SCALE
                     for g in range(GRP)], axis=1).astype(BF)
                bias = bias_ref[first, hk] if sb == 0 else bias_ref[0, hk]
                p, _ = _attn_probs(kw, qc, bias, sink_ref[hk])
                o = _dot(vw, p.astype(BF))
                for g in range(GRP):
                    head = GRP * hk + g
                    o_ref[head * HD:(head + 1) * HD, cols] = o[:, g * BLK:(g + 1) * BLK].astype(BF)

    return pl.pallas_call(
        body, grid=(t // tq,),
        in_specs=_attn_specs(t, tq),
        out_specs=pl.BlockSpec((D, tq), lambda i: (0, i)),
        out_shape=jax.ShapeDtypeStruct((D, t), BF),
        compiler_params=_params(1), name="attn_fwd")(qkv_t, qkv_t, qg, kg, sink_rows, bias_t)


def _attn_bwd(qkv_t, do_t, qg, kg, sink_rows, bias_t, deps=()):
    t = qkv_t.shape[1]
    tq = min(ATT_TQ, t)
    n_sub = tq // BLK
    n_tiles = t // tq

    def body(qkv_ref, halo_ref, do_ref, qg_ref, kg_ref, sink_ref, bias_ref,
             dq_ref, ckv_ref, dqg_ref, dsink_ref, dsacc_ref, qg_scr):
        i = pl.program_id(0)

        @pl.when(i == 0)
        def _():
            qg_scr[...] = jnp.zeros_like(qg_scr)
            dsink_ref[...] = jnp.zeros_like(dsink_ref)
            dsacc_ref[...] = jnp.zeros_like(dsacc_ref)

        first = (i == 0).astype(jnp.int32)
        kgain = kg_ref[...]
        qgain = qg_ref[...]
        kn_cur = [_norm_rows(qkv_ref[D + h * HD:D + (h + 1) * HD, :].astype(F32), kgain)[0] for h in range(NKV)]
        kn_halo = [_norm_rows(halo_ref[h * HD:(h + 1) * HD, :].astype(F32), kgain)[0] for h in range(NKV)]
        dqg = jnp.zeros((HD, BLK), F32)
        for hk in range(NKV):
            for sb in range(n_sub):
                cols = slice(sb * BLK, (sb + 1) * BLK)
                kw, vw = _attn_window(hk, sb, qkv_ref, halo_ref, kn_cur, kn_halo)
                qn, qr, qh = [], [], []
                for g in range(GRP):
                    head = GRP * hk + g
                    n_, r_, h_ = _norm_rows(qkv_ref[head * HD:(head + 1) * HD, cols].astype(F32), qgain)
                    qn.append(n_)
                    qr.append(r_)
                    qh.append(h_)
                qc = (jnp.concatenate(qn, axis=1) * QK_SCALE).astype(BF)
                bias = bias_ref[first, hk] if sb == 0 else bias_ref[0, hk]
                p, p_sink = _attn_probs(kw, qc, bias, sink_ref[hk])
                doc = jnp.concatenate([do_ref[(GRP * hk + g) * HD:(GRP * hk + g + 1) * HD, cols] for g in range(GRP)], axis=1)
                dp = _dot_tn(vw, doc)
                delta = jnp.sum(p * dp, axis=0, keepdims=True)
                ds = p * (dp - delta)
                dsink_ref[hk] += -(p_sink * delta)
                dsacc_ref[hk] += ds
                dsb = ds.astype(BF)
                dqc = _dot(kw, dsb) * QK_SCALE
                ckv_ref[sb, hk * HD:(hk + 1) * HD, :] = _dot_nt(qc, dsb)
                ckv_ref[sb, NKV * HD + hk * HD:NKV * HD + (hk + 1) * HD, :] = _dot_nt(doc, p.astype(BF))
                for g in range(GRP):
                    head = GRP * hk + g
                    dqn = dqc[:, g * BLK:(g + 1) * BLK]
                    dqh = dqn * qgain
                    dq = qr[g] * (dqh - qh[g] * jnp.mean(dqh * qh[g], axis=0, keepdims=True))
                    dq_ref[head * HD:(head + 1) * HD, cols] = dq.astype(BF)
                    dqg = dqg + dqn * qh[g]
        qg_scr[...] += dqg

        @pl.when(i == n_tiles - 1)
        def _():
            dqg_ref[...] = jnp.sum(qg_scr[...], axis=1, keepdims=True)

    return _call(
        body, deps, (qkv_t, qkv_t, do_t, qg, kg, sink_rows, bias_t), grid=(n_tiles,),
        in_specs=_attn_specs(t, tq)[:2] + [pl.BlockSpec((D, tq), lambda i: (0, i))] + _attn_specs(t, tq)[2:],
        out_specs=[pl.BlockSpec((D, tq), lambda i: (0, i)),
                   pl.BlockSpec((n_sub, 2 * NKV * HD, 2 * BLK), lambda i: (i, 0, 0)),
                   pl.BlockSpec((HD, 1), lambda i: (0, 0)),
                   pl.BlockSpec((NKV, 1, GRP * BLK), lambda i: (0, 0, 0)),
                   pl.BlockSpec((NKV, 2 * BLK, GRP * BLK), lambda i: (0, 0, 0))],
        out_shape=[jax.ShapeDtypeStruct((D, t), BF),
                   jax.ShapeDtypeStruct((t // BLK, 2 * NKV * HD, 2 * BLK), F32),
                   jax.ShapeDtypeStruct((HD, 1), F32),
                   jax.ShapeDtypeStruct((NKV, 1, GRP * BLK), F32),
                   jax.ShapeDtypeStruct((NKV, 2 * BLK, GRP * BLK), F32)],
        scratch_shapes=[pltpu.VMEM((HD, BLK), F32)],
        compiler_params=_params(1), name="attn_bwd")


def _kv_combine(ckv, qkv_t, kg):
    nb = ckv.shape[0]
    t = nb * BLK
    rows = NKV * HD
    per = min(4, nb)
    steps = nb // per

    def body(c_ref, cn_ref, k_ref, kg_ref, o_ref, dkg_ref, kg_scr):
        n = pl.program_id(0)

        @pl.when(n == 0)
        def _():
            kg_scr[...] = jnp.zeros_like(kg_scr)

        has_next = (n < steps - 1).astype(F32)
        kgain = kg_ref[...]
        dkg = jnp.zeros((HD, BLK), F32)
        for s in range(per):
            cols = slice(s * BLK, (s + 1) * BLK)
            after = c_ref[s + 1, :, :BLK] if s + 1 < per else cn_ref[0, :, :BLK] * has_next
            d = c_ref[s, :, BLK:] + after
            o_ref[rows:, cols] = d[rows:, :].astype(BF)
            for h in range(NKV):
                _, r, kh = _norm_rows(k_ref[h * HD:(h + 1) * HD, cols].astype(F32), kgain)
                dkn = d[h * HD:(h + 1) * HD, :]
                dkh = dkn * kgain
                o_ref[h * HD:(h + 1) * HD, cols] = (r * (dkh - kh * jnp.mean(dkh * kh, axis=0, keepdims=True))).astype(BF)
                dkg = dkg + dkn * kh
        kg_scr[...] += dkg

        @pl.when(n == steps - 1)
        def _():
            dkg_ref[...] = jnp.sum(kg_scr[...], axis=1, keepdims=True)

    return pl.pallas_call(
        body, grid=(steps,),
        in_specs=[pl.BlockSpec((per, 2 * rows, 2 * BLK), lambda n: (n, 0, 0)),
                  pl.BlockSpec((1, 2 * rows, 2 * BLK), lambda n: (jnp.minimum((n + 1) * per, nb - 1), 0, 0)),
                  pl.BlockSpec((rows, per * BLK), lambda n: (D // rows, n)),
                  _resident((HD, 1))],
        out_specs=[pl.BlockSpec((2 * rows, per * BLK), lambda n: (0, n)), pl.BlockSpec((HD, 1), lambda n: (0, 0))],
        out_shape=[jax.ShapeDtypeStruct((2 * rows, t), BF), jax.ShapeDtypeStruct((HD, 1), F32)],
        scratch_shapes=[pltpu.VMEM((HD, BLK), F32)],
        compiler_params=_params(1), name="kv_combine")(ckv, ckv, qkv_t, kg)


def _group_lane_sums(v):
    lane_group = lax.broadcasted_iota(jnp.int32, (1, GRP * BLK), 1) // BLK
    col = lax.broadcasted_iota(jnp.int32, (1, BLK), 1)
    out = jnp.zeros((NKV, BLK), F32)
    for g in range(GRP):
        s = jnp.sum(jnp.where(lane_group == g, v, 0.0), axis=1, keepdims=True)
        out = jnp.where(col == g, s, out)
    return out


def _bias_grad(dsacc, onehot_t):
    per = 8

    def body(ds_ref, oh_ref, o_ref):
        for b in range(per):
            oh = jnp.concatenate([oh_ref[b]] * GRP, axis=1)
            o_ref[b] = _group_lane_sums(jnp.sum(ds_ref[...] * oh[None], axis=1))

    return pl.pallas_call(
        body, grid=(NBUCKET // per,),
        in_specs=[_resident((NKV, 2 * BLK, GRP * BLK)), pl.BlockSpec((per, 2 * BLK, BLK), lambda b: (b, 0, 0))],
        out_specs=pl.BlockSpec((per, NKV, BLK), lambda b: (b, 0, 0)),
        out_shape=jax.ShapeDtypeStruct((NBUCKET, NKV, BLK), F32),
        compiler_params=_params(1), name="bias_grad")(dsacc, onehot_t)


def _sink_grad(dsink_rows):
    def body(d_ref, o_ref):
        o_ref[...] = _group_lane_sums(d_ref[:, 0, :])

    return pl.pallas_call(body, out_shape=jax.ShapeDtypeStruct((NKV, BLK), F32), name="sink_grad")(dsink_rows)


def _mix_out(zs, o_t, gp, x, w_cp, w_o, w_out):
    t = x.shape[0]
    tm = min(ROW_TILE_WIDE, t)

    def body(zs_ref, ot_ref, gp_ref, x_ref, wcp_ref, wo_ref, wout_ref, xo_ref, a_ref, b_ref, m_ref):
        a = _dot(zs_ref[...], wcp_ref[...])
        b = _dot_tn(ot_ref[...], wo_ref[...])
        a_ref[...] = a.astype(BF)
        b_ref[...] = b.astype(BF)
        merged = (_sig(gp_ref[:, :D].astype(F32)) * a + _sig(gp_ref[:, D:].astype(F32)) * b).astype(BF)
        m_ref[...] = merged
        xo_ref[...] = x_ref[...] + _dot(merged, wout_ref[...])

    return pl.pallas_call(
        body, grid=(t // tm,),
        in_specs=[_row_tile(tm, D), pl.BlockSpec((D, tm), lambda i: (0, i)), _row_tile(tm, 2 * D), _row_tile(tm, D),
                  _resident((D, D)), _resident((D, D)), _resident((D, D))],
        out_specs=[_row_tile(tm, D)] * 4,
        out_shape=[jax.ShapeDtypeStruct((t, D), F32)] + [jax.ShapeDtypeStruct((t, D), BF)] * 3,
        compiler_params=_params(1), name="mix_out")(zs, o_t, gp, x, w_cp, w_o, w_out)


def _mix_out_bwd(dx, a, b, gp, w_cp, w_o, w_out, deps=()):
    t = dx.shape[0]
    tm = min(ROW_TILE_WIDE, t)

    def body(dx_ref, a_ref, b_ref, gp_ref, wcp_ref, wo_ref, wout_ref, dzs_ref, dot_ref, dgp_ref, da_ref, db_ref, dxb_ref):
        dxb = dx_ref[...].astype(BF)
        dxb_ref[...] = dxb
        dm = _dot_nt(dxb, wout_ref[...])
        gc = _sig(gp_ref[:, :D].astype(F32))
        ga = _sig(gp_ref[:, D:].astype(F32))
        da = (dm * gc).astype(BF)
        db = (dm * ga).astype(BF)
        da_ref[...] = da
        db_ref[...] = db
        dgp_ref[:, :D] = (dm * a_ref[...].astype(F32) * gc * (1.0 - gc)).astype(BF)
        dgp_ref[:, D:] = (dm * b_ref[...].astype(F32) * ga * (1.0 - ga)).astype(BF)
        dzs_ref[...] = _dot_nt(da, wcp_ref[...])
        dot_ref[...] = _dot_nt(wo_ref[...], db).astype(BF)

    return _call(
        body, deps, (dx, a, b, gp, w_cp, w_o, w_out), grid=(t // tm,),
        in_specs=[_row_tile(tm, D), _row_tile(tm, D), _row_tile(tm, D), _row_tile(tm, 2 * D),
                  _resident((D, D)), _resident((D, D)), _resident((D, D))],
        out_specs=[_row_tile(tm, D), pl.BlockSpec((D, tm), lambda i: (0, i)), _row_tile(tm, 2 * D),
                   _row_tile(tm, D), _row_tile(tm, D), _row_tile(tm, D)],
        out_shape=[jax.ShapeDtypeStruct((t, D), F32), jax.ShapeDtypeStruct((D, t), BF), jax.ShapeDtypeStruct((t, 2 * D), BF),
                   jax.ShapeDtypeStruct((t, D), BF), jax.ShapeDtypeStruct((t, D), BF), jax.ShapeDtypeStruct((t, D), BF)],
        compiler_params=_params(1), name="mix_out_bwd")


def _mix_proj_bwd(dxo, duc, dq_t, dkv_t, dgp, x, g, w_t):
    t = x.shape[0]
    tm = min(ROW_TILE_WIDE, t)

    def body(dxo_ref, duc_ref, dq_ref, dkv_ref, dgp_ref, x_ref, g_ref, w_ref, dx_ref, dg_ref):
        dn = _dot(duc_ref[...], w_ref[R_CONV[0]:R_CONV[1], :])
        dn = dn + _dot(dgp_ref[...], w_ref[R_GATE[0]:R_GATE[1], :])
        dn = dn + _dot_tn(dq_ref[...], w_ref[R_Q[0]:R_Q[1], :])
        dn = dn + _dot_tn(dkv_ref[...], w_ref[R_KV[0]:R_KV[1], :])
        dx, dg = _rms_bwd(dn, x_ref[...], g_ref[...])
        dx_ref[...] = dxo_ref[...] + dx

        @pl.when(pl.program_id(0) == 0)
        def _():
            dg_ref[...] = jnp.zeros_like(dg_ref)

        dg_ref[...] += dg

    return pl.pallas_call(
        body, grid=(t // tm,),
        in_specs=[_row_tile(tm, D), _row_tile(tm, 2 * D), pl.BlockSpec((D, tm), lambda i: (0, i)),
                  pl.BlockSpec((2 * NKV * HD, tm), lambda i: (0, i)), _row_tile(tm, 2 * D), _row_tile(tm, D),
                  _resident((1, D)), _resident((INW, D))],
        out_specs=[_row_tile(tm, D), pl.BlockSpec((1, D), lambda i: (0, 0))],
        out_shape=[jax.ShapeDtypeStruct((t, D), F32), jax.ShapeDtypeStruct((1, D), F32)],
        compiler_params=_params(1), name="mix_proj_bwd")(dxo, duc, dq_t, dkv_t, dgp, x, g, w_t)


def _attention_tables():
    kj = np.arange(2 * BLK)[:, None]
    qi = np.arange(BLK)[None, :]
    dist = qi + BLK - kj
    in_win = (dist >= 0) & (dist < BLK)
    dpos = np.maximum(dist, 0)
    max_exact = NBUCKET // 2
    dfl = np.maximum(dpos, 1).astype(np.float32)
    large = max_exact + (np.log(dfl / np.float32(max_exact)) / np.float32(math.log(BLK / max_exact))
                         * np.float32(NBUCKET - max_exact)).astype(np.int32)
    large = np.minimum(large, NBUCKET - 1)
    bucket = np.where(dpos < max_exact, dpos, large)
    onehot = (bucket[None] == np.arange(NBUCKET)[:, None, None]).astype(np.float32)
    mask = in_win.astype(np.float32)
    mask_first = mask * (kj >= BLK)
    masks = np.stack([np.tile(mask, (1, GRP)), np.tile(mask_first, (1, GRP))])
    return onehot, masks


def _bias_table(rel_bias, onehot):
    tab = jnp.einsum("bkq,bh->hkq", onehot, rel_bias, precision=lax.Precision.HIGHEST)
    tab = tab.reshape(NKV, GRP, 2 * BLK, BLK)
    return jnp.transpose(tab, (0, 2, 1, 3)).reshape(NKV, 2 * BLK, GRP * BLK)


def _local_step(x, target, vec, weights_of, wgrad, grads_done, small_done):
    onehot_np, masks_np = _attention_tables()
    onehot = jnp.asarray(onehot_np)
    masks = jnp.asarray(masks_np)
    bias_t = jnp.where(masks[:, None] > 0.5, _bias_table(vec["rel_bias"], onehot)[None], NEG)
    sink_rows = jnp.repeat(vec["attn_sinks"].reshape(NKV, 1, GRP), BLK, axis=2)
    qg = vec["q_norm"].reshape(HD, 1)
    kg = vec["k_norm"].reshape(HD, 1)
    g1 = vec["ffn1_norm"].reshape(1, D)
    gm = vec["mix_norm"].reshape(1, D)
    g2 = vec["ffn2_norm"].reshape(1, D)
    dwb = vec["conv_dw_bias"].reshape(1, D)
    lng = vec["conv_ln_g"].reshape(1, D)
    lnb = vec["conv_ln_b"].reshape(1, D)

    w1 = weights_of("ffn1_in", (bias_t, sink_rows))
    n1, u1 = _ffn_up(x, g1, w1["ffn1_w_in"], "ffn1_up")
    w1.update(weights_of("ffn1_out", (u1,)))
    x1 = _ffn_down(x, u1, w1["ffn1_w_out"], "ffn1_down")
    wm = weights_of("mix", (x1,))
    dwk = jnp.pad(wm["conv_dw_kernel"], ((0, CWP - CW), (0, 0)))
    hm, uc, gp, qkv_t = _mix_proj(x1, gm, wm["w_in"])
    zs, zc = _conv_fwd(uc, dwk, dwb, lng, lnb)
    o_t = _attn_fwd(qkv_t, qg, kg, sink_rows, bias_t)
    x2, a, b, merged = _mix_out(zs, o_t, gp, x1, wm["conv_w_proj"], wm["attn_w_o"], wm["w_out"])
    w2 = weights_of("ffn2", (x2,))
    n2, u2, dx3, sq = _ffn_fwd(x2, g2, w2["ffn2_w_in"], w2["ffn2_w_out"], "ffn2_fwd", target=target)

    gv = {}
    dx2, du2, h2, dy2, gv["ffn2_norm"] = _ffn_bwd(dx3, x2, g2, u2, w2["ffn2_w_in"], w2["ffn2_w_out"], "ffn2_bwd")
    deps = grads_done("ffn2", {"ffn2_w_in": wgrad(du2, n2, "ffn2_dw_in", False),
                               "ffn2_w_out": wgrad(h2, dy2, "ffn2_dw_out", False)})

    dzs, do_t, dgp, da, db, dx2b = _mix_out_bwd(dx2, a, b, gp, wm["conv_w_proj"], wm["attn_w_o"], wm["w_out"], deps=deps)
    deps = grads_done("mix_out", {"w_out": wgrad(merged, dx2b, "mix_dw_out", False),
                                  "conv_w_proj": wgrad(zs, da, "mix_dw_cp", False),
                                  "attn_w_o": wgrad(o_t, db, "mix_dw_o", True)})

    dq_t, ckv, dqg, dsink_rows, dsacc = _attn_bwd(qkv_t, do_t, qg, kg, sink_rows, bias_t, deps=deps)
    dkv_t, dkg = _kv_combine(ckv, qkv_t, kg)
    gv["q_norm"] = dqg.reshape(HD)
    gv["k_norm"] = dkg.reshape(HD)
    gv["attn_sinks"] = _sink_grad(dsink_rows)[:, :GRP].reshape(NQ)
    gv["rel_bias"] = _bias_grad(dsacc, onehot)[:, :, :GRP].reshape(NBUCKET, NQ)

    duc, dk_conv, gv["conv_dw_bias"], gv["conv_ln_g"], gv["conv_ln_b"] = _conv_bwd(uc, zc, dzs, dwk, lng, lnb)
    gv["conv_dw_kernel"] = dk_conv[:CW]

    dx1, gv["mix_norm"] = _mix_proj_bwd(dx2, duc, dq_t, dkv_t, dgp, x1, gm, wm["w_in"])
    deps = grads_done("mix_in", {"w_in": _wgrad_mix(duc, dq_t, dkv_t, dgp, hm)})

    dx0, du1, h1, dy1, gv["ffn1_norm"] = _ffn_bwd(dx1, x, g1, u1, w1["ffn1_w_in"], w1["ffn1_w_out"], "ffn1_bwd", deps=deps)
    for k in ("ffn1_norm", "mix_norm", "ffn2_norm", "conv_dw_bias", "conv_ln_g", "conv_ln_b"):
        gv[k] = gv[k].reshape(D)
    deps = small_done(gv, sq)
    deps = grads_done("ffn1_in", {"ffn1_w_in": wgrad(du1, n1, "ffn1_dw_in", False, deps)})
    grads_done("ffn1_out", {"ffn1_w_out": wgrad(h1, dy1, "ffn1_dw_out", False, deps)})
    return dx0


MESH_ID = pl.DeviceIdType.MESH


def _position():
    return lax.axis_index("x"), lax.axis_index("y"), lax.axis_index("c")


def _shard_rows(ref, index, rows):
    return ref.at[pl.ds(pl.multiple_of(index * rows, 16), rows), :]


def _prep(weights, taps, me):
    n = len(weights)

    def body(me_ref, *refs):
        for k in range(n):
            refs[n + 1 + k][...] = refs[k][...].astype(BF)
        refs[2 * n + 1][0:CW, :] = refs[n][...]
        refs[2 * n + 1][CW:, :] = jnp.zeros((CWP - CW, BLK), F32)

    shard_shapes = [w.shape for w in weights] + [(CWP, BLK)]
    dtypes = [BF] * n + [F32]
    ins = list(weights) + [taps]
    return pl.pallas_call(
        body,
        grid_spec=pltpu.PrefetchScalarGridSpec(
            num_scalar_prefetch=1, grid=(1,),
            in_specs=[pl.BlockSpec(a.shape, lambda i, m: (0, 0), pipeline_mode=pl.Buffered(1)) for a in ins],
            out_specs=[pl.BlockSpec(s, lambda i, m: (m[0], 0)) for s in shard_shapes]),
        out_shape=[jax.ShapeDtypeStruct((N_DEV * s[0], s[1]), d) for s, d in zip(shard_shapes, dtypes)],
        compiler_params=_params(1), name="prep")(me, *ins)


HBM = pl.BlockSpec(memory_space=pltpu.HBM)
SEM = pl.BlockSpec(memory_space=pltpu.SEMAPHORE)
DATAFLOW = pltpu.SideEffectType.DATAFLOW_SIDE_EFFECTING
TOKEN = jax.ShapeDtypeStruct((8, 128), F32)


def _in_hbm(x):
    return pltpu.with_memory_space_constraint(x, pltpu.HBM)


def _hbm_like(arrays):
    return [pltpu.HBM(a.shape, a.dtype) for a in arrays]


def _other_chips(x, y):
    return [(1 - x, y), (x, 1 - y), (1 - x, 1 - y)]


def _device_index(chip, c):
    return 4 * chip[0] + 2 * chip[1] + c


def _chip_index(chip):
    return 2 * chip[0] + chip[1]


class _Exchange:
    def __init__(self, gather, all_cores=False):
        self.gather = gather
        self.all_cores = all_cores
        self.n_peers = N_DEV - 1 if all_cores else 3

    def peers(self, x, y, c):
        if self.all_cores:
            return [(x ^ (k >> 2), y ^ ((k >> 1) & 1), c ^ (k & 1)) for k in range(1, N_DEV)]
        return [(*chip, c) for chip in _other_chips(x, y)]

    def sent(self, x, y, c, peer):
        return _device_index((x, y), c) if self.gather else _chip_index(peer[:2])

    def lands_at(self, x, y, c):
        return _device_index((x, y), c) if self.gather else _chip_index((x, y))

    def arrives_at(self, peer):
        return _device_index(peer[:2], peer[2]) if self.gather else _chip_index(peer[:2])


def _ici_copies_start(sets, sources, landings, exchanges, name, deps=()):
    n = len(landings)
    arrays = (list(sources) if sources is not None else []) + list(landings)
    first_land = len(arrays) - n
    n_sets = len(sets)
    n_deps = len(deps)

    def body(*refs):
        refs = refs[n_deps:]
        src, land = refs[:n], refs[first_land:first_land + n]
        sems = refs[len(arrays):len(arrays) + 2 * n_sets]
        token = refs[-1]
        x, y, c = _position()
        for s, (members, exchange) in enumerate(zip(sets, exchanges)):
            for slot, (k, rows) in enumerate(members):
                for j, peer in enumerate(exchange.peers(x, y, c)):
                    at = exchange.n_peers * slot + j
                    pltpu.make_async_remote_copy(
                        src_ref=_shard_rows(src[k], exchange.sent(x, y, c, peer), rows),
                        dst_ref=_shard_rows(land[k], exchange.lands_at(x, y, c), rows),
                        send_sem=sems[2 * s].at[at], recv_sem=sems[2 * s + 1].at[at],
                        device_id=peer, device_id_type=MESH_ID).start()
        token[...] = jnp.zeros_like(token)

    sem_shapes = []
    for members, exchange in zip(sets, exchanges):
        sem_shapes += [pltpu.SemaphoreType.DMA((exchange.n_peers * len(members),))] * 2
    out = pl.pallas_call(
        body, name=name,
        out_shape=sem_shapes + _hbm_like(arrays) + [TOKEN],
        in_specs=[ANY] * n_deps + [HBM] * len(arrays),
        out_specs=[SEM] * (2 * n_sets) + [HBM] * len(arrays) + [pl.BlockSpec(memory_space=pltpu.VMEM)],
        input_output_aliases={n_deps + i: 2 * n_sets + i for i in range(len(arrays))},
        compiler_params=pltpu.CompilerParams(has_side_effects=DATAFLOW),
    )(*deps, *[_in_hbm(a) for a in arrays])
    sems = [(out[2 * s], out[2 * s + 1]) for s in range(n_sets)]
    thru = list(out[2 * n_sets:2 * n_sets + len(arrays)])
    return sems, (thru[:first_land] if sources is not None else None), thru[first_land:], out[-1]


def _ici_copies_wait(sems, members, sources, landings, exchange, after, name):
    n = len(landings)
    arrays = (list(sources) if sources is not None else []) + list(landings)
    first_land = len(arrays) - n

    def body(*refs):
        src, land = refs[:n], refs[first_land:first_land + n]
        send_sems, recv_sems = refs[len(arrays)], refs[len(arrays) + 1]
        x, y, c = _position()
        for slot, rows in enumerate(members):
            for j, peer in enumerate(exchange.peers(x, y, c)):
                at = exchange.n_peers * slot + j
                cp = pltpu.make_async_remote_copy(
                    src_ref=_shard_rows(src[slot], exchange.sent(x, y, c, peer), rows),
                    dst_ref=_shard_rows(land[slot], exchange.arrives_at(peer), rows),
                    send_sem=send_sems.at[at], recv_sem=recv_sems.at[at], device_id=peer, device_id_type=MESH_ID)
                cp.wait_send()
                cp.wait_recv()

    out = pl.pallas_call(
        body, name=name, out_shape=_hbm_like(arrays),
        in_specs=[HBM] * len(arrays) + [SEM, SEM] + [ANY] * len(after), out_specs=[HBM] * len(arrays),
        input_output_aliases={i: i for i in range(len(arrays))},
        compiler_params=pltpu.CompilerParams(has_side_effects=DATAFLOW),
    )(*arrays, sems[0], sems[1], *after)
    return list(out[first_land:])


def _d2d_gather(buffers, rows, name):
    n = len(buffers)

    def body(*refs):
        land = refs[n:2 * n]
        send_sems, recv_sems = refs[2 * n:]
        x, y, c = _position()
        chips = [(x, y)] + _other_chips(x, y)
        sends, recvs = [], []
        for k in range(n):
            for j, chip in enumerate(chips):
                for copies, core in ((sends, c), (recvs, 1 - c)):
                    block = _shard_rows(land[k], _device_index(chip, core), rows[k])
                    copies.append(pltpu.make_async_remote_copy(
                        src_ref=block, dst_ref=block, send_sem=send_sems.at[k, j], recv_sem=recv_sems.at[k, j],
                        device_id=(x, y, 1 - c), device_id_type=MESH_ID))
        for cp in sends:
            cp.start()
        for cp in recvs:
            cp.wait_recv()
        for cp in sends:
            cp.wait_send()

    return pl.pallas_call(
        body, name=name, out_shape=[jax.ShapeDtypeStruct(a.shape, a.dtype) for a in buffers],
        in_specs=[ANY] * n, out_specs=[ANY] * n, input_output_aliases={i: i for i in range(n)},
        scratch_shapes=[pltpu.SemaphoreType.DMA((n, 4)), pltpu.SemaphoreType.DMA((n, 4))],
    )(*buffers)


def _rs_pair(grads, name):
    n = len(grads)
    rows = [g.shape[0] // N_DEV for g in grads]

    def body(*refs):
        ins, outs = refs[:n], refs[n:2 * n]
        send_sems, recv_sems = refs[2 * n:]
        x, y, c = _position()
        copies = []
        for k in range(n):
            for q in range(4):
                copies.append(pltpu.make_async_remote_copy(
                    src_ref=_shard_rows(ins[k], 2 * q + 1 - c, rows[k]), dst_ref=_shard_rows(outs[k], q, rows[k]),
                    send_sem=send_sems.at[k, q], recv_sem=recv_sems.at[k, q], device_id=(x, y, 1 - c),
                    device_id_type=MESH_ID))
        for cp in copies:
            cp.start()
        for cp in copies:
            cp.wait()

    return pl.pallas_call(
        body, out_shape=[jax.ShapeDtypeStruct((4 * r, g.shape[1]), g.dtype) for g, r in zip(grads, rows)],
        in_specs=[ANY] * n, out_specs=[ANY] * n,
        scratch_shapes=[pltpu.SemaphoreType.DMA((n, 4)), pltpu.SemaphoreType.DMA((n, 4))],
        name=name)(*grads)


def _wgrad_pair(lhs, rhs, name, *, lhs_is_transposed, deps=()):
    t = rhs.shape[0]
    n = lhs.shape[0] if lhs_is_transposed else lhs.shape[1]
    r = n // N_DEV
    n_chips = N_DEV // 2
    per = 1 if (2 * r) % BLK == 0 else 2
    steps = n_chips // per

    def body(l_ref, r_ref, kept_ref, recv_ref, res, send_sems, recv_sems):
        q = pl.program_id(0)
        slot = q % 2
        x, y, c = _position()

        def send(step, buf, i):
            return pltpu.make_async_remote_copy(
                src_ref=res.at[buf, pl.ds(pl.multiple_of((2 * i + 1 - c) * r, 16), r), :],
                dst_ref=_shard_rows(recv_ref, step * per + i, r),
                send_sem=send_sems.at[buf, i], recv_sem=recv_sems.at[step * per + i],
                device_id=(x, y, 1 - c), device_id_type=MESH_ID)

        @pl.when(q >= 2)
        def _():
            for i in range(per):
                send(q - 2, slot, i).wait_send()

        if lhs_is_transposed:
            res[slot] = _dot(l_ref[...], r_ref[...]).astype(BF)
        else:
            res[slot] = _dot_tn(l_ref[...], r_ref[...]).astype(BF)
        for i in range(per):
            kept_ref[i * r:(i + 1) * r, :] = res[slot, pl.ds(pl.multiple_of((2 * i + c) * r, 16), r), :]
            send(q, slot, i).start()

        @pl.when(q == steps - 1)
        def _():
            for i in range(per):
                if steps > 1:
                    send(q - 1, 1 - slot, i).wait_send()
                send(q, slot, i).wait_send()
            for chip in range(n_chips):
                send(chip // per, 0, chip % per).wait_recv()

    width = 2 * r * per
    lhs_spec = pl.BlockSpec((width, t), lambda q: (q, 0)) if lhs_is_transposed else pl.BlockSpec((t, width), lambda q: (0, q))
    return _call(
        body, deps, (lhs, rhs), grid=(steps,),
        in_specs=[lhs_spec, _resident((t, D))],
        out_specs=[pl.BlockSpec((per * r, D), lambda q: (q, 0)), ANY],
        out_shape=[jax.ShapeDtypeStruct((n // 2, D), BF)] * 2,
        scratch_shapes=[pltpu.VMEM((2, width, D), BF), pltpu.SemaphoreType.DMA((2, per)),
                        pltpu.SemaphoreType.DMA((n_chips,))],
        compiler_params=_params(1), name=name)


def _pair_add(grad, received, place, name, kept_only=False):
    r = received.shape[0] // 4
    tr = 352 if r % 352 == 0 else r
    per = r // tr
    parity = 0 if kept_only else 1

    def body(place_ref, g_ref, r_ref, o_ref, land_ref):
        total = (g_ref[...].astype(F32) + r_ref[...].astype(F32)).astype(BF)
        o_ref[...] = total

        @pl.when(pl.program_id(1) == place_ref[1])
        def _():
            land_ref[...] = total

    return pl.pallas_call(
        body,
        grid_spec=pltpu.PrefetchScalarGridSpec(
            num_scalar_prefetch=1, grid=(per, 4),
            in_specs=[pl.BlockSpec((tr, D), lambda i, q, p: (((1 + parity) * q + parity * p[0]) * per + i, 0)),
                      pl.BlockSpec((tr, D), lambda i, q, p: (q * per + i, 0))],
            out_specs=[pl.BlockSpec((tr, D), lambda i, q, p: (q * per + i, 0)),
                       pl.BlockSpec((tr, D), lambda i, q, p: (p[1] * per + i, 0))]),
        out_shape=[jax.ShapeDtypeStruct(received.shape, BF)] * 2,
        compiler_params=_params(2), name=name)(place, grad, received)


def _sum_blocks(gathered, rows):
    def body(b_ref, o_ref):
        acc = b_ref[0:rows, :]
        for d in range(1, N_DEV):
            acc = acc + b_ref[d * rows:(d + 1) * rows, :]
        o_ref[...] = acc

    return pl.pallas_call(body, out_shape=jax.ShapeDtypeStruct((rows, D), F32), name="small_sum")(gathered)


def _adamw_math(w, g, m, v):
    m = ADAM_B1 * m + (1.0 - ADAM_B1) * g
    v = ADAM_B2 * v + (1.0 - ADAM_B2) * (g * g)
    m_hat = m / (1.0 - ADAM_B1 ** ADAM_STEP)
    v_hat = v / (1.0 - ADAM_B2 ** ADAM_STEP)
    delta = -ADAM_LR * (m_hat / (jnp.sqrt(v_hat) + ADAM_EPS) + ADAM_WD * w)
    return delta, m, v


def _sum_partials(blocks):
    g = blocks[0].astype(F32)
    for blk in blocks[1:]:
        g = g + blk.astype(F32)
    return g


def _reduce_adamw(landed, w, m, v, name):
    r = w.shape[0]
    tr = 352 if r % 352 == 0 else r
    per = r // tr

    def body(r0, r1, r2, r3, w_ref, m_ref, v_ref, g_ref, d_ref, nm_ref, nv_ref):
        g = _sum_partials([r0[...], r1[...], r2[...], r3[...]])
        g_ref[...] = g
        d_ref[...], nm_ref[...], nv_ref[...] = _adamw_math(w_ref[...], g, m_ref[...], v_ref[...])

    tile = _row_tile(tr, D)
    return pl.pallas_call(
        body, grid=(per,),
        in_specs=[pl.BlockSpec((tr, D), lambda i, q=q: (q * per + i, 0)) for q in range(4)] + [tile] * 3,
        out_specs=[tile] * 4, out_shape=[jax.ShapeDtypeStruct(w.shape, F32)] * 4,
        compiler_params=_params(1), name=name)(landed, landed, landed, landed, w, m, v)


def _adamw_small(w, g, m, v, name):
    def body(w_ref, g_ref, m_ref, v_ref, d_ref, nm_ref, nv_ref):
        d_ref[...], nm_ref[...], nv_ref[...] = _adamw_math(w_ref[...], g_ref[...], m_ref[...], v_ref[...])

    return pl.pallas_call(body, out_shape=[jax.ShapeDtypeStruct(w.shape, F32)] * 3, name=name)(w, g, m, v)


WEIGHTS = ("ffn1_norm", "ffn1_w_in", "ffn1_w_out", "mix_norm", "w_in", "conv_dw_kernel", "conv_dw_bias", "conv_ln_g",
           "conv_ln_b", "conv_w_proj", "q_norm", "k_norm", "attn_sinks", "rel_bias", "attn_w_o", "w_out", "ffn2_norm",
           "ffn2_w_in", "ffn2_w_out")
MATRICES = ("ffn1_w_in", "ffn1_w_out", "w_in", "conv_w_proj", "attn_w_o", "w_out", "ffn2_w_in", "ffn2_w_out")
COLUMN_SHARDED = ("ffn1_w_in", "w_in", "ffn2_w_in")
ROW_VECTORS = ("ffn1_norm", "mix_norm", "conv_dw_bias", "conv_ln_g", "conv_ln_b", "ffn2_norm")
PACKED = (("q_norm", HD), ("k_norm", HD), ("attn_sinks", NQ), ("rel_bias", NBUCKET * NQ))
GATHER = _Exchange(gather=True)
GATHER_ALL = _Exchange(gather=True, all_cores=True)
SCATTER = _Exchange(gather=False)
GATHER_STAGES = ("ffn1_in", "ffn1_out", "mix", "ffn2")
STAGE_GATHER = {"ffn1_in": GATHER, "ffn1_out": GATHER, "mix": GATHER, "ffn2": GATHER_ALL}
STAGE_MEMBERS = {"ffn1_in": ("ffn1_w_in",), "ffn1_out": ("ffn1_w_out",),
                 "mix": ("w_in", "conv_w_proj", "attn_w_o", "w_out", "taps"), "ffn2": ("ffn2_w_in", "ffn2_w_out")}
ROW_PACKED = len(ROW_VECTORS)
ROW_LOSS = ROW_PACKED + 1
ROW_TAPS = 8
PAYLOAD_ROWS = 48


def _pack_small(values, last_row):
    packed = jnp.concatenate([values[k].reshape(-1) for k, _ in PACKED])
    packed = jnp.pad(packed, (0, D - packed.shape[0])).reshape(1, D)
    return jnp.concatenate([values[k].reshape(1, D) for k in ROW_VECTORS] + [packed, last_row], axis=0)


def _unpack_small(rows):
    out = {k: rows[i] for i, k in enumerate(ROW_VECTORS)}
    at = 0
    for k, size in PACKED:
        out[k] = rows[ROW_PACKED, at:at + size]
        at += size
    out["rel_bias"] = out["rel_bias"].reshape(NBUCKET, NQ)
    return out


def kernel(x, ffn1_norm, ffn1_w_in, ffn1_w_out, mix_norm, w_in, conv_dw_kernel, conv_dw_bias, conv_ln_g, conv_ln_b, conv_w_proj, q_norm, k_norm, attn_sinks, rel_bias, attn_w_o, w_out, ffn2_norm, ffn2_w_in, ffn2_w_out, loss_target, m_ffn1_norm, m_ffn1_w_in, m_ffn1_w_out, m_mix_norm, m_w_in, m_conv_dw_kernel, m_conv_dw_bias, m_conv_ln_g, m_conv_ln_b, m_conv_w_proj, m_q_norm, m_k_norm, m_attn_sinks, m_rel_bias, m_attn_w_o, m_w_out, m_ffn2_norm, m_ffn2_w_in, m_ffn2_w_out, v_ffn1_norm, v_ffn1_w_in, v_ffn1_w_out, v_mix_norm, v_w_in, v_conv_dw_kernel, v_conv_dw_bias, v_conv_ln_g, v_conv_ln_b, v_conv_w_proj, v_q_norm, v_k_norm, v_attn_sinks, v_rel_bias, v_attn_w_o, v_w_out, v_ffn2_norm, v_ffn2_w_in, v_ffn2_w_out):
    w = dict(ffn1_norm=ffn1_norm, ffn1_w_in=ffn1_w_in, ffn1_w_out=ffn1_w_out, mix_norm=mix_norm, w_in=w_in,
             conv_dw_kernel=conv_dw_kernel, conv_dw_bias=conv_dw_bias, conv_ln_g=conv_ln_g, conv_ln_b=conv_ln_b,
             conv_w_proj=conv_w_proj, q_norm=q_norm, k_norm=k_norm, attn_sinks=attn_sinks, rel_bias=rel_bias,
             attn_w_o=attn_w_o, w_out=w_out, ffn2_norm=ffn2_norm, ffn2_w_in=ffn2_w_in, ffn2_w_out=ffn2_w_out)
    m = dict(ffn1_norm=m_ffn1_norm, ffn1_w_in=m_ffn1_w_in, ffn1_w_out=m_ffn1_w_out, mix_norm=m_mix_norm, w_in=m_w_in,
             conv_dw_kernel=m_conv_dw_kernel, conv_dw_bias=m_conv_dw_bias, conv_ln_g=m_conv_ln_g, conv_ln_b=m_conv_ln_b,
             conv_w_proj=m_conv_w_proj, q_norm=m_q_norm, k_norm=m_k_norm, attn_sinks=m_attn_sinks, rel_bias=m_rel_bias,
             attn_w_o=m_attn_w_o, w_out=m_w_out, ffn2_norm=m_ffn2_norm, ffn2_w_in=m_ffn2_w_in, ffn2_w_out=m_ffn2_w_out)
    v = dict(ffn1_norm=v_ffn1_norm, ffn1_w_in=v_ffn1_w_in, ffn1_w_out=v_ffn1_w_out, mix_norm=v_mix_norm, w_in=v_w_in,
             conv_dw_kernel=v_conv_dw_kernel, conv_dw_bias=v_conv_dw_bias, conv_ln_g=v_conv_ln_g, conv_ln_b=v_conv_ln_b,
             conv_w_proj=v_conv_w_proj, q_norm=v_q_norm, k_norm=v_k_norm, attn_sinks=v_attn_sinks, rel_bias=v_rel_bias,
             attn_w_o=v_attn_w_o, w_out=v_w_out, ffn2_norm=v_ffn2_norm, ffn2_w_in=v_ffn2_w_in, ffn2_w_out=v_ffn2_w_out)
    px, py, pc = _position()
    me = 4 * px + 2 * py + pc
    place = jnp.stack([pc, 2 * px + py]).astype(jnp.int32)

    rows_of = lambda k, a: a.T if k in COLUMN_SHARDED else a
    buffers = dict(zip(MATRICES + ("taps",), _prep([rows_of(k, w[k]) for k in MATRICES], conv_dw_kernel,
                                                   me.astype(jnp.int32).reshape(1))))
    landings, sets = [], []
    for stage in GATHER_STAGES:
        sets.append([(len(landings) + i, buffers[k].shape[0] // N_DEV) for i, k in enumerate(STAGE_MEMBERS[stage])])
        landings += [buffers[k] for k in STAGE_MEMBERS[stage]]
    sems, _, land_thru, _ = _ici_copies_start(sets, None, landings, [STAGE_GATHER[s] for s in GATHER_STAGES],
                                              "gather_start")

    def weights_of(stage, after):
        s = GATHER_STAGES.index(stage)
        rows = [r for _, r in sets[s]]
        landed = _ici_copies_wait(sems[s], rows, None, [land_thru[k] for k, _ in sets[s]], STAGE_GATHER[stage],
                                  list(after), "gather_wait_" + stage)
        if not STAGE_GATHER[stage].all_cores:
            landed = _d2d_gather(landed, rows, "gather_d2d_" + stage)
        out = dict(zip(STAGE_MEMBERS[stage], landed))
        if "taps" in out:
            taps = out.pop("taps")
            out["conv_dw_kernel"] = jnp.transpose(taps.reshape(N_DEV, CWP, BLK), (1, 0, 2)).reshape(CWP, D)[:CW]
        return out

    in_flight = []

    def wgrad(lhs, rhs, name, lhs_is_transposed, deps=()):
        return _wgrad_pair(lhs, rhs, name, lhs_is_transposed=lhs_is_transposed, deps=deps)

    def grads_done(stage, grads):
        names = list(grads)
        added = []
        for k in names:
            if isinstance(grads[k], (tuple, list)):
                kept, received = grads[k]
                added.append(_pair_add(kept, received, place, "pair_add_" + k, kept_only=True))
            else:
                received, = _rs_pair([grads[k]], "rs_pair_" + k)
                added.append(_pair_add(grads[k], received, place, "pair_add_" + k))
        partials = [p for p, _ in added]
        members = [(i, p.shape[0] // 4) for i, p in enumerate(partials)]
        sem, p_thru, l_thru, token = _ici_copies_start([members], partials, [l for _, l in added], [SCATTER],
                                                       "scatter_start_" + stage)
        in_flight.append((stage, names, sem[0], p_thru, l_thru, token))
        return [token]

    small = []

    def small_done(gv, sq):
        payload = jnp.concatenate([_pack_small(gv, sq), jnp.pad(gv["conv_dw_kernel"], ((0, PAYLOAD_ROWS - ROW_TAPS - CW), (0, 0)))],
                                  axis=0)
        mine = lax.dynamic_update_slice_in_dim(lax.empty((N_DEV * PAYLOAD_ROWS, D), F32), payload, me * PAYLOAD_ROWS, axis=0)
        sems, _, thru, token = _ici_copies_start([[(0, PAYLOAD_ROWS)]], None, [mine], [GATHER_ALL], "small_start")
        small.append((sems[0], thru))
        return [token]

    vec = {k: w[k] for k in WEIGHTS if k not in MATRICES and k != "conv_dw_kernel"}
    dx0 = _local_step(x[0], loss_target[0], vec, weights_of, wgrad, grads_done, small_done)
    gathered, = _ici_copies_wait(small[0][0], [PAYLOAD_ROWS], None, small[0][1], GATHER_ALL, [in_flight[-1][-1]], "small_wait")
    total = _sum_blocks(gathered, PAYLOAD_ROWS)
    loss = (0.5 / D) * jnp.sum(total[ROW_LOSS])

    grads, delta, new_m, new_v = {}, {}, {}, {}
    after = [total]
    for stage, names, sem, p_thru, l_thru, _ in in_flight:
        landed = _ici_copies_wait(sem, [p.shape[0] // 4 for p in p_thru], p_thru, l_thru, SCATTER, after,
                                  "scatter_wait_" + stage)
        after = []
        for k, buf in zip(names, landed):
            out = _reduce_adamw(buf, rows_of(k, w[k]), rows_of(k, m[k]), rows_of(k, v[k]), "adamw_" + k)
            grads[k], delta[k], new_m[k], new_v[k] = [rows_of(k, a) for a in out]
            after.append(out[1])
    zero_row = jnp.zeros((1, D), F32)
    d8, m8, v8 = _adamw_small(_pack_small(w, zero_row), total[:ROW_TAPS], _pack_small(m, zero_row),
                              _pack_small(v, zero_row), "adamw_small")
    grads.update(_unpack_small(total[:ROW_TAPS]))
    delta.update(_unpack_small(d8))
    new_m.update(_unpack_small(m8))
    new_v.update(_unpack_small(v8))
    k = "conv_dw_kernel"
    grads[k] = lax.dynamic_slice_in_dim(total[ROW_TAPS:ROW_TAPS + CW], me * BLK, BLK, axis=1)
    delta[k], new_m[k], new_v[k] = _adamw_small(w[k], grads[k], m[k], v[k], "adamw_taps")

    return (loss, dx0[None], *[grads[k] for k in WEIGHTS], *[delta[k] for k in WEIGHTS],
            *[new_m[k] for k in WEIGHTS], *[new_v[k] for k in WEIGHTS])
```

```python
import functools
import math

import numpy as np
import jax
import jax.numpy as jnp
from jax import lax
from jax.experimental import pallas as pl
from jax.experimental.pallas import tpu as pltpu

F32 = jnp.float32
BF = jnp.bfloat16

D = 1024
F = 2816
INW = 5632
CW = 31
CWP = 32
HD = 64
NQ = 16
NKV = 4
GRP = NQ // NKV
BLK = 128
NBUCKET = 32
EPS = 1e-6
NEG = float(jnp.finfo(jnp.float32).min)
QK_SCALE = 1.0 / math.sqrt(HD)
R_CONV = (0, 2048)
R_QKV = (2048, 3584)
R_Q = (2048, 3072)
R_KV = (3072, 3584)
R_GATE = (3584, 5632)

N_DEV = 8
VMEM_LIMIT_V7X = 56 * 1024 * 1024
ROW_TILE = 256
ROW_TILE_WIDE = 512

ADAM_LR = 0.001
ADAM_B1 = 0.9
ADAM_B2 = 0.999
ADAM_EPS = 1e-08
ADAM_WD = 0.01
ADAM_STEP = 10

NT_DIMS = (((1,), (1,)), ((), ()))
TN_DIMS = (((0,), (0,)), ((), ()))


def _dot(a, b):
    return jnp.dot(a, b, preferred_element_type=F32)


def _dot_nt(a, b):
    return lax.dot_general(a, b, NT_DIMS, preferred_element_type=F32)


def _dot_tn(a, b):
    return lax.dot_general(a, b, TN_DIMS, preferred_element_type=F32)


def _sig(x):
    return 0.5 * jnp.tanh(0.5 * x) + 0.5


ANY = pl.BlockSpec(memory_space=pl.ANY)


def _call(body, deps, args, **kw):
    n = len(deps)
    if n:
        kw["in_specs"] = [ANY] * n + list(kw["in_specs"])
        return pl.pallas_call(lambda *refs: body(*refs[n:]), **kw)(*deps, *args)
    return pl.pallas_call(body, **kw)(*args)


def _params(n_axes):
    return pltpu.CompilerParams(dimension_semantics=("arbitrary",) * n_axes, vmem_limit_bytes=VMEM_LIMIT_V7X)


def _resident(shape):
    zeros = (0,) * len(shape)
    return pl.BlockSpec(shape, lambda *_: zeros, pipeline_mode=pl.Buffered(1))


def _row_tile(rows, cols):
    return pl.BlockSpec((rows, cols), lambda i: (i, 0))


def _rms_stats(x):
    r = lax.rsqrt(jnp.mean(x * x, axis=-1, keepdims=True) + EPS)
    return r, x * r


def _rms_bwd(dn, x, g):
    r, xh = _rms_stats(x)
    dxh = dn * g
    dx = r * (dxh - xh * jnp.mean(dxh * xh, axis=-1, keepdims=True))
    return dx, jnp.sum(dn * xh, axis=0, keepdims=True)


def _ffn_last(x, target, g, w_in_t, w_out, name):
    t = x.shape[0]
    tm = min(ROW_TILE, t)

    def body(x_ref, t_ref, g_ref, w_ref, wo_ref, n_ref, du_ref, h_ref, dy_ref, dx_ref, sq_ref, dg_ref):
        @pl.when(pl.program_id(0) == 0)
        def _():
            sq_ref[...] = jnp.zeros_like(sq_ref)
            dg_ref[...] = jnp.zeros_like(dg_ref)

        x = x_ref[...]
        g = g_ref[...]
        r, xh = _rms_stats(x)
        n = (xh * g).astype(BF)
        n_ref[...] = n
        u = _dot_nt(n, w_ref[...])
        a = u[:, :F]
        b = u[:, F:]
        s = _sig(a)
        sa = a * s
        h = (sa * b).astype(BF)
        h_ref[...] = h
        err = x + 0.5 * _dot(h, wo_ref[...]) - t_ref[...]
        sq_ref[...] += jnp.sum(err * err, axis=0, keepdims=True)
        dxo = err * (1.0 / D)
        dy = (0.5 * dxo).astype(BF)
        dy_ref[...] = dy
        dh = _dot_nt(dy, wo_ref[...])
        du_ref[:, :F] = (dh * b * (s * (1.0 + a * (1.0 - s)))).astype(BF)
        du_ref[:, F:] = (dh * sa).astype(BF)
        dn = _dot(du_ref[...], w_ref[...])
        dxh = dn * g
        dx_ref[...] = dxo + r * (dxh - xh * jnp.mean(dxh * xh, axis=-1, keepdims=True))
        dg_ref[...] += jnp.sum(dn * xh, axis=0, keepdims=True)

    vec = pl.BlockSpec((1, D), lambda i: (0, 0))
    return pl.pallas_call(
        body, grid=(t // tm,),
        in_specs=[_row_tile(tm, D), _row_tile(tm, D), _resident((1, D)), _resident((INW, D)), _resident((F, D))],
        out_specs=[_row_tile(tm, D), _row_tile(tm, INW), _row_tile(tm, F), _row_tile(tm, D), _row_tile(tm, D), vec, vec],
        out_shape=[jax.ShapeDtypeStruct((t, D), BF), jax.ShapeDtypeStruct((t, INW), BF), jax.ShapeDtypeStruct((t, F), BF),
                   jax.ShapeDtypeStruct((t, D), BF), jax.ShapeDtypeStruct((t, D), F32), jax.ShapeDtypeStruct((1, D), F32),
                   jax.ShapeDtypeStruct((1, D), F32)],
        compiler_params=_params(1), name=name)(x, target, g, w_in_t, w_out)


def _ffn_up(x, g, w_in_t, name):
    t = x.shape[0]
    tm = min(ROW_TILE_WIDE, t)

    def body(x_ref, g_ref, w_ref, n_ref, u_ref):
        r, xh = _rms_stats(x_ref[...])
        n = (xh * g_ref[...]).astype(BF)
        n_ref[...] = n
        u_ref[...] = _dot_nt(n, w_ref[...]).astype(BF)

    return pl.pallas_call(
        body, grid=(t // tm,), in_specs=[_row_tile(tm, D), _resident((1, D)), _resident((INW, D))],
        out_specs=[_row_tile(tm, D), _row_tile(tm, INW)],
        out_shape=[jax.ShapeDtypeStruct((t, D), BF), jax.ShapeDtypeStruct((t, INW), BF)],
        compiler_params=_params(1), name=name)(x, g, w_in_t)


def _ffn_down(x, u, w_out, name):
    t = x.shape[0]
    tm = min(ROW_TILE_WIDE, t)

    def body(x_ref, u_ref, wo_ref, xo_ref):
        a = u_ref[:, :F].astype(F32)
        b = u_ref[:, F:].astype(F32)
        h = (a * _sig(a) * b).astype(BF)
        xo_ref[...] = x_ref[...] + 0.5 * _dot(h, wo_ref[...])

    return pl.pallas_call(
        body, grid=(t // tm,), in_specs=[_row_tile(tm, D), _row_tile(tm, INW), _resident((F, D))],
        out_specs=_row_tile(tm, D), out_shape=jax.ShapeDtypeStruct((t, D), F32),
        compiler_params=_params(1), name=name)(x, u, w_out)


def _ffn_bwd(dxo, x, g, u, w_in_t, w_out, name, deps=()):
    t = x.shape[0]
    tm = min(ROW_TILE, t)

    def body(dxo_ref, x_ref, g_ref, u_ref, w_ref, wo_ref, dx_ref, du_ref, h_ref, dy_ref, dg_ref):
        dxo = dxo_ref[...]
        dy = (0.5 * dxo).astype(BF)
        dy_ref[...] = dy
        dh = _dot_nt(dy, wo_ref[...])
        a = u_ref[:, :F].astype(F32)
        b = u_ref[:, F:].astype(F32)
        s = _sig(a)
        sa = a * s
        h_ref[...] = (sa * b).astype(BF)
        du_ref[:, :F] = (dh * b * (s * (1.0 + a * (1.0 - s)))).astype(BF)
        du_ref[:, F:] = (dh * sa).astype(BF)
        dn = _dot(du_ref[...], w_ref[...])
        dx, dg = _rms_bwd(dn, x_ref[...], g_ref[...])
        dx_ref[...] = dxo + dx

        @pl.when(pl.program_id(0) == 0)
        def _():
            dg_ref[...] = jnp.zeros_like(dg_ref)

        dg_ref[...] += dg

    return _call(
        body, deps, (dxo, x, g, u, w_in_t, w_out), grid=(t // tm,),
        in_specs=[_row_tile(tm, D), _row_tile(tm, D), _resident((1, D)), _row_tile(tm, INW), _resident((INW, D)),
                  _resident((F, D))],
        out_specs=[_row_tile(tm, D), _row_tile(tm, INW), _row_tile(tm, F), _row_tile(tm, D),
                   pl.BlockSpec((1, D), lambda i: (0, 0))],
        out_shape=[jax.ShapeDtypeStruct((t, D), F32), jax.ShapeDtypeStruct((t, INW), BF), jax.ShapeDtypeStruct((t, F), BF),
                   jax.ShapeDtypeStruct((t, D), BF), jax.ShapeDtypeStruct((1, D), F32)],
        compiler_params=_params(1), name=name)


def _wgrad(lhs, rhs, name, *, lhs_is_transposed, chunk, deps=()):
    t = rhs.shape[0]
    n = lhs.shape[0] if lhs_is_transposed else lhs.shape[1]
    c = min(chunk, n)

    def body(l_ref, r_ref, o_ref):
        if lhs_is_transposed:
            o_ref[...] = _dot(l_ref[...], r_ref[...]).astype(BF)
        else:
            o_ref[...] = _dot_tn(l_ref[...], r_ref[...]).astype(BF)

    lhs_spec = pl.BlockSpec((c, t), lambda j: (j, 0)) if lhs_is_transposed else pl.BlockSpec((t, c), lambda j: (0, j))
    return _call(
        body, deps, (lhs, rhs), grid=(n // c,),
        in_specs=[lhs_spec, _resident((t, D))],
        out_specs=pl.BlockSpec((c, D), lambda j: (j, 0)),
        out_shape=jax.ShapeDtypeStruct((n, D), BF),
        compiler_params=_params(1), name=name)


def _wgrad_mix(duc, dq_t, dkv_t, dgp, hm):
    t = hm.shape[0]
    c = 512
    first_q, first_kv, first_gate = R_Q[0] // c, R_KV[0] // c, R_GATE[0] // c

    def body(uc_ref, q_ref, kv_ref, gp_ref, h_ref, o_ref):
        j = pl.program_id(0)

        @pl.when(j < first_q)
        def _():
            o_ref[...] = _dot_tn(uc_ref[...], h_ref[...]).astype(BF)

        @pl.when((j >= first_q) & (j < first_kv))
        def _():
            o_ref[...] = _dot(q_ref[...], h_ref[...]).astype(BF)

        @pl.when((j >= first_kv) & (j < first_gate))
        def _():
            o_ref[...] = _dot(kv_ref[...], h_ref[...]).astype(BF)

        @pl.when(j >= first_gate)
        def _():
            o_ref[...] = _dot_tn(gp_ref[...], h_ref[...]).astype(BF)

    return pl.pallas_call(
        body, grid=(INW // c,),
        in_specs=[pl.BlockSpec((t, c), lambda j: (0, jnp.clip(j, 0, first_q - 1))),
                  pl.BlockSpec((c, t), lambda j: (jnp.clip(j - first_q, 0, first_kv - first_q - 1), 0)),
                  pl.BlockSpec((c, t), lambda j: (jnp.clip(j - first_kv, 0, first_gate - first_kv - 1), 0)),
                  pl.BlockSpec((t, c), lambda j: (0, jnp.clip(j - first_gate, 0, INW // c - first_gate - 1))),
                  _resident((t, D))],
        out_specs=pl.BlockSpec((c, D), lambda j: (j, 0)),
        out_shape=jax.ShapeDtypeStruct((INW, D), BF),
        compiler_params=_params(1), name="mix_dw_in")(duc, dq_t, dkv_t, dgp, hm)


def _mix_proj(x, g, w_t):
    t = x.shape[0]
    tm = min(ROW_TILE_WIDE, t)

    def body(x_ref, g_ref, w_ref, hm_ref, uc_ref, gp_ref, qkv_ref):
        r, xh = _rms_stats(x_ref[...])
        hm = (xh * g_ref[...]).astype(BF)
        hm_ref[...] = hm
        uc_ref[...] = _dot_nt(hm, w_ref[R_CONV[0]:R_CONV[1], :]).astype(BF)
        gp_ref[...] = _dot_nt(hm, w_ref[R_GATE[0]:R_GATE[1], :]).astype(BF)
        qkv_ref[...] = _dot_nt(w_ref[R_QKV[0]:R_QKV[1], :], hm).astype(BF)

    return pl.pallas_call(
        body, grid=(t // tm,),
        in_specs=[_row_tile(tm, D), _resident((1, D)), _resident((INW, D))],
        out_specs=[_row_tile(tm, D), _row_tile(tm, 2 * D), _row_tile(tm, 2 * D), pl.BlockSpec((1536, tm), lambda i: (0, i))],
        out_shape=[jax.ShapeDtypeStruct((t, D), BF), jax.ShapeDtypeStruct((t, 2 * D), BF),
                   jax.ShapeDtypeStruct((t, 2 * D), BF), jax.ShapeDtypeStruct((1536, t), BF)],
        compiler_params=_params(1), name="mix_proj")(x, g, w_t)


CONV_HALO = 32
CONV_LEAD = CONV_HALO - (CW - 1)


def _glu(uc):
    uc = uc.astype(F32)
    return uc[:, :D] * _sig(uc[:, D:])


def _ln_stats(zc):
    mu = jnp.mean(zc, axis=-1, keepdims=True)
    zm = zc - mu
    r = lax.rsqrt(jnp.mean(zm * zm, axis=-1, keepdims=True) + EPS)
    return r, zm * r


CONV_SHIFTS = 8
CONV_CHUNK = 32


def _store_shifted(buf, rows):
    for b in range(1, CONV_SHIFTS):
        buf[b, 0:rows - 8, :] = buf[0, pl.ds(b, rows - 8), :]


def _conv_fwd(uc, dwk, dwb, lng, lnb):
    t = uc.shape[0]
    tm = min(512, t)
    per = tm // CONV_HALO
    ext = tm + CONV_HALO

    def body(cur_ref, prev_ref, k_ref, kb_ref, g_ref, b_ref, o_ref, zc_ref, zsh):
        i = pl.program_id(0)
        zsh[0, 0:CONV_HALO, :] = _glu(prev_ref[...]) * (i > 0).astype(F32)
        zsh[0, CONV_HALO:, :] = _glu(cur_ref[...])
        _store_shifted(zsh, ext)

        def chunk(ci, carry):
            r0 = pl.multiple_of(ci * CONV_CHUNK, CONV_CHUNK)
            acc = jnp.zeros((CONV_CHUNK, D), F32) + kb_ref[...]
            for w in range(CW):
                a, b = divmod(CONV_LEAD + w, 8)
                acc = acc + k_ref[w:w + 1, :] * zsh[b, pl.ds(r0 + 8 * a, CONV_CHUNK), :]
            zc_ref[pl.ds(r0, CONV_CHUNK), :] = acc
            return carry

        lax.fori_loop(0, tm // CONV_CHUNK, chunk, 0)
        r, xh = _ln_stats(zc_ref[...])
        y = xh * g_ref[...] + b_ref[...]
        o_ref[...] = (y * _sig(y)).astype(BF)

    return pl.pallas_call(
        body, grid=(t // tm,),
        in_specs=[_row_tile(tm, 2 * D),
                  pl.BlockSpec((CONV_HALO, 2 * D), lambda i: (jnp.maximum(i * per - 1, 0), 0)),
                  _resident((CWP, D)), _resident((1, D)), _resident((1, D)), _resident((1, D))],
        out_specs=[_row_tile(tm, D), _row_tile(tm, D)],
        out_shape=[jax.ShapeDtypeStruct((t, D), BF), jax.ShapeDtypeStruct((t, D), F32)],
        scratch_shapes=[pltpu.VMEM((CONV_SHIFTS, ext, D), F32)],
        compiler_params=_params(1), name="conv_fwd")(uc, uc, dwk, dwb, lng, lnb)


def _conv_bwd(uc, zc, dzs, dwk, lng, lnb):
    t = uc.shape[0]
    tm = min(ROW_TILE_WIDE, t)
    per = tm // CONV_HALO
    n_tiles = t // tm
    ext = tm + CONV_HALO
    last_block = t // CONV_HALO - 1

    def body(cur_ref, zc_ref, zcn_ref, dz_ref, dzn_ref, k_ref, g_ref, b_ref,
             duc_ref, dk_ref, dkb_ref, dg_ref, db_ref, dsh, dk8, z_scr):
        i = pl.program_id(0)

        @pl.when(i == 0)
        def _():
            dk8[...] = jnp.zeros_like(dk8)
            dkb_ref[...] = jnp.zeros_like(dkb_ref)
            dg_ref[...] = jnp.zeros_like(dg_ref)
            db_ref[...] = jnp.zeros_like(db_ref)

        has_next = (i < n_tiles - 1).astype(F32)
        z_scr[...] = _glu(cur_ref[...])
        gain = g_ref[...]

        def ln_silu_bwd(zc, dzs, live):
            r, xh = _ln_stats(zc)
            y = xh * gain + b_ref[...]
            sy = _sig(y)
            dy = dzs * (sy * (1.0 + y * (1.0 - sy))) * live
            dxh = dy * gain
            dzc = r * (dxh - jnp.mean(dxh, axis=-1, keepdims=True) - xh * jnp.mean(dxh * xh, axis=-1, keepdims=True))
            return dzc, dy, xh

        dzc, dy, xh = ln_silu_bwd(zc_ref[...], dz_ref[...], 1.0)
        dsh[0, 0:tm, :] = dzc
        dg_ref[...] += jnp.sum(dy * xh, axis=0, keepdims=True)
        db_ref[...] += jnp.sum(dy, axis=0, keepdims=True)
        dkb_ref[...] += jnp.sum(dzc, axis=0, keepdims=True)
        dsh[0, tm:, :] = ln_silu_bwd(zcn_ref[...], dzn_ref[...], has_next)[0]
        _store_shifted(dsh, ext)

        def chunk(ci, carry):
            r0 = pl.multiple_of(ci * CONV_CHUNK, CONV_CHUNK)
            z_c = z_scr[pl.ds(r0, CONV_CHUNK), :]
            dz = jnp.zeros((CONV_CHUNK, D), F32)
            for w in range(CW):
                a, b = divmod(CW - 1 - w, 8)
                window = dsh[b, pl.ds(r0 + 8 * a, CONV_CHUNK), :]
                dz = dz + k_ref[w:w + 1, :] * window
                prod = z_c * window
                part = prod[0:8, :]
                for j in range(1, CONV_CHUNK // 8):
                    part = part + prod[8 * j:8 * j + 8, :]
                dk8[w] += part
            ucc = cur_ref[pl.ds(r0, CONV_CHUNK), :].astype(F32)
            sg = _sig(ucc[:, D:])
            duc_ref[pl.ds(r0, CONV_CHUNK), 0:D] = (dz * sg).astype(BF)
            duc_ref[pl.ds(r0, CONV_CHUNK), D:2 * D] = (dz * ucc[:, :D] * sg * (1.0 - sg)).astype(BF)
            return carry

        lax.fori_loop(0, tm // CONV_CHUNK, chunk, 0)

        @pl.when(i == n_tiles - 1)
        def _():
            dk_ref[...] = jnp.sum(dk8[...], axis=1)

    vec = pl.BlockSpec((1, D), lambda i: (0, 0))
    next_halo = pl.BlockSpec((CONV_HALO, D), lambda i: (jnp.minimum((i + 1) * per, last_block), 0))
    return pl.pallas_call(
        body, grid=(n_tiles,),
        in_specs=[_row_tile(tm, 2 * D), _row_tile(tm, D), next_halo, _row_tile(tm, D), next_halo,
                  _resident((CWP, D)), _resident((1, D)), _resident((1, D))],
        out_specs=[_row_tile(tm, 2 * D), pl.BlockSpec((CWP, D), lambda i: (0, 0)), vec, vec, vec],
        out_shape=[jax.ShapeDtypeStruct((t, 2 * D), BF), jax.ShapeDtypeStruct((CWP, D), F32),
                   jax.ShapeDtypeStruct((1, D), F32), jax.ShapeDtypeStruct((1, D), F32), jax.ShapeDtypeStruct((1, D), F32)],
        scratch_shapes=[pltpu.VMEM((CONV_SHIFTS, ext, D), F32), pltpu.VMEM((CWP, 8, D), F32), pltpu.VMEM((tm, D), F32)],
        compiler_params=_params(1), name="conv_bwd")(uc, zc, zc, dzs, dzs, dwk, lng, lnb)


def _norm_rows(xt, g):
    r = lax.rsqrt(jnp.mean(xt * xt, axis=0, keepdims=True) + EPS)
    xh = xt * r
    return xh * g, r, xh


ATT_TQ = 1024


def _attn_specs(t, tq):
    per = tq // BLK
    return [pl.BlockSpec((1536, tq), lambda i: (0, i)),
            pl.BlockSpec((512, BLK), lambda i: (2, jnp.maximum(i * per - 1, 0))),
            _resident((HD, 1)), _resident((HD, 1)), _resident((NKV, 1, GRP * BLK)),
            _resident((2, NKV, 2 * BLK, GRP * BLK))]


def _attn_window(hk, sb, qkv_ref, halo_ref, kn_cur, kn_halo):
    v0 = D + NKV * HD + hk * HD
    if sb == 0:
        k_prev = kn_halo[hk]
        v_prev = halo_ref[NKV * HD + hk * HD:NKV * HD + (hk + 1) * HD, :]
    else:
        k_prev = kn_cur[hk][:, (sb - 1) * BLK:sb * BLK]
        v_prev = qkv_ref[v0:v0 + HD, (sb - 1) * BLK:sb * BLK]
    kw = jnp.concatenate([k_prev, kn_cur[hk][:, sb * BLK:(sb + 1) * BLK]], axis=1).astype(BF)
    vw = jnp.concatenate([v_prev, qkv_ref[v0:v0 + HD, sb * BLK:(sb + 1) * BLK]], axis=1)
    return kw, vw


def _attn_probs(kw, qc, bias, sink):
    st = _dot_tn(kw, qc) + bias
    m = jnp.maximum(jnp.max(st, axis=0, keepdims=True), sink)
    p = jnp.exp(st - m)
    e_sink = jnp.exp(sink - m)
    inv = 1.0 / (jnp.sum(p, axis=0, keepdims=True) + e_sink)
    return p * inv, e_sink * inv


def _attn_fwd(qkv_t, qg, kg, sink_rows, bias_t):
    t = qkv_t.shape[1]
    tq = min(ATT_TQ, t)
    n_sub = tq // BLK

    def body(qkv_ref, halo_ref, qg_ref, kg_ref, sink_ref, bias_ref, o_ref):
        i = pl.program_id(0)
        first = (i == 0).astype(jnp.int32)
        kgain = kg_ref[...]
        qgain = qg_ref[...]
        kn_cur = [_norm_rows(qkv_ref[D + h * HD:D + (h + 1) * HD, :].astype(F32), kgain)[0] for h in range(NKV)]
        kn_halo = [_norm_rows(halo_ref[h * HD:(h + 1) * HD, :].astype(F32), kgain)[0] for h in range(NKV)]
        for hk in range(NKV):
            for sb in range(n_sub):
                cols = slice(sb * BLK, (sb + 1) * BLK)
                kw, vw = _attn_window(hk, sb, qkv_ref, halo_ref, kn_cur, kn_halo)
                qc = jnp.concatenate(
                    [_norm_rows(qkv_ref[(GRP * hk + g) * HD:(GRP * hk + g + 1) * HD, cols].astype(F32), qgain)[0] * QK_SCALE
                     for g in range(GRP)], axis=1).astype(BF)
                bias = bias_ref[first, hk] if sb == 0 else bias_ref[0, hk]
                p, _ = _attn_probs(kw, qc, bias, sink_ref[hk])
                o = _dot(vw, p.astype(BF))
                for g in range(GRP):
                    head = GRP * hk + g
                    o_ref[head * HD:(head + 1) * HD, cols] = o[:, g * BLK:(g + 1) * BLK].astype(BF)

    return pl.pallas_call(
        body, grid=(t // tq,),
        in_specs=_attn_specs(t, tq),
        out_specs=pl.BlockSpec((D, tq), lambda i: (0, i)),
        out_shape=jax.ShapeDtypeStruct((D, t), BF),
        compiler_params=_params(1), name="attn_fwd")(qkv_t, qkv_t, qg, kg, sink_rows, bias_t)


def _attn_bwd(qkv_t, do_t, qg, kg, sink_rows, bias_t, deps=()):
    t = qkv_t.shape[1]
    tq = min(ATT_TQ, t)
    n_sub = tq // BLK
    n_tiles = t // tq

    def body(qkv_ref, halo_ref, do_ref, qg_ref, kg_ref, sink_ref, bias_ref,
             dq_ref, ckv_ref, dqg_ref, dsink_ref, dsacc_ref, qg_scr):
        i = pl.program_id(0)

        @pl.when(i == 0)
        def _():
            qg_scr[...] = jnp.zeros_like(qg_scr)
            dsink_ref[...] = jnp.zeros_like(dsink_ref)
            dsacc_ref[...] = jnp.zeros_like(dsacc_ref)

        first = (i == 0).astype(jnp.int32)
        kgain = kg_ref[...]
        qgain = qg_ref[...]
        kn_cur = [_norm_rows(qkv_ref[D + h * HD:D + (h + 1) * HD, :].astype(F32), kgain)[0] for h in range(NKV)]
        kn_halo = [_norm_rows(halo_ref[h * HD:(h + 1) * HD, :].astype(F32), kgain)[0] for h in range(NKV)]
        dqg = jnp.zeros((HD, BLK), F32)
        for hk in range(NKV):
            for sb in range(n_sub):
                cols = slice(sb * BLK, (sb + 1) * BLK)
                kw, vw = _attn_window(hk, sb, qkv_ref, halo_ref, kn_cur, kn_halo)
                qn, qr, qh = [], [], []
                for g in range(GRP):
                    head = GRP * hk + g
                    n_, r_, h_ = _norm_rows(qkv_ref[head * HD:(head + 1) * HD, cols].astype(F32), qgain)
                    qn.append(n_)
                    qr.append(r_)
                    qh.append(h_)
                qc = (jnp.concatenate(qn, axis=1) * QK_SCALE).astype(BF)
                bias = bias_ref[first, hk] if sb == 0 else bias_ref[0, hk]
                p, p_sink = _attn_probs(kw, qc, bias, sink_ref[hk])
                doc = jnp.concatenate([do_ref[(GRP * hk + g) * HD:(GRP * hk + g + 1) * HD, cols] for g in range(GRP)], axis=1)
                dp = _dot_tn(vw, doc)
                delta = jnp.sum(p * dp, axis=0, keepdims=True)
                ds = p * (dp - delta)
                dsink_ref[hk] += -(p_sink * delta)
                dsacc_ref[hk] += ds
                dsb = ds.astype(BF)
                dqc = _dot(kw, dsb) * QK_SCALE
                ckv_ref[sb, hk * HD:(hk + 1) * HD, :] = _dot_nt(qc, dsb)
                ckv_ref[sb, NKV * HD + hk * HD:NKV * HD + (hk + 1) * HD, :] = _dot_nt(doc, p.astype(BF))
                for g in range(GRP):
                    head = GRP * hk + g
                    dqn = dqc[:, g * BLK:(g + 1) * BLK]
                    dqh = dqn * qgain
                    dq = qr[g] * (dqh - qh[g] * jnp.mean(dqh * qh[g], axis=0, keepdims=True))
                    dq_ref[head * HD:(head + 1) * HD, cols] = dq.astype(BF)
                    dqg = dqg + dqn * qh[g]
        qg_scr[...] += dqg

        @pl.when(i == n_tiles - 1)
        def _():
            dqg_ref[...] = jnp.sum(qg_scr[...], axis=1, keepdims=True)

    return _call(
        body, deps, (qkv_t, qkv_t, do_t, qg, kg, sink_rows, bias_t), grid=(n_tiles,),
        in_specs=_attn_specs(t, tq)[:2] + [pl.BlockSpec((D, tq), lambda i: (0, i))] + _attn_specs(t, tq)[2:],
        out_specs=[pl.BlockSpec((D, tq), lambda i: (0, i)),
                   pl.BlockSpec((n_sub, 2 * NKV * HD, 2 * BLK), lambda i: (i, 0, 0)),
                   pl.BlockSpec((HD, 1), lambda i: (0, 0)),
                   pl.BlockSpec((NKV, 1, GRP * BLK), lambda i: (0, 0, 0)),
                   pl.BlockSpec((NKV, 2 * BLK, GRP * BLK), lambda i: (0, 0, 0))],
        out_shape=[jax.ShapeDtypeStruct((D, t), BF),
                   jax.ShapeDtypeStruct((t // BLK, 2 * NKV * HD, 2 * BLK), F32),
                   jax.ShapeDtypeStruct((HD, 1), F32),
                   jax.ShapeDtypeStruct((NKV, 1, GRP * BLK), F32),
                   jax.ShapeDtypeStruct((NKV, 2 * BLK, GRP * BLK), F32)],
        scratch_shapes=[pltpu.VMEM((HD, BLK), F32)],
        compiler_params=_params(1), name="attn_bwd")


def _kv_combine(ckv, qkv_t, kg):
    nb = ckv.shape[0]
    t = nb * BLK
    rows = NKV * HD
    per = min(4, nb)
    steps = nb // per

    def body(c_ref, cn_ref, k_ref, kg_ref, o_ref, dkg_ref, kg_scr):
        n = pl.program_id(0)

        @pl.when(n == 0)
        def _():
            kg_scr[...] = jnp.zeros_like(kg_scr)

        has_next = (n < steps - 1).astype(F32)
        kgain = kg_ref[...]
        dkg = jnp.zeros((HD, BLK), F32)
        for s in range(per):
            cols = slice(s * BLK, (s + 1) * BLK)
            after = c_ref[s + 1, :, :BLK] if s + 1 < per else cn_ref[0, :, :BLK] * has_next
            d = c_ref[s, :, BLK:] + after
            o_ref[rows:, cols] = d[rows:, :].astype(BF)
            for h in range(NKV):
                _, r, kh = _norm_rows(k_ref[h * HD:(h + 1) * HD, cols].astype(F32), kgain)
                dkn = d[h * HD:(h + 1) * HD, :]
                dkh = dkn * kgain
                o_ref[h * HD:(h + 1) * HD, cols] = (r * (dkh - kh * jnp.mean(dkh * kh, axis=0, keepdims=True))).astype(BF)
                dkg = dkg + dkn * kh
        kg_scr[...] += dkg

        @pl.when(n == steps - 1)
        def _():
            dkg_ref[...] = jnp.sum(kg_scr[...], axis=1, keepdims=True)

    return pl.pallas_call(
        body, grid=(steps,),
        in_specs=[pl.BlockSpec((per, 2 * rows, 2 * BLK), lambda n: (n, 0, 0)),
                  pl.BlockSpec((1, 2 * rows, 2 * BLK), lambda n: (jnp.minimum((n + 1) * per, nb - 1), 0, 0)),
                  pl.BlockSpec((rows, per * BLK), lambda n: (D // rows, n)),
                  _resident((HD, 1))],
        out_specs=[pl.BlockSpec((2 * rows, per * BLK), lambda n: (0, n)), pl.BlockSpec((HD, 1), lambda n: (0, 0))],
        out_shape=[jax.ShapeDtypeStruct((2 * rows, t), BF), jax.ShapeDtypeStruct((HD, 1), F32)],
        scratch_shapes=[pltpu.VMEM((HD, BLK), F32)],
        compiler_params=_params(1), name="kv_combine")(ckv, ckv, qkv_t, kg)


def _group_lane_sums(v):
    lane_group = lax.broadcasted_iota(jnp.int32, (1, GRP * BLK), 1) // BLK
    col = lax.broadcasted_iota(jnp.int32, (1, BLK), 1)
    out = jnp.zeros((NKV, BLK), F32)
    for g in range(GRP):
        s = jnp.sum(jnp.where(lane_group == g, v, 0.0), axis=1, keepdims=True)
        out = jnp.where(col == g, s, out)
    return out


def _bias_grad(dsacc, onehot_t):
    per = 8

    def body(ds_ref, oh_ref, o_ref):
        for b in range(per):
            oh = jnp.concatenate([oh_ref[b]] * GRP, axis=1)
            o_ref[b] = _group_lane_sums(jnp.sum(ds_ref[...] * oh[None], axis=1))

    return pl.pallas_call(
        body, grid=(NBUCKET // per,),
        in_specs=[_resident((NKV, 2 * BLK, GRP * BLK)), pl.BlockSpec((per, 2 * BLK, BLK), lambda b: (b, 0, 0))],
        out_specs=pl.BlockSpec((per, NKV, BLK), lambda b: (b, 0, 0)),
        out_shape=jax.ShapeDtypeStruct((NBUCKET, NKV, BLK), F32),
        compiler_params=_params(1), name="bias_grad")(dsacc, onehot_t)


def _sink_grad(dsink_rows):
    def body(d_ref, o_ref):
        o_ref[...] = _group_lane_sums(d_ref[:, 0, :])

    return pl.pallas_call(body, out_shape=jax.ShapeDtypeStruct((NKV, BLK), F32), name="sink_grad")(dsink_rows)


def _mix_out(zs, o_t, gp, x, w_cp, w_o, w_out):
    t = x.shape[0]
    tm = min(ROW_TILE_WIDE, t)

    def body(zs_ref, ot_ref, gp_ref, x_ref, wcp_ref, wo_ref, wout_ref, xo_ref, a_ref, b_ref, m_ref):
        a = _dot(zs_ref[...], wcp_ref[...])
        b = _dot_tn(ot_ref[...], wo_ref[...])
        a_ref[...] = a.astype(BF)
        b_ref[...] = b.astype(BF)
        merged = (_sig(gp_ref[:, :D].astype(F32)) * a + _sig(gp_ref[:, D:].astype(F32)) * b).astype(BF)
        m_ref[...] = merged
        xo_ref[...] = x_ref[...] + _dot(merged, wout_ref[...])

    return pl.pallas_call(
        body, grid=(t // tm,),
        in_specs=[_row_tile(tm, D), pl.BlockSpec((D, tm), lambda i: (0, i)), _row_tile(tm, 2 * D), _row_tile(tm, D),
                  _resident((D, D)), _resident((D, D)), _resident((D, D))],
        out_specs=[_row_tile(tm, D)] * 4,
        out_shape=[jax.ShapeDtypeStruct((t, D), F32)] + [jax.ShapeDtypeStruct((t, D), BF)] * 3,
        compiler_params=_params(1), name="mix_out")(zs, o_t, gp, x, w_cp, w_o, w_out)


def _mix_out_bwd(dx, a, b, gp, w_cp, w_o, w_out, deps=()):
    t = dx.shape[0]
    tm = min(ROW_TILE_WIDE, t)

    def body(dx_ref, a_ref, b_ref, gp_ref, wcp_ref, wo_ref, wout_ref, dzs_ref, dot_ref, dgp_ref, da_ref, db_ref, dxb_ref):
        dxb = dx_ref[...].astype(BF)
        dxb_ref[...] = dxb
        dm = _dot_nt(dxb, wout_ref[...])
        gc = _sig(gp_ref[:, :D].astype(F32))
        ga = _sig(gp_ref[:, D:].astype(F32))
        da = (dm * gc).astype(BF)
        db = (dm * ga).astype(BF)
        da_ref[...] = da
        db_ref[...] = db
        dgp_ref[:, :D] = (dm * a_ref[...].astype(F32) * gc * (1.0 - gc)).astype(BF)
        dgp_ref[:, D:] = (dm * b_ref[...].astype(F32) * ga * (1.0 - ga)).astype(BF)
        dzs_ref[...] = _dot_nt(da, wcp_ref[...])
        dot_ref[...] = _dot_nt(wo_ref[...], db).astype(BF)

    return _call(
        body, deps, (dx, a, b, gp, w_cp, w_o, w_out), grid=(t // tm,),
        in_specs=[_row_tile(tm, D), _row_tile(tm, D), _row_tile(tm, D), _row_tile(tm, 2 * D),
                  _resident((D, D)), _resident((D, D)), _resident((D, D))],
        out_specs=[_row_tile(tm, D), pl.BlockSpec((D, tm), lambda i: (0, i)), _row_tile(tm, 2 * D),
                   _row_tile(tm, D), _row_tile(tm, D), _row_tile(tm, D)],
        out_shape=[jax.ShapeDtypeStruct((t, D), F32), jax.ShapeDtypeStruct((D, t), BF), jax.ShapeDtypeStruct((t, 2 * D), BF),
                   jax.ShapeDtypeStruct((t, D), BF), jax.ShapeDtypeStruct((t, D), BF), jax.ShapeDtypeStruct((t, D), BF)],
        compiler_params=_params(1), name="mix_out_bwd")


def _mix_proj_bwd(dxo, duc, dq_t, dkv_t, dgp, x, g, w_t):
    t = x.shape[0]
    tm = min(ROW_TILE_WIDE, t)

    def body(dxo_ref, duc_ref, dq_ref, dkv_ref, dgp_ref, x_ref, g_ref, w_ref, dx_ref, dg_ref):
        dn = _dot(duc_ref[...], w_ref[R_CONV[0]:R_CONV[1], :])
        dn = dn + _dot(dgp_ref[...], w_ref[R_GATE[0]:R_GATE[1], :])
        dn = dn + _dot_tn(dq_ref[...], w_ref[R_Q[0]:R_Q[1], :])
        dn = dn + _dot_tn(dkv_ref[...], w_ref[R_KV[0]:R_KV[1], :])
        dx, dg = _rms_bwd(dn, x_ref[...], g_ref[...])
        dx_ref[...] = dxo_ref[...] + dx

        @pl.when(pl.program_id(0) == 0)
        def _():
            dg_ref[...] = jnp.zeros_like(dg_ref)

        dg_ref[...] += dg

    return pl.pallas_call(
        body, grid=(t // tm,),
        in_specs=[_row_tile(tm, D), _row_tile(tm, 2 * D), pl.BlockSpec((D, tm), lambda i: (0, i)),
                  pl.BlockSpec((2 * NKV * HD, tm), lambda i: (0, i)), _row_tile(tm, 2 * D), _row_tile(tm, D),
                  _resident((1, D)), _resident((INW, D))],
        out_specs=[_row_tile(tm, D), pl.BlockSpec((1, D), lambda i: (0, 0))],
        out_shape=[jax.ShapeDtypeStruct((t, D), F32), jax.ShapeDtypeStruct((1, D), F32)],
        compiler_params=_params(1), name="mix_proj_bwd")(dxo, duc, dq_t, dkv_t, dgp, x, g, w_t)


def _attention_tables():
    kj = np.arange(2 * BLK)[:, None]
    qi = np.arange(BLK)[None, :]
    dist = qi + BLK - kj
    in_win = (dist >= 0) & (dist < BLK)
    dpos = np.maximum(dist, 0)
    max_exact = NBUCKET // 2
    dfl = np.maximum(dpos, 1).astype(np.float32)
    large = max_exact + (np.log(dfl / np.float32(max_exact)) / np.float32(math.log(BLK / max_exact))
                         * np.float32(NBUCKET - max_exact)).astype(np.int32)
    large = np.minimum(large, NBUCKET - 1)
    bucket = np.where(dpos < max_exact, dpos, large)
    onehot = (bucket[None] == np.arange(NBUCKET)[:, None, None]).astype(np.float32)
    mask = in_win.astype(np.float32)
    mask_first = mask * (kj >= BLK)
    masks = np.stack([np.tile(mask, (1, GRP)), np.tile(mask_first, (1, GRP))])
    return onehot, masks


def _bias_table(rel_bias, onehot):
    tab = jnp.einsum("bkq,bh->hkq", onehot, rel_bias, precision=lax.Precision.HIGHEST)
    tab = tab.reshape(NKV, GRP, 2 * BLK, BLK)
    return jnp.transpose(tab, (0, 2, 1, 3)).reshape(NKV, 2 * BLK, GRP * BLK)


def _local_step(x, target, vec, weights_of, wgrad, grads_done, small_done):
    onehot_np, masks_np = _attention_tables()
    onehot = jnp.asarray(onehot_np)
    masks = jnp.asarray(masks_np)
    bias_t = jnp.where(masks[:, None] > 0.5, _bias_table(vec["rel_bias"], onehot)[None], NEG)
    sink_rows = jnp.repeat(vec["attn_sinks"].reshape(NKV, 1, GRP), BLK, axis=2)
    qg = vec["q_norm"].reshape(HD, 1)
    kg = vec["k_norm"].reshape(HD, 1)
    g1 = vec["ffn1_norm"].reshape(1, D)
    gm = vec["mix_norm"].reshape(1, D)
    g2 = vec["ffn2_norm"].reshape(1, D)
    dwb = vec["conv_dw_bias"].reshape(1, D)
    lng = vec["conv_ln_g"].reshape(1, D)
    lnb = vec["conv_ln_b"].reshape(1, D)

    w1 = weights_of("ffn1_in", (bias_t, sink_rows))
    n1, u1 = _ffn_up(x, g1, w1["ffn1_w_in"], "ffn1_up")
    w1.update(weights_of("ffn1_out", (u1,)))
    x1 = _ffn_down(x, u1, w1["ffn1_w_out"], "ffn1_down")
    wm = weights_of("mix", (x1,))
    dwk = jnp.pad(wm["conv_dw_kernel"], ((0, CWP - CW), (0, 0)))
    hm, uc, gp, qkv_t = _mix_proj(x1, gm, wm["w_in"])
    zs, zc = _conv_fwd(uc, dwk, dwb, lng, lnb)
    o_t = _attn_fwd(qkv_t, qg, kg, sink_rows, bias_t)
    x2, a, b, merged = _mix_out(zs, o_t, gp, x1, wm["conv_w_proj"], wm["attn_w_o"], wm["w_out"])
    w2 = weights_of("ffn2", (x2,))
    gv = {}
    n2, du2, h2, dy2, dx2, sq, gv["ffn2_norm"] = _ffn_last(x2, target, g2, w2["ffn2_w_in"], w2["ffn2_w_out"], "ffn2")

    deps = grads_done("ffn2", {"ffn2_w_in": wgrad(du2, n2, "ffn2_dw_in", False),
                               "ffn2_w_out": wgrad(h2, dy2, "ffn2_dw_out", False)})

    dzs, do_t, dgp, da, db, dx2b = _mix_out_bwd(dx2, a, b, gp, wm["conv_w_proj"], wm["attn_w_o"], wm["w_out"], deps=deps)
    deps = grads_done("mix_out", {"w_out": wgrad(merged, dx2b, "mix_dw_out", False),
                                  "conv_w_proj": wgrad(zs, da, "mix_dw_cp", False),
                                  "attn_w_o": wgrad(o_t, db, "mix_dw_o", True)})

    dq_t, ckv, dqg, dsink_rows, dsacc = _attn_bwd(qkv_t, do_t, qg, kg, sink_rows, bias_t, deps=deps)
    dkv_t, dkg = _kv_combine(ckv, qkv_t, kg)
    gv["q_norm"] = dqg.reshape(HD)
    gv["k_norm"] = dkg.reshape(HD)
    gv["attn_sinks"] = _sink_grad(dsink_rows)[:, :GRP].reshape(NQ)
    gv["rel_bias"] = _bias_grad(dsacc, onehot)[:, :, :GRP].reshape(NBUCKET, NQ)

    duc, dk_conv, gv["conv_dw_bias"], gv["conv_ln_g"], gv["conv_ln_b"] = _conv_bwd(uc, zc, dzs, dwk, lng, lnb)
    gv["conv_dw_kernel"] = dk_conv[:CW]

    dx1, gv["mix_norm"] = _mix_proj_bwd(dx2, duc, dq_t, dkv_t, dgp, x1, gm, wm["w_in"])
    deps = grads_done("mix_in", {"w_in": _wgrad_mix(duc, dq_t, dkv_t, dgp, hm)})

    dx0, du1, h1, dy1, gv["ffn1_norm"] = _ffn_bwd(dx1, x, g1, u1, w1["ffn1_w_in"], w1["ffn1_w_out"], "ffn1_bwd", deps=deps)
    for k in ("ffn1_norm", "mix_norm", "ffn2_norm", "conv_dw_bias", "conv_ln_g", "conv_ln_b"):
        gv[k] = gv[k].reshape(D)
    deps = small_done(gv, sq)
    deps = grads_done("ffn1_in", {"ffn1_w_in": wgrad(du1, n1, "ffn1_dw_in", False, deps)})
    grads_done("ffn1_out", {"ffn1_w_out": wgrad(h1, dy1, "ffn1_dw_out", False, deps)})
    return dx0


MESH_ID = pl.DeviceIdType.MESH


def _position():
    return lax.axis_index("x"), lax.axis_index("y"), lax.axis_index("c")


def _shard_rows(ref, index, rows):
    return ref.at[pl.ds(pl.multiple_of(index * rows, 16), rows), :]


def _prep(weights, taps, me):
    n = len(weights)

    def body(me_ref, *refs):
        for k in range(n):
            refs[n + 1 + k][...] = refs[k][...].astype(BF)
        refs[2 * n + 1][0:CW, :] = refs[n][...]
        refs[2 * n + 1][CW:, :] = jnp.zeros((CWP - CW, BLK), F32)

    shard_shapes = [w.shape for w in weights] + [(CWP, BLK)]
    dtypes = [BF] * n + [F32]
    ins = list(weights) + [taps]
    return pl.pallas_call(
        body,
        grid_spec=pltpu.PrefetchScalarGridSpec(
            num_scalar_prefetch=1, grid=(1,),
            in_specs=[pl.BlockSpec(a.shape, lambda i, m: (0, 0), pipeline_mode=pl.Buffered(1)) for a in ins],
            out_specs=[pl.BlockSpec(s, lambda i, m: (m[0], 0)) for s in shard_shapes]),
        out_shape=[jax.ShapeDtypeStruct((N_DEV * s[0], s[1]), d) for s, d in zip(shard_shapes, dtypes)],
        compiler_params=_params(1), name="prep")(me, *ins)


HBM = pl.BlockSpec(memory_space=pltpu.HBM)
SEM = pl.BlockSpec(memory_space=pltpu.SEMAPHORE)
DATAFLOW = pltpu.SideEffectType.DATAFLOW_SIDE_EFFECTING
TOKEN = jax.ShapeDtypeStruct((8, 128), F32)


def _in_hbm(x):
    return pltpu.with_memory_space_constraint(x, pltpu.HBM)


def _hbm_like(arrays):
    return [pltpu.HBM(a.shape, a.dtype) for a in arrays]


def _other_chips(x, y):
    return [(1 - x, y), (x, 1 - y), (1 - x, 1 - y)]


def _device_index(chip, c):
    return 4 * chip[0] + 2 * chip[1] + c


def _chip_index(chip):
    return 2 * chip[0] + chip[1]


class _Exchange:
    def __init__(self, gather, all_cores=False):
        self.gather = gather
        self.all_cores = all_cores
        self.n_peers = N_DEV - 1 if all_cores else 3

    def peers(self, x, y, c):
        if self.all_cores:
            return [(x ^ (k >> 2), y ^ ((k >> 1) & 1), c ^ (k & 1)) for k in range(1, N_DEV)]
        return [(*chip, c) for chip in _other_chips(x, y)]

    def sent(self, x, y, c, peer):
        return _device_index((x, y), c) if self.gather else _chip_index(peer[:2])

    def lands_at(self, x, y, c):
        return _device_index((x, y), c) if self.gather else _chip_index((x, y))

    def arrives_at(self, peer):
        return _device_index(peer[:2], peer[2]) if self.gather else _chip_index(peer[:2])


def _ici_copies_start(sets, sources, landings, exchanges, name, deps=()):
    n = len(landings)
    arrays = (list(sources) if sources is not None else []) + list(landings)
    first_land = len(arrays) - n
    n_sets = len(sets)
    n_deps = len(deps)

    def body(*refs):
        refs = refs[n_deps:]
        src, land = refs[:n], refs[first_land:first_land + n]
        sems = refs[len(arrays):len(arrays) + 2 * n_sets]
        token = refs[-1]
        x, y, c = _position()
        for s, (members, exchange) in enumerate(zip(sets, exchanges)):
            for slot, (k, rows) in enumerate(members):
                for j, peer in enumerate(exchange.peers(x, y, c)):
                    at = exchange.n_peers * slot + j
                    pltpu.make_async_remote_copy(
                        src_ref=_shard_rows(src[k], exchange.sent(x, y, c, peer), rows),
                        dst_ref=_shard_rows(land[k], exchange.lands_at(x, y, c), rows),
                        send_sem=sems[2 * s].at[at], recv_sem=sems[2 * s + 1].at[at],
                        device_id=peer, device_id_type=MESH_ID).start()
        token[...] = jnp.zeros_like(token)

    sem_shapes = []
    for members, exchange in zip(sets, exchanges):
        sem_shapes += [pltpu.SemaphoreType.DMA((exchange.n_peers * len(members),))] * 2
    out = pl.pallas_call(
        body, name=name,
        out_shape=sem_shapes + _hbm_like(arrays) + [TOKEN],
        in_specs=[ANY] * n_deps + [HBM] * len(arrays),
        out_specs=[SEM] * (2 * n_sets) + [HBM] * len(arrays) + [pl.BlockSpec(memory_space=pltpu.VMEM)],
        input_output_aliases={n_deps + i: 2 * n_sets + i for i in range(len(arrays))},
        compiler_params=pltpu.CompilerParams(has_side_effects=DATAFLOW),
    )(*deps, *[_in_hbm(a) for a in arrays])
    sems = [(out[2 * s], out[2 * s + 1]) for s in range(n_sets)]
    thru = list(out[2 * n_sets:2 * n_sets + len(arrays)])
    return sems, (thru[:first_land] if sources is not None else None), thru[first_land:], out[-1]


def _ici_copies_wait(sems, members, sources, landings, exchange, after, name):
    n = len(landings)
    arrays = (list(sources) if sources is not None else []) + list(landings)
    first_land = len(arrays) - n

    def body(*refs):
        src, land = refs[:n], refs[first_land:first_land + n]
        send_sems, recv_sems = refs[len(arrays)], refs[len(arrays) + 1]
        x, y, c = _position()
        for slot, rows in enumerate(members):
            for j, peer in enumerate(exchange.peers(x, y, c)):
                at = exchange.n_peers * slot + j
                cp = pltpu.make_async_remote_copy(
                    src_ref=_shard_rows(src[slot], exchange.sent(x, y, c, peer), rows),
                    dst_ref=_shard_rows(land[slot], exchange.arrives_at(peer), rows),
                    send_sem=send_sems.at[at], recv_sem=recv_sems.at[at], device_id=peer, device_id_type=MESH_ID)
                cp.wait_send()
                cp.wait_recv()

    out = pl.pallas_call(
        body, name=name, out_shape=_hbm_like(arrays),
        in_specs=[HBM] * len(arrays) + [SEM, SEM] + [ANY] * len(after), out_specs=[HBM] * len(arrays),
        input_output_aliases={i: i for i in range(len(arrays))},
        compiler_params=pltpu.CompilerParams(has_side_effects=DATAFLOW),
    )(*arrays, sems[0], sems[1], *after)
    return list(out[first_land:])


def _d2d_gather(buffers, rows, name):
    n = len(buffers)

    def body(*refs):
        land = refs[n:2 * n]
        send_sems, recv_sems = refs[2 * n:]
        x, y, c = _position()
        chips = [(x, y)] + _other_chips(x, y)
        sends, recvs = [], []
        for k in range(n):
            for j, chip in enumerate(chips):
                for copies, core in ((sends, c), (recvs, 1 - c)):
                    block = _shard_rows(land[k], _device_index(chip, core), rows[k])
                    copies.append(pltpu.make_async_remote_copy(
                        src_ref=block, dst_ref=block, send_sem=send_sems.at[k, j], recv_sem=recv_sems.at[k, j],
                        device_id=(x, y, 1 - c), device_id_type=MESH_ID))
        for cp in sends:
            cp.start()
        for cp in recvs:
            cp.wait_recv()
        for cp in sends:
            cp.wait_send()

    return pl.pallas_call(
        body, name=name, out_shape=[jax.ShapeDtypeStruct(a.shape, a.dtype) for a in buffers],
        in_specs=[ANY] * n, out_specs=[ANY] * n, input_output_aliases={i: i for i in range(n)},
        scratch_shapes=[pltpu.SemaphoreType.DMA((n, 4)), pltpu.SemaphoreType.DMA((n, 4))],
    )(*buffers)


def _rs_pair(grads, name):
    n = len(grads)
    rows = [g.shape[0] // N_DEV for g in grads]

    def body(*refs):
        ins, outs = refs[:n], refs[n:2 * n]
        send_sems, recv_sems = refs[2 * n:]
        x, y, c = _position()
        copies = []
        for k in range(n):
            for q in range(4):
                copies.append(pltpu.make_async_remote_copy(
                    src_ref=_shard_rows(ins[k], 2 * q + 1 - c, rows[k]), dst_ref=_shard_rows(outs[k], q, rows[k]),
                    send_sem=send_sems.at[k, q], recv_sem=recv_sems.at[k, q], device_id=(x, y, 1 - c),
                    device_id_type=MESH_ID))
        for cp in copies:
            cp.start()
        for cp in copies:
            cp.wait()

    return pl.pallas_call(
        body, out_shape=[jax.ShapeDtypeStruct((4 * r, g.shape[1]), g.dtype) for g, r in zip(grads, rows)],
        in_specs=[ANY] * n, out_specs=[ANY] * n,
        scratch_shapes=[pltpu.SemaphoreType.DMA((n, 4)), pltpu.SemaphoreType.DMA((n, 4))],
        name=name)(*grads)


def _wgrad_pair(lhs, rhs, name, *, lhs_is_transposed, deps=()):
    t = rhs.shape[0]
    n = lhs.shape[0] if lhs_is_transposed else lhs.shape[1]
    r = n // N_DEV
    n_chips = N_DEV // 2
    per = 1 if (2 * r) % BLK == 0 else 2
    steps = n_chips // per

    def body(l_ref, r_ref, kept_ref, recv_ref, res, send_sems, recv_sems):
        q = pl.program_id(0)
        slot = q % 2
        x, y, c = _position()

        def send(step, buf, i):
            return pltpu.make_async_remote_copy(
                src_ref=res.at[buf, pl.ds(pl.multiple_of((2 * i + 1 - c) * r, 16), r), :],
                dst_ref=_shard_rows(recv_ref, step * per + i, r),
                send_sem=send_sems.at[buf, i], recv_sem=recv_sems.at[step * per + i],
                device_id=(x, y, 1 - c), device_id_type=MESH_ID)

        @pl.when(q >= 2)
        def _():
            for i in range(per):
                send(q - 2, slot, i).wait_send()

        if lhs_is_transposed:
            res[slot] = _dot(l_ref[...], r_ref[...]).astype(BF)
        else:
            res[slot] = _dot_tn(l_ref[...], r_ref[...]).astype(BF)
        for i in range(per):
            kept_ref[i * r:(i + 1) * r, :] = res[slot, pl.ds(pl.multiple_of((2 * i + c) * r, 16), r), :]
            send(q, slot, i).start()

        @pl.when(q == steps - 1)
        def _():
            for i in range(per):
                if steps > 1:
                    send(q - 1, 1 - slot, i).wait_send()
                send(q, slot, i).wait_send()
            for chip in range(n_chips):
                send(chip // per, 0, chip % per).wait_recv()

    width = 2 * r * per
    lhs_spec = pl.BlockSpec((width, t), lambda q: (q, 0)) if lhs_is_transposed else pl.BlockSpec((t, width), lambda q: (0, q))
    return _call(
        body, deps, (lhs, rhs), grid=(steps,),
        in_specs=[lhs_spec, _resident((t, D))],
        out_specs=[pl.BlockSpec((per * r, D), lambda q: (q, 0)), ANY],
        out_shape=[jax.ShapeDtypeStruct((n // 2, D), BF)] * 2,
        scratch_shapes=[pltpu.VMEM((2, width, D), BF), pltpu.SemaphoreType.DMA((2, per)),
                        pltpu.SemaphoreType.DMA((n_chips,))],
        compiler_params=_params(1), name=name)


def _pair_add(grad, received, place, name, kept_only=False):
    r = received.shape[0] // 4
    parity = 0 if kept_only else 1

    def body(place_ref, g_ref, r_ref, o_ref, land_ref):
        total = (g_ref[...].astype(F32) + r_ref[...].astype(F32)).astype(BF)
        o_ref[...] = total

        @pl.when(pl.program_id(0) == place_ref[1])
        def _():
            land_ref[...] = total

    return pl.pallas_call(
        body,
        grid_spec=pltpu.PrefetchScalarGridSpec(
            num_scalar_prefetch=1, grid=(4,),
            in_specs=[pl.BlockSpec((r, D), lambda q, p: ((1 + parity) * q + parity * p[0], 0)),
                      pl.BlockSpec((r, D), lambda q, p: (q, 0))],
            out_specs=[pl.BlockSpec((r, D), lambda q, p: (q, 0)), pl.BlockSpec((r, D), lambda q, p: (p[1], 0))]),
        out_shape=[jax.ShapeDtypeStruct(received.shape, BF)] * 2,
        compiler_params=_params(1), name=name)(place, grad, received)


def _sum_blocks(gathered, rows):
    def body(b_ref, o_ref):
        acc = b_ref[0:rows, :]
        for d in range(1, N_DEV):
            acc = acc + b_ref[d * rows:(d + 1) * rows, :]
        o_ref[...] = acc

    return pl.pallas_call(body, out_shape=jax.ShapeDtypeStruct((rows, D), F32), name="small_sum")(gathered)


def _adamw_math(w, g, m, v):
    m = ADAM_B1 * m + (1.0 - ADAM_B1) * g
    v = ADAM_B2 * v + (1.0 - ADAM_B2) * (g * g)
    m_hat = m / (1.0 - ADAM_B1 ** ADAM_STEP)
    v_hat = v / (1.0 - ADAM_B2 ** ADAM_STEP)
    delta = -ADAM_LR * (m_hat / (jnp.sqrt(v_hat) + ADAM_EPS) + ADAM_WD * w)
    return delta, m, v


def _sum_partials(blocks):
    g = blocks[0].astype(F32)
    for blk in blocks[1:]:
        g = g + blk.astype(F32)
    return g


def _reduce_adamw(landed, w, m, v, name):
    r = w.shape[0]
    tr = 352 if r % 352 == 0 else r
    per = r // tr

    def body(r0, r1, r2, r3, w_ref, m_ref, v_ref, g_ref, d_ref, nm_ref, nv_ref):
        g = _sum_partials([r0[...], r1[...], r2[...], r3[...]])
        g_ref[...] = g
        d_ref[...], nm_ref[...], nv_ref[...] = _adamw_math(w_ref[...], g, m_ref[...], v_ref[...])

    tile = _row_tile(tr, D)
    return pl.pallas_call(
        body, grid=(per,),
        in_specs=[pl.BlockSpec((tr, D), lambda i, q=q: (q * per + i, 0)) for q in range(4)] + [tile] * 3,
        out_specs=[tile] * 4, out_shape=[jax.ShapeDtypeStruct(w.shape, F32)] * 4,
        compiler_params=_params(1), name=name)(landed, landed, landed, landed, w, m, v)


def _adamw_small(w, g, m, v, name):
    def body(w_ref, g_ref, m_ref, v_ref, d_ref, nm_ref, nv_ref):
        d_ref[...], nm_ref[...], nv_ref[...] = _adamw_math(w_ref[...], g_ref[...], m_ref[...], v_ref[...])

    return pl.pallas_call(body, out_shape=[jax.ShapeDtypeStruct(w.shape, F32)] * 3, name=name)(w, g, m, v)


WEIGHTS = ("ffn1_norm", "ffn1_w_in", "ffn1_w_out", "mix_norm", "w_in", "conv_dw_kernel", "conv_dw_bias", "conv_ln_g",
           "conv_ln_b", "conv_w_proj", "q_norm", "k_norm", "attn_sinks", "rel_bias", "attn_w_o", "w_out", "ffn2_norm",
           "ffn2_w_in", "ffn2_w_out")
MATRICES = ("ffn1_w_in", "ffn1_w_out", "w_in", "conv_w_proj", "attn_w_o", "w_out", "ffn2_w_in", "ffn2_w_out")
COLUMN_SHARDED = ("ffn1_w_in", "w_in", "ffn2_w_in")
ROW_VECTORS = ("ffn1_norm", "mix_norm", "conv_dw_bias", "conv_ln_g", "conv_ln_b", "ffn2_norm")
PACKED = (("q_norm", HD), ("k_norm", HD), ("attn_sinks", NQ), ("rel_bias", NBUCKET * NQ))
GATHER = _Exchange(gather=True)
GATHER_ALL = _Exchange(gather=True, all_cores=True)
SCATTER = _Exchange(gather=False)
GATHER_STAGES = ("ffn1_in", "ffn1_out", "mix", "ffn2")
STAGE_GATHER = {"ffn1_in": GATHER, "ffn1_out": GATHER, "mix": GATHER, "ffn2": GATHER_ALL}
STAGE_MEMBERS = {"ffn1_in": ("ffn1_w_in",), "ffn1_out": ("ffn1_w_out",),
                 "mix": ("w_in", "conv_w_proj", "attn_w_o", "w_out", "taps"), "ffn2": ("ffn2_w_in", "ffn2_w_out")}
ROW_PACKED = len(ROW_VECTORS)
ROW_LOSS = ROW_PACKED + 1
ROW_TAPS = 8
PAYLOAD_ROWS = 48


def _pack_small(values, last_row):
    packed = jnp.concatenate([values[k].reshape(-1) for k, _ in PACKED])
    packed = jnp.pad(packed, (0, D - packed.shape[0])).reshape(1, D)
    return jnp.concatenate([values[k].reshape(1, D) for k in ROW_VECTORS] + [packed, last_row], axis=0)


def _unpack_small(rows):
    out = {k: rows[i] for i, k in enumerate(ROW_VECTORS)}
    at = 0
    for k, size in PACKED:
        out[k] = rows[ROW_PACKED, at:at + size]
        at += size
    out["rel_bias"] = out["rel_bias"].reshape(NBUCKET, NQ)
    return out


def kernel(x, ffn1_norm, ffn1_w_in, ffn1_w_out, mix_norm, w_in, conv_dw_kernel, conv_dw_bias, conv_ln_g, conv_ln_b, conv_w_proj, q_norm, k_norm, attn_sinks, rel_bias, attn_w_o, w_out, ffn2_norm, ffn2_w_in, ffn2_w_out, loss_target, m_ffn1_norm, m_ffn1_w_in, m_ffn1_w_out, m_mix_norm, m_w_in, m_conv_dw_kernel, m_conv_dw_bias, m_conv_ln_g, m_conv_ln_b, m_conv_w_proj, m_q_norm, m_k_norm, m_attn_sinks, m_rel_bias, m_attn_w_o, m_w_out, m_ffn2_norm, m_ffn2_w_in, m_ffn2_w_out, v_ffn1_norm, v_ffn1_w_in, v_ffn1_w_out, v_mix_norm, v_w_in, v_conv_dw_kernel, v_conv_dw_bias, v_conv_ln_g, v_conv_ln_b, v_conv_w_proj, v_q_norm, v_k_norm, v_attn_sinks, v_rel_bias, v_attn_w_o, v_w_out, v_ffn2_norm, v_ffn2_w_in, v_ffn2_w_out):
    w = dict(ffn1_norm=ffn1_norm, ffn1_w_in=ffn1_w_in, ffn1_w_out=ffn1_w_out, mix_norm=mix_norm, w_in=w_in,
             conv_dw_kernel=conv_dw_kernel, conv_dw_bias=conv_dw_bias, conv_ln_g=conv_ln_g, conv_ln_b=conv_ln_b,
             conv_w_proj=conv_w_proj, q_norm=q_norm, k_norm=k_norm, attn_sinks=attn_sinks, rel_bias=rel_bias,
             attn_w_o=attn_w_o, w_out=w_out, ffn2_norm=ffn2_norm, ffn2_w_in=ffn2_w_in, ffn2_w_out=ffn2_w_out)
    m = dict(ffn1_norm=m_ffn1_norm, ffn1_w_in=m_ffn1_w_in, ffn1_w_out=m_ffn1_w_out, mix_norm=m_mix_norm, w_in=m_w_in,
             conv_dw_kernel=m_conv_dw_kernel, conv_dw_bias=m_conv_dw_bias, conv_ln_g=m_conv_ln_g, conv_ln_b=m_conv_ln_b,
             conv_w_proj=m_conv_w_proj, q_norm=m_q_norm, k_norm=m_k_norm, attn_sinks=m_attn_sinks, rel_bias=m_rel_bias,
             attn_w_o=m_attn_w_o, w_out=m_w_out, ffn2_norm=m_ffn2_norm, ffn2_w_in=m_ffn2_w_in, ffn2_w_out=m_ffn2_w_out)
    v = dict(ffn1_norm=v_ffn1_norm, ffn1_w_in=v_ffn1_w_in, ffn1_w_out=v_ffn1_w_out, mix_norm=v_mix_norm, w_in=v_w_in,
             conv_dw_kernel=v_conv_dw_kernel, conv_dw_bias=v_conv_dw_bias, conv_ln_g=v_conv_ln_g, conv_ln_b=v_conv_ln_b,
             conv_w_proj=v_conv_w_proj, q_norm=v_q_norm, k_norm=v_k_norm, attn_sinks=v_attn_sinks, rel_bias=v_rel_bias,
             attn_w_o=v_attn_w_o, w_out=v_w_out, ffn2_norm=v_ffn2_norm, ffn2_w_in=v_ffn2_w_in, ffn2_w_out=v_ffn2_w_out)
    px, py, pc = _position()
    me = 4 * px + 2 * py + pc
    place = jnp.stack([pc, 2 * px + py]).astype(jnp.int32)

    rows_of = lambda k, a: a.T if k in COLUMN_SHARDED else a
    buffers = dict(zip(MATRICES + ("taps",), _prep([rows_of(k, w[k]) for k in MATRICES], conv_dw_kernel,
                                                   me.astype(jnp.int32).reshape(1))))
    landings, sets = [], []
    for stage in GATHER_STAGES:
        sets.append([(len(landings) + i, buffers[k].shape[0] // N_DEV) for i, k in enumerate(STAGE_MEMBERS[stage])])
        landings += [buffers[k] for k in STAGE_MEMBERS[stage]]
    sems, _, land_thru, _ = _ici_copies_start(sets, None, landings, [STAGE_GATHER[s] for s in GATHER_STAGES],
                                              "gather_start")

    def weights_of(stage, after):
        s = GATHER_STAGES.index(stage)
        rows = [r for _, r in sets[s]]
        landed = _ici_copies_wait(sems[s], rows, None, [land_thru[k] for k, _ in sets[s]], STAGE_GATHER[stage],
                                  list(after), "gather_wait_" + stage)
        if not STAGE_GATHER[stage].all_cores:
            landed = _d2d_gather(landed, rows, "gather_d2d_" + stage)
        out = dict(zip(STAGE_MEMBERS[stage], landed))
        if "taps" in out:
            taps = out.pop("taps")
            out["conv_dw_kernel"] = jnp.transpose(taps.reshape(N_DEV, CWP, BLK), (1, 0, 2)).reshape(CWP, D)[:CW]
        return out

    in_flight = []

    def wgrad(lhs, rhs, name, lhs_is_transposed, deps=()):
        return _wgrad_pair(lhs, rhs, name, lhs_is_transposed=lhs_is_transposed, deps=deps)

    def grads_done(stage, grads):
        names = list(grads)
        added = []
        for k in names:
            if isinstance(grads[k], (tuple, list)):
                kept, received = grads[k]
                added.append(_pair_add(kept, received, place, "pair_add_" + k, kept_only=True))
            else:
                received, = _rs_pair([grads[k]], "rs_pair_" + k)
                added.append(_pair_add(grads[k], received, place, "pair_add_" + k))
        partials = [p for p, _ in added]
        members = [(i, p.shape[0] // 4) for i, p in enumerate(partials)]
        sem, p_thru, l_thru, token = _ici_copies_start([members], partials, [l for _, l in added], [SCATTER],
                                                       "scatter_start_" + stage)
        in_flight.append((stage, names, sem[0], p_thru, l_thru, token))
        return [token]

    small = []

    def small_done(gv, sq):
        payload = jnp.concatenate([_pack_small(gv, sq), jnp.pad(gv["conv_dw_kernel"], ((0, PAYLOAD_ROWS - ROW_TAPS - CW), (0, 0)))],
                                  axis=0)
        mine = lax.dynamic_update_slice_in_dim(lax.empty((N_DEV * PAYLOAD_ROWS, D), F32), payload, me * PAYLOAD_ROWS, axis=0)
        sems, _, thru, token = _ici_copies_start([[(0, PAYLOAD_ROWS)]], None, [mine], [GATHER_ALL], "small_start")
        small.append((sems[0], thru))
        return [token]

    vec = {k: w[k] for k in WEIGHTS if k not in MATRICES and k != "conv_dw_kernel"}
    dx0 = _local_step(x[0], loss_target[0], vec, weights_of, wgrad, grads_done, small_done)
    gathered, = _ici_copies_wait(small[0][0], [PAYLOAD_ROWS], None, small[0][1], GATHER_ALL, [in_flight[-1][-1]], "small_wait")
    total = _sum_blocks(gathered, PAYLOAD_ROWS)
    loss = (0.5 / D) * jnp.sum(total[ROW_LOSS])

    grads, delta, new_m, new_v = {}, {}, {}, {}
    after = [total]
    for stage, names, sem, p_thru, l_thru, _ in in_flight:
        landed = _ici_copies_wait(sem, [p.shape[0] // 4 for p in p_thru], p_thru, l_thru, SCATTER, after,
                                  "scatter_wait_" + stage)
        after = []
        for k, buf in zip(names, landed):
            out = _reduce_adamw(buf, rows_of(k, w[k]), rows_of(k, m[k]), rows_of(k, v[k]), "adamw_" + k)
            grads[k], delta[k], new_m[k], new_v[k] = [rows_of(k, a) for a in out]
            after.append(out[1])
    zero_row = jnp.zeros((1, D), F32)
    d8, m8, v8 = _adamw_small(_pack_small(w, zero_row), total[:ROW_TAPS], _pack_small(m, zero_row),
                              _pack_small(v, zero_row), "adamw_small")
    grads.update(_unpack_small(total[:ROW_TAPS]))
    delta.update(_unpack_small(d8))
    new_m.update(_unpack_small(m8))
    new_v.update(_unpack_small(v8))
    k = "conv_dw_kernel"
    grads[k] = lax.dynamic_slice_in_dim(total[ROW_TAPS:ROW_TAPS + CW], me * BLK, BLK, axis=1)
    delta[k], new_m[k], new_v[k] = _adamw_small(w[k], grads[k], m[k], v[k], "adamw_taps")

    return (loss, dx0[None], *[grads[k] for k in WEIGHTS], *[delta[k] for k in WEIGHTS],
            *[new_m[k] for k in WEIGHTS], *[new_v[k] for k in WEIGHTS])
```

```python
import functools
import math

import numpy as np
import jax
import jax.numpy as jnp
from jax import lax
from jax.experimental import pallas as pl
from jax.experimental.pallas import tpu as pltpu

F32 = jnp.float32
BF = jnp.bfloat16

D = 1024
F = 2816
INW = 5632
CW = 31
CWP = 32
HD = 64
NQ = 16
NKV = 4
GRP = NQ // NKV
BLK = 128
NBUCKET = 32
EPS = 1e-6
NEG = float(jnp.finfo(jnp.float32).min)
QK_SCALE = 1.0 / math.sqrt(HD)
R_CONV = (0, 2048)
R_QKV = (2048, 3584)
R_Q = (2048, 3072)
R_KV = (3072, 3584)
R_GATE = (3584, 5632)

N_DEV = 8
VMEM_LIMIT_V7X = 56 * 1024 * 1024
ROW_TILE = 256
ROW_TILE_WIDE = 512

ADAM_LR = 0.001
ADAM_B1 = 0.9
ADAM_B2 = 0.999
ADAM_EPS = 1e-08
ADAM_WD = 0.01
ADAM_STEP = 10

NT_DIMS = (((1,), (1,)), ((), ()))
TN_DIMS = (((0,), (0,)), ((), ()))


def _dot(a, b):
    return jnp.dot(a, b, preferred_element_type=F32)


def _dot_nt(a, b):
    return lax.dot_general(a, b, NT_DIMS, preferred_element_type=F32)


def _dot_tn(a, b):
    return lax.dot_general(a, b, TN_DIMS, preferred_element_type=F32)


def _sig(x):
    return 0.5 * jnp.tanh(0.5 * x) + 0.5


ANY = pl.BlockSpec(memory_space=pl.ANY)


def _call(body, deps, args, **kw):
    n = len(deps)
    if n:
        kw["in_specs"] = [ANY] * n + list(kw["in_specs"])
        return pl.pallas_call(lambda *refs: body(*refs[n:]), **kw)(*deps, *args)
    return pl.pallas_call(body, **kw)(*args)


def _params(n_axes):
    return pltpu.CompilerParams(dimension_semantics=("arbitrary",) * n_axes, vmem_limit_bytes=VMEM_LIMIT_V7X)


def _resident(shape):
    zeros = (0,) * len(shape)
    return pl.BlockSpec(shape, lambda *_: zeros, pipeline_mode=pl.Buffered(1))


def _row_tile(rows, cols):
    return pl.BlockSpec((rows, cols), lambda i: (i, 0))


def _rms_stats(x):
    r = lax.rsqrt(jnp.mean(x * x, axis=-1, keepdims=True) + EPS)
    return r, x * r


def _rms_bwd(dn, x, g):
    r, xh = _rms_stats(x)
    dxh = dn * g
    dx = r * (dxh - xh * jnp.mean(dxh * xh, axis=-1, keepdims=True))
    return dx, jnp.sum(dn * xh, axis=0, keepdims=True)


def _ffn_last(x, target, g, w_in_t, w_out, name):
    t = x.shape[0]
    tm = min(ROW_TILE, t)

    def body(x_ref, t_ref, g_ref, w_ref, wo_ref, n_ref, du_ref, h_ref, dy_ref, dx_ref, sq_ref, dg_ref):
        @pl.when(pl.program_id(0) == 0)
        def _():
            sq_ref[...] = jnp.zeros_like(sq_ref)
            dg_ref[...] = jnp.zeros_like(dg_ref)

        x = x_ref[...]
        g = g_ref[...]
        r, xh = _rms_stats(x)
        n = (xh * g).astype(BF)
        n_ref[...] = n
        u = _dot_nt(n, w_ref[...])
        a = u[:, :F]
        b = u[:, F:]
        s = _sig(a)
        sa = a * s
        h = (sa * b).astype(BF)
        h_ref[...] = h
        err = x + 0.5 * _dot(h, wo_ref[...]) - t_ref[...]
        sq_ref[...] += jnp.sum(err * err, axis=0, keepdims=True)
        dxo = err * (1.0 / D)
        dy = (0.5 * dxo).astype(BF)
        dy_ref[...] = dy
        dh = _dot_nt(dy, wo_ref[...])
        du_ref[:, :F] = (dh * b * (s * (1.0 + a * (1.0 - s)))).astype(BF)
        du_ref[:, F:] = (dh * sa).astype(BF)
        dn = _dot(du_ref[...], w_ref[...])
        dxh = dn * g
        dx_ref[...] = dxo + r * (dxh - xh * jnp.mean(dxh * xh, axis=-1, keepdims=True))
        dg_ref[...] += jnp.sum(dn * xh, axis=0, keepdims=True)

    vec = pl.BlockSpec((1, D), lambda i: (0, 0))
    return pl.pallas_call(
        body, grid=(t // tm,),
        in_specs=[_row_tile(tm, D), _row_tile(tm, D), _resident((1, D)), _resident((INW, D)), _resident((F, D))],
        out_specs=[_row_tile(tm, D), _row_tile(tm, INW), _row_tile(tm, F), _row_tile(tm, D), _row_tile(tm, D), vec, vec],
        out_shape=[jax.ShapeDtypeStruct((t, D), BF), jax.ShapeDtypeStruct((t, INW), BF), jax.ShapeDtypeStruct((t, F), BF),
                   jax.ShapeDtypeStruct((t, D), BF), jax.ShapeDtypeStruct((t, D), F32), jax.ShapeDtypeStruct((1, D), F32),
                   jax.ShapeDtypeStruct((1, D), F32)],
        compiler_params=_params(1), name=name)(x, target, g, w_in_t, w_out)


def _ffn_up(x, g, w_in_t, name):
    t = x.shape[0]
    tm = min(ROW_TILE_WIDE, t)

    def body(x_ref, g_ref, w_ref, n_ref, u_ref):
        r, xh = _rms_stats(x_ref[...])
        n = (xh * g_ref[...]).astype(BF)
        n_ref[...] = n
        u_ref[...] = _dot_nt(n, w_ref[...]).astype(BF)

    return pl.pallas_call(
        body, grid=(t // tm,), in_specs=[_row_tile(tm, D), _resident((1, D)), _resident((INW, D))],
        out_specs=[_row_tile(tm, D), _row_tile(tm, INW)],
        out_shape=[jax.ShapeDtypeStruct((t, D), BF), jax.ShapeDtypeStruct((t, INW), BF)],
        compiler_params=_params(1), name=name)(x, g, w_in_t)


def _ffn_down(x, u, w_out, name):
    t = x.shape[0]
    tm = min(ROW_TILE_WIDE, t)

    def body(x_ref, u_ref, wo_ref, xo_ref):
        a = u_ref[:, :F].astype(F32)
        b = u_ref[:, F:].astype(F32)
        h = (a * _sig(a) * b).astype(BF)
        xo_ref[...] = x_ref[...] + 0.5 * _dot(h, wo_ref[...])

    return pl.pallas_call(
        body, grid=(t // tm,), in_specs=[_row_tile(tm, D), _row_tile(tm, INW), _resident((F, D))],
        out_specs=_row_tile(tm, D), out_shape=jax.ShapeDtypeStruct((t, D), F32),
        compiler_params=_params(1), name=name)(x, u, w_out)


def _ffn_bwd(dxo, x, g, u, w_in_t, w_out, name, deps=()):
    t = x.shape[0]
    tm = min(ROW_TILE, t)

    def body(dxo_ref, x_ref, g_ref, u_ref, w_ref, wo_ref, dx_ref, du_ref, h_ref, dy_ref, dg_ref):
        dxo = dxo_ref[...]
        dy = (0.5 * dxo).astype(BF)
        dy_ref[...] = dy
        dh = _dot_nt(dy, wo_ref[...])
        a = u_ref[:, :F].astype(F32)
        b = u_ref[:, F:].astype(F32)
        s = _sig(a)
        sa = a * s
        h_ref[...] = (sa * b).astype(BF)
        du_ref[:, :F] = (dh * b * (s * (1.0 + a * (1.0 - s)))).astype(BF)
        du_ref[:, F:] = (dh * sa).astype(BF)
        dn = _dot(du_ref[...], w_ref[...])
        dx, dg = _rms_bwd(dn, x_ref[...], g_ref[...])
        dx_ref[...] = dxo + dx

        @pl.when(pl.program_id(0) == 0)
        def _():
            dg_ref[...] = jnp.zeros_like(dg_ref)

        dg_ref[...] += dg

    return _call(
        body, deps, (dxo, x, g, u, w_in_t, w_out), grid=(t // tm,),
        in_specs=[_row_tile(tm, D), _row_tile(tm, D), _resident((1, D)), _row_tile(tm, INW), _resident((INW, D)),
                  _resident((F, D))],
        out_specs=[_row_tile(tm, D), _row_tile(tm, INW), _row_tile(tm, F), _row_tile(tm, D),
                   pl.BlockSpec((1, D), lambda i: (0, 0))],
        out_shape=[jax.ShapeDtypeStruct((t, D), F32), jax.ShapeDtypeStruct((t, INW), BF), jax.ShapeDtypeStruct((t, F), BF),
                   jax.ShapeDtypeStruct((t, D), BF), jax.ShapeDtypeStruct((1, D), F32)],
        compiler_params=_params(1), name=name)


def _wgrad(lhs, rhs, name, *, lhs_is_transposed, chunk, deps=()):
    t = rhs.shape[0]
    n = lhs.shape[0] if lhs_is_transposed else lhs.shape[1]
    c = min(chunk, n)

    def body(l_ref, r_ref, o_ref):
        if lhs_is_transposed:
            o_ref[...] = _dot(l_ref[...], r_ref[...]).astype(BF)
        else:
            o_ref[...] = _dot_tn(l_ref[...], r_ref[...]).astype(BF)

    lhs_spec = pl.BlockSpec((c, t), lambda j: (j, 0)) if lhs_is_transposed else pl.BlockSpec((t, c), lambda j: (0, j))
    return _call(
        body, deps, (lhs, rhs), grid=(n // c,),
        in_specs=[lhs_spec, _resident((t, D))],
        out_specs=pl.BlockSpec((c, D), lambda j: (j, 0)),
        out_shape=jax.ShapeDtypeStruct((n, D), BF),
        compiler_params=_params(1), name=name)


def _wgrad_mix(duc, dq_t, dkv_t, dgp, hm):
    t = hm.shape[0]
    c = 512
    first_q, first_kv, first_gate = R_Q[0] // c, R_KV[0] // c, R_GATE[0] // c

    def body(uc_ref, q_ref, kv_ref, gp_ref, h_ref, o_ref):
        j = pl.program_id(0)

        @pl.when(j < first_q)
        def _():
            o_ref[...] = _dot_tn(uc_ref[...], h_ref[...]).astype(BF)

        @pl.when((j >= first_q) & (j < first_kv))
        def _():
            o_ref[...] = _dot(q_ref[...], h_ref[...]).astype(BF)

        @pl.when((j >= first_kv) & (j < first_gate))
        def _():
            o_ref[...] = _dot(kv_ref[...], h_ref[...]).astype(BF)

        @pl.when(j >= first_gate)
        def _():
            o_ref[...] = _dot_tn(gp_ref[...], h_ref[...]).astype(BF)

    return pl.pallas_call(
        body, grid=(INW // c,),
        in_specs=[pl.BlockSpec((t, c), lambda j: (0, jnp.clip(j, 0, first_q - 1))),
                  pl.BlockSpec((c, t), lambda j: (jnp.clip(j - first_q, 0, first_kv - first_q - 1), 0)),
                  pl.BlockSpec((c, t), lambda j: (jnp.clip(j - first_kv, 0, first_gate - first_kv - 1), 0)),
                  pl.BlockSpec((t, c), lambda j: (0, jnp.clip(j - first_gate, 0, INW // c - first_gate - 1))),
                  _resident((t, D))],
        out_specs=pl.BlockSpec((c, D), lambda j: (j, 0)),
        out_shape=jax.ShapeDtypeStruct((INW, D), BF),
        compiler_params=_params(1), name="mix_dw_in")(duc, dq_t, dkv_t, dgp, hm)


def _mix_proj(x, g, w_t):
    t = x.shape[0]
    tm = min(ROW_TILE_WIDE, t)

    def body(x_ref, g_ref, w_ref, hm_ref, uc_ref, gp_ref, qkv_ref):
        r, xh = _rms_stats(x_ref[...])
        hm = (xh * g_ref[...]).astype(BF)
        hm_ref[...] = hm
        uc_ref[...] = _dot_nt(hm, w_ref[R_CONV[0]:R_CONV[1], :]).astype(BF)
        gp_ref[...] = _dot_nt(hm, w_ref[R_GATE[0]:R_GATE[1], :]).astype(BF)
        qkv_ref[...] = _dot_nt(w_ref[R_QKV[0]:R_QKV[1], :], hm).astype(BF)

    return pl.pallas_call(
        body, grid=(t // tm,),
        in_specs=[_row_tile(tm, D), _resident((1, D)), _resident((INW, D))],
        out_specs=[_row_tile(tm, D), _row_tile(tm, 2 * D), _row_tile(tm, 2 * D), pl.BlockSpec((1536, tm), lambda i: (0, i))],
        out_shape=[jax.ShapeDtypeStruct((t, D), BF), jax.ShapeDtypeStruct((t, 2 * D), BF),
                   jax.ShapeDtypeStruct((t, 2 * D), BF), jax.ShapeDtypeStruct((1536, t), BF)],
        compiler_params=_params(1), name="mix_proj")(x, g, w_t)


CONV_HALO = 32
CONV_LEAD = CONV_HALO - (CW - 1)


def _glu(uc):
    uc = uc.astype(F32)
    return uc[:, :D] * _sig(uc[:, D:])


def _ln_stats(zc):
    mu = jnp.mean(zc, axis=-1, keepdims=True)
    zm = zc - mu
    r = lax.rsqrt(jnp.mean(zm * zm, axis=-1, keepdims=True) + EPS)
    return r, zm * r


CONV_SHIFTS = 8
CONV_CHUNK = 32


def _store_shifted(buf, rows):
    for b in range(1, CONV_SHIFTS):
        buf[b, 0:rows - 8, :] = buf[0, pl.ds(b, rows - 8), :]


def _conv_fwd(uc, dwk, dwb, lng, lnb):
    t = uc.shape[0]
    tm = min(512, t)
    per = tm // CONV_HALO
    ext = tm + CONV_HALO

    def body(cur_ref, prev_ref, k_ref, kb_ref, g_ref, b_ref, o_ref, zc_ref, zsh):
        i = pl.program_id(0)
        zsh[0, 0:CONV_HALO, :] = _glu(prev_ref[...]) * (i > 0).astype(F32)
        zsh[0, CONV_HALO:, :] = _glu(cur_ref[...])
        _store_shifted(zsh, ext)

        def chunk(ci, carry):
            r0 = pl.multiple_of(ci * CONV_CHUNK, CONV_CHUNK)
            acc = jnp.zeros((CONV_CHUNK, D), F32) + kb_ref[...]
            for w in range(CW):
                a, b = divmod(CONV_LEAD + w, 8)
                acc = acc + k_ref[w:w + 1, :] * zsh[b, pl.ds(r0 + 8 * a, CONV_CHUNK), :]
            zc_ref[pl.ds(r0, CONV_CHUNK), :] = acc
            return carry

        lax.fori_loop(0, tm // CONV_CHUNK, chunk, 0)
        r, xh = _ln_stats(zc_ref[...])
        y = xh * g_ref[...] + b_ref[...]
        o_ref[...] = (y * _sig(y)).astype(BF)

    return pl.pallas_call(
        body, grid=(t // tm,),
        in_specs=[_row_tile(tm, 2 * D),
                  pl.BlockSpec((CONV_HALO, 2 * D), lambda i: (jnp.maximum(i * per - 1, 0), 0)),
                  _resident((CWP, D)), _resident((1, D)), _resident((1, D)), _resident((1, D))],
        out_specs=[_row_tile(tm, D), _row_tile(tm, D)],
        out_shape=[jax.ShapeDtypeStruct((t, D), BF), jax.ShapeDtypeStruct((t, D), F32)],
        scratch_shapes=[pltpu.VMEM((CONV_SHIFTS, ext, D), F32)],
        compiler_params=_params(1), name="conv_fwd")(uc, uc, dwk, dwb, lng, lnb)


def _conv_bwd(uc, zc, dzs, dwk, lng, lnb):
    t = uc.shape[0]
    tm = min(ROW_TILE_WIDE, t)
    per = tm // CONV_HALO
    n_tiles = t // tm
    ext = tm + CONV_HALO
    last_block = t // CONV_HALO - 1

    def body(cur_ref, zc_ref, zcn_ref, dz_ref, dzn_ref, k_ref, g_ref, b_ref,
             duc_ref, dk_ref, dkb_ref, dg_ref, db_ref, dsh, dk8, z_scr):
        i = pl.program_id(0)

        @pl.when(i == 0)
        def _():
            dk8[...] = jnp.zeros_like(dk8)
            dkb_ref[...] = jnp.zeros_like(dkb_ref)
            dg_ref[...] = jnp.zeros_like(dg_ref)
            db_ref[...] = jnp.zeros_like(db_ref)

        has_next = (i < n_tiles - 1).astype(F32)
        z_scr[...] = _glu(cur_ref[...])
        gain = g_ref[...]

        def ln_silu_bwd(zc, dzs, live):
            r, xh = _ln_stats(zc)
            y = xh * gain + b_ref[...]
            sy = _sig(y)
            dy = dzs * (sy * (1.0 + y * (1.0 - sy))) * live
            dxh = dy * gain
            dzc = r * (dxh - jnp.mean(dxh, axis=-1, keepdims=True) - xh * jnp.mean(dxh * xh, axis=-1, keepdims=True))
            return dzc, dy, xh

        dzc, dy, xh = ln_silu_bwd(zc_ref[...], dz_ref[...], 1.0)
        dsh[0, 0:tm, :] = dzc
        dg_ref[...] += jnp.sum(dy * xh, axis=0, keepdims=True)
        db_ref[...] += jnp.sum(dy, axis=0, keepdims=True)
        dkb_ref[...] += jnp.sum(dzc, axis=0, keepdims=True)
        dsh[0, tm:, :] = ln_silu_bwd(zcn_ref[...], dzn_ref[...], has_next)[0]
        _store_shifted(dsh, ext)

        def chunk(ci, carry):
            r0 = pl.multiple_of(ci * CONV_CHUNK, CONV_CHUNK)
            z_c = z_scr[pl.ds(r0, CONV_CHUNK), :]
            dz = jnp.zeros((CONV_CHUNK, D), F32)
            for w in range(CW):
                a, b = divmod(CW - 1 - w, 8)
                window = dsh[b, pl.ds(r0 + 8 * a, CONV_CHUNK), :]
                dz = dz + k_ref[w:w + 1, :] * window
                prod = z_c * window
                part = prod[0:8, :]
                for j in range(1, CONV_CHUNK // 8):
                    part = part + prod[8 * j:8 * j + 8, :]
                dk8[w] += part
            ucc = cur_ref[pl.ds(r0, CONV_CHUNK), :].astype(F32)
            sg = _sig(ucc[:, D:])
            duc_ref[pl.ds(r0, CONV_CHUNK), 0:D] = (dz * sg).astype(BF)
            duc_ref[pl.ds(r0, CONV_CHUNK), D:2 * D] = (dz * ucc[:, :D] * sg * (1.0 - sg)).astype(BF)
            return carry

        lax.fori_loop(0, tm // CONV_CHUNK, chunk, 0)

        @pl.when(i == n_tiles - 1)
        def _():
            dk_ref[...] = jnp.sum(dk8[...], axis=1)

    vec = pl.BlockSpec((1, D), lambda i: (0, 0))
    next_halo = pl.BlockSpec((CONV_HALO, D), lambda i: (jnp.minimum((i + 1) * per, last_block), 0))
    return pl.pallas_call(
        body, grid=(n_tiles,),
        in_specs=[_row_tile(tm, 2 * D), _row_tile(tm, D), next_halo, _row_tile(tm, D), next_halo,
                  _resident((CWP, D)), _resident((1, D)), _resident((1, D))],
        out_specs=[_row_tile(tm, 2 * D), pl.BlockSpec((CWP, D), lambda i: (0, 0)), vec, vec, vec],
        out_shape=[jax.ShapeDtypeStruct((t, 2 * D), BF), jax.ShapeDtypeStruct((CWP, D), F32),
                   jax.ShapeDtypeStruct((1, D), F32), jax.ShapeDtypeStruct((1, D), F32), jax.ShapeDtypeStruct((1, D), F32)],
        scratch_shapes=[pltpu.VMEM((CONV_SHIFTS, ext, D), F32), pltpu.VMEM((CWP, 8, D), F32), pltpu.VMEM((tm, D), F32)],
        compiler_params=_params(1), name="conv_bwd")(uc, zc, zc, dzs, dzs, dwk, lng, lnb)


def _norm_rows(xt, g):
    r = lax.rsqrt(jnp.mean(xt * xt, axis=0, keepdims=True) + EPS)
    xh = xt * r
    return xh * g, r, xh


ATT_TQ = 1024


def _attn_specs(t, tq):
    per = tq // BLK
    return [pl.BlockSpec((1536, tq), lambda i: (0, i)),
            pl.BlockSpec((512, BLK), lambda i: (2, jnp.maximum(i * per - 1, 0))),
            _resident((HD, 1)), _resident((HD, 1)), _resident((NKV, 1, GRP * BLK)),
            _resident((2, NKV, 2 * BLK, GRP * BLK))]


def _attn_window(hk, sb, qkv_ref, halo_ref, kn_cur, kn_halo):
    v0 = D + NKV * HD + hk * HD
    if sb == 0:
        k_prev = kn_halo[hk]
        v_prev = halo_ref[NKV * HD + hk * HD:NKV * HD + (hk + 1) * HD, :]
    else:
        k_prev = kn_cur[hk][:, (sb - 1) * BLK:sb * BLK]
        v_prev = qkv_ref[v0:v0 + HD, (sb - 1) * BLK:sb * BLK]
    kw = jnp.concatenate([k_prev, kn_cur[hk][:, sb * BLK:(sb + 1) * BLK]], axis=1).astype(BF)
    vw = jnp.concatenate([v_prev, qkv_ref[v0:v0 + HD, sb * BLK:(sb + 1) * BLK]], axis=1)
    return kw, vw


def _attn_probs(kw, qc, bias, sink):
    st = _dot_tn(kw, qc) + bias
    m = jnp.maximum(jnp.max(st, axis=0, keepdims=True), sink)
    p = jnp.exp(st - m)
    e_sink = jnp.exp(sink - m)
    inv = 1.0 / (jnp.sum(p, axis=0, keepdims=True) + e_sink)
    return p * inv, e_sink * inv


def _attn_fwd(qkv_t, qg, kg, sink_rows, bias_t):
    t = qkv_t.shape[1]
    tq = min(ATT_TQ, t)
    n_sub = tq // BLK

    def body(qkv_ref, halo_ref, qg_ref, kg_ref, sink_ref, bias_ref, o_ref, p_ref, ps_ref):
        i = pl.program_id(0)
        first = (i == 0).astype(jnp.int32)
        kgain = kg_ref[...]
        qgain = qg_ref[...]
        kn_cur = [_norm_rows(qkv_ref[D + h * HD:D + (h + 1) * HD, :].astype(F32), kgain)[0] for h in range(NKV)]
        kn_halo = [_norm_rows(halo_ref[h * HD:(h + 1) * HD, :].astype(F32), kgain)[0] for h in range(NKV)]
        for hk in range(NKV):
            for sb in range(n_sub):
                cols = slice(sb * BLK, (sb + 1) * BLK)
                kw, vw = _attn_window(hk, sb, qkv_ref, halo_ref, kn_cur, kn_halo)
                qc = jnp.concatenate(
                    [_norm_rows(qkv_ref[(GRP * hk + g) * HD:(GRP * hk + g + 1) * HD, cols].astype(F32), qgain)[0] * QK_SCALE
                     for g in range(GRP)], axis=1).astype(BF)
                bias = bias_ref[first, hk] if sb == 0 else bias_ref[0, hk]
                p, p_sink = _attn_probs(kw, qc, bias, sink_ref[hk])
                p = p.astype(BF)
                p_ref[sb, hk] = p
                ps_ref[sb, hk] = p_sink
                o = _dot(vw, p)
                for g in range(GRP):
                    head = GRP * hk + g
                    o_ref[head * HD:(head + 1) * HD, cols] = o[:, g * BLK:(g + 1) * BLK].astype(BF)

    return pl.pallas_call(
        body, grid=(t // tq,),
        in_specs=_attn_specs(t, tq),
        out_specs=[pl.BlockSpec((D, tq), lambda i: (0, i)),
                   pl.BlockSpec((n_sub, NKV, 2 * BLK, GRP * BLK), lambda i: (i, 0, 0, 0)),
                   pl.BlockSpec((n_sub, NKV, 1, GRP * BLK), lambda i: (i, 0, 0, 0))],
        out_shape=[jax.ShapeDtypeStruct((D, t), BF), jax.ShapeDtypeStruct((t // BLK, NKV, 2 * BLK, GRP * BLK), BF),
                   jax.ShapeDtypeStruct((t // BLK, NKV, 1, GRP * BLK), F32)],
        compiler_params=_params(1), name="attn_fwd")(qkv_t, qkv_t, qg, kg, sink_rows, bias_t)


def _attn_bwd(qkv_t, do_t, probs, sink_probs, qg, kg, deps=()):
    t = qkv_t.shape[1]
    tq = min(ATT_TQ, t)
    n_sub = tq // BLK
    n_tiles = t // tq

    def body(qkv_ref, halo_ref, do_ref, p_ref, ps_ref, qg_ref, kg_ref,
             dq_ref, ckv_ref, dqg_ref, dsink_ref, dsacc_ref, qg_scr):
        i = pl.program_id(0)

        @pl.when(i == 0)
        def _():
            qg_scr[...] = jnp.zeros_like(qg_scr)
            dsink_ref[...] = jnp.zeros_like(dsink_ref)
            dsacc_ref[...] = jnp.zeros_like(dsacc_ref)

        kgain = kg_ref[...]
        qgain = qg_ref[...]
        kn_cur = [_norm_rows(qkv_ref[D + h * HD:D + (h + 1) * HD, :].astype(F32), kgain)[0] for h in range(NKV)]
        kn_halo = [_norm_rows(halo_ref[h * HD:(h + 1) * HD, :].astype(F32), kgain)[0] for h in range(NKV)]
        dqg = jnp.zeros((HD, BLK), F32)
        for hk in range(NKV):
            for sb in range(n_sub):
                cols = slice(sb * BLK, (sb + 1) * BLK)
                kw, vw = _attn_window(hk, sb, qkv_ref, halo_ref, kn_cur, kn_halo)
                qn, qr, qh = [], [], []
                for g in range(GRP):
                    head = GRP * hk + g
                    n_, r_, h_ = _norm_rows(qkv_ref[head * HD:(head + 1) * HD, cols].astype(F32), qgain)
                    qn.append(n_)
                    qr.append(r_)
                    qh.append(h_)
                qc = (jnp.concatenate(qn, axis=1) * QK_SCALE).astype(BF)
                p_bf = p_ref[sb, hk]
                p = p_bf.astype(F32)
                doc = jnp.concatenate([do_ref[(GRP * hk + g) * HD:(GRP * hk + g + 1) * HD, cols] for g in range(GRP)], axis=1)
                dp = _dot_tn(vw, doc)
                delta = jnp.sum(p * dp, axis=0, keepdims=True)
                ds = p * (dp - delta)
                dsink_ref[hk] += -(ps_ref[sb, hk] * delta)
                dsacc_ref[hk] += ds
                dsb = ds.astype(BF)
                dqc = _dot(kw, dsb) * QK_SCALE
                ckv_ref[sb, hk * HD:(hk + 1) * HD, :] = _dot_nt(qc, dsb)
                ckv_ref[sb, NKV * HD + hk * HD:NKV * HD + (hk + 1) * HD, :] = _dot_nt(doc, p_bf)
                for g in range(GRP):
                    head = GRP * hk + g
                    dqn = dqc[:, g * BLK:(g + 1) * BLK]
                    dqh = dqn * qgain
                    dq = qr[g] * (dqh - qh[g] * jnp.mean(dqh * qh[g], axis=0, keepdims=True))
                    dq_ref[head * HD:(head + 1) * HD, cols] = dq.astype(BF)
                    dqg = dqg + dqn * qh[g]
        qg_scr[...] += dqg

        @pl.when(i == n_tiles - 1)
        def _():
            dqg_ref[...] = jnp.sum(qg_scr[...], axis=1, keepdims=True)

    return _call(
        body, deps, (qkv_t, qkv_t, do_t, probs, sink_probs, qg, kg), grid=(n_tiles,),
        in_specs=_attn_specs(t, tq)[:2] + [pl.BlockSpec((D, tq), lambda i: (0, i)),
                                           pl.BlockSpec((n_sub, NKV, 2 * BLK, GRP * BLK), lambda i: (i, 0, 0, 0)),
                                           pl.BlockSpec((n_sub, NKV, 1, GRP * BLK), lambda i: (i, 0, 0, 0))]
        + _attn_specs(t, tq)[2:4],
        out_specs=[pl.BlockSpec((D, tq), lambda i: (0, i)),
                   pl.BlockSpec((n_sub, 2 * NKV * HD, 2 * BLK), lambda i: (i, 0, 0)),
                   pl.BlockSpec((HD, 1), lambda i: (0, 0)),
                   pl.BlockSpec((NKV, 1, GRP * BLK), lambda i: (0, 0, 0)),
                   pl.BlockSpec((NKV, 2 * BLK, GRP * BLK), lambda i: (0, 0, 0))],
        out_shape=[jax.ShapeDtypeStruct((D, t), BF),
                   jax.ShapeDtypeStruct((t // BLK, 2 * NKV * HD, 2 * BLK), F32),
                   jax.ShapeDtypeStruct((HD, 1), F32),
                   jax.ShapeDtypeStruct((NKV, 1, GRP * BLK), F32),
                   jax.ShapeDtypeStruct((NKV, 2 * BLK, GRP * BLK), F32)],
        scratch_shapes=[pltpu.VMEM((HD, BLK), F32)],
        compiler_params=_params(1), name="attn_bwd")


def _kv_combine(ckv, qkv_t, kg):
    nb = ckv.shape[0]
    t = nb * BLK
    rows = NKV * HD
    per = min(4, nb)
    steps = nb // per

    def body(c_ref, cn_ref, k_ref, kg_ref, o_ref, dkg_ref, kg_scr):
        n = pl.program_id(0)

        @pl.when(n == 0)
        def _():
            kg_scr[...] = jnp.zeros_like(kg_scr)

        has_next = (n < steps - 1).astype(F32)
        kgain = kg_ref[...]
        dkg = jnp.zeros((HD, BLK), F32)
        for s in range(per):
            cols = slice(s * BLK, (s + 1) * BLK)
            after = c_ref[s + 1, :, :BLK] if s + 1 < per else cn_ref[0, :, :BLK] * has_next
            d = c_ref[s, :, BLK:] + after
            o_ref[rows:, cols] = d[rows:, :].astype(BF)
            for h in range(NKV):
                _, r, kh = _norm_rows(k_ref[h * HD:(h + 1) * HD, cols].astype(F32), kgain)
                dkn = d[h * HD:(h + 1) * HD, :]
                dkh = dkn * kgain
                o_ref[h * HD:(h + 1) * HD, cols] = (r * (dkh - kh * jnp.mean(dkh * kh, axis=0, keepdims=True))).astype(BF)
                dkg = dkg + dkn * kh
        kg_scr[...] += dkg

        @pl.when(n == steps - 1)
        def _():
            dkg_ref[...] = jnp.sum(kg_scr[...], axis=1, keepdims=True)

    return pl.pallas_call(
        body, grid=(steps,),
        in_specs=[pl.BlockSpec((per, 2 * rows, 2 * BLK), lambda n: (n, 0, 0)),
                  pl.BlockSpec((1, 2 * rows, 2 * BLK), lambda n: (jnp.minimum((n + 1) * per, nb - 1), 0, 0)),
                  pl.BlockSpec((rows, per * BLK), lambda n: (D // rows, n)),
                  _resident((HD, 1))],
        out_specs=[pl.BlockSpec((2 * rows, per * BLK), lambda n: (0, n)), pl.BlockSpec((HD, 1), lambda n: (0, 0))],
        out_shape=[jax.ShapeDtypeStruct((2 * rows, t), BF), jax.ShapeDtypeStruct((HD, 1), F32)],
        scratch_shapes=[pltpu.VMEM((HD, BLK), F32)],
        compiler_params=_params(1), name="kv_combine")(ckv, ckv, qkv_t, kg)


def _group_lane_sums(v):
    lane_group = lax.broadcasted_iota(jnp.int32, (1, GRP * BLK), 1) // BLK
    col = lax.broadcasted_iota(jnp.int32, (1, BLK), 1)
    out = jnp.zeros((NKV, BLK), F32)
    for g in range(GRP):
        s = jnp.sum(jnp.where(lane_group == g, v, 0.0), axis=1, keepdims=True)
        out = jnp.where(col == g, s, out)
    return out


def _bias_grad(dsacc, onehot_t):
    per = 8

    def body(ds_ref, oh_ref, o_ref):
        for b in range(per):
            oh = jnp.concatenate([oh_ref[b]] * GRP, axis=1)
            o_ref[b] = _group_lane_sums(jnp.sum(ds_ref[...] * oh[None], axis=1))

    return pl.pallas_call(
        body, grid=(NBUCKET // per,),
        in_specs=[_resident((NKV, 2 * BLK, GRP * BLK)), pl.BlockSpec((per, 2 * BLK, BLK), lambda b: (b, 0, 0))],
        out_specs=pl.BlockSpec((per, NKV, BLK), lambda b: (b, 0, 0)),
        out_shape=jax.ShapeDtypeStruct((NBUCKET, NKV, BLK), F32),
        compiler_params=_params(1), name="bias_grad")(dsacc, onehot_t)


def _sink_grad(dsink_rows):
    def body(d_ref, o_ref):
        o_ref[...] = _group_lane_sums(d_ref[:, 0, :])

    return pl.pallas_call(body, out_shape=jax.ShapeDtypeStruct((NKV, BLK), F32), name="sink_grad")(dsink_rows)


def _mix_out(zs, o_t, gp, x, w_cp, w_o, w_out):
    t = x.shape[0]
    tm = min(ROW_TILE_WIDE, t)

    def body(zs_ref, ot_ref, gp_ref, x_ref, wcp_ref, wo_ref, wout_ref, xo_ref, a_ref, b_ref, m_ref):
        a = _dot(zs_ref[...], wcp_ref[...])
        b = _dot_tn(ot_ref[...], wo_ref[...])
        a_ref[...] = a.astype(BF)
        b_ref[...] = b.astype(BF)
        merged = (_sig(gp_ref[:, :D].astype(F32)) * a + _sig(gp_ref[:, D:].astype(F32)) * b).astype(BF)
        m_ref[...] = merged
        xo_ref[...] = x_ref[...] + _dot(merged, wout_ref[...])

    return pl.pallas_call(
        body, grid=(t // tm,),
        in_specs=[_row_tile(tm, D), pl.BlockSpec((D, tm), lambda i: (0, i)), _row_tile(tm, 2 * D), _row_tile(tm, D),
                  _resident((D, D)), _resident((D, D)), _resident((D, D))],
        out_specs=[_row_tile(tm, D)] * 4,
        out_shape=[jax.ShapeDtypeStruct((t, D), F32)] + [jax.ShapeDtypeStruct((t, D), BF)] * 3,
        compiler_params=_params(1), name="mix_out")(zs, o_t, gp, x, w_cp, w_o, w_out)


def _mix_out_bwd(dx, a, b, gp, w_cp, w_o, w_out, deps=()):
    t = dx.shape[0]
    tm = min(ROW_TILE_WIDE, t)

    def body(dx_ref, a_ref, b_ref, gp_ref, wcp_ref, wo_ref, wout_ref, dzs_ref, dot_ref, dgp_ref, da_ref, db_ref, dxb_ref):
        dxb = dx_ref[...].astype(BF)
        dxb_ref[...] = dxb
        dm = _dot_nt(dxb, wout_ref[...])
        gc = _sig(gp_ref[:, :D].astype(F32))
        ga = _sig(gp_ref[:, D:].astype(F32))
        da = (dm * gc).astype(BF)
        db = (dm * ga).astype(BF)
        da_ref[...] = da
        db_ref[...] = db
        dgp_ref[:, :D] = (dm * a_ref[...].astype(F32) * gc * (1.0 - gc)).astype(BF)
        dgp_ref[:, D:] = (dm * b_ref[...].astype(F32) * ga * (1.0 - ga)).astype(BF)
        dzs_ref[...] = _dot_nt(da, wcp_ref[...])
        dot_ref[...] = _dot_nt(wo_ref[...], db).astype(BF)

    return _call(
        body, deps, (dx, a, b, gp, w_cp, w_o, w_out), grid=(t // tm,),
        in_specs=[_row_tile(tm, D), _row_tile(tm, D), _row_tile(tm, D), _row_tile(tm, 2 * D),
                  _resident((D, D)), _resident((D, D)), _resident((D, D))],
        out_specs=[_row_tile(tm, D), pl.BlockSpec((D, tm), lambda i: (0, i)), _row_tile(tm, 2 * D),
                   _row_tile(tm, D), _row_tile(tm, D), _row_tile(tm, D)],
        out_shape=[jax.ShapeDtypeStruct((t, D), F32), jax.ShapeDtypeStruct((D, t), BF), jax.ShapeDtypeStruct((t, 2 * D), BF),
                   jax.ShapeDtypeStruct((t, D), BF), jax.ShapeDtypeStruct((t, D), BF), jax.ShapeDtypeStruct((t, D), BF)],
        compiler_params=_params(1), name="mix_out_bwd")


def _mix_proj_bwd(dxo, duc, dq_t, dkv_t, dgp, x, g, w_t):
    t = x.shape[0]
    tm = min(ROW_TILE_WIDE, t)

    def body(dxo_ref, duc_ref, dq_ref, dkv_ref, dgp_ref, x_ref, g_ref, w_ref, dx_ref, dg_ref):
        dn = _dot(duc_ref[...], w_ref[R_CONV[0]:R_CONV[1], :])
        dn = dn + _dot(dgp_ref[...], w_ref[R_GATE[0]:R_GATE[1], :])
        dn = dn + _dot_tn(dq_ref[...], w_ref[R_Q[0]:R_Q[1], :])
        dn = dn + _dot_tn(dkv_ref[...], w_ref[R_KV[0]:R_KV[1], :])
        dx, dg = _rms_bwd(dn, x_ref[...], g_ref[...])
        dx_ref[...] = dxo_ref[...] + dx

        @pl.when(pl.program_id(0) == 0)
        def _():
            dg_ref[...] = jnp.zeros_like(dg_ref)

        dg_ref[...] += dg

    return pl.pallas_call(
        body, grid=(t // tm,),
        in_specs=[_row_tile(tm, D), _row_tile(tm, 2 * D), pl.BlockSpec((D, tm), lambda i: (0, i)),
                  pl.BlockSpec((2 * NKV * HD, tm), lambda i: (0, i)), _row_tile(tm, 2 * D), _row_tile(tm, D),
                  _resident((1, D)), _resident((INW, D))],
        out_specs=[_row_tile(tm, D), pl.BlockSpec((1, D), lambda i: (0, 0))],
        out_shape=[jax.ShapeDtypeStruct((t, D), F32), jax.ShapeDtypeStruct((1, D), F32)],
        compiler_params=_params(1), name="mix_proj_bwd")(dxo, duc, dq_t, dkv_t, dgp, x, g, w_t)


def _attention_tables():
    kj = np.arange(2 * BLK)[:, None]
    qi = np.arange(BLK)[None, :]
    dist = qi + BLK - kj
    in_win = (dist >= 0) & (dist < BLK)
    dpos = np.maximum(dist, 0)
    max_exact = NBUCKET // 2
    dfl = np.maximum(dpos, 1).astype(np.float32)
    large = max_exact + (np.log(dfl / np.float32(max_exact)) / np.float32(math.log(BLK / max_exact))
                         * np.float32(NBUCKET - max_exact)).astype(np.int32)
    large = np.minimum(large, NBUCKET - 1)
    bucket = np.where(dpos < max_exact, dpos, large)
    onehot = (bucket[None] == np.arange(NBUCKET)[:, None, None]).astype(np.float32)
    mask = in_win.astype(np.float32)
    mask_first = mask * (kj >= BLK)
    masks = np.stack([np.tile(mask, (1, GRP)), np.tile(mask_first, (1, GRP))])
    return onehot, masks


def _bias_table(rel_bias, onehot):
    tab = jnp.einsum("bkq,bh->hkq", onehot, rel_bias, precision=lax.Precision.HIGHEST)
    tab = tab.reshape(NKV, GRP, 2 * BLK, BLK)
    return jnp.transpose(tab, (0, 2, 1, 3)).reshape(NKV, 2 * BLK, GRP * BLK)


def _local_step(x, target, vec, weights_of, wgrad, grads_done, small_done):
    onehot_np, masks_np = _attention_tables()
    onehot = jnp.asarray(onehot_np)
    masks = jnp.asarray(masks_np)
    bias_t = jnp.where(masks[:, None] > 0.5, _bias_table(vec["rel_bias"], onehot)[None], NEG)
    sink_rows = jnp.repeat(vec["attn_sinks"].reshape(NKV, 1, GRP), BLK, axis=2)
    qg = vec["q_norm"].reshape(HD, 1)
    kg = vec["k_norm"].reshape(HD, 1)
    g1 = vec["ffn1_norm"].reshape(1, D)
    gm = vec["mix_norm"].reshape(1, D)
    g2 = vec["ffn2_norm"].reshape(1, D)
    dwb = vec["conv_dw_bias"].reshape(1, D)
    lng = vec["conv_ln_g"].reshape(1, D)
    lnb = vec["conv_ln_b"].reshape(1, D)

    w1 = weights_of("ffn1_in", (bias_t, sink_rows))
    n1, u1 = _ffn_up(x, g1, w1["ffn1_w_in"], "ffn1_up")
    w1.update(weights_of("ffn1_out", (u1,)))
    x1 = _ffn_down(x, u1, w1["ffn1_w_out"], "ffn1_down")
    wm = weights_of("mix", (x1,))
    dwk = jnp.pad(wm["conv_dw_kernel"], ((0, CWP - CW), (0, 0)))
    hm, uc, gp, qkv_t = _mix_proj(x1, gm, wm["w_in"])
    zs, zc = _conv_fwd(uc, dwk, dwb, lng, lnb)
    o_t, probs, sink_probs = _attn_fwd(qkv_t, qg, kg, sink_rows, bias_t)
    x2, a, b, merged = _mix_out(zs, o_t, gp, x1, wm["conv_w_proj"], wm["attn_w_o"], wm["w_out"])
    w2 = weights_of("ffn2", (x2,))
    gv = {}
    n2, du2, h2, dy2, dx2, sq, gv["ffn2_norm"] = _ffn_last(x2, target, g2, w2["ffn2_w_in"], w2["ffn2_w_out"], "ffn2")

    deps = grads_done("ffn2", {"ffn2_w_in": wgrad(du2, n2, "ffn2_dw_in", False),
                               "ffn2_w_out": wgrad(h2, dy2, "ffn2_dw_out", False)})

    dzs, do_t, dgp, da, db, dx2b = _mix_out_bwd(dx2, a, b, gp, wm["conv_w_proj"], wm["attn_w_o"], wm["w_out"], deps=deps)
    deps = grads_done("mix_out", {"w_out": wgrad(merged, dx2b, "mix_dw_out", False),
                                  "conv_w_proj": wgrad(zs, da, "mix_dw_cp", False),
                                  "attn_w_o": wgrad(o_t, db, "mix_dw_o", True)})

    dq_t, ckv, dqg, dsink_rows, dsacc = _attn_bwd(qkv_t, do_t, probs, sink_probs, qg, kg, deps=deps)
    dkv_t, dkg = _kv_combine(ckv, qkv_t, kg)
    gv["q_norm"] = dqg.reshape(HD)
    gv["k_norm"] = dkg.reshape(HD)
    gv["attn_sinks"] = _sink_grad(dsink_rows)[:, :GRP].reshape(NQ)
    gv["rel_bias"] = _bias_grad(dsacc, onehot)[:, :, :GRP].reshape(NBUCKET, NQ)

    duc, dk_conv, gv["conv_dw_bias"], gv["conv_ln_g"], gv["conv_ln_b"] = _conv_bwd(uc, zc, dzs, dwk, lng, lnb)
    gv["conv_dw_kernel"] = dk_conv[:CW]

    dx1, gv["mix_norm"] = _mix_proj_bwd(dx2, duc, dq_t, dkv_t, dgp, x1, gm, wm["w_in"])
    deps = grads_done("mix_in", {"w_in": _wgrad_mix(duc, dq_t, dkv_t, dgp, hm)})

    dx0, du1, h1, dy1, gv["ffn1_norm"] = _ffn_bwd(dx1, x, g1, u1, w1["ffn1_w_in"], w1["ffn1_w_out"], "ffn1_bwd", deps=deps)
    for k in ("ffn1_norm", "mix_norm", "ffn2_norm", "conv_dw_bias", "conv_ln_g", "conv_ln_b"):
        gv[k] = gv[k].reshape(D)
    deps = small_done(gv, sq)
    deps = grads_done("ffn1_in", {"ffn1_w_in": wgrad(du1, n1, "ffn1_dw_in", False, deps)})
    grads_done("ffn1_out", {"ffn1_w_out": wgrad(h1, dy1, "ffn1_dw_out", False, deps)})
    return dx0


MESH_ID = pl.DeviceIdType.MESH


def _position():
    return lax.axis_index("x"), lax.axis_index("y"), lax.axis_index("c")


def _shard_rows(ref, index, rows):
    return ref.at[pl.ds(pl.multiple_of(index * rows, 16), rows), :]


def _prep(weights, taps, me):
    n = len(weights)

    def body(me_ref, *refs):
        for k in range(n):
            refs[n + 1 + k][...] = refs[k][...].astype(BF)
        refs[2 * n + 1][0:CW, :] = refs[n][...]
        refs[2 * n + 1][CW:, :] = jnp.zeros((CWP - CW, BLK), F32)

    shard_shapes = [w.shape for w in weights] + [(CWP, BLK)]
    dtypes = [BF] * n + [F32]
    ins = list(weights) + [taps]
    return pl.pallas_call(
        body,
        grid_spec=pltpu.PrefetchScalarGridSpec(
            num_scalar_prefetch=1, grid=(1,),
            in_specs=[pl.BlockSpec(a.shape, lambda i, m: (0, 0), pipeline_mode=pl.Buffered(1)) for a in ins],
            out_specs=[pl.BlockSpec(s, lambda i, m: (m[0], 0)) for s in shard_shapes]),
        out_shape=[jax.ShapeDtypeStruct((N_DEV * s[0], s[1]), d) for s, d in zip(shard_shapes, dtypes)],
        compiler_params=_params(1), name="prep")(me, *ins)


HBM = pl.BlockSpec(memory_space=pltpu.HBM)
SEM = pl.BlockSpec(memory_space=pltpu.SEMAPHORE)
DATAFLOW = pltpu.SideEffectType.DATAFLOW_SIDE_EFFECTING
TOKEN = jax.ShapeDtypeStruct((8, 128), F32)


def _in_hbm(x):
    return pltpu.with_memory_space_constraint(x, pltpu.HBM)


def _hbm_like(arrays):
    return [pltpu.HBM(a.shape, a.dtype) for a in arrays]


def _other_chips(x, y):
    return [(1 - x, y), (x, 1 - y), (1 - x, 1 - y)]


def _device_index(chip, c):
    return 4 * chip[0] + 2 * chip[1] + c


def _chip_index(chip):
    return 2 * chip[0] + chip[1]


class _Exchange:
    def __init__(self, gather, all_cores=False):
        self.gather = gather
        self.all_cores = all_cores
        self.n_peers = N_DEV - 1 if all_cores else 3

    def peers(self, x, y, c):
        if self.all_cores:
            return [(x ^ (k >> 2), y ^ ((k >> 1) & 1), c ^ (k & 1)) for k in range(1, N_DEV)]
        return [(*chip, c) for chip in _other_chips(x, y)]

    def sent(self, x, y, c, peer):
        return _device_index((x, y), c) if self.gather else _chip_index(peer[:2])

    def lands_at(self, x, y, c):
        return _device_index((x, y), c) if self.gather else _chip_index((x, y))

    def arrives_at(self, peer):
        return _device_index(peer[:2], peer[2]) if self.gather else _chip_index(peer[:2])


def _ici_copies_start(sets, sources, landings, exchanges, name, deps=()):
    n = len(landings)
    arrays = (list(sources) if sources is not None else []) + list(landings)
    first_land = len(arrays) - n
    n_sets = len(sets)
    n_deps = len(deps)

    def body(*refs):
        refs = refs[n_deps:]
        src, land = refs[:n], refs[first_land:first_land + n]
        sems = refs[len(arrays):len(arrays) + 2 * n_sets]
        token = refs[-1]
        x, y, c = _position()
        for s, (members, exchange) in enumerate(zip(sets, exchanges)):
            for slot, (k, rows) in enumerate(members):
                for j, peer in enumerate(exchange.peers(x, y, c)):
                    at = exchange.n_peers * slot + j
                    pltpu.make_async_remote_copy(
                        src_ref=_shard_rows(src[k], exchange.sent(x, y, c, peer), rows),
                        dst_ref=_shard_rows(land[k], exchange.lands_at(x, y, c), rows),
                        send_sem=sems[2 * s].at[at], recv_sem=sems[2 * s + 1].at[at],
                        device_id=peer, device_id_type=MESH_ID).start()
        token[...] = jnp.zeros_like(token)

    sem_shapes = []
    for members, exchange in zip(sets, exchanges):
        sem_shapes += [pltpu.SemaphoreType.DMA((exchange.n_peers * len(members),))] * 2
    out = pl.pallas_call(
        body, name=name,
        out_shape=sem_shapes + _hbm_like(arrays) + [TOKEN],
        in_specs=[ANY] * n_deps + [HBM] * len(arrays),
        out_specs=[SEM] * (2 * n_sets) + [HBM] * len(arrays) + [pl.BlockSpec(memory_space=pltpu.VMEM)],
        input_output_aliases={n_deps + i: 2 * n_sets + i for i in range(len(arrays))},
        compiler_params=pltpu.CompilerParams(has_side_effects=DATAFLOW),
    )(*deps, *[_in_hbm(a) for a in arrays])
    sems = [(out[2 * s], out[2 * s + 1]) for s in range(n_sets)]
    thru = list(out[2 * n_sets:2 * n_sets + len(arrays)])
    return sems, (thru[:first_land] if sources is not None else None), thru[first_land:], out[-1]


def _ici_copies_wait(sems, members, sources, landings, exchange, after, name):
    n = len(landings)
    arrays = (list(sources) if sources is not None else []) + list(landings)
    first_land = len(arrays) - n

    def body(*refs):
        src, land = refs[:n], refs[first_land:first_land + n]
        send_sems, recv_sems = refs[len(arrays)], refs[len(arrays) + 1]
        x, y, c = _position()
        for slot, rows in enumerate(members):
            for j, peer in enumerate(exchange.peers(x, y, c)):
                at = exchange.n_peers * slot + j
                cp = pltpu.make_async_remote_copy(
                    src_ref=_shard_rows(src[slot], exchange.sent(x, y, c, peer), rows),
                    dst_ref=_shard_rows(land[slot], exchange.arrives_at(peer), rows),
                    send_sem=send_sems.at[at], recv_sem=recv_sems.at[at], device_id=peer, device_id_type=MESH_ID)
                cp.wait_send()
                cp.wait_recv()

    out = pl.pallas_call(
        body, name=name, out_shape=_hbm_like(arrays),
        in_specs=[HBM] * len(arrays) + [SEM, SEM] + [ANY] * len(after), out_specs=[HBM] * len(arrays),
        input_output_aliases={i: i for i in range(len(arrays))},
        compiler_params=pltpu.CompilerParams(has_side_effects=DATAFLOW),
    )(*arrays, sems[0], sems[1], *after)
    return list(out[first_land:])


def _d2d_gather(buffers, rows, name):
    n = len(buffers)

    def body(*refs):
        land = refs[n:2 * n]
        send_sems, recv_sems = refs[2 * n:]
        x, y, c = _position()
        chips = [(x, y)] + _other_chips(x, y)
        sends, recvs = [], []
        for k in range(n):
            for j, chip in enumerate(chips):
                for copies, core in ((sends, c), (recvs, 1 - c)):
                    block = _shard_rows(land[k], _device_index(chip, core), rows[k])
                    copies.append(pltpu.make_async_remote_copy(
                        src_ref=block, dst_ref=block, send_sem=send_sems.at[k, j], recv_sem=recv_sems.at[k, j],
                        device_id=(x, y, 1 - c), device_id_type=MESH_ID))
        for cp in sends:
            cp.start()
        for cp in recvs:
            cp.wait_recv()
        for cp in sends:
            cp.wait_send()

    return pl.pallas_call(
        body, name=name, out_shape=[jax.ShapeDtypeStruct(a.shape, a.dtype) for a in buffers],
        in_specs=[ANY] * n, out_specs=[ANY] * n, input_output_aliases={i: i for i in range(n)},
        scratch_shapes=[pltpu.SemaphoreType.DMA((n, 4)), pltpu.SemaphoreType.DMA((n, 4))],
    )(*buffers)


def _rs_pair(grads, name):
    n = len(grads)
    rows = [g.shape[0] // N_DEV for g in grads]

    def body(*refs):
        ins, outs = refs[:n], refs[n:2 * n]
        send_sems, recv_sems = refs[2 * n:]
        x, y, c = _position()
        copies = []
        for k in range(n):
            for q in range(4):
                copies.append(pltpu.make_async_remote_copy(
                    src_ref=_shard_rows(ins[k], 2 * q + 1 - c, rows[k]), dst_ref=_shard_rows(outs[k], q, rows[k]),
                    send_sem=send_sems.at[k, q], recv_sem=recv_sems.at[k, q], device_id=(x, y, 1 - c),
                    device_id_type=MESH_ID))
        for cp in copies:
            cp.start()
        for cp in copies:
            cp.wait()

    return pl.pallas_call(
        body, out_shape=[jax.ShapeDtypeStruct((4 * r, g.shape[1]), g.dtype) for g, r in zip(grads, rows)],
        in_specs=[ANY] * n, out_specs=[ANY] * n,
        scratch_shapes=[pltpu.SemaphoreType.DMA((n, 4)), pltpu.SemaphoreType.DMA((n, 4))],
        name=name)(*grads)


def _wgrad_pair(lhs, rhs, name, *, lhs_is_transposed, deps=()):
    t = rhs.shape[0]
    n = lhs.shape[0] if lhs_is_transposed else lhs.shape[1]
    r = n // N_DEV
    n_chips = N_DEV // 2
    per = 1 if (2 * r) % BLK == 0 else 2
    steps = n_chips // per

    def body(l_ref, r_ref, kept_ref, recv_ref, res, send_sems, recv_sems):
        q = pl.program_id(0)
        slot = q % 2
        x, y, c = _position()

        def send(step, buf, i):
            return pltpu.make_async_remote_copy(
                src_ref=res.at[buf, pl.ds(pl.multiple_of((2 * i + 1 - c) * r, 16), r), :],
                dst_ref=_shard_rows(recv_ref, step * per + i, r),
                send_sem=send_sems.at[buf, i], recv_sem=recv_sems.at[step * per + i],
                device_id=(x, y, 1 - c), device_id_type=MESH_ID)

        @pl.when(q >= 2)
        def _():
            for i in range(per):
                send(q - 2, slot, i).wait_send()

        if lhs_is_transposed:
            res[slot] = _dot(l_ref[...], r_ref[...]).astype(BF)
        else:
            res[slot] = _dot_tn(l_ref[...], r_ref[...]).astype(BF)
        for i in range(per):
            kept_ref[i * r:(i + 1) * r, :] = res[slot, pl.ds(pl.multiple_of((2 * i + c) * r, 16), r), :]
            send(q, slot, i).start()

        @pl.when(q == steps - 1)
        def _():
            for i in range(per):
                if steps > 1:
                    send(q - 1, 1 - slot, i).wait_send()
                send(q, slot, i).wait_send()
            for chip in range(n_chips):
                send(chip // per, 0, chip % per).wait_recv()

    width = 2 * r * per
    lhs_spec = pl.BlockSpec((width, t), lambda q: (q, 0)) if lhs_is_transposed else pl.BlockSpec((t, width), lambda q: (0, q))
    return _call(
        body, deps, (lhs, rhs), grid=(steps,),
        in_specs=[lhs_spec, _resident((t, D))],
        out_specs=[pl.BlockSpec((per * r, D), lambda q: (q, 0)), ANY],
        out_shape=[jax.ShapeDtypeStruct((n // 2, D), BF)] * 2,
        scratch_shapes=[pltpu.VMEM((2, width, D), BF), pltpu.SemaphoreType.DMA((2, per)),
                        pltpu.SemaphoreType.DMA((n_chips,))],
        compiler_params=_params(1), name=name)


def _pair_add(grad, received, place, name, kept_only=False):
    r = received.shape[0] // 4
    parity = 0 if kept_only else 1

    def body(place_ref, g_ref, r_ref, o_ref, land_ref):
        total = (g_ref[...].astype(F32) + r_ref[...].astype(F32)).astype(BF)
        o_ref[...] = total

        @pl.when(pl.program_id(0) == place_ref[1])
        def _():
            land_ref[...] = total

    return pl.pallas_call(
        body,
        grid_spec=pltpu.PrefetchScalarGridSpec(
            num_scalar_prefetch=1, grid=(4,),
            in_specs=[pl.BlockSpec((r, D), lambda q, p: ((1 + parity) * q + parity * p[0], 0)),
                      pl.BlockSpec((r, D), lambda q, p: (q, 0))],
            out_specs=[pl.BlockSpec((r, D), lambda q, p: (q, 0)), pl.BlockSpec((r, D), lambda q, p: (p[1], 0))]),
        out_shape=[jax.ShapeDtypeStruct(received.shape, BF)] * 2,
        compiler_params=_params(1), name=name)(place, grad, received)


def _sum_blocks(gathered, rows):
    def body(b_ref, o_ref):
        acc = b_ref[0:rows, :]
        for d in range(1, N_DEV):
            acc = acc + b_ref[d * rows:(d + 1) * rows, :]
        o_ref[...] = acc

    return pl.pallas_call(body, out_shape=jax.ShapeDtypeStruct((rows, D), F32), name="small_sum")(gathered)


def _adamw_math(w, g, m, v):
    m = ADAM_B1 * m + (1.0 - ADAM_B1) * g
    v = ADAM_B2 * v + (1.0 - ADAM_B2) * (g * g)
    m_hat = m / (1.0 - ADAM_B1 ** ADAM_STEP)
    v_hat = v / (1.0 - ADAM_B2 ** ADAM_STEP)
    delta = -ADAM_LR * (m_hat / (jnp.sqrt(v_hat) + ADAM_EPS) + ADAM_WD * w)
    return delta, m, v


def _sum_partials(blocks):
    g = blocks[0].astype(F32)
    for blk in blocks[1:]:
        g = g + blk.astype(F32)
    return g


def _reduce_adamw(landed, w, m, v, name):
    r = w.shape[0]
    tr = 352 if r % 352 == 0 else r
    per = r // tr

    def body(r0, r1, r2, r3, w_ref, m_ref, v_ref, g_ref, d_ref, nm_ref, nv_ref):
        g = _sum_partials([r0[...], r1[...], r2[...], r3[...]])
        g_ref[...] = g
        d_ref[...], nm_ref[...], nv_ref[...] = _adamw_math(w_ref[...], g, m_ref[...], v_ref[...])

    tile = _row_tile(tr, D)
    return pl.pallas_call(
        body, grid=(per,),
        in_specs=[pl.BlockSpec((tr, D), lambda i, q=q: (q * per + i, 0)) for q in range(4)] + [tile] * 3,
        out_specs=[tile] * 4, out_shape=[jax.ShapeDtypeStruct(w.shape, F32)] * 4,
        compiler_params=_params(1), name=name)(landed, landed, landed, landed, w, m, v)


def _adamw_small(w, g, m, v, name):
    def body(w_ref, g_ref, m_ref, v_ref, d_ref, nm_ref, nv_ref):
        d_ref[...], nm_ref[...], nv_ref[...] = _adamw_math(w_ref[...], g_ref[...], m_ref[...], v_ref[...])

    return pl.pallas_call(body, out_shape=[jax.ShapeDtypeStruct(w.shape, F32)] * 3, name=name)(w, g, m, v)


WEIGHTS = ("ffn1_norm", "ffn1_w_in", "ffn1_w_out", "mix_norm", "w_in", "conv_dw_kernel", "conv_dw_bias", "conv_ln_g",
           "conv_ln_b", "conv_w_proj", "q_norm", "k_norm", "attn_sinks", "rel_bias", "attn_w_o", "w_out", "ffn2_norm",
           "ffn2_w_in", "ffn2_w_out")
MATRICES = ("ffn1_w_in", "ffn1_w_out", "w_in", "conv_w_proj", "attn_w_o", "w_out", "ffn2_w_in", "ffn2_w_out")
COLUMN_SHARDED = ("ffn1_w_in", "w_in", "ffn2_w_in")
ROW_VECTORS = ("ffn1_norm", "mix_norm", "conv_dw_bias", "conv_ln_g", "conv_ln_b", "ffn2_norm")
PACKED = (("q_norm", HD), ("k_norm", HD), ("attn_sinks", NQ), ("rel_bias", NBUCKET * NQ))
GATHER = _Exchange(gather=True)
GATHER_ALL = _Exchange(gather=True, all_cores=True)
SCATTER = _Exchange(gather=False)
GATHER_STAGES = ("ffn1_in", "ffn1_out", "mix", "ffn2")
STAGE_GATHER = {"ffn1_in": GATHER, "ffn1_out": GATHER, "mix": GATHER, "ffn2": GATHER_ALL}
STAGE_MEMBERS = {"ffn1_in": ("ffn1_w_in",), "ffn1_out": ("ffn1_w_out",),
                 "mix": ("w_in", "conv_w_proj", "attn_w_o", "w_out", "taps"), "ffn2": ("ffn2_w_in", "ffn2_w_out")}
ROW_PACKED = len(ROW_VECTORS)
ROW_LOSS = ROW_PACKED + 1
ROW_TAPS = 8
PAYLOAD_ROWS = 48


def _pack_small(values, last_row):
    packed = jnp.concatenate([values[k].reshape(-1) for k, _ in PACKED])
    packed = jnp.pad(packed, (0, D - packed.shape[0])).reshape(1, D)
    return jnp.concatenate([values[k].reshape(1, D) for k in ROW_VECTORS] + [packed, last_row], axis=0)


def _unpack_small(rows):
    out = {k: rows[i] for i, k in enumerate(ROW_VECTORS)}
    at = 0
    for k, size in PACKED:
        out[k] = rows[ROW_PACKED, at:at + size]
        at += size
    out["rel_bias"] = out["rel_bias"].reshape(NBUCKET, NQ)
    return out


def kernel(x, ffn1_norm, ffn1_w_in, ffn1_w_out, mix_norm, w_in, conv_dw_kernel, conv_dw_bias, conv_ln_g, conv_ln_b, conv_w_proj, q_norm, k_norm, attn_sinks, rel_bias, attn_w_o, w_out, ffn2_norm, ffn2_w_in, ffn2_w_out, loss_target, m_ffn1_norm, m_ffn1_w_in, m_ffn1_w_out, m_mix_norm, m_w_in, m_conv_dw_kernel, m_conv_dw_bias, m_conv_ln_g, m_conv_ln_b, m_conv_w_proj, m_q_norm, m_k_norm, m_attn_sinks, m_rel_bias, m_attn_w_o, m_w_out, m_ffn2_norm, m_ffn2_w_in, m_ffn2_w_out, v_ffn1_norm, v_ffn1_w_in, v_ffn1_w_out, v_mix_norm, v_w_in, v_conv_dw_kernel, v_conv_dw_bias, v_conv_ln_g, v_conv_ln_b, v_conv_w_proj, v_q_norm, v_k_norm, v_attn_sinks, v_rel_bias, v_attn_w_o, v_w_out, v_ffn2_norm, v_ffn2_w_in, v_ffn2_w_out):
    w = dict(ffn1_norm=ffn1_norm, ffn1_w_in=ffn1_w_in, ffn1_w_out=ffn1_w_out, mix_norm=mix_norm, w_in=w_in,
             conv_dw_kernel=conv_dw_kernel, conv_dw_bias=conv_dw_bias, conv_ln_g=conv_ln_g, conv_ln_b=conv_ln_b,
             conv_w_proj=conv_w_proj, q_norm=q_norm, k_norm=k_norm, attn_sinks=attn_sinks, rel_bias=rel_bias,
             attn_w_o=attn_w_o, w_out=w_out, ffn2_norm=ffn2_norm, ffn2_w_in=ffn2_w_in, ffn2_w_out=ffn2_w_out)
    m = dict(ffn1_norm=m_ffn1_norm, ffn1_w_in=m_ffn1_w_in, ffn1_w_out=m_ffn1_w_out, mix_norm=m_mix_norm, w_in=m_w_in,
             conv_dw_kernel=m_conv_dw_kernel, conv_dw_bias=m_conv_dw_bias, conv_ln_g=m_conv_ln_g, conv_ln_b=m_conv_ln_b,
             conv_w_proj=m_conv_w_proj, q_norm=m_q_norm, k_norm=m_k_norm, attn_sinks=m_attn_sinks, rel_bias=m_rel_bias,
             attn_w_o=m_attn_w_o, w_out=m_w_out, ffn2_norm=m_ffn2_norm, ffn2_w_in=m_ffn2_w_in, ffn2_w_out=m_ffn2_w_out)
    v = dict(ffn1_norm=v_ffn1_norm, ffn1_w_in=v_ffn1_w_in, ffn1_w_out=v_ffn1_w_out, mix_norm=v_mix_norm, w_in=v_w_in,
             conv_dw_kernel=v_conv_dw_kernel, conv_dw_bias=v_conv_dw_bias, conv_ln_g=v_conv_ln_g, conv_ln_b=v_conv_ln_b,
             conv_w_proj=v_conv_w_proj, q_norm=v_q_norm, k_norm=v_k_norm, attn_sinks=v_attn_sinks, rel_bias=v_rel_bias,
             attn_w_o=v_attn_w_o, w_out=v_w_out, ffn2_norm=v_ffn2_norm, ffn2_w_in=v_ffn2_w_in, ffn2_w_out=v_ffn2_w_out)
    px, py, pc = _position()
    me = 4 * px + 2 * py + pc
    place = jnp.stack([pc, 2 * px + py]).astype(jnp.int32)

    rows_of = lambda k, a: a.T if k in COLUMN_SHARDED else a
    buffers = dict(zip(MATRICES + ("taps",), _prep([rows_of(k, w[k]) for k in MATRICES], conv_dw_kernel,
                                                   me.astype(jnp.int32).reshape(1))))
    landings, sets = [], []
    for stage in GATHER_STAGES:
        sets.append([(len(landings) + i, buffers[k].shape[0] // N_DEV) for i, k in enumerate(STAGE_MEMBERS[stage])])
        landings += [buffers[k] for k in STAGE_MEMBERS[stage]]
    sems, _, land_thru, _ = _ici_copies_start(sets, None, landings, [STAGE_GATHER[s] for s in GATHER_STAGES],
                                              "gather_start")

    def weights_of(stage, after):
        s = GATHER_STAGES.index(stage)
        rows = [r for _, r in sets[s]]
        landed = _ici_copies_wait(sems[s], rows, None, [land_thru[k] for k, _ in sets[s]], STAGE_GATHER[stage],
                                  list(after), "gather_wait_" + stage)
        if not STAGE_GATHER[stage].all_cores:
            landed = _d2d_gather(landed, rows, "gather_d2d_" + stage)
        out = dict(zip(STAGE_MEMBERS[stage], landed))
        if "taps" in out:
            taps = out.pop("taps")
            out["conv_dw_kernel"] = jnp.transpose(taps.reshape(N_DEV, CWP, BLK), (1, 0, 2)).reshape(CWP, D)[:CW]
        return out

    in_flight = []

    def wgrad(lhs, rhs, name, lhs_is_transposed, deps=()):
        return _wgrad_pair(lhs, rhs, name, lhs_is_transposed=lhs_is_transposed, deps=deps)

    def grads_done(stage, grads):
        names = list(grads)
        added = []
        for k in names:
            if isinstance(grads[k], (tuple, list)):
                kept, received = grads[k]
                added.append(_pair_add(kept, received, place, "pair_add_" + k, kept_only=True))
            else:
                received, = _rs_pair([grads[k]], "rs_pair_" + k)
                added.append(_pair_add(grads[k], received, place, "pair_add_" + k))
        partials = [p for p, _ in added]
        members = [(i, p.shape[0] // 4) for i, p in enumerate(partials)]
        sem, p_thru, l_thru, token = _ici_copies_start([members], partials, [l for _, l in added], [SCATTER],
                                                       "scatter_start_" + stage)
        in_flight.append((stage, names, sem[0], p_thru, l_thru, token))
        return [token]

    small = []

    def small_done(gv, sq):
        payload = jnp.concatenate([_pack_small(gv, sq), jnp.pad(gv["conv_dw_kernel"], ((0, PAYLOAD_ROWS - ROW_TAPS - CW), (0, 0)))],
                                  axis=0)
        mine = lax.dynamic_update_slice_in_dim(lax.empty((N_DEV * PAYLOAD_ROWS, D), F32), payload, me * PAYLOAD_ROWS, axis=0)
        sems, _, thru, token = _ici_copies_start([[(0, PAYLOAD_ROWS)]], None, [mine], [GATHER_ALL], "small_start")
        small.append((sems[0], thru))
        return [token]

    vec = {k: w[k] for k in WEIGHTS if k not in MATRICES and k != "conv_dw_kernel"}
    dx0 = _local_step(x[0], loss_target[0], vec, weights_of, wgrad, grads_done, small_done)
    gathered, = _ici_copies_wait(small[0][0], [PAYLOAD_ROWS], None, small[0][1], GATHER_ALL, [in_flight[-1][-1]], "small_wait")
    total = _sum_blocks(gathered, PAYLOAD_ROWS)
    loss = (0.5 / D) * jnp.sum(total[ROW_LOSS])

    grads, delta, new_m, new_v = {}, {}, {}, {}
    after = [total]
    for stage, names, sem, p_thru, l_thru, _ in in_flight:
        landed = _ici_copies_wait(sem, [p.shape[0] // 4 for p in p_thru], p_thru, l_thru, SCATTER, after,
                                  "scatter_wait_" + stage)
        after = []
        for k, buf in zip(names, landed):
            out = _reduce_adamw(buf, rows_of(k, w[k]), rows_of(k, m[k]), rows_of(k, v[k]), "adamw_" + k)
            grads[k], delta[k], new_m[k], new_v[k] = [rows_of(k, a) for a in out]
            after.append(out[1])
    zero_row = jnp.zeros((1, D), F32)
    d8, m8, v8 = _adamw_small(_pack_small(w, zero_row), total[:ROW_TAPS], _pack_small(m, zero_row),
                              _pack_small(v, zero_row), "adamw_small")
    grads.update(_unpack_small(total[:ROW_TAPS]))
    delta.update(_unpack_small(d8))
    new_m.update(_unpack_small(m8))
    new_v.update(_unpack_small(v8))
    k = "conv_dw_kernel"
    grads[k] = lax.dynamic_slice_in_dim(total[ROW_TAPS:ROW_TAPS + CW], me * BLK, BLK, axis=1)
    delta[k], new_m[k], new_v[k] = _adamw_small(w[k], grads[k], m[k], v[k], "adamw_taps")

    return (loss, dx0[None], *[grads[k] for k in WEIGHTS], *[delta[k] for k in WEIGHTS],
            *[new_m[k] for k in WEIGHTS], *[new_v[k] for k in WEIGHTS])
```

```python
import functools
import math

import numpy as np
import jax
import jax.numpy as jnp
from jax import lax
from jax.experimental import pallas as pl
from jax.experimental.pallas import tpu as pltpu

F32 = jnp.float32
BF = jnp.bfloat16

D = 1024
F = 2816
INW = 5632
CW = 31
CWP = 32
HD = 64
NQ = 16
NKV = 4
GRP = NQ // NKV
BLK = 128
NBUCKET = 32
EPS = 1e-6
NEG = float(jnp.finfo(jnp.float32).min)
QK_SCALE = 1.0 / math.sqrt(HD)
R_CONV = (0, 2048)
R_QKV = (2048, 3584)
R_Q = (2048, 3072)
R_KV = (3072, 3584)
R_GATE = (3584, 5632)

N_DEV = 8
VMEM_LIMIT_V7X = 56 * 1024 * 1024
ROW_TILE = 256
ROW_TILE_WIDE = 512
WGRAD_SUM_MAX_ROWS = 352

ADAM_LR = 0.001
ADAM_B1 = 0.9
ADAM_B2 = 0.999
ADAM_EPS = 1e-08
ADAM_WD = 0.01
ADAM_STEP = 10

NT_DIMS = (((1,), (1,)), ((), ()))
TN_DIMS = (((0,), (0,)), ((), ()))


def _dot(a, b):
    return jnp.dot(a, b, preferred_element_type=F32)


def _dot_nt(a, b):
    return lax.dot_general(a, b, NT_DIMS, preferred_element_type=F32)


def _dot_tn(a, b):
    return lax.dot_general(a, b, TN_DIMS, preferred_element_type=F32)


def _sig(x):
    return 0.5 * jnp.tanh(0.5 * x) + 0.5


ANY = pl.BlockSpec(memory_space=pl.ANY)


def _call(body, deps, args, **kw):
    n = len(deps)
    if n:
        kw["in_specs"] = [ANY] * n + list(kw["in_specs"])
        return pl.pallas_call(lambda *refs: body(*refs[n:]), **kw)(*deps, *args)
    return pl.pallas_call(body, **kw)(*args)


def _params(n_axes):
    return pltpu.CompilerParams(dimension_semantics=("arbitrary",) * n_axes, vmem_limit_bytes=VMEM_LIMIT_V7X)


def _resident(shape):
    zeros = (0,) * len(shape)
    return pl.BlockSpec(shape, lambda *_: zeros, pipeline_mode=pl.Buffered(1))


def _row_tile(rows, cols):
    return pl.BlockSpec((rows, cols), lambda i: (i, 0))


def _rms_stats(x):
    r = lax.rsqrt(jnp.mean(x * x, axis=-1, keepdims=True) + EPS)
    return r, x * r


def _rms_bwd(dn, x, g):
    r, xh = _rms_stats(x)
    dxh = dn * g
    dx = r * (dxh - xh * jnp.mean(dxh * xh, axis=-1, keepdims=True))
    return dx, jnp.sum(dn * xh, axis=0, keepdims=True)


def _ffn_last(x, target, g, w_in_t, w_out, name):
    t = x.shape[0]
    tm = min(ROW_TILE, t)

    def body(x_ref, t_ref, g_ref, w_ref, wo_ref, n_ref, du_ref, h_ref, dy_ref, dx_ref, sq_ref, dg_ref):
        @pl.when(pl.program_id(0) == 0)
        def _():
            sq_ref[...] = jnp.zeros_like(sq_ref)
            dg_ref[...] = jnp.zeros_like(dg_ref)

        x = x_ref[...]
        g = g_ref[...]
        r, xh = _rms_stats(x)
        n = (xh * g).astype(BF)
        n_ref[...] = n
        u = _dot_nt(n, w_ref[...])
        a = u[:, :F]
        b = u[:, F:]
        s = _sig(a)
        sa = a * s
        h = (sa * b).astype(BF)
        h_ref[...] = h
        err = x + 0.5 * _dot(h, wo_ref[...]) - t_ref[...]
        sq_ref[...] += jnp.sum(err * err, axis=0, keepdims=True)
        dxo = err * (1.0 / D)
        dy = (0.5 * dxo).astype(BF)
        dy_ref[...] = dy
        dh = _dot_nt(dy, wo_ref[...])
        du_ref[:, :F] = (dh * b * (s * (1.0 + a * (1.0 - s)))).astype(BF)
        du_ref[:, F:] = (dh * sa).astype(BF)
        dn = _dot(du_ref[...], w_ref[...])
        dxh = dn * g
        dx_ref[...] = dxo + r * (dxh - xh * jnp.mean(dxh * xh, axis=-1, keepdims=True))
        dg_ref[...] += jnp.sum(dn * xh, axis=0, keepdims=True)

    vec = pl.BlockSpec((1, D), lambda i: (0, 0))
    return pl.pallas_call(
        body, grid=(t // tm,),
        in_specs=[_row_tile(tm, D), _row_tile(tm, D), _resident((1, D)), _resident((INW, D)), _resident((F, D))],
        out_specs=[_row_tile(tm, D), _row_tile(tm, INW), _row_tile(tm, F), _row_tile(tm, D), _row_tile(tm, D), vec, vec],
        out_shape=[jax.ShapeDtypeStruct((t, D), BF), jax.ShapeDtypeStruct((t, INW), BF), jax.ShapeDtypeStruct((t, F), BF),
                   jax.ShapeDtypeStruct((t, D), BF), jax.ShapeDtypeStruct((t, D), F32), jax.ShapeDtypeStruct((1, D), F32),
                   jax.ShapeDtypeStruct((1, D), F32)],
        compiler_params=_params(1), name=name)(x, target, g, w_in_t, w_out)


def _ffn_up(x, g, w_in_t, name):
    t = x.shape[0]
    tm = min(ROW_TILE_WIDE, t)

    def body(x_ref, g_ref, w_ref, n_ref, u_ref):
        r, xh = _rms_stats(x_ref[...])
        n = (xh * g_ref[...]).astype(BF)
        n_ref[...] = n
        u_ref[...] = _dot_nt(n, w_ref[...]).astype(BF)

    return pl.pallas_call(
        body, grid=(t // tm,), in_specs=[_row_tile(tm, D), _resident((1, D)), _resident((INW, D))],
        out_specs=[_row_tile(tm, D), _row_tile(tm, INW)],
        out_shape=[jax.ShapeDtypeStruct((t, D), BF), jax.ShapeDtypeStruct((t, INW), BF)],
        compiler_params=_params(1), name=name)(x, g, w_in_t)


def _ffn_down(x, u, w_out, name):
    t = x.shape[0]
    tm = min(ROW_TILE_WIDE, t)

    def body(x_ref, u_ref, wo_ref, xo_ref):
        a = u_ref[:, :F].astype(F32)
        b = u_ref[:, F:].astype(F32)
        h = (a * _sig(a) * b).astype(BF)
        xo_ref[...] = x_ref[...] + 0.5 * _dot(h, wo_ref[...])

    return pl.pallas_call(
        body, grid=(t // tm,), in_specs=[_row_tile(tm, D), _row_tile(tm, INW), _resident((F, D))],
        out_specs=_row_tile(tm, D), out_shape=jax.ShapeDtypeStruct((t, D), F32),
        compiler_params=_params(1), name=name)(x, u, w_out)


def _ffn_bwd(dxo, x, g, u, w_in_t, w_out, name, deps=()):
    t = x.shape[0]
    tm = min(ROW_TILE, t)

    def body(dxo_ref, x_ref, g_ref, u_ref, w_ref, wo_ref, dx_ref, du_ref, h_ref, dy_ref, dg_ref):
        dxo = dxo_ref[...]
        dy = (0.5 * dxo).astype(BF)
        dy_ref[...] = dy
        dh = _dot_nt(dy, wo_ref[...])
        a = u_ref[:, :F].astype(F32)
        b = u_ref[:, F:].astype(F32)
        s = _sig(a)
        sa = a * s
        h_ref[...] = (sa * b).astype(BF)
        du_ref[:, :F] = (dh * b * (s * (1.0 + a * (1.0 - s)))).astype(BF)
        du_ref[:, F:] = (dh * sa).astype(BF)
        dn = _dot(du_ref[...], w_ref[...])
        dx, dg = _rms_bwd(dn, x_ref[...], g_ref[...])
        dx_ref[...] = dxo + dx

        @pl.when(pl.program_id(0) == 0)
        def _():
            dg_ref[...] = jnp.zeros_like(dg_ref)

        dg_ref[...] += dg

    return _call(
        body, deps, (dxo, x, g, u, w_in_t, w_out), grid=(t // tm,),
        in_specs=[_row_tile(tm, D), _row_tile(tm, D), _resident((1, D)), _row_tile(tm, INW), _resident((INW, D)),
                  _resident((F, D))],
        out_specs=[_row_tile(tm, D), _row_tile(tm, INW), _row_tile(tm, F), _row_tile(tm, D),
                   pl.BlockSpec((1, D), lambda i: (0, 0))],
        out_shape=[jax.ShapeDtypeStruct((t, D), F32), jax.ShapeDtypeStruct((t, INW), BF), jax.ShapeDtypeStruct((t, F), BF),
                   jax.ShapeDtypeStruct((t, D), BF), jax.ShapeDtypeStruct((1, D), F32)],
        compiler_params=_params(1), name=name)


def _wgrad(lhs, rhs, name, *, lhs_is_transposed, chunk, deps=()):
    t = rhs.shape[0]
    n = lhs.shape[0] if lhs_is_transposed else lhs.shape[1]
    c = min(chunk, n)

    def body(l_ref, r_ref, o_ref):
        if lhs_is_transposed:
            o_ref[...] = _dot(l_ref[...], r_ref[...]).astype(BF)
        else:
            o_ref[...] = _dot_tn(l_ref[...], r_ref[...]).astype(BF)

    lhs_spec = pl.BlockSpec((c, t), lambda j: (j, 0)) if lhs_is_transposed else pl.BlockSpec((t, c), lambda j: (0, j))
    return _call(
        body, deps, (lhs, rhs), grid=(n // c,),
        in_specs=[lhs_spec, _resident((t, D))],
        out_specs=pl.BlockSpec((c, D), lambda j: (j, 0)),
        out_shape=jax.ShapeDtypeStruct((n, D), BF),
        compiler_params=_params(1), name=name)


def _wgrad_mix(duc, dq_t, dkv_t, dgp, hm):
    t = hm.shape[0]
    c = 512
    first_q, first_kv, first_gate = R_Q[0] // c, R_KV[0] // c, R_GATE[0] // c

    def body(uc_ref, q_ref, kv_ref, gp_ref, h_ref, o_ref):
        j = pl.program_id(0)

        @pl.when(j < first_q)
        def _():
            o_ref[...] = _dot_tn(uc_ref[...], h_ref[...]).astype(BF)

        @pl.when((j >= first_q) & (j < first_kv))
        def _():
            o_ref[...] = _dot(q_ref[...], h_ref[...]).astype(BF)

        @pl.when((j >= first_kv) & (j < first_gate))
        def _():
            o_ref[...] = _dot(kv_ref[...], h_ref[...]).astype(BF)

        @pl.when(j >= first_gate)
        def _():
            o_ref[...] = _dot_tn(gp_ref[...], h_ref[...]).astype(BF)

    return pl.pallas_call(
        body, grid=(INW // c,),
        in_specs=[pl.BlockSpec((t, c), lambda j: (0, jnp.clip(j, 0, first_q - 1))),
                  pl.BlockSpec((c, t), lambda j: (jnp.clip(j - first_q, 0, first_kv - first_q - 1), 0)),
                  pl.BlockSpec((c, t), lambda j: (jnp.clip(j - first_kv, 0, first_gate - first_kv - 1), 0)),
                  pl.BlockSpec((t, c), lambda j: (0, jnp.clip(j - first_gate, 0, INW // c - first_gate - 1))),
                  _resident((t, D))],
        out_specs=pl.BlockSpec((c, D), lambda j: (j, 0)),
        out_shape=jax.ShapeDtypeStruct((INW, D), BF),
        compiler_params=_params(1), name="mix_dw_in")(duc, dq_t, dkv_t, dgp, hm)


def _mix_proj(x, g, w_t):
    t = x.shape[0]
    tm = min(ROW_TILE_WIDE, t)

    def body(x_ref, g_ref, w_ref, hm_ref, uc_ref, gp_ref, qkv_ref):
        r, xh = _rms_stats(x_ref[...])
        hm = (xh * g_ref[...]).astype(BF)
        hm_ref[...] = hm
        uc_ref[...] = _dot_nt(hm, w_ref[R_CONV[0]:R_CONV[1], :]).astype(BF)
        gp_ref[...] = _dot_nt(hm, w_ref[R_GATE[0]:R_GATE[1], :]).astype(BF)
        qkv_ref[...] = _dot_nt(w_ref[R_QKV[0]:R_QKV[1], :], hm).astype(BF)

    return pl.pallas_call(
        body, grid=(t // tm,),
        in_specs=[_row_tile(tm, D), _resident((1, D)), _resident((INW, D))],
        out_specs=[_row_tile(tm, D), _row_tile(tm, 2 * D), _row_tile(tm, 2 * D), pl.BlockSpec((1536, tm), lambda i: (0, i))],
        out_shape=[jax.ShapeDtypeStruct((t, D), BF), jax.ShapeDtypeStruct((t, 2 * D), BF),
                   jax.ShapeDtypeStruct((t, 2 * D), BF), jax.ShapeDtypeStruct((1536, t), BF)],
        compiler_params=_params(1), name="mix_proj")(x, g, w_t)


CONV_HALO = 32
CONV_LEAD = CONV_HALO - (CW - 1)


def _glu(uc):
    uc = uc.astype(F32)
    return uc[:, :D] * _sig(uc[:, D:])


def _ln_stats(zc):
    mu = jnp.mean(zc, axis=-1, keepdims=True)
    zm = zc - mu
    r = lax.rsqrt(jnp.mean(zm * zm, axis=-1, keepdims=True) + EPS)
    return r, zm * r


CONV_SHIFTS = 8
CONV_CHUNK = 32


def _store_shifted(buf, rows):
    for b in range(1, CONV_SHIFTS):
        buf[b, 0:rows - 8, :] = buf[0, pl.ds(b, rows - 8), :]


def _conv_fwd(uc, dwk, dwb, lng, lnb):
    t = uc.shape[0]
    tm = min(512, t)
    per = tm // CONV_HALO
    ext = tm + CONV_HALO

    def body(cur_ref, prev_ref, k_ref, kb_ref, g_ref, b_ref, o_ref, zc_ref, zsh):
        i = pl.program_id(0)
        zsh[0, 0:CONV_HALO, :] = _glu(prev_ref[...]) * (i > 0).astype(F32)
        zsh[0, CONV_HALO:, :] = _glu(cur_ref[...])
        _store_shifted(zsh, ext)

        def chunk(ci, carry):
            r0 = pl.multiple_of(ci * CONV_CHUNK, CONV_CHUNK)
            acc = jnp.zeros((CONV_CHUNK, D), F32) + kb_ref[...]
            for w in range(CW):
                a, b = divmod(CONV_LEAD + w, 8)
                acc = acc + k_ref[w:w + 1, :] * zsh[b, pl.ds(r0 + 8 * a, CONV_CHUNK), :]
            zc_ref[pl.ds(r0, CONV_CHUNK), :] = acc
            return carry

        lax.fori_loop(0, tm // CONV_CHUNK, chunk, 0)
        r, xh = _ln_stats(zc_ref[...])
        y = xh * g_ref[...] + b_ref[...]
        o_ref[...] = (y * _sig(y)).astype(BF)

    return pl.pallas_call(
        body, grid=(t // tm,),
        in_specs=[_row_tile(tm, 2 * D),
                  pl.BlockSpec((CONV_HALO, 2 * D), lambda i: (jnp.maximum(i * per - 1, 0), 0)),
                  _resident((CWP, D)), _resident((1, D)), _resident((1, D)), _resident((1, D))],
        out_specs=[_row_tile(tm, D), _row_tile(tm, D)],
        out_shape=[jax.ShapeDtypeStruct((t, D), BF), jax.ShapeDtypeStruct((t, D), F32)],
        scratch_shapes=[pltpu.VMEM((CONV_SHIFTS, ext, D), F32)],
        compiler_params=_params(1), name="conv_fwd")(uc, uc, dwk, dwb, lng, lnb)


def _conv_bwd(uc, zc, dzs, dwk, lng, lnb):
    t = uc.shape[0]
    tm = min(ROW_TILE_WIDE, t)
    per = tm // CONV_HALO
    n_tiles = t // tm
    ext = tm + CONV_HALO
    last_block = t // CONV_HALO - 1

    def body(cur_ref, zc_ref, zcn_ref, dz_ref, dzn_ref, k_ref, g_ref, b_ref,
             duc_ref, dk_ref, dkb_ref, dg_ref, db_ref, dsh, dk8, z_scr):
        i = pl.program_id(0)

        @pl.when(i == 0)
        def _():
            dk8[...] = jnp.zeros_like(dk8)
            dkb_ref[...] = jnp.zeros_like(dkb_ref)
            dg_ref[...] = jnp.zeros_like(dg_ref)
            db_ref[...] = jnp.zeros_like(db_ref)

        has_next = (i < n_tiles - 1).astype(F32)
        z_scr[...] = _glu(cur_ref[...])
        gain = g_ref[...]

        def ln_silu_bwd(zc, dzs, live):
            r, xh = _ln_stats(zc)
            y = xh * gain + b_ref[...]
            sy = _sig(y)
            dy = dzs * (sy * (1.0 + y * (1.0 - sy))) * live
            dxh = dy * gain
            dzc = r * (dxh - jnp.mean(dxh, axis=-1, keepdims=True) - xh * jnp.mean(dxh * xh, axis=-1, keepdims=True))
            return dzc, dy, xh

        dzc, dy, xh = ln_silu_bwd(zc_ref[...], dz_ref[...], 1.0)
        dsh[0, 0:tm, :] = dzc
        dg_ref[...] += jnp.sum(dy * xh, axis=0, keepdims=True)
        db_ref[...] += jnp.sum(dy, axis=0, keepdims=True)
        dkb_ref[...] += jnp.sum(dzc, axis=0, keepdims=True)
        dsh[0, tm:, :] = ln_silu_bwd(zcn_ref[...], dzn_ref[...], has_next)[0]
        _store_shifted(dsh, ext)

        def chunk(ci, carry):
            r0 = pl.multiple_of(ci * CONV_CHUNK, CONV_CHUNK)
            z_c = z_scr[pl.ds(r0, CONV_CHUNK), :]
            dz = jnp.zeros((CONV_CHUNK, D), F32)
            for w in range(CW):
                a, b = divmod(CW - 1 - w, 8)
                window = dsh[b, pl.ds(r0 + 8 * a, CONV_CHUNK), :]
                dz = dz + k_ref[w:w + 1, :] * window
                prod = z_c * window
                part = prod[0:8, :]
                for j in range(1, CONV_CHUNK // 8):
                    part = part + prod[8 * j:8 * j + 8, :]
                dk8[w] += part
            ucc = cur_ref[pl.ds(r0, CONV_CHUNK), :].astype(F32)
            sg = _sig(ucc[:, D:])
            duc_ref[pl.ds(r0, CONV_CHUNK), 0:D] = (dz * sg).astype(BF)
            duc_ref[pl.ds(r0, CONV_CHUNK), D:2 * D] = (dz * ucc[:, :D] * sg * (1.0 - sg)).astype(BF)
            return carry

        lax.fori_loop(0, tm // CONV_CHUNK, chunk, 0)

        @pl.when(i == n_tiles - 1)
        def _():
            dk_ref[...] = jnp.sum(dk8[...], axis=1)

    vec = pl.BlockSpec((1, D), lambda i: (0, 0))
    next_halo = pl.BlockSpec((CONV_HALO, D), lambda i: (jnp.minimum((i + 1) * per, last_block), 0))
    return pl.pallas_call(
        body, grid=(n_tiles,),
        in_specs=[_row_tile(tm, 2 * D), _row_tile(tm, D), next_halo, _row_tile(tm, D), next_halo,
                  _resident((CWP, D)), _resident((1, D)), _resident((1, D))],
        out_specs=[_row_tile(tm, 2 * D), pl.BlockSpec((CWP, D), lambda i: (0, 0)), vec, vec, vec],
        out_shape=[jax.ShapeDtypeStruct((t, 2 * D), BF), jax.ShapeDtypeStruct((CWP, D), F32),
                   jax.ShapeDtypeStruct((1, D), F32), jax.ShapeDtypeStruct((1, D), F32), jax.ShapeDtypeStruct((1, D), F32)],
        scratch_shapes=[pltpu.VMEM((CONV_SHIFTS, ext, D), F32), pltpu.VMEM((CWP, 8, D), F32), pltpu.VMEM((tm, D), F32)],
        compiler_params=_params(1), name="conv_bwd")(uc, zc, zc, dzs, dzs, dwk, lng, lnb)


def _norm_rows(xt, g):
    r = lax.rsqrt(jnp.mean(xt * xt, axis=0, keepdims=True) + EPS)
    xh = xt * r
    return xh * g, r, xh


ATT_TQ = 1024


def _attn_specs(t, tq):
    per = tq // BLK
    return [pl.BlockSpec((1536, tq), lambda i: (0, i)),
            pl.BlockSpec((512, BLK), lambda i: (2, jnp.maximum(i * per - 1, 0))),
            _resident((HD, 1)), _resident((HD, 1)), _resident((NKV, 1, GRP * BLK)),
            _resident((2, NKV, 2 * BLK, GRP * BLK))]


def _attn_window(hk, sb, qkv_ref, halo_ref, kn_cur, kn_halo):
    v0 = D + NKV * HD + hk * HD
    if sb == 0:
        k_prev = kn_halo[hk]
        v_prev = halo_ref[NKV * HD + hk * HD:NKV * HD + (hk + 1) * HD, :]
    else:
        k_prev = kn_cur[hk][:, (sb - 1) * BLK:sb * BLK]
        v_prev = qkv_ref[v0:v0 + HD, (sb - 1) * BLK:sb * BLK]
    kw = jnp.concatenate([k_prev, kn_cur[hk][:, sb * BLK:(sb + 1) * BLK]], axis=1).astype(BF)
    vw = jnp.concatenate([v_prev, qkv_ref[v0:v0 + HD, sb * BLK:(sb + 1) * BLK]], axis=1)
    return kw, vw


def _attn_probs(kw, qc, bias, sink):
    st = _dot_tn(kw, qc) + bias
    m = jnp.maximum(jnp.max(st, axis=0, keepdims=True), sink)
    p = jnp.exp(st - m)
    e_sink = jnp.exp(sink - m)
    inv = 1.0 / (jnp.sum(p, axis=0, keepdims=True) + e_sink)
    return p * inv, e_sink * inv


def _attn_fwd(qkv_t, qg, kg, sink_rows, bias_t):
    t = qkv_t.shape[1]
    tq = min(ATT_TQ, t)
    n_sub = tq // BLK

    def body(qkv_ref, halo_ref, qg_ref, kg_ref, sink_ref, bias_ref, o_ref, p_ref, ps_ref):
        i = pl.program_id(0)
        first = (i == 0).astype(jnp.int32)
        kgain = kg_ref[...]
        qgain = qg_ref[...]
        kn_cur = [_norm_rows(qkv_ref[D + h * HD:D + (h + 1) * HD, :].astype(F32), kgain)[0] for h in range(NKV)]
        kn_halo = [_norm_rows(halo_ref[h * HD:(h + 1) * HD, :].astype(F32), kgain)[0] for h in range(NKV)]
        for hk in range(NKV):
            for sb in range(n_sub):
                cols = slice(sb * BLK, (sb + 1) * BLK)
                kw, vw = _attn_window(hk, sb, qkv_ref, halo_ref, kn_cur, kn_halo)
                qc = jnp.concatenate(
                    [_norm_rows(qkv_ref[(GRP * hk + g) * HD:(GRP * hk + g + 1) * HD, cols].astype(F32), qgain)[0] * QK_SCALE
                     for g in range(GRP)], axis=1).astype(BF)
                bias = bias_ref[first, hk] if sb == 0 else bias_ref[0, hk]
                p, p_sink = _attn_probs(kw, qc, bias, sink_ref[hk])
                p = p.astype(BF)
                p_ref[sb, hk] = p
                ps_ref[sb, hk] = p_sink
                o = _dot(vw, p)
                for g in range(GRP):
                    head = GRP * hk + g
                    o_ref[head * HD:(head + 1) * HD, cols] = o[:, g * BLK:(g + 1) * BLK].astype(BF)

    return pl.pallas_call(
        body, grid=(t // tq,),
        in_specs=_attn_specs(t, tq),
        out_specs=[pl.BlockSpec((D, tq), lambda i: (0, i)),
                   pl.BlockSpec((n_sub, NKV, 2 * BLK, GRP * BLK), lambda i: (i, 0, 0, 0)),
                   pl.BlockSpec((n_sub, NKV, 1, GRP * BLK), lambda i: (i, 0, 0, 0))],
        out_shape=[jax.ShapeDtypeStruct((D, t), BF), jax.ShapeDtypeStruct((t // BLK, NKV, 2 * BLK, GRP * BLK), BF),
                   jax.ShapeDtypeStruct((t // BLK, NKV, 1, GRP * BLK), F32)],
        compiler_params=_params(1), name="attn_fwd")(qkv_t, qkv_t, qg, kg, sink_rows, bias_t)


def _attn_bwd(qkv_t, do_t, probs, sink_probs, qg, kg, deps=()):
    t = qkv_t.shape[1]
    tq = min(ATT_TQ, t)
    n_sub = tq // BLK
    n_tiles = t // tq

    def body(qkv_ref, halo_ref, do_ref, p_ref, ps_ref, qg_ref, kg_ref,
             dq_ref, ckv_ref, dqg_ref, dsink_ref, dsacc_ref, qg_scr):
        i = pl.program_id(0)

        @pl.when(i == 0)
        def _():
            qg_scr[...] = jnp.zeros_like(qg_scr)
            dsink_ref[...] = jnp.zeros_like(dsink_ref)
            dsacc_ref[...] = jnp.zeros_like(dsacc_ref)

        kgain = kg_ref[...]
        qgain = qg_ref[...]
        kn_cur = [_norm_rows(qkv_ref[D + h * HD:D + (h + 1) * HD, :].astype(F32), kgain)[0] for h in range(NKV)]
        kn_halo = [_norm_rows(halo_ref[h * HD:(h + 1) * HD, :].astype(F32), kgain)[0] for h in range(NKV)]
        dqg = jnp.zeros((HD, BLK), F32)
        for hk in range(NKV):
            for sb in range(n_sub):
                cols = slice(sb * BLK, (sb + 1) * BLK)
                kw, vw = _attn_window(hk, sb, qkv_ref, halo_ref, kn_cur, kn_halo)
                qn, qr, qh = [], [], []
                for g in range(GRP):
                    head = GRP * hk + g
                    n_, r_, h_ = _norm_rows(qkv_ref[head * HD:(head + 1) * HD, cols].astype(F32), qgain)
                    qn.append(n_)
                    qr.append(r_)
                    qh.append(h_)
                qc = (jnp.concatenate(qn, axis=1) * QK_SCALE).astype(BF)
                p_bf = p_ref[sb, hk]
                p = p_bf.astype(F32)
                doc = jnp.concatenate([do_ref[(GRP * hk + g) * HD:(GRP * hk + g + 1) * HD, cols] for g in range(GRP)], axis=1)
                dp = _dot_tn(vw, doc)
                delta = jnp.sum(p * dp, axis=0, keepdims=True)
                ds = p * (dp - delta)
                dsink_ref[hk] += -(ps_ref[sb, hk] * delta)
                dsacc_ref[hk] += ds
                dsb = ds.astype(BF)
                dqc = _dot(kw, dsb) * QK_SCALE
                ckv_ref[sb, hk * HD:(hk + 1) * HD, :] = _dot_nt(qc, dsb)
                ckv_ref[sb, NKV * HD + hk * HD:NKV * HD + (hk + 1) * HD, :] = _dot_nt(doc, p_bf)
                for g in range(GRP):
                    head = GRP * hk + g
                    dqn = dqc[:, g * BLK:(g + 1) * BLK]
                    dqh = dqn * qgain
                    dq = qr[g] * (dqh - qh[g] * jnp.mean(dqh * qh[g], axis=0, keepdims=True))
                    dq_ref[head * HD:(head + 1) * HD, cols] = dq.astype(BF)
                    dqg = dqg + dqn * qh[g]
        qg_scr[...] += dqg

        @pl.when(i == n_tiles - 1)
        def _():
            dqg_ref[...] = jnp.sum(qg_scr[...], axis=1, keepdims=True)

    return _call(
        body, deps, (qkv_t, qkv_t, do_t, probs, sink_probs, qg, kg), grid=(n_tiles,),
        in_specs=_attn_specs(t, tq)[:2] + [pl.BlockSpec((D, tq), lambda i: (0, i)),
                                           pl.BlockSpec((n_sub, NKV, 2 * BLK, GRP * BLK), lambda i: (i, 0, 0, 0)),
                                           pl.BlockSpec((n_sub, NKV, 1, GRP * BLK), lambda i: (i, 0, 0, 0))]
        + _attn_specs(t, tq)[2:4],
        out_specs=[pl.BlockSpec((D, tq), lambda i: (0, i)),
                   pl.BlockSpec((n_sub, 2 * NKV * HD, 2 * BLK), lambda i: (i, 0, 0)),
                   pl.BlockSpec((HD, 1), lambda i: (0, 0)),
                   pl.BlockSpec((NKV, 1, GRP * BLK), lambda i: (0, 0, 0)),
                   pl.BlockSpec((NKV, 2 * BLK, GRP * BLK), lambda i: (0, 0, 0))],
        out_shape=[jax.ShapeDtypeStruct((D, t), BF),
                   jax.ShapeDtypeStruct((t // BLK, 2 * NKV * HD, 2 * BLK), F32),
                   jax.ShapeDtypeStruct((HD, 1), F32),
                   jax.ShapeDtypeStruct((NKV, 1, GRP * BLK), F32),
                   jax.ShapeDtypeStruct((NKV, 2 * BLK, GRP * BLK), F32)],
        scratch_shapes=[pltpu.VMEM((HD, BLK), F32)],
        compiler_params=_params(1), name="attn_bwd")


def _kv_combine(ckv, qkv_t, kg):
    nb = ckv.shape[0]
    t = nb * BLK
    rows = NKV * HD
    per = min(4, nb)
    steps = nb // per

    def body(c_ref, cn_ref, k_ref, kg_ref, o_ref, dkg_ref, kg_scr):
        n = pl.program_id(0)

        @pl.when(n == 0)
        def _():
            kg_scr[...] = jnp.zeros_like(kg_scr)

        has_next = (n < steps - 1).astype(F32)
        kgain = kg_ref[...]
        dkg = jnp.zeros((HD, BLK), F32)
        for s in range(per):
            cols = slice(s * BLK, (s + 1) * BLK)
            after = c_ref[s + 1, :, :BLK] if s + 1 < per else cn_ref[0, :, :BLK] * has_next
            d = c_ref[s, :, BLK:] + after
            o_ref[rows:, cols] = d[rows:, :].astype(BF)
            for h in range(NKV):
                _, r, kh = _norm_rows(k_ref[h * HD:(h + 1) * HD, cols].astype(F32), kgain)
                dkn = d[h * HD:(h + 1) * HD, :]
                dkh = dkn * kgain
                o_ref[h * HD:(h + 1) * HD, cols] = (r * (dkh - kh * jnp.mean(dkh * kh, axis=0, keepdims=True))).astype(BF)
                dkg = dkg + dkn * kh
        kg_scr[...] += dkg

        @pl.when(n == steps - 1)
        def _():
            dkg_ref[...] = jnp.sum(kg_scr[...], axis=1, keepdims=True)

    return pl.pallas_call(
        body, grid=(steps,),
        in_specs=[pl.BlockSpec((per, 2 * rows, 2 * BLK), lambda n: (n, 0, 0)),
                  pl.BlockSpec((1, 2 * rows, 2 * BLK), lambda n: (jnp.minimum((n + 1) * per, nb - 1), 0, 0)),
                  pl.BlockSpec((rows, per * BLK), lambda n: (D // rows, n)),
                  _resident((HD, 1))],
        out_specs=[pl.BlockSpec((2 * rows, per * BLK), lambda n: (0, n)), pl.BlockSpec((HD, 1), lambda n: (0, 0))],
        out_shape=[jax.ShapeDtypeStruct((2 * rows, t), BF), jax.ShapeDtypeStruct((HD, 1), F32)],
        scratch_shapes=[pltpu.VMEM((HD, BLK), F32)],
        compiler_params=_params(1), name="kv_combine")(ckv, ckv, qkv_t, kg)


def _group_lane_sums(v):
    lane_group = lax.broadcasted_iota(jnp.int32, (1, GRP * BLK), 1) // BLK
    col = lax.broadcasted_iota(jnp.int32, (1, BLK), 1)
    out = jnp.zeros((NKV, BLK), F32)
    for g in range(GRP):
        s = jnp.sum(jnp.where(lane_group == g, v, 0.0), axis=1, keepdims=True)
        out = jnp.where(col == g, s, out)
    return out


def _bias_grad(dsacc, onehot_t):
    per = 8

    def body(ds_ref, oh_ref, o_ref):
        for b in range(per):
            oh = jnp.concatenate([oh_ref[b]] * GRP, axis=1)
            o_ref[b] = _group_lane_sums(jnp.sum(ds_ref[...] * oh[None], axis=1))

    return pl.pallas_call(
        body, grid=(NBUCKET // per,),
        in_specs=[_resident((NKV, 2 * BLK, GRP * BLK)), pl.BlockSpec((per, 2 * BLK, BLK), lambda b: (b, 0, 0))],
        out_specs=pl.BlockSpec((per, NKV, BLK), lambda b: (b, 0, 0)),
        out_shape=jax.ShapeDtypeStruct((NBUCKET, NKV, BLK), F32),
        compiler_params=_params(1), name="bias_grad")(dsacc, onehot_t)


def _sink_grad(dsink_rows):
    def body(d_ref, o_ref):
        o_ref[...] = _group_lane_sums(d_ref[:, 0, :])

    return pl.pallas_call(body, out_shape=jax.ShapeDtypeStruct((NKV, BLK), F32), name="sink_grad")(dsink_rows)


def _mix_out(zs, o_t, gp, x, w_cp, w_o, w_out):
    t = x.shape[0]
    tm = min(ROW_TILE_WIDE, t)

    def body(zs_ref, ot_ref, gp_ref, x_ref, wcp_ref, wo_ref, wout_ref, xo_ref, a_ref, b_ref, m_ref):
        a = _dot(zs_ref[...], wcp_ref[...])
        b = _dot_tn(ot_ref[...], wo_ref[...])
        a_ref[...] = a.astype(BF)
        b_ref[...] = b.astype(BF)
        merged = (_sig(gp_ref[:, :D].astype(F32)) * a + _sig(gp_ref[:, D:].astype(F32)) * b).astype(BF)
        m_ref[...] = merged
        xo_ref[...] = x_ref[...] + _dot(merged, wout_ref[...])

    return pl.pallas_call(
        body, grid=(t // tm,),
        in_specs=[_row_tile(tm, D), pl.BlockSpec((D, tm), lambda i: (0, i)), _row_tile(tm, 2 * D), _row_tile(tm, D),
                  _resident((D, D)), _resident((D, D)), _resident((D, D))],
        out_specs=[_row_tile(tm, D)] * 4,
        out_shape=[jax.ShapeDtypeStruct((t, D), F32)] + [jax.ShapeDtypeStruct((t, D), BF)] * 3,
        compiler_params=_params(1), name="mix_out")(zs, o_t, gp, x, w_cp, w_o, w_out)


def _mix_out_bwd(dx, a, b, gp, w_cp, w_o, w_out, deps=()):
    t = dx.shape[0]
    tm = min(ROW_TILE_WIDE, t)

    def body(dx_ref, a_ref, b_ref, gp_ref, wcp_ref, wo_ref, wout_ref, dzs_ref, dot_ref, dgp_ref, da_ref, db_ref, dxb_ref):
        dxb = dx_ref[...].astype(BF)
        dxb_ref[...] = dxb
        dm = _dot_nt(dxb, wout_ref[...])
        gc = _sig(gp_ref[:, :D].astype(F32))
        ga = _sig(gp_ref[:, D:].astype(F32))
        da = (dm * gc).astype(BF)
        db = (dm * ga).astype(BF)
        da_ref[...] = da
        db_ref[...] = db
        dgp_ref[:, :D] = (dm * a_ref[...].astype(F32) * gc * (1.0 - gc)).astype(BF)
        dgp_ref[:, D:] = (dm * b_ref[...].astype(F32) * ga * (1.0 - ga)).astype(BF)
        dzs_ref[...] = _dot_nt(da, wcp_ref[...])
        dot_ref[...] = _dot_nt(wo_ref[...], db).astype(BF)

    return _call(
        body, deps, (dx, a, b, gp, w_cp, w_o, w_out), grid=(t // tm,),
        in_specs=[_row_tile(tm, D), _row_tile(tm, D), _row_tile(tm, D), _row_tile(tm, 2 * D),
                  _resident((D, D)), _resident((D, D)), _resident((D, D))],
        out_specs=[_row_tile(tm, D), pl.BlockSpec((D, tm), lambda i: (0, i)), _row_tile(tm, 2 * D),
                   _row_tile(tm, D), _row_tile(tm, D), _row_tile(tm, D)],
        out_shape=[jax.ShapeDtypeStruct((t, D), F32), jax.ShapeDtypeStruct((D, t), BF), jax.ShapeDtypeStruct((t, 2 * D), BF),
                   jax.ShapeDtypeStruct((t, D), BF), jax.ShapeDtypeStruct((t, D), BF), jax.ShapeDtypeStruct((t, D), BF)],
        compiler_params=_params(1), name="mix_out_bwd")


def _mix_proj_bwd(dxo, duc, dq_t, dkv_t, dgp, x, g, w_t):
    t = x.shape[0]
    tm = min(ROW_TILE_WIDE, t)

    def body(dxo_ref, duc_ref, dq_ref, dkv_ref, dgp_ref, x_ref, g_ref, w_ref, dx_ref, dg_ref):
        dn = _dot(duc_ref[...], w_ref[R_CONV[0]:R_CONV[1], :])
        dn = dn + _dot(dgp_ref[...], w_ref[R_GATE[0]:R_GATE[1], :])
        dn = dn + _dot_tn(dq_ref[...], w_ref[R_Q[0]:R_Q[1], :])
        dn = dn + _dot_tn(dkv_ref[...], w_ref[R_KV[0]:R_KV[1], :])
        dx, dg = _rms_bwd(dn, x_ref[...], g_ref[...])
        dx_ref[...] = dxo_ref[...] + dx

        @pl.when(pl.program_id(0) == 0)
        def _():
            dg_ref[...] = jnp.zeros_like(dg_ref)

        dg_ref[...] += dg

    return pl.pallas_call(
        body, grid=(t // tm,),
        in_specs=[_row_tile(tm, D), _row_tile(tm, 2 * D), pl.BlockSpec((D, tm), lambda i: (0, i)),
                  pl.BlockSpec((2 * NKV * HD, tm), lambda i: (0, i)), _row_tile(tm, 2 * D), _row_tile(tm, D),
                  _resident((1, D)), _resident((INW, D))],
        out_specs=[_row_tile(tm, D), pl.BlockSpec((1, D), lambda i: (0, 0))],
        out_shape=[jax.ShapeDtypeStruct((t, D), F32), jax.ShapeDtypeStruct((1, D), F32)],
        compiler_params=_params(1), name="mix_proj_bwd")(dxo, duc, dq_t, dkv_t, dgp, x, g, w_t)


def _attention_tables():
    kj = np.arange(2 * BLK)[:, None]
    qi = np.arange(BLK)[None, :]
    dist = qi + BLK - kj
    in_win = (dist >= 0) & (dist < BLK)
    dpos = np.maximum(dist, 0)
    max_exact = NBUCKET // 2
    dfl = np.maximum(dpos, 1).astype(np.float32)
    large = max_exact + (np.log(dfl / np.float32(max_exact)) / np.float32(math.log(BLK / max_exact))
                         * np.float32(NBUCKET - max_exact)).astype(np.int32)
    large = np.minimum(large, NBUCKET - 1)
    bucket = np.where(dpos < max_exact, dpos, large)
    onehot = (bucket[None] == np.arange(NBUCKET)[:, None, None]).astype(np.float32)
    mask = in_win.astype(np.float32)
    mask_first = mask * (kj >= BLK)
    masks = np.stack([np.tile(mask, (1, GRP)), np.tile(mask_first, (1, GRP))])
    return onehot, masks


def _bias_table(rel_bias, onehot):
    tab = jnp.einsum("bkq,bh->hkq", onehot, rel_bias, precision=lax.Precision.HIGHEST)
    tab = tab.reshape(NKV, GRP, 2 * BLK, BLK)
    return jnp.transpose(tab, (0, 2, 1, 3)).reshape(NKV, 2 * BLK, GRP * BLK)


def _local_step(x, target, vec, weights_of, wgrad, grads_done, small_done):
    onehot_np, masks_np = _attention_tables()
    onehot = jnp.asarray(onehot_np)
    masks = jnp.asarray(masks_np)
    bias_t = jnp.where(masks[:, None] > 0.5, _bias_table(vec["rel_bias"], onehot)[None], NEG)
    sink_rows = jnp.repeat(vec["attn_sinks"].reshape(NKV, 1, GRP), BLK, axis=2)
    qg = vec["q_norm"].reshape(HD, 1)
    kg = vec["k_norm"].reshape(HD, 1)
    g1 = vec["ffn1_norm"].reshape(1, D)
    gm = vec["mix_norm"].reshape(1, D)
    g2 = vec["ffn2_norm"].reshape(1, D)
    dwb = vec["conv_dw_bias"].reshape(1, D)
    lng = vec["conv_ln_g"].reshape(1, D)
    lnb = vec["conv_ln_b"].reshape(1, D)

    w1 = weights_of("ffn1_in", (bias_t, sink_rows))
    n1, u1 = _ffn_up(x, g1, w1["ffn1_w_in"], "ffn1_up")
    w1.update(weights_of("ffn1_out", (u1,)))
    x1 = _ffn_down(x, u1, w1["ffn1_w_out"], "ffn1_down")
    wm = weights_of("mix", (x1,))
    dwk = jnp.pad(wm["conv_dw_kernel"], ((0, CWP - CW), (0, 0)))
    hm, uc, gp, qkv_t = _mix_proj(x1, gm, wm["w_in"])
    zs, zc = _conv_fwd(uc, dwk, dwb, lng, lnb)
    o_t, probs, sink_probs = _attn_fwd(qkv_t, qg, kg, sink_rows, bias_t)
    x2, a, b, merged = _mix_out(zs, o_t, gp, x1, wm["conv_w_proj"], wm["attn_w_o"], wm["w_out"])
    w2 = weights_of("ffn2", (x2,))
    gv = {}
    n2, du2, h2, dy2, dx2, sq, gv["ffn2_norm"] = _ffn_last(x2, target, g2, w2["ffn2_w_in"], w2["ffn2_w_out"], "ffn2")

    deps = grads_done("ffn2", {"ffn2_w_in": wgrad(du2, n2, "ffn2_dw_in", False),
                               "ffn2_w_out": wgrad(h2, dy2, "ffn2_dw_out", False)})

    dzs, do_t, dgp, da, db, dx2b = _mix_out_bwd(dx2, a, b, gp, wm["conv_w_proj"], wm["attn_w_o"], wm["w_out"], deps=deps)
    deps = grads_done("mix_out", {"w_out": wgrad(merged, dx2b, "mix_dw_out", False),
                                  "conv_w_proj": wgrad(zs, da, "mix_dw_cp", False),
                                  "attn_w_o": wgrad(o_t, db, "mix_dw_o", True)})

    dq_t, ckv, dqg, dsink_rows, dsacc = _attn_bwd(qkv_t, do_t, probs, sink_probs, qg, kg, deps=deps)
    dkv_t, dkg = _kv_combine(ckv, qkv_t, kg)
    gv["q_norm"] = dqg.reshape(HD)
    gv["k_norm"] = dkg.reshape(HD)
    gv["attn_sinks"] = _sink_grad(dsink_rows)[:, :GRP].reshape(NQ)
    gv["rel_bias"] = _bias_grad(dsacc, onehot)[:, :, :GRP].reshape(NBUCKET, NQ)

    duc, dk_conv, gv["conv_dw_bias"], gv["conv_ln_g"], gv["conv_ln_b"] = _conv_bwd(uc, zc, dzs, dwk, lng, lnb)
    gv["conv_dw_kernel"] = dk_conv[:CW]

    dx1, gv["mix_norm"] = _mix_proj_bwd(dx2, duc, dq_t, dkv_t, dgp, x1, gm, wm["w_in"])
    deps = grads_done("mix_in", {"w_in": _wgrad_mix(duc, dq_t, dkv_t, dgp, hm)})

    dx0, du1, h1, dy1, gv["ffn1_norm"] = _ffn_bwd(dx1, x, g1, u1, w1["ffn1_w_in"], w1["ffn1_w_out"], "ffn1_bwd", deps=deps)
    for k in ("ffn1_norm", "mix_norm", "ffn2_norm", "conv_dw_bias", "conv_ln_g", "conv_ln_b"):
        gv[k] = gv[k].reshape(D)
    deps = small_done(gv, sq)
    deps = grads_done("ffn1_in", {"ffn1_w_in": wgrad(du1, n1, "ffn1_dw_in", False, deps)})
    grads_done("ffn1_out", {"ffn1_w_out": wgrad(h1, dy1, "ffn1_dw_out", False, deps)})
    return dx0


MESH_ID = pl.DeviceIdType.MESH


def _position():
    return lax.axis_index("x"), lax.axis_index("y"), lax.axis_index("c")


def _shard_rows(ref, index, rows):
    return ref.at[pl.ds(pl.multiple_of(index * rows, 16), rows), :]


def _prep(weights, taps, me):
    n = len(weights)

    def body(me_ref, *refs):
        for k in range(n):
            refs[n + 1 + k][...] = refs[k][...].astype(BF)
        refs[2 * n + 1][0:CW, :] = refs[n][...]
        refs[2 * n + 1][CW:, :] = jnp.zeros((CWP - CW, BLK), F32)

    shard_shapes = [w.shape for w in weights] + [(CWP, BLK)]
    dtypes = [BF] * n + [F32]
    ins = list(weights) + [taps]
    return pl.pallas_call(
        body,
        grid_spec=pltpu.PrefetchScalarGridSpec(
            num_scalar_prefetch=1, grid=(1,),
            in_specs=[pl.BlockSpec(a.shape, lambda i, m: (0, 0), pipeline_mode=pl.Buffered(1)) for a in ins],
            out_specs=[pl.BlockSpec(s, lambda i, m: (m[0], 0)) for s in shard_shapes]),
        out_shape=[jax.ShapeDtypeStruct((N_DEV * s[0], s[1]), d) for s, d in zip(shard_shapes, dtypes)],
        compiler_params=_params(1), name="prep")(me, *ins)


HBM = pl.BlockSpec(memory_space=pltpu.HBM)
SEM = pl.BlockSpec(memory_space=pltpu.SEMAPHORE)
DATAFLOW = pltpu.SideEffectType.DATAFLOW_SIDE_EFFECTING
TOKEN = jax.ShapeDtypeStruct((8, 128), F32)


def _in_hbm(x):
    return pltpu.with_memory_space_constraint(x, pltpu.HBM)


def _hbm_like(arrays):
    return [pltpu.HBM(a.shape, a.dtype) for a in arrays]


def _other_chips(x, y):
    return [(1 - x, y), (x, 1 - y), (1 - x, 1 - y)]


def _device_index(chip, c):
    return 4 * chip[0] + 2 * chip[1] + c


def _chip_index(chip):
    return 2 * chip[0] + chip[1]


class _Exchange:
    def __init__(self, gather, all_cores=False):
        self.gather = gather
        self.all_cores = all_cores
        self.n_peers = N_DEV - 1 if all_cores else 3

    def peers(self, x, y, c):
        if self.all_cores:
            return [(x ^ (k >> 2), y ^ ((k >> 1) & 1), c ^ (k & 1)) for k in range(1, N_DEV)]
        return [(*chip, c) for chip in _other_chips(x, y)]

    def sent(self, x, y, c, peer):
        return _device_index((x, y), c) if self.gather else _chip_index(peer[:2])

    def lands_at(self, x, y, c):
        return _device_index((x, y), c) if self.gather else _chip_index((x, y))

    def arrives_at(self, peer):
        return _device_index(peer[:2], peer[2]) if self.gather else _chip_index(peer[:2])


def _ici_copies_start(sets, sources, landings, exchanges, name, deps=()):
    n = len(landings)
    arrays = (list(sources) if sources is not None else []) + list(landings)
    first_land = len(arrays) - n
    n_sets = len(sets)
    n_deps = len(deps)

    def body(*refs):
        refs = refs[n_deps:]
        src, land = refs[:n], refs[first_land:first_land + n]
        sems = refs[len(arrays):len(arrays) + 2 * n_sets]
        token = refs[-1]
        x, y, c = _position()
        for s, (members, exchange) in enumerate(zip(sets, exchanges)):
            for slot, (k, rows) in enumerate(members):
                for j, peer in enumerate(exchange.peers(x, y, c)):
                    at = exchange.n_peers * slot + j
                    pltpu.make_async_remote_copy(
                        src_ref=_shard_rows(src[k], exchange.sent(x, y, c, peer), rows),
                        dst_ref=_shard_rows(land[k], exchange.lands_at(x, y, c), rows),
                        send_sem=sems[2 * s].at[at], recv_sem=sems[2 * s + 1].at[at],
                        device_id=peer, device_id_type=MESH_ID).start()
        token[...] = jnp.zeros_like(token)

    sem_shapes = []
    for members, exchange in zip(sets, exchanges):
        sem_shapes += [pltpu.SemaphoreType.DMA((exchange.n_peers * len(members),))] * 2
    out = pl.pallas_call(
        body, name=name,
        out_shape=sem_shapes + _hbm_like(arrays) + [TOKEN],
        in_specs=[ANY] * n_deps + [HBM] * len(arrays),
        out_specs=[SEM] * (2 * n_sets) + [HBM] * len(arrays) + [pl.BlockSpec(memory_space=pltpu.VMEM)],
        input_output_aliases={n_deps + i: 2 * n_sets + i for i in range(len(arrays))},
        compiler_params=pltpu.CompilerParams(has_side_effects=DATAFLOW),
    )(*deps, *[_in_hbm(a) for a in arrays])
    sems = [(out[2 * s], out[2 * s + 1]) for s in range(n_sets)]
    thru = list(out[2 * n_sets:2 * n_sets + len(arrays)])
    return sems, (thru[:first_land] if sources is not None else None), thru[first_land:], out[-1]


def _ici_copies_wait(sems, members, sources, landings, exchange, after, name):
    n = len(landings)
    arrays = (list(sources) if sources is not None else []) + list(landings)
    first_land = len(arrays) - n

    def body(*refs):
        src, land = refs[:n], refs[first_land:first_land + n]
        send_sems, recv_sems = refs[len(arrays)], refs[len(arrays) + 1]
        x, y, c = _position()
        for slot, rows in enumerate(members):
            for j, peer in enumerate(exchange.peers(x, y, c)):
                at = exchange.n_peers * slot + j
                cp = pltpu.make_async_remote_copy(
                    src_ref=_shard_rows(src[slot], exchange.sent(x, y, c, peer), rows),
                    dst_ref=_shard_rows(land[slot], exchange.arrives_at(peer), rows),
                    send_sem=send_sems.at[at], recv_sem=recv_sems.at[at], device_id=peer, device_id_type=MESH_ID)
                cp.wait_send()
                cp.wait_recv()

    out = pl.pallas_call(
        body, name=name, out_shape=_hbm_like(arrays),
        in_specs=[HBM] * len(arrays) + [SEM, SEM] + [ANY] * len(after), out_specs=[HBM] * len(arrays),
        input_output_aliases={i: i for i in range(len(arrays))},
        compiler_params=pltpu.CompilerParams(has_side_effects=DATAFLOW),
    )(*arrays, sems[0], sems[1], *after)
    return list(out[first_land:])


def _d2d_gather(buffers, rows, name):
    n = len(buffers)

    def body(*refs):
        land = refs[n:2 * n]
        send_sems, recv_sems = refs[2 * n:]
        x, y, c = _position()
        chips = [(x, y)] + _other_chips(x, y)
        sends, recvs = [], []
        for k in range(n):
            for j, chip in enumerate(chips):
                for copies, core in ((sends, c), (recvs, 1 - c)):
                    block = _shard_rows(land[k], _device_index(chip, core), rows[k])
                    copies.append(pltpu.make_async_remote_copy(
                        src_ref=block, dst_ref=block, send_sem=send_sems.at[k, j], recv_sem=recv_sems.at[k, j],
                        device_id=(x, y, 1 - c), device_id_type=MESH_ID))
        for cp in sends:
            cp.start()
        for cp in recvs:
            cp.wait_recv()
        for cp in sends:
            cp.wait_send()

    return pl.pallas_call(
        body, name=name, out_shape=[jax.ShapeDtypeStruct(a.shape, a.dtype) for a in buffers],
        in_specs=[ANY] * n, out_specs=[ANY] * n, input_output_aliases={i: i for i in range(n)},
        scratch_shapes=[pltpu.SemaphoreType.DMA((n, 4)), pltpu.SemaphoreType.DMA((n, 4))],
    )(*buffers)


def _rs_pair(grads, name):
    n = len(grads)
    rows = [g.shape[0] // N_DEV for g in grads]

    def body(*refs):
        ins, outs = refs[:n], refs[n:2 * n]
        send_sems, recv_sems = refs[2 * n:]
        x, y, c = _position()
        copies = []
        for k in range(n):
            for q in range(4):
                copies.append(pltpu.make_async_remote_copy(
                    src_ref=_shard_rows(ins[k], 2 * q + 1 - c, rows[k]), dst_ref=_shard_rows(outs[k], q, rows[k]),
                    send_sem=send_sems.at[k, q], recv_sem=recv_sems.at[k, q], device_id=(x, y, 1 - c),
                    device_id_type=MESH_ID))
        for cp in copies:
            cp.start()
        for cp in copies:
            cp.wait()

    return pl.pallas_call(
        body, out_shape=[jax.ShapeDtypeStruct((4 * r, g.shape[1]), g.dtype) for g, r in zip(grads, rows)],
        in_specs=[ANY] * n, out_specs=[ANY] * n,
        scratch_shapes=[pltpu.SemaphoreType.DMA((n, 4)), pltpu.SemaphoreType.DMA((n, 4))],
        name=name)(*grads)


def _wgrad_pair(lhs, rhs, name, *, lhs_is_transposed, deps=()):
    t = rhs.shape[0]
    n = lhs.shape[0] if lhs_is_transposed else lhs.shape[1]
    r = n // N_DEV
    n_chips = N_DEV // 2
    per = 1 if (2 * r) % BLK == 0 else 2
    steps = n_chips // per

    def body(l_ref, r_ref, kept_ref, recv_ref, res, send_sems, recv_sems):
        q = pl.program_id(0)
        slot = q % 2
        x, y, c = _position()

        def send(step, buf, i):
            return pltpu.make_async_remote_copy(
                src_ref=res.at[buf, pl.ds(pl.multiple_of((2 * i + 1 - c) * r, 16), r), :],
                dst_ref=_shard_rows(recv_ref, step * per + i, r),
                send_sem=send_sems.at[buf, i], recv_sem=recv_sems.at[step * per + i],
                device_id=(x, y, 1 - c), device_id_type=MESH_ID)

        @pl.when(q >= 2)
        def _():
            for i in range(per):
                send(q - 2, slot, i).wait_send()

        if lhs_is_transposed:
            res[slot] = _dot(l_ref[...], r_ref[...]).astype(BF)
        else:
            res[slot] = _dot_tn(l_ref[...], r_ref[...]).astype(BF)
        for i in range(per):
            kept_ref[i * r:(i + 1) * r, :] = res[slot, pl.ds(pl.multiple_of((2 * i + c) * r, 16), r), :]
            send(q, slot, i).start()

        @pl.when(q == steps - 1)
        def _():
            for i in range(per):
                if steps > 1:
                    send(q - 1, 1 - slot, i).wait_send()
                send(q, slot, i).wait_send()
            for chip in range(n_chips):
                send(chip // per, 0, chip % per).wait_recv()

    width = 2 * r * per
    lhs_spec = pl.BlockSpec((width, t), lambda q: (q, 0)) if lhs_is_transposed else pl.BlockSpec((t, width), lambda q: (0, q))
    return _call(
        body, deps, (lhs, rhs), grid=(steps,),
        in_specs=[lhs_spec, _resident((t, D))],
        out_specs=[pl.BlockSpec((per * r, D), lambda q: (q, 0)), ANY],
        out_shape=[jax.ShapeDtypeStruct((n // 2, D), BF)] * 2,
        scratch_shapes=[pltpu.VMEM((2, width, D), BF), pltpu.SemaphoreType.DMA((2, per)),
                        pltpu.SemaphoreType.DMA((n_chips,))],
        compiler_params=_params(1), name=name)


def _wgrad_pair_sum(lhs, rhs, place, name, *, lhs_is_transposed, deps=()):
    t = rhs.shape[0]
    n = lhs.shape[0] if lhs_is_transposed else lhs.shape[1]
    r = n // N_DEV
    n_chips = N_DEV // 2
    per = 1 if (2 * r) % BLK == 0 else 2
    steps = n_chips // per
    n_deps = len(deps)

    def body(place_ref, *refs):
        l_ref, r_ref, part_ref, land_ref, res, inbox, send_sems, recv_sems = refs[n_deps:]
        q = pl.program_id(0)
        slot = q % 2
        x, y, c = _position()

        def send(step, buf, i):
            return pltpu.make_async_remote_copy(
                src_ref=res.at[buf, pl.ds(pl.multiple_of((2 * i + 1 - c) * r, 16), r), :], dst_ref=inbox.at[step * per + i],
                send_sem=send_sems.at[buf, i], recv_sem=recv_sems.at[step * per + i],
                device_id=(x, y, 1 - c), device_id_type=MESH_ID)

        @pl.when(q < steps)
        def _():
            @pl.when(q >= 2)
            def _():
                for i in range(per):
                    send(q - 2, slot, i).wait_send()

            if lhs_is_transposed:
                res[slot] = _dot(l_ref[...], r_ref[...]).astype(BF)
            else:
                res[slot] = _dot_tn(l_ref[...], r_ref[...]).astype(BF)
            for i in range(per):
                send(q, slot, i).start()

        @pl.when(q >= 1)
        def _():
            for i in range(per):
                chip = (q - 1) * per + i
                send(q - 1, 1 - slot, i).wait_recv()
                kept = res[1 - slot, pl.ds(pl.multiple_of((2 * i + c) * r, 16), r), :]
                total = (kept.astype(F32) + inbox[chip].astype(F32)).astype(BF)
                part_ref[i * r:(i + 1) * r, :] = total

                @pl.when(chip == place_ref[1])
                def _():
                    land_ref[...] = total

        @pl.when(q == steps)
        def _():
            for i in range(per):
                if steps > 1:
                    send(q - 2, slot, i).wait_send()
                send(q - 1, 1 - slot, i).wait_send()

    width = 2 * r * per
    last = steps - 1
    if lhs_is_transposed:
        lhs_spec = pl.BlockSpec((width, t), lambda q, p: (jnp.minimum(q, last), 0))
    else:
        lhs_spec = pl.BlockSpec((t, width), lambda q, p: (0, jnp.minimum(q, last)))
    return pl.pallas_call(
        body,
        grid_spec=pltpu.PrefetchScalarGridSpec(
            num_scalar_prefetch=1, grid=(steps + 1,),
            in_specs=[ANY] * n_deps + [lhs_spec, pl.BlockSpec((t, D), lambda q, p: (0, 0), pipeline_mode=pl.Buffered(1))],
            out_specs=[pl.BlockSpec((per * r, D), lambda q, p: (jnp.maximum(q - 1, 0), 0)),
                       pl.BlockSpec((r, D), lambda q, p: (p[1], 0))],
            scratch_shapes=[pltpu.VMEM((2, width, D), BF), pltpu.VMEM((n_chips, r, D), BF),
                            pltpu.SemaphoreType.DMA((2, per)), pltpu.SemaphoreType.DMA((n_chips,))]),
        out_shape=[jax.ShapeDtypeStruct((n // 2, D), BF)] * 2,
        compiler_params=_params(1), name=name)(place, *deps, lhs, rhs)


def _pair_add(grad, received, place, name, kept_only=False):
    r = received.shape[0] // 4
    parity = 0 if kept_only else 1

    def body(place_ref, g_ref, r_ref, o_ref, land_ref):
        total = (g_ref[...].astype(F32) + r_ref[...].astype(F32)).astype(BF)
        o_ref[...] = total

        @pl.when(pl.program_id(0) == place_ref[1])
        def _():
            land_ref[...] = total

    return pl.pallas_call(
        body,
        grid_spec=pltpu.PrefetchScalarGridSpec(
            num_scalar_prefetch=1, grid=(4,),
            in_specs=[pl.BlockSpec((r, D), lambda q, p: ((1 + parity) * q + parity * p[0], 0)),
                      pl.BlockSpec((r, D), lambda q, p: (q, 0))],
            out_specs=[pl.BlockSpec((r, D), lambda q, p: (q, 0)), pl.BlockSpec((r, D), lambda q, p: (p[1], 0))]),
        out_shape=[jax.ShapeDtypeStruct(received.shape, BF)] * 2,
        compiler_params=_params(1), name=name)(place, grad, received)


def _sum_blocks(gathered, rows):
    def body(b_ref, o_ref):
        acc = b_ref[0:rows, :]
        for d in range(1, N_DEV):
            acc = acc + b_ref[d * rows:(d + 1) * rows, :]
        o_ref[...] = acc

    return pl.pallas_call(body, out_shape=jax.ShapeDtypeStruct((rows, D), F32), name="small_sum")(gathered)


def _adamw_math(w, g, m, v):
    m = ADAM_B1 * m + (1.0 - ADAM_B1) * g
    v = ADAM_B2 * v + (1.0 - ADAM_B2) * (g * g)
    m_hat = m / (1.0 - ADAM_B1 ** ADAM_STEP)
    v_hat = v / (1.0 - ADAM_B2 ** ADAM_STEP)
    delta = -ADAM_LR * (m_hat / (jnp.sqrt(v_hat) + ADAM_EPS) + ADAM_WD * w)
    return delta, m, v


def _sum_partials(blocks):
    g = blocks[0].astype(F32)
    for blk in blocks[1:]:
        g = g + blk.astype(F32)
    return g


def _reduce_adamw(landed, w, m, v, name):
    r = w.shape[0]
    tr = 352 if r % 352 == 0 else r
    per = r // tr

    def body(r0, r1, r2, r3, w_ref, m_ref, v_ref, g_ref, d_ref, nm_ref, nv_ref):
        g = _sum_partials([r0[...], r1[...], r2[...], r3[...]])
        g_ref[...] = g
        d_ref[...], nm_ref[...], nv_ref[...] = _adamw_math(w_ref[...], g, m_ref[...], v_ref[...])

    tile = _row_tile(tr, D)
    return pl.pallas_call(
        body, grid=(per,),
        in_specs=[pl.BlockSpec((tr, D), lambda i, q=q: (q * per + i, 0)) for q in range(4)] + [tile] * 3,
        out_specs=[tile] * 4, out_shape=[jax.ShapeDtypeStruct(w.shape, F32)] * 4,
        compiler_params=_params(1), name=name)(landed, landed, landed, landed, w, m, v)


def _adamw_small(w, g, m, v, name):
    def body(w_ref, g_ref, m_ref, v_ref, d_ref, nm_ref, nv_ref):
        d_ref[...], nm_ref[...], nv_ref[...] = _adamw_math(w_ref[...], g_ref[...], m_ref[...], v_ref[...])

    return pl.pallas_call(body, out_shape=[jax.ShapeDtypeStruct(w.shape, F32)] * 3, name=name)(w, g, m, v)


WEIGHTS = ("ffn1_norm", "ffn1_w_in", "ffn1_w_out", "mix_norm", "w_in", "conv_dw_kernel", "conv_dw_bias", "conv_ln_g",
           "conv_ln_b", "conv_w_proj", "q_norm", "k_norm", "attn_sinks", "rel_bias", "attn_w_o", "w_out", "ffn2_norm",
           "ffn2_w_in", "ffn2_w_out")
MATRICES = ("ffn1_w_in", "ffn1_w_out", "w_in", "conv_w_proj", "attn_w_o", "w_out", "ffn2_w_in", "ffn2_w_out")
COLUMN_SHARDED = ("ffn1_w_in", "w_in", "ffn2_w_in")
ROW_VECTORS = ("ffn1_norm", "mix_norm", "conv_dw_bias", "conv_ln_g", "conv_ln_b", "ffn2_norm")
PACKED = (("q_norm", HD), ("k_norm", HD), ("attn_sinks", NQ), ("rel_bias", NBUCKET * NQ))
GATHER = _Exchange(gather=True)
GATHER_ALL = _Exchange(gather=True, all_cores=True)
SCATTER = _Exchange(gather=False)
GATHER_STAGES = ("ffn1_in", "ffn1_out", "mix", "ffn2")
STAGE_GATHER = {"ffn1_in": GATHER, "ffn1_out": GATHER, "mix": GATHER, "ffn2": GATHER_ALL}
STAGE_MEMBERS = {"ffn1_in": ("ffn1_w_in",), "ffn1_out": ("ffn1_w_out",),
                 "mix": ("w_in", "conv_w_proj", "attn_w_o", "w_out", "taps"), "ffn2": ("ffn2_w_in", "ffn2_w_out")}
ROW_PACKED = len(ROW_VECTORS)
ROW_LOSS = ROW_PACKED + 1
ROW_TAPS = 8
PAYLOAD_ROWS = 48


def _pack_small(values, last_row):
    packed = jnp.concatenate([values[k].reshape(-1) for k, _ in PACKED])
    packed = jnp.pad(packed, (0, D - packed.shape[0])).reshape(1, D)
    return jnp.concatenate([values[k].reshape(1, D) for k in ROW_VECTORS] + [packed, last_row], axis=0)


def _unpack_small(rows):
    out = {k: rows[i] for i, k in enumerate(ROW_VECTORS)}
    at = 0
    for k, size in PACKED:
        out[k] = rows[ROW_PACKED, at:at + size]
        at += size
    out["rel_bias"] = out["rel_bias"].reshape(NBUCKET, NQ)
    return out


def kernel(x, ffn1_norm, ffn1_w_in, ffn1_w_out, mix_norm, w_in, conv_dw_kernel, conv_dw_bias, conv_ln_g, conv_ln_b, conv_w_proj, q_norm, k_norm, attn_sinks, rel_bias, attn_w_o, w_out, ffn2_norm, ffn2_w_in, ffn2_w_out, loss_target, m_ffn1_norm, m_ffn1_w_in, m_ffn1_w_out, m_mix_norm, m_w_in, m_conv_dw_kernel, m_conv_dw_bias, m_conv_ln_g, m_conv_ln_b, m_conv_w_proj, m_q_norm, m_k_norm, m_attn_sinks, m_rel_bias, m_attn_w_o, m_w_out, m_ffn2_norm, m_ffn2_w_in, m_ffn2_w_out, v_ffn1_norm, v_ffn1_w_in, v_ffn1_w_out, v_mix_norm, v_w_in, v_conv_dw_kernel, v_conv_dw_bias, v_conv_ln_g, v_conv_ln_b, v_conv_w_proj, v_q_norm, v_k_norm, v_attn_sinks, v_rel_bias, v_attn_w_o, v_w_out, v_ffn2_norm, v_ffn2_w_in, v_ffn2_w_out):
    w = dict(ffn1_norm=ffn1_norm, ffn1_w_in=ffn1_w_in, ffn1_w_out=ffn1_w_out, mix_norm=mix_norm, w_in=w_in,
             conv_dw_kernel=conv_dw_kernel, conv_dw_bias=conv_dw_bias, conv_ln_g=conv_ln_g, conv_ln_b=conv_ln_b,
             conv_w_proj=conv_w_proj, q_norm=q_norm, k_norm=k_norm, attn_sinks=attn_sinks, rel_bias=rel_bias,
             attn_w_o=attn_w_o, w_out=w_out, ffn2_norm=ffn2_norm, ffn2_w_in=ffn2_w_in, ffn2_w_out=ffn2_w_out)
    m = dict(ffn1_norm=m_ffn1_norm, ffn1_w_in=m_ffn1_w_in, ffn1_w_out=m_ffn1_w_out, mix_norm=m_mix_norm, w_in=m_w_in,
             conv_dw_kernel=m_conv_dw_kernel, conv_dw_bias=m_conv_dw_bias, conv_ln_g=m_conv_ln_g, conv_ln_b=m_conv_ln_b,
             conv_w_proj=m_conv_w_proj, q_norm=m_q_norm, k_norm=m_k_norm, attn_sinks=m_attn_sinks, rel_bias=m_rel_bias,
             attn_w_o=m_attn_w_o, w_out=m_w_out, ffn2_norm=m_ffn2_norm, ffn2_w_in=m_ffn2_w_in, ffn2_w_out=m_ffn2_w_out)
    v = dict(ffn1_norm=v_ffn1_norm, ffn1_w_in=v_ffn1_w_in, ffn1_w_out=v_ffn1_w_out, mix_norm=v_mix_norm, w_in=v_w_in,
             conv_dw_kernel=v_conv_dw_kernel, conv_dw_bias=v_conv_dw_bias, conv_ln_g=v_conv_ln_g, conv_ln_b=v_conv_ln_b,
             conv_w_proj=v_conv_w_proj, q_norm=v_q_norm, k_norm=v_k_norm, attn_sinks=v_attn_sinks, rel_bias=v_rel_bias,
             attn_w_o=v_attn_w_o, w_out=v_w_out, ffn2_norm=v_ffn2_norm, ffn2_w_in=v_ffn2_w_in, ffn2_w_out=v_ffn2_w_out)
    px, py, pc = _position()
    me = 4 * px + 2 * py + pc
    place = jnp.stack([pc, 2 * px + py]).astype(jnp.int32)

    rows_of = lambda k, a: a.T if k in COLUMN_SHARDED else a
    buffers = dict(zip(MATRICES + ("taps",), _prep([rows_of(k, w[k]) for k in MATRICES], conv_dw_kernel,
                                                   me.astype(jnp.int32).reshape(1))))
    landings, sets = [], []
    for stage in GATHER_STAGES:
        sets.append([(len(landings) + i, buffers[k].shape[0] // N_DEV) for i, k in enumerate(STAGE_MEMBERS[stage])])
        landings += [buffers[k] for k in STAGE_MEMBERS[stage]]
    sems, _, land_thru, _ = _ici_copies_start(sets, None, landings, [STAGE_GATHER[s] for s in GATHER_STAGES],
                                              "gather_start")

    def weights_of(stage, after):
        s = GATHER_STAGES.index(stage)
        rows = [r for _, r in sets[s]]
        landed = _ici_copies_wait(sems[s], rows, None, [land_thru[k] for k, _ in sets[s]], STAGE_GATHER[stage],
                                  list(after), "gather_wait_" + stage)
        if not STAGE_GATHER[stage].all_cores:
            landed = _d2d_gather(landed, rows, "gather_d2d_" + stage)
        out = dict(zip(STAGE_MEMBERS[stage], landed))
        if "taps" in out:
            taps = out.pop("taps")
            out["conv_dw_kernel"] = jnp.transpose(taps.reshape(N_DEV, CWP, BLK), (1, 0, 2)).reshape(CWP, D)[:CW]
        return out

    in_flight = []

    def wgrad(lhs, rhs, name, lhs_is_transposed, deps=()):
        rows = (lhs.shape[0] if lhs_is_transposed else lhs.shape[1]) // N_DEV
        if rows <= WGRAD_SUM_MAX_ROWS:
            return ("summed",) + tuple(_wgrad_pair_sum(lhs, rhs, place, name, lhs_is_transposed=lhs_is_transposed, deps=deps))
        return ("paired",) + tuple(_wgrad_pair(lhs, rhs, name, lhs_is_transposed=lhs_is_transposed, deps=deps))

    def grads_done(stage, grads):
        names = list(grads)
        added = []
        for k in names:
            if not isinstance(grads[k], tuple):
                received, = _rs_pair([grads[k]], "rs_pair_" + k)
                added.append(_pair_add(grads[k], received, place, "pair_add_" + k))
            elif grads[k][0] == "paired":
                added.append(_pair_add(grads[k][1], grads[k][2], place, "pair_add_" + k, kept_only=True))
            else:
                added.append(grads[k][1:])
        partials = [p for p, _ in added]
        members = [(i, p.shape[0] // 4) for i, p in enumerate(partials)]
        sem, p_thru, l_thru, token = _ici_copies_start([members], partials, [l for _, l in added], [SCATTER],
                                                       "scatter_start_" + stage)
        in_flight.append((stage, names, sem[0], p_thru, l_thru, token))
        return [token]

    small = []

    def small_done(gv, sq):
        payload = jnp.concatenate([_pack_small(gv, sq), jnp.pad(gv["conv_dw_kernel"], ((0, PAYLOAD_ROWS - ROW_TAPS - CW), (0, 0)))],
                                  axis=0)
        mine = lax.dynamic_update_slice_in_dim(lax.empty((N_DEV * PAYLOAD_ROWS, D), F32), payload, me * PAYLOAD_ROWS, axis=0)
        sems, _, thru, token = _ici_copies_start([[(0, PAYLOAD_ROWS)]], None, [mine], [GATHER_ALL], "small_start")
        small.append((sems[0], thru))
        return [token]

    vec = {k: w[k] for k in WEIGHTS if k not in MATRICES and k != "conv_dw_kernel"}
    dx0 = _local_step(x[0], loss_target[0], vec, weights_of, wgrad, grads_done, small_done)
    gathered, = _ici_copies_wait(small[0][0], [PAYLOAD_ROWS], None, small[0][1], GATHER_ALL, [in_flight[-1][-1]], "small_wait")
    total = _sum_blocks(gathered, PAYLOAD_ROWS)
    loss = (0.5 / D) * jnp.sum(total[ROW_LOSS])

    grads, delta, new_m, new_v = {}, {}, {}, {}
    after = [total]
    for stage, names, sem, p_thru, l_thru, _ in in_flight:
        landed = _ici_copies_wait(sem, [p.shape[0] // 4 for p in p_thru], p_thru, l_thru, SCATTER, after,
                                  "scatter_wait_" + stage)
        after = []
        for k, buf in zip(names, landed):
            out = _reduce_adamw(buf, rows_of(k, w[k]), rows_of(k, m[k]), rows_of(k, v[k]), "adamw_" + k)
            grads[k], delta[k], new_m[k], new_v[k] = [rows_of(k, a) for a in out]
            after.append(out[1])
    zero_row = jnp.zeros((1, D), F32)
    d8, m8, v8 = _adamw_small(_pack_small(w, zero_row), total[:ROW_TAPS], _pack_small(m, zero_row),
                              _pack_small(v, zero_row), "adamw_small")
    grads.update(_unpack_small(total[:ROW_TAPS]))
    delta.update(_unpack_small(d8))
    new_m.update(_unpack_small(m8))
    new_v.update(_unpack_small(v8))
    k = "conv_dw_kernel"
    grads[k] = lax.dynamic_slice_in_dim(total[ROW_TAPS:ROW_TAPS + CW], me * BLK, BLK, axis=1)
    delta[k], new_m[k], new_v[k] = _adamw_small(w[k], grads[k], m[k], v[k], "adamw_taps")

    return (loss, dx0[None], *[grads[k] for k in WEIGHTS], *[delta[k] for k in WEIGHTS],
            *[new_m[k] for k in WEIGHTS], *[new_v[k] for k in WEIGHTS])
```

```python
import functools
import math

import numpy as np
import jax
import jax.numpy as jnp
from jax import lax
from jax.experimental import pallas as pl
from jax.experimental.pallas import tpu as pltpu

F32 = jnp.float32
BF = jnp.bfloat16

D = 1024
F = 2816
INW = 5632
CW = 31
CWP = 32
HD = 64
NQ = 16
NKV = 4
GRP = NQ // NKV
BLK = 128
NBUCKET = 32
EPS = 1e-6
NEG = float(jnp.finfo(jnp.float32).min)
QK_SCALE = 1.0 / math.sqrt(HD)
R_CONV = (0, 2048)
R_QKV = (2048, 3584)
R_Q = (2048, 3072)
R_KV = (3072, 3584)
R_GATE = (3584, 5632)

N_DEV = 8
VMEM_LIMIT_V7X = 56 * 1024 * 1024
ROW_TILE = 256
ROW_TILE_WIDE = 512
WGRAD_SUM_MAX_ROWS = 352

ADAM_LR = 0.001
ADAM_B1 = 0.9
ADAM_B2 = 0.999
ADAM_EPS = 1e-08
ADAM_WD = 0.01
ADAM_STEP = 10

NT_DIMS = (((1,), (1,)), ((), ()))
TN_DIMS = (((0,), (0,)), ((), ()))


def _dot(a, b):
    return jnp.dot(a, b, preferred_element_type=F32)


def _dot_nt(a, b):
    return lax.dot_general(a, b, NT_DIMS, preferred_element_type=F32)


def _dot_tn(a, b):
    return lax.dot_general(a, b, TN_DIMS, preferred_element_type=F32)


def _sig(x):
    return 0.5 * jnp.tanh(0.5 * x) + 0.5


ANY = pl.BlockSpec(memory_space=pl.ANY)


def _call(body, deps, args, **kw):
    n = len(deps)
    if n:
        kw["in_specs"] = [ANY] * n + list(kw["in_specs"])
        return pl.pallas_call(lambda *refs: body(*refs[n:]), **kw)(*deps, *args)
    return pl.pallas_call(body, **kw)(*args)


def _params(n_axes):
    return pltpu.CompilerParams(dimension_semantics=("arbitrary",) * n_axes, vmem_limit_bytes=VMEM_LIMIT_V7X)


def _resident(shape):
    zeros = (0,) * len(shape)
    return pl.BlockSpec(shape, lambda *_: zeros, pipeline_mode=pl.Buffered(1))


def _row_tile(rows, cols):
    return pl.BlockSpec((rows, cols), lambda i: (i, 0))


def _rms_stats(x):
    r = lax.rsqrt(jnp.mean(x * x, axis=-1, keepdims=True) + EPS)
    return r, x * r


def _rms_bwd(dn, x, g):
    r, xh = _rms_stats(x)
    dxh = dn * g
    dx = r * (dxh - xh * jnp.mean(dxh * xh, axis=-1, keepdims=True))
    return dx, jnp.sum(dn * xh, axis=0, keepdims=True)


def _ffn_last(x, target, g, w_in_t, w_out, name):
    t = x.shape[0]
    tm = min(ROW_TILE, t)

    def body(x_ref, t_ref, g_ref, w_ref, wo_ref, n_ref, du_ref, h_ref, dy_ref, dx_ref, sq_ref, dg_ref):
        @pl.when(pl.program_id(0) == 0)
        def _():
            sq_ref[...] = jnp.zeros_like(sq_ref)
            dg_ref[...] = jnp.zeros_like(dg_ref)

        x = x_ref[...]
        g = g_ref[...]
        r, xh = _rms_stats(x)
        n = (xh * g).astype(BF)
        n_ref[...] = n
        u = _dot_nt(n, w_ref[...])
        a = u[:, :F]
        b = u[:, F:]
        s = _sig(a)
        sa = a * s
        h = (sa * b).astype(BF)
        h_ref[...] = h
        err = x + 0.5 * _dot(h, wo_ref[...]) - t_ref[...]
        sq_ref[...] += jnp.sum(err * err, axis=0, keepdims=True)
        dxo = err * (1.0 / D)
        dy = (0.5 * dxo).astype(BF)
        dy_ref[...] = dy
        dh = _dot_nt(dy, wo_ref[...])
        du_ref[:, :F] = (dh * b * (s * (1.0 + a * (1.0 - s)))).astype(BF)
        du_ref[:, F:] = (dh * sa).astype(BF)
        dn = _dot(du_ref[...], w_ref[...])
        dxh = dn * g
        dx_ref[...] = dxo + r * (dxh - xh * jnp.mean(dxh * xh, axis=-1, keepdims=True))
        dg_ref[...] += jnp.sum(dn * xh, axis=0, keepdims=True)

    vec = pl.BlockSpec((1, D), lambda i: (0, 0))
    return pl.pallas_call(
        body, grid=(t // tm,),
        in_specs=[_row_tile(tm, D), _row_tile(tm, D), _resident((1, D)), _resident((INW, D)), _resident((F, D))],
        out_specs=[_row_tile(tm, D), _row_tile(tm, INW), _row_tile(tm, F), _row_tile(tm, D), _row_tile(tm, D), vec, vec],
        out_shape=[jax.ShapeDtypeStruct((t, D), BF), jax.ShapeDtypeStruct((t, INW), BF), jax.ShapeDtypeStruct((t, F), BF),
                   jax.ShapeDtypeStruct((t, D), BF), jax.ShapeDtypeStruct((t, D), F32), jax.ShapeDtypeStruct((1, D), F32),
                   jax.ShapeDtypeStruct((1, D), F32)],
        compiler_params=_params(1), name=name)(x, target, g, w_in_t, w_out)


def _ffn_up(x, g, w_in_t, name):
    t = x.shape[0]
    tm = min(ROW_TILE_WIDE, t)

    def body(x_ref, g_ref, w_ref, n_ref, u_ref):
        r, xh = _rms_stats(x_ref[...])
        n = (xh * g_ref[...]).astype(BF)
        n_ref[...] = n
        u_ref[...] = _dot_nt(n, w_ref[...]).astype(BF)

    return pl.pallas_call(
        body, grid=(t // tm,), in_specs=[_row_tile(tm, D), _resident((1, D)), _resident((INW, D))],
        out_specs=[_row_tile(tm, D), _row_tile(tm, INW)],
        out_shape=[jax.ShapeDtypeStruct((t, D), BF), jax.ShapeDtypeStruct((t, INW), BF)],
        compiler_params=_params(1), name=name)(x, g, w_in_t)


def _ffn_down(x, u, w_out, name):
    t = x.shape[0]
    tm = min(ROW_TILE_WIDE, t)

    def body(x_ref, u_ref, wo_ref, xo_ref):
        a = u_ref[:, :F].astype(F32)
        b = u_ref[:, F:].astype(F32)
        h = (a * _sig(a) * b).astype(BF)
        xo_ref[...] = x_ref[...] + 0.5 * _dot(h, wo_ref[...])

    return pl.pallas_call(
        body, grid=(t // tm,), in_specs=[_row_tile(tm, D), _row_tile(tm, INW), _resident((F, D))],
        out_specs=_row_tile(tm, D), out_shape=jax.ShapeDtypeStruct((t, D), F32),
        compiler_params=_params(1), name=name)(x, u, w_out)


def _ffn_bwd(dxo, x, g, u, w_in_t, w_out, name, deps=()):
    t = x.shape[0]
    tm = min(ROW_TILE, t)

    def body(dxo_ref, x_ref, g_ref, u_ref, w_ref, wo_ref, dx_ref, du_ref, h_ref, dy_ref, dg_ref):
        dxo = dxo_ref[...]
        dy = (0.5 * dxo).astype(BF)
        dy_ref[...] = dy
        dh = _dot_nt(dy, wo_ref[...])
        a = u_ref[:, :F].astype(F32)
        b = u_ref[:, F:].astype(F32)
        s = _sig(a)
        sa = a * s
        h_ref[...] = (sa * b).astype(BF)
        du_ref[:, :F] = (dh * b * (s * (1.0 + a * (1.0 - s)))).astype(BF)
        du_ref[:, F:] = (dh * sa).astype(BF)
        dn = _dot(du_ref[...], w_ref[...])
        dx, dg = _rms_bwd(dn, x_ref[...], g_ref[...])
        dx_ref[...] = dxo + dx

        @pl.when(pl.program_id(0) == 0)
        def _():
            dg_ref[...] = jnp.zeros_like(dg_ref)

        dg_ref[...] += dg

    return _call(
        body, deps, (dxo, x, g, u, w_in_t, w_out), grid=(t // tm,),
        in_specs=[_row_tile(tm, D), _row_tile(tm, D), _resident((1, D)), _row_tile(tm, INW), _resident((INW, D)),
                  _resident((F, D))],
        out_specs=[_row_tile(tm, D), _row_tile(tm, INW), _row_tile(tm, F), _row_tile(tm, D),
                   pl.BlockSpec((1, D), lambda i: (0, 0))],
        out_shape=[jax.ShapeDtypeStruct((t, D), F32), jax.ShapeDtypeStruct((t, INW), BF), jax.ShapeDtypeStruct((t, F), BF),
                   jax.ShapeDtypeStruct((t, D), BF), jax.ShapeDtypeStruct((1, D), F32)],
        compiler_params=_params(1), name=name)


def _wgrad(lhs, rhs, name, *, lhs_is_transposed, chunk, deps=()):
    t = rhs.shape[0]
    n = lhs.shape[0] if lhs_is_transposed else lhs.shape[1]
    c = min(chunk, n)

    def body(l_ref, r_ref, o_ref):
        if lhs_is_transposed:
            o_ref[...] = _dot(l_ref[...], r_ref[...]).astype(BF)
        else:
            o_ref[...] = _dot_tn(l_ref[...], r_ref[...]).astype(BF)

    lhs_spec = pl.BlockSpec((c, t), lambda j: (j, 0)) if lhs_is_transposed else pl.BlockSpec((t, c), lambda j: (0, j))
    return _call(
        body, deps, (lhs, rhs), grid=(n // c,),
        in_specs=[lhs_spec, _resident((t, D))],
        out_specs=pl.BlockSpec((c, D), lambda j: (j, 0)),
        out_shape=jax.ShapeDtypeStruct((n, D), BF),
        compiler_params=_params(1), name=name)


def _wgrad_mix(duc, dq_t, dkv_t, dgp, hm):
    t = hm.shape[0]
    c = 512
    first_q, first_kv, first_gate = R_Q[0] // c, R_KV[0] // c, R_GATE[0] // c

    def body(uc_ref, q_ref, kv_ref, gp_ref, h_ref, o_ref):
        j = pl.program_id(0)

        @pl.when(j < first_q)
        def _():
            o_ref[...] = _dot_tn(uc_ref[...], h_ref[...]).astype(BF)

        @pl.when((j >= first_q) & (j < first_kv))
        def _():
            o_ref[...] = _dot(q_ref[...], h_ref[...]).astype(BF)

        @pl.when((j >= first_kv) & (j < first_gate))
        def _():
            o_ref[...] = _dot(kv_ref[...], h_ref[...]).astype(BF)

        @pl.when(j >= first_gate)
        def _():
            o_ref[...] = _dot_tn(gp_ref[...], h_ref[...]).astype(BF)

    return pl.pallas_call(
        body, grid=(INW // c,),
        in_specs=[pl.BlockSpec((t, c), lambda j: (0, jnp.clip(j, 0, first_q - 1))),
                  pl.BlockSpec((c, t), lambda j: (jnp.clip(j - first_q, 0, first_kv - first_q - 1), 0)),
                  pl.BlockSpec((c, t), lambda j: (jnp.clip(j - first_kv, 0, first_gate - first_kv - 1), 0)),
                  pl.BlockSpec((t, c), lambda j: (0, jnp.clip(j - first_gate, 0, INW // c - first_gate - 1))),
                  _resident((t, D))],
        out_specs=pl.BlockSpec((c, D), lambda j: (j, 0)),
        out_shape=jax.ShapeDtypeStruct((INW, D), BF),
        compiler_params=_params(1), name="mix_dw_in")(duc, dq_t, dkv_t, dgp, hm)


def _mix_proj(x, g, w_t):
    t = x.shape[0]
    tm = min(ROW_TILE_WIDE, t)

    def body(x_ref, g_ref, w_ref, hm_ref, uc_ref, gp_ref, qkv_ref):
        r, xh = _rms_stats(x_ref[...])
        hm = (xh * g_ref[...]).astype(BF)
        hm_ref[...] = hm
        uc_ref[...] = _dot_nt(hm, w_ref[R_CONV[0]:R_CONV[1], :]).astype(BF)
        gp_ref[...] = _dot_nt(hm, w_ref[R_GATE[0]:R_GATE[1], :]).astype(BF)
        qkv_ref[...] = _dot_nt(w_ref[R_QKV[0]:R_QKV[1], :], hm).astype(BF)

    return pl.pallas_call(
        body, grid=(t // tm,),
        in_specs=[_row_tile(tm, D), _resident((1, D)), _resident((INW, D))],
        out_specs=[_row_tile(tm, D), _row_tile(tm, 2 * D), _row_tile(tm, 2 * D), pl.BlockSpec((1536, tm), lambda i: (0, i))],
        out_shape=[jax.ShapeDtypeStruct((t, D), BF), jax.ShapeDtypeStruct((t, 2 * D), BF),
                   jax.ShapeDtypeStruct((t, 2 * D), BF), jax.ShapeDtypeStruct((1536, t), BF)],
        compiler_params=_params(1), name="mix_proj")(x, g, w_t)


CONV_HALO = 32
CONV_LEAD = CONV_HALO - (CW - 1)


def _glu(uc):
    uc = uc.astype(F32)
    return uc[:, :D] * _sig(uc[:, D:])


def _ln_stats(zc):
    mu = jnp.mean(zc, axis=-1, keepdims=True)
    zm = zc - mu
    r = lax.rsqrt(jnp.mean(zm * zm, axis=-1, keepdims=True) + EPS)
    return r, zm * r


CONV_SHIFTS = 8
CONV_CHUNK = 32


def _store_shifted(buf, rows):
    for b in range(1, CONV_SHIFTS):
        buf[b, 0:rows - 8, :] = buf[0, pl.ds(b, rows - 8), :]


def _conv_fwd(uc, dwk, dwb, lng, lnb):
    t = uc.shape[0]
    tm = min(512, t)
    per = tm // CONV_HALO
    ext = tm + CONV_HALO

    def body(cur_ref, prev_ref, k_ref, kb_ref, g_ref, b_ref, o_ref, zc_ref, zsh):
        i = pl.program_id(0)
        zsh[0, 0:CONV_HALO, :] = _glu(prev_ref[...]) * (i > 0).astype(F32)
        zsh[0, CONV_HALO:, :] = _glu(cur_ref[...])
        _store_shifted(zsh, ext)

        def chunk(ci, carry):
            r0 = pl.multiple_of(ci * CONV_CHUNK, CONV_CHUNK)
            acc = jnp.zeros((CONV_CHUNK, D), F32) + kb_ref[...]
            for w in range(CW):
                a, b = divmod(CONV_LEAD + w, 8)
                acc = acc + k_ref[w:w + 1, :] * zsh[b, pl.ds(r0 + 8 * a, CONV_CHUNK), :]
            zc_ref[pl.ds(r0, CONV_CHUNK), :] = acc
            return carry

        lax.fori_loop(0, tm // CONV_CHUNK, chunk, 0)
        r, xh = _ln_stats(zc_ref[...])
        y = xh * g_ref[...] + b_ref[...]
        o_ref[...] = (y * _sig(y)).astype(BF)

    return pl.pallas_call(
        body, grid=(t // tm,),
        in_specs=[_row_tile(tm, 2 * D),
                  pl.BlockSpec((CONV_HALO, 2 * D), lambda i: (jnp.maximum(i * per - 1, 0), 0)),
                  _resident((CWP, D)), _resident((1, D)), _resident((1, D)), _resident((1, D))],
        out_specs=[_row_tile(tm, D), _row_tile(tm, D)],
        out_shape=[jax.ShapeDtypeStruct((t, D), BF), jax.ShapeDtypeStruct((t, D), F32)],
        scratch_shapes=[pltpu.VMEM((CONV_SHIFTS, ext, D), F32)],
        compiler_params=_params(1), name="conv_fwd")(uc, uc, dwk, dwb, lng, lnb)


def _conv_bwd(uc, zc, dzs, dwk, lng, lnb):
    t = uc.shape[0]
    tm = min(ROW_TILE_WIDE, t)
    per = tm // CONV_HALO
    n_tiles = t // tm
    ext = tm + CONV_HALO
    last_block = t // CONV_HALO - 1

    def body(cur_ref, zc_ref, zcn_ref, dz_ref, dzn_ref, k_ref, g_ref, b_ref,
             duc_ref, dk_ref, dkb_ref, dg_ref, db_ref, dsh, dk8, z_scr):
        i = pl.program_id(0)

        @pl.when(i == 0)
        def _():
            dk8[...] = jnp.zeros_like(dk8)
            dkb_ref[...] = jnp.zeros_like(dkb_ref)
            dg_ref[...] = jnp.zeros_like(dg_ref)
            db_ref[...] = jnp.zeros_like(db_ref)

        has_next = (i < n_tiles - 1).astype(F32)
        z_scr[...] = _glu(cur_ref[...])
        gain = g_ref[...]

        def ln_silu_bwd(zc, dzs, live):
            r, xh = _ln_stats(zc)
            y = xh * gain + b_ref[...]
            sy = _sig(y)
            dy = dzs * (sy * (1.0 + y * (1.0 - sy))) * live
            dxh = dy * gain
            dzc = r * (dxh - jnp.mean(dxh, axis=-1, keepdims=True) - xh * jnp.mean(dxh * xh, axis=-1, keepdims=True))
            return dzc, dy, xh

        dzc, dy, xh = ln_silu_bwd(zc_ref[...], dz_ref[...], 1.0)
        dsh[0, 0:tm, :] = dzc
        dg_ref[...] += jnp.sum(dy * xh, axis=0, keepdims=True)
        db_ref[...] += jnp.sum(dy, axis=0, keepdims=True)
        dkb_ref[...] += jnp.sum(dzc, axis=0, keepdims=True)
        dsh[0, tm:, :] = ln_silu_bwd(zcn_ref[...], dzn_ref[...], has_next)[0]
        _store_shifted(dsh, ext)

        def chunk(ci, carry):
            r0 = pl.multiple_of(ci * CONV_CHUNK, CONV_CHUNK)
            z_c = z_scr[pl.ds(r0, CONV_CHUNK), :]
            dz = jnp.zeros((CONV_CHUNK, D), F32)
            for w in range(CW):
                a, b = divmod(CW - 1 - w, 8)
                window = dsh[b, pl.ds(r0 + 8 * a, CONV_CHUNK), :]
                dz = dz + k_ref[w:w + 1, :] * window
                prod = z_c * window
                part = prod[0:8, :]
                for j in range(1, CONV_CHUNK // 8):
                    part = part + prod[8 * j:8 * j + 8, :]
                dk8[w] += part
            ucc = cur_ref[pl.ds(r0, CONV_CHUNK), :].astype(F32)
            sg = _sig(ucc[:, D:])
            duc_ref[pl.ds(r0, CONV_CHUNK), 0:D] = (dz * sg).astype(BF)
            duc_ref[pl.ds(r0, CONV_CHUNK), D:2 * D] = (dz * ucc[:, :D] * sg * (1.0 - sg)).astype(BF)
            return carry

        lax.fori_loop(0, tm // CONV_CHUNK, chunk, 0)

        @pl.when(i == n_tiles - 1)
        def _():
            dk_ref[...] = jnp.sum(dk8[...], axis=1)

    vec = pl.BlockSpec((1, D), lambda i: (0, 0))
    next_halo = pl.BlockSpec((CONV_HALO, D), lambda i: (jnp.minimum((i + 1) * per, last_block), 0))
    return pl.pallas_call(
        body, grid=(n_tiles,),
        in_specs=[_row_tile(tm, 2 * D), _row_tile(tm, D), next_halo, _row_tile(tm, D), next_halo,
                  _resident((CWP, D)), _resident((1, D)), _resident((1, D))],
        out_specs=[_row_tile(tm, 2 * D), pl.BlockSpec((CWP, D), lambda i: (0, 0)), vec, vec, vec],
        out_shape=[jax.ShapeDtypeStruct((t, 2 * D), BF), jax.ShapeDtypeStruct((CWP, D), F32),
                   jax.ShapeDtypeStruct((1, D), F32), jax.ShapeDtypeStruct((1, D), F32), jax.ShapeDtypeStruct((1, D), F32)],
        scratch_shapes=[pltpu.VMEM((CONV_SHIFTS, ext, D), F32), pltpu.VMEM((CWP, 8, D), F32), pltpu.VMEM((tm, D), F32)],
        compiler_params=_params(1), name="conv_bwd")(uc, zc, zc, dzs, dzs, dwk, lng, lnb)


def _norm_rows(xt, g):
    r = lax.rsqrt(jnp.mean(xt * xt, axis=0, keepdims=True) + EPS)
    xh = xt * r
    return xh * g, r, xh


ATT_TQ = 1024


def _attn_specs(t, tq):
    per = tq // BLK
    return [pl.BlockSpec((1536, tq), lambda i: (0, i)),
            pl.BlockSpec((512, BLK), lambda i: (2, jnp.maximum(i * per - 1, 0))),
            _resident((HD, 1)), _resident((HD, 1)), _resident((NKV, 1, GRP * BLK)),
            _resident((2, NKV, 2 * BLK, GRP * BLK))]


def _attn_window(hk, sb, qkv_ref, halo_ref, kn_cur, kn_halo):
    v0 = D + NKV * HD + hk * HD
    if sb == 0:
        k_prev = kn_halo[hk]
        v_prev = halo_ref[NKV * HD + hk * HD:NKV * HD + (hk + 1) * HD, :]
    else:
        k_prev = kn_cur[hk][:, (sb - 1) * BLK:sb * BLK]
        v_prev = qkv_ref[v0:v0 + HD, (sb - 1) * BLK:sb * BLK]
    kw = jnp.concatenate([k_prev, kn_cur[hk][:, sb * BLK:(sb + 1) * BLK]], axis=1).astype(BF)
    vw = jnp.concatenate([v_prev, qkv_ref[v0:v0 + HD, sb * BLK:(sb + 1) * BLK]], axis=1)
    return kw, vw


def _attn_probs(kw, qc, bias, sink):
    st = _dot_tn(kw, qc) + bias
    m = jnp.maximum(jnp.max(st, axis=0, keepdims=True), sink)
    p = jnp.exp(st - m)
    e_sink = jnp.exp(sink - m)
    inv = 1.0 / (jnp.sum(p, axis=0, keepdims=True) + e_sink)
    return p * inv, e_sink * inv


def _attn_fwd(qkv_t, qg, kg, sink_rows, bias_t):
    t = qkv_t.shape[1]
    tq = min(ATT_TQ, t)
    n_sub = tq // BLK

    def body(qkv_ref, halo_ref, qg_ref, kg_ref, sink_ref, bias_ref, o_ref, p_ref, ps_ref):
        i = pl.program_id(0)
        first = (i == 0).astype(jnp.int32)
        kgain = kg_ref[...]
        qgain = qg_ref[...]
        kn_cur = [_norm_rows(qkv_ref[D + h * HD:D + (h + 1) * HD, :].astype(F32), kgain)[0] for h in range(NKV)]
        kn_halo = [_norm_rows(halo_ref[h * HD:(h + 1) * HD, :].astype(F32), kgain)[0] for h in range(NKV)]
        for hk in range(NKV):
            for sb in range(n_sub):
                cols = slice(sb * BLK, (sb + 1) * BLK)
                kw, vw = _attn_window(hk, sb, qkv_ref, halo_ref, kn_cur, kn_halo)
                qc = jnp.concatenate(
                    [_norm_rows(qkv_ref[(GRP * hk + g) * HD:(GRP * hk + g + 1) * HD, cols].astype(F32), qgain)[0] * QK_SCALE
                     for g in range(GRP)], axis=1).astype(BF)
                bias = bias_ref[first, hk] if sb == 0 else bias_ref[0, hk]
                p, p_sink = _attn_probs(kw, qc, bias, sink_ref[hk])
                p = p.astype(BF)
                p_ref[sb, hk] = p
                ps_ref[sb, hk] = p_sink
                o = _dot(vw, p)
                for g in range(GRP):
                    head = GRP * hk + g
                    o_ref[head * HD:(head + 1) * HD, cols] = o[:, g * BLK:(g + 1) * BLK].astype(BF)

    return pl.pallas_call(
        body, grid=(t // tq,),
        in_specs=_attn_specs(t, tq),
        out_specs=[pl.BlockSpec((D, tq), lambda i: (0, i)),
                   pl.BlockSpec((n_sub, NKV, 2 * BLK, GRP * BLK), lambda i: (i, 0, 0, 0)),
                   pl.BlockSpec((n_sub, NKV, 1, GRP * BLK), lambda i: (i, 0, 0, 0))],
        out_shape=[jax.ShapeDtypeStruct((D, t), BF), jax.ShapeDtypeStruct((t // BLK, NKV, 2 * BLK, GRP * BLK), BF),
                   jax.ShapeDtypeStruct((t // BLK, NKV, 1, GRP * BLK), F32)],
        compiler_params=_params(1), name="attn_fwd")(qkv_t, qkv_t, qg, kg, sink_rows, bias_t)


def _attn_bwd(qkv_t, do_t, probs, sink_probs, qg, kg, onehot_t, deps=()):
    t = qkv_t.shape[1]
    tq = min(ATT_TQ, t)
    n_sub = tq // BLK
    n_tiles = t // tq

    def body(qkv_ref, halo_ref, do_ref, p_ref, ps_ref, qg_ref, kg_ref, oh_ref,
             dq_ref, ckv_ref, dqg_ref, dsink_ref, dbias_ref, qg_scr, sink_scr, ds_scr):
        i = pl.program_id(0)

        @pl.when(i == 0)
        def _():
            qg_scr[...] = jnp.zeros_like(qg_scr)
            sink_scr[...] = jnp.zeros_like(sink_scr)
            ds_scr[...] = jnp.zeros_like(ds_scr)

        kgain = kg_ref[...]
        qgain = qg_ref[...]
        kn_cur = [_norm_rows(qkv_ref[D + h * HD:D + (h + 1) * HD, :].astype(F32), kgain)[0] for h in range(NKV)]
        kn_halo = [_norm_rows(halo_ref[h * HD:(h + 1) * HD, :].astype(F32), kgain)[0] for h in range(NKV)]
        dqg = jnp.zeros((HD, BLK), F32)
        for hk in range(NKV):
            for sb in range(n_sub):
                cols = slice(sb * BLK, (sb + 1) * BLK)
                kw, vw = _attn_window(hk, sb, qkv_ref, halo_ref, kn_cur, kn_halo)
                qn, qr, qh = [], [], []
                for g in range(GRP):
                    head = GRP * hk + g
                    n_, r_, h_ = _norm_rows(qkv_ref[head * HD:(head + 1) * HD, cols].astype(F32), qgain)
                    qn.append(n_)
                    qr.append(r_)
                    qh.append(h_)
                qc = (jnp.concatenate(qn, axis=1) * QK_SCALE).astype(BF)
                p_bf = p_ref[sb, hk]
                p = p_bf.astype(F32)
                doc = jnp.concatenate([do_ref[(GRP * hk + g) * HD:(GRP * hk + g + 1) * HD, cols] for g in range(GRP)], axis=1)
                dp = _dot_tn(vw, doc)
                delta = jnp.sum(p * dp, axis=0, keepdims=True)
                ds = p * (dp - delta)
                sink_scr[hk] += -(ps_ref[sb, hk] * delta)
                ds_scr[hk] += ds
                dsb = ds.astype(BF)
                dqc = _dot(kw, dsb) * QK_SCALE
                ckv_ref[sb, hk * HD:(hk + 1) * HD, :] = _dot_nt(qc, dsb)
                ckv_ref[sb, NKV * HD + hk * HD:NKV * HD + (hk + 1) * HD, :] = _dot_nt(doc, p_bf)
                for g in range(GRP):
                    head = GRP * hk + g
                    dqn = dqc[:, g * BLK:(g + 1) * BLK]
                    dqh = dqn * qgain
                    dq = qr[g] * (dqh - qh[g] * jnp.mean(dqh * qh[g], axis=0, keepdims=True))
                    dq_ref[head * HD:(head + 1) * HD, cols] = dq.astype(BF)
                    dqg = dqg + dqn * qh[g]
        qg_scr[...] += dqg

        @pl.when(i == n_tiles - 1)
        def _():
            dqg_ref[...] = jnp.sum(qg_scr[...], axis=1, keepdims=True)
            dsink_ref[...] = _group_lane_sums(sink_scr[:, 0, :])

            def bucket(b, carry):
                oh = jnp.concatenate([oh_ref[b]] * GRP, axis=1)
                dbias_ref[b] = _group_lane_sums(jnp.sum(ds_scr[...] * oh[None], axis=1))
                return carry

            lax.fori_loop(0, NBUCKET, bucket, 0)

    return _call(
        body, deps, (qkv_t, qkv_t, do_t, probs, sink_probs, qg, kg, onehot_t), grid=(n_tiles,),
        in_specs=_attn_specs(t, tq)[:2] + [pl.BlockSpec((D, tq), lambda i: (0, i)),
                                           pl.BlockSpec((n_sub, NKV, 2 * BLK, GRP * BLK), lambda i: (i, 0, 0, 0)),
                                           pl.BlockSpec((n_sub, NKV, 1, GRP * BLK), lambda i: (i, 0, 0, 0))]
        + _attn_specs(t, tq)[2:4] + [_resident((NBUCKET, 2 * BLK, BLK))],
        out_specs=[pl.BlockSpec((D, tq), lambda i: (0, i)),
                   pl.BlockSpec((n_sub, 2 * NKV * HD, 2 * BLK), lambda i: (i, 0, 0)),
                   pl.BlockSpec((HD, 1), lambda i: (0, 0)),
                   pl.BlockSpec((NKV, BLK), lambda i: (0, 0)),
                   pl.BlockSpec((NBUCKET, NKV, BLK), lambda i: (0, 0, 0))],
        out_shape=[jax.ShapeDtypeStruct((D, t), BF),
                   jax.ShapeDtypeStruct((t // BLK, 2 * NKV * HD, 2 * BLK), F32),
                   jax.ShapeDtypeStruct((HD, 1), F32),
                   jax.ShapeDtypeStruct((NKV, BLK), F32),
                   jax.ShapeDtypeStruct((NBUCKET, NKV, BLK), F32)],
        scratch_shapes=[pltpu.VMEM((HD, BLK), F32), pltpu.VMEM((NKV, 1, GRP * BLK), F32),
                        pltpu.VMEM((NKV, 2 * BLK, GRP * BLK), F32)],
        compiler_params=_params(1), name="attn_bwd")


def _kv_combine_tile(c_ref, cn_ref, has_next, k_ref, kgain, o_ref):
    rows = NKV * HD
    per = c_ref.shape[0]
    dkg = jnp.zeros((HD, BLK), F32)
    for s in range(per):
        cols = slice(s * BLK, (s + 1) * BLK)
        after = c_ref[s + 1, :, :BLK] if s + 1 < per else cn_ref[0, :, :BLK] * has_next
        d = c_ref[s, :, BLK:] + after
        o_ref[rows:, cols] = d[rows:, :].astype(BF)
        for h in range(NKV):
            _, r, kh = _norm_rows(k_ref[h * HD:(h + 1) * HD, cols].astype(F32), kgain)
            dkn = d[h * HD:(h + 1) * HD, :]
            dkh = dkn * kgain
            o_ref[h * HD:(h + 1) * HD, cols] = (r * (dkh - kh * jnp.mean(dkh * kh, axis=0, keepdims=True))).astype(BF)
            dkg = dkg + dkn * kh
    return dkg


def _group_lane_sums(v):
    lane_group = lax.broadcasted_iota(jnp.int32, (1, GRP * BLK), 1) // BLK
    col = lax.broadcasted_iota(jnp.int32, (1, BLK), 1)
    out = jnp.zeros((NKV, BLK), F32)
    for g in range(GRP):
        s = jnp.sum(jnp.where(lane_group == g, v, 0.0), axis=1, keepdims=True)
        out = jnp.where(col == g, s, out)
    return out


def _mix_out(zs, o_t, gp, x, w_cp, w_o, w_out):
    t = x.shape[0]
    tm = min(ROW_TILE_WIDE, t)

    def body(zs_ref, ot_ref, gp_ref, x_ref, wcp_ref, wo_ref, wout_ref, xo_ref, a_ref, b_ref, m_ref):
        a = _dot(zs_ref[...], wcp_ref[...])
        b = _dot_tn(ot_ref[...], wo_ref[...])
        a_ref[...] = a.astype(BF)
        b_ref[...] = b.astype(BF)
        merged = (_sig(gp_ref[:, :D].astype(F32)) * a + _sig(gp_ref[:, D:].astype(F32)) * b).astype(BF)
        m_ref[...] = merged
        xo_ref[...] = x_ref[...] + _dot(merged, wout_ref[...])

    return pl.pallas_call(
        body, grid=(t // tm,),
        in_specs=[_row_tile(tm, D), pl.BlockSpec((D, tm), lambda i: (0, i)), _row_tile(tm, 2 * D), _row_tile(tm, D),
                  _resident((D, D)), _resident((D, D)), _resident((D, D))],
        out_specs=[_row_tile(tm, D)] * 4,
        out_shape=[jax.ShapeDtypeStruct((t, D), F32)] + [jax.ShapeDtypeStruct((t, D), BF)] * 3,
        compiler_params=_params(1), name="mix_out")(zs, o_t, gp, x, w_cp, w_o, w_out)


def _mix_out_bwd(dx, a, b, gp, w_cp, w_o, w_out, deps=()):
    t = dx.shape[0]
    tm = min(ROW_TILE_WIDE, t)

    def body(dx_ref, a_ref, b_ref, gp_ref, wcp_ref, wo_ref, wout_ref, dzs_ref, dot_ref, dgp_ref, da_ref, db_ref, dxb_ref):
        dxb = dx_ref[...].astype(BF)
        dxb_ref[...] = dxb
        dm = _dot_nt(dxb, wout_ref[...])
        gc = _sig(gp_ref[:, :D].astype(F32))
        ga = _sig(gp_ref[:, D:].astype(F32))
        da = (dm * gc).astype(BF)
        db = (dm * ga).astype(BF)
        da_ref[...] = da
        db_ref[...] = db
        dgp_ref[:, :D] = (dm * a_ref[...].astype(F32) * gc * (1.0 - gc)).astype(BF)
        dgp_ref[:, D:] = (dm * b_ref[...].astype(F32) * ga * (1.0 - ga)).astype(BF)
        dzs_ref[...] = _dot_nt(da, wcp_ref[...])
        dot_ref[...] = _dot_nt(wo_ref[...], db).astype(BF)

    return _call(
        body, deps, (dx, a, b, gp, w_cp, w_o, w_out), grid=(t // tm,),
        in_specs=[_row_tile(tm, D), _row_tile(tm, D), _row_tile(tm, D), _row_tile(tm, 2 * D),
                  _resident((D, D)), _resident((D, D)), _resident((D, D))],
        out_specs=[_row_tile(tm, D), pl.BlockSpec((D, tm), lambda i: (0, i)), _row_tile(tm, 2 * D),
                   _row_tile(tm, D), _row_tile(tm, D), _row_tile(tm, D)],
        out_shape=[jax.ShapeDtypeStruct((t, D), F32), jax.ShapeDtypeStruct((D, t), BF), jax.ShapeDtypeStruct((t, 2 * D), BF),
                   jax.ShapeDtypeStruct((t, D), BF), jax.ShapeDtypeStruct((t, D), BF), jax.ShapeDtypeStruct((t, D), BF)],
        compiler_params=_params(1), name="mix_out_bwd")


def _mix_proj_bwd(dxo, duc, dq_t, ckv, qkv_t, kg, dgp, x, g, w_t):
    t = x.shape[0]
    tm = min(ROW_TILE_WIDE, t)
    per = tm // BLK
    steps = t // tm
    kv_rows = 2 * NKV * HD

    def body(dxo_ref, duc_ref, dq_ref, c_ref, cn_ref, k_ref, kg_ref, dgp_ref, x_ref, g_ref, w_ref,
             dx_ref, dg_ref, dkv_ref, dkg_ref, kg_scr):
        i = pl.program_id(0)

        @pl.when(i == 0)
        def _():
            dg_ref[...] = jnp.zeros_like(dg_ref)
            kg_scr[...] = jnp.zeros_like(kg_scr)

        kg_scr[...] += _kv_combine_tile(c_ref, cn_ref, (i < steps - 1).astype(F32), k_ref, kg_ref[...], dkv_ref)
        dn = _dot(duc_ref[...], w_ref[R_CONV[0]:R_CONV[1], :])
        dn = dn + _dot(dgp_ref[...], w_ref[R_GATE[0]:R_GATE[1], :])
        dn = dn + _dot_tn(dq_ref[...], w_ref[R_Q[0]:R_Q[1], :])
        dn = dn + _dot_tn(dkv_ref[...], w_ref[R_KV[0]:R_KV[1], :])
        dx, dg = _rms_bwd(dn, x_ref[...], g_ref[...])
        dx_ref[...] = dxo_ref[...] + dx
        dg_ref[...] += dg

        @pl.when(i == steps - 1)
        def _():
            dkg_ref[...] = jnp.sum(kg_scr[...], axis=1, keepdims=True)

    return pl.pallas_call(
        body, grid=(steps,),
        in_specs=[_row_tile(tm, D), _row_tile(tm, 2 * D), pl.BlockSpec((D, tm), lambda i: (0, i)),
                  pl.BlockSpec((per, kv_rows, 2 * BLK), lambda i: (i, 0, 0)),
                  pl.BlockSpec((1, kv_rows, 2 * BLK), lambda i: (jnp.minimum((i + 1) * per, t // BLK - 1), 0, 0)),
                  pl.BlockSpec((NKV * HD, tm), lambda i: (D // (NKV * HD), i)), _resident((HD, 1)),
                  _row_tile(tm, 2 * D), _row_tile(tm, D), _resident((1, D)), _resident((INW, D))],
        out_specs=[_row_tile(tm, D), pl.BlockSpec((1, D), lambda i: (0, 0)), pl.BlockSpec((kv_rows, tm), lambda i: (0, i)),
                   pl.BlockSpec((HD, 1), lambda i: (0, 0))],
        out_shape=[jax.ShapeDtypeStruct((t, D), F32), jax.ShapeDtypeStruct((1, D), F32),
                   jax.ShapeDtypeStruct((kv_rows, t), BF), jax.ShapeDtypeStruct((HD, 1), F32)],
        scratch_shapes=[pltpu.VMEM((HD, BLK), F32)],
        compiler_params=_params(1), name="mix_proj_bwd")(dxo, duc, dq_t, ckv, ckv, qkv_t, kg, dgp, x, g, w_t)


def _attention_tables():
    kj = np.arange(2 * BLK)[:, None]
    qi = np.arange(BLK)[None, :]
    dist = qi + BLK - kj
    in_win = (dist >= 0) & (dist < BLK)
    dpos = np.maximum(dist, 0)
    max_exact = NBUCKET // 2
    dfl = np.maximum(dpos, 1).astype(np.float32)
    large = max_exact + (np.log(dfl / np.float32(max_exact)) / np.float32(math.log(BLK / max_exact))
                         * np.float32(NBUCKET - max_exact)).astype(np.int32)
    large = np.minimum(large, NBUCKET - 1)
    bucket = np.where(dpos < max_exact, dpos, large)
    onehot = (bucket[None] == np.arange(NBUCKET)[:, None, None]).astype(np.float32)
    mask = in_win.astype(np.float32)
    mask_first = mask * (kj >= BLK)
    masks = np.stack([np.tile(mask, (1, GRP)), np.tile(mask_first, (1, GRP))])
    return onehot, masks


def _bias_table(rel_bias, onehot):
    tab = jnp.einsum("bkq,bh->hkq", onehot, rel_bias, precision=lax.Precision.HIGHEST)
    tab = tab.reshape(NKV, GRP, 2 * BLK, BLK)
    return jnp.transpose(tab, (0, 2, 1, 3)).reshape(NKV, 2 * BLK, GRP * BLK)


def _local_step(x, target, vec, weights_of, wgrad, grads_done, small_done):
    onehot_np, masks_np = _attention_tables()
    onehot = jnp.asarray(onehot_np)
    masks = jnp.asarray(masks_np)
    bias_t = jnp.where(masks[:, None] > 0.5, _bias_table(vec["rel_bias"], onehot)[None], NEG)
    sink_rows = jnp.repeat(vec["attn_sinks"].reshape(NKV, 1, GRP), BLK, axis=2)
    qg = vec["q_norm"].reshape(HD, 1)
    kg = vec["k_norm"].reshape(HD, 1)
    g1 = vec["ffn1_norm"].reshape(1, D)
    gm = vec["mix_norm"].reshape(1, D)
    g2 = vec["ffn2_norm"].reshape(1, D)
    dwb = vec["conv_dw_bias"].reshape(1, D)
    lng = vec["conv_ln_g"].reshape(1, D)
    lnb = vec["conv_ln_b"].reshape(1, D)

    w1 = weights_of("ffn1_in", (bias_t, sink_rows))
    n1, u1 = _ffn_up(x, g1, w1["ffn1_w_in"], "ffn1_up")
    w1.update(weights_of("ffn1_out", (u1,)))
    x1 = _ffn_down(x, u1, w1["ffn1_w_out"], "ffn1_down")
    wm = weights_of("mix", (x1,))
    dwk = jnp.pad(wm["conv_dw_kernel"], ((0, CWP - CW), (0, 0)))
    hm, uc, gp, qkv_t = _mix_proj(x1, gm, wm["w_in"])
    zs, zc = _conv_fwd(uc, dwk, dwb, lng, lnb)
    o_t, probs, sink_probs = _attn_fwd(qkv_t, qg, kg, sink_rows, bias_t)
    x2, a, b, merged = _mix_out(zs, o_t, gp, x1, wm["conv_w_proj"], wm["attn_w_o"], wm["w_out"])
    w2 = weights_of("ffn2", (x2,))
    gv = {}
    n2, du2, h2, dy2, dx2, sq, gv["ffn2_norm"] = _ffn_last(x2, target, g2, w2["ffn2_w_in"], w2["ffn2_w_out"], "ffn2")

    deps = grads_done("ffn2", {"ffn2_w_in": wgrad(du2, n2, "ffn2_dw_in", False),
                               "ffn2_w_out": wgrad(h2, dy2, "ffn2_dw_out", False)})

    dzs, do_t, dgp, da, db, dx2b = _mix_out_bwd(dx2, a, b, gp, wm["conv_w_proj"], wm["attn_w_o"], wm["w_out"], deps=deps)
    deps = grads_done("mix_out", {"w_out": wgrad(merged, dx2b, "mix_dw_out", False),
                                  "conv_w_proj": wgrad(zs, da, "mix_dw_cp", False),
                                  "attn_w_o": wgrad(o_t, db, "mix_dw_o", True)})

    dq_t, ckv, dqg, dsink, dbias = _attn_bwd(qkv_t, do_t, probs, sink_probs, qg, kg, onehot, deps=deps)
    gv["q_norm"] = dqg.reshape(HD)
    gv["attn_sinks"] = dsink[:, :GRP].reshape(NQ)
    gv["rel_bias"] = dbias[:, :, :GRP].reshape(NBUCKET, NQ)

    duc, dk_conv, gv["conv_dw_bias"], gv["conv_ln_g"], gv["conv_ln_b"] = _conv_bwd(uc, zc, dzs, dwk, lng, lnb)
    gv["conv_dw_kernel"] = dk_conv[:CW]

    dx1, gv["mix_norm"], dkv_t, dkg = _mix_proj_bwd(dx2, duc, dq_t, ckv, qkv_t, kg, dgp, x1, gm, wm["w_in"])
    gv["k_norm"] = dkg.reshape(HD)
    deps = grads_done("mix_in", {"w_in": _wgrad_mix(duc, dq_t, dkv_t, dgp, hm)})

    dx0, du1, h1, dy1, gv["ffn1_norm"] = _ffn_bwd(dx1, x, g1, u1, w1["ffn1_w_in"], w1["ffn1_w_out"], "ffn1_bwd", deps=deps)
    for k in ("ffn1_norm", "mix_norm", "ffn2_norm", "conv_dw_bias", "conv_ln_g", "conv_ln_b"):
        gv[k] = gv[k].reshape(D)
    deps = small_done(gv, sq)
    deps = grads_done("ffn1_in", {"ffn1_w_in": wgrad(du1, n1, "ffn1_dw_in", False, deps)})
    grads_done("ffn1_out", {"ffn1_w_out": wgrad(h1, dy1, "ffn1_dw_out", False, deps)})
    return dx0


MESH_ID = pl.DeviceIdType.MESH


def _position():
    return lax.axis_index("x"), lax.axis_index("y"), lax.axis_index("c")


def _shard_rows(ref, index, rows):
    return ref.at[pl.ds(pl.multiple_of(index * rows, 16), rows), :]


def _prep(weights, taps, me, name, deps=()):
    n = len(weights)
    n_deps = len(deps)
    with_taps = taps is not None

    def body(me_ref, *refs):
        refs = refs[n_deps:]
        ins, outs = refs[:len(refs) // 2], refs[len(refs) // 2:]
        for k in range(n):
            outs[k][...] = ins[k][...].astype(BF)
        if with_taps:
            outs[n][0:CW, :] = ins[n][...]
            outs[n][CW:, :] = jnp.zeros((CWP - CW, BLK), F32)

    shard_shapes = [w.shape for w in weights] + [(CWP, BLK)] * with_taps
    dtypes = [BF] * n + [F32] * with_taps
    ins = list(weights) + [taps] * with_taps
    return pl.pallas_call(
        body,
        grid_spec=pltpu.PrefetchScalarGridSpec(
            num_scalar_prefetch=1, grid=(1,),
            in_specs=[ANY] * n_deps + [pl.BlockSpec(a.shape, lambda i, m: (0, 0), pipeline_mode=pl.Buffered(1)) for a in ins],
            out_specs=[pl.BlockSpec(s, lambda i, m: (m[0], 0)) for s in shard_shapes]),
        out_shape=[jax.ShapeDtypeStruct((N_DEV * s[0], s[1]), d) for s, d in zip(shard_shapes, dtypes)],
        compiler_params=_params(1), name=name)(me, *deps, *ins)


HBM = pl.BlockSpec(memory_space=pltpu.HBM)
SEM = pl.BlockSpec(memory_space=pltpu.SEMAPHORE)
DATAFLOW = pltpu.SideEffectType.DATAFLOW_SIDE_EFFECTING
TOKEN = jax.ShapeDtypeStruct((8, 128), F32)


def _in_hbm(x):
    return pltpu.with_memory_space_constraint(x, pltpu.HBM)


def _hbm_like(arrays):
    return [pltpu.HBM(a.shape, a.dtype) for a in arrays]


def _other_chips(x, y):
    return [(1 - x, y), (x, 1 - y), (1 - x, 1 - y)]


def _device_index(chip, c):
    return 4 * chip[0] + 2 * chip[1] + c


def _chip_index(chip):
    return 2 * chip[0] + chip[1]


class _Exchange:
    def __init__(self, gather, all_cores=False):
        self.gather = gather
        self.all_cores = all_cores
        self.n_peers = N_DEV - 1 if all_cores else 3

    def peers(self, x, y, c):
        if self.all_cores:
            return [(x ^ (k >> 2), y ^ ((k >> 1) & 1), c ^ (k & 1)) for k in range(1, N_DEV)]
        return [(*chip, c) for chip in _other_chips(x, y)]

    def sent(self, x, y, c, peer):
        return _device_index((x, y), c) if self.gather else _chip_index(peer[:2])

    def lands_at(self, x, y, c):
        return _device_index((x, y), c) if self.gather else _chip_index((x, y))

    def arrives_at(self, peer):
        return _device_index(peer[:2], peer[2]) if self.gather else _chip_index(peer[:2])


def _ici_copies_start(sets, sources, landings, exchanges, name, deps=()):
    n = len(landings)
    arrays = (list(sources) if sources is not None else []) + list(landings)
    first_land = len(arrays) - n
    n_sets = len(sets)
    n_deps = len(deps)

    def body(*refs):
        refs = refs[n_deps:]
        src, land = refs[:n], refs[first_land:first_land + n]
        sems = refs[len(arrays):len(arrays) + 2 * n_sets]
        token = refs[-1]
        x, y, c = _position()
        for s, (members, exchange) in enumerate(zip(sets, exchanges)):
            for slot, (k, rows) in enumerate(members):
                for j, peer in enumerate(exchange.peers(x, y, c)):
                    at = exchange.n_peers * slot + j
                    pltpu.make_async_remote_copy(
                        src_ref=_shard_rows(src[k], exchange.sent(x, y, c, peer), rows),
                        dst_ref=_shard_rows(land[k], exchange.lands_at(x, y, c), rows),
                        send_sem=sems[2 * s].at[at], recv_sem=sems[2 * s + 1].at[at],
                        device_id=peer, device_id_type=MESH_ID).start()
        token[...] = jnp.zeros_like(token)

    sem_shapes = []
    for members, exchange in zip(sets, exchanges):
        sem_shapes += [pltpu.SemaphoreType.DMA((exchange.n_peers * len(members),))] * 2
    out = pl.pallas_call(
        body, name=name,
        out_shape=sem_shapes + _hbm_like(arrays) + [TOKEN],
        in_specs=[ANY] * n_deps + [HBM] * len(arrays),
        out_specs=[SEM] * (2 * n_sets) + [HBM] * len(arrays) + [pl.BlockSpec(memory_space=pltpu.VMEM)],
        input_output_aliases={n_deps + i: 2 * n_sets + i for i in range(len(arrays))},
        compiler_params=pltpu.CompilerParams(has_side_effects=DATAFLOW),
    )(*deps, *[_in_hbm(a) for a in arrays])
    sems = [(out[2 * s], out[2 * s + 1]) for s in range(n_sets)]
    thru = list(out[2 * n_sets:2 * n_sets + len(arrays)])
    return sems, (thru[:first_land] if sources is not None else None), thru[first_land:], out[-1]


def _ici_copies_wait(sems, members, sources, landings, exchange, after, name):
    n = len(landings)
    arrays = (list(sources) if sources is not None else []) + list(landings)
    first_land = len(arrays) - n

    def body(*refs):
        src, land = refs[:n], refs[first_land:first_land + n]
        send_sems, recv_sems = refs[len(arrays)], refs[len(arrays) + 1]
        x, y, c = _position()
        for slot, rows in enumerate(members):
            for j, peer in enumerate(exchange.peers(x, y, c)):
                at = exchange.n_peers * slot + j
                cp = pltpu.make_async_remote_copy(
                    src_ref=_shard_rows(src[slot], exchange.sent(x, y, c, peer), rows),
                    dst_ref=_shard_rows(land[slot], exchange.arrives_at(peer), rows),
                    send_sem=send_sems.at[at], recv_sem=recv_sems.at[at], device_id=peer, device_id_type=MESH_ID)
                cp.wait_send()
                cp.wait_recv()

    out = pl.pallas_call(
        body, name=name, out_shape=_hbm_like(arrays),
        in_specs=[HBM] * len(arrays) + [SEM, SEM] + [ANY] * len(after), out_specs=[HBM] * len(arrays),
        input_output_aliases={i: i for i in range(len(arrays))},
        compiler_params=pltpu.CompilerParams(has_side_effects=DATAFLOW),
    )(*arrays, sems[0], sems[1], *after)
    return list(out[first_land:])


def _d2d_gather(buffers, rows, name):
    n = len(buffers)

    def body(*refs):
        land = refs[n:2 * n]
        send_sems, recv_sems = refs[2 * n:]
        x, y, c = _position()
        chips = [(x, y)] + _other_chips(x, y)
        sends, recvs = [], []
        for k in range(n):
            for j, chip in enumerate(chips):
                for copies, core in ((sends, c), (recvs, 1 - c)):
                    block = _shard_rows(land[k], _device_index(chip, core), rows[k])
                    copies.append(pltpu.make_async_remote_copy(
                        src_ref=block, dst_ref=block, send_sem=send_sems.at[k, j], recv_sem=recv_sems.at[k, j],
                        device_id=(x, y, 1 - c), device_id_type=MESH_ID))
        for cp in sends:
            cp.start()
        for cp in recvs:
            cp.wait_recv()
        for cp in sends:
            cp.wait_send()

    return pl.pallas_call(
        body, name=name, out_shape=[jax.ShapeDtypeStruct(a.shape, a.dtype) for a in buffers],
        in_specs=[ANY] * n, out_specs=[ANY] * n, input_output_aliases={i: i for i in range(n)},
        scratch_shapes=[pltpu.SemaphoreType.DMA((n, 4)), pltpu.SemaphoreType.DMA((n, 4))],
    )(*buffers)


def _rs_pair(grads, name):
    n = len(grads)
    rows = [g.shape[0] // N_DEV for g in grads]

    def body(*refs):
        ins, outs = refs[:n], refs[n:2 * n]
        send_sems, recv_sems = refs[2 * n:]
        x, y, c = _position()
        copies = []
        for k in range(n):
            for q in range(4):
                copies.append(pltpu.make_async_remote_copy(
                    src_ref=_shard_rows(ins[k], 2 * q + 1 - c, rows[k]), dst_ref=_shard_rows(outs[k], q, rows[k]),
                    send_sem=send_sems.at[k, q], recv_sem=recv_sems.at[k, q], device_id=(x, y, 1 - c),
                    device_id_type=MESH_ID))
        for cp in copies:
            cp.start()
        for cp in copies:
            cp.wait()

    return pl.pallas_call(
        body, out_shape=[jax.ShapeDtypeStruct((4 * r, g.shape[1]), g.dtype) for g, r in zip(grads, rows)],
        in_specs=[ANY] * n, out_specs=[ANY] * n,
        scratch_shapes=[pltpu.SemaphoreType.DMA((n, 4)), pltpu.SemaphoreType.DMA((n, 4))],
        name=name)(*grads)


def _wgrad_pair(lhs, rhs, name, *, lhs_is_transposed, deps=()):
    t = rhs.shape[0]
    n = lhs.shape[0] if lhs_is_transposed else lhs.shape[1]
    r = n // N_DEV
    n_chips = N_DEV // 2
    per = 1 if (2 * r) % BLK == 0 else 2
    steps = n_chips // per

    def body(l_ref, r_ref, kept_ref, recv_ref, res, send_sems, recv_sems):
        q = pl.program_id(0)
        slot = q % 2
        x, y, c = _position()

        def send(step, buf, i):
            return pltpu.make_async_remote_copy(
                src_ref=res.at[buf, pl.ds(pl.multiple_of((2 * i + 1 - c) * r, 16), r), :],
                dst_ref=_shard_rows(recv_ref, step * per + i, r),
                send_sem=send_sems.at[buf, i], recv_sem=recv_sems.at[step * per + i],
                device_id=(x, y, 1 - c), device_id_type=MESH_ID)

        @pl.when(q >= 2)
        def _():
            for i in range(per):
                send(q - 2, slot, i).wait_send()

        if lhs_is_transposed:
            res[slot] = _dot(l_ref[...], r_ref[...]).astype(BF)
        else:
            res[slot] = _dot_tn(l_ref[...], r_ref[...]).astype(BF)
        for i in range(per):
            kept_ref[i * r:(i + 1) * r, :] = res[slot, pl.ds(pl.multiple_of((2 * i + c) * r, 16), r), :]
            send(q, slot, i).start()

        @pl.when(q == steps - 1)
        def _():
            for i in range(per):
                if steps > 1:
                    send(q - 1, 1 - slot, i).wait_send()
                send(q, slot, i).wait_send()
            for chip in range(n_chips):
                send(chip // per, 0, chip % per).wait_recv()

    width = 2 * r * per
    lhs_spec = pl.BlockSpec((width, t), lambda q: (q, 0)) if lhs_is_transposed else pl.BlockSpec((t, width), lambda q: (0, q))
    return _call(
        body, deps, (lhs, rhs), grid=(steps,),
        in_specs=[lhs_spec, _resident((t, D))],
        out_specs=[pl.BlockSpec((per * r, D), lambda q: (q, 0)), ANY],
        out_shape=[jax.ShapeDtypeStruct((n // 2, D), BF)] * 2,
        scratch_shapes=[pltpu.VMEM((2, width, D), BF), pltpu.SemaphoreType.DMA((2, per)),
                        pltpu.SemaphoreType.DMA((n_chips,))],
        compiler_params=_params(1), name=name)


def _wgrad_pair_sum(lhs, rhs, place, name, *, lhs_is_transposed, deps=()):
    t = rhs.shape[0]
    n = lhs.shape[0] if lhs_is_transposed else lhs.shape[1]
    r = n // N_DEV
    n_chips = N_DEV // 2
    per = 1 if (2 * r) % BLK == 0 else 2
    steps = n_chips // per
    n_deps = len(deps)

    def body(place_ref, *refs):
        l_ref, r_ref, part_ref, land_ref, res, inbox, send_sems, recv_sems = refs[n_deps:]
        q = pl.program_id(0)
        slot = q % 2
        x, y, c = _position()

        def send(step, buf, i):
            return pltpu.make_async_remote_copy(
                src_ref=res.at[buf, pl.ds(pl.multiple_of((2 * i + 1 - c) * r, 16), r), :], dst_ref=inbox.at[step * per + i],
                send_sem=send_sems.at[buf, i], recv_sem=recv_sems.at[step * per + i],
                device_id=(x, y, 1 - c), device_id_type=MESH_ID)

        @pl.when(q < steps)
        def _():
            @pl.when(q >= 2)
            def _():
                for i in range(per):
                    send(q - 2, slot, i).wait_send()

            if lhs_is_transposed:
                res[slot] = _dot(l_ref[...], r_ref[...]).astype(BF)
            else:
                res[slot] = _dot_tn(l_ref[...], r_ref[...]).astype(BF)
            for i in range(per):
                send(q, slot, i).start()

        @pl.when(q >= 1)
        def _():
            for i in range(per):
                chip = (q - 1) * per + i
                send(q - 1, 1 - slot, i).wait_recv()
                kept = res[1 - slot, pl.ds(pl.multiple_of((2 * i + c) * r, 16), r), :]
                total = (kept.astype(F32) + inbox[chip].astype(F32)).astype(BF)
                part_ref[i * r:(i + 1) * r, :] = total

                @pl.when(chip == place_ref[1])
                def _():
                    land_ref[...] = total

        @pl.when(q == steps)
        def _():
            for i in range(per):
                if steps > 1:
                    send(q - 2, slot, i).wait_send()
                send(q - 1, 1 - slot, i).wait_send()

    width = 2 * r * per
    last = steps - 1
    if lhs_is_transposed:
        lhs_spec = pl.BlockSpec((width, t), lambda q, p: (jnp.minimum(q, last), 0))
    else:
        lhs_spec = pl.BlockSpec((t, width), lambda q, p: (0, jnp.minimum(q, last)))
    return pl.pallas_call(
        body,
        grid_spec=pltpu.PrefetchScalarGridSpec(
            num_scalar_prefetch=1, grid=(steps + 1,),
            in_specs=[ANY] * n_deps + [lhs_spec, pl.BlockSpec((t, D), lambda q, p: (0, 0), pipeline_mode=pl.Buffered(1))],
            out_specs=[pl.BlockSpec((per * r, D), lambda q, p: (jnp.maximum(q - 1, 0), 0)),
                       pl.BlockSpec((r, D), lambda q, p: (p[1], 0))],
            scratch_shapes=[pltpu.VMEM((2, width, D), BF), pltpu.VMEM((n_chips, r, D), BF),
                            pltpu.SemaphoreType.DMA((2, per)), pltpu.SemaphoreType.DMA((n_chips,))]),
        out_shape=[jax.ShapeDtypeStruct((n // 2, D), BF)] * 2,
        compiler_params=_params(1), name=name)(place, *deps, lhs, rhs)


def _pair_add(grad, received, place, name, kept_only=False):
    r = received.shape[0] // 4
    parity = 0 if kept_only else 1

    def body(place_ref, g_ref, r_ref, o_ref, land_ref):
        total = (g_ref[...].astype(F32) + r_ref[...].astype(F32)).astype(BF)
        o_ref[...] = total

        @pl.when(pl.program_id(0) == place_ref[1])
        def _():
            land_ref[...] = total

    return pl.pallas_call(
        body,
        grid_spec=pltpu.PrefetchScalarGridSpec(
            num_scalar_prefetch=1, grid=(4,),
            in_specs=[pl.BlockSpec((r, D), lambda q, p: ((1 + parity) * q + parity * p[0], 0)),
                      pl.BlockSpec((r, D), lambda q, p: (q, 0))],
            out_specs=[pl.BlockSpec((r, D), lambda q, p: (q, 0)), pl.BlockSpec((r, D), lambda q, p: (p[1], 0))]),
        out_shape=[jax.ShapeDtypeStruct(received.shape, BF)] * 2,
        compiler_params=_params(1), name=name)(place, grad, received)


def _sum_blocks(gathered, rows):
    def body(b_ref, o_ref):
        acc = b_ref[0:rows, :]
        for d in range(1, N_DEV):
            acc = acc + b_ref[d * rows:(d + 1) * rows, :]
        o_ref[...] = acc

    return pl.pallas_call(body, out_shape=jax.ShapeDtypeStruct((rows, D), F32), name="small_sum")(gathered)


def _adamw_math(w, g, m, v):
    m = ADAM_B1 * m + (1.0 - ADAM_B1) * g
    v = ADAM_B2 * v + (1.0 - ADAM_B2) * (g * g)
    m_hat = m / (1.0 - ADAM_B1 ** ADAM_STEP)
    v_hat = v / (1.0 - ADAM_B2 ** ADAM_STEP)
    delta = -ADAM_LR * (m_hat / (jnp.sqrt(v_hat) + ADAM_EPS) + ADAM_WD * w)
    return delta, m, v


def _sum_partials(blocks):
    g = blocks[0].astype(F32)
    for blk in blocks[1:]:
        g = g + blk.astype(F32)
    return g


def _reduce_adamw(landed, w, m, v, name):
    r = w.shape[0]
    tr = 352 if r % 352 == 0 else r
    per = r // tr

    def body(r0, r1, r2, r3, w_ref, m_ref, v_ref, g_ref, d_ref, nm_ref, nv_ref):
        g = _sum_partials([r0[...], r1[...], r2[...], r3[...]])
        g_ref[...] = g
        d_ref[...], nm_ref[...], nv_ref[...] = _adamw_math(w_ref[...], g, m_ref[...], v_ref[...])

    tile = _row_tile(tr, D)
    return pl.pallas_call(
        body, grid=(per,),
        in_specs=[pl.BlockSpec((tr, D), lambda i, q=q: (q * per + i, 0)) for q in range(4)] + [tile] * 3,
        out_specs=[tile] * 4, out_shape=[jax.ShapeDtypeStruct(w.shape, F32)] * 4,
        compiler_params=_params(1), name=name)(landed, landed, landed, landed, w, m, v)


def _adamw_small(w, g, m, v, name):
    def body(w_ref, g_ref, m_ref, v_ref, d_ref, nm_ref, nv_ref):
        d_ref[...], nm_ref[...], nv_ref[...] = _adamw_math(w_ref[...], g_ref[...], m_ref[...], v_ref[...])

    return pl.pallas_call(body, out_shape=[jax.ShapeDtypeStruct(w.shape, F32)] * 3, name=name)(w, g, m, v)


WEIGHTS = ("ffn1_norm", "ffn1_w_in", "ffn1_w_out", "mix_norm", "w_in", "conv_dw_kernel", "conv_dw_bias", "conv_ln_g",
           "conv_ln_b", "conv_w_proj", "q_norm", "k_norm", "attn_sinks", "rel_bias", "attn_w_o", "w_out", "ffn2_norm",
           "ffn2_w_in", "ffn2_w_out")
MATRICES = ("ffn1_w_in", "ffn1_w_out", "w_in", "conv_w_proj", "attn_w_o", "w_out", "ffn2_w_in", "ffn2_w_out")
COLUMN_SHARDED = ("ffn1_w_in", "w_in", "ffn2_w_in")
ROW_VECTORS = ("ffn1_norm", "mix_norm", "conv_dw_bias", "conv_ln_g", "conv_ln_b", "ffn2_norm")
PACKED = (("q_norm", HD), ("k_norm", HD), ("attn_sinks", NQ), ("rel_bias", NBUCKET * NQ))
GATHER = _Exchange(gather=True)
GATHER_ALL = _Exchange(gather=True, all_cores=True)
SCATTER = _Exchange(gather=False)
GATHER_STAGES = ("ffn1_in", "ffn1_out", "mix", "ffn2")
STAGE_GATHER = {"ffn1_in": GATHER, "ffn1_out": GATHER, "mix": GATHER, "ffn2": GATHER_ALL}
STAGE_MEMBERS = {"ffn1_in": ("ffn1_w_in",), "ffn1_out": ("ffn1_w_out",),
                 "mix": ("w_in", "conv_w_proj", "attn_w_o", "w_out", "taps"), "ffn2": ("ffn2_w_in", "ffn2_w_out")}
ROW_PACKED = len(ROW_VECTORS)
ROW_LOSS = ROW_PACKED + 1
ROW_TAPS = 8
PAYLOAD_ROWS = 48


def _pack_small(values, last_row):
    packed = jnp.concatenate([values[k].reshape(-1) for k, _ in PACKED])
    packed = jnp.pad(packed, (0, D - packed.shape[0])).reshape(1, D)
    return jnp.concatenate([values[k].reshape(1, D) for k in ROW_VECTORS] + [packed, last_row], axis=0)


def _unpack_small(rows):
    out = {k: rows[i] for i, k in enumerate(ROW_VECTORS)}
    at = 0
    for k, size in PACKED:
        out[k] = rows[ROW_PACKED, at:at + size]
        at += size
    out["rel_bias"] = out["rel_bias"].reshape(NBUCKET, NQ)
    return out


def kernel(x, ffn1_norm, ffn1_w_in, ffn1_w_out, mix_norm, w_in, conv_dw_kernel, conv_dw_bias, conv_ln_g, conv_ln_b, conv_w_proj, q_norm, k_norm, attn_sinks, rel_bias, attn_w_o, w_out, ffn2_norm, ffn2_w_in, ffn2_w_out, loss_target, m_ffn1_norm, m_ffn1_w_in, m_ffn1_w_out, m_mix_norm, m_w_in, m_conv_dw_kernel, m_conv_dw_bias, m_conv_ln_g, m_conv_ln_b, m_conv_w_proj, m_q_norm, m_k_norm, m_attn_sinks, m_rel_bias, m_attn_w_o, m_w_out, m_ffn2_norm, m_ffn2_w_in, m_ffn2_w_out, v_ffn1_norm, v_ffn1_w_in, v_ffn1_w_out, v_mix_norm, v_w_in, v_conv_dw_kernel, v_conv_dw_bias, v_conv_ln_g, v_conv_ln_b, v_conv_w_proj, v_q_norm, v_k_norm, v_attn_sinks, v_rel_bias, v_attn_w_o, v_w_out, v_ffn2_norm, v_ffn2_w_in, v_ffn2_w_out):
    w = dict(ffn1_norm=ffn1_norm, ffn1_w_in=ffn1_w_in, ffn1_w_out=ffn1_w_out, mix_norm=mix_norm, w_in=w_in,
             conv_dw_kernel=conv_dw_kernel, conv_dw_bias=conv_dw_bias, conv_ln_g=conv_ln_g, conv_ln_b=conv_ln_b,
             conv_w_proj=conv_w_proj, q_norm=q_norm, k_norm=k_norm, attn_sinks=attn_sinks, rel_bias=rel_bias,
             attn_w_o=attn_w_o, w_out=w_out, ffn2_norm=ffn2_norm, ffn2_w_in=ffn2_w_in, ffn2_w_out=ffn2_w_out)
    m = dict(ffn1_norm=m_ffn1_norm, ffn1_w_in=m_ffn1_w_in, ffn1_w_out=m_ffn1_w_out, mix_norm=m_mix_norm, w_in=m_w_in,
             conv_dw_kernel=m_conv_dw_kernel, conv_dw_bias=m_conv_dw_bias, conv_ln_g=m_conv_ln_g, conv_ln_b=m_conv_ln_b,
             conv_w_proj=m_conv_w_proj, q_norm=m_q_norm, k_norm=m_k_norm, attn_sinks=m_attn_sinks, rel_bias=m_rel_bias,
             attn_w_o=m_attn_w_o, w_out=m_w_out, ffn2_norm=m_ffn2_norm, ffn2_w_in=m_ffn2_w_in, ffn2_w_out=m_ffn2_w_out)
    v = dict(ffn1_norm=v_ffn1_norm, ffn1_w_in=v_ffn1_w_in, ffn1_w_out=v_ffn1_w_out, mix_norm=v_mix_norm, w_in=v_w_in,
             conv_dw_kernel=v_conv_dw_kernel, conv_dw_bias=v_conv_dw_bias, conv_ln_g=v_conv_ln_g, conv_ln_b=v_conv_ln_b,
             conv_w_proj=v_conv_w_proj, q_norm=v_q_norm, k_norm=v_k_norm, attn_sinks=v_attn_sinks, rel_bias=v_rel_bias,
             attn_w_o=v_attn_w_o, w_out=v_w_out, ffn2_norm=v_ffn2_norm, ffn2_w_in=v_ffn2_w_in, ffn2_w_out=v_ffn2_w_out)
    px, py, pc = _position()
    me = 4 * px + 2 * py + pc
    place = jnp.stack([pc, 2 * px + py]).astype(jnp.int32)

    rows_of = lambda k, a: a.T if k in COLUMN_SHARDED else a
    me1 = me.astype(jnp.int32).reshape(1)
    first = STAGE_MEMBERS[GATHER_STAGES[0]]
    rest = tuple(k for k in MATRICES if k not in first)
    buffers = dict(zip(first, _prep([rows_of(k, w[k]) for k in first], None, me1, "prep_first")))
    shard_rows = dict({k: rows_of(k, w[k]).shape[0] for k in MATRICES}, taps=CWP)
    landings, sets = [], []
    for stage in GATHER_STAGES:
        sets.append([(len(landings) + i, shard_rows[k]) for i, k in enumerate(STAGE_MEMBERS[stage])])
        landings += list(STAGE_MEMBERS[stage])
    sems, _, land_thru, token = _ici_copies_start(sets[:1], None, [buffers[k] for k in first],
                                                  [STAGE_GATHER[GATHER_STAGES[0]]], "gather_start_first")
    buffers.update(zip(rest + ("taps",), _prep([rows_of(k, w[k]) for k in rest], conv_dw_kernel, me1, "prep", deps=[token])))
    later = [[(k - len(first), r) for k, r in members] for members in sets[1:]]
    sems_later, _, thru_later, _ = _ici_copies_start(later, None, [buffers[k] for k in landings[len(first):]],
                                                     [STAGE_GATHER[s] for s in GATHER_STAGES[1:]], "gather_start")
    sems, land_thru = sems + sems_later, land_thru + thru_later

    def weights_of(stage, after):
        s = GATHER_STAGES.index(stage)
        rows = [r for _, r in sets[s]]
        landed = _ici_copies_wait(sems[s], rows, None, [land_thru[k] for k, _ in sets[s]], STAGE_GATHER[stage],
                                  list(after), "gather_wait_" + stage)
        if not STAGE_GATHER[stage].all_cores:
            landed = _d2d_gather(landed, rows, "gather_d2d_" + stage)
        out = dict(zip(STAGE_MEMBERS[stage], landed))
        if "taps" in out:
            taps = out.pop("taps")
            out["conv_dw_kernel"] = jnp.transpose(taps.reshape(N_DEV, CWP, BLK), (1, 0, 2)).reshape(CWP, D)[:CW]
        return out

    in_flight = []

    def wgrad(lhs, rhs, name, lhs_is_transposed, deps=()):
        rows = (lhs.shape[0] if lhs_is_transposed else lhs.shape[1]) // N_DEV
        if rows <= WGRAD_SUM_MAX_ROWS:
            return ("summed",) + tuple(_wgrad_pair_sum(lhs, rhs, place, name, lhs_is_transposed=lhs_is_transposed, deps=deps))
        return ("paired",) + tuple(_wgrad_pair(lhs, rhs, name, lhs_is_transposed=lhs_is_transposed, deps=deps))

    def grads_done(stage, grads):
        names = list(grads)
        added = []
        for k in names:
            if not isinstance(grads[k], tuple):
                received, = _rs_pair([grads[k]], "rs_pair_" + k)
                added.append(_pair_add(grads[k], received, place, "pair_add_" + k))
            elif grads[k][0] == "paired":
                added.append(_pair_add(grads[k][1], grads[k][2], place, "pair_add_" + k, kept_only=True))
            else:
                added.append(grads[k][1:])
        partials = [p for p, _ in added]
        members = [(i, p.shape[0] // 4) for i, p in enumerate(partials)]
        sem, p_thru, l_thru, token = _ici_copies_start([members], partials, [l for _, l in added], [SCATTER],
                                                       "scatter_start_" + stage)
        in_flight.append((stage, names, sem[0], p_thru, l_thru, token))
        return [token]

    small = []

    def small_done(gv, sq):
        payload = jnp.concatenate([_pack_small(gv, sq), jnp.pad(gv["conv_dw_kernel"], ((0, PAYLOAD_ROWS - ROW_TAPS - CW), (0, 0)))],
                                  axis=0)
        mine = lax.dynamic_update_slice_in_dim(lax.empty((N_DEV * PAYLOAD_ROWS, D), F32), payload, me * PAYLOAD_ROWS, axis=0)
        sems, _, thru, token = _ici_copies_start([[(0, PAYLOAD_ROWS)]], None, [mine], [GATHER_ALL], "small_start")
        small.append((sems[0], thru))
        return [token]

    vec = {k: w[k] for k in WEIGHTS if k not in MATRICES and k != "conv_dw_kernel"}
    dx0 = _local_step(x[0], loss_target[0], vec, weights_of, wgrad, grads_done, small_done)
    gathered, = _ici_copies_wait(small[0][0], [PAYLOAD_ROWS], None, small[0][1], GATHER_ALL, [in_flight[-1][-1]], "small_wait")
    total = _sum_blocks(gathered, PAYLOAD_ROWS)
    loss = (0.5 / D) * jnp.sum(total[ROW_LOSS])

    grads, delta, new_m, new_v = {}, {}, {}, {}
    after = [total]
    for stage, names, sem, p_thru, l_thru, _ in in_flight:
        landed = _ici_copies_wait(sem, [p.shape[0] // 4 for p in p_thru], p_thru, l_thru, SCATTER, after,
                                  "scatter_wait_" + stage)
        after = []
        for k, buf in zip(names, landed):
            out = _reduce_adamw(buf, rows_of(k, w[k]), rows_of(k, m[k]), rows_of(k, v[k]), "adamw_" + k)
            grads[k], delta[k], new_m[k], new_v[k] = [rows_of(k, a) for a in out]
            after.append(out[1])
    zero_row = jnp.zeros((1, D), F32)
    d8, m8, v8 = _adamw_small(_pack_small(w, zero_row), total[:ROW_TAPS], _pack_small(m, zero_row),
                              _pack_small(v, zero_row), "adamw_small")
    grads.update(_unpack_small(total[:ROW_TAPS]))
    delta.update(_unpack_small(d8))
    new_m.update(_unpack_small(m8))
    new_v.update(_unpack_small(v8))
    k = "conv_dw_kernel"
    grads[k] = lax.dynamic_slice_in_dim(total[ROW_TAPS:ROW_TAPS + CW], me * BLK, BLK, axis=1)
    delta[k], new_m[k], new_v[k] = _adamw_small(w[k], grads[k], m[k], v[k], "adamw_taps")

    return (loss, dx0[None], *[grads[k] for k in WEIGHTS], *[delta[k] for k in WEIGHTS],
            *[new_m[k] for k in WEIGHTS], *[new_v[k] for k in WEIGHTS])
```

```python
import functools
import math

import numpy as np
import jax
import jax.numpy as jnp
from jax import lax
from jax.experimental import pallas as pl
from jax.experimental.pallas import tpu as pltpu

F32 = jnp.float32
BF = jnp.bfloat16

D = 1024
F = 2816
INW = 5632
CW = 31
CWP = 32
HD = 64
NQ = 16
NKV = 4
GRP = NQ // NKV
BLK = 128
NBUCKET = 32
EPS = 1e-6
NEG = float(jnp.finfo(jnp.float32).min)
QK_SCALE = 1.0 / math.sqrt(HD)
R_CONV = (0, 2048)
R_QKV = (2048, 3584)
R_Q = (2048, 3072)
R_KV = (3072, 3584)
R_GATE = (3584, 5632)

N_DEV = 8
VMEM_LIMIT_V7X = 56 * 1024 * 1024
ROW_TILE = 256
ROW_TILE_WIDE = 512
WGRAD_SUM_MAX_ROWS = 352

ADAM_LR = 0.001
ADAM_B1 = 0.9
ADAM_B2 = 0.999
ADAM_EPS = 1e-08
ADAM_WD = 0.01
ADAM_STEP = 10

NT_DIMS = (((1,), (1,)), ((), ()))
TN_DIMS = (((0,), (0,)), ((), ()))


def _dot(a, b):
    return jnp.dot(a, b, preferred_element_type=F32)


def _dot_nt(a, b):
    return lax.dot_general(a, b, NT_DIMS, preferred_element_type=F32)


def _dot_tn(a, b):
    return lax.dot_general(a, b, TN_DIMS, preferred_element_type=F32)


def _sig(x):
    return 0.5 * jnp.tanh(0.5 * x) + 0.5


ANY = pl.BlockSpec(memory_space=pl.ANY)


def _call(body, deps, args, **kw):
    n = len(deps)
    if n:
        kw["in_specs"] = [ANY] * n + list(kw["in_specs"])
        return pl.pallas_call(lambda *refs: body(*refs[n:]), **kw)(*deps, *args)
    return pl.pallas_call(body, **kw)(*args)


def _params(n_axes):
    return pltpu.CompilerParams(dimension_semantics=("arbitrary",) * n_axes, vmem_limit_bytes=VMEM_LIMIT_V7X)


def _resident(shape):
    zeros = (0,) * len(shape)
    return pl.BlockSpec(shape, lambda *_: zeros, pipeline_mode=pl.Buffered(1))


def _row_tile(rows, cols):
    return pl.BlockSpec((rows, cols), lambda i: (i, 0))


def _rms_stats(x):
    r = lax.rsqrt(jnp.mean(x * x, axis=-1, keepdims=True) + EPS)
    return r, x * r


def _rms_bwd(dn, x, g):
    r, xh = _rms_stats(x)
    dxh = dn * g
    dx = r * (dxh - xh * jnp.mean(dxh * xh, axis=-1, keepdims=True))
    return dx, jnp.sum(dn * xh, axis=0, keepdims=True)


def _ffn_last(x, target, g, w_in_t, w_out, name):
    t = x.shape[0]
    tm = min(ROW_TILE, t)

    def body(x_ref, t_ref, g_ref, w_ref, wo_ref, n_ref, du_ref, h_ref, dy_ref, dx_ref, sq_ref, dg_ref):
        @pl.when(pl.program_id(0) == 0)
        def _():
            sq_ref[...] = jnp.zeros_like(sq_ref)
            dg_ref[...] = jnp.zeros_like(dg_ref)

        x = x_ref[...]
        g = g_ref[...]
        r, xh = _rms_stats(x)
        n = (xh * g).astype(BF)
        n_ref[...] = n
        u = _dot_nt(n, w_ref[...])
        a = u[:, :F]
        b = u[:, F:]
        s = _sig(a)
        sa = a * s
        h = (sa * b).astype(BF)
        h_ref[...] = h
        err = x + 0.5 * _dot(h, wo_ref[...]) - t_ref[...]
        sq_ref[...] += jnp.sum(err * err, axis=0, keepdims=True)
        dxo = err * (1.0 / D)
        dy = (0.5 * dxo).astype(BF)
        dy_ref[...] = dy
        dh = _dot_nt(dy, wo_ref[...])
        du_ref[:, :F] = (dh * b * (s * (1.0 + a * (1.0 - s)))).astype(BF)
        du_ref[:, F:] = (dh * sa).astype(BF)
        dn = _dot(du_ref[...], w_ref[...])
        dxh = dn * g
        dx_ref[...] = dxo + r * (dxh - xh * jnp.mean(dxh * xh, axis=-1, keepdims=True))
        dg_ref[...] += jnp.sum(dn * xh, axis=0, keepdims=True)

    vec = pl.BlockSpec((1, D), lambda i: (0, 0))
    return pl.pallas_call(
        body, grid=(t // tm,),
        in_specs=[_row_tile(tm, D), _row_tile(tm, D), _resident((1, D)), _resident((INW, D)), _resident((F, D))],
        out_specs=[_row_tile(tm, D), _row_tile(tm, INW), _row_tile(tm, F), _row_tile(tm, D), _row_tile(tm, D), vec, vec],
        out_shape=[jax.ShapeDtypeStruct((t, D), BF), jax.ShapeDtypeStruct((t, INW), BF), jax.ShapeDtypeStruct((t, F), BF),
                   jax.ShapeDtypeStruct((t, D), BF), jax.ShapeDtypeStruct((t, D), F32), jax.ShapeDtypeStruct((1, D), F32),
                   jax.ShapeDtypeStruct((1, D), F32)],
        compiler_params=_params(1), name=name)(x, target, g, w_in_t, w_out)


def _ffn_up(x, g, w_in_t, name):
    t = x.shape[0]
    tm = min(ROW_TILE_WIDE, t)

    def body(x_ref, g_ref, w_ref, n_ref, u_ref):
        r, xh = _rms_stats(x_ref[...])
        n = (xh * g_ref[...]).astype(BF)
        n_ref[...] = n
        u_ref[...] = _dot_nt(n, w_ref[...]).astype(BF)

    return pl.pallas_call(
        body, grid=(t // tm,), in_specs=[_row_tile(tm, D), _resident((1, D)), _resident((INW, D))],
        out_specs=[_row_tile(tm, D), _row_tile(tm, INW)],
        out_shape=[jax.ShapeDtypeStruct((t, D), BF), jax.ShapeDtypeStruct((t, INW), BF)],
        compiler_params=_params(1), name=name)(x, g, w_in_t)


def _ffn_down(x, u, w_out, name):
    t = x.shape[0]
    tm = min(ROW_TILE_WIDE, t)

    def body(x_ref, u_ref, wo_ref, xo_ref):
        a = u_ref[:, :F].astype(F32)
        b = u_ref[:, F:].astype(F32)
        h = (a * _sig(a) * b).astype(BF)
        xo_ref[...] = x_ref[...] + 0.5 * _dot(h, wo_ref[...])

    return pl.pallas_call(
        body, grid=(t // tm,), in_specs=[_row_tile(tm, D), _row_tile(tm, INW), _resident((F, D))],
        out_specs=_row_tile(tm, D), out_shape=jax.ShapeDtypeStruct((t, D), F32),
        compiler_params=_params(1), name=name)(x, u, w_out)


def _ffn_bwd(dxo, x, g, u, w_in_t, w_out, name, deps=()):
    t = x.shape[0]
    tm = min(ROW_TILE, t)

    def body(dxo_ref, x_ref, g_ref, u_ref, w_ref, wo_ref, dx_ref, du_ref, h_ref, dy_ref, dg_ref):
        dxo = dxo_ref[...]
        dy = (0.5 * dxo).astype(BF)
        dy_ref[...] = dy
        dh = _dot_nt(dy, wo_ref[...])
        a = u_ref[:, :F].astype(F32)
        b = u_ref[:, F:].astype(F32)
        s = _sig(a)
        sa = a * s
        h_ref[...] = (sa * b).astype(BF)
        du_ref[:, :F] = (dh * b * (s * (1.0 + a * (1.0 - s)))).astype(BF)
        du_ref[:, F:] = (dh * sa).astype(BF)
        dn = _dot(du_ref[...], w_ref[...])
        dx, dg = _rms_bwd(dn, x_ref[...], g_ref[...])
        dx_ref[...] = dxo + dx

        @pl.when(pl.program_id(0) == 0)
        def _():
            dg_ref[...] = jnp.zeros_like(dg_ref)

        dg_ref[...] += dg

    return _call(
        body, deps, (dxo, x, g, u, w_in_t, w_out), grid=(t // tm,),
        in_specs=[_row_tile(tm, D), _row_tile(tm, D), _resident((1, D)), _row_tile(tm, INW), _resident((INW, D)),
                  _resident((F, D))],
        out_specs=[_row_tile(tm, D), _row_tile(tm, INW), _row_tile(tm, F), _row_tile(tm, D),
                   pl.BlockSpec((1, D), lambda i: (0, 0))],
        out_shape=[jax.ShapeDtypeStruct((t, D), F32), jax.ShapeDtypeStruct((t, INW), BF), jax.ShapeDtypeStruct((t, F), BF),
                   jax.ShapeDtypeStruct((t, D), BF), jax.ShapeDtypeStruct((1, D), F32)],
        compiler_params=_params(1), name=name)


def _wgrad(lhs, rhs, name, *, lhs_is_transposed, chunk, deps=()):
    t = rhs.shape[0]
    n = lhs.shape[0] if lhs_is_transposed else lhs.shape[1]
    c = min(chunk, n)

    def body(l_ref, r_ref, o_ref):
        if lhs_is_transposed:
            o_ref[...] = _dot(l_ref[...], r_ref[...]).astype(BF)
        else:
            o_ref[...] = _dot_tn(l_ref[...], r_ref[...]).astype(BF)

    lhs_spec = pl.BlockSpec((c, t), lambda j: (j, 0)) if lhs_is_transposed else pl.BlockSpec((t, c), lambda j: (0, j))
    return _call(
        body, deps, (lhs, rhs), grid=(n // c,),
        in_specs=[lhs_spec, _resident((t, D))],
        out_specs=pl.BlockSpec((c, D), lambda j: (j, 0)),
        out_shape=jax.ShapeDtypeStruct((n, D), BF),
        compiler_params=_params(1), name=name)


def _wgrad_mix(duc, dq_t, dkv_t, dgp, hm):
    t = hm.shape[0]
    c = 512
    first_q, first_kv, first_gate = R_Q[0] // c, R_KV[0] // c, R_GATE[0] // c

    def body(uc_ref, q_ref, kv_ref, gp_ref, h_ref, o_ref):
        j = pl.program_id(0)

        @pl.when(j < first_q)
        def _():
            o_ref[...] = _dot_tn(uc_ref[...], h_ref[...]).astype(BF)

        @pl.when((j >= first_q) & (j < first_kv))
        def _():
            o_ref[...] = _dot(q_ref[...], h_ref[...]).astype(BF)

        @pl.when((j >= first_kv) & (j < first_gate))
        def _():
            o_ref[...] = _dot(kv_ref[...], h_ref[...]).astype(BF)

        @pl.when(j >= first_gate)
        def _():
            o_ref[...] = _dot_tn(gp_ref[...], h_ref[...]).astype(BF)

    return pl.pallas_call(
        body, grid=(INW // c,),
        in_specs=[pl.BlockSpec((t, c), lambda j: (0, jnp.clip(j, 0, first_q - 1))),
                  pl.BlockSpec((c, t), lambda j: (jnp.clip(j - first_q, 0, first_kv - first_q - 1), 0)),
                  pl.BlockSpec((c, t), lambda j: (jnp.clip(j - first_kv, 0, first_gate - first_kv - 1), 0)),
                  pl.BlockSpec((t, c), lambda j: (0, jnp.clip(j - first_gate, 0, INW // c - first_gate - 1))),
                  _resident((t, D))],
        out_specs=pl.BlockSpec((c, D), lambda j: (j, 0)),
        out_shape=jax.ShapeDtypeStruct((INW, D), BF),
        compiler_params=_params(1), name="mix_dw_in")(duc, dq_t, dkv_t, dgp, hm)


def _mix_proj(x, g, w_t):
    t = x.shape[0]
    tm = min(ROW_TILE_WIDE, t)

    def body(x_ref, g_ref, w_ref, hm_ref, uc_ref, gp_ref, qkv_ref):
        r, xh = _rms_stats(x_ref[...])
        hm = (xh * g_ref[...]).astype(BF)
        hm_ref[...] = hm
        uc_ref[...] = _dot_nt(hm, w_ref[R_CONV[0]:R_CONV[1], :]).astype(BF)
        gp_ref[...] = _dot_nt(hm, w_ref[R_GATE[0]:R_GATE[1], :]).astype(BF)
        qkv_ref[...] = _dot_nt(w_ref[R_QKV[0]:R_QKV[1], :], hm).astype(BF)

    return pl.pallas_call(
        body, grid=(t // tm,),
        in_specs=[_row_tile(tm, D), _resident((1, D)), _resident((INW, D))],
        out_specs=[_row_tile(tm, D), _row_tile(tm, 2 * D), _row_tile(tm, 2 * D), pl.BlockSpec((1536, tm), lambda i: (0, i))],
        out_shape=[jax.ShapeDtypeStruct((t, D), BF), jax.ShapeDtypeStruct((t, 2 * D), BF),
                   jax.ShapeDtypeStruct((t, 2 * D), BF), jax.ShapeDtypeStruct((1536, t), BF)],
        compiler_params=_params(1), name="mix_proj")(x, g, w_t)


CONV_HALO = 32
CONV_LEAD = CONV_HALO - (CW - 1)


def _glu(uc):
    uc = uc.astype(F32)
    return uc[:, :D] * _sig(uc[:, D:])


def _ln_stats(zc):
    mu = jnp.mean(zc, axis=-1, keepdims=True)
    zm = zc - mu
    r = lax.rsqrt(jnp.mean(zm * zm, axis=-1, keepdims=True) + EPS)
    return r, zm * r


CONV_SHIFTS = 8
CONV_CHUNK = 32


def _store_shifted(buf, rows):
    for b in range(1, CONV_SHIFTS):
        buf[b, 0:rows - 8, :] = buf[0, pl.ds(b, rows - 8), :]


def _conv_fwd(uc, dwk, dwb, lng, lnb):
    t = uc.shape[0]
    tm = min(512, t)
    per = tm // CONV_HALO
    ext = tm + CONV_HALO

    def body(cur_ref, prev_ref, k_ref, kb_ref, g_ref, b_ref, o_ref, zc_ref, zsh):
        i = pl.program_id(0)
        zsh[0, 0:CONV_HALO, :] = _glu(prev_ref[...]) * (i > 0).astype(F32)
        zsh[0, CONV_HALO:, :] = _glu(cur_ref[...])
        _store_shifted(zsh, ext)

        def chunk(ci, carry):
            r0 = pl.multiple_of(ci * CONV_CHUNK, CONV_CHUNK)
            acc = jnp.zeros((CONV_CHUNK, D), F32) + kb_ref[...]
            for w in range(CW):
                a, b = divmod(CONV_LEAD + w, 8)
                acc = acc + k_ref[w:w + 1, :] * zsh[b, pl.ds(r0 + 8 * a, CONV_CHUNK), :]
            zc_ref[pl.ds(r0, CONV_CHUNK), :] = acc
            return carry

        lax.fori_loop(0, tm // CONV_CHUNK, chunk, 0)
        r, xh = _ln_stats(zc_ref[...])
        y = xh * g_ref[...] + b_ref[...]
        o_ref[...] = (y * _sig(y)).astype(BF)

    return pl.pallas_call(
        body, grid=(t // tm,),
        in_specs=[_row_tile(tm, 2 * D),
                  pl.BlockSpec((CONV_HALO, 2 * D), lambda i: (jnp.maximum(i * per - 1, 0), 0)),
                  _resident((CWP, D)), _resident((1, D)), _resident((1, D)), _resident((1, D))],
        out_specs=[_row_tile(tm, D), _row_tile(tm, D)],
        out_shape=[jax.ShapeDtypeStruct((t, D), BF), jax.ShapeDtypeStruct((t, D), F32)],
        scratch_shapes=[pltpu.VMEM((CONV_SHIFTS, ext, D), F32)],
        compiler_params=_params(1), name="conv_fwd")(uc, uc, dwk, dwb, lng, lnb)


def _conv_bwd(uc, zc, dzs, dwk, lng, lnb):
    t = uc.shape[0]
    tm = min(ROW_TILE_WIDE, t)
    per = tm // CONV_HALO
    n_tiles = t // tm
    ext = tm + CONV_HALO
    last_block = t // CONV_HALO - 1

    def body(cur_ref, zc_ref, zcn_ref, dz_ref, dzn_ref, k_ref, g_ref, b_ref,
             duc_ref, dk_ref, dkb_ref, dg_ref, db_ref, dsh, dk8, z_scr):
        i = pl.program_id(0)

        @pl.when(i == 0)
        def _():
            dk8[...] = jnp.zeros_like(dk8)
            dkb_ref[...] = jnp.zeros_like(dkb_ref)
            dg_ref[...] = jnp.zeros_like(dg_ref)
            db_ref[...] = jnp.zeros_like(db_ref)

        has_next = (i < n_tiles - 1).astype(F32)
        z_scr[...] = _glu(cur_ref[...])
        gain = g_ref[...]

        def ln_silu_bwd(zc, dzs, live):
            r, xh = _ln_stats(zc)
            y = xh * gain + b_ref[...]
            sy = _sig(y)
            dy = dzs * (sy * (1.0 + y * (1.0 - sy))) * live
            dxh = dy * gain
            dzc = r * (dxh - jnp.mean(dxh, axis=-1, keepdims=True) - xh * jnp.mean(dxh * xh, axis=-1, keepdims=True))
            return dzc, dy, xh

        dzc, dy, xh = ln_silu_bwd(zc_ref[...], dz_ref[...], 1.0)
        dsh[0, 0:tm, :] = dzc
        dg_ref[...] += jnp.sum(dy * xh, axis=0, keepdims=True)
        db_ref[...] += jnp.sum(dy, axis=0, keepdims=True)
        dkb_ref[...] += jnp.sum(dzc, axis=0, keepdims=True)
        dsh[0, tm:, :] = ln_silu_bwd(zcn_ref[...], dzn_ref[...], has_next)[0]
        _store_shifted(dsh, ext)

        def chunk(ci, carry):
            r0 = pl.multiple_of(ci * CONV_CHUNK, CONV_CHUNK)
            z_c = z_scr[pl.ds(r0, CONV_CHUNK), :]
            dz = jnp.zeros((CONV_CHUNK, D), F32)
            for w in range(CW):
                a, b = divmod(CW - 1 - w, 8)
                window = dsh[b, pl.ds(r0 + 8 * a, CONV_CHUNK), :]
                dz = dz + k_ref[w:w + 1, :] * window
                prod = z_c * window
                part = prod[0:8, :]
                for j in range(1, CONV_CHUNK // 8):
                    part = part + prod[8 * j:8 * j + 8, :]
                dk8[w] += part
            ucc = cur_ref[pl.ds(r0, CONV_CHUNK), :].astype(F32)
            sg = _sig(ucc[:, D:])
            duc_ref[pl.ds(r0, CONV_CHUNK), 0:D] = (dz * sg).astype(BF)
            duc_ref[pl.ds(r0, CONV_CHUNK), D:2 * D] = (dz * ucc[:, :D] * sg * (1.0 - sg)).astype(BF)
            return carry

        lax.fori_loop(0, tm // CONV_CHUNK, chunk, 0)

        @pl.when(i == n_tiles - 1)
        def _():
            dk_ref[...] = jnp.sum(dk8[...], axis=1)

    vec = pl.BlockSpec((1, D), lambda i: (0, 0))
    next_halo = pl.BlockSpec((CONV_HALO, D), lambda i: (jnp.minimum((i + 1) * per, last_block), 0))
    return pl.pallas_call(
        body, grid=(n_tiles,),
        in_specs=[_row_tile(tm, 2 * D), _row_tile(tm, D), next_halo, _row_tile(tm, D), next_halo,
                  _resident((CWP, D)), _resident((1, D)), _resident((1, D))],
        out_specs=[_row_tile(tm, 2 * D), pl.BlockSpec((CWP, D), lambda i: (0, 0)), vec, vec, vec],
        out_shape=[jax.ShapeDtypeStruct((t, 2 * D), BF), jax.ShapeDtypeStruct((CWP, D), F32),
                   jax.ShapeDtypeStruct((1, D), F32), jax.ShapeDtypeStruct((1, D), F32), jax.ShapeDtypeStruct((1, D), F32)],
        scratch_shapes=[pltpu.VMEM((CONV_SHIFTS, ext, D), F32), pltpu.VMEM((CWP, 8, D), F32), pltpu.VMEM((tm, D), F32)],
        compiler_params=_params(1), name="conv_bwd")(uc, zc, zc, dzs, dzs, dwk, lng, lnb)


def _norm_rows(xt, g):
    r = lax.rsqrt(jnp.mean(xt * xt, axis=0, keepdims=True) + EPS)
    xh = xt * r
    return xh * g, r, xh


ATT_TQ = 1024


def _attn_specs(t, tq):
    per = tq // BLK
    return [pl.BlockSpec((1536, tq), lambda i: (0, i)),
            pl.BlockSpec((512, BLK), lambda i: (2, jnp.maximum(i * per - 1, 0))),
            _resident((HD, 1)), _resident((HD, 1)), _resident((NKV, 1, GRP * BLK)),
            _resident((2, NKV, 2 * BLK, GRP * BLK))]


def _attn_window(hk, sb, qkv_ref, halo_ref, kn_cur, kn_halo):
    v0 = D + NKV * HD + hk * HD
    if sb == 0:
        k_prev = kn_halo[hk]
        v_prev = halo_ref[NKV * HD + hk * HD:NKV * HD + (hk + 1) * HD, :]
    else:
        k_prev = kn_cur[hk][:, (sb - 1) * BLK:sb * BLK]
        v_prev = qkv_ref[v0:v0 + HD, (sb - 1) * BLK:sb * BLK]
    kw = jnp.concatenate([k_prev, kn_cur[hk][:, sb * BLK:(sb + 1) * BLK]], axis=1).astype(BF)
    vw = jnp.concatenate([v_prev, qkv_ref[v0:v0 + HD, sb * BLK:(sb + 1) * BLK]], axis=1)
    return kw, vw


def _attn_probs(kw, qc, bias, sink):
    st = _dot_tn(kw, qc) + bias
    m = jnp.maximum(jnp.max(st, axis=0, keepdims=True), sink)
    p = jnp.exp(st - m)
    e_sink = jnp.exp(sink - m)
    inv = 1.0 / (jnp.sum(p, axis=0, keepdims=True) + e_sink)
    return p * inv, e_sink * inv


def _attn_fwd(qkv_t, qg, kg, sink_rows, bias_t):
    t = qkv_t.shape[1]
    tq = min(ATT_TQ, t)
    n_sub = tq // BLK

    def body(qkv_ref, halo_ref, qg_ref, kg_ref, sink_ref, bias_ref, o_ref, p_ref, ps_ref):
        i = pl.program_id(0)
        first = (i == 0).astype(jnp.int32)
        kgain = kg_ref[...]
        qgain = qg_ref[...]
        kn_cur = [_norm_rows(qkv_ref[D + h * HD:D + (h + 1) * HD, :].astype(F32), kgain)[0] for h in range(NKV)]
        kn_halo = [_norm_rows(halo_ref[h * HD:(h + 1) * HD, :].astype(F32), kgain)[0] for h in range(NKV)]
        for hk in range(NKV):
            for sb in range(n_sub):
                cols = slice(sb * BLK, (sb + 1) * BLK)
                kw, vw = _attn_window(hk, sb, qkv_ref, halo_ref, kn_cur, kn_halo)
                qc = jnp.concatenate(
                    [_norm_rows(qkv_ref[(GRP * hk + g) * HD:(GRP * hk + g + 1) * HD, cols].astype(F32), qgain)[0] * QK_SCALE
                     for g in range(GRP)], axis=1).astype(BF)
                bias = bias_ref[first, hk] if sb == 0 else bias_ref[0, hk]
                p, p_sink = _attn_probs(kw, qc, bias, sink_ref[hk])
                p = p.astype(BF)
                p_ref[sb, hk] = p
                ps_ref[sb, hk] = p_sink
                o = _dot(vw, p)
                for g in range(GRP):
                    head = GRP * hk + g
                    o_ref[head * HD:(head + 1) * HD, cols] = o[:, g * BLK:(g + 1) * BLK].astype(BF)

    return pl.pallas_call(
        body, grid=(t // tq,),
        in_specs=_attn_specs(t, tq),
        out_specs=[pl.BlockSpec((D, tq), lambda i: (0, i)),
                   pl.BlockSpec((n_sub, NKV, 2 * BLK, GRP * BLK), lambda i: (i, 0, 0, 0)),
                   pl.BlockSpec((n_sub, NKV, 1, GRP * BLK), lambda i: (i, 0, 0, 0))],
        out_shape=[jax.ShapeDtypeStruct((D, t), BF), jax.ShapeDtypeStruct((t // BLK, NKV, 2 * BLK, GRP * BLK), BF),
                   jax.ShapeDtypeStruct((t // BLK, NKV, 1, GRP * BLK), F32)],
        compiler_params=_params(1), name="attn_fwd")(qkv_t, qkv_t, qg, kg, sink_rows, bias_t)


def _attn_bwd(qkv_t, do_t, probs, sink_probs, qg, kg, onehot_t, deps=()):
    t = qkv_t.shape[1]
    tq = min(ATT_TQ, t)
    n_sub = tq // BLK
    n_tiles = t // tq

    def body(qkv_ref, halo_ref, do_ref, p_ref, ps_ref, qg_ref, kg_ref, oh_ref,
             dq_ref, ckv_ref, dqg_ref, dsink_ref, dbias_ref, qg_scr, sink_scr, ds_scr):
        i = pl.program_id(0)

        @pl.when(i == 0)
        def _():
            qg_scr[...] = jnp.zeros_like(qg_scr)
            sink_scr[...] = jnp.zeros_like(sink_scr)
            ds_scr[...] = jnp.zeros_like(ds_scr)

        kgain = kg_ref[...]
        qgain = qg_ref[...]
        kn_cur = [_norm_rows(qkv_ref[D + h * HD:D + (h + 1) * HD, :].astype(F32), kgain)[0] for h in range(NKV)]
        kn_halo = [_norm_rows(halo_ref[h * HD:(h + 1) * HD, :].astype(F32), kgain)[0] for h in range(NKV)]
        dqg = jnp.zeros((HD, BLK), F32)
        for hk in range(NKV):
            for sb in range(n_sub):
                cols = slice(sb * BLK, (sb + 1) * BLK)
                kw, vw = _attn_window(hk, sb, qkv_ref, halo_ref, kn_cur, kn_halo)
                qn, qr, qh = [], [], []
                for g in range(GRP):
                    head = GRP * hk + g
                    n_, r_, h_ = _norm_rows(qkv_ref[head * HD:(head + 1) * HD, cols].astype(F32), qgain)
                    qn.append(n_)
                    qr.append(r_)
                    qh.append(h_)
                qc = (jnp.concatenate(qn, axis=1) * QK_SCALE).astype(BF)
                p_bf = p_ref[sb, hk]
                p = p_bf.astype(F32)
                doc = jnp.concatenate([do_ref[(GRP * hk + g) * HD:(GRP * hk + g + 1) * HD, cols] for g in range(GRP)], axis=1)
                dp = _dot_tn(vw, doc)
                delta = jnp.sum(p * dp, axis=0, keepdims=True)
                ds = p * (dp - delta)
                sink_scr[hk] += -(ps_ref[sb, hk] * delta)
                ds_scr[hk] += ds
                dsb = ds.astype(BF)
                dqc = _dot(kw, dsb) * QK_SCALE
                ckv_ref[sb, hk * HD:(hk + 1) * HD, :] = _dot_nt(qc, dsb)
                ckv_ref[sb, NKV * HD + hk * HD:NKV * HD + (hk + 1) * HD, :] = _dot_nt(doc, p_bf)
                for g in range(GRP):
                    head = GRP * hk + g
                    dqn = dqc[:, g * BLK:(g + 1) * BLK]
                    dqh = dqn * qgain
                    dq = qr[g] * (dqh - qh[g] * jnp.mean(dqh * qh[g], axis=0, keepdims=True))
                    dq_ref[head * HD:(head + 1) * HD, cols] = dq.astype(BF)
                    dqg = dqg + dqn * qh[g]
        qg_scr[...] += dqg

        @pl.when(i == n_tiles - 1)
        def _():
            dqg_ref[...] = jnp.sum(qg_scr[...], axis=1, keepdims=True)
            dsink_ref[...] = _group_lane_sums(sink_scr[:, 0, :])

            def bucket(b, carry):
                oh = jnp.concatenate([oh_ref[b]] * GRP, axis=1)
                dbias_ref[b] = _group_lane_sums(jnp.sum(ds_scr[...] * oh[None], axis=1))
                return carry

            lax.fori_loop(0, NBUCKET, bucket, 0)

    return _call(
        body, deps, (qkv_t, qkv_t, do_t, probs, sink_probs, qg, kg, onehot_t), grid=(n_tiles,),
        in_specs=_attn_specs(t, tq)[:2] + [pl.BlockSpec((D, tq), lambda i: (0, i)),
                                           pl.BlockSpec((n_sub, NKV, 2 * BLK, GRP * BLK), lambda i: (i, 0, 0, 0)),
                                           pl.BlockSpec((n_sub, NKV, 1, GRP * BLK), lambda i: (i, 0, 0, 0))]
        + _attn_specs(t, tq)[2:4] + [_resident((NBUCKET, 2 * BLK, BLK))],
        out_specs=[pl.BlockSpec((D, tq), lambda i: (0, i)),
                   pl.BlockSpec((n_sub, 2 * NKV * HD, 2 * BLK), lambda i: (i, 0, 0)),
                   pl.BlockSpec((HD, 1), lambda i: (0, 0)),
                   pl.BlockSpec((NKV, BLK), lambda i: (0, 0)),
                   pl.BlockSpec((NBUCKET, NKV, BLK), lambda i: (0, 0, 0))],
        out_shape=[jax.ShapeDtypeStruct((D, t), BF),
                   jax.ShapeDtypeStruct((t // BLK, 2 * NKV * HD, 2 * BLK), F32),
                   jax.ShapeDtypeStruct((HD, 1), F32),
                   jax.ShapeDtypeStruct((NKV, BLK), F32),
                   jax.ShapeDtypeStruct((NBUCKET, NKV, BLK), F32)],
        scratch_shapes=[pltpu.VMEM((HD, BLK), F32), pltpu.VMEM((NKV, 1, GRP * BLK), F32),
                        pltpu.VMEM((NKV, 2 * BLK, GRP * BLK), F32)],
        compiler_params=_params(1), name="attn_bwd")


def _kv_combine_tile(c_ref, cn_ref, has_next, k_ref, kgain, o_ref):
    rows = NKV * HD
    per = c_ref.shape[0]
    dkg = jnp.zeros((HD, BLK), F32)
    for s in range(per):
        cols = slice(s * BLK, (s + 1) * BLK)
        after = c_ref[s + 1, :, :BLK] if s + 1 < per else cn_ref[0, :, :BLK] * has_next
        d = c_ref[s, :, BLK:] + after
        o_ref[rows:, cols] = d[rows:, :].astype(BF)
        for h in range(NKV):
            _, r, kh = _norm_rows(k_ref[h * HD:(h + 1) * HD, cols].astype(F32), kgain)
            dkn = d[h * HD:(h + 1) * HD, :]
            dkh = dkn * kgain
            o_ref[h * HD:(h + 1) * HD, cols] = (r * (dkh - kh * jnp.mean(dkh * kh, axis=0, keepdims=True))).astype(BF)
            dkg = dkg + dkn * kh
    return dkg


def _group_lane_sums(v):
    lane_group = lax.broadcasted_iota(jnp.int32, (1, GRP * BLK), 1) // BLK
    col = lax.broadcasted_iota(jnp.int32, (1, BLK), 1)
    out = jnp.zeros((NKV, BLK), F32)
    for g in range(GRP):
        s = jnp.sum(jnp.where(lane_group == g, v, 0.0), axis=1, keepdims=True)
        out = jnp.where(col == g, s, out)
    return out


def _mix_out(zs, o_t, gp, x, w_cp, w_o, w_out):
    t = x.shape[0]
    tm = min(ROW_TILE_WIDE, t)

    def body(zs_ref, ot_ref, gp_ref, x_ref, wcp_ref, wo_ref, wout_ref, xo_ref, a_ref, b_ref, m_ref):
        a = _dot(zs_ref[...], wcp_ref[...])
        b = _dot_tn(ot_ref[...], wo_ref[...])
        a_ref[...] = a.astype(BF)
        b_ref[...] = b.astype(BF)
        merged = (_sig(gp_ref[:, :D].astype(F32)) * a + _sig(gp_ref[:, D:].astype(F32)) * b).astype(BF)
        m_ref[...] = merged
        xo_ref[...] = x_ref[...] + _dot(merged, wout_ref[...])

    return pl.pallas_call(
        body, grid=(t // tm,),
        in_specs=[_row_tile(tm, D), pl.BlockSpec((D, tm), lambda i: (0, i)), _row_tile(tm, 2 * D), _row_tile(tm, D),
                  _resident((D, D)), _resident((D, D)), _resident((D, D))],
        out_specs=[_row_tile(tm, D)] * 4,
        out_shape=[jax.ShapeDtypeStruct((t, D), F32)] + [jax.ShapeDtypeStruct((t, D), BF)] * 3,
        compiler_params=_params(1), name="mix_out")(zs, o_t, gp, x, w_cp, w_o, w_out)


def _mix_out_bwd(dx, a, b, gp, w_cp, w_o, w_out, deps=()):
    t = dx.shape[0]
    tm = min(ROW_TILE_WIDE, t)

    def body(dx_ref, a_ref, b_ref, gp_ref, wcp_ref, wo_ref, wout_ref, dzs_ref, dot_ref, dgp_ref, da_ref, db_ref, dxb_ref):
        dxb = dx_ref[...].astype(BF)
        dxb_ref[...] = dxb
        dm = _dot_nt(dxb, wout_ref[...])
        gc = _sig(gp_ref[:, :D].astype(F32))
        ga = _sig(gp_ref[:, D:].astype(F32))
        da = (dm * gc).astype(BF)
        db = (dm * ga).astype(BF)
        da_ref[...] = da
        db_ref[...] = db
        dgp_ref[:, :D] = (dm * a_ref[...].astype(F32) * gc * (1.0 - gc)).astype(BF)
        dgp_ref[:, D:] = (dm * b_ref[...].astype(F32) * ga * (1.0 - ga)).astype(BF)
        dzs_ref[...] = _dot_nt(da, wcp_ref[...])
        dot_ref[...] = _dot_nt(wo_ref[...], db).astype(BF)

    return _call(
        body, deps, (dx, a, b, gp, w_cp, w_o, w_out), grid=(t // tm,),
        in_specs=[_row_tile(tm, D), _row_tile(tm, D), _row_tile(tm, D), _row_tile(tm, 2 * D),
                  _resident((D, D)), _resident((D, D)), _resident((D, D))],
        out_specs=[_row_tile(tm, D), pl.BlockSpec((D, tm), lambda i: (0, i)), _row_tile(tm, 2 * D),
                   _row_tile(tm, D), _row_tile(tm, D), _row_tile(tm, D)],
        out_shape=[jax.ShapeDtypeStruct((t, D), F32), jax.ShapeDtypeStruct((D, t), BF), jax.ShapeDtypeStruct((t, 2 * D), BF),
                   jax.ShapeDtypeStruct((t, D), BF), jax.ShapeDtypeStruct((t, D), BF), jax.ShapeDtypeStruct((t, D), BF)],
        compiler_params=_params(1), name="mix_out_bwd")


def _mix_proj_bwd(dxo, duc, dq_t, ckv, qkv_t, kg, dgp, x, g, w_t):
    t = x.shape[0]
    tm = min(ROW_TILE_WIDE, t)
    per = tm // BLK
    steps = t // tm
    kv_rows = 2 * NKV * HD

    def body(dxo_ref, duc_ref, dq_ref, c_ref, cn_ref, k_ref, kg_ref, dgp_ref, x_ref, g_ref, w_ref,
             dx_ref, dg_ref, dkv_ref, dkg_ref, kg_scr):
        i = pl.program_id(0)

        @pl.when(i == 0)
        def _():
            dg_ref[...] = jnp.zeros_like(dg_ref)
            kg_scr[...] = jnp.zeros_like(kg_scr)

        kg_scr[...] += _kv_combine_tile(c_ref, cn_ref, (i < steps - 1).astype(F32), k_ref, kg_ref[...], dkv_ref)
        dn = _dot(duc_ref[...], w_ref[R_CONV[0]:R_CONV[1], :])
        dn = dn + _dot(dgp_ref[...], w_ref[R_GATE[0]:R_GATE[1], :])
        dn = dn + _dot_tn(dq_ref[...], w_ref[R_Q[0]:R_Q[1], :])
        dn = dn + _dot_tn(dkv_ref[...], w_ref[R_KV[0]:R_KV[1], :])
        dx, dg = _rms_bwd(dn, x_ref[...], g_ref[...])
        dx_ref[...] = dxo_ref[...] + dx
        dg_ref[...] += dg

        @pl.when(i == steps - 1)
        def _():
            dkg_ref[...] = jnp.sum(kg_scr[...], axis=1, keepdims=True)

    return pl.pallas_call(
        body, grid=(steps,),
        in_specs=[_row_tile(tm, D), _row_tile(tm, 2 * D), pl.BlockSpec((D, tm), lambda i: (0, i)),
                  pl.BlockSpec((per, kv_rows, 2 * BLK), lambda i: (i, 0, 0)),
                  pl.BlockSpec((1, kv_rows, 2 * BLK), lambda i: (jnp.minimum((i + 1) * per, t // BLK - 1), 0, 0)),
                  pl.BlockSpec((NKV * HD, tm), lambda i: (D // (NKV * HD), i)), _resident((HD, 1)),
                  _row_tile(tm, 2 * D), _row_tile(tm, D), _resident((1, D)), _resident((INW, D))],
        out_specs=[_row_tile(tm, D), pl.BlockSpec((1, D), lambda i: (0, 0)), pl.BlockSpec((kv_rows, tm), lambda i: (0, i)),
                   pl.BlockSpec((HD, 1), lambda i: (0, 0))],
        out_shape=[jax.ShapeDtypeStruct((t, D), F32), jax.ShapeDtypeStruct((1, D), F32),
                   jax.ShapeDtypeStruct((kv_rows, t), BF), jax.ShapeDtypeStruct((HD, 1), F32)],
        scratch_shapes=[pltpu.VMEM((HD, BLK), F32)],
        compiler_params=_params(1), name="mix_proj_bwd")(dxo, duc, dq_t, ckv, ckv, qkv_t, kg, dgp, x, g, w_t)


def _attention_tables():
    kj = np.arange(2 * BLK)[:, None]
    qi = np.arange(BLK)[None, :]
    dist = qi + BLK - kj
    in_win = (dist >= 0) & (dist < BLK)
    dpos = np.maximum(dist, 0)
    max_exact = NBUCKET // 2
    dfl = np.maximum(dpos, 1).astype(np.float32)
    large = max_exact + (np.log(dfl / np.float32(max_exact)) / np.float32(math.log(BLK / max_exact))
                         * np.float32(NBUCKET - max_exact)).astype(np.int32)
    large = np.minimum(large, NBUCKET - 1)
    bucket = np.where(dpos < max_exact, dpos, large)
    onehot = (bucket[None] == np.arange(NBUCKET)[:, None, None]).astype(np.float32)
    mask = in_win.astype(np.float32)
    mask_first = mask * (kj >= BLK)
    masks = np.stack([np.tile(mask, (1, GRP)), np.tile(mask_first, (1, GRP))])
    return onehot, masks


def _bias_table(rel_bias, onehot):
    tab = jnp.einsum("bkq,bh->hkq", onehot, rel_bias, precision=lax.Precision.HIGHEST)
    tab = tab.reshape(NKV, GRP, 2 * BLK, BLK)
    return jnp.transpose(tab, (0, 2, 1, 3)).reshape(NKV, 2 * BLK, GRP * BLK)


def _local_step(x, target, vec, weights_of, wgrad, grads_done, small_done):
    onehot_np, masks_np = _attention_tables()
    onehot = jnp.asarray(onehot_np)
    masks = jnp.asarray(masks_np)
    bias_t = jnp.where(masks[:, None] > 0.5, _bias_table(vec["rel_bias"], onehot)[None], NEG)
    sink_rows = jnp.repeat(vec["attn_sinks"].reshape(NKV, 1, GRP), BLK, axis=2)
    qg = vec["q_norm"].reshape(HD, 1)
    kg = vec["k_norm"].reshape(HD, 1)
    g1 = vec["ffn1_norm"].reshape(1, D)
    gm = vec["mix_norm"].reshape(1, D)
    g2 = vec["ffn2_norm"].reshape(1, D)
    dwb = vec["conv_dw_bias"].reshape(1, D)
    lng = vec["conv_ln_g"].reshape(1, D)
    lnb = vec["conv_ln_b"].reshape(1, D)

    w1 = weights_of("ffn1_in", (bias_t, sink_rows))
    n1, u1 = _ffn_up(x, g1, w1["ffn1_w_in"], "ffn1_up")
    w1.update(weights_of("ffn1_out", (u1,)))
    x1 = _ffn_down(x, u1, w1["ffn1_w_out"], "ffn1_down")
    wm = weights_of("mix_proj", (x1,))
    dwk = jnp.pad(wm["conv_dw_kernel"], ((0, CWP - CW), (0, 0)))
    hm, uc, gp, qkv_t = _mix_proj(x1, gm, wm["w_in"])
    zs, zc = _conv_fwd(uc, dwk, dwb, lng, lnb)
    wm.update(weights_of("mix_merge", (zs,)))
    o_t, probs, sink_probs = _attn_fwd(qkv_t, qg, kg, sink_rows, bias_t)
    x2, a, b, merged = _mix_out(zs, o_t, gp, x1, wm["conv_w_proj"], wm["attn_w_o"], wm["w_out"])
    w2 = weights_of("ffn2", (x2,))
    gv = {}
    n2, du2, h2, dy2, dx2, sq, gv["ffn2_norm"] = _ffn_last(x2, target, g2, w2["ffn2_w_in"], w2["ffn2_w_out"], "ffn2")

    deps = grads_done("ffn2", {"ffn2_w_in": wgrad(du2, n2, "ffn2_dw_in", False),
                               "ffn2_w_out": wgrad(h2, dy2, "ffn2_dw_out", False)})

    dzs, do_t, dgp, da, db, dx2b = _mix_out_bwd(dx2, a, b, gp, wm["conv_w_proj"], wm["attn_w_o"], wm["w_out"], deps=deps)
    deps = grads_done("mix_out", {"w_out": wgrad(merged, dx2b, "mix_dw_out", False),
                                  "conv_w_proj": wgrad(zs, da, "mix_dw_cp", False),
                                  "attn_w_o": wgrad(o_t, db, "mix_dw_o", True)})

    dq_t, ckv, dqg, dsink, dbias = _attn_bwd(qkv_t, do_t, probs, sink_probs, qg, kg, onehot, deps=deps)
    gv["q_norm"] = dqg.reshape(HD)
    gv["attn_sinks"] = dsink[:, :GRP].reshape(NQ)
    gv["rel_bias"] = dbias[:, :, :GRP].reshape(NBUCKET, NQ)

    duc, dk_conv, gv["conv_dw_bias"], gv["conv_ln_g"], gv["conv_ln_b"] = _conv_bwd(uc, zc, dzs, dwk, lng, lnb)
    gv["conv_dw_kernel"] = dk_conv[:CW]

    dx1, gv["mix_norm"], dkv_t, dkg = _mix_proj_bwd(dx2, duc, dq_t, ckv, qkv_t, kg, dgp, x1, gm, wm["w_in"])
    gv["k_norm"] = dkg.reshape(HD)
    deps = grads_done("mix_in", {"w_in": _wgrad_mix(duc, dq_t, dkv_t, dgp, hm)})

    dx0, du1, h1, dy1, gv["ffn1_norm"] = _ffn_bwd(dx1, x, g1, u1, w1["ffn1_w_in"], w1["ffn1_w_out"], "ffn1_bwd", deps=deps)
    for k in ("ffn1_norm", "mix_norm", "ffn2_norm", "conv_dw_bias", "conv_ln_g", "conv_ln_b"):
        gv[k] = gv[k].reshape(D)
    deps = small_done(gv, sq)
    deps = grads_done("ffn1_in", {"ffn1_w_in": wgrad(du1, n1, "ffn1_dw_in", False, deps)})
    grads_done("ffn1_out", {"ffn1_w_out": wgrad(h1, dy1, "ffn1_dw_out", False, deps)})
    return dx0


MESH_ID = pl.DeviceIdType.MESH


def _position():
    return lax.axis_index("x"), lax.axis_index("y"), lax.axis_index("c")


def _shard_rows(ref, index, rows):
    return ref.at[pl.ds(pl.multiple_of(index * rows, 16), rows), :]


def _prep(weights, taps, me, name, deps=()):
    n = len(weights)
    n_deps = len(deps)
    with_taps = taps is not None

    def body(me_ref, *refs):
        refs = refs[n_deps:]
        ins, outs = refs[:len(refs) // 2], refs[len(refs) // 2:]
        for k in range(n):
            outs[k][...] = ins[k][...].astype(BF)
        if with_taps:
            outs[n][0:CW, :] = ins[n][...]
            outs[n][CW:, :] = jnp.zeros((CWP - CW, BLK), F32)

    shard_shapes = [w.shape for w in weights] + [(CWP, BLK)] * with_taps
    dtypes = [BF] * n + [F32] * with_taps
    ins = list(weights) + [taps] * with_taps
    return pl.pallas_call(
        body,
        grid_spec=pltpu.PrefetchScalarGridSpec(
            num_scalar_prefetch=1, grid=(1,),
            in_specs=[ANY] * n_deps + [pl.BlockSpec(a.shape, lambda i, m: (0, 0), pipeline_mode=pl.Buffered(1)) for a in ins],
            out_specs=[pl.BlockSpec(s, lambda i, m: (m[0], 0)) for s in shard_shapes]),
        out_shape=[jax.ShapeDtypeStruct((N_DEV * s[0], s[1]), d) for s, d in zip(shard_shapes, dtypes)],
        compiler_params=_params(1), name=name)(me, *deps, *ins)


HBM = pl.BlockSpec(memory_space=pltpu.HBM)
SEM = pl.BlockSpec(memory_space=pltpu.SEMAPHORE)
DATAFLOW = pltpu.SideEffectType.DATAFLOW_SIDE_EFFECTING
TOKEN = jax.ShapeDtypeStruct((8, 128), F32)


def _in_hbm(x):
    return pltpu.with_memory_space_constraint(x, pltpu.HBM)


def _hbm_like(arrays):
    return [pltpu.HBM(a.shape, a.dtype) for a in arrays]


def _other_chips(x, y):
    return [(1 - x, y), (x, 1 - y), (1 - x, 1 - y)]


def _device_index(chip, c):
    return 4 * chip[0] + 2 * chip[1] + c


def _chip_index(chip):
    return 2 * chip[0] + chip[1]


class _Exchange:
    def __init__(self, gather, all_cores=False):
        self.gather = gather
        self.all_cores = all_cores
        self.n_peers = N_DEV - 1 if all_cores else 3

    def peers(self, x, y, c):
        if self.all_cores:
            return [(x ^ (k >> 2), y ^ ((k >> 1) & 1), c ^ (k & 1)) for k in range(1, N_DEV)]
        return [(*chip, c) for chip in _other_chips(x, y)]

    def sent(self, x, y, c, peer):
        return _device_index((x, y), c) if self.gather else _chip_index(peer[:2])

    def lands_at(self, x, y, c):
        return _device_index((x, y), c) if self.gather else _chip_index((x, y))

    def arrives_at(self, peer):
        return _device_index(peer[:2], peer[2]) if self.gather else _chip_index(peer[:2])


def _ici_copies_start(sets, sources, landings, exchanges, name, deps=()):
    n = len(landings)
    arrays = (list(sources) if sources is not None else []) + list(landings)
    first_land = len(arrays) - n
    n_sets = len(sets)
    n_deps = len(deps)

    def body(*refs):
        refs = refs[n_deps:]
        src, land = refs[:n], refs[first_land:first_land + n]
        sems = refs[len(arrays):len(arrays) + 2 * n_sets]
        token = refs[-1]
        x, y, c = _position()
        for s, (members, exchange) in enumerate(zip(sets, exchanges)):
            for slot, (k, rows) in enumerate(members):
                for j, peer in enumerate(exchange.peers(x, y, c)):
                    at = exchange.n_peers * slot + j
                    pltpu.make_async_remote_copy(
                        src_ref=_shard_rows(src[k], exchange.sent(x, y, c, peer), rows),
                        dst_ref=_shard_rows(land[k], exchange.lands_at(x, y, c), rows),
                        send_sem=sems[2 * s].at[at], recv_sem=sems[2 * s + 1].at[at],
                        device_id=peer, device_id_type=MESH_ID).start()
        token[...] = jnp.zeros_like(token)

    sem_shapes = []
    for members, exchange in zip(sets, exchanges):
        sem_shapes += [pltpu.SemaphoreType.DMA((exchange.n_peers * len(members),))] * 2
    out = pl.pallas_call(
        body, name=name,
        out_shape=sem_shapes + _hbm_like(arrays) + [TOKEN],
        in_specs=[ANY] * n_deps + [HBM] * len(arrays),
        out_specs=[SEM] * (2 * n_sets) + [HBM] * len(arrays) + [pl.BlockSpec(memory_space=pltpu.VMEM)],
        input_output_aliases={n_deps + i: 2 * n_sets + i for i in range(len(arrays))},
        compiler_params=pltpu.CompilerParams(has_side_effects=DATAFLOW),
    )(*deps, *[_in_hbm(a) for a in arrays])
    sems = [(out[2 * s], out[2 * s + 1]) for s in range(n_sets)]
    thru = list(out[2 * n_sets:2 * n_sets + len(arrays)])
    return sems, (thru[:first_land] if sources is not None else None), thru[first_land:], out[-1]


def _ici_copies_wait(sems, members, sources, landings, exchange, after, name):
    n = len(landings)
    arrays = (list(sources) if sources is not None else []) + list(landings)
    first_land = len(arrays) - n

    def body(*refs):
        src, land = refs[:n], refs[first_land:first_land + n]
        send_sems, recv_sems = refs[len(arrays)], refs[len(arrays) + 1]
        x, y, c = _position()
        for slot, rows in enumerate(members):
            for j, peer in enumerate(exchange.peers(x, y, c)):
                at = exchange.n_peers * slot + j
                cp = pltpu.make_async_remote_copy(
                    src_ref=_shard_rows(src[slot], exchange.sent(x, y, c, peer), rows),
                    dst_ref=_shard_rows(land[slot], exchange.arrives_at(peer), rows),
                    send_sem=send_sems.at[at], recv_sem=recv_sems.at[at], device_id=peer, device_id_type=MESH_ID)
                cp.wait_send()
                cp.wait_recv()

    out = pl.pallas_call(
        body, name=name, out_shape=_hbm_like(arrays),
        in_specs=[HBM] * len(arrays) + [SEM, SEM] + [ANY] * len(after), out_specs=[HBM] * len(arrays),
        input_output_aliases={i: i for i in range(len(arrays))},
        compiler_params=pltpu.CompilerParams(has_side_effects=DATAFLOW),
    )(*arrays, sems[0], sems[1], *after)
    return list(out[first_land:])


def _d2d_gather(buffers, rows, name):
    n = len(buffers)

    def body(*refs):
        land = refs[n:2 * n]
        send_sems, recv_sems = refs[2 * n:]
        x, y, c = _position()
        chips = [(x, y)] + _other_chips(x, y)
        sends, recvs = [], []
        for k in range(n):
            for j, chip in enumerate(chips):
                for copies, core in ((sends, c), (recvs, 1 - c)):
                    block = _shard_rows(land[k], _device_index(chip, core), rows[k])
                    copies.append(pltpu.make_async_remote_copy(
                        src_ref=block, dst_ref=block, send_sem=send_sems.at[k, j], recv_sem=recv_sems.at[k, j],
                        device_id=(x, y, 1 - c), device_id_type=MESH_ID))
        for cp in sends:
            cp.start()
        for cp in recvs:
            cp.wait_recv()
        for cp in sends:
            cp.wait_send()

    return pl.pallas_call(
        body, name=name, out_shape=[jax.ShapeDtypeStruct(a.shape, a.dtype) for a in buffers],
        in_specs=[ANY] * n, out_specs=[ANY] * n, input_output_aliases={i: i for i in range(n)},
        scratch_shapes=[pltpu.SemaphoreType.DMA((n, 4)), pltpu.SemaphoreType.DMA((n, 4))],
    )(*buffers)


def _rs_pair(grads, name):
    n = len(grads)
    rows = [g.shape[0] // N_DEV for g in grads]

    def body(*refs):
        ins, outs = refs[:n], refs[n:2 * n]
        send_sems, recv_sems = refs[2 * n:]
        x, y, c = _position()
        copies = []
        for k in range(n):
            for q in range(4):
                copies.append(pltpu.make_async_remote_copy(
                    src_ref=_shard_rows(ins[k], 2 * q + 1 - c, rows[k]), dst_ref=_shard_rows(outs[k], q, rows[k]),
                    send_sem=send_sems.at[k, q], recv_sem=recv_sems.at[k, q], device_id=(x, y, 1 - c),
                    device_id_type=MESH_ID))
        for cp in copies:
            cp.start()
        for cp in copies:
            cp.wait()

    return pl.pallas_call(
        body, out_shape=[jax.ShapeDtypeStruct((4 * r, g.shape[1]), g.dtype) for g, r in zip(grads, rows)],
        in_specs=[ANY] * n, out_specs=[ANY] * n,
        scratch_shapes=[pltpu.SemaphoreType.DMA((n, 4)), pltpu.SemaphoreType.DMA((n, 4))],
        name=name)(*grads)


def _wgrad_pair(lhs, rhs, name, *, lhs_is_transposed, deps=()):
    t = rhs.shape[0]
    n = lhs.shape[0] if lhs_is_transposed else lhs.shape[1]
    r = n // N_DEV
    n_chips = N_DEV // 2
    per = 1 if (2 * r) % BLK == 0 else 2
    steps = n_chips // per

    def body(l_ref, r_ref, kept_ref, recv_ref, res, send_sems, recv_sems):
        q = pl.program_id(0)
        slot = q % 2
        x, y, c = _position()

        def send(step, buf, i):
            return pltpu.make_async_remote_copy(
                src_ref=res.at[buf, pl.ds(pl.multiple_of((2 * i + 1 - c) * r, 16), r), :],
                dst_ref=_shard_rows(recv_ref, step * per + i, r),
                send_sem=send_sems.at[buf, i], recv_sem=recv_sems.at[step * per + i],
                device_id=(x, y, 1 - c), device_id_type=MESH_ID)

        @pl.when(q >= 2)
        def _():
            for i in range(per):
                send(q - 2, slot, i).wait_send()

        if lhs_is_transposed:
            res[slot] = _dot(l_ref[...], r_ref[...]).astype(BF)
        else:
            res[slot] = _dot_tn(l_ref[...], r_ref[...]).astype(BF)
        for i in range(per):
            kept_ref[i * r:(i + 1) * r, :] = res[slot, pl.ds(pl.multiple_of((2 * i + c) * r, 16), r), :]
            send(q, slot, i).start()

        @pl.when(q == steps - 1)
        def _():
            for i in range(per):
                if steps > 1:
                    send(q - 1, 1 - slot, i).wait_send()
                send(q, slot, i).wait_send()
            for chip in range(n_chips):
                send(chip // per, 0, chip % per).wait_recv()

    width = 2 * r * per
    lhs_spec = pl.BlockSpec((width, t), lambda q: (q, 0)) if lhs_is_transposed else pl.BlockSpec((t, width), lambda q: (0, q))
    return _call(
        body, deps, (lhs, rhs), grid=(steps,),
        in_specs=[lhs_spec, _resident((t, D))],
        out_specs=[pl.BlockSpec((per * r, D), lambda q: (q, 0)), ANY],
        out_shape=[jax.ShapeDtypeStruct((n // 2, D), BF)] * 2,
        scratch_shapes=[pltpu.VMEM((2, width, D), BF), pltpu.SemaphoreType.DMA((2, per)),
                        pltpu.SemaphoreType.DMA((n_chips,))],
        compiler_params=_params(1), name=name)


def _wgrad_pair_sum(lhs, rhs, place, name, *, lhs_is_transposed, deps=()):
    t = rhs.shape[0]
    n = lhs.shape[0] if lhs_is_transposed else lhs.shape[1]
    r = n // N_DEV
    n_chips = N_DEV // 2
    per = 1 if (2 * r) % BLK == 0 else 2
    steps = n_chips // per
    n_deps = len(deps)

    def body(place_ref, *refs):
        l_ref, r_ref, part_ref, land_ref, res, inbox, send_sems, recv_sems = refs[n_deps:]
        q = pl.program_id(0)
        slot = q % 2
        x, y, c = _position()

        def send(step, buf, i):
            return pltpu.make_async_remote_copy(
                src_ref=res.at[buf, pl.ds(pl.multiple_of((2 * i + 1 - c) * r, 16), r), :], dst_ref=inbox.at[step * per + i],
                send_sem=send_sems.at[buf, i], recv_sem=recv_sems.at[step * per + i],
                device_id=(x, y, 1 - c), device_id_type=MESH_ID)

        @pl.when(q < steps)
        def _():
            @pl.when(q >= 2)
            def _():
                for i in range(per):
                    send(q - 2, slot, i).wait_send()

            if lhs_is_transposed:
                res[slot] = _dot(l_ref[...], r_ref[...]).astype(BF)
            else:
                res[slot] = _dot_tn(l_ref[...], r_ref[...]).astype(BF)
            for i in range(per):
                send(q, slot, i).start()

        @pl.when(q >= 1)
        def _():
            for i in range(per):
                chip = (q - 1) * per + i
                send(q - 1, 1 - slot, i).wait_recv()
                kept = res[1 - slot, pl.ds(pl.multiple_of((2 * i + c) * r, 16), r), :]
                total = (kept.astype(F32) + inbox[chip].astype(F32)).astype(BF)
                part_ref[i * r:(i + 1) * r, :] = total

                @pl.when(chip == place_ref[1])
                def _():
                    land_ref[...] = total

        @pl.when(q == steps)
        def _():
            for i in range(per):
                if steps > 1:
                    send(q - 2, slot, i).wait_send()
                send(q - 1, 1 - slot, i).wait_send()

    width = 2 * r * per
    last = steps - 1
    if lhs_is_transposed:
        lhs_spec = pl.BlockSpec((width, t), lambda q, p: (jnp.minimum(q, last), 0))
    else:
        lhs_spec = pl.BlockSpec((t, width), lambda q, p: (0, jnp.minimum(q, last)))
    return pl.pallas_call(
        body,
        grid_spec=pltpu.PrefetchScalarGridSpec(
            num_scalar_prefetch=1, grid=(steps + 1,),
            in_specs=[ANY] * n_deps + [lhs_spec, pl.BlockSpec((t, D), lambda q, p: (0, 0), pipeline_mode=pl.Buffered(1))],
            out_specs=[pl.BlockSpec((per * r, D), lambda q, p: (jnp.maximum(q - 1, 0), 0)),
                       pl.BlockSpec((r, D), lambda q, p: (p[1], 0))],
            scratch_shapes=[pltpu.VMEM((2, width, D), BF), pltpu.VMEM((n_chips, r, D), BF),
                            pltpu.SemaphoreType.DMA((2, per)), pltpu.SemaphoreType.DMA((n_chips,))]),
        out_shape=[jax.ShapeDtypeStruct((n // 2, D), BF)] * 2,
        compiler_params=_params(1), name=name)(place, *deps, lhs, rhs)


def _pair_add(grad, received, place, name, kept_only=False):
    r = received.shape[0] // 4
    parity = 0 if kept_only else 1

    def body(place_ref, g_ref, r_ref, o_ref, land_ref):
        total = (g_ref[...].astype(F32) + r_ref[...].astype(F32)).astype(BF)
        o_ref[...] = total

        @pl.when(pl.program_id(0) == place_ref[1])
        def _():
            land_ref[...] = total

    return pl.pallas_call(
        body,
        grid_spec=pltpu.PrefetchScalarGridSpec(
            num_scalar_prefetch=1, grid=(4,),
            in_specs=[pl.BlockSpec((r, D), lambda q, p: ((1 + parity) * q + parity * p[0], 0)),
                      pl.BlockSpec((r, D), lambda q, p: (q, 0))],
            out_specs=[pl.BlockSpec((r, D), lambda q, p: (q, 0)), pl.BlockSpec((r, D), lambda q, p: (p[1], 0))]),
        out_shape=[jax.ShapeDtypeStruct(received.shape, BF)] * 2,
        compiler_params=_params(1), name=name)(place, grad, received)


def _sum_blocks(gathered, rows):
    def body(b_ref, o_ref):
        acc = b_ref[0:rows, :]
        for d in range(1, N_DEV):
            acc = acc + b_ref[d * rows:(d + 1) * rows, :]
        o_ref[...] = acc

    return pl.pallas_call(body, out_shape=jax.ShapeDtypeStruct((rows, D), F32), name="small_sum")(gathered)


def _adamw_math(w, g, m, v):
    m = ADAM_B1 * m + (1.0 - ADAM_B1) * g
    v = ADAM_B2 * v + (1.0 - ADAM_B2) * (g * g)
    m_hat = m / (1.0 - ADAM_B1 ** ADAM_STEP)
    v_hat = v / (1.0 - ADAM_B2 ** ADAM_STEP)
    delta = -ADAM_LR * (m_hat / (jnp.sqrt(v_hat) + ADAM_EPS) + ADAM_WD * w)
    return delta, m, v


def _sum_partials(blocks):
    g = blocks[0].astype(F32)
    for blk in blocks[1:]:
        g = g + blk.astype(F32)
    return g


def _reduce_adamw(landed, w, m, v, name):
    r = w.shape[0]
    tr = 352 if r % 352 == 0 else r
    per = r // tr

    def body(r0, r1, r2, r3, w_ref, m_ref, v_ref, g_ref, d_ref, nm_ref, nv_ref):
        g = _sum_partials([r0[...], r1[...], r2[...], r3[...]])
        g_ref[...] = g
        d_ref[...], nm_ref[...], nv_ref[...] = _adamw_math(w_ref[...], g, m_ref[...], v_ref[...])

    tile = _row_tile(tr, D)
    return pl.pallas_call(
        body, grid=(per,),
        in_specs=[pl.BlockSpec((tr, D), lambda i, q=q: (q * per + i, 0)) for q in range(4)] + [tile] * 3,
        out_specs=[tile] * 4, out_shape=[jax.ShapeDtypeStruct(w.shape, F32)] * 4,
        compiler_params=_params(1), name=name)(landed, landed, landed, landed, w, m, v)


def _adamw_small(w, g, m, v, name):
    def body(w_ref, g_ref, m_ref, v_ref, d_ref, nm_ref, nv_ref):
        d_ref[...], nm_ref[...], nv_ref[...] = _adamw_math(w_ref[...], g_ref[...], m_ref[...], v_ref[...])

    return pl.pallas_call(body, out_shape=[jax.ShapeDtypeStruct(w.shape, F32)] * 3, name=name)(w, g, m, v)


WEIGHTS = ("ffn1_norm", "ffn1_w_in", "ffn1_w_out", "mix_norm", "w_in", "conv_dw_kernel", "conv_dw_bias", "conv_ln_g",
           "conv_ln_b", "conv_w_proj", "q_norm", "k_norm", "attn_sinks", "rel_bias", "attn_w_o", "w_out", "ffn2_norm",
           "ffn2_w_in", "ffn2_w_out")
MATRICES = ("ffn1_w_in", "ffn1_w_out", "w_in", "conv_w_proj", "attn_w_o", "w_out", "ffn2_w_in", "ffn2_w_out")
COLUMN_SHARDED = ("ffn1_w_in", "w_in", "ffn2_w_in")
ROW_VECTORS = ("ffn1_norm", "mix_norm", "conv_dw_bias", "conv_ln_g", "conv_ln_b", "ffn2_norm")
PACKED = (("q_norm", HD), ("k_norm", HD), ("attn_sinks", NQ), ("rel_bias", NBUCKET * NQ))
GATHER = _Exchange(gather=True)
GATHER_ALL = _Exchange(gather=True, all_cores=True)
SCATTER = _Exchange(gather=False)
GATHER_STAGES = ("ffn1_in", "ffn1_out", "mix_proj", "mix_merge", "ffn2")
STAGE_GATHER = {"ffn1_in": GATHER, "ffn1_out": GATHER, "mix_proj": GATHER, "mix_merge": GATHER, "ffn2": GATHER_ALL}
STAGE_MEMBERS = {"ffn1_in": ("ffn1_w_in",), "ffn1_out": ("ffn1_w_out",),
                 "mix_proj": ("w_in", "taps"), "mix_merge": ("conv_w_proj", "attn_w_o", "w_out"),
                 "ffn2": ("ffn2_w_in", "ffn2_w_out")}
ROW_PACKED = len(ROW_VECTORS)
ROW_LOSS = ROW_PACKED + 1
ROW_TAPS = 8
PAYLOAD_ROWS = 48


def _pack_small(values, last_row):
    packed = jnp.concatenate([values[k].reshape(-1) for k, _ in PACKED])
    packed = jnp.pad(packed, (0, D - packed.shape[0])).reshape(1, D)
    return jnp.concatenate([values[k].reshape(1, D) for k in ROW_VECTORS] + [packed, last_row], axis=0)


def _unpack_small(rows):
    out = {k: rows[i] for i, k in enumerate(ROW_VECTORS)}
    at = 0
    for k, size in PACKED:
        out[k] = rows[ROW_PACKED, at:at + size]
        at += size
    out["rel_bias"] = out["rel_bias"].reshape(NBUCKET, NQ)
    return out


def kernel(x, ffn1_norm, ffn1_w_in, ffn1_w_out, mix_norm, w_in, conv_dw_kernel, conv_dw_bias, conv_ln_g, conv_ln_b, conv_w_proj, q_norm, k_norm, attn_sinks, rel_bias, attn_w_o, w_out, ffn2_norm, ffn2_w_in, ffn2_w_out, loss_target, m_ffn1_norm, m_ffn1_w_in, m_ffn1_w_out, m_mix_norm, m_w_in, m_conv_dw_kernel, m_conv_dw_bias, m_conv_ln_g, m_conv_ln_b, m_conv_w_proj, m_q_norm, m_k_norm, m_attn_sinks, m_rel_bias, m_attn_w_o, m_w_out, m_ffn2_norm, m_ffn2_w_in, m_ffn2_w_out, v_ffn1_norm, v_ffn1_w_in, v_ffn1_w_out, v_mix_norm, v_w_in, v_conv_dw_kernel, v_conv_dw_bias, v_conv_ln_g, v_conv_ln_b, v_conv_w_proj, v_q_norm, v_k_norm, v_attn_sinks, v_rel_bias, v_attn_w_o, v_w_out, v_ffn2_norm, v_ffn2_w_in, v_ffn2_w_out):
    w = dict(ffn1_norm=ffn1_norm, ffn1_w_in=ffn1_w_in, ffn1_w_out=ffn1_w_out, mix_norm=mix_norm, w_in=w_in,
             conv_dw_kernel=conv_dw_kernel, conv_dw_bias=conv_dw_bias, conv_ln_g=conv_ln_g, conv_ln_b=conv_ln_b,
             conv_w_proj=conv_w_proj, q_norm=q_norm, k_norm=k_norm, attn_sinks=attn_sinks, rel_bias=rel_bias,
             attn_w_o=attn_w_o, w_out=w_out, ffn2_norm=ffn2_norm, ffn2_w_in=ffn2_w_in, ffn2_w_out=ffn2_w_out)
    m = dict(ffn1_norm=m_ffn1_norm, ffn1_w_in=m_ffn1_w_in, ffn1_w_out=m_ffn1_w_out, mix_norm=m_mix_norm, w_in=m_w_in,
             conv_dw_kernel=m_conv_dw_kernel, conv_dw_bias=m_conv_dw_bias, conv_ln_g=m_conv_ln_g, conv_ln_b=m_conv_ln_b,
             conv_w_proj=m_conv_w_proj, q_norm=m_q_norm, k_norm=m_k_norm, attn_sinks=m_attn_sinks, rel_bias=m_rel_bias,
             attn_w_o=m_attn_w_o, w_out=m_w_out, ffn2_norm=m_ffn2_norm, ffn2_w_in=m_ffn2_w_in, ffn2_w_out=m_ffn2_w_out)
    v = dict(ffn1_norm=v_ffn1_norm, ffn1_w_in=v_ffn1_w_in, ffn1_w_out=v_ffn1_w_out, mix_norm=v_mix_norm, w_in=v_w_in,
             conv_dw_kernel=v_conv_dw_kernel, conv_dw_bias=v_conv_dw_bias, conv_ln_g=v_conv_ln_g, conv_ln_b=v_conv_ln_b,
             conv_w_proj=v_conv_w_proj, q_norm=v_q_norm, k_norm=v_k_norm, attn_sinks=v_attn_sinks, rel_bias=v_rel_bias,
             attn_w_o=v_attn_w_o, w_out=v_w_out, ffn2_norm=v_ffn2_norm, ffn2_w_in=v_ffn2_w_in, ffn2_w_out=v_ffn2_w_out)
    px, py, pc = _position()
    me = 4 * px + 2 * py + pc
    place = jnp.stack([pc, 2 * px + py]).astype(jnp.int32)

    rows_of = lambda k, a: a.T if k in COLUMN_SHARDED else a
    me1 = me.astype(jnp.int32).reshape(1)
    first = STAGE_MEMBERS[GATHER_STAGES[0]]
    rest = tuple(k for k in MATRICES if k not in first)
    buffers = dict(zip(first, _prep([rows_of(k, w[k]) for k in first], None, me1, "prep_first")))
    shard_rows = dict({k: rows_of(k, w[k]).shape[0] for k in MATRICES}, taps=CWP)
    landings, sets = [], []
    for stage in GATHER_STAGES:
        sets.append([(len(landings) + i, shard_rows[k]) for i, k in enumerate(STAGE_MEMBERS[stage])])
        landings += list(STAGE_MEMBERS[stage])
    sems, _, land_thru, token = _ici_copies_start(sets[:1], None, [buffers[k] for k in first],
                                                  [STAGE_GATHER[GATHER_STAGES[0]]], "gather_start_first")
    buffers.update(zip(rest + ("taps",), _prep([rows_of(k, w[k]) for k in rest], conv_dw_kernel, me1, "prep", deps=[token])))
    later = [[(k - len(first), r) for k, r in members] for members in sets[1:]]
    sems_later, _, thru_later, _ = _ici_copies_start(later, None, [buffers[k] for k in landings[len(first):]],
                                                     [STAGE_GATHER[s] for s in GATHER_STAGES[1:]], "gather_start")
    sems, land_thru = sems + sems_later, land_thru + thru_later

    def weights_of(stage, after):
        s = GATHER_STAGES.index(stage)
        rows = [r for _, r in sets[s]]
        landed = _ici_copies_wait(sems[s], rows, None, [land_thru[k] for k, _ in sets[s]], STAGE_GATHER[stage],
                                  list(after), "gather_wait_" + stage)
        if not STAGE_GATHER[stage].all_cores:
            landed = _d2d_gather(landed, rows, "gather_d2d_" + stage)
        out = dict(zip(STAGE_MEMBERS[stage], landed))
        if "taps" in out:
            taps = out.pop("taps")
            out["conv_dw_kernel"] = jnp.transpose(taps.reshape(N_DEV, CWP, BLK), (1, 0, 2)).reshape(CWP, D)[:CW]
        return out

    in_flight = []

    def wgrad(lhs, rhs, name, lhs_is_transposed, deps=()):
        rows = (lhs.shape[0] if lhs_is_transposed else lhs.shape[1]) // N_DEV
        if rows <= WGRAD_SUM_MAX_ROWS:
            return ("summed",) + tuple(_wgrad_pair_sum(lhs, rhs, place, name, lhs_is_transposed=lhs_is_transposed, deps=deps))
        return ("paired",) + tuple(_wgrad_pair(lhs, rhs, name, lhs_is_transposed=lhs_is_transposed, deps=deps))

    def grads_done(stage, grads):
        names = list(grads)
        added = []
        for k in names:
            if not isinstance(grads[k], tuple):
                received, = _rs_pair([grads[k]], "rs_pair_" + k)
                added.append(_pair_add(grads[k], received, place, "pair_add_" + k))
            elif grads[k][0] == "paired":
                added.append(_pair_add(grads[k][1], grads[k][2], place, "pair_add_" + k, kept_only=True))
            else:
                added.append(grads[k][1:])
        partials = [p for p, _ in added]
        members = [(i, p.shape[0] // 4) for i, p in enumerate(partials)]
        sem, p_thru, l_thru, token = _ici_copies_start([members], partials, [l for _, l in added], [SCATTER],
                                                       "scatter_start_" + stage)
        in_flight.append((stage, names, sem[0], p_thru, l_thru, token))
        return [token]

    small = []

    def small_done(gv, sq):
        payload = jnp.concatenate([_pack_small(gv, sq), jnp.pad(gv["conv_dw_kernel"], ((0, PAYLOAD_ROWS - ROW_TAPS - CW), (0, 0)))],
                                  axis=0)
        mine = lax.dynamic_update_slice_in_dim(lax.empty((N_DEV * PAYLOAD_ROWS, D), F32), payload, me * PAYLOAD_ROWS, axis=0)
        sems, _, thru, token = _ici_copies_start([[(0, PAYLOAD_ROWS)]], None, [mine], [GATHER_ALL], "small_start")
        small.append((sems[0], thru))
        return [token]

    vec = {k: w[k] for k in WEIGHTS if k not in MATRICES and k != "conv_dw_kernel"}
    dx0 = _local_step(x[0], loss_target[0], vec, weights_of, wgrad, grads_done, small_done)
    gathered, = _ici_copies_wait(small[0][0], [PAYLOAD_ROWS], None, small[0][1], GATHER_ALL, [in_flight[-1][-1]], "small_wait")
    total = _sum_blocks(gathered, PAYLOAD_ROWS)
    loss = (0.5 / D) * jnp.sum(total[ROW_LOSS])

    grads, delta, new_m, new_v = {}, {}, {}, {}
    after = [total]
    for stage, names, sem, p_thru, l_thru, _ in in_flight:
        landed = _ici_copies_wait(sem, [p.shape[0] // 4 for p in p_thru], p_thru, l_thru, SCATTER, after,
                                  "scatter_wait_" + stage)
        after = []
        for k, buf in zip(names, landed):
            out = _reduce_adamw(buf, rows_of(k, w[k]), rows_of(k, m[k]), rows_of(k, v[k]), "adamw_" + k)
            grads[k], delta[k], new_m[k], new_v[k] = [rows_of(k, a) for a in out]
            after.append(out[1])
    zero_row = jnp.zeros((1, D), F32)
    d8, m8, v8 = _adamw_small(_pack_small(w, zero_row), total[:ROW_TAPS], _pack_small(m, zero_row),
                              _pack_small(v, zero_row), "adamw_small")
    grads.update(_unpack_small(total[:ROW_TAPS]))
    delta.update(_unpack_small(d8))
    new_m.update(_unpack_small(m8))
    new_v.update(_unpack_small(v8))
    k = "conv_dw_kernel"
    grads[k] = lax.dynamic_slice_in_dim(total[ROW_TAPS:ROW_TAPS + CW], me * BLK, BLK, axis=1)
    delta[k], new_m[k], new_v[k] = _adamw_small(w[k], grads[k], m[k], v[k], "adamw_taps")

    return (loss, dx0[None], *[grads[k] for k in WEIGHTS], *[delta[k] for k in WEIGHTS],
            *[new_m[k] for k in WEIGHTS], *[new_v[k] for k in WEIGHTS])
```

```python
import functools
import math

import numpy as np
import jax
import jax.numpy as jnp
from jax import lax
from jax.experimental import pallas as pl
from jax.experimental.pallas import tpu as pltpu

F32 = jnp.float32
BF = jnp.bfloat16

D = 1024
F = 2816
INW = 5632
CW = 31
CWP = 32
HD = 64
NQ = 16
NKV = 4
GRP = NQ // NKV
BLK = 128
NBUCKET = 32
EPS = 1e-6
NEG = float(jnp.finfo(jnp.float32).min)
QK_SCALE = 1.0 / math.sqrt(HD)
R_CONV = (0, 2048)
R_QKV = (2048, 3584)
R_Q = (2048, 3072)
R_KV = (3072, 3584)
R_GATE = (3584, 5632)

N_DEV = 8
VMEM_LIMIT_V7X = 56 * 1024 * 1024
ROW_TILE = 256
ROW_TILE_WIDE = 512
WGRAD_SUM_MAX_ROWS = 352

ADAM_LR = 0.001
ADAM_B1 = 0.9
ADAM_B2 = 0.999
ADAM_EPS = 1e-08
ADAM_WD = 0.01
ADAM_STEP = 10

NT_DIMS = (((1,), (1,)), ((), ()))
TN_DIMS = (((0,), (0,)), ((), ()))


def _dot(a, b):
    return jnp.dot(a, b, preferred_element_type=F32)


def _dot_nt(a, b):
    return lax.dot_general(a, b, NT_DIMS, preferred_element_type=F32)


def _dot_tn(a, b):
    return lax.dot_general(a, b, TN_DIMS, preferred_element_type=F32)


def _sig(x):
    return 0.5 * jnp.tanh(0.5 * x) + 0.5


ANY = pl.BlockSpec(memory_space=pl.ANY)


def _call(body, deps, args, **kw):
    n = len(deps)
    if n:
        kw["in_specs"] = [ANY] * n + list(kw["in_specs"])
        return pl.pallas_call(lambda *refs: body(*refs[n:]), **kw)(*deps, *args)
    return pl.pallas_call(body, **kw)(*args)


def _params(n_axes):
    return pltpu.CompilerParams(dimension_semantics=("arbitrary",) * n_axes, vmem_limit_bytes=VMEM_LIMIT_V7X)


def _resident(shape):
    zeros = (0,) * len(shape)
    return pl.BlockSpec(shape, lambda *_: zeros, pipeline_mode=pl.Buffered(1))


def _row_tile(rows, cols):
    return pl.BlockSpec((rows, cols), lambda i: (i, 0))


def _rms_stats(x):
    r = lax.rsqrt(jnp.mean(x * x, axis=-1, keepdims=True) + EPS)
    return r, x * r


def _rms_bwd(dn, x, g):
    r, xh = _rms_stats(x)
    dxh = dn * g
    dx = r * (dxh - xh * jnp.mean(dxh * xh, axis=-1, keepdims=True))
    return dx, jnp.sum(dn * xh, axis=0, keepdims=True)


def _ffn_last(x, target, g, w_in_t, w_out, name):
    t = x.shape[0]
    tm = min(ROW_TILE, t)

    def body(x_ref, t_ref, g_ref, w_ref, wo_ref, n_ref, du_ref, h_ref, dy_ref, dx_ref, sq_ref, dg_ref):
        @pl.when(pl.program_id(0) == 0)
        def _():
            sq_ref[...] = jnp.zeros_like(sq_ref)
            dg_ref[...] = jnp.zeros_like(dg_ref)

        x = x_ref[...]
        g = g_ref[...]
        r, xh = _rms_stats(x)
        n = (xh * g).astype(BF)
        n_ref[...] = n
        u = _dot_nt(n, w_ref[...])
        a = u[:, :F]
        b = u[:, F:]
        s = _sig(a)
        sa = a * s
        h = (sa * b).astype(BF)
        h_ref[...] = h
        err = x + 0.5 * _dot(h, wo_ref[...]) - t_ref[...]
        sq_ref[...] += jnp.sum(err * err, axis=0, keepdims=True)
        dxo = err * (1.0 / D)
        dy = (0.5 * dxo).astype(BF)
        dy_ref[...] = dy
        dh = _dot_nt(dy, wo_ref[...])
        du_ref[:, :F] = (dh * b * (s * (1.0 + a * (1.0 - s)))).astype(BF)
        du_ref[:, F:] = (dh * sa).astype(BF)
        dn = _dot(du_ref[...], w_ref[...])
        dxh = dn * g
        dx_ref[...] = dxo + r * (dxh - xh * jnp.mean(dxh * xh, axis=-1, keepdims=True))
        dg_ref[...] += jnp.sum(dn * xh, axis=0, keepdims=True)

    vec = pl.BlockSpec((1, D), lambda i: (0, 0))
    return pl.pallas_call(
        body, grid=(t // tm,),
        in_specs=[_row_tile(tm, D), _row_tile(tm, D), _resident((1, D)), _resident((INW, D)), _resident((F, D))],
        out_specs=[_row_tile(tm, D), _row_tile(tm, INW), _row_tile(tm, F), _row_tile(tm, D), _row_tile(tm, D), vec, vec],
        out_shape=[jax.ShapeDtypeStruct((t, D), BF), jax.ShapeDtypeStruct((t, INW), BF), jax.ShapeDtypeStruct((t, F), BF),
                   jax.ShapeDtypeStruct((t, D), BF), jax.ShapeDtypeStruct((t, D), F32), jax.ShapeDtypeStruct((1, D), F32),
                   jax.ShapeDtypeStruct((1, D), F32)],
        compiler_params=_params(1), name=name)(x, target, g, w_in_t, w_out)


def _norm(x, g, name):
    t = x.shape[0]
    tm = min(ROW_TILE_WIDE, t)

    def body(x_ref, g_ref, n_ref):
        n_ref[...] = (_rms_stats(x_ref[...])[1] * g_ref[...]).astype(BF)

    return pl.pallas_call(
        body, grid=(t // tm,), in_specs=[_row_tile(tm, D), _resident((1, D))], out_specs=_row_tile(tm, D),
        out_shape=jax.ShapeDtypeStruct((t, D), BF), compiler_params=_params(1), name=name)(x, g)


FFN_UP_BLOCKS = 4


def _ffn_up_block(n, w_in_t, block, u, name):
    t = n.shape[0]
    tm = min(ROW_TILE_WIDE, t)
    c = INW // FFN_UP_BLOCKS
    others = [u] if u is not None else []

    def body(b_ref, n_ref, w_ref, *rest):
        rest[-1][...] = _dot_nt(n_ref[...], w_ref[...]).astype(BF)

    return pl.pallas_call(
        body,
        grid_spec=pltpu.PrefetchScalarGridSpec(
            num_scalar_prefetch=1, grid=(t // tm,),
            in_specs=[pl.BlockSpec((tm, D), lambda i, b: (i, 0)),
                      pl.BlockSpec((c, D), lambda i, b: (b[0], 0), pipeline_mode=pl.Buffered(1))] + [ANY] * len(others),
            out_specs=pl.BlockSpec((tm, c), lambda i, b: (i, b[0]))),
        out_shape=jax.ShapeDtypeStruct((t, INW), BF), input_output_aliases={3: 0} if others else {},
        compiler_params=_params(1), name=name)(block, n, w_in_t, *others)


def _ffn_down(x, u, w_out, name):
    t = x.shape[0]
    tm = min(ROW_TILE_WIDE, t)

    def body(x_ref, u_ref, wo_ref, xo_ref):
        a = u_ref[:, :F].astype(F32)
        b = u_ref[:, F:].astype(F32)
        h = (a * _sig(a) * b).astype(BF)
        xo_ref[...] = x_ref[...] + 0.5 * _dot(h, wo_ref[...])

    return pl.pallas_call(
        body, grid=(t // tm,), in_specs=[_row_tile(tm, D), _row_tile(tm, INW), _resident((F, D))],
        out_specs=_row_tile(tm, D), out_shape=jax.ShapeDtypeStruct((t, D), F32),
        compiler_params=_params(1), name=name)(x, u, w_out)


def _ffn_bwd(dxo, x, g, u, w_in_t, w_out, name, deps=()):
    t = x.shape[0]
    tm = min(ROW_TILE, t)

    def body(dxo_ref, x_ref, g_ref, u_ref, w_ref, wo_ref, dx_ref, du_ref, h_ref, dy_ref, dg_ref):
        dxo = dxo_ref[...]
        dy = (0.5 * dxo).astype(BF)
        dy_ref[...] = dy
        dh = _dot_nt(dy, wo_ref[...])
        a = u_ref[:, :F].astype(F32)
        b = u_ref[:, F:].astype(F32)
        s = _sig(a)
        sa = a * s
        h_ref[...] = (sa * b).astype(BF)
        du_ref[:, :F] = (dh * b * (s * (1.0 + a * (1.0 - s)))).astype(BF)
        du_ref[:, F:] = (dh * sa).astype(BF)
        dn = _dot(du_ref[...], w_ref[...])
        dx, dg = _rms_bwd(dn, x_ref[...], g_ref[...])
        dx_ref[...] = dxo + dx

        @pl.when(pl.program_id(0) == 0)
        def _():
            dg_ref[...] = jnp.zeros_like(dg_ref)

        dg_ref[...] += dg

    return _call(
        body, deps, (dxo, x, g, u, w_in_t, w_out), grid=(t // tm,),
        in_specs=[_row_tile(tm, D), _row_tile(tm, D), _resident((1, D)), _row_tile(tm, INW), _resident((INW, D)),
                  _resident((F, D))],
        out_specs=[_row_tile(tm, D), _row_tile(tm, INW), _row_tile(tm, F), _row_tile(tm, D),
                   pl.BlockSpec((1, D), lambda i: (0, 0))],
        out_shape=[jax.ShapeDtypeStruct((t, D), F32), jax.ShapeDtypeStruct((t, INW), BF), jax.ShapeDtypeStruct((t, F), BF),
                   jax.ShapeDtypeStruct((t, D), BF), jax.ShapeDtypeStruct((1, D), F32)],
        compiler_params=_params(1), name=name)


def _wgrad(lhs, rhs, name, *, lhs_is_transposed, chunk, deps=()):
    t = rhs.shape[0]
    n = lhs.shape[0] if lhs_is_transposed else lhs.shape[1]
    c = min(chunk, n)

    def body(l_ref, r_ref, o_ref):
        if lhs_is_transposed:
            o_ref[...] = _dot(l_ref[...], r_ref[...]).astype(BF)
        else:
            o_ref[...] = _dot_tn(l_ref[...], r_ref[...]).astype(BF)

    lhs_spec = pl.BlockSpec((c, t), lambda j: (j, 0)) if lhs_is_transposed else pl.BlockSpec((t, c), lambda j: (0, j))
    return _call(
        body, deps, (lhs, rhs), grid=(n // c,),
        in_specs=[lhs_spec, _resident((t, D))],
        out_specs=pl.BlockSpec((c, D), lambda j: (j, 0)),
        out_shape=jax.ShapeDtypeStruct((n, D), BF),
        compiler_params=_params(1), name=name)


def _wgrad_mix(duc, dq_t, dkv_t, dgp, hm):
    t = hm.shape[0]
    c = 512
    first_q, first_kv, first_gate = R_Q[0] // c, R_KV[0] // c, R_GATE[0] // c

    def body(uc_ref, q_ref, kv_ref, gp_ref, h_ref, o_ref):
        j = pl.program_id(0)

        @pl.when(j < first_q)
        def _():
            o_ref[...] = _dot_tn(uc_ref[...], h_ref[...]).astype(BF)

        @pl.when((j >= first_q) & (j < first_kv))
        def _():
            o_ref[...] = _dot(q_ref[...], h_ref[...]).astype(BF)

        @pl.when((j >= first_kv) & (j < first_gate))
        def _():
            o_ref[...] = _dot(kv_ref[...], h_ref[...]).astype(BF)

        @pl.when(j >= first_gate)
        def _():
            o_ref[...] = _dot_tn(gp_ref[...], h_ref[...]).astype(BF)

    return pl.pallas_call(
        body, grid=(INW // c,),
        in_specs=[pl.BlockSpec((t, c), lambda j: (0, jnp.clip(j, 0, first_q - 1))),
                  pl.BlockSpec((c, t), lambda j: (jnp.clip(j - first_q, 0, first_kv - first_q - 1), 0)),
                  pl.BlockSpec((c, t), lambda j: (jnp.clip(j - first_kv, 0, first_gate - first_kv - 1), 0)),
                  pl.BlockSpec((t, c), lambda j: (0, jnp.clip(j - first_gate, 0, INW // c - first_gate - 1))),
                  _resident((t, D))],
        out_specs=pl.BlockSpec((c, D), lambda j: (j, 0)),
        out_shape=jax.ShapeDtypeStruct((INW, D), BF),
        compiler_params=_params(1), name="mix_dw_in")(duc, dq_t, dkv_t, dgp, hm)


def _mix_proj(x, g, w_t):
    t = x.shape[0]
    tm = min(ROW_TILE_WIDE, t)

    def body(x_ref, g_ref, w_ref, hm_ref, uc_ref, gp_ref, qkv_ref):
        r, xh = _rms_stats(x_ref[...])
        hm = (xh * g_ref[...]).astype(BF)
        hm_ref[...] = hm
        uc_ref[...] = _dot_nt(hm, w_ref[R_CONV[0]:R_CONV[1], :]).astype(BF)
        gp_ref[...] = _dot_nt(hm, w_ref[R_GATE[0]:R_GATE[1], :]).astype(BF)
        qkv_ref[...] = _dot_nt(w_ref[R_QKV[0]:R_QKV[1], :], hm).astype(BF)

    return pl.pallas_call(
        body, grid=(t // tm,),
        in_specs=[_row_tile(tm, D), _resident((1, D)), _resident((INW, D))],
        out_specs=[_row_tile(tm, D), _row_tile(tm, 2 * D), _row_tile(tm, 2 * D), pl.BlockSpec((1536, tm), lambda i: (0, i))],
        out_shape=[jax.ShapeDtypeStruct((t, D), BF), jax.ShapeDtypeStruct((t, 2 * D), BF),
                   jax.ShapeDtypeStruct((t, 2 * D), BF), jax.ShapeDtypeStruct((1536, t), BF)],
        compiler_params=_params(1), name="mix_proj")(x, g, w_t)


CONV_HALO = 32
CONV_LEAD = CONV_HALO - (CW - 1)


def _glu(uc):
    uc = uc.astype(F32)
    return uc[:, :D] * _sig(uc[:, D:])


def _ln_stats(zc):
    mu = jnp.mean(zc, axis=-1, keepdims=True)
    zm = zc - mu
    r = lax.rsqrt(jnp.mean(zm * zm, axis=-1, keepdims=True) + EPS)
    return r, zm * r


CONV_SHIFTS = 8
CONV_CHUNK = 32


def _store_shifted(buf, rows):
    for b in range(1, CONV_SHIFTS):
        buf[b, 0:rows - 8, :] = buf[0, pl.ds(b, rows - 8), :]


def _conv_fwd(uc, dwk, dwb, lng, lnb):
    t = uc.shape[0]
    tm = min(512, t)
    per = tm // CONV_HALO
    ext = tm + CONV_HALO

    def body(cur_ref, prev_ref, k_ref, kb_ref, g_ref, b_ref, o_ref, zc_ref, zsh):
        i = pl.program_id(0)
        zsh[0, 0:CONV_HALO, :] = _glu(prev_ref[...]) * (i > 0).astype(F32)
        zsh[0, CONV_HALO:, :] = _glu(cur_ref[...])
        _store_shifted(zsh, ext)

        def chunk(ci, carry):
            r0 = pl.multiple_of(ci * CONV_CHUNK, CONV_CHUNK)
            acc = jnp.zeros((CONV_CHUNK, D), F32) + kb_ref[...]
            for w in range(CW):
                a, b = divmod(CONV_LEAD + w, 8)
                acc = acc + k_ref[w:w + 1, :] * zsh[b, pl.ds(r0 + 8 * a, CONV_CHUNK), :]
            zc_ref[pl.ds(r0, CONV_CHUNK), :] = acc
            return carry

        lax.fori_loop(0, tm // CONV_CHUNK, chunk, 0)
        r, xh = _ln_stats(zc_ref[...])
        y = xh * g_ref[...] + b_ref[...]
        o_ref[...] = (y * _sig(y)).astype(BF)

    return pl.pallas_call(
        body, grid=(t // tm,),
        in_specs=[_row_tile(tm, 2 * D),
                  pl.BlockSpec((CONV_HALO, 2 * D), lambda i: (jnp.maximum(i * per - 1, 0), 0)),
                  _resident((CWP, D)), _resident((1, D)), _resident((1, D)), _resident((1, D))],
        out_specs=[_row_tile(tm, D), _row_tile(tm, D)],
        out_shape=[jax.ShapeDtypeStruct((t, D), BF), jax.ShapeDtypeStruct((t, D), F32)],
        scratch_shapes=[pltpu.VMEM((CONV_SHIFTS, ext, D), F32)],
        compiler_params=_params(1), name="conv_fwd")(uc, uc, dwk, dwb, lng, lnb)


def _conv_bwd(uc, zc, dzs, dwk, lng, lnb):
    t = uc.shape[0]
    tm = min(ROW_TILE_WIDE, t)
    per = tm // CONV_HALO
    n_tiles = t // tm
    ext = tm + CONV_HALO
    last_block = t // CONV_HALO - 1

    def body(cur_ref, zc_ref, zcn_ref, dz_ref, dzn_ref, k_ref, g_ref, b_ref,
             duc_ref, dk_ref, dkb_ref, dg_ref, db_ref, dsh, dk8, z_scr):
        i = pl.program_id(0)

        @pl.when(i == 0)
        def _():
            dk8[...] = jnp.zeros_like(dk8)
            dkb_ref[...] = jnp.zeros_like(dkb_ref)
            dg_ref[...] = jnp.zeros_like(dg_ref)
            db_ref[...] = jnp.zeros_like(db_ref)

        has_next = (i < n_tiles - 1).astype(F32)
        z_scr[...] = _glu(cur_ref[...])
        gain = g_ref[...]

        def ln_silu_bwd(zc, dzs, live):
            r, xh = _ln_stats(zc)
            y = xh * gain + b_ref[...]
            sy = _sig(y)
            dy = dzs * (sy * (1.0 + y * (1.0 - sy))) * live
            dxh = dy * gain
            dzc = r * (dxh - jnp.mean(dxh, axis=-1, keepdims=True) - xh * jnp.mean(dxh * xh, axis=-1, keepdims=True))
            return dzc, dy, xh

        dzc, dy, xh = ln_silu_bwd(zc_ref[...], dz_ref[...], 1.0)
        dsh[0, 0:tm, :] = dzc
        dg_ref[...] += jnp.sum(dy * xh, axis=0, keepdims=True)
        db_ref[...] += jnp.sum(dy, axis=0, keepdims=True)
        dkb_ref[...] += jnp.sum(dzc, axis=0, keepdims=True)
        dsh[0, tm:, :] = ln_silu_bwd(zcn_ref[...], dzn_ref[...], has_next)[0]
        _store_shifted(dsh, ext)

        def chunk(ci, carry):
            r0 = pl.multiple_of(ci * CONV_CHUNK, CONV_CHUNK)
            z_c = z_scr[pl.ds(r0, CONV_CHUNK), :]
            dz = jnp.zeros((CONV_CHUNK, D), F32)
            for w in range(CW):
                a, b = divmod(CW - 1 - w, 8)
                window = dsh[b, pl.ds(r0 + 8 * a, CONV_CHUNK), :]
                dz = dz + k_ref[w:w + 1, :] * window
                prod = z_c * window
                part = prod[0:8, :]
                for j in range(1, CONV_CHUNK // 8):
                    part = part + prod[8 * j:8 * j + 8, :]
                dk8[w] += part
            ucc = cur_ref[pl.ds(r0, CONV_CHUNK), :].astype(F32)
            sg = _sig(ucc[:, D:])
            duc_ref[pl.ds(r0, CONV_CHUNK), 0:D] = (dz * sg).astype(BF)
            duc_ref[pl.ds(r0, CONV_CHUNK), D:2 * D] = (dz * ucc[:, :D] * sg * (1.0 - sg)).astype(BF)
            return carry

        lax.fori_loop(0, tm // CONV_CHUNK, chunk, 0)

        @pl.when(i == n_tiles - 1)
        def _():
            dk_ref[...] = jnp.sum(dk8[...], axis=1)

    vec = pl.BlockSpec((1, D), lambda i: (0, 0))
    next_halo = pl.BlockSpec((CONV_HALO, D), lambda i: (jnp.minimum((i + 1) * per, last_block), 0))
    return pl.pallas_call(
        body, grid=(n_tiles,),
        in_specs=[_row_tile(tm, 2 * D), _row_tile(tm, D), next_halo, _row_tile(tm, D), next_halo,
                  _resident((CWP, D)), _resident((1, D)), _resident((1, D))],
        out_specs=[_row_tile(tm, 2 * D), pl.BlockSpec((CWP, D), lambda i: (0, 0)), vec, vec, vec],
        out_shape=[jax.ShapeDtypeStruct((t, 2 * D), BF), jax.ShapeDtypeStruct((CWP, D), F32),
                   jax.ShapeDtypeStruct((1, D), F32), jax.ShapeDtypeStruct((1, D), F32), jax.ShapeDtypeStruct((1, D), F32)],
        scratch_shapes=[pltpu.VMEM((CONV_SHIFTS, ext, D), F32), pltpu.VMEM((CWP, 8, D), F32), pltpu.VMEM((tm, D), F32)],
        compiler_params=_params(1), name="conv_bwd")(uc, zc, zc, dzs, dzs, dwk, lng, lnb)


def _norm_rows(xt, g):
    r = lax.rsqrt(jnp.mean(xt * xt, axis=0, keepdims=True) + EPS)
    xh = xt * r
    return xh * g, r, xh


ATT_TQ = 1024


def _attn_specs(t, tq):
    per = tq // BLK
    return [pl.BlockSpec((1536, tq), lambda i: (0, i)),
            pl.BlockSpec((512, BLK), lambda i: (2, jnp.maximum(i * per - 1, 0))),
            _resident((HD, 1)), _resident((HD, 1)), _resident((NKV, 1, GRP * BLK)),
            _resident((2, NKV, 2 * BLK, GRP * BLK))]


def _attn_window(hk, sb, qkv_ref, halo_ref, kn_cur, kn_halo):
    v0 = D + NKV * HD + hk * HD
    if sb == 0:
        k_prev = kn_halo[hk]
        v_prev = halo_ref[NKV * HD + hk * HD:NKV * HD + (hk + 1) * HD, :]
    else:
        k_prev = kn_cur[hk][:, (sb - 1) * BLK:sb * BLK]
        v_prev = qkv_ref[v0:v0 + HD, (sb - 1) * BLK:sb * BLK]
    kw = jnp.concatenate([k_prev, kn_cur[hk][:, sb * BLK:(sb + 1) * BLK]], axis=1).astype(BF)
    vw = jnp.concatenate([v_prev, qkv_ref[v0:v0 + HD, sb * BLK:(sb + 1) * BLK]], axis=1)
    return kw, vw


def _attn_probs(kw, qc, bias, sink):
    st = _dot_tn(kw, qc) + bias
    m = jnp.maximum(jnp.max(st, axis=0, keepdims=True), sink)
    p = jnp.exp(st - m)
    e_sink = jnp.exp(sink - m)
    inv = 1.0 / (jnp.sum(p, axis=0, keepdims=True) + e_sink)
    return p * inv, e_sink * inv


def _attn_fwd(qkv_t, qg, kg, sink_rows, bias_t):
    t = qkv_t.shape[1]
    tq = min(ATT_TQ, t)
    n_sub = tq // BLK

    def body(qkv_ref, halo_ref, qg_ref, kg_ref, sink_ref, bias_ref, o_ref, p_ref, ps_ref):
        i = pl.program_id(0)
        first = (i == 0).astype(jnp.int32)
        kgain = kg_ref[...]
        qgain = qg_ref[...]
        kn_cur = [_norm_rows(qkv_ref[D + h * HD:D + (h + 1) * HD, :].astype(F32), kgain)[0] for h in range(NKV)]
        kn_halo = [_norm_rows(halo_ref[h * HD:(h + 1) * HD, :].astype(F32), kgain)[0] for h in range(NKV)]
        for hk in range(NKV):
            for sb in range(n_sub):
                cols = slice(sb * BLK, (sb + 1) * BLK)
                kw, vw = _attn_window(hk, sb, qkv_ref, halo_ref, kn_cur, kn_halo)
                qc = jnp.concatenate(
                    [_norm_rows(qkv_ref[(GRP * hk + g) * HD:(GRP * hk + g + 1) * HD, cols].astype(F32), qgain)[0] * QK_SCALE
                     for g in range(GRP)], axis=1).astype(BF)
                bias = bias_ref[first, hk] if sb == 0 else bias_ref[0, hk]
                p, p_sink = _attn_probs(kw, qc, bias, sink_ref[hk])
                p = p.astype(BF)
                p_ref[sb, hk] = p
                ps_ref[sb, hk] = p_sink
                o = _dot(vw, p)
                for g in range(GRP):
                    head = GRP * hk + g
                    o_ref[head * HD:(head + 1) * HD, cols] = o[:, g * BLK:(g + 1) * BLK].astype(BF)

    return pl.pallas_call(
        body, grid=(t // tq,),
        in_specs=_attn_specs(t, tq),
        out_specs=[pl.BlockSpec((D, tq), lambda i: (0, i)),
                   pl.BlockSpec((n_sub, NKV, 2 * BLK, GRP * BLK), lambda i: (i, 0, 0, 0)),
                   pl.BlockSpec((n_sub, NKV, 1, GRP * BLK), lambda i: (i, 0, 0, 0))],
        out_shape=[jax.ShapeDtypeStruct((D, t), BF), jax.ShapeDtypeStruct((t // BLK, NKV, 2 * BLK, GRP * BLK), BF),
                   jax.ShapeDtypeStruct((t // BLK, NKV, 1, GRP * BLK), F32)],
        compiler_params=_params(1), name="attn_fwd")(qkv_t, qkv_t, qg, kg, sink_rows, bias_t)


def _attn_bwd(qkv_t, do_t, probs, sink_probs, qg, kg, onehot_t, deps=()):
    t = qkv_t.shape[1]
    tq = min(ATT_TQ, t)
    n_sub = tq // BLK
    n_tiles = t // tq

    def body(qkv_ref, halo_ref, do_ref, p_ref, ps_ref, qg_ref, kg_ref, oh_ref,
             dq_ref, ckv_ref, dqg_ref, dsink_ref, dbias_ref, qg_scr, sink_scr, ds_scr):
        i = pl.program_id(0)

        @pl.when(i == 0)
        def _():
            qg_scr[...] = jnp.zeros_like(qg_scr)
            sink_scr[...] = jnp.zeros_like(sink_scr)
            ds_scr[...] = jnp.zeros_like(ds_scr)

        kgain = kg_ref[...]
        qgain = qg_ref[...]
        kn_cur = [_norm_rows(qkv_ref[D + h * HD:D + (h + 1) * HD, :].astype(F32), kgain)[0] for h in range(NKV)]
        kn_halo = [_norm_rows(halo_ref[h * HD:(h + 1) * HD, :].astype(F32), kgain)[0] for h in range(NKV)]
        dqg = jnp.zeros((HD, BLK), F32)
        for hk in range(NKV):
            for sb in range(n_sub):
                cols = slice(sb * BLK, (sb + 1) * BLK)
                kw, vw = _attn_window(hk, sb, qkv_ref, halo_ref, kn_cur, kn_halo)
                qn, qr, qh = [], [], []
                for g in range(GRP):
                    head = GRP * hk + g
                    n_, r_, h_ = _norm_rows(qkv_ref[head * HD:(head + 1) * HD, cols].astype(F32), qgain)
                    qn.append(n_)
                    qr.append(r_)
                    qh.append(h_)
                qc = (jnp.concatenate(qn, axis=1) * QK_SCALE).astype(BF)
                p_bf = p_ref[sb, hk]
                p = p_bf.astype(F32)
                doc = jnp.concatenate([do_ref[(GRP * hk + g) * HD:(GRP * hk + g + 1) * HD, cols] for g in range(GRP)], axis=1)
                dp = _dot_tn(vw, doc)
                delta = jnp.sum(p * dp, axis=0, keepdims=True)
                ds = p * (dp - delta)
                sink_scr[hk] += -(ps_ref[sb, hk] * delta)
                ds_scr[hk] += ds
                dsb = ds.astype(BF)
                dqc = _dot(kw, dsb) * QK_SCALE
                ckv_ref[sb, hk * HD:(hk + 1) * HD, :] = _dot_nt(qc, dsb)
                ckv_ref[sb, NKV * HD + hk * HD:NKV * HD + (hk + 1) * HD, :] = _dot_nt(doc, p_bf)
                for g in range(GRP):
                    head = GRP * hk + g
                    dqn = dqc[:, g * BLK:(g + 1) * BLK]
                    dqh = dqn * qgain
                    dq = qr[g] * (dqh - qh[g] * jnp.mean(dqh * qh[g], axis=0, keepdims=True))
                    dq_ref[head * HD:(head + 1) * HD, cols] = dq.astype(BF)
                    dqg = dqg + dqn * qh[g]
        qg_scr[...] += dqg

        @pl.when(i == n_tiles - 1)
        def _():
            dqg_ref[...] = jnp.sum(qg_scr[...], axis=1, keepdims=True)
            dsink_ref[...] = _group_lane_sums(sink_scr[:, 0, :])

            def bucket(b, carry):
                oh = jnp.concatenate([oh_ref[b]] * GRP, axis=1)
                dbias_ref[b] = _group_lane_sums(jnp.sum(ds_scr[...] * oh[None], axis=1))
                return carry

            lax.fori_loop(0, NBUCKET, bucket, 0)

    return _call(
        body, deps, (qkv_t, qkv_t, do_t, probs, sink_probs, qg, kg, onehot_t), grid=(n_tiles,),
        in_specs=_attn_specs(t, tq)[:2] + [pl.BlockSpec((D, tq), lambda i: (0, i)),
                                           pl.BlockSpec((n_sub, NKV, 2 * BLK, GRP * BLK), lambda i: (i, 0, 0, 0)),
                                           pl.BlockSpec((n_sub, NKV, 1, GRP * BLK), lambda i: (i, 0, 0, 0))]
        + _attn_specs(t, tq)[2:4] + [_resident((NBUCKET, 2 * BLK, BLK))],
        out_specs=[pl.BlockSpec((D, tq), lambda i: (0, i)),
                   pl.BlockSpec((n_sub, 2 * NKV * HD, 2 * BLK), lambda i: (i, 0, 0)),
                   pl.BlockSpec((HD, 1), lambda i: (0, 0)),
                   pl.BlockSpec((NKV, BLK), lambda i: (0, 0)),
                   pl.BlockSpec((NBUCKET, NKV, BLK), lambda i: (0, 0, 0))],
        out_shape=[jax.ShapeDtypeStruct((D, t), BF),
                   jax.ShapeDtypeStruct((t // BLK, 2 * NKV * HD, 2 * BLK), F32),
                   jax.ShapeDtypeStruct((HD, 1), F32),
                   jax.ShapeDtypeStruct((NKV, BLK), F32),
                   jax.ShapeDtypeStruct((NBUCKET, NKV, BLK), F32)],
        scratch_shapes=[pltpu.VMEM((HD, BLK), F32), pltpu.VMEM((NKV, 1, GRP * BLK), F32),
                        pltpu.VMEM((NKV, 2 * BLK, GRP * BLK), F32)],
        compiler_params=_params(1), name="attn_bwd")


def _kv_combine_tile(c_ref, cn_ref, has_next, k_ref, kgain, o_ref):
    rows = NKV * HD
    per = c_ref.shape[0]
    dkg = jnp.zeros((HD, BLK), F32)
    for s in range(per):
        cols = slice(s * BLK, (s + 1) * BLK)
        after = c_ref[s + 1, :, :BLK] if s + 1 < per else cn_ref[0, :, :BLK] * has_next
        d = c_ref[s, :, BLK:] + after
        o_ref[rows:, cols] = d[rows:, :].astype(BF)
        for h in range(NKV):
            _, r, kh = _norm_rows(k_ref[h * HD:(h + 1) * HD, cols].astype(F32), kgain)
            dkn = d[h * HD:(h + 1) * HD, :]
            dkh = dkn * kgain
            o_ref[h * HD:(h + 1) * HD, cols] = (r * (dkh - kh * jnp.mean(dkh * kh, axis=0, keepdims=True))).astype(BF)
            dkg = dkg + dkn * kh
    return dkg


def _group_lane_sums(v):
    lane_group = lax.broadcasted_iota(jnp.int32, (1, GRP * BLK), 1) // BLK
    col = lax.broadcasted_iota(jnp.int32, (1, BLK), 1)
    out = jnp.zeros((NKV, BLK), F32)
    for g in range(GRP):
        s = jnp.sum(jnp.where(lane_group == g, v, 0.0), axis=1, keepdims=True)
        out = jnp.where(col == g, s, out)
    return out


def _mix_out(zs, o_t, gp, x, w_cp, w_o, w_out):
    t = x.shape[0]
    tm = min(ROW_TILE_WIDE, t)

    def body(zs_ref, ot_ref, gp_ref, x_ref, wcp_ref, wo_ref, wout_ref, xo_ref, a_ref, b_ref, m_ref):
        a = _dot(zs_ref[...], wcp_ref[...])
        b = _dot_tn(ot_ref[...], wo_ref[...])
        a_ref[...] = a.astype(BF)
        b_ref[...] = b.astype(BF)
        merged = (_sig(gp_ref[:, :D].astype(F32)) * a + _sig(gp_ref[:, D:].astype(F32)) * b).astype(BF)
        m_ref[...] = merged
        xo_ref[...] = x_ref[...] + _dot(merged, wout_ref[...])

    return pl.pallas_call(
        body, grid=(t // tm,),
        in_specs=[_row_tile(tm, D), pl.BlockSpec((D, tm), lambda i: (0, i)), _row_tile(tm, 2 * D), _row_tile(tm, D),
                  _resident((D, D)), _resident((D, D)), _resident((D, D))],
        out_specs=[_row_tile(tm, D)] * 4,
        out_shape=[jax.ShapeDtypeStruct((t, D), F32)] + [jax.ShapeDtypeStruct((t, D), BF)] * 3,
        compiler_params=_params(1), name="mix_out")(zs, o_t, gp, x, w_cp, w_o, w_out)


def _mix_out_bwd(dx, a, b, gp, w_cp, w_o, w_out, deps=()):
    t = dx.shape[0]
    tm = min(ROW_TILE_WIDE, t)

    def body(dx_ref, a_ref, b_ref, gp_ref, wcp_ref, wo_ref, wout_ref, dzs_ref, dot_ref, dgp_ref, da_ref, db_ref, dxb_ref):
        dxb = dx_ref[...].astype(BF)
        dxb_ref[...] = dxb
        dm = _dot_nt(dxb, wout_ref[...])
        gc = _sig(gp_ref[:, :D].astype(F32))
        ga = _sig(gp_ref[:, D:].astype(F32))
        da = (dm * gc).astype(BF)
        db = (dm * ga).astype(BF)
        da_ref[...] = da
        db_ref[...] = db
        dgp_ref[:, :D] = (dm * a_ref[...].astype(F32) * gc * (1.0 - gc)).astype(BF)
        dgp_ref[:, D:] = (dm * b_ref[...].astype(F32) * ga * (1.0 - ga)).astype(BF)
        dzs_ref[...] = _dot_nt(da, wcp_ref[...])
        dot_ref[...] = _dot_nt(wo_ref[...], db).astype(BF)

    return _call(
        body, deps, (dx, a, b, gp, w_cp, w_o, w_out), grid=(t // tm,),
        in_specs=[_row_tile(tm, D), _row_tile(tm, D), _row_tile(tm, D), _row_tile(tm, 2 * D),
                  _resident((D, D)), _resident((D, D)), _resident((D, D))],
        out_specs=[_row_tile(tm, D), pl.BlockSpec((D, tm), lambda i: (0, i)), _row_tile(tm, 2 * D),
                   _row_tile(tm, D), _row_tile(tm, D), _row_tile(tm, D)],
        out_shape=[jax.ShapeDtypeStruct((t, D), F32), jax.ShapeDtypeStruct((D, t), BF), jax.ShapeDtypeStruct((t, 2 * D), BF),
                   jax.ShapeDtypeStruct((t, D), BF), jax.ShapeDtypeStruct((t, D), BF), jax.ShapeDtypeStruct((t, D), BF)],
        compiler_params=_params(1), name="mix_out_bwd")


def _mix_proj_bwd(dxo, duc, dq_t, ckv, qkv_t, kg, dgp, x, g, w_t):
    t = x.shape[0]
    tm = min(ROW_TILE_WIDE, t)
    per = tm // BLK
    steps = t // tm
    kv_rows = 2 * NKV * HD

    def body(dxo_ref, duc_ref, dq_ref, c_ref, cn_ref, k_ref, kg_ref, dgp_ref, x_ref, g_ref, w_ref,
             dx_ref, dg_ref, dkv_ref, dkg_ref, kg_scr):
        i = pl.program_id(0)

        @pl.when(i == 0)
        def _():
            dg_ref[...] = jnp.zeros_like(dg_ref)
            kg_scr[...] = jnp.zeros_like(kg_scr)

        kg_scr[...] += _kv_combine_tile(c_ref, cn_ref, (i < steps - 1).astype(F32), k_ref, kg_ref[...], dkv_ref)
        dn = _dot(duc_ref[...], w_ref[R_CONV[0]:R_CONV[1], :])
        dn = dn + _dot(dgp_ref[...], w_ref[R_GATE[0]:R_GATE[1], :])
        dn = dn + _dot_tn(dq_ref[...], w_ref[R_Q[0]:R_Q[1], :])
        dn = dn + _dot_tn(dkv_ref[...], w_ref[R_KV[0]:R_KV[1], :])
        dx, dg = _rms_bwd(dn, x_ref[...], g_ref[...])
        dx_ref[...] = dxo_ref[...] + dx
        dg_ref[...] += dg

        @pl.when(i == steps - 1)
        def _():
            dkg_ref[...] = jnp.sum(kg_scr[...], axis=1, keepdims=True)

    return pl.pallas_call(
        body, grid=(steps,),
        in_specs=[_row_tile(tm, D), _row_tile(tm, 2 * D), pl.BlockSpec((D, tm), lambda i: (0, i)),
                  pl.BlockSpec((per, kv_rows, 2 * BLK), lambda i: (i, 0, 0)),
                  pl.BlockSpec((1, kv_rows, 2 * BLK), lambda i: (jnp.minimum((i + 1) * per, t // BLK - 1), 0, 0)),
                  pl.BlockSpec((NKV * HD, tm), lambda i: (D // (NKV * HD), i)), _resident((HD, 1)),
                  _row_tile(tm, 2 * D), _row_tile(tm, D), _resident((1, D)), _resident((INW, D))],
        out_specs=[_row_tile(tm, D), pl.BlockSpec((1, D), lambda i: (0, 0)), pl.BlockSpec((kv_rows, tm), lambda i: (0, i)),
                   pl.BlockSpec((HD, 1), lambda i: (0, 0))],
        out_shape=[jax.ShapeDtypeStruct((t, D), F32), jax.ShapeDtypeStruct((1, D), F32),
                   jax.ShapeDtypeStruct((kv_rows, t), BF), jax.ShapeDtypeStruct((HD, 1), F32)],
        scratch_shapes=[pltpu.VMEM((HD, BLK), F32)],
        compiler_params=_params(1), name="mix_proj_bwd")(dxo, duc, dq_t, ckv, ckv, qkv_t, kg, dgp, x, g, w_t)


def _attention_tables():
    kj = np.arange(2 * BLK)[:, None]
    qi = np.arange(BLK)[None, :]
    dist = qi + BLK - kj
    in_win = (dist >= 0) & (dist < BLK)
    dpos = np.maximum(dist, 0)
    max_exact = NBUCKET // 2
    dfl = np.maximum(dpos, 1).astype(np.float32)
    large = max_exact + (np.log(dfl / np.float32(max_exact)) / np.float32(math.log(BLK / max_exact))
                         * np.float32(NBUCKET - max_exact)).astype(np.int32)
    large = np.minimum(large, NBUCKET - 1)
    bucket = np.where(dpos < max_exact, dpos, large)
    onehot = (bucket[None] == np.arange(NBUCKET)[:, None, None]).astype(np.float32)
    mask = in_win.astype(np.float32)
    mask_first = mask * (kj >= BLK)
    masks = np.stack([np.tile(mask, (1, GRP)), np.tile(mask_first, (1, GRP))])
    return onehot, masks


def _bias_table(rel_bias, onehot):
    tab = jnp.einsum("bkq,bh->hkq", onehot, rel_bias, precision=lax.Precision.HIGHEST)
    tab = tab.reshape(NKV, GRP, 2 * BLK, BLK)
    return jnp.transpose(tab, (0, 2, 1, 3)).reshape(NKV, 2 * BLK, GRP * BLK)


def _local_step(x, target, vec, first_weight_blocks, weights_of, wgrad, grads_done, small_done):
    onehot_np, masks_np = _attention_tables()
    onehot = jnp.asarray(onehot_np)
    masks = jnp.asarray(masks_np)
    bias_t = jnp.where(masks[:, None] > 0.5, _bias_table(vec["rel_bias"], onehot)[None], NEG)
    sink_rows = jnp.repeat(vec["attn_sinks"].reshape(NKV, 1, GRP), BLK, axis=2)
    qg = vec["q_norm"].reshape(HD, 1)
    kg = vec["k_norm"].reshape(HD, 1)
    g1 = vec["ffn1_norm"].reshape(1, D)
    gm = vec["mix_norm"].reshape(1, D)
    g2 = vec["ffn2_norm"].reshape(1, D)
    dwb = vec["conv_dw_bias"].reshape(1, D)
    lng = vec["conv_ln_g"].reshape(1, D)
    lnb = vec["conv_ln_b"].reshape(1, D)

    n1 = _norm(x, g1, "ffn1_norm")
    u1 = None
    for j, next_block in enumerate(first_weight_blocks((bias_t, sink_rows))):
        w_in1, block = next_block(u1 if u1 is not None else n1)
        u1 = _ffn_up_block(n1, w_in1, block, u1, "ffn1_up_%d" % j)
    w1 = dict(weights_of("ffn1_out", (u1,)), ffn1_w_in=w_in1)
    x1 = _ffn_down(x, u1, w1["ffn1_w_out"], "ffn1_down")
    wm = weights_of("mix_proj", (x1,))
    dwk = jnp.pad(wm["conv_dw_kernel"], ((0, CWP - CW), (0, 0)))
    hm, uc, gp, qkv_t = _mix_proj(x1, gm, wm["w_in"])
    zs, zc = _conv_fwd(uc, dwk, dwb, lng, lnb)
    wm.update(weights_of("mix_merge", (zs,)))
    o_t, probs, sink_probs = _attn_fwd(qkv_t, qg, kg, sink_rows, bias_t)
    x2, a, b, merged = _mix_out(zs, o_t, gp, x1, wm["conv_w_proj"], wm["attn_w_o"], wm["w_out"])
    w2 = weights_of("ffn2", (x2,))
    gv = {}
    n2, du2, h2, dy2, dx2, sq, gv["ffn2_norm"] = _ffn_last(x2, target, g2, w2["ffn2_w_in"], w2["ffn2_w_out"], "ffn2")

    deps = grads_done("ffn2", {"ffn2_w_in": wgrad(du2, n2, "ffn2_dw_in", False),
                               "ffn2_w_out": wgrad(h2, dy2, "ffn2_dw_out", False)})

    dzs, do_t, dgp, da, db, dx2b = _mix_out_bwd(dx2, a, b, gp, wm["conv_w_proj"], wm["attn_w_o"], wm["w_out"], deps=deps)
    deps = grads_done("mix_out", {"w_out": wgrad(merged, dx2b, "mix_dw_out", False),
                                  "conv_w_proj": wgrad(zs, da, "mix_dw_cp", False),
                                  "attn_w_o": wgrad(o_t, db, "mix_dw_o", True)})

    dq_t, ckv, dqg, dsink, dbias = _attn_bwd(qkv_t, do_t, probs, sink_probs, qg, kg, onehot, deps=deps)
    gv["q_norm"] = dqg.reshape(HD)
    gv["attn_sinks"] = dsink[:, :GRP].reshape(NQ)
    gv["rel_bias"] = dbias[:, :, :GRP].reshape(NBUCKET, NQ)

    duc, dk_conv, gv["conv_dw_bias"], gv["conv_ln_g"], gv["conv_ln_b"] = _conv_bwd(uc, zc, dzs, dwk, lng, lnb)
    gv["conv_dw_kernel"] = dk_conv[:CW]

    dx1, gv["mix_norm"], dkv_t, dkg = _mix_proj_bwd(dx2, duc, dq_t, ckv, qkv_t, kg, dgp, x1, gm, wm["w_in"])
    gv["k_norm"] = dkg.reshape(HD)
    deps = grads_done("mix_in", {"w_in": _wgrad_mix(duc, dq_t, dkv_t, dgp, hm)})

    dx0, du1, h1, dy1, gv["ffn1_norm"] = _ffn_bwd(dx1, x, g1, u1, w1["ffn1_w_in"], w1["ffn1_w_out"], "ffn1_bwd", deps=deps)
    for k in ("ffn1_norm", "mix_norm", "ffn2_norm", "conv_dw_bias", "conv_ln_g", "conv_ln_b"):
        gv[k] = gv[k].reshape(D)
    deps = small_done(gv, sq)
    deps = grads_done("ffn1_in", {"ffn1_w_in": wgrad(du1, n1, "ffn1_dw_in", False, deps)})
    grads_done("ffn1_out", {"ffn1_w_out": wgrad(h1, dy1, "ffn1_dw_out", False, deps)})
    return dx0


MESH_ID = pl.DeviceIdType.MESH


def _position():
    return lax.axis_index("x"), lax.axis_index("y"), lax.axis_index("c")


def _shard_rows(ref, index, rows):
    return ref.at[pl.ds(pl.multiple_of(index * rows, 16), rows), :]


def _prep(weights, taps, me, name, deps=()):
    n = len(weights)
    n_deps = len(deps)
    with_taps = taps is not None

    def body(me_ref, *refs):
        refs = refs[n_deps:]
        ins, outs = refs[:len(refs) // 2], refs[len(refs) // 2:]
        for k in range(n):
            outs[k][...] = ins[k][...].astype(BF)
        if with_taps:
            outs[n][0:CW, :] = ins[n][...]
            outs[n][CW:, :] = jnp.zeros((CWP - CW, BLK), F32)

    shard_shapes = [w.shape for w in weights] + [(CWP, BLK)] * with_taps
    dtypes = [BF] * n + [F32] * with_taps
    ins = list(weights) + [taps] * with_taps
    return pl.pallas_call(
        body,
        grid_spec=pltpu.PrefetchScalarGridSpec(
            num_scalar_prefetch=1, grid=(1,),
            in_specs=[ANY] * n_deps + [pl.BlockSpec(a.shape, lambda i, m: (0, 0), pipeline_mode=pl.Buffered(1)) for a in ins],
            out_specs=[pl.BlockSpec(s, lambda i, m: (m[0], 0)) for s in shard_shapes]),
        out_shape=[jax.ShapeDtypeStruct((N_DEV * s[0], s[1]), d) for s, d in zip(shard_shapes, dtypes)],
        compiler_params=_params(1), name=name)(me, *deps, *ins)


HBM = pl.BlockSpec(memory_space=pltpu.HBM)
SEM = pl.BlockSpec(memory_space=pltpu.SEMAPHORE)
DATAFLOW = pltpu.SideEffectType.DATAFLOW_SIDE_EFFECTING
TOKEN = jax.ShapeDtypeStruct((8, 128), F32)


def _in_hbm(x):
    return pltpu.with_memory_space_constraint(x, pltpu.HBM)


def _hbm_like(arrays):
    return [pltpu.HBM(a.shape, a.dtype) for a in arrays]


def _other_chips(x, y):
    return [(1 - x, y), (x, 1 - y), (1 - x, 1 - y)]


def _device_index(chip, c):
    return 4 * chip[0] + 2 * chip[1] + c


def _chip_index(chip):
    return 2 * chip[0] + chip[1]


class _Exchange:
    def __init__(self, gather, all_cores=False, only=None):
        self.gather = gather
        self.all_cores = all_cores
        self.only = only
        self.n_peers = 1 if only is not None else N_DEV - 1 if all_cores else 3

    def peers(self, x, y, c):
        if self.all_cores:
            every = [(x ^ (k >> 2), y ^ ((k >> 1) & 1), c ^ (k & 1)) for k in range(1, N_DEV)]
        else:
            every = [(*chip, c) for chip in _other_chips(x, y)]
        return every if self.only is None else [every[self.only]]

    def sent(self, x, y, c, peer):
        return _device_index((x, y), c) if self.gather else _chip_index(peer[:2])

    def lands_at(self, x, y, c):
        return _device_index((x, y), c) if self.gather else _chip_index((x, y))

    def arrives_at(self, peer):
        return _device_index(peer[:2], peer[2]) if self.gather else _chip_index(peer[:2])


def _ici_copies_start(sets, sources, landings, exchanges, name, deps=()):
    n = len(landings)
    arrays = (list(sources) if sources is not None else []) + list(landings)
    first_land = len(arrays) - n
    n_sets = len(sets)
    n_deps = len(deps)

    def body(*refs):
        refs = refs[n_deps:]
        src, land = refs[:n], refs[first_land:first_land + n]
        sems = refs[len(arrays):len(arrays) + 2 * n_sets]
        token = refs[-1]
        x, y, c = _position()
        for s, (members, exchange) in enumerate(zip(sets, exchanges)):
            for slot, (k, rows) in enumerate(members):
                for j, peer in enumerate(exchange.peers(x, y, c)):
                    at = exchange.n_peers * slot + j
                    pltpu.make_async_remote_copy(
                        src_ref=_shard_rows(src[k], exchange.sent(x, y, c, peer), rows),
                        dst_ref=_shard_rows(land[k], exchange.lands_at(x, y, c), rows),
                        send_sem=sems[2 * s].at[at], recv_sem=sems[2 * s + 1].at[at],
                        device_id=peer, device_id_type=MESH_ID).start()
        token[...] = jnp.zeros_like(token)

    sem_shapes = []
    for members, exchange in zip(sets, exchanges):
        sem_shapes += [pltpu.SemaphoreType.DMA((exchange.n_peers * len(members),))] * 2
    out = pl.pallas_call(
        body, name=name,
        out_shape=sem_shapes + _hbm_like(arrays) + [TOKEN],
        in_specs=[ANY] * n_deps + [HBM] * len(arrays),
        out_specs=[SEM] * (2 * n_sets) + [HBM] * len(arrays) + [pl.BlockSpec(memory_space=pltpu.VMEM)],
        input_output_aliases={n_deps + i: 2 * n_sets + i for i in range(len(arrays))},
        compiler_params=pltpu.CompilerParams(has_side_effects=DATAFLOW),
    )(*deps, *[_in_hbm(a) for a in arrays])
    sems = [(out[2 * s], out[2 * s + 1]) for s in range(n_sets)]
    thru = list(out[2 * n_sets:2 * n_sets + len(arrays)])
    return sems, (thru[:first_land] if sources is not None else None), thru[first_land:], out[-1]


def _ici_copies_wait(sems, members, sources, landings, exchange, after, name):
    n = len(landings)
    arrays = (list(sources) if sources is not None else []) + list(landings)
    first_land = len(arrays) - n

    def body(*refs):
        src, land = refs[:n], refs[first_land:first_land + n]
        send_sems, recv_sems = refs[len(arrays)], refs[len(arrays) + 1]
        x, y, c = _position()
        for slot, rows in enumerate(members):
            for j, peer in enumerate(exchange.peers(x, y, c)):
                at = exchange.n_peers * slot + j
                cp = pltpu.make_async_remote_copy(
                    src_ref=_shard_rows(src[slot], exchange.sent(x, y, c, peer), rows),
                    dst_ref=_shard_rows(land[slot], exchange.arrives_at(peer), rows),
                    send_sem=send_sems.at[at], recv_sem=recv_sems.at[at], device_id=peer, device_id_type=MESH_ID)
                cp.wait_send()
                cp.wait_recv()

    out = pl.pallas_call(
        body, name=name, out_shape=_hbm_like(arrays),
        in_specs=[HBM] * len(arrays) + [SEM, SEM] + [ANY] * len(after), out_specs=[HBM] * len(arrays),
        input_output_aliases={i: i for i in range(len(arrays))},
        compiler_params=pltpu.CompilerParams(has_side_effects=DATAFLOW),
    )(*arrays, sems[0], sems[1], *after)
    return list(out[first_land:])


def _d2d_gather(buffers, rows, name, which=None):
    n = len(buffers)
    n_chips = N_DEV // 2 if which is None else 1

    def body(*refs):
        land = refs[n:2 * n]
        send_sems, recv_sems = refs[2 * n:]
        x, y, c = _position()
        chips = [(x, y)] + _other_chips(x, y)
        if which is not None:
            chips = [chips[which]]
        sends, recvs = [], []
        for k in range(n):
            for j, chip in enumerate(chips):
                for copies, core in ((sends, c), (recvs, 1 - c)):
                    block = _shard_rows(land[k], _device_index(chip, core), rows[k])
                    copies.append(pltpu.make_async_remote_copy(
                        src_ref=block, dst_ref=block, send_sem=send_sems.at[k, j], recv_sem=recv_sems.at[k, j],
                        device_id=(x, y, 1 - c), device_id_type=MESH_ID))
        for cp in sends:
            cp.start()
        for cp in recvs:
            cp.wait_recv()
        for cp in sends:
            cp.wait_send()

    return pl.pallas_call(
        body, name=name, out_shape=[jax.ShapeDtypeStruct(a.shape, a.dtype) for a in buffers],
        in_specs=[ANY] * n, out_specs=[ANY] * n, input_output_aliases={i: i for i in range(n)},
        scratch_shapes=[pltpu.SemaphoreType.DMA((n, n_chips)), pltpu.SemaphoreType.DMA((n, n_chips))],
    )(*buffers)


def _rs_pair(grads, name):
    n = len(grads)
    rows = [g.shape[0] // N_DEV for g in grads]

    def body(*refs):
        ins, outs = refs[:n], refs[n:2 * n]
        send_sems, recv_sems = refs[2 * n:]
        x, y, c = _position()
        copies = []
        for k in range(n):
            for q in range(4):
                copies.append(pltpu.make_async_remote_copy(
                    src_ref=_shard_rows(ins[k], 2 * q + 1 - c, rows[k]), dst_ref=_shard_rows(outs[k], q, rows[k]),
                    send_sem=send_sems.at[k, q], recv_sem=recv_sems.at[k, q], device_id=(x, y, 1 - c),
                    device_id_type=MESH_ID))
        for cp in copies:
            cp.start()
        for cp in copies:
            cp.wait()

    return pl.pallas_call(
        body, out_shape=[jax.ShapeDtypeStruct((4 * r, g.shape[1]), g.dtype) for g, r in zip(grads, rows)],
        in_specs=[ANY] * n, out_specs=[ANY] * n,
        scratch_shapes=[pltpu.SemaphoreType.DMA((n, 4)), pltpu.SemaphoreType.DMA((n, 4))],
        name=name)(*grads)


def _wgrad_pair(lhs, rhs, name, *, lhs_is_transposed, deps=()):
    t = rhs.shape[0]
    n = lhs.shape[0] if lhs_is_transposed else lhs.shape[1]
    r = n // N_DEV
    n_chips = N_DEV // 2
    per = 1 if (2 * r) % BLK == 0 else 2
    steps = n_chips // per

    def body(l_ref, r_ref, kept_ref, recv_ref, res, send_sems, recv_sems):
        q = pl.program_id(0)
        slot = q % 2
        x, y, c = _position()

        def send(step, buf, i):
            return pltpu.make_async_remote_copy(
                src_ref=res.at[buf, pl.ds(pl.multiple_of((2 * i + 1 - c) * r, 16), r), :],
                dst_ref=_shard_rows(recv_ref, step * per + i, r),
                send_sem=send_sems.at[buf, i], recv_sem=recv_sems.at[step * per + i],
                device_id=(x, y, 1 - c), device_id_type=MESH_ID)

        @pl.when(q >= 2)
        def _():
            for i in range(per):
                send(q - 2, slot, i).wait_send()

        if lhs_is_transposed:
            res[slot] = _dot(l_ref[...], r_ref[...]).astype(BF)
        else:
            res[slot] = _dot_tn(l_ref[...], r_ref[...]).astype(BF)
        for i in range(per):
            kept_ref[i * r:(i + 1) * r, :] = res[slot, pl.ds(pl.multiple_of((2 * i + c) * r, 16), r), :]
            send(q, slot, i).start()

        @pl.when(q == steps - 1)
        def _():
            for i in range(per):
                if steps > 1:
                    send(q - 1, 1 - slot, i).wait_send()
                send(q, slot, i).wait_send()
            for chip in range(n_chips):
                send(chip // per, 0, chip % per).wait_recv()

    width = 2 * r * per
    lhs_spec = pl.BlockSpec((width, t), lambda q: (q, 0)) if lhs_is_transposed else pl.BlockSpec((t, width), lambda q: (0, q))
    return _call(
        body, deps, (lhs, rhs), grid=(steps,),
        in_specs=[lhs_spec, _resident((t, D))],
        out_specs=[pl.BlockSpec((per * r, D), lambda q: (q, 0)), ANY],
        out_shape=[jax.ShapeDtypeStruct((n // 2, D), BF)] * 2,
        scratch_shapes=[pltpu.VMEM((2, width, D), BF), pltpu.SemaphoreType.DMA((2, per)),
                        pltpu.SemaphoreType.DMA((n_chips,))],
        compiler_params=_params(1), name=name)


def _wgrad_pair_sum(lhs, rhs, place, name, *, lhs_is_transposed, deps=()):
    t = rhs.shape[0]
    n = lhs.shape[0] if lhs_is_transposed else lhs.shape[1]
    r = n // N_DEV
    n_chips = N_DEV // 2
    per = 1 if (2 * r) % BLK == 0 else 2
    steps = n_chips // per
    n_deps = len(deps)

    def body(place_ref, *refs):
        l_ref, r_ref, part_ref, land_ref, res, inbox, send_sems, recv_sems = refs[n_deps:]
        q = pl.program_id(0)
        slot = q % 2
        x, y, c = _position()

        def send(step, buf, i):
            return pltpu.make_async_remote_copy(
                src_ref=res.at[buf, pl.ds(pl.multiple_of((2 * i + 1 - c) * r, 16), r), :], dst_ref=inbox.at[step * per + i],
                send_sem=send_sems.at[buf, i], recv_sem=recv_sems.at[step * per + i],
                device_id=(x, y, 1 - c), device_id_type=MESH_ID)

        @pl.when(q < steps)
        def _():
            @pl.when(q >= 2)
            def _():
                for i in range(per):
                    send(q - 2, slot, i).wait_send()

            if lhs_is_transposed:
                res[slot] = _dot(l_ref[...], r_ref[...]).astype(BF)
            else:
                res[slot] = _dot_tn(l_ref[...], r_ref[...]).astype(BF)
            for i in range(per):
                send(q, slot, i).start()

        @pl.when(q >= 1)
        def _():
            for i in range(per):
                chip = (q - 1) * per + i
                send(q - 1, 1 - slot, i).wait_recv()
                kept = res[1 - slot, pl.ds(pl.multiple_of((2 * i + c) * r, 16), r), :]
                total = (kept.astype(F32) + inbox[chip].astype(F32)).astype(BF)
                part_ref[i * r:(i + 1) * r, :] = total

                @pl.when(chip == place_ref[1])
                def _():
                    land_ref[...] = total

        @pl.when(q == steps)
        def _():
            for i in range(per):
                if steps > 1:
                    send(q - 2, slot, i).wait_send()
                send(q - 1, 1 - slot, i).wait_send()

    width = 2 * r * per
    last = steps - 1
    if lhs_is_transposed:
        lhs_spec = pl.BlockSpec((width, t), lambda q, p: (jnp.minimum(q, last), 0))
    else:
        lhs_spec = pl.BlockSpec((t, width), lambda q, p: (0, jnp.minimum(q, last)))
    return pl.pallas_call(
        body,
        grid_spec=pltpu.PrefetchScalarGridSpec(
            num_scalar_prefetch=1, grid=(steps + 1,),
            in_specs=[ANY] * n_deps + [lhs_spec, pl.BlockSpec((t, D), lambda q, p: (0, 0), pipeline_mode=pl.Buffered(1))],
            out_specs=[pl.BlockSpec((per * r, D), lambda q, p: (jnp.maximum(q - 1, 0), 0)),
                       pl.BlockSpec((r, D), lambda q, p: (p[1], 0))],
            scratch_shapes=[pltpu.VMEM((2, width, D), BF), pltpu.VMEM((n_chips, r, D), BF),
                            pltpu.SemaphoreType.DMA((2, per)), pltpu.SemaphoreType.DMA((n_chips,))]),
        out_shape=[jax.ShapeDtypeStruct((n // 2, D), BF)] * 2,
        compiler_params=_params(1), name=name)(place, *deps, lhs, rhs)


def _pair_add(grad, received, place, name, kept_only=False):
    r = received.shape[0] // 4
    parity = 0 if kept_only else 1

    def body(place_ref, g_ref, r_ref, o_ref, land_ref):
        total = (g_ref[...].astype(F32) + r_ref[...].astype(F32)).astype(BF)
        o_ref[...] = total

        @pl.when(pl.program_id(0) == place_ref[1])
        def _():
            land_ref[...] = total

    return pl.pallas_call(
        body,
        grid_spec=pltpu.PrefetchScalarGridSpec(
            num_scalar_prefetch=1, grid=(4,),
            in_specs=[pl.BlockSpec((r, D), lambda q, p: ((1 + parity) * q + parity * p[0], 0)),
                      pl.BlockSpec((r, D), lambda q, p: (q, 0))],
            out_specs=[pl.BlockSpec((r, D), lambda q, p: (q, 0)), pl.BlockSpec((r, D), lambda q, p: (p[1], 0))]),
        out_shape=[jax.ShapeDtypeStruct(received.shape, BF)] * 2,
        compiler_params=_params(1), name=name)(place, grad, received)


def _sum_blocks(gathered, rows):
    def body(b_ref, o_ref):
        acc = b_ref[0:rows, :]
        for d in range(1, N_DEV):
            acc = acc + b_ref[d * rows:(d + 1) * rows, :]
        o_ref[...] = acc

    return pl.pallas_call(body, out_shape=jax.ShapeDtypeStruct((rows, D), F32), name="small_sum")(gathered)


def _adamw_math(w, g, m, v):
    m = ADAM_B1 * m + (1.0 - ADAM_B1) * g
    v = ADAM_B2 * v + (1.0 - ADAM_B2) * (g * g)
    m_hat = m / (1.0 - ADAM_B1 ** ADAM_STEP)
    v_hat = v / (1.0 - ADAM_B2 ** ADAM_STEP)
    delta = -ADAM_LR * (m_hat / (jnp.sqrt(v_hat) + ADAM_EPS) + ADAM_WD * w)
    return delta, m, v


def _sum_partials(blocks):
    g = blocks[0].astype(F32)
    for blk in blocks[1:]:
        g = g + blk.astype(F32)
    return g


def _reduce_adamw(landed, w, m, v, name):
    r = w.shape[0]
    tr = 352 if r % 352 == 0 else r
    per = r // tr

    def body(r0, r1, r2, r3, w_ref, m_ref, v_ref, g_ref, d_ref, nm_ref, nv_ref):
        g = _sum_partials([r0[...], r1[...], r2[...], r3[...]])
        g_ref[...] = g
        d_ref[...], nm_ref[...], nv_ref[...] = _adamw_math(w_ref[...], g, m_ref[...], v_ref[...])

    tile = _row_tile(tr, D)
    return pl.pallas_call(
        body, grid=(per,),
        in_specs=[pl.BlockSpec((tr, D), lambda i, q=q: (q * per + i, 0)) for q in range(4)] + [tile] * 3,
        out_specs=[tile] * 4, out_shape=[jax.ShapeDtypeStruct(w.shape, F32)] * 4,
        compiler_params=_params(1), name=name)(landed, landed, landed, landed, w, m, v)


def _adamw_small(w, g, m, v, name):
    def body(w_ref, g_ref, m_ref, v_ref, d_ref, nm_ref, nv_ref):
        d_ref[...], nm_ref[...], nv_ref[...] = _adamw_math(w_ref[...], g_ref[...], m_ref[...], v_ref[...])

    return pl.pallas_call(body, out_shape=[jax.ShapeDtypeStruct(w.shape, F32)] * 3, name=name)(w, g, m, v)


WEIGHTS = ("ffn1_norm", "ffn1_w_in", "ffn1_w_out", "mix_norm", "w_in", "conv_dw_kernel", "conv_dw_bias", "conv_ln_g",
           "conv_ln_b", "conv_w_proj", "q_norm", "k_norm", "attn_sinks", "rel_bias", "attn_w_o", "w_out", "ffn2_norm",
           "ffn2_w_in", "ffn2_w_out")
MATRICES = ("ffn1_w_in", "ffn1_w_out", "w_in", "conv_w_proj", "attn_w_o", "w_out", "ffn2_w_in", "ffn2_w_out")
COLUMN_SHARDED = ("ffn1_w_in", "w_in", "ffn2_w_in")
ROW_VECTORS = ("ffn1_norm", "mix_norm", "conv_dw_bias", "conv_ln_g", "conv_ln_b", "ffn2_norm")
PACKED = (("q_norm", HD), ("k_norm", HD), ("attn_sinks", NQ), ("rel_bias", NBUCKET * NQ))
GATHER = _Exchange(gather=True)
GATHER_ALL = _Exchange(gather=True, all_cores=True)
SCATTER = _Exchange(gather=False)
FIRST = "ffn1_w_in"
GATHER_STAGES = ("ffn1_out", "mix_proj", "mix_merge", "ffn2")
STAGE_GATHER = {"ffn1_out": GATHER, "mix_proj": GATHER, "mix_merge": GATHER, "ffn2": GATHER_ALL}
STAGE_MEMBERS = {"ffn1_out": ("ffn1_w_out",),
                 "mix_proj": ("w_in", "taps"), "mix_merge": ("conv_w_proj", "attn_w_o", "w_out"),
                 "ffn2": ("ffn2_w_in", "ffn2_w_out")}
ROW_PACKED = len(ROW_VECTORS)
ROW_LOSS = ROW_PACKED + 1
ROW_TAPS = 8
PAYLOAD_ROWS = 48


def _pack_small(values, last_row):
    packed = jnp.concatenate([values[k].reshape(-1) for k, _ in PACKED])
    packed = jnp.pad(packed, (0, D - packed.shape[0])).reshape(1, D)
    return jnp.concatenate([values[k].reshape(1, D) for k in ROW_VECTORS] + [packed, last_row], axis=0)


def _unpack_small(rows):
    out = {k: rows[i] for i, k in enumerate(ROW_VECTORS)}
    at = 0
    for k, size in PACKED:
        out[k] = rows[ROW_PACKED, at:at + size]
        at += size
    out["rel_bias"] = out["rel_bias"].reshape(NBUCKET, NQ)
    return out


def kernel(x, ffn1_norm, ffn1_w_in, ffn1_w_out, mix_norm, w_in, conv_dw_kernel, conv_dw_bias, conv_ln_g, conv_ln_b, conv_w_proj, q_norm, k_norm, attn_sinks, rel_bias, attn_w_o, w_out, ffn2_norm, ffn2_w_in, ffn2_w_out, loss_target, m_ffn1_norm, m_ffn1_w_in, m_ffn1_w_out, m_mix_norm, m_w_in, m_conv_dw_kernel, m_conv_dw_bias, m_conv_ln_g, m_conv_ln_b, m_conv_w_proj, m_q_norm, m_k_norm, m_attn_sinks, m_rel_bias, m_attn_w_o, m_w_out, m_ffn2_norm, m_ffn2_w_in, m_ffn2_w_out, v_ffn1_norm, v_ffn1_w_in, v_ffn1_w_out, v_mix_norm, v_w_in, v_conv_dw_kernel, v_conv_dw_bias, v_conv_ln_g, v_conv_ln_b, v_conv_w_proj, v_q_norm, v_k_norm, v_attn_sinks, v_rel_bias, v_attn_w_o, v_w_out, v_ffn2_norm, v_ffn2_w_in, v_ffn2_w_out):
    w = dict(ffn1_norm=ffn1_norm, ffn1_w_in=ffn1_w_in, ffn1_w_out=ffn1_w_out, mix_norm=mix_norm, w_in=w_in,
             conv_dw_kernel=conv_dw_kernel, conv_dw_bias=conv_dw_bias, conv_ln_g=conv_ln_g, conv_ln_b=conv_ln_b,
             conv_w_proj=conv_w_proj, q_norm=q_norm, k_norm=k_norm, attn_sinks=attn_sinks, rel_bias=rel_bias,
             attn_w_o=attn_w_o, w_out=w_out, ffn2_norm=ffn2_norm, ffn2_w_in=ffn2_w_in, ffn2_w_out=ffn2_w_out)
    m = dict(ffn1_norm=m_ffn1_norm, ffn1_w_in=m_ffn1_w_in, ffn1_w_out=m_ffn1_w_out, mix_norm=m_mix_norm, w_in=m_w_in,
             conv_dw_kernel=m_conv_dw_kernel, conv_dw_bias=m_conv_dw_bias, conv_ln_g=m_conv_ln_g, conv_ln_b=m_conv_ln_b,
             conv_w_proj=m_conv_w_proj, q_norm=m_q_norm, k_norm=m_k_norm, attn_sinks=m_attn_sinks, rel_bias=m_rel_bias,
             attn_w_o=m_attn_w_o, w_out=m_w_out, ffn2_norm=m_ffn2_norm, ffn2_w_in=m_ffn2_w_in, ffn2_w_out=m_ffn2_w_out)
    v = dict(ffn1_norm=v_ffn1_norm, ffn1_w_in=v_ffn1_w_in, ffn1_w_out=v_ffn1_w_out, mix_norm=v_mix_norm, w_in=v_w_in,
             conv_dw_kernel=v_conv_dw_kernel, conv_dw_bias=v_conv_dw_bias, conv_ln_g=v_conv_ln_g, conv_ln_b=v_conv_ln_b,
             conv_w_proj=v_conv_w_proj, q_norm=v_q_norm, k_norm=v_k_norm, attn_sinks=v_attn_sinks, rel_bias=v_rel_bias,
             attn_w_o=v_attn_w_o, w_out=v_w_out, ffn2_norm=v_ffn2_norm, ffn2_w_in=v_ffn2_w_in, ffn2_w_out=v_ffn2_w_out)
    px, py, pc = _position()
    me = 4 * px + 2 * py + pc
    place = jnp.stack([pc, 2 * px + py]).astype(jnp.int32)

    rows_of = lambda k, a: a.T if k in COLUMN_SHARDED else a
    me1 = me.astype(jnp.int32).reshape(1)
    rest = tuple(k for k in MATRICES if k != FIRST)
    shard_rows = dict({k: rows_of(k, w[k]).shape[0] for k in MATRICES}, taps=CWP)
    to_peer = [_Exchange(gather=True, only=j) for j in range(N_DEV // 2 - 1)]
    sems_first, _, thru_first, token = _ici_copies_start(
        [[(0, shard_rows[FIRST])]] * len(to_peer), None, _prep([rows_of(FIRST, w[FIRST])], None, me1, "prep_first"), to_peer,
        "gather_start_first")
    buffers = dict(zip(rest + ("taps",), _prep([rows_of(k, w[k]) for k in rest], conv_dw_kernel, me1, "prep", deps=[token])))
    landings, sets = [], []
    for stage in GATHER_STAGES:
        sets.append([(len(landings) + i, shard_rows[k]) for i, k in enumerate(STAGE_MEMBERS[stage])])
        landings += list(STAGE_MEMBERS[stage])
    sems, _, land_thru, _ = _ici_copies_start(sets, None, [buffers[k] for k in landings],
                                              [STAGE_GATHER[s] for s in GATHER_STAGES], "gather_start")

    def first_weight_blocks(after):
        state = {"buffer": thru_first[0]}
        chips = [(px, py)] + _other_chips(px, py)

        def next_block(j, before):
            landed = [state["buffer"]]
            if j > 0:
                landed = _ici_copies_wait(sems_first[j - 1], [shard_rows[FIRST]], None, landed, to_peer[j - 1],
                                          [before] + (list(after) if j == 1 else []), "gather_wait_first_%d" % j)
            state["buffer"], = _d2d_gather(landed, [shard_rows[FIRST]], "gather_d2d_first_%d" % j, which=j)
            return state["buffer"], _chip_index(chips[j]).astype(jnp.int32).reshape(1)

        return [functools.partial(next_block, j) for j in range(len(chips))]

    def weights_of(stage, after):
        s = GATHER_STAGES.index(stage)
        rows = [r for _, r in sets[s]]
        landed = _ici_copies_wait(sems[s], rows, None, [land_thru[k] for k, _ in sets[s]], STAGE_GATHER[stage],
                                  list(after), "gather_wait_" + stage)
        if not STAGE_GATHER[stage].all_cores:
            landed = _d2d_gather(landed, rows, "gather_d2d_" + stage)
        out = dict(zip(STAGE_MEMBERS[stage], landed))
        if "taps" in out:
            taps = out.pop("taps")
            out["conv_dw_kernel"] = jnp.transpose(taps.reshape(N_DEV, CWP, BLK), (1, 0, 2)).reshape(CWP, D)[:CW]
        return out

    in_flight = []

    def wgrad(lhs, rhs, name, lhs_is_transposed, deps=()):
        rows = (lhs.shape[0] if lhs_is_transposed else lhs.shape[1]) // N_DEV
        if rows <= WGRAD_SUM_MAX_ROWS:
            return ("summed",) + tuple(_wgrad_pair_sum(lhs, rhs, place, name, lhs_is_transposed=lhs_is_transposed, deps=deps))
        return ("paired",) + tuple(_wgrad_pair(lhs, rhs, name, lhs_is_transposed=lhs_is_transposed, deps=deps))

    def grads_done(stage, grads):
        names = list(grads)
        added = []
        for k in names:
            if not isinstance(grads[k], tuple):
                received, = _rs_pair([grads[k]], "rs_pair_" + k)
                added.append(_pair_add(grads[k], received, place, "pair_add_" + k))
            elif grads[k][0] == "paired":
                added.append(_pair_add(grads[k][1], grads[k][2], place, "pair_add_" + k, kept_only=True))
            else:
                added.append(grads[k][1:])
        partials = [p for p, _ in added]
        members = [(i, p.shape[0] // 4) for i, p in enumerate(partials)]
        sem, p_thru, l_thru, token = _ici_copies_start([members], partials, [l for _, l in added], [SCATTER],
                                                       "scatter_start_" + stage)
        in_flight.append((stage, names, sem[0], p_thru, l_thru, token))
        return [token]

    small = []

    def small_done(gv, sq):
        payload = jnp.concatenate([_pack_small(gv, sq), jnp.pad(gv["conv_dw_kernel"], ((0, PAYLOAD_ROWS - ROW_TAPS - CW), (0, 0)))],
                                  axis=0)
        mine = lax.dynamic_update_slice_in_dim(lax.empty((N_DEV * PAYLOAD_ROWS, D), F32), payload, me * PAYLOAD_ROWS, axis=0)
        sems, _, thru, token = _ici_copies_start([[(0, PAYLOAD_ROWS)]], None, [mine], [GATHER_ALL], "small_start")
        small.append((sems[0], thru))
        return [token]

    vec = {k: w[k] for k in WEIGHTS if k not in MATRICES and k != "conv_dw_kernel"}
    dx0 = _local_step(x[0], loss_target[0], vec, first_weight_blocks, weights_of, wgrad, grads_done, small_done)
    gathered, = _ici_copies_wait(small[0][0], [PAYLOAD_ROWS], None, small[0][1], GATHER_ALL, [in_flight[-1][-1]], "small_wait")
    total = _sum_blocks(gathered, PAYLOAD_ROWS)
    loss = (0.5 / D) * jnp.sum(total[ROW_LOSS])

    grads, delta, new_m, new_v = {}, {}, {}, {}
    after = [total]
    for stage, names, sem, p_thru, l_thru, _ in in_flight:
        landed = _ici_copies_wait(sem, [p.shape[0] // 4 for p in p_thru], p_thru, l_thru, SCATTER, after,
                                  "scatter_wait_" + stage)
        after = []
        for k, buf in zip(names, landed):
            out = _reduce_adamw(buf, rows_of(k, w[k]), rows_of(k, m[k]), rows_of(k, v[k]), "adamw_" + k)
            grads[k], delta[k], new_m[k], new_v[k] = [rows_of(k, a) for a in out]
            after.append(out[1])
    zero_row = jnp.zeros((1, D), F32)
    d8, m8, v8 = _adamw_small(_pack_small(w, zero_row), total[:ROW_TAPS], _pack_small(m, zero_row),
                              _pack_small(v, zero_row), "adamw_small")
    grads.update(_unpack_small(total[:ROW_TAPS]))
    delta.update(_unpack_small(d8))
    new_m.update(_unpack_small(m8))
    new_v.update(_unpack_small(v8))
    k = "conv_dw_kernel"
    grads[k] = lax.dynamic_slice_in_dim(total[ROW_TAPS:ROW_TAPS + CW], me * BLK, BLK, axis=1)
    delta[k], new_m[k], new_v[k] = _adamw_small(w[k], grads[k], m[k], v[k], "adamw_taps")

    return (loss, dx0[None], *[grads[k] for k in WEIGHTS], *[delta[k] for k in WEIGHTS],
            *[new_m[k] for k in WEIGHTS], *[new_v[k] for k in WEIGHTS])
```

```python
import functools
import math

import numpy as np
import jax
import jax.numpy as jnp
from jax import lax
from jax.experimental import pallas as pl
from jax.experimental.pallas import tpu as pltpu

F32 = jnp.float32
BF = jnp.bfloat16

D = 1024
F = 2816
INW = 5632
CW = 31
CWP = 32
HD = 64
NQ = 16
NKV = 4
GRP = NQ // NKV
BLK = 128
NBUCKET = 32
EPS = 1e-6
NEG = float(jnp.finfo(jnp.float32).min)
QK_SCALE = 1.0 / math.sqrt(HD)
R_CONV = (0, 2048)
R_QKV = (2048, 3584)
R_Q = (2048, 3072)
R_KV = (3072, 3584)
R_GATE = (3584, 5632)

N_DEV = 8
VMEM_LIMIT_V7X = 56 * 1024 * 1024
ROW_TILE = 256
ROW_TILE_WIDE = 512
WGRAD_SUM_MAX_ROWS = 352

ADAM_LR = 0.001
ADAM_B1 = 0.9
ADAM_B2 = 0.999
ADAM_EPS = 1e-08
ADAM_WD = 0.01
ADAM_STEP = 10

NT_DIMS = (((1,), (1,)), ((), ()))
TN_DIMS = (((0,), (0,)), ((), ()))


def _dot(a, b):
    return jnp.dot(a, b, preferred_element_type=F32)


def _dot_nt(a, b):
    return lax.dot_general(a, b, NT_DIMS, preferred_element_type=F32)


def _dot_tn(a, b):
    return lax.dot_general(a, b, TN_DIMS, preferred_element_type=F32)


def _sig(x):
    return 0.5 * jnp.tanh(0.5 * x) + 0.5


ANY = pl.BlockSpec(memory_space=pl.ANY)


def _call(body, deps, args, **kw):
    n = len(deps)
    if n:
        kw["in_specs"] = [ANY] * n + list(kw["in_specs"])
        return pl.pallas_call(lambda *refs: body(*refs[n:]), **kw)(*deps, *args)
    return pl.pallas_call(body, **kw)(*args)


def _params(n_axes):
    return pltpu.CompilerParams(dimension_semantics=("arbitrary",) * n_axes, vmem_limit_bytes=VMEM_LIMIT_V7X)


def _resident(shape):
    zeros = (0,) * len(shape)
    return pl.BlockSpec(shape, lambda *_: zeros, pipeline_mode=pl.Buffered(1))


def _row_tile(rows, cols):
    return pl.BlockSpec((rows, cols), lambda i: (i, 0))


def _rms_stats(x):
    r = lax.rsqrt(jnp.mean(x * x, axis=-1, keepdims=True) + EPS)
    return r, x * r


def _rms_bwd(dn, x, g):
    r, xh = _rms_stats(x)
    dxh = dn * g
    dx = r * (dxh - xh * jnp.mean(dxh * xh, axis=-1, keepdims=True))
    return dx, jnp.sum(dn * xh, axis=0, keepdims=True)


def _ffn_last(x, target, g, w_in_t, w_out, name):
    t = x.shape[0]
    tm = min(ROW_TILE, t)

    def body(x_ref, t_ref, g_ref, w_ref, wo_ref, n_ref, du_ref, h_ref, dy_ref, dx_ref, sq_ref, dg_ref):
        @pl.when(pl.program_id(0) == 0)
        def _():
            sq_ref[...] = jnp.zeros_like(sq_ref)
            dg_ref[...] = jnp.zeros_like(dg_ref)

        x = x_ref[...]
        g = g_ref[...]
        r, xh = _rms_stats(x)
        n = (xh * g).astype(BF)
        n_ref[...] = n
        u = _dot_nt(n, w_ref[...])
        a = u[:, :F]
        b = u[:, F:]
        s = _sig(a)
        sa = a * s
        h = (sa * b).astype(BF)
        h_ref[...] = h
        err = x + 0.5 * _dot(h, wo_ref[...]) - t_ref[...]
        sq_ref[...] += jnp.sum(err * err, axis=0, keepdims=True)
        dxo = err * (1.0 / D)
        dy = (0.5 * dxo).astype(BF)
        dy_ref[...] = dy
        dh = _dot_nt(dy, wo_ref[...])
        du_ref[:, :F] = (dh * b * (s * (1.0 + a * (1.0 - s)))).astype(BF)
        du_ref[:, F:] = (dh * sa).astype(BF)
        dn = _dot(du_ref[...], w_ref[...])
        dxh = dn * g
        dx_ref[...] = dxo + r * (dxh - xh * jnp.mean(dxh * xh, axis=-1, keepdims=True))
        dg_ref[...] += jnp.sum(dn * xh, axis=0, keepdims=True)

    vec = pl.BlockSpec((1, D), lambda i: (0, 0))
    return pl.pallas_call(
        body, grid=(t // tm,),
        in_specs=[_row_tile(tm, D), _row_tile(tm, D), _resident((1, D)), _resident((INW, D)), _resident((F, D))],
        out_specs=[_row_tile(tm, D), _row_tile(tm, INW), _row_tile(tm, F), _row_tile(tm, D), _row_tile(tm, D), vec, vec],
        out_shape=[jax.ShapeDtypeStruct((t, D), BF), jax.ShapeDtypeStruct((t, INW), BF), jax.ShapeDtypeStruct((t, F), BF),
                   jax.ShapeDtypeStruct((t, D), BF), jax.ShapeDtypeStruct((t, D), F32), jax.ShapeDtypeStruct((1, D), F32),
                   jax.ShapeDtypeStruct((1, D), F32)],
        compiler_params=_params(1), name=name)(x, target, g, w_in_t, w_out)


def _ffn_up_blocks(x, g, n, w_in_t, order, u, name, deps=()):
    t = (x if n is None else n).shape[0]
    tm = min(ROW_TILE_WIDE, t)
    c = INW * 2 // N_DEV
    n_deps = len(deps)
    first = n is None
    assert not first or order.shape == (1,)

    def body(order_ref, *refs):
        refs = refs[n_deps:]
        if first:
            x_ref, g_ref, w_ref, n_ref, u_ref = refs
            nt = (_rms_stats(x_ref[...])[1] * g_ref[...]).astype(BF)
            n_ref[...] = nt
        else:
            n_ref, w_ref, _, u_ref = refs
            nt = n_ref[...]
        u_ref[...] = _dot_nt(nt, w_ref[...]).astype(BF)

    rows = pl.BlockSpec((tm, D), lambda k, i, o: (i, 0))
    block = pl.BlockSpec((c, D), lambda k, i, o: (o[k], 0))
    cols = pl.BlockSpec((tm, c), lambda k, i, o: (i, o[k]))
    u_shape = jax.ShapeDtypeStruct((t, INW), BF)
    if first:
        args, in_specs = (x, g, w_in_t), [rows, _resident((1, D)), block]
        out_specs, out_shape, aliases = [rows, cols], [jax.ShapeDtypeStruct((t, D), BF), u_shape], {}
    else:
        args, in_specs = (n, w_in_t, u), [rows, block, ANY]
        out_specs, out_shape, aliases = cols, u_shape, {1 + n_deps + 2: 0}
    out = pl.pallas_call(
        body,
        grid_spec=pltpu.PrefetchScalarGridSpec(num_scalar_prefetch=1, grid=(order.shape[0], t // tm),
                                               in_specs=[ANY] * n_deps + in_specs, out_specs=out_specs),
        out_shape=out_shape, input_output_aliases=aliases, compiler_params=_params(2), name=name)(order, *deps, *args)
    return tuple(out) if first else (n, out)


def _ffn_down(x, u, w_out, name):
    t = x.shape[0]
    tm = min(ROW_TILE_WIDE, t)

    def body(x_ref, u_ref, wo_ref, xo_ref):
        a = u_ref[:, :F].astype(F32)
        b = u_ref[:, F:].astype(F32)
        h = (a * _sig(a) * b).astype(BF)
        xo_ref[...] = x_ref[...] + 0.5 * _dot(h, wo_ref[...])

    return pl.pallas_call(
        body, grid=(t // tm,), in_specs=[_row_tile(tm, D), _row_tile(tm, INW), _resident((F, D))],
        out_specs=_row_tile(tm, D), out_shape=jax.ShapeDtypeStruct((t, D), F32),
        compiler_params=_params(1), name=name)(x, u, w_out)


def _ffn_bwd(dxo, x, g, u, w_in_t, w_out, name, deps=()):
    t = x.shape[0]
    tm = min(ROW_TILE, t)

    def body(dxo_ref, x_ref, g_ref, u_ref, w_ref, wo_ref, dx_ref, du_ref, h_ref, dy_ref, dg_ref):
        dxo = dxo_ref[...]
        dy = (0.5 * dxo).astype(BF)
        dy_ref[...] = dy
        dh = _dot_nt(dy, wo_ref[...])
        a = u_ref[:, :F].astype(F32)
        b = u_ref[:, F:].astype(F32)
        s = _sig(a)
        sa = a * s
        h_ref[...] = (sa * b).astype(BF)
        du_ref[:, :F] = (dh * b * (s * (1.0 + a * (1.0 - s)))).astype(BF)
        du_ref[:, F:] = (dh * sa).astype(BF)
        dn = _dot(du_ref[...], w_ref[...])
        dx, dg = _rms_bwd(dn, x_ref[...], g_ref[...])
        dx_ref[...] = dxo + dx

        @pl.when(pl.program_id(0) == 0)
        def _():
            dg_ref[...] = jnp.zeros_like(dg_ref)

        dg_ref[...] += dg

    return _call(
        body, deps, (dxo, x, g, u, w_in_t, w_out), grid=(t // tm,),
        in_specs=[_row_tile(tm, D), _row_tile(tm, D), _resident((1, D)), _row_tile(tm, INW), _resident((INW, D)),
                  _resident((F, D))],
        out_specs=[_row_tile(tm, D), _row_tile(tm, INW), _row_tile(tm, F), _row_tile(tm, D),
                   pl.BlockSpec((1, D), lambda i: (0, 0))],
        out_shape=[jax.ShapeDtypeStruct((t, D), F32), jax.ShapeDtypeStruct((t, INW), BF), jax.ShapeDtypeStruct((t, F), BF),
                   jax.ShapeDtypeStruct((t, D), BF), jax.ShapeDtypeStruct((1, D), F32)],
        compiler_params=_params(1), name=name)


def _wgrad(lhs, rhs, name, *, lhs_is_transposed, chunk, deps=()):
    t = rhs.shape[0]
    n = lhs.shape[0] if lhs_is_transposed else lhs.shape[1]
    c = min(chunk, n)

    def body(l_ref, r_ref, o_ref):
        if lhs_is_transposed:
            o_ref[...] = _dot(l_ref[...], r_ref[...]).astype(BF)
        else:
            o_ref[...] = _dot_tn(l_ref[...], r_ref[...]).astype(BF)

    lhs_spec = pl.BlockSpec((c, t), lambda j: (j, 0)) if lhs_is_transposed else pl.BlockSpec((t, c), lambda j: (0, j))
    return _call(
        body, deps, (lhs, rhs), grid=(n // c,),
        in_specs=[lhs_spec, _resident((t, D))],
        out_specs=pl.BlockSpec((c, D), lambda j: (j, 0)),
        out_shape=jax.ShapeDtypeStruct((n, D), BF),
        compiler_params=_params(1), name=name)


def _wgrad_mix(duc, dq_t, dkv_t, dgp, hm):
    t = hm.shape[0]
    c = 512
    first_q, first_kv, first_gate = R_Q[0] // c, R_KV[0] // c, R_GATE[0] // c

    def body(uc_ref, q_ref, kv_ref, gp_ref, h_ref, o_ref):
        j = pl.program_id(0)

        @pl.when(j < first_q)
        def _():
            o_ref[...] = _dot_tn(uc_ref[...], h_ref[...]).astype(BF)

        @pl.when((j >= first_q) & (j < first_kv))
        def _():
            o_ref[...] = _dot(q_ref[...], h_ref[...]).astype(BF)

        @pl.when((j >= first_kv) & (j < first_gate))
        def _():
            o_ref[...] = _dot(kv_ref[...], h_ref[...]).astype(BF)

        @pl.when(j >= first_gate)
        def _():
            o_ref[...] = _dot_tn(gp_ref[...], h_ref[...]).astype(BF)

    return pl.pallas_call(
        body, grid=(INW // c,),
        in_specs=[pl.BlockSpec((t, c), lambda j: (0, jnp.clip(j, 0, first_q - 1))),
                  pl.BlockSpec((c, t), lambda j: (jnp.clip(j - first_q, 0, first_kv - first_q - 1), 0)),
                  pl.BlockSpec((c, t), lambda j: (jnp.clip(j - first_kv, 0, first_gate - first_kv - 1), 0)),
                  pl.BlockSpec((t, c), lambda j: (0, jnp.clip(j - first_gate, 0, INW // c - first_gate - 1))),
                  _resident((t, D))],
        out_specs=pl.BlockSpec((c, D), lambda j: (j, 0)),
        out_shape=jax.ShapeDtypeStruct((INW, D), BF),
        compiler_params=_params(1), name="mix_dw_in")(duc, dq_t, dkv_t, dgp, hm)


def _mix_proj(x, g, w_t):
    t = x.shape[0]
    tm = min(ROW_TILE_WIDE, t)

    def body(x_ref, g_ref, w_ref, hm_ref, uc_ref, gp_ref, qkv_ref):
        r, xh = _rms_stats(x_ref[...])
        hm = (xh * g_ref[...]).astype(BF)
        hm_ref[...] = hm
        uc_ref[...] = _dot_nt(hm, w_ref[R_CONV[0]:R_CONV[1], :]).astype(BF)
        gp_ref[...] = _dot_nt(hm, w_ref[R_GATE[0]:R_GATE[1], :]).astype(BF)
        qkv_ref[...] = _dot_nt(w_ref[R_QKV[0]:R_QKV[1], :], hm).astype(BF)

    return pl.pallas_call(
        body, grid=(t // tm,),
        in_specs=[_row_tile(tm, D), _resident((1, D)), _resident((INW, D))],
        out_specs=[_row_tile(tm, D), _row_tile(tm, 2 * D), _row_tile(tm, 2 * D), pl.BlockSpec((1536, tm), lambda i: (0, i))],
        out_shape=[jax.ShapeDtypeStruct((t, D), BF), jax.ShapeDtypeStruct((t, 2 * D), BF),
                   jax.ShapeDtypeStruct((t, 2 * D), BF), jax.ShapeDtypeStruct((1536, t), BF)],
        compiler_params=_params(1), name="mix_proj")(x, g, w_t)


CONV_HALO = 32
CONV_LEAD = CONV_HALO - (CW - 1)


def _glu(uc):
    uc = uc.astype(F32)
    return uc[:, :D] * _sig(uc[:, D:])


def _ln_stats(zc):
    mu = jnp.mean(zc, axis=-1, keepdims=True)
    zm = zc - mu
    r = lax.rsqrt(jnp.mean(zm * zm, axis=-1, keepdims=True) + EPS)
    return r, zm * r


CONV_SHIFTS = 8
CONV_CHUNK = 32


def _store_shifted(buf, rows):
    for b in range(1, CONV_SHIFTS):
        buf[b, 0:rows - 8, :] = buf[0, pl.ds(b, rows - 8), :]


def _conv_fwd(uc, dwk, dwb, lng, lnb):
    t = uc.shape[0]
    tm = min(512, t)
    per = tm // CONV_HALO
    ext = tm + CONV_HALO

    def body(cur_ref, prev_ref, k_ref, kb_ref, g_ref, b_ref, o_ref, zc_ref, zsh):
        i = pl.program_id(0)
        zsh[0, 0:CONV_HALO, :] = _glu(prev_ref[...]) * (i > 0).astype(F32)
        zsh[0, CONV_HALO:, :] = _glu(cur_ref[...])
        _store_shifted(zsh, ext)

        def chunk(ci, carry):
            r0 = pl.multiple_of(ci * CONV_CHUNK, CONV_CHUNK)
            acc = jnp.zeros((CONV_CHUNK, D), F32) + kb_ref[...]
            for w in range(CW):
                a, b = divmod(CONV_LEAD + w, 8)
                acc = acc + k_ref[w:w + 1, :] * zsh[b, pl.ds(r0 + 8 * a, CONV_CHUNK), :]
            zc_ref[pl.ds(r0, CONV_CHUNK), :] = acc
            return carry

        lax.fori_loop(0, tm // CONV_CHUNK, chunk, 0)
        r, xh = _ln_stats(zc_ref[...])
        y = xh * g_ref[...] + b_ref[...]
        o_ref[...] = (y * _sig(y)).astype(BF)

    return pl.pallas_call(
        body, grid=(t // tm,),
        in_specs=[_row_tile(tm, 2 * D),
                  pl.BlockSpec((CONV_HALO, 2 * D), lambda i: (jnp.maximum(i * per - 1, 0), 0)),
                  _resident((CWP, D)), _resident((1, D)), _resident((1, D)), _resident((1, D))],
        out_specs=[_row_tile(tm, D), _row_tile(tm, D)],
        out_shape=[jax.ShapeDtypeStruct((t, D), BF), jax.ShapeDtypeStruct((t, D), F32)],
        scratch_shapes=[pltpu.VMEM((CONV_SHIFTS, ext, D), F32)],
        compiler_params=_params(1), name="conv_fwd")(uc, uc, dwk, dwb, lng, lnb)


def _conv_bwd(uc, zc, dzs, dwk, lng, lnb):
    t = uc.shape[0]
    tm = min(ROW_TILE_WIDE, t)
    per = tm // CONV_HALO
    n_tiles = t // tm
    ext = tm + CONV_HALO
    last_block = t // CONV_HALO - 1

    def body(cur_ref, zc_ref, zcn_ref, dz_ref, dzn_ref, k_ref, g_ref, b_ref,
             duc_ref, dk_ref, dkb_ref, dg_ref, db_ref, dsh, dk8, z_scr):
        i = pl.program_id(0)

        @pl.when(i == 0)
        def _():
            dk8[...] = jnp.zeros_like(dk8)
            dkb_ref[...] = jnp.zeros_like(dkb_ref)
            dg_ref[...] = jnp.zeros_like(dg_ref)
            db_ref[...] = jnp.zeros_like(db_ref)

        has_next = (i < n_tiles - 1).astype(F32)
        z_scr[...] = _glu(cur_ref[...])
        gain = g_ref[...]

        def ln_silu_bwd(zc, dzs, live):
            r, xh = _ln_stats(zc)
            y = xh * gain + b_ref[...]
            sy = _sig(y)
            dy = dzs * (sy * (1.0 + y * (1.0 - sy))) * live
            dxh = dy * gain
            dzc = r * (dxh - jnp.mean(dxh, axis=-1, keepdims=True) - xh * jnp.mean(dxh * xh, axis=-1, keepdims=True))
            return dzc, dy, xh

        dzc, dy, xh = ln_silu_bwd(zc_ref[...], dz_ref[...], 1.0)
        dsh[0, 0:tm, :] = dzc
        dg_ref[...] += jnp.sum(dy * xh, axis=0, keepdims=True)
        db_ref[...] += jnp.sum(dy, axis=0, keepdims=True)
        dkb_ref[...] += jnp.sum(dzc, axis=0, keepdims=True)
        dsh[0, tm:, :] = ln_silu_bwd(zcn_ref[...], dzn_ref[...], has_next)[0]
        _store_shifted(dsh, ext)

        def chunk(ci, carry):
            r0 = pl.multiple_of(ci * CONV_CHUNK, CONV_CHUNK)
            z_c = z_scr[pl.ds(r0, CONV_CHUNK), :]
            dz = jnp.zeros((CONV_CHUNK, D), F32)
            for w in range(CW):
                a, b = divmod(CW - 1 - w, 8)
                window = dsh[b, pl.ds(r0 + 8 * a, CONV_CHUNK), :]
                dz = dz + k_ref[w:w + 1, :] * window
                prod = z_c * window
                part = prod[0:8, :]
                for j in range(1, CONV_CHUNK // 8):
                    part = part + prod[8 * j:8 * j + 8, :]
                dk8[w] += part
            ucc = cur_ref[pl.ds(r0, CONV_CHUNK), :].astype(F32)
            sg = _sig(ucc[:, D:])
            duc_ref[pl.ds(r0, CONV_CHUNK), 0:D] = (dz * sg).astype(BF)
            duc_ref[pl.ds(r0, CONV_CHUNK), D:2 * D] = (dz * ucc[:, :D] * sg * (1.0 - sg)).astype(BF)
            return carry

        lax.fori_loop(0, tm // CONV_CHUNK, chunk, 0)

        @pl.when(i == n_tiles - 1)
        def _():
            dk_ref[...] = jnp.sum(dk8[...], axis=1)

    vec = pl.BlockSpec((1, D), lambda i: (0, 0))
    next_halo = pl.BlockSpec((CONV_HALO, D), lambda i: (jnp.minimum((i + 1) * per, last_block), 0))
    return pl.pallas_call(
        body, grid=(n_tiles,),
        in_specs=[_row_tile(tm, 2 * D), _row_tile(tm, D), next_halo, _row_tile(tm, D), next_halo,
                  _resident((CWP, D)), _resident((1, D)), _resident((1, D))],
        out_specs=[_row_tile(tm, 2 * D), pl.BlockSpec((CWP, D), lambda i: (0, 0)), vec, vec, vec],
        out_shape=[jax.ShapeDtypeStruct((t, 2 * D), BF), jax.ShapeDtypeStruct((CWP, D), F32),
                   jax.ShapeDtypeStruct((1, D), F32), jax.ShapeDtypeStruct((1, D), F32), jax.ShapeDtypeStruct((1, D), F32)],
        scratch_shapes=[pltpu.VMEM((CONV_SHIFTS, ext, D), F32), pltpu.VMEM((CWP, 8, D), F32), pltpu.VMEM((tm, D), F32)],
        compiler_params=_params(1), name="conv_bwd")(uc, zc, zc, dzs, dzs, dwk, lng, lnb)


def _norm_rows(xt, g):
    r = lax.rsqrt(jnp.mean(xt * xt, axis=0, keepdims=True) + EPS)
    xh = xt * r
    return xh * g, r, xh


ATT_TQ = 1024


def _attn_specs(t, tq):
    per = tq // BLK
    return [pl.BlockSpec((1536, tq), lambda i: (0, i)),
            pl.BlockSpec((512, BLK), lambda i: (2, jnp.maximum(i * per - 1, 0))),
            _resident((HD, 1)), _resident((HD, 1)), _resident((NKV, 1, GRP * BLK)),
            _resident((2, NKV, 2 * BLK, GRP * BLK))]


def _attn_window(hk, sb, qkv_ref, halo_ref, kn_cur, kn_halo):
    v0 = D + NKV * HD + hk * HD
    if sb == 0:
        k_prev = kn_halo[hk]
        v_prev = halo_ref[NKV * HD + hk * HD:NKV * HD + (hk + 1) * HD, :]
    else:
        k_prev = kn_cur[hk][:, (sb - 1) * BLK:sb * BLK]
        v_prev = qkv_ref[v0:v0 + HD, (sb - 1) * BLK:sb * BLK]
    kw = jnp.concatenate([k_prev, kn_cur[hk][:, sb * BLK:(sb + 1) * BLK]], axis=1).astype(BF)
    vw = jnp.concatenate([v_prev, qkv_ref[v0:v0 + HD, sb * BLK:(sb + 1) * BLK]], axis=1)
    return kw, vw


def _attn_probs(kw, qc, bias, sink):
    st = _dot_tn(kw, qc) + bias
    m = jnp.maximum(jnp.max(st, axis=0, keepdims=True), sink)
    p = jnp.exp(st - m)
    e_sink = jnp.exp(sink - m)
    inv = 1.0 / (jnp.sum(p, axis=0, keepdims=True) + e_sink)
    return p * inv, e_sink * inv


def _attn_fwd(qkv_t, qg, kg, sink_rows, bias_t):
    t = qkv_t.shape[1]
    tq = min(ATT_TQ, t)
    n_sub = tq // BLK

    def body(qkv_ref, halo_ref, qg_ref, kg_ref, sink_ref, bias_ref, o_ref, p_ref, ps_ref):
        i = pl.program_id(0)
        first = (i == 0).astype(jnp.int32)
        kgain = kg_ref[...]
        qgain = qg_ref[...]
        kn_cur = [_norm_rows(qkv_ref[D + h * HD:D + (h + 1) * HD, :].astype(F32), kgain)[0] for h in range(NKV)]
        kn_halo = [_norm_rows(halo_ref[h * HD:(h + 1) * HD, :].astype(F32), kgain)[0] for h in range(NKV)]
        for hk in range(NKV):
            for sb in range(n_sub):
                cols = slice(sb * BLK, (sb + 1) * BLK)
                kw, vw = _attn_window(hk, sb, qkv_ref, halo_ref, kn_cur, kn_halo)
                qc = jnp.concatenate(
                    [_norm_rows(qkv_ref[(GRP * hk + g) * HD:(GRP * hk + g + 1) * HD, cols].astype(F32), qgain)[0] * QK_SCALE
                     for g in range(GRP)], axis=1).astype(BF)
                bias = bias_ref[first, hk] if sb == 0 else bias_ref[0, hk]
                p, p_sink = _attn_probs(kw, qc, bias, sink_ref[hk])
                p = p.astype(BF)
                p_ref[sb, hk] = p
                ps_ref[sb, hk] = p_sink
                o = _dot(vw, p)
                for g in range(GRP):
                    head = GRP * hk + g
                    o_ref[head * HD:(head + 1) * HD, cols] = o[:, g * BLK:(g + 1) * BLK].astype(BF)

    return pl.pallas_call(
        body, grid=(t // tq,),
        in_specs=_attn_specs(t, tq),
        out_specs=[pl.BlockSpec((D, tq), lambda i: (0, i)),
                   pl.BlockSpec((n_sub, NKV, 2 * BLK, GRP * BLK), lambda i: (i, 0, 0, 0)),
                   pl.BlockSpec((n_sub, NKV, 1, GRP * BLK), lambda i: (i, 0, 0, 0))],
        out_shape=[jax.ShapeDtypeStruct((D, t), BF), jax.ShapeDtypeStruct((t // BLK, NKV, 2 * BLK, GRP * BLK), BF),
                   jax.ShapeDtypeStruct((t // BLK, NKV, 1, GRP * BLK), F32)],
        compiler_params=_params(1), name="attn_fwd")(qkv_t, qkv_t, qg, kg, sink_rows, bias_t)


def _attn_bwd(qkv_t, do_t, probs, sink_probs, qg, kg, onehot_t, deps=()):
    t = qkv_t.shape[1]
    tq = min(ATT_TQ, t)
    n_sub = tq // BLK
    n_tiles = t // tq

    def body(qkv_ref, halo_ref, do_ref, p_ref, ps_ref, qg_ref, kg_ref, oh_ref,
             dq_ref, ckv_ref, dqg_ref, dsink_ref, dbias_ref, qg_scr, sink_scr, ds_scr):
        i = pl.program_id(0)

        @pl.when(i == 0)
        def _():
            qg_scr[...] = jnp.zeros_like(qg_scr)
            sink_scr[...] = jnp.zeros_like(sink_scr)
            ds_scr[...] = jnp.zeros_like(ds_scr)

        kgain = kg_ref[...]
        qgain = qg_ref[...]
        kn_cur = [_norm_rows(qkv_ref[D + h * HD:D + (h + 1) * HD, :].astype(F32), kgain)[0] for h in range(NKV)]
        kn_halo = [_norm_rows(halo_ref[h * HD:(h + 1) * HD, :].astype(F32), kgain)[0] for h in range(NKV)]
        dqg = jnp.zeros((HD, BLK), F32)
        for hk in range(NKV):
            for sb in range(n_sub):
                cols = slice(sb * BLK, (sb + 1) * BLK)
                kw, vw = _attn_window(hk, sb, qkv_ref, halo_ref, kn_cur, kn_halo)
                qn, qr, qh = [], [], []
                for g in range(GRP):
                    head = GRP * hk + g
                    n_, r_, h_ = _norm_rows(qkv_ref[head * HD:(head + 1) * HD, cols].astype(F32), qgain)
                    qn.append(n_)
                    qr.append(r_)
                    qh.append(h_)
                qc = (jnp.concatenate(qn, axis=1) * QK_SCALE).astype(BF)
                p_bf = p_ref[sb, hk]
                p = p_bf.astype(F32)
                doc = jnp.concatenate([do_ref[(GRP * hk + g) * HD:(GRP * hk + g + 1) * HD, cols] for g in range(GRP)], axis=1)
                dp = _dot_tn(vw, doc)
                delta = jnp.sum(p * dp, axis=0, keepdims=True)
                ds = p * (dp - delta)
                sink_scr[hk] += -(ps_ref[sb, hk] * delta)
                ds_scr[hk] += ds
                dsb = ds.astype(BF)
                dqc = _dot(kw, dsb) * QK_SCALE
                ckv_ref[sb, hk * HD:(hk + 1) * HD, :] = _dot_nt(qc, dsb)
                ckv_ref[sb, NKV * HD + hk * HD:NKV * HD + (hk + 1) * HD, :] = _dot_nt(doc, p_bf)
                for g in range(GRP):
                    head = GRP * hk + g
                    dqn = dqc[:, g * BLK:(g + 1) * BLK]
                    dqh = dqn * qgain
                    dq = qr[g] * (dqh - qh[g] * jnp.mean(dqh * qh[g], axis=0, keepdims=True))
                    dq_ref[head * HD:(head + 1) * HD, cols] = dq.astype(BF)
                    dqg = dqg + dqn * qh[g]
        qg_scr[...] += dqg

        @pl.when(i == n_tiles - 1)
        def _():
            dqg_ref[...] = jnp.sum(qg_scr[...], axis=1, keepdims=True)
            dsink_ref[...] = _group_lane_sums(sink_scr[:, 0, :])

            def bucket(b, carry):
                oh = jnp.concatenate([oh_ref[b]] * GRP, axis=1)
                dbias_ref[b] = _group_lane_sums(jnp.sum(ds_scr[...] * oh[None], axis=1))
                return carry

            lax.fori_loop(0, NBUCKET, bucket, 0)

    return _call(
        body, deps, (qkv_t, qkv_t, do_t, probs, sink_probs, qg, kg, onehot_t), grid=(n_tiles,),
        in_specs=_attn_specs(t, tq)[:2] + [pl.BlockSpec((D, tq), lambda i: (0, i)),
                                           pl.BlockSpec((n_sub, NKV, 2 * BLK, GRP * BLK), lambda i: (i, 0, 0, 0)),
                                           pl.BlockSpec((n_sub, NKV, 1, GRP * BLK), lambda i: (i, 0, 0, 0))]
        + _attn_specs(t, tq)[2:4] + [_resident((NBUCKET, 2 * BLK, BLK))],
        out_specs=[pl.BlockSpec((D, tq), lambda i: (0, i)),
                   pl.BlockSpec((n_sub, 2 * NKV * HD, 2 * BLK), lambda i: (i, 0, 0)),
                   pl.BlockSpec((HD, 1), lambda i: (0, 0)),
                   pl.BlockSpec((NKV, BLK), lambda i: (0, 0)),
                   pl.BlockSpec((NBUCKET, NKV, BLK), lambda i: (0, 0, 0))],
        out_shape=[jax.ShapeDtypeStruct((D, t), BF),
                   jax.ShapeDtypeStruct((t // BLK, 2 * NKV * HD, 2 * BLK), F32),
                   jax.ShapeDtypeStruct((HD, 1), F32),
                   jax.ShapeDtypeStruct((NKV, BLK), F32),
                   jax.ShapeDtypeStruct((NBUCKET, NKV, BLK), F32)],
        scratch_shapes=[pltpu.VMEM((HD, BLK), F32), pltpu.VMEM((NKV, 1, GRP * BLK), F32),
                        pltpu.VMEM((NKV, 2 * BLK, GRP * BLK), F32)],
        compiler_params=_params(1), name="attn_bwd")


def _kv_combine_tile(c_ref, cn_ref, has_next, k_ref, kgain, o_ref):
    rows = NKV * HD
    per = c_ref.shape[0]
    dkg = jnp.zeros((HD, BLK), F32)
    for s in range(per):
        cols = slice(s * BLK, (s + 1) * BLK)
        after = c_ref[s + 1, :, :BLK] if s + 1 < per else cn_ref[0, :, :BLK] * has_next
        d = c_ref[s, :, BLK:] + after
        o_ref[rows:, cols] = d[rows:, :].astype(BF)
        for h in range(NKV):
            _, r, kh = _norm_rows(k_ref[h * HD:(h + 1) * HD, cols].astype(F32), kgain)
            dkn = d[h * HD:(h + 1) * HD, :]
            dkh = dkn * kgain
            o_ref[h * HD:(h + 1) * HD, cols] = (r * (dkh - kh * jnp.mean(dkh * kh, axis=0, keepdims=True))).astype(BF)
            dkg = dkg + dkn * kh
    return dkg


def _group_lane_sums(v):
    lane_group = lax.broadcasted_iota(jnp.int32, (1, GRP * BLK), 1) // BLK
    col = lax.broadcasted_iota(jnp.int32, (1, BLK), 1)
    out = jnp.zeros((NKV, BLK), F32)
    for g in range(GRP):
        s = jnp.sum(jnp.where(lane_group == g, v, 0.0), axis=1, keepdims=True)
        out = jnp.where(col == g, s, out)
    return out


def _mix_out(zs, o_t, gp, x, w_cp, w_o, w_out):
    t = x.shape[0]
    tm = min(ROW_TILE_WIDE, t)

    def body(zs_ref, ot_ref, gp_ref, x_ref, wcp_ref, wo_ref, wout_ref, xo_ref, a_ref, b_ref, m_ref):
        a = _dot(zs_ref[...], wcp_ref[...])
        b = _dot_tn(ot_ref[...], wo_ref[...])
        a_ref[...] = a.astype(BF)
        b_ref[...] = b.astype(BF)
        merged = (_sig(gp_ref[:, :D].astype(F32)) * a + _sig(gp_ref[:, D:].astype(F32)) * b).astype(BF)
        m_ref[...] = merged
        xo_ref[...] = x_ref[...] + _dot(merged, wout_ref[...])

    return pl.pallas_call(
        body, grid=(t // tm,),
        in_specs=[_row_tile(tm, D), pl.BlockSpec((D, tm), lambda i: (0, i)), _row_tile(tm, 2 * D), _row_tile(tm, D),
                  _resident((D, D)), _resident((D, D)), _resident((D, D))],
        out_specs=[_row_tile(tm, D)] * 4,
        out_shape=[jax.ShapeDtypeStruct((t, D), F32)] + [jax.ShapeDtypeStruct((t, D), BF)] * 3,
        compiler_params=_params(1), name="mix_out")(zs, o_t, gp, x, w_cp, w_o, w_out)


def _mix_out_bwd(dx, a, b, gp, w_cp, w_o, w_out, deps=()):
    t = dx.shape[0]
    tm = min(ROW_TILE_WIDE, t)

    def body(dx_ref, a_ref, b_ref, gp_ref, wcp_ref, wo_ref, wout_ref, dzs_ref, dot_ref, dgp_ref, da_ref, db_ref, dxb_ref):
        dxb = dx_ref[...].astype(BF)
        dxb_ref[...] = dxb
        dm = _dot_nt(dxb, wout_ref[...])
        gc = _sig(gp_ref[:, :D].astype(F32))
        ga = _sig(gp_ref[:, D:].astype(F32))
        da = (dm * gc).astype(BF)
        db = (dm * ga).astype(BF)
        da_ref[...] = da
        db_ref[...] = db
        dgp_ref[:, :D] = (dm * a_ref[...].astype(F32) * gc * (1.0 - gc)).astype(BF)
        dgp_ref[:, D:] = (dm * b_ref[...].astype(F32) * ga * (1.0 - ga)).astype(BF)
        dzs_ref[...] = _dot_nt(da, wcp_ref[...])
        dot_ref[...] = _dot_nt(wo_ref[...], db).astype(BF)

    return _call(
        body, deps, (dx, a, b, gp, w_cp, w_o, w_out), grid=(t // tm,),
        in_specs=[_row_tile(tm, D), _row_tile(tm, D), _row_tile(tm, D), _row_tile(tm, 2 * D),
                  _resident((D, D)), _resident((D, D)), _resident((D, D))],
        out_specs=[_row_tile(tm, D), pl.BlockSpec((D, tm), lambda i: (0, i)), _row_tile(tm, 2 * D),
                   _row_tile(tm, D), _row_tile(tm, D), _row_tile(tm, D)],
        out_shape=[jax.ShapeDtypeStruct((t, D), F32), jax.ShapeDtypeStruct((D, t), BF), jax.ShapeDtypeStruct((t, 2 * D), BF),
                   jax.ShapeDtypeStruct((t, D), BF), jax.ShapeDtypeStruct((t, D), BF), jax.ShapeDtypeStruct((t, D), BF)],
        compiler_params=_params(1), name="mix_out_bwd")


def _mix_proj_bwd(dxo, duc, dq_t, ckv, qkv_t, kg, dgp, x, g, w_t):
    t = x.shape[0]
    tm = min(ROW_TILE_WIDE, t)
    per = tm // BLK
    steps = t // tm
    kv_rows = 2 * NKV * HD

    def body(dxo_ref, duc_ref, dq_ref, c_ref, cn_ref, k_ref, kg_ref, dgp_ref, x_ref, g_ref, w_ref,
             dx_ref, dg_ref, dkv_ref, dkg_ref, kg_scr):
        i = pl.program_id(0)

        @pl.when(i == 0)
        def _():
            dg_ref[...] = jnp.zeros_like(dg_ref)
            kg_scr[...] = jnp.zeros_like(kg_scr)

        kg_scr[...] += _kv_combine_tile(c_ref, cn_ref, (i < steps - 1).astype(F32), k_ref, kg_ref[...], dkv_ref)
        dn = _dot(duc_ref[...], w_ref[R_CONV[0]:R_CONV[1], :])
        dn = dn + _dot(dgp_ref[...], w_ref[R_GATE[0]:R_GATE[1], :])
        dn = dn + _dot_tn(dq_ref[...], w_ref[R_Q[0]:R_Q[1], :])
        dn = dn + _dot_tn(dkv_ref[...], w_ref[R_KV[0]:R_KV[1], :])
        dx, dg = _rms_bwd(dn, x_ref[...], g_ref[...])
        dx_ref[...] = dxo_ref[...] + dx
        dg_ref[...] += dg

        @pl.when(i == steps - 1)
        def _():
            dkg_ref[...] = jnp.sum(kg_scr[...], axis=1, keepdims=True)

    return pl.pallas_call(
        body, grid=(steps,),
        in_specs=[_row_tile(tm, D), _row_tile(tm, 2 * D), pl.BlockSpec((D, tm), lambda i: (0, i)),
                  pl.BlockSpec((per, kv_rows, 2 * BLK), lambda i: (i, 0, 0)),
                  pl.BlockSpec((1, kv_rows, 2 * BLK), lambda i: (jnp.minimum((i + 1) * per, t // BLK - 1), 0, 0)),
                  pl.BlockSpec((NKV * HD, tm), lambda i: (D // (NKV * HD), i)), _resident((HD, 1)),
                  _row_tile(tm, 2 * D), _row_tile(tm, D), _resident((1, D)), _resident((INW, D))],
        out_specs=[_row_tile(tm, D), pl.BlockSpec((1, D), lambda i: (0, 0)), pl.BlockSpec((kv_rows, tm), lambda i: (0, i)),
                   pl.BlockSpec((HD, 1), lambda i: (0, 0))],
        out_shape=[jax.ShapeDtypeStruct((t, D), F32), jax.ShapeDtypeStruct((1, D), F32),
                   jax.ShapeDtypeStruct((kv_rows, t), BF), jax.ShapeDtypeStruct((HD, 1), F32)],
        scratch_shapes=[pltpu.VMEM((HD, BLK), F32)],
        compiler_params=_params(1), name="mix_proj_bwd")(dxo, duc, dq_t, ckv, ckv, qkv_t, kg, dgp, x, g, w_t)


def _attention_tables():
    kj = np.arange(2 * BLK)[:, None]
    qi = np.arange(BLK)[None, :]
    dist = qi + BLK - kj
    in_win = (dist >= 0) & (dist < BLK)
    dpos = np.maximum(dist, 0)
    max_exact = NBUCKET // 2
    dfl = np.maximum(dpos, 1).astype(np.float32)
    large = max_exact + (np.log(dfl / np.float32(max_exact)) / np.float32(math.log(BLK / max_exact))
                         * np.float32(NBUCKET - max_exact)).astype(np.int32)
    large = np.minimum(large, NBUCKET - 1)
    bucket = np.where(dpos < max_exact, dpos, large)
    onehot = (bucket[None] == np.arange(NBUCKET)[:, None, None]).astype(np.float32)
    mask = in_win.astype(np.float32)
    mask_first = mask * (kj >= BLK)
    masks = np.stack([np.tile(mask, (1, GRP)), np.tile(mask_first, (1, GRP))])
    return onehot, masks


def _bias_table(rel_bias, onehot):
    tab = jnp.einsum("bkq,bh->hkq", onehot, rel_bias, precision=lax.Precision.HIGHEST)
    tab = tab.reshape(NKV, GRP, 2 * BLK, BLK)
    return jnp.transpose(tab, (0, 2, 1, 3)).reshape(NKV, 2 * BLK, GRP * BLK)


def _local_step(x, target, vec, ffn1_up, weights_of, wgrad, grads_done, small_done):
    onehot_np, masks_np = _attention_tables()
    onehot = jnp.asarray(onehot_np)
    masks = jnp.asarray(masks_np)
    bias_t = jnp.where(masks[:, None] > 0.5, _bias_table(vec["rel_bias"], onehot)[None], NEG)
    sink_rows = jnp.repeat(vec["attn_sinks"].reshape(NKV, 1, GRP), BLK, axis=2)
    qg = vec["q_norm"].reshape(HD, 1)
    kg = vec["k_norm"].reshape(HD, 1)
    g1 = vec["ffn1_norm"].reshape(1, D)
    gm = vec["mix_norm"].reshape(1, D)
    g2 = vec["ffn2_norm"].reshape(1, D)
    dwb = vec["conv_dw_bias"].reshape(1, D)
    lng = vec["conv_ln_g"].reshape(1, D)
    lnb = vec["conv_ln_b"].reshape(1, D)

    n1, u1, w_in1 = ffn1_up(x, g1, (bias_t, sink_rows))
    w1 = dict(weights_of("ffn1_out", (u1,)), ffn1_w_in=w_in1)
    x1 = _ffn_down(x, u1, w1["ffn1_w_out"], "ffn1_down")
    wm = weights_of("mix_proj", (x1,))
    dwk = jnp.pad(wm["conv_dw_kernel"], ((0, CWP - CW), (0, 0)))
    hm, uc, gp, qkv_t = _mix_proj(x1, gm, wm["w_in"])
    zs, zc = _conv_fwd(uc, dwk, dwb, lng, lnb)
    wm.update(weights_of("mix_merge", (zs,)))
    o_t, probs, sink_probs = _attn_fwd(qkv_t, qg, kg, sink_rows, bias_t)
    x2, a, b, merged = _mix_out(zs, o_t, gp, x1, wm["conv_w_proj"], wm["attn_w_o"], wm["w_out"])
    w2 = weights_of("ffn2", (x2,))
    gv = {}
    n2, du2, h2, dy2, dx2, sq, gv["ffn2_norm"] = _ffn_last(x2, target, g2, w2["ffn2_w_in"], w2["ffn2_w_out"], "ffn2")

    deps = grads_done("ffn2", {"ffn2_w_in": wgrad(du2, n2, "ffn2_dw_in", False),
                               "ffn2_w_out": wgrad(h2, dy2, "ffn2_dw_out", False)})

    dzs, do_t, dgp, da, db, dx2b = _mix_out_bwd(dx2, a, b, gp, wm["conv_w_proj"], wm["attn_w_o"], wm["w_out"], deps=deps)
    deps = grads_done("mix_out", {"w_out": wgrad(merged, dx2b, "mix_dw_out", False),
                                  "conv_w_proj": wgrad(zs, da, "mix_dw_cp", False),
                                  "attn_w_o": wgrad(o_t, db, "mix_dw_o", True)})

    dq_t, ckv, dqg, dsink, dbias = _attn_bwd(qkv_t, do_t, probs, sink_probs, qg, kg, onehot, deps=deps)
    gv["q_norm"] = dqg.reshape(HD)
    gv["attn_sinks"] = dsink[:, :GRP].reshape(NQ)
    gv["rel_bias"] = dbias[:, :, :GRP].reshape(NBUCKET, NQ)

    duc, dk_conv, gv["conv_dw_bias"], gv["conv_ln_g"], gv["conv_ln_b"] = _conv_bwd(uc, zc, dzs, dwk, lng, lnb)
    gv["conv_dw_kernel"] = dk_conv[:CW]

    dx1, gv["mix_norm"], dkv_t, dkg = _mix_proj_bwd(dx2, duc, dq_t, ckv, qkv_t, kg, dgp, x1, gm, wm["w_in"])
    gv["k_norm"] = dkg.reshape(HD)
    deps = grads_done("mix_in", {"w_in": _wgrad_mix(duc, dq_t, dkv_t, dgp, hm)})

    dx0, du1, h1, dy1, gv["ffn1_norm"] = _ffn_bwd(dx1, x, g1, u1, w1["ffn1_w_in"], w1["ffn1_w_out"], "ffn1_bwd", deps=deps)
    for k in ("ffn1_norm", "mix_norm", "ffn2_norm", "conv_dw_bias", "conv_ln_g", "conv_ln_b"):
        gv[k] = gv[k].reshape(D)
    deps = small_done(gv, sq)
    deps = grads_done("ffn1_in", {"ffn1_w_in": wgrad(du1, n1, "ffn1_dw_in", False, deps)})
    grads_done("ffn1_out", {"ffn1_w_out": wgrad(h1, dy1, "ffn1_dw_out", False, deps)})
    return dx0


MESH_ID = pl.DeviceIdType.MESH


def _position():
    return lax.axis_index("x"), lax.axis_index("y"), lax.axis_index("c")


def _shard_rows(ref, index, rows):
    return ref.at[pl.ds(pl.multiple_of(index * rows, 16), rows), :]


def _prep(weights, taps, me, name, deps=()):
    n = len(weights)
    n_deps = len(deps)
    with_taps = taps is not None

    def body(me_ref, *refs):
        refs = refs[n_deps:]
        ins, outs = refs[:len(refs) // 2], refs[len(refs) // 2:]
        for k in range(n):
            outs[k][...] = ins[k][...].astype(BF)
        if with_taps:
            outs[n][0:CW, :] = ins[n][...]
            outs[n][CW:, :] = jnp.zeros((CWP - CW, BLK), F32)

    shard_shapes = [w.shape for w in weights] + [(CWP, BLK)] * with_taps
    dtypes = [BF] * n + [F32] * with_taps
    ins = list(weights) + [taps] * with_taps
    return pl.pallas_call(
        body,
        grid_spec=pltpu.PrefetchScalarGridSpec(
            num_scalar_prefetch=1, grid=(1,),
            in_specs=[ANY] * n_deps + [pl.BlockSpec(a.shape, lambda i, m: (0, 0), pipeline_mode=pl.Buffered(1)) for a in ins],
            out_specs=[pl.BlockSpec(s, lambda i, m: (m[0], 0)) for s in shard_shapes]),
        out_shape=[jax.ShapeDtypeStruct((N_DEV * s[0], s[1]), d) for s, d in zip(shard_shapes, dtypes)],
        compiler_params=_params(1), name=name)(me, *deps, *ins)


HBM = pl.BlockSpec(memory_space=pltpu.HBM)
SEM = pl.BlockSpec(memory_space=pltpu.SEMAPHORE)
DATAFLOW = pltpu.SideEffectType.DATAFLOW_SIDE_EFFECTING
TOKEN = jax.ShapeDtypeStruct((8, 128), F32)


def _in_hbm(x):
    return pltpu.with_memory_space_constraint(x, pltpu.HBM)


def _hbm_like(arrays):
    return [pltpu.HBM(a.shape, a.dtype) for a in arrays]


def _other_chips(x, y):
    return [(1 - x, y), (x, 1 - y), (1 - x, 1 - y)]


def _device_index(chip, c):
    return 4 * chip[0] + 2 * chip[1] + c


def _chip_index(chip):
    return 2 * chip[0] + chip[1]


class _Exchange:
    def __init__(self, gather, all_cores=False):
        self.gather = gather
        self.all_cores = all_cores
        self.n_peers = N_DEV - 1 if all_cores else 3

    def peers(self, x, y, c):
        if self.all_cores:
            return [(x ^ (k >> 2), y ^ ((k >> 1) & 1), c ^ (k & 1)) for k in range(1, N_DEV)]
        return [(*chip, c) for chip in _other_chips(x, y)]

    def sent(self, x, y, c, peer):
        return _device_index((x, y), c) if self.gather else _chip_index(peer[:2])

    def lands_at(self, x, y, c):
        return _device_index((x, y), c) if self.gather else _chip_index((x, y))

    def arrives_at(self, peer):
        return _device_index(peer[:2], peer[2]) if self.gather else _chip_index(peer[:2])


def _ici_copies_start(sets, sources, landings, exchanges, name, deps=()):
    n = len(landings)
    arrays = (list(sources) if sources is not None else []) + list(landings)
    first_land = len(arrays) - n
    n_sets = len(sets)
    n_deps = len(deps)

    def body(*refs):
        refs = refs[n_deps:]
        src, land = refs[:n], refs[first_land:first_land + n]
        sems = refs[len(arrays):len(arrays) + 2 * n_sets]
        token = refs[-1]
        x, y, c = _position()
        for s, (members, exchange) in enumerate(zip(sets, exchanges)):
            for slot, (k, rows) in enumerate(members):
                for j, peer in enumerate(exchange.peers(x, y, c)):
                    at = exchange.n_peers * slot + j
                    pltpu.make_async_remote_copy(
                        src_ref=_shard_rows(src[k], exchange.sent(x, y, c, peer), rows),
                        dst_ref=_shard_rows(land[k], exchange.lands_at(x, y, c), rows),
                        send_sem=sems[2 * s].at[at], recv_sem=sems[2 * s + 1].at[at],
                        device_id=peer, device_id_type=MESH_ID).start()
        token[...] = jnp.zeros_like(token)

    sem_shapes = []
    for members, exchange in zip(sets, exchanges):
        sem_shapes += [pltpu.SemaphoreType.DMA((exchange.n_peers * len(members),))] * 2
    out = pl.pallas_call(
        body, name=name,
        out_shape=sem_shapes + _hbm_like(arrays) + [TOKEN],
        in_specs=[ANY] * n_deps + [HBM] * len(arrays),
        out_specs=[SEM] * (2 * n_sets) + [HBM] * len(arrays) + [pl.BlockSpec(memory_space=pltpu.VMEM)],
        input_output_aliases={n_deps + i: 2 * n_sets + i for i in range(len(arrays))},
        compiler_params=pltpu.CompilerParams(has_side_effects=DATAFLOW),
    )(*deps, *[_in_hbm(a) for a in arrays])
    sems = [(out[2 * s], out[2 * s + 1]) for s in range(n_sets)]
    thru = list(out[2 * n_sets:2 * n_sets + len(arrays)])
    return sems, (thru[:first_land] if sources is not None else None), thru[first_land:], out[-1]


def _ici_copies_wait(sems, members, sources, landings, exchange, after, name):
    n = len(landings)
    arrays = (list(sources) if sources is not None else []) + list(landings)
    first_land = len(arrays) - n

    def body(*refs):
        src, land = refs[:n], refs[first_land:first_land + n]
        send_sems, recv_sems = refs[len(arrays)], refs[len(arrays) + 1]
        x, y, c = _position()
        for slot, rows in enumerate(members):
            for j, peer in enumerate(exchange.peers(x, y, c)):
                at = exchange.n_peers * slot + j
                cp = pltpu.make_async_remote_copy(
                    src_ref=_shard_rows(src[slot], exchange.sent(x, y, c, peer), rows),
                    dst_ref=_shard_rows(land[slot], exchange.arrives_at(peer), rows),
                    send_sem=send_sems.at[at], recv_sem=recv_sems.at[at], device_id=peer, device_id_type=MESH_ID)
                cp.wait_send()
                cp.wait_recv()

    out = pl.pallas_call(
        body, name=name, out_shape=_hbm_like(arrays),
        in_specs=[HBM] * len(arrays) + [SEM, SEM] + [ANY] * len(after), out_specs=[HBM] * len(arrays),
        input_output_aliases={i: i for i in range(len(arrays))},
        compiler_params=pltpu.CompilerParams(has_side_effects=DATAFLOW),
    )(*arrays, sems[0], sems[1], *after)
    return list(out[first_land:])


def _d2d_gather(buffers, rows, name, which=(0, 1, 2, 3), deps=()):
    n = len(buffers)
    n_deps = len(deps)

    def body(*refs):
        refs = refs[n_deps:]
        land = refs[n:2 * n]
        send_sems, recv_sems = refs[2 * n:]
        x, y, c = _position()
        chips = [([(x, y)] + _other_chips(x, y))[j] for j in which]
        sends, recvs = [], []
        for k in range(n):
            for j, chip in enumerate(chips):
                for copies, core in ((sends, c), (recvs, 1 - c)):
                    block = _shard_rows(land[k], _device_index(chip, core), rows[k])
                    copies.append(pltpu.make_async_remote_copy(
                        src_ref=block, dst_ref=block, send_sem=send_sems.at[k, j], recv_sem=recv_sems.at[k, j],
                        device_id=(x, y, 1 - c), device_id_type=MESH_ID))
        for cp in sends:
            cp.start()
        for cp in recvs:
            cp.wait_recv()
        for cp in sends:
            cp.wait_send()

    return pl.pallas_call(
        body, name=name, out_shape=[jax.ShapeDtypeStruct(a.shape, a.dtype) for a in buffers],
        in_specs=[ANY] * (n_deps + n), out_specs=[ANY] * n, input_output_aliases={n_deps + i: i for i in range(n)},
        scratch_shapes=[pltpu.SemaphoreType.DMA((n, len(which))), pltpu.SemaphoreType.DMA((n, len(which)))],
    )(*deps, *buffers)


def _rs_pair(grads, name):
    n = len(grads)
    rows = [g.shape[0] // N_DEV for g in grads]

    def body(*refs):
        ins, outs = refs[:n], refs[n:2 * n]
        send_sems, recv_sems = refs[2 * n:]
        x, y, c = _position()
        copies = []
        for k in range(n):
            for q in range(4):
                copies.append(pltpu.make_async_remote_copy(
                    src_ref=_shard_rows(ins[k], 2 * q + 1 - c, rows[k]), dst_ref=_shard_rows(outs[k], q, rows[k]),
                    send_sem=send_sems.at[k, q], recv_sem=recv_sems.at[k, q], device_id=(x, y, 1 - c),
                    device_id_type=MESH_ID))
        for cp in copies:
            cp.start()
        for cp in copies:
            cp.wait()

    return pl.pallas_call(
        body, out_shape=[jax.ShapeDtypeStruct((4 * r, g.shape[1]), g.dtype) for g, r in zip(grads, rows)],
        in_specs=[ANY] * n, out_specs=[ANY] * n,
        scratch_shapes=[pltpu.SemaphoreType.DMA((n, 4)), pltpu.SemaphoreType.DMA((n, 4))],
        name=name)(*grads)


def _wgrad_pair(lhs, rhs, name, *, lhs_is_transposed, deps=()):
    t = rhs.shape[0]
    n = lhs.shape[0] if lhs_is_transposed else lhs.shape[1]
    r = n // N_DEV
    n_chips = N_DEV // 2
    per = 1 if (2 * r) % BLK == 0 else 2
    steps = n_chips // per

    def body(l_ref, r_ref, kept_ref, recv_ref, res, send_sems, recv_sems):
        q = pl.program_id(0)
        slot = q % 2
        x, y, c = _position()

        def send(step, buf, i):
            return pltpu.make_async_remote_copy(
                src_ref=res.at[buf, pl.ds(pl.multiple_of((2 * i + 1 - c) * r, 16), r), :],
                dst_ref=_shard_rows(recv_ref, step * per + i, r),
                send_sem=send_sems.at[buf, i], recv_sem=recv_sems.at[step * per + i],
                device_id=(x, y, 1 - c), device_id_type=MESH_ID)

        @pl.when(q >= 2)
        def _():
            for i in range(per):
                send(q - 2, slot, i).wait_send()

        if lhs_is_transposed:
            res[slot] = _dot(l_ref[...], r_ref[...]).astype(BF)
        else:
            res[slot] = _dot_tn(l_ref[...], r_ref[...]).astype(BF)
        for i in range(per):
            kept_ref[i * r:(i + 1) * r, :] = res[slot, pl.ds(pl.multiple_of((2 * i + c) * r, 16), r), :]
            send(q, slot, i).start()

        @pl.when(q == steps - 1)
        def _():
            for i in range(per):
                if steps > 1:
                    send(q - 1, 1 - slot, i).wait_send()
                send(q, slot, i).wait_send()
            for chip in range(n_chips):
                send(chip // per, 0, chip % per).wait_recv()

    width = 2 * r * per
    lhs_spec = pl.BlockSpec((width, t), lambda q: (q, 0)) if lhs_is_transposed else pl.BlockSpec((t, width), lambda q: (0, q))
    return _call(
        body, deps, (lhs, rhs), grid=(steps,),
        in_specs=[lhs_spec, _resident((t, D))],
        out_specs=[pl.BlockSpec((per * r, D), lambda q: (q, 0)), ANY],
        out_shape=[jax.ShapeDtypeStruct((n // 2, D), BF)] * 2,
        scratch_shapes=[pltpu.VMEM((2, width, D), BF), pltpu.SemaphoreType.DMA((2, per)),
                        pltpu.SemaphoreType.DMA((n_chips,))],
        compiler_params=_params(1), name=name)


def _wgrad_pair_sum(lhs, rhs, place, name, *, lhs_is_transposed, deps=()):
    t = rhs.shape[0]
    n = lhs.shape[0] if lhs_is_transposed else lhs.shape[1]
    r = n // N_DEV
    n_chips = N_DEV // 2
    per = 1 if (2 * r) % BLK == 0 else 2
    steps = n_chips // per
    n_deps = len(deps)

    def body(place_ref, *refs):
        l_ref, r_ref, part_ref, land_ref, res, inbox, send_sems, recv_sems = refs[n_deps:]
        q = pl.program_id(0)
        slot = q % 2
        x, y, c = _position()

        def send(step, buf, i):
            return pltpu.make_async_remote_copy(
                src_ref=res.at[buf, pl.ds(pl.multiple_of((2 * i + 1 - c) * r, 16), r), :], dst_ref=inbox.at[step * per + i],
                send_sem=send_sems.at[buf, i], recv_sem=recv_sems.at[step * per + i],
                device_id=(x, y, 1 - c), device_id_type=MESH_ID)

        @pl.when(q < steps)
        def _():
            @pl.when(q >= 2)
            def _():
                for i in range(per):
                    send(q - 2, slot, i).wait_send()

            if lhs_is_transposed:
                res[slot] = _dot(l_ref[...], r_ref[...]).astype(BF)
            else:
                res[slot] = _dot_tn(l_ref[...], r_ref[...]).astype(BF)
            for i in range(per):
                send(q, slot, i).start()

        @pl.when(q >= 1)
        def _():
            for i in range(per):
                chip = (q - 1) * per + i
                send(q - 1, 1 - slot, i).wait_recv()
                kept = res[1 - slot, pl.ds(pl.multiple_of((2 * i + c) * r, 16), r), :]
                total = (kept.astype(F32) + inbox[chip].astype(F32)).astype(BF)
                part_ref[i * r:(i + 1) * r, :] = total

                @pl.when(chip == place_ref[1])
                def _():
                    land_ref[...] = total

        @pl.when(q == steps)
        def _():
            for i in range(per):
                if steps > 1:
                    send(q - 2, slot, i).wait_send()
                send(q - 1, 1 - slot, i).wait_send()

    width = 2 * r * per
    last = steps - 1
    if lhs_is_transposed:
        lhs_spec = pl.BlockSpec((width, t), lambda q, p: (jnp.minimum(q, last), 0))
    else:
        lhs_spec = pl.BlockSpec((t, width), lambda q, p: (0, jnp.minimum(q, last)))
    return pl.pallas_call(
        body,
        grid_spec=pltpu.PrefetchScalarGridSpec(
            num_scalar_prefetch=1, grid=(steps + 1,),
            in_specs=[ANY] * n_deps + [lhs_spec, pl.BlockSpec((t, D), lambda q, p: (0, 0), pipeline_mode=pl.Buffered(1))],
            out_specs=[pl.BlockSpec((per * r, D), lambda q, p: (jnp.maximum(q - 1, 0), 0)),
                       pl.BlockSpec((r, D), lambda q, p: (p[1], 0))],
            scratch_shapes=[pltpu.VMEM((2, width, D), BF), pltpu.VMEM((n_chips, r, D), BF),
                            pltpu.SemaphoreType.DMA((2, per)), pltpu.SemaphoreType.DMA((n_chips,))]),
        out_shape=[jax.ShapeDtypeStruct((n // 2, D), BF)] * 2,
        compiler_params=_params(1), name=name)(place, *deps, lhs, rhs)


def _pair_add(grad, received, place, name, kept_only=False):
    r = received.shape[0] // 4
    parity = 0 if kept_only else 1

    def body(place_ref, g_ref, r_ref, o_ref, land_ref):
        total = (g_ref[...].astype(F32) + r_ref[...].astype(F32)).astype(BF)
        o_ref[...] = total

        @pl.when(pl.program_id(0) == place_ref[1])
        def _():
            land_ref[...] = total

    return pl.pallas_call(
        body,
        grid_spec=pltpu.PrefetchScalarGridSpec(
            num_scalar_prefetch=1, grid=(4,),
            in_specs=[pl.BlockSpec((r, D), lambda q, p: ((1 + parity) * q + parity * p[0], 0)),
                      pl.BlockSpec((r, D), lambda q, p: (q, 0))],
            out_specs=[pl.BlockSpec((r, D), lambda q, p: (q, 0)), pl.BlockSpec((r, D), lambda q, p: (p[1], 0))]),
        out_shape=[jax.ShapeDtypeStruct(received.shape, BF)] * 2,
        compiler_params=_params(1), name=name)(place, grad, received)


def _sum_blocks(gathered, rows):
    def body(b_ref, o_ref):
        acc = b_ref[0:rows, :]
        for d in range(1, N_DEV):
            acc = acc + b_ref[d * rows:(d + 1) * rows, :]
        o_ref[...] = acc

    return pl.pallas_call(body, out_shape=jax.ShapeDtypeStruct((rows, D), F32), name="small_sum")(gathered)


def _adamw_math(w, g, m, v):
    m = ADAM_B1 * m + (1.0 - ADAM_B1) * g
    v = ADAM_B2 * v + (1.0 - ADAM_B2) * (g * g)
    m_hat = m / (1.0 - ADAM_B1 ** ADAM_STEP)
    v_hat = v / (1.0 - ADAM_B2 ** ADAM_STEP)
    delta = -ADAM_LR * (m_hat / (jnp.sqrt(v_hat) + ADAM_EPS) + ADAM_WD * w)
    return delta, m, v


def _sum_partials(blocks):
    g = blocks[0].astype(F32)
    for blk in blocks[1:]:
        g = g + blk.astype(F32)
    return g


def _reduce_adamw(landed, w, m, v, name):
    r = w.shape[0]
    tr = 352 if r % 352 == 0 else r
    per = r // tr

    def body(r0, r1, r2, r3, w_ref, m_ref, v_ref, g_ref, d_ref, nm_ref, nv_ref):
        g = _sum_partials([r0[...], r1[...], r2[...], r3[...]])
        g_ref[...] = g
        d_ref[...], nm_ref[...], nv_ref[...] = _adamw_math(w_ref[...], g, m_ref[...], v_ref[...])

    tile = _row_tile(tr, D)
    return pl.pallas_call(
        body, grid=(per,),
        in_specs=[pl.BlockSpec((tr, D), lambda i, q=q: (q * per + i, 0)) for q in range(4)] + [tile] * 3,
        out_specs=[tile] * 4, out_shape=[jax.ShapeDtypeStruct(w.shape, F32)] * 4,
        compiler_params=_params(1), name=name)(landed, landed, landed, landed, w, m, v)


def _adamw_small(w, g, m, v, name):
    def body(w_ref, g_ref, m_ref, v_ref, d_ref, nm_ref, nv_ref):
        d_ref[...], nm_ref[...], nv_ref[...] = _adamw_math(w_ref[...], g_ref[...], m_ref[...], v_ref[...])

    return pl.pallas_call(body, out_shape=[jax.ShapeDtypeStruct(w.shape, F32)] * 3, name=name)(w, g, m, v)


WEIGHTS = ("ffn1_norm", "ffn1_w_in", "ffn1_w_out", "mix_norm", "w_in", "conv_dw_kernel", "conv_dw_bias", "conv_ln_g",
           "conv_ln_b", "conv_w_proj", "q_norm", "k_norm", "attn_sinks", "rel_bias", "attn_w_o", "w_out", "ffn2_norm",
           "ffn2_w_in", "ffn2_w_out")
MATRICES = ("ffn1_w_in", "ffn1_w_out", "w_in", "conv_w_proj", "attn_w_o", "w_out", "ffn2_w_in", "ffn2_w_out")
COLUMN_SHARDED = ("ffn1_w_in", "w_in", "ffn2_w_in")
ROW_VECTORS = ("ffn1_norm", "mix_norm", "conv_dw_bias", "conv_ln_g", "conv_ln_b", "ffn2_norm")
PACKED = (("q_norm", HD), ("k_norm", HD), ("attn_sinks", NQ), ("rel_bias", NBUCKET * NQ))
GATHER = _Exchange(gather=True)
GATHER_ALL = _Exchange(gather=True, all_cores=True)
SCATTER = _Exchange(gather=False)
FIRST = "ffn1_w_in"
GATHER_STAGES = ("ffn1_out", "mix_proj", "mix_merge", "ffn2")
STAGE_GATHER = {"ffn1_out": GATHER, "mix_proj": GATHER, "mix_merge": GATHER, "ffn2": GATHER_ALL}
STAGE_MEMBERS = {"ffn1_out": ("ffn1_w_out",),
                 "mix_proj": ("w_in", "taps"), "mix_merge": ("conv_w_proj", "attn_w_o", "w_out"),
                 "ffn2": ("ffn2_w_in", "ffn2_w_out")}
ROW_PACKED = len(ROW_VECTORS)
ROW_LOSS = ROW_PACKED + 1
ROW_TAPS = 8
PAYLOAD_ROWS = 48


def _pack_small(values, last_row):
    packed = jnp.concatenate([values[k].reshape(-1) for k, _ in PACKED])
    packed = jnp.pad(packed, (0, D - packed.shape[0])).reshape(1, D)
    return jnp.concatenate([values[k].reshape(1, D) for k in ROW_VECTORS] + [packed, last_row], axis=0)


def _unpack_small(rows):
    out = {k: rows[i] for i, k in enumerate(ROW_VECTORS)}
    at = 0
    for k, size in PACKED:
        out[k] = rows[ROW_PACKED, at:at + size]
        at += size
    out["rel_bias"] = out["rel_bias"].reshape(NBUCKET, NQ)
    return out


def kernel(x, ffn1_norm, ffn1_w_in, ffn1_w_out, mix_norm, w_in, conv_dw_kernel, conv_dw_bias, conv_ln_g, conv_ln_b, conv_w_proj, q_norm, k_norm, attn_sinks, rel_bias, attn_w_o, w_out, ffn2_norm, ffn2_w_in, ffn2_w_out, loss_target, m_ffn1_norm, m_ffn1_w_in, m_ffn1_w_out, m_mix_norm, m_w_in, m_conv_dw_kernel, m_conv_dw_bias, m_conv_ln_g, m_conv_ln_b, m_conv_w_proj, m_q_norm, m_k_norm, m_attn_sinks, m_rel_bias, m_attn_w_o, m_w_out, m_ffn2_norm, m_ffn2_w_in, m_ffn2_w_out, v_ffn1_norm, v_ffn1_w_in, v_ffn1_w_out, v_mix_norm, v_w_in, v_conv_dw_kernel, v_conv_dw_bias, v_conv_ln_g, v_conv_ln_b, v_conv_w_proj, v_q_norm, v_k_norm, v_attn_sinks, v_rel_bias, v_attn_w_o, v_w_out, v_ffn2_norm, v_ffn2_w_in, v_ffn2_w_out):
    w = dict(ffn1_norm=ffn1_norm, ffn1_w_in=ffn1_w_in, ffn1_w_out=ffn1_w_out, mix_norm=mix_norm, w_in=w_in,
             conv_dw_kernel=conv_dw_kernel, conv_dw_bias=conv_dw_bias, conv_ln_g=conv_ln_g, conv_ln_b=conv_ln_b,
             conv_w_proj=conv_w_proj, q_norm=q_norm, k_norm=k_norm, attn_sinks=attn_sinks, rel_bias=rel_bias,
             attn_w_o=attn_w_o, w_out=w_out, ffn2_norm=ffn2_norm, ffn2_w_in=ffn2_w_in, ffn2_w_out=ffn2_w_out)
    m = dict(ffn1_norm=m_ffn1_norm, ffn1_w_in=m_ffn1_w_in, ffn1_w_out=m_ffn1_w_out, mix_norm=m_mix_norm, w_in=m_w_in,
             conv_dw_kernel=m_conv_dw_kernel, conv_dw_bias=m_conv_dw_bias, conv_ln_g=m_conv_ln_g, conv_ln_b=m_conv_ln_b,
             conv_w_proj=m_conv_w_proj, q_norm=m_q_norm, k_norm=m_k_norm, attn_sinks=m_attn_sinks, rel_bias=m_rel_bias,
             attn_w_o=m_attn_w_o, w_out=m_w_out, ffn2_norm=m_ffn2_norm, ffn2_w_in=m_ffn2_w_in, ffn2_w_out=m_ffn2_w_out)
    v = dict(ffn1_norm=v_ffn1_norm, ffn1_w_in=v_ffn1_w_in, ffn1_w_out=v_ffn1_w_out, mix_norm=v_mix_norm, w_in=v_w_in,
             conv_dw_kernel=v_conv_dw_kernel, conv_dw_bias=v_conv_dw_bias, conv_ln_g=v_conv_ln_g, conv_ln_b=v_conv_ln_b,
             conv_w_proj=v_conv_w_proj, q_norm=v_q_norm, k_norm=v_k_norm, attn_sinks=v_attn_sinks, rel_bias=v_rel_bias,
             attn_w_o=v_attn_w_o, w_out=v_w_out, ffn2_norm=v_ffn2_norm, ffn2_w_in=v_ffn2_w_in, ffn2_w_out=v_ffn2_w_out)
    px, py, pc = _position()
    me = 4 * px + 2 * py + pc
    place = jnp.stack([pc, 2 * px + py]).astype(jnp.int32)

    rows_of = lambda k, a: a.T if k in COLUMN_SHARDED else a
    me1 = me.astype(jnp.int32).reshape(1)
    rest = tuple(k for k in MATRICES if k != FIRST)
    shard_rows = dict({k: rows_of(k, w[k]).shape[0] for k in MATRICES}, taps=CWP)
    sems_first, _, thru_first, token = _ici_copies_start(
        [[(0, shard_rows[FIRST])]], None, _prep([rows_of(FIRST, w[FIRST])], None, me1, "prep_first"), [GATHER],
        "gather_start_first")
    buffers = dict(zip(rest + ("taps",), _prep([rows_of(k, w[k]) for k in rest], conv_dw_kernel, me1, "prep", deps=[token])))
    landings, sets = [], []
    for stage in GATHER_STAGES:
        sets.append([(len(landings) + i, shard_rows[k]) for i, k in enumerate(STAGE_MEMBERS[stage])])
        landings += list(STAGE_MEMBERS[stage])
    sems, _, land_thru, started = _ici_copies_start(sets, None, [buffers[k] for k in landings],
                                                    [STAGE_GATHER[s] for s in GATHER_STAGES], "gather_start")

    def ffn1_up(x, g, after):
        chips = jnp.stack([_chip_index(chip) for chip in [(px, py)] + _other_chips(px, py)]).astype(jnp.int32)
        rows = [shard_rows[FIRST]]
        mine = _d2d_gather(thru_first, rows, "gather_d2d_first_mine", which=(0,), deps=[started])
        n, u = _ffn_up_blocks(x, g, None, mine[0], chips[:1], None, "ffn1_up_mine")
        landed = _ici_copies_wait(sems_first[0], rows, None, mine, GATHER, [u, *after], "gather_wait_first")
        w_in_t, = _d2d_gather(landed, rows, "gather_d2d_first", which=(1, 2, 3))
        return (*_ffn_up_blocks(None, None, n, w_in_t, chips[1:], u, "ffn1_up"), w_in_t)

    def weights_of(stage, after):
        s = GATHER_STAGES.index(stage)
        rows = [r for _, r in sets[s]]
        landed = _ici_copies_wait(sems[s], rows, None, [land_thru[k] for k, _ in sets[s]], STAGE_GATHER[stage],
                                  list(after), "gather_wait_" + stage)
        if not STAGE_GATHER[stage].all_cores:
            landed = _d2d_gather(landed, rows, "gather_d2d_" + stage)
        out = dict(zip(STAGE_MEMBERS[stage], landed))
        if "taps" in out:
            taps = out.pop("taps")
            out["conv_dw_kernel"] = jnp.transpose(taps.reshape(N_DEV, CWP, BLK), (1, 0, 2)).reshape(CWP, D)[:CW]
        return out

    in_flight = []

    def wgrad(lhs, rhs, name, lhs_is_transposed, deps=()):
        rows = (lhs.shape[0] if lhs_is_transposed else lhs.shape[1]) // N_DEV
        if rows <= WGRAD_SUM_MAX_ROWS:
            return ("summed",) + tuple(_wgrad_pair_sum(lhs, rhs, place, name, lhs_is_transposed=lhs_is_transposed, deps=deps))
        return ("paired",) + tuple(_wgrad_pair(lhs, rhs, name, lhs_is_transposed=lhs_is_transposed, deps=deps))

    def grads_done(stage, grads):
        names = list(grads)
        added = []
        for k in names:
            if not isinstance(grads[k], tuple):
                received, = _rs_pair([grads[k]], "rs_pair_" + k)
                added.append(_pair_add(grads[k], received, place, "pair_add_" + k))
            elif grads[k][0] == "paired":
                added.append(_pair_add(grads[k][1], grads[k][2], place, "pair_add_" + k, kept_only=True))
            else:
                added.append(grads[k][1:])
        partials = [p for p, _ in added]
        members = [(i, p.shape[0] // 4) for i, p in enumerate(partials)]
        sem, p_thru, l_thru, token = _ici_copies_start([members], partials, [l for _, l in added], [SCATTER],
                                                       "scatter_start_" + stage)
        in_flight.append((stage, names, sem[0], p_thru, l_thru, token))
        return [token]

    small = []

    def small_done(gv, sq):
        payload = jnp.concatenate([_pack_small(gv, sq), jnp.pad(gv["conv_dw_kernel"], ((0, PAYLOAD_ROWS - ROW_TAPS - CW), (0, 0)))],
                                  axis=0)
        mine = lax.dynamic_update_slice_in_dim(lax.empty((N_DEV * PAYLOAD_ROWS, D), F32), payload, me * PAYLOAD_ROWS, axis=0)
        sems, _, thru, token = _ici_copies_start([[(0, PAYLOAD_ROWS)]], None, [mine], [GATHER_ALL], "small_start")
        small.append((sems[0], thru))
        return [token]

    vec = {k: w[k] for k in WEIGHTS if k not in MATRICES and k != "conv_dw_kernel"}
    dx0 = _local_step(x[0], loss_target[0], vec, ffn1_up, weights_of, wgrad, grads_done, small_done)
    gathered, = _ici_copies_wait(small[0][0], [PAYLOAD_ROWS], None, small[0][1], GATHER_ALL, [in_flight[-1][-1]], "small_wait")
    total = _sum_blocks(gathered, PAYLOAD_ROWS)
    loss = (0.5 / D) * jnp.sum(total[ROW_LOSS])

    grads, delta, new_m, new_v = {}, {}, {}, {}
    after = [total]
    for stage, names, sem, p_thru, l_thru, _ in in_flight:
        landed = _ici_copies_wait(sem, [p.shape[0] // 4 for p in p_thru], p_thru, l_thru, SCATTER, after,
                                  "scatter_wait_" + stage)
        after = []
        for k, buf in zip(names, landed):
            out = _reduce_adamw(buf, rows_of(k, w[k]), rows_of(k, m[k]), rows_of(k, v[k]), "adamw_" + k)
            grads[k], delta[k], new_m[k], new_v[k] = [rows_of(k, a) for a in out]
            after.append(out[1])
    zero_row = jnp.zeros((1, D), F32)
    d8, m8, v8 = _adamw_small(_pack_small(w, zero_row), total[:ROW_TAPS], _pack_small(m, zero_row),
                              _pack_small(v, zero_row), "adamw_small")
    grads.update(_unpack_small(total[:ROW_TAPS]))
    delta.update(_unpack_small(d8))
    new_m.update(_unpack_small(m8))
    new_v.update(_unpack_small(v8))
    k = "conv_dw_kernel"
    grads[k] = lax.dynamic_slice_in_dim(total[ROW_TAPS:ROW_TAPS + CW], me * BLK, BLK, axis=1)
    delta[k], new_m[k], new_v[k] = _adamw_small(w[k], grads[k], m[k], v[k], "adamw_taps")

    return (loss, dx0[None], *[grads[k] for k in WEIGHTS], *[delta[k] for k in WEIGHTS],
            *[new_m[k] for k in WEIGHTS], *[new_v[k] for k in WEIGHTS])
```

```python
import functools
import math

import numpy as np
import jax
import jax.numpy as jnp
from jax import lax
from jax.experimental import pallas as pl
from jax.experimental.pallas import tpu as pltpu

F32 = jnp.float32
BF = jnp.bfloat16

D = 1024
F = 2816
INW = 5632
CW = 31
CWP = 32
HD = 64
NQ = 16
NKV = 4
GRP = NQ // NKV
BLK = 128
NBUCKET = 32
EPS = 1e-6
NEG = float(jnp.finfo(jnp.float32).min)
QK_SCALE = 1.0 / math.sqrt(HD)
R_CONV = (0, 2048)
R_QKV = (2048, 3584)
R_Q = (2048, 3072)
R_KV = (3072, 3584)
R_GATE = (3584, 5632)

N_DEV = 8
VMEM_LIMIT_V7X = 56 * 1024 * 1024
ROW_TILE = 256
ROW_TILE_WIDE = 512
ROW_TILE_BLOCK = 1024
WGRAD_SUM_MAX_ROWS = 352

ADAM_LR = 0.001
ADAM_B1 = 0.9
ADAM_B2 = 0.999
ADAM_EPS = 1e-08
ADAM_WD = 0.01
ADAM_STEP = 10

NT_DIMS = (((1,), (1,)), ((), ()))
TN_DIMS = (((0,), (0,)), ((), ()))


def _dot(a, b):
    return jnp.dot(a, b, preferred_element_type=F32)


def _dot_nt(a, b):
    return lax.dot_general(a, b, NT_DIMS, preferred_element_type=F32)


def _dot_tn(a, b):
    return lax.dot_general(a, b, TN_DIMS, preferred_element_type=F32)


def _sig(x):
    return 0.5 * jnp.tanh(0.5 * x) + 0.5


ANY = pl.BlockSpec(memory_space=pl.ANY)


def _call(body, deps, args, **kw):
    n = len(deps)
    if n:
        kw["in_specs"] = [ANY] * n + list(kw["in_specs"])
        return pl.pallas_call(lambda *refs: body(*refs[n:]), **kw)(*deps, *args)
    return pl.pallas_call(body, **kw)(*args)


def _params(n_axes):
    return pltpu.CompilerParams(dimension_semantics=("arbitrary",) * n_axes, vmem_limit_bytes=VMEM_LIMIT_V7X)


def _resident(shape):
    zeros = (0,) * len(shape)
    return pl.BlockSpec(shape, lambda *_: zeros, pipeline_mode=pl.Buffered(1))


def _row_tile(rows, cols):
    return pl.BlockSpec((rows, cols), lambda i: (i, 0))


def _rms_stats(x):
    r = lax.rsqrt(jnp.mean(x * x, axis=-1, keepdims=True) + EPS)
    return r, x * r


def _rms_bwd(dn, x, g):
    r, xh = _rms_stats(x)
    dxh = dn * g
    dx = r * (dxh - xh * jnp.mean(dxh * xh, axis=-1, keepdims=True))
    return dx, jnp.sum(dn * xh, axis=0, keepdims=True)


def _ffn_last(x, target, g, w_in_t, w_out, name):
    t = x.shape[0]
    tm = min(ROW_TILE, t)

    def body(x_ref, t_ref, g_ref, w_ref, wo_ref, n_ref, du_ref, h_ref, dy_ref, dx_ref, sq_ref, dg_ref):
        @pl.when(pl.program_id(0) == 0)
        def _():
            sq_ref[...] = jnp.zeros_like(sq_ref)
            dg_ref[...] = jnp.zeros_like(dg_ref)

        x = x_ref[...]
        g = g_ref[...]
        r, xh = _rms_stats(x)
        n = (xh * g).astype(BF)
        n_ref[...] = n
        u = _dot_nt(n, w_ref[...])
        a = u[:, :F]
        b = u[:, F:]
        s = _sig(a)
        sa = a * s
        h = (sa * b).astype(BF)
        h_ref[...] = h
        err = x + 0.5 * _dot(h, wo_ref[...]) - t_ref[...]
        sq_ref[...] += jnp.sum(err * err, axis=0, keepdims=True)
        dxo = err * (1.0 / D)
        dy = (0.5 * dxo).astype(BF)
        dy_ref[...] = dy
        dh = _dot_nt(dy, wo_ref[...])
        du_ref[:, :F] = (dh * b * (s * (1.0 + a * (1.0 - s)))).astype(BF)
        du_ref[:, F:] = (dh * sa).astype(BF)
        dn = _dot(du_ref[...], w_ref[...])
        dxh = dn * g
        dx_ref[...] = dxo + r * (dxh - xh * jnp.mean(dxh * xh, axis=-1, keepdims=True))
        dg_ref[...] += jnp.sum(dn * xh, axis=0, keepdims=True)

    vec = pl.BlockSpec((1, D), lambda i: (0, 0))
    return pl.pallas_call(
        body, grid=(t // tm,),
        in_specs=[_row_tile(tm, D), _row_tile(tm, D), _resident((1, D)), _resident((INW, D)), _resident((F, D))],
        out_specs=[_row_tile(tm, D), _row_tile(tm, INW), _row_tile(tm, F), _row_tile(tm, D), _row_tile(tm, D), vec, vec],
        out_shape=[jax.ShapeDtypeStruct((t, D), BF), jax.ShapeDtypeStruct((t, INW), BF), jax.ShapeDtypeStruct((t, F), BF),
                   jax.ShapeDtypeStruct((t, D), BF), jax.ShapeDtypeStruct((t, D), F32), jax.ShapeDtypeStruct((1, D), F32),
                   jax.ShapeDtypeStruct((1, D), F32)],
        compiler_params=_params(1), name=name)(x, target, g, w_in_t, w_out)


def _ffn_up_blocks(x, g, n, w_in_t, order, u, name, deps=()):
    t = (x if n is None else n).shape[0]
    tm = min(ROW_TILE_BLOCK, t)
    c = INW * 2 // N_DEV
    n_deps = len(deps)
    first = n is None
    assert not first or order.shape == (1,)

    def body(order_ref, *refs):
        refs = refs[n_deps:]
        if first:
            x_ref, g_ref, w_ref, n_ref, u_ref = refs
            nt = (_rms_stats(x_ref[...])[1] * g_ref[...]).astype(BF)
            n_ref[...] = nt
        else:
            n_ref, w_ref, _, u_ref = refs
            nt = n_ref[...]
        u_ref[...] = _dot_nt(nt, w_ref[...]).astype(BF)

    rows = pl.BlockSpec((tm, D), lambda k, i, o: (i, 0))
    block = pl.BlockSpec((c, D), lambda k, i, o: (o[k], 0))
    cols = pl.BlockSpec((tm, c), lambda k, i, o: (i, o[k]))
    u_shape = jax.ShapeDtypeStruct((t, INW), BF)
    if first:
        args, in_specs = (x, g, w_in_t), [rows, _resident((1, D)), block]
        out_specs, out_shape, aliases = [rows, cols], [jax.ShapeDtypeStruct((t, D), BF), u_shape], {}
    else:
        args, in_specs = (n, w_in_t, u), [rows, block, ANY]
        out_specs, out_shape, aliases = cols, u_shape, {1 + n_deps + 2: 0}
    out = pl.pallas_call(
        body,
        grid_spec=pltpu.PrefetchScalarGridSpec(num_scalar_prefetch=1, grid=(order.shape[0], t // tm),
                                               in_specs=[ANY] * n_deps + in_specs, out_specs=out_specs),
        out_shape=out_shape, input_output_aliases=aliases, compiler_params=_params(2), name=name)(order, *deps, *args)
    return tuple(out) if first else (n, out)


def _ffn_down(x, u, w_out, name):
    t = x.shape[0]
    tm = min(ROW_TILE_WIDE, t)

    def body(x_ref, u_ref, wo_ref, xo_ref):
        a = u_ref[:, :F].astype(F32)
        b = u_ref[:, F:].astype(F32)
        h = (a * _sig(a) * b).astype(BF)
        xo_ref[...] = x_ref[...] + 0.5 * _dot(h, wo_ref[...])

    return pl.pallas_call(
        body, grid=(t // tm,), in_specs=[_row_tile(tm, D), _row_tile(tm, INW), _resident((F, D))],
        out_specs=_row_tile(tm, D), out_shape=jax.ShapeDtypeStruct((t, D), F32),
        compiler_params=_params(1), name=name)(x, u, w_out)


def _ffn_bwd(dxo, x, g, u, w_in_t, w_out, name, deps=()):
    t = x.shape[0]
    tm = min(ROW_TILE, t)

    def body(dxo_ref, x_ref, g_ref, u_ref, w_ref, wo_ref, dx_ref, du_ref, h_ref, dy_ref, dg_ref):
        dxo = dxo_ref[...]
        dy = (0.5 * dxo).astype(BF)
        dy_ref[...] = dy
        dh = _dot_nt(dy, wo_ref[...])
        a = u_ref[:, :F].astype(F32)
        b = u_ref[:, F:].astype(F32)
        s = _sig(a)
        sa = a * s
        h_ref[...] = (sa * b).astype(BF)
        du_ref[:, :F] = (dh * b * (s * (1.0 + a * (1.0 - s)))).astype(BF)
        du_ref[:, F:] = (dh * sa).astype(BF)
        dn = _dot(du_ref[...], w_ref[...])
        dx, dg = _rms_bwd(dn, x_ref[...], g_ref[...])
        dx_ref[...] = dxo + dx

        @pl.when(pl.program_id(0) == 0)
        def _():
            dg_ref[...] = jnp.zeros_like(dg_ref)

        dg_ref[...] += dg

    return _call(
        body, deps, (dxo, x, g, u, w_in_t, w_out), grid=(t // tm,),
        in_specs=[_row_tile(tm, D), _row_tile(tm, D), _resident((1, D)), _row_tile(tm, INW), _resident((INW, D)),
                  _resident((F, D))],
        out_specs=[_row_tile(tm, D), _row_tile(tm, INW), _row_tile(tm, F), _row_tile(tm, D),
                   pl.BlockSpec((1, D), lambda i: (0, 0))],
        out_shape=[jax.ShapeDtypeStruct((t, D), F32), jax.ShapeDtypeStruct((t, INW), BF), jax.ShapeDtypeStruct((t, F), BF),
                   jax.ShapeDtypeStruct((t, D), BF), jax.ShapeDtypeStruct((1, D), F32)],
        compiler_params=_params(1), name=name)


def _wgrad(lhs, rhs, name, *, lhs_is_transposed, chunk, deps=()):
    t = rhs.shape[0]
    n = lhs.shape[0] if lhs_is_transposed else lhs.shape[1]
    c = min(chunk, n)

    def body(l_ref, r_ref, o_ref):
        if lhs_is_transposed:
            o_ref[...] = _dot(l_ref[...], r_ref[...]).astype(BF)
        else:
            o_ref[...] = _dot_tn(l_ref[...], r_ref[...]).astype(BF)

    lhs_spec = pl.BlockSpec((c, t), lambda j: (j, 0)) if lhs_is_transposed else pl.BlockSpec((t, c), lambda j: (0, j))
    return _call(
        body, deps, (lhs, rhs), grid=(n // c,),
        in_specs=[lhs_spec, _resident((t, D))],
        out_specs=pl.BlockSpec((c, D), lambda j: (j, 0)),
        out_shape=jax.ShapeDtypeStruct((n, D), BF),
        compiler_params=_params(1), name=name)


def _wgrad_mix(duc, dq_t, dkv_t, dgp, hm):
    t = hm.shape[0]
    c = 512
    first_q, first_kv, first_gate = R_Q[0] // c, R_KV[0] // c, R_GATE[0] // c

    def body(uc_ref, q_ref, kv_ref, gp_ref, h_ref, o_ref):
        j = pl.program_id(0)

        @pl.when(j < first_q)
        def _():
            o_ref[...] = _dot_tn(uc_ref[...], h_ref[...]).astype(BF)

        @pl.when((j >= first_q) & (j < first_kv))
        def _():
            o_ref[...] = _dot(q_ref[...], h_ref[...]).astype(BF)

        @pl.when((j >= first_kv) & (j < first_gate))
        def _():
            o_ref[...] = _dot(kv_ref[...], h_ref[...]).astype(BF)

        @pl.when(j >= first_gate)
        def _():
            o_ref[...] = _dot_tn(gp_ref[...], h_ref[...]).astype(BF)

    return pl.pallas_call(
        body, grid=(INW // c,),
        in_specs=[pl.BlockSpec((t, c), lambda j: (0, jnp.clip(j, 0, first_q - 1))),
                  pl.BlockSpec((c, t), lambda j: (jnp.clip(j - first_q, 0, first_kv - first_q - 1), 0)),
                  pl.BlockSpec((c, t), lambda j: (jnp.clip(j - first_kv, 0, first_gate - first_kv - 1), 0)),
                  pl.BlockSpec((t, c), lambda j: (0, jnp.clip(j - first_gate, 0, INW // c - first_gate - 1))),
                  _resident((t, D))],
        out_specs=pl.BlockSpec((c, D), lambda j: (j, 0)),
        out_shape=jax.ShapeDtypeStruct((INW, D), BF),
        compiler_params=_params(1), name="mix_dw_in")(duc, dq_t, dkv_t, dgp, hm)


def _mix_proj(x, g, w_t):
    t = x.shape[0]
    tm = min(ROW_TILE_WIDE, t)

    def body(x_ref, g_ref, w_ref, hm_ref, uc_ref, gp_ref, qkv_ref):
        r, xh = _rms_stats(x_ref[...])
        hm = (xh * g_ref[...]).astype(BF)
        hm_ref[...] = hm
        uc_ref[...] = _dot_nt(hm, w_ref[R_CONV[0]:R_CONV[1], :]).astype(BF)
        gp_ref[...] = _dot_nt(hm, w_ref[R_GATE[0]:R_GATE[1], :]).astype(BF)
        qkv_ref[...] = _dot_nt(w_ref[R_QKV[0]:R_QKV[1], :], hm).astype(BF)

    return pl.pallas_call(
        body, grid=(t // tm,),
        in_specs=[_row_tile(tm, D), _resident((1, D)), _resident((INW, D))],
        out_specs=[_row_tile(tm, D), _row_tile(tm, 2 * D), _row_tile(tm, 2 * D), pl.BlockSpec((1536, tm), lambda i: (0, i))],
        out_shape=[jax.ShapeDtypeStruct((t, D), BF), jax.ShapeDtypeStruct((t, 2 * D), BF),
                   jax.ShapeDtypeStruct((t, 2 * D), BF), jax.ShapeDtypeStruct((1536, t), BF)],
        compiler_params=_params(1), name="mix_proj")(x, g, w_t)


CONV_HALO = 32
CONV_LEAD = CONV_HALO - (CW - 1)


def _glu(uc):
    uc = uc.astype(F32)
    return uc[:, :D] * _sig(uc[:, D:])


def _ln_stats(zc):
    mu = jnp.mean(zc, axis=-1, keepdims=True)
    zm = zc - mu
    r = lax.rsqrt(jnp.mean(zm * zm, axis=-1, keepdims=True) + EPS)
    return r, zm * r


CONV_SHIFTS = 8
CONV_CHUNK = 32


def _store_shifted(buf, rows):
    for b in range(1, CONV_SHIFTS):
        buf[b, 0:rows - 8, :] = buf[0, pl.ds(b, rows - 8), :]


def _conv_fwd(uc, dwk, dwb, lng, lnb, swap=None):
    t = uc.shape[0]
    tm = min(512, t)
    per = tm // CONV_HALO
    ext = tm + CONV_HALO

    def body(cur_ref, prev_ref, k_ref, kb_ref, g_ref, b_ref, o_ref, zc_ref, zsh):
        i = pl.program_id(0)
        zsh[0, 0:CONV_HALO, :] = _glu(prev_ref[...]) * (i > 0).astype(F32)
        zsh[0, CONV_HALO:, :] = _glu(cur_ref[...])
        _store_shifted(zsh, ext)

        def chunk(ci, carry):
            r0 = pl.multiple_of(ci * CONV_CHUNK, CONV_CHUNK)
            acc = jnp.zeros((CONV_CHUNK, D), F32) + kb_ref[...]
            for w in range(CW):
                a, b = divmod(CONV_LEAD + w, 8)
                acc = acc + k_ref[w:w + 1, :] * zsh[b, pl.ds(r0 + 8 * a, CONV_CHUNK), :]
            zc_ref[pl.ds(r0, CONV_CHUNK), :] = acc
            return carry

        lax.fori_loop(0, tm // CONV_CHUNK, chunk, 0)
        r, xh = _ln_stats(zc_ref[...])
        y = xh * g_ref[...] + b_ref[...]
        o_ref[...] = (y * _sig(y)).astype(BF)

    kw = dict(
        grid=(t // tm,),
        in_specs=[_row_tile(tm, 2 * D),
                  pl.BlockSpec((CONV_HALO, 2 * D), lambda i: (jnp.maximum(i * per - 1, 0), 0)),
                  _resident((CWP, D)), _resident((1, D)), _resident((1, D)), _resident((1, D))],
        out_specs=[_row_tile(tm, D), _row_tile(tm, D)],
        out_shape=[jax.ShapeDtypeStruct((t, D), BF), jax.ShapeDtypeStruct((t, D), F32)],
        scratch_shapes=[pltpu.VMEM((CONV_SHIFTS, ext, D), F32)],
        compiler_params=_params(1), name="conv_fwd")
    args = (uc, uc, dwk, dwb, lng, lnb)
    return pl.pallas_call(body, **kw)(*args) if swap is None else _call_with_swap(body, args, swap, **kw)


def _conv_bwd(uc, zc, dzs, dwk, lng, lnb):
    t = uc.shape[0]
    tm = min(ROW_TILE_WIDE, t)
    per = tm // CONV_HALO
    n_tiles = t // tm
    ext = tm + CONV_HALO
    last_block = t // CONV_HALO - 1

    def body(cur_ref, zc_ref, zcn_ref, dz_ref, dzn_ref, k_ref, g_ref, b_ref,
             duc_ref, dk_ref, dkb_ref, dg_ref, db_ref, dsh, dk8, z_scr):
        i = pl.program_id(0)

        @pl.when(i == 0)
        def _():
            dk8[...] = jnp.zeros_like(dk8)
            dkb_ref[...] = jnp.zeros_like(dkb_ref)
            dg_ref[...] = jnp.zeros_like(dg_ref)
            db_ref[...] = jnp.zeros_like(db_ref)

        has_next = (i < n_tiles - 1).astype(F32)
        z_scr[...] = _glu(cur_ref[...])
        gain = g_ref[...]

        def ln_silu_bwd(zc, dzs, live):
            r, xh = _ln_stats(zc)
            y = xh * gain + b_ref[...]
            sy = _sig(y)
            dy = dzs * (sy * (1.0 + y * (1.0 - sy))) * live
            dxh = dy * gain
            dzc = r * (dxh - jnp.mean(dxh, axis=-1, keepdims=True) - xh * jnp.mean(dxh * xh, axis=-1, keepdims=True))
            return dzc, dy, xh

        dzc, dy, xh = ln_silu_bwd(zc_ref[...], dz_ref[...], 1.0)
        dsh[0, 0:tm, :] = dzc
        dg_ref[...] += jnp.sum(dy * xh, axis=0, keepdims=True)
        db_ref[...] += jnp.sum(dy, axis=0, keepdims=True)
        dkb_ref[...] += jnp.sum(dzc, axis=0, keepdims=True)
        dsh[0, tm:, :] = ln_silu_bwd(zcn_ref[...], dzn_ref[...], has_next)[0]
        _store_shifted(dsh, ext)

        def chunk(ci, carry):
            r0 = pl.multiple_of(ci * CONV_CHUNK, CONV_CHUNK)
            z_c = z_scr[pl.ds(r0, CONV_CHUNK), :]
            dz = jnp.zeros((CONV_CHUNK, D), F32)
            for w in range(CW):
                a, b = divmod(CW - 1 - w, 8)
                window = dsh[b, pl.ds(r0 + 8 * a, CONV_CHUNK), :]
                dz = dz + k_ref[w:w + 1, :] * window
                prod = z_c * window
                part = prod[0:8, :]
                for j in range(1, CONV_CHUNK // 8):
                    part = part + prod[8 * j:8 * j + 8, :]
                dk8[w] += part
            ucc = cur_ref[pl.ds(r0, CONV_CHUNK), :].astype(F32)
            sg = _sig(ucc[:, D:])
            duc_ref[pl.ds(r0, CONV_CHUNK), 0:D] = (dz * sg).astype(BF)
            duc_ref[pl.ds(r0, CONV_CHUNK), D:2 * D] = (dz * ucc[:, :D] * sg * (1.0 - sg)).astype(BF)
            return carry

        lax.fori_loop(0, tm // CONV_CHUNK, chunk, 0)

        @pl.when(i == n_tiles - 1)
        def _():
            dk_ref[...] = jnp.sum(dk8[...], axis=1)

    vec = pl.BlockSpec((1, D), lambda i: (0, 0))
    next_halo = pl.BlockSpec((CONV_HALO, D), lambda i: (jnp.minimum((i + 1) * per, last_block), 0))
    return pl.pallas_call(
        body, grid=(n_tiles,),
        in_specs=[_row_tile(tm, 2 * D), _row_tile(tm, D), next_halo, _row_tile(tm, D), next_halo,
                  _resident((CWP, D)), _resident((1, D)), _resident((1, D))],
        out_specs=[_row_tile(tm, 2 * D), pl.BlockSpec((CWP, D), lambda i: (0, 0)), vec, vec, vec],
        out_shape=[jax.ShapeDtypeStruct((t, 2 * D), BF), jax.ShapeDtypeStruct((CWP, D), F32),
                   jax.ShapeDtypeStruct((1, D), F32), jax.ShapeDtypeStruct((1, D), F32), jax.ShapeDtypeStruct((1, D), F32)],
        scratch_shapes=[pltpu.VMEM((CONV_SHIFTS, ext, D), F32), pltpu.VMEM((CWP, 8, D), F32), pltpu.VMEM((tm, D), F32)],
        compiler_params=_params(1), name="conv_bwd")(uc, zc, zc, dzs, dzs, dwk, lng, lnb)


def _norm_rows(xt, g):
    r = lax.rsqrt(jnp.mean(xt * xt, axis=0, keepdims=True) + EPS)
    xh = xt * r
    return xh * g, r, xh


ATT_TQ = 1024


def _attn_specs(t, tq):
    per = tq // BLK
    return [pl.BlockSpec((1536, tq), lambda i: (0, i)),
            pl.BlockSpec((512, BLK), lambda i: (2, jnp.maximum(i * per - 1, 0))),
            _resident((HD, 1)), _resident((HD, 1)), _resident((NKV, 1, GRP * BLK)),
            _resident((2, NKV, 2 * BLK, GRP * BLK))]


def _attn_window(hk, sb, qkv_ref, halo_ref, kn_cur, kn_halo):
    v0 = D + NKV * HD + hk * HD
    if sb == 0:
        k_prev = kn_halo[hk]
        v_prev = halo_ref[NKV * HD + hk * HD:NKV * HD + (hk + 1) * HD, :]
    else:
        k_prev = kn_cur[hk][:, (sb - 1) * BLK:sb * BLK]
        v_prev = qkv_ref[v0:v0 + HD, (sb - 1) * BLK:sb * BLK]
    kw = jnp.concatenate([k_prev, kn_cur[hk][:, sb * BLK:(sb + 1) * BLK]], axis=1).astype(BF)
    vw = jnp.concatenate([v_prev, qkv_ref[v0:v0 + HD, sb * BLK:(sb + 1) * BLK]], axis=1)
    return kw, vw


def _attn_probs(kw, qc, bias, sink):
    st = _dot_tn(kw, qc) + bias
    m = jnp.maximum(jnp.max(st, axis=0, keepdims=True), sink)
    p = jnp.exp(st - m)
    e_sink = jnp.exp(sink - m)
    inv = 1.0 / (jnp.sum(p, axis=0, keepdims=True) + e_sink)
    return p * inv, e_sink * inv


def _attn_fwd(qkv_t, qg, kg, sink_rows, bias_t):
    t = qkv_t.shape[1]
    tq = min(ATT_TQ, t)
    n_sub = tq // BLK

    def body(qkv_ref, halo_ref, qg_ref, kg_ref, sink_ref, bias_ref, o_ref, p_ref, ps_ref):
        i = pl.program_id(0)
        first = (i == 0).astype(jnp.int32)
        kgain = kg_ref[...]
        qgain = qg_ref[...]
        kn_cur = [_norm_rows(qkv_ref[D + h * HD:D + (h + 1) * HD, :].astype(F32), kgain)[0] for h in range(NKV)]
        kn_halo = [_norm_rows(halo_ref[h * HD:(h + 1) * HD, :].astype(F32), kgain)[0] for h in range(NKV)]
        for hk in range(NKV):
            for sb in range(n_sub):
                cols = slice(sb * BLK, (sb + 1) * BLK)
                kw, vw = _attn_window(hk, sb, qkv_ref, halo_ref, kn_cur, kn_halo)
                qc = jnp.concatenate(
                    [_norm_rows(qkv_ref[(GRP * hk + g) * HD:(GRP * hk + g + 1) * HD, cols].astype(F32), qgain)[0] * QK_SCALE
                     for g in range(GRP)], axis=1).astype(BF)
                bias = bias_ref[first, hk] if sb == 0 else bias_ref[0, hk]
                p, p_sink = _attn_probs(kw, qc, bias, sink_ref[hk])
                p = p.astype(BF)
                p_ref[sb, hk] = p
                ps_ref[sb, hk] = p_sink
                o = _dot(vw, p)
                for g in range(GRP):
                    head = GRP * hk + g
                    o_ref[head * HD:(head + 1) * HD, cols] = o[:, g * BLK:(g + 1) * BLK].astype(BF)

    return pl.pallas_call(
        body, grid=(t // tq,),
        in_specs=_attn_specs(t, tq),
        out_specs=[pl.BlockSpec((D, tq), lambda i: (0, i)),
                   pl.BlockSpec((n_sub, NKV, 2 * BLK, GRP * BLK), lambda i: (i, 0, 0, 0)),
                   pl.BlockSpec((n_sub, NKV, 1, GRP * BLK), lambda i: (i, 0, 0, 0))],
        out_shape=[jax.ShapeDtypeStruct((D, t), BF), jax.ShapeDtypeStruct((t // BLK, NKV, 2 * BLK, GRP * BLK), BF),
                   jax.ShapeDtypeStruct((t // BLK, NKV, 1, GRP * BLK), F32)],
        compiler_params=_params(1), name="attn_fwd")(qkv_t, qkv_t, qg, kg, sink_rows, bias_t)


def _attn_bwd(qkv_t, do_t, probs, sink_probs, qg, kg, onehot_t, deps=()):
    t = qkv_t.shape[1]
    tq = min(ATT_TQ, t)
    n_sub = tq // BLK
    n_tiles = t // tq

    def body(qkv_ref, halo_ref, do_ref, p_ref, ps_ref, qg_ref, kg_ref, oh_ref,
             dq_ref, ckv_ref, dqg_ref, dsink_ref, dbias_ref, qg_scr, sink_scr, ds_scr):
        i = pl.program_id(0)

        @pl.when(i == 0)
        def _():
            qg_scr[...] = jnp.zeros_like(qg_scr)
            sink_scr[...] = jnp.zeros_like(sink_scr)
            ds_scr[...] = jnp.zeros_like(ds_scr)

        kgain = kg_ref[...]
        qgain = qg_ref[...]
        kn_cur = [_norm_rows(qkv_ref[D + h * HD:D + (h + 1) * HD, :].astype(F32), kgain)[0] for h in range(NKV)]
        kn_halo = [_norm_rows(halo_ref[h * HD:(h + 1) * HD, :].astype(F32), kgain)[0] for h in range(NKV)]
        dqg = jnp.zeros((HD, BLK), F32)
        for hk in range(NKV):
            for sb in range(n_sub):
                cols = slice(sb * BLK, (sb + 1) * BLK)
                kw, vw = _attn_window(hk, sb, qkv_ref, halo_ref, kn_cur, kn_halo)
                qn, qr, qh = [], [], []
                for g in range(GRP):
                    head = GRP * hk + g
                    n_, r_, h_ = _norm_rows(qkv_ref[head * HD:(head + 1) * HD, cols].astype(F32), qgain)
                    qn.append(n_)
                    qr.append(r_)
                    qh.append(h_)
                qc = (jnp.concatenate(qn, axis=1) * QK_SCALE).astype(BF)
                p_bf = p_ref[sb, hk]
                p = p_bf.astype(F32)
                doc = jnp.concatenate([do_ref[(GRP * hk + g) * HD:(GRP * hk + g + 1) * HD, cols] for g in range(GRP)], axis=1)
                dp = _dot_tn(vw, doc)
                delta = jnp.sum(p * dp, axis=0, keepdims=True)
                ds = p * (dp - delta)
                sink_scr[hk] += -(ps_ref[sb, hk] * delta)
                ds_scr[hk] += ds
                dsb = ds.astype(BF)
                dqc = _dot(kw, dsb) * QK_SCALE
                ckv_ref[sb, hk * HD:(hk + 1) * HD, :] = _dot_nt(qc, dsb)
                ckv_ref[sb, NKV * HD + hk * HD:NKV * HD + (hk + 1) * HD, :] = _dot_nt(doc, p_bf)
                for g in range(GRP):
                    head = GRP * hk + g
                    dqn = dqc[:, g * BLK:(g + 1) * BLK]
                    dqh = dqn * qgain
                    dq = qr[g] * (dqh - qh[g] * jnp.mean(dqh * qh[g], axis=0, keepdims=True))
                    dq_ref[head * HD:(head + 1) * HD, cols] = dq.astype(BF)
                    dqg = dqg + dqn * qh[g]
        qg_scr[...] += dqg

        @pl.when(i == n_tiles - 1)
        def _():
            dqg_ref[...] = jnp.sum(qg_scr[...], axis=1, keepdims=True)
            dsink_ref[...] = _group_lane_sums(sink_scr[:, 0, :])

            def bucket(b, carry):
                oh = jnp.concatenate([oh_ref[b]] * GRP, axis=1)
                dbias_ref[b] = _group_lane_sums(jnp.sum(ds_scr[...] * oh[None], axis=1))
                return carry

            lax.fori_loop(0, NBUCKET, bucket, 0)

    return _call(
        body, deps, (qkv_t, qkv_t, do_t, probs, sink_probs, qg, kg, onehot_t), grid=(n_tiles,),
        in_specs=_attn_specs(t, tq)[:2] + [pl.BlockSpec((D, tq), lambda i: (0, i)),
                                           pl.BlockSpec((n_sub, NKV, 2 * BLK, GRP * BLK), lambda i: (i, 0, 0, 0)),
                                           pl.BlockSpec((n_sub, NKV, 1, GRP * BLK), lambda i: (i, 0, 0, 0))]
        + _attn_specs(t, tq)[2:4] + [_resident((NBUCKET, 2 * BLK, BLK))],
        out_specs=[pl.BlockSpec((D, tq), lambda i: (0, i)),
                   pl.BlockSpec((n_sub, 2 * NKV * HD, 2 * BLK), lambda i: (i, 0, 0)),
                   pl.BlockSpec((HD, 1), lambda i: (0, 0)),
                   pl.BlockSpec((NKV, BLK), lambda i: (0, 0)),
                   pl.BlockSpec((NBUCKET, NKV, BLK), lambda i: (0, 0, 0))],
        out_shape=[jax.ShapeDtypeStruct((D, t), BF),
                   jax.ShapeDtypeStruct((t // BLK, 2 * NKV * HD, 2 * BLK), F32),
                   jax.ShapeDtypeStruct((HD, 1), F32),
                   jax.ShapeDtypeStruct((NKV, BLK), F32),
                   jax.ShapeDtypeStruct((NBUCKET, NKV, BLK), F32)],
        scratch_shapes=[pltpu.VMEM((HD, BLK), F32), pltpu.VMEM((NKV, 1, GRP * BLK), F32),
                        pltpu.VMEM((NKV, 2 * BLK, GRP * BLK), F32)],
        compiler_params=_params(1), name="attn_bwd")


def _kv_combine_tile(c_ref, cn_ref, has_next, k_ref, kgain, o_ref):
    rows = NKV * HD
    per = c_ref.shape[0]
    dkg = jnp.zeros((HD, BLK), F32)
    for s in range(per):
        cols = slice(s * BLK, (s + 1) * BLK)
        after = c_ref[s + 1, :, :BLK] if s + 1 < per else cn_ref[0, :, :BLK] * has_next
        d = c_ref[s, :, BLK:] + after
        o_ref[rows:, cols] = d[rows:, :].astype(BF)
        for h in range(NKV):
            _, r, kh = _norm_rows(k_ref[h * HD:(h + 1) * HD, cols].astype(F32), kgain)
            dkn = d[h * HD:(h + 1) * HD, :]
            dkh = dkn * kgain
            o_ref[h * HD:(h + 1) * HD, cols] = (r * (dkh - kh * jnp.mean(dkh * kh, axis=0, keepdims=True))).astype(BF)
            dkg = dkg + dkn * kh
    return dkg


def _group_lane_sums(v):
    lane_group = lax.broadcasted_iota(jnp.int32, (1, GRP * BLK), 1) // BLK
    col = lax.broadcasted_iota(jnp.int32, (1, BLK), 1)
    out = jnp.zeros((v.shape[0], BLK), F32)
    for g in range(GRP):
        s = jnp.sum(jnp.where(lane_group == g, v, 0.0), axis=1, keepdims=True)
        out = jnp.where(col == g, s, out)
    return out


def _mix_out(zs, o_t, gp, x, w_cp, w_o, w_out):
    t = x.shape[0]
    tm = min(ROW_TILE_WIDE, t)

    def body(zs_ref, ot_ref, gp_ref, x_ref, wcp_ref, wo_ref, wout_ref, xo_ref, a_ref, b_ref, m_ref):
        a = _dot(zs_ref[...], wcp_ref[...])
        b = _dot_tn(ot_ref[...], wo_ref[...])
        a_ref[...] = a.astype(BF)
        b_ref[...] = b.astype(BF)
        merged = (_sig(gp_ref[:, :D].astype(F32)) * a + _sig(gp_ref[:, D:].astype(F32)) * b).astype(BF)
        m_ref[...] = merged
        xo_ref[...] = x_ref[...] + _dot(merged, wout_ref[...])

    return pl.pallas_call(
        body, grid=(t // tm,),
        in_specs=[_row_tile(tm, D), pl.BlockSpec((D, tm), lambda i: (0, i)), _row_tile(tm, 2 * D), _row_tile(tm, D),
                  _resident((D, D)), _resident((D, D)), _resident((D, D))],
        out_specs=[_row_tile(tm, D)] * 4,
        out_shape=[jax.ShapeDtypeStruct((t, D), F32)] + [jax.ShapeDtypeStruct((t, D), BF)] * 3,
        compiler_params=_params(1), name="mix_out")(zs, o_t, gp, x, w_cp, w_o, w_out)


def _mix_out_bwd(dx, a, b, gp, w_cp, w_o, w_out, deps=()):
    t = dx.shape[0]
    tm = min(ROW_TILE_WIDE, t)

    def body(dx_ref, a_ref, b_ref, gp_ref, wcp_ref, wo_ref, wout_ref, dzs_ref, dot_ref, dgp_ref, da_ref, db_ref, dxb_ref):
        dxb = dx_ref[...].astype(BF)
        dxb_ref[...] = dxb
        dm = _dot_nt(dxb, wout_ref[...])
        gc = _sig(gp_ref[:, :D].astype(F32))
        ga = _sig(gp_ref[:, D:].astype(F32))
        da = (dm * gc).astype(BF)
        db = (dm * ga).astype(BF)
        da_ref[...] = da
        db_ref[...] = db
        dgp_ref[:, :D] = (dm * a_ref[...].astype(F32) * gc * (1.0 - gc)).astype(BF)
        dgp_ref[:, D:] = (dm * b_ref[...].astype(F32) * ga * (1.0 - ga)).astype(BF)
        dzs_ref[...] = _dot_nt(da, wcp_ref[...])
        dot_ref[...] = _dot_nt(wo_ref[...], db).astype(BF)

    return _call(
        body, deps, (dx, a, b, gp, w_cp, w_o, w_out), grid=(t // tm,),
        in_specs=[_row_tile(tm, D), _row_tile(tm, D), _row_tile(tm, D), _row_tile(tm, 2 * D),
                  _resident((D, D)), _resident((D, D)), _resident((D, D))],
        out_specs=[_row_tile(tm, D), pl.BlockSpec((D, tm), lambda i: (0, i)), _row_tile(tm, 2 * D),
                   _row_tile(tm, D), _row_tile(tm, D), _row_tile(tm, D)],
        out_shape=[jax.ShapeDtypeStruct((t, D), F32), jax.ShapeDtypeStruct((D, t), BF), jax.ShapeDtypeStruct((t, 2 * D), BF),
                   jax.ShapeDtypeStruct((t, D), BF), jax.ShapeDtypeStruct((t, D), BF), jax.ShapeDtypeStruct((t, D), BF)],
        compiler_params=_params(1), name="mix_out_bwd")


def _mix_proj_bwd(dxo, duc, dq_t, ckv, qkv_t, kg, dgp, x, g, w_t):
    t = x.shape[0]
    tm = min(ROW_TILE_WIDE, t)
    per = tm // BLK
    steps = t // tm
    kv_rows = 2 * NKV * HD

    def body(dxo_ref, duc_ref, dq_ref, c_ref, cn_ref, k_ref, kg_ref, dgp_ref, x_ref, g_ref, w_ref,
             dx_ref, dg_ref, dkv_ref, dkg_ref, kg_scr):
        i = pl.program_id(0)

        @pl.when(i == 0)
        def _():
            dg_ref[...] = jnp.zeros_like(dg_ref)
            kg_scr[...] = jnp.zeros_like(kg_scr)

        kg_scr[...] += _kv_combine_tile(c_ref, cn_ref, (i < steps - 1).astype(F32), k_ref, kg_ref[...], dkv_ref)
        dn = _dot(duc_ref[...], w_ref[R_CONV[0]:R_CONV[1], :])
        dn = dn + _dot(dgp_ref[...], w_ref[R_GATE[0]:R_GATE[1], :])
        dn = dn + _dot_tn(dq_ref[...], w_ref[R_Q[0]:R_Q[1], :])
        dn = dn + _dot_tn(dkv_ref[...], w_ref[R_KV[0]:R_KV[1], :])
        dx, dg = _rms_bwd(dn, x_ref[...], g_ref[...])
        dx_ref[...] = dxo_ref[...] + dx
        dg_ref[...] += dg

        @pl.when(i == steps - 1)
        def _():
            dkg_ref[...] = jnp.sum(kg_scr[...], axis=1, keepdims=True)

    return pl.pallas_call(
        body, grid=(steps,),
        in_specs=[_row_tile(tm, D), _row_tile(tm, 2 * D), pl.BlockSpec((D, tm), lambda i: (0, i)),
                  pl.BlockSpec((per, kv_rows, 2 * BLK), lambda i: (i, 0, 0)),
                  pl.BlockSpec((1, kv_rows, 2 * BLK), lambda i: (jnp.minimum((i + 1) * per, t // BLK - 1), 0, 0)),
                  pl.BlockSpec((NKV * HD, tm), lambda i: (D // (NKV * HD), i)), _resident((HD, 1)),
                  _row_tile(tm, 2 * D), _row_tile(tm, D), _resident((1, D)), _resident((INW, D))],
        out_specs=[_row_tile(tm, D), pl.BlockSpec((1, D), lambda i: (0, 0)), pl.BlockSpec((kv_rows, tm), lambda i: (0, i)),
                   pl.BlockSpec((HD, 1), lambda i: (0, 0))],
        out_shape=[jax.ShapeDtypeStruct((t, D), F32), jax.ShapeDtypeStruct((1, D), F32),
                   jax.ShapeDtypeStruct((kv_rows, t), BF), jax.ShapeDtypeStruct((HD, 1), F32)],
        scratch_shapes=[pltpu.VMEM((HD, BLK), F32)],
        compiler_params=_params(1), name="mix_proj_bwd")(dxo, duc, dq_t, ckv, ckv, qkv_t, kg, dgp, x, g, w_t)


def _attention_tables():
    kj = np.arange(2 * BLK)[:, None]
    qi = np.arange(BLK)[None, :]
    dist = qi + BLK - kj
    in_win = (dist >= 0) & (dist < BLK)
    dpos = np.maximum(dist, 0)
    max_exact = NBUCKET // 2
    dfl = np.maximum(dpos, 1).astype(np.float32)
    large = max_exact + (np.log(dfl / np.float32(max_exact)) / np.float32(math.log(BLK / max_exact))
                         * np.float32(NBUCKET - max_exact)).astype(np.int32)
    large = np.minimum(large, NBUCKET - 1)
    bucket = np.where(dpos < max_exact, dpos, large)
    onehot = (bucket[None] == np.arange(NBUCKET)[:, None, None]).astype(np.float32)
    mask = in_win.astype(np.float32)
    mask_first = mask * (kj >= BLK)
    masks = np.stack([np.tile(mask, (1, GRP)), np.tile(mask_first, (1, GRP))])
    return onehot, masks


def _bias_table(rel_bias, onehot):
    tab = jnp.einsum("bkq,bh->hkq", onehot, rel_bias, precision=lax.Precision.HIGHEST)
    tab = tab.reshape(NKV, GRP, 2 * BLK, BLK)
    return jnp.transpose(tab, (0, 2, 1, 3)).reshape(NKV, 2 * BLK, GRP * BLK)


def _local_step(x, target, vec, ffn1_up, weights_of, wgrad, grads_done, small_done):
    onehot_np, masks_np = _attention_tables()
    onehot = jnp.asarray(onehot_np)
    masks = jnp.asarray(masks_np)
    bias_t = jnp.where(masks[:, None] > 0.5, _bias_table(vec["rel_bias"], onehot)[None], NEG)
    sink_rows = jnp.repeat(vec["attn_sinks"].reshape(NKV, 1, GRP), BLK, axis=2)
    qg = vec["q_norm"].reshape(HD, 1)
    kg = vec["k_norm"].reshape(HD, 1)
    g1 = vec["ffn1_norm"].reshape(1, D)
    gm = vec["mix_norm"].reshape(1, D)
    g2 = vec["ffn2_norm"].reshape(1, D)
    dwb = vec["conv_dw_bias"].reshape(1, D)
    lng = vec["conv_ln_g"].reshape(1, D)
    lnb = vec["conv_ln_b"].reshape(1, D)

    n1, u1, w_in1 = ffn1_up(x, g1, (bias_t, sink_rows))
    w1 = dict(weights_of("ffn1_out", (u1,)), ffn1_w_in=w_in1)
    x1 = _ffn_down(x, u1, w1["ffn1_w_out"], "ffn1_down")
    wm = weights_of("mix_proj", (x1,))
    dwk = jnp.pad(wm["conv_dw_kernel"], ((0, CWP - CW), (0, 0)))
    hm, uc, gp, qkv_t = _mix_proj(x1, gm, wm["w_in"])
    (zs, zc), merge = weights_of("mix_merge", (uc,), during=functools.partial(_conv_fwd, uc, dwk, dwb, lng, lnb))
    wm.update(merge)
    o_t, probs, sink_probs = _attn_fwd(qkv_t, qg, kg, sink_rows, bias_t)
    x2, a, b, merged = _mix_out(zs, o_t, gp, x1, wm["conv_w_proj"], wm["attn_w_o"], wm["w_out"])
    w2 = weights_of("ffn2", (x2,))
    gv = {}
    n2, du2, h2, dy2, dx2, sq, gv["ffn2_norm"] = _ffn_last(x2, target, g2, w2["ffn2_w_in"], w2["ffn2_w_out"], "ffn2")

    deps = grads_done("ffn2", {"ffn2_w_in": wgrad(du2, n2, "ffn2_dw_in", False),
                               "ffn2_w_out": wgrad(h2, dy2, "ffn2_dw_out", False)})

    dzs, do_t, dgp, da, db, dx2b = _mix_out_bwd(dx2, a, b, gp, wm["conv_w_proj"], wm["attn_w_o"], wm["w_out"], deps=deps)
    deps = grads_done("mix_out", {"w_out": wgrad(merged, dx2b, "mix_dw_out", False),
                                  "conv_w_proj": wgrad(zs, da, "mix_dw_cp", False),
                                  "attn_w_o": wgrad(o_t, db, "mix_dw_o", True)})

    dq_t, ckv, dqg, dsink, dbias = _attn_bwd(qkv_t, do_t, probs, sink_probs, qg, kg, onehot, deps=deps)
    gv["q_norm"] = dqg.reshape(HD)
    gv["attn_sinks"] = dsink[:, :GRP].reshape(NQ)
    gv["rel_bias"] = dbias[:, :, :GRP].reshape(NBUCKET, NQ)

    duc, dk_conv, gv["conv_dw_bias"], gv["conv_ln_g"], gv["conv_ln_b"] = _conv_bwd(uc, zc, dzs, dwk, lng, lnb)
    gv["conv_dw_kernel"] = dk_conv[:CW]

    dx1, gv["mix_norm"], dkv_t, dkg = _mix_proj_bwd(dx2, duc, dq_t, ckv, qkv_t, kg, dgp, x1, gm, wm["w_in"])
    gv["k_norm"] = dkg.reshape(HD)
    deps = grads_done("mix_in", {"w_in": _wgrad_mix(duc, dq_t, dkv_t, dgp, hm)})

    dx0, du1, h1, dy1, gv["ffn1_norm"] = _ffn_bwd(dx1, x, g1, u1, w1["ffn1_w_in"], w1["ffn1_w_out"], "ffn1_bwd", deps=deps)
    for k in ("ffn1_norm", "mix_norm", "ffn2_norm", "conv_dw_bias", "conv_ln_g", "conv_ln_b"):
        gv[k] = gv[k].reshape(D)
    deps = small_done(gv, sq)
    deps = grads_done("ffn1_in", {"ffn1_w_in": wgrad(du1, n1, "ffn1_dw_in", False, deps)})
    grads_done("ffn1_out", {"ffn1_w_out": wgrad(h1, dy1, "ffn1_dw_out", False, deps)})
    return dx0


MESH_ID = pl.DeviceIdType.MESH


def _position():
    return lax.axis_index("x"), lax.axis_index("y"), lax.axis_index("c")


def _shard_rows(ref, index, rows):
    return ref.at[pl.ds(pl.multiple_of(index * rows, 16), rows), :]


def _prep(weights, taps, me, name, deps=()):
    n = len(weights)
    n_deps = len(deps)
    with_taps = taps is not None

    def body(me_ref, *refs):
        refs = refs[n_deps:]
        ins, outs = refs[:len(refs) // 2], refs[len(refs) // 2:]
        for k in range(n):
            outs[k][...] = ins[k][...].astype(BF)
        if with_taps:
            outs[n][0:CW, :] = ins[n][...]
            outs[n][CW:, :] = jnp.zeros((CWP - CW, BLK), F32)

    shard_shapes = [w.shape for w in weights] + [(CWP, BLK)] * with_taps
    dtypes = [BF] * n + [F32] * with_taps
    ins = list(weights) + [taps] * with_taps
    return pl.pallas_call(
        body,
        grid_spec=pltpu.PrefetchScalarGridSpec(
            num_scalar_prefetch=1, grid=(1,),
            in_specs=[ANY] * n_deps + [pl.BlockSpec(a.shape, lambda i, m: (0, 0), pipeline_mode=pl.Buffered(1)) for a in ins],
            out_specs=[pl.BlockSpec(s, lambda i, m: (m[0], 0)) for s in shard_shapes]),
        out_shape=[jax.ShapeDtypeStruct((N_DEV * s[0], s[1]), d) for s, d in zip(shard_shapes, dtypes)],
        compiler_params=_params(1), name=name)(me, *deps, *ins)


HBM = pl.BlockSpec(memory_space=pltpu.HBM)
SEM = pl.BlockSpec(memory_space=pltpu.SEMAPHORE)
DATAFLOW = pltpu.SideEffectType.DATAFLOW_SIDE_EFFECTING
TOKEN = jax.ShapeDtypeStruct((8, 128), F32)


def _in_hbm(x):
    return pltpu.with_memory_space_constraint(x, pltpu.HBM)


def _hbm_like(arrays):
    return [pltpu.HBM(a.shape, a.dtype) for a in arrays]


def _other_chips(x, y):
    return [(1 - x, y), (x, 1 - y), (1 - x, 1 - y)]


def _device_index(chip, c):
    return 4 * chip[0] + 2 * chip[1] + c


def _chip_index(chip):
    return 2 * chip[0] + chip[1]


class _Exchange:
    def __init__(self, gather, all_cores=False):
        self.gather = gather
        self.all_cores = all_cores
        self.n_peers = N_DEV - 1 if all_cores else 3

    def peers(self, x, y, c):
        if self.all_cores:
            return [(x ^ (k >> 2), y ^ ((k >> 1) & 1), c ^ (k & 1)) for k in range(1, N_DEV)]
        return [(*chip, c) for chip in _other_chips(x, y)]

    def sent(self, x, y, c, peer):
        return _device_index((x, y), c) if self.gather else _chip_index(peer[:2])

    def lands_at(self, x, y, c):
        return _device_index((x, y), c) if self.gather else _chip_index((x, y))

    def arrives_at(self, peer):
        return _device_index(peer[:2], peer[2]) if self.gather else _chip_index(peer[:2])


def _ici_copies_start(sets, sources, landings, exchanges, name, deps=()):
    n = len(landings)
    arrays = (list(sources) if sources is not None else []) + list(landings)
    first_land = len(arrays) - n
    n_sets = len(sets)
    n_deps = len(deps)

    def body(*refs):
        refs = refs[n_deps:]
        src, land = refs[:n], refs[first_land:first_land + n]
        sems = refs[len(arrays):len(arrays) + 2 * n_sets]
        token = refs[-1]
        x, y, c = _position()
        for s, (members, exchange) in enumerate(zip(sets, exchanges)):
            for slot, (k, rows) in enumerate(members):
                for j, peer in enumerate(exchange.peers(x, y, c)):
                    at = exchange.n_peers * slot + j
                    pltpu.make_async_remote_copy(
                        src_ref=_shard_rows(src[k], exchange.sent(x, y, c, peer), rows),
                        dst_ref=_shard_rows(land[k], exchange.lands_at(x, y, c), rows),
                        send_sem=sems[2 * s].at[at], recv_sem=sems[2 * s + 1].at[at],
                        device_id=peer, device_id_type=MESH_ID).start()
        token[...] = jnp.zeros_like(token)

    sem_shapes = []
    for members, exchange in zip(sets, exchanges):
        sem_shapes += [pltpu.SemaphoreType.DMA((exchange.n_peers * len(members),))] * 2
    out = pl.pallas_call(
        body, name=name,
        out_shape=sem_shapes + _hbm_like(arrays) + [TOKEN],
        in_specs=[ANY] * n_deps + [HBM] * len(arrays),
        out_specs=[SEM] * (2 * n_sets) + [HBM] * len(arrays) + [pl.BlockSpec(memory_space=pltpu.VMEM)],
        input_output_aliases={n_deps + i: 2 * n_sets + i for i in range(len(arrays))},
        compiler_params=pltpu.CompilerParams(has_side_effects=DATAFLOW),
    )(*deps, *[_in_hbm(a) for a in arrays])
    sems = [(out[2 * s], out[2 * s + 1]) for s in range(n_sets)]
    thru = list(out[2 * n_sets:2 * n_sets + len(arrays)])
    return sems, (thru[:first_land] if sources is not None else None), thru[first_land:], out[-1]


def _ici_copies_wait(sems, members, sources, landings, exchange, after, name):
    n = len(landings)
    arrays = (list(sources) if sources is not None else []) + list(landings)
    first_land = len(arrays) - n

    def body(*refs):
        src, land = refs[:n], refs[first_land:first_land + n]
        send_sems, recv_sems = refs[len(arrays)], refs[len(arrays) + 1]
        x, y, c = _position()
        for slot, rows in enumerate(members):
            for j, peer in enumerate(exchange.peers(x, y, c)):
                at = exchange.n_peers * slot + j
                cp = pltpu.make_async_remote_copy(
                    src_ref=_shard_rows(src[slot], exchange.sent(x, y, c, peer), rows),
                    dst_ref=_shard_rows(land[slot], exchange.arrives_at(peer), rows),
                    send_sem=send_sems.at[at], recv_sem=recv_sems.at[at], device_id=peer, device_id_type=MESH_ID)
                cp.wait_send()
                cp.wait_recv()

    out = pl.pallas_call(
        body, name=name, out_shape=_hbm_like(arrays),
        in_specs=[HBM] * len(arrays) + [SEM, SEM] + [ANY] * len(after), out_specs=[HBM] * len(arrays),
        input_output_aliases={i: i for i in range(len(arrays))},
        compiler_params=pltpu.CompilerParams(has_side_effects=DATAFLOW),
    )(*arrays, sems[0], sems[1], *after)
    return list(out[first_land:])


def _swap_copies(land, rows, which, send_sems, recv_sems):
    x, y, c = _position()
    chips = [([(x, y)] + _other_chips(x, y))[j] for j in which]
    sends, recvs = [], []
    for k in range(len(land)):
        for j, chip in enumerate(chips):
            for copies, core in ((sends, c), (recvs, 1 - c)):
                block = _shard_rows(land[k], _device_index(chip, core), rows[k])
                copies.append(pltpu.make_async_remote_copy(
                    src_ref=block, dst_ref=block, send_sem=send_sems.at[k, j], recv_sem=recv_sems.at[k, j],
                    device_id=(x, y, 1 - c), device_id_type=MESH_ID))
    return sends, recvs


def _d2d_gather(buffers, rows, name, which=(0, 1, 2, 3), deps=()):
    n = len(buffers)
    n_deps = len(deps)

    def body(*refs):
        sends, recvs = _swap_copies(refs[n_deps + n:n_deps + 2 * n], rows, which, *refs[n_deps + 2 * n:])
        for cp in sends:
            cp.start()
        for cp in recvs:
            cp.wait_recv()
        for cp in sends:
            cp.wait_send()

    return pl.pallas_call(
        body, name=name, out_shape=[jax.ShapeDtypeStruct(a.shape, a.dtype) for a in buffers],
        in_specs=[ANY] * (n_deps + n), out_specs=[ANY] * n, input_output_aliases={n_deps + i: i for i in range(n)},
        scratch_shapes=[pltpu.SemaphoreType.DMA((n, len(which))), pltpu.SemaphoreType.DMA((n, len(which)))],
    )(*deps, *buffers)


def _call_with_swap(body, args, swap, **kw):
    buffers, rows = swap
    n, n_in, n_out = len(buffers), len(args), len(kw["out_shape"])
    n_scratch = len(kw["scratch_shapes"])
    steps, = kw["grid"]
    which = (0, 1, 2, 3)

    def hosted(*refs):
        ins, refs = refs[:n_in], refs[n_in + n:]
        outs, land, scratch = refs[:n_out], refs[n_out:n_out + n], refs[n_out + n:n_out + n + n_scratch]
        sends, recvs = _swap_copies(land, rows, which, *refs[n_out + n + n_scratch:])

        @pl.when(pl.program_id(0) == 0)
        def _():
            for cp in sends:
                cp.start()

        body(*ins, *outs, *scratch)

        @pl.when(pl.program_id(0) == steps - 1)
        def _():
            for cp in recvs:
                cp.wait_recv()
            for cp in sends:
                cp.wait_send()

    sem_shape = pltpu.SemaphoreType.DMA((n, len(which)))
    out = pl.pallas_call(
        hosted, grid=kw["grid"], in_specs=kw["in_specs"] + [ANY] * n, out_specs=kw["out_specs"] + [ANY] * n,
        out_shape=kw["out_shape"] + [jax.ShapeDtypeStruct(a.shape, a.dtype) for a in buffers],
        input_output_aliases={n_in + i: n_out + i for i in range(n)},
        scratch_shapes=kw["scratch_shapes"] + [sem_shape, sem_shape],
        compiler_params=kw["compiler_params"], name=kw["name"])(*args, *buffers)
    return out[:n_out], out[n_out:]


def _rs_pair(grads, name):
    n = len(grads)
    rows = [g.shape[0] // N_DEV for g in grads]

    def body(*refs):
        ins, outs = refs[:n], refs[n:2 * n]
        send_sems, recv_sems = refs[2 * n:]
        x, y, c = _position()
        copies = []
        for k in range(n):
            for q in range(4):
                copies.append(pltpu.make_async_remote_copy(
                    src_ref=_shard_rows(ins[k], 2 * q + 1 - c, rows[k]), dst_ref=_shard_rows(outs[k], q, rows[k]),
                    send_sem=send_sems.at[k, q], recv_sem=recv_sems.at[k, q], device_id=(x, y, 1 - c),
                    device_id_type=MESH_ID))
        for cp in copies:
            cp.start()
        for cp in copies:
            cp.wait()

    return pl.pallas_call(
        body, out_shape=[jax.ShapeDtypeStruct((4 * r, g.shape[1]), g.dtype) for g, r in zip(grads, rows)],
        in_specs=[ANY] * n, out_specs=[ANY] * n,
        scratch_shapes=[pltpu.SemaphoreType.DMA((n, 4)), pltpu.SemaphoreType.DMA((n, 4))],
        name=name)(*grads)


def _wgrad_pair(lhs, rhs, name, *, lhs_is_transposed, deps=()):
    t = rhs.shape[0]
    n = lhs.shape[0] if lhs_is_transposed else lhs.shape[1]
    r = n // N_DEV
    n_chips = N_DEV // 2
    per = 1 if (2 * r) % BLK == 0 else 2
    steps = n_chips // per

    def body(l_ref, r_ref, kept_ref, recv_ref, res, send_sems, recv_sems):
        q = pl.program_id(0)
        slot = q % 2
        x, y, c = _position()

        def send(step, buf, i):
            return pltpu.make_async_remote_copy(
                src_ref=res.at[buf, pl.ds(pl.multiple_of((2 * i + 1 - c) * r, 16), r), :],
                dst_ref=_shard_rows(recv_ref, step * per + i, r),
                send_sem=send_sems.at[buf, i], recv_sem=recv_sems.at[step * per + i],
                device_id=(x, y, 1 - c), device_id_type=MESH_ID)

        @pl.when(q >= 2)
        def _():
            for i in range(per):
                send(q - 2, slot, i).wait_send()

        if lhs_is_transposed:
            res[slot] = _dot(l_ref[...], r_ref[...]).astype(BF)
        else:
            res[slot] = _dot_tn(l_ref[...], r_ref[...]).astype(BF)
        for i in range(per):
            kept_ref[i * r:(i + 1) * r, :] = res[slot, pl.ds(pl.multiple_of((2 * i + c) * r, 16), r), :]
            send(q, slot, i).start()

        @pl.when(q == steps - 1)
        def _():
            for i in range(per):
                if steps > 1:
                    send(q - 1, 1 - slot, i).wait_send()
                send(q, slot, i).wait_send()
            for chip in range(n_chips):
                send(chip // per, 0, chip % per).wait_recv()

    width = 2 * r * per
    lhs_spec = pl.BlockSpec((width, t), lambda q: (q, 0)) if lhs_is_transposed else pl.BlockSpec((t, width), lambda q: (0, q))
    return _call(
        body, deps, (lhs, rhs), grid=(steps,),
        in_specs=[lhs_spec, _resident((t, D))],
        out_specs=[pl.BlockSpec((per * r, D), lambda q: (q, 0)), ANY],
        out_shape=[jax.ShapeDtypeStruct((n // 2, D), BF)] * 2,
        scratch_shapes=[pltpu.VMEM((2, width, D), BF), pltpu.SemaphoreType.DMA((2, per)),
                        pltpu.SemaphoreType.DMA((n_chips,))],
        compiler_params=_params(1), name=name)


def _wgrad_pair_sum(lhs, rhs, place, name, *, lhs_is_transposed, deps=()):
    t = rhs.shape[0]
    n = lhs.shape[0] if lhs_is_transposed else lhs.shape[1]
    r = n // N_DEV
    n_chips = N_DEV // 2
    per = 1 if (2 * r) % BLK == 0 else 2
    steps = n_chips // per
    n_deps = len(deps)

    def body(place_ref, *refs):
        l_ref, r_ref, part_ref, land_ref, res, inbox, send_sems, recv_sems = refs[n_deps:]
        q = pl.program_id(0)
        slot = q % 2
        x, y, c = _position()

        def send(step, buf, i):
            return pltpu.make_async_remote_copy(
                src_ref=res.at[buf, pl.ds(pl.multiple_of((2 * i + 1 - c) * r, 16), r), :], dst_ref=inbox.at[step * per + i],
                send_sem=send_sems.at[buf, i], recv_sem=recv_sems.at[step * per + i],
                device_id=(x, y, 1 - c), device_id_type=MESH_ID)

        @pl.when(q < steps)
        def _():
            @pl.when(q >= 2)
            def _():
                for i in range(per):
                    send(q - 2, slot, i).wait_send()

            if lhs_is_transposed:
                res[slot] = _dot(l_ref[...], r_ref[...]).astype(BF)
            else:
                res[slot] = _dot_tn(l_ref[...], r_ref[...]).astype(BF)
            for i in range(per):
                send(q, slot, i).start()

        @pl.when(q >= 1)
        def _():
            for i in range(per):
                chip = (q - 1) * per + i
                send(q - 1, 1 - slot, i).wait_recv()
                kept = res[1 - slot, pl.ds(pl.multiple_of((2 * i + c) * r, 16), r), :]
                total = (kept.astype(F32) + inbox[chip].astype(F32)).astype(BF)
                part_ref[i * r:(i + 1) * r, :] = total

                @pl.when(chip == place_ref[1])
                def _():
                    land_ref[...] = total

        @pl.when(q == steps)
        def _():
            for i in range(per):
                if steps > 1:
                    send(q - 2, slot, i).wait_send()
                send(q - 1, 1 - slot, i).wait_send()

    width = 2 * r * per
    last = steps - 1
    if lhs_is_transposed:
        lhs_spec = pl.BlockSpec((width, t), lambda q, p: (jnp.minimum(q, last), 0))
    else:
        lhs_spec = pl.BlockSpec((t, width), lambda q, p: (0, jnp.minimum(q, last)))
    return pl.pallas_call(
        body,
        grid_spec=pltpu.PrefetchScalarGridSpec(
            num_scalar_prefetch=1, grid=(steps + 1,),
            in_specs=[ANY] * n_deps + [lhs_spec, pl.BlockSpec((t, D), lambda q, p: (0, 0), pipeline_mode=pl.Buffered(1))],
            out_specs=[pl.BlockSpec((per * r, D), lambda q, p: (jnp.maximum(q - 1, 0), 0)),
                       pl.BlockSpec((r, D), lambda q, p: (p[1], 0))],
            scratch_shapes=[pltpu.VMEM((2, width, D), BF), pltpu.VMEM((n_chips, r, D), BF),
                            pltpu.SemaphoreType.DMA((2, per)), pltpu.SemaphoreType.DMA((n_chips,))]),
        out_shape=[jax.ShapeDtypeStruct((n // 2, D), BF)] * 2,
        compiler_params=_params(1), name=name)(place, *deps, lhs, rhs)


def _pair_add(grad, received, place, name, kept_only=False):
    r = received.shape[0] // 4
    parity = 0 if kept_only else 1

    def body(place_ref, g_ref, r_ref, o_ref, land_ref):
        total = (g_ref[...].astype(F32) + r_ref[...].astype(F32)).astype(BF)
        o_ref[...] = total

        @pl.when(pl.program_id(0) == place_ref[1])
        def _():
            land_ref[...] = total

    return pl.pallas_call(
        body,
        grid_spec=pltpu.PrefetchScalarGridSpec(
            num_scalar_prefetch=1, grid=(4,),
            in_specs=[pl.BlockSpec((r, D), lambda q, p: ((1 + parity) * q + parity * p[0], 0)),
                      pl.BlockSpec((r, D), lambda q, p: (q, 0))],
            out_specs=[pl.BlockSpec((r, D), lambda q, p: (q, 0)), pl.BlockSpec((r, D), lambda q, p: (p[1], 0))]),
        out_shape=[jax.ShapeDtypeStruct(received.shape, BF)] * 2,
        compiler_params=_params(1), name=name)(place, grad, received)


def _sum_blocks(gathered, rows):
    def body(b_ref, o_ref):
        acc = b_ref[0:rows, :]
        for d in range(1, N_DEV):
            acc = acc + b_ref[d * rows:(d + 1) * rows, :]
        o_ref[...] = acc

    return pl.pallas_call(body, out_shape=jax.ShapeDtypeStruct((rows, D), F32), name="small_sum")(gathered)


def _adamw_math(w, g, m, v):
    m = ADAM_B1 * m + (1.0 - ADAM_B1) * g
    v = ADAM_B2 * v + (1.0 - ADAM_B2) * (g * g)
    m_hat = m / (1.0 - ADAM_B1 ** ADAM_STEP)
    v_hat = v / (1.0 - ADAM_B2 ** ADAM_STEP)
    delta = -ADAM_LR * (m_hat / (jnp.sqrt(v_hat) + ADAM_EPS) + ADAM_WD * w)
    return delta, m, v


def _sum_partials(blocks):
    g = blocks[0].astype(F32)
    for blk in blocks[1:]:
        g = g + blk.astype(F32)
    return g


def _reduce_adamw(landed, w, m, v, name):
    r = w.shape[0]
    tr = 352 if r % 352 == 0 else r
    per = r // tr

    def body(r0, r1, r2, r3, w_ref, m_ref, v_ref, g_ref, d_ref, nm_ref, nv_ref):
        g = _sum_partials([r0[...], r1[...], r2[...], r3[...]])
        g_ref[...] = g
        d_ref[...], nm_ref[...], nv_ref[...] = _adamw_math(w_ref[...], g, m_ref[...], v_ref[...])

    tile = _row_tile(tr, D)
    return pl.pallas_call(
        body, grid=(per,),
        in_specs=[pl.BlockSpec((tr, D), lambda i, q=q: (q * per + i, 0)) for q in range(4)] + [tile] * 3,
        out_specs=[tile] * 4, out_shape=[jax.ShapeDtypeStruct(w.shape, F32)] * 4,
        compiler_params=_params(1), name=name)(landed, landed, landed, landed, w, m, v)


def _adamw_small(w, g, m, v, name):
    def body(w_ref, g_ref, m_ref, v_ref, d_ref, nm_ref, nv_ref):
        d_ref[...], nm_ref[...], nv_ref[...] = _adamw_math(w_ref[...], g_ref[...], m_ref[...], v_ref[...])

    return pl.pallas_call(body, out_shape=[jax.ShapeDtypeStruct(w.shape, F32)] * 3, name=name)(w, g, m, v)


WEIGHTS = ("ffn1_norm", "ffn1_w_in", "ffn1_w_out", "mix_norm", "w_in", "conv_dw_kernel", "conv_dw_bias", "conv_ln_g",
           "conv_ln_b", "conv_w_proj", "q_norm", "k_norm", "attn_sinks", "rel_bias", "attn_w_o", "w_out", "ffn2_norm",
           "ffn2_w_in", "ffn2_w_out")
MATRICES = ("ffn1_w_in", "ffn1_w_out", "w_in", "conv_w_proj", "attn_w_o", "w_out", "ffn2_w_in", "ffn2_w_out")
COLUMN_SHARDED = ("ffn1_w_in", "w_in", "ffn2_w_in")
ROW_VECTORS = ("ffn1_norm", "mix_norm", "conv_dw_bias", "conv_ln_g", "conv_ln_b", "ffn2_norm")
PACKED = (("q_norm", HD), ("k_norm", HD), ("attn_sinks", NQ), ("rel_bias", NBUCKET * NQ))
GATHER = _Exchange(gather=True)
GATHER_ALL = _Exchange(gather=True, all_cores=True)
SCATTER = _Exchange(gather=False)
FIRST = "ffn1_w_in"
GATHER_STAGES = ("ffn1_out", "mix_proj", "mix_merge", "ffn2")
STAGE_GATHER = {"ffn1_out": GATHER, "mix_proj": GATHER, "mix_merge": GATHER, "ffn2": GATHER_ALL}
STAGE_MEMBERS = {"ffn1_out": ("ffn1_w_out",),
                 "mix_proj": ("w_in", "taps"), "mix_merge": ("conv_w_proj", "attn_w_o", "w_out"),
                 "ffn2": ("ffn2_w_in", "ffn2_w_out")}
ROW_PACKED = len(ROW_VECTORS)
ROW_LOSS = ROW_PACKED + 1
ROW_TAPS = 8
PAYLOAD_ROWS = 48


def _pack_small(values, last_row):
    packed = jnp.concatenate([values[k].reshape(-1) for k, _ in PACKED])
    packed = jnp.pad(packed, (0, D - packed.shape[0])).reshape(1, D)
    return jnp.concatenate([values[k].reshape(1, D) for k in ROW_VECTORS] + [packed, last_row], axis=0)


def _unpack_small(rows):
    out = {k: rows[i] for i, k in enumerate(ROW_VECTORS)}
    at = 0
    for k, size in PACKED:
        out[k] = rows[ROW_PACKED, at:at + size]
        at += size
    out["rel_bias"] = out["rel_bias"].reshape(NBUCKET, NQ)
    return out


def kernel(x, ffn1_norm, ffn1_w_in, ffn1_w_out, mix_norm, w_in, conv_dw_kernel, conv_dw_bias, conv_ln_g, conv_ln_b, conv_w_proj, q_norm, k_norm, attn_sinks, rel_bias, attn_w_o, w_out, ffn2_norm, ffn2_w_in, ffn2_w_out, loss_target, m_ffn1_norm, m_ffn1_w_in, m_ffn1_w_out, m_mix_norm, m_w_in, m_conv_dw_kernel, m_conv_dw_bias, m_conv_ln_g, m_conv_ln_b, m_conv_w_proj, m_q_norm, m_k_norm, m_attn_sinks, m_rel_bias, m_attn_w_o, m_w_out, m_ffn2_norm, m_ffn2_w_in, m_ffn2_w_out, v_ffn1_norm, v_ffn1_w_in, v_ffn1_w_out, v_mix_norm, v_w_in, v_conv_dw_kernel, v_conv_dw_bias, v_conv_ln_g, v_conv_ln_b, v_conv_w_proj, v_q_norm, v_k_norm, v_attn_sinks, v_rel_bias, v_attn_w_o, v_w_out, v_ffn2_norm, v_ffn2_w_in, v_ffn2_w_out):
    w = dict(ffn1_norm=ffn1_norm, ffn1_w_in=ffn1_w_in, ffn1_w_out=ffn1_w_out, mix_norm=mix_norm, w_in=w_in,
             conv_dw_kernel=conv_dw_kernel, conv_dw_bias=conv_dw_bias, conv_ln_g=conv_ln_g, conv_ln_b=conv_ln_b,
             conv_w_proj=conv_w_proj, q_norm=q_norm, k_norm=k_norm, attn_sinks=attn_sinks, rel_bias=rel_bias,
             attn_w_o=attn_w_o, w_out=w_out, ffn2_norm=ffn2_norm, ffn2_w_in=ffn2_w_in, ffn2_w_out=ffn2_w_out)
    m = dict(ffn1_norm=m_ffn1_norm, ffn1_w_in=m_ffn1_w_in, ffn1_w_out=m_ffn1_w_out, mix_norm=m_mix_norm, w_in=m_w_in,
             conv_dw_kernel=m_conv_dw_kernel, conv_dw_bias=m_conv_dw_bias, conv_ln_g=m_conv_ln_g, conv_ln_b=m_conv_ln_b,
             conv_w_proj=m_conv_w_proj, q_norm=m_q_norm, k_norm=m_k_norm, attn_sinks=m_attn_sinks, rel_bias=m_rel_bias,
             attn_w_o=m_attn_w_o, w_out=m_w_out, ffn2_norm=m_ffn2_norm, ffn2_w_in=m_ffn2_w_in, ffn2_w_out=m_ffn2_w_out)
    v = dict(ffn1_norm=v_ffn1_norm, ffn1_w_in=v_ffn1_w_in, ffn1_w_out=v_ffn1_w_out, mix_norm=v_mix_norm, w_in=v_w_in,
             conv_dw_kernel=v_conv_dw_kernel, conv_dw_bias=v_conv_dw_bias, conv_ln_g=v_conv_ln_g, conv_ln_b=v_conv_ln_b,
             conv_w_proj=v_conv_w_proj, q_norm=v_q_norm, k_norm=v_k_norm, attn_sinks=v_attn_sinks, rel_bias=v_rel_bias,
             attn_w_o=v_attn_w_o, w_out=v_w_out, ffn2_norm=v_ffn2_norm, ffn2_w_in=v_ffn2_w_in, ffn2_w_out=v_ffn2_w_out)
    px, py, pc = _position()
    me = 4 * px + 2 * py + pc
    place = jnp.stack([pc, 2 * px + py]).astype(jnp.int32)

    rows_of = lambda k, a: a.T if k in COLUMN_SHARDED else a
    me1 = me.astype(jnp.int32).reshape(1)
    rest = tuple(k for k in MATRICES if k != FIRST)
    shard_rows = dict({k: rows_of(k, w[k]).shape[0] for k in MATRICES}, taps=CWP)
    sems_first, _, thru_first, token = _ici_copies_start(
        [[(0, shard_rows[FIRST])]], None, _prep([rows_of(FIRST, w[FIRST])], None, me1, "prep_first"), [GATHER],
        "gather_start_first")
    buffers = dict(zip(rest + ("taps",), _prep([rows_of(k, w[k]) for k in rest], conv_dw_kernel, me1, "prep", deps=[token])))
    landings, sets = [], []
    for stage in GATHER_STAGES:
        sets.append([(len(landings) + i, shard_rows[k]) for i, k in enumerate(STAGE_MEMBERS[stage])])
        landings += list(STAGE_MEMBERS[stage])
    sems, _, land_thru, started = _ici_copies_start(sets, None, [buffers[k] for k in landings],
                                                    [STAGE_GATHER[s] for s in GATHER_STAGES], "gather_start")

    packed = [_pack_small(a, jnp.zeros((1, D), F32)) for a in (w, m, v)]

    def ffn1_up(x, g, after):
        chips = jnp.stack([_chip_index(chip) for chip in [(px, py)] + _other_chips(px, py)]).astype(jnp.int32)
        rows = [shard_rows[FIRST]]
        mine = _d2d_gather(thru_first, rows, "gather_d2d_first_mine", which=(0,), deps=[started])
        n, u = _ffn_up_blocks(x, g, None, mine[0], chips[:1], None, "ffn1_up_mine")
        landed = _ici_copies_wait(sems_first[0], rows, None, mine, GATHER, [u, *after, *packed], "gather_wait_first")
        w_in_t, = _d2d_gather(landed, rows, "gather_d2d_first", which=(1, 2, 3))
        return (*_ffn_up_blocks(None, None, n, w_in_t, chips[1:], u, "ffn1_up"), w_in_t)

    def weights_of(stage, after, during=None):
        s = GATHER_STAGES.index(stage)
        rows = [r for _, r in sets[s]]
        landed = _ici_copies_wait(sems[s], rows, None, [land_thru[k] for k, _ in sets[s]], STAGE_GATHER[stage],
                                  list(after), "gather_wait_" + stage)
        if during is not None:
            results, landed = during(swap=(landed, rows))
        elif not STAGE_GATHER[stage].all_cores:
            landed = _d2d_gather(landed, rows, "gather_d2d_" + stage)
        out = dict(zip(STAGE_MEMBERS[stage], landed))
        if "taps" in out:
            taps = out.pop("taps")
            out["conv_dw_kernel"] = jnp.transpose(taps.reshape(N_DEV, CWP, BLK), (1, 0, 2)).reshape(CWP, D)[:CW]
        return out if during is None else (results, out)

    in_flight = []

    def wgrad(lhs, rhs, name, lhs_is_transposed, deps=()):
        rows = (lhs.shape[0] if lhs_is_transposed else lhs.shape[1]) // N_DEV
        if rows <= WGRAD_SUM_MAX_ROWS:
            return ("summed",) + tuple(_wgrad_pair_sum(lhs, rhs, place, name, lhs_is_transposed=lhs_is_transposed, deps=deps))
        return ("paired",) + tuple(_wgrad_pair(lhs, rhs, name, lhs_is_transposed=lhs_is_transposed, deps=deps))

    def grads_done(stage, grads):
        names = list(grads)
        added = []
        for k in names:
            if not isinstance(grads[k], tuple):
                received, = _rs_pair([grads[k]], "rs_pair_" + k)
                added.append(_pair_add(grads[k], received, place, "pair_add_" + k))
            elif grads[k][0] == "paired":
                added.append(_pair_add(grads[k][1], grads[k][2], place, "pair_add_" + k, kept_only=True))
            else:
                added.append(grads[k][1:])
        partials = [p for p, _ in added]
        members = [(i, p.shape[0] // 4) for i, p in enumerate(partials)]
        sem, p_thru, l_thru, token = _ici_copies_start([members], partials, [l for _, l in added], [SCATTER],
                                                       "scatter_start_" + stage)
        in_flight.append((stage, names, sem[0], p_thru, l_thru, token))
        return [token]

    small = []

    def small_done(gv, sq):
        payload = jnp.concatenate([_pack_small(gv, sq), jnp.pad(gv["conv_dw_kernel"], ((0, PAYLOAD_ROWS - ROW_TAPS - CW), (0, 0)))],
                                  axis=0)
        mine = lax.dynamic_update_slice_in_dim(lax.empty((N_DEV * PAYLOAD_ROWS, D), F32), payload, me * PAYLOAD_ROWS, axis=0)
        sems, _, thru, token = _ici_copies_start([[(0, PAYLOAD_ROWS)]], None, [mine], [GATHER_ALL], "small_start")
        small.append((sems[0], thru))
        return [token]

    vec = {k: w[k] for k in WEIGHTS if k not in MATRICES and k != "conv_dw_kernel"}
    dx0 = _local_step(x[0], loss_target[0], vec, ffn1_up, weights_of, wgrad, grads_done, small_done)
    gathered, = _ici_copies_wait(small[0][0], [PAYLOAD_ROWS], None, small[0][1], GATHER_ALL, [in_flight[-1][-1]], "small_wait")
    total = _sum_blocks(gathered, PAYLOAD_ROWS)
    loss = (0.5 / D) * jnp.sum(total[ROW_LOSS])

    grads, delta, new_m, new_v = {}, {}, {}, {}
    after = [total]
    for stage, names, sem, p_thru, l_thru, _ in in_flight:
        landed = _ici_copies_wait(sem, [p.shape[0] // 4 for p in p_thru], p_thru, l_thru, SCATTER, after,
                                  "scatter_wait_" + stage)
        after = []
        for k, buf in zip(names, landed):
            out = _reduce_adamw(buf, rows_of(k, w[k]), rows_of(k, m[k]), rows_of(k, v[k]), "adamw_" + k)
            grads[k], delta[k], new_m[k], new_v[k] = [rows_of(k, a) for a in out]
            after.append(out[1])
    d8, m8, v8 = _adamw_small(packed[0], total[:ROW_TAPS], packed[1], packed[2], "adamw_small")
    grads.update(_unpack_small(total[:ROW_TAPS]))
    delta.update(_unpack_small(d8))
    new_m.update(_unpack_small(m8))
    new_v.update(_unpack_small(v8))
    k = "conv_dw_kernel"
    grads[k] = lax.dynamic_slice_in_dim(total[ROW_TAPS:ROW_TAPS + CW], me * BLK, BLK, axis=1)
    delta[k], new_m[k], new_v[k] = _adamw_small(w[k], grads[k], m[k], v[k], "adamw_taps")

    return (loss, dx0[None], *[grads[k] for k in WEIGHTS], *[delta[k] for k in WEIGHTS],
            *[new_m[k] for k in WEIGHTS], *[new_v[k] for k in WEIGHTS])
```

```python
import functools
import math

import numpy as np
import jax
import jax.numpy as jnp
from jax import lax
from jax.experimental import pallas as pl
from jax.experimental.pallas import tpu as pltpu

F32 = jnp.float32
BF = jnp.bfloat16

D = 1024
F = 2816
INW = 5632
CW = 31
CWP = 32
HD = 64
NQ = 16
NKV = 4
GRP = NQ // NKV
BLK = 128
NBUCKET = 32
EPS = 1e-6
NEG = float(jnp.finfo(jnp.float32).min)
QK_SCALE = 1.0 / math.sqrt(HD)
R_CONV = (0, 2048)
R_QKV = (2048, 3584)
R_Q = (2048, 3072)
R_KV = (3072, 3584)
R_GATE = (3584, 5632)

N_DEV = 8
VMEM_LIMIT_V7X = 56 * 1024 * 1024
ROW_TILE = 256
ROW_TILE_WIDE = 512
ROW_TILE_BLOCK = 1024
WGRAD_SUM_MAX_ROWS = 352

ADAM_LR = 0.001
ADAM_B1 = 0.9
ADAM_B2 = 0.999
ADAM_EPS = 1e-08
ADAM_WD = 0.01
ADAM_STEP = 10

NT_DIMS = (((1,), (1,)), ((), ()))
TN_DIMS = (((0,), (0,)), ((), ()))


def _dot(a, b):
    return jnp.dot(a, b, preferred_element_type=F32)


def _dot_nt(a, b):
    return lax.dot_general(a, b, NT_DIMS, preferred_element_type=F32)


def _dot_tn(a, b):
    return lax.dot_general(a, b, TN_DIMS, preferred_element_type=F32)


def _sig(x):
    return 0.5 * jnp.tanh(0.5 * x) + 0.5


ANY = pl.BlockSpec(memory_space=pl.ANY)


def _call(body, deps, args, **kw):
    n = len(deps)
    if n:
        kw["in_specs"] = [ANY] * n + list(kw["in_specs"])
        return pl.pallas_call(lambda *refs: body(*refs[n:]), **kw)(*deps, *args)
    return pl.pallas_call(body, **kw)(*args)


def _params(n_axes):
    return pltpu.CompilerParams(dimension_semantics=("arbitrary",) * n_axes, vmem_limit_bytes=VMEM_LIMIT_V7X)


def _resident(shape):
    zeros = (0,) * len(shape)
    return pl.BlockSpec(shape, lambda *_: zeros, pipeline_mode=pl.Buffered(1))


def _row_tile(rows, cols):
    return pl.BlockSpec((rows, cols), lambda i: (i, 0))


def _rms_stats(x):
    r = lax.rsqrt(jnp.mean(x * x, axis=-1, keepdims=True) + EPS)
    return r, x * r


def _rms_bwd(dn, x, g):
    r, xh = _rms_stats(x)
    dxh = dn * g
    dx = r * (dxh - xh * jnp.mean(dxh * xh, axis=-1, keepdims=True))
    return dx, jnp.sum(dn * xh, axis=0, keepdims=True)


def _ffn_last(x, target, g, w_in_t, w_out, name):
    t = x.shape[0]
    tm = min(ROW_TILE, t)

    def body(x_ref, t_ref, g_ref, w_ref, wo_ref, n_ref, du_ref, h_ref, dy_ref, dx_ref, sq_ref, dg_ref):
        @pl.when(pl.program_id(0) == 0)
        def _():
            sq_ref[...] = jnp.zeros_like(sq_ref)
            dg_ref[...] = jnp.zeros_like(dg_ref)

        x = x_ref[...]
        g = g_ref[...]
        r, xh = _rms_stats(x)
        n = (xh * g).astype(BF)
        n_ref[...] = n
        u = _dot_nt(n, w_ref[...])
        a = u[:, :F]
        b = u[:, F:]
        s = _sig(a)
        sa = a * s
        h = (sa * b).astype(BF)
        h_ref[...] = h
        err = x + 0.5 * _dot(h, wo_ref[...]) - t_ref[...]
        sq_ref[...] += jnp.sum(err * err, axis=0, keepdims=True)
        dxo = err * (1.0 / D)
        dy = (0.5 * dxo).astype(BF)
        dy_ref[...] = dy
        dh = _dot_nt(dy, wo_ref[...])
        du_ref[:, :F] = (dh * b * (s * (1.0 + a * (1.0 - s)))).astype(BF)
        du_ref[:, F:] = (dh * sa).astype(BF)
        dn = _dot(du_ref[...], w_ref[...])
        dxh = dn * g
        dx_ref[...] = dxo + r * (dxh - xh * jnp.mean(dxh * xh, axis=-1, keepdims=True))
        dg_ref[...] += jnp.sum(dn * xh, axis=0, keepdims=True)

    vec = pl.BlockSpec((1, D), lambda i: (0, 0))
    return pl.pallas_call(
        body, grid=(t // tm,),
        in_specs=[_row_tile(tm, D), _row_tile(tm, D), _resident((1, D)), _resident((INW, D)), _resident((F, D))],
        out_specs=[_row_tile(tm, D), _row_tile(tm, INW), _row_tile(tm, F), _row_tile(tm, D), _row_tile(tm, D), vec, vec],
        out_shape=[jax.ShapeDtypeStruct((t, D), BF), jax.ShapeDtypeStruct((t, INW), BF), jax.ShapeDtypeStruct((t, F), BF),
                   jax.ShapeDtypeStruct((t, D), BF), jax.ShapeDtypeStruct((t, D), F32), jax.ShapeDtypeStruct((1, D), F32),
                   jax.ShapeDtypeStruct((1, D), F32)],
        compiler_params=_params(1), name=name)(x, target, g, w_in_t, w_out)


def _ffn_up_blocks(x, g, n, w_in_t, order, u, name, deps=(), swap=None):
    t = (x if n is None else n).shape[0]
    tm = min(ROW_TILE_BLOCK, t)
    c = INW * 2 // N_DEV
    n_deps = len(deps)
    first = n is None
    assert not first or order.shape == (1,)

    def body(order_ref, *refs):
        refs = refs[n_deps:]
        if first:
            x_ref, g_ref, w_ref, n_ref, u_ref = refs
            nt = (_rms_stats(x_ref[...])[1] * g_ref[...]).astype(BF)
            n_ref[...] = nt
        else:
            n_ref, w_ref, _, u_ref = refs
            nt = n_ref[...]
        u_ref[...] = _dot_nt(nt, w_ref[...]).astype(BF)

    rows = pl.BlockSpec((tm, D), lambda k, i, o: (i, 0))
    block = pl.BlockSpec((c, D), lambda k, i, o: (o[k], 0))
    cols = pl.BlockSpec((tm, c), lambda k, i, o: (i, o[k]))
    u_shape = jax.ShapeDtypeStruct((t, INW), BF)
    if first:
        args, in_specs = (x, g, w_in_t), [rows, _resident((1, D)), block]
        out_specs, out_shape, aliases = [rows, cols], [jax.ShapeDtypeStruct((t, D), BF), u_shape], {}
    else:
        args, in_specs = (n, w_in_t, u), [rows, block, ANY]
        out_specs, out_shape, aliases = [cols], [u_shape], {1 + n_deps + 2: 0}
    grid = (order.shape[0], t // tm)
    if swap is not None:
        (out,), swapped = _call_with_swap(
            body, (*deps, *args), swap, prefetch=(order,), grid=grid, in_specs=[ANY] * n_deps + in_specs, out_specs=out_specs,
            out_shape=out_shape, scratch_shapes=[], input_output_aliases={n_deps + 2: 0}, compiler_params=_params(2), name=name)
        return (n, out), swapped
    out = pl.pallas_call(
        body,
        grid_spec=pltpu.PrefetchScalarGridSpec(num_scalar_prefetch=1, grid=grid, in_specs=[ANY] * n_deps + in_specs,
                                               out_specs=out_specs),
        out_shape=out_shape, input_output_aliases=aliases, compiler_params=_params(2), name=name)(order, *deps, *args)
    return tuple(out) if first else (n, out[0])


def _ffn_down(x, u, w_out, name, part=(0, 1), out=None, swap=None):
    t = x.shape[0]
    tm = min(ROW_TILE_WIDE, t)
    steps = t // tm // part[1]
    first = part[0] * steps
    others = [out] if out is not None else []

    def body(x_ref, u_ref, wo_ref, *rest):
        a = u_ref[:, :F].astype(F32)
        b = u_ref[:, F:].astype(F32)
        h = (a * _sig(a) * b).astype(BF)
        rest[-1][...] = x_ref[...] + 0.5 * _dot(h, wo_ref[...])

    tile = lambda cols: pl.BlockSpec((tm, cols), lambda i: (first + i, 0))
    kw = dict(grid=(steps,), in_specs=[tile(D), tile(INW), _resident((F, D))] + [ANY] * len(others), out_specs=[tile(D)],
              out_shape=[jax.ShapeDtypeStruct((t, D), F32)], scratch_shapes=[],
              input_output_aliases={3: 0} if others else {}, compiler_params=_params(1), name=name)
    args = (x, u, w_out, *others)
    return pl.pallas_call(body, **kw)(*args) if swap is None else _call_with_swap(body, args, swap, **kw)


def _ffn_bwd(dxo, x, g, u, w_in_t, w_out, name, deps=()):
    t = x.shape[0]
    tm = min(ROW_TILE, t)

    def body(dxo_ref, x_ref, g_ref, u_ref, w_ref, wo_ref, dx_ref, du_ref, h_ref, dy_ref, dg_ref):
        dxo = dxo_ref[...]
        dy = (0.5 * dxo).astype(BF)
        dy_ref[...] = dy
        dh = _dot_nt(dy, wo_ref[...])
        a = u_ref[:, :F].astype(F32)
        b = u_ref[:, F:].astype(F32)
        s = _sig(a)
        sa = a * s
        h_ref[...] = (sa * b).astype(BF)
        du_ref[:, :F] = (dh * b * (s * (1.0 + a * (1.0 - s)))).astype(BF)
        du_ref[:, F:] = (dh * sa).astype(BF)
        dn = _dot(du_ref[...], w_ref[...])
        dx, dg = _rms_bwd(dn, x_ref[...], g_ref[...])
        dx_ref[...] = dxo + dx

        @pl.when(pl.program_id(0) == 0)
        def _():
            dg_ref[...] = jnp.zeros_like(dg_ref)

        dg_ref[...] += dg

    return _call(
        body, deps, (dxo, x, g, u, w_in_t, w_out), grid=(t // tm,),
        in_specs=[_row_tile(tm, D), _row_tile(tm, D), _resident((1, D)), _row_tile(tm, INW), _resident((INW, D)),
                  _resident((F, D))],
        out_specs=[_row_tile(tm, D), _row_tile(tm, INW), _row_tile(tm, F), _row_tile(tm, D),
                   pl.BlockSpec((1, D), lambda i: (0, 0))],
        out_shape=[jax.ShapeDtypeStruct((t, D), F32), jax.ShapeDtypeStruct((t, INW), BF), jax.ShapeDtypeStruct((t, F), BF),
                   jax.ShapeDtypeStruct((t, D), BF), jax.ShapeDtypeStruct((1, D), F32)],
        compiler_params=_params(1), name=name)


def _wgrad(lhs, rhs, name, *, lhs_is_transposed, chunk, deps=()):
    t = rhs.shape[0]
    n = lhs.shape[0] if lhs_is_transposed else lhs.shape[1]
    c = min(chunk, n)

    def body(l_ref, r_ref, o_ref):
        if lhs_is_transposed:
            o_ref[...] = _dot(l_ref[...], r_ref[...]).astype(BF)
        else:
            o_ref[...] = _dot_tn(l_ref[...], r_ref[...]).astype(BF)

    lhs_spec = pl.BlockSpec((c, t), lambda j: (j, 0)) if lhs_is_transposed else pl.BlockSpec((t, c), lambda j: (0, j))
    return _call(
        body, deps, (lhs, rhs), grid=(n // c,),
        in_specs=[lhs_spec, _resident((t, D))],
        out_specs=pl.BlockSpec((c, D), lambda j: (j, 0)),
        out_shape=jax.ShapeDtypeStruct((n, D), BF),
        compiler_params=_params(1), name=name)


def _wgrad_mix(duc, dq_t, dkv_t, dgp, hm):
    t = hm.shape[0]
    c = 512
    first_q, first_kv, first_gate = R_Q[0] // c, R_KV[0] // c, R_GATE[0] // c

    def body(uc_ref, q_ref, kv_ref, gp_ref, h_ref, o_ref):
        j = pl.program_id(0)

        @pl.when(j < first_q)
        def _():
            o_ref[...] = _dot_tn(uc_ref[...], h_ref[...]).astype(BF)

        @pl.when((j >= first_q) & (j < first_kv))
        def _():
            o_ref[...] = _dot(q_ref[...], h_ref[...]).astype(BF)

        @pl.when((j >= first_kv) & (j < first_gate))
        def _():
            o_ref[...] = _dot(kv_ref[...], h_ref[...]).astype(BF)

        @pl.when(j >= first_gate)
        def _():
            o_ref[...] = _dot_tn(gp_ref[...], h_ref[...]).astype(BF)

    return pl.pallas_call(
        body, grid=(INW // c,),
        in_specs=[pl.BlockSpec((t, c), lambda j: (0, jnp.clip(j, 0, first_q - 1))),
                  pl.BlockSpec((c, t), lambda j: (jnp.clip(j - first_q, 0, first_kv - first_q - 1), 0)),
                  pl.BlockSpec((c, t), lambda j: (jnp.clip(j - first_kv, 0, first_gate - first_kv - 1), 0)),
                  pl.BlockSpec((t, c), lambda j: (0, jnp.clip(j - first_gate, 0, INW // c - first_gate - 1))),
                  _resident((t, D))],
        out_specs=pl.BlockSpec((c, D), lambda j: (j, 0)),
        out_shape=jax.ShapeDtypeStruct((INW, D), BF),
        compiler_params=_params(1), name="mix_dw_in")(duc, dq_t, dkv_t, dgp, hm)


def _mix_proj(x, g, w_t):
    t = x.shape[0]
    tm = min(ROW_TILE_WIDE, t)

    def body(x_ref, g_ref, w_ref, hm_ref, uc_ref, gp_ref, qkv_ref):
        r, xh = _rms_stats(x_ref[...])
        hm = (xh * g_ref[...]).astype(BF)
        hm_ref[...] = hm
        uc_ref[...] = _dot_nt(hm, w_ref[R_CONV[0]:R_CONV[1], :]).astype(BF)
        gp_ref[...] = _dot_nt(hm, w_ref[R_GATE[0]:R_GATE[1], :]).astype(BF)
        qkv_ref[...] = _dot_nt(w_ref[R_QKV[0]:R_QKV[1], :], hm).astype(BF)

    return pl.pallas_call(
        body, grid=(t // tm,),
        in_specs=[_row_tile(tm, D), _resident((1, D)), _resident((INW, D))],
        out_specs=[_row_tile(tm, D), _row_tile(tm, 2 * D), _row_tile(tm, 2 * D), pl.BlockSpec((1536, tm), lambda i: (0, i))],
        out_shape=[jax.ShapeDtypeStruct((t, D), BF), jax.ShapeDtypeStruct((t, 2 * D), BF),
                   jax.ShapeDtypeStruct((t, 2 * D), BF), jax.ShapeDtypeStruct((1536, t), BF)],
        compiler_params=_params(1), name="mix_proj")(x, g, w_t)


CONV_HALO = 32
CONV_LEAD = CONV_HALO - (CW - 1)


def _glu(uc):
    uc = uc.astype(F32)
    return uc[:, :D] * _sig(uc[:, D:])


def _ln_stats(zc):
    mu = jnp.mean(zc, axis=-1, keepdims=True)
    zm = zc - mu
    r = lax.rsqrt(jnp.mean(zm * zm, axis=-1, keepdims=True) + EPS)
    return r, zm * r


CONV_SHIFTS = 8
CONV_CHUNK = 32


def _store_shifted(buf, rows):
    for b in range(1, CONV_SHIFTS):
        buf[b, 0:rows - 8, :] = buf[0, pl.ds(b, rows - 8), :]


def _conv_fwd(uc, dwk, dwb, lng, lnb, swap=None):
    t = uc.shape[0]
    tm = min(512, t)
    per = tm // CONV_HALO
    ext = tm + CONV_HALO

    def body(cur_ref, prev_ref, k_ref, kb_ref, g_ref, b_ref, o_ref, zc_ref, zsh):
        i = pl.program_id(0)
        zsh[0, 0:CONV_HALO, :] = _glu(prev_ref[...]) * (i > 0).astype(F32)
        zsh[0, CONV_HALO:, :] = _glu(cur_ref[...])
        _store_shifted(zsh, ext)

        def chunk(ci, carry):
            r0 = pl.multiple_of(ci * CONV_CHUNK, CONV_CHUNK)
            acc = jnp.zeros((CONV_CHUNK, D), F32) + kb_ref[...]
            for w in range(CW):
                a, b = divmod(CONV_LEAD + w, 8)
                acc = acc + k_ref[w:w + 1, :] * zsh[b, pl.ds(r0 + 8 * a, CONV_CHUNK), :]
            zc_ref[pl.ds(r0, CONV_CHUNK), :] = acc
            return carry

        lax.fori_loop(0, tm // CONV_CHUNK, chunk, 0)
        r, xh = _ln_stats(zc_ref[...])
        y = xh * g_ref[...] + b_ref[...]
        o_ref[...] = (y * _sig(y)).astype(BF)

    kw = dict(
        grid=(t // tm,),
        in_specs=[_row_tile(tm, 2 * D),
                  pl.BlockSpec((CONV_HALO, 2 * D), lambda i: (jnp.maximum(i * per - 1, 0), 0)),
                  _resident((CWP, D)), _resident((1, D)), _resident((1, D)), _resident((1, D))],
        out_specs=[_row_tile(tm, D), _row_tile(tm, D)],
        out_shape=[jax.ShapeDtypeStruct((t, D), BF), jax.ShapeDtypeStruct((t, D), F32)],
        scratch_shapes=[pltpu.VMEM((CONV_SHIFTS, ext, D), F32)],
        compiler_params=_params(1), name="conv_fwd")
    args = (uc, uc, dwk, dwb, lng, lnb)
    return pl.pallas_call(body, **kw)(*args) if swap is None else _call_with_swap(body, args, swap, **kw)


def _conv_bwd(uc, zc, dzs, dwk, lng, lnb):
    t = uc.shape[0]
    tm = min(ROW_TILE_WIDE, t)
    per = tm // CONV_HALO
    n_tiles = t // tm
    ext = tm + CONV_HALO
    last_block = t // CONV_HALO - 1

    def body(cur_ref, zc_ref, zcn_ref, dz_ref, dzn_ref, k_ref, g_ref, b_ref,
             duc_ref, dk_ref, dkb_ref, dg_ref, db_ref, dsh, dk8, z_scr):
        i = pl.program_id(0)

        @pl.when(i == 0)
        def _():
            dk8[...] = jnp.zeros_like(dk8)
            dkb_ref[...] = jnp.zeros_like(dkb_ref)
            dg_ref[...] = jnp.zeros_like(dg_ref)
            db_ref[...] = jnp.zeros_like(db_ref)

        has_next = (i < n_tiles - 1).astype(F32)
        z_scr[...] = _glu(cur_ref[...])
        gain = g_ref[...]

        def ln_silu_bwd(zc, dzs, live):
            r, xh = _ln_stats(zc)
            y = xh * gain + b_ref[...]
            sy = _sig(y)
            dy = dzs * (sy * (1.0 + y * (1.0 - sy))) * live
            dxh = dy * gain
            dzc = r * (dxh - jnp.mean(dxh, axis=-1, keepdims=True) - xh * jnp.mean(dxh * xh, axis=-1, keepdims=True))
            return dzc, dy, xh

        dzc, dy, xh = ln_silu_bwd(zc_ref[...], dz_ref[...], 1.0)
        dsh[0, 0:tm, :] = dzc
        dg_ref[...] += jnp.sum(dy * xh, axis=0, keepdims=True)
        db_ref[...] += jnp.sum(dy, axis=0, keepdims=True)
        dkb_ref[...] += jnp.sum(dzc, axis=0, keepdims=True)
        dsh[0, tm:, :] = ln_silu_bwd(zcn_ref[...], dzn_ref[...], has_next)[0]
        _store_shifted(dsh, ext)

        def chunk(ci, carry):
            r0 = pl.multiple_of(ci * CONV_CHUNK, CONV_CHUNK)
            z_c = z_scr[pl.ds(r0, CONV_CHUNK), :]
            dz = jnp.zeros((CONV_CHUNK, D), F32)
            for w in range(CW):
                a, b = divmod(CW - 1 - w, 8)
                window = dsh[b, pl.ds(r0 + 8 * a, CONV_CHUNK), :]
                dz = dz + k_ref[w:w + 1, :] * window
                prod = z_c * window
                part = prod[0:8, :]
                for j in range(1, CONV_CHUNK // 8):
                    part = part + prod[8 * j:8 * j + 8, :]
                dk8[w] += part
            ucc = cur_ref[pl.ds(r0, CONV_CHUNK), :].astype(F32)
            sg = _sig(ucc[:, D:])
            duc_ref[pl.ds(r0, CONV_CHUNK), 0:D] = (dz * sg).astype(BF)
            duc_ref[pl.ds(r0, CONV_CHUNK), D:2 * D] = (dz * ucc[:, :D] * sg * (1.0 - sg)).astype(BF)
            return carry

        lax.fori_loop(0, tm // CONV_CHUNK, chunk, 0)

        @pl.when(i == n_tiles - 1)
        def _():
            dk_ref[...] = jnp.sum(dk8[...], axis=1)

    vec = pl.BlockSpec((1, D), lambda i: (0, 0))
    next_halo = pl.BlockSpec((CONV_HALO, D), lambda i: (jnp.minimum((i + 1) * per, last_block), 0))
    return pl.pallas_call(
        body, grid=(n_tiles,),
        in_specs=[_row_tile(tm, 2 * D), _row_tile(tm, D), next_halo, _row_tile(tm, D), next_halo,
                  _resident((CWP, D)), _resident((1, D)), _resident((1, D))],
        out_specs=[_row_tile(tm, 2 * D), pl.BlockSpec((CWP, D), lambda i: (0, 0)), vec, vec, vec],
        out_shape=[jax.ShapeDtypeStruct((t, 2 * D), BF), jax.ShapeDtypeStruct((CWP, D), F32),
                   jax.ShapeDtypeStruct((1, D), F32), jax.ShapeDtypeStruct((1, D), F32), jax.ShapeDtypeStruct((1, D), F32)],
        scratch_shapes=[pltpu.VMEM((CONV_SHIFTS, ext, D), F32), pltpu.VMEM((CWP, 8, D), F32), pltpu.VMEM((tm, D), F32)],
        compiler_params=_params(1), name="conv_bwd")(uc, zc, zc, dzs, dzs, dwk, lng, lnb)


def _norm_rows(xt, g):
    r = lax.rsqrt(jnp.mean(xt * xt, axis=0, keepdims=True) + EPS)
    xh = xt * r
    return xh * g, r, xh


ATT_TQ = 1024


def _attn_specs(t, tq):
    per = tq // BLK
    return [pl.BlockSpec((1536, tq), lambda i: (0, i)),
            pl.BlockSpec((512, BLK), lambda i: (2, jnp.maximum(i * per - 1, 0))),
            _resident((HD, 1)), _resident((HD, 1)), _resident((NKV, 1, GRP * BLK)),
            _resident((2, NKV, 2 * BLK, GRP * BLK))]


def _attn_window(hk, sb, qkv_ref, halo_ref, kn_cur, kn_halo):
    v0 = D + NKV * HD + hk * HD
    if sb == 0:
        k_prev = kn_halo[hk]
        v_prev = halo_ref[NKV * HD + hk * HD:NKV * HD + (hk + 1) * HD, :]
    else:
        k_prev = kn_cur[hk][:, (sb - 1) * BLK:sb * BLK]
        v_prev = qkv_ref[v0:v0 + HD, (sb - 1) * BLK:sb * BLK]
    kw = jnp.concatenate([k_prev, kn_cur[hk][:, sb * BLK:(sb + 1) * BLK]], axis=1).astype(BF)
    vw = jnp.concatenate([v_prev, qkv_ref[v0:v0 + HD, sb * BLK:(sb + 1) * BLK]], axis=1)
    return kw, vw


def _attn_probs(kw, qc, bias, sink):
    st = _dot_tn(kw, qc) + bias
    m = jnp.maximum(jnp.max(st, axis=0, keepdims=True), sink)
    p = jnp.exp(st - m)
    e_sink = jnp.exp(sink - m)
    inv = 1.0 / (jnp.sum(p, axis=0, keepdims=True) + e_sink)
    return p * inv, e_sink * inv


def _attn_fwd(qkv_t, qg, kg, sink_rows, bias_t):
    t = qkv_t.shape[1]
    tq = min(ATT_TQ, t)
    n_sub = tq // BLK

    def body(qkv_ref, halo_ref, qg_ref, kg_ref, sink_ref, bias_ref, o_ref, p_ref, ps_ref):
        i = pl.program_id(0)
        first = (i == 0).astype(jnp.int32)
        kgain = kg_ref[...]
        qgain = qg_ref[...]
        kn_cur = [_norm_rows(qkv_ref[D + h * HD:D + (h + 1) * HD, :].astype(F32), kgain)[0] for h in range(NKV)]
        kn_halo = [_norm_rows(halo_ref[h * HD:(h + 1) * HD, :].astype(F32), kgain)[0] for h in range(NKV)]
        for hk in range(NKV):
            for sb in range(n_sub):
                cols = slice(sb * BLK, (sb + 1) * BLK)
                kw, vw = _attn_window(hk, sb, qkv_ref, halo_ref, kn_cur, kn_halo)
                qc = jnp.concatenate(
                    [_norm_rows(qkv_ref[(GRP * hk + g) * HD:(GRP * hk + g + 1) * HD, cols].astype(F32), qgain)[0] * QK_SCALE
                     for g in range(GRP)], axis=1).astype(BF)
                bias = bias_ref[first, hk] if sb == 0 else bias_ref[0, hk]
                p, p_sink = _attn_probs(kw, qc, bias, sink_ref[hk])
                p = p.astype(BF)
                p_ref[sb, hk] = p
                ps_ref[sb, hk] = p_sink
                o = _dot(vw, p)
                for g in range(GRP):
                    head = GRP * hk + g
                    o_ref[head * HD:(head + 1) * HD, cols] = o[:, g * BLK:(g + 1) * BLK].astype(BF)

    return pl.pallas_call(
        body, grid=(t // tq,),
        in_specs=_attn_specs(t, tq),
        out_specs=[pl.BlockSpec((D, tq), lambda i: (0, i)),
                   pl.BlockSpec((n_sub, NKV, 2 * BLK, GRP * BLK), lambda i: (i, 0, 0, 0)),
                   pl.BlockSpec((n_sub, NKV, 1, GRP * BLK), lambda i: (i, 0, 0, 0))],
        out_shape=[jax.ShapeDtypeStruct((D, t), BF), jax.ShapeDtypeStruct((t // BLK, NKV, 2 * BLK, GRP * BLK), BF),
                   jax.ShapeDtypeStruct((t // BLK, NKV, 1, GRP * BLK), F32)],
        compiler_params=_params(1), name="attn_fwd")(qkv_t, qkv_t, qg, kg, sink_rows, bias_t)


def _attn_bwd(qkv_t, do_t, probs, sink_probs, qg, kg, onehot_t, deps=()):
    t = qkv_t.shape[1]
    tq = min(ATT_TQ, t)
    n_sub = tq // BLK
    n_tiles = t // tq

    def body(qkv_ref, halo_ref, do_ref, p_ref, ps_ref, qg_ref, kg_ref, oh_ref,
             dq_ref, ckv_ref, dqg_ref, dsink_ref, dbias_ref, qg_scr, sink_scr, ds_scr):
        i = pl.program_id(0)

        @pl.when(i == 0)
        def _():
            qg_scr[...] = jnp.zeros_like(qg_scr)
            sink_scr[...] = jnp.zeros_like(sink_scr)
            ds_scr[...] = jnp.zeros_like(ds_scr)

        kgain = kg_ref[...]
        qgain = qg_ref[...]
        kn_cur = [_norm_rows(qkv_ref[D + h * HD:D + (h + 1) * HD, :].astype(F32), kgain)[0] for h in range(NKV)]
        kn_halo = [_norm_rows(halo_ref[h * HD:(h + 1) * HD, :].astype(F32), kgain)[0] for h in range(NKV)]
        dqg = jnp.zeros((HD, BLK), F32)
        for hk in range(NKV):
            for sb in range(n_sub):
                cols = slice(sb * BLK, (sb + 1) * BLK)
                kw, vw = _attn_window(hk, sb, qkv_ref, halo_ref, kn_cur, kn_halo)
                qn, qr, qh = [], [], []
                for g in range(GRP):
                    head = GRP * hk + g
                    n_, r_, h_ = _norm_rows(qkv_ref[head * HD:(head + 1) * HD, cols].astype(F32), qgain)
                    qn.append(n_)
                    qr.append(r_)
                    qh.append(h_)
                qc = (jnp.concatenate(qn, axis=1) * QK_SCALE).astype(BF)
                p_bf = p_ref[sb, hk]
                p = p_bf.astype(F32)
                doc = jnp.concatenate([do_ref[(GRP * hk + g) * HD:(GRP * hk + g + 1) * HD, cols] for g in range(GRP)], axis=1)
                dp = _dot_tn(vw, doc)
                delta = jnp.sum(p * dp, axis=0, keepdims=True)
                ds = p * (dp - delta)
                sink_scr[hk] += -(ps_ref[sb, hk] * delta)
                ds_scr[hk] += ds
                dsb = ds.astype(BF)
                dqc = _dot(kw, dsb) * QK_SCALE
                ckv_ref[sb, hk * HD:(hk + 1) * HD, :] = _dot_nt(qc, dsb)
                ckv_ref[sb, NKV * HD + hk * HD:NKV * HD + (hk + 1) * HD, :] = _dot_nt(doc, p_bf)
                for g in range(GRP):
                    head = GRP * hk + g
                    dqn = dqc[:, g * BLK:(g + 1) * BLK]
                    dqh = dqn * qgain
                    dq = qr[g] * (dqh - qh[g] * jnp.mean(dqh * qh[g], axis=0, keepdims=True))
                    dq_ref[head * HD:(head + 1) * HD, cols] = dq.astype(BF)
                    dqg = dqg + dqn * qh[g]
        qg_scr[...] += dqg

        @pl.when(i == n_tiles - 1)
        def _():
            dqg_ref[...] = jnp.sum(qg_scr[...], axis=1, keepdims=True)
            dsink_ref[...] = _group_lane_sums(sink_scr[:, 0, :])

            def bucket(b, carry):
                oh = jnp.concatenate([oh_ref[b]] * GRP, axis=1)
                dbias_ref[b] = _group_lane_sums(jnp.sum(ds_scr[...] * oh[None], axis=1))
                return carry

            lax.fori_loop(0, NBUCKET, bucket, 0)

    return _call(
        body, deps, (qkv_t, qkv_t, do_t, probs, sink_probs, qg, kg, onehot_t), grid=(n_tiles,),
        in_specs=_attn_specs(t, tq)[:2] + [pl.BlockSpec((D, tq), lambda i: (0, i)),
                                           pl.BlockSpec((n_sub, NKV, 2 * BLK, GRP * BLK), lambda i: (i, 0, 0, 0)),
                                           pl.BlockSpec((n_sub, NKV, 1, GRP * BLK), lambda i: (i, 0, 0, 0))]
        + _attn_specs(t, tq)[2:4] + [_resident((NBUCKET, 2 * BLK, BLK))],
        out_specs=[pl.BlockSpec((D, tq), lambda i: (0, i)),
                   pl.BlockSpec((n_sub, 2 * NKV * HD, 2 * BLK), lambda i: (i, 0, 0)),
                   pl.BlockSpec((HD, 1), lambda i: (0, 0)),
                   pl.BlockSpec((NKV, BLK), lambda i: (0, 0)),
                   pl.BlockSpec((NBUCKET, NKV, BLK), lambda i: (0, 0, 0))],
        out_shape=[jax.ShapeDtypeStruct((D, t), BF),
                   jax.ShapeDtypeStruct((t // BLK, 2 * NKV * HD, 2 * BLK), F32),
                   jax.ShapeDtypeStruct((HD, 1), F32),
                   jax.ShapeDtypeStruct((NKV, BLK), F32),
                   jax.ShapeDtypeStruct((NBUCKET, NKV, BLK), F32)],
        scratch_shapes=[pltpu.VMEM((HD, BLK), F32), pltpu.VMEM((NKV, 1, GRP * BLK), F32),
                        pltpu.VMEM((NKV, 2 * BLK, GRP * BLK), F32)],
        compiler_params=_params(1), name="attn_bwd")


def _kv_combine_tile(c_ref, cn_ref, has_next, k_ref, kgain, o_ref):
    rows = NKV * HD
    per = c_ref.shape[0]
    dkg = jnp.zeros((HD, BLK), F32)
    for s in range(per):
        cols = slice(s * BLK, (s + 1) * BLK)
        after = c_ref[s + 1, :, :BLK] if s + 1 < per else cn_ref[0, :, :BLK] * has_next
        d = c_ref[s, :, BLK:] + after
        o_ref[rows:, cols] = d[rows:, :].astype(BF)
        for h in range(NKV):
            _, r, kh = _norm_rows(k_ref[h * HD:(h + 1) * HD, cols].astype(F32), kgain)
            dkn = d[h * HD:(h + 1) * HD, :]
            dkh = dkn * kgain
            o_ref[h * HD:(h + 1) * HD, cols] = (r * (dkh - kh * jnp.mean(dkh * kh, axis=0, keepdims=True))).astype(BF)
            dkg = dkg + dkn * kh
    return dkg


def _group_lane_sums(v):
    lane_group = lax.broadcasted_iota(jnp.int32, (1, GRP * BLK), 1) // BLK
    col = lax.broadcasted_iota(jnp.int32, (1, BLK), 1)
    out = jnp.zeros((v.shape[0], BLK), F32)
    for g in range(GRP):
        s = jnp.sum(jnp.where(lane_group == g, v, 0.0), axis=1, keepdims=True)
        out = jnp.where(col == g, s, out)
    return out


def _mix_out(zs, o_t, gp, x, w_cp, w_o, w_out):
    t = x.shape[0]
    tm = min(ROW_TILE_WIDE, t)

    def body(zs_ref, ot_ref, gp_ref, x_ref, wcp_ref, wo_ref, wout_ref, xo_ref, a_ref, b_ref, m_ref):
        a = _dot(zs_ref[...], wcp_ref[...])
        b = _dot_tn(ot_ref[...], wo_ref[...])
        a_ref[...] = a.astype(BF)
        b_ref[...] = b.astype(BF)
        merged = (_sig(gp_ref[:, :D].astype(F32)) * a + _sig(gp_ref[:, D:].astype(F32)) * b).astype(BF)
        m_ref[...] = merged
        xo_ref[...] = x_ref[...] + _dot(merged, wout_ref[...])

    return pl.pallas_call(
        body, grid=(t // tm,),
        in_specs=[_row_tile(tm, D), pl.BlockSpec((D, tm), lambda i: (0, i)), _row_tile(tm, 2 * D), _row_tile(tm, D),
                  _resident((D, D)), _resident((D, D)), _resident((D, D))],
        out_specs=[_row_tile(tm, D)] * 4,
        out_shape=[jax.ShapeDtypeStruct((t, D), F32)] + [jax.ShapeDtypeStruct((t, D), BF)] * 3,
        compiler_params=_params(1), name="mix_out")(zs, o_t, gp, x, w_cp, w_o, w_out)


def _mix_out_bwd(dx, a, b, gp, w_cp, w_o, w_out, deps=()):
    t = dx.shape[0]
    tm = min(ROW_TILE_WIDE, t)

    def body(dx_ref, a_ref, b_ref, gp_ref, wcp_ref, wo_ref, wout_ref, dzs_ref, dot_ref, dgp_ref, da_ref, db_ref, dxb_ref):
        dxb = dx_ref[...].astype(BF)
        dxb_ref[...] = dxb
        dm = _dot_nt(dxb, wout_ref[...])
        gc = _sig(gp_ref[:, :D].astype(F32))
        ga = _sig(gp_ref[:, D:].astype(F32))
        da = (dm * gc).astype(BF)
        db = (dm * ga).astype(BF)
        da_ref[...] = da
        db_ref[...] = db
        dgp_ref[:, :D] = (dm * a_ref[...].astype(F32) * gc * (1.0 - gc)).astype(BF)
        dgp_ref[:, D:] = (dm * b_ref[...].astype(F32) * ga * (1.0 - ga)).astype(BF)
        dzs_ref[...] = _dot_nt(da, wcp_ref[...])
        dot_ref[...] = _dot_nt(wo_ref[...], db).astype(BF)

    return _call(
        body, deps, (dx, a, b, gp, w_cp, w_o, w_out), grid=(t // tm,),
        in_specs=[_row_tile(tm, D), _row_tile(tm, D), _row_tile(tm, D), _row_tile(tm, 2 * D),
                  _resident((D, D)), _resident((D, D)), _resident((D, D))],
        out_specs=[_row_tile(tm, D), pl.BlockSpec((D, tm), lambda i: (0, i)), _row_tile(tm, 2 * D),
                   _row_tile(tm, D), _row_tile(tm, D), _row_tile(tm, D)],
        out_shape=[jax.ShapeDtypeStruct((t, D), F32), jax.ShapeDtypeStruct((D, t), BF), jax.ShapeDtypeStruct((t, 2 * D), BF),
                   jax.ShapeDtypeStruct((t, D), BF), jax.ShapeDtypeStruct((t, D), BF), jax.ShapeDtypeStruct((t, D), BF)],
        compiler_params=_params(1), name="mix_out_bwd")


def _mix_proj_bwd(dxo, duc, dq_t, ckv, qkv_t, kg, dgp, x, g, w_t):
    t = x.shape[0]
    tm = min(ROW_TILE_WIDE, t)
    per = tm // BLK
    steps = t // tm
    kv_rows = 2 * NKV * HD

    def body(dxo_ref, duc_ref, dq_ref, c_ref, cn_ref, k_ref, kg_ref, dgp_ref, x_ref, g_ref, w_ref,
             dx_ref, dg_ref, dkv_ref, dkg_ref, kg_scr):
        i = pl.program_id(0)

        @pl.when(i == 0)
        def _():
            dg_ref[...] = jnp.zeros_like(dg_ref)
            kg_scr[...] = jnp.zeros_like(kg_scr)

        kg_scr[...] += _kv_combine_tile(c_ref, cn_ref, (i < steps - 1).astype(F32), k_ref, kg_ref[...], dkv_ref)
        dn = _dot(duc_ref[...], w_ref[R_CONV[0]:R_CONV[1], :])
        dn = dn + _dot(dgp_ref[...], w_ref[R_GATE[0]:R_GATE[1], :])
        dn = dn + _dot_tn(dq_ref[...], w_ref[R_Q[0]:R_Q[1], :])
        dn = dn + _dot_tn(dkv_ref[...], w_ref[R_KV[0]:R_KV[1], :])
        dx, dg = _rms_bwd(dn, x_ref[...], g_ref[...])
        dx_ref[...] = dxo_ref[...] + dx
        dg_ref[...] += dg

        @pl.when(i == steps - 1)
        def _():
            dkg_ref[...] = jnp.sum(kg_scr[...], axis=1, keepdims=True)

    return pl.pallas_call(
        body, grid=(steps,),
        in_specs=[_row_tile(tm, D), _row_tile(tm, 2 * D), pl.BlockSpec((D, tm), lambda i: (0, i)),
                  pl.BlockSpec((per, kv_rows, 2 * BLK), lambda i: (i, 0, 0)),
                  pl.BlockSpec((1, kv_rows, 2 * BLK), lambda i: (jnp.minimum((i + 1) * per, t // BLK - 1), 0, 0)),
                  pl.BlockSpec((NKV * HD, tm), lambda i: (D // (NKV * HD), i)), _resident((HD, 1)),
                  _row_tile(tm, 2 * D), _row_tile(tm, D), _resident((1, D)), _resident((INW, D))],
        out_specs=[_row_tile(tm, D), pl.BlockSpec((1, D), lambda i: (0, 0)), pl.BlockSpec((kv_rows, tm), lambda i: (0, i)),
                   pl.BlockSpec((HD, 1), lambda i: (0, 0))],
        out_shape=[jax.ShapeDtypeStruct((t, D), F32), jax.ShapeDtypeStruct((1, D), F32),
                   jax.ShapeDtypeStruct((kv_rows, t), BF), jax.ShapeDtypeStruct((HD, 1), F32)],
        scratch_shapes=[pltpu.VMEM((HD, BLK), F32)],
        compiler_params=_params(1), name="mix_proj_bwd")(dxo, duc, dq_t, ckv, ckv, qkv_t, kg, dgp, x, g, w_t)


def _attention_tables():
    kj = np.arange(2 * BLK)[:, None]
    qi = np.arange(BLK)[None, :]
    dist = qi + BLK - kj
    in_win = (dist >= 0) & (dist < BLK)
    dpos = np.maximum(dist, 0)
    max_exact = NBUCKET // 2
    dfl = np.maximum(dpos, 1).astype(np.float32)
    large = max_exact + (np.log(dfl / np.float32(max_exact)) / np.float32(math.log(BLK / max_exact))
                         * np.float32(NBUCKET - max_exact)).astype(np.int32)
    large = np.minimum(large, NBUCKET - 1)
    bucket = np.where(dpos < max_exact, dpos, large)
    onehot = (bucket[None] == np.arange(NBUCKET)[:, None, None]).astype(np.float32)
    mask = in_win.astype(np.float32)
    mask_first = mask * (kj >= BLK)
    masks = np.stack([np.tile(mask, (1, GRP)), np.tile(mask_first, (1, GRP))])
    return onehot, masks


def _bias_table(rel_bias, onehot):
    tab = jnp.einsum("bkq,bh->hkq", onehot, rel_bias, precision=lax.Precision.HIGHEST)
    tab = tab.reshape(NKV, GRP, 2 * BLK, BLK)
    return jnp.transpose(tab, (0, 2, 1, 3)).reshape(NKV, 2 * BLK, GRP * BLK)


def _local_step(x, target, vec, ffn1_up, weights_of, wgrad, grads_done, small_done):
    onehot_np, masks_np = _attention_tables()
    onehot = jnp.asarray(onehot_np)
    masks = jnp.asarray(masks_np)
    bias_t = jnp.where(masks[:, None] > 0.5, _bias_table(vec["rel_bias"], onehot)[None], NEG)
    sink_rows = jnp.repeat(vec["attn_sinks"].reshape(NKV, 1, GRP), BLK, axis=2)
    qg = vec["q_norm"].reshape(HD, 1)
    kg = vec["k_norm"].reshape(HD, 1)
    g1 = vec["ffn1_norm"].reshape(1, D)
    gm = vec["mix_norm"].reshape(1, D)
    g2 = vec["ffn2_norm"].reshape(1, D)
    dwb = vec["conv_dw_bias"].reshape(1, D)
    lng = vec["conv_ln_g"].reshape(1, D)
    lnb = vec["conv_ln_b"].reshape(1, D)

    n1, u1, w1 = ffn1_up(x, g1, (bias_t, sink_rows))
    x1, = _ffn_down(x, u1, w1["ffn1_w_out"], "ffn1_down_first", part=(0, 2))
    (x1,), wm = weights_of("mix_proj", (x1,), during=functools.partial(
        _ffn_down, x, u1, w1["ffn1_w_out"], "ffn1_down", part=(1, 2), out=x1))
    dwk = jnp.pad(wm["conv_dw_kernel"], ((0, CWP - CW), (0, 0)))
    hm, uc, gp, qkv_t = _mix_proj(x1, gm, wm["w_in"])
    (zs, zc), merge = weights_of("mix_merge", (uc,), during=functools.partial(_conv_fwd, uc, dwk, dwb, lng, lnb))
    wm.update(merge)
    o_t, probs, sink_probs = _attn_fwd(qkv_t, qg, kg, sink_rows, bias_t)
    x2, a, b, merged = _mix_out(zs, o_t, gp, x1, wm["conv_w_proj"], wm["attn_w_o"], wm["w_out"])
    w2 = weights_of("ffn2", (x2,))
    gv = {}
    n2, du2, h2, dy2, dx2, sq, gv["ffn2_norm"] = _ffn_last(x2, target, g2, w2["ffn2_w_in"], w2["ffn2_w_out"], "ffn2")

    deps = grads_done("ffn2", {"ffn2_w_in": wgrad(du2, n2, "ffn2_dw_in", False),
                               "ffn2_w_out": wgrad(h2, dy2, "ffn2_dw_out", False)})

    dzs, do_t, dgp, da, db, dx2b = _mix_out_bwd(dx2, a, b, gp, wm["conv_w_proj"], wm["attn_w_o"], wm["w_out"], deps=deps)
    deps = grads_done("mix_out", {"w_out": wgrad(merged, dx2b, "mix_dw_out", False),
                                  "conv_w_proj": wgrad(zs, da, "mix_dw_cp", False),
                                  "attn_w_o": wgrad(o_t, db, "mix_dw_o", True)})

    dq_t, ckv, dqg, dsink, dbias = _attn_bwd(qkv_t, do_t, probs, sink_probs, qg, kg, onehot, deps=deps)
    gv["q_norm"] = dqg.reshape(HD)
    gv["attn_sinks"] = dsink[:, :GRP].reshape(NQ)
    gv["rel_bias"] = dbias[:, :, :GRP].reshape(NBUCKET, NQ)

    duc, dk_conv, gv["conv_dw_bias"], gv["conv_ln_g"], gv["conv_ln_b"] = _conv_bwd(uc, zc, dzs, dwk, lng, lnb)
    gv["conv_dw_kernel"] = dk_conv[:CW]

    dx1, gv["mix_norm"], dkv_t, dkg = _mix_proj_bwd(dx2, duc, dq_t, ckv, qkv_t, kg, dgp, x1, gm, wm["w_in"])
    gv["k_norm"] = dkg.reshape(HD)
    deps = grads_done("mix_in", {"w_in": _wgrad_mix(duc, dq_t, dkv_t, dgp, hm)})

    dx0, du1, h1, dy1, gv["ffn1_norm"] = _ffn_bwd(dx1, x, g1, u1, w1["ffn1_w_in"], w1["ffn1_w_out"], "ffn1_bwd", deps=deps)
    for k in ("ffn1_norm", "mix_norm", "ffn2_norm", "conv_dw_bias", "conv_ln_g", "conv_ln_b"):
        gv[k] = gv[k].reshape(D)
    deps = small_done(gv, sq)
    deps = grads_done("ffn1_in", {"ffn1_w_in": wgrad(du1, n1, "ffn1_dw_in", False, deps)})
    grads_done("ffn1_out", {"ffn1_w_out": wgrad(h1, dy1, "ffn1_dw_out", False, deps)})
    return dx0


MESH_ID = pl.DeviceIdType.MESH


def _position():
    return lax.axis_index("x"), lax.axis_index("y"), lax.axis_index("c")


def _shard_rows(ref, index, rows):
    return ref.at[pl.ds(pl.multiple_of(index * rows, 16), rows), :]


def _prep(weights, taps, me, name, deps=()):
    n = len(weights)
    n_deps = len(deps)
    with_taps = taps is not None

    def body(me_ref, *refs):
        refs = refs[n_deps:]
        ins, outs = refs[:len(refs) // 2], refs[len(refs) // 2:]
        for k in range(n):
            outs[k][...] = ins[k][...].astype(BF)
        if with_taps:
            outs[n][0:CW, :] = ins[n][...]
            outs[n][CW:, :] = jnp.zeros((CWP - CW, BLK), F32)

    shard_shapes = [w.shape for w in weights] + [(CWP, BLK)] * with_taps
    dtypes = [BF] * n + [F32] * with_taps
    ins = list(weights) + [taps] * with_taps
    return pl.pallas_call(
        body,
        grid_spec=pltpu.PrefetchScalarGridSpec(
            num_scalar_prefetch=1, grid=(1,),
            in_specs=[ANY] * n_deps + [pl.BlockSpec(a.shape, lambda i, m: (0, 0), pipeline_mode=pl.Buffered(1)) for a in ins],
            out_specs=[pl.BlockSpec(s, lambda i, m: (m[0], 0)) for s in shard_shapes]),
        out_shape=[jax.ShapeDtypeStruct((N_DEV * s[0], s[1]), d) for s, d in zip(shard_shapes, dtypes)],
        compiler_params=_params(1), name=name)(me, *deps, *ins)


HBM = pl.BlockSpec(memory_space=pltpu.HBM)
SEM = pl.BlockSpec(memory_space=pltpu.SEMAPHORE)
DATAFLOW = pltpu.SideEffectType.DATAFLOW_SIDE_EFFECTING
TOKEN = jax.ShapeDtypeStruct((8, 128), F32)


def _in_hbm(x):
    return pltpu.with_memory_space_constraint(x, pltpu.HBM)


def _hbm_like(arrays):
    return [pltpu.HBM(a.shape, a.dtype) for a in arrays]


def _other_chips(x, y):
    return [(1 - x, y), (x, 1 - y), (1 - x, 1 - y)]


def _device_index(chip, c):
    return 4 * chip[0] + 2 * chip[1] + c


def _chip_index(chip):
    return 2 * chip[0] + chip[1]


class _Exchange:
    def __init__(self, gather, all_cores=False):
        self.gather = gather
        self.all_cores = all_cores
        self.n_peers = N_DEV - 1 if all_cores else 3

    def peers(self, x, y, c):
        if self.all_cores:
            return [(x ^ (k >> 2), y ^ ((k >> 1) & 1), c ^ (k & 1)) for k in range(1, N_DEV)]
        return [(*chip, c) for chip in _other_chips(x, y)]

    def sent(self, x, y, c, peer):
        return _device_index((x, y), c) if self.gather else _chip_index(peer[:2])

    def lands_at(self, x, y, c):
        return _device_index((x, y), c) if self.gather else _chip_index((x, y))

    def arrives_at(self, peer):
        return _device_index(peer[:2], peer[2]) if self.gather else _chip_index(peer[:2])


def _ici_copies_start(sets, sources, landings, exchanges, name, deps=()):
    n = len(landings)
    arrays = (list(sources) if sources is not None else []) + list(landings)
    first_land = len(arrays) - n
    n_sets = len(sets)
    n_deps = len(deps)

    def body(*refs):
        refs = refs[n_deps:]
        src, land = refs[:n], refs[first_land:first_land + n]
        sems = refs[len(arrays):len(arrays) + 2 * n_sets]
        token = refs[-1]
        x, y, c = _position()
        for s, (members, exchange) in enumerate(zip(sets, exchanges)):
            for slot, (k, rows) in enumerate(members):
                for j, peer in enumerate(exchange.peers(x, y, c)):
                    at = exchange.n_peers * slot + j
                    pltpu.make_async_remote_copy(
                        src_ref=_shard_rows(src[k], exchange.sent(x, y, c, peer), rows),
                        dst_ref=_shard_rows(land[k], exchange.lands_at(x, y, c), rows),
                        send_sem=sems[2 * s].at[at], recv_sem=sems[2 * s + 1].at[at],
                        device_id=peer, device_id_type=MESH_ID).start()
        token[...] = jnp.zeros_like(token)

    sem_shapes = []
    for members, exchange in zip(sets, exchanges):
        sem_shapes += [pltpu.SemaphoreType.DMA((exchange.n_peers * len(members),))] * 2
    out = pl.pallas_call(
        body, name=name,
        out_shape=sem_shapes + _hbm_like(arrays) + [TOKEN],
        in_specs=[ANY] * n_deps + [HBM] * len(arrays),
        out_specs=[SEM] * (2 * n_sets) + [HBM] * len(arrays) + [pl.BlockSpec(memory_space=pltpu.VMEM)],
        input_output_aliases={n_deps + i: 2 * n_sets + i for i in range(len(arrays))},
        compiler_params=pltpu.CompilerParams(has_side_effects=DATAFLOW),
    )(*deps, *[_in_hbm(a) for a in arrays])
    sems = [(out[2 * s], out[2 * s + 1]) for s in range(n_sets)]
    thru = list(out[2 * n_sets:2 * n_sets + len(arrays)])
    return sems, (thru[:first_land] if sources is not None else None), thru[first_land:], out[-1]


def _ici_copies_wait(sems, members, sources, landings, exchange, after, name):
    n = len(landings)
    arrays = (list(sources) if sources is not None else []) + list(landings)
    first_land = len(arrays) - n

    def body(*refs):
        src, land = refs[:n], refs[first_land:first_land + n]
        send_sems, recv_sems = refs[len(arrays)], refs[len(arrays) + 1]
        x, y, c = _position()
        for slot, rows in enumerate(members):
            for j, peer in enumerate(exchange.peers(x, y, c)):
                at = exchange.n_peers * slot + j
                cp = pltpu.make_async_remote_copy(
                    src_ref=_shard_rows(src[slot], exchange.sent(x, y, c, peer), rows),
                    dst_ref=_shard_rows(land[slot], exchange.arrives_at(peer), rows),
                    send_sem=send_sems.at[at], recv_sem=recv_sems.at[at], device_id=peer, device_id_type=MESH_ID)
                cp.wait_send()
                cp.wait_recv()

    out = pl.pallas_call(
        body, name=name, out_shape=_hbm_like(arrays),
        in_specs=[HBM] * len(arrays) + [SEM, SEM] + [ANY] * len(after), out_specs=[HBM] * len(arrays),
        input_output_aliases={i: i for i in range(len(arrays))},
        compiler_params=pltpu.CompilerParams(has_side_effects=DATAFLOW),
    )(*arrays, sems[0], sems[1], *after)
    return list(out[first_land:])


def _swap_copies(land, rows, which, send_sems, recv_sems):
    x, y, c = _position()
    chips = [([(x, y)] + _other_chips(x, y))[j] for j in which]
    sends, recvs = [], []
    for k in range(len(land)):
        for j, chip in enumerate(chips):
            for copies, core in ((sends, c), (recvs, 1 - c)):
                block = _shard_rows(land[k], _device_index(chip, core), rows[k])
                copies.append(pltpu.make_async_remote_copy(
                    src_ref=block, dst_ref=block, send_sem=send_sems.at[k, j], recv_sem=recv_sems.at[k, j],
                    device_id=(x, y, 1 - c), device_id_type=MESH_ID))
    return sends, recvs


def _d2d_gather(buffers, rows, name, which=(0, 1, 2, 3), deps=()):
    n = len(buffers)
    n_deps = len(deps)

    def body(*refs):
        sends, recvs = _swap_copies(refs[n_deps + n:n_deps + 2 * n], rows, which, *refs[n_deps + 2 * n:])
        for cp in sends:
            cp.start()
        for cp in recvs:
            cp.wait_recv()
        for cp in sends:
            cp.wait_send()

    return pl.pallas_call(
        body, name=name, out_shape=[jax.ShapeDtypeStruct(a.shape, a.dtype) for a in buffers],
        in_specs=[ANY] * (n_deps + n), out_specs=[ANY] * n, input_output_aliases={n_deps + i: i for i in range(n)},
        scratch_shapes=[pltpu.SemaphoreType.DMA((n, len(which))), pltpu.SemaphoreType.DMA((n, len(which)))],
    )(*deps, *buffers)


def _call_with_swap(body, args, swap, prefetch=(), **kw):
    buffers, rows = swap
    n, n_pre, n_in, n_out = len(buffers), len(prefetch), len(args), len(kw["out_shape"])
    n_scratch = len(kw["scratch_shapes"])
    grid = kw["grid"]
    which = (0, 1, 2, 3)

    def at_step(last):
        hit = [pl.program_id(a) == (extent - 1 if last else 0) for a, extent in enumerate(grid)]
        return functools.reduce(jnp.logical_and, hit)

    def hosted(*refs):
        pre, ins, refs = refs[:n_pre], refs[n_pre:n_pre + n_in], refs[n_pre + n_in + n:]
        outs, land, scratch = refs[:n_out], refs[n_out:n_out + n], refs[n_out + n:n_out + n + n_scratch]
        sends, recvs = _swap_copies(land, rows, which, *refs[n_out + n + n_scratch:])

        @pl.when(at_step(False))
        def _():
            for cp in sends:
                cp.start()

        body(*pre, *ins, *outs, *scratch)

        @pl.when(at_step(True))
        def _():
            for cp in recvs:
                cp.wait_recv()
            for cp in sends:
                cp.wait_send()

    sem_shape = pltpu.SemaphoreType.DMA((n, len(which)))
    aliases = {**kw.get("input_output_aliases", {}), **{n_in + i: n_out + i for i in range(n)}}
    out = pl.pallas_call(
        hosted,
        grid_spec=pltpu.PrefetchScalarGridSpec(
            num_scalar_prefetch=n_pre, grid=grid, in_specs=kw["in_specs"] + [ANY] * n, out_specs=kw["out_specs"] + [ANY] * n,
            scratch_shapes=kw["scratch_shapes"] + [sem_shape, sem_shape]),
        out_shape=kw["out_shape"] + [jax.ShapeDtypeStruct(a.shape, a.dtype) for a in buffers],
        input_output_aliases={n_pre + i: o for i, o in aliases.items()},
        compiler_params=kw["compiler_params"], name=kw["name"])(*prefetch, *args, *buffers)
    return out[:n_out], out[n_out:]


def _rs_pair(grads, name):
    n = len(grads)
    rows = [g.shape[0] // N_DEV for g in grads]

    def body(*refs):
        ins, outs = refs[:n], refs[n:2 * n]
        send_sems, recv_sems = refs[2 * n:]
        x, y, c = _position()
        copies = []
        for k in range(n):
            for q in range(4):
                copies.append(pltpu.make_async_remote_copy(
                    src_ref=_shard_rows(ins[k], 2 * q + 1 - c, rows[k]), dst_ref=_shard_rows(outs[k], q, rows[k]),
                    send_sem=send_sems.at[k, q], recv_sem=recv_sems.at[k, q], device_id=(x, y, 1 - c),
                    device_id_type=MESH_ID))
        for cp in copies:
            cp.start()
        for cp in copies:
            cp.wait()

    return pl.pallas_call(
        body, out_shape=[jax.ShapeDtypeStruct((4 * r, g.shape[1]), g.dtype) for g, r in zip(grads, rows)],
        in_specs=[ANY] * n, out_specs=[ANY] * n,
        scratch_shapes=[pltpu.SemaphoreType.DMA((n, 4)), pltpu.SemaphoreType.DMA((n, 4))],
        name=name)(*grads)


def _wgrad_pair(lhs, rhs, name, *, lhs_is_transposed, deps=()):
    t = rhs.shape[0]
    n = lhs.shape[0] if lhs_is_transposed else lhs.shape[1]
    r = n // N_DEV
    n_chips = N_DEV // 2
    per = 1 if (2 * r) % BLK == 0 else 2
    steps = n_chips // per

    def body(l_ref, r_ref, kept_ref, recv_ref, res, send_sems, recv_sems):
        q = pl.program_id(0)
        slot = q % 2
        x, y, c = _position()

        def send(step, buf, i):
            return pltpu.make_async_remote_copy(
                src_ref=res.at[buf, pl.ds(pl.multiple_of((2 * i + 1 - c) * r, 16), r), :],
                dst_ref=_shard_rows(recv_ref, step * per + i, r),
                send_sem=send_sems.at[buf, i], recv_sem=recv_sems.at[step * per + i],
                device_id=(x, y, 1 - c), device_id_type=MESH_ID)

        @pl.when(q >= 2)
        def _():
            for i in range(per):
                send(q - 2, slot, i).wait_send()

        if lhs_is_transposed:
            res[slot] = _dot(l_ref[...], r_ref[...]).astype(BF)
        else:
            res[slot] = _dot_tn(l_ref[...], r_ref[...]).astype(BF)
        for i in range(per):
            kept_ref[i * r:(i + 1) * r, :] = res[slot, pl.ds(pl.multiple_of((2 * i + c) * r, 16), r), :]
            send(q, slot, i).start()

        @pl.when(q == steps - 1)
        def _():
            for i in range(per):
                if steps > 1:
                    send(q - 1, 1 - slot, i).wait_send()
                send(q, slot, i).wait_send()
            for chip in range(n_chips):
                send(chip // per, 0, chip % per).wait_recv()

    width = 2 * r * per
    lhs_spec = pl.BlockSpec((width, t), lambda q: (q, 0)) if lhs_is_transposed else pl.BlockSpec((t, width), lambda q: (0, q))
    return _call(
        body, deps, (lhs, rhs), grid=(steps,),
        in_specs=[lhs_spec, _resident((t, D))],
        out_specs=[pl.BlockSpec((per * r, D), lambda q: (q, 0)), ANY],
        out_shape=[jax.ShapeDtypeStruct((n // 2, D), BF)] * 2,
        scratch_shapes=[pltpu.VMEM((2, width, D), BF), pltpu.SemaphoreType.DMA((2, per)),
                        pltpu.SemaphoreType.DMA((n_chips,))],
        compiler_params=_params(1), name=name)


def _wgrad_pair_sum(lhs, rhs, place, name, *, lhs_is_transposed, deps=()):
    t = rhs.shape[0]
    n = lhs.shape[0] if lhs_is_transposed else lhs.shape[1]
    r = n // N_DEV
    n_chips = N_DEV // 2
    per = 1 if (2 * r) % BLK == 0 else 2
    steps = n_chips // per
    n_deps = len(deps)

    def body(place_ref, *refs):
        l_ref, r_ref, part_ref, land_ref, res, inbox, send_sems, recv_sems = refs[n_deps:]
        q = pl.program_id(0)
        slot = q % 2
        x, y, c = _position()

        def send(step, buf, i):
            return pltpu.make_async_remote_copy(
                src_ref=res.at[buf, pl.ds(pl.multiple_of((2 * i + 1 - c) * r, 16), r), :], dst_ref=inbox.at[step * per + i],
                send_sem=send_sems.at[buf, i], recv_sem=recv_sems.at[step * per + i],
                device_id=(x, y, 1 - c), device_id_type=MESH_ID)

        @pl.when(q < steps)
        def _():
            @pl.when(q >= 2)
            def _():
                for i in range(per):
                    send(q - 2, slot, i).wait_send()

            if lhs_is_transposed:
                res[slot] = _dot(l_ref[...], r_ref[...]).astype(BF)
            else:
                res[slot] = _dot_tn(l_ref[...], r_ref[...]).astype(BF)
            for i in range(per):
                send(q, slot, i).start()

        @pl.when(q >= 1)
        def _():
            for i in range(per):
                chip = (q - 1) * per + i
                send(q - 1, 1 - slot, i).wait_recv()
                kept = res[1 - slot, pl.ds(pl.multiple_of((2 * i + c) * r, 16), r), :]
                total = (kept.astype(F32) + inbox[chip].astype(F32)).astype(BF)
                part_ref[i * r:(i + 1) * r, :] = total

                @pl.when(chip == place_ref[1])
                def _():
                    land_ref[...] = total

        @pl.when(q == steps)
        def _():
            for i in range(per):
                if steps > 1:
                    send(q - 2, slot, i).wait_send()
                send(q - 1, 1 - slot, i).wait_send()

    width = 2 * r * per
    last = steps - 1
    if lhs_is_transposed:
        lhs_spec = pl.BlockSpec((width, t), lambda q, p: (jnp.minimum(q, last), 0))
    else:
        lhs_spec = pl.BlockSpec((t, width), lambda q, p: (0, jnp.minimum(q, last)))
    return pl.pallas_call(
        body,
        grid_spec=pltpu.PrefetchScalarGridSpec(
            num_scalar_prefetch=1, grid=(steps + 1,),
            in_specs=[ANY] * n_deps + [lhs_spec, pl.BlockSpec((t, D), lambda q, p: (0, 0), pipeline_mode=pl.Buffered(1))],
            out_specs=[pl.BlockSpec((per * r, D), lambda q, p: (jnp.maximum(q - 1, 0), 0)),
                       pl.BlockSpec((r, D), lambda q, p: (p[1], 0))],
            scratch_shapes=[pltpu.VMEM((2, width, D), BF), pltpu.VMEM((n_chips, r, D), BF),
                            pltpu.SemaphoreType.DMA((2, per)), pltpu.SemaphoreType.DMA((n_chips,))]),
        out_shape=[jax.ShapeDtypeStruct((n // 2, D), BF)] * 2,
        compiler_params=_params(1), name=name)(place, *deps, lhs, rhs)


def _pair_add(grad, received, place, name, kept_only=False):
    r = received.shape[0] // 4
    parity = 0 if kept_only else 1

    def body(place_ref, g_ref, r_ref, o_ref, land_ref):
        total = (g_ref[...].astype(F32) + r_ref[...].astype(F32)).astype(BF)
        o_ref[...] = total

        @pl.when(pl.program_id(0) == place_ref[1])
        def _():
            land_ref[...] = total

    return pl.pallas_call(
        body,
        grid_spec=pltpu.PrefetchScalarGridSpec(
            num_scalar_prefetch=1, grid=(4,),
            in_specs=[pl.BlockSpec((r, D), lambda q, p: ((1 + parity) * q + parity * p[0], 0)),
                      pl.BlockSpec((r, D), lambda q, p: (q, 0))],
            out_specs=[pl.BlockSpec((r, D), lambda q, p: (q, 0)), pl.BlockSpec((r, D), lambda q, p: (p[1], 0))]),
        out_shape=[jax.ShapeDtypeStruct(received.shape, BF)] * 2,
        compiler_params=_params(1), name=name)(place, grad, received)


def _sum_blocks(gathered, rows):
    def body(b_ref, o_ref):
        acc = b_ref[0:rows, :]
        for d in range(1, N_DEV):
            acc = acc + b_ref[d * rows:(d + 1) * rows, :]
        o_ref[...] = acc

    return pl.pallas_call(body, out_shape=jax.ShapeDtypeStruct((rows, D), F32), name="small_sum")(gathered)


def _adamw_math(w, g, m, v):
    m = ADAM_B1 * m + (1.0 - ADAM_B1) * g
    v = ADAM_B2 * v + (1.0 - ADAM_B2) * (g * g)
    m_hat = m / (1.0 - ADAM_B1 ** ADAM_STEP)
    v_hat = v / (1.0 - ADAM_B2 ** ADAM_STEP)
    delta = -ADAM_LR * (m_hat / (jnp.sqrt(v_hat) + ADAM_EPS) + ADAM_WD * w)
    return delta, m, v


def _sum_partials(blocks):
    g = blocks[0].astype(F32)
    for blk in blocks[1:]:
        g = g + blk.astype(F32)
    return g


def _reduce_adamw(landed, w, m, v, name):
    r = w.shape[0]
    tr = 352 if r % 352 == 0 else r
    per = r // tr

    def body(r0, r1, r2, r3, w_ref, m_ref, v_ref, g_ref, d_ref, nm_ref, nv_ref):
        g = _sum_partials([r0[...], r1[...], r2[...], r3[...]])
        g_ref[...] = g
        d_ref[...], nm_ref[...], nv_ref[...] = _adamw_math(w_ref[...], g, m_ref[...], v_ref[...])

    tile = _row_tile(tr, D)
    return pl.pallas_call(
        body, grid=(per,),
        in_specs=[pl.BlockSpec((tr, D), lambda i, q=q: (q * per + i, 0)) for q in range(4)] + [tile] * 3,
        out_specs=[tile] * 4, out_shape=[jax.ShapeDtypeStruct(w.shape, F32)] * 4,
        compiler_params=_params(1), name=name)(landed, landed, landed, landed, w, m, v)


def _adamw_small(w, g, m, v, name):
    def body(w_ref, g_ref, m_ref, v_ref, d_ref, nm_ref, nv_ref):
        d_ref[...], nm_ref[...], nv_ref[...] = _adamw_math(w_ref[...], g_ref[...], m_ref[...], v_ref[...])

    return pl.pallas_call(body, out_shape=[jax.ShapeDtypeStruct(w.shape, F32)] * 3, name=name)(w, g, m, v)


WEIGHTS = ("ffn1_norm", "ffn1_w_in", "ffn1_w_out", "mix_norm", "w_in", "conv_dw_kernel", "conv_dw_bias", "conv_ln_g",
           "conv_ln_b", "conv_w_proj", "q_norm", "k_norm", "attn_sinks", "rel_bias", "attn_w_o", "w_out", "ffn2_norm",
           "ffn2_w_in", "ffn2_w_out")
MATRICES = ("ffn1_w_in", "ffn1_w_out", "w_in", "conv_w_proj", "attn_w_o", "w_out", "ffn2_w_in", "ffn2_w_out")
COLUMN_SHARDED = ("ffn1_w_in", "w_in", "ffn2_w_in")
ROW_VECTORS = ("ffn1_norm", "mix_norm", "conv_dw_bias", "conv_ln_g", "conv_ln_b", "ffn2_norm")
PACKED = (("q_norm", HD), ("k_norm", HD), ("attn_sinks", NQ), ("rel_bias", NBUCKET * NQ))
GATHER = _Exchange(gather=True)
GATHER_ALL = _Exchange(gather=True, all_cores=True)
SCATTER = _Exchange(gather=False)
FIRST = "ffn1_w_in"
GATHER_STAGES = ("ffn1_out", "mix_proj", "mix_merge", "ffn2")
STAGE_GATHER = {"ffn1_out": GATHER, "mix_proj": GATHER, "mix_merge": GATHER, "ffn2": GATHER_ALL}
STAGE_MEMBERS = {"ffn1_out": ("ffn1_w_out",),
                 "mix_proj": ("w_in", "taps"), "mix_merge": ("conv_w_proj", "attn_w_o", "w_out"),
                 "ffn2": ("ffn2_w_in", "ffn2_w_out")}
ROW_PACKED = len(ROW_VECTORS)
ROW_LOSS = ROW_PACKED + 1
ROW_TAPS = 8
PAYLOAD_ROWS = 48


def _pack_small(values, last_row):
    packed = jnp.concatenate([values[k].reshape(-1) for k, _ in PACKED])
    packed = jnp.pad(packed, (0, D - packed.shape[0])).reshape(1, D)
    return jnp.concatenate([values[k].reshape(1, D) for k in ROW_VECTORS] + [packed, last_row], axis=0)


def _unpack_small(rows):
    out = {k: rows[i] for i, k in enumerate(ROW_VECTORS)}
    at = 0
    for k, size in PACKED:
        out[k] = rows[ROW_PACKED, at:at + size]
        at += size
    out["rel_bias"] = out["rel_bias"].reshape(NBUCKET, NQ)
    return out


def kernel(x, ffn1_norm, ffn1_w_in, ffn1_w_out, mix_norm, w_in, conv_dw_kernel, conv_dw_bias, conv_ln_g, conv_ln_b, conv_w_proj, q_norm, k_norm, attn_sinks, rel_bias, attn_w_o, w_out, ffn2_norm, ffn2_w_in, ffn2_w_out, loss_target, m_ffn1_norm, m_ffn1_w_in, m_ffn1_w_out, m_mix_norm, m_w_in, m_conv_dw_kernel, m_conv_dw_bias, m_conv_ln_g, m_conv_ln_b, m_conv_w_proj, m_q_norm, m_k_norm, m_attn_sinks, m_rel_bias, m_attn_w_o, m_w_out, m_ffn2_norm, m_ffn2_w_in, m_ffn2_w_out, v_ffn1_norm, v_ffn1_w_in, v_ffn1_w_out, v_mix_norm, v_w_in, v_conv_dw_kernel, v_conv_dw_bias, v_conv_ln_g, v_conv_ln_b, v_conv_w_proj, v_q_norm, v_k_norm, v_attn_sinks, v_rel_bias, v_attn_w_o, v_w_out, v_ffn2_norm, v_ffn2_w_in, v_ffn2_w_out):
    w = dict(ffn1_norm=ffn1_norm, ffn1_w_in=ffn1_w_in, ffn1_w_out=ffn1_w_out, mix_norm=mix_norm, w_in=w_in,
             conv_dw_kernel=conv_dw_kernel, conv_dw_bias=conv_dw_bias, conv_ln_g=conv_ln_g, conv_ln_b=conv_ln_b,
             conv_w_proj=conv_w_proj, q_norm=q_norm, k_norm=k_norm, attn_sinks=attn_sinks, rel_bias=rel_bias,
             attn_w_o=attn_w_o, w_out=w_out, ffn2_norm=ffn2_norm, ffn2_w_in=ffn2_w_in, ffn2_w_out=ffn2_w_out)
    m = dict(ffn1_norm=m_ffn1_norm, ffn1_w_in=m_ffn1_w_in, ffn1_w_out=m_ffn1_w_out, mix_norm=m_mix_norm, w_in=m_w_in,
             conv_dw_kernel=m_conv_dw_kernel, conv_dw_bias=m_conv_dw_bias, conv_ln_g=m_conv_ln_g, conv_ln_b=m_conv_ln_b,
             conv_w_proj=m_conv_w_proj, q_norm=m_q_norm, k_norm=m_k_norm, attn_sinks=m_attn_sinks, rel_bias=m_rel_bias,
             attn_w_o=m_attn_w_o, w_out=m_w_out, ffn2_norm=m_ffn2_norm, ffn2_w_in=m_ffn2_w_in, ffn2_w_out=m_ffn2_w_out)
    v = dict(ffn1_norm=v_ffn1_norm, ffn1_w_in=v_ffn1_w_in, ffn1_w_out=v_ffn1_w_out, mix_norm=v_mix_norm, w_in=v_w_in,
             conv_dw_kernel=v_conv_dw_kernel, conv_dw_bias=v_conv_dw_bias, conv_ln_g=v_conv_ln_g, conv_ln_b=v_conv_ln_b,
             conv_w_proj=v_conv_w_proj, q_norm=v_q_norm, k_norm=v_k_norm, attn_sinks=v_attn_sinks, rel_bias=v_rel_bias,
             attn_w_o=v_attn_w_o, w_out=v_w_out, ffn2_norm=v_ffn2_norm, ffn2_w_in=v_ffn2_w_in, ffn2_w_out=v_ffn2_w_out)
    px, py, pc = _position()
    me = 4 * px + 2 * py + pc
    place = jnp.stack([pc, 2 * px + py]).astype(jnp.int32)

    rows_of = lambda k, a: a.T if k in COLUMN_SHARDED else a
    me1 = me.astype(jnp.int32).reshape(1)
    rest = tuple(k for k in MATRICES if k != FIRST)
    shard_rows = dict({k: rows_of(k, w[k]).shape[0] for k in MATRICES}, taps=CWP)
    sems_first, _, thru_first, token = _ici_copies_start(
        [[(0, shard_rows[FIRST])]], None, _prep([rows_of(FIRST, w[FIRST])], None, me1, "prep_first"), [GATHER],
        "gather_start_first")
    buffers = dict(zip(rest + ("taps",), _prep([rows_of(k, w[k]) for k in rest], conv_dw_kernel, me1, "prep", deps=[token])))
    landings, sets = [], []
    for stage in GATHER_STAGES:
        sets.append([(len(landings) + i, shard_rows[k]) for i, k in enumerate(STAGE_MEMBERS[stage])])
        landings += list(STAGE_MEMBERS[stage])
    sems, _, land_thru, started = _ici_copies_start(sets, None, [buffers[k] for k in landings],
                                                    [STAGE_GATHER[s] for s in GATHER_STAGES], "gather_start")

    packed = [_pack_small(a, jnp.zeros((1, D), F32)) for a in (w, m, v)]

    def ffn1_up(x, g, after):
        chips = jnp.stack([_chip_index(chip) for chip in [(px, py)] + _other_chips(px, py)]).astype(jnp.int32)
        rows = [shard_rows[FIRST]]
        mine = _d2d_gather(thru_first, rows, "gather_d2d_first_mine", which=(0,), deps=[started])
        n, u = _ffn_up_blocks(x, g, None, mine[0], chips[:1], None, "ffn1_up_mine")
        landed = _ici_copies_wait(sems_first[0], rows, None, mine, GATHER, [u, *after, *packed], "gather_wait_first")
        w_in_t, = _d2d_gather(landed, rows, "gather_d2d_first", which=(1, 2, 3))
        n, u = _ffn_up_blocks(None, None, n, w_in_t, chips[1:3], u, "ffn1_up_next")
        (n, u), w1 = weights_of("ffn1_out", (u,), during=functools.partial(
            _ffn_up_blocks, None, None, n, w_in_t, chips[3:], u, "ffn1_up"))
        return n, u, dict(w1, ffn1_w_in=w_in_t)

    def weights_of(stage, after, during=None):
        s = GATHER_STAGES.index(stage)
        rows = [r for _, r in sets[s]]
        landed = _ici_copies_wait(sems[s], rows, None, [land_thru[k] for k, _ in sets[s]], STAGE_GATHER[stage],
                                  list(after), "gather_wait_" + stage)
        if during is not None:
            results, landed = during(swap=(landed, rows))
        elif not STAGE_GATHER[stage].all_cores:
            landed = _d2d_gather(landed, rows, "gather_d2d_" + stage)
        out = dict(zip(STAGE_MEMBERS[stage], landed))
        if "taps" in out:
            taps = out.pop("taps")
            out["conv_dw_kernel"] = jnp.transpose(taps.reshape(N_DEV, CWP, BLK), (1, 0, 2)).reshape(CWP, D)[:CW]
        return out if during is None else (results, out)

    in_flight = []

    def wgrad(lhs, rhs, name, lhs_is_transposed, deps=()):
        rows = (lhs.shape[0] if lhs_is_transposed else lhs.shape[1]) // N_DEV
        if rows <= WGRAD_SUM_MAX_ROWS:
            return ("summed",) + tuple(_wgrad_pair_sum(lhs, rhs, place, name, lhs_is_transposed=lhs_is_transposed, deps=deps))
        return ("paired",) + tuple(_wgrad_pair(lhs, rhs, name, lhs_is_transposed=lhs_is_transposed, deps=deps))

    def grads_done(stage, grads):
        names = list(grads)
        added = []
        for k in names:
            if not isinstance(grads[k], tuple):
                received, = _rs_pair([grads[k]], "rs_pair_" + k)
                added.append(_pair_add(grads[k], received, place, "pair_add_" + k))
            elif grads[k][0] == "paired":
                added.append(_pair_add(grads[k][1], grads[k][2], place, "pair_add_" + k, kept_only=True))
            else:
                added.append(grads[k][1:])
        partials = [p for p, _ in added]
        members = [(i, p.shape[0] // 4) for i, p in enumerate(partials)]
        sem, p_thru, l_thru, token = _ici_copies_start([members], partials, [l for _, l in added], [SCATTER],
                                                       "scatter_start_" + stage)
        in_flight.append((stage, names, sem[0], p_thru, l_thru, token))
        return [token]

    small = []

    def small_done(gv, sq):
        payload = jnp.concatenate([_pack_small(gv, sq), jnp.pad(gv["conv_dw_kernel"], ((0, PAYLOAD_ROWS - ROW_TAPS - CW), (0, 0)))],
                                  axis=0)
        mine = lax.dynamic_update_slice_in_dim(lax.empty((N_DEV * PAYLOAD_ROWS, D), F32), payload, me * PAYLOAD_ROWS, axis=0)
        sems, _, thru, token = _ici_copies_start([[(0, PAYLOAD_ROWS)]], None, [mine], [GATHER_ALL], "small_start")
        small.append((sems[0], thru))
        return [token]

    vec = {k: w[k] for k in WEIGHTS if k not in MATRICES and k != "conv_dw_kernel"}
    dx0 = _local_step(x[0], loss_target[0], vec, ffn1_up, weights_of, wgrad, grads_done, small_done)
    gathered, = _ici_copies_wait(small[0][0], [PAYLOAD_ROWS], None, small[0][1], GATHER_ALL, [in_flight[-1][-1]], "small_wait")
    total = _sum_blocks(gathered, PAYLOAD_ROWS)
    loss = (0.5 / D) * jnp.sum(total[ROW_LOSS])

    grads, delta, new_m, new_v = {}, {}, {}, {}
    after = [total]
    for stage, names, sem, p_thru, l_thru, _ in in_flight:
        landed = _ici_copies_wait(sem, [p.shape[0] // 4 for p in p_thru], p_thru, l_thru, SCATTER, after,
                                  "scatter_wait_" + stage)
        after = []
        for k, buf in zip(names, landed):
            out = _reduce_adamw(buf, rows_of(k, w[k]), rows_of(k, m[k]), rows_of(k, v[k]), "adamw_" + k)
            grads[k], delta[k], new_m[k], new_v[k] = [rows_of(k, a) for a in out]
            after.append(out[1])
    d8, m8, v8 = _adamw_small(packed[0], total[:ROW_TAPS], packed[1], packed[2], "adamw_small")
    grads.update(_unpack_small(total[:ROW_TAPS]))
    delta.update(_unpack_small(d8))
    new_m.update(_unpack_small(m8))
    new_v.update(_unpack_small(v8))
    k = "conv_dw_kernel"
    grads[k] = lax.dynamic_slice_in_dim(total[ROW_TAPS:ROW_TAPS + CW], me * BLK, BLK, axis=1)
    delta[k], new_m[k], new_v[k] = _adamw_small(w[k], grads[k], m[k], v[k], "adamw_taps")

    return (loss, dx0[None], *[grads[k] for k in WEIGHTS], *[delta[k] for k in WEIGHTS],
            *[new_m[k] for k in WEIGHTS], *[new_v[k] for k in WEIGHTS])
```

```python
import functools
import math

import numpy as np
import jax
import jax.numpy as jnp
from jax import lax
from jax.experimental import pallas as pl
from jax.experimental.pallas import tpu as pltpu

F32 = jnp.float32
BF = jnp.bfloat16

D = 1024
F = 2816
INW = 5632
CW = 31
CWP = 32
HD = 64
NQ = 16
NKV = 4
GRP = NQ // NKV
BLK = 128
NBUCKET = 32
EPS = 1e-6
NEG = float(jnp.finfo(jnp.float32).min)
QK_SCALE = 1.0 / math.sqrt(HD)
R_CONV = (0, 2048)
R_QKV = (2048, 3584)
R_Q = (2048, 3072)
R_KV = (3072, 3584)
R_GATE = (3584, 5632)

N_DEV = 8
VMEM_LIMIT_V7X = 56 * 1024 * 1024
ROW_TILE = 256
ROW_TILE_WIDE = 512
ROW_TILE_BLOCK = 1024
WGRAD_SUM_MAX_ROWS = 352

ADAM_LR = 0.001
ADAM_B1 = 0.9
ADAM_B2 = 0.999
ADAM_EPS = 1e-08
ADAM_WD = 0.01
ADAM_STEP = 10

NT_DIMS = (((1,), (1,)), ((), ()))
TN_DIMS = (((0,), (0,)), ((), ()))


def _dot(a, b):
    return jnp.dot(a, b, preferred_element_type=F32)


def _dot_nt(a, b):
    return lax.dot_general(a, b, NT_DIMS, preferred_element_type=F32)


def _dot_tn(a, b):
    return lax.dot_general(a, b, TN_DIMS, preferred_element_type=F32)


def _sig(x):
    return 0.5 * jnp.tanh(0.5 * x) + 0.5


ANY = pl.BlockSpec(memory_space=pl.ANY)


def _call(body, deps, args, **kw):
    n = len(deps)
    if n:
        kw["in_specs"] = [ANY] * n + list(kw["in_specs"])
        return pl.pallas_call(lambda *refs: body(*refs[n:]), **kw)(*deps, *args)
    return pl.pallas_call(body, **kw)(*args)


def _params(n_axes):
    return pltpu.CompilerParams(dimension_semantics=("arbitrary",) * n_axes, vmem_limit_bytes=VMEM_LIMIT_V7X)


def _resident(shape):
    zeros = (0,) * len(shape)
    return pl.BlockSpec(shape, lambda *_: zeros, pipeline_mode=pl.Buffered(1))


def _row_tile(rows, cols):
    return pl.BlockSpec((rows, cols), lambda i: (i, 0))


def _rms_stats(x):
    r = lax.rsqrt(jnp.mean(x * x, axis=-1, keepdims=True) + EPS)
    return r, x * r


def _rms_bwd(dn, x, g):
    r, xh = _rms_stats(x)
    dxh = dn * g
    dx = r * (dxh - xh * jnp.mean(dxh * xh, axis=-1, keepdims=True))
    return dx, jnp.sum(dn * xh, axis=0, keepdims=True)


def _ffn_last(x, target, g, w_in_t, w_out, name):
    t = x.shape[0]
    tm = min(ROW_TILE, t)

    def body(x_ref, t_ref, g_ref, w_ref, wo_ref, n_ref, du_ref, h_ref, dy_ref, dx_ref, sq_ref, dg_ref):
        @pl.when(pl.program_id(0) == 0)
        def _():
            sq_ref[...] = jnp.zeros_like(sq_ref)
            dg_ref[...] = jnp.zeros_like(dg_ref)

        x = x_ref[...]
        g = g_ref[...]
        r, xh = _rms_stats(x)
        n = (xh * g).astype(BF)
        n_ref[...] = n
        u = _dot_nt(n, w_ref[...])
        a = u[:, :F]
        b = u[:, F:]
        s = _sig(a)
        sa = a * s
        h = (sa * b).astype(BF)
        h_ref[...] = h
        err = x + 0.5 * _dot(h, wo_ref[...]) - t_ref[...]
        sq_ref[...] += jnp.sum(err * err, axis=0, keepdims=True)
        dxo = err * (1.0 / D)
        dy = (0.5 * dxo).astype(BF)
        dy_ref[...] = dy
        dh = _dot_nt(dy, wo_ref[...])
        du_ref[:, :F] = (dh * b * (s * (1.0 + a * (1.0 - s)))).astype(BF)
        du_ref[:, F:] = (dh * sa).astype(BF)
        dn = _dot(du_ref[...], w_ref[...])
        dxh = dn * g
        dx_ref[...] = dxo + r * (dxh - xh * jnp.mean(dxh * xh, axis=-1, keepdims=True))
        dg_ref[...] += jnp.sum(dn * xh, axis=0, keepdims=True)

    vec = pl.BlockSpec((1, D), lambda i: (0, 0))
    return pl.pallas_call(
        body, grid=(t // tm,),
        in_specs=[_row_tile(tm, D), _row_tile(tm, D), _resident((1, D)), _resident((INW, D)), _resident((F, D))],
        out_specs=[_row_tile(tm, D), _row_tile(tm, INW), _row_tile(tm, F), _row_tile(tm, D), _row_tile(tm, D), vec, vec],
        out_shape=[jax.ShapeDtypeStruct((t, D), BF), jax.ShapeDtypeStruct((t, INW), BF), jax.ShapeDtypeStruct((t, F), BF),
                   jax.ShapeDtypeStruct((t, D), BF), jax.ShapeDtypeStruct((t, D), F32), jax.ShapeDtypeStruct((1, D), F32),
                   jax.ShapeDtypeStruct((1, D), F32)],
        compiler_params=_params(1), name=name)(x, target, g, w_in_t, w_out)


def _ffn_up_blocks(x, g, n, w_in_t, order, u, name, deps=(), swap=None):
    t = (x if n is None else n).shape[0]
    tm = min(ROW_TILE_BLOCK, t)
    c = INW * 2 // N_DEV
    n_deps = len(deps)
    first = n is None
    assert not first or order.shape == (1,)

    def body(order_ref, *refs):
        refs = refs[n_deps:]
        if first:
            x_ref, g_ref, w_ref, n_ref, u_ref = refs
            nt = (_rms_stats(x_ref[...])[1] * g_ref[...]).astype(BF)
            n_ref[...] = nt
        else:
            n_ref, w_ref, _, u_ref = refs
            nt = n_ref[...]
        u_ref[...] = _dot_nt(nt, w_ref[...]).astype(BF)

    rows = pl.BlockSpec((tm, D), lambda k, i, o: (i, 0))
    block = pl.BlockSpec((c, D), lambda k, i, o: (o[k], 0))
    cols = pl.BlockSpec((tm, c), lambda k, i, o: (i, o[k]))
    u_shape = jax.ShapeDtypeStruct((t, INW), BF)
    if first:
        args, in_specs = (x, g, w_in_t), [rows, _resident((1, D)), block]
        out_specs, out_shape, aliases = [rows, cols], [jax.ShapeDtypeStruct((t, D), BF), u_shape], {}
    else:
        args, in_specs = (n, w_in_t, u), [rows, block, ANY]
        out_specs, out_shape, aliases = [cols], [u_shape], {1 + n_deps + 2: 0}
    grid = (order.shape[0], t // tm)
    if swap is not None:
        (out,), swapped = _call_with_swap(
            body, (*deps, *args), swap, prefetch=(order,), grid=grid, in_specs=[ANY] * n_deps + in_specs, out_specs=out_specs,
            out_shape=out_shape, scratch_shapes=[], input_output_aliases={n_deps + 2: 0}, compiler_params=_params(2), name=name)
        return (n, out), swapped
    out = pl.pallas_call(
        body,
        grid_spec=pltpu.PrefetchScalarGridSpec(num_scalar_prefetch=1, grid=grid, in_specs=[ANY] * n_deps + in_specs,
                                               out_specs=out_specs),
        out_shape=out_shape, input_output_aliases=aliases, compiler_params=_params(2), name=name)(order, *deps, *args)
    return tuple(out) if first else (n, out[0])


def _ffn_down(x, u, w_out, name, part=(0, 1), out=None, swap=None):
    t = x.shape[0]
    tm = min(ROW_TILE_WIDE, t)
    steps = t // tm // part[1]
    first = part[0] * steps
    others = [out] if out is not None else []

    def body(x_ref, u_ref, wo_ref, *rest):
        a = u_ref[:, :F].astype(F32)
        b = u_ref[:, F:].astype(F32)
        h = (a * _sig(a) * b).astype(BF)
        rest[-1][...] = x_ref[...] + 0.5 * _dot(h, wo_ref[...])

    tile = lambda cols: pl.BlockSpec((tm, cols), lambda i: (first + i, 0))
    kw = dict(grid=(steps,), in_specs=[tile(D), tile(INW), _resident((F, D))] + [ANY] * len(others), out_specs=[tile(D)],
              out_shape=[jax.ShapeDtypeStruct((t, D), F32)], scratch_shapes=[],
              input_output_aliases={3: 0} if others else {}, compiler_params=_params(1), name=name)
    args = (x, u, w_out, *others)
    return pl.pallas_call(body, **kw)(*args) if swap is None else _call_with_swap(body, args, swap, **kw)


def _ffn_bwd(dxo, x, g, u, w_in_t, w_out, name, deps=()):
    t = x.shape[0]
    tm = min(ROW_TILE, t)

    def body(dxo_ref, x_ref, g_ref, u_ref, w_ref, wo_ref, dx_ref, du_ref, h_ref, dy_ref, dg_ref):
        dxo = dxo_ref[...]
        dy = (0.5 * dxo).astype(BF)
        dy_ref[...] = dy
        dh = _dot_nt(dy, wo_ref[...])
        a = u_ref[:, :F].astype(F32)
        b = u_ref[:, F:].astype(F32)
        s = _sig(a)
        sa = a * s
        h_ref[...] = (sa * b).astype(BF)
        du_ref[:, :F] = (dh * b * (s * (1.0 + a * (1.0 - s)))).astype(BF)
        du_ref[:, F:] = (dh * sa).astype(BF)
        dn = _dot(du_ref[...], w_ref[...])
        dx, dg = _rms_bwd(dn, x_ref[...], g_ref[...])
        dx_ref[...] = dxo + dx

        @pl.when(pl.program_id(0) == 0)
        def _():
            dg_ref[...] = jnp.zeros_like(dg_ref)

        dg_ref[...] += dg

    return _call(
        body, deps, (dxo, x, g, u, w_in_t, w_out), grid=(t // tm,),
        in_specs=[_row_tile(tm, D), _row_tile(tm, D), _resident((1, D)), _row_tile(tm, INW), _resident((INW, D)),
                  _resident((F, D))],
        out_specs=[_row_tile(tm, D), _row_tile(tm, INW), _row_tile(tm, F), _row_tile(tm, D),
                   pl.BlockSpec((1, D), lambda i: (0, 0))],
        out_shape=[jax.ShapeDtypeStruct((t, D), F32), jax.ShapeDtypeStruct((t, INW), BF), jax.ShapeDtypeStruct((t, F), BF),
                   jax.ShapeDtypeStruct((t, D), BF), jax.ShapeDtypeStruct((1, D), F32)],
        compiler_params=_params(1), name=name)


def _wgrad(lhs, rhs, name, *, lhs_is_transposed, chunk, deps=()):
    t = rhs.shape[0]
    n = lhs.shape[0] if lhs_is_transposed else lhs.shape[1]
    c = min(chunk, n)

    def body(l_ref, r_ref, o_ref):
        if lhs_is_transposed:
            o_ref[...] = _dot(l_ref[...], r_ref[...]).astype(BF)
        else:
            o_ref[...] = _dot_tn(l_ref[...], r_ref[...]).astype(BF)

    lhs_spec = pl.BlockSpec((c, t), lambda j: (j, 0)) if lhs_is_transposed else pl.BlockSpec((t, c), lambda j: (0, j))
    return _call(
        body, deps, (lhs, rhs), grid=(n // c,),
        in_specs=[lhs_spec, _resident((t, D))],
        out_specs=pl.BlockSpec((c, D), lambda j: (j, 0)),
        out_shape=jax.ShapeDtypeStruct((n, D), BF),
        compiler_params=_params(1), name=name)


def _wgrad_mix(duc, dq_t, dkv_t, dgp, hm):
    t = hm.shape[0]
    c = 512
    first_q, first_kv, first_gate = R_Q[0] // c, R_KV[0] // c, R_GATE[0] // c

    def body(uc_ref, q_ref, kv_ref, gp_ref, h_ref, o_ref):
        j = pl.program_id(0)

        @pl.when(j < first_q)
        def _():
            o_ref[...] = _dot_tn(uc_ref[...], h_ref[...]).astype(BF)

        @pl.when((j >= first_q) & (j < first_kv))
        def _():
            o_ref[...] = _dot(q_ref[...], h_ref[...]).astype(BF)

        @pl.when((j >= first_kv) & (j < first_gate))
        def _():
            o_ref[...] = _dot(kv_ref[...], h_ref[...]).astype(BF)

        @pl.when(j >= first_gate)
        def _():
            o_ref[...] = _dot_tn(gp_ref[...], h_ref[...]).astype(BF)

    return pl.pallas_call(
        body, grid=(INW // c,),
        in_specs=[pl.BlockSpec((t, c), lambda j: (0, jnp.clip(j, 0, first_q - 1))),
                  pl.BlockSpec((c, t), lambda j: (jnp.clip(j - first_q, 0, first_kv - first_q - 1), 0)),
                  pl.BlockSpec((c, t), lambda j: (jnp.clip(j - first_kv, 0, first_gate - first_kv - 1), 0)),
                  pl.BlockSpec((t, c), lambda j: (0, jnp.clip(j - first_gate, 0, INW // c - first_gate - 1))),
                  _resident((t, D))],
        out_specs=pl.BlockSpec((c, D), lambda j: (j, 0)),
        out_shape=jax.ShapeDtypeStruct((INW, D), BF),
        compiler_params=_params(1), name="mix_dw_in")(duc, dq_t, dkv_t, dgp, hm)


def _mix_proj(x, g, w_t):
    t = x.shape[0]
    tm = min(ROW_TILE_WIDE, t)

    def body(x_ref, g_ref, w_ref, hm_ref, uc_ref, gp_ref, qkv_ref):
        r, xh = _rms_stats(x_ref[...])
        hm = (xh * g_ref[...]).astype(BF)
        hm_ref[...] = hm
        uc_ref[...] = _dot_nt(hm, w_ref[R_CONV[0]:R_CONV[1], :]).astype(BF)
        gp_ref[...] = _dot_nt(hm, w_ref[R_GATE[0]:R_GATE[1], :]).astype(BF)
        qkv_ref[...] = _dot_nt(w_ref[R_QKV[0]:R_QKV[1], :], hm).astype(BF)

    return pl.pallas_call(
        body, grid=(t // tm,),
        in_specs=[_row_tile(tm, D), _resident((1, D)), _resident((INW, D))],
        out_specs=[_row_tile(tm, D), _row_tile(tm, 2 * D), _row_tile(tm, 2 * D), pl.BlockSpec((1536, tm), lambda i: (0, i))],
        out_shape=[jax.ShapeDtypeStruct((t, D), BF), jax.ShapeDtypeStruct((t, 2 * D), BF),
                   jax.ShapeDtypeStruct((t, 2 * D), BF), jax.ShapeDtypeStruct((1536, t), BF)],
        compiler_params=_params(1), name="mix_proj")(x, g, w_t)


CONV_HALO = 32
CONV_LEAD = CONV_HALO - (CW - 1)


def _glu(uc):
    uc = uc.astype(F32)
    return uc[:, :D] * _sig(uc[:, D:])


def _ln_stats(zc):
    mu = jnp.mean(zc, axis=-1, keepdims=True)
    zm = zc - mu
    r = lax.rsqrt(jnp.mean(zm * zm, axis=-1, keepdims=True) + EPS)
    return r, zm * r


CONV_SHIFTS = 8
CONV_CHUNK = 32


def _store_shifted(buf, rows):
    for b in range(1, CONV_SHIFTS):
        buf[b, 0:rows - 8, :] = buf[0, pl.ds(b, rows - 8), :]


def _conv_fwd(uc, dwk, dwb, lng, lnb, swap=None):
    t = uc.shape[0]
    tm = min(512, t)
    per = tm // CONV_HALO
    ext = tm + CONV_HALO

    def body(cur_ref, prev_ref, k_ref, kb_ref, g_ref, b_ref, o_ref, zc_ref, zsh):
        i = pl.program_id(0)
        zsh[0, 0:CONV_HALO, :] = _glu(prev_ref[...]) * (i > 0).astype(F32)
        zsh[0, CONV_HALO:, :] = _glu(cur_ref[...])
        _store_shifted(zsh, ext)

        def chunk(ci, carry):
            r0 = pl.multiple_of(ci * CONV_CHUNK, CONV_CHUNK)
            acc = jnp.zeros((CONV_CHUNK, D), F32) + kb_ref[...]
            for w in range(CW):
                a, b = divmod(CONV_LEAD + w, 8)
                acc = acc + k_ref[w:w + 1, :] * zsh[b, pl.ds(r0 + 8 * a, CONV_CHUNK), :]
            zc_ref[pl.ds(r0, CONV_CHUNK), :] = acc
            return carry

        lax.fori_loop(0, tm // CONV_CHUNK, chunk, 0)
        r, xh = _ln_stats(zc_ref[...])
        y = xh * g_ref[...] + b_ref[...]
        o_ref[...] = (y * _sig(y)).astype(BF)

    kw = dict(
        grid=(t // tm,),
        in_specs=[_row_tile(tm, 2 * D),
                  pl.BlockSpec((CONV_HALO, 2 * D), lambda i: (jnp.maximum(i * per - 1, 0), 0)),
                  _resident((CWP, D)), _resident((1, D)), _resident((1, D)), _resident((1, D))],
        out_specs=[_row_tile(tm, D), _row_tile(tm, D)],
        out_shape=[jax.ShapeDtypeStruct((t, D), BF), jax.ShapeDtypeStruct((t, D), F32)],
        scratch_shapes=[pltpu.VMEM((CONV_SHIFTS, ext, D), F32)],
        compiler_params=_params(1), name="conv_fwd")
    args = (uc, uc, dwk, dwb, lng, lnb)
    return pl.pallas_call(body, **kw)(*args) if swap is None else _call_with_swap(body, args, swap, **kw)


def _conv_bwd(uc, zc, dzs, dwk, lng, lnb):
    t = uc.shape[0]
    tm = min(ROW_TILE_WIDE, t)
    per = tm // CONV_HALO
    n_tiles = t // tm
    ext = tm + CONV_HALO
    last_block = t // CONV_HALO - 1

    def body(cur_ref, zc_ref, zcn_ref, dz_ref, dzn_ref, k_ref, g_ref, b_ref,
             duc_ref, dk_ref, dkb_ref, dg_ref, db_ref, dsh, dk8, z_scr):
        i = pl.program_id(0)

        @pl.when(i == 0)
        def _():
            dk8[...] = jnp.zeros_like(dk8)
            dkb_ref[...] = jnp.zeros_like(dkb_ref)
            dg_ref[...] = jnp.zeros_like(dg_ref)
            db_ref[...] = jnp.zeros_like(db_ref)

        has_next = (i < n_tiles - 1).astype(F32)
        z_scr[...] = _glu(cur_ref[...])
        gain = g_ref[...]

        def ln_silu_bwd(zc, dzs, live):
            r, xh = _ln_stats(zc)
            y = xh * gain + b_ref[...]
            sy = _sig(y)
            dy = dzs * (sy * (1.0 + y * (1.0 - sy))) * live
            dxh = dy * gain
            dzc = r * (dxh - jnp.mean(dxh, axis=-1, keepdims=True) - xh * jnp.mean(dxh * xh, axis=-1, keepdims=True))
            return dzc, dy, xh

        dzc, dy, xh = ln_silu_bwd(zc_ref[...], dz_ref[...], 1.0)
        dsh[0, 0:tm, :] = dzc
        dg_ref[...] += jnp.sum(dy * xh, axis=0, keepdims=True)
        db_ref[...] += jnp.sum(dy, axis=0, keepdims=True)
        dkb_ref[...] += jnp.sum(dzc, axis=0, keepdims=True)
        dsh[0, tm:, :] = ln_silu_bwd(zcn_ref[...], dzn_ref[...], has_next)[0]
        _store_shifted(dsh, ext)

        def chunk(ci, carry):
            r0 = pl.multiple_of(ci * CONV_CHUNK, CONV_CHUNK)
            z_c = z_scr[pl.ds(r0, CONV_CHUNK), :]
            dz = jnp.zeros((CONV_CHUNK, D), F32)
            for w in range(CW):
                a, b = divmod(CW - 1 - w, 8)
                window = dsh[b, pl.ds(r0 + 8 * a, CONV_CHUNK), :]
                dz = dz + k_ref[w:w + 1, :] * window
                prod = z_c * window
                part = prod[0:8, :]
                for j in range(1, CONV_CHUNK // 8):
                    part = part + prod[8 * j:8 * j + 8, :]
                dk8[w] += part
            ucc = cur_ref[pl.ds(r0, CONV_CHUNK), :].astype(F32)
            sg = _sig(ucc[:, D:])
            duc_ref[pl.ds(r0, CONV_CHUNK), 0:D] = (dz * sg).astype(BF)
            duc_ref[pl.ds(r0, CONV_CHUNK), D:2 * D] = (dz * ucc[:, :D] * sg * (1.0 - sg)).astype(BF)
            return carry

        lax.fori_loop(0, tm // CONV_CHUNK, chunk, 0)

        @pl.when(i == n_tiles - 1)
        def _():
            dk_ref[...] = jnp.sum(dk8[...], axis=1)

    vec = pl.BlockSpec((1, D), lambda i: (0, 0))
    next_halo = pl.BlockSpec((CONV_HALO, D), lambda i: (jnp.minimum((i + 1) * per, last_block), 0))
    return pl.pallas_call(
        body, grid=(n_tiles,),
        in_specs=[_row_tile(tm, 2 * D), _row_tile(tm, D), next_halo, _row_tile(tm, D), next_halo,
                  _resident((CWP, D)), _resident((1, D)), _resident((1, D))],
        out_specs=[_row_tile(tm, 2 * D), pl.BlockSpec((CWP, D), lambda i: (0, 0)), vec, vec, vec],
        out_shape=[jax.ShapeDtypeStruct((t, 2 * D), BF), jax.ShapeDtypeStruct((CWP, D), F32),
                   jax.ShapeDtypeStruct((1, D), F32), jax.ShapeDtypeStruct((1, D), F32), jax.ShapeDtypeStruct((1, D), F32)],
        scratch_shapes=[pltpu.VMEM((CONV_SHIFTS, ext, D), F32), pltpu.VMEM((CWP, 8, D), F32), pltpu.VMEM((tm, D), F32)],
        compiler_params=_params(1), name="conv_bwd")(uc, zc, zc, dzs, dzs, dwk, lng, lnb)


def _norm_rows(xt, g):
    r = lax.rsqrt(jnp.mean(xt * xt, axis=0, keepdims=True) + EPS)
    xh = xt * r
    return xh * g, r, xh


ATT_TQ = 1024


def _attn_specs(t, tq):
    per = tq // BLK
    return [pl.BlockSpec((1536, tq), lambda i: (0, i)),
            pl.BlockSpec((512, BLK), lambda i: (2, jnp.maximum(i * per - 1, 0))),
            _resident((HD, 1)), _resident((HD, 1)), _resident((NKV, 1, GRP * BLK)),
            _resident((2, NKV, 2 * BLK, GRP * BLK))]


def _attn_window(hk, sb, qkv_ref, halo_ref, kn_cur, kn_halo):
    v0 = D + NKV * HD + hk * HD
    if sb == 0:
        k_prev = kn_halo[hk]
        v_prev = halo_ref[NKV * HD + hk * HD:NKV * HD + (hk + 1) * HD, :]
    else:
        k_prev = kn_cur[hk][:, (sb - 1) * BLK:sb * BLK]
        v_prev = qkv_ref[v0:v0 + HD, (sb - 1) * BLK:sb * BLK]
    kw = jnp.concatenate([k_prev, kn_cur[hk][:, sb * BLK:(sb + 1) * BLK]], axis=1).astype(BF)
    vw = jnp.concatenate([v_prev, qkv_ref[v0:v0 + HD, sb * BLK:(sb + 1) * BLK]], axis=1)
    return kw, vw


def _attn_probs(kw, qc, bias, sink):
    st = _dot_tn(kw, qc) + bias
    m = jnp.maximum(jnp.max(st, axis=0, keepdims=True), sink)
    p = jnp.exp(st - m)
    e_sink = jnp.exp(sink - m)
    inv = 1.0 / (jnp.sum(p, axis=0, keepdims=True) + e_sink)
    return p * inv, e_sink * inv


def _attn_fwd(qkv_t, qg, kg, sink_rows, bias_t):
    t = qkv_t.shape[1]
    tq = min(ATT_TQ, t)
    n_sub = tq // BLK

    def body(qkv_ref, halo_ref, qg_ref, kg_ref, sink_ref, bias_ref, o_ref, p_ref, ps_ref):
        i = pl.program_id(0)
        first = (i == 0).astype(jnp.int32)
        kgain = kg_ref[...]
        qgain = qg_ref[...]
        kn_cur = [_norm_rows(qkv_ref[D + h * HD:D + (h + 1) * HD, :].astype(F32), kgain)[0] for h in range(NKV)]
        kn_halo = [_norm_rows(halo_ref[h * HD:(h + 1) * HD, :].astype(F32), kgain)[0] for h in range(NKV)]
        for hk in range(NKV):
            for sb in range(n_sub):
                cols = slice(sb * BLK, (sb + 1) * BLK)
                kw, vw = _attn_window(hk, sb, qkv_ref, halo_ref, kn_cur, kn_halo)
                qc = jnp.concatenate(
                    [_norm_rows(qkv_ref[(GRP * hk + g) * HD:(GRP * hk + g + 1) * HD, cols].astype(F32), qgain)[0] * QK_SCALE
                     for g in range(GRP)], axis=1).astype(BF)
                bias = bias_ref[first, hk] if sb == 0 else bias_ref[0, hk]
                p, p_sink = _attn_probs(kw, qc, bias, sink_ref[hk])
                p = p.astype(BF)
                p_ref[sb, hk] = p
                ps_ref[sb, hk] = p_sink
                o = _dot(vw, p)
                for g in range(GRP):
                    head = GRP * hk + g
                    o_ref[head * HD:(head + 1) * HD, cols] = o[:, g * BLK:(g + 1) * BLK].astype(BF)

    return pl.pallas_call(
        body, grid=(t // tq,),
        in_specs=_attn_specs(t, tq),
        out_specs=[pl.BlockSpec((D, tq), lambda i: (0, i)),
                   pl.BlockSpec((n_sub, NKV, 2 * BLK, GRP * BLK), lambda i: (i, 0, 0, 0)),
                   pl.BlockSpec((n_sub, NKV, 1, GRP * BLK), lambda i: (i, 0, 0, 0))],
        out_shape=[jax.ShapeDtypeStruct((D, t), BF), jax.ShapeDtypeStruct((t // BLK, NKV, 2 * BLK, GRP * BLK), BF),
                   jax.ShapeDtypeStruct((t // BLK, NKV, 1, GRP * BLK), F32)],
        compiler_params=_params(1), name="attn_fwd")(qkv_t, qkv_t, qg, kg, sink_rows, bias_t)


def _attn_bwd(qkv_t, do_t, probs, sink_probs, qg, kg, onehot_t, deps=()):
    t = qkv_t.shape[1]
    tq = min(ATT_TQ, t)
    n_sub = tq // BLK
    n_tiles = t // tq

    def body(qkv_ref, halo_ref, do_ref, p_ref, ps_ref, qg_ref, kg_ref, oh_ref,
             dq_ref, ckv_ref, dqg_ref, dsink_ref, dbias_ref, qg_scr, sink_scr, ds_scr):
        i = pl.program_id(0)

        @pl.when(i == 0)
        def _():
            qg_scr[...] = jnp.zeros_like(qg_scr)
            sink_scr[...] = jnp.zeros_like(sink_scr)
            ds_scr[...] = jnp.zeros_like(ds_scr)

        kgain = kg_ref[...]
        qgain = qg_ref[...]
        kn_cur = [_norm_rows(qkv_ref[D + h * HD:D + (h + 1) * HD, :].astype(F32), kgain)[0] for h in range(NKV)]
        kn_halo = [_norm_rows(halo_ref[h * HD:(h + 1) * HD, :].astype(F32), kgain)[0] for h in range(NKV)]
        dqg = jnp.zeros((HD, BLK), F32)
        for hk in range(NKV):
            for sb in range(n_sub):
                cols = slice(sb * BLK, (sb + 1) * BLK)
                kw, vw = _attn_window(hk, sb, qkv_ref, halo_ref, kn_cur, kn_halo)
                qn, qr, qh = [], [], []
                for g in range(GRP):
                    head = GRP * hk + g
                    n_, r_, h_ = _norm_rows(qkv_ref[head * HD:(head + 1) * HD, cols].astype(F32), qgain)
                    qn.append(n_)
                    qr.append(r_)
                    qh.append(h_)
                qc = (jnp.concatenate(qn, axis=1) * QK_SCALE).astype(BF)
                p_bf = p_ref[sb, hk]
                p = p_bf.astype(F32)
                doc = jnp.concatenate([do_ref[(GRP * hk + g) * HD:(GRP * hk + g + 1) * HD, cols] for g in range(GRP)], axis=1)
                dp = _dot_tn(vw, doc)
                delta = jnp.sum(p * dp, axis=0, keepdims=True)
                ds = p * (dp - delta)
                sink_scr[hk] += -(ps_ref[sb, hk] * delta)
                ds_scr[hk] += ds
                dsb = ds.astype(BF)
                dqc = _dot(kw, dsb) * QK_SCALE
                ckv_ref[sb, hk * HD:(hk + 1) * HD, :] = _dot_nt(qc, dsb)
                ckv_ref[sb, NKV * HD + hk * HD:NKV * HD + (hk + 1) * HD, :] = _dot_nt(doc, p_bf)
                for g in range(GRP):
                    head = GRP * hk + g
                    dqn = dqc[:, g * BLK:(g + 1) * BLK]
                    dqh = dqn * qgain
                    dq = qr[g] * (dqh - qh[g] * jnp.mean(dqh * qh[g], axis=0, keepdims=True))
                    dq_ref[head * HD:(head + 1) * HD, cols] = dq.astype(BF)
                    dqg = dqg + dqn * qh[g]
        qg_scr[...] += dqg

        @pl.when(i == n_tiles - 1)
        def _():
            dqg_ref[...] = jnp.sum(qg_scr[...], axis=1, keepdims=True)
            dsink_ref[...] = _group_lane_sums(sink_scr[:, 0, :])

            def bucket(b, carry):
                oh = jnp.concatenate([oh_ref[b]] * GRP, axis=1)
                dbias_ref[b] = _group_lane_sums(jnp.sum(ds_scr[...] * oh[None], axis=1))
                return carry

            lax.fori_loop(0, NBUCKET, bucket, 0)

    return _call(
        body, deps, (qkv_t, qkv_t, do_t, probs, sink_probs, qg, kg, onehot_t), grid=(n_tiles,),
        in_specs=_attn_specs(t, tq)[:2] + [pl.BlockSpec((D, tq), lambda i: (0, i)),
                                           pl.BlockSpec((n_sub, NKV, 2 * BLK, GRP * BLK), lambda i: (i, 0, 0, 0)),
                                           pl.BlockSpec((n_sub, NKV, 1, GRP * BLK), lambda i: (i, 0, 0, 0))]
        + _attn_specs(t, tq)[2:4] + [_resident((NBUCKET, 2 * BLK, BLK))],
        out_specs=[pl.BlockSpec((D, tq), lambda i: (0, i)),
                   pl.BlockSpec((n_sub, 2 * NKV * HD, 2 * BLK), lambda i: (i, 0, 0)),
                   pl.BlockSpec((HD, 1), lambda i: (0, 0)),
                   pl.BlockSpec((NKV, BLK), lambda i: (0, 0)),
                   pl.BlockSpec((NBUCKET, NKV, BLK), lambda i: (0, 0, 0))],
        out_shape=[jax.ShapeDtypeStruct((D, t), BF),
                   jax.ShapeDtypeStruct((t // BLK, 2 * NKV * HD, 2 * BLK), F32),
                   jax.ShapeDtypeStruct((HD, 1), F32),
                   jax.ShapeDtypeStruct((NKV, BLK), F32),
                   jax.ShapeDtypeStruct((NBUCKET, NKV, BLK), F32)],
        scratch_shapes=[pltpu.VMEM((HD, BLK), F32), pltpu.VMEM((NKV, 1, GRP * BLK), F32),
                        pltpu.VMEM((NKV, 2 * BLK, GRP * BLK), F32)],
        compiler_params=_params(1), name="attn_bwd")


def _kv_combine_tile(c_ref, cn_ref, has_next, k_ref, kgain, o_ref):
    rows = NKV * HD
    per = c_ref.shape[0]
    dkg = jnp.zeros((HD, BLK), F32)
    for s in range(per):
        cols = slice(s * BLK, (s + 1) * BLK)
        after = c_ref[s + 1, :, :BLK] if s + 1 < per else cn_ref[0, :, :BLK] * has_next
        d = c_ref[s, :, BLK:] + after
        o_ref[rows:, cols] = d[rows:, :].astype(BF)
        for h in range(NKV):
            _, r, kh = _norm_rows(k_ref[h * HD:(h + 1) * HD, cols].astype(F32), kgain)
            dkn = d[h * HD:(h + 1) * HD, :]
            dkh = dkn * kgain
            o_ref[h * HD:(h + 1) * HD, cols] = (r * (dkh - kh * jnp.mean(dkh * kh, axis=0, keepdims=True))).astype(BF)
            dkg = dkg + dkn * kh
    return dkg


def _group_lane_sums(v):
    lane_group = lax.broadcasted_iota(jnp.int32, (1, GRP * BLK), 1) // BLK
    col = lax.broadcasted_iota(jnp.int32, (1, BLK), 1)
    out = jnp.zeros((v.shape[0], BLK), F32)
    for g in range(GRP):
        s = jnp.sum(jnp.where(lane_group == g, v, 0.0), axis=1, keepdims=True)
        out = jnp.where(col == g, s, out)
    return out


def _mix_out(zs, o_t, gp, x, w_cp, w_o, w_out):
    t = x.shape[0]
    tm = min(ROW_TILE_WIDE, t)

    def body(zs_ref, ot_ref, gp_ref, x_ref, wcp_ref, wo_ref, wout_ref, xo_ref, a_ref, b_ref, m_ref):
        a = _dot(zs_ref[...], wcp_ref[...])
        b = _dot_tn(ot_ref[...], wo_ref[...])
        a_ref[...] = a.astype(BF)
        b_ref[...] = b.astype(BF)
        merged = (_sig(gp_ref[:, :D].astype(F32)) * a + _sig(gp_ref[:, D:].astype(F32)) * b).astype(BF)
        m_ref[...] = merged
        xo_ref[...] = x_ref[...] + _dot(merged, wout_ref[...])

    return pl.pallas_call(
        body, grid=(t // tm,),
        in_specs=[_row_tile(tm, D), pl.BlockSpec((D, tm), lambda i: (0, i)), _row_tile(tm, 2 * D), _row_tile(tm, D),
                  _resident((D, D)), _resident((D, D)), _resident((D, D))],
        out_specs=[_row_tile(tm, D)] * 4,
        out_shape=[jax.ShapeDtypeStruct((t, D), F32)] + [jax.ShapeDtypeStruct((t, D), BF)] * 3,
        compiler_params=_params(1), name="mix_out")(zs, o_t, gp, x, w_cp, w_o, w_out)


def _mix_out_bwd(dx, a, b, gp, w_cp, w_o, w_out, deps=()):
    t = dx.shape[0]
    tm = min(ROW_TILE_WIDE, t)

    def body(dx_ref, a_ref, b_ref, gp_ref, wcp_ref, wo_ref, wout_ref, dzs_ref, dot_ref, dgp_ref, da_ref, db_ref, dxb_ref):
        dxb = dx_ref[...].astype(BF)
        dxb_ref[...] = dxb
        dm = _dot_nt(dxb, wout_ref[...])
        gc = _sig(gp_ref[:, :D].astype(F32))
        ga = _sig(gp_ref[:, D:].astype(F32))
        da = (dm * gc).astype(BF)
        db = (dm * ga).astype(BF)
        da_ref[...] = da
        db_ref[...] = db
        dgp_ref[:, :D] = (dm * a_ref[...].astype(F32) * gc * (1.0 - gc)).astype(BF)
        dgp_ref[:, D:] = (dm * b_ref[...].astype(F32) * ga * (1.0 - ga)).astype(BF)
        dzs_ref[...] = _dot_nt(da, wcp_ref[...])
        dot_ref[...] = _dot_nt(wo_ref[...], db).astype(BF)

    return _call(
        body, deps, (dx, a, b, gp, w_cp, w_o, w_out), grid=(t // tm,),
        in_specs=[_row_tile(tm, D), _row_tile(tm, D), _row_tile(tm, D), _row_tile(tm, 2 * D),
                  _resident((D, D)), _resident((D, D)), _resident((D, D))],
        out_specs=[_row_tile(tm, D), pl.BlockSpec((D, tm), lambda i: (0, i)), _row_tile(tm, 2 * D),
                   _row_tile(tm, D), _row_tile(tm, D), _row_tile(tm, D)],
        out_shape=[jax.ShapeDtypeStruct((t, D), F32), jax.ShapeDtypeStruct((D, t), BF), jax.ShapeDtypeStruct((t, 2 * D), BF),
                   jax.ShapeDtypeStruct((t, D), BF), jax.ShapeDtypeStruct((t, D), BF), jax.ShapeDtypeStruct((t, D), BF)],
        compiler_params=_params(1), name="mix_out_bwd")


def _mix_proj_bwd(dxo, duc, dq_t, ckv, qkv_t, kg, dgp, x, g, w_t):
    t = x.shape[0]
    tm = min(ROW_TILE_WIDE, t)
    per = tm // BLK
    steps = t // tm
    kv_rows = 2 * NKV * HD

    def body(dxo_ref, duc_ref, dq_ref, c_ref, cn_ref, k_ref, kg_ref, dgp_ref, x_ref, g_ref, w_ref,
             dx_ref, dg_ref, dkv_ref, dkg_ref, kg_scr):
        i = pl.program_id(0)

        @pl.when(i == 0)
        def _():
            dg_ref[...] = jnp.zeros_like(dg_ref)
            kg_scr[...] = jnp.zeros_like(kg_scr)

        kg_scr[...] += _kv_combine_tile(c_ref, cn_ref, (i < steps - 1).astype(F32), k_ref, kg_ref[...], dkv_ref)
        dn = _dot(duc_ref[...], w_ref[R_CONV[0]:R_CONV[1], :])
        dn = dn + _dot(dgp_ref[...], w_ref[R_GATE[0]:R_GATE[1], :])
        dn = dn + _dot_tn(dq_ref[...], w_ref[R_Q[0]:R_Q[1], :])
        dn = dn + _dot_tn(dkv_ref[...], w_ref[R_KV[0]:R_KV[1], :])
        dx, dg = _rms_bwd(dn, x_ref[...], g_ref[...])
        dx_ref[...] = dxo_ref[...] + dx
        dg_ref[...] += dg

        @pl.when(i == steps - 1)
        def _():
            dkg_ref[...] = jnp.sum(kg_scr[...], axis=1, keepdims=True)

    return pl.pallas_call(
        body, grid=(steps,),
        in_specs=[_row_tile(tm, D), _row_tile(tm, 2 * D), pl.BlockSpec((D, tm), lambda i: (0, i)),
                  pl.BlockSpec((per, kv_rows, 2 * BLK), lambda i: (i, 0, 0)),
                  pl.BlockSpec((1, kv_rows, 2 * BLK), lambda i: (jnp.minimum((i + 1) * per, t // BLK - 1), 0, 0)),
                  pl.BlockSpec((NKV * HD, tm), lambda i: (D // (NKV * HD), i)), _resident((HD, 1)),
                  _row_tile(tm, 2 * D), _row_tile(tm, D), _resident((1, D)), _resident((INW, D))],
        out_specs=[_row_tile(tm, D), pl.BlockSpec((1, D), lambda i: (0, 0)), pl.BlockSpec((kv_rows, tm), lambda i: (0, i)),
                   pl.BlockSpec((HD, 1), lambda i: (0, 0))],
        out_shape=[jax.ShapeDtypeStruct((t, D), F32), jax.ShapeDtypeStruct((1, D), F32),
                   jax.ShapeDtypeStruct((kv_rows, t), BF), jax.ShapeDtypeStruct((HD, 1), F32)],
        scratch_shapes=[pltpu.VMEM((HD, BLK), F32)],
        compiler_params=_params(1), name="mix_proj_bwd")(dxo, duc, dq_t, ckv, ckv, qkv_t, kg, dgp, x, g, w_t)


def _attention_tables():
    kj = np.arange(2 * BLK)[:, None]
    qi = np.arange(BLK)[None, :]
    dist = qi + BLK - kj
    in_win = (dist >= 0) & (dist < BLK)
    dpos = np.maximum(dist, 0)
    max_exact = NBUCKET // 2
    dfl = np.maximum(dpos, 1).astype(np.float32)
    large = max_exact + (np.log(dfl / np.float32(max_exact)) / np.float32(math.log(BLK / max_exact))
                         * np.float32(NBUCKET - max_exact)).astype(np.int32)
    large = np.minimum(large, NBUCKET - 1)
    bucket = np.where(dpos < max_exact, dpos, large)
    onehot = (bucket[None] == np.arange(NBUCKET)[:, None, None]).astype(np.float32)
    mask = in_win.astype(np.float32)
    mask_first = mask * (kj >= BLK)
    masks = np.stack([np.tile(mask, (1, GRP)), np.tile(mask_first, (1, GRP))])
    return onehot, masks


def _bias_table(rel_bias, onehot):
    tab = jnp.einsum("bkq,bh->hkq", onehot, rel_bias, precision=lax.Precision.HIGHEST)
    tab = tab.reshape(NKV, GRP, 2 * BLK, BLK)
    return jnp.transpose(tab, (0, 2, 1, 3)).reshape(NKV, 2 * BLK, GRP * BLK)


def _local_step(x, target, vec, ffn1_up, weights_of, wgrad, wgrads, grads_done, small_done):
    onehot_np, masks_np = _attention_tables()
    onehot = jnp.asarray(onehot_np)
    masks = jnp.asarray(masks_np)
    bias_t = jnp.where(masks[:, None] > 0.5, _bias_table(vec["rel_bias"], onehot)[None], NEG)
    sink_rows = jnp.repeat(vec["attn_sinks"].reshape(NKV, 1, GRP), BLK, axis=2)
    qg = vec["q_norm"].reshape(HD, 1)
    kg = vec["k_norm"].reshape(HD, 1)
    g1 = vec["ffn1_norm"].reshape(1, D)
    gm = vec["mix_norm"].reshape(1, D)
    g2 = vec["ffn2_norm"].reshape(1, D)
    dwb = vec["conv_dw_bias"].reshape(1, D)
    lng = vec["conv_ln_g"].reshape(1, D)
    lnb = vec["conv_ln_b"].reshape(1, D)

    n1, u1, w1 = ffn1_up(x, g1, (bias_t, sink_rows))
    x1, = _ffn_down(x, u1, w1["ffn1_w_out"], "ffn1_down_first", part=(0, 2))
    (x1,), wm = weights_of("mix_proj", (x1,), during=functools.partial(
        _ffn_down, x, u1, w1["ffn1_w_out"], "ffn1_down", part=(1, 2), out=x1))
    dwk = jnp.pad(wm["conv_dw_kernel"], ((0, CWP - CW), (0, 0)))
    hm, uc, gp, qkv_t = _mix_proj(x1, gm, wm["w_in"])
    (zs, zc), merge = weights_of("mix_merge", (uc,), during=functools.partial(_conv_fwd, uc, dwk, dwb, lng, lnb))
    wm.update(merge)
    o_t, probs, sink_probs = _attn_fwd(qkv_t, qg, kg, sink_rows, bias_t)
    x2, a, b, merged = _mix_out(zs, o_t, gp, x1, wm["conv_w_proj"], wm["attn_w_o"], wm["w_out"])
    w2 = weights_of("ffn2", (x2,))
    gv = {}
    n2, du2, h2, dy2, dx2, sq, gv["ffn2_norm"] = _ffn_last(x2, target, g2, w2["ffn2_w_in"], w2["ffn2_w_out"], "ffn2")

    deps = grads_done("ffn2", {"ffn2_w_in": wgrad(du2, n2, "ffn2_dw_in", False),
                               "ffn2_w_out": wgrad(h2, dy2, "ffn2_dw_out", False)})

    dzs, do_t, dgp, da, db, dx2b = _mix_out_bwd(dx2, a, b, gp, wm["conv_w_proj"], wm["attn_w_o"], wm["w_out"], deps=deps)
    grads = wgrads([(merged, dx2b, False), (zs, da, False), (o_t, db, True)], "mix_dw_merge")
    deps = grads_done("mix_out", dict(zip(("w_out", "conv_w_proj", "attn_w_o"), grads)))

    dq_t, ckv, dqg, dsink, dbias = _attn_bwd(qkv_t, do_t, probs, sink_probs, qg, kg, onehot, deps=deps)
    gv["q_norm"] = dqg.reshape(HD)
    gv["attn_sinks"] = dsink[:, :GRP].reshape(NQ)
    gv["rel_bias"] = dbias[:, :, :GRP].reshape(NBUCKET, NQ)

    duc, dk_conv, gv["conv_dw_bias"], gv["conv_ln_g"], gv["conv_ln_b"] = _conv_bwd(uc, zc, dzs, dwk, lng, lnb)
    gv["conv_dw_kernel"] = dk_conv[:CW]

    dx1, gv["mix_norm"], dkv_t, dkg = _mix_proj_bwd(dx2, duc, dq_t, ckv, qkv_t, kg, dgp, x1, gm, wm["w_in"])
    gv["k_norm"] = dkg.reshape(HD)
    deps = grads_done("mix_in", {"w_in": _wgrad_mix(duc, dq_t, dkv_t, dgp, hm)})

    dx0, du1, h1, dy1, gv["ffn1_norm"] = _ffn_bwd(dx1, x, g1, u1, w1["ffn1_w_in"], w1["ffn1_w_out"], "ffn1_bwd", deps=deps)
    for k in ("ffn1_norm", "mix_norm", "ffn2_norm", "conv_dw_bias", "conv_ln_g", "conv_ln_b"):
        gv[k] = gv[k].reshape(D)
    deps = small_done(gv, sq)
    deps = grads_done("ffn1_in", {"ffn1_w_in": wgrad(du1, n1, "ffn1_dw_in", False, deps)})
    grads_done("ffn1_out", {"ffn1_w_out": wgrad(h1, dy1, "ffn1_dw_out", False, deps)})
    return dx0


MESH_ID = pl.DeviceIdType.MESH


def _position():
    return lax.axis_index("x"), lax.axis_index("y"), lax.axis_index("c")


def _shard_rows(ref, index, rows):
    return ref.at[pl.ds(pl.multiple_of(index * rows, 16), rows), :]


def _prep(weights, taps, me, name, deps=()):
    n = len(weights)
    n_deps = len(deps)
    with_taps = taps is not None

    def body(me_ref, *refs):
        refs = refs[n_deps:]
        ins, outs = refs[:len(refs) // 2], refs[len(refs) // 2:]
        for k in range(n):
            outs[k][...] = ins[k][...].astype(BF)
        if with_taps:
            outs[n][0:CW, :] = ins[n][...]
            outs[n][CW:, :] = jnp.zeros((CWP - CW, BLK), F32)

    shard_shapes = [w.shape for w in weights] + [(CWP, BLK)] * with_taps
    dtypes = [BF] * n + [F32] * with_taps
    ins = list(weights) + [taps] * with_taps
    return pl.pallas_call(
        body,
        grid_spec=pltpu.PrefetchScalarGridSpec(
            num_scalar_prefetch=1, grid=(1,),
            in_specs=[ANY] * n_deps + [pl.BlockSpec(a.shape, lambda i, m: (0, 0), pipeline_mode=pl.Buffered(1)) for a in ins],
            out_specs=[pl.BlockSpec(s, lambda i, m: (m[0], 0)) for s in shard_shapes]),
        out_shape=[jax.ShapeDtypeStruct((N_DEV * s[0], s[1]), d) for s, d in zip(shard_shapes, dtypes)],
        compiler_params=_params(1), name=name)(me, *deps, *ins)


HBM = pl.BlockSpec(memory_space=pltpu.HBM)
SEM = pl.BlockSpec(memory_space=pltpu.SEMAPHORE)
DATAFLOW = pltpu.SideEffectType.DATAFLOW_SIDE_EFFECTING
TOKEN = jax.ShapeDtypeStruct((8, 128), F32)


def _in_hbm(x):
    return pltpu.with_memory_space_constraint(x, pltpu.HBM)


def _hbm_like(arrays):
    return [pltpu.HBM(a.shape, a.dtype) for a in arrays]


def _other_chips(x, y):
    return [(1 - x, y), (x, 1 - y), (1 - x, 1 - y)]


def _device_index(chip, c):
    return 4 * chip[0] + 2 * chip[1] + c


def _chip_index(chip):
    return 2 * chip[0] + chip[1]


class _Exchange:
    def __init__(self, gather, all_cores=False):
        self.gather = gather
        self.all_cores = all_cores
        self.n_peers = N_DEV - 1 if all_cores else 3

    def peers(self, x, y, c):
        if self.all_cores:
            return [(x ^ (k >> 2), y ^ ((k >> 1) & 1), c ^ (k & 1)) for k in range(1, N_DEV)]
        return [(*chip, c) for chip in _other_chips(x, y)]

    def sent(self, x, y, c, peer):
        return _device_index((x, y), c) if self.gather else _chip_index(peer[:2])

    def lands_at(self, x, y, c):
        return _device_index((x, y), c) if self.gather else _chip_index((x, y))

    def arrives_at(self, peer):
        return _device_index(peer[:2], peer[2]) if self.gather else _chip_index(peer[:2])


def _ici_copies_start(sets, sources, landings, exchanges, name, deps=()):
    n = len(landings)
    arrays = (list(sources) if sources is not None else []) + list(landings)
    first_land = len(arrays) - n
    n_sets = len(sets)
    n_deps = len(deps)

    def body(*refs):
        refs = refs[n_deps:]
        src, land = refs[:n], refs[first_land:first_land + n]
        sems = refs[len(arrays):len(arrays) + 2 * n_sets]
        token = refs[-1]
        x, y, c = _position()
        for s, (members, exchange) in enumerate(zip(sets, exchanges)):
            for slot, (k, rows) in enumerate(members):
                for j, peer in enumerate(exchange.peers(x, y, c)):
                    at = exchange.n_peers * slot + j
                    pltpu.make_async_remote_copy(
                        src_ref=_shard_rows(src[k], exchange.sent(x, y, c, peer), rows),
                        dst_ref=_shard_rows(land[k], exchange.lands_at(x, y, c), rows),
                        send_sem=sems[2 * s].at[at], recv_sem=sems[2 * s + 1].at[at],
                        device_id=peer, device_id_type=MESH_ID).start()
        token[...] = jnp.zeros_like(token)

    sem_shapes = []
    for members, exchange in zip(sets, exchanges):
        sem_shapes += [pltpu.SemaphoreType.DMA((exchange.n_peers * len(members),))] * 2
    out = pl.pallas_call(
        body, name=name,
        out_shape=sem_shapes + _hbm_like(arrays) + [TOKEN],
        in_specs=[ANY] * n_deps + [HBM] * len(arrays),
        out_specs=[SEM] * (2 * n_sets) + [HBM] * len(arrays) + [pl.BlockSpec(memory_space=pltpu.VMEM)],
        input_output_aliases={n_deps + i: 2 * n_sets + i for i in range(len(arrays))},
        compiler_params=pltpu.CompilerParams(has_side_effects=DATAFLOW),
    )(*deps, *[_in_hbm(a) for a in arrays])
    sems = [(out[2 * s], out[2 * s + 1]) for s in range(n_sets)]
    thru = list(out[2 * n_sets:2 * n_sets + len(arrays)])
    return sems, (thru[:first_land] if sources is not None else None), thru[first_land:], out[-1]


def _ici_copies_wait(sems, members, sources, landings, exchange, after, name):
    n = len(landings)
    arrays = (list(sources) if sources is not None else []) + list(landings)
    first_land = len(arrays) - n

    def body(*refs):
        src, land = refs[:n], refs[first_land:first_land + n]
        send_sems, recv_sems = refs[len(arrays)], refs[len(arrays) + 1]
        x, y, c = _position()
        for slot, rows in enumerate(members):
            for j, peer in enumerate(exchange.peers(x, y, c)):
                at = exchange.n_peers * slot + j
                cp = pltpu.make_async_remote_copy(
                    src_ref=_shard_rows(src[slot], exchange.sent(x, y, c, peer), rows),
                    dst_ref=_shard_rows(land[slot], exchange.arrives_at(peer), rows),
                    send_sem=send_sems.at[at], recv_sem=recv_sems.at[at], device_id=peer, device_id_type=MESH_ID)
                cp.wait_send()
                cp.wait_recv()

    out = pl.pallas_call(
        body, name=name, out_shape=_hbm_like(arrays),
        in_specs=[HBM] * len(arrays) + [SEM, SEM] + [ANY] * len(after), out_specs=[HBM] * len(arrays),
        input_output_aliases={i: i for i in range(len(arrays))},
        compiler_params=pltpu.CompilerParams(has_side_effects=DATAFLOW),
    )(*arrays, sems[0], sems[1], *after)
    return list(out[first_land:])


def _swap_copies(land, rows, which, send_sems, recv_sems):
    x, y, c = _position()
    chips = [([(x, y)] + _other_chips(x, y))[j] for j in which]
    sends, recvs = [], []
    for k in range(len(land)):
        for j, chip in enumerate(chips):
            for copies, core in ((sends, c), (recvs, 1 - c)):
                block = _shard_rows(land[k], _device_index(chip, core), rows[k])
                copies.append(pltpu.make_async_remote_copy(
                    src_ref=block, dst_ref=block, send_sem=send_sems.at[k, j], recv_sem=recv_sems.at[k, j],
                    device_id=(x, y, 1 - c), device_id_type=MESH_ID))
    return sends, recvs


def _d2d_gather(buffers, rows, name, which=(0, 1, 2, 3), deps=()):
    n = len(buffers)
    n_deps = len(deps)

    def body(*refs):
        sends, recvs = _swap_copies(refs[n_deps + n:n_deps + 2 * n], rows, which, *refs[n_deps + 2 * n:])
        for cp in sends:
            cp.start()
        for cp in recvs:
            cp.wait_recv()
        for cp in sends:
            cp.wait_send()

    return pl.pallas_call(
        body, name=name, out_shape=[jax.ShapeDtypeStruct(a.shape, a.dtype) for a in buffers],
        in_specs=[ANY] * (n_deps + n), out_specs=[ANY] * n, input_output_aliases={n_deps + i: i for i in range(n)},
        scratch_shapes=[pltpu.SemaphoreType.DMA((n, len(which))), pltpu.SemaphoreType.DMA((n, len(which)))],
    )(*deps, *buffers)


def _call_with_swap(body, args, swap, prefetch=(), **kw):
    buffers, rows = swap
    n, n_pre, n_in, n_out = len(buffers), len(prefetch), len(args), len(kw["out_shape"])
    n_scratch = len(kw["scratch_shapes"])
    grid = kw["grid"]
    which = (0, 1, 2, 3)

    def at_step(last):
        hit = [pl.program_id(a) == (extent - 1 if last else 0) for a, extent in enumerate(grid)]
        return functools.reduce(jnp.logical_and, hit)

    def hosted(*refs):
        pre, ins, refs = refs[:n_pre], refs[n_pre:n_pre + n_in], refs[n_pre + n_in + n:]
        outs, land, scratch = refs[:n_out], refs[n_out:n_out + n], refs[n_out + n:n_out + n + n_scratch]
        sends, recvs = _swap_copies(land, rows, which, *refs[n_out + n + n_scratch:])

        @pl.when(at_step(False))
        def _():
            for cp in sends:
                cp.start()

        body(*pre, *ins, *outs, *scratch)

        @pl.when(at_step(True))
        def _():
            for cp in recvs:
                cp.wait_recv()
            for cp in sends:
                cp.wait_send()

    sem_shape = pltpu.SemaphoreType.DMA((n, len(which)))
    aliases = {**kw.get("input_output_aliases", {}), **{n_in + i: n_out + i for i in range(n)}}
    out = pl.pallas_call(
        hosted,
        grid_spec=pltpu.PrefetchScalarGridSpec(
            num_scalar_prefetch=n_pre, grid=grid, in_specs=kw["in_specs"] + [ANY] * n, out_specs=kw["out_specs"] + [ANY] * n,
            scratch_shapes=kw["scratch_shapes"] + [sem_shape, sem_shape]),
        out_shape=kw["out_shape"] + [jax.ShapeDtypeStruct(a.shape, a.dtype) for a in buffers],
        input_output_aliases={n_pre + i: o for i, o in aliases.items()},
        compiler_params=kw["compiler_params"], name=kw["name"])(*prefetch, *args, *buffers)
    return out[:n_out], out[n_out:]


def _rs_pair(grads, name):
    n = len(grads)
    rows = [g.shape[0] // N_DEV for g in grads]

    def body(*refs):
        ins, outs = refs[:n], refs[n:2 * n]
        send_sems, recv_sems = refs[2 * n:]
        x, y, c = _position()
        copies = []
        for k in range(n):
            for q in range(4):
                copies.append(pltpu.make_async_remote_copy(
                    src_ref=_shard_rows(ins[k], 2 * q + 1 - c, rows[k]), dst_ref=_shard_rows(outs[k], q, rows[k]),
                    send_sem=send_sems.at[k, q], recv_sem=recv_sems.at[k, q], device_id=(x, y, 1 - c),
                    device_id_type=MESH_ID))
        for cp in copies:
            cp.start()
        for cp in copies:
            cp.wait()

    return pl.pallas_call(
        body, out_shape=[jax.ShapeDtypeStruct((4 * r, g.shape[1]), g.dtype) for g, r in zip(grads, rows)],
        in_specs=[ANY] * n, out_specs=[ANY] * n,
        scratch_shapes=[pltpu.SemaphoreType.DMA((n, 4)), pltpu.SemaphoreType.DMA((n, 4))],
        name=name)(*grads)


def _wgrad_pair(lhs, rhs, name, *, lhs_is_transposed, deps=()):
    t = rhs.shape[0]
    n = lhs.shape[0] if lhs_is_transposed else lhs.shape[1]
    r = n // N_DEV
    n_chips = N_DEV // 2
    per = 1 if (2 * r) % BLK == 0 else 2
    steps = n_chips // per

    def body(l_ref, r_ref, kept_ref, recv_ref, res, send_sems, recv_sems):
        q = pl.program_id(0)
        slot = q % 2
        x, y, c = _position()

        def send(step, buf, i):
            return pltpu.make_async_remote_copy(
                src_ref=res.at[buf, pl.ds(pl.multiple_of((2 * i + 1 - c) * r, 16), r), :],
                dst_ref=_shard_rows(recv_ref, step * per + i, r),
                send_sem=send_sems.at[buf, i], recv_sem=recv_sems.at[step * per + i],
                device_id=(x, y, 1 - c), device_id_type=MESH_ID)

        @pl.when(q >= 2)
        def _():
            for i in range(per):
                send(q - 2, slot, i).wait_send()

        if lhs_is_transposed:
            res[slot] = _dot(l_ref[...], r_ref[...]).astype(BF)
        else:
            res[slot] = _dot_tn(l_ref[...], r_ref[...]).astype(BF)
        for i in range(per):
            kept_ref[i * r:(i + 1) * r, :] = res[slot, pl.ds(pl.multiple_of((2 * i + c) * r, 16), r), :]
            send(q, slot, i).start()

        @pl.when(q == steps - 1)
        def _():
            for i in range(per):
                if steps > 1:
                    send(q - 1, 1 - slot, i).wait_send()
                send(q, slot, i).wait_send()
            for chip in range(n_chips):
                send(chip // per, 0, chip % per).wait_recv()

    width = 2 * r * per
    lhs_spec = pl.BlockSpec((width, t), lambda q: (q, 0)) if lhs_is_transposed else pl.BlockSpec((t, width), lambda q: (0, q))
    return _call(
        body, deps, (lhs, rhs), grid=(steps,),
        in_specs=[lhs_spec, _resident((t, D))],
        out_specs=[pl.BlockSpec((per * r, D), lambda q: (q, 0)), ANY],
        out_shape=[jax.ShapeDtypeStruct((n // 2, D), BF)] * 2,
        scratch_shapes=[pltpu.VMEM((2, width, D), BF), pltpu.SemaphoreType.DMA((2, per)),
                        pltpu.SemaphoreType.DMA((n_chips,))],
        compiler_params=_params(1), name=name)


def _wgrad_pair_sum(lhs, rhs, place, name, *, lhs_is_transposed, deps=()):
    t = rhs.shape[0]
    n = lhs.shape[0] if lhs_is_transposed else lhs.shape[1]
    r = n // N_DEV
    n_chips = N_DEV // 2
    per = 1 if (2 * r) % BLK == 0 else 2
    steps = n_chips // per
    n_deps = len(deps)

    def body(place_ref, *refs):
        l_ref, r_ref, part_ref, land_ref, res, inbox, send_sems, recv_sems = refs[n_deps:]
        q = pl.program_id(0)
        slot = q % 2
        x, y, c = _position()

        def send(step, buf, i):
            return pltpu.make_async_remote_copy(
                src_ref=res.at[buf, pl.ds(pl.multiple_of((2 * i + 1 - c) * r, 16), r), :], dst_ref=inbox.at[step * per + i],
                send_sem=send_sems.at[buf, i], recv_sem=recv_sems.at[step * per + i],
                device_id=(x, y, 1 - c), device_id_type=MESH_ID)

        @pl.when(q < steps)
        def _():
            @pl.when(q >= 2)
            def _():
                for i in range(per):
                    send(q - 2, slot, i).wait_send()

            if lhs_is_transposed:
                res[slot] = _dot(l_ref[...], r_ref[...]).astype(BF)
            else:
                res[slot] = _dot_tn(l_ref[...], r_ref[...]).astype(BF)
            for i in range(per):
                send(q, slot, i).start()

        @pl.when(q >= 1)
        def _():
            for i in range(per):
                chip = (q - 1) * per + i
                send(q - 1, 1 - slot, i).wait_recv()
                kept = res[1 - slot, pl.ds(pl.multiple_of((2 * i + c) * r, 16), r), :]
                total = (kept.astype(F32) + inbox[chip].astype(F32)).astype(BF)
                part_ref[i * r:(i + 1) * r, :] = total

                @pl.when(chip == place_ref[1])
                def _():
                    land_ref[...] = total

        @pl.when(q == steps)
        def _():
            for i in range(per):
                if steps > 1:
                    send(q - 2, slot, i).wait_send()
                send(q - 1, 1 - slot, i).wait_send()

    width = 2 * r * per
    last = steps - 1
    if lhs_is_transposed:
        lhs_spec = pl.BlockSpec((width, t), lambda q, p: (jnp.minimum(q, last), 0))
    else:
        lhs_spec = pl.BlockSpec((t, width), lambda q, p: (0, jnp.minimum(q, last)))
    return pl.pallas_call(
        body,
        grid_spec=pltpu.PrefetchScalarGridSpec(
            num_scalar_prefetch=1, grid=(steps + 1,),
            in_specs=[ANY] * n_deps + [lhs_spec, pl.BlockSpec((t, D), lambda q, p: (0, 0), pipeline_mode=pl.Buffered(1))],
            out_specs=[pl.BlockSpec((per * r, D), lambda q, p: (jnp.maximum(q - 1, 0), 0)),
                       pl.BlockSpec((r, D), lambda q, p: (p[1], 0))],
            scratch_shapes=[pltpu.VMEM((2, width, D), BF), pltpu.VMEM((n_chips, r, D), BF),
                            pltpu.SemaphoreType.DMA((2, per)), pltpu.SemaphoreType.DMA((n_chips,))]),
        out_shape=[jax.ShapeDtypeStruct((n // 2, D), BF)] * 2,
        compiler_params=_params(1), name=name)(place, *deps, lhs, rhs)


def _wgrad_pair_sum_many(items, place, name, deps=()):
    m = len(items)
    t = items[0][1].shape[0]
    n = items[0][0].shape[0] if items[0][2] else items[0][0].shape[1]
    r = n // N_DEV
    assert (2 * r) % BLK == 0 and r <= WGRAD_SUM_MAX_ROWS
    steps = N_DEV // 2
    chunks = m * steps
    n_deps = len(deps)

    def body(place_ref, *refs):
        refs = refs[n_deps:]
        l_refs, r_first, r_later = refs[:m], refs[m], refs[m + 1:2 * m]
        parts, lands = refs[2 * m:3 * m], refs[3 * m:4 * m]
        res, inbox, r_scr, send_sems, recv_sems, fetch_sems = refs[4 * m:]
        g = pl.program_id(0)
        slot = g % 2
        x, y, c = _position()

        def send(chunk, buf):
            return pltpu.make_async_remote_copy(
                src_ref=res.at[buf, pl.ds(pl.multiple_of((1 - c) * r, 16), r), :], dst_ref=inbox.at[chunk],
                send_sem=send_sems.at[buf], recv_sem=recv_sems.at[chunk], device_id=(x, y, 1 - c), device_id_type=MESH_ID)

        def fetch(k):
            return pltpu.make_async_copy(r_later[k - 1], r_scr.at[k - 1], fetch_sems.at[k - 1])

        @pl.when(g == 0)
        def _():
            for k in range(1, m):
                fetch(k).start()

        @pl.when(g < chunks)
        def _():
            @pl.when(g >= 2)
            def _():
                send(g - 2, slot).wait_send()

            for k, (_, _, transposed) in enumerate(items):
                @pl.when(g // steps == k)
                def _(k=k, transposed=transposed):
                    if k > 0:
                        @pl.when(g == k * steps)
                        def _():
                            fetch(k).wait()
                    rhs = r_first[...] if k == 0 else r_scr[k - 1]
                    res[slot] = (_dot(l_refs[k][...], rhs) if transposed else _dot_tn(l_refs[k][...], rhs)).astype(BF)

            send(g, slot).start()

        @pl.when(g >= 1)
        def _():
            chunk = g - 1
            send(chunk, 1 - slot).wait_recv()
            kept = res[1 - slot, pl.ds(pl.multiple_of(c * r, 16), r), :]
            total = (kept.astype(F32) + inbox[chunk].astype(F32)).astype(BF)
            for k in range(m):
                @pl.when(chunk // steps == k)
                def _(k=k):
                    parts[k][...] = total

                    @pl.when(chunk % steps == place_ref[1])
                    def _():
                        lands[k][...] = total

        @pl.when(g == chunks)
        def _():
            send(g - 2, slot).wait_send()
            send(g - 1, 1 - slot).wait_send()

    def own_steps(k):
        return lambda g: jnp.clip(g - k * steps, 0, steps - 1)

    lhs_specs = []
    for k, (lhs, _, transposed) in enumerate(items):
        at = own_steps(k)
        lhs_specs.append(pl.BlockSpec((2 * r, t), lambda g, p, at=at: (at(g), 0)) if transposed
                         else pl.BlockSpec((t, 2 * r), lambda g, p, at=at: (0, at(g))))
    out = pl.pallas_call(
        body,
        grid_spec=pltpu.PrefetchScalarGridSpec(
            num_scalar_prefetch=1, grid=(chunks + 1,),
            in_specs=[ANY] * n_deps + lhs_specs
            + [pl.BlockSpec((t, D), lambda g, p: (0, 0), pipeline_mode=pl.Buffered(1))] + [ANY] * (m - 1),
            out_specs=[pl.BlockSpec((r, D), lambda g, p, at=own_steps(k): (at(g - 1), 0)) for k in range(m)]
            + [pl.BlockSpec((r, D), lambda g, p: (p[1], 0))] * m,
            scratch_shapes=[pltpu.VMEM((2, 2 * r, D), BF), pltpu.VMEM((chunks, r, D), BF), pltpu.VMEM((m - 1, t, D), BF),
                            pltpu.SemaphoreType.DMA((2,)), pltpu.SemaphoreType.DMA((chunks,)),
                            pltpu.SemaphoreType.DMA((m - 1,))]),
        out_shape=[jax.ShapeDtypeStruct((n // 2, D), BF)] * (2 * m),
        compiler_params=_params(1), name=name)(place, *deps, *[i[0] for i in items], *[i[1] for i in items])
    return [(out[k], out[m + k]) for k in range(m)]


def _pair_add(grad, received, place, name, kept_only=False):
    r = received.shape[0] // 4
    parity = 0 if kept_only else 1

    def body(place_ref, g_ref, r_ref, o_ref, land_ref):
        total = (g_ref[...].astype(F32) + r_ref[...].astype(F32)).astype(BF)
        o_ref[...] = total

        @pl.when(pl.program_id(0) == place_ref[1])
        def _():
            land_ref[...] = total

    return pl.pallas_call(
        body,
        grid_spec=pltpu.PrefetchScalarGridSpec(
            num_scalar_prefetch=1, grid=(4,),
            in_specs=[pl.BlockSpec((r, D), lambda q, p: ((1 + parity) * q + parity * p[0], 0)),
                      pl.BlockSpec((r, D), lambda q, p: (q, 0))],
            out_specs=[pl.BlockSpec((r, D), lambda q, p: (q, 0)), pl.BlockSpec((r, D), lambda q, p: (p[1], 0))]),
        out_shape=[jax.ShapeDtypeStruct(received.shape, BF)] * 2,
        compiler_params=_params(1), name=name)(place, grad, received)


def _sum_blocks(gathered, rows):
    def body(b_ref, o_ref):
        acc = b_ref[0:rows, :]
        for d in range(1, N_DEV):
            acc = acc + b_ref[d * rows:(d + 1) * rows, :]
        o_ref[...] = acc

    return pl.pallas_call(body, out_shape=jax.ShapeDtypeStruct((rows, D), F32), name="small_sum")(gathered)


def _adamw_math(w, g, m, v):
    m = ADAM_B1 * m + (1.0 - ADAM_B1) * g
    v = ADAM_B2 * v + (1.0 - ADAM_B2) * (g * g)
    m_hat = m / (1.0 - ADAM_B1 ** ADAM_STEP)
    v_hat = v / (1.0 - ADAM_B2 ** ADAM_STEP)
    delta = -ADAM_LR * (m_hat / (jnp.sqrt(v_hat) + ADAM_EPS) + ADAM_WD * w)
    return delta, m, v


def _sum_partials(blocks):
    g = blocks[0].astype(F32)
    for blk in blocks[1:]:
        g = g + blk.astype(F32)
    return g


def _reduce_adamw(landed, w, m, v, name):
    r = w.shape[0]
    tr = 352 if r % 352 == 0 else r
    per = r // tr

    def body(r0, r1, r2, r3, w_ref, m_ref, v_ref, g_ref, d_ref, nm_ref, nv_ref):
        g = _sum_partials([r0[...], r1[...], r2[...], r3[...]])
        g_ref[...] = g
        d_ref[...], nm_ref[...], nv_ref[...] = _adamw_math(w_ref[...], g, m_ref[...], v_ref[...])

    tile = _row_tile(tr, D)
    return pl.pallas_call(
        body, grid=(per,),
        in_specs=[pl.BlockSpec((tr, D), lambda i, q=q: (q * per + i, 0)) for q in range(4)] + [tile] * 3,
        out_specs=[tile] * 4, out_shape=[jax.ShapeDtypeStruct(w.shape, F32)] * 4,
        compiler_params=_params(1), name=name)(landed, landed, landed, landed, w, m, v)


def _adamw_small(w, g, m, v, name):
    def body(w_ref, g_ref, m_ref, v_ref, d_ref, nm_ref, nv_ref):
        d_ref[...], nm_ref[...], nv_ref[...] = _adamw_math(w_ref[...], g_ref[...], m_ref[...], v_ref[...])

    return pl.pallas_call(body, out_shape=[jax.ShapeDtypeStruct(w.shape, F32)] * 3, name=name)(w, g, m, v)


WEIGHTS = ("ffn1_norm", "ffn1_w_in", "ffn1_w_out", "mix_norm", "w_in", "conv_dw_kernel", "conv_dw_bias", "conv_ln_g",
           "conv_ln_b", "conv_w_proj", "q_norm", "k_norm", "attn_sinks", "rel_bias", "attn_w_o", "w_out", "ffn2_norm",
           "ffn2_w_in", "ffn2_w_out")
MATRICES = ("ffn1_w_in", "ffn1_w_out", "w_in", "conv_w_proj", "attn_w_o", "w_out", "ffn2_w_in", "ffn2_w_out")
COLUMN_SHARDED = ("ffn1_w_in", "w_in", "ffn2_w_in")
ROW_VECTORS = ("ffn1_norm", "mix_norm", "conv_dw_bias", "conv_ln_g", "conv_ln_b", "ffn2_norm")
PACKED = (("q_norm", HD), ("k_norm", HD), ("attn_sinks", NQ), ("rel_bias", NBUCKET * NQ))
GATHER = _Exchange(gather=True)
GATHER_ALL = _Exchange(gather=True, all_cores=True)
SCATTER = _Exchange(gather=False)
FIRST = "ffn1_w_in"
GATHER_STAGES = ("ffn1_out", "mix_proj", "mix_merge", "ffn2")
STAGE_GATHER = {"ffn1_out": GATHER, "mix_proj": GATHER, "mix_merge": GATHER, "ffn2": GATHER_ALL}
STAGE_MEMBERS = {"ffn1_out": ("ffn1_w_out",),
                 "mix_proj": ("w_in", "taps"), "mix_merge": ("conv_w_proj", "attn_w_o", "w_out"),
                 "ffn2": ("ffn2_w_in", "ffn2_w_out")}
ROW_PACKED = len(ROW_VECTORS)
ROW_LOSS = ROW_PACKED + 1
ROW_TAPS = 8
PAYLOAD_ROWS = 48


def _pack_small(values, last_row):
    packed = jnp.concatenate([values[k].reshape(-1) for k, _ in PACKED])
    packed = jnp.pad(packed, (0, D - packed.shape[0])).reshape(1, D)
    return jnp.concatenate([values[k].reshape(1, D) for k in ROW_VECTORS] + [packed, last_row], axis=0)


def _unpack_small(rows):
    out = {k: rows[i] for i, k in enumerate(ROW_VECTORS)}
    at = 0
    for k, size in PACKED:
        out[k] = rows[ROW_PACKED, at:at + size]
        at += size
    out["rel_bias"] = out["rel_bias"].reshape(NBUCKET, NQ)
    return out


def kernel(x, ffn1_norm, ffn1_w_in, ffn1_w_out, mix_norm, w_in, conv_dw_kernel, conv_dw_bias, conv_ln_g, conv_ln_b, conv_w_proj, q_norm, k_norm, attn_sinks, rel_bias, attn_w_o, w_out, ffn2_norm, ffn2_w_in, ffn2_w_out, loss_target, m_ffn1_norm, m_ffn1_w_in, m_ffn1_w_out, m_mix_norm, m_w_in, m_conv_dw_kernel, m_conv_dw_bias, m_conv_ln_g, m_conv_ln_b, m_conv_w_proj, m_q_norm, m_k_norm, m_attn_sinks, m_rel_bias, m_attn_w_o, m_w_out, m_ffn2_norm, m_ffn2_w_in, m_ffn2_w_out, v_ffn1_norm, v_ffn1_w_in, v_ffn1_w_out, v_mix_norm, v_w_in, v_conv_dw_kernel, v_conv_dw_bias, v_conv_ln_g, v_conv_ln_b, v_conv_w_proj, v_q_norm, v_k_norm, v_attn_sinks, v_rel_bias, v_attn_w_o, v_w_out, v_ffn2_norm, v_ffn2_w_in, v_ffn2_w_out):
    w = dict(ffn1_norm=ffn1_norm, ffn1_w_in=ffn1_w_in, ffn1_w_out=ffn1_w_out, mix_norm=mix_norm, w_in=w_in,
             conv_dw_kernel=conv_dw_kernel, conv_dw_bias=conv_dw_bias, conv_ln_g=conv_ln_g, conv_ln_b=conv_ln_b,
             conv_w_proj=conv_w_proj, q_norm=q_norm, k_norm=k_norm, attn_sinks=attn_sinks, rel_bias=rel_bias,
             attn_w_o=attn_w_o, w_out=w_out, ffn2_norm=ffn2_norm, ffn2_w_in=ffn2_w_in, ffn2_w_out=ffn2_w_out)
    m = dict(ffn1_norm=m_ffn1_norm, ffn1_w_in=m_ffn1_w_in, ffn1_w_out=m_ffn1_w_out, mix_norm=m_mix_norm, w_in=m_w_in,
             conv_dw_kernel=m_conv_dw_kernel, conv_dw_bias=m_conv_dw_bias, conv_ln_g=m_conv_ln_g, conv_ln_b=m_conv_ln_b,
             conv_w_proj=m_conv_w_proj, q_norm=m_q_norm, k_norm=m_k_norm, attn_sinks=m_attn_sinks, rel_bias=m_rel_bias,
             attn_w_o=m_attn_w_o, w_out=m_w_out, ffn2_norm=m_ffn2_norm, ffn2_w_in=m_ffn2_w_in, ffn2_w_out=m_ffn2_w_out)
    v = dict(ffn1_norm=v_ffn1_norm, ffn1_w_in=v_ffn1_w_in, ffn1_w_out=v_ffn1_w_out, mix_norm=v_mix_norm, w_in=v_w_in,
             conv_dw_kernel=v_conv_dw_kernel, conv_dw_bias=v_conv_dw_bias, conv_ln_g=v_conv_ln_g, conv_ln_b=v_conv_ln_b,
             conv_w_proj=v_conv_w_proj, q_norm=v_q_norm, k_norm=v_k_norm, attn_sinks=v_attn_sinks, rel_bias=v_rel_bias,
             attn_w_o=v_attn_w_o, w_out=v_w_out, ffn2_norm=v_ffn2_norm, ffn2_w_in=v_ffn2_w_in, ffn2_w_out=v_ffn2_w_out)
    px, py, pc = _position()
    me = 4 * px + 2 * py + pc
    place = jnp.stack([pc, 2 * px + py]).astype(jnp.int32)

    rows_of = lambda k, a: a.T if k in COLUMN_SHARDED else a
    me1 = me.astype(jnp.int32).reshape(1)
    rest = tuple(k for k in MATRICES if k != FIRST)
    shard_rows = dict({k: rows_of(k, w[k]).shape[0] for k in MATRICES}, taps=CWP)
    sems_first, _, thru_first, token = _ici_copies_start(
        [[(0, shard_rows[FIRST])]], None, _prep([rows_of(FIRST, w[FIRST])], None, me1, "prep_first"), [GATHER],
        "gather_start_first")
    buffers = dict(zip(rest + ("taps",), _prep([rows_of(k, w[k]) for k in rest], conv_dw_kernel, me1, "prep", deps=[token])))
    landings, sets = [], []
    for stage in GATHER_STAGES:
        sets.append([(len(landings) + i, shard_rows[k]) for i, k in enumerate(STAGE_MEMBERS[stage])])
        landings += list(STAGE_MEMBERS[stage])
    sems, _, land_thru, started = _ici_copies_start(sets, None, [buffers[k] for k in landings],
                                                    [STAGE_GATHER[s] for s in GATHER_STAGES], "gather_start")

    packed = [_pack_small(a, jnp.zeros((1, D), F32)) for a in (w, m, v)]

    def ffn1_up(x, g, after):
        chips = jnp.stack([_chip_index(chip) for chip in [(px, py)] + _other_chips(px, py)]).astype(jnp.int32)
        rows = [shard_rows[FIRST]]
        mine = _d2d_gather(thru_first, rows, "gather_d2d_first_mine", which=(0,), deps=[started])
        n, u = _ffn_up_blocks(x, g, None, mine[0], chips[:1], None, "ffn1_up_mine")
        landed = _ici_copies_wait(sems_first[0], rows, None, mine, GATHER, [u, *after, *packed], "gather_wait_first")
        w_in_t, = _d2d_gather(landed, rows, "gather_d2d_first", which=(1, 2, 3))
        n, u = _ffn_up_blocks(None, None, n, w_in_t, chips[1:3], u, "ffn1_up_next")
        (n, u), w1 = weights_of("ffn1_out", (u,), during=functools.partial(
            _ffn_up_blocks, None, None, n, w_in_t, chips[3:], u, "ffn1_up"))
        return n, u, dict(w1, ffn1_w_in=w_in_t)

    def weights_of(stage, after, during=None):
        s = GATHER_STAGES.index(stage)
        rows = [r for _, r in sets[s]]
        landed = _ici_copies_wait(sems[s], rows, None, [land_thru[k] for k, _ in sets[s]], STAGE_GATHER[stage],
                                  list(after), "gather_wait_" + stage)
        if during is not None:
            results, landed = during(swap=(landed, rows))
        elif not STAGE_GATHER[stage].all_cores:
            landed = _d2d_gather(landed, rows, "gather_d2d_" + stage)
        out = dict(zip(STAGE_MEMBERS[stage], landed))
        if "taps" in out:
            taps = out.pop("taps")
            out["conv_dw_kernel"] = jnp.transpose(taps.reshape(N_DEV, CWP, BLK), (1, 0, 2)).reshape(CWP, D)[:CW]
        return out if during is None else (results, out)

    in_flight = []

    def wgrad(lhs, rhs, name, lhs_is_transposed, deps=()):
        rows = (lhs.shape[0] if lhs_is_transposed else lhs.shape[1]) // N_DEV
        if rows <= WGRAD_SUM_MAX_ROWS:
            return ("summed",) + tuple(_wgrad_pair_sum(lhs, rhs, place, name, lhs_is_transposed=lhs_is_transposed, deps=deps))
        return ("paired",) + tuple(_wgrad_pair(lhs, rhs, name, lhs_is_transposed=lhs_is_transposed, deps=deps))

    def wgrads(items, name):
        return [("summed",) + pair for pair in _wgrad_pair_sum_many(items, place, name)]

    def grads_done(stage, grads):
        names = list(grads)
        added = []
        for k in names:
            if not isinstance(grads[k], tuple):
                received, = _rs_pair([grads[k]], "rs_pair_" + k)
                added.append(_pair_add(grads[k], received, place, "pair_add_" + k))
            elif grads[k][0] == "paired":
                added.append(_pair_add(grads[k][1], grads[k][2], place, "pair_add_" + k, kept_only=True))
            else:
                added.append(grads[k][1:])
        partials = [p for p, _ in added]
        members = [(i, p.shape[0] // 4) for i, p in enumerate(partials)]
        sem, p_thru, l_thru, token = _ici_copies_start([members], partials, [l for _, l in added], [SCATTER],
                                                       "scatter_start_" + stage)
        in_flight.append((stage, names, sem[0], p_thru, l_thru, token))
        return [token]

    small = []

    def small_done(gv, sq):
        payload = jnp.concatenate([_pack_small(gv, sq), jnp.pad(gv["conv_dw_kernel"], ((0, PAYLOAD_ROWS - ROW_TAPS - CW), (0, 0)))],
                                  axis=0)
        mine = lax.dynamic_update_slice_in_dim(lax.empty((N_DEV * PAYLOAD_ROWS, D), F32), payload, me * PAYLOAD_ROWS, axis=0)
        sems, _, thru, token = _ici_copies_start([[(0, PAYLOAD_ROWS)]], None, [mine], [GATHER_ALL], "small_start")
        small.append((sems[0], thru))
        return [token]

    vec = {k: w[k] for k in WEIGHTS if k not in MATRICES and k != "conv_dw_kernel"}
    dx0 = _local_step(x[0], loss_target[0], vec, ffn1_up, weights_of, wgrad, wgrads, grads_done, small_done)
    gathered, = _ici_copies_wait(small[0][0], [PAYLOAD_ROWS], None, small[0][1], GATHER_ALL, [in_flight[-1][-1]], "small_wait")
    total = _sum_blocks(gathered, PAYLOAD_ROWS)
    loss = (0.5 / D) * jnp.sum(total[ROW_LOSS])

    grads, delta, new_m, new_v = {}, {}, {}, {}
    after = [total]
    for stage, names, sem, p_thru, l_thru, _ in in_flight:
        landed = _ici_copies_wait(sem, [p.shape[0] // 4 for p in p_thru], p_thru, l_thru, SCATTER, after,
                                  "scatter_wait_" + stage)
        after = []
        for k, buf in zip(names, landed):
            out = _reduce_adamw(buf, rows_of(k, w[k]), rows_of(k, m[k]), rows_of(k, v[k]), "adamw_" + k)
            grads[k], delta[k], new_m[k], new_v[k] = [rows_of(k, a) for a in out]
            after.append(out[1])
    d8, m8, v8 = _adamw_small(packed[0], total[:ROW_TAPS], packed[1], packed[2], "adamw_small")
    grads.update(_unpack_small(total[:ROW_TAPS]))
    delta.update(_unpack_small(d8))
    new_m.update(_unpack_small(m8))
    new_v.update(_unpack_small(v8))
    k = "conv_dw_kernel"
    grads[k] = lax.dynamic_slice_in_dim(total[ROW_TAPS:ROW_TAPS + CW], me * BLK, BLK, axis=1)
    delta[k], new_m[k], new_v[k] = _adamw_small(w[k], grads[k], m[k], v[k], "adamw_taps")

    return (loss, dx0[None], *[grads[k] for k in WEIGHTS], *[delta[k] for k in WEIGHTS],
            *[new_m[k] for k in WEIGHTS], *[new_v[k] for k in WEIGHTS])
```

```python
import functools
import math

import numpy as np
import jax
import jax.numpy as jnp
from jax import lax
from jax.experimental import pallas as pl
from jax.experimental.pallas import tpu as pltpu

F32 = jnp.float32
BF = jnp.bfloat16

D = 1024
F = 2816
INW = 5632
CW = 31
CWP = 32
HD = 64
NQ = 16
NKV = 4
GRP = NQ // NKV
BLK = 128
NBUCKET = 32
EPS = 1e-6
NEG = float(jnp.finfo(jnp.float32).min)
QK_SCALE = 1.0 / math.sqrt(HD)
R_CONV = (0, 2048)
R_QKV = (2048, 3584)
R_Q = (2048, 3072)
R_KV = (3072, 3584)
R_GATE = (3584, 5632)

N_DEV = 8
VMEM_LIMIT_V7X = 56 * 1024 * 1024
ROW_TILE = 256
ROW_TILE_WIDE = 512
ROW_TILE_BLOCK = 1024
WGRAD_SUM_MAX_ROWS = 352

ADAM_LR = 0.001
ADAM_B1 = 0.9
ADAM_B2 = 0.999
ADAM_EPS = 1e-08
ADAM_WD = 0.01
ADAM_STEP = 10

NT_DIMS = (((1,), (1,)), ((), ()))
TN_DIMS = (((0,), (0,)), ((), ()))


def _dot(a, b):
    return jnp.dot(a, b, preferred_element_type=F32)


def _dot_nt(a, b):
    return lax.dot_general(a, b, NT_DIMS, preferred_element_type=F32)


def _dot_tn(a, b):
    return lax.dot_general(a, b, TN_DIMS, preferred_element_type=F32)


def _sig(x):
    return 0.5 * jnp.tanh(0.5 * x) + 0.5


ANY = pl.BlockSpec(memory_space=pl.ANY)


def _call(body, deps, args, **kw):
    n = len(deps)
    if n:
        kw["in_specs"] = [ANY] * n + list(kw["in_specs"])
        return pl.pallas_call(lambda *refs: body(*refs[n:]), **kw)(*deps, *args)
    return pl.pallas_call(body, **kw)(*args)


def _params(n_axes):
    return pltpu.CompilerParams(dimension_semantics=("arbitrary",) * n_axes, vmem_limit_bytes=VMEM_LIMIT_V7X)


def _resident(shape):
    zeros = (0,) * len(shape)
    return pl.BlockSpec(shape, lambda *_: zeros, pipeline_mode=pl.Buffered(1))


def _row_tile(rows, cols):
    return pl.BlockSpec((rows, cols), lambda i: (i, 0))


def _rms_stats(x):
    r = lax.rsqrt(jnp.mean(x * x, axis=-1, keepdims=True) + EPS)
    return r, x * r


def _rms_bwd(dn, x, g):
    r, xh = _rms_stats(x)
    dxh = dn * g
    dx = r * (dxh - xh * jnp.mean(dxh * xh, axis=-1, keepdims=True))
    return dx, jnp.sum(dn * xh, axis=0, keepdims=True)


def _ffn_last(x, target, g, w_in_t, w_out, name):
    t = x.shape[0]
    tm = min(ROW_TILE, t)

    def body(x_ref, t_ref, g_ref, w_ref, wo_ref, n_ref, du_ref, h_ref, dy_ref, dx_ref, sq_ref, dg_ref):
        @pl.when(pl.program_id(0) == 0)
        def _():
            sq_ref[...] = jnp.zeros_like(sq_ref)
            dg_ref[...] = jnp.zeros_like(dg_ref)

        x = x_ref[...]
        g = g_ref[...]
        r, xh = _rms_stats(x)
        n = (xh * g).astype(BF)
        n_ref[...] = n
        u = _dot_nt(n, w_ref[...])
        a = u[:, :F]
        b = u[:, F:]
        s = _sig(a)
        sa = a * s
        h = (sa * b).astype(BF)
        h_ref[...] = h
        err = x + 0.5 * _dot(h, wo_ref[...]) - t_ref[...]
        sq_ref[...] += jnp.sum(err * err, axis=0, keepdims=True)
        dxo = err * (1.0 / D)
        dy = (0.5 * dxo).astype(BF)
        dy_ref[...] = dy
        dh = _dot_nt(dy, wo_ref[...])
        du_ref[:, :F] = (dh * b * (s * (1.0 + a * (1.0 - s)))).astype(BF)
        du_ref[:, F:] = (dh * sa).astype(BF)
        dn = _dot(du_ref[...], w_ref[...])
        dxh = dn * g
        dx_ref[...] = dxo + r * (dxh - xh * jnp.mean(dxh * xh, axis=-1, keepdims=True))
        dg_ref[...] += jnp.sum(dn * xh, axis=0, keepdims=True)

    vec = pl.BlockSpec((1, D), lambda i: (0, 0))
    return pl.pallas_call(
        body, grid=(t // tm,),
        in_specs=[_row_tile(tm, D), _row_tile(tm, D), _resident((1, D)), _resident((INW, D)), _resident((F, D))],
        out_specs=[_row_tile(tm, D), _row_tile(tm, INW), _row_tile(tm, F), _row_tile(tm, D), _row_tile(tm, D), vec, vec],
        out_shape=[jax.ShapeDtypeStruct((t, D), BF), jax.ShapeDtypeStruct((t, INW), BF), jax.ShapeDtypeStruct((t, F), BF),
                   jax.ShapeDtypeStruct((t, D), BF), jax.ShapeDtypeStruct((t, D), F32), jax.ShapeDtypeStruct((1, D), F32),
                   jax.ShapeDtypeStruct((1, D), F32)],
        compiler_params=_params(1), name=name)(x, target, g, w_in_t, w_out)


def _ffn_up_blocks(x, g, n, w_in_t, order, u, name, deps=(), swap=None):
    t = (x if n is None else n).shape[0]
    tm = min(ROW_TILE_BLOCK, t)
    c = INW * 2 // N_DEV
    n_deps = len(deps)
    first = n is None
    assert not first or order.shape == (1,)

    def body(order_ref, *refs):
        refs = refs[n_deps:]
        if first:
            x_ref, g_ref, w_ref, n_ref, u_ref = refs
            nt = (_rms_stats(x_ref[...])[1] * g_ref[...]).astype(BF)
            n_ref[...] = nt
        else:
            n_ref, w_ref, _, u_ref = refs
            nt = n_ref[...]
        u_ref[...] = _dot_nt(nt, w_ref[...]).astype(BF)

    rows = pl.BlockSpec((tm, D), lambda k, i, o: (i, 0))
    block = pl.BlockSpec((c, D), lambda k, i, o: (o[k], 0))
    cols = pl.BlockSpec((tm, c), lambda k, i, o: (i, o[k]))
    u_shape = jax.ShapeDtypeStruct((t, INW), BF)
    if first:
        args, in_specs = (x, g, w_in_t), [rows, _resident((1, D)), block]
        out_specs, out_shape, aliases = [rows, cols], [jax.ShapeDtypeStruct((t, D), BF), u_shape], {}
    else:
        args, in_specs = (n, w_in_t, u), [rows, block, ANY]
        out_specs, out_shape, aliases = [cols], [u_shape], {1 + n_deps + 2: 0}
    grid = (order.shape[0], t // tm)
    if swap is not None:
        (out,), swapped = _call_with_swap(
            body, (*deps, *args), swap, prefetch=(order,), grid=grid, in_specs=[ANY] * n_deps + in_specs, out_specs=out_specs,
            out_shape=out_shape, scratch_shapes=[], input_output_aliases={n_deps + 2: 0}, compiler_params=_params(2), name=name)
        return (n, out), swapped
    out = pl.pallas_call(
        body,
        grid_spec=pltpu.PrefetchScalarGridSpec(num_scalar_prefetch=1, grid=grid, in_specs=[ANY] * n_deps + in_specs,
                                               out_specs=out_specs),
        out_shape=out_shape, input_output_aliases=aliases, compiler_params=_params(2), name=name)(order, *deps, *args)
    return tuple(out) if first else (n, out[0])


def _ffn_down(x, u, w_out, name, part=(0, 1), out=None, swap=None):
    t = x.shape[0]
    tm = min(ROW_TILE_WIDE, t)
    steps = t // tm // part[1]
    first = part[0] * steps
    others = [out] if out is not None else []

    def body(x_ref, u_ref, wo_ref, *rest):
        a = u_ref[:, :F].astype(F32)
        b = u_ref[:, F:].astype(F32)
        h = (a * _sig(a) * b).astype(BF)
        rest[-1][...] = x_ref[...] + 0.5 * _dot(h, wo_ref[...])

    tile = lambda cols: pl.BlockSpec((tm, cols), lambda i: (first + i, 0))
    kw = dict(grid=(steps,), in_specs=[tile(D), tile(INW), _resident((F, D))] + [ANY] * len(others), out_specs=[tile(D)],
              out_shape=[jax.ShapeDtypeStruct((t, D), F32)], scratch_shapes=[],
              input_output_aliases={3: 0} if others else {}, compiler_params=_params(1), name=name)
    args = (x, u, w_out, *others)
    return pl.pallas_call(body, **kw)(*args) if swap is None else _call_with_swap(body, args, swap, **kw)


def _ffn_bwd(dxo, x, g, u, w_in_t, w_out, name, deps=()):
    t = x.shape[0]
    tm = min(ROW_TILE, t)

    def body(dxo_ref, x_ref, g_ref, u_ref, w_ref, wo_ref, dx_ref, du_ref, h_ref, dy_ref, dg_ref):
        dxo = dxo_ref[...]
        dy = (0.5 * dxo).astype(BF)
        dy_ref[...] = dy
        dh = _dot_nt(dy, wo_ref[...])
        a = u_ref[:, :F].astype(F32)
        b = u_ref[:, F:].astype(F32)
        s = _sig(a)
        sa = a * s
        h_ref[...] = (sa * b).astype(BF)
        du_ref[:, :F] = (dh * b * (s * (1.0 + a * (1.0 - s)))).astype(BF)
        du_ref[:, F:] = (dh * sa).astype(BF)
        dn = _dot(du_ref[...], w_ref[...])
        dx, dg = _rms_bwd(dn, x_ref[...], g_ref[...])
        dx_ref[...] = dxo + dx

        @pl.when(pl.program_id(0) == 0)
        def _():
            dg_ref[...] = jnp.zeros_like(dg_ref)

        dg_ref[...] += dg

    return _call(
        body, deps, (dxo, x, g, u, w_in_t, w_out), grid=(t // tm,),
        in_specs=[_row_tile(tm, D), _row_tile(tm, D), _resident((1, D)), _row_tile(tm, INW), _resident((INW, D)),
                  _resident((F, D))],
        out_specs=[_row_tile(tm, D), _row_tile(tm, INW), _row_tile(tm, F), _row_tile(tm, D),
                   pl.BlockSpec((1, D), lambda i: (0, 0))],
        out_shape=[jax.ShapeDtypeStruct((t, D), F32), jax.ShapeDtypeStruct((t, INW), BF), jax.ShapeDtypeStruct((t, F), BF),
                   jax.ShapeDtypeStruct((t, D), BF), jax.ShapeDtypeStruct((1, D), F32)],
        compiler_params=_params(1), name=name)


def _wgrad(lhs, rhs, name, *, lhs_is_transposed, chunk, deps=()):
    t = rhs.shape[0]
    n = lhs.shape[0] if lhs_is_transposed else lhs.shape[1]
    c = min(chunk, n)

    def body(l_ref, r_ref, o_ref):
        if lhs_is_transposed:
            o_ref[...] = _dot(l_ref[...], r_ref[...]).astype(BF)
        else:
            o_ref[...] = _dot_tn(l_ref[...], r_ref[...]).astype(BF)

    lhs_spec = pl.BlockSpec((c, t), lambda j: (j, 0)) if lhs_is_transposed else pl.BlockSpec((t, c), lambda j: (0, j))
    return _call(
        body, deps, (lhs, rhs), grid=(n // c,),
        in_specs=[lhs_spec, _resident((t, D))],
        out_specs=pl.BlockSpec((c, D), lambda j: (j, 0)),
        out_shape=jax.ShapeDtypeStruct((n, D), BF),
        compiler_params=_params(1), name=name)


def _wgrad_mix(duc, dq_t, dkv_t, dgp, hm):
    t = hm.shape[0]
    c = 512
    first_q, first_kv, first_gate = R_Q[0] // c, R_KV[0] // c, R_GATE[0] // c

    def body(uc_ref, q_ref, kv_ref, gp_ref, h_ref, o_ref):
        j = pl.program_id(0)

        @pl.when(j < first_q)
        def _():
            o_ref[...] = _dot_tn(uc_ref[...], h_ref[...]).astype(BF)

        @pl.when((j >= first_q) & (j < first_kv))
        def _():
            o_ref[...] = _dot(q_ref[...], h_ref[...]).astype(BF)

        @pl.when((j >= first_kv) & (j < first_gate))
        def _():
            o_ref[...] = _dot(kv_ref[...], h_ref[...]).astype(BF)

        @pl.when(j >= first_gate)
        def _():
            o_ref[...] = _dot_tn(gp_ref[...], h_ref[...]).astype(BF)

    return pl.pallas_call(
        body, grid=(INW // c,),
        in_specs=[pl.BlockSpec((t, c), lambda j: (0, jnp.clip(j, 0, first_q - 1))),
                  pl.BlockSpec((c, t), lambda j: (jnp.clip(j - first_q, 0, first_kv - first_q - 1), 0)),
                  pl.BlockSpec((c, t), lambda j: (jnp.clip(j - first_kv, 0, first_gate - first_kv - 1), 0)),
                  pl.BlockSpec((t, c), lambda j: (0, jnp.clip(j - first_gate, 0, INW // c - first_gate - 1))),
                  _resident((t, D))],
        out_specs=pl.BlockSpec((c, D), lambda j: (j, 0)),
        out_shape=jax.ShapeDtypeStruct((INW, D), BF),
        compiler_params=_params(1), name="mix_dw_in")(duc, dq_t, dkv_t, dgp, hm)


def _mix_proj(x, g, w_t):
    t = x.shape[0]
    tm = min(ROW_TILE_WIDE, t)

    def body(x_ref, g_ref, w_ref, hm_ref, uc_ref, gp_ref, qkv_ref):
        r, xh = _rms_stats(x_ref[...])
        hm = (xh * g_ref[...]).astype(BF)
        hm_ref[...] = hm
        uc_ref[...] = _dot_nt(hm, w_ref[R_CONV[0]:R_CONV[1], :]).astype(BF)
        gp_ref[...] = _dot_nt(hm, w_ref[R_GATE[0]:R_GATE[1], :]).astype(BF)
        qkv_ref[...] = _dot_nt(w_ref[R_QKV[0]:R_QKV[1], :], hm).astype(BF)

    return pl.pallas_call(
        body, grid=(t // tm,),
        in_specs=[_row_tile(tm, D), _resident((1, D)), _resident((INW, D))],
        out_specs=[_row_tile(tm, D), _row_tile(tm, 2 * D), _row_tile(tm, 2 * D), pl.BlockSpec((1536, tm), lambda i: (0, i))],
        out_shape=[jax.ShapeDtypeStruct((t, D), BF), jax.ShapeDtypeStruct((t, 2 * D), BF),
                   jax.ShapeDtypeStruct((t, 2 * D), BF), jax.ShapeDtypeStruct((1536, t), BF)],
        compiler_params=_params(1), name="mix_proj")(x, g, w_t)


CONV_HALO = 32
CONV_LEAD = CONV_HALO - (CW - 1)


def _glu(uc):
    uc = uc.astype(F32)
    return uc[:, :D] * _sig(uc[:, D:])


def _ln_stats(zc):
    mu = jnp.mean(zc, axis=-1, keepdims=True)
    zm = zc - mu
    r = lax.rsqrt(jnp.mean(zm * zm, axis=-1, keepdims=True) + EPS)
    return r, zm * r


CONV_SHIFTS = 8
CONV_CHUNK = 32


def _store_shifted(buf, rows):
    for b in range(1, CONV_SHIFTS):
        buf[b, 0:rows - 8, :] = buf[0, pl.ds(b, rows - 8), :]


def _conv_fwd(uc, dwk, dwb, lng, lnb, swap=None):
    t = uc.shape[0]
    tm = min(512, t)
    per = tm // CONV_HALO
    ext = tm + CONV_HALO

    def body(cur_ref, prev_ref, k_ref, kb_ref, g_ref, b_ref, o_ref, zc_ref, zsh):
        i = pl.program_id(0)
        zsh[0, 0:CONV_HALO, :] = _glu(prev_ref[...]) * (i > 0).astype(F32)
        zsh[0, CONV_HALO:, :] = _glu(cur_ref[...])
        _store_shifted(zsh, ext)

        def chunk(ci, carry):
            r0 = pl.multiple_of(ci * CONV_CHUNK, CONV_CHUNK)
            acc = jnp.zeros((CONV_CHUNK, D), F32) + kb_ref[...]
            for w in range(CW):
                a, b = divmod(CONV_LEAD + w, 8)
                acc = acc + k_ref[w:w + 1, :] * zsh[b, pl.ds(r0 + 8 * a, CONV_CHUNK), :]
            zc_ref[pl.ds(r0, CONV_CHUNK), :] = acc
            return carry

        lax.fori_loop(0, tm // CONV_CHUNK, chunk, 0)
        r, xh = _ln_stats(zc_ref[...])
        y = xh * g_ref[...] + b_ref[...]
        o_ref[...] = (y * _sig(y)).astype(BF)

    kw = dict(
        grid=(t // tm,),
        in_specs=[_row_tile(tm, 2 * D),
                  pl.BlockSpec((CONV_HALO, 2 * D), lambda i: (jnp.maximum(i * per - 1, 0), 0)),
                  _resident((CWP, D)), _resident((1, D)), _resident((1, D)), _resident((1, D))],
        out_specs=[_row_tile(tm, D), _row_tile(tm, D)],
        out_shape=[jax.ShapeDtypeStruct((t, D), BF), jax.ShapeDtypeStruct((t, D), F32)],
        scratch_shapes=[pltpu.VMEM((CONV_SHIFTS, ext, D), F32)],
        compiler_params=_params(1), name="conv_fwd")
    args = (uc, uc, dwk, dwb, lng, lnb)
    return pl.pallas_call(body, **kw)(*args) if swap is None else _call_with_swap(body, args, swap, **kw)


def _conv_bwd(uc, zc, dzs, dwk, lng, lnb):
    t = uc.shape[0]
    tm = min(ROW_TILE_WIDE, t)
    per = tm // CONV_HALO
    n_tiles = t // tm
    ext = tm + CONV_HALO
    last_block = t // CONV_HALO - 1

    def body(cur_ref, zc_ref, zcn_ref, dz_ref, dzn_ref, k_ref, g_ref, b_ref,
             duc_ref, dk_ref, dkb_ref, dg_ref, db_ref, dsh, dk8, z_scr):
        i = pl.program_id(0)

        @pl.when(i == 0)
        def _():
            dk8[...] = jnp.zeros_like(dk8)
            dkb_ref[...] = jnp.zeros_like(dkb_ref)
            dg_ref[...] = jnp.zeros_like(dg_ref)
            db_ref[...] = jnp.zeros_like(db_ref)

        has_next = (i < n_tiles - 1).astype(F32)
        z_scr[...] = _glu(cur_ref[...])
        gain = g_ref[...]

        def ln_silu_bwd(zc, dzs, live):
            r, xh = _ln_stats(zc)
            y = xh * gain + b_ref[...]
            sy = _sig(y)
            dy = dzs * (sy * (1.0 + y * (1.0 - sy))) * live
            dxh = dy * gain
            dzc = r * (dxh - jnp.mean(dxh, axis=-1, keepdims=True) - xh * jnp.mean(dxh * xh, axis=-1, keepdims=True))
            return dzc, dy, xh

        dzc, dy, xh = ln_silu_bwd(zc_ref[...], dz_ref[...], 1.0)
        dsh[0, 0:tm, :] = dzc
        dg_ref[...] += jnp.sum(dy * xh, axis=0, keepdims=True)
        db_ref[...] += jnp.sum(dy, axis=0, keepdims=True)
        dkb_ref[...] += jnp.sum(dzc, axis=0, keepdims=True)
        dsh[0, tm:, :] = ln_silu_bwd(zcn_ref[...], dzn_ref[...], has_next)[0]
        _store_shifted(dsh, ext)

        def chunk(ci, carry):
            r0 = pl.multiple_of(ci * CONV_CHUNK, CONV_CHUNK)
            z_c = z_scr[pl.ds(r0, CONV_CHUNK), :]
            dz = jnp.zeros((CONV_CHUNK, D), F32)
            for w in range(CW):
                a, b = divmod(CW - 1 - w, 8)
                window = dsh[b, pl.ds(r0 + 8 * a, CONV_CHUNK), :]
                dz = dz + k_ref[w:w + 1, :] * window
                prod = z_c * window
                part = prod[0:8, :]
                for j in range(1, CONV_CHUNK // 8):
                    part = part + prod[8 * j:8 * j + 8, :]
                dk8[w] += part
            ucc = cur_ref[pl.ds(r0, CONV_CHUNK), :].astype(F32)
            sg = _sig(ucc[:, D:])
            duc_ref[pl.ds(r0, CONV_CHUNK), 0:D] = (dz * sg).astype(BF)
            duc_ref[pl.ds(r0, CONV_CHUNK), D:2 * D] = (dz * ucc[:, :D] * sg * (1.0 - sg)).astype(BF)
            return carry

        lax.fori_loop(0, tm // CONV_CHUNK, chunk, 0)

        @pl.when(i == n_tiles - 1)
        def _():
            dk_ref[...] = jnp.sum(dk8[...], axis=1)

    vec = pl.BlockSpec((1, D), lambda i: (0, 0))
    next_halo = pl.BlockSpec((CONV_HALO, D), lambda i: (jnp.minimum((i + 1) * per, last_block), 0))
    return pl.pallas_call(
        body, grid=(n_tiles,),
        in_specs=[_row_tile(tm, 2 * D), _row_tile(tm, D), next_halo, _row_tile(tm, D), next_halo,
                  _resident((CWP, D)), _resident((1, D)), _resident((1, D))],
        out_specs=[_row_tile(tm, 2 * D), pl.BlockSpec((CWP, D), lambda i: (0, 0)), vec, vec, vec],
        out_shape=[jax.ShapeDtypeStruct((t, 2 * D), BF), jax.ShapeDtypeStruct((CWP, D), F32),
                   jax.ShapeDtypeStruct((1, D), F32), jax.ShapeDtypeStruct((1, D), F32), jax.ShapeDtypeStruct((1, D), F32)],
        scratch_shapes=[pltpu.VMEM((CONV_SHIFTS, ext, D), F32), pltpu.VMEM((CWP, 8, D), F32), pltpu.VMEM((tm, D), F32)],
        compiler_params=_params(1), name="conv_bwd")(uc, zc, zc, dzs, dzs, dwk, lng, lnb)


def _norm_rows(xt, g):
    r = lax.rsqrt(jnp.mean(xt * xt, axis=0, keepdims=True) + EPS)
    xh = xt * r
    return xh * g, r, xh


ATT_TQ = 1024


def _attn_specs(t, tq):
    per = tq // BLK
    return [pl.BlockSpec((1536, tq), lambda i: (0, i)),
            pl.BlockSpec((512, BLK), lambda i: (2, jnp.maximum(i * per - 1, 0))),
            _resident((HD, 1)), _resident((HD, 1)), _resident((NKV, 1, GRP * BLK)),
            _resident((2, NKV, 2 * BLK, GRP * BLK))]


def _attn_window(hk, sb, qkv_ref, halo_ref, kn_cur, kn_halo):
    v0 = D + NKV * HD + hk * HD
    if sb == 0:
        k_prev = kn_halo[hk]
        v_prev = halo_ref[NKV * HD + hk * HD:NKV * HD + (hk + 1) * HD, :]
    else:
        k_prev = kn_cur[hk][:, (sb - 1) * BLK:sb * BLK]
        v_prev = qkv_ref[v0:v0 + HD, (sb - 1) * BLK:sb * BLK]
    kw = jnp.concatenate([k_prev, kn_cur[hk][:, sb * BLK:(sb + 1) * BLK]], axis=1).astype(BF)
    vw = jnp.concatenate([v_prev, qkv_ref[v0:v0 + HD, sb * BLK:(sb + 1) * BLK]], axis=1)
    return kw, vw


def _attn_probs(kw, qc, bias, sink):
    st = _dot_tn(kw, qc) + bias
    m = jnp.maximum(jnp.max(st, axis=0, keepdims=True), sink)
    p = jnp.exp(st - m)
    e_sink = jnp.exp(sink - m)
    inv = 1.0 / (jnp.sum(p, axis=0, keepdims=True) + e_sink)
    return p * inv, e_sink * inv


def _attn_fwd(qkv_t, qg, kg, sink_rows, bias_t):
    t = qkv_t.shape[1]
    tq = min(ATT_TQ, t)
    n_sub = tq // BLK

    def body(qkv_ref, halo_ref, qg_ref, kg_ref, sink_ref, bias_ref, o_ref, p_ref, ps_ref):
        i = pl.program_id(0)
        first = (i == 0).astype(jnp.int32)
        kgain = kg_ref[...]
        qgain = qg_ref[...]
        kn_cur = [_norm_rows(qkv_ref[D + h * HD:D + (h + 1) * HD, :].astype(F32), kgain)[0] for h in range(NKV)]
        kn_halo = [_norm_rows(halo_ref[h * HD:(h + 1) * HD, :].astype(F32), kgain)[0] for h in range(NKV)]
        for hk in range(NKV):
            for sb in range(n_sub):
                cols = slice(sb * BLK, (sb + 1) * BLK)
                kw, vw = _attn_window(hk, sb, qkv_ref, halo_ref, kn_cur, kn_halo)
                qc = jnp.concatenate(
                    [_norm_rows(qkv_ref[(GRP * hk + g) * HD:(GRP * hk + g + 1) * HD, cols].astype(F32), qgain)[0] * QK_SCALE
                     for g in range(GRP)], axis=1).astype(BF)
                bias = bias_ref[first, hk] if sb == 0 else bias_ref[0, hk]
                p, p_sink = _attn_probs(kw, qc, bias, sink_ref[hk])
                p = p.astype(BF)
                p_ref[sb, hk] = p
                ps_ref[sb, hk] = p_sink
                o = _dot(vw, p)
                for g in range(GRP):
                    head = GRP * hk + g
                    o_ref[head * HD:(head + 1) * HD, cols] = o[:, g * BLK:(g + 1) * BLK].astype(BF)

    return pl.pallas_call(
        body, grid=(t // tq,),
        in_specs=_attn_specs(t, tq),
        out_specs=[pl.BlockSpec((D, tq), lambda i: (0, i)),
                   pl.BlockSpec((n_sub, NKV, 2 * BLK, GRP * BLK), lambda i: (i, 0, 0, 0)),
                   pl.BlockSpec((n_sub, NKV, 1, GRP * BLK), lambda i: (i, 0, 0, 0))],
        out_shape=[jax.ShapeDtypeStruct((D, t), BF), jax.ShapeDtypeStruct((t // BLK, NKV, 2 * BLK, GRP * BLK), BF),
                   jax.ShapeDtypeStruct((t // BLK, NKV, 1, GRP * BLK), F32)],
        compiler_params=_params(1), name="attn_fwd")(qkv_t, qkv_t, qg, kg, sink_rows, bias_t)


def _attn_bwd(qkv_t, do_t, probs, sink_probs, qg, kg, onehot_t, deps=()):
    t = qkv_t.shape[1]
    tq = min(ATT_TQ, t)
    n_sub = tq // BLK
    n_tiles = t // tq

    def body(qkv_ref, halo_ref, do_ref, p_ref, ps_ref, qg_ref, kg_ref, oh_ref,
             dq_ref, ckv_ref, dqg_ref, dsink_ref, dbias_ref, qg_scr, sink_scr, ds_scr):
        i = pl.program_id(0)

        @pl.when(i == 0)
        def _():
            qg_scr[...] = jnp.zeros_like(qg_scr)
            sink_scr[...] = jnp.zeros_like(sink_scr)
            ds_scr[...] = jnp.zeros_like(ds_scr)

        kgain = kg_ref[...]
        qgain = qg_ref[...]
        kn_cur = [_norm_rows(qkv_ref[D + h * HD:D + (h + 1) * HD, :].astype(F32), kgain)[0] for h in range(NKV)]
        kn_halo = [_norm_rows(halo_ref[h * HD:(h + 1) * HD, :].astype(F32), kgain)[0] for h in range(NKV)]
        dqg = jnp.zeros((HD, BLK), F32)
        for hk in range(NKV):
            for sb in range(n_sub):
                cols = slice(sb * BLK, (sb + 1) * BLK)
                kw, vw = _attn_window(hk, sb, qkv_ref, halo_ref, kn_cur, kn_halo)
                qn, qr, qh = [], [], []
                for g in range(GRP):
                    head = GRP * hk + g
                    n_, r_, h_ = _norm_rows(qkv_ref[head * HD:(head + 1) * HD, cols].astype(F32), qgain)
                    qn.append(n_)
                    qr.append(r_)
                    qh.append(h_)
                qc = (jnp.concatenate(qn, axis=1) * QK_SCALE).astype(BF)
                p_bf = p_ref[sb, hk]
                p = p_bf.astype(F32)
                doc = jnp.concatenate([do_ref[(GRP * hk + g) * HD:(GRP * hk + g + 1) * HD, cols] for g in range(GRP)], axis=1)
                dp = _dot_tn(vw, doc)
                delta = jnp.sum(p * dp, axis=0, keepdims=True)
                ds = p * (dp - delta)
                sink_scr[hk] += -(ps_ref[sb, hk] * delta)
                ds_scr[hk] += ds
                dsb = ds.astype(BF)
                dqc = _dot(kw, dsb) * QK_SCALE
                ckv_ref[sb, hk * HD:(hk + 1) * HD, :] = _dot_nt(qc, dsb)
                ckv_ref[sb, NKV * HD + hk * HD:NKV * HD + (hk + 1) * HD, :] = _dot_nt(doc, p_bf)
                for g in range(GRP):
                    head = GRP * hk + g
                    dqn = dqc[:, g * BLK:(g + 1) * BLK]
                    dqh = dqn * qgain
                    dq = qr[g] * (dqh - qh[g] * jnp.mean(dqh * qh[g], axis=0, keepdims=True))
                    dq_ref[head * HD:(head + 1) * HD, cols] = dq.astype(BF)
                    dqg = dqg + dqn * qh[g]
        qg_scr[...] += dqg

        @pl.when(i == n_tiles - 1)
        def _():
            dqg_ref[...] = jnp.sum(qg_scr[...], axis=1, keepdims=True)
            dsink_ref[...] = _group_lane_sums(sink_scr[:, 0, :])

            def bucket(b, carry):
                oh = jnp.concatenate([oh_ref[b]] * GRP, axis=1)
                dbias_ref[b] = _group_lane_sums(jnp.sum(ds_scr[...] * oh[None], axis=1))
                return carry

            lax.fori_loop(0, NBUCKET, bucket, 0)

    return _call(
        body, deps, (qkv_t, qkv_t, do_t, probs, sink_probs, qg, kg, onehot_t), grid=(n_tiles,),
        in_specs=_attn_specs(t, tq)[:2] + [pl.BlockSpec((D, tq), lambda i: (0, i)),
                                           pl.BlockSpec((n_sub, NKV, 2 * BLK, GRP * BLK), lambda i: (i, 0, 0, 0)),
                                           pl.BlockSpec((n_sub, NKV, 1, GRP * BLK), lambda i: (i, 0, 0, 0))]
        + _attn_specs(t, tq)[2:4] + [_resident((NBUCKET, 2 * BLK, BLK))],
        out_specs=[pl.BlockSpec((D, tq), lambda i: (0, i)),
                   pl.BlockSpec((n_sub, 2 * NKV * HD, 2 * BLK), lambda i: (i, 0, 0)),
                   pl.BlockSpec((HD, 1), lambda i: (0, 0)),
                   pl.BlockSpec((NKV, BLK), lambda i: (0, 0)),
                   pl.BlockSpec((NBUCKET, NKV, BLK), lambda i: (0, 0, 0))],
        out_shape=[jax.ShapeDtypeStruct((D, t), BF),
                   jax.ShapeDtypeStruct((t // BLK, 2 * NKV * HD, 2 * BLK), F32),
                   jax.ShapeDtypeStruct((HD, 1), F32),
                   jax.ShapeDtypeStruct((NKV, BLK), F32),
                   jax.ShapeDtypeStruct((NBUCKET, NKV, BLK), F32)],
        scratch_shapes=[pltpu.VMEM((HD, BLK), F32), pltpu.VMEM((NKV, 1, GRP * BLK), F32),
                        pltpu.VMEM((NKV, 2 * BLK, GRP * BLK), F32)],
        compiler_params=_params(1), name="attn_bwd")


def _kv_combine_tile(c_ref, cn_ref, has_next, k_ref, kgain, o_ref):
    rows = NKV * HD
    per = c_ref.shape[0]
    dkg = jnp.zeros((HD, BLK), F32)
    for s in range(per):
        cols = slice(s * BLK, (s + 1) * BLK)
        after = c_ref[s + 1, :, :BLK] if s + 1 < per else cn_ref[0, :, :BLK] * has_next
        d = c_ref[s, :, BLK:] + after
        o_ref[rows:, cols] = d[rows:, :].astype(BF)
        for h in range(NKV):
            _, r, kh = _norm_rows(k_ref[h * HD:(h + 1) * HD, cols].astype(F32), kgain)
            dkn = d[h * HD:(h + 1) * HD, :]
            dkh = dkn * kgain
            o_ref[h * HD:(h + 1) * HD, cols] = (r * (dkh - kh * jnp.mean(dkh * kh, axis=0, keepdims=True))).astype(BF)
            dkg = dkg + dkn * kh
    return dkg


def _group_lane_sums(v):
    lane_group = lax.broadcasted_iota(jnp.int32, (1, GRP * BLK), 1) // BLK
    col = lax.broadcasted_iota(jnp.int32, (1, BLK), 1)
    out = jnp.zeros((v.shape[0], BLK), F32)
    for g in range(GRP):
        s = jnp.sum(jnp.where(lane_group == g, v, 0.0), axis=1, keepdims=True)
        out = jnp.where(col == g, s, out)
    return out


def _mix_out(zs, o_t, gp, x, w_cp, w_o, w_out):
    t = x.shape[0]
    tm = min(ROW_TILE_WIDE, t)

    def body(zs_ref, ot_ref, gp_ref, x_ref, wcp_ref, wo_ref, wout_ref, xo_ref, a_ref, b_ref, m_ref):
        a = _dot(zs_ref[...], wcp_ref[...])
        b = _dot_tn(ot_ref[...], wo_ref[...])
        a_ref[...] = a.astype(BF)
        b_ref[...] = b.astype(BF)
        merged = (_sig(gp_ref[:, :D].astype(F32)) * a + _sig(gp_ref[:, D:].astype(F32)) * b).astype(BF)
        m_ref[...] = merged
        xo_ref[...] = x_ref[...] + _dot(merged, wout_ref[...])

    return pl.pallas_call(
        body, grid=(t // tm,),
        in_specs=[_row_tile(tm, D), pl.BlockSpec((D, tm), lambda i: (0, i)), _row_tile(tm, 2 * D), _row_tile(tm, D),
                  _resident((D, D)), _resident((D, D)), _resident((D, D))],
        out_specs=[_row_tile(tm, D)] * 4,
        out_shape=[jax.ShapeDtypeStruct((t, D), F32)] + [jax.ShapeDtypeStruct((t, D), BF)] * 3,
        compiler_params=_params(1), name="mix_out")(zs, o_t, gp, x, w_cp, w_o, w_out)


def _mix_out_bwd(dx, a, b, gp, w_cp, w_o, w_out, deps=()):
    t = dx.shape[0]
    tm = min(ROW_TILE_WIDE, t)

    def body(dx_ref, a_ref, b_ref, gp_ref, wcp_ref, wo_ref, wout_ref, dzs_ref, dot_ref, dgp_ref, da_ref, db_ref, dxb_ref):
        dxb = dx_ref[...].astype(BF)
        dxb_ref[...] = dxb
        dm = _dot_nt(dxb, wout_ref[...])
        gc = _sig(gp_ref[:, :D].astype(F32))
        ga = _sig(gp_ref[:, D:].astype(F32))
        da = (dm * gc).astype(BF)
        db = (dm * ga).astype(BF)
        da_ref[...] = da
        db_ref[...] = db
        dgp_ref[:, :D] = (dm * a_ref[...].astype(F32) * gc * (1.0 - gc)).astype(BF)
        dgp_ref[:, D:] = (dm * b_ref[...].astype(F32) * ga * (1.0 - ga)).astype(BF)
        dzs_ref[...] = _dot_nt(da, wcp_ref[...])
        dot_ref[...] = _dot_nt(wo_ref[...], db).astype(BF)

    return _call(
        body, deps, (dx, a, b, gp, w_cp, w_o, w_out), grid=(t // tm,),
        in_specs=[_row_tile(tm, D), _row_tile(tm, D), _row_tile(tm, D), _row_tile(tm, 2 * D),
                  _resident((D, D)), _resident((D, D)), _resident((D, D))],
        out_specs=[_row_tile(tm, D), pl.BlockSpec((D, tm), lambda i: (0, i)), _row_tile(tm, 2 * D),
                   _row_tile(tm, D), _row_tile(tm, D), _row_tile(tm, D)],
        out_shape=[jax.ShapeDtypeStruct((t, D), F32), jax.ShapeDtypeStruct((D, t), BF), jax.ShapeDtypeStruct((t, 2 * D), BF),
                   jax.ShapeDtypeStruct((t, D), BF), jax.ShapeDtypeStruct((t, D), BF), jax.ShapeDtypeStruct((t, D), BF)],
        compiler_params=_params(1), name="mix_out_bwd")


def _mix_proj_bwd(dxo, duc, dq_t, ckv, qkv_t, kg, dgp, x, g, w_t):
    t = x.shape[0]
    tm = min(ROW_TILE_WIDE, t)
    per = tm // BLK
    steps = t // tm
    kv_rows = 2 * NKV * HD

    def body(dxo_ref, duc_ref, dq_ref, c_ref, cn_ref, k_ref, kg_ref, dgp_ref, x_ref, g_ref, w_ref,
             dx_ref, dg_ref, dkv_ref, dkg_ref, kg_scr):
        i = pl.program_id(0)

        @pl.when(i == 0)
        def _():
            dg_ref[...] = jnp.zeros_like(dg_ref)
            kg_scr[...] = jnp.zeros_like(kg_scr)

        kg_scr[...] += _kv_combine_tile(c_ref, cn_ref, (i < steps - 1).astype(F32), k_ref, kg_ref[...], dkv_ref)
        dn = _dot(duc_ref[...], w_ref[R_CONV[0]:R_CONV[1], :])
        dn = dn + _dot(dgp_ref[...], w_ref[R_GATE[0]:R_GATE[1], :])
        dn = dn + _dot_tn(dq_ref[...], w_ref[R_Q[0]:R_Q[1], :])
        dn = dn + _dot_tn(dkv_ref[...], w_ref[R_KV[0]:R_KV[1], :])
        dx, dg = _rms_bwd(dn, x_ref[...], g_ref[...])
        dx_ref[...] = dxo_ref[...] + dx
        dg_ref[...] += dg

        @pl.when(i == steps - 1)
        def _():
            dkg_ref[...] = jnp.sum(kg_scr[...], axis=1, keepdims=True)

    return pl.pallas_call(
        body, grid=(steps,),
        in_specs=[_row_tile(tm, D), _row_tile(tm, 2 * D), pl.BlockSpec((D, tm), lambda i: (0, i)),
                  pl.BlockSpec((per, kv_rows, 2 * BLK), lambda i: (i, 0, 0)),
                  pl.BlockSpec((1, kv_rows, 2 * BLK), lambda i: (jnp.minimum((i + 1) * per, t // BLK - 1), 0, 0)),
                  pl.BlockSpec((NKV * HD, tm), lambda i: (D // (NKV * HD), i)), _resident((HD, 1)),
                  _row_tile(tm, 2 * D), _row_tile(tm, D), _resident((1, D)), _resident((INW, D))],
        out_specs=[_row_tile(tm, D), pl.BlockSpec((1, D), lambda i: (0, 0)), pl.BlockSpec((kv_rows, tm), lambda i: (0, i)),
                   pl.BlockSpec((HD, 1), lambda i: (0, 0))],
        out_shape=[jax.ShapeDtypeStruct((t, D), F32), jax.ShapeDtypeStruct((1, D), F32),
                   jax.ShapeDtypeStruct((kv_rows, t), BF), jax.ShapeDtypeStruct((HD, 1), F32)],
        scratch_shapes=[pltpu.VMEM((HD, BLK), F32)],
        compiler_params=_params(1), name="mix_proj_bwd")(dxo, duc, dq_t, ckv, ckv, qkv_t, kg, dgp, x, g, w_t)


def _attention_tables():
    kj = np.arange(2 * BLK)[:, None]
    qi = np.arange(BLK)[None, :]
    dist = qi + BLK - kj
    in_win = (dist >= 0) & (dist < BLK)
    dpos = np.maximum(dist, 0)
    max_exact = NBUCKET // 2
    dfl = np.maximum(dpos, 1).astype(np.float32)
    large = max_exact + (np.log(dfl / np.float32(max_exact)) / np.float32(math.log(BLK / max_exact))
                         * np.float32(NBUCKET - max_exact)).astype(np.int32)
    large = np.minimum(large, NBUCKET - 1)
    bucket = np.where(dpos < max_exact, dpos, large)
    onehot = (bucket[None] == np.arange(NBUCKET)[:, None, None]).astype(np.float32)
    mask = in_win.astype(np.float32)
    mask_first = mask * (kj >= BLK)
    masks = np.stack([np.tile(mask, (1, GRP)), np.tile(mask_first, (1, GRP))])
    return onehot, masks


def _bias_table(rel_bias, onehot):
    tab = jnp.einsum("bkq,bh->hkq", onehot, rel_bias, precision=lax.Precision.HIGHEST)
    tab = tab.reshape(NKV, GRP, 2 * BLK, BLK)
    return jnp.transpose(tab, (0, 2, 1, 3)).reshape(NKV, 2 * BLK, GRP * BLK)


def _local_step(x, target, vec, ffn1_up, weights_of, wgrad, wgrads, grads_done, small_done):
    onehot_np, masks_np = _attention_tables()
    onehot = jnp.asarray(onehot_np)
    masks = jnp.asarray(masks_np)
    bias_t = jnp.where(masks[:, None] > 0.5, _bias_table(vec["rel_bias"], onehot)[None], NEG)
    sink_rows = jnp.repeat(vec["attn_sinks"].reshape(NKV, 1, GRP), BLK, axis=2)
    qg = vec["q_norm"].reshape(HD, 1)
    kg = vec["k_norm"].reshape(HD, 1)
    g1 = vec["ffn1_norm"].reshape(1, D)
    gm = vec["mix_norm"].reshape(1, D)
    g2 = vec["ffn2_norm"].reshape(1, D)
    dwb = vec["conv_dw_bias"].reshape(1, D)
    lng = vec["conv_ln_g"].reshape(1, D)
    lnb = vec["conv_ln_b"].reshape(1, D)

    n1, u1, w1 = ffn1_up(x, g1, (bias_t, sink_rows))
    x1, = _ffn_down(x, u1, w1["ffn1_w_out"], "ffn1_down_first", part=(0, 2))
    (x1,), wm = weights_of("mix_proj", (x1,), during=functools.partial(
        _ffn_down, x, u1, w1["ffn1_w_out"], "ffn1_down", part=(1, 2), out=x1))
    dwk = jnp.pad(wm["conv_dw_kernel"], ((0, CWP - CW), (0, 0)))
    hm, uc, gp, qkv_t = _mix_proj(x1, gm, wm["w_in"])
    (zs, zc), merge = weights_of("mix_merge", (uc,), during=functools.partial(_conv_fwd, uc, dwk, dwb, lng, lnb))
    wm.update(merge)
    o_t, probs, sink_probs = _attn_fwd(qkv_t, qg, kg, sink_rows, bias_t)
    x2, a, b, merged = _mix_out(zs, o_t, gp, x1, wm["conv_w_proj"], wm["attn_w_o"], wm["w_out"])
    w2 = weights_of("ffn2", (x2,))
    gv = {}
    n2, du2, h2, dy2, dx2, sq, gv["ffn2_norm"] = _ffn_last(x2, target, g2, w2["ffn2_w_in"], w2["ffn2_w_out"], "ffn2")

    deps = grads_done("ffn2", {"ffn2_w_in": wgrad(du2, n2, "ffn2_dw_in", False),
                               "ffn2_w_out": wgrad(h2, dy2, "ffn2_dw_out", False)})

    dzs, do_t, dgp, da, db, dx2b = _mix_out_bwd(dx2, a, b, gp, wm["conv_w_proj"], wm["attn_w_o"], wm["w_out"], deps=deps)
    grads = wgrads([(merged, dx2b, False), (zs, da, False), (o_t, db, True)], "mix_dw_merge")
    deps = grads_done("mix_out", dict(zip(("w_out", "conv_w_proj", "attn_w_o"), grads)))

    dq_t, ckv, dqg, dsink, dbias = _attn_bwd(qkv_t, do_t, probs, sink_probs, qg, kg, onehot, deps=deps)
    gv["q_norm"] = dqg.reshape(HD)
    gv["attn_sinks"] = dsink[:, :GRP].reshape(NQ)
    gv["rel_bias"] = dbias[:, :, :GRP].reshape(NBUCKET, NQ)

    duc, dk_conv, gv["conv_dw_bias"], gv["conv_ln_g"], gv["conv_ln_b"] = _conv_bwd(uc, zc, dzs, dwk, lng, lnb)
    gv["conv_dw_kernel"] = dk_conv[:CW]

    dx1, gv["mix_norm"], dkv_t, dkg = _mix_proj_bwd(dx2, duc, dq_t, ckv, qkv_t, kg, dgp, x1, gm, wm["w_in"])
    gv["k_norm"] = dkg.reshape(HD)
    deps = grads_done("mix_in", {"w_in": _wgrad_mix(duc, dq_t, dkv_t, dgp, hm)})

    dx0, du1, h1, dy1, gv["ffn1_norm"] = _ffn_bwd(dx1, x, g1, u1, w1["ffn1_w_in"], w1["ffn1_w_out"], "ffn1_bwd", deps=deps)
    for k in ("ffn1_norm", "mix_norm", "ffn2_norm", "conv_dw_bias", "conv_ln_g", "conv_ln_b"):
        gv[k] = gv[k].reshape(D)
    deps = small_done(gv, sq)
    deps = grads_done("ffn1_in", {"ffn1_w_in": wgrad(du1, n1, "ffn1_dw_in", False, deps)})
    grads_done("ffn1_out", {"ffn1_w_out": wgrad(h1, dy1, "ffn1_dw_out", False, deps)})
    return dx0


MESH_ID = pl.DeviceIdType.MESH


def _position():
    return lax.axis_index("x"), lax.axis_index("y"), lax.axis_index("c")


def _shard_rows(ref, index, rows):
    return ref.at[pl.ds(pl.multiple_of(index * rows, 16), rows), :]


def _prep(weights, taps, me, name, deps=()):
    n = len(weights)
    n_deps = len(deps)
    with_taps = taps is not None

    def body(me_ref, *refs):
        refs = refs[n_deps:]
        ins, outs = refs[:len(refs) // 2], refs[len(refs) // 2:]
        for k in range(n):
            outs[k][...] = ins[k][...].astype(BF)
        if with_taps:
            outs[n][0:CW, :] = ins[n][...]
            outs[n][CW:, :] = jnp.zeros((CWP - CW, BLK), F32)

    shard_shapes = [w.shape for w in weights] + [(CWP, BLK)] * with_taps
    dtypes = [BF] * n + [F32] * with_taps
    ins = list(weights) + [taps] * with_taps
    return pl.pallas_call(
        body,
        grid_spec=pltpu.PrefetchScalarGridSpec(
            num_scalar_prefetch=1, grid=(1,),
            in_specs=[ANY] * n_deps + [pl.BlockSpec(a.shape, lambda i, m: (0, 0), pipeline_mode=pl.Buffered(1)) for a in ins],
            out_specs=[pl.BlockSpec(s, lambda i, m: (m[0], 0)) for s in shard_shapes]),
        out_shape=[jax.ShapeDtypeStruct((N_DEV * s[0], s[1]), d) for s, d in zip(shard_shapes, dtypes)],
        compiler_params=_params(1), name=name)(me, *deps, *ins)


HBM = pl.BlockSpec(memory_space=pltpu.HBM)
SEM = pl.BlockSpec(memory_space=pltpu.SEMAPHORE)
DATAFLOW = pltpu.SideEffectType.DATAFLOW_SIDE_EFFECTING
TOKEN = jax.ShapeDtypeStruct((8, 128), F32)


def _in_hbm(x):
    return pltpu.with_memory_space_constraint(x, pltpu.HBM)


def _hbm_like(arrays):
    return [pltpu.HBM(a.shape, a.dtype) for a in arrays]


def _other_chips(x, y):
    return [(1 - x, y), (x, 1 - y), (1 - x, 1 - y)]


def _device_index(chip, c):
    return 4 * chip[0] + 2 * chip[1] + c


def _chip_index(chip):
    return 2 * chip[0] + chip[1]


class _Exchange:
    def __init__(self, gather, all_cores=False):
        self.gather = gather
        self.all_cores = all_cores
        self.n_peers = N_DEV - 1 if all_cores else 3

    def peers(self, x, y, c):
        if self.all_cores:
            return [(x ^ (k >> 2), y ^ ((k >> 1) & 1), c ^ (k & 1)) for k in range(1, N_DEV)]
        return [(*chip, c) for chip in _other_chips(x, y)]

    def sent(self, x, y, c, peer):
        return _device_index((x, y), c) if self.gather else _chip_index(peer[:2])

    def lands_at(self, x, y, c):
        return _device_index((x, y), c) if self.gather else _chip_index((x, y))

    def arrives_at(self, peer):
        return _device_index(peer[:2], peer[2]) if self.gather else _chip_index(peer[:2])


def _ici_copies_start(sets, sources, landings, exchanges, name, deps=()):
    n = len(landings)
    arrays = (list(sources) if sources is not None else []) + list(landings)
    first_land = len(arrays) - n
    n_sets = len(sets)
    n_deps = len(deps)

    def body(*refs):
        refs = refs[n_deps:]
        src, land = refs[:n], refs[first_land:first_land + n]
        sems = refs[len(arrays):len(arrays) + 2 * n_sets]
        token = refs[-1]
        x, y, c = _position()
        for s, (members, exchange) in enumerate(zip(sets, exchanges)):
            for slot, (k, rows) in enumerate(members):
                for j, peer in enumerate(exchange.peers(x, y, c)):
                    at = exchange.n_peers * slot + j
                    pltpu.make_async_remote_copy(
                        src_ref=_shard_rows(src[k], exchange.sent(x, y, c, peer), rows),
                        dst_ref=_shard_rows(land[k], exchange.lands_at(x, y, c), rows),
                        send_sem=sems[2 * s].at[at], recv_sem=sems[2 * s + 1].at[at],
                        device_id=peer, device_id_type=MESH_ID).start()
        token[...] = jnp.zeros_like(token)

    sem_shapes = []
    for members, exchange in zip(sets, exchanges):
        sem_shapes += [pltpu.SemaphoreType.DMA((exchange.n_peers * len(members),))] * 2
    out = pl.pallas_call(
        body, name=name,
        out_shape=sem_shapes + _hbm_like(arrays) + [TOKEN],
        in_specs=[ANY] * n_deps + [HBM] * len(arrays),
        out_specs=[SEM] * (2 * n_sets) + [HBM] * len(arrays) + [pl.BlockSpec(memory_space=pltpu.VMEM)],
        input_output_aliases={n_deps + i: 2 * n_sets + i for i in range(len(arrays))},
        compiler_params=pltpu.CompilerParams(has_side_effects=DATAFLOW),
    )(*deps, *[_in_hbm(a) for a in arrays])
    sems = [(out[2 * s], out[2 * s + 1]) for s in range(n_sets)]
    thru = list(out[2 * n_sets:2 * n_sets + len(arrays)])
    return sems, (thru[:first_land] if sources is not None else None), thru[first_land:], out[-1]


def _ici_copies_wait(sems, members, sources, landings, exchange, after, name):
    n = len(landings)
    arrays = (list(sources) if sources is not None else []) + list(landings)
    first_land = len(arrays) - n

    def body(*refs):
        src, land = refs[:n], refs[first_land:first_land + n]
        send_sems, recv_sems = refs[len(arrays)], refs[len(arrays) + 1]
        x, y, c = _position()
        for slot, rows in enumerate(members):
            for j, peer in enumerate(exchange.peers(x, y, c)):
                at = exchange.n_peers * slot + j
                cp = pltpu.make_async_remote_copy(
                    src_ref=_shard_rows(src[slot], exchange.sent(x, y, c, peer), rows),
                    dst_ref=_shard_rows(land[slot], exchange.arrives_at(peer), rows),
                    send_sem=send_sems.at[at], recv_sem=recv_sems.at[at], device_id=peer, device_id_type=MESH_ID)
                cp.wait_send()
                cp.wait_recv()

    out = pl.pallas_call(
        body, name=name, out_shape=_hbm_like(arrays),
        in_specs=[HBM] * len(arrays) + [SEM, SEM] + [ANY] * len(after), out_specs=[HBM] * len(arrays),
        input_output_aliases={i: i for i in range(len(arrays))},
        compiler_params=pltpu.CompilerParams(has_side_effects=DATAFLOW),
    )(*arrays, sems[0], sems[1], *after)
    return list(out[first_land:])


def _swap_copies(land, rows, which, send_sems, recv_sems):
    x, y, c = _position()
    chips = [([(x, y)] + _other_chips(x, y))[j] for j in which]
    sends, recvs = [], []
    for k in range(len(land)):
        for j, chip in enumerate(chips):
            for copies, core in ((sends, c), (recvs, 1 - c)):
                block = _shard_rows(land[k], _device_index(chip, core), rows[k])
                copies.append(pltpu.make_async_remote_copy(
                    src_ref=block, dst_ref=block, send_sem=send_sems.at[k, j], recv_sem=recv_sems.at[k, j],
                    device_id=(x, y, 1 - c), device_id_type=MESH_ID))
    return sends, recvs


def _d2d_gather(buffers, rows, name, which=(0, 1, 2, 3), deps=()):
    n = len(buffers)
    n_deps = len(deps)

    def body(*refs):
        sends, recvs = _swap_copies(refs[n_deps + n:n_deps + 2 * n], rows, which, *refs[n_deps + 2 * n:])
        for cp in sends:
            cp.start()
        for cp in recvs:
            cp.wait_recv()
        for cp in sends:
            cp.wait_send()

    return pl.pallas_call(
        body, name=name, out_shape=[jax.ShapeDtypeStruct(a.shape, a.dtype) for a in buffers],
        in_specs=[ANY] * (n_deps + n), out_specs=[ANY] * n, input_output_aliases={n_deps + i: i for i in range(n)},
        scratch_shapes=[pltpu.SemaphoreType.DMA((n, len(which))), pltpu.SemaphoreType.DMA((n, len(which)))],
    )(*deps, *buffers)


def _call_with_swap(body, args, swap, prefetch=(), **kw):
    buffers, rows = swap
    n, n_pre, n_in, n_out = len(buffers), len(prefetch), len(args), len(kw["out_shape"])
    n_scratch = len(kw["scratch_shapes"])
    grid = kw["grid"]
    which = (0, 1, 2, 3)

    def at_step(last):
        hit = [pl.program_id(a) == (extent - 1 if last else 0) for a, extent in enumerate(grid)]
        return functools.reduce(jnp.logical_and, hit)

    def hosted(*refs):
        pre, ins, refs = refs[:n_pre], refs[n_pre:n_pre + n_in], refs[n_pre + n_in + n:]
        outs, land, scratch = refs[:n_out], refs[n_out:n_out + n], refs[n_out + n:n_out + n + n_scratch]
        sends, recvs = _swap_copies(land, rows, which, *refs[n_out + n + n_scratch:])

        @pl.when(at_step(False))
        def _():
            for cp in sends:
                cp.start()

        body(*pre, *ins, *outs, *scratch)

        @pl.when(at_step(True))
        def _():
            for cp in recvs:
                cp.wait_recv()
            for cp in sends:
                cp.wait_send()

    sem_shape = pltpu.SemaphoreType.DMA((n, len(which)))
    aliases = {**kw.get("input_output_aliases", {}), **{n_in + i: n_out + i for i in range(n)}}
    out = pl.pallas_call(
        hosted,
        grid_spec=pltpu.PrefetchScalarGridSpec(
            num_scalar_prefetch=n_pre, grid=grid, in_specs=kw["in_specs"] + [ANY] * n, out_specs=kw["out_specs"] + [ANY] * n,
            scratch_shapes=kw["scratch_shapes"] + [sem_shape, sem_shape]),
        out_shape=kw["out_shape"] + [jax.ShapeDtypeStruct(a.shape, a.dtype) for a in buffers],
        input_output_aliases={n_pre + i: o for i, o in aliases.items()},
        compiler_params=kw["compiler_params"], name=kw["name"])(*prefetch, *args, *buffers)
    return out[:n_out], out[n_out:]


def _rs_pair(grads, name):
    n = len(grads)
    rows = [g.shape[0] // N_DEV for g in grads]

    def body(*refs):
        ins, outs = refs[:n], refs[n:2 * n]
        send_sems, recv_sems = refs[2 * n:]
        x, y, c = _position()
        copies = []
        for k in range(n):
            for q in range(4):
                copies.append(pltpu.make_async_remote_copy(
                    src_ref=_shard_rows(ins[k], 2 * q + 1 - c, rows[k]), dst_ref=_shard_rows(outs[k], q, rows[k]),
                    send_sem=send_sems.at[k, q], recv_sem=recv_sems.at[k, q], device_id=(x, y, 1 - c),
                    device_id_type=MESH_ID))
        for cp in copies:
            cp.start()
        for cp in copies:
            cp.wait()

    return pl.pallas_call(
        body, out_shape=[jax.ShapeDtypeStruct((4 * r, g.shape[1]), g.dtype) for g, r in zip(grads, rows)],
        in_specs=[ANY] * n, out_specs=[ANY] * n,
        scratch_shapes=[pltpu.SemaphoreType.DMA((n, 4)), pltpu.SemaphoreType.DMA((n, 4))],
        name=name)(*grads)


def _wgrad_pair(lhs, rhs, name, *, lhs_is_transposed, deps=()):
    t = rhs.shape[0]
    n = lhs.shape[0] if lhs_is_transposed else lhs.shape[1]
    r = n // N_DEV
    n_chips = N_DEV // 2
    per = 1 if (2 * r) % BLK == 0 else 2
    steps = n_chips // per

    def body(l_ref, r_ref, kept_ref, recv_ref, res, send_sems, recv_sems):
        q = pl.program_id(0)
        slot = q % 2
        x, y, c = _position()

        def send(step, buf, i):
            return pltpu.make_async_remote_copy(
                src_ref=res.at[buf, pl.ds(pl.multiple_of((2 * i + 1 - c) * r, 16), r), :],
                dst_ref=_shard_rows(recv_ref, step * per + i, r),
                send_sem=send_sems.at[buf, i], recv_sem=recv_sems.at[step * per + i],
                device_id=(x, y, 1 - c), device_id_type=MESH_ID)

        @pl.when(q >= 2)
        def _():
            for i in range(per):
                send(q - 2, slot, i).wait_send()

        if lhs_is_transposed:
            res[slot] = _dot(l_ref[...], r_ref[...]).astype(BF)
        else:
            res[slot] = _dot_tn(l_ref[...], r_ref[...]).astype(BF)
        for i in range(per):
            kept_ref[i * r:(i + 1) * r, :] = res[slot, pl.ds(pl.multiple_of((2 * i + c) * r, 16), r), :]
            send(q, slot, i).start()

        @pl.when(q == steps - 1)
        def _():
            for i in range(per):
                if steps > 1:
                    send(q - 1, 1 - slot, i).wait_send()
                send(q, slot, i).wait_send()
            for chip in range(n_chips):
                send(chip // per, 0, chip % per).wait_recv()

    width = 2 * r * per
    lhs_spec = pl.BlockSpec((width, t), lambda q: (q, 0)) if lhs_is_transposed else pl.BlockSpec((t, width), lambda q: (0, q))
    return _call(
        body, deps, (lhs, rhs), grid=(steps,),
        in_specs=[lhs_spec, _resident((t, D))],
        out_specs=[pl.BlockSpec((per * r, D), lambda q: (q, 0)), ANY],
        out_shape=[jax.ShapeDtypeStruct((n // 2, D), BF)] * 2,
        scratch_shapes=[pltpu.VMEM((2, width, D), BF), pltpu.SemaphoreType.DMA((2, per)),
                        pltpu.SemaphoreType.DMA((n_chips,))],
        compiler_params=_params(1), name=name)


def _wgrad_pair_sum(lhs, rhs, place, name, *, lhs_is_transposed, deps=()):
    t = rhs.shape[0]
    n = lhs.shape[0] if lhs_is_transposed else lhs.shape[1]
    r = n // N_DEV
    n_chips = N_DEV // 2
    per = 1 if (2 * r) % BLK == 0 else 2
    steps = n_chips // per
    n_deps = len(deps)

    def body(place_ref, *refs):
        l_ref, r_ref, part_ref, land_ref, res, inbox, send_sems, recv_sems = refs[n_deps:]
        q = pl.program_id(0)
        slot = q % 2
        x, y, c = _position()

        def send(step, buf, i):
            return pltpu.make_async_remote_copy(
                src_ref=res.at[buf, pl.ds(pl.multiple_of((2 * i + 1 - c) * r, 16), r), :], dst_ref=inbox.at[step * per + i],
                send_sem=send_sems.at[buf, i], recv_sem=recv_sems.at[step * per + i],
                device_id=(x, y, 1 - c), device_id_type=MESH_ID)

        @pl.when(q < steps)
        def _():
            @pl.when(q >= 2)
            def _():
                for i in range(per):
                    send(q - 2, slot, i).wait_send()

            if lhs_is_transposed:
                res[slot] = _dot(l_ref[...], r_ref[...]).astype(BF)
            else:
                res[slot] = _dot_tn(l_ref[...], r_ref[...]).astype(BF)
            for i in range(per):
                send(q, slot, i).start()

        @pl.when(q >= 1)
        def _():
            for i in range(per):
                chip = (q - 1) * per + i
                send(q - 1, 1 - slot, i).wait_recv()
                kept = res[1 - slot, pl.ds(pl.multiple_of((2 * i + c) * r, 16), r), :]
                total = (kept.astype(F32) + inbox[chip].astype(F32)).astype(BF)
                part_ref[i * r:(i + 1) * r, :] = total

                @pl.when(chip == place_ref[1])
                def _():
                    land_ref[...] = total

        @pl.when(q == steps)
        def _():
            for i in range(per):
                if steps > 1:
                    send(q - 2, slot, i).wait_send()
                send(q - 1, 1 - slot, i).wait_send()

    width = 2 * r * per
    last = steps - 1
    if lhs_is_transposed:
        lhs_spec = pl.BlockSpec((width, t), lambda q, p: (jnp.minimum(q, last), 0))
    else:
        lhs_spec = pl.BlockSpec((t, width), lambda q, p: (0, jnp.minimum(q, last)))
    return pl.pallas_call(
        body,
        grid_spec=pltpu.PrefetchScalarGridSpec(
            num_scalar_prefetch=1, grid=(steps + 1,),
            in_specs=[ANY] * n_deps + [lhs_spec, pl.BlockSpec((t, D), lambda q, p: (0, 0), pipeline_mode=pl.Buffered(1))],
            out_specs=[pl.BlockSpec((per * r, D), lambda q, p: (jnp.maximum(q - 1, 0), 0)),
                       pl.BlockSpec((r, D), lambda q, p: (p[1], 0))],
            scratch_shapes=[pltpu.VMEM((2, width, D), BF), pltpu.VMEM((n_chips, r, D), BF),
                            pltpu.SemaphoreType.DMA((2, per)), pltpu.SemaphoreType.DMA((n_chips,))]),
        out_shape=[jax.ShapeDtypeStruct((n // 2, D), BF)] * 2,
        compiler_params=_params(1), name=name)(place, *deps, lhs, rhs)


def _wgrad_pair_sum_many(items, place, name, deps=()):
    m = len(items)
    t = items[0][1].shape[0]
    n = items[0][0].shape[0] if items[0][2] else items[0][0].shape[1]
    r = n // N_DEV
    assert (2 * r) % BLK == 0 and r <= WGRAD_SUM_MAX_ROWS
    steps = N_DEV // 2
    chunks = m * steps
    n_deps = len(deps)

    def body(place_ref, *refs):
        refs = refs[n_deps:]
        l_refs, r_first, r_later = refs[:m], refs[m], refs[m + 1:2 * m]
        parts, lands = refs[2 * m:3 * m], refs[3 * m:4 * m]
        res, inbox, r_scr, send_sems, recv_sems, fetch_sems = refs[4 * m:]
        g = pl.program_id(0)
        slot = g % 2
        x, y, c = _position()

        def send(chunk, buf):
            return pltpu.make_async_remote_copy(
                src_ref=res.at[buf, pl.ds(pl.multiple_of((1 - c) * r, 16), r), :], dst_ref=inbox.at[chunk],
                send_sem=send_sems.at[buf], recv_sem=recv_sems.at[chunk], device_id=(x, y, 1 - c), device_id_type=MESH_ID)

        def fetch(k):
            return pltpu.make_async_copy(r_later[k - 1], r_scr.at[k - 1], fetch_sems.at[k - 1])

        @pl.when(g == 0)
        def _():
            for k in range(1, m):
                fetch(k).start()

        @pl.when(g < chunks)
        def _():
            @pl.when(g >= 2)
            def _():
                send(g - 2, slot).wait_send()

            for k, (_, _, transposed) in enumerate(items):
                @pl.when(g // steps == k)
                def _(k=k, transposed=transposed):
                    if k > 0:
                        @pl.when(g == k * steps)
                        def _():
                            fetch(k).wait()
                    rhs = r_first[...] if k == 0 else r_scr[k - 1]
                    res[slot] = (_dot(l_refs[k][...], rhs) if transposed else _dot_tn(l_refs[k][...], rhs)).astype(BF)

            send(g, slot).start()

        @pl.when(g >= 1)
        def _():
            chunk = g - 1
            send(chunk, 1 - slot).wait_recv()
            kept = res[1 - slot, pl.ds(pl.multiple_of(c * r, 16), r), :]
            total = (kept.astype(F32) + inbox[chunk].astype(F32)).astype(BF)
            for k in range(m):
                @pl.when(chunk // steps == k)
                def _(k=k):
                    parts[k][...] = total

                    @pl.when(chunk % steps == place_ref[1])
                    def _():
                        lands[k][...] = total

        @pl.when(g == chunks)
        def _():
            send(g - 2, slot).wait_send()
            send(g - 1, 1 - slot).wait_send()

    def own_steps(k):
        return lambda g: jnp.clip(g - k * steps, 0, steps - 1)

    lhs_specs = []
    for k, (lhs, _, transposed) in enumerate(items):
        at = own_steps(k)
        lhs_specs.append(pl.BlockSpec((2 * r, t), lambda g, p, at=at: (at(g), 0)) if transposed
                         else pl.BlockSpec((t, 2 * r), lambda g, p, at=at: (0, at(g))))
    out = pl.pallas_call(
        body,
        grid_spec=pltpu.PrefetchScalarGridSpec(
            num_scalar_prefetch=1, grid=(chunks + 1,),
            in_specs=[ANY] * n_deps + lhs_specs
            + [pl.BlockSpec((t, D), lambda g, p: (0, 0), pipeline_mode=pl.Buffered(1))] + [ANY] * (m - 1),
            out_specs=[pl.BlockSpec((r, D), lambda g, p, at=own_steps(k): (at(g - 1), 0)) for k in range(m)]
            + [pl.BlockSpec((r, D), lambda g, p: (p[1], 0))] * m,
            scratch_shapes=[pltpu.VMEM((2, 2 * r, D), BF), pltpu.VMEM((chunks, r, D), BF), pltpu.VMEM((m - 1, t, D), BF),
                            pltpu.SemaphoreType.DMA((2,)), pltpu.SemaphoreType.DMA((chunks,)),
                            pltpu.SemaphoreType.DMA((m - 1,))]),
        out_shape=[jax.ShapeDtypeStruct((n // 2, D), BF)] * (2 * m),
        compiler_params=_params(1), name=name)(place, *deps, *[i[0] for i in items], *[i[1] for i in items])
    return [(out[k], out[m + k]) for k in range(m)]


def _pair_add(grad, received, place, name, kept_only=False):
    r = received.shape[0] // 4
    parity = 0 if kept_only else 1

    def body(place_ref, g_ref, r_ref, o_ref, land_ref):
        total = (g_ref[...].astype(F32) + r_ref[...].astype(F32)).astype(BF)
        o_ref[...] = total

        @pl.when(pl.program_id(0) == place_ref[1])
        def _():
            land_ref[...] = total

    return pl.pallas_call(
        body,
        grid_spec=pltpu.PrefetchScalarGridSpec(
            num_scalar_prefetch=1, grid=(4,),
            in_specs=[pl.BlockSpec((r, D), lambda q, p: ((1 + parity) * q + parity * p[0], 0)),
                      pl.BlockSpec((r, D), lambda q, p: (q, 0))],
            out_specs=[pl.BlockSpec((r, D), lambda q, p: (q, 0)), pl.BlockSpec((r, D), lambda q, p: (p[1], 0))]),
        out_shape=[jax.ShapeDtypeStruct(received.shape, BF)] * 2,
        compiler_params=_params(1), name=name)(place, grad, received)


def _sum_blocks(gathered, rows):
    def body(b_ref, o_ref):
        acc = b_ref[0:rows, :]
        for d in range(1, N_DEV):
            acc = acc + b_ref[d * rows:(d + 1) * rows, :]
        o_ref[...] = acc

    return pl.pallas_call(body, out_shape=jax.ShapeDtypeStruct((rows, D), F32), name="small_sum")(gathered)


def _adamw_math(w, g, m, v):
    m = ADAM_B1 * m + (1.0 - ADAM_B1) * g
    v = ADAM_B2 * v + (1.0 - ADAM_B2) * (g * g)
    m_hat = m / (1.0 - ADAM_B1 ** ADAM_STEP)
    v_hat = v / (1.0 - ADAM_B2 ** ADAM_STEP)
    delta = -ADAM_LR * (m_hat / (jnp.sqrt(v_hat) + ADAM_EPS) + ADAM_WD * w)
    return delta, m, v


def _sum_partials(blocks):
    g = blocks[0].astype(F32)
    for blk in blocks[1:]:
        g = g + blk.astype(F32)
    return g


ADAMW_MAX_ROWS = 352


def _reduce_adamw(items, name):
    n = len(items)
    per = -(-max(w.shape[0] for _, w, _, _ in items) // ADAMW_MAX_ROWS)

    def body(*refs):
        for k in range(n):
            r0, r1, r2, r3, w_ref, m_ref, v_ref = refs[7 * k:7 * k + 7]
            g_ref, d_ref, nm_ref, nv_ref = refs[7 * n + 4 * k:7 * n + 4 * k + 4]
            g = _sum_partials([r0[...], r1[...], r2[...], r3[...]])
            g_ref[...] = g
            d_ref[...], nm_ref[...], nv_ref[...] = _adamw_math(w_ref[...], g, m_ref[...], v_ref[...])

    in_specs, out_specs, out_shape, args = [], [], [], []
    for landed, w, m, v in items:
        tr = w.shape[0] // per
        assert tr * per == w.shape[0] and tr % 16 == 0
        tile = _row_tile(tr, D)
        in_specs += [pl.BlockSpec((tr, D), lambda i, q=q: (q * per + i, 0)) for q in range(4)] + [tile] * 3
        out_specs += [tile] * 4
        out_shape += [jax.ShapeDtypeStruct(w.shape, F32)] * 4
        args += [landed] * 4 + [w, m, v]
    out = pl.pallas_call(body, grid=(per,), in_specs=in_specs, out_specs=out_specs, out_shape=out_shape,
                         compiler_params=_params(1), name=name)(*args)
    return [out[4 * k:4 * k + 4] for k in range(n)]


def _adamw_small(w, g, m, v, name):
    def body(w_ref, g_ref, m_ref, v_ref, d_ref, nm_ref, nv_ref):
        d_ref[...], nm_ref[...], nv_ref[...] = _adamw_math(w_ref[...], g_ref[...], m_ref[...], v_ref[...])

    return pl.pallas_call(body, out_shape=[jax.ShapeDtypeStruct(w.shape, F32)] * 3, name=name)(w, g, m, v)


WEIGHTS = ("ffn1_norm", "ffn1_w_in", "ffn1_w_out", "mix_norm", "w_in", "conv_dw_kernel", "conv_dw_bias", "conv_ln_g",
           "conv_ln_b", "conv_w_proj", "q_norm", "k_norm", "attn_sinks", "rel_bias", "attn_w_o", "w_out", "ffn2_norm",
           "ffn2_w_in", "ffn2_w_out")
MATRICES = ("ffn1_w_in", "ffn1_w_out", "w_in", "conv_w_proj", "attn_w_o", "w_out", "ffn2_w_in", "ffn2_w_out")
COLUMN_SHARDED = ("ffn1_w_in", "w_in", "ffn2_w_in")
ROW_VECTORS = ("ffn1_norm", "mix_norm", "conv_dw_bias", "conv_ln_g", "conv_ln_b", "ffn2_norm")
PACKED = (("q_norm", HD), ("k_norm", HD), ("attn_sinks", NQ), ("rel_bias", NBUCKET * NQ))
GATHER = _Exchange(gather=True)
GATHER_ALL = _Exchange(gather=True, all_cores=True)
SCATTER = _Exchange(gather=False)
FIRST = "ffn1_w_in"
GATHER_STAGES = ("ffn1_out", "mix_proj", "mix_merge", "ffn2")
STAGE_GATHER = {"ffn1_out": GATHER, "mix_proj": GATHER, "mix_merge": GATHER, "ffn2": GATHER_ALL}
STAGE_MEMBERS = {"ffn1_out": ("ffn1_w_out",),
                 "mix_proj": ("w_in", "taps"), "mix_merge": ("conv_w_proj", "attn_w_o", "w_out"),
                 "ffn2": ("ffn2_w_in", "ffn2_w_out")}
ROW_PACKED = len(ROW_VECTORS)
ROW_LOSS = ROW_PACKED + 1
ROW_TAPS = 8
PAYLOAD_ROWS = 48


def _pack_small(values, last_row):
    packed = jnp.concatenate([values[k].reshape(-1) for k, _ in PACKED])
    packed = jnp.pad(packed, (0, D - packed.shape[0])).reshape(1, D)
    return jnp.concatenate([values[k].reshape(1, D) for k in ROW_VECTORS] + [packed, last_row], axis=0)


def _unpack_small(rows):
    out = {k: rows[i] for i, k in enumerate(ROW_VECTORS)}
    at = 0
    for k, size in PACKED:
        out[k] = rows[ROW_PACKED, at:at + size]
        at += size
    out["rel_bias"] = out["rel_bias"].reshape(NBUCKET, NQ)
    return out


def kernel(x, ffn1_norm, ffn1_w_in, ffn1_w_out, mix_norm, w_in, conv_dw_kernel, conv_dw_bias, conv_ln_g, conv_ln_b, conv_w_proj, q_norm, k_norm, attn_sinks, rel_bias, attn_w_o, w_out, ffn2_norm, ffn2_w_in, ffn2_w_out, loss_target, m_ffn1_norm, m_ffn1_w_in, m_ffn1_w_out, m_mix_norm, m_w_in, m_conv_dw_kernel, m_conv_dw_bias, m_conv_ln_g, m_conv_ln_b, m_conv_w_proj, m_q_norm, m_k_norm, m_attn_sinks, m_rel_bias, m_attn_w_o, m_w_out, m_ffn2_norm, m_ffn2_w_in, m_ffn2_w_out, v_ffn1_norm, v_ffn1_w_in, v_ffn1_w_out, v_mix_norm, v_w_in, v_conv_dw_kernel, v_conv_dw_bias, v_conv_ln_g, v_conv_ln_b, v_conv_w_proj, v_q_norm, v_k_norm, v_attn_sinks, v_rel_bias, v_attn_w_o, v_w_out, v_ffn2_norm, v_ffn2_w_in, v_ffn2_w_out):
    w = dict(ffn1_norm=ffn1_norm, ffn1_w_in=ffn1_w_in, ffn1_w_out=ffn1_w_out, mix_norm=mix_norm, w_in=w_in,
             conv_dw_kernel=conv_dw_kernel, conv_dw_bias=conv_dw_bias, conv_ln_g=conv_ln_g, conv_ln_b=conv_ln_b,
             conv_w_proj=conv_w_proj, q_norm=q_norm, k_norm=k_norm, attn_sinks=attn_sinks, rel_bias=rel_bias,
             attn_w_o=attn_w_o, w_out=w_out, ffn2_norm=ffn2_norm, ffn2_w_in=ffn2_w_in, ffn2_w_out=ffn2_w_out)
    m = dict(ffn1_norm=m_ffn1_norm, ffn1_w_in=m_ffn1_w_in, ffn1_w_out=m_ffn1_w_out, mix_norm=m_mix_norm, w_in=m_w_in,
             conv_dw_kernel=m_conv_dw_kernel, conv_dw_bias=m_conv_dw_bias, conv_ln_g=m_conv_ln_g, conv_ln_b=m_conv_ln_b,
             conv_w_proj=m_conv_w_proj, q_norm=m_q_norm, k_norm=m_k_norm, attn_sinks=m_attn_sinks, rel_bias=m_rel_bias,
             attn_w_o=m_attn_w_o, w_out=m_w_out, ffn2_norm=m_ffn2_norm, ffn2_w_in=m_ffn2_w_in, ffn2_w_out=m_ffn2_w_out)
    v = dict(ffn1_norm=v_ffn1_norm, ffn1_w_in=v_ffn1_w_in, ffn1_w_out=v_ffn1_w_out, mix_norm=v_mix_norm, w_in=v_w_in,
             conv_dw_kernel=v_conv_dw_kernel, conv_dw_bias=v_conv_dw_bias, conv_ln_g=v_conv_ln_g, conv_ln_b=v_conv_ln_b,
             conv_w_proj=v_conv_w_proj, q_norm=v_q_norm, k_norm=v_k_norm, attn_sinks=v_attn_sinks, rel_bias=v_rel_bias,
             attn_w_o=v_attn_w_o, w_out=v_w_out, ffn2_norm=v_ffn2_norm, ffn2_w_in=v_ffn2_w_in, ffn2_w_out=v_ffn2_w_out)
    px, py, pc = _position()
    me = 4 * px + 2 * py + pc
    place = jnp.stack([pc, 2 * px + py]).astype(jnp.int32)

    rows_of = lambda k, a: a.T if k in COLUMN_SHARDED else a
    me1 = me.astype(jnp.int32).reshape(1)
    rest = tuple(k for k in MATRICES if k != FIRST)
    shard_rows = dict({k: rows_of(k, w[k]).shape[0] for k in MATRICES}, taps=CWP)
    sems_first, _, thru_first, token = _ici_copies_start(
        [[(0, shard_rows[FIRST])]], None, _prep([rows_of(FIRST, w[FIRST])], None, me1, "prep_first"), [GATHER],
        "gather_start_first")
    buffers = dict(zip(rest + ("taps",), _prep([rows_of(k, w[k]) for k in rest], conv_dw_kernel, me1, "prep", deps=[token])))
    landings, sets = [], []
    for stage in GATHER_STAGES:
        sets.append([(len(landings) + i, shard_rows[k]) for i, k in enumerate(STAGE_MEMBERS[stage])])
        landings += list(STAGE_MEMBERS[stage])
    sems, _, land_thru, started = _ici_copies_start(sets, None, [buffers[k] for k in landings],
                                                    [STAGE_GATHER[s] for s in GATHER_STAGES], "gather_start")

    packed = [_pack_small(a, jnp.zeros((1, D), F32)) for a in (w, m, v)]

    def ffn1_up(x, g, after):
        chips = jnp.stack([_chip_index(chip) for chip in [(px, py)] + _other_chips(px, py)]).astype(jnp.int32)
        rows = [shard_rows[FIRST]]
        mine = _d2d_gather(thru_first, rows, "gather_d2d_first_mine", which=(0,), deps=[started])
        n, u = _ffn_up_blocks(x, g, None, mine[0], chips[:1], None, "ffn1_up_mine")
        landed = _ici_copies_wait(sems_first[0], rows, None, mine, GATHER, [u, *after, *packed], "gather_wait_first")
        w_in_t, = _d2d_gather(landed, rows, "gather_d2d_first", which=(1, 2, 3))
        n, u = _ffn_up_blocks(None, None, n, w_in_t, chips[1:3], u, "ffn1_up_next")
        (n, u), w1 = weights_of("ffn1_out", (u,), during=functools.partial(
            _ffn_up_blocks, None, None, n, w_in_t, chips[3:], u, "ffn1_up"))
        return n, u, dict(w1, ffn1_w_in=w_in_t)

    def weights_of(stage, after, during=None):
        s = GATHER_STAGES.index(stage)
        rows = [r for _, r in sets[s]]
        landed = _ici_copies_wait(sems[s], rows, None, [land_thru[k] for k, _ in sets[s]], STAGE_GATHER[stage],
                                  list(after), "gather_wait_" + stage)
        if during is not None:
            results, landed = during(swap=(landed, rows))
        elif not STAGE_GATHER[stage].all_cores:
            landed = _d2d_gather(landed, rows, "gather_d2d_" + stage)
        out = dict(zip(STAGE_MEMBERS[stage], landed))
        if "taps" in out:
            taps = out.pop("taps")
            out["conv_dw_kernel"] = jnp.transpose(taps.reshape(N_DEV, CWP, BLK), (1, 0, 2)).reshape(CWP, D)[:CW]
        return out if during is None else (results, out)

    in_flight = []

    def wgrad(lhs, rhs, name, lhs_is_transposed, deps=()):
        rows = (lhs.shape[0] if lhs_is_transposed else lhs.shape[1]) // N_DEV
        if rows <= WGRAD_SUM_MAX_ROWS:
            return ("summed",) + tuple(_wgrad_pair_sum(lhs, rhs, place, name, lhs_is_transposed=lhs_is_transposed, deps=deps))
        return ("paired",) + tuple(_wgrad_pair(lhs, rhs, name, lhs_is_transposed=lhs_is_transposed, deps=deps))

    def wgrads(items, name):
        return [("summed",) + pair for pair in _wgrad_pair_sum_many(items, place, name)]

    def grads_done(stage, grads):
        names = list(grads)
        added = []
        for k in names:
            if not isinstance(grads[k], tuple):
                received, = _rs_pair([grads[k]], "rs_pair_" + k)
                added.append(_pair_add(grads[k], received, place, "pair_add_" + k))
            elif grads[k][0] == "paired":
                added.append(_pair_add(grads[k][1], grads[k][2], place, "pair_add_" + k, kept_only=True))
            else:
                added.append(grads[k][1:])
        partials = [p for p, _ in added]
        members = [(i, p.shape[0] // 4) for i, p in enumerate(partials)]
        sem, p_thru, l_thru, token = _ici_copies_start([members], partials, [l for _, l in added], [SCATTER],
                                                       "scatter_start_" + stage)
        in_flight.append((stage, names, sem[0], p_thru, l_thru, token))
        return [token]

    small = []

    def small_done(gv, sq):
        payload = jnp.concatenate([_pack_small(gv, sq), jnp.pad(gv["conv_dw_kernel"], ((0, PAYLOAD_ROWS - ROW_TAPS - CW), (0, 0)))],
                                  axis=0)
        mine = lax.dynamic_update_slice_in_dim(lax.empty((N_DEV * PAYLOAD_ROWS, D), F32), payload, me * PAYLOAD_ROWS, axis=0)
        sems, _, thru, token = _ici_copies_start([[(0, PAYLOAD_ROWS)]], None, [mine], [GATHER_ALL], "small_start")
        small.append((sems[0], thru))
        return [token]

    vec = {k: w[k] for k in WEIGHTS if k not in MATRICES and k != "conv_dw_kernel"}
    dx0 = _local_step(x[0], loss_target[0], vec, ffn1_up, weights_of, wgrad, wgrads, grads_done, small_done)
    gathered, = _ici_copies_wait(small[0][0], [PAYLOAD_ROWS], None, small[0][1], GATHER_ALL, [in_flight[-1][-1]], "small_wait")
    total = _sum_blocks(gathered, PAYLOAD_ROWS)
    loss = (0.5 / D) * jnp.sum(total[ROW_LOSS])

    grads, delta, new_m, new_v = {}, {}, {}, {}
    after, pending = [total], []
    for stage, names, sem, p_thru, l_thru, _ in in_flight:
        landed = _ici_copies_wait(sem, [p.shape[0] // 4 for p in p_thru], p_thru, l_thru, SCATTER, after,
                                  "scatter_wait_" + stage)
        pending += zip(names, landed)
        after = list(landed)
        if stage == in_flight[-2][0]:
            continue
        outs = _reduce_adamw([(buf, rows_of(k, w[k]), rows_of(k, m[k]), rows_of(k, v[k])) for k, buf in pending],
                             "adamw_" + stage)
        for (k, _), out in zip(pending, outs):
            grads[k], delta[k], new_m[k], new_v[k] = [rows_of(k, a) for a in out]
        after, pending = [out[1] for out in outs], []
    d8, m8, v8 = _adamw_small(packed[0], total[:ROW_TAPS], packed[1], packed[2], "adamw_small")
    grads.update(_unpack_small(total[:ROW_TAPS]))
    delta.update(_unpack_small(d8))
    new_m.update(_unpack_small(m8))
    new_v.update(_unpack_small(v8))
    k = "conv_dw_kernel"
    grads[k] = lax.dynamic_slice_in_dim(total[ROW_TAPS:ROW_TAPS + CW], me * BLK, BLK, axis=1)
    delta[k], new_m[k], new_v[k] = _adamw_small(w[k], grads[k], m[k], v[k], "adamw_taps")

    return (loss, dx0[None], *[grads[k] for k in WEIGHTS], *[delta[k] for k in WEIGHTS],
            *[new_m[k] for k in WEIGHTS], *[new_v[k] for k in WEIGHTS])
```

```python
import functools
import math

import numpy as np
import jax
import jax.numpy as jnp
from jax import lax
from jax.experimental import pallas as pl
from jax.experimental.pallas import tpu as pltpu

F32 = jnp.float32
BF = jnp.bfloat16

D = 1024
F = 2816
INW = 5632
CW = 31
CWP = 32
HD = 64
NQ = 16
NKV = 4
GRP = NQ // NKV
BLK = 128
NBUCKET = 32
EPS = 1e-6
NEG = float(jnp.finfo(jnp.float32).min)
QK_SCALE = 1.0 / math.sqrt(HD)
R_CONV = (0, 2048)
R_QKV = (2048, 3584)
R_Q = (2048, 3072)
R_KV = (3072, 3584)
R_GATE = (3584, 5632)

N_DEV = 8
VMEM_LIMIT_V7X = 56 * 1024 * 1024
ROW_TILE = 256
ROW_TILE_WIDE = 512
ROW_TILE_BLOCK = 1024
WGRAD_SUM_MAX_ROWS = 352

ADAM_LR = 0.001
ADAM_B1 = 0.9
ADAM_B2 = 0.999
ADAM_EPS = 1e-08
ADAM_WD = 0.01
ADAM_STEP = 10

NT_DIMS = (((1,), (1,)), ((), ()))
TN_DIMS = (((0,), (0,)), ((), ()))


def _dot(a, b):
    return jnp.dot(a, b, preferred_element_type=F32)


def _dot_nt(a, b):
    return lax.dot_general(a, b, NT_DIMS, preferred_element_type=F32)


def _dot_tn(a, b):
    return lax.dot_general(a, b, TN_DIMS, preferred_element_type=F32)


def _sig(x):
    return 0.5 * jnp.tanh(0.5 * x) + 0.5


ANY = pl.BlockSpec(memory_space=pl.ANY)


def _call(body, deps, args, **kw):
    n = len(deps)
    if n:
        kw["in_specs"] = [ANY] * n + list(kw["in_specs"])
        return pl.pallas_call(lambda *refs: body(*refs[n:]), **kw)(*deps, *args)
    return pl.pallas_call(body, **kw)(*args)


def _params(n_axes):
    return pltpu.CompilerParams(dimension_semantics=("arbitrary",) * n_axes, vmem_limit_bytes=VMEM_LIMIT_V7X)


def _resident(shape):
    zeros = (0,) * len(shape)
    return pl.BlockSpec(shape, lambda *_: zeros, pipeline_mode=pl.Buffered(1))


def _row_tile(rows, cols):
    return pl.BlockSpec((rows, cols), lambda i: (i, 0))


def _rms_stats(x):
    r = lax.rsqrt(jnp.mean(x * x, axis=-1, keepdims=True) + EPS)
    return r, x * r


def _rms_bwd(dn, x, g):
    r, xh = _rms_stats(x)
    dxh = dn * g
    dx = r * (dxh - xh * jnp.mean(dxh * xh, axis=-1, keepdims=True))
    return dx, jnp.sum(dn * xh, axis=0, keepdims=True)


def _ffn_last(x, target, g, w_in_t, w_out, name):
    t = x.shape[0]
    tm = min(ROW_TILE, t)

    def body(x_ref, t_ref, g_ref, w_ref, wo_ref, n_ref, du_ref, h_ref, dy_ref, dx_ref, sq_ref, dg_ref):
        @pl.when(pl.program_id(0) == 0)
        def _():
            sq_ref[...] = jnp.zeros_like(sq_ref)
            dg_ref[...] = jnp.zeros_like(dg_ref)

        x = x_ref[...]
        g = g_ref[...]
        r, xh = _rms_stats(x)
        n = (xh * g).astype(BF)
        n_ref[...] = n
        u = _dot_nt(n, w_ref[...])
        a = u[:, :F]
        b = u[:, F:]
        s = _sig(a)
        sa = a * s
        h = (sa * b).astype(BF)
        h_ref[...] = h
        err = x + 0.5 * _dot(h, wo_ref[...]) - t_ref[...]
        sq_ref[...] += jnp.sum(err * err, axis=0, keepdims=True)
        dxo = err * (1.0 / D)
        dy = (0.5 * dxo).astype(BF)
        dy_ref[...] = dy
        dh = _dot_nt(dy, wo_ref[...])
        du_ref[:, :F] = (dh * b * (s * (1.0 + a * (1.0 - s)))).astype(BF)
        du_ref[:, F:] = (dh * sa).astype(BF)
        dn = _dot(du_ref[...], w_ref[...])
        dxh = dn * g
        dx_ref[...] = dxo + r * (dxh - xh * jnp.mean(dxh * xh, axis=-1, keepdims=True))
        dg_ref[...] += jnp.sum(dn * xh, axis=0, keepdims=True)

    vec = pl.BlockSpec((1, D), lambda i: (0, 0))
    return pl.pallas_call(
        body, grid=(t // tm,),
        in_specs=[_row_tile(tm, D), _row_tile(tm, D), _resident((1, D)), _resident((INW, D)), _resident((F, D))],
        out_specs=[_row_tile(tm, D), _row_tile(tm, INW), _row_tile(tm, F), _row_tile(tm, D), _row_tile(tm, D), vec, vec],
        out_shape=[jax.ShapeDtypeStruct((t, D), BF), jax.ShapeDtypeStruct((t, INW), BF), jax.ShapeDtypeStruct((t, F), BF),
                   jax.ShapeDtypeStruct((t, D), BF), jax.ShapeDtypeStruct((t, D), F32), jax.ShapeDtypeStruct((1, D), F32),
                   jax.ShapeDtypeStruct((1, D), F32)],
        compiler_params=_params(1), name=name)(x, target, g, w_in_t, w_out)


def _ffn_up_blocks(x, g, n, w_in_t, order, u, name, deps=(), swap=None):
    t = (x if n is None else n).shape[0]
    tm = min(ROW_TILE_BLOCK, t)
    c = INW * 2 // N_DEV
    n_deps = len(deps)
    first = n is None
    assert not first or order.shape == (1,)

    def body(order_ref, *refs):
        refs = refs[n_deps:]
        if first:
            x_ref, g_ref, w_ref, n_ref, u_ref = refs
            nt = (_rms_stats(x_ref[...])[1] * g_ref[...]).astype(BF)
            n_ref[...] = nt
        else:
            n_ref, w_ref, _, u_ref = refs
            nt = n_ref[...]
        u_ref[...] = _dot_nt(nt, w_ref[...]).astype(BF)

    rows = pl.BlockSpec((tm, D), lambda k, i, o: (i, 0))
    block = pl.BlockSpec((c, D), lambda k, i, o: (o[k], 0))
    cols = pl.BlockSpec((tm, c), lambda k, i, o: (i, o[k]))
    u_shape = jax.ShapeDtypeStruct((t, INW), BF)
    if first:
        args, in_specs = (x, g, w_in_t), [rows, _resident((1, D)), block]
        out_specs, out_shape, aliases = [rows, cols], [jax.ShapeDtypeStruct((t, D), BF), u_shape], {}
    else:
        args, in_specs = (n, w_in_t, u), [rows, block, ANY]
        out_specs, out_shape, aliases = [cols], [u_shape], {1 + n_deps + 2: 0}
    grid = (order.shape[0], t // tm)
    if swap is not None:
        (out,), swapped = _call_with_swap(
            body, (*deps, *args), swap, prefetch=(order,), grid=grid, in_specs=[ANY] * n_deps + in_specs, out_specs=out_specs,
            out_shape=out_shape, scratch_shapes=[], input_output_aliases={n_deps + 2: 0}, compiler_params=_params(2), name=name)
        return (n, out), swapped
    out = pl.pallas_call(
        body,
        grid_spec=pltpu.PrefetchScalarGridSpec(num_scalar_prefetch=1, grid=grid, in_specs=[ANY] * n_deps + in_specs,
                                               out_specs=out_specs),
        out_shape=out_shape, input_output_aliases=aliases, compiler_params=_params(2), name=name)(order, *deps, *args)
    return tuple(out) if first else (n, out[0])


def _ffn_down(x, u, w_out, name, part=(0, 1), out=None, swap=None):
    t = x.shape[0]
    tm = min(ROW_TILE_WIDE, t)
    steps = t // tm // part[1]
    first = part[0] * steps
    others = [out] if out is not None else []

    def body(x_ref, u_ref, wo_ref, *rest):
        a = u_ref[:, :F].astype(F32)
        b = u_ref[:, F:].astype(F32)
        h = (a * _sig(a) * b).astype(BF)
        rest[-1][...] = x_ref[...] + 0.5 * _dot(h, wo_ref[...])

    tile = lambda cols: pl.BlockSpec((tm, cols), lambda i: (first + i, 0))
    kw = dict(grid=(steps,), in_specs=[tile(D), tile(INW), _resident((F, D))] + [ANY] * len(others), out_specs=[tile(D)],
              out_shape=[jax.ShapeDtypeStruct((t, D), F32)], scratch_shapes=[],
              input_output_aliases={3: 0} if others else {}, compiler_params=_params(1), name=name)
    args = (x, u, w_out, *others)
    return pl.pallas_call(body, **kw)(*args) if swap is None else _call_with_swap(body, args, swap, **kw)


def _ffn_bwd(dxo, x, g, u, w_in_t, w_out, name, deps=()):
    t = x.shape[0]
    tm = min(ROW_TILE, t)

    def body(dxo_ref, x_ref, g_ref, u_ref, w_ref, wo_ref, dx_ref, du_ref, h_ref, dy_ref, dg_ref):
        dxo = dxo_ref[...]
        dy = (0.5 * dxo).astype(BF)
        dy_ref[...] = dy
        dh = _dot_nt(dy, wo_ref[...])
        a = u_ref[:, :F].astype(F32)
        b = u_ref[:, F:].astype(F32)
        s = _sig(a)
        sa = a * s
        h_ref[...] = (sa * b).astype(BF)
        du_ref[:, :F] = (dh * b * (s * (1.0 + a * (1.0 - s)))).astype(BF)
        du_ref[:, F:] = (dh * sa).astype(BF)
        dn = _dot(du_ref[...], w_ref[...])
        dx, dg = _rms_bwd(dn, x_ref[...], g_ref[...])
        dx_ref[...] = dxo + dx

        @pl.when(pl.program_id(0) == 0)
        def _():
            dg_ref[...] = jnp.zeros_like(dg_ref)

        dg_ref[...] += dg

    return _call(
        body, deps, (dxo, x, g, u, w_in_t, w_out), grid=(t // tm,),
        in_specs=[_row_tile(tm, D), _row_tile(tm, D), _resident((1, D)), _row_tile(tm, INW), _resident((INW, D)),
                  _resident((F, D))],
        out_specs=[_row_tile(tm, D), _row_tile(tm, INW), _row_tile(tm, F), _row_tile(tm, D),
                   pl.BlockSpec((1, D), lambda i: (0, 0))],
        out_shape=[jax.ShapeDtypeStruct((t, D), F32), jax.ShapeDtypeStruct((t, INW), BF), jax.ShapeDtypeStruct((t, F), BF),
                   jax.ShapeDtypeStruct((t, D), BF), jax.ShapeDtypeStruct((1, D), F32)],
        compiler_params=_params(1), name=name)


def _wgrad(lhs, rhs, name, *, lhs_is_transposed, chunk, deps=()):
    t = rhs.shape[0]
    n = lhs.shape[0] if lhs_is_transposed else lhs.shape[1]
    c = min(chunk, n)

    def body(l_ref, r_ref, o_ref):
        if lhs_is_transposed:
            o_ref[...] = _dot(l_ref[...], r_ref[...]).astype(BF)
        else:
            o_ref[...] = _dot_tn(l_ref[...], r_ref[...]).astype(BF)

    lhs_spec = pl.BlockSpec((c, t), lambda j: (j, 0)) if lhs_is_transposed else pl.BlockSpec((t, c), lambda j: (0, j))
    return _call(
        body, deps, (lhs, rhs), grid=(n // c,),
        in_specs=[lhs_spec, _resident((t, D))],
        out_specs=pl.BlockSpec((c, D), lambda j: (j, 0)),
        out_shape=jax.ShapeDtypeStruct((n, D), BF),
        compiler_params=_params(1), name=name)


def _wgrad_mix(duc, dq_t, dkv_t, dgp, hm):
    t = hm.shape[0]
    c = 512
    first_q, first_kv, first_gate = R_Q[0] // c, R_KV[0] // c, R_GATE[0] // c

    def body(uc_ref, q_ref, kv_ref, gp_ref, h_ref, o_ref):
        j = pl.program_id(0)

        @pl.when(j < first_q)
        def _():
            o_ref[...] = _dot_tn(uc_ref[...], h_ref[...]).astype(BF)

        @pl.when((j >= first_q) & (j < first_kv))
        def _():
            o_ref[...] = _dot(q_ref[...], h_ref[...]).astype(BF)

        @pl.when((j >= first_kv) & (j < first_gate))
        def _():
            o_ref[...] = _dot(kv_ref[...], h_ref[...]).astype(BF)

        @pl.when(j >= first_gate)
        def _():
            o_ref[...] = _dot_tn(gp_ref[...], h_ref[...]).astype(BF)

    return pl.pallas_call(
        body, grid=(INW // c,),
        in_specs=[pl.BlockSpec((t, c), lambda j: (0, jnp.clip(j, 0, first_q - 1))),
                  pl.BlockSpec((c, t), lambda j: (jnp.clip(j - first_q, 0, first_kv - first_q - 1), 0)),
                  pl.BlockSpec((c, t), lambda j: (jnp.clip(j - first_kv, 0, first_gate - first_kv - 1), 0)),
                  pl.BlockSpec((t, c), lambda j: (0, jnp.clip(j - first_gate, 0, INW // c - first_gate - 1))),
                  _resident((t, D))],
        out_specs=pl.BlockSpec((c, D), lambda j: (j, 0)),
        out_shape=jax.ShapeDtypeStruct((INW, D), BF),
        compiler_params=_params(1), name="mix_dw_in")(duc, dq_t, dkv_t, dgp, hm)


def _mix_proj(x, g, w_t):
    t = x.shape[0]
    tm = min(ROW_TILE_WIDE, t)

    def body(x_ref, g_ref, w_ref, hm_ref, uc_ref, gp_ref, qkv_ref):
        r, xh = _rms_stats(x_ref[...])
        hm = (xh * g_ref[...]).astype(BF)
        hm_ref[...] = hm
        uc_ref[...] = _dot_nt(hm, w_ref[R_CONV[0]:R_CONV[1], :]).astype(BF)
        gp_ref[...] = _dot_nt(hm, w_ref[R_GATE[0]:R_GATE[1], :]).astype(BF)
        qkv_ref[...] = _dot_nt(w_ref[R_QKV[0]:R_QKV[1], :], hm).astype(BF)

    return pl.pallas_call(
        body, grid=(t // tm,),
        in_specs=[_row_tile(tm, D), _resident((1, D)), _resident((INW, D))],
        out_specs=[_row_tile(tm, D), _row_tile(tm, 2 * D), _row_tile(tm, 2 * D), pl.BlockSpec((1536, tm), lambda i: (0, i))],
        out_shape=[jax.ShapeDtypeStruct((t, D), BF), jax.ShapeDtypeStruct((t, 2 * D), BF),
                   jax.ShapeDtypeStruct((t, 2 * D), BF), jax.ShapeDtypeStruct((1536, t), BF)],
        compiler_params=_params(1), name="mix_proj")(x, g, w_t)


CONV_HALO = 32
CONV_LEAD = CONV_HALO - (CW - 1)


def _glu(uc):
    uc = uc.astype(F32)
    return uc[:, :D] * _sig(uc[:, D:])


def _ln_stats(zc):
    mu = jnp.mean(zc, axis=-1, keepdims=True)
    zm = zc - mu
    r = lax.rsqrt(jnp.mean(zm * zm, axis=-1, keepdims=True) + EPS)
    return r, zm * r


CONV_SHIFTS = 8
CONV_CHUNK = 32


def _store_shifted(buf, rows):
    for b in range(1, CONV_SHIFTS):
        buf[b, 0:rows - 8, :] = buf[0, pl.ds(b, rows - 8), :]


def _conv_fwd(uc, dwk, dwb, lng, lnb, swap=None):
    t = uc.shape[0]
    tm = min(512, t)
    per = tm // CONV_HALO
    ext = tm + CONV_HALO

    def body(cur_ref, prev_ref, k_ref, kb_ref, g_ref, b_ref, o_ref, zc_ref, zsh):
        i = pl.program_id(0)
        zsh[0, 0:CONV_HALO, :] = _glu(prev_ref[...]) * (i > 0).astype(F32)
        zsh[0, CONV_HALO:, :] = _glu(cur_ref[...])
        _store_shifted(zsh, ext)

        def chunk(ci, carry):
            r0 = pl.multiple_of(ci * CONV_CHUNK, CONV_CHUNK)
            acc = jnp.zeros((CONV_CHUNK, D), F32) + kb_ref[...]
            for w in range(CW):
                a, b = divmod(CONV_LEAD + w, 8)
                acc = acc + k_ref[w:w + 1, :] * zsh[b, pl.ds(r0 + 8 * a, CONV_CHUNK), :]
            zc_ref[pl.ds(r0, CONV_CHUNK), :] = acc
            return carry

        lax.fori_loop(0, tm // CONV_CHUNK, chunk, 0)
        r, xh = _ln_stats(zc_ref[...])
        y = xh * g_ref[...] + b_ref[...]
        o_ref[...] = (y * _sig(y)).astype(BF)

    kw = dict(
        grid=(t // tm,),
        in_specs=[_row_tile(tm, 2 * D),
                  pl.BlockSpec((CONV_HALO, 2 * D), lambda i: (jnp.maximum(i * per - 1, 0), 0)),
                  _resident((CWP, D)), _resident((1, D)), _resident((1, D)), _resident((1, D))],
        out_specs=[_row_tile(tm, D), _row_tile(tm, D)],
        out_shape=[jax.ShapeDtypeStruct((t, D), BF), jax.ShapeDtypeStruct((t, D), F32)],
        scratch_shapes=[pltpu.VMEM((CONV_SHIFTS, ext, D), F32)],
        compiler_params=_params(1), name="conv_fwd")
    args = (uc, uc, dwk, dwb, lng, lnb)
    return pl.pallas_call(body, **kw)(*args) if swap is None else _call_with_swap(body, args, swap, **kw)


def _conv_bwd(uc, zc, dzs, dwk, lng, lnb):
    t = uc.shape[0]
    tm = min(ROW_TILE_WIDE, t)
    per = tm // CONV_HALO
    n_tiles = t // tm
    ext = tm + CONV_HALO
    last_block = t // CONV_HALO - 1

    def body(cur_ref, zc_ref, zcn_ref, dz_ref, dzn_ref, k_ref, g_ref, b_ref,
             duc_ref, dk_ref, dkb_ref, dg_ref, db_ref, dsh, dk8, z_scr):
        i = pl.program_id(0)

        @pl.when(i == 0)
        def _():
            dk8[...] = jnp.zeros_like(dk8)
            dkb_ref[...] = jnp.zeros_like(dkb_ref)
            dg_ref[...] = jnp.zeros_like(dg_ref)
            db_ref[...] = jnp.zeros_like(db_ref)

        has_next = (i < n_tiles - 1).astype(F32)
        z_scr[...] = _glu(cur_ref[...])
        gain = g_ref[...]

        def ln_silu_bwd(zc, dzs, live):
            r, xh = _ln_stats(zc)
            y = xh * gain + b_ref[...]
            sy = _sig(y)
            dy = dzs * (sy * (1.0 + y * (1.0 - sy))) * live
            dxh = dy * gain
            dzc = r * (dxh - jnp.mean(dxh, axis=-1, keepdims=True) - xh * jnp.mean(dxh * xh, axis=-1, keepdims=True))
            return dzc, dy, xh

        dzc, dy, xh = ln_silu_bwd(zc_ref[...], dz_ref[...], 1.0)
        dsh[0, 0:tm, :] = dzc
        dg_ref[...] += jnp.sum(dy * xh, axis=0, keepdims=True)
        db_ref[...] += jnp.sum(dy, axis=0, keepdims=True)
        dkb_ref[...] += jnp.sum(dzc, axis=0, keepdims=True)
        dsh[0, tm:, :] = ln_silu_bwd(zcn_ref[...], dzn_ref[...], has_next)[0]
        _store_shifted(dsh, ext)

        def chunk(ci, carry):
            r0 = pl.multiple_of(ci * CONV_CHUNK, CONV_CHUNK)
            z_c = z_scr[pl.ds(r0, CONV_CHUNK), :]
            dz = jnp.zeros((CONV_CHUNK, D), F32)
            for w in range(CW):
                a, b = divmod(CW - 1 - w, 8)
                window = dsh[b, pl.ds(r0 + 8 * a, CONV_CHUNK), :]
                dz = dz + k_ref[w:w + 1, :] * window
                prod = z_c * window
                part = prod[0:8, :]
                for j in range(1, CONV_CHUNK // 8):
                    part = part + prod[8 * j:8 * j + 8, :]
                dk8[w] += part
            ucc = cur_ref[pl.ds(r0, CONV_CHUNK), :].astype(F32)
            sg = _sig(ucc[:, D:])
            duc_ref[pl.ds(r0, CONV_CHUNK), 0:D] = (dz * sg).astype(BF)
            duc_ref[pl.ds(r0, CONV_CHUNK), D:2 * D] = (dz * ucc[:, :D] * sg * (1.0 - sg)).astype(BF)
            return carry

        lax.fori_loop(0, tm // CONV_CHUNK, chunk, 0)

        @pl.when(i == n_tiles - 1)
        def _():
            dk_ref[...] = jnp.sum(dk8[...], axis=1)

    vec = pl.BlockSpec((1, D), lambda i: (0, 0))
    next_halo = pl.BlockSpec((CONV_HALO, D), lambda i: (jnp.minimum((i + 1) * per, last_block), 0))
    return pl.pallas_call(
        body, grid=(n_tiles,),
        in_specs=[_row_tile(tm, 2 * D), _row_tile(tm, D), next_halo, _row_tile(tm, D), next_halo,
                  _resident((CWP, D)), _resident((1, D)), _resident((1, D))],
        out_specs=[_row_tile(tm, 2 * D), pl.BlockSpec((CWP, D), lambda i: (0, 0)), vec, vec, vec],
        out_shape=[jax.ShapeDtypeStruct((t, 2 * D), BF), jax.ShapeDtypeStruct((CWP, D), F32),
                   jax.ShapeDtypeStruct((1, D), F32), jax.ShapeDtypeStruct((1, D), F32), jax.ShapeDtypeStruct((1, D), F32)],
        scratch_shapes=[pltpu.VMEM((CONV_SHIFTS, ext, D), F32), pltpu.VMEM((CWP, 8, D), F32), pltpu.VMEM((tm, D), F32)],
        compiler_params=_params(1), name="conv_bwd")(uc, zc, zc, dzs, dzs, dwk, lng, lnb)


def _norm_rows(xt, g):
    r = lax.rsqrt(jnp.mean(xt * xt, axis=0, keepdims=True) + EPS)
    xh = xt * r
    return xh * g, r, xh


ATT_TQ = 1024


def _attn_specs(t, tq):
    per = tq // BLK
    return [pl.BlockSpec((1536, tq), lambda i: (0, i)),
            pl.BlockSpec((512, BLK), lambda i: (2, jnp.maximum(i * per - 1, 0))),
            _resident((HD, 1)), _resident((HD, 1)), _resident((NKV, 1, GRP * BLK)),
            _resident((2, NKV, 2 * BLK, GRP * BLK))]


def _attn_window(hk, sb, qkv_ref, halo_ref, kn_cur, kn_halo):
    v0 = D + NKV * HD + hk * HD
    if sb == 0:
        k_prev = kn_halo[hk]
        v_prev = halo_ref[NKV * HD + hk * HD:NKV * HD + (hk + 1) * HD, :]
    else:
        k_prev = kn_cur[hk][:, (sb - 1) * BLK:sb * BLK]
        v_prev = qkv_ref[v0:v0 + HD, (sb - 1) * BLK:sb * BLK]
    kw = jnp.concatenate([k_prev, kn_cur[hk][:, sb * BLK:(sb + 1) * BLK]], axis=1).astype(BF)
    vw = jnp.concatenate([v_prev, qkv_ref[v0:v0 + HD, sb * BLK:(sb + 1) * BLK]], axis=1)
    return kw, vw


def _attn_probs(kw, qc, bias, sink):
    st = _dot_tn(kw, qc) + bias
    m = jnp.maximum(jnp.max(st, axis=0, keepdims=True), sink)
    p = jnp.exp(st - m)
    e_sink = jnp.exp(sink - m)
    inv = 1.0 / (jnp.sum(p, axis=0, keepdims=True) + e_sink)
    return p * inv, e_sink * inv


def _attn_fwd(qkv_t, qg, kg, sink_rows, bias_t):
    t = qkv_t.shape[1]
    tq = min(ATT_TQ, t)
    n_sub = tq // BLK

    def body(qkv_ref, halo_ref, qg_ref, kg_ref, sink_ref, bias_ref, o_ref, p_ref, ps_ref):
        i = pl.program_id(0)
        first = (i == 0).astype(jnp.int32)
        kgain = kg_ref[...]
        qgain = qg_ref[...]
        kn_cur = [_norm_rows(qkv_ref[D + h * HD:D + (h + 1) * HD, :].astype(F32), kgain)[0] for h in range(NKV)]
        kn_halo = [_norm_rows(halo_ref[h * HD:(h + 1) * HD, :].astype(F32), kgain)[0] for h in range(NKV)]
        for hk in range(NKV):
            for sb in range(n_sub):
                cols = slice(sb * BLK, (sb + 1) * BLK)
                kw, vw = _attn_window(hk, sb, qkv_ref, halo_ref, kn_cur, kn_halo)
                qc = jnp.concatenate(
                    [_norm_rows(qkv_ref[(GRP * hk + g) * HD:(GRP * hk + g + 1) * HD, cols].astype(F32), qgain)[0] * QK_SCALE
                     for g in range(GRP)], axis=1).astype(BF)
                bias = bias_ref[first, hk] if sb == 0 else bias_ref[0, hk]
                p, p_sink = _attn_probs(kw, qc, bias, sink_ref[hk])
                p = p.astype(BF)
                p_ref[sb, hk] = p
                ps_ref[sb, hk] = p_sink
                o = _dot(vw, p)
                for g in range(GRP):
                    head = GRP * hk + g
                    o_ref[head * HD:(head + 1) * HD, cols] = o[:, g * BLK:(g + 1) * BLK].astype(BF)

    return pl.pallas_call(
        body, grid=(t // tq,),
        in_specs=_attn_specs(t, tq),
        out_specs=[pl.BlockSpec((D, tq), lambda i: (0, i)),
                   pl.BlockSpec((n_sub, NKV, 2 * BLK, GRP * BLK), lambda i: (i, 0, 0, 0)),
                   pl.BlockSpec((n_sub, NKV, 1, GRP * BLK), lambda i: (i, 0, 0, 0))],
        out_shape=[jax.ShapeDtypeStruct((D, t), BF), jax.ShapeDtypeStruct((t // BLK, NKV, 2 * BLK, GRP * BLK), BF),
                   jax.ShapeDtypeStruct((t // BLK, NKV, 1, GRP * BLK), F32)],
        compiler_params=_params(1), name="attn_fwd")(qkv_t, qkv_t, qg, kg, sink_rows, bias_t)


def _attn_bwd(qkv_t, do_t, probs, sink_probs, qg, kg, onehot_t, deps=()):
    t = qkv_t.shape[1]
    tq = min(ATT_TQ, t)
    n_sub = tq // BLK
    n_tiles = t // tq

    def body(qkv_ref, halo_ref, do_ref, p_ref, ps_ref, qg_ref, kg_ref, oh_ref,
             dq_ref, ckv_ref, dqg_ref, dsink_ref, dbias_ref, qg_scr, sink_scr, ds_scr):
        i = pl.program_id(0)

        @pl.when(i == 0)
        def _():
            qg_scr[...] = jnp.zeros_like(qg_scr)
            sink_scr[...] = jnp.zeros_like(sink_scr)
            ds_scr[...] = jnp.zeros_like(ds_scr)

        kgain = kg_ref[...]
        qgain = qg_ref[...]
        kn_cur = [_norm_rows(qkv_ref[D + h * HD:D + (h + 1) * HD, :].astype(F32), kgain)[0] for h in range(NKV)]
        kn_halo = [_norm_rows(halo_ref[h * HD:(h + 1) * HD, :].astype(F32), kgain)[0] for h in range(NKV)]
        dqg = jnp.zeros((HD, BLK), F32)
        for hk in range(NKV):
            for sb in range(n_sub):
                cols = slice(sb * BLK, (sb + 1) * BLK)
                kw, vw = _attn_window(hk, sb, qkv_ref, halo_ref, kn_cur, kn_halo)
                qn, qr, qh = [], [], []
                for g in range(GRP):
                    head = GRP * hk + g
                    n_, r_, h_ = _norm_rows(qkv_ref[head * HD:(head + 1) * HD, cols].astype(F32), qgain)
                    qn.append(n_)
                    qr.append(r_)
                    qh.append(h_)
                qc = (jnp.concatenate(qn, axis=1) * QK_SCALE).astype(BF)
                p_bf = p_ref[sb, hk]
                p = p_bf.astype(F32)
                doc = jnp.concatenate([do_ref[(GRP * hk + g) * HD:(GRP * hk + g + 1) * HD, cols] for g in range(GRP)], axis=1)
                dp = _dot_tn(vw, doc)
                delta = jnp.sum(p * dp, axis=0, keepdims=True)
                ds = p * (dp - delta)
                sink_scr[hk] += -(ps_ref[sb, hk] * delta)
                ds_scr[hk] += ds
                dsb = ds.astype(BF)
                dqc = _dot(kw, dsb) * QK_SCALE
                ckv_ref[sb, hk * HD:(hk + 1) * HD, :] = _dot_nt(qc, dsb)
                ckv_ref[sb, NKV * HD + hk * HD:NKV * HD + (hk + 1) * HD, :] = _dot_nt(doc, p_bf)
                for g in range(GRP):
                    head = GRP * hk + g
                    dqn = dqc[:, g * BLK:(g + 1) * BLK]
                    dqh = dqn * qgain
                    dq = qr[g] * (dqh - qh[g] * jnp.mean(dqh * qh[g], axis=0, keepdims=True))
                    dq_ref[head * HD:(head + 1) * HD, cols] = dq.astype(BF)
                    dqg = dqg + dqn * qh[g]
        qg_scr[...] += dqg

        @pl.when(i == n_tiles - 1)
        def _():
            dqg_ref[...] = jnp.sum(qg_scr[...], axis=1, keepdims=True)
            dsink_ref[...] = _group_lane_sums(sink_scr[:, 0, :])

            def bucket(b, carry):
                oh = jnp.concatenate([oh_ref[b]] * GRP, axis=1)
                dbias_ref[b] = _group_lane_sums(jnp.sum(ds_scr[...] * oh[None], axis=1))
                return carry

            lax.fori_loop(0, NBUCKET, bucket, 0)

    return _call(
        body, deps, (qkv_t, qkv_t, do_t, probs, sink_probs, qg, kg, onehot_t), grid=(n_tiles,),
        in_specs=_attn_specs(t, tq)[:2] + [pl.BlockSpec((D, tq), lambda i: (0, i)),
                                           pl.BlockSpec((n_sub, NKV, 2 * BLK, GRP * BLK), lambda i: (i, 0, 0, 0)),
                                           pl.BlockSpec((n_sub, NKV, 1, GRP * BLK), lambda i: (i, 0, 0, 0))]
        + _attn_specs(t, tq)[2:4] + [_resident((NBUCKET, 2 * BLK, BLK))],
        out_specs=[pl.BlockSpec((D, tq), lambda i: (0, i)),
                   pl.BlockSpec((n_sub, 2 * NKV * HD, 2 * BLK), lambda i: (i, 0, 0)),
                   pl.BlockSpec((HD, 1), lambda i: (0, 0)),
                   pl.BlockSpec((NKV, BLK), lambda i: (0, 0)),
                   pl.BlockSpec((NBUCKET, NKV, BLK), lambda i: (0, 0, 0))],
        out_shape=[jax.ShapeDtypeStruct((D, t), BF),
                   jax.ShapeDtypeStruct((t // BLK, 2 * NKV * HD, 2 * BLK), F32),
                   jax.ShapeDtypeStruct((HD, 1), F32),
                   jax.ShapeDtypeStruct((NKV, BLK), F32),
                   jax.ShapeDtypeStruct((NBUCKET, NKV, BLK), F32)],
        scratch_shapes=[pltpu.VMEM((HD, BLK), F32), pltpu.VMEM((NKV, 1, GRP * BLK), F32),
                        pltpu.VMEM((NKV, 2 * BLK, GRP * BLK), F32)],
        compiler_params=_params(1), name="attn_bwd")


def _kv_combine_tile(c_ref, cn_ref, has_next, k_ref, kgain, o_ref):
    rows = NKV * HD
    per = c_ref.shape[0]
    dkg = jnp.zeros((HD, BLK), F32)
    for s in range(per):
        cols = slice(s * BLK, (s + 1) * BLK)
        after = c_ref[s + 1, :, :BLK] if s + 1 < per else cn_ref[0, :, :BLK] * has_next
        d = c_ref[s, :, BLK:] + after
        o_ref[rows:, cols] = d[rows:, :].astype(BF)
        for h in range(NKV):
            _, r, kh = _norm_rows(k_ref[h * HD:(h + 1) * HD, cols].astype(F32), kgain)
            dkn = d[h * HD:(h + 1) * HD, :]
            dkh = dkn * kgain
            o_ref[h * HD:(h + 1) * HD, cols] = (r * (dkh - kh * jnp.mean(dkh * kh, axis=0, keepdims=True))).astype(BF)
            dkg = dkg + dkn * kh
    return dkg


def _group_lane_sums(v):
    lane_group = lax.broadcasted_iota(jnp.int32, (1, GRP * BLK), 1) // BLK
    col = lax.broadcasted_iota(jnp.int32, (1, BLK), 1)
    out = jnp.zeros((v.shape[0], BLK), F32)
    for g in range(GRP):
        s = jnp.sum(jnp.where(lane_group == g, v, 0.0), axis=1, keepdims=True)
        out = jnp.where(col == g, s, out)
    return out


def _mix_out(zs, o_t, gp, x, w_cp, w_o, w_out):
    t = x.shape[0]
    tm = min(ROW_TILE_WIDE, t)

    def body(zs_ref, ot_ref, gp_ref, x_ref, wcp_ref, wo_ref, wout_ref, xo_ref, a_ref, b_ref, m_ref):
        a = _dot(zs_ref[...], wcp_ref[...])
        b = _dot_tn(ot_ref[...], wo_ref[...])
        a_ref[...] = a.astype(BF)
        b_ref[...] = b.astype(BF)
        merged = (_sig(gp_ref[:, :D].astype(F32)) * a + _sig(gp_ref[:, D:].astype(F32)) * b).astype(BF)
        m_ref[...] = merged
        xo_ref[...] = x_ref[...] + _dot(merged, wout_ref[...])

    return pl.pallas_call(
        body, grid=(t // tm,),
        in_specs=[_row_tile(tm, D), pl.BlockSpec((D, tm), lambda i: (0, i)), _row_tile(tm, 2 * D), _row_tile(tm, D),
                  _resident((D, D)), _resident((D, D)), _resident((D, D))],
        out_specs=[_row_tile(tm, D)] * 4,
        out_shape=[jax.ShapeDtypeStruct((t, D), F32)] + [jax.ShapeDtypeStruct((t, D), BF)] * 3,
        compiler_params=_params(1), name="mix_out")(zs, o_t, gp, x, w_cp, w_o, w_out)


def _mix_out_bwd(dx, a, b, gp, w_cp, w_o, w_out, deps=()):
    t = dx.shape[0]
    tm = min(ROW_TILE_WIDE, t)

    def body(dx_ref, a_ref, b_ref, gp_ref, wcp_ref, wo_ref, wout_ref, dzs_ref, dot_ref, dgp_ref, da_ref, db_ref, dxb_ref):
        dxb = dx_ref[...].astype(BF)
        dxb_ref[...] = dxb
        dm = _dot_nt(dxb, wout_ref[...])
        gc = _sig(gp_ref[:, :D].astype(F32))
        ga = _sig(gp_ref[:, D:].astype(F32))
        da = (dm * gc).astype(BF)
        db = (dm * ga).astype(BF)
        da_ref[...] = da
        db_ref[...] = db
        dgp_ref[:, :D] = (dm * a_ref[...].astype(F32) * gc * (1.0 - gc)).astype(BF)
        dgp_ref[:, D:] = (dm * b_ref[...].astype(F32) * ga * (1.0 - ga)).astype(BF)
        dzs_ref[...] = _dot_nt(da, wcp_ref[...])
        dot_ref[...] = _dot_nt(wo_ref[...], db).astype(BF)

    return _call(
        body, deps, (dx, a, b, gp, w_cp, w_o, w_out), grid=(t // tm,),
        in_specs=[_row_tile(tm, D), _row_tile(tm, D), _row_tile(tm, D), _row_tile(tm, 2 * D),
                  _resident((D, D)), _resident((D, D)), _resident((D, D))],
        out_specs=[_row_tile(tm, D), pl.BlockSpec((D, tm), lambda i: (0, i)), _row_tile(tm, 2 * D),
                   _row_tile(tm, D), _row_tile(tm, D), _row_tile(tm, D)],
        out_shape=[jax.ShapeDtypeStruct((t, D), F32), jax.ShapeDtypeStruct((D, t), BF), jax.ShapeDtypeStruct((t, 2 * D), BF),
                   jax.ShapeDtypeStruct((t, D), BF), jax.ShapeDtypeStruct((t, D), BF), jax.ShapeDtypeStruct((t, D), BF)],
        compiler_params=_params(1), name="mix_out_bwd")


def _mix_proj_bwd(dxo, duc, dq_t, ckv, qkv_t, kg, dgp, x, g, w_t):
    t = x.shape[0]
    tm = min(ROW_TILE_WIDE, t)
    per = tm // BLK
    steps = t // tm
    kv_rows = 2 * NKV * HD

    def body(dxo_ref, duc_ref, dq_ref, c_ref, cn_ref, k_ref, kg_ref, dgp_ref, x_ref, g_ref, w_ref,
             dx_ref, dg_ref, dkv_ref, dkg_ref, kg_scr):
        i = pl.program_id(0)

        @pl.when(i == 0)
        def _():
            dg_ref[...] = jnp.zeros_like(dg_ref)
            kg_scr[...] = jnp.zeros_like(kg_scr)

        kg_scr[...] += _kv_combine_tile(c_ref, cn_ref, (i < steps - 1).astype(F32), k_ref, kg_ref[...], dkv_ref)
        dn = _dot(duc_ref[...], w_ref[R_CONV[0]:R_CONV[1], :])
        dn = dn + _dot(dgp_ref[...], w_ref[R_GATE[0]:R_GATE[1], :])
        dn = dn + _dot_tn(dq_ref[...], w_ref[R_Q[0]:R_Q[1], :])
        dn = dn + _dot_tn(dkv_ref[...], w_ref[R_KV[0]:R_KV[1], :])
        dx, dg = _rms_bwd(dn, x_ref[...], g_ref[...])
        dx_ref[...] = dxo_ref[...] + dx
        dg_ref[...] += dg

        @pl.when(i == steps - 1)
        def _():
            dkg_ref[...] = jnp.sum(kg_scr[...], axis=1, keepdims=True)

    return pl.pallas_call(
        body, grid=(steps,),
        in_specs=[_row_tile(tm, D), _row_tile(tm, 2 * D), pl.BlockSpec((D, tm), lambda i: (0, i)),
                  pl.BlockSpec((per, kv_rows, 2 * BLK), lambda i: (i, 0, 0)),
                  pl.BlockSpec((1, kv_rows, 2 * BLK), lambda i: (jnp.minimum((i + 1) * per, t // BLK - 1), 0, 0)),
                  pl.BlockSpec((NKV * HD, tm), lambda i: (D // (NKV * HD), i)), _resident((HD, 1)),
                  _row_tile(tm, 2 * D), _row_tile(tm, D), _resident((1, D)), _resident((INW, D))],
        out_specs=[_row_tile(tm, D), pl.BlockSpec((1, D), lambda i: (0, 0)), pl.BlockSpec((kv_rows, tm), lambda i: (0, i)),
                   pl.BlockSpec((HD, 1), lambda i: (0, 0))],
        out_shape=[jax.ShapeDtypeStruct((t, D), F32), jax.ShapeDtypeStruct((1, D), F32),
                   jax.ShapeDtypeStruct((kv_rows, t), BF), jax.ShapeDtypeStruct((HD, 1), F32)],
        scratch_shapes=[pltpu.VMEM((HD, BLK), F32)],
        compiler_params=_params(1), name="mix_proj_bwd")(dxo, duc, dq_t, ckv, ckv, qkv_t, kg, dgp, x, g, w_t)


def _attention_tables():
    kj = np.arange(2 * BLK)[:, None]
    qi = np.arange(BLK)[None, :]
    dist = qi + BLK - kj
    in_win = (dist >= 0) & (dist < BLK)
    dpos = np.maximum(dist, 0)
    max_exact = NBUCKET // 2
    dfl = np.maximum(dpos, 1).astype(np.float32)
    large = max_exact + (np.log(dfl / np.float32(max_exact)) / np.float32(math.log(BLK / max_exact))
                         * np.float32(NBUCKET - max_exact)).astype(np.int32)
    large = np.minimum(large, NBUCKET - 1)
    bucket = np.where(dpos < max_exact, dpos, large)
    onehot = (bucket[None] == np.arange(NBUCKET)[:, None, None]).astype(np.float32)
    mask = in_win.astype(np.float32)
    mask_first = mask * (kj >= BLK)
    masks = np.stack([np.tile(mask, (1, GRP)), np.tile(mask_first, (1, GRP))])
    return onehot, masks


def _bias_table(rel_bias, onehot):
    tab = jnp.einsum("bkq,bh->hkq", onehot, rel_bias, precision=lax.Precision.HIGHEST)
    tab = tab.reshape(NKV, GRP, 2 * BLK, BLK)
    return jnp.transpose(tab, (0, 2, 1, 3)).reshape(NKV, 2 * BLK, GRP * BLK)


def _local_step(x, target, vec, ffn1_up, weights_of, wgrad, wgrads, grads_done, small_done):
    onehot_np, masks_np = _attention_tables()
    onehot = jnp.asarray(onehot_np)
    masks = jnp.asarray(masks_np)
    bias_t = jnp.where(masks[:, None] > 0.5, _bias_table(vec["rel_bias"], onehot)[None], NEG)
    sink_rows = jnp.repeat(vec["attn_sinks"].reshape(NKV, 1, GRP), BLK, axis=2)
    qg = vec["q_norm"].reshape(HD, 1)
    kg = vec["k_norm"].reshape(HD, 1)
    g1 = vec["ffn1_norm"].reshape(1, D)
    gm = vec["mix_norm"].reshape(1, D)
    g2 = vec["ffn2_norm"].reshape(1, D)
    dwb = vec["conv_dw_bias"].reshape(1, D)
    lng = vec["conv_ln_g"].reshape(1, D)
    lnb = vec["conv_ln_b"].reshape(1, D)

    n1, u1, w1 = ffn1_up(x, g1, (bias_t, sink_rows))
    x1, = _ffn_down(x, u1, w1["ffn1_w_out"], "ffn1_down_first", part=(0, 2))
    (x1,), wm = weights_of("mix_proj", (x1,), during=functools.partial(
        _ffn_down, x, u1, w1["ffn1_w_out"], "ffn1_down", part=(1, 2), out=x1))
    dwk = jnp.pad(wm["conv_dw_kernel"], ((0, CWP - CW), (0, 0)))
    hm, uc, gp, qkv_t = _mix_proj(x1, gm, wm["w_in"])
    (zs, zc), merge = weights_of("mix_merge", (uc,), during=functools.partial(_conv_fwd, uc, dwk, dwb, lng, lnb))
    wm.update(merge)
    o_t, probs, sink_probs = _attn_fwd(qkv_t, qg, kg, sink_rows, bias_t)
    x2, a, b, merged = _mix_out(zs, o_t, gp, x1, wm["conv_w_proj"], wm["attn_w_o"], wm["w_out"])
    w2 = weights_of("ffn2", (x2,))
    gv = {}
    n2, du2, h2, dy2, dx2, sq, gv["ffn2_norm"] = _ffn_last(x2, target, g2, w2["ffn2_w_in"], w2["ffn2_w_out"], "ffn2")

    deps = grads_done("ffn2", {"ffn2_w_in": wgrad(du2, n2, "ffn2_dw_in", False),
                               "ffn2_w_out": wgrad(h2, dy2, "ffn2_dw_out", False)})

    dzs, do_t, dgp, da, db, dx2b = _mix_out_bwd(dx2, a, b, gp, wm["conv_w_proj"], wm["attn_w_o"], wm["w_out"], deps=deps)
    grads = wgrads([(merged, dx2b, False), (zs, da, False), (o_t, db, True)], "mix_dw_merge")
    deps = grads_done("mix_out", dict(zip(("w_out", "conv_w_proj", "attn_w_o"), grads)))

    dq_t, ckv, dqg, dsink, dbias = _attn_bwd(qkv_t, do_t, probs, sink_probs, qg, kg, onehot, deps=deps)
    gv["q_norm"] = dqg.reshape(HD)
    gv["attn_sinks"] = dsink[:, :GRP].reshape(NQ)
    gv["rel_bias"] = dbias[:, :, :GRP].reshape(NBUCKET, NQ)

    duc, dk_conv, gv["conv_dw_bias"], gv["conv_ln_g"], gv["conv_ln_b"] = _conv_bwd(uc, zc, dzs, dwk, lng, lnb)
    gv["conv_dw_kernel"] = dk_conv[:CW]

    dx1, gv["mix_norm"], dkv_t, dkg = _mix_proj_bwd(dx2, duc, dq_t, ckv, qkv_t, kg, dgp, x1, gm, wm["w_in"])
    gv["k_norm"] = dkg.reshape(HD)
    deps = grads_done("mix_in", {"w_in": _wgrad_mix(duc, dq_t, dkv_t, dgp, hm)})

    dx0, du1, h1, dy1, gv["ffn1_norm"] = _ffn_bwd(dx1, x, g1, u1, w1["ffn1_w_in"], w1["ffn1_w_out"], "ffn1_bwd", deps=deps)
    for k in ("ffn1_norm", "mix_norm", "ffn2_norm", "conv_dw_bias", "conv_ln_g", "conv_ln_b"):
        gv[k] = gv[k].reshape(D)
    deps = small_done(gv, sq)
    deps = grads_done("ffn1_in", {"ffn1_w_in": wgrad(du1, n1, "ffn1_dw_in", False, deps)})
    grads_done("ffn1_out", {"ffn1_w_out": wgrad(h1, dy1, "ffn1_dw_out", False, deps)})
    return dx0


MESH_ID = pl.DeviceIdType.MESH


def _position():
    return lax.axis_index("x"), lax.axis_index("y"), lax.axis_index("c")


def _shard_rows(ref, index, rows):
    return ref.at[pl.ds(pl.multiple_of(index * rows, 16), rows), :]


def _prep(weights, taps, me, name, deps=()):
    n = len(weights)
    n_deps = len(deps)
    with_taps = taps is not None

    def body(me_ref, *refs):
        refs = refs[n_deps:]
        ins, outs = refs[:len(refs) // 2], refs[len(refs) // 2:]
        for k in range(n):
            outs[k][...] = ins[k][...].astype(BF)
        if with_taps:
            outs[n][0:CW, :] = ins[n][...]
            outs[n][CW:, :] = jnp.zeros((CWP - CW, BLK), F32)

    shard_shapes = [w.shape for w in weights] + [(CWP, BLK)] * with_taps
    dtypes = [BF] * n + [F32] * with_taps
    ins = list(weights) + [taps] * with_taps
    return pl.pallas_call(
        body,
        grid_spec=pltpu.PrefetchScalarGridSpec(
            num_scalar_prefetch=1, grid=(1,),
            in_specs=[ANY] * n_deps + [pl.BlockSpec(a.shape, lambda i, m: (0, 0), pipeline_mode=pl.Buffered(1)) for a in ins],
            out_specs=[pl.BlockSpec(s, lambda i, m: (m[0], 0)) for s in shard_shapes]),
        out_shape=[jax.ShapeDtypeStruct((N_DEV * s[0], s[1]), d) for s, d in zip(shard_shapes, dtypes)],
        compiler_params=_params(1), name=name)(me, *deps, *ins)


HBM = pl.BlockSpec(memory_space=pltpu.HBM)
SEM = pl.BlockSpec(memory_space=pltpu.SEMAPHORE)
DATAFLOW = pltpu.SideEffectType.DATAFLOW_SIDE_EFFECTING
TOKEN = jax.ShapeDtypeStruct((8, 128), F32)


def _in_hbm(x):
    return pltpu.with_memory_space_constraint(x, pltpu.HBM)


def _hbm_like(arrays):
    return [pltpu.HBM(a.shape, a.dtype) for a in arrays]


def _other_chips(x, y):
    return [(1 - x, y), (x, 1 - y), (1 - x, 1 - y)]


def _device_index(chip, c):
    return 4 * chip[0] + 2 * chip[1] + c


def _chip_index(chip):
    return 2 * chip[0] + chip[1]


class _Exchange:
    def __init__(self, gather, all_cores=False):
        self.gather = gather
        self.all_cores = all_cores
        self.n_peers = N_DEV - 1 if all_cores else 3

    def peers(self, x, y, c):
        if self.all_cores:
            return [(x ^ (k >> 2), y ^ ((k >> 1) & 1), c ^ (k & 1)) for k in range(1, N_DEV)]
        return [(*chip, c) for chip in _other_chips(x, y)]

    def sent(self, x, y, c, peer):
        return _device_index((x, y), c) if self.gather else _chip_index(peer[:2])

    def lands_at(self, x, y, c):
        return _device_index((x, y), c) if self.gather else _chip_index((x, y))

    def arrives_at(self, peer):
        return _device_index(peer[:2], peer[2]) if self.gather else _chip_index(peer[:2])


def _ici_copies_start(sets, sources, landings, exchanges, name, deps=()):
    n = len(landings)
    arrays = (list(sources) if sources is not None else []) + list(landings)
    first_land = len(arrays) - n
    n_sets = len(sets)
    n_deps = len(deps)

    def body(*refs):
        refs = refs[n_deps:]
        src, land = refs[:n], refs[first_land:first_land + n]
        sems = refs[len(arrays):len(arrays) + 2 * n_sets]
        token = refs[-1]
        x, y, c = _position()
        for s, (members, exchange) in enumerate(zip(sets, exchanges)):
            for slot, (k, rows) in enumerate(members):
                for j, peer in enumerate(exchange.peers(x, y, c)):
                    at = exchange.n_peers * slot + j
                    pltpu.make_async_remote_copy(
                        src_ref=_shard_rows(src[k], exchange.sent(x, y, c, peer), rows),
                        dst_ref=_shard_rows(land[k], exchange.lands_at(x, y, c), rows),
                        send_sem=sems[2 * s].at[at], recv_sem=sems[2 * s + 1].at[at],
                        device_id=peer, device_id_type=MESH_ID).start()
        token[...] = jnp.zeros_like(token)

    sem_shapes = []
    for members, exchange in zip(sets, exchanges):
        sem_shapes += [pltpu.SemaphoreType.DMA((exchange.n_peers * len(members),))] * 2
    out = pl.pallas_call(
        body, name=name,
        out_shape=sem_shapes + _hbm_like(arrays) + [TOKEN],
        in_specs=[ANY] * n_deps + [HBM] * len(arrays),
        out_specs=[SEM] * (2 * n_sets) + [HBM] * len(arrays) + [pl.BlockSpec(memory_space=pltpu.VMEM)],
        input_output_aliases={n_deps + i: 2 * n_sets + i for i in range(len(arrays))},
        compiler_params=pltpu.CompilerParams(has_side_effects=DATAFLOW),
    )(*deps, *[_in_hbm(a) for a in arrays])
    sems = [(out[2 * s], out[2 * s + 1]) for s in range(n_sets)]
    thru = list(out[2 * n_sets:2 * n_sets + len(arrays)])
    return sems, (thru[:first_land] if sources is not None else None), thru[first_land:], out[-1]


def _ici_copies_wait(sems, members, sources, landings, exchange, after, name):
    n = len(landings)
    arrays = (list(sources) if sources is not None else []) + list(landings)
    first_land = len(arrays) - n

    def body(*refs):
        src, land = refs[:n], refs[first_land:first_land + n]
        send_sems, recv_sems = refs[len(arrays)], refs[len(arrays) + 1]
        x, y, c = _position()
        for slot, rows in enumerate(members):
            for j, peer in enumerate(exchange.peers(x, y, c)):
                at = exchange.n_peers * slot + j
                cp = pltpu.make_async_remote_copy(
                    src_ref=_shard_rows(src[slot], exchange.sent(x, y, c, peer), rows),
                    dst_ref=_shard_rows(land[slot], exchange.arrives_at(peer), rows),
                    send_sem=send_sems.at[at], recv_sem=recv_sems.at[at], device_id=peer, device_id_type=MESH_ID)
                cp.wait_send()
                cp.wait_recv()

    out = pl.pallas_call(
        body, name=name, out_shape=_hbm_like(arrays),
        in_specs=[HBM] * len(arrays) + [SEM, SEM] + [ANY] * len(after), out_specs=[HBM] * len(arrays),
        input_output_aliases={i: i for i in range(len(arrays))},
        compiler_params=pltpu.CompilerParams(has_side_effects=DATAFLOW),
    )(*arrays, sems[0], sems[1], *after)
    return list(out[first_land:])


def _swap_copies(land, rows, which, send_sems, recv_sems):
    x, y, c = _position()
    chips = [([(x, y)] + _other_chips(x, y))[j] for j in which]
    sends, recvs = [], []
    for k in range(len(land)):
        for j, chip in enumerate(chips):
            for copies, core in ((sends, c), (recvs, 1 - c)):
                block = _shard_rows(land[k], _device_index(chip, core), rows[k])
                copies.append(pltpu.make_async_remote_copy(
                    src_ref=block, dst_ref=block, send_sem=send_sems.at[k, j], recv_sem=recv_sems.at[k, j],
                    device_id=(x, y, 1 - c), device_id_type=MESH_ID))
    return sends, recvs


def _d2d_gather(buffers, rows, name, which=(0, 1, 2, 3), deps=()):
    n = len(buffers)
    n_deps = len(deps)

    def body(*refs):
        sends, recvs = _swap_copies(refs[n_deps + n:n_deps + 2 * n], rows, which, *refs[n_deps + 2 * n:])
        for cp in sends:
            cp.start()
        for cp in recvs:
            cp.wait_recv()
        for cp in sends:
            cp.wait_send()

    return pl.pallas_call(
        body, name=name, out_shape=[jax.ShapeDtypeStruct(a.shape, a.dtype) for a in buffers],
        in_specs=[ANY] * (n_deps + n), out_specs=[ANY] * n, input_output_aliases={n_deps + i: i for i in range(n)},
        scratch_shapes=[pltpu.SemaphoreType.DMA((n, len(which))), pltpu.SemaphoreType.DMA((n, len(which)))],
    )(*deps, *buffers)


def _call_with_swap(body, args, swap, prefetch=(), **kw):
    buffers, rows = swap
    n, n_pre, n_in, n_out = len(buffers), len(prefetch), len(args), len(kw["out_shape"])
    n_scratch = len(kw["scratch_shapes"])
    grid = kw["grid"]
    which = (0, 1, 2, 3)

    def at_step(last):
        hit = [pl.program_id(a) == (extent - 1 if last else 0) for a, extent in enumerate(grid)]
        return functools.reduce(jnp.logical_and, hit)

    def hosted(*refs):
        pre, ins, refs = refs[:n_pre], refs[n_pre:n_pre + n_in], refs[n_pre + n_in + n:]
        outs, land, scratch = refs[:n_out], refs[n_out:n_out + n], refs[n_out + n:n_out + n + n_scratch]
        sends, recvs = _swap_copies(land, rows, which, *refs[n_out + n + n_scratch:])

        @pl.when(at_step(False))
        def _():
            for cp in sends:
                cp.start()

        body(*pre, *ins, *outs, *scratch)

        @pl.when(at_step(True))
        def _():
            for cp in recvs:
                cp.wait_recv()
            for cp in sends:
                cp.wait_send()

    sem_shape = pltpu.SemaphoreType.DMA((n, len(which)))
    aliases = {**kw.get("input_output_aliases", {}), **{n_in + i: n_out + i for i in range(n)}}
    out = pl.pallas_call(
        hosted,
        grid_spec=pltpu.PrefetchScalarGridSpec(
            num_scalar_prefetch=n_pre, grid=grid, in_specs=kw["in_specs"] + [ANY] * n, out_specs=kw["out_specs"] + [ANY] * n,
            scratch_shapes=kw["scratch_shapes"] + [sem_shape, sem_shape]),
        out_shape=kw["out_shape"] + [jax.ShapeDtypeStruct(a.shape, a.dtype) for a in buffers],
        input_output_aliases={n_pre + i: o for i, o in aliases.items()},
        compiler_params=kw["compiler_params"], name=kw["name"])(*prefetch, *args, *buffers)
    return out[:n_out], out[n_out:]


def _pair_exchange_add(grad, place, name):
    r = grad.shape[0] // N_DEV
    n_chips = N_DEV // 2

    def body(place_ref, g_hbm, kept_ref, part_ref, land_ref, inbox, send_sems, recv_sems):
        q = pl.program_id(0)
        x, y, c = _position()
        copies = [pltpu.make_async_remote_copy(
            src_ref=_shard_rows(g_hbm, 2 * i + 1 - c, r), dst_ref=inbox.at[i], send_sem=send_sems.at[i],
            recv_sem=recv_sems.at[i], device_id=(x, y, 1 - c), device_id_type=MESH_ID) for i in range(n_chips)]

        @pl.when(q == 0)
        def _():
            for cp in copies:
                cp.start()

        for i, cp in enumerate(copies):
            @pl.when(q == i)
            def _(cp=cp):
                cp.wait_recv()

        total = (kept_ref[...].astype(F32) + inbox[q].astype(F32)).astype(BF)
        part_ref[...] = total

        @pl.when(q == place_ref[1])
        def _():
            land_ref[...] = total

        @pl.when(q == n_chips - 1)
        def _():
            for cp in copies:
                cp.wait_send()

    return pl.pallas_call(
        body,
        grid_spec=pltpu.PrefetchScalarGridSpec(
            num_scalar_prefetch=1, grid=(n_chips,),
            in_specs=[ANY, pl.BlockSpec((r, D), lambda q, p: (2 * q + p[0], 0))],
            out_specs=[pl.BlockSpec((r, D), lambda q, p: (q, 0)), pl.BlockSpec((r, D), lambda q, p: (p[1], 0))],
            scratch_shapes=[pltpu.VMEM((n_chips, r, D), BF), pltpu.SemaphoreType.DMA((n_chips,)),
                            pltpu.SemaphoreType.DMA((n_chips,))]),
        out_shape=[jax.ShapeDtypeStruct((n_chips * r, D), BF)] * 2,
        compiler_params=_params(1), name=name)(place, grad, grad)


def _wgrad_pair(lhs, rhs, name, *, lhs_is_transposed, deps=()):
    t = rhs.shape[0]
    n = lhs.shape[0] if lhs_is_transposed else lhs.shape[1]
    r = n // N_DEV
    n_chips = N_DEV // 2
    per = 1 if (2 * r) % BLK == 0 else 2
    steps = n_chips // per

    def body(l_ref, r_ref, kept_ref, recv_ref, res, send_sems, recv_sems):
        q = pl.program_id(0)
        slot = q % 2
        x, y, c = _position()

        def send(step, buf, i):
            return pltpu.make_async_remote_copy(
                src_ref=res.at[buf, pl.ds(pl.multiple_of((2 * i + 1 - c) * r, 16), r), :],
                dst_ref=_shard_rows(recv_ref, step * per + i, r),
                send_sem=send_sems.at[buf, i], recv_sem=recv_sems.at[step * per + i],
                device_id=(x, y, 1 - c), device_id_type=MESH_ID)

        @pl.when(q >= 2)
        def _():
            for i in range(per):
                send(q - 2, slot, i).wait_send()

        if lhs_is_transposed:
            res[slot] = _dot(l_ref[...], r_ref[...]).astype(BF)
        else:
            res[slot] = _dot_tn(l_ref[...], r_ref[...]).astype(BF)
        for i in range(per):
            kept_ref[i * r:(i + 1) * r, :] = res[slot, pl.ds(pl.multiple_of((2 * i + c) * r, 16), r), :]
            send(q, slot, i).start()

        @pl.when(q == steps - 1)
        def _():
            for i in range(per):
                if steps > 1:
                    send(q - 1, 1 - slot, i).wait_send()
                send(q, slot, i).wait_send()
            for chip in range(n_chips):
                send(chip // per, 0, chip % per).wait_recv()

    width = 2 * r * per
    lhs_spec = pl.BlockSpec((width, t), lambda q: (q, 0)) if lhs_is_transposed else pl.BlockSpec((t, width), lambda q: (0, q))
    return _call(
        body, deps, (lhs, rhs), grid=(steps,),
        in_specs=[lhs_spec, _resident((t, D))],
        out_specs=[pl.BlockSpec((per * r, D), lambda q: (q, 0)), ANY],
        out_shape=[jax.ShapeDtypeStruct((n // 2, D), BF)] * 2,
        scratch_shapes=[pltpu.VMEM((2, width, D), BF), pltpu.SemaphoreType.DMA((2, per)),
                        pltpu.SemaphoreType.DMA((n_chips,))],
        compiler_params=_params(1), name=name)


def _wgrad_pair_sum(lhs, rhs, place, name, *, lhs_is_transposed, deps=()):
    t = rhs.shape[0]
    n = lhs.shape[0] if lhs_is_transposed else lhs.shape[1]
    r = n // N_DEV
    n_chips = N_DEV // 2
    per = 1 if (2 * r) % BLK == 0 else 2
    steps = n_chips // per
    n_deps = len(deps)

    def body(place_ref, *refs):
        l_ref, r_ref, part_ref, land_ref, res, inbox, send_sems, recv_sems = refs[n_deps:]
        q = pl.program_id(0)
        slot = q % 2
        x, y, c = _position()

        def send(step, buf, i):
            return pltpu.make_async_remote_copy(
                src_ref=res.at[buf, pl.ds(pl.multiple_of((2 * i + 1 - c) * r, 16), r), :], dst_ref=inbox.at[step * per + i],
                send_sem=send_sems.at[buf, i], recv_sem=recv_sems.at[step * per + i],
                device_id=(x, y, 1 - c), device_id_type=MESH_ID)

        @pl.when(q < steps)
        def _():
            @pl.when(q >= 2)
            def _():
                for i in range(per):
                    send(q - 2, slot, i).wait_send()

            if lhs_is_transposed:
                res[slot] = _dot(l_ref[...], r_ref[...]).astype(BF)
            else:
                res[slot] = _dot_tn(l_ref[...], r_ref[...]).astype(BF)
            for i in range(per):
                send(q, slot, i).start()

        @pl.when(q >= 1)
        def _():
            for i in range(per):
                chip = (q - 1) * per + i
                send(q - 1, 1 - slot, i).wait_recv()
                kept = res[1 - slot, pl.ds(pl.multiple_of((2 * i + c) * r, 16), r), :]
                total = (kept.astype(F32) + inbox[chip].astype(F32)).astype(BF)
                part_ref[i * r:(i + 1) * r, :] = total

                @pl.when(chip == place_ref[1])
                def _():
                    land_ref[...] = total

        @pl.when(q == steps)
        def _():
            for i in range(per):
                if steps > 1:
                    send(q - 2, slot, i).wait_send()
                send(q - 1, 1 - slot, i).wait_send()

    width = 2 * r * per
    last = steps - 1
    if lhs_is_transposed:
        lhs_spec = pl.BlockSpec((width, t), lambda q, p: (jnp.minimum(q, last), 0))
    else:
        lhs_spec = pl.BlockSpec((t, width), lambda q, p: (0, jnp.minimum(q, last)))
    return pl.pallas_call(
        body,
        grid_spec=pltpu.PrefetchScalarGridSpec(
            num_scalar_prefetch=1, grid=(steps + 1,),
            in_specs=[ANY] * n_deps + [lhs_spec, pl.BlockSpec((t, D), lambda q, p: (0, 0), pipeline_mode=pl.Buffered(1))],
            out_specs=[pl.BlockSpec((per * r, D), lambda q, p: (jnp.maximum(q - 1, 0), 0)),
                       pl.BlockSpec((r, D), lambda q, p: (p[1], 0))],
            scratch_shapes=[pltpu.VMEM((2, width, D), BF), pltpu.VMEM((n_chips, r, D), BF),
                            pltpu.SemaphoreType.DMA((2, per)), pltpu.SemaphoreType.DMA((n_chips,))]),
        out_shape=[jax.ShapeDtypeStruct((n // 2, D), BF)] * 2,
        compiler_params=_params(1), name=name)(place, *deps, lhs, rhs)


def _wgrad_pair_sum_many(items, place, name, deps=()):
    m = len(items)
    t = items[0][1].shape[0]
    n = items[0][0].shape[0] if items[0][2] else items[0][0].shape[1]
    r = n // N_DEV
    assert (2 * r) % BLK == 0 and r <= WGRAD_SUM_MAX_ROWS
    steps = N_DEV // 2
    chunks = m * steps
    n_deps = len(deps)

    def body(place_ref, *refs):
        refs = refs[n_deps:]
        l_refs, r_first, r_later = refs[:m], refs[m], refs[m + 1:2 * m]
        parts, lands = refs[2 * m:3 * m], refs[3 * m:4 * m]
        res, inbox, r_scr, send_sems, recv_sems, fetch_sems = refs[4 * m:]
        g = pl.program_id(0)
        slot = g % 2
        x, y, c = _position()

        def send(chunk, buf):
            return pltpu.make_async_remote_copy(
                src_ref=res.at[buf, pl.ds(pl.multiple_of((1 - c) * r, 16), r), :], dst_ref=inbox.at[chunk],
                send_sem=send_sems.at[buf], recv_sem=recv_sems.at[chunk], device_id=(x, y, 1 - c), device_id_type=MESH_ID)

        def fetch(k):
            return pltpu.make_async_copy(r_later[k - 1], r_scr.at[k - 1], fetch_sems.at[k - 1])

        @pl.when(g == 0)
        def _():
            for k in range(1, m):
                fetch(k).start()

        @pl.when(g < chunks)
        def _():
            @pl.when(g >= 2)
            def _():
                send(g - 2, slot).wait_send()

            for k, (_, _, transposed) in enumerate(items):
                @pl.when(g // steps == k)
                def _(k=k, transposed=transposed):
                    if k > 0:
                        @pl.when(g == k * steps)
                        def _():
                            fetch(k).wait()
                    rhs = r_first[...] if k == 0 else r_scr[k - 1]
                    res[slot] = (_dot(l_refs[k][...], rhs) if transposed else _dot_tn(l_refs[k][...], rhs)).astype(BF)

            send(g, slot).start()

        @pl.when(g >= 1)
        def _():
            chunk = g - 1
            send(chunk, 1 - slot).wait_recv()
            kept = res[1 - slot, pl.ds(pl.multiple_of(c * r, 16), r), :]
            total = (kept.astype(F32) + inbox[chunk].astype(F32)).astype(BF)
            for k in range(m):
                @pl.when(chunk // steps == k)
                def _(k=k):
                    parts[k][...] = total

                    @pl.when(chunk % steps == place_ref[1])
                    def _():
                        lands[k][...] = total

        @pl.when(g == chunks)
        def _():
            send(g - 2, slot).wait_send()
            send(g - 1, 1 - slot).wait_send()

    def own_steps(k):
        return lambda g: jnp.clip(g - k * steps, 0, steps - 1)

    lhs_specs = []
    for k, (lhs, _, transposed) in enumerate(items):
        at = own_steps(k)
        lhs_specs.append(pl.BlockSpec((2 * r, t), lambda g, p, at=at: (at(g), 0)) if transposed
                         else pl.BlockSpec((t, 2 * r), lambda g, p, at=at: (0, at(g))))
    out = pl.pallas_call(
        body,
        grid_spec=pltpu.PrefetchScalarGridSpec(
            num_scalar_prefetch=1, grid=(chunks + 1,),
            in_specs=[ANY] * n_deps + lhs_specs
            + [pl.BlockSpec((t, D), lambda g, p: (0, 0), pipeline_mode=pl.Buffered(1))] + [ANY] * (m - 1),
            out_specs=[pl.BlockSpec((r, D), lambda g, p, at=own_steps(k): (at(g - 1), 0)) for k in range(m)]
            + [pl.BlockSpec((r, D), lambda g, p: (p[1], 0))] * m,
            scratch_shapes=[pltpu.VMEM((2, 2 * r, D), BF), pltpu.VMEM((chunks, r, D), BF), pltpu.VMEM((m - 1, t, D), BF),
                            pltpu.SemaphoreType.DMA((2,)), pltpu.SemaphoreType.DMA((chunks,)),
                            pltpu.SemaphoreType.DMA((m - 1,))]),
        out_shape=[jax.ShapeDtypeStruct((n // 2, D), BF)] * (2 * m),
        compiler_params=_params(1), name=name)(place, *deps, *[i[0] for i in items], *[i[1] for i in items])
    return [(out[k], out[m + k]) for k in range(m)]


def _pair_add(kept, received, place, name):
    r = received.shape[0] // 4

    def body(place_ref, g_ref, r_ref, o_ref, land_ref):
        total = (g_ref[...].astype(F32) + r_ref[...].astype(F32)).astype(BF)
        o_ref[...] = total

        @pl.when(pl.program_id(0) == place_ref[1])
        def _():
            land_ref[...] = total

    return pl.pallas_call(
        body,
        grid_spec=pltpu.PrefetchScalarGridSpec(
            num_scalar_prefetch=1, grid=(4,),
            in_specs=[pl.BlockSpec((r, D), lambda q, p: (q, 0))] * 2,
            out_specs=[pl.BlockSpec((r, D), lambda q, p: (q, 0)), pl.BlockSpec((r, D), lambda q, p: (p[1], 0))]),
        out_shape=[jax.ShapeDtypeStruct(received.shape, BF)] * 2,
        compiler_params=_params(1), name=name)(place, kept, received)


def _sum_blocks(gathered, rows):
    def body(b_ref, o_ref):
        acc = b_ref[0:rows, :]
        for d in range(1, N_DEV):
            acc = acc + b_ref[d * rows:(d + 1) * rows, :]
        o_ref[...] = acc

    return pl.pallas_call(body, out_shape=jax.ShapeDtypeStruct((rows, D), F32), name="small_sum")(gathered)


def _adamw_math(w, g, m, v):
    m = ADAM_B1 * m + (1.0 - ADAM_B1) * g
    v = ADAM_B2 * v + (1.0 - ADAM_B2) * (g * g)
    m_hat = m / (1.0 - ADAM_B1 ** ADAM_STEP)
    v_hat = v / (1.0 - ADAM_B2 ** ADAM_STEP)
    delta = -ADAM_LR * (m_hat / (jnp.sqrt(v_hat) + ADAM_EPS) + ADAM_WD * w)
    return delta, m, v


def _sum_partials(blocks):
    g = blocks[0].astype(F32)
    for blk in blocks[1:]:
        g = g + blk.astype(F32)
    return g


ADAMW_MAX_ROWS = 352


def _reduce_adamw(items, name):
    n = len(items)
    per = -(-max(w.shape[0] for _, w, _, _ in items) // ADAMW_MAX_ROWS)

    def body(*refs):
        for k in range(n):
            r0, r1, r2, r3, w_ref, m_ref, v_ref = refs[7 * k:7 * k + 7]
            g_ref, d_ref, nm_ref, nv_ref = refs[7 * n + 4 * k:7 * n + 4 * k + 4]
            g = _sum_partials([r0[...], r1[...], r2[...], r3[...]])
            g_ref[...] = g
            d_ref[...], nm_ref[...], nv_ref[...] = _adamw_math(w_ref[...], g, m_ref[...], v_ref[...])

    in_specs, out_specs, out_shape, args = [], [], [], []
    for landed, w, m, v in items:
        tr = w.shape[0] // per
        assert tr * per == w.shape[0] and tr % 16 == 0
        tile = _row_tile(tr, D)
        in_specs += [pl.BlockSpec((tr, D), lambda i, q=q: (q * per + i, 0)) for q in range(4)] + [tile] * 3
        out_specs += [tile] * 4
        out_shape += [jax.ShapeDtypeStruct(w.shape, F32)] * 4
        args += [landed] * 4 + [w, m, v]
    out = pl.pallas_call(body, grid=(per,), in_specs=in_specs, out_specs=out_specs, out_shape=out_shape,
                         compiler_params=_params(1), name=name)(*args)
    return [out[4 * k:4 * k + 4] for k in range(n)]


def _adamw_small(w, g, m, v, name):
    def body(w_ref, g_ref, m_ref, v_ref, d_ref, nm_ref, nv_ref):
        d_ref[...], nm_ref[...], nv_ref[...] = _adamw_math(w_ref[...], g_ref[...], m_ref[...], v_ref[...])

    return pl.pallas_call(body, out_shape=[jax.ShapeDtypeStruct(w.shape, F32)] * 3, name=name)(w, g, m, v)


WEIGHTS = ("ffn1_norm", "ffn1_w_in", "ffn1_w_out", "mix_norm", "w_in", "conv_dw_kernel", "conv_dw_bias", "conv_ln_g",
           "conv_ln_b", "conv_w_proj", "q_norm", "k_norm", "attn_sinks", "rel_bias", "attn_w_o", "w_out", "ffn2_norm",
           "ffn2_w_in", "ffn2_w_out")
MATRICES = ("ffn1_w_in", "ffn1_w_out", "w_in", "conv_w_proj", "attn_w_o", "w_out", "ffn2_w_in", "ffn2_w_out")
COLUMN_SHARDED = ("ffn1_w_in", "w_in", "ffn2_w_in")
ROW_VECTORS = ("ffn1_norm", "mix_norm", "conv_dw_bias", "conv_ln_g", "conv_ln_b", "ffn2_norm")
PACKED = (("q_norm", HD), ("k_norm", HD), ("attn_sinks", NQ), ("rel_bias", NBUCKET * NQ))
GATHER = _Exchange(gather=True)
GATHER_ALL = _Exchange(gather=True, all_cores=True)
SCATTER = _Exchange(gather=False)
FIRST = "ffn1_w_in"
GATHER_STAGES = ("ffn1_out", "mix_proj", "mix_merge", "ffn2")
STAGE_GATHER = {"ffn1_out": GATHER, "mix_proj": GATHER, "mix_merge": GATHER, "ffn2": GATHER_ALL}
STAGE_MEMBERS = {"ffn1_out": ("ffn1_w_out",),
                 "mix_proj": ("w_in", "taps"), "mix_merge": ("conv_w_proj", "attn_w_o", "w_out"),
                 "ffn2": ("ffn2_w_in", "ffn2_w_out")}
ROW_PACKED = len(ROW_VECTORS)
ROW_LOSS = ROW_PACKED + 1
ROW_TAPS = 8
PAYLOAD_ROWS = 48


def _pack_small(values, last_row):
    packed = jnp.concatenate([values[k].reshape(-1) for k, _ in PACKED])
    packed = jnp.pad(packed, (0, D - packed.shape[0])).reshape(1, D)
    return jnp.concatenate([values[k].reshape(1, D) for k in ROW_VECTORS] + [packed, last_row], axis=0)


def _unpack_small(rows):
    out = {k: rows[i] for i, k in enumerate(ROW_VECTORS)}
    at = 0
    for k, size in PACKED:
        out[k] = rows[ROW_PACKED, at:at + size]
        at += size
    out["rel_bias"] = out["rel_bias"].reshape(NBUCKET, NQ)
    return out


def kernel(x, ffn1_norm, ffn1_w_in, ffn1_w_out, mix_norm, w_in, conv_dw_kernel, conv_dw_bias, conv_ln_g, conv_ln_b, conv_w_proj, q_norm, k_norm, attn_sinks, rel_bias, attn_w_o, w_out, ffn2_norm, ffn2_w_in, ffn2_w_out, loss_target, m_ffn1_norm, m_ffn1_w_in, m_ffn1_w_out, m_mix_norm, m_w_in, m_conv_dw_kernel, m_conv_dw_bias, m_conv_ln_g, m_conv_ln_b, m_conv_w_proj, m_q_norm, m_k_norm, m_attn_sinks, m_rel_bias, m_attn_w_o, m_w_out, m_ffn2_norm, m_ffn2_w_in, m_ffn2_w_out, v_ffn1_norm, v_ffn1_w_in, v_ffn1_w_out, v_mix_norm, v_w_in, v_conv_dw_kernel, v_conv_dw_bias, v_conv_ln_g, v_conv_ln_b, v_conv_w_proj, v_q_norm, v_k_norm, v_attn_sinks, v_rel_bias, v_attn_w_o, v_w_out, v_ffn2_norm, v_ffn2_w_in, v_ffn2_w_out):
    w = dict(ffn1_norm=ffn1_norm, ffn1_w_in=ffn1_w_in, ffn1_w_out=ffn1_w_out, mix_norm=mix_norm, w_in=w_in,
             conv_dw_kernel=conv_dw_kernel, conv_dw_bias=conv_dw_bias, conv_ln_g=conv_ln_g, conv_ln_b=conv_ln_b,
             conv_w_proj=conv_w_proj, q_norm=q_norm, k_norm=k_norm, attn_sinks=attn_sinks, rel_bias=rel_bias,
             attn_w_o=attn_w_o, w_out=w_out, ffn2_norm=ffn2_norm, ffn2_w_in=ffn2_w_in, ffn2_w_out=ffn2_w_out)
    m = dict(ffn1_norm=m_ffn1_norm, ffn1_w_in=m_ffn1_w_in, ffn1_w_out=m_ffn1_w_out, mix_norm=m_mix_norm, w_in=m_w_in,
             conv_dw_kernel=m_conv_dw_kernel, conv_dw_bias=m_conv_dw_bias, conv_ln_g=m_conv_ln_g, conv_ln_b=m_conv_ln_b,
             conv_w_proj=m_conv_w_proj, q_norm=m_q_norm, k_norm=m_k_norm, attn_sinks=m_attn_sinks, rel_bias=m_rel_bias,
             attn_w_o=m_attn_w_o, w_out=m_w_out, ffn2_norm=m_ffn2_norm, ffn2_w_in=m_ffn2_w_in, ffn2_w_out=m_ffn2_w_out)
    v = dict(ffn1_norm=v_ffn1_norm, ffn1_w_in=v_ffn1_w_in, ffn1_w_out=v_ffn1_w_out, mix_norm=v_mix_norm, w_in=v_w_in,
             conv_dw_kernel=v_conv_dw_kernel, conv_dw_bias=v_conv_dw_bias, conv_ln_g=v_conv_ln_g, conv_ln_b=v_conv_ln_b,
             conv_w_proj=v_conv_w_proj, q_norm=v_q_norm, k_norm=v_k_norm, attn_sinks=v_attn_sinks, rel_bias=v_rel_bias,
             attn_w_o=v_attn_w_o, w_out=v_w_out, ffn2_norm=v_ffn2_norm, ffn2_w_in=v_ffn2_w_in, ffn2_w_out=v_ffn2_w_out)
    px, py, pc = _position()
    me = 4 * px + 2 * py + pc
    place = jnp.stack([pc, 2 * px + py]).astype(jnp.int32)

    rows_of = lambda k, a: a.T if k in COLUMN_SHARDED else a
    me1 = me.astype(jnp.int32).reshape(1)
    rest = tuple(k for k in MATRICES if k != FIRST)
    shard_rows = dict({k: rows_of(k, w[k]).shape[0] for k in MATRICES}, taps=CWP)
    sems_first, _, thru_first, token = _ici_copies_start(
        [[(0, shard_rows[FIRST])]], None, _prep([rows_of(FIRST, w[FIRST])], None, me1, "prep_first"), [GATHER],
        "gather_start_first")
    buffers = dict(zip(rest + ("taps",), _prep([rows_of(k, w[k]) for k in rest], conv_dw_kernel, me1, "prep", deps=[token])))
    landings, sets = [], []
    for stage in GATHER_STAGES:
        sets.append([(len(landings) + i, shard_rows[k]) for i, k in enumerate(STAGE_MEMBERS[stage])])
        landings += list(STAGE_MEMBERS[stage])
    sems, _, land_thru, started = _ici_copies_start(sets, None, [buffers[k] for k in landings],
                                                    [STAGE_GATHER[s] for s in GATHER_STAGES], "gather_start")

    packed = [_pack_small(a, jnp.zeros((1, D), F32)) for a in (w, m, v)]

    def ffn1_up(x, g, after):
        chips = jnp.stack([_chip_index(chip) for chip in [(px, py)] + _other_chips(px, py)]).astype(jnp.int32)
        rows = [shard_rows[FIRST]]
        mine = _d2d_gather(thru_first, rows, "gather_d2d_first_mine", which=(0,), deps=[started])
        n, u = _ffn_up_blocks(x, g, None, mine[0], chips[:1], None, "ffn1_up_mine")
        landed = _ici_copies_wait(sems_first[0], rows, None, mine, GATHER, [u, *after, *packed], "gather_wait_first")
        w_in_t, = _d2d_gather(landed, rows, "gather_d2d_first", which=(1, 2, 3))
        n, u = _ffn_up_blocks(None, None, n, w_in_t, chips[1:3], u, "ffn1_up_next")
        (n, u), w1 = weights_of("ffn1_out", (u,), during=functools.partial(
            _ffn_up_blocks, None, None, n, w_in_t, chips[3:], u, "ffn1_up"))
        return n, u, dict(w1, ffn1_w_in=w_in_t)

    def weights_of(stage, after, during=None):
        s = GATHER_STAGES.index(stage)
        rows = [r for _, r in sets[s]]
        landed = _ici_copies_wait(sems[s], rows, None, [land_thru[k] for k, _ in sets[s]], STAGE_GATHER[stage],
                                  list(after), "gather_wait_" + stage)
        if during is not None:
            results, landed = during(swap=(landed, rows))
        elif not STAGE_GATHER[stage].all_cores:
            landed = _d2d_gather(landed, rows, "gather_d2d_" + stage)
        out = dict(zip(STAGE_MEMBERS[stage], landed))
        if "taps" in out:
            taps = out.pop("taps")
            out["conv_dw_kernel"] = jnp.transpose(taps.reshape(N_DEV, CWP, BLK), (1, 0, 2)).reshape(CWP, D)[:CW]
        return out if during is None else (results, out)

    in_flight = []

    def wgrad(lhs, rhs, name, lhs_is_transposed, deps=()):
        rows = (lhs.shape[0] if lhs_is_transposed else lhs.shape[1]) // N_DEV
        if rows <= WGRAD_SUM_MAX_ROWS:
            return ("summed",) + tuple(_wgrad_pair_sum(lhs, rhs, place, name, lhs_is_transposed=lhs_is_transposed, deps=deps))
        return ("paired",) + tuple(_wgrad_pair(lhs, rhs, name, lhs_is_transposed=lhs_is_transposed, deps=deps))

    def wgrads(items, name):
        return [("summed",) + pair for pair in _wgrad_pair_sum_many(items, place, name)]

    def grads_done(stage, grads):
        names = list(grads)
        added = []
        for k in names:
            if not isinstance(grads[k], tuple):
                added.append(_pair_exchange_add(grads[k], place, "pair_add_" + k))
            elif grads[k][0] == "paired":
                added.append(_pair_add(grads[k][1], grads[k][2], place, "pair_add_" + k))
            else:
                added.append(grads[k][1:])
        partials = [p for p, _ in added]
        members = [(i, p.shape[0] // 4) for i, p in enumerate(partials)]
        sem, p_thru, l_thru, token = _ici_copies_start([members], partials, [l for _, l in added], [SCATTER],
                                                       "scatter_start_" + stage)
        in_flight.append((stage, names, sem[0], p_thru, l_thru, token))
        return [token]

    small = []

    def small_done(gv, sq):
        payload = jnp.concatenate([_pack_small(gv, sq), jnp.pad(gv["conv_dw_kernel"], ((0, PAYLOAD_ROWS - ROW_TAPS - CW), (0, 0)))],
                                  axis=0)
        mine = lax.dynamic_update_slice_in_dim(lax.empty((N_DEV * PAYLOAD_ROWS, D), F32), payload, me * PAYLOAD_ROWS, axis=0)
        sems, _, thru, token = _ici_copies_start([[(0, PAYLOAD_ROWS)]], None, [mine], [GATHER_ALL], "small_start")
        small.append((sems[0], thru))
        return [token]

    vec = {k: w[k] for k in WEIGHTS if k not in MATRICES and k != "conv_dw_kernel"}
    dx0 = _local_step(x[0], loss_target[0], vec, ffn1_up, weights_of, wgrad, wgrads, grads_done, small_done)
    gathered, = _ici_copies_wait(small[0][0], [PAYLOAD_ROWS], None, small[0][1], GATHER_ALL, [in_flight[-1][-1]], "small_wait")
    total = _sum_blocks(gathered, PAYLOAD_ROWS)
    loss = (0.5 / D) * jnp.sum(total[ROW_LOSS])

    grads, delta, new_m, new_v = {}, {}, {}, {}
    after, pending = [total], []
    for stage, names, sem, p_thru, l_thru, _ in in_flight:
        landed = _ici_copies_wait(sem, [p.shape[0] // 4 for p in p_thru], p_thru, l_thru, SCATTER, after,
                                  "scatter_wait_" + stage)
        pending += zip(names, landed)
        after = list(landed)
        if stage == in_flight[-2][0]:
            continue
        outs = _reduce_adamw([(buf, rows_of(k, w[k]), rows_of(k, m[k]), rows_of(k, v[k])) for k, buf in pending],
                             "adamw_" + stage)
        for (k, _), out in zip(pending, outs):
            grads[k], delta[k], new_m[k], new_v[k] = [rows_of(k, a) for a in out]
        after, pending = [out[1] for out in outs], []
    d8, m8, v8 = _adamw_small(packed[0], total[:ROW_TAPS], packed[1], packed[2], "adamw_small")
    grads.update(_unpack_small(total[:ROW_TAPS]))
    delta.update(_unpack_small(d8))
    new_m.update(_unpack_small(m8))
    new_v.update(_unpack_small(v8))
    k = "conv_dw_kernel"
    grads[k] = lax.dynamic_slice_in_dim(total[ROW_TAPS:ROW_TAPS + CW], me * BLK, BLK, axis=1)
    delta[k], new_m[k], new_v[k] = _adamw_small(w[k], grads[k], m[k], v[k], "adamw_taps")

    return (loss, dx0[None], *[grads[k] for k in WEIGHTS], *[delta[k] for k in WEIGHTS],
            *[new_m[k] for k in WEIGHTS], *[new_v[k] for k in WEIGHTS])
```

```python
import functools
import math

import numpy as np
import jax
import jax.numpy as jnp
from jax import lax
from jax.experimental import pallas as pl
from jax.experimental.pallas import tpu as pltpu

F32 = jnp.float32
BF = jnp.bfloat16

D = 1024
F = 2816
INW = 5632
CW = 31
CWP = 32
HD = 64
NQ = 16
NKV = 4
GRP = NQ // NKV
BLK = 128
NBUCKET = 32
EPS = 1e-6
NEG = float(jnp.finfo(jnp.float32).min)
QK_SCALE = 1.0 / math.sqrt(HD)
R_CONV = (0, 2048)
R_QKV = (2048, 3584)
R_Q = (2048, 3072)
R_KV = (3072, 3584)
R_GATE = (3584, 5632)

N_DEV = 8
VMEM_LIMIT_V7X = 56 * 1024 * 1024
ROW_TILE = 256
ROW_TILE_WIDE = 512
ROW_TILE_BLOCK = 1024
WGRAD_SUM_MAX_ROWS = 352

ADAM_LR = 0.001
ADAM_B1 = 0.9
ADAM_B2 = 0.999
ADAM_EPS = 1e-08
ADAM_WD = 0.01
ADAM_STEP = 10

NT_DIMS = (((1,), (1,)), ((), ()))
TN_DIMS = (((0,), (0,)), ((), ()))


def _dot(a, b):
    return jnp.dot(a, b, preferred_element_type=F32)


def _dot_nt(a, b):
    return lax.dot_general(a, b, NT_DIMS, preferred_element_type=F32)


def _dot_tn(a, b):
    return lax.dot_general(a, b, TN_DIMS, preferred_element_type=F32)


def _sig(x):
    return 0.5 * jnp.tanh(0.5 * x) + 0.5


ANY = pl.BlockSpec(memory_space=pl.ANY)


def _call(body, deps, args, **kw):
    n = len(deps)
    if n:
        kw["in_specs"] = [ANY] * n + list(kw["in_specs"])
        return pl.pallas_call(lambda *refs: body(*refs[n:]), **kw)(*deps, *args)
    return pl.pallas_call(body, **kw)(*args)


def _params(n_axes):
    return pltpu.CompilerParams(dimension_semantics=("arbitrary",) * n_axes, vmem_limit_bytes=VMEM_LIMIT_V7X)


def _resident(shape):
    zeros = (0,) * len(shape)
    return pl.BlockSpec(shape, lambda *_: zeros, pipeline_mode=pl.Buffered(1))


def _row_tile(rows, cols):
    return pl.BlockSpec((rows, cols), lambda i: (i, 0))


def _rms_stats(x):
    r = lax.rsqrt(jnp.mean(x * x, axis=-1, keepdims=True) + EPS)
    return r, x * r


def _rms_bwd(dn, x, g):
    r, xh = _rms_stats(x)
    dxh = dn * g
    dx = r * (dxh - xh * jnp.mean(dxh * xh, axis=-1, keepdims=True))
    return dx, jnp.sum(dn * xh, axis=0, keepdims=True)


def _ffn_last(x, target, g, w_in_t, w_out, name):
    t = x.shape[0]
    tm = min(ROW_TILE, t)

    def body(x_ref, t_ref, g_ref, w_ref, wo_ref, n_ref, du_ref, h_ref, dy_ref, dx_ref, sq_ref, dg_ref):
        @pl.when(pl.program_id(0) == 0)
        def _():
            sq_ref[...] = jnp.zeros_like(sq_ref)
            dg_ref[...] = jnp.zeros_like(dg_ref)

        x = x_ref[...]
        g = g_ref[...]
        r, xh = _rms_stats(x)
        n = (xh * g).astype(BF)
        n_ref[...] = n
        u = _dot_nt(n, w_ref[...])
        a = u[:, :F]
        b = u[:, F:]
        s = _sig(a)
        sa = a * s
        h = (sa * b).astype(BF)
        h_ref[...] = h
        err = x + 0.5 * _dot(h, wo_ref[...]) - t_ref[...]
        sq_ref[...] += jnp.sum(err * err, axis=0, keepdims=True)
        dxo = err * (1.0 / D)
        dy = (0.5 * dxo).astype(BF)
        dy_ref[...] = dy
        dh = _dot_nt(dy, wo_ref[...])
        du_ref[:, :F] = (dh * b * (s * (1.0 + a * (1.0 - s)))).astype(BF)
        du_ref[:, F:] = (dh * sa).astype(BF)
        dn = _dot(du_ref[...], w_ref[...])
        dxh = dn * g
        dx_ref[...] = dxo + r * (dxh - xh * jnp.mean(dxh * xh, axis=-1, keepdims=True))
        dg_ref[...] += jnp.sum(dn * xh, axis=0, keepdims=True)

    vec = pl.BlockSpec((1, D), lambda i: (0, 0))
    return pl.pallas_call(
        body, grid=(t // tm,),
        in_specs=[_row_tile(tm, D), _row_tile(tm, D), _resident((1, D)), _resident((INW, D)), _resident((F, D))],
        out_specs=[_row_tile(tm, D), _row_tile(tm, INW), _row_tile(tm, F), _row_tile(tm, D), _row_tile(tm, D), vec, vec],
        out_shape=[jax.ShapeDtypeStruct((t, D), BF), jax.ShapeDtypeStruct((t, INW), BF), jax.ShapeDtypeStruct((t, F), BF),
                   jax.ShapeDtypeStruct((t, D), BF), jax.ShapeDtypeStruct((t, D), F32), jax.ShapeDtypeStruct((1, D), F32),
                   jax.ShapeDtypeStruct((1, D), F32)],
        compiler_params=_params(1), name=name)(x, target, g, w_in_t, w_out)


def _ffn_up_blocks(x, g, n, w_in_t, order, u, name, deps=(), swap=None):
    t = (x if n is None else n).shape[0]
    tm = min(ROW_TILE_BLOCK, t)
    c = INW * 2 // N_DEV
    n_deps = len(deps)
    first = n is None
    assert not first or order.shape == (1,)

    def body(order_ref, *refs):
        refs = refs[n_deps:]
        if first:
            x_ref, g_ref, w_ref, n_ref, u_ref = refs
            nt = (_rms_stats(x_ref[...])[1] * g_ref[...]).astype(BF)
            n_ref[...] = nt
        else:
            n_ref, w_ref, _, u_ref = refs
            nt = n_ref[...]
        u_ref[...] = _dot_nt(nt, w_ref[...]).astype(BF)

    rows = pl.BlockSpec((tm, D), lambda k, i, o: (i, 0))
    block = pl.BlockSpec((c, D), lambda k, i, o: (o[k], 0))
    cols = pl.BlockSpec((tm, c), lambda k, i, o: (i, o[k]))
    u_shape = jax.ShapeDtypeStruct((t, INW), BF)
    if first:
        args, in_specs = (x, g, w_in_t), [rows, _resident((1, D)), block]
        out_specs, out_shape, aliases = [rows, cols], [jax.ShapeDtypeStruct((t, D), BF), u_shape], {}
    else:
        args, in_specs = (n, w_in_t, u), [rows, block, ANY]
        out_specs, out_shape, aliases = [cols], [u_shape], {1 + n_deps + 2: 0}
    grid = (order.shape[0], t // tm)
    if swap is not None:
        (out,), swapped = _call_with_swap(
            body, (*deps, *args), swap, prefetch=(order,), grid=grid, in_specs=[ANY] * n_deps + in_specs, out_specs=out_specs,
            out_shape=out_shape, scratch_shapes=[], input_output_aliases={n_deps + 2: 0}, compiler_params=_params(2), name=name)
        return (n, out), swapped
    out = pl.pallas_call(
        body,
        grid_spec=pltpu.PrefetchScalarGridSpec(num_scalar_prefetch=1, grid=grid, in_specs=[ANY] * n_deps + in_specs,
                                               out_specs=out_specs),
        out_shape=out_shape, input_output_aliases=aliases, compiler_params=_params(2), name=name)(order, *deps, *args)
    return tuple(out) if first else (n, out[0])


def _ffn_down(x, u, w_out, name, part=(0, 1), out=None, swap=None):
    t = x.shape[0]
    tm = min(ROW_TILE_WIDE, t)
    steps = t // tm // part[1]
    first = part[0] * steps
    others = [out] if out is not None else []

    def body(x_ref, u_ref, wo_ref, *rest):
        a = u_ref[:, :F].astype(F32)
        b = u_ref[:, F:].astype(F32)
        h = (a * _sig(a) * b).astype(BF)
        rest[-1][...] = x_ref[...] + 0.5 * _dot(h, wo_ref[...])

    tile = lambda cols: pl.BlockSpec((tm, cols), lambda i: (first + i, 0))
    kw = dict(grid=(steps,), in_specs=[tile(D), tile(INW), _resident((F, D))] + [ANY] * len(others), out_specs=[tile(D)],
              out_shape=[jax.ShapeDtypeStruct((t, D), F32)], scratch_shapes=[],
              input_output_aliases={3: 0} if others else {}, compiler_params=_params(1), name=name)
    args = (x, u, w_out, *others)
    return pl.pallas_call(body, **kw)(*args) if swap is None else _call_with_swap(body, args, swap, **kw)


def _ffn_bwd(dxo, x, g, u, w_in_t, w_out, name, deps=()):
    t = x.shape[0]
    tm = min(ROW_TILE, t)

    def body(dxo_ref, x_ref, g_ref, u_ref, w_ref, wo_ref, dx_ref, du_ref, h_ref, dy_ref, dg_ref):
        dxo = dxo_ref[...]
        dy = (0.5 * dxo).astype(BF)
        dy_ref[...] = dy
        dh = _dot_nt(dy, wo_ref[...])
        a = u_ref[:, :F].astype(F32)
        b = u_ref[:, F:].astype(F32)
        s = _sig(a)
        sa = a * s
        h_ref[...] = (sa * b).astype(BF)
        du_ref[:, :F] = (dh * b * (s * (1.0 + a * (1.0 - s)))).astype(BF)
        du_ref[:, F:] = (dh * sa).astype(BF)
        dn = _dot(du_ref[...], w_ref[...])
        dx, dg = _rms_bwd(dn, x_ref[...], g_ref[...])
        dx_ref[...] = dxo + dx

        @pl.when(pl.program_id(0) == 0)
        def _():
            dg_ref[...] = jnp.zeros_like(dg_ref)

        dg_ref[...] += dg

    return _call(
        body, deps, (dxo, x, g, u, w_in_t, w_out), grid=(t // tm,),
        in_specs=[_row_tile(tm, D), _row_tile(tm, D), _resident((1, D)), _row_tile(tm, INW), _resident((INW, D)),
                  _resident((F, D))],
        out_specs=[_row_tile(tm, D), _row_tile(tm, INW), _row_tile(tm, F), _row_tile(tm, D),
                   pl.BlockSpec((1, D), lambda i: (0, 0))],
        out_shape=[jax.ShapeDtypeStruct((t, D), F32), jax.ShapeDtypeStruct((t, INW), BF), jax.ShapeDtypeStruct((t, F), BF),
                   jax.ShapeDtypeStruct((t, D), BF), jax.ShapeDtypeStruct((1, D), F32)],
        compiler_params=_params(1), name=name)


def _wgrad(lhs, rhs, name, *, lhs_is_transposed, chunk, deps=()):
    t = rhs.shape[0]
    n = lhs.shape[0] if lhs_is_transposed else lhs.shape[1]
    c = min(chunk, n)

    def body(l_ref, r_ref, o_ref):
        if lhs_is_transposed:
            o_ref[...] = _dot(l_ref[...], r_ref[...]).astype(BF)
        else:
            o_ref[...] = _dot_tn(l_ref[...], r_ref[...]).astype(BF)

    lhs_spec = pl.BlockSpec((c, t), lambda j: (j, 0)) if lhs_is_transposed else pl.BlockSpec((t, c), lambda j: (0, j))
    return _call(
        body, deps, (lhs, rhs), grid=(n // c,),
        in_specs=[lhs_spec, _resident((t, D))],
        out_specs=pl.BlockSpec((c, D), lambda j: (j, 0)),
        out_shape=jax.ShapeDtypeStruct((n, D), BF),
        compiler_params=_params(1), name=name)


def _wgrad_mix(duc, dq_t, dkv_t, dgp, hm):
    t = hm.shape[0]
    c = 512
    first_q, first_kv, first_gate = R_Q[0] // c, R_KV[0] // c, R_GATE[0] // c

    def body(uc_ref, q_ref, kv_ref, gp_ref, h_ref, o_ref):
        j = pl.program_id(0)

        @pl.when(j < first_q)
        def _():
            o_ref[...] = _dot_tn(uc_ref[...], h_ref[...]).astype(BF)

        @pl.when((j >= first_q) & (j < first_kv))
        def _():
            o_ref[...] = _dot(q_ref[...], h_ref[...]).astype(BF)

        @pl.when((j >= first_kv) & (j < first_gate))
        def _():
            o_ref[...] = _dot(kv_ref[...], h_ref[...]).astype(BF)

        @pl.when(j >= first_gate)
        def _():
            o_ref[...] = _dot_tn(gp_ref[...], h_ref[...]).astype(BF)

    return pl.pallas_call(
        body, grid=(INW // c,),
        in_specs=[pl.BlockSpec((t, c), lambda j: (0, jnp.clip(j, 0, first_q - 1))),
                  pl.BlockSpec((c, t), lambda j: (jnp.clip(j - first_q, 0, first_kv - first_q - 1), 0)),
                  pl.BlockSpec((c, t), lambda j: (jnp.clip(j - first_kv, 0, first_gate - first_kv - 1), 0)),
                  pl.BlockSpec((t, c), lambda j: (0, jnp.clip(j - first_gate, 0, INW // c - first_gate - 1))),
                  _resident((t, D))],
        out_specs=pl.BlockSpec((c, D), lambda j: (j, 0)),
        out_shape=jax.ShapeDtypeStruct((INW, D), BF),
        compiler_params=_params(1), name="mix_dw_in")(duc, dq_t, dkv_t, dgp, hm)


def _mix_proj(x, g, w_t):
    t = x.shape[0]
    tm = min(ROW_TILE_WIDE, t)

    def body(x_ref, g_ref, w_ref, hm_ref, uc_ref, gp_ref, qkv_ref):
        r, xh = _rms_stats(x_ref[...])
        hm = (xh * g_ref[...]).astype(BF)
        hm_ref[...] = hm
        uc_ref[...] = _dot_nt(hm, w_ref[R_CONV[0]:R_CONV[1], :]).astype(BF)
        gp_ref[...] = _dot_nt(hm, w_ref[R_GATE[0]:R_GATE[1], :]).astype(BF)
        qkv_ref[...] = _dot_nt(w_ref[R_QKV[0]:R_QKV[1], :], hm).astype(BF)

    return pl.pallas_call(
        body, grid=(t // tm,),
        in_specs=[_row_tile(tm, D), _resident((1, D)), _resident((INW, D))],
        out_specs=[_row_tile(tm, D), _row_tile(tm, 2 * D), _row_tile(tm, 2 * D), pl.BlockSpec((1536, tm), lambda i: (0, i))],
        out_shape=[jax.ShapeDtypeStruct((t, D), BF), jax.ShapeDtypeStruct((t, 2 * D), BF),
                   jax.ShapeDtypeStruct((t, 2 * D), BF), jax.ShapeDtypeStruct((1536, t), BF)],
        compiler_params=_params(1), name="mix_proj")(x, g, w_t)


CONV_HALO = 32
CONV_LEAD = CONV_HALO - (CW - 1)


def _glu(uc):
    uc = uc.astype(F32)
    return uc[:, :D] * _sig(uc[:, D:])


def _ln_stats(zc):
    mu = jnp.mean(zc, axis=-1, keepdims=True)
    zm = zc - mu
    r = lax.rsqrt(jnp.mean(zm * zm, axis=-1, keepdims=True) + EPS)
    return r, zm * r


CONV_SHIFTS = 8
CONV_CHUNK = 32


def _store_shifted(buf, rows):
    for b in range(1, CONV_SHIFTS):
        buf[b, 0:rows - 8, :] = buf[0, pl.ds(b, rows - 8), :]


def _conv_fwd(uc, dwk, dwb, lng, lnb, swap=None):
    t = uc.shape[0]
    tm = min(512, t)
    per = tm // CONV_HALO
    ext = tm + CONV_HALO

    def body(cur_ref, prev_ref, k_ref, kb_ref, g_ref, b_ref, o_ref, zc_ref, zsh):
        i = pl.program_id(0)
        zsh[0, 0:CONV_HALO, :] = _glu(prev_ref[...]) * (i > 0).astype(F32)
        zsh[0, CONV_HALO:, :] = _glu(cur_ref[...])
        _store_shifted(zsh, ext)

        def chunk(ci, carry):
            r0 = pl.multiple_of(ci * CONV_CHUNK, CONV_CHUNK)
            acc = jnp.zeros((CONV_CHUNK, D), F32) + kb_ref[...]
            for w in range(CW):
                a, b = divmod(CONV_LEAD + w, 8)
                acc = acc + k_ref[w:w + 1, :] * zsh[b, pl.ds(r0 + 8 * a, CONV_CHUNK), :]
            zc_ref[pl.ds(r0, CONV_CHUNK), :] = acc
            return carry

        lax.fori_loop(0, tm // CONV_CHUNK, chunk, 0)
        r, xh = _ln_stats(zc_ref[...])
        y = xh * g_ref[...] + b_ref[...]
        o_ref[...] = (y * _sig(y)).astype(BF)

    kw = dict(
        grid=(t // tm,),
        in_specs=[_row_tile(tm, 2 * D),
                  pl.BlockSpec((CONV_HALO, 2 * D), lambda i: (jnp.maximum(i * per - 1, 0), 0)),
                  _resident((CWP, D)), _resident((1, D)), _resident((1, D)), _resident((1, D))],
        out_specs=[_row_tile(tm, D), _row_tile(tm, D)],
        out_shape=[jax.ShapeDtypeStruct((t, D), BF), jax.ShapeDtypeStruct((t, D), F32)],
        scratch_shapes=[pltpu.VMEM((CONV_SHIFTS, ext, D), F32)],
        compiler_params=_params(1), name="conv_fwd")
    args = (uc, uc, dwk, dwb, lng, lnb)
    return pl.pallas_call(body, **kw)(*args) if swap is None else _call_with_swap(body, args, swap, **kw)


def _conv_bwd(uc, zc, dzs, dwk, lng, lnb):
    t = uc.shape[0]
    tm = min(ROW_TILE_WIDE, t)
    per = tm // CONV_HALO
    n_tiles = t // tm
    ext = tm + CONV_HALO
    last_block = t // CONV_HALO - 1

    def body(cur_ref, zc_ref, zcn_ref, dz_ref, dzn_ref, k_ref, g_ref, b_ref,
             duc_ref, dk_ref, dkb_ref, dg_ref, db_ref, dsh, dk8, z_scr):
        i = pl.program_id(0)

        @pl.when(i == 0)
        def _():
            dk8[...] = jnp.zeros_like(dk8)
            dkb_ref[...] = jnp.zeros_like(dkb_ref)
            dg_ref[...] = jnp.zeros_like(dg_ref)
            db_ref[...] = jnp.zeros_like(db_ref)

        has_next = (i < n_tiles - 1).astype(F32)
        z_scr[...] = _glu(cur_ref[...])
        gain = g_ref[...]

        def ln_silu_bwd(zc, dzs, live):
            r, xh = _ln_stats(zc)
            y = xh * gain + b_ref[...]
            sy = _sig(y)
            dy = dzs * (sy * (1.0 + y * (1.0 - sy))) * live
            dxh = dy * gain
            dzc = r * (dxh - jnp.mean(dxh, axis=-1, keepdims=True) - xh * jnp.mean(dxh * xh, axis=-1, keepdims=True))
            return dzc, dy, xh

        dzc, dy, xh = ln_silu_bwd(zc_ref[...], dz_ref[...], 1.0)
        dsh[0, 0:tm, :] = dzc
        dg_ref[...] += jnp.sum(dy * xh, axis=0, keepdims=True)
        db_ref[...] += jnp.sum(dy, axis=0, keepdims=True)
        dkb_ref[...] += jnp.sum(dzc, axis=0, keepdims=True)
        dsh[0, tm:, :] = ln_silu_bwd(zcn_ref[...], dzn_ref[...], has_next)[0]
        _store_shifted(dsh, ext)

        def chunk(ci, carry):
            r0 = pl.multiple_of(ci * CONV_CHUNK, CONV_CHUNK)
            z_c = z_scr[pl.ds(r0, CONV_CHUNK), :]
            dz = jnp.zeros((CONV_CHUNK, D), F32)
            for w in range(CW):
                a, b = divmod(CW - 1 - w, 8)
                window = dsh[b, pl.ds(r0 + 8 * a, CONV_CHUNK), :]
                dz = dz + k_ref[w:w + 1, :] * window
                prod = z_c * window
                part = prod[0:8, :]
                for j in range(1, CONV_CHUNK // 8):
                    part = part + prod[8 * j:8 * j + 8, :]
                dk8[w] += part
            ucc = cur_ref[pl.ds(r0, CONV_CHUNK), :].astype(F32)
            sg = _sig(ucc[:, D:])
            duc_ref[pl.ds(r0, CONV_CHUNK), 0:D] = (dz * sg).astype(BF)
            duc_ref[pl.ds(r0, CONV_CHUNK), D:2 * D] = (dz * ucc[:, :D] * sg * (1.0 - sg)).astype(BF)
            return carry

        lax.fori_loop(0, tm // CONV_CHUNK, chunk, 0)

        @pl.when(i == n_tiles - 1)
        def _():
            dk_ref[...] = jnp.sum(dk8[...], axis=1)

    vec = pl.BlockSpec((1, D), lambda i: (0, 0))
    next_halo = pl.BlockSpec((CONV_HALO, D), lambda i: (jnp.minimum((i + 1) * per, last_block), 0))
    return pl.pallas_call(
        body, grid=(n_tiles,),
        in_specs=[_row_tile(tm, 2 * D), _row_tile(tm, D), next_halo, _row_tile(tm, D), next_halo,
                  _resident((CWP, D)), _resident((1, D)), _resident((1, D))],
        out_specs=[_row_tile(tm, 2 * D), pl.BlockSpec((CWP, D), lambda i: (0, 0)), vec, vec, vec],
        out_shape=[jax.ShapeDtypeStruct((t, 2 * D), BF), jax.ShapeDtypeStruct((CWP, D), F32),
                   jax.ShapeDtypeStruct((1, D), F32), jax.ShapeDtypeStruct((1, D), F32), jax.ShapeDtypeStruct((1, D), F32)],
        scratch_shapes=[pltpu.VMEM((CONV_SHIFTS, ext, D), F32), pltpu.VMEM((CWP, 8, D), F32), pltpu.VMEM((tm, D), F32)],
        compiler_params=_params(1), name="conv_bwd")(uc, zc, zc, dzs, dzs, dwk, lng, lnb)


def _norm_rows(xt, g):
    r = lax.rsqrt(jnp.mean(xt * xt, axis=0, keepdims=True) + EPS)
    xh = xt * r
    return xh * g, r, xh


ATT_TQ = 1024


def _attn_specs(t, tq):
    per = tq // BLK
    return [pl.BlockSpec((1536, tq), lambda i: (0, i)),
            pl.BlockSpec((512, BLK), lambda i: (2, jnp.maximum(i * per - 1, 0))),
            _resident((HD, 1)), _resident((HD, 1)), _resident((NKV, 1, GRP * BLK)),
            _resident((2, NKV, 2 * BLK, GRP * BLK))]


def _attn_window(hk, sb, qkv_ref, halo_ref, kn_cur, kn_halo):
    v0 = D + NKV * HD + hk * HD
    if sb == 0:
        k_prev = kn_halo[hk]
        v_prev = halo_ref[NKV * HD + hk * HD:NKV * HD + (hk + 1) * HD, :]
    else:
        k_prev = kn_cur[hk][:, (sb - 1) * BLK:sb * BLK]
        v_prev = qkv_ref[v0:v0 + HD, (sb - 1) * BLK:sb * BLK]
    kw = jnp.concatenate([k_prev, kn_cur[hk][:, sb * BLK:(sb + 1) * BLK]], axis=1).astype(BF)
    vw = jnp.concatenate([v_prev, qkv_ref[v0:v0 + HD, sb * BLK:(sb + 1) * BLK]], axis=1)
    return kw, vw


def _attn_probs(kw, qc, bias, sink):
    st = _dot_tn(kw, qc) + bias
    m = jnp.maximum(jnp.max(st, axis=0, keepdims=True), sink)
    p = jnp.exp(st - m)
    e_sink = jnp.exp(sink - m)
    inv = 1.0 / (jnp.sum(p, axis=0, keepdims=True) + e_sink)
    return p * inv, e_sink * inv


def _attn_fwd(qkv_t, qg, kg, sink_rows, bias_t):
    t = qkv_t.shape[1]
    tq = min(ATT_TQ, t)
    n_sub = tq // BLK

    def body(qkv_ref, halo_ref, qg_ref, kg_ref, sink_ref, bias_ref, o_ref, p_ref, ps_ref):
        i = pl.program_id(0)
        first = (i == 0).astype(jnp.int32)
        kgain = kg_ref[...]
        qgain = qg_ref[...]
        kn_cur = [_norm_rows(qkv_ref[D + h * HD:D + (h + 1) * HD, :].astype(F32), kgain)[0] for h in range(NKV)]
        kn_halo = [_norm_rows(halo_ref[h * HD:(h + 1) * HD, :].astype(F32), kgain)[0] for h in range(NKV)]
        for hk in range(NKV):
            for sb in range(n_sub):
                cols = slice(sb * BLK, (sb + 1) * BLK)
                kw, vw = _attn_window(hk, sb, qkv_ref, halo_ref, kn_cur, kn_halo)
                qc = jnp.concatenate(
                    [_norm_rows(qkv_ref[(GRP * hk + g) * HD:(GRP * hk + g + 1) * HD, cols].astype(F32), qgain)[0] * QK_SCALE
                     for g in range(GRP)], axis=1).astype(BF)
                bias = bias_ref[first, hk] if sb == 0 else bias_ref[0, hk]
                p, p_sink = _attn_probs(kw, qc, bias, sink_ref[hk])
                p = p.astype(BF)
                p_ref[sb, hk] = p
                ps_ref[sb, hk] = p_sink
                o = _dot(vw, p)
                for g in range(GRP):
                    head = GRP * hk + g
                    o_ref[head * HD:(head + 1) * HD, cols] = o[:, g * BLK:(g + 1) * BLK].astype(BF)

    return pl.pallas_call(
        body, grid=(t // tq,),
        in_specs=_attn_specs(t, tq),
        out_specs=[pl.BlockSpec((D, tq), lambda i: (0, i)),
                   pl.BlockSpec((n_sub, NKV, 2 * BLK, GRP * BLK), lambda i: (i, 0, 0, 0)),
                   pl.BlockSpec((n_sub, NKV, 1, GRP * BLK), lambda i: (i, 0, 0, 0))],
        out_shape=[jax.ShapeDtypeStruct((D, t), BF), jax.ShapeDtypeStruct((t // BLK, NKV, 2 * BLK, GRP * BLK), BF),
                   jax.ShapeDtypeStruct((t // BLK, NKV, 1, GRP * BLK), F32)],
        compiler_params=_params(1), name="attn_fwd")(qkv_t, qkv_t, qg, kg, sink_rows, bias_t)


def _attn_bwd(qkv_t, do_t, probs, sink_probs, qg, kg, onehot_t, deps=()):
    t = qkv_t.shape[1]
    tq = min(ATT_TQ, t)
    n_sub = tq // BLK
    n_tiles = t // tq

    def body(qkv_ref, halo_ref, do_ref, p_ref, ps_ref, qg_ref, kg_ref, oh_ref,
             dq_ref, ckv_ref, dqg_ref, dsink_ref, dbias_ref, qg_scr, sink_scr, ds_scr):
        i = pl.program_id(0)

        @pl.when(i == 0)
        def _():
            qg_scr[...] = jnp.zeros_like(qg_scr)
            sink_scr[...] = jnp.zeros_like(sink_scr)
            ds_scr[...] = jnp.zeros_like(ds_scr)

        kgain = kg_ref[...]
        qgain = qg_ref[...]
        kn_cur = [_norm_rows(qkv_ref[D + h * HD:D + (h + 1) * HD, :].astype(F32), kgain)[0] for h in range(NKV)]
        kn_halo = [_norm_rows(halo_ref[h * HD:(h + 1) * HD, :].astype(F32), kgain)[0] for h in range(NKV)]
        dqg = jnp.zeros((HD, BLK), F32)
        for hk in range(NKV):
            for sb in range(n_sub):
                cols = slice(sb * BLK, (sb + 1) * BLK)
                kw, vw = _attn_window(hk, sb, qkv_ref, halo_ref, kn_cur, kn_halo)
                qn, qr, qh = [], [], []
                for g in range(GRP):
                    head = GRP * hk + g
                    n_, r_, h_ = _norm_rows(qkv_ref[head * HD:(head + 1) * HD, cols].astype(F32), qgain)
                    qn.append(n_)
                    qr.append(r_)
                    qh.append(h_)
                qc = (jnp.concatenate(qn, axis=1) * QK_SCALE).astype(BF)
                p_bf = p_ref[sb, hk]
                p = p_bf.astype(F32)
                doc = jnp.concatenate([do_ref[(GRP * hk + g) * HD:(GRP * hk + g + 1) * HD, cols] for g in range(GRP)], axis=1)
                dp = _dot_tn(vw, doc)
                delta = jnp.sum(p * dp, axis=0, keepdims=True)
                ds = p * (dp - delta)
                sink_scr[hk] += -(ps_ref[sb, hk] * delta)
                ds_scr[hk] += ds
                dsb = ds.astype(BF)
                dqc = _dot(kw, dsb) * QK_SCALE
                ckv_ref[sb, hk * HD:(hk + 1) * HD, :] = _dot_nt(qc, dsb)
                ckv_ref[sb, NKV * HD + hk * HD:NKV * HD + (hk + 1) * HD, :] = _dot_nt(doc, p_bf)
                for g in range(GRP):
                    head = GRP * hk + g
                    dqn = dqc[:, g * BLK:(g + 1) * BLK]
                    dqh = dqn * qgain
                    dq = qr[g] * (dqh - qh[g] * jnp.mean(dqh * qh[g], axis=0, keepdims=True))
                    dq_ref[head * HD:(head + 1) * HD, cols] = dq.astype(BF)
                    dqg = dqg + dqn * qh[g]
        qg_scr[...] += dqg

        @pl.when(i == n_tiles - 1)
        def _():
            dqg_ref[...] = jnp.sum(qg_scr[...], axis=1, keepdims=True)
            dsink_ref[...] = _group_lane_sums(sink_scr[:, 0, :])

            def bucket(b, carry):
                oh = jnp.concatenate([oh_ref[b]] * GRP, axis=1)
                dbias_ref[b] = _group_lane_sums(jnp.sum(ds_scr[...] * oh[None], axis=1))
                return carry

            lax.fori_loop(0, NBUCKET, bucket, 0)

    return _call(
        body, deps, (qkv_t, qkv_t, do_t, probs, sink_probs, qg, kg, onehot_t), grid=(n_tiles,),
        in_specs=_attn_specs(t, tq)[:2] + [pl.BlockSpec((D, tq), lambda i: (0, i)),
                                           pl.BlockSpec((n_sub, NKV, 2 * BLK, GRP * BLK), lambda i: (i, 0, 0, 0)),
                                           pl.BlockSpec((n_sub, NKV, 1, GRP * BLK), lambda i: (i, 0, 0, 0))]
        + _attn_specs(t, tq)[2:4] + [_resident((NBUCKET, 2 * BLK, BLK))],
        out_specs=[pl.BlockSpec((D, tq), lambda i: (0, i)),
                   pl.BlockSpec((n_sub, 2 * NKV * HD, 2 * BLK), lambda i: (i, 0, 0)),
                   pl.BlockSpec((HD, 1), lambda i: (0, 0)),
                   pl.BlockSpec((NKV, BLK), lambda i: (0, 0)),
                   pl.BlockSpec((NBUCKET, NKV, BLK), lambda i: (0, 0, 0))],
        out_shape=[jax.ShapeDtypeStruct((D, t), BF),
                   jax.ShapeDtypeStruct((t // BLK, 2 * NKV * HD, 2 * BLK), F32),
                   jax.ShapeDtypeStruct((HD, 1), F32),
                   jax.ShapeDtypeStruct((NKV, BLK), F32),
                   jax.ShapeDtypeStruct((NBUCKET, NKV, BLK), F32)],
        scratch_shapes=[pltpu.VMEM((HD, BLK), F32), pltpu.VMEM((NKV, 1, GRP * BLK), F32),
                        pltpu.VMEM((NKV, 2 * BLK, GRP * BLK), F32)],
        compiler_params=_params(1), name="attn_bwd")


def _kv_combine_tile(c_ref, cn_ref, has_next, k_ref, kgain, o_ref):
    rows = NKV * HD
    per = c_ref.shape[0]
    dkg = jnp.zeros((HD, BLK), F32)
    for s in range(per):
        cols = slice(s * BLK, (s + 1) * BLK)
        after = c_ref[s + 1, :, :BLK] if s + 1 < per else cn_ref[0, :, :BLK] * has_next
        d = c_ref[s, :, BLK:] + after
        o_ref[rows:, cols] = d[rows:, :].astype(BF)
        for h in range(NKV):
            _, r, kh = _norm_rows(k_ref[h * HD:(h + 1) * HD, cols].astype(F32), kgain)
            dkn = d[h * HD:(h + 1) * HD, :]
            dkh = dkn * kgain
            o_ref[h * HD:(h + 1) * HD, cols] = (r * (dkh - kh * jnp.mean(dkh * kh, axis=0, keepdims=True))).astype(BF)
            dkg = dkg + dkn * kh
    return dkg


def _group_lane_sums(v):
    lane_group = lax.broadcasted_iota(jnp.int32, (1, GRP * BLK), 1) // BLK
    col = lax.broadcasted_iota(jnp.int32, (1, BLK), 1)
    out = jnp.zeros((v.shape[0], BLK), F32)
    for g in range(GRP):
        s = jnp.sum(jnp.where(lane_group == g, v, 0.0), axis=1, keepdims=True)
        out = jnp.where(col == g, s, out)
    return out


def _mix_out(zs, o_t, gp, x, w_cp, w_o, w_out):
    t = x.shape[0]
    tm = min(ROW_TILE_WIDE, t)

    def body(zs_ref, ot_ref, gp_ref, x_ref, wcp_ref, wo_ref, wout_ref, xo_ref, a_ref, b_ref, m_ref):
        a = _dot(zs_ref[...], wcp_ref[...])
        b = _dot_tn(ot_ref[...], wo_ref[...])
        a_ref[...] = a.astype(BF)
        b_ref[...] = b.astype(BF)
        merged = (_sig(gp_ref[:, :D].astype(F32)) * a + _sig(gp_ref[:, D:].astype(F32)) * b).astype(BF)
        m_ref[...] = merged
        xo_ref[...] = x_ref[...] + _dot(merged, wout_ref[...])

    return pl.pallas_call(
        body, grid=(t // tm,),
        in_specs=[_row_tile(tm, D), pl.BlockSpec((D, tm), lambda i: (0, i)), _row_tile(tm, 2 * D), _row_tile(tm, D),
                  _resident((D, D)), _resident((D, D)), _resident((D, D))],
        out_specs=[_row_tile(tm, D)] * 4,
        out_shape=[jax.ShapeDtypeStruct((t, D), F32)] + [jax.ShapeDtypeStruct((t, D), BF)] * 3,
        compiler_params=_params(1), name="mix_out")(zs, o_t, gp, x, w_cp, w_o, w_out)


def _mix_out_bwd(dx, a, b, gp, w_cp, w_o, w_out, deps=()):
    t = dx.shape[0]
    tm = min(ROW_TILE_WIDE, t)

    def body(dx_ref, a_ref, b_ref, gp_ref, wcp_ref, wo_ref, wout_ref, dzs_ref, dot_ref, dgp_ref, da_ref, db_ref, dxb_ref):
        dxb = dx_ref[...].astype(BF)
        dxb_ref[...] = dxb
        dm = _dot_nt(dxb, wout_ref[...])
        gc = _sig(gp_ref[:, :D].astype(F32))
        ga = _sig(gp_ref[:, D:].astype(F32))
        da = (dm * gc).astype(BF)
        db = (dm * ga).astype(BF)
        da_ref[...] = da
        db_ref[...] = db
        dgp_ref[:, :D] = (dm * a_ref[...].astype(F32) * gc * (1.0 - gc)).astype(BF)
        dgp_ref[:, D:] = (dm * b_ref[...].astype(F32) * ga * (1.0 - ga)).astype(BF)
        dzs_ref[...] = _dot_nt(da, wcp_ref[...])
        dot_ref[...] = _dot_nt(wo_ref[...], db).astype(BF)

    return _call(
        body, deps, (dx, a, b, gp, w_cp, w_o, w_out), grid=(t // tm,),
        in_specs=[_row_tile(tm, D), _row_tile(tm, D), _row_tile(tm, D), _row_tile(tm, 2 * D),
                  _resident((D, D)), _resident((D, D)), _resident((D, D))],
        out_specs=[_row_tile(tm, D), pl.BlockSpec((D, tm), lambda i: (0, i)), _row_tile(tm, 2 * D),
                   _row_tile(tm, D), _row_tile(tm, D), _row_tile(tm, D)],
        out_shape=[jax.ShapeDtypeStruct((t, D), F32), jax.ShapeDtypeStruct((D, t), BF), jax.ShapeDtypeStruct((t, 2 * D), BF),
                   jax.ShapeDtypeStruct((t, D), BF), jax.ShapeDtypeStruct((t, D), BF), jax.ShapeDtypeStruct((t, D), BF)],
        compiler_params=_params(1), name="mix_out_bwd")


def _mix_proj_bwd(dxo, duc, dq_t, ckv, qkv_t, kg, dgp, x, g, w_t):
    t = x.shape[0]
    tm = min(ROW_TILE_WIDE, t)
    per = tm // BLK
    steps = t // tm
    kv_rows = 2 * NKV * HD

    def body(dxo_ref, duc_ref, dq_ref, c_ref, cn_ref, k_ref, kg_ref, dgp_ref, x_ref, g_ref, w_ref,
             dx_ref, dg_ref, dkv_ref, dkg_ref, kg_scr):
        i = pl.program_id(0)

        @pl.when(i == 0)
        def _():
            dg_ref[...] = jnp.zeros_like(dg_ref)
            kg_scr[...] = jnp.zeros_like(kg_scr)

        kg_scr[...] += _kv_combine_tile(c_ref, cn_ref, (i < steps - 1).astype(F32), k_ref, kg_ref[...], dkv_ref)
        dn = _dot(duc_ref[...], w_ref[R_CONV[0]:R_CONV[1], :])
        dn = dn + _dot(dgp_ref[...], w_ref[R_GATE[0]:R_GATE[1], :])
        dn = dn + _dot_tn(dq_ref[...], w_ref[R_Q[0]:R_Q[1], :])
        dn = dn + _dot_tn(dkv_ref[...], w_ref[R_KV[0]:R_KV[1], :])
        dx, dg = _rms_bwd(dn, x_ref[...], g_ref[...])
        dx_ref[...] = dxo_ref[...] + dx
        dg_ref[...] += dg

        @pl.when(i == steps - 1)
        def _():
            dkg_ref[...] = jnp.sum(kg_scr[...], axis=1, keepdims=True)

    return pl.pallas_call(
        body, grid=(steps,),
        in_specs=[_row_tile(tm, D), _row_tile(tm, 2 * D), pl.BlockSpec((D, tm), lambda i: (0, i)),
                  pl.BlockSpec((per, kv_rows, 2 * BLK), lambda i: (i, 0, 0)),
                  pl.BlockSpec((1, kv_rows, 2 * BLK), lambda i: (jnp.minimum((i + 1) * per, t // BLK - 1), 0, 0)),
                  pl.BlockSpec((NKV * HD, tm), lambda i: (D // (NKV * HD), i)), _resident((HD, 1)),
                  _row_tile(tm, 2 * D), _row_tile(tm, D), _resident((1, D)), _resident((INW, D))],
        out_specs=[_row_tile(tm, D), pl.BlockSpec((1, D), lambda i: (0, 0)), pl.BlockSpec((kv_rows, tm), lambda i: (0, i)),
                   pl.BlockSpec((HD, 1), lambda i: (0, 0))],
        out_shape=[jax.ShapeDtypeStruct((t, D), F32), jax.ShapeDtypeStruct((1, D), F32),
                   jax.ShapeDtypeStruct((kv_rows, t), BF), jax.ShapeDtypeStruct((HD, 1), F32)],
        scratch_shapes=[pltpu.VMEM((HD, BLK), F32)],
        compiler_params=_params(1), name="mix_proj_bwd")(dxo, duc, dq_t, ckv, ckv, qkv_t, kg, dgp, x, g, w_t)


def _attention_tables():
    kj = np.arange(2 * BLK)[:, None]
    qi = np.arange(BLK)[None, :]
    dist = qi + BLK - kj
    in_win = (dist >= 0) & (dist < BLK)
    dpos = np.maximum(dist, 0)
    max_exact = NBUCKET // 2
    dfl = np.maximum(dpos, 1).astype(np.float32)
    large = max_exact + (np.log(dfl / np.float32(max_exact)) / np.float32(math.log(BLK / max_exact))
                         * np.float32(NBUCKET - max_exact)).astype(np.int32)
    large = np.minimum(large, NBUCKET - 1)
    bucket = np.where(dpos < max_exact, dpos, large)
    onehot = (bucket[None] == np.arange(NBUCKET)[:, None, None]).astype(np.float32)
    mask = in_win.astype(np.float32)
    mask_first = mask * (kj >= BLK)
    masks = np.stack([np.tile(mask, (1, GRP)), np.tile(mask_first, (1, GRP))])
    return onehot, masks


def _bias_table(rel_bias, onehot):
    tab = jnp.einsum("bkq,bh->hkq", onehot, rel_bias, precision=lax.Precision.HIGHEST)
    tab = tab.reshape(NKV, GRP, 2 * BLK, BLK)
    return jnp.transpose(tab, (0, 2, 1, 3)).reshape(NKV, 2 * BLK, GRP * BLK)


def _local_step(x, target, vec, ffn1_up, weights_of, wgrad, wgrads, grads_done, small_done):
    onehot_np, masks_np = _attention_tables()
    onehot = jnp.asarray(onehot_np)
    masks = jnp.asarray(masks_np)
    bias_t = jnp.where(masks[:, None] > 0.5, _bias_table(vec["rel_bias"], onehot)[None], NEG)
    sink_rows = jnp.repeat(vec["attn_sinks"].reshape(NKV, 1, GRP), BLK, axis=2)
    qg = vec["q_norm"].reshape(HD, 1)
    kg = vec["k_norm"].reshape(HD, 1)
    g1 = vec["ffn1_norm"].reshape(1, D)
    gm = vec["mix_norm"].reshape(1, D)
    g2 = vec["ffn2_norm"].reshape(1, D)
    dwb = vec["conv_dw_bias"].reshape(1, D)
    lng = vec["conv_ln_g"].reshape(1, D)
    lnb = vec["conv_ln_b"].reshape(1, D)

    n1, u1, w1 = ffn1_up(x, g1, (bias_t, sink_rows))
    x1, = _ffn_down(x, u1, w1["ffn1_w_out"], "ffn1_down_first", part=(0, 2))
    (x1,), wm = weights_of("mix_proj", (x1,), during=functools.partial(
        _ffn_down, x, u1, w1["ffn1_w_out"], "ffn1_down", part=(1, 2), out=x1))
    dwk = jnp.pad(wm["conv_dw_kernel"], ((0, CWP - CW), (0, 0)))
    hm, uc, gp, qkv_t = _mix_proj(x1, gm, wm["w_in"])
    (zs, zc), merge = weights_of("mix_merge", (uc,), during=functools.partial(_conv_fwd, uc, dwk, dwb, lng, lnb))
    wm.update(merge)
    o_t, probs, sink_probs = _attn_fwd(qkv_t, qg, kg, sink_rows, bias_t)
    x2, a, b, merged = _mix_out(zs, o_t, gp, x1, wm["conv_w_proj"], wm["attn_w_o"], wm["w_out"])
    w2 = weights_of("ffn2", (x2,))
    gv = {}
    n2, du2, h2, dy2, dx2, sq, gv["ffn2_norm"] = _ffn_last(x2, target, g2, w2["ffn2_w_in"], w2["ffn2_w_out"], "ffn2")

    deps = grads_done("ffn2", {"ffn2_w_in": wgrad(du2, n2, "ffn2_dw_in", False),
                               "ffn2_w_out": wgrad(h2, dy2, "ffn2_dw_out", False)})

    dzs, do_t, dgp, da, db, dx2b = _mix_out_bwd(dx2, a, b, gp, wm["conv_w_proj"], wm["attn_w_o"], wm["w_out"], deps=deps)
    grads = wgrads([(merged, dx2b, False), (zs, da, False), (o_t, db, True)], "mix_dw_merge")
    deps = grads_done("mix_out", dict(zip(("w_out", "conv_w_proj", "attn_w_o"), grads)))

    dq_t, ckv, dqg, dsink, dbias = _attn_bwd(qkv_t, do_t, probs, sink_probs, qg, kg, onehot, deps=deps)
    gv["q_norm"] = dqg.reshape(HD)
    gv["attn_sinks"] = dsink[:, :GRP].reshape(NQ)
    gv["rel_bias"] = dbias[:, :, :GRP].reshape(NBUCKET, NQ)

    duc, dk_conv, gv["conv_dw_bias"], gv["conv_ln_g"], gv["conv_ln_b"] = _conv_bwd(uc, zc, dzs, dwk, lng, lnb)
    gv["conv_dw_kernel"] = dk_conv[:CW]

    dx1, gv["mix_norm"], dkv_t, dkg = _mix_proj_bwd(dx2, duc, dq_t, ckv, qkv_t, kg, dgp, x1, gm, wm["w_in"])
    gv["k_norm"] = dkg.reshape(HD)
    deps = grads_done("mix_in", {"w_in": _wgrad_mix(duc, dq_t, dkv_t, dgp, hm)})

    dx0, du1, h1, dy1, gv["ffn1_norm"] = _ffn_bwd(dx1, x, g1, u1, w1["ffn1_w_in"], w1["ffn1_w_out"], "ffn1_bwd", deps=deps)
    for k in ("ffn1_norm", "mix_norm", "ffn2_norm", "conv_dw_bias", "conv_ln_g", "conv_ln_b"):
        gv[k] = gv[k].reshape(D)
    deps = small_done(gv, sq)
    deps = grads_done("ffn1_in", {"ffn1_w_in": wgrad(du1, n1, "ffn1_dw_in", False, deps)})
    grads_done("ffn1_out", {"ffn1_w_out": wgrad(h1, dy1, "ffn1_dw_out", False, deps)})
    return dx0


MESH_ID = pl.DeviceIdType.MESH


def _position():
    return lax.axis_index("x"), lax.axis_index("y"), lax.axis_index("c")


def _shard_rows(ref, index, rows):
    return ref.at[pl.ds(pl.multiple_of(index * rows, 16), rows), :]


def _prep(weights, taps, me, name, deps=()):
    n = len(weights)
    n_deps = len(deps)
    with_taps = taps is not None

    def body(me_ref, *refs):
        refs = refs[n_deps:]
        ins, outs = refs[:len(refs) // 2], refs[len(refs) // 2:]
        for k in range(n):
            outs[k][...] = ins[k][...].astype(BF)
        if with_taps:
            outs[n][0:CW, :] = ins[n][...]
            outs[n][CW:, :] = jnp.zeros((CWP - CW, BLK), F32)

    shard_shapes = [w.shape for w in weights] + [(CWP, BLK)] * with_taps
    dtypes = [BF] * n + [F32] * with_taps
    ins = list(weights) + [taps] * with_taps
    return pl.pallas_call(
        body,
        grid_spec=pltpu.PrefetchScalarGridSpec(
            num_scalar_prefetch=1, grid=(1,),
            in_specs=[ANY] * n_deps + [pl.BlockSpec(a.shape, lambda i, m: (0, 0), pipeline_mode=pl.Buffered(1)) for a in ins],
            out_specs=[pl.BlockSpec(s, lambda i, m: (m[0], 0)) for s in shard_shapes]),
        out_shape=[jax.ShapeDtypeStruct((N_DEV * s[0], s[1]), d) for s, d in zip(shard_shapes, dtypes)],
        compiler_params=_params(1), name=name)(me, *deps, *ins)


HBM = pl.BlockSpec(memory_space=pltpu.HBM)
SEM = pl.BlockSpec(memory_space=pltpu.SEMAPHORE)
DATAFLOW = pltpu.SideEffectType.DATAFLOW_SIDE_EFFECTING
TOKEN = jax.ShapeDtypeStruct((8, 128), F32)


def _in_hbm(x):
    return pltpu.with_memory_space_constraint(x, pltpu.HBM)


def _hbm_like(arrays):
    return [pltpu.HBM(a.shape, a.dtype) for a in arrays]


def _other_chips(x, y):
    return [(1 - x, y), (x, 1 - y), (1 - x, 1 - y)]


def _device_index(chip, c):
    return 4 * chip[0] + 2 * chip[1] + c


def _chip_index(chip):
    return 2 * chip[0] + chip[1]


class _Exchange:
    def __init__(self, gather, all_cores=False):
        self.gather = gather
        self.all_cores = all_cores
        self.n_peers = N_DEV - 1 if all_cores else 3

    def peers(self, x, y, c):
        if self.all_cores:
            return [(x ^ (k >> 2), y ^ ((k >> 1) & 1), c ^ (k & 1)) for k in range(1, N_DEV)]
        return [(*chip, c) for chip in _other_chips(x, y)]

    def sent(self, x, y, c, peer):
        return _device_index((x, y), c) if self.gather else _chip_index(peer[:2])

    def lands_at(self, x, y, c):
        return _device_index((x, y), c) if self.gather else _chip_index((x, y))

    def arrives_at(self, peer):
        return _device_index(peer[:2], peer[2]) if self.gather else _chip_index(peer[:2])


def _ici_copies_start(sets, sources, landings, exchanges, name, deps=()):
    n = len(landings)
    arrays = (list(sources) if sources is not None else []) + list(landings)
    first_land = len(arrays) - n
    n_sets = len(sets)
    n_deps = len(deps)

    def body(*refs):
        refs = refs[n_deps:]
        src, land = refs[:n], refs[first_land:first_land + n]
        sems = refs[len(arrays):len(arrays) + 2 * n_sets]
        token = refs[-1]
        x, y, c = _position()
        for s, (members, exchange) in enumerate(zip(sets, exchanges)):
            for slot, (k, rows) in enumerate(members):
                for j, peer in enumerate(exchange.peers(x, y, c)):
                    at = exchange.n_peers * slot + j
                    pltpu.make_async_remote_copy(
                        src_ref=_shard_rows(src[k], exchange.sent(x, y, c, peer), rows),
                        dst_ref=_shard_rows(land[k], exchange.lands_at(x, y, c), rows),
                        send_sem=sems[2 * s].at[at], recv_sem=sems[2 * s + 1].at[at],
                        device_id=peer, device_id_type=MESH_ID).start()
        token[...] = jnp.zeros_like(token)

    sem_shapes = []
    for members, exchange in zip(sets, exchanges):
        sem_shapes += [pltpu.SemaphoreType.DMA((exchange.n_peers * len(members),))] * 2
    out = pl.pallas_call(
        body, name=name,
        out_shape=sem_shapes + _hbm_like(arrays) + [TOKEN],
        in_specs=[ANY] * n_deps + [HBM] * len(arrays),
        out_specs=[SEM] * (2 * n_sets) + [HBM] * len(arrays) + [pl.BlockSpec(memory_space=pltpu.VMEM)],
        input_output_aliases={n_deps + i: 2 * n_sets + i for i in range(len(arrays))},
        compiler_params=pltpu.CompilerParams(has_side_effects=DATAFLOW),
    )(*deps, *[_in_hbm(a) for a in arrays])
    sems = [(out[2 * s], out[2 * s + 1]) for s in range(n_sets)]
    thru = list(out[2 * n_sets:2 * n_sets + len(arrays)])
    return sems, (thru[:first_land] if sources is not None else None), thru[first_land:], out[-1]


def _ici_copies_wait(sems, members, sources, landings, exchange, after, name):
    n = len(landings)
    arrays = (list(sources) if sources is not None else []) + list(landings)
    first_land = len(arrays) - n

    def body(*refs):
        src, land = refs[:n], refs[first_land:first_land + n]
        send_sems, recv_sems = refs[len(arrays)], refs[len(arrays) + 1]
        x, y, c = _position()
        for slot, rows in enumerate(members):
            for j, peer in enumerate(exchange.peers(x, y, c)):
                at = exchange.n_peers * slot + j
                cp = pltpu.make_async_remote_copy(
                    src_ref=_shard_rows(src[slot], exchange.sent(x, y, c, peer), rows),
                    dst_ref=_shard_rows(land[slot], exchange.arrives_at(peer), rows),
                    send_sem=send_sems.at[at], recv_sem=recv_sems.at[at], device_id=peer, device_id_type=MESH_ID)
                cp.wait_send()
                cp.wait_recv()

    out = pl.pallas_call(
        body, name=name, out_shape=_hbm_like(arrays),
        in_specs=[HBM] * len(arrays) + [SEM, SEM] + [ANY] * len(after), out_specs=[HBM] * len(arrays),
        input_output_aliases={i: i for i in range(len(arrays))},
        compiler_params=pltpu.CompilerParams(has_side_effects=DATAFLOW),
    )(*arrays, sems[0], sems[1], *after)
    return list(out[first_land:])


def _swap_copies(land, rows, which, send_sems, recv_sems):
    x, y, c = _position()
    chips = [([(x, y)] + _other_chips(x, y))[j] for j in which]
    sends, recvs = [], []
    for k in range(len(land)):
        for j, chip in enumerate(chips):
            for copies, core in ((sends, c), (recvs, 1 - c)):
                block = _shard_rows(land[k], _device_index(chip, core), rows[k])
                copies.append(pltpu.make_async_remote_copy(
                    src_ref=block, dst_ref=block, send_sem=send_sems.at[k, j], recv_sem=recv_sems.at[k, j],
                    device_id=(x, y, 1 - c), device_id_type=MESH_ID))
    return sends, recvs


def _d2d_gather(buffers, rows, name, which=(0, 1, 2, 3), deps=()):
    n = len(buffers)
    n_deps = len(deps)

    def body(*refs):
        sends, recvs = _swap_copies(refs[n_deps + n:n_deps + 2 * n], rows, which, *refs[n_deps + 2 * n:])
        for cp in sends:
            cp.start()
        for cp in recvs:
            cp.wait_recv()
        for cp in sends:
            cp.wait_send()

    return pl.pallas_call(
        body, name=name, out_shape=[jax.ShapeDtypeStruct(a.shape, a.dtype) for a in buffers],
        in_specs=[ANY] * (n_deps + n), out_specs=[ANY] * n, input_output_aliases={n_deps + i: i for i in range(n)},
        scratch_shapes=[pltpu.SemaphoreType.DMA((n, len(which))), pltpu.SemaphoreType.DMA((n, len(which)))],
    )(*deps, *buffers)


def _call_with_swap(body, args, swap, prefetch=(), **kw):
    buffers, rows = swap
    n, n_pre, n_in, n_out = len(buffers), len(prefetch), len(args), len(kw["out_shape"])
    n_scratch = len(kw["scratch_shapes"])
    grid = kw["grid"]
    which = (0, 1, 2, 3)

    def at_step(last):
        hit = [pl.program_id(a) == (extent - 1 if last else 0) for a, extent in enumerate(grid)]
        return functools.reduce(jnp.logical_and, hit)

    def hosted(*refs):
        pre, ins, refs = refs[:n_pre], refs[n_pre:n_pre + n_in], refs[n_pre + n_in + n:]
        outs, land, scratch = refs[:n_out], refs[n_out:n_out + n], refs[n_out + n:n_out + n + n_scratch]
        sends, recvs = _swap_copies(land, rows, which, *refs[n_out + n + n_scratch:])

        @pl.when(at_step(False))
        def _():
            for cp in sends:
                cp.start()

        body(*pre, *ins, *outs, *scratch)

        @pl.when(at_step(True))
        def _():
            for cp in recvs:
                cp.wait_recv()
            for cp in sends:
                cp.wait_send()

    sem_shape = pltpu.SemaphoreType.DMA((n, len(which)))
    aliases = {**kw.get("input_output_aliases", {}), **{n_in + i: n_out + i for i in range(n)}}
    out = pl.pallas_call(
        hosted,
        grid_spec=pltpu.PrefetchScalarGridSpec(
            num_scalar_prefetch=n_pre, grid=grid, in_specs=kw["in_specs"] + [ANY] * n, out_specs=kw["out_specs"] + [ANY] * n,
            scratch_shapes=kw["scratch_shapes"] + [sem_shape, sem_shape]),
        out_shape=kw["out_shape"] + [jax.ShapeDtypeStruct(a.shape, a.dtype) for a in buffers],
        input_output_aliases={n_pre + i: o for i, o in aliases.items()},
        compiler_params=kw["compiler_params"], name=kw["name"])(*prefetch, *args, *buffers)
    return out[:n_out], out[n_out:]


def _pair_exchange_add(grad, place, name):
    r = grad.shape[0] // N_DEV
    n_chips = N_DEV // 2

    def body(place_ref, g_hbm, kept_ref, part_ref, land_ref, inbox, send_sems, recv_sems):
        q = pl.program_id(0)
        x, y, c = _position()
        copies = [pltpu.make_async_remote_copy(
            src_ref=_shard_rows(g_hbm, 2 * i + 1 - c, r), dst_ref=inbox.at[i], send_sem=send_sems.at[i],
            recv_sem=recv_sems.at[i], device_id=(x, y, 1 - c), device_id_type=MESH_ID) for i in range(n_chips)]

        @pl.when(q == 0)
        def _():
            for cp in copies:
                cp.start()

        for i, cp in enumerate(copies):
            @pl.when(q == i)
            def _(cp=cp):
                cp.wait_recv()

        total = (kept_ref[...].astype(F32) + inbox[q].astype(F32)).astype(BF)
        part_ref[...] = total

        @pl.when(q == place_ref[1])
        def _():
            land_ref[...] = total

        @pl.when(q == n_chips - 1)
        def _():
            for cp in copies:
                cp.wait_send()

    return pl.pallas_call(
        body,
        grid_spec=pltpu.PrefetchScalarGridSpec(
            num_scalar_prefetch=1, grid=(n_chips,),
            in_specs=[ANY, pl.BlockSpec((r, D), lambda q, p: (2 * q + p[0], 0))],
            out_specs=[pl.BlockSpec((r, D), lambda q, p: (q, 0)), pl.BlockSpec((r, D), lambda q, p: (p[1], 0))],
            scratch_shapes=[pltpu.VMEM((n_chips, r, D), BF), pltpu.SemaphoreType.DMA((n_chips,)),
                            pltpu.SemaphoreType.DMA((n_chips,))]),
        out_shape=[jax.ShapeDtypeStruct((n_chips * r, D), BF)] * 2,
        compiler_params=_params(1), name=name)(place, grad, grad)


def _wgrad_pair_sum(lhs, rhs, place, name, *, lhs_is_transposed, deps=()):
    t = rhs.shape[0]
    n = lhs.shape[0] if lhs_is_transposed else lhs.shape[1]
    r = n // N_DEV
    n_chips = N_DEV // 2
    per = 1 if (2 * r) % BLK == 0 else 2
    steps = n_chips // per
    n_deps = len(deps)
    in_vmem = r <= WGRAD_SUM_MAX_ROWS

    def body(place_ref, *refs):
        if in_vmem:
            l_ref, r_ref, part_ref, land_ref, res, inbox, send_sems, recv_sems = refs[n_deps:]
        else:
            l_ref, r_ref, part_ref, land_ref, inbox, res, staged, send_sems, recv_sems, stage_sem = refs[n_deps:]
        q = pl.program_id(0)
        slot = q % 2
        x, y, c = _position()

        def send(step, buf, i):
            return pltpu.make_async_remote_copy(
                src_ref=res.at[buf, pl.ds(pl.multiple_of((2 * i + 1 - c) * r, 16), r), :], dst_ref=inbox.at[step * per + i],
                send_sem=send_sems.at[buf, i], recv_sem=recv_sems.at[step * per + i],
                device_id=(x, y, 1 - c), device_id_type=MESH_ID)

        @pl.when(q < steps)
        def _():
            @pl.when(q >= 2)
            def _():
                for i in range(per):
                    send(q - 2, slot, i).wait_send()

            if lhs_is_transposed:
                res[slot] = _dot(l_ref[...], r_ref[...]).astype(BF)
            else:
                res[slot] = _dot_tn(l_ref[...], r_ref[...]).astype(BF)
            for i in range(per):
                send(q, slot, i).start()

        @pl.when(q >= 1)
        def _():
            for i in range(per):
                chip = (q - 1) * per + i
                send(q - 1, 1 - slot, i).wait_recv()
                kept = res[1 - slot, pl.ds(pl.multiple_of((2 * i + c) * r, 16), r), :]
                if in_vmem:
                    theirs = inbox[chip]
                else:
                    stage = pltpu.make_async_copy(inbox.at[chip], staged, stage_sem)
                    stage.start()
                    stage.wait()
                    theirs = staged[...]
                total = (kept.astype(F32) + theirs.astype(F32)).astype(BF)
                part_ref[i * r:(i + 1) * r, :] = total

                @pl.when(chip == place_ref[1])
                def _():
                    land_ref[...] = total

        @pl.when(q == steps)
        def _():
            for i in range(per):
                if steps > 1:
                    send(q - 2, slot, i).wait_send()
                send(q - 1, 1 - slot, i).wait_send()

    width = 2 * r * per
    last = steps - 1
    if lhs_is_transposed:
        lhs_spec = pl.BlockSpec((width, t), lambda q, p: (jnp.minimum(q, last), 0))
    else:
        lhs_spec = pl.BlockSpec((t, width), lambda q, p: (0, jnp.minimum(q, last)))
    sems = [pltpu.SemaphoreType.DMA((2, per)), pltpu.SemaphoreType.DMA((n_chips,))]
    inbox_shape = (n_chips, r, D)
    if in_vmem:
        extra_specs, extra_shapes = [], []
        scratch = [pltpu.VMEM((2, width, D), BF), pltpu.VMEM(inbox_shape, BF)] + sems
    else:
        extra_specs, extra_shapes = [ANY], [jax.ShapeDtypeStruct(inbox_shape, BF)]
        scratch = [pltpu.VMEM((2, width, D), BF), pltpu.VMEM((r, D), BF)] + sems + [pltpu.SemaphoreType.DMA(())]
    out = pl.pallas_call(
        body,
        grid_spec=pltpu.PrefetchScalarGridSpec(
            num_scalar_prefetch=1, grid=(steps + 1,),
            in_specs=[ANY] * n_deps + [lhs_spec, pl.BlockSpec((t, D), lambda q, p: (0, 0), pipeline_mode=pl.Buffered(1))],
            out_specs=[pl.BlockSpec((per * r, D), lambda q, p: (jnp.maximum(q - 1, 0), 0)),
                       pl.BlockSpec((r, D), lambda q, p: (p[1], 0))] + extra_specs,
            scratch_shapes=scratch),
        out_shape=[jax.ShapeDtypeStruct((n // 2, D), BF)] * 2 + extra_shapes,
        compiler_params=_params(1), name=name)(place, *deps, lhs, rhs)
    return out[:2]


def _wgrad_pair_sum_many(items, place, name, deps=()):
    m = len(items)
    t = items[0][1].shape[0]
    n = items[0][0].shape[0] if items[0][2] else items[0][0].shape[1]
    r = n // N_DEV
    assert (2 * r) % BLK == 0 and r <= WGRAD_SUM_MAX_ROWS
    steps = N_DEV // 2
    chunks = m * steps
    n_deps = len(deps)

    def body(place_ref, *refs):
        refs = refs[n_deps:]
        l_refs, r_first, r_later = refs[:m], refs[m], refs[m + 1:2 * m]
        parts, lands = refs[2 * m:3 * m], refs[3 * m:4 * m]
        res, inbox, r_scr, send_sems, recv_sems, fetch_sems = refs[4 * m:]
        g = pl.program_id(0)
        slot = g % 2
        x, y, c = _position()

        def send(chunk, buf):
            return pltpu.make_async_remote_copy(
                src_ref=res.at[buf, pl.ds(pl.multiple_of((1 - c) * r, 16), r), :], dst_ref=inbox.at[chunk],
                send_sem=send_sems.at[buf], recv_sem=recv_sems.at[chunk], device_id=(x, y, 1 - c), device_id_type=MESH_ID)

        def fetch(k):
            return pltpu.make_async_copy(r_later[k - 1], r_scr.at[k - 1], fetch_sems.at[k - 1])

        @pl.when(g == 0)
        def _():
            for k in range(1, m):
                fetch(k).start()

        @pl.when(g < chunks)
        def _():
            @pl.when(g >= 2)
            def _():
                send(g - 2, slot).wait_send()

            for k, (_, _, transposed) in enumerate(items):
                @pl.when(g // steps == k)
                def _(k=k, transposed=transposed):
                    if k > 0:
                        @pl.when(g == k * steps)
                        def _():
                            fetch(k).wait()
                    rhs = r_first[...] if k == 0 else r_scr[k - 1]
                    res[slot] = (_dot(l_refs[k][...], rhs) if transposed else _dot_tn(l_refs[k][...], rhs)).astype(BF)

            send(g, slot).start()

        @pl.when(g >= 1)
        def _():
            chunk = g - 1
            send(chunk, 1 - slot).wait_recv()
            kept = res[1 - slot, pl.ds(pl.multiple_of(c * r, 16), r), :]
            total = (kept.astype(F32) + inbox[chunk].astype(F32)).astype(BF)
            for k in range(m):
                @pl.when(chunk // steps == k)
                def _(k=k):
                    parts[k][...] = total

                    @pl.when(chunk % steps == place_ref[1])
                    def _():
                        lands[k][...] = total

        @pl.when(g == chunks)
        def _():
            send(g - 2, slot).wait_send()
            send(g - 1, 1 - slot).wait_send()

    def own_steps(k):
        return lambda g: jnp.clip(g - k * steps, 0, steps - 1)

    lhs_specs = []
    for k, (lhs, _, transposed) in enumerate(items):
        at = own_steps(k)
        lhs_specs.append(pl.BlockSpec((2 * r, t), lambda g, p, at=at: (at(g), 0)) if transposed
                         else pl.BlockSpec((t, 2 * r), lambda g, p, at=at: (0, at(g))))
    out = pl.pallas_call(
        body,
        grid_spec=pltpu.PrefetchScalarGridSpec(
            num_scalar_prefetch=1, grid=(chunks + 1,),
            in_specs=[ANY] * n_deps + lhs_specs
            + [pl.BlockSpec((t, D), lambda g, p: (0, 0), pipeline_mode=pl.Buffered(1))] + [ANY] * (m - 1),
            out_specs=[pl.BlockSpec((r, D), lambda g, p, at=own_steps(k): (at(g - 1), 0)) for k in range(m)]
            + [pl.BlockSpec((r, D), lambda g, p: (p[1], 0))] * m,
            scratch_shapes=[pltpu.VMEM((2, 2 * r, D), BF), pltpu.VMEM((chunks, r, D), BF), pltpu.VMEM((m - 1, t, D), BF),
                            pltpu.SemaphoreType.DMA((2,)), pltpu.SemaphoreType.DMA((chunks,)),
                            pltpu.SemaphoreType.DMA((m - 1,))]),
        out_shape=[jax.ShapeDtypeStruct((n // 2, D), BF)] * (2 * m),
        compiler_params=_params(1), name=name)(place, *deps, *[i[0] for i in items], *[i[1] for i in items])
    return [(out[k], out[m + k]) for k in range(m)]


def _sum_blocks(gathered, rows):
    def body(b_ref, o_ref):
        acc = b_ref[0:rows, :]
        for d in range(1, N_DEV):
            acc = acc + b_ref[d * rows:(d + 1) * rows, :]
        o_ref[...] = acc

    return pl.pallas_call(body, out_shape=jax.ShapeDtypeStruct((rows, D), F32), name="small_sum")(gathered)


def _adamw_math(w, g, m, v):
    m = ADAM_B1 * m + (1.0 - ADAM_B1) * g
    v = ADAM_B2 * v + (1.0 - ADAM_B2) * (g * g)
    m_hat = m / (1.0 - ADAM_B1 ** ADAM_STEP)
    v_hat = v / (1.0 - ADAM_B2 ** ADAM_STEP)
    delta = -ADAM_LR * (m_hat / (jnp.sqrt(v_hat) + ADAM_EPS) + ADAM_WD * w)
    return delta, m, v


def _sum_partials(blocks):
    g = blocks[0].astype(F32)
    for blk in blocks[1:]:
        g = g + blk.astype(F32)
    return g


ADAMW_MAX_ROWS = 352


def _reduce_adamw(items, name):
    n = len(items)
    per = -(-max(w.shape[0] for _, w, _, _ in items) // ADAMW_MAX_ROWS)

    def body(*refs):
        for k in range(n):
            r0, r1, r2, r3, w_ref, m_ref, v_ref = refs[7 * k:7 * k + 7]
            g_ref, d_ref, nm_ref, nv_ref = refs[7 * n + 4 * k:7 * n + 4 * k + 4]
            g = _sum_partials([r0[...], r1[...], r2[...], r3[...]])
            g_ref[...] = g
            d_ref[...], nm_ref[...], nv_ref[...] = _adamw_math(w_ref[...], g, m_ref[...], v_ref[...])

    in_specs, out_specs, out_shape, args = [], [], [], []
    for landed, w, m, v in items:
        tr = w.shape[0] // per
        assert tr * per == w.shape[0] and tr % 16 == 0
        tile = _row_tile(tr, D)
        in_specs += [pl.BlockSpec((tr, D), lambda i, q=q: (q * per + i, 0)) for q in range(4)] + [tile] * 3
        out_specs += [tile] * 4
        out_shape += [jax.ShapeDtypeStruct(w.shape, F32)] * 4
        args += [landed] * 4 + [w, m, v]
    out = pl.pallas_call(body, grid=(per,), in_specs=in_specs, out_specs=out_specs, out_shape=out_shape,
                         compiler_params=_params(1), name=name)(*args)
    return [out[4 * k:4 * k + 4] for k in range(n)]


def _adamw_small(w, g, m, v, name):
    def body(w_ref, g_ref, m_ref, v_ref, d_ref, nm_ref, nv_ref):
        d_ref[...], nm_ref[...], nv_ref[...] = _adamw_math(w_ref[...], g_ref[...], m_ref[...], v_ref[...])

    return pl.pallas_call(body, out_shape=[jax.ShapeDtypeStruct(w.shape, F32)] * 3, name=name)(w, g, m, v)


WEIGHTS = ("ffn1_norm", "ffn1_w_in", "ffn1_w_out", "mix_norm", "w_in", "conv_dw_kernel", "conv_dw_bias", "conv_ln_g",
           "conv_ln_b", "conv_w_proj", "q_norm", "k_norm", "attn_sinks", "rel_bias", "attn_w_o", "w_out", "ffn2_norm",
           "ffn2_w_in", "ffn2_w_out")
MATRICES = ("ffn1_w_in", "ffn1_w_out", "w_in", "conv_w_proj", "attn_w_o", "w_out", "ffn2_w_in", "ffn2_w_out")
COLUMN_SHARDED = ("ffn1_w_in", "w_in", "ffn2_w_in")
ROW_VECTORS = ("ffn1_norm", "mix_norm", "conv_dw_bias", "conv_ln_g", "conv_ln_b", "ffn2_norm")
PACKED = (("q_norm", HD), ("k_norm", HD), ("attn_sinks", NQ), ("rel_bias", NBUCKET * NQ))
GATHER = _Exchange(gather=True)
GATHER_ALL = _Exchange(gather=True, all_cores=True)
SCATTER = _Exchange(gather=False)
FIRST = "ffn1_w_in"
GATHER_STAGES = ("ffn1_out", "mix_proj", "mix_merge", "ffn2")
STAGE_GATHER = {"ffn1_out": GATHER, "mix_proj": GATHER, "mix_merge": GATHER, "ffn2": GATHER_ALL}
STAGE_MEMBERS = {"ffn1_out": ("ffn1_w_out",),
                 "mix_proj": ("w_in", "taps"), "mix_merge": ("conv_w_proj", "attn_w_o", "w_out"),
                 "ffn2": ("ffn2_w_in", "ffn2_w_out")}
ROW_PACKED = len(ROW_VECTORS)
ROW_LOSS = ROW_PACKED + 1
ROW_TAPS = 8
PAYLOAD_ROWS = 48


def _pack_small(values, last_row):
    packed = jnp.concatenate([values[k].reshape(-1) for k, _ in PACKED])
    packed = jnp.pad(packed, (0, D - packed.shape[0])).reshape(1, D)
    return jnp.concatenate([values[k].reshape(1, D) for k in ROW_VECTORS] + [packed, last_row], axis=0)


def _unpack_small(rows):
    out = {k: rows[i] for i, k in enumerate(ROW_VECTORS)}
    at = 0
    for k, size in PACKED:
        out[k] = rows[ROW_PACKED, at:at + size]
        at += size
    out["rel_bias"] = out["rel_bias"].reshape(NBUCKET, NQ)
    return out


def kernel(x, ffn1_norm, ffn1_w_in, ffn1_w_out, mix_norm, w_in, conv_dw_kernel, conv_dw_bias, conv_ln_g, conv_ln_b, conv_w_proj, q_norm, k_norm, attn_sinks, rel_bias, attn_w_o, w_out, ffn2_norm, ffn2_w_in, ffn2_w_out, loss_target, m_ffn1_norm, m_ffn1_w_in, m_ffn1_w_out, m_mix_norm, m_w_in, m_conv_dw_kernel, m_conv_dw_bias, m_conv_ln_g, m_conv_ln_b, m_conv_w_proj, m_q_norm, m_k_norm, m_attn_sinks, m_rel_bias, m_attn_w_o, m_w_out, m_ffn2_norm, m_ffn2_w_in, m_ffn2_w_out, v_ffn1_norm, v_ffn1_w_in, v_ffn1_w_out, v_mix_norm, v_w_in, v_conv_dw_kernel, v_conv_dw_bias, v_conv_ln_g, v_conv_ln_b, v_conv_w_proj, v_q_norm, v_k_norm, v_attn_sinks, v_rel_bias, v_attn_w_o, v_w_out, v_ffn2_norm, v_ffn2_w_in, v_ffn2_w_out):
    w = dict(ffn1_norm=ffn1_norm, ffn1_w_in=ffn1_w_in, ffn1_w_out=ffn1_w_out, mix_norm=mix_norm, w_in=w_in,
             conv_dw_kernel=conv_dw_kernel, conv_dw_bias=conv_dw_bias, conv_ln_g=conv_ln_g, conv_ln_b=conv_ln_b,
             conv_w_proj=conv_w_proj, q_norm=q_norm, k_norm=k_norm, attn_sinks=attn_sinks, rel_bias=rel_bias,
             attn_w_o=attn_w_o, w_out=w_out, ffn2_norm=ffn2_norm, ffn2_w_in=ffn2_w_in, ffn2_w_out=ffn2_w_out)
    m = dict(ffn1_norm=m_ffn1_norm, ffn1_w_in=m_ffn1_w_in, ffn1_w_out=m_ffn1_w_out, mix_norm=m_mix_norm, w_in=m_w_in,
             conv_dw_kernel=m_conv_dw_kernel, conv_dw_bias=m_conv_dw_bias, conv_ln_g=m_conv_ln_g, conv_ln_b=m_conv_ln_b,
             conv_w_proj=m_conv_w_proj, q_norm=m_q_norm, k_norm=m_k_norm, attn_sinks=m_attn_sinks, rel_bias=m_rel_bias,
             attn_w_o=m_attn_w_o, w_out=m_w_out, ffn2_norm=m_ffn2_norm, ffn2_w_in=m_ffn2_w_in, ffn2_w_out=m_ffn2_w_out)
    v = dict(ffn1_norm=v_ffn1_norm, ffn1_w_in=v_ffn1_w_in, ffn1_w_out=v_ffn1_w_out, mix_norm=v_mix_norm, w_in=v_w_in,
             conv_dw_kernel=v_conv_dw_kernel, conv_dw_bias=v_conv_dw_bias, conv_ln_g=v_conv_ln_g, conv_ln_b=v_conv_ln_b,
             conv_w_proj=v_conv_w_proj, q_norm=v_q_norm, k_norm=v_k_norm, attn_sinks=v_attn_sinks, rel_bias=v_rel_bias,
             attn_w_o=v_attn_w_o, w_out=v_w_out, ffn2_norm=v_ffn2_norm, ffn2_w_in=v_ffn2_w_in, ffn2_w_out=v_ffn2_w_out)
    px, py, pc = _position()
    me = 4 * px + 2 * py + pc
    place = jnp.stack([pc, 2 * px + py]).astype(jnp.int32)

    rows_of = lambda k, a: a.T if k in COLUMN_SHARDED else a
    me1 = me.astype(jnp.int32).reshape(1)
    rest = tuple(k for k in MATRICES if k != FIRST)
    shard_rows = dict({k: rows_of(k, w[k]).shape[0] for k in MATRICES}, taps=CWP)
    sems_first, _, thru_first, token = _ici_copies_start(
        [[(0, shard_rows[FIRST])]], None, _prep([rows_of(FIRST, w[FIRST])], None, me1, "prep_first"), [GATHER],
        "gather_start_first")
    buffers = dict(zip(rest + ("taps",), _prep([rows_of(k, w[k]) for k in rest], conv_dw_kernel, me1, "prep", deps=[token])))
    landings, sets = [], []
    for stage in GATHER_STAGES:
        sets.append([(len(landings) + i, shard_rows[k]) for i, k in enumerate(STAGE_MEMBERS[stage])])
        landings += list(STAGE_MEMBERS[stage])
    sems, _, land_thru, started = _ici_copies_start(sets, None, [buffers[k] for k in landings],
                                                    [STAGE_GATHER[s] for s in GATHER_STAGES], "gather_start")

    packed = [_pack_small(a, jnp.zeros((1, D), F32)) for a in (w, m, v)]

    def ffn1_up(x, g, after):
        chips = jnp.stack([_chip_index(chip) for chip in [(px, py)] + _other_chips(px, py)]).astype(jnp.int32)
        rows = [shard_rows[FIRST]]
        mine = _d2d_gather(thru_first, rows, "gather_d2d_first_mine", which=(0,), deps=[started])
        n, u = _ffn_up_blocks(x, g, None, mine[0], chips[:1], None, "ffn1_up_mine")
        landed = _ici_copies_wait(sems_first[0], rows, None, mine, GATHER, [u, *after, *packed], "gather_wait_first")
        w_in_t, = _d2d_gather(landed, rows, "gather_d2d_first", which=(1, 2, 3))
        n, u = _ffn_up_blocks(None, None, n, w_in_t, chips[1:3], u, "ffn1_up_next")
        (n, u), w1 = weights_of("ffn1_out", (u,), during=functools.partial(
            _ffn_up_blocks, None, None, n, w_in_t, chips[3:], u, "ffn1_up"))
        return n, u, dict(w1, ffn1_w_in=w_in_t)

    def weights_of(stage, after, during=None):
        s = GATHER_STAGES.index(stage)
        rows = [r for _, r in sets[s]]
        landed = _ici_copies_wait(sems[s], rows, None, [land_thru[k] for k, _ in sets[s]], STAGE_GATHER[stage],
                                  list(after), "gather_wait_" + stage)
        if during is not None:
            results, landed = during(swap=(landed, rows))
        elif not STAGE_GATHER[stage].all_cores:
            landed = _d2d_gather(landed, rows, "gather_d2d_" + stage)
        out = dict(zip(STAGE_MEMBERS[stage], landed))
        if "taps" in out:
            taps = out.pop("taps")
            out["conv_dw_kernel"] = jnp.transpose(taps.reshape(N_DEV, CWP, BLK), (1, 0, 2)).reshape(CWP, D)[:CW]
        return out if during is None else (results, out)

    in_flight = []

    def wgrad(lhs, rhs, name, lhs_is_transposed, deps=()):
        return ("summed",) + tuple(_wgrad_pair_sum(lhs, rhs, place, name, lhs_is_transposed=lhs_is_transposed, deps=deps))

    def wgrads(items, name):
        return [("summed",) + pair for pair in _wgrad_pair_sum_many(items, place, name)]

    def grads_done(stage, grads):
        names = list(grads)
        added = []
        for k in names:
            if not isinstance(grads[k], tuple):
                added.append(_pair_exchange_add(grads[k], place, "pair_add_" + k))
            else:
                added.append(grads[k][1:])
        partials = [p for p, _ in added]
        members = [(i, p.shape[0] // 4) for i, p in enumerate(partials)]
        sem, p_thru, l_thru, token = _ici_copies_start([members], partials, [l for _, l in added], [SCATTER],
                                                       "scatter_start_" + stage)
        in_flight.append((stage, names, sem[0], p_thru, l_thru, token))
        return [token]

    small = []

    def small_done(gv, sq):
        payload = jnp.concatenate([_pack_small(gv, sq), jnp.pad(gv["conv_dw_kernel"], ((0, PAYLOAD_ROWS - ROW_TAPS - CW), (0, 0)))],
                                  axis=0)
        mine = lax.dynamic_update_slice_in_dim(lax.empty((N_DEV * PAYLOAD_ROWS, D), F32), payload, me * PAYLOAD_ROWS, axis=0)
        sems, _, thru, token = _ici_copies_start([[(0, PAYLOAD_ROWS)]], None, [mine], [GATHER_ALL], "small_start")
        small.append((sems[0], thru))
        return [token]

    vec = {k: w[k] for k in WEIGHTS if k not in MATRICES and k != "conv_dw_kernel"}
    dx0 = _local_step(x[0], loss_target[0], vec, ffn1_up, weights_of, wgrad, wgrads, grads_done, small_done)
    gathered, = _ici_copies_wait(small[0][0], [PAYLOAD_ROWS], None, small[0][1], GATHER_ALL, [in_flight[-1][-1]], "small_wait")
    total = _sum_blocks(gathered, PAYLOAD_ROWS)
    loss = (0.5 / D) * jnp.sum(total[ROW_LOSS])

    grads, delta, new_m, new_v = {}, {}, {}, {}
    after, pending = [total], []
    for stage, names, sem, p_thru, l_thru, _ in in_flight:
        landed = _ici_copies_wait(sem, [p.shape[0] // 4 for p in p_thru], p_thru, l_thru, SCATTER, after,
                                  "scatter_wait_" + stage)
        pending += zip(names, landed)
        after = list(landed)
        if stage == in_flight[-2][0]:
            continue
        outs = _reduce_adamw([(buf, rows_of(k, w[k]), rows_of(k, m[k]), rows_of(k, v[k])) for k, buf in pending],
                             "adamw_" + stage)
        for (k, _), out in zip(pending, outs):
            grads[k], delta[k], new_m[k], new_v[k] = [rows_of(k, a) for a in out]
        after, pending = [out[1] for out in outs], []
    d8, m8, v8 = _adamw_small(packed[0], total[:ROW_TAPS], packed[1], packed[2], "adamw_small")
    grads.update(_unpack_small(total[:ROW_TAPS]))
    delta.update(_unpack_small(d8))
    new_m.update(_unpack_small(m8))
    new_v.update(_unpack_small(v8))
    k = "conv_dw_kernel"
    grads[k] = lax.dynamic_slice_in_dim(total[ROW_TAPS:ROW_TAPS + CW], me * BLK, BLK, axis=1)
    delta[k], new_m[k], new_v[k] = _adamw_small(w[k], grads[k], m[k], v[k], "adamw_taps")

    return (loss, dx0[None], *[grads[k] for k in WEIGHTS], *[delta[k] for k in WEIGHTS],
            *[new_m[k] for k in WEIGHTS], *[new_v[k] for k in WEIGHTS])
```

```python
import functools
import math

import numpy as np
import jax
import jax.numpy as jnp
from jax import lax
from jax.experimental import pallas as pl
from jax.experimental.pallas import tpu as pltpu

F32 = jnp.float32
BF = jnp.bfloat16

D = 1024
F = 2816
INW = 5632
CW = 31
CWP = 32
HD = 64
NQ = 16
NKV = 4
GRP = NQ // NKV
BLK = 128
NBUCKET = 32
EPS = 1e-6
NEG = float(jnp.finfo(jnp.float32).min)
QK_SCALE = 1.0 / math.sqrt(HD)
R_CONV = (0, 2048)
R_QKV = (2048, 3584)
R_Q = (2048, 3072)
R_KV = (3072, 3584)
R_GATE = (3584, 5632)

N_DEV = 8
VMEM_LIMIT_V7X = 56 * 1024 * 1024
ROW_TILE = 256
ROW_TILE_WIDE = 512
ROW_TILE_BLOCK = 1024
WGRAD_SUM_MAX_ROWS = 352

ADAM_LR = 0.001
ADAM_B1 = 0.9
ADAM_B2 = 0.999
ADAM_EPS = 1e-08
ADAM_WD = 0.01
ADAM_STEP = 10

NT_DIMS = (((1,), (1,)), ((), ()))
TN_DIMS = (((0,), (0,)), ((), ()))


def _dot(a, b):
    return jnp.dot(a, b, preferred_element_type=F32)


def _dot_nt(a, b):
    return lax.dot_general(a, b, NT_DIMS, preferred_element_type=F32)


def _dot_tn(a, b):
    return lax.dot_general(a, b, TN_DIMS, preferred_element_type=F32)


def _sig(x):
    return 0.5 * jnp.tanh(0.5 * x) + 0.5


ANY = pl.BlockSpec(memory_space=pl.ANY)


def _call(body, deps, args, **kw):
    n = len(deps)
    if n:
        kw["in_specs"] = [ANY] * n + list(kw["in_specs"])
        return pl.pallas_call(lambda *refs: body(*refs[n:]), **kw)(*deps, *args)
    return pl.pallas_call(body, **kw)(*args)


def _params(n_axes):
    return pltpu.CompilerParams(dimension_semantics=("arbitrary",) * n_axes, vmem_limit_bytes=VMEM_LIMIT_V7X)


def _resident(shape):
    zeros = (0,) * len(shape)
    return pl.BlockSpec(shape, lambda *_: zeros, pipeline_mode=pl.Buffered(1))


def _row_tile(rows, cols):
    return pl.BlockSpec((rows, cols), lambda i: (i, 0))


def _rms_stats(x):
    r = lax.rsqrt(jnp.mean(x * x, axis=-1, keepdims=True) + EPS)
    return r, x * r


def _rms_bwd(dn, x, g):
    r, xh = _rms_stats(x)
    dxh = dn * g
    dx = r * (dxh - xh * jnp.mean(dxh * xh, axis=-1, keepdims=True))
    return dx, jnp.sum(dn * xh, axis=0, keepdims=True)


def _ffn_last(x, target, g, w_in_t, w_out, name):
    t = x.shape[0]
    tm = min(ROW_TILE, t)

    def body(x_ref, t_ref, g_ref, w_ref, wo_ref, n_ref, du_ref, h_ref, dy_ref, dx_ref, sq_ref, dg_ref):
        @pl.when(pl.program_id(0) == 0)
        def _():
            sq_ref[...] = jnp.zeros_like(sq_ref)
            dg_ref[...] = jnp.zeros_like(dg_ref)

        x = x_ref[...]
        g = g_ref[...]
        r, xh = _rms_stats(x)
        n = (xh * g).astype(BF)
        n_ref[...] = n
        u = _dot_nt(n, w_ref[...])
        a = u[:, :F]
        b = u[:, F:]
        s = _sig(a)
        sa = a * s
        h = (sa * b).astype(BF)
        h_ref[...] = h
        err = x + 0.5 * _dot(h, wo_ref[...]) - t_ref[...]
        sq_ref[...] += jnp.sum(err * err, axis=0, keepdims=True)
        dxo = err * (1.0 / D)
        dy = (0.5 * dxo).astype(BF)
        dy_ref[...] = dy
        dh = _dot_nt(dy, wo_ref[...])
        du_ref[:, :F] = (dh * b * (s * (1.0 + a * (1.0 - s)))).astype(BF)
        du_ref[:, F:] = (dh * sa).astype(BF)
        dn = _dot(du_ref[...], w_ref[...])
        dxh = dn * g
        dx_ref[...] = dxo + r * (dxh - xh * jnp.mean(dxh * xh, axis=-1, keepdims=True))
        dg_ref[...] += jnp.sum(dn * xh, axis=0, keepdims=True)

    vec = pl.BlockSpec((1, D), lambda i: (0, 0))
    return pl.pallas_call(
        body, grid=(t // tm,),
        in_specs=[_row_tile(tm, D), _row_tile(tm, D), _resident((1, D)), _resident((INW, D)), _resident((F, D))],
        out_specs=[_row_tile(tm, D), _row_tile(tm, INW), _row_tile(tm, F), _row_tile(tm, D), _row_tile(tm, D), vec, vec],
        out_shape=[jax.ShapeDtypeStruct((t, D), BF), jax.ShapeDtypeStruct((t, INW), BF), jax.ShapeDtypeStruct((t, F), BF),
                   jax.ShapeDtypeStruct((t, D), BF), jax.ShapeDtypeStruct((t, D), F32), jax.ShapeDtypeStruct((1, D), F32),
                   jax.ShapeDtypeStruct((1, D), F32)],
        compiler_params=_params(1), name=name)(x, target, g, w_in_t, w_out)


def _ffn_up_blocks(x, g, n, w_in_t, order, u, name, deps=(), swap=None):
    t = (x if n is None else n).shape[0]
    tm = min(ROW_TILE_BLOCK, t)
    c = INW * 2 // N_DEV
    n_deps = len(deps)
    first = n is None
    assert not first or order.shape == (1,)

    def body(order_ref, *refs):
        refs = refs[n_deps:]
        if first:
            x_ref, g_ref, w_ref, n_ref, u_ref = refs
            nt = (_rms_stats(x_ref[...])[1] * g_ref[...]).astype(BF)
            n_ref[...] = nt
        else:
            n_ref, w_ref, _, u_ref = refs
            nt = n_ref[...]
        u_ref[...] = _dot_nt(nt, w_ref[...]).astype(BF)

    rows = pl.BlockSpec((tm, D), lambda k, i, o: (i, 0))
    block = pl.BlockSpec((c, D), lambda k, i, o: (o[k], 0))
    cols = pl.BlockSpec((tm, c), lambda k, i, o: (i, o[k]))
    u_shape = jax.ShapeDtypeStruct((t, INW), BF)
    if first:
        args, in_specs = (x, g, w_in_t), [rows, _resident((1, D)), block]
        out_specs, out_shape, aliases = [rows, cols], [jax.ShapeDtypeStruct((t, D), BF), u_shape], {}
    else:
        args, in_specs = (n, w_in_t, u), [rows, block, ANY]
        out_specs, out_shape, aliases = [cols], [u_shape], {1 + n_deps + 2: 0}
    grid = (order.shape[0], t // tm)
    if swap is not None:
        (out,), swapped = _call_with_swap(
            body, (*deps, *args), swap, prefetch=(order,), grid=grid, in_specs=[ANY] * n_deps + in_specs, out_specs=out_specs,
            out_shape=out_shape, scratch_shapes=[], input_output_aliases={n_deps + 2: 0}, compiler_params=_params(2), name=name)
        return (n, out), swapped
    out = pl.pallas_call(
        body,
        grid_spec=pltpu.PrefetchScalarGridSpec(num_scalar_prefetch=1, grid=grid, in_specs=[ANY] * n_deps + in_specs,
                                               out_specs=out_specs),
        out_shape=out_shape, input_output_aliases=aliases, compiler_params=_params(2), name=name)(order, *deps, *args)
    return tuple(out) if first else (n, out[0])


def _ffn_down(x, u, w_out, name, part=(0, 1), out=None, swap=None):
    t = x.shape[0]
    tm = min(ROW_TILE_WIDE, t)
    steps = t // tm // part[1]
    first = part[0] * steps
    others = [out] if out is not None else []

    def body(x_ref, u_ref, wo_ref, *rest):
        a = u_ref[:, :F].astype(F32)
        b = u_ref[:, F:].astype(F32)
        h = (a * _sig(a) * b).astype(BF)
        rest[-1][...] = x_ref[...] + 0.5 * _dot(h, wo_ref[...])

    tile = lambda cols: pl.BlockSpec((tm, cols), lambda i: (first + i, 0))
    kw = dict(grid=(steps,), in_specs=[tile(D), tile(INW), _resident((F, D))] + [ANY] * len(others), out_specs=[tile(D)],
              out_shape=[jax.ShapeDtypeStruct((t, D), F32)], scratch_shapes=[],
              input_output_aliases={3: 0} if others else {}, compiler_params=_params(1), name=name)
    args = (x, u, w_out, *others)
    return pl.pallas_call(body, **kw)(*args) if swap is None else _call_with_swap(body, args, swap, **kw)


def _ffn_bwd(dxo, x, g, u, w_in_t, w_out, name, deps=()):
    t = x.shape[0]
    tm = min(ROW_TILE, t)

    def body(dxo_ref, x_ref, g_ref, u_ref, w_ref, wo_ref, dx_ref, du_ref, h_ref, dy_ref, dg_ref):
        dxo = dxo_ref[...]
        dy = (0.5 * dxo).astype(BF)
        dy_ref[...] = dy
        dh = _dot_nt(dy, wo_ref[...])
        a = u_ref[:, :F].astype(F32)
        b = u_ref[:, F:].astype(F32)
        s = _sig(a)
        sa = a * s
        h_ref[...] = (sa * b).astype(BF)
        du_ref[:, :F] = (dh * b * (s * (1.0 + a * (1.0 - s)))).astype(BF)
        du_ref[:, F:] = (dh * sa).astype(BF)
        dn = _dot(du_ref[...], w_ref[...])
        dx, dg = _rms_bwd(dn, x_ref[...], g_ref[...])
        dx_ref[...] = dxo + dx

        @pl.when(pl.program_id(0) == 0)
        def _():
            dg_ref[...] = jnp.zeros_like(dg_ref)

        dg_ref[...] += dg

    return _call(
        body, deps, (dxo, x, g, u, w_in_t, w_out), grid=(t // tm,),
        in_specs=[_row_tile(tm, D), _row_tile(tm, D), _resident((1, D)), _row_tile(tm, INW), _resident((INW, D)),
                  _resident((F, D))],
        out_specs=[_row_tile(tm, D), _row_tile(tm, INW), _row_tile(tm, F), _row_tile(tm, D),
                   pl.BlockSpec((1, D), lambda i: (0, 0))],
        out_shape=[jax.ShapeDtypeStruct((t, D), F32), jax.ShapeDtypeStruct((t, INW), BF), jax.ShapeDtypeStruct((t, F), BF),
                   jax.ShapeDtypeStruct((t, D), BF), jax.ShapeDtypeStruct((1, D), F32)],
        compiler_params=_params(1), name=name)


def _wgrad(lhs, rhs, name, *, lhs_is_transposed, chunk, deps=()):
    t = rhs.shape[0]
    n = lhs.shape[0] if lhs_is_transposed else lhs.shape[1]
    c = min(chunk, n)

    def body(l_ref, r_ref, o_ref):
        if lhs_is_transposed:
            o_ref[...] = _dot(l_ref[...], r_ref[...]).astype(BF)
        else:
            o_ref[...] = _dot_tn(l_ref[...], r_ref[...]).astype(BF)

    lhs_spec = pl.BlockSpec((c, t), lambda j: (j, 0)) if lhs_is_transposed else pl.BlockSpec((t, c), lambda j: (0, j))
    return _call(
        body, deps, (lhs, rhs), grid=(n // c,),
        in_specs=[lhs_spec, _resident((t, D))],
        out_specs=pl.BlockSpec((c, D), lambda j: (j, 0)),
        out_shape=jax.ShapeDtypeStruct((n, D), BF),
        compiler_params=_params(1), name=name)


def _wgrad_mix(duc, dq_t, dkv_t, dgp, hm):
    t = hm.shape[0]
    c = 512
    first_q, first_kv, first_gate = R_Q[0] // c, R_KV[0] // c, R_GATE[0] // c

    def body(uc_ref, q_ref, kv_ref, gp_ref, h_ref, o_ref):
        j = pl.program_id(0)

        @pl.when(j < first_q)
        def _():
            o_ref[...] = _dot_tn(uc_ref[...], h_ref[...]).astype(BF)

        @pl.when((j >= first_q) & (j < first_kv))
        def _():
            o_ref[...] = _dot(q_ref[...], h_ref[...]).astype(BF)

        @pl.when((j >= first_kv) & (j < first_gate))
        def _():
            o_ref[...] = _dot(kv_ref[...], h_ref[...]).astype(BF)

        @pl.when(j >= first_gate)
        def _():
            o_ref[...] = _dot_tn(gp_ref[...], h_ref[...]).astype(BF)

    return pl.pallas_call(
        body, grid=(INW // c,),
        in_specs=[pl.BlockSpec((t, c), lambda j: (0, jnp.clip(j, 0, first_q - 1))),
                  pl.BlockSpec((c, t), lambda j: (jnp.clip(j - first_q, 0, first_kv - first_q - 1), 0)),
                  pl.BlockSpec((c, t), lambda j: (jnp.clip(j - first_kv, 0, first_gate - first_kv - 1), 0)),
                  pl.BlockSpec((t, c), lambda j: (0, jnp.clip(j - first_gate, 0, INW // c - first_gate - 1))),
                  _resident((t, D))],
        out_specs=pl.BlockSpec((c, D), lambda j: (j, 0)),
        out_shape=jax.ShapeDtypeStruct((INW, D), BF),
        compiler_params=_params(1), name="mix_dw_in")(duc, dq_t, dkv_t, dgp, hm)


def _mix_proj(x, g, w_t):
    t = x.shape[0]
    tm = min(ROW_TILE_WIDE, t)

    def body(x_ref, g_ref, w_ref, hm_ref, uc_ref, gp_ref, qkv_ref):
        r, xh = _rms_stats(x_ref[...])
        hm = (xh * g_ref[...]).astype(BF)
        hm_ref[...] = hm
        uc_ref[...] = _dot_nt(hm, w_ref[R_CONV[0]:R_CONV[1], :]).astype(BF)
        gp_ref[...] = _dot_nt(hm, w_ref[R_GATE[0]:R_GATE[1], :]).astype(BF)
        qkv_ref[...] = _dot_nt(w_ref[R_QKV[0]:R_QKV[1], :], hm).astype(BF)

    return pl.pallas_call(
        body, grid=(t // tm,),
        in_specs=[_row_tile(tm, D), _resident((1, D)), _resident((INW, D))],
        out_specs=[_row_tile(tm, D), _row_tile(tm, 2 * D), _row_tile(tm, 2 * D), pl.BlockSpec((1536, tm), lambda i: (0, i))],
        out_shape=[jax.ShapeDtypeStruct((t, D), BF), jax.ShapeDtypeStruct((t, 2 * D), BF),
                   jax.ShapeDtypeStruct((t, 2 * D), BF), jax.ShapeDtypeStruct((1536, t), BF)],
        compiler_params=_params(1), name="mix_proj")(x, g, w_t)


CONV_HALO = 32
CONV_LEAD = CONV_HALO - (CW - 1)


def _glu(uc):
    uc = uc.astype(F32)
    return uc[:, :D] * _sig(uc[:, D:])


def _ln_stats(zc):
    mu = jnp.mean(zc, axis=-1, keepdims=True)
    zm = zc - mu
    r = lax.rsqrt(jnp.mean(zm * zm, axis=-1, keepdims=True) + EPS)
    return r, zm * r


CONV_SHIFTS = 8
CONV_CHUNK = 32


def _store_shifted(buf, rows):
    for b in range(1, CONV_SHIFTS):
        buf[b, 0:rows - 8, :] = buf[0, pl.ds(b, rows - 8), :]


def _conv_fwd(uc, dwk, dwb, lng, lnb, swap=None):
    t = uc.shape[0]
    tm = min(512, t)
    per = tm // CONV_HALO
    ext = tm + CONV_HALO

    def body(cur_ref, prev_ref, k_ref, kb_ref, g_ref, b_ref, o_ref, zc_ref, zsh):
        i = pl.program_id(0)
        zsh[0, 0:CONV_HALO, :] = _glu(prev_ref[...]) * (i > 0).astype(F32)
        zsh[0, CONV_HALO:, :] = _glu(cur_ref[...])
        _store_shifted(zsh, ext)

        def chunk(ci, carry):
            r0 = pl.multiple_of(ci * CONV_CHUNK, CONV_CHUNK)
            acc = jnp.zeros((CONV_CHUNK, D), F32) + kb_ref[...]
            for w in range(CW):
                a, b = divmod(CONV_LEAD + w, 8)
                acc = acc + k_ref[w:w + 1, :] * zsh[b, pl.ds(r0 + 8 * a, CONV_CHUNK), :]
            zc_ref[pl.ds(r0, CONV_CHUNK), :] = acc
            return carry

        lax.fori_loop(0, tm // CONV_CHUNK, chunk, 0)
        r, xh = _ln_stats(zc_ref[...])
        y = xh * g_ref[...] + b_ref[...]
        o_ref[...] = (y * _sig(y)).astype(BF)

    kw = dict(
        grid=(t // tm,),
        in_specs=[_row_tile(tm, 2 * D),
                  pl.BlockSpec((CONV_HALO, 2 * D), lambda i: (jnp.maximum(i * per - 1, 0), 0)),
                  _resident((CWP, D)), _resident((1, D)), _resident((1, D)), _resident((1, D))],
        out_specs=[_row_tile(tm, D), _row_tile(tm, D)],
        out_shape=[jax.ShapeDtypeStruct((t, D), BF), jax.ShapeDtypeStruct((t, D), F32)],
        scratch_shapes=[pltpu.VMEM((CONV_SHIFTS, ext, D), F32)],
        compiler_params=_params(1), name="conv_fwd")
    args = (uc, uc, dwk, dwb, lng, lnb)
    return pl.pallas_call(body, **kw)(*args) if swap is None else _call_with_swap(body, args, swap, **kw)


def _conv_bwd(uc, zc, dzs, dwk, lng, lnb):
    t = uc.shape[0]
    tm = min(ROW_TILE_WIDE, t)
    per = tm // CONV_HALO
    n_tiles = t // tm
    ext = tm + CONV_HALO
    last_block = t // CONV_HALO - 1

    def body(cur_ref, zc_ref, zcn_ref, dz_ref, dzn_ref, k_ref, g_ref, b_ref,
             duc_ref, dk_ref, dkb_ref, dg_ref, db_ref, dsh, dk8, z_scr):
        i = pl.program_id(0)

        @pl.when(i == 0)
        def _():
            dk8[...] = jnp.zeros_like(dk8)
            dkb_ref[...] = jnp.zeros_like(dkb_ref)
            dg_ref[...] = jnp.zeros_like(dg_ref)
            db_ref[...] = jnp.zeros_like(db_ref)

        has_next = (i < n_tiles - 1).astype(F32)
        z_scr[...] = _glu(cur_ref[...])
        gain = g_ref[...]

        def ln_silu_bwd(zc, dzs, live):
            r, xh = _ln_stats(zc)
            y = xh * gain + b_ref[...]
            sy = _sig(y)
            dy = dzs * (sy * (1.0 + y * (1.0 - sy))) * live
            dxh = dy * gain
            dzc = r * (dxh - jnp.mean(dxh, axis=-1, keepdims=True) - xh * jnp.mean(dxh * xh, axis=-1, keepdims=True))
            return dzc, dy, xh

        dzc, dy, xh = ln_silu_bwd(zc_ref[...], dz_ref[...], 1.0)
        dsh[0, 0:tm, :] = dzc
        dg_ref[...] += jnp.sum(dy * xh, axis=0, keepdims=True)
        db_ref[...] += jnp.sum(dy, axis=0, keepdims=True)
        dkb_ref[...] += jnp.sum(dzc, axis=0, keepdims=True)
        dsh[0, tm:, :] = ln_silu_bwd(zcn_ref[...], dzn_ref[...], has_next)[0]
        _store_shifted(dsh, ext)

        def chunk(ci, carry):
            r0 = pl.multiple_of(ci * CONV_CHUNK, CONV_CHUNK)
            z_c = z_scr[pl.ds(r0, CONV_CHUNK), :]
            dz = jnp.zeros((CONV_CHUNK, D), F32)
            for w in range(CW):
                a, b = divmod(CW - 1 - w, 8)
                window = dsh[b, pl.ds(r0 + 8 * a, CONV_CHUNK), :]
                dz = dz + k_ref[w:w + 1, :] * window
                prod = z_c * window
                part = prod[0:8, :]
                for j in range(1, CONV_CHUNK // 8):
                    part = part + prod[8 * j:8 * j + 8, :]
                dk8[w] += part
            ucc = cur_ref[pl.ds(r0, CONV_CHUNK), :].astype(F32)
            sg = _sig(ucc[:, D:])
            duc_ref[pl.ds(r0, CONV_CHUNK), 0:D] = (dz * sg).astype(BF)
            duc_ref[pl.ds(r0, CONV_CHUNK), D:2 * D] = (dz * ucc[:, :D] * sg * (1.0 - sg)).astype(BF)
            return carry

        lax.fori_loop(0, tm // CONV_CHUNK, chunk, 0)

        @pl.when(i == n_tiles - 1)
        def _():
            dk_ref[...] = jnp.sum(dk8[...], axis=1)

    vec = pl.BlockSpec((1, D), lambda i: (0, 0))
    next_halo = pl.BlockSpec((CONV_HALO, D), lambda i: (jnp.minimum((i + 1) * per, last_block), 0))
    return pl.pallas_call(
        body, grid=(n_tiles,),
        in_specs=[_row_tile(tm, 2 * D), _row_tile(tm, D), next_halo, _row_tile(tm, D), next_halo,
                  _resident((CWP, D)), _resident((1, D)), _resident((1, D))],
        out_specs=[_row_tile(tm, 2 * D), pl.BlockSpec((CWP, D), lambda i: (0, 0)), vec, vec, vec],
        out_shape=[jax.ShapeDtypeStruct((t, 2 * D), BF), jax.ShapeDtypeStruct((CWP, D), F32),
                   jax.ShapeDtypeStruct((1, D), F32), jax.ShapeDtypeStruct((1, D), F32), jax.ShapeDtypeStruct((1, D), F32)],
        scratch_shapes=[pltpu.VMEM((CONV_SHIFTS, ext, D), F32), pltpu.VMEM((CWP, 8, D), F32), pltpu.VMEM((tm, D), F32)],
        compiler_params=_params(1), name="conv_bwd")(uc, zc, zc, dzs, dzs, dwk, lng, lnb)


def _norm_rows(xt, g):
    r = lax.rsqrt(jnp.mean(xt * xt, axis=0, keepdims=True) + EPS)
    xh = xt * r
    return xh * g, r, xh


ATT_TQ = 1024


def _attn_specs(t, tq):
    per = tq // BLK
    return [pl.BlockSpec((1536, tq), lambda i: (0, i)),
            pl.BlockSpec((512, BLK), lambda i: (2, jnp.maximum(i * per - 1, 0))),
            _resident((HD, 1)), _resident((HD, 1)), _resident((NKV, 1, GRP * BLK)),
            _resident((2, NKV, 2 * BLK, GRP * BLK))]


def _attn_window(hk, sb, qkv_ref, halo_ref, kn_cur, kn_halo):
    v0 = D + NKV * HD + hk * HD
    if sb == 0:
        k_prev = kn_halo[hk]
        v_prev = halo_ref[NKV * HD + hk * HD:NKV * HD + (hk + 1) * HD, :]
    else:
        k_prev = kn_cur[hk][:, (sb - 1) * BLK:sb * BLK]
        v_prev = qkv_ref[v0:v0 + HD, (sb - 1) * BLK:sb * BLK]
    kw = jnp.concatenate([k_prev, kn_cur[hk][:, sb * BLK:(sb + 1) * BLK]], axis=1).astype(BF)
    vw = jnp.concatenate([v_prev, qkv_ref[v0:v0 + HD, sb * BLK:(sb + 1) * BLK]], axis=1)
    return kw, vw


def _attn_probs(kw, qc, bias, sink):
    st = _dot_tn(kw, qc) + bias
    m = jnp.maximum(jnp.max(st, axis=0, keepdims=True), sink)
    p = jnp.exp(st - m)
    e_sink = jnp.exp(sink - m)
    inv = 1.0 / (jnp.sum(p, axis=0, keepdims=True) + e_sink)
    return p * inv, e_sink * inv


def _attn_fwd(qkv_t, qg, kg, sink_rows, bias_t):
    t = qkv_t.shape[1]
    tq = min(ATT_TQ, t)
    n_sub = tq // BLK

    def body(qkv_ref, halo_ref, qg_ref, kg_ref, sink_ref, bias_ref, o_ref, p_ref, ps_ref):
        i = pl.program_id(0)
        first = (i == 0).astype(jnp.int32)
        kgain = kg_ref[...]
        qgain = qg_ref[...]
        kn_cur = [_norm_rows(qkv_ref[D + h * HD:D + (h + 1) * HD, :].astype(F32), kgain)[0] for h in range(NKV)]
        kn_halo = [_norm_rows(halo_ref[h * HD:(h + 1) * HD, :].astype(F32), kgain)[0] for h in range(NKV)]
        for hk in range(NKV):
            for sb in range(n_sub):
                cols = slice(sb * BLK, (sb + 1) * BLK)
                kw, vw = _attn_window(hk, sb, qkv_ref, halo_ref, kn_cur, kn_halo)
                qc = jnp.concatenate(
                    [_norm_rows(qkv_ref[(GRP * hk + g) * HD:(GRP * hk + g + 1) * HD, cols].astype(F32), qgain)[0] * QK_SCALE
                     for g in range(GRP)], axis=1).astype(BF)
                bias = bias_ref[first, hk] if sb == 0 else bias_ref[0, hk]
                p, p_sink = _attn_probs(kw, qc, bias, sink_ref[hk])
                p = p.astype(BF)
                p_ref[sb, hk] = p
                ps_ref[sb, hk] = p_sink
                o = _dot(vw, p)
                for g in range(GRP):
                    head = GRP * hk + g
                    o_ref[head * HD:(head + 1) * HD, cols] = o[:, g * BLK:(g + 1) * BLK].astype(BF)

    return pl.pallas_call(
        body, grid=(t // tq,),
        in_specs=_attn_specs(t, tq),
        out_specs=[pl.BlockSpec((D, tq), lambda i: (0, i)),
                   pl.BlockSpec((n_sub, NKV, 2 * BLK, GRP * BLK), lambda i: (i, 0, 0, 0)),
                   pl.BlockSpec((n_sub, NKV, 1, GRP * BLK), lambda i: (i, 0, 0, 0))],
        out_shape=[jax.ShapeDtypeStruct((D, t), BF), jax.ShapeDtypeStruct((t // BLK, NKV, 2 * BLK, GRP * BLK), BF),
                   jax.ShapeDtypeStruct((t // BLK, NKV, 1, GRP * BLK), F32)],
        compiler_params=_params(1), name="attn_fwd")(qkv_t, qkv_t, qg, kg, sink_rows, bias_t)


def _attn_bwd(qkv_t, do_t, probs, sink_probs, qg, kg, onehot_t, deps=()):
    t = qkv_t.shape[1]
    tq = min(ATT_TQ, t)
    n_sub = tq // BLK
    n_tiles = t // tq

    def body(qkv_ref, halo_ref, do_ref, p_ref, ps_ref, qg_ref, kg_ref, oh_ref,
             dq_ref, ckv_ref, dqg_ref, dsink_ref, dbias_ref, qg_scr, sink_scr, ds_scr):
        i = pl.program_id(0)

        @pl.when(i == 0)
        def _():
            qg_scr[...] = jnp.zeros_like(qg_scr)
            sink_scr[...] = jnp.zeros_like(sink_scr)
            ds_scr[...] = jnp.zeros_like(ds_scr)

        kgain = kg_ref[...]
        qgain = qg_ref[...]
        kn_cur = [_norm_rows(qkv_ref[D + h * HD:D + (h + 1) * HD, :].astype(F32), kgain)[0] for h in range(NKV)]
        kn_halo = [_norm_rows(halo_ref[h * HD:(h + 1) * HD, :].astype(F32), kgain)[0] for h in range(NKV)]
        dqg = jnp.zeros((HD, BLK), F32)
        for hk in range(NKV):
            for sb in range(n_sub):
                cols = slice(sb * BLK, (sb + 1) * BLK)
                kw, vw = _attn_window(hk, sb, qkv_ref, halo_ref, kn_cur, kn_halo)
                qn, qr, qh = [], [], []
                for g in range(GRP):
                    head = GRP * hk + g
                    n_, r_, h_ = _norm_rows(qkv_ref[head * HD:(head + 1) * HD, cols].astype(F32), qgain)
                    qn.append(n_)
                    qr.append(r_)
                    qh.append(h_)
                qc = (jnp.concatenate(qn, axis=1) * QK_SCALE).astype(BF)
                p_bf = p_ref[sb, hk]
                p = p_bf.astype(F32)
                doc = jnp.concatenate([do_ref[(GRP * hk + g) * HD:(GRP * hk + g + 1) * HD, cols] for g in range(GRP)], axis=1)
                dp = _dot_tn(vw, doc)
                delta = jnp.sum(p * dp, axis=0, keepdims=True)
                ds = p * (dp - delta)
                sink_scr[hk] += -(ps_ref[sb, hk] * delta)
                ds_scr[hk] += ds
                dsb = ds.astype(BF)
                dqc = _dot(kw, dsb) * QK_SCALE
                ckv_ref[sb, hk * HD:(hk + 1) * HD, :] = _dot_nt(qc, dsb)
                ckv_ref[sb, NKV * HD + hk * HD:NKV * HD + (hk + 1) * HD, :] = _dot_nt(doc, p_bf)
                for g in range(GRP):
                    head = GRP * hk + g
                    dqn = dqc[:, g * BLK:(g + 1) * BLK]
                    dqh = dqn * qgain
                    dq = qr[g] * (dqh - qh[g] * jnp.mean(dqh * qh[g], axis=0, keepdims=True))
                    dq_ref[head * HD:(head + 1) * HD, cols] = dq.astype(BF)
                    dqg = dqg + dqn * qh[g]
        qg_scr[...] += dqg

        @pl.when(i == n_tiles - 1)
        def _():
            dqg_ref[...] = jnp.sum(qg_scr[...], axis=1, keepdims=True)
            dsink_ref[...] = _group_lane_sums(sink_scr[:, 0, :])

            def bucket(b, carry):
                oh = jnp.concatenate([oh_ref[b]] * GRP, axis=1)
                dbias_ref[b] = _group_lane_sums(jnp.sum(ds_scr[...] * oh[None], axis=1))
                return carry

            lax.fori_loop(0, NBUCKET, bucket, 0)

    return _call(
        body, deps, (qkv_t, qkv_t, do_t, probs, sink_probs, qg, kg, onehot_t), grid=(n_tiles,),
        in_specs=_attn_specs(t, tq)[:2] + [pl.BlockSpec((D, tq), lambda i: (0, i)),
                                           pl.BlockSpec((n_sub, NKV, 2 * BLK, GRP * BLK), lambda i: (i, 0, 0, 0)),
                                           pl.BlockSpec((n_sub, NKV, 1, GRP * BLK), lambda i: (i, 0, 0, 0))]
        + _attn_specs(t, tq)[2:4] + [_resident((NBUCKET, 2 * BLK, BLK))],
        out_specs=[pl.BlockSpec((D, tq), lambda i: (0, i)),
                   pl.BlockSpec((n_sub, 2 * NKV * HD, 2 * BLK), lambda i: (i, 0, 0)),
                   pl.BlockSpec((HD, 1), lambda i: (0, 0)),
                   pl.BlockSpec((NKV, BLK), lambda i: (0, 0)),
                   pl.BlockSpec((NBUCKET, NKV, BLK), lambda i: (0, 0, 0))],
        out_shape=[jax.ShapeDtypeStruct((D, t), BF),
                   jax.ShapeDtypeStruct((t // BLK, 2 * NKV * HD, 2 * BLK), F32),
                   jax.ShapeDtypeStruct((HD, 1), F32),
                   jax.ShapeDtypeStruct((NKV, BLK), F32),
                   jax.ShapeDtypeStruct((NBUCKET, NKV, BLK), F32)],
        scratch_shapes=[pltpu.VMEM((HD, BLK), F32), pltpu.VMEM((NKV, 1, GRP * BLK), F32),
                        pltpu.VMEM((NKV, 2 * BLK, GRP * BLK), F32)],
        compiler_params=_params(1), name="attn_bwd")


def _kv_combine_tile(c_ref, cn_ref, has_next, k_ref, kgain, o_ref):
    rows = NKV * HD
    per = c_ref.shape[0]
    dkg = jnp.zeros((HD, BLK), F32)
    for s in range(per):
        cols = slice(s * BLK, (s + 1) * BLK)
        after = c_ref[s + 1, :, :BLK] if s + 1 < per else cn_ref[0, :, :BLK] * has_next
        d = c_ref[s, :, BLK:] + after
        o_ref[rows:, cols] = d[rows:, :].astype(BF)
        for h in range(NKV):
            _, r, kh = _norm_rows(k_ref[h * HD:(h + 1) * HD, cols].astype(F32), kgain)
            dkn = d[h * HD:(h + 1) * HD, :]
            dkh = dkn * kgain
            o_ref[h * HD:(h + 1) * HD, cols] = (r * (dkh - kh * jnp.mean(dkh * kh, axis=0, keepdims=True))).astype(BF)
            dkg = dkg + dkn * kh
    return dkg


def _group_lane_sums(v):
    lane_group = lax.broadcasted_iota(jnp.int32, (1, GRP * BLK), 1) // BLK
    col = lax.broadcasted_iota(jnp.int32, (1, BLK), 1)
    out = jnp.zeros((v.shape[0], BLK), F32)
    for g in range(GRP):
        s = jnp.sum(jnp.where(lane_group == g, v, 0.0), axis=1, keepdims=True)
        out = jnp.where(col == g, s, out)
    return out


def _mix_out(zs, o_t, gp, x, w_cp, w_o, w_out):
    t = x.shape[0]
    tm = min(ROW_TILE_WIDE, t)

    def body(zs_ref, ot_ref, gp_ref, x_ref, wcp_ref, wo_ref, wout_ref, xo_ref, a_ref, b_ref, m_ref):
        a = _dot(zs_ref[...], wcp_ref[...])
        b = _dot_tn(ot_ref[...], wo_ref[...])
        a_ref[...] = a.astype(BF)
        b_ref[...] = b.astype(BF)
        merged = (_sig(gp_ref[:, :D].astype(F32)) * a + _sig(gp_ref[:, D:].astype(F32)) * b).astype(BF)
        m_ref[...] = merged
        xo_ref[...] = x_ref[...] + _dot(merged, wout_ref[...])

    return pl.pallas_call(
        body, grid=(t // tm,),
        in_specs=[_row_tile(tm, D), pl.BlockSpec((D, tm), lambda i: (0, i)), _row_tile(tm, 2 * D), _row_tile(tm, D),
                  _resident((D, D)), _resident((D, D)), _resident((D, D))],
        out_specs=[_row_tile(tm, D)] * 4,
        out_shape=[jax.ShapeDtypeStruct((t, D), F32)] + [jax.ShapeDtypeStruct((t, D), BF)] * 3,
        compiler_params=_params(1), name="mix_out")(zs, o_t, gp, x, w_cp, w_o, w_out)


def _mix_out_bwd(dx, a, b, gp, w_cp, w_o, w_out, deps=()):
    t = dx.shape[0]
    tm = min(ROW_TILE_WIDE, t)

    def body(dx_ref, a_ref, b_ref, gp_ref, wcp_ref, wo_ref, wout_ref, dzs_ref, dot_ref, dgp_ref, da_ref, db_ref, dxb_ref):
        dxb = dx_ref[...].astype(BF)
        dxb_ref[...] = dxb
        dm = _dot_nt(dxb, wout_ref[...])
        gc = _sig(gp_ref[:, :D].astype(F32))
        ga = _sig(gp_ref[:, D:].astype(F32))
        da = (dm * gc).astype(BF)
        db = (dm * ga).astype(BF)
        da_ref[...] = da
        db_ref[...] = db
        dgp_ref[:, :D] = (dm * a_ref[...].astype(F32) * gc * (1.0 - gc)).astype(BF)
        dgp_ref[:, D:] = (dm * b_ref[...].astype(F32) * ga * (1.0 - ga)).astype(BF)
        dzs_ref[...] = _dot_nt(da, wcp_ref[...])
        dot_ref[...] = _dot_nt(wo_ref[...], db).astype(BF)

    return _call(
        body, deps, (dx, a, b, gp, w_cp, w_o, w_out), grid=(t // tm,),
        in_specs=[_row_tile(tm, D), _row_tile(tm, D), _row_tile(tm, D), _row_tile(tm, 2 * D),
                  _resident((D, D)), _resident((D, D)), _resident((D, D))],
        out_specs=[_row_tile(tm, D), pl.BlockSpec((D, tm), lambda i: (0, i)), _row_tile(tm, 2 * D),
                   _row_tile(tm, D), _row_tile(tm, D), _row_tile(tm, D)],
        out_shape=[jax.ShapeDtypeStruct((t, D), F32), jax.ShapeDtypeStruct((D, t), BF), jax.ShapeDtypeStruct((t, 2 * D), BF),
                   jax.ShapeDtypeStruct((t, D), BF), jax.ShapeDtypeStruct((t, D), BF), jax.ShapeDtypeStruct((t, D), BF)],
        compiler_params=_params(1), name="mix_out_bwd")


def _mix_proj_bwd(dxo, duc, dq_t, ckv, qkv_t, kg, dgp, x, g, w_t):
    t = x.shape[0]
    tm = min(ROW_TILE_WIDE, t)
    per = tm // BLK
    steps = t // tm
    kv_rows = 2 * NKV * HD

    def body(dxo_ref, duc_ref, dq_ref, c_ref, cn_ref, k_ref, kg_ref, dgp_ref, x_ref, g_ref, w_ref,
             dx_ref, dg_ref, dkv_ref, dkg_ref, kg_scr):
        i = pl.program_id(0)

        @pl.when(i == 0)
        def _():
            dg_ref[...] = jnp.zeros_like(dg_ref)
            kg_scr[...] = jnp.zeros_like(kg_scr)

        kg_scr[...] += _kv_combine_tile(c_ref, cn_ref, (i < steps - 1).astype(F32), k_ref, kg_ref[...], dkv_ref)
        dn = _dot(duc_ref[...], w_ref[R_CONV[0]:R_CONV[1], :])
        dn = dn + _dot(dgp_ref[...], w_ref[R_GATE[0]:R_GATE[1], :])
        dn = dn + _dot_tn(dq_ref[...], w_ref[R_Q[0]:R_Q[1], :])
        dn = dn + _dot_tn(dkv_ref[...], w_ref[R_KV[0]:R_KV[1], :])
        dx, dg = _rms_bwd(dn, x_ref[...], g_ref[...])
        dx_ref[...] = dxo_ref[...] + dx
        dg_ref[...] += dg

        @pl.when(i == steps - 1)
        def _():
            dkg_ref[...] = jnp.sum(kg_scr[...], axis=1, keepdims=True)

    return pl.pallas_call(
        body, grid=(steps,),
        in_specs=[_row_tile(tm, D), _row_tile(tm, 2 * D), pl.BlockSpec((D, tm), lambda i: (0, i)),
                  pl.BlockSpec((per, kv_rows, 2 * BLK), lambda i: (i, 0, 0)),
                  pl.BlockSpec((1, kv_rows, 2 * BLK), lambda i: (jnp.minimum((i + 1) * per, t // BLK - 1), 0, 0)),
                  pl.BlockSpec((NKV * HD, tm), lambda i: (D // (NKV * HD), i)), _resident((HD, 1)),
                  _row_tile(tm, 2 * D), _row_tile(tm, D), _resident((1, D)), _resident((INW, D))],
        out_specs=[_row_tile(tm, D), pl.BlockSpec((1, D), lambda i: (0, 0)), pl.BlockSpec((kv_rows, tm), lambda i: (0, i)),
                   pl.BlockSpec((HD, 1), lambda i: (0, 0))],
        out_shape=[jax.ShapeDtypeStruct((t, D), F32), jax.ShapeDtypeStruct((1, D), F32),
                   jax.ShapeDtypeStruct((kv_rows, t), BF), jax.ShapeDtypeStruct((HD, 1), F32)],
        scratch_shapes=[pltpu.VMEM((HD, BLK), F32)],
        compiler_params=_params(1), name="mix_proj_bwd")(dxo, duc, dq_t, ckv, ckv, qkv_t, kg, dgp, x, g, w_t)


def _attention_tables():
    kj = np.arange(2 * BLK)[:, None]
    qi = np.arange(BLK)[None, :]
    dist = qi + BLK - kj
    in_win = (dist >= 0) & (dist < BLK)
    dpos = np.maximum(dist, 0)
    max_exact = NBUCKET // 2
    dfl = np.maximum(dpos, 1).astype(np.float32)
    large = max_exact + (np.log(dfl / np.float32(max_exact)) / np.float32(math.log(BLK / max_exact))
                         * np.float32(NBUCKET - max_exact)).astype(np.int32)
    large = np.minimum(large, NBUCKET - 1)
    bucket = np.where(dpos < max_exact, dpos, large)
    onehot = (bucket[None] == np.arange(NBUCKET)[:, None, None]).astype(np.float32)
    mask = in_win.astype(np.float32)
    mask_first = mask * (kj >= BLK)
    masks = np.stack([np.tile(mask, (1, GRP)), np.tile(mask_first, (1, GRP))])
    return onehot, masks


def _bias_table(rel_bias, onehot):
    tab = jnp.einsum("bkq,bh->hkq", onehot, rel_bias, precision=lax.Precision.HIGHEST)
    tab = tab.reshape(NKV, GRP, 2 * BLK, BLK)
    return jnp.transpose(tab, (0, 2, 1, 3)).reshape(NKV, 2 * BLK, GRP * BLK)


def _local_step(x, target, vec, ffn1_up, weights_of, wgrad, wgrads, grads_done, small_done):
    onehot_np, masks_np = _attention_tables()
    onehot = jnp.asarray(onehot_np)
    masks = jnp.asarray(masks_np)
    bias_t = jnp.where(masks[:, None] > 0.5, _bias_table(vec["rel_bias"], onehot)[None], NEG)
    sink_rows = jnp.repeat(vec["attn_sinks"].reshape(NKV, 1, GRP), BLK, axis=2)
    qg = vec["q_norm"].reshape(HD, 1)
    kg = vec["k_norm"].reshape(HD, 1)
    g1 = vec["ffn1_norm"].reshape(1, D)
    gm = vec["mix_norm"].reshape(1, D)
    g2 = vec["ffn2_norm"].reshape(1, D)
    dwb = vec["conv_dw_bias"].reshape(1, D)
    lng = vec["conv_ln_g"].reshape(1, D)
    lnb = vec["conv_ln_b"].reshape(1, D)

    n1, u1, w1 = ffn1_up(x, g1, (bias_t, sink_rows))
    x1, = _ffn_down(x, u1, w1["ffn1_w_out"], "ffn1_down_first", part=(0, 2))
    (x1,), wm = weights_of("mix_proj", (x1,), during=functools.partial(
        _ffn_down, x, u1, w1["ffn1_w_out"], "ffn1_down", part=(1, 2), out=x1))
    dwk = jnp.pad(wm["conv_dw_kernel"], ((0, CWP - CW), (0, 0)))
    hm, uc, gp, qkv_t = _mix_proj(x1, gm, wm["w_in"])
    (zs, zc), merge = weights_of("mix_merge", (uc,), during=functools.partial(_conv_fwd, uc, dwk, dwb, lng, lnb))
    wm.update(merge)
    o_t, probs, sink_probs = _attn_fwd(qkv_t, qg, kg, sink_rows, bias_t)
    x2, a, b, merged = _mix_out(zs, o_t, gp, x1, wm["conv_w_proj"], wm["attn_w_o"], wm["w_out"])
    w2 = weights_of("ffn2", (x2,))
    gv = {}
    n2, du2, h2, dy2, dx2, sq, gv["ffn2_norm"] = _ffn_last(x2, target, g2, w2["ffn2_w_in"], w2["ffn2_w_out"], "ffn2")

    deps = grads_done("ffn2", {"ffn2_w_in": wgrad(du2, n2, "ffn2_dw_in", False),
                               "ffn2_w_out": wgrad(h2, dy2, "ffn2_dw_out", False)})

    dzs, do_t, dgp, da, db, dx2b = _mix_out_bwd(dx2, a, b, gp, wm["conv_w_proj"], wm["attn_w_o"], wm["w_out"], deps=deps)
    grads = wgrads([(merged, dx2b, False), (zs, da, False), (o_t, db, True)], "mix_dw_merge")
    deps = grads_done("mix_out", dict(zip(("w_out", "conv_w_proj", "attn_w_o"), grads)))

    dq_t, ckv, dqg, dsink, dbias = _attn_bwd(qkv_t, do_t, probs, sink_probs, qg, kg, onehot, deps=deps)
    gv["q_norm"] = dqg.reshape(HD)
    gv["attn_sinks"] = dsink[:, :GRP].reshape(NQ)
    gv["rel_bias"] = dbias[:, :, :GRP].reshape(NBUCKET, NQ)

    duc, dk_conv, gv["conv_dw_bias"], gv["conv_ln_g"], gv["conv_ln_b"] = _conv_bwd(uc, zc, dzs, dwk, lng, lnb)
    gv["conv_dw_kernel"] = dk_conv[:CW]

    dx1, gv["mix_norm"], dkv_t, dkg = _mix_proj_bwd(dx2, duc, dq_t, ckv, qkv_t, kg, dgp, x1, gm, wm["w_in"])
    gv["k_norm"] = dkg.reshape(HD)
    deps = grads_done("mix_in", {"w_in": _wgrad_mix(duc, dq_t, dkv_t, dgp, hm)})

    dx0, du1, h1, dy1, gv["ffn1_norm"] = _ffn_bwd(dx1, x, g1, u1, w1["ffn1_w_in"], w1["ffn1_w_out"], "ffn1_bwd", deps=deps)
    for k in ("ffn1_norm", "mix_norm", "ffn2_norm", "conv_dw_bias", "conv_ln_g", "conv_ln_b"):
        gv[k] = gv[k].reshape(D)
    deps = small_done(gv, sq)
    deps = grads_done("ffn1_in", {"ffn1_w_in": wgrad(du1, n1, "ffn1_dw_in", False, deps)})
    grads_done("ffn1_out", {"ffn1_w_out": wgrad(h1, dy1, "ffn1_dw_out", False, deps)})
    return dx0


MESH_ID = pl.DeviceIdType.MESH


def _position():
    return lax.axis_index("x"), lax.axis_index("y"), lax.axis_index("c")


def _shard_rows(ref, index, rows):
    return ref.at[pl.ds(pl.multiple_of(index * rows, 16), rows), :]


def _prep(weights, taps, me, name, deps=(), swap=None):
    n = len(weights)
    n_deps = len(deps)
    with_taps = taps is not None

    def body(me_ref, *refs):
        refs = refs[n_deps:]
        ins, outs = refs[:len(refs) // 2], refs[len(refs) // 2:]
        for k in range(n):
            outs[k][...] = ins[k][...].astype(BF)
        if with_taps:
            outs[n][0:CW, :] = ins[n][...]
            outs[n][CW:, :] = jnp.zeros((CWP - CW, BLK), F32)

    shard_shapes = [w.shape for w in weights] + [(CWP, BLK)] * with_taps
    dtypes = [BF] * n + [F32] * with_taps
    ins = list(weights) + [taps] * with_taps
    in_specs = [ANY] * n_deps + [pl.BlockSpec(a.shape, lambda i, m: (0, 0), pipeline_mode=pl.Buffered(1)) for a in ins]
    out_specs = [pl.BlockSpec(s, lambda i, m: (m[0], 0)) for s in shard_shapes]
    out_shape = [jax.ShapeDtypeStruct((N_DEV * s[0], s[1]), d) for s, d in zip(shard_shapes, dtypes)]
    if swap is not None:
        return _call_with_swap(body, (*deps, *ins), swap, prefetch=(me,), grid=(1,), in_specs=in_specs, out_specs=out_specs,
                               out_shape=out_shape, scratch_shapes=[], compiler_params=_params(1), name=name)
    return pl.pallas_call(
        body,
        grid_spec=pltpu.PrefetchScalarGridSpec(num_scalar_prefetch=1, grid=(1,), in_specs=in_specs, out_specs=out_specs),
        out_shape=out_shape, compiler_params=_params(1), name=name)(me, *deps, *ins)


HBM = pl.BlockSpec(memory_space=pltpu.HBM)
SEM = pl.BlockSpec(memory_space=pltpu.SEMAPHORE)
DATAFLOW = pltpu.SideEffectType.DATAFLOW_SIDE_EFFECTING
TOKEN = jax.ShapeDtypeStruct((8, 128), F32)


def _in_hbm(x):
    return pltpu.with_memory_space_constraint(x, pltpu.HBM)


def _hbm_like(arrays):
    return [pltpu.HBM(a.shape, a.dtype) for a in arrays]


def _other_chips(x, y):
    return [(1 - x, y), (x, 1 - y), (1 - x, 1 - y)]


def _device_index(chip, c):
    return 4 * chip[0] + 2 * chip[1] + c


def _chip_index(chip):
    return 2 * chip[0] + chip[1]


class _Exchange:
    def __init__(self, gather, all_cores=False):
        self.gather = gather
        self.all_cores = all_cores
        self.n_peers = N_DEV - 1 if all_cores else 3

    def peers(self, x, y, c):
        if self.all_cores:
            return [(x ^ (k >> 2), y ^ ((k >> 1) & 1), c ^ (k & 1)) for k in range(1, N_DEV)]
        return [(*chip, c) for chip in _other_chips(x, y)]

    def sent(self, x, y, c, peer):
        return _device_index((x, y), c) if self.gather else _chip_index(peer[:2])

    def lands_at(self, x, y, c):
        return _device_index((x, y), c) if self.gather else _chip_index((x, y))

    def arrives_at(self, peer):
        return _device_index(peer[:2], peer[2]) if self.gather else _chip_index(peer[:2])


def _ici_copies_start(sets, sources, landings, exchanges, name, deps=()):
    n = len(landings)
    arrays = (list(sources) if sources is not None else []) + list(landings)
    first_land = len(arrays) - n
    n_sets = len(sets)
    n_deps = len(deps)

    def body(*refs):
        refs = refs[n_deps:]
        src, land = refs[:n], refs[first_land:first_land + n]
        sems = refs[len(arrays):len(arrays) + 2 * n_sets]
        token = refs[-1]
        x, y, c = _position()
        for s, (members, exchange) in enumerate(zip(sets, exchanges)):
            for slot, (k, rows) in enumerate(members):
                for j, peer in enumerate(exchange.peers(x, y, c)):
                    at = exchange.n_peers * slot + j
                    pltpu.make_async_remote_copy(
                        src_ref=_shard_rows(src[k], exchange.sent(x, y, c, peer), rows),
                        dst_ref=_shard_rows(land[k], exchange.lands_at(x, y, c), rows),
                        send_sem=sems[2 * s].at[at], recv_sem=sems[2 * s + 1].at[at],
                        device_id=peer, device_id_type=MESH_ID).start()
        token[...] = jnp.zeros_like(token)

    sem_shapes = []
    for members, exchange in zip(sets, exchanges):
        sem_shapes += [pltpu.SemaphoreType.DMA((exchange.n_peers * len(members),))] * 2
    out = pl.pallas_call(
        body, name=name,
        out_shape=sem_shapes + _hbm_like(arrays) + [TOKEN],
        in_specs=[ANY] * n_deps + [HBM] * len(arrays),
        out_specs=[SEM] * (2 * n_sets) + [HBM] * len(arrays) + [pl.BlockSpec(memory_space=pltpu.VMEM)],
        input_output_aliases={n_deps + i: 2 * n_sets + i for i in range(len(arrays))},
        compiler_params=pltpu.CompilerParams(has_side_effects=DATAFLOW),
    )(*deps, *[_in_hbm(a) for a in arrays])
    sems = [(out[2 * s], out[2 * s + 1]) for s in range(n_sets)]
    thru = list(out[2 * n_sets:2 * n_sets + len(arrays)])
    return sems, (thru[:first_land] if sources is not None else None), thru[first_land:], out[-1]


def _ici_copies_wait(sems, members, sources, landings, exchange, after, name):
    n = len(landings)
    arrays = (list(sources) if sources is not None else []) + list(landings)
    first_land = len(arrays) - n

    def body(*refs):
        src, land = refs[:n], refs[first_land:first_land + n]
        send_sems, recv_sems = refs[len(arrays)], refs[len(arrays) + 1]
        x, y, c = _position()
        for slot, rows in enumerate(members):
            for j, peer in enumerate(exchange.peers(x, y, c)):
                at = exchange.n_peers * slot + j
                cp = pltpu.make_async_remote_copy(
                    src_ref=_shard_rows(src[slot], exchange.sent(x, y, c, peer), rows),
                    dst_ref=_shard_rows(land[slot], exchange.arrives_at(peer), rows),
                    send_sem=send_sems.at[at], recv_sem=recv_sems.at[at], device_id=peer, device_id_type=MESH_ID)
                cp.wait_send()
                cp.wait_recv()

    out = pl.pallas_call(
        body, name=name, out_shape=_hbm_like(arrays),
        in_specs=[HBM] * len(arrays) + [SEM, SEM] + [ANY] * len(after), out_specs=[HBM] * len(arrays),
        input_output_aliases={i: i for i in range(len(arrays))},
        compiler_params=pltpu.CompilerParams(has_side_effects=DATAFLOW),
    )(*arrays, sems[0], sems[1], *after)
    return list(out[first_land:])


def _swap_copies(land, rows, which, send_sems, recv_sems):
    x, y, c = _position()
    chips = [([(x, y)] + _other_chips(x, y))[j] for j in which]
    sends, recvs = [], []
    for k in range(len(land)):
        for j, chip in enumerate(chips):
            for copies, core in ((sends, c), (recvs, 1 - c)):
                block = _shard_rows(land[k], _device_index(chip, core), rows[k])
                copies.append(pltpu.make_async_remote_copy(
                    src_ref=block, dst_ref=block, send_sem=send_sems.at[k, j], recv_sem=recv_sems.at[k, j],
                    device_id=(x, y, 1 - c), device_id_type=MESH_ID))
    return sends, recvs


def _d2d_gather(buffers, rows, name, which=(0, 1, 2, 3), deps=()):
    n = len(buffers)
    n_deps = len(deps)

    def body(*refs):
        sends, recvs = _swap_copies(refs[n_deps + n:n_deps + 2 * n], rows, which, *refs[n_deps + 2 * n:])
        for cp in sends:
            cp.start()
        for cp in recvs:
            cp.wait_recv()
        for cp in sends:
            cp.wait_send()

    return pl.pallas_call(
        body, name=name, out_shape=[jax.ShapeDtypeStruct(a.shape, a.dtype) for a in buffers],
        in_specs=[ANY] * (n_deps + n), out_specs=[ANY] * n, input_output_aliases={n_deps + i: i for i in range(n)},
        scratch_shapes=[pltpu.SemaphoreType.DMA((n, len(which))), pltpu.SemaphoreType.DMA((n, len(which)))],
    )(*deps, *buffers)


def _call_with_swap(body, args, swap, prefetch=(), **kw):
    buffers, rows, *chips = swap
    which = chips[0] if chips else (0, 1, 2, 3)
    n, n_pre, n_in, n_out = len(buffers), len(prefetch), len(args), len(kw["out_shape"])
    n_scratch = len(kw["scratch_shapes"])
    grid = kw["grid"]

    def at_step(last):
        hit = [pl.program_id(a) == (extent - 1 if last else 0) for a, extent in enumerate(grid)]
        return functools.reduce(jnp.logical_and, hit)

    def hosted(*refs):
        pre, ins, refs = refs[:n_pre], refs[n_pre:n_pre + n_in], refs[n_pre + n_in + n:]
        outs, land, scratch = refs[:n_out], refs[n_out:n_out + n], refs[n_out + n:n_out + n + n_scratch]
        sends, recvs = _swap_copies(land, rows, which, *refs[n_out + n + n_scratch:])

        @pl.when(at_step(False))
        def _():
            for cp in sends:
                cp.start()

        body(*pre, *ins, *outs, *scratch)

        @pl.when(at_step(True))
        def _():
            for cp in recvs:
                cp.wait_recv()
            for cp in sends:
                cp.wait_send()

    sem_shape = pltpu.SemaphoreType.DMA((n, len(which)))
    aliases = {**kw.get("input_output_aliases", {}), **{n_in + i: n_out + i for i in range(n)}}
    out = pl.pallas_call(
        hosted,
        grid_spec=pltpu.PrefetchScalarGridSpec(
            num_scalar_prefetch=n_pre, grid=grid, in_specs=kw["in_specs"] + [ANY] * n, out_specs=kw["out_specs"] + [ANY] * n,
            scratch_shapes=kw["scratch_shapes"] + [sem_shape, sem_shape]),
        out_shape=kw["out_shape"] + [jax.ShapeDtypeStruct(a.shape, a.dtype) for a in buffers],
        input_output_aliases={n_pre + i: o for i, o in aliases.items()},
        compiler_params=kw["compiler_params"], name=kw["name"])(*prefetch, *args, *buffers)
    return out[:n_out], out[n_out:]


def _pair_exchange_add(grad, place, name):
    r = grad.shape[0] // N_DEV
    n_chips = N_DEV // 2

    def body(place_ref, g_hbm, kept_ref, part_ref, land_ref, inbox, send_sems, recv_sems):
        q = pl.program_id(0)
        x, y, c = _position()
        copies = [pltpu.make_async_remote_copy(
            src_ref=_shard_rows(g_hbm, 2 * i + 1 - c, r), dst_ref=inbox.at[i], send_sem=send_sems.at[i],
            recv_sem=recv_sems.at[i], device_id=(x, y, 1 - c), device_id_type=MESH_ID) for i in range(n_chips)]

        @pl.when(q == 0)
        def _():
            for cp in copies:
                cp.start()

        for i, cp in enumerate(copies):
            @pl.when(q == i)
            def _(cp=cp):
                cp.wait_recv()

        total = (kept_ref[...].astype(F32) + inbox[q].astype(F32)).astype(BF)
        part_ref[...] = total

        @pl.when(q == place_ref[1])
        def _():
            land_ref[...] = total

        @pl.when(q == n_chips - 1)
        def _():
            for cp in copies:
                cp.wait_send()

    return pl.pallas_call(
        body,
        grid_spec=pltpu.PrefetchScalarGridSpec(
            num_scalar_prefetch=1, grid=(n_chips,),
            in_specs=[ANY, pl.BlockSpec((r, D), lambda q, p: (2 * q + p[0], 0))],
            out_specs=[pl.BlockSpec((r, D), lambda q, p: (q, 0)), pl.BlockSpec((r, D), lambda q, p: (p[1], 0))],
            scratch_shapes=[pltpu.VMEM((n_chips, r, D), BF), pltpu.SemaphoreType.DMA((n_chips,)),
                            pltpu.SemaphoreType.DMA((n_chips,))]),
        out_shape=[jax.ShapeDtypeStruct((n_chips * r, D), BF)] * 2,
        compiler_params=_params(1), name=name)(place, grad, grad)


def _wgrad_pair_sum(lhs, rhs, place, name, *, lhs_is_transposed, deps=()):
    t = rhs.shape[0]
    n = lhs.shape[0] if lhs_is_transposed else lhs.shape[1]
    r = n // N_DEV
    n_chips = N_DEV // 2
    per = 1 if (2 * r) % BLK == 0 else 2
    steps = n_chips // per
    n_deps = len(deps)
    in_vmem = r <= WGRAD_SUM_MAX_ROWS

    def body(place_ref, *refs):
        if in_vmem:
            l_ref, r_ref, part_ref, land_ref, res, inbox, send_sems, recv_sems = refs[n_deps:]
        else:
            l_ref, r_ref, part_ref, land_ref, inbox, res, staged, send_sems, recv_sems, stage_sem = refs[n_deps:]
        q = pl.program_id(0)
        slot = q % 2
        x, y, c = _position()

        def send(step, buf, i):
            return pltpu.make_async_remote_copy(
                src_ref=res.at[buf, pl.ds(pl.multiple_of((2 * i + 1 - c) * r, 16), r), :], dst_ref=inbox.at[step * per + i],
                send_sem=send_sems.at[buf, i], recv_sem=recv_sems.at[step * per + i],
                device_id=(x, y, 1 - c), device_id_type=MESH_ID)

        @pl.when(q < steps)
        def _():
            @pl.when(q >= 2)
            def _():
                for i in range(per):
                    send(q - 2, slot, i).wait_send()

            if lhs_is_transposed:
                res[slot] = _dot(l_ref[...], r_ref[...]).astype(BF)
            else:
                res[slot] = _dot_tn(l_ref[...], r_ref[...]).astype(BF)
            for i in range(per):
                send(q, slot, i).start()

        @pl.when(q >= 1)
        def _():
            for i in range(per):
                chip = (q - 1) * per + i
                send(q - 1, 1 - slot, i).wait_recv()
                kept = res[1 - slot, pl.ds(pl.multiple_of((2 * i + c) * r, 16), r), :]
                if in_vmem:
                    theirs = inbox[chip]
                else:
                    stage = pltpu.make_async_copy(inbox.at[chip], staged, stage_sem)
                    stage.start()
                    stage.wait()
                    theirs = staged[...]
                total = (kept.astype(F32) + theirs.astype(F32)).astype(BF)
                part_ref[i * r:(i + 1) * r, :] = total

                @pl.when(chip == place_ref[1])
                def _():
                    land_ref[...] = total

        @pl.when(q == steps)
        def _():
            for i in range(per):
                if steps > 1:
                    send(q - 2, slot, i).wait_send()
                send(q - 1, 1 - slot, i).wait_send()

    width = 2 * r * per
    last = steps - 1
    if lhs_is_transposed:
        lhs_spec = pl.BlockSpec((width, t), lambda q, p: (jnp.minimum(q, last), 0))
    else:
        lhs_spec = pl.BlockSpec((t, width), lambda q, p: (0, jnp.minimum(q, last)))
    sems = [pltpu.SemaphoreType.DMA((2, per)), pltpu.SemaphoreType.DMA((n_chips,))]
    inbox_shape = (n_chips, r, D)
    if in_vmem:
        extra_specs, extra_shapes = [], []
        scratch = [pltpu.VMEM((2, width, D), BF), pltpu.VMEM(inbox_shape, BF)] + sems
    else:
        extra_specs, extra_shapes = [ANY], [jax.ShapeDtypeStruct(inbox_shape, BF)]
        scratch = [pltpu.VMEM((2, width, D), BF), pltpu.VMEM((r, D), BF)] + sems + [pltpu.SemaphoreType.DMA(())]
    out = pl.pallas_call(
        body,
        grid_spec=pltpu.PrefetchScalarGridSpec(
            num_scalar_prefetch=1, grid=(steps + 1,),
            in_specs=[ANY] * n_deps + [lhs_spec, pl.BlockSpec((t, D), lambda q, p: (0, 0), pipeline_mode=pl.Buffered(1))],
            out_specs=[pl.BlockSpec((per * r, D), lambda q, p: (jnp.maximum(q - 1, 0), 0)),
                       pl.BlockSpec((r, D), lambda q, p: (p[1], 0))] + extra_specs,
            scratch_shapes=scratch),
        out_shape=[jax.ShapeDtypeStruct((n // 2, D), BF)] * 2 + extra_shapes,
        compiler_params=_params(1), name=name)(place, *deps, lhs, rhs)
    return out[:2]


def _wgrad_pair_sum_many(items, place, name, deps=()):
    m = len(items)
    t = items[0][1].shape[0]
    n = items[0][0].shape[0] if items[0][2] else items[0][0].shape[1]
    r = n // N_DEV
    assert (2 * r) % BLK == 0 and r <= WGRAD_SUM_MAX_ROWS
    steps = N_DEV // 2
    chunks = m * steps
    n_deps = len(deps)

    def body(place_ref, *refs):
        refs = refs[n_deps:]
        l_refs, r_first, r_later = refs[:m], refs[m], refs[m + 1:2 * m]
        parts, lands = refs[2 * m:3 * m], refs[3 * m:4 * m]
        res, inbox, r_scr, send_sems, recv_sems, fetch_sems = refs[4 * m:]
        g = pl.program_id(0)
        slot = g % 2
        x, y, c = _position()

        def send(chunk, buf):
            return pltpu.make_async_remote_copy(
                src_ref=res.at[buf, pl.ds(pl.multiple_of((1 - c) * r, 16), r), :], dst_ref=inbox.at[chunk],
                send_sem=send_sems.at[buf], recv_sem=recv_sems.at[chunk], device_id=(x, y, 1 - c), device_id_type=MESH_ID)

        def fetch(k):
            return pltpu.make_async_copy(r_later[k - 1], r_scr.at[k - 1], fetch_sems.at[k - 1])

        @pl.when(g == 0)
        def _():
            for k in range(1, m):
                fetch(k).start()

        @pl.when(g < chunks)
        def _():
            @pl.when(g >= 2)
            def _():
                send(g - 2, slot).wait_send()

            for k, (_, _, transposed) in enumerate(items):
                @pl.when(g // steps == k)
                def _(k=k, transposed=transposed):
                    if k > 0:
                        @pl.when(g == k * steps)
                        def _():
                            fetch(k).wait()
                    rhs = r_first[...] if k == 0 else r_scr[k - 1]
                    res[slot] = (_dot(l_refs[k][...], rhs) if transposed else _dot_tn(l_refs[k][...], rhs)).astype(BF)

            send(g, slot).start()

        @pl.when(g >= 1)
        def _():
            chunk = g - 1
            send(chunk, 1 - slot).wait_recv()
            kept = res[1 - slot, pl.ds(pl.multiple_of(c * r, 16), r), :]
            total = (kept.astype(F32) + inbox[chunk].astype(F32)).astype(BF)
            for k in range(m):
                @pl.when(chunk // steps == k)
                def _(k=k):
                    parts[k][...] = total

                    @pl.when(chunk % steps == place_ref[1])
                    def _():
                        lands[k][...] = total

        @pl.when(g == chunks)
        def _():
            send(g - 2, slot).wait_send()
            send(g - 1, 1 - slot).wait_send()

    def own_steps(k):
        return lambda g: jnp.clip(g - k * steps, 0, steps - 1)

    lhs_specs = []
    for k, (lhs, _, transposed) in enumerate(items):
        at = own_steps(k)
        lhs_specs.append(pl.BlockSpec((2 * r, t), lambda g, p, at=at: (at(g), 0)) if transposed
                         else pl.BlockSpec((t, 2 * r), lambda g, p, at=at: (0, at(g))))
    out = pl.pallas_call(
        body,
        grid_spec=pltpu.PrefetchScalarGridSpec(
            num_scalar_prefetch=1, grid=(chunks + 1,),
            in_specs=[ANY] * n_deps + lhs_specs
            + [pl.BlockSpec((t, D), lambda g, p: (0, 0), pipeline_mode=pl.Buffered(1))] + [ANY] * (m - 1),
            out_specs=[pl.BlockSpec((r, D), lambda g, p, at=own_steps(k): (at(g - 1), 0)) for k in range(m)]
            + [pl.BlockSpec((r, D), lambda g, p: (p[1], 0))] * m,
            scratch_shapes=[pltpu.VMEM((2, 2 * r, D), BF), pltpu.VMEM((chunks, r, D), BF), pltpu.VMEM((m - 1, t, D), BF),
                            pltpu.SemaphoreType.DMA((2,)), pltpu.SemaphoreType.DMA((chunks,)),
                            pltpu.SemaphoreType.DMA((m - 1,))]),
        out_shape=[jax.ShapeDtypeStruct((n // 2, D), BF)] * (2 * m),
        compiler_params=_params(1), name=name)(place, *deps, *[i[0] for i in items], *[i[1] for i in items])
    return [(out[k], out[m + k]) for k in range(m)]


def _sum_blocks(gathered, rows):
    def body(b_ref, o_ref):
        acc = b_ref[0:rows, :]
        for d in range(1, N_DEV):
            acc = acc + b_ref[d * rows:(d + 1) * rows, :]
        o_ref[...] = acc

    return pl.pallas_call(body, out_shape=jax.ShapeDtypeStruct((rows, D), F32), name="small_sum")(gathered)


def _adamw_math(w, g, m, v):
    m = ADAM_B1 * m + (1.0 - ADAM_B1) * g
    v = ADAM_B2 * v + (1.0 - ADAM_B2) * (g * g)
    m_hat = m / (1.0 - ADAM_B1 ** ADAM_STEP)
    v_hat = v / (1.0 - ADAM_B2 ** ADAM_STEP)
    delta = -ADAM_LR * (m_hat / (jnp.sqrt(v_hat) + ADAM_EPS) + ADAM_WD * w)
    return delta, m, v


def _sum_partials(blocks):
    g = blocks[0].astype(F32)
    for blk in blocks[1:]:
        g = g + blk.astype(F32)
    return g


ADAMW_MAX_ROWS = 352


def _reduce_adamw(items, name):
    n = len(items)
    per = -(-max(w.shape[0] for _, w, _, _ in items) // ADAMW_MAX_ROWS)

    def body(*refs):
        for k in range(n):
            r0, r1, r2, r3, w_ref, m_ref, v_ref = refs[7 * k:7 * k + 7]
            g_ref, d_ref, nm_ref, nv_ref = refs[7 * n + 4 * k:7 * n + 4 * k + 4]
            g = _sum_partials([r0[...], r1[...], r2[...], r3[...]])
            g_ref[...] = g
            d_ref[...], nm_ref[...], nv_ref[...] = _adamw_math(w_ref[...], g, m_ref[...], v_ref[...])

    in_specs, out_specs, out_shape, args = [], [], [], []
    for landed, w, m, v in items:
        tr = w.shape[0] // per
        assert tr * per == w.shape[0] and tr % 16 == 0
        tile = _row_tile(tr, D)
        in_specs += [pl.BlockSpec((tr, D), lambda i, q=q: (q * per + i, 0)) for q in range(4)] + [tile] * 3
        out_specs += [tile] * 4
        out_shape += [jax.ShapeDtypeStruct(w.shape, F32)] * 4
        args += [landed] * 4 + [w, m, v]
    out = pl.pallas_call(body, grid=(per,), in_specs=in_specs, out_specs=out_specs, out_shape=out_shape,
                         compiler_params=_params(1), name=name)(*args)
    return [out[4 * k:4 * k + 4] for k in range(n)]


def _adamw_small(w, g, m, v, name):
    def body(w_ref, g_ref, m_ref, v_ref, d_ref, nm_ref, nv_ref):
        d_ref[...], nm_ref[...], nv_ref[...] = _adamw_math(w_ref[...], g_ref[...], m_ref[...], v_ref[...])

    return pl.pallas_call(body, out_shape=[jax.ShapeDtypeStruct(w.shape, F32)] * 3, name=name)(w, g, m, v)


WEIGHTS = ("ffn1_norm", "ffn1_w_in", "ffn1_w_out", "mix_norm", "w_in", "conv_dw_kernel", "conv_dw_bias", "conv_ln_g",
           "conv_ln_b", "conv_w_proj", "q_norm", "k_norm", "attn_sinks", "rel_bias", "attn_w_o", "w_out", "ffn2_norm",
           "ffn2_w_in", "ffn2_w_out")
MATRICES = ("ffn1_w_in", "ffn1_w_out", "w_in", "conv_w_proj", "attn_w_o", "w_out", "ffn2_w_in", "ffn2_w_out")
COLUMN_SHARDED = ("ffn1_w_in", "w_in", "ffn2_w_in")
ROW_VECTORS = ("ffn1_norm", "mix_norm", "conv_dw_bias", "conv_ln_g", "conv_ln_b", "ffn2_norm")
PACKED = (("q_norm", HD), ("k_norm", HD), ("attn_sinks", NQ), ("rel_bias", NBUCKET * NQ))
GATHER = _Exchange(gather=True)
GATHER_ALL = _Exchange(gather=True, all_cores=True)
SCATTER = _Exchange(gather=False)
FIRST = "ffn1_w_in"
GATHER_STAGES = ("ffn1_out", "mix_proj", "mix_merge", "ffn2")
STAGE_GATHER = {"ffn1_out": GATHER, "mix_proj": GATHER, "mix_merge": GATHER, "ffn2": GATHER_ALL}
STAGE_MEMBERS = {"ffn1_out": ("ffn1_w_out",),
                 "mix_proj": ("w_in", "taps"), "mix_merge": ("conv_w_proj", "attn_w_o", "w_out"),
                 "ffn2": ("ffn2_w_in", "ffn2_w_out")}
ROW_PACKED = len(ROW_VECTORS)
ROW_LOSS = ROW_PACKED + 1
ROW_TAPS = 8
PAYLOAD_ROWS = 48


def _pack_small(values, last_row):
    packed = jnp.concatenate([values[k].reshape(-1) for k, _ in PACKED])
    packed = jnp.pad(packed, (0, D - packed.shape[0])).reshape(1, D)
    return jnp.concatenate([values[k].reshape(1, D) for k in ROW_VECTORS] + [packed, last_row], axis=0)


def _unpack_small(rows):
    out = {k: rows[i] for i, k in enumerate(ROW_VECTORS)}
    at = 0
    for k, size in PACKED:
        out[k] = rows[ROW_PACKED, at:at + size]
        at += size
    out["rel_bias"] = out["rel_bias"].reshape(NBUCKET, NQ)
    return out


def kernel(x, ffn1_norm, ffn1_w_in, ffn1_w_out, mix_norm, w_in, conv_dw_kernel, conv_dw_bias, conv_ln_g, conv_ln_b, conv_w_proj, q_norm, k_norm, attn_sinks, rel_bias, attn_w_o, w_out, ffn2_norm, ffn2_w_in, ffn2_w_out, loss_target, m_ffn1_norm, m_ffn1_w_in, m_ffn1_w_out, m_mix_norm, m_w_in, m_conv_dw_kernel, m_conv_dw_bias, m_conv_ln_g, m_conv_ln_b, m_conv_w_proj, m_q_norm, m_k_norm, m_attn_sinks, m_rel_bias, m_attn_w_o, m_w_out, m_ffn2_norm, m_ffn2_w_in, m_ffn2_w_out, v_ffn1_norm, v_ffn1_w_in, v_ffn1_w_out, v_mix_norm, v_w_in, v_conv_dw_kernel, v_conv_dw_bias, v_conv_ln_g, v_conv_ln_b, v_conv_w_proj, v_q_norm, v_k_norm, v_attn_sinks, v_rel_bias, v_attn_w_o, v_w_out, v_ffn2_norm, v_ffn2_w_in, v_ffn2_w_out):
    w = dict(ffn1_norm=ffn1_norm, ffn1_w_in=ffn1_w_in, ffn1_w_out=ffn1_w_out, mix_norm=mix_norm, w_in=w_in,
             conv_dw_kernel=conv_dw_kernel, conv_dw_bias=conv_dw_bias, conv_ln_g=conv_ln_g, conv_ln_b=conv_ln_b,
             conv_w_proj=conv_w_proj, q_norm=q_norm, k_norm=k_norm, attn_sinks=attn_sinks, rel_bias=rel_bias,
             attn_w_o=attn_w_o, w_out=w_out, ffn2_norm=ffn2_norm, ffn2_w_in=ffn2_w_in, ffn2_w_out=ffn2_w_out)
    m = dict(ffn1_norm=m_ffn1_norm, ffn1_w_in=m_ffn1_w_in, ffn1_w_out=m_ffn1_w_out, mix_norm=m_mix_norm, w_in=m_w_in,
             conv_dw_kernel=m_conv_dw_kernel, conv_dw_bias=m_conv_dw_bias, conv_ln_g=m_conv_ln_g, conv_ln_b=m_conv_ln_b,
             conv_w_proj=m_conv_w_proj, q_norm=m_q_norm, k_norm=m_k_norm, attn_sinks=m_attn_sinks, rel_bias=m_rel_bias,
             attn_w_o=m_attn_w_o, w_out=m_w_out, ffn2_norm=m_ffn2_norm, ffn2_w_in=m_ffn2_w_in, ffn2_w_out=m_ffn2_w_out)
    v = dict(ffn1_norm=v_ffn1_norm, ffn1_w_in=v_ffn1_w_in, ffn1_w_out=v_ffn1_w_out, mix_norm=v_mix_norm, w_in=v_w_in,
             conv_dw_kernel=v_conv_dw_kernel, conv_dw_bias=v_conv_dw_bias, conv_ln_g=v_conv_ln_g, conv_ln_b=v_conv_ln_b,
             conv_w_proj=v_conv_w_proj, q_norm=v_q_norm, k_norm=v_k_norm, attn_sinks=v_attn_sinks, rel_bias=v_rel_bias,
             attn_w_o=v_attn_w_o, w_out=v_w_out, ffn2_norm=v_ffn2_norm, ffn2_w_in=v_ffn2_w_in, ffn2_w_out=v_ffn2_w_out)
    px, py, pc = _position()
    me = 4 * px + 2 * py + pc
    place = jnp.stack([pc, 2 * px + py]).astype(jnp.int32)

    rows_of = lambda k, a: a.T if k in COLUMN_SHARDED else a
    me1 = me.astype(jnp.int32).reshape(1)
    rest = tuple(k for k in MATRICES if k != FIRST)
    shard_rows = dict({k: rows_of(k, w[k]).shape[0] for k in MATRICES}, taps=CWP)
    sems_first, _, thru_first, token = _ici_copies_start(
        [[(0, shard_rows[FIRST])]], None, _prep([rows_of(FIRST, w[FIRST])], None, me1, "prep_first"), [GATHER],
        "gather_start_first")
    prepped, mine = _prep([rows_of(k, w[k]) for k in rest], conv_dw_kernel, me1, "prep", deps=[token],
                          swap=(thru_first, [shard_rows[FIRST]], (0,)))
    buffers = dict(zip(rest + ("taps",), prepped))
    landings, sets = [], []
    for stage in GATHER_STAGES:
        sets.append([(len(landings) + i, shard_rows[k]) for i, k in enumerate(STAGE_MEMBERS[stage])])
        landings += list(STAGE_MEMBERS[stage])
    sems, _, land_thru, started = _ici_copies_start(sets, None, [buffers[k] for k in landings],
                                                    [STAGE_GATHER[s] for s in GATHER_STAGES], "gather_start")

    packed = [_pack_small(a, jnp.zeros((1, D), F32)) for a in (w, m, v)]

    def ffn1_up(x, g, after):
        chips = jnp.stack([_chip_index(chip) for chip in [(px, py)] + _other_chips(px, py)]).astype(jnp.int32)
        rows = [shard_rows[FIRST]]
        n, u = _ffn_up_blocks(x, g, None, mine[0], chips[:1], None, "ffn1_up_mine", deps=[started])
        landed = _ici_copies_wait(sems_first[0], rows, None, mine, GATHER, [u, *after, *packed], "gather_wait_first")
        w_in_t, = _d2d_gather(landed, rows, "gather_d2d_first", which=(1, 2, 3))
        n, u = _ffn_up_blocks(None, None, n, w_in_t, chips[1:3], u, "ffn1_up_next")
        (n, u), w1 = weights_of("ffn1_out", (u,), during=functools.partial(
            _ffn_up_blocks, None, None, n, w_in_t, chips[3:], u, "ffn1_up"))
        return n, u, dict(w1, ffn1_w_in=w_in_t)

    def weights_of(stage, after, during=None):
        s = GATHER_STAGES.index(stage)
        rows = [r for _, r in sets[s]]
        landed = _ici_copies_wait(sems[s], rows, None, [land_thru[k] for k, _ in sets[s]], STAGE_GATHER[stage],
                                  list(after), "gather_wait_" + stage)
        if during is not None:
            results, landed = during(swap=(landed, rows))
        elif not STAGE_GATHER[stage].all_cores:
            landed = _d2d_gather(landed, rows, "gather_d2d_" + stage)
        out = dict(zip(STAGE_MEMBERS[stage], landed))
        if "taps" in out:
            taps = out.pop("taps")
            out["conv_dw_kernel"] = jnp.transpose(taps.reshape(N_DEV, CWP, BLK), (1, 0, 2)).reshape(CWP, D)[:CW]
        return out if during is None else (results, out)

    in_flight = []

    def wgrad(lhs, rhs, name, lhs_is_transposed, deps=()):
        return ("summed",) + tuple(_wgrad_pair_sum(lhs, rhs, place, name, lhs_is_transposed=lhs_is_transposed, deps=deps))

    def wgrads(items, name):
        return [("summed",) + pair for pair in _wgrad_pair_sum_many(items, place, name)]

    def grads_done(stage, grads):
        names = list(grads)
        added = []
        for k in names:
            if not isinstance(grads[k], tuple):
                added.append(_pair_exchange_add(grads[k], place, "pair_add_" + k))
            else:
                added.append(grads[k][1:])
        partials = [p for p, _ in added]
        members = [(i, p.shape[0] // 4) for i, p in enumerate(partials)]
        sem, p_thru, l_thru, token = _ici_copies_start([members], partials, [l for _, l in added], [SCATTER],
                                                       "scatter_start_" + stage)
        in_flight.append((stage, names, sem[0], p_thru, l_thru, token))
        return [token]

    small = []

    def small_done(gv, sq):
        payload = jnp.concatenate([_pack_small(gv, sq), jnp.pad(gv["conv_dw_kernel"], ((0, PAYLOAD_ROWS - ROW_TAPS - CW), (0, 0)))],
                                  axis=0)
        mine = lax.dynamic_update_slice_in_dim(lax.empty((N_DEV * PAYLOAD_ROWS, D), F32), payload, me * PAYLOAD_ROWS, axis=0)
        sems, _, thru, token = _ici_copies_start([[(0, PAYLOAD_ROWS)]], None, [mine], [GATHER_ALL], "small_start")
        small.append((sems[0], thru))
        return [token]

    vec = {k: w[k] for k in WEIGHTS if k not in MATRICES and k != "conv_dw_kernel"}
    dx0 = _local_step(x[0], loss_target[0], vec, ffn1_up, weights_of, wgrad, wgrads, grads_done, small_done)
    gathered, = _ici_copies_wait(small[0][0], [PAYLOAD_ROWS], None, small[0][1], GATHER_ALL, [in_flight[-1][-1]], "small_wait")
    total = _sum_blocks(gathered, PAYLOAD_ROWS)
    loss = (0.5 / D) * jnp.sum(total[ROW_LOSS])

    grads, delta, new_m, new_v = {}, {}, {}, {}
    after, pending = [total], []
    for stage, names, sem, p_thru, l_thru, _ in in_flight:
        landed = _ici_copies_wait(sem, [p.shape[0] // 4 for p in p_thru], p_thru, l_thru, SCATTER, after,
                                  "scatter_wait_" + stage)
        pending += zip(names, landed)
        after = list(landed)
        if stage == in_flight[-2][0]:
            continue
        outs = _reduce_adamw([(buf, rows_of(k, w[k]), rows_of(k, m[k]), rows_of(k, v[k])) for k, buf in pending],
                             "adamw_" + stage)
        for (k, _), out in zip(pending, outs):
            grads[k], delta[k], new_m[k], new_v[k] = [rows_of(k, a) for a in out]
        after, pending = [out[1] for out in outs], []
    d8, m8, v8 = _adamw_small(packed[0], total[:ROW_TAPS], packed[1], packed[2], "adamw_small")
    grads.update(_unpack_small(total[:ROW_TAPS]))
    delta.update(_unpack_small(d8))
    new_m.update(_unpack_small(m8))
    new_v.update(_unpack_small(v8))
    k = "conv_dw_kernel"
    grads[k] = lax.dynamic_slice_in_dim(total[ROW_TAPS:ROW_TAPS + CW], me * BLK, BLK, axis=1)
    delta[k], new_m[k], new_v[k] = _adamw_small(w[k], grads[k], m[k], v[k], "adamw_taps")

    return (loss, dx0[None], *[grads[k] for k in WEIGHTS], *[delta[k] for k in WEIGHTS],
            *[new_m[k] for k in WEIGHTS], *[new_v[k] for k in WEIGHTS])
```

```python
import functools
import math

import numpy as np
import jax
import jax.numpy as jnp
from jax import lax
from jax.experimental import pallas as pl
from jax.experimental.pallas import tpu as pltpu

F32 = jnp.float32
BF = jnp.bfloat16

D = 1024
F = 2816
INW = 5632
CW = 31
CWP = 32
HD = 64
NQ = 16
NKV = 4
GRP = NQ // NKV
BLK = 128
NBUCKET = 32
EPS = 1e-6
NEG = float(jnp.finfo(jnp.float32).min)
QK_SCALE = 1.0 / math.sqrt(HD)
R_CONV = (0, 2048)
R_QKV = (2048, 3584)
R_Q = (2048, 3072)
R_KV = (3072, 3584)
R_GATE = (3584, 5632)

N_DEV = 8
VMEM_LIMIT_V7X = 56 * 1024 * 1024
ROW_TILE = 256
ROW_TILE_WIDE = 512
ROW_TILE_BLOCK = 1024
WGRAD_SUM_MAX_ROWS = 352

ADAM_LR = 0.001
ADAM_B1 = 0.9
ADAM_B2 = 0.999
ADAM_EPS = 1e-08
ADAM_WD = 0.01
ADAM_STEP = 10

NT_DIMS = (((1,), (1,)), ((), ()))
TN_DIMS = (((0,), (0,)), ((), ()))


def _dot(a, b):
    return jnp.dot(a, b, preferred_element_type=F32)


def _dot_nt(a, b):
    return lax.dot_general(a, b, NT_DIMS, preferred_element_type=F32)


def _dot_tn(a, b):
    return lax.dot_general(a, b, TN_DIMS, preferred_element_type=F32)


def _sig(x):
    return 0.5 * jnp.tanh(0.5 * x) + 0.5


ANY = pl.BlockSpec(memory_space=pl.ANY)


def _call(body, deps, args, **kw):
    n = len(deps)
    if n:
        kw["in_specs"] = [ANY] * n + list(kw["in_specs"])
        return pl.pallas_call(lambda *refs: body(*refs[n:]), **kw)(*deps, *args)
    return pl.pallas_call(body, **kw)(*args)


def _params(n_axes):
    return pltpu.CompilerParams(dimension_semantics=("arbitrary",) * n_axes, vmem_limit_bytes=VMEM_LIMIT_V7X)


def _resident(shape):
    zeros = (0,) * len(shape)
    return pl.BlockSpec(shape, lambda *_: zeros, pipeline_mode=pl.Buffered(1))


def _row_tile(rows, cols):
    return pl.BlockSpec((rows, cols), lambda i: (i, 0))


def _rms_stats(x):
    r = lax.rsqrt(jnp.mean(x * x, axis=-1, keepdims=True) + EPS)
    return r, x * r


def _rms_bwd(dn, x, g):
    r, xh = _rms_stats(x)
    dxh = dn * g
    dx = r * (dxh - xh * jnp.mean(dxh * xh, axis=-1, keepdims=True))
    return dx, jnp.sum(dn * xh, axis=0, keepdims=True)


def _ffn_last(x, target, g, w_in_t, w_out, name):
    t = x.shape[0]
    tm = min(ROW_TILE, t)

    def body(x_ref, t_ref, g_ref, w_ref, wo_hbm, n_ref, du_ref, h_ref, dy_ref, dx_ref, sq_ref, dg_ref, wo_ref, wo_sem):
        fetch = pltpu.make_async_copy(wo_hbm, wo_ref, wo_sem)

        @pl.when(pl.program_id(0) == 0)
        def _():
            fetch.start()
            sq_ref[...] = jnp.zeros_like(sq_ref)
            dg_ref[...] = jnp.zeros_like(dg_ref)

        x = x_ref[...]
        g = g_ref[...]
        r, xh = _rms_stats(x)
        n = (xh * g).astype(BF)
        n_ref[...] = n
        u = _dot_nt(n, w_ref[...])
        a = u[:, :F]
        b = u[:, F:]
        s = _sig(a)
        sa = a * s
        h = (sa * b).astype(BF)
        h_ref[...] = h

        @pl.when(pl.program_id(0) == 0)
        def _():
            fetch.wait()

        err = x + 0.5 * _dot(h, wo_ref[...]) - t_ref[...]
        sq_ref[...] += jnp.sum(err * err, axis=0, keepdims=True)
        dxo = err * (1.0 / D)
        dy = (0.5 * dxo).astype(BF)
        dy_ref[...] = dy
        dh = _dot_nt(dy, wo_ref[...])
        du_ref[:, :F] = (dh * b * (s * (1.0 + a * (1.0 - s)))).astype(BF)
        du_ref[:, F:] = (dh * sa).astype(BF)
        dn = _dot(du_ref[...], w_ref[...])
        dxh = dn * g
        dx_ref[...] = dxo + r * (dxh - xh * jnp.mean(dxh * xh, axis=-1, keepdims=True))
        dg_ref[...] += jnp.sum(dn * xh, axis=0, keepdims=True)

    vec = pl.BlockSpec((1, D), lambda i: (0, 0))
    return pl.pallas_call(
        body, grid=(t // tm,),
        in_specs=[_row_tile(tm, D), _row_tile(tm, D), _resident((1, D)), _resident((INW, D)), ANY],
        out_specs=[_row_tile(tm, D), _row_tile(tm, INW), _row_tile(tm, F), _row_tile(tm, D), _row_tile(tm, D), vec, vec],
        out_shape=[jax.ShapeDtypeStruct((t, D), BF), jax.ShapeDtypeStruct((t, INW), BF), jax.ShapeDtypeStruct((t, F), BF),
                   jax.ShapeDtypeStruct((t, D), BF), jax.ShapeDtypeStruct((t, D), F32), jax.ShapeDtypeStruct((1, D), F32),
                   jax.ShapeDtypeStruct((1, D), F32)],
        scratch_shapes=[pltpu.VMEM((F, D), BF), pltpu.SemaphoreType.DMA(())],
        compiler_params=_params(1), name=name)(x, target, g, w_in_t, w_out)


def _ffn_up_blocks(x, g, n, w_in_t, order, u, name, deps=(), swap=None):
    t = (x if n is None else n).shape[0]
    tm = min(ROW_TILE_BLOCK, t)
    c = INW * 2 // N_DEV
    n_deps = len(deps)
    first = n is None
    assert not first or order.shape == (1,)

    def body(order_ref, *refs):
        refs = refs[n_deps:]
        if first:
            x_ref, g_ref, w_ref, n_ref, u_ref = refs
            nt = (_rms_stats(x_ref[...])[1] * g_ref[...]).astype(BF)
            n_ref[...] = nt
        else:
            n_ref, w_ref, _, u_ref = refs
            nt = n_ref[...]
        u_ref[...] = _dot_nt(nt, w_ref[...]).astype(BF)

    rows = pl.BlockSpec((tm, D), lambda k, i, o: (i, 0))
    block = pl.BlockSpec((c, D), lambda k, i, o: (o[k], 0))
    cols = pl.BlockSpec((tm, c), lambda k, i, o: (i, o[k]))
    u_shape = jax.ShapeDtypeStruct((t, INW), BF)
    if first:
        args, in_specs = (x, g, w_in_t), [rows, _resident((1, D)), block]
        out_specs, out_shape, aliases = [rows, cols], [jax.ShapeDtypeStruct((t, D), BF), u_shape], {}
    else:
        args, in_specs = (n, w_in_t, u), [rows, block, ANY]
        out_specs, out_shape, aliases = [cols], [u_shape], {1 + n_deps + 2: 0}
    grid = (order.shape[0], t // tm)
    if swap is not None:
        (out,), swapped = _call_with_swap(
            body, (*deps, *args), swap, prefetch=(order,), grid=grid, in_specs=[ANY] * n_deps + in_specs, out_specs=out_specs,
            out_shape=out_shape, scratch_shapes=[], input_output_aliases={n_deps + 2: 0}, compiler_params=_params(2), name=name)
        return (n, out), swapped
    out = pl.pallas_call(
        body,
        grid_spec=pltpu.PrefetchScalarGridSpec(num_scalar_prefetch=1, grid=grid, in_specs=[ANY] * n_deps + in_specs,
                                               out_specs=out_specs),
        out_shape=out_shape, input_output_aliases=aliases, compiler_params=_params(2), name=name)(order, *deps, *args)
    return tuple(out) if first else (n, out[0])


def _ffn_down(x, u, w_out, name, part=(0, 1), out=None, swap=None):
    t = x.shape[0]
    tm = min(ROW_TILE_WIDE, t)
    steps = t // tm // part[1]
    first = part[0] * steps
    others = [out] if out is not None else []

    def body(x_ref, u_ref, wo_ref, *rest):
        a = u_ref[:, :F].astype(F32)
        b = u_ref[:, F:].astype(F32)
        h = (a * _sig(a) * b).astype(BF)
        rest[-1][...] = x_ref[...] + 0.5 * _dot(h, wo_ref[...])

    tile = lambda cols: pl.BlockSpec((tm, cols), lambda i: (first + i, 0))
    kw = dict(grid=(steps,), in_specs=[tile(D), tile(INW), _resident((F, D))] + [ANY] * len(others), out_specs=[tile(D)],
              out_shape=[jax.ShapeDtypeStruct((t, D), F32)], scratch_shapes=[],
              input_output_aliases={3: 0} if others else {}, compiler_params=_params(1), name=name)
    args = (x, u, w_out, *others)
    return pl.pallas_call(body, **kw)(*args) if swap is None else _call_with_swap(body, args, swap, **kw)


def _ffn_bwd(dxo, x, g, u, w_in_t, w_out, name, deps=()):
    t = x.shape[0]
    tm = min(ROW_TILE, t)

    def body(dxo_ref, x_ref, g_ref, u_ref, w_hbm, wo_ref, dx_ref, du_ref, h_ref, dy_ref, dg_ref, w_ref, w_sem):
        fetch = pltpu.make_async_copy(w_hbm, w_ref, w_sem)

        @pl.when(pl.program_id(0) == 0)
        def _():
            fetch.start()
            dg_ref[...] = jnp.zeros_like(dg_ref)

        dxo = dxo_ref[...]
        dy = (0.5 * dxo).astype(BF)
        dy_ref[...] = dy
        dh = _dot_nt(dy, wo_ref[...])
        a = u_ref[:, :F].astype(F32)
        b = u_ref[:, F:].astype(F32)
        s = _sig(a)
        sa = a * s
        h_ref[...] = (sa * b).astype(BF)
        du_ref[:, :F] = (dh * b * (s * (1.0 + a * (1.0 - s)))).astype(BF)
        du_ref[:, F:] = (dh * sa).astype(BF)

        @pl.when(pl.program_id(0) == 0)
        def _():
            fetch.wait()

        dn = _dot(du_ref[...], w_ref[...])
        dx, dg = _rms_bwd(dn, x_ref[...], g_ref[...])
        dx_ref[...] = dxo + dx
        dg_ref[...] += dg

    return _call(
        body, deps, (dxo, x, g, u, w_in_t, w_out), grid=(t // tm,),
        in_specs=[_row_tile(tm, D), _row_tile(tm, D), _resident((1, D)), _row_tile(tm, INW), ANY, _resident((F, D))],
        out_specs=[_row_tile(tm, D), _row_tile(tm, INW), _row_tile(tm, F), _row_tile(tm, D),
                   pl.BlockSpec((1, D), lambda i: (0, 0))],
        out_shape=[jax.ShapeDtypeStruct((t, D), F32), jax.ShapeDtypeStruct((t, INW), BF), jax.ShapeDtypeStruct((t, F), BF),
                   jax.ShapeDtypeStruct((t, D), BF), jax.ShapeDtypeStruct((1, D), F32)],
        scratch_shapes=[pltpu.VMEM((INW, D), BF), pltpu.SemaphoreType.DMA(())],
        compiler_params=_params(1), name=name)


def _wgrad(lhs, rhs, name, *, lhs_is_transposed, chunk, deps=()):
    t = rhs.shape[0]
    n = lhs.shape[0] if lhs_is_transposed else lhs.shape[1]
    c = min(chunk, n)

    def body(l_ref, r_ref, o_ref):
        if lhs_is_transposed:
            o_ref[...] = _dot(l_ref[...], r_ref[...]).astype(BF)
        else:
            o_ref[...] = _dot_tn(l_ref[...], r_ref[...]).astype(BF)

    lhs_spec = pl.BlockSpec((c, t), lambda j: (j, 0)) if lhs_is_transposed else pl.BlockSpec((t, c), lambda j: (0, j))
    return _call(
        body, deps, (lhs, rhs), grid=(n // c,),
        in_specs=[lhs_spec, _resident((t, D))],
        out_specs=pl.BlockSpec((c, D), lambda j: (j, 0)),
        out_shape=jax.ShapeDtypeStruct((n, D), BF),
        compiler_params=_params(1), name=name)


def _wgrad_mix(duc, dq_t, dkv_t, dgp, hm):
    t = hm.shape[0]
    c = 512
    first_q, first_kv, first_gate = R_Q[0] // c, R_KV[0] // c, R_GATE[0] // c

    def body(uc_ref, q_ref, kv_ref, gp_ref, h_ref, o_ref):
        j = pl.program_id(0)

        @pl.when(j < first_q)
        def _():
            o_ref[...] = _dot_tn(uc_ref[...], h_ref[...]).astype(BF)

        @pl.when((j >= first_q) & (j < first_kv))
        def _():
            o_ref[...] = _dot(q_ref[...], h_ref[...]).astype(BF)

        @pl.when((j >= first_kv) & (j < first_gate))
        def _():
            o_ref[...] = _dot(kv_ref[...], h_ref[...]).astype(BF)

        @pl.when(j >= first_gate)
        def _():
            o_ref[...] = _dot_tn(gp_ref[...], h_ref[...]).astype(BF)

    return pl.pallas_call(
        body, grid=(INW // c,),
        in_specs=[pl.BlockSpec((t, c), lambda j: (0, jnp.clip(j, 0, first_q - 1))),
                  pl.BlockSpec((c, t), lambda j: (jnp.clip(j - first_q, 0, first_kv - first_q - 1), 0)),
                  pl.BlockSpec((c, t), lambda j: (jnp.clip(j - first_kv, 0, first_gate - first_kv - 1), 0)),
                  pl.BlockSpec((t, c), lambda j: (0, jnp.clip(j - first_gate, 0, INW // c - first_gate - 1))),
                  _resident((t, D))],
        out_specs=pl.BlockSpec((c, D), lambda j: (j, 0)),
        out_shape=jax.ShapeDtypeStruct((INW, D), BF),
        compiler_params=_params(1), name="mix_dw_in")(duc, dq_t, dkv_t, dgp, hm)


def _mix_proj(x, g, w_t):
    t = x.shape[0]
    tm = min(ROW_TILE_WIDE, t)

    def body(x_ref, g_ref, w_ref, hm_ref, uc_ref, gp_ref, qkv_ref):
        r, xh = _rms_stats(x_ref[...])
        hm = (xh * g_ref[...]).astype(BF)
        hm_ref[...] = hm
        uc_ref[...] = _dot_nt(hm, w_ref[R_CONV[0]:R_CONV[1], :]).astype(BF)
        gp_ref[...] = _dot_nt(hm, w_ref[R_GATE[0]:R_GATE[1], :]).astype(BF)
        qkv_ref[...] = _dot_nt(w_ref[R_QKV[0]:R_QKV[1], :], hm).astype(BF)

    return pl.pallas_call(
        body, grid=(t // tm,),
        in_specs=[_row_tile(tm, D), _resident((1, D)), _resident((INW, D))],
        out_specs=[_row_tile(tm, D), _row_tile(tm, 2 * D), _row_tile(tm, 2 * D), pl.BlockSpec((1536, tm), lambda i: (0, i))],
        out_shape=[jax.ShapeDtypeStruct((t, D), BF), jax.ShapeDtypeStruct((t, 2 * D), BF),
                   jax.ShapeDtypeStruct((t, 2 * D), BF), jax.ShapeDtypeStruct((1536, t), BF)],
        compiler_params=_params(1), name="mix_proj")(x, g, w_t)


CONV_HALO = 32
CONV_LEAD = CONV_HALO - (CW - 1)


def _glu(uc):
    uc = uc.astype(F32)
    return uc[:, :D] * _sig(uc[:, D:])


def _ln_stats(zc):
    mu = jnp.mean(zc, axis=-1, keepdims=True)
    zm = zc - mu
    r = lax.rsqrt(jnp.mean(zm * zm, axis=-1, keepdims=True) + EPS)
    return r, zm * r


CONV_SHIFTS = 8
CONV_CHUNK = 32


def _store_shifted(buf, rows):
    for b in range(1, CONV_SHIFTS):
        buf[b, 0:rows - 8, :] = buf[0, pl.ds(b, rows - 8), :]


def _conv_fwd(uc, dwk, dwb, lng, lnb, swap=None):
    t = uc.shape[0]
    tm = min(512, t)
    per = tm // CONV_HALO
    ext = tm + CONV_HALO

    def body(cur_ref, prev_ref, k_ref, kb_ref, g_ref, b_ref, o_ref, zc_ref, zsh):
        i = pl.program_id(0)
        zsh[0, 0:CONV_HALO, :] = _glu(prev_ref[...]) * (i > 0).astype(F32)
        zsh[0, CONV_HALO:, :] = _glu(cur_ref[...])
        _store_shifted(zsh, ext)

        def chunk(ci, carry):
            r0 = pl.multiple_of(ci * CONV_CHUNK, CONV_CHUNK)
            acc = jnp.zeros((CONV_CHUNK, D), F32) + kb_ref[...]
            for w in range(CW):
                a, b = divmod(CONV_LEAD + w, 8)
                acc = acc + k_ref[w:w + 1, :] * zsh[b, pl.ds(r0 + 8 * a, CONV_CHUNK), :]
            zc_ref[pl.ds(r0, CONV_CHUNK), :] = acc
            return carry

        lax.fori_loop(0, tm // CONV_CHUNK, chunk, 0)
        r, xh = _ln_stats(zc_ref[...])
        y = xh * g_ref[...] + b_ref[...]
        o_ref[...] = (y * _sig(y)).astype(BF)

    kw = dict(
        grid=(t // tm,),
        in_specs=[_row_tile(tm, 2 * D),
                  pl.BlockSpec((CONV_HALO, 2 * D), lambda i: (jnp.maximum(i * per - 1, 0), 0)),
                  _resident((CWP, D)), _resident((1, D)), _resident((1, D)), _resident((1, D))],
        out_specs=[_row_tile(tm, D), _row_tile(tm, D)],
        out_shape=[jax.ShapeDtypeStruct((t, D), BF), jax.ShapeDtypeStruct((t, D), F32)],
        scratch_shapes=[pltpu.VMEM((CONV_SHIFTS, ext, D), F32)],
        compiler_params=_params(1), name="conv_fwd")
    args = (uc, uc, dwk, dwb, lng, lnb)
    return pl.pallas_call(body, **kw)(*args) if swap is None else _call_with_swap(body, args, swap, **kw)


def _conv_bwd(uc, zc, dzs, dwk, lng, lnb):
    t = uc.shape[0]
    tm = min(ROW_TILE_WIDE, t)
    per = tm // CONV_HALO
    n_tiles = t // tm
    ext = tm + CONV_HALO
    last_block = t // CONV_HALO - 1

    def body(cur_ref, zc_ref, zcn_ref, dz_ref, dzn_ref, k_ref, g_ref, b_ref,
             duc_ref, dk_ref, dkb_ref, dg_ref, db_ref, dsh, dk8, z_scr):
        i = pl.program_id(0)

        @pl.when(i == 0)
        def _():
            dk8[...] = jnp.zeros_like(dk8)
            dkb_ref[...] = jnp.zeros_like(dkb_ref)
            dg_ref[...] = jnp.zeros_like(dg_ref)
            db_ref[...] = jnp.zeros_like(db_ref)

        has_next = (i < n_tiles - 1).astype(F32)
        z_scr[...] = _glu(cur_ref[...])
        gain = g_ref[...]

        def ln_silu_bwd(zc, dzs, live):
            r, xh = _ln_stats(zc)
            y = xh * gain + b_ref[...]
            sy = _sig(y)
            dy = dzs * (sy * (1.0 + y * (1.0 - sy))) * live
            dxh = dy * gain
            dzc = r * (dxh - jnp.mean(dxh, axis=-1, keepdims=True) - xh * jnp.mean(dxh * xh, axis=-1, keepdims=True))
            return dzc, dy, xh

        dzc, dy, xh = ln_silu_bwd(zc_ref[...], dz_ref[...], 1.0)
        dsh[0, 0:tm, :] = dzc
        dg_ref[...] += jnp.sum(dy * xh, axis=0, keepdims=True)
        db_ref[...] += jnp.sum(dy, axis=0, keepdims=True)
        dkb_ref[...] += jnp.sum(dzc, axis=0, keepdims=True)
        dsh[0, tm:, :] = ln_silu_bwd(zcn_ref[...], dzn_ref[...], has_next)[0]
        _store_shifted(dsh, ext)

        def chunk(ci, carry):
            r0 = pl.multiple_of(ci * CONV_CHUNK, CONV_CHUNK)
            z_c = z_scr[pl.ds(r0, CONV_CHUNK), :]
            dz = jnp.zeros((CONV_CHUNK, D), F32)
            for w in range(CW):
                a, b = divmod(CW - 1 - w, 8)
                window = dsh[b, pl.ds(r0 + 8 * a, CONV_CHUNK), :]
                dz = dz + k_ref[w:w + 1, :] * window
                prod = z_c * window
                part = prod[0:8, :]
                for j in range(1, CONV_CHUNK // 8):
                    part = part + prod[8 * j:8 * j + 8, :]
                dk8[w] += part
            ucc = cur_ref[pl.ds(r0, CONV_CHUNK), :].astype(F32)
            sg = _sig(ucc[:, D:])
            duc_ref[pl.ds(r0, CONV_CHUNK), 0:D] = (dz * sg).astype(BF)
            duc_ref[pl.ds(r0, CONV_CHUNK), D:2 * D] = (dz * ucc[:, :D] * sg * (1.0 - sg)).astype(BF)
            return carry

        lax.fori_loop(0, tm // CONV_CHUNK, chunk, 0)

        @pl.when(i == n_tiles - 1)
        def _():
            dk_ref[...] = jnp.sum(dk8[...], axis=1)

    vec = pl.BlockSpec((1, D), lambda i: (0, 0))
    next_halo = pl.BlockSpec((CONV_HALO, D), lambda i: (jnp.minimum((i + 1) * per, last_block), 0))
    return pl.pallas_call(
        body, grid=(n_tiles,),
        in_specs=[_row_tile(tm, 2 * D), _row_tile(tm, D), next_halo, _row_tile(tm, D), next_halo,
                  _resident((CWP, D)), _resident((1, D)), _resident((1, D))],
        out_specs=[_row_tile(tm, 2 * D), pl.BlockSpec((CWP, D), lambda i: (0, 0)), vec, vec, vec],
        out_shape=[jax.ShapeDtypeStruct((t, 2 * D), BF), jax.ShapeDtypeStruct((CWP, D), F32),
                   jax.ShapeDtypeStruct((1, D), F32), jax.ShapeDtypeStruct((1, D), F32), jax.ShapeDtypeStruct((1, D), F32)],
        scratch_shapes=[pltpu.VMEM((CONV_SHIFTS, ext, D), F32), pltpu.VMEM((CWP, 8, D), F32), pltpu.VMEM((tm, D), F32)],
        compiler_params=_params(1), name="conv_bwd")(uc, zc, zc, dzs, dzs, dwk, lng, lnb)


def _norm_rows(xt, g):
    r = lax.rsqrt(jnp.mean(xt * xt, axis=0, keepdims=True) + EPS)
    xh = xt * r
    return xh * g, r, xh


ATT_TQ = 1024


def _attn_specs(t, tq):
    per = tq // BLK
    return [pl.BlockSpec((1536, tq), lambda i: (0, i)),
            pl.BlockSpec((512, BLK), lambda i: (2, jnp.maximum(i * per - 1, 0))),
            _resident((HD, 1)), _resident((HD, 1)), _resident((NKV, 1, GRP * BLK)),
            _resident((2, NKV, 2 * BLK, GRP * BLK))]


def _attn_window(hk, sb, qkv_ref, halo_ref, kn_cur, kn_halo):
    v0 = D + NKV * HD + hk * HD
    if sb == 0:
        k_prev = kn_halo[hk]
        v_prev = halo_ref[NKV * HD + hk * HD:NKV * HD + (hk + 1) * HD, :]
    else:
        k_prev = kn_cur[hk][:, (sb - 1) * BLK:sb * BLK]
        v_prev = qkv_ref[v0:v0 + HD, (sb - 1) * BLK:sb * BLK]
    kw = jnp.concatenate([k_prev, kn_cur[hk][:, sb * BLK:(sb + 1) * BLK]], axis=1).astype(BF)
    vw = jnp.concatenate([v_prev, qkv_ref[v0:v0 + HD, sb * BLK:(sb + 1) * BLK]], axis=1)
    return kw, vw


def _attn_probs(kw, qc, bias, sink):
    st = _dot_tn(kw, qc) + bias
    m = jnp.maximum(jnp.max(st, axis=0, keepdims=True), sink)
    p = jnp.exp(st - m)
    e_sink = jnp.exp(sink - m)
    inv = 1.0 / (jnp.sum(p, axis=0, keepdims=True) + e_sink)
    return p * inv, e_sink * inv


def _attn_fwd(qkv_t, qg, kg, sink_rows, bias_t):
    t = qkv_t.shape[1]
    tq = min(ATT_TQ, t)
    n_sub = tq // BLK

    def body(qkv_ref, halo_ref, qg_ref, kg_ref, sink_ref, bias_ref, o_ref, p_ref, ps_ref):
        i = pl.program_id(0)
        first = (i == 0).astype(jnp.int32)
        kgain = kg_ref[...]
        qgain = qg_ref[...]
        kn_cur = [_norm_rows(qkv_ref[D + h * HD:D + (h + 1) * HD, :].astype(F32), kgain)[0] for h in range(NKV)]
        kn_halo = [_norm_rows(halo_ref[h * HD:(h + 1) * HD, :].astype(F32), kgain)[0] for h in range(NKV)]
        for hk in range(NKV):
            for sb in range(n_sub):
                cols = slice(sb * BLK, (sb + 1) * BLK)
                kw, vw = _attn_window(hk, sb, qkv_ref, halo_ref, kn_cur, kn_halo)
                qc = jnp.concatenate(
                    [_norm_rows(qkv_ref[(GRP * hk + g) * HD:(GRP * hk + g + 1) * HD, cols].astype(F32), qgain)[0] * QK_SCALE
                     for g in range(GRP)], axis=1).astype(BF)
                bias = bias_ref[first, hk] if sb == 0 else bias_ref[0, hk]
                p, p_sink = _attn_probs(kw, qc, bias, sink_ref[hk])
                p = p.astype(BF)
                p_ref[sb, hk] = p
                ps_ref[sb, hk] = p_sink
                o = _dot(vw, p)
                for g in range(GRP):
                    head = GRP * hk + g
                    o_ref[head * HD:(head + 1) * HD, cols] = o[:, g * BLK:(g + 1) * BLK].astype(BF)

    return pl.pallas_call(
        body, grid=(t // tq,),
        in_specs=_attn_specs(t, tq),
        out_specs=[pl.BlockSpec((D, tq), lambda i: (0, i)),
                   pl.BlockSpec((n_sub, NKV, 2 * BLK, GRP * BLK), lambda i: (i, 0, 0, 0)),
                   pl.BlockSpec((n_sub, NKV, 1, GRP * BLK), lambda i: (i, 0, 0, 0))],
        out_shape=[jax.ShapeDtypeStruct((D, t), BF), jax.ShapeDtypeStruct((t // BLK, NKV, 2 * BLK, GRP * BLK), BF),
                   jax.ShapeDtypeStruct((t // BLK, NKV, 1, GRP * BLK), F32)],
        compiler_params=_params(1), name="attn_fwd")(qkv_t, qkv_t, qg, kg, sink_rows, bias_t)


def _attn_bwd(qkv_t, do_t, probs, sink_probs, qg, kg, onehot_t, deps=()):
    t = qkv_t.shape[1]
    tq = min(ATT_TQ, t)
    n_sub = tq // BLK
    n_tiles = t // tq

    def body(qkv_ref, halo_ref, do_ref, p_ref, ps_ref, qg_ref, kg_ref, oh_ref,
             dq_ref, ckv_ref, dqg_ref, dsink_ref, dbias_ref, qg_scr, sink_scr, ds_scr):
        i = pl.program_id(0)

        @pl.when(i == 0)
        def _():
            qg_scr[...] = jnp.zeros_like(qg_scr)
            sink_scr[...] = jnp.zeros_like(sink_scr)
            ds_scr[...] = jnp.zeros_like(ds_scr)

        kgain = kg_ref[...]
        qgain = qg_ref[...]
        kn_cur = [_norm_rows(qkv_ref[D + h * HD:D + (h + 1) * HD, :].astype(F32), kgain)[0] for h in range(NKV)]
        kn_halo = [_norm_rows(halo_ref[h * HD:(h + 1) * HD, :].astype(F32), kgain)[0] for h in range(NKV)]
        dqg = jnp.zeros((HD, BLK), F32)
        for hk in range(NKV):
            for sb in range(n_sub):
                cols = slice(sb * BLK, (sb + 1) * BLK)
                kw, vw = _attn_window(hk, sb, qkv_ref, halo_ref, kn_cur, kn_halo)
                qn, qr, qh = [], [], []
                for g in range(GRP):
                    head = GRP * hk + g
                    n_, r_, h_ = _norm_rows(qkv_ref[head * HD:(head + 1) * HD, cols].astype(F32), qgain)
                    qn.append(n_)
                    qr.append(r_)
                    qh.append(h_)
                qc = (jnp.concatenate(qn, axis=1) * QK_SCALE).astype(BF)
                p_bf = p_ref[sb, hk]
                p = p_bf.astype(F32)
                doc = jnp.concatenate([do_ref[(GRP * hk + g) * HD:(GRP * hk + g + 1) * HD, cols] for g in range(GRP)], axis=1)
                dp = _dot_tn(vw, doc)
                delta = jnp.sum(p * dp, axis=0, keepdims=True)
                ds = p * (dp - delta)
                sink_scr[hk] += -(ps_ref[sb, hk] * delta)
                ds_scr[hk] += ds
                dsb = ds.astype(BF)
                dqc = _dot(kw, dsb) * QK_SCALE
                ckv_ref[sb, hk * HD:(hk + 1) * HD, :] = _dot_nt(qc, dsb)
                ckv_ref[sb, NKV * HD + hk * HD:NKV * HD + (hk + 1) * HD, :] = _dot_nt(doc, p_bf)
                for g in range(GRP):
                    head = GRP * hk + g
                    dqn = dqc[:, g * BLK:(g + 1) * BLK]
                    dqh = dqn * qgain
                    dq = qr[g] * (dqh - qh[g] * jnp.mean(dqh * qh[g], axis=0, keepdims=True))
                    dq_ref[head * HD:(head + 1) * HD, cols] = dq.astype(BF)
                    dqg = dqg + dqn * qh[g]
        qg_scr[...] += dqg

        @pl.when(i == n_tiles - 1)
        def _():
            dqg_ref[...] = jnp.sum(qg_scr[...], axis=1, keepdims=True)
            dsink_ref[...] = _group_lane_sums(sink_scr[:, 0, :])

            def bucket(b, carry):
                oh = jnp.concatenate([oh_ref[b]] * GRP, axis=1)
                dbias_ref[b] = _group_lane_sums(jnp.sum(ds_scr[...] * oh[None], axis=1))
                return carry

            lax.fori_loop(0, NBUCKET, bucket, 0)

    return _call(
        body, deps, (qkv_t, qkv_t, do_t, probs, sink_probs, qg, kg, onehot_t), grid=(n_tiles,),
        in_specs=_attn_specs(t, tq)[:2] + [pl.BlockSpec((D, tq), lambda i: (0, i)),
                                           pl.BlockSpec((n_sub, NKV, 2 * BLK, GRP * BLK), lambda i: (i, 0, 0, 0)),
                                           pl.BlockSpec((n_sub, NKV, 1, GRP * BLK), lambda i: (i, 0, 0, 0))]
        + _attn_specs(t, tq)[2:4] + [_resident((NBUCKET, 2 * BLK, BLK))],
        out_specs=[pl.BlockSpec((D, tq), lambda i: (0, i)),
                   pl.BlockSpec((n_sub, 2 * NKV * HD, 2 * BLK), lambda i: (i, 0, 0)),
                   pl.BlockSpec((HD, 1), lambda i: (0, 0)),
                   pl.BlockSpec((NKV, BLK), lambda i: (0, 0)),
                   pl.BlockSpec((NBUCKET, NKV, BLK), lambda i: (0, 0, 0))],
        out_shape=[jax.ShapeDtypeStruct((D, t), BF),
                   jax.ShapeDtypeStruct((t // BLK, 2 * NKV * HD, 2 * BLK), F32),
                   jax.ShapeDtypeStruct((HD, 1), F32),
                   jax.ShapeDtypeStruct((NKV, BLK), F32),
                   jax.ShapeDtypeStruct((NBUCKET, NKV, BLK), F32)],
        scratch_shapes=[pltpu.VMEM((HD, BLK), F32), pltpu.VMEM((NKV, 1, GRP * BLK), F32),
                        pltpu.VMEM((NKV, 2 * BLK, GRP * BLK), F32)],
        compiler_params=_params(1), name="attn_bwd")


def _kv_combine_tile(c_ref, cn_ref, has_next, k_ref, kgain, o_ref):
    rows = NKV * HD
    per = c_ref.shape[0]
    dkg = jnp.zeros((HD, BLK), F32)
    for s in range(per):
        cols = slice(s * BLK, (s + 1) * BLK)
        after = c_ref[s + 1, :, :BLK] if s + 1 < per else cn_ref[0, :, :BLK] * has_next
        d = c_ref[s, :, BLK:] + after
        o_ref[rows:, cols] = d[rows:, :].astype(BF)
        for h in range(NKV):
            _, r, kh = _norm_rows(k_ref[h * HD:(h + 1) * HD, cols].astype(F32), kgain)
            dkn = d[h * HD:(h + 1) * HD, :]
            dkh = dkn * kgain
            o_ref[h * HD:(h + 1) * HD, cols] = (r * (dkh - kh * jnp.mean(dkh * kh, axis=0, keepdims=True))).astype(BF)
            dkg = dkg + dkn * kh
    return dkg


def _group_lane_sums(v):
    lane_group = lax.broadcasted_iota(jnp.int32, (1, GRP * BLK), 1) // BLK
    col = lax.broadcasted_iota(jnp.int32, (1, BLK), 1)
    out = jnp.zeros((v.shape[0], BLK), F32)
    for g in range(GRP):
        s = jnp.sum(jnp.where(lane_group == g, v, 0.0), axis=1, keepdims=True)
        out = jnp.where(col == g, s, out)
    return out


def _mix_out(zs, o_t, gp, x, w_cp, w_o, w_out):
    t = x.shape[0]
    tm = min(ROW_TILE_WIDE, t)

    def body(zs_ref, ot_ref, gp_ref, x_ref, wcp_ref, wo_ref, wout_ref, xo_ref, a_ref, b_ref, m_ref):
        a = _dot(zs_ref[...], wcp_ref[...])
        b = _dot_tn(ot_ref[...], wo_ref[...])
        a_ref[...] = a.astype(BF)
        b_ref[...] = b.astype(BF)
        merged = (_sig(gp_ref[:, :D].astype(F32)) * a + _sig(gp_ref[:, D:].astype(F32)) * b).astype(BF)
        m_ref[...] = merged
        xo_ref[...] = x_ref[...] + _dot(merged, wout_ref[...])

    return pl.pallas_call(
        body, grid=(t // tm,),
        in_specs=[_row_tile(tm, D), pl.BlockSpec((D, tm), lambda i: (0, i)), _row_tile(tm, 2 * D), _row_tile(tm, D),
                  _resident((D, D)), _resident((D, D)), _resident((D, D))],
        out_specs=[_row_tile(tm, D)] * 4,
        out_shape=[jax.ShapeDtypeStruct((t, D), F32)] + [jax.ShapeDtypeStruct((t, D), BF)] * 3,
        compiler_params=_params(1), name="mix_out")(zs, o_t, gp, x, w_cp, w_o, w_out)


def _mix_out_bwd(dx, a, b, gp, w_cp, w_o, w_out, deps=()):
    t = dx.shape[0]
    tm = min(ROW_TILE_WIDE, t)

    def body(dx_ref, a_ref, b_ref, gp_ref, wcp_ref, wo_ref, wout_ref, dzs_ref, dot_ref, dgp_ref, da_ref, db_ref, dxb_ref):
        dxb = dx_ref[...].astype(BF)
        dxb_ref[...] = dxb
        dm = _dot_nt(dxb, wout_ref[...])
        gc = _sig(gp_ref[:, :D].astype(F32))
        ga = _sig(gp_ref[:, D:].astype(F32))
        da = (dm * gc).astype(BF)
        db = (dm * ga).astype(BF)
        da_ref[...] = da
        db_ref[...] = db
        dgp_ref[:, :D] = (dm * a_ref[...].astype(F32) * gc * (1.0 - gc)).astype(BF)
        dgp_ref[:, D:] = (dm * b_ref[...].astype(F32) * ga * (1.0 - ga)).astype(BF)
        dzs_ref[...] = _dot_nt(da, wcp_ref[...])
        dot_ref[...] = _dot_nt(wo_ref[...], db).astype(BF)

    return _call(
        body, deps, (dx, a, b, gp, w_cp, w_o, w_out), grid=(t // tm,),
        in_specs=[_row_tile(tm, D), _row_tile(tm, D), _row_tile(tm, D), _row_tile(tm, 2 * D),
                  _resident((D, D)), _resident((D, D)), _resident((D, D))],
        out_specs=[_row_tile(tm, D), pl.BlockSpec((D, tm), lambda i: (0, i)), _row_tile(tm, 2 * D),
                   _row_tile(tm, D), _row_tile(tm, D), _row_tile(tm, D)],
        out_shape=[jax.ShapeDtypeStruct((t, D), F32), jax.ShapeDtypeStruct((D, t), BF), jax.ShapeDtypeStruct((t, 2 * D), BF),
                   jax.ShapeDtypeStruct((t, D), BF), jax.ShapeDtypeStruct((t, D), BF), jax.ShapeDtypeStruct((t, D), BF)],
        compiler_params=_params(1), name="mix_out_bwd")


def _mix_proj_bwd(dxo, duc, dq_t, ckv, qkv_t, kg, dgp, x, g, w_t):
    t = x.shape[0]
    tm = min(ROW_TILE_WIDE, t)
    per = tm // BLK
    steps = t // tm
    kv_rows = 2 * NKV * HD

    def body(dxo_ref, duc_ref, dq_ref, c_ref, cn_ref, k_ref, kg_ref, dgp_ref, x_ref, g_ref, w_ref,
             dx_ref, dg_ref, dkv_ref, dkg_ref, kg_scr):
        i = pl.program_id(0)

        @pl.when(i == 0)
        def _():
            dg_ref[...] = jnp.zeros_like(dg_ref)
            kg_scr[...] = jnp.zeros_like(kg_scr)

        kg_scr[...] += _kv_combine_tile(c_ref, cn_ref, (i < steps - 1).astype(F32), k_ref, kg_ref[...], dkv_ref)
        dn = _dot(duc_ref[...], w_ref[R_CONV[0]:R_CONV[1], :])
        dn = dn + _dot(dgp_ref[...], w_ref[R_GATE[0]:R_GATE[1], :])
        dn = dn + _dot_tn(dq_ref[...], w_ref[R_Q[0]:R_Q[1], :])
        dn = dn + _dot_tn(dkv_ref[...], w_ref[R_KV[0]:R_KV[1], :])
        dx, dg = _rms_bwd(dn, x_ref[...], g_ref[...])
        dx_ref[...] = dxo_ref[...] + dx
        dg_ref[...] += dg

        @pl.when(i == steps - 1)
        def _():
            dkg_ref[...] = jnp.sum(kg_scr[...], axis=1, keepdims=True)

    return pl.pallas_call(
        body, grid=(steps,),
        in_specs=[_row_tile(tm, D), _row_tile(tm, 2 * D), pl.BlockSpec((D, tm), lambda i: (0, i)),
                  pl.BlockSpec((per, kv_rows, 2 * BLK), lambda i: (i, 0, 0)),
                  pl.BlockSpec((1, kv_rows, 2 * BLK), lambda i: (jnp.minimum((i + 1) * per, t // BLK - 1), 0, 0)),
                  pl.BlockSpec((NKV * HD, tm), lambda i: (D // (NKV * HD), i)), _resident((HD, 1)),
                  _row_tile(tm, 2 * D), _row_tile(tm, D), _resident((1, D)), _resident((INW, D))],
        out_specs=[_row_tile(tm, D), pl.BlockSpec((1, D), lambda i: (0, 0)), pl.BlockSpec((kv_rows, tm), lambda i: (0, i)),
                   pl.BlockSpec((HD, 1), lambda i: (0, 0))],
        out_shape=[jax.ShapeDtypeStruct((t, D), F32), jax.ShapeDtypeStruct((1, D), F32),
                   jax.ShapeDtypeStruct((kv_rows, t), BF), jax.ShapeDtypeStruct((HD, 1), F32)],
        scratch_shapes=[pltpu.VMEM((HD, BLK), F32)],
        compiler_params=_params(1), name="mix_proj_bwd")(dxo, duc, dq_t, ckv, ckv, qkv_t, kg, dgp, x, g, w_t)


def _attention_tables():
    kj = np.arange(2 * BLK)[:, None]
    qi = np.arange(BLK)[None, :]
    dist = qi + BLK - kj
    in_win = (dist >= 0) & (dist < BLK)
    dpos = np.maximum(dist, 0)
    max_exact = NBUCKET // 2
    dfl = np.maximum(dpos, 1).astype(np.float32)
    large = max_exact + (np.log(dfl / np.float32(max_exact)) / np.float32(math.log(BLK / max_exact))
                         * np.float32(NBUCKET - max_exact)).astype(np.int32)
    large = np.minimum(large, NBUCKET - 1)
    bucket = np.where(dpos < max_exact, dpos, large)
    onehot = (bucket[None] == np.arange(NBUCKET)[:, None, None]).astype(np.float32)
    mask = in_win.astype(np.float32)
    mask_first = mask * (kj >= BLK)
    masks = np.stack([np.tile(mask, (1, GRP)), np.tile(mask_first, (1, GRP))])
    return onehot, masks


def _bias_table(rel_bias, onehot):
    tab = jnp.einsum("bkq,bh->hkq", onehot, rel_bias, precision=lax.Precision.HIGHEST)
    tab = tab.reshape(NKV, GRP, 2 * BLK, BLK)
    return jnp.transpose(tab, (0, 2, 1, 3)).reshape(NKV, 2 * BLK, GRP * BLK)


def _local_step(x, target, vec, ffn1_up, weights_of, wgrad, wgrads, grads_done, small_done):
    onehot_np, masks_np = _attention_tables()
    onehot = jnp.asarray(onehot_np)
    masks = jnp.asarray(masks_np)
    bias_t = jnp.where(masks[:, None] > 0.5, _bias_table(vec["rel_bias"], onehot)[None], NEG)
    sink_rows = jnp.repeat(vec["attn_sinks"].reshape(NKV, 1, GRP), BLK, axis=2)
    qg = vec["q_norm"].reshape(HD, 1)
    kg = vec["k_norm"].reshape(HD, 1)
    g1 = vec["ffn1_norm"].reshape(1, D)
    gm = vec["mix_norm"].reshape(1, D)
    g2 = vec["ffn2_norm"].reshape(1, D)
    dwb = vec["conv_dw_bias"].reshape(1, D)
    lng = vec["conv_ln_g"].reshape(1, D)
    lnb = vec["conv_ln_b"].reshape(1, D)

    n1, u1, w1 = ffn1_up(x, g1, (bias_t, sink_rows))
    x1, = _ffn_down(x, u1, w1["ffn1_w_out"], "ffn1_down_first", part=(0, 2))
    (x1,), wm = weights_of("mix_proj", (x1,), during=functools.partial(
        _ffn_down, x, u1, w1["ffn1_w_out"], "ffn1_down", part=(1, 2), out=x1))
    dwk = jnp.pad(wm["conv_dw_kernel"], ((0, CWP - CW), (0, 0)))
    hm, uc, gp, qkv_t = _mix_proj(x1, gm, wm["w_in"])
    (zs, zc), merge = weights_of("mix_merge", (uc,), during=functools.partial(_conv_fwd, uc, dwk, dwb, lng, lnb))
    wm.update(merge)
    o_t, probs, sink_probs = _attn_fwd(qkv_t, qg, kg, sink_rows, bias_t)
    x2, a, b, merged = _mix_out(zs, o_t, gp, x1, wm["conv_w_proj"], wm["attn_w_o"], wm["w_out"])
    w2 = weights_of("ffn2", (x2,))
    gv = {}
    n2, du2, h2, dy2, dx2, sq, gv["ffn2_norm"] = _ffn_last(x2, target, g2, w2["ffn2_w_in"], w2["ffn2_w_out"], "ffn2")

    deps = grads_done("ffn2", {"ffn2_w_in": wgrad(du2, n2, "ffn2_dw_in", False),
                               "ffn2_w_out": wgrad(h2, dy2, "ffn2_dw_out", False)})

    dzs, do_t, dgp, da, db, dx2b = _mix_out_bwd(dx2, a, b, gp, wm["conv_w_proj"], wm["attn_w_o"], wm["w_out"], deps=deps)
    grads = wgrads([(merged, dx2b, False), (zs, da, False), (o_t, db, True)], "mix_dw_merge")
    deps = grads_done("mix_out", dict(zip(("w_out", "conv_w_proj", "attn_w_o"), grads)))

    dq_t, ckv, dqg, dsink, dbias = _attn_bwd(qkv_t, do_t, probs, sink_probs, qg, kg, onehot, deps=deps)
    gv["q_norm"] = dqg.reshape(HD)
    gv["attn_sinks"] = dsink[:, :GRP].reshape(NQ)
    gv["rel_bias"] = dbias[:, :, :GRP].reshape(NBUCKET, NQ)

    duc, dk_conv, gv["conv_dw_bias"], gv["conv_ln_g"], gv["conv_ln_b"] = _conv_bwd(uc, zc, dzs, dwk, lng, lnb)
    gv["conv_dw_kernel"] = dk_conv[:CW]

    dx1, gv["mix_norm"], dkv_t, dkg = _mix_proj_bwd(dx2, duc, dq_t, ckv, qkv_t, kg, dgp, x1, gm, wm["w_in"])
    gv["k_norm"] = dkg.reshape(HD)
    deps = grads_done("mix_in", {"w_in": _wgrad_mix(duc, dq_t, dkv_t, dgp, hm)})

    dx0, du1, h1, dy1, gv["ffn1_norm"] = _ffn_bwd(dx1, x, g1, u1, w1["ffn1_w_in"], w1["ffn1_w_out"], "ffn1_bwd", deps=deps)
    for k in ("ffn1_norm", "mix_norm", "ffn2_norm", "conv_dw_bias", "conv_ln_g", "conv_ln_b"):
        gv[k] = gv[k].reshape(D)
    deps = small_done(gv, sq)
    deps = grads_done("ffn1_in", {"ffn1_w_in": wgrad(du1, n1, "ffn1_dw_in", False, deps)})
    grads_done("ffn1_out", {"ffn1_w_out": wgrad(h1, dy1, "ffn1_dw_out", False, deps)})
    return dx0


MESH_ID = pl.DeviceIdType.MESH


def _position():
    return lax.axis_index("x"), lax.axis_index("y"), lax.axis_index("c")


def _shard_rows(ref, index, rows):
    return ref.at[pl.ds(pl.multiple_of(index * rows, 16), rows), :]


def _prep(weights, taps, me, name, deps=(), swap=None):
    n = len(weights)
    n_deps = len(deps)
    with_taps = taps is not None

    def body(me_ref, *refs):
        refs = refs[n_deps:]
        ins, outs = refs[:len(refs) // 2], refs[len(refs) // 2:]
        for k in range(n):
            outs[k][...] = ins[k][...].astype(BF)
        if with_taps:
            outs[n][0:CW, :] = ins[n][...]
            outs[n][CW:, :] = jnp.zeros((CWP - CW, BLK), F32)

    shard_shapes = [w.shape for w in weights] + [(CWP, BLK)] * with_taps
    dtypes = [BF] * n + [F32] * with_taps
    ins = list(weights) + [taps] * with_taps
    in_specs = [ANY] * n_deps + [pl.BlockSpec(a.shape, lambda i, m: (0, 0), pipeline_mode=pl.Buffered(1)) for a in ins]
    out_specs = [pl.BlockSpec(s, lambda i, m: (m[0], 0)) for s in shard_shapes]
    out_shape = [jax.ShapeDtypeStruct((N_DEV * s[0], s[1]), d) for s, d in zip(shard_shapes, dtypes)]
    if swap is not None:
        return _call_with_swap(body, (*deps, *ins), swap, prefetch=(me,), grid=(1,), in_specs=in_specs, out_specs=out_specs,
                               out_shape=out_shape, scratch_shapes=[], compiler_params=_params(1), name=name)
    return pl.pallas_call(
        body,
        grid_spec=pltpu.PrefetchScalarGridSpec(num_scalar_prefetch=1, grid=(1,), in_specs=in_specs, out_specs=out_specs),
        out_shape=out_shape, compiler_params=_params(1), name=name)(me, *deps, *ins)


HBM = pl.BlockSpec(memory_space=pltpu.HBM)
SEM = pl.BlockSpec(memory_space=pltpu.SEMAPHORE)
DATAFLOW = pltpu.SideEffectType.DATAFLOW_SIDE_EFFECTING
TOKEN = jax.ShapeDtypeStruct((8, 128), F32)


def _in_hbm(x):
    return pltpu.with_memory_space_constraint(x, pltpu.HBM)


def _hbm_like(arrays):
    return [pltpu.HBM(a.shape, a.dtype) for a in arrays]


def _other_chips(x, y):
    return [(1 - x, y), (x, 1 - y), (1 - x, 1 - y)]


def _device_index(chip, c):
    return 4 * chip[0] + 2 * chip[1] + c


def _chip_index(chip):
    return 2 * chip[0] + chip[1]


class _Exchange:
    def __init__(self, gather, all_cores=False):
        self.gather = gather
        self.all_cores = all_cores
        self.n_peers = N_DEV - 1 if all_cores else 3

    def peers(self, x, y, c):
        if self.all_cores:
            return [(x ^ (k >> 2), y ^ ((k >> 1) & 1), c ^ (k & 1)) for k in range(1, N_DEV)]
        return [(*chip, c) for chip in _other_chips(x, y)]

    def sent(self, x, y, c, peer):
        return _device_index((x, y), c) if self.gather else _chip_index(peer[:2])

    def lands_at(self, x, y, c):
        return _device_index((x, y), c) if self.gather else _chip_index((x, y))

    def arrives_at(self, peer):
        return _device_index(peer[:2], peer[2]) if self.gather else _chip_index(peer[:2])


def _ici_copies_start(sets, sources, landings, exchanges, name, deps=()):
    n = len(landings)
    arrays = (list(sources) if sources is not None else []) + list(landings)
    first_land = len(arrays) - n
    n_sets = len(sets)
    n_deps = len(deps)

    def body(*refs):
        refs = refs[n_deps:]
        src, land = refs[:n], refs[first_land:first_land + n]
        sems = refs[len(arrays):len(arrays) + 2 * n_sets]
        token = refs[-1]
        x, y, c = _position()
        for s, (members, exchange) in enumerate(zip(sets, exchanges)):
            for slot, (k, rows) in enumerate(members):
                for j, peer in enumerate(exchange.peers(x, y, c)):
                    at = exchange.n_peers * slot + j
                    pltpu.make_async_remote_copy(
                        src_ref=_shard_rows(src[k], exchange.sent(x, y, c, peer), rows),
                        dst_ref=_shard_rows(land[k], exchange.lands_at(x, y, c), rows),
                        send_sem=sems[2 * s].at[at], recv_sem=sems[2 * s + 1].at[at],
                        device_id=peer, device_id_type=MESH_ID).start()
        token[...] = jnp.zeros_like(token)

    sem_shapes = []
    for members, exchange in zip(sets, exchanges):
        sem_shapes += [pltpu.SemaphoreType.DMA((exchange.n_peers * len(members),))] * 2
    out = pl.pallas_call(
        body, name=name,
        out_shape=sem_shapes + _hbm_like(arrays) + [TOKEN],
        in_specs=[ANY] * n_deps + [HBM] * len(arrays),
        out_specs=[SEM] * (2 * n_sets) + [HBM] * len(arrays) + [pl.BlockSpec(memory_space=pltpu.VMEM)],
        input_output_aliases={n_deps + i: 2 * n_sets + i for i in range(len(arrays))},
        compiler_params=pltpu.CompilerParams(has_side_effects=DATAFLOW),
    )(*deps, *[_in_hbm(a) for a in arrays])
    sems = [(out[2 * s], out[2 * s + 1]) for s in range(n_sets)]
    thru = list(out[2 * n_sets:2 * n_sets + len(arrays)])
    return sems, (thru[:first_land] if sources is not None else None), thru[first_land:], out[-1]


def _ici_copies_wait(sems, members, sources, landings, exchange, after, name):
    n = len(landings)
    arrays = (list(sources) if sources is not None else []) + list(landings)
    first_land = len(arrays) - n

    def body(*refs):
        src, land = refs[:n], refs[first_land:first_land + n]
        send_sems, recv_sems = refs[len(arrays)], refs[len(arrays) + 1]
        x, y, c = _position()
        for slot, rows in enumerate(members):
            for j, peer in enumerate(exchange.peers(x, y, c)):
                at = exchange.n_peers * slot + j
                cp = pltpu.make_async_remote_copy(
                    src_ref=_shard_rows(src[slot], exchange.sent(x, y, c, peer), rows),
                    dst_ref=_shard_rows(land[slot], exchange.arrives_at(peer), rows),
                    send_sem=send_sems.at[at], recv_sem=recv_sems.at[at], device_id=peer, device_id_type=MESH_ID)
                cp.wait_send()
                cp.wait_recv()

    out = pl.pallas_call(
        body, name=name, out_shape=_hbm_like(arrays),
        in_specs=[HBM] * len(arrays) + [SEM, SEM] + [ANY] * len(after), out_specs=[HBM] * len(arrays),
        input_output_aliases={i: i for i in range(len(arrays))},
        compiler_params=pltpu.CompilerParams(has_side_effects=DATAFLOW),
    )(*arrays, sems[0], sems[1], *after)
    return list(out[first_land:])


def _swap_copies(land, rows, which, send_sems, recv_sems):
    x, y, c = _position()
    chips = [([(x, y)] + _other_chips(x, y))[j] for j in which]
    sends, recvs = [], []
    for k in range(len(land)):
        for j, chip in enumerate(chips):
            for copies, core in ((sends, c), (recvs, 1 - c)):
                block = _shard_rows(land[k], _device_index(chip, core), rows[k])
                copies.append(pltpu.make_async_remote_copy(
                    src_ref=block, dst_ref=block, send_sem=send_sems.at[k, j], recv_sem=recv_sems.at[k, j],
                    device_id=(x, y, 1 - c), device_id_type=MESH_ID))
    return sends, recvs


def _d2d_gather(buffers, rows, name, which=(0, 1, 2, 3), deps=()):
    n = len(buffers)
    n_deps = len(deps)

    def body(*refs):
        sends, recvs = _swap_copies(refs[n_deps + n:n_deps + 2 * n], rows, which, *refs[n_deps + 2 * n:])
        for cp in sends:
            cp.start()
        for cp in recvs:
            cp.wait_recv()
        for cp in sends:
            cp.wait_send()

    return pl.pallas_call(
        body, name=name, out_shape=[jax.ShapeDtypeStruct(a.shape, a.dtype) for a in buffers],
        in_specs=[ANY] * (n_deps + n), out_specs=[ANY] * n, input_output_aliases={n_deps + i: i for i in range(n)},
        scratch_shapes=[pltpu.SemaphoreType.DMA((n, len(which))), pltpu.SemaphoreType.DMA((n, len(which)))],
    )(*deps, *buffers)


def _call_with_swap(body, args, swap, prefetch=(), **kw):
    buffers, rows, *chips = swap
    which = chips[0] if chips else (0, 1, 2, 3)
    n, n_pre, n_in, n_out = len(buffers), len(prefetch), len(args), len(kw["out_shape"])
    n_scratch = len(kw["scratch_shapes"])
    grid = kw["grid"]

    def at_step(last):
        hit = [pl.program_id(a) == (extent - 1 if last else 0) for a, extent in enumerate(grid)]
        return functools.reduce(jnp.logical_and, hit)

    def hosted(*refs):
        pre, ins, refs = refs[:n_pre], refs[n_pre:n_pre + n_in], refs[n_pre + n_in + n:]
        outs, land, scratch = refs[:n_out], refs[n_out:n_out + n], refs[n_out + n:n_out + n + n_scratch]
        sends, recvs = _swap_copies(land, rows, which, *refs[n_out + n + n_scratch:])

        @pl.when(at_step(False))
        def _():
            for cp in sends:
                cp.start()

        body(*pre, *ins, *outs, *scratch)

        @pl.when(at_step(True))
        def _():
            for cp in recvs:
                cp.wait_recv()
            for cp in sends:
                cp.wait_send()

    sem_shape = pltpu.SemaphoreType.DMA((n, len(which)))
    aliases = {**kw.get("input_output_aliases", {}), **{n_in + i: n_out + i for i in range(n)}}
    out = pl.pallas_call(
        hosted,
        grid_spec=pltpu.PrefetchScalarGridSpec(
            num_scalar_prefetch=n_pre, grid=grid, in_specs=kw["in_specs"] + [ANY] * n, out_specs=kw["out_specs"] + [ANY] * n,
            scratch_shapes=kw["scratch_shapes"] + [sem_shape, sem_shape]),
        out_shape=kw["out_shape"] + [jax.ShapeDtypeStruct(a.shape, a.dtype) for a in buffers],
        input_output_aliases={n_pre + i: o for i, o in aliases.items()},
        compiler_params=kw["compiler_params"], name=kw["name"])(*prefetch, *args, *buffers)
    return out[:n_out], out[n_out:]


def _pair_exchange_add(grad, place, name):
    r = grad.shape[0] // N_DEV
    n_chips = N_DEV // 2

    def body(place_ref, g_hbm, kept_ref, part_ref, land_ref, inbox, send_sems, recv_sems):
        q = pl.program_id(0)
        x, y, c = _position()
        copies = [pltpu.make_async_remote_copy(
            src_ref=_shard_rows(g_hbm, 2 * i + 1 - c, r), dst_ref=inbox.at[i], send_sem=send_sems.at[i],
            recv_sem=recv_sems.at[i], device_id=(x, y, 1 - c), device_id_type=MESH_ID) for i in range(n_chips)]

        @pl.when(q == 0)
        def _():
            for cp in copies:
                cp.start()

        for i, cp in enumerate(copies):
            @pl.when(q == i)
            def _(cp=cp):
                cp.wait_recv()

        total = (kept_ref[...].astype(F32) + inbox[q].astype(F32)).astype(BF)
        part_ref[...] = total

        @pl.when(q == place_ref[1])
        def _():
            land_ref[...] = total

        @pl.when(q == n_chips - 1)
        def _():
            for cp in copies:
                cp.wait_send()

    return pl.pallas_call(
        body,
        grid_spec=pltpu.PrefetchScalarGridSpec(
            num_scalar_prefetch=1, grid=(n_chips,),
            in_specs=[ANY, pl.BlockSpec((r, D), lambda q, p: (2 * q + p[0], 0))],
            out_specs=[pl.BlockSpec((r, D), lambda q, p: (q, 0)), pl.BlockSpec((r, D), lambda q, p: (p[1], 0))],
            scratch_shapes=[pltpu.VMEM((n_chips, r, D), BF), pltpu.SemaphoreType.DMA((n_chips,)),
                            pltpu.SemaphoreType.DMA((n_chips,))]),
        out_shape=[jax.ShapeDtypeStruct((n_chips * r, D), BF)] * 2,
        compiler_params=_params(1), name=name)(place, grad, grad)


def _wgrad_pair_sum(lhs, rhs, place, name, *, lhs_is_transposed, deps=()):
    t = rhs.shape[0]
    n = lhs.shape[0] if lhs_is_transposed else lhs.shape[1]
    r = n // N_DEV
    n_chips = N_DEV // 2
    per = 1 if (2 * r) % BLK == 0 else 2
    steps = n_chips // per
    n_deps = len(deps)
    in_vmem = r <= WGRAD_SUM_MAX_ROWS

    def body(place_ref, *refs):
        if in_vmem:
            l_ref, r_ref, part_ref, land_ref, res, inbox, send_sems, recv_sems = refs[n_deps:]
        else:
            l_ref, r_ref, part_ref, land_ref, inbox, res, staged, send_sems, recv_sems, stage_sem = refs[n_deps:]
        q = pl.program_id(0)
        slot = q % 2
        x, y, c = _position()

        def send(step, buf, i):
            return pltpu.make_async_remote_copy(
                src_ref=res.at[buf, pl.ds(pl.multiple_of((2 * i + 1 - c) * r, 16), r), :], dst_ref=inbox.at[step * per + i],
                send_sem=send_sems.at[buf, i], recv_sem=recv_sems.at[step * per + i],
                device_id=(x, y, 1 - c), device_id_type=MESH_ID)

        @pl.when(q < steps)
        def _():
            @pl.when(q >= 2)
            def _():
                for i in range(per):
                    send(q - 2, slot, i).wait_send()

            if lhs_is_transposed:
                res[slot] = _dot(l_ref[...], r_ref[...]).astype(BF)
            else:
                res[slot] = _dot_tn(l_ref[...], r_ref[...]).astype(BF)
            for i in range(per):
                send(q, slot, i).start()

        @pl.when(q >= 1)
        def _():
            for i in range(per):
                chip = (q - 1) * per + i
                send(q - 1, 1 - slot, i).wait_recv()
                kept = res[1 - slot, pl.ds(pl.multiple_of((2 * i + c) * r, 16), r), :]
                if in_vmem:
                    theirs = inbox[chip]
                else:
                    stage = pltpu.make_async_copy(inbox.at[chip], staged, stage_sem)
                    stage.start()
                    stage.wait()
                    theirs = staged[...]
                total = (kept.astype(F32) + theirs.astype(F32)).astype(BF)
                part_ref[i * r:(i + 1) * r, :] = total

                @pl.when(chip == place_ref[1])
                def _():
                    land_ref[...] = total

        @pl.when(q == steps)
        def _():
            for i in range(per):
                if steps > 1:
                    send(q - 2, slot, i).wait_send()
                send(q - 1, 1 - slot, i).wait_send()

    width = 2 * r * per
    last = steps - 1
    if lhs_is_transposed:
        lhs_spec = pl.BlockSpec((width, t), lambda q, p: (jnp.minimum(q, last), 0))
    else:
        lhs_spec = pl.BlockSpec((t, width), lambda q, p: (0, jnp.minimum(q, last)))
    sems = [pltpu.SemaphoreType.DMA((2, per)), pltpu.SemaphoreType.DMA((n_chips,))]
    inbox_shape = (n_chips, r, D)
    if in_vmem:
        extra_specs, extra_shapes = [], []
        scratch = [pltpu.VMEM((2, width, D), BF), pltpu.VMEM(inbox_shape, BF)] + sems
    else:
        extra_specs, extra_shapes = [ANY], [jax.ShapeDtypeStruct(inbox_shape, BF)]
        scratch = [pltpu.VMEM((2, width, D), BF), pltpu.VMEM((r, D), BF)] + sems + [pltpu.SemaphoreType.DMA(())]
    out = pl.pallas_call(
        body,
        grid_spec=pltpu.PrefetchScalarGridSpec(
            num_scalar_prefetch=1, grid=(steps + 1,),
            in_specs=[ANY] * n_deps + [lhs_spec, pl.BlockSpec((t, D), lambda q, p: (0, 0), pipeline_mode=pl.Buffered(1))],
            out_specs=[pl.BlockSpec((per * r, D), lambda q, p: (jnp.maximum(q - 1, 0), 0)),
                       pl.BlockSpec((r, D), lambda q, p: (p[1], 0))] + extra_specs,
            scratch_shapes=scratch),
        out_shape=[jax.ShapeDtypeStruct((n // 2, D), BF)] * 2 + extra_shapes,
        compiler_params=_params(1), name=name)(place, *deps, lhs, rhs)
    return out[:2]


def _wgrad_pair_sum_many(items, place, name, deps=()):
    m = len(items)
    t = items[0][1].shape[0]
    n = items[0][0].shape[0] if items[0][2] else items[0][0].shape[1]
    r = n // N_DEV
    assert (2 * r) % BLK == 0 and r <= WGRAD_SUM_MAX_ROWS
    steps = N_DEV // 2
    chunks = m * steps
    n_deps = len(deps)

    def body(place_ref, *refs):
        refs = refs[n_deps:]
        l_refs, r_first, r_later = refs[:m], refs[m], refs[m + 1:2 * m]
        parts, lands = refs[2 * m:3 * m], refs[3 * m:4 * m]
        res, inbox, r_scr, send_sems, recv_sems, fetch_sems = refs[4 * m:]
        g = pl.program_id(0)
        slot = g % 2
        x, y, c = _position()

        def send(chunk, buf):
            return pltpu.make_async_remote_copy(
                src_ref=res.at[buf, pl.ds(pl.multiple_of((1 - c) * r, 16), r), :], dst_ref=inbox.at[chunk],
                send_sem=send_sems.at[buf], recv_sem=recv_sems.at[chunk], device_id=(x, y, 1 - c), device_id_type=MESH_ID)

        def fetch(k):
            return pltpu.make_async_copy(r_later[k - 1], r_scr.at[k - 1], fetch_sems.at[k - 1])

        @pl.when(g == 0)
        def _():
            for k in range(1, m):
                fetch(k).start()

        @pl.when(g < chunks)
        def _():
            @pl.when(g >= 2)
            def _():
                send(g - 2, slot).wait_send()

            for k, (_, _, transposed) in enumerate(items):
                @pl.when(g // steps == k)
                def _(k=k, transposed=transposed):
                    if k > 0:
                        @pl.when(g == k * steps)
                        def _():
                            fetch(k).wait()
                    rhs = r_first[...] if k == 0 else r_scr[k - 1]
                    res[slot] = (_dot(l_refs[k][...], rhs) if transposed else _dot_tn(l_refs[k][...], rhs)).astype(BF)

            send(g, slot).start()

        @pl.when(g >= 1)
        def _():
            chunk = g - 1
            send(chunk, 1 - slot).wait_recv()
            kept = res[1 - slot, pl.ds(pl.multiple_of(c * r, 16), r), :]
            total = (kept.astype(F32) + inbox[chunk].astype(F32)).astype(BF)
            for k in range(m):
                @pl.when(chunk // steps == k)
                def _(k=k):
                    parts[k][...] = total

                    @pl.when(chunk % steps == place_ref[1])
                    def _():
                        lands[k][...] = total

        @pl.when(g == chunks)
        def _():
            send(g - 2, slot).wait_send()
            send(g - 1, 1 - slot).wait_send()

    def own_steps(k):
        return lambda g: jnp.clip(g - k * steps, 0, steps - 1)

    lhs_specs = []
    for k, (lhs, _, transposed) in enumerate(items):
        at = own_steps(k)
        lhs_specs.append(pl.BlockSpec((2 * r, t), lambda g, p, at=at: (at(g), 0)) if transposed
                         else pl.BlockSpec((t, 2 * r), lambda g, p, at=at: (0, at(g))))
    out = pl.pallas_call(
        body,
        grid_spec=pltpu.PrefetchScalarGridSpec(
            num_scalar_prefetch=1, grid=(chunks + 1,),
            in_specs=[ANY] * n_deps + lhs_specs
            + [pl.BlockSpec((t, D), lambda g, p: (0, 0), pipeline_mode=pl.Buffered(1))] + [ANY] * (m - 1),
            out_specs=[pl.BlockSpec((r, D), lambda g, p, at=own_steps(k): (at(g - 1), 0)) for k in range(m)]
            + [pl.BlockSpec((r, D), lambda g, p: (p[1], 0))] * m,
            scratch_shapes=[pltpu.VMEM((2, 2 * r, D), BF), pltpu.VMEM((chunks, r, D), BF), pltpu.VMEM((m - 1, t, D), BF),
                            pltpu.SemaphoreType.DMA((2,)), pltpu.SemaphoreType.DMA((chunks,)),
                            pltpu.SemaphoreType.DMA((m - 1,))]),
        out_shape=[jax.ShapeDtypeStruct((n // 2, D), BF)] * (2 * m),
        compiler_params=_params(1), name=name)(place, *deps, *[i[0] for i in items], *[i[1] for i in items])
    return [(out[k], out[m + k]) for k in range(m)]


def _sum_blocks(gathered, rows):
    def body(b_ref, o_ref):
        acc = b_ref[0:rows, :]
        for d in range(1, N_DEV):
            acc = acc + b_ref[d * rows:(d + 1) * rows, :]
        o_ref[...] = acc

    return pl.pallas_call(body, out_shape=jax.ShapeDtypeStruct((rows, D), F32), name="small_sum")(gathered)


def _adamw_math(w, g, m, v):
    m = ADAM_B1 * m + (1.0 - ADAM_B1) * g
    v = ADAM_B2 * v + (1.0 - ADAM_B2) * (g * g)
    m_hat = m / (1.0 - ADAM_B1 ** ADAM_STEP)
    v_hat = v / (1.0 - ADAM_B2 ** ADAM_STEP)
    delta = -ADAM_LR * (m_hat / (jnp.sqrt(v_hat) + ADAM_EPS) + ADAM_WD * w)
    return delta, m, v


def _sum_partials(blocks):
    g = blocks[0].astype(F32)
    for blk in blocks[1:]:
        g = g + blk.astype(F32)
    return g


ADAMW_MAX_ROWS = 352


def _reduce_adamw(items, name):
    n = len(items)
    per = -(-max(w.shape[0] for _, w, _, _ in items) // ADAMW_MAX_ROWS)

    def body(*refs):
        for k in range(n):
            r0, r1, r2, r3, w_ref, m_ref, v_ref = refs[7 * k:7 * k + 7]
            g_ref, d_ref, nm_ref, nv_ref = refs[7 * n + 4 * k:7 * n + 4 * k + 4]
            g = _sum_partials([r0[...], r1[...], r2[...], r3[...]])
            g_ref[...] = g
            d_ref[...], nm_ref[...], nv_ref[...] = _adamw_math(w_ref[...], g, m_ref[...], v_ref[...])

    in_specs, out_specs, out_shape, args = [], [], [], []
    for landed, w, m, v in items:
        tr = w.shape[0] // per
        assert tr * per == w.shape[0] and tr % 16 == 0
        tile = _row_tile(tr, D)
        in_specs += [pl.BlockSpec((tr, D), lambda i, q=q: (q * per + i, 0)) for q in range(4)] + [tile] * 3
        out_specs += [tile] * 4
        out_shape += [jax.ShapeDtypeStruct(w.shape, F32)] * 4
        args += [landed] * 4 + [w, m, v]
    out = pl.pallas_call(body, grid=(per,), in_specs=in_specs, out_specs=out_specs, out_shape=out_shape,
                         compiler_params=_params(1), name=name)(*args)
    return [out[4 * k:4 * k + 4] for k in range(n)]


def _adamw_small(w, g, m, v, name):
    def body(w_ref, g_ref, m_ref, v_ref, d_ref, nm_ref, nv_ref):
        d_ref[...], nm_ref[...], nv_ref[...] = _adamw_math(w_ref[...], g_ref[...], m_ref[...], v_ref[...])

    return pl.pallas_call(body, out_shape=[jax.ShapeDtypeStruct(w.shape, F32)] * 3, name=name)(w, g, m, v)


WEIGHTS = ("ffn1_norm", "ffn1_w_in", "ffn1_w_out", "mix_norm", "w_in", "conv_dw_kernel", "conv_dw_bias", "conv_ln_g",
           "conv_ln_b", "conv_w_proj", "q_norm", "k_norm", "attn_sinks", "rel_bias", "attn_w_o", "w_out", "ffn2_norm",
           "ffn2_w_in", "ffn2_w_out")
MATRICES = ("ffn1_w_in", "ffn1_w_out", "w_in", "conv_w_proj", "attn_w_o", "w_out", "ffn2_w_in", "ffn2_w_out")
COLUMN_SHARDED = ("ffn1_w_in", "w_in", "ffn2_w_in")
ROW_VECTORS = ("ffn1_norm", "mix_norm", "conv_dw_bias", "conv_ln_g", "conv_ln_b", "ffn2_norm")
PACKED = (("q_norm", HD), ("k_norm", HD), ("attn_sinks", NQ), ("rel_bias", NBUCKET * NQ))
GATHER = _Exchange(gather=True)
GATHER_ALL = _Exchange(gather=True, all_cores=True)
SCATTER = _Exchange(gather=False)
FIRST = "ffn1_w_in"
GATHER_STAGES = ("ffn1_out", "mix_proj", "mix_merge", "ffn2")
STAGE_GATHER = {"ffn1_out": GATHER, "mix_proj": GATHER, "mix_merge": GATHER, "ffn2": GATHER_ALL}
STAGE_MEMBERS = {"ffn1_out": ("ffn1_w_out",),
                 "mix_proj": ("w_in", "taps"), "mix_merge": ("conv_w_proj", "attn_w_o", "w_out"),
                 "ffn2": ("ffn2_w_in", "ffn2_w_out")}
ROW_PACKED = len(ROW_VECTORS)
ROW_LOSS = ROW_PACKED + 1
ROW_TAPS = 8
PAYLOAD_ROWS = 48


def _pack_small(values, last_row):
    packed = jnp.concatenate([values[k].reshape(-1) for k, _ in PACKED])
    packed = jnp.pad(packed, (0, D - packed.shape[0])).reshape(1, D)
    return jnp.concatenate([values[k].reshape(1, D) for k in ROW_VECTORS] + [packed, last_row], axis=0)


def _unpack_small(rows):
    out = {k: rows[i] for i, k in enumerate(ROW_VECTORS)}
    at = 0
    for k, size in PACKED:
        out[k] = rows[ROW_PACKED, at:at + size]
        at += size
    out["rel_bias"] = out["rel_bias"].reshape(NBUCKET, NQ)
    return out


def kernel(x, ffn1_norm, ffn1_w_in, ffn1_w_out, mix_norm, w_in, conv_dw_kernel, conv_dw_bias, conv_ln_g, conv_ln_b, conv_w_proj, q_norm, k_norm, attn_sinks, rel_bias, attn_w_o, w_out, ffn2_norm, ffn2_w_in, ffn2_w_out, loss_target, m_ffn1_norm, m_ffn1_w_in, m_ffn1_w_out, m_mix_norm, m_w_in, m_conv_dw_kernel, m_conv_dw_bias, m_conv_ln_g, m_conv_ln_b, m_conv_w_proj, m_q_norm, m_k_norm, m_attn_sinks, m_rel_bias, m_attn_w_o, m_w_out, m_ffn2_norm, m_ffn2_w_in, m_ffn2_w_out, v_ffn1_norm, v_ffn1_w_in, v_ffn1_w_out, v_mix_norm, v_w_in, v_conv_dw_kernel, v_conv_dw_bias, v_conv_ln_g, v_conv_ln_b, v_conv_w_proj, v_q_norm, v_k_norm, v_attn_sinks, v_rel_bias, v_attn_w_o, v_w_out, v_ffn2_norm, v_ffn2_w_in, v_ffn2_w_out):
    w = dict(ffn1_norm=ffn1_norm, ffn1_w_in=ffn1_w_in, ffn1_w_out=ffn1_w_out, mix_norm=mix_norm, w_in=w_in,
             conv_dw_kernel=conv_dw_kernel, conv_dw_bias=conv_dw_bias, conv_ln_g=conv_ln_g, conv_ln_b=conv_ln_b,
             conv_w_proj=conv_w_proj, q_norm=q_norm, k_norm=k_norm, attn_sinks=attn_sinks, rel_bias=rel_bias,
             attn_w_o=attn_w_o, w_out=w_out, ffn2_norm=ffn2_norm, ffn2_w_in=ffn2_w_in, ffn2_w_out=ffn2_w_out)
    m = dict(ffn1_norm=m_ffn1_norm, ffn1_w_in=m_ffn1_w_in, ffn1_w_out=m_ffn1_w_out, mix_norm=m_mix_norm, w_in=m_w_in,
             conv_dw_kernel=m_conv_dw_kernel, conv_dw_bias=m_conv_dw_bias, conv_ln_g=m_conv_ln_g, conv_ln_b=m_conv_ln_b,
             conv_w_proj=m_conv_w_proj, q_norm=m_q_norm, k_norm=m_k_norm, attn_sinks=m_attn_sinks, rel_bias=m_rel_bias,
             attn_w_o=m_attn_w_o, w_out=m_w_out, ffn2_norm=m_ffn2_norm, ffn2_w_in=m_ffn2_w_in, ffn2_w_out=m_ffn2_w_out)
    v = dict(ffn1_norm=v_ffn1_norm, ffn1_w_in=v_ffn1_w_in, ffn1_w_out=v_ffn1_w_out, mix_norm=v_mix_norm, w_in=v_w_in,
             conv_dw_kernel=v_conv_dw_kernel, conv_dw_bias=v_conv_dw_bias, conv_ln_g=v_conv_ln_g, conv_ln_b=v_conv_ln_b,
             conv_w_proj=v_conv_w_proj, q_norm=v_q_norm, k_norm=v_k_norm, attn_sinks=v_attn_sinks, rel_bias=v_rel_bias,
             attn_w_o=v_attn_w_o, w_out=v_w_out, ffn2_norm=v_ffn2_norm, ffn2_w_in=v_ffn2_w_in, ffn2_w_out=v_ffn2_w_out)
    px, py, pc = _position()
    me = 4 * px + 2 * py + pc
    place = jnp.stack([pc, 2 * px + py]).astype(jnp.int32)

    rows_of = lambda k, a: a.T if k in COLUMN_SHARDED else a
    me1 = me.astype(jnp.int32).reshape(1)
    rest = tuple(k for k in MATRICES if k != FIRST)
    shard_rows = dict({k: rows_of(k, w[k]).shape[0] for k in MATRICES}, taps=CWP)
    sems_first, _, thru_first, token = _ici_copies_start(
        [[(0, shard_rows[FIRST])]], None, _prep([rows_of(FIRST, w[FIRST])], None, me1, "prep_first"), [GATHER],
        "gather_start_first")
    prepped, mine = _prep([rows_of(k, w[k]) for k in rest], conv_dw_kernel, me1, "prep", deps=[token],
                          swap=(thru_first, [shard_rows[FIRST]], (0,)))
    buffers = dict(zip(rest + ("taps",), prepped))
    landings, sets = [], []
    for stage in GATHER_STAGES:
        sets.append([(len(landings) + i, shard_rows[k]) for i, k in enumerate(STAGE_MEMBERS[stage])])
        landings += list(STAGE_MEMBERS[stage])
    sems, _, land_thru, started = _ici_copies_start(sets, None, [buffers[k] for k in landings],
                                                    [STAGE_GATHER[s] for s in GATHER_STAGES], "gather_start")

    packed = [_pack_small(a, jnp.zeros((1, D), F32)) for a in (w, m, v)]

    def ffn1_up(x, g, after):
        chips = jnp.stack([_chip_index(chip) for chip in [(px, py)] + _other_chips(px, py)]).astype(jnp.int32)
        rows = [shard_rows[FIRST]]
        n, u = _ffn_up_blocks(x, g, None, mine[0], chips[:1], None, "ffn1_up_mine", deps=[started])
        landed = _ici_copies_wait(sems_first[0], rows, None, mine, GATHER, [u, *after, *packed], "gather_wait_first")
        w_in_t, = _d2d_gather(landed, rows, "gather_d2d_first", which=(1, 2, 3))
        n, u = _ffn_up_blocks(None, None, n, w_in_t, chips[1:3], u, "ffn1_up_next")
        (n, u), w1 = weights_of("ffn1_out", (u,), during=functools.partial(
            _ffn_up_blocks, None, None, n, w_in_t, chips[3:], u, "ffn1_up"))
        return n, u, dict(w1, ffn1_w_in=w_in_t)

    def weights_of(stage, after, during=None):
        s = GATHER_STAGES.index(stage)
        rows = [r for _, r in sets[s]]
        landed = _ici_copies_wait(sems[s], rows, None, [land_thru[k] for k, _ in sets[s]], STAGE_GATHER[stage],
                                  list(after), "gather_wait_" + stage)
        if during is not None:
            results, landed = during(swap=(landed, rows))
        elif not STAGE_GATHER[stage].all_cores:
            landed = _d2d_gather(landed, rows, "gather_d2d_" + stage)
        out = dict(zip(STAGE_MEMBERS[stage], landed))
        if "taps" in out:
            taps = out.pop("taps")
            out["conv_dw_kernel"] = jnp.transpose(taps.reshape(N_DEV, CWP, BLK), (1, 0, 2)).reshape(CWP, D)[:CW]
        return out if during is None else (results, out)

    in_flight = []

    def wgrad(lhs, rhs, name, lhs_is_transposed, deps=()):
        return ("summed",) + tuple(_wgrad_pair_sum(lhs, rhs, place, name, lhs_is_transposed=lhs_is_transposed, deps=deps))

    def wgrads(items, name):
        return [("summed",) + pair for pair in _wgrad_pair_sum_many(items, place, name)]

    def grads_done(stage, grads):
        names = list(grads)
        added = []
        for k in names:
            if not isinstance(grads[k], tuple):
                added.append(_pair_exchange_add(grads[k], place, "pair_add_" + k))
            else:
                added.append(grads[k][1:])
        partials = [p for p, _ in added]
        members = [(i, p.shape[0] // 4) for i, p in enumerate(partials)]
        sem, p_thru, l_thru, token = _ici_copies_start([members], partials, [l for _, l in added], [SCATTER],
                                                       "scatter_start_" + stage)
        in_flight.append((stage, names, sem[0], p_thru, l_thru, token))
        return [token]

    small = []

    def small_done(gv, sq):
        payload = jnp.concatenate([_pack_small(gv, sq), jnp.pad(gv["conv_dw_kernel"], ((0, PAYLOAD_ROWS - ROW_TAPS - CW), (0, 0)))],
                                  axis=0)
        mine = lax.dynamic_update_slice_in_dim(lax.empty((N_DEV * PAYLOAD_ROWS, D), F32), payload, me * PAYLOAD_ROWS, axis=0)
        sems, _, thru, token = _ici_copies_start([[(0, PAYLOAD_ROWS)]], None, [mine], [GATHER_ALL], "small_start")
        small.append((sems[0], thru))
        return [token]

    vec = {k: w[k] for k in WEIGHTS if k not in MATRICES and k != "conv_dw_kernel"}
    dx0 = _local_step(x[0], loss_target[0], vec, ffn1_up, weights_of, wgrad, wgrads, grads_done, small_done)
    gathered, = _ici_copies_wait(small[0][0], [PAYLOAD_ROWS], None, small[0][1], GATHER_ALL, [in_flight[-1][-1]], "small_wait")
    total = _sum_blocks(gathered, PAYLOAD_ROWS)
    loss = (0.5 / D) * jnp.sum(total[ROW_LOSS])

    grads, delta, new_m, new_v = {}, {}, {}, {}
    after, pending = [total], []
    for stage, names, sem, p_thru, l_thru, _ in in_flight:
        landed = _ici_copies_wait(sem, [p.shape[0] // 4 for p in p_thru], p_thru, l_thru, SCATTER, after,
                                  "scatter_wait_" + stage)
        pending += zip(names, landed)
        after = list(landed)
        if stage == in_flight[-2][0]:
            continue
        outs = _reduce_adamw([(buf, rows_of(k, w[k]), rows_of(k, m[k]), rows_of(k, v[k])) for k, buf in pending],
                             "adamw_" + stage)
        for (k, _), out in zip(pending, outs):
            grads[k], delta[k], new_m[k], new_v[k] = [rows_of(k, a) for a in out]
        after, pending = [out[1] for out in outs], []
    d8, m8, v8 = _adamw_small(packed[0], total[:ROW_TAPS], packed[1], packed[2], "adamw_small")
    grads.update(_unpack_small(total[:ROW_TAPS]))
    delta.update(_unpack_small(d8))
    new_m.update(_unpack_small(m8))
    new_v.update(_unpack_small(v8))
    k = "conv_dw_kernel"
    grads[k] = lax.dynamic_slice_in_dim(total[ROW_TAPS:ROW_TAPS + CW], me * BLK, BLK, axis=1)
    delta[k], new_m[k], new_v[k] = _adamw_small(w[k], grads[k], m[k], v[k], "adamw_taps")

    return (loss, dx0[None], *[grads[k] for k in WEIGHTS], *[delta[k] for k in WEIGHTS],
            *[new_m[k] for k in WEIGHTS], *[new_v[k] for k in WEIGHTS])
```

```python
import functools
import math

import numpy as np
import jax
import jax.numpy as jnp
from jax import lax
from jax.experimental import pallas as pl
from jax.experimental.pallas import tpu as pltpu

F32 = jnp.float32
BF = jnp.bfloat16

D = 1024
F = 2816
INW = 5632
CW = 31
CWP = 32
HD = 64
NQ = 16
NKV = 4
GRP = NQ // NKV
BLK = 128
NBUCKET = 32
EPS = 1e-6
NEG = float(jnp.finfo(jnp.float32).min)
QK_SCALE = 1.0 / math.sqrt(HD)
R_CONV = (0, 2048)
R_QKV = (2048, 3584)
R_Q = (2048, 3072)
R_KV = (3072, 3584)
R_GATE = (3584, 5632)

N_DEV = 8
VMEM_LIMIT_V7X = 56 * 1024 * 1024
ROW_TILE = 256
ROW_TILE_WIDE = 512
ROW_TILE_BLOCK = 1024
WGRAD_SUM_MAX_ROWS = 352

ADAM_LR = 0.001
ADAM_B1 = 0.9
ADAM_B2 = 0.999
ADAM_EPS = 1e-08
ADAM_WD = 0.01
ADAM_STEP = 10

NT_DIMS = (((1,), (1,)), ((), ()))
TN_DIMS = (((0,), (0,)), ((), ()))


def _dot(a, b):
    return jnp.dot(a, b, preferred_element_type=F32)


def _dot_nt(a, b):
    return lax.dot_general(a, b, NT_DIMS, preferred_element_type=F32)


def _dot_tn(a, b):
    return lax.dot_general(a, b, TN_DIMS, preferred_element_type=F32)


def _sig(x):
    return 0.5 * jnp.tanh(0.5 * x) + 0.5


ANY = pl.BlockSpec(memory_space=pl.ANY)


def _call(body, deps, args, **kw):
    n = len(deps)
    if n:
        kw["in_specs"] = [ANY] * n + list(kw["in_specs"])
        return pl.pallas_call(lambda *refs: body(*refs[n:]), **kw)(*deps, *args)
    return pl.pallas_call(body, **kw)(*args)


def _params(n_axes):
    return pltpu.CompilerParams(dimension_semantics=("arbitrary",) * n_axes, vmem_limit_bytes=VMEM_LIMIT_V7X)


def _resident(shape):
    zeros = (0,) * len(shape)
    return pl.BlockSpec(shape, lambda *_: zeros, pipeline_mode=pl.Buffered(1))


def _row_tile(rows, cols):
    return pl.BlockSpec((rows, cols), lambda i: (i, 0))


def _rms_stats(x):
    r = lax.rsqrt(jnp.mean(x * x, axis=-1, keepdims=True) + EPS)
    return r, x * r


def _rms_bwd(dn, x, g):
    r, xh = _rms_stats(x)
    dxh = dn * g
    dx = r * (dxh - xh * jnp.mean(dxh * xh, axis=-1, keepdims=True))
    return dx, jnp.sum(dn * xh, axis=0, keepdims=True)


def _ffn_last(x, target, g, w_in_t, w_out, name):
    t = x.shape[0]
    tm = min(ROW_TILE, t)

    def body(x_ref, t_ref, g_ref, w_ref, wo_ref, n_ref, du_ref, h_ref, dy_ref, dx_ref, sq_ref, dg_ref):
        @pl.when(pl.program_id(0) == 0)
        def _():
            sq_ref[...] = jnp.zeros_like(sq_ref)
            dg_ref[...] = jnp.zeros_like(dg_ref)

        x = x_ref[...]
        g = g_ref[...]
        r, xh = _rms_stats(x)
        n = (xh * g).astype(BF)
        n_ref[...] = n
        u = _dot_nt(n, w_ref[...])
        a = u[:, :F]
        b = u[:, F:]
        s = _sig(a)
        sa = a * s
        h = (sa * b).astype(BF)
        h_ref[...] = h
        err = x + 0.5 * _dot(h, wo_ref[...]) - t_ref[...]
        sq_ref[...] += jnp.sum(err * err, axis=0, keepdims=True)
        dxo = err * (1.0 / D)
        dy = (0.5 * dxo).astype(BF)
        dy_ref[...] = dy
        dh = _dot_nt(dy, wo_ref[...])
        du_ref[:, :F] = (dh * b * (s * (1.0 + a * (1.0 - s)))).astype(BF)
        du_ref[:, F:] = (dh * sa).astype(BF)
        dn = _dot(du_ref[...], w_ref[...])
        dxh = dn * g
        dx_ref[...] = dxo + r * (dxh - xh * jnp.mean(dxh * xh, axis=-1, keepdims=True))
        dg_ref[...] += jnp.sum(dn * xh, axis=0, keepdims=True)

    vec = pl.BlockSpec((1, D), lambda i: (0, 0))
    return pl.pallas_call(
        body, grid=(t // tm,),
        in_specs=[_row_tile(tm, D), _row_tile(tm, D), _resident((1, D)), _resident((INW, D)), _resident((F, D))],
        out_specs=[_row_tile(tm, D), _row_tile(tm, INW), _row_tile(tm, F), _row_tile(tm, D), _row_tile(tm, D), vec, vec],
        out_shape=[jax.ShapeDtypeStruct((t, D), BF), jax.ShapeDtypeStruct((t, INW), BF), jax.ShapeDtypeStruct((t, F), BF),
                   jax.ShapeDtypeStruct((t, D), BF), jax.ShapeDtypeStruct((t, D), F32), jax.ShapeDtypeStruct((1, D), F32),
                   jax.ShapeDtypeStruct((1, D), F32)],
        compiler_params=_params(1), name=name)(x, target, g, w_in_t, w_out)


def _ffn_up_blocks(x, g, n, w_in_t, order, u, name, deps=(), swap=None):
    t = (x if n is None else n).shape[0]
    tm = min(ROW_TILE_BLOCK, t)
    c = INW * 2 // N_DEV
    n_deps = len(deps)
    first = n is None
    assert not first or order.shape == (1,)

    def body(order_ref, *refs):
        refs = refs[n_deps:]
        if first:
            x_ref, g_ref, w_ref, n_ref, u_ref = refs
            nt = (_rms_stats(x_ref[...])[1] * g_ref[...]).astype(BF)
            n_ref[...] = nt
        else:
            n_ref, w_ref, _, u_ref = refs
            nt = n_ref[...]
        u_ref[...] = _dot_nt(nt, w_ref[...]).astype(BF)

    rows = pl.BlockSpec((tm, D), lambda k, i, o: (i, 0))
    block = pl.BlockSpec((c, D), lambda k, i, o: (o[k], 0))
    cols = pl.BlockSpec((tm, c), lambda k, i, o: (i, o[k]))
    u_shape = jax.ShapeDtypeStruct((t, INW), BF)
    if first:
        args, in_specs = (x, g, w_in_t), [rows, _resident((1, D)), block]
        out_specs, out_shape, aliases = [rows, cols], [jax.ShapeDtypeStruct((t, D), BF), u_shape], {}
    else:
        args, in_specs = (n, w_in_t, u), [rows, block, ANY]
        out_specs, out_shape, aliases = [cols], [u_shape], {1 + n_deps + 2: 0}
    grid = (order.shape[0], t // tm)
    if swap is not None:
        (out,), swapped = _call_with_swap(
            body, (*deps, *args), swap, prefetch=(order,), grid=grid, in_specs=[ANY] * n_deps + in_specs, out_specs=out_specs,
            out_shape=out_shape, scratch_shapes=[], input_output_aliases={n_deps + 2: 0}, compiler_params=_params(2), name=name)
        return (n, out), swapped
    out = pl.pallas_call(
        body,
        grid_spec=pltpu.PrefetchScalarGridSpec(num_scalar_prefetch=1, grid=grid, in_specs=[ANY] * n_deps + in_specs,
                                               out_specs=out_specs),
        out_shape=out_shape, input_output_aliases=aliases, compiler_params=_params(2), name=name)(order, *deps, *args)
    return tuple(out) if first else (n, out[0])


def _ffn_down(x, u, w_out, name, part=(0, 1), out=None, swap=None):
    t = x.shape[0]
    tm = min(ROW_TILE_WIDE, t)
    steps = t // tm // part[1]
    first = part[0] * steps
    others = [out] if out is not None else []

    def body(x_ref, u_ref, wo_ref, *rest):
        a = u_ref[:, :F].astype(F32)
        b = u_ref[:, F:].astype(F32)
        h = (a * _sig(a) * b).astype(BF)
        rest[-1][...] = x_ref[...] + 0.5 * _dot(h, wo_ref[...])

    tile = lambda cols: pl.BlockSpec((tm, cols), lambda i: (first + i, 0))
    kw = dict(grid=(steps,), in_specs=[tile(D), tile(INW), _resident((F, D))] + [ANY] * len(others), out_specs=[tile(D)],
              out_shape=[jax.ShapeDtypeStruct((t, D), F32)], scratch_shapes=[],
              input_output_aliases={3: 0} if others else {}, compiler_params=_params(1), name=name)
    args = (x, u, w_out, *others)
    return pl.pallas_call(body, **kw)(*args) if swap is None else _call_with_swap(body, args, swap, **kw)


def _ffn_bwd(dxo, x, g, u, w_in_t, w_out, name, deps=()):
    t = x.shape[0]
    tm = min(ROW_TILE, t)
    n_tiles = t // tm
    slots = 3

    def body(dxo_ref, x_ref, g_ref, u_hbm, w_ref, wo_ref, dx_ref, du_ref, h_ref, dy_ref, dg_ref, u_buf, u_sems):
        i = pl.program_id(0)

        def fetch(tile):
            first = tile * tm if isinstance(tile, int) else pl.multiple_of(tile * tm, tm)
            return pltpu.make_async_copy(u_hbm.at[pl.ds(first, tm), :], u_buf.at[tile % slots], u_sems.at[tile % slots])

        @pl.when(i == 0)
        def _():
            for tile in range(min(slots - 1, n_tiles)):
                fetch(tile).start()

        @pl.when(i + slots - 1 < n_tiles)
        def _():
            fetch(i + slots - 1).start()

        fetch(i).wait()
        u_ref = u_buf.at[i % slots]
        dxo = dxo_ref[...]
        dy = (0.5 * dxo).astype(BF)
        dy_ref[...] = dy
        dh = _dot_nt(dy, wo_ref[...])
        a = u_ref[:, :F].astype(F32)
        b = u_ref[:, F:].astype(F32)
        s = _sig(a)
        sa = a * s
        h_ref[...] = (sa * b).astype(BF)
        du_ref[:, :F] = (dh * b * (s * (1.0 + a * (1.0 - s)))).astype(BF)
        du_ref[:, F:] = (dh * sa).astype(BF)
        dn = _dot(du_ref[...], w_ref[...])
        dx, dg = _rms_bwd(dn, x_ref[...], g_ref[...])
        dx_ref[...] = dxo + dx

        @pl.when(pl.program_id(0) == 0)
        def _():
            dg_ref[...] = jnp.zeros_like(dg_ref)

        dg_ref[...] += dg

    return _call(
        body, deps, (dxo, x, g, u, w_in_t, w_out), grid=(t // tm,),
        in_specs=[_row_tile(tm, D), _row_tile(tm, D), _resident((1, D)), ANY, _resident((INW, D)), _resident((F, D))],
        out_specs=[_row_tile(tm, D), _row_tile(tm, INW), _row_tile(tm, F), _row_tile(tm, D),
                   pl.BlockSpec((1, D), lambda i: (0, 0))],
        out_shape=[jax.ShapeDtypeStruct((t, D), F32), jax.ShapeDtypeStruct((t, INW), BF), jax.ShapeDtypeStruct((t, F), BF),
                   jax.ShapeDtypeStruct((t, D), BF), jax.ShapeDtypeStruct((1, D), F32)],
        scratch_shapes=[pltpu.VMEM((slots, tm, INW), BF), pltpu.SemaphoreType.DMA((slots,))],
        compiler_params=_params(1), name=name)


def _wgrad(lhs, rhs, name, *, lhs_is_transposed, chunk, deps=()):
    t = rhs.shape[0]
    n = lhs.shape[0] if lhs_is_transposed else lhs.shape[1]
    c = min(chunk, n)

    def body(l_ref, r_ref, o_ref):
        if lhs_is_transposed:
            o_ref[...] = _dot(l_ref[...], r_ref[...]).astype(BF)
        else:
            o_ref[...] = _dot_tn(l_ref[...], r_ref[...]).astype(BF)

    lhs_spec = pl.BlockSpec((c, t), lambda j: (j, 0)) if lhs_is_transposed else pl.BlockSpec((t, c), lambda j: (0, j))
    return _call(
        body, deps, (lhs, rhs), grid=(n // c,),
        in_specs=[lhs_spec, _resident((t, D))],
        out_specs=pl.BlockSpec((c, D), lambda j: (j, 0)),
        out_shape=jax.ShapeDtypeStruct((n, D), BF),
        compiler_params=_params(1), name=name)


def _wgrad_mix(duc, dq_t, dkv_t, dgp, hm):
    t = hm.shape[0]
    c = 512
    first_q, first_kv, first_gate = R_Q[0] // c, R_KV[0] // c, R_GATE[0] // c

    def body(uc_ref, q_ref, kv_ref, gp_ref, h_ref, o_ref):
        j = pl.program_id(0)

        @pl.when(j < first_q)
        def _():
            o_ref[...] = _dot_tn(uc_ref[...], h_ref[...]).astype(BF)

        @pl.when((j >= first_q) & (j < first_kv))
        def _():
            o_ref[...] = _dot(q_ref[...], h_ref[...]).astype(BF)

        @pl.when((j >= first_kv) & (j < first_gate))
        def _():
            o_ref[...] = _dot(kv_ref[...], h_ref[...]).astype(BF)

        @pl.when(j >= first_gate)
        def _():
            o_ref[...] = _dot_tn(gp_ref[...], h_ref[...]).astype(BF)

    return pl.pallas_call(
        body, grid=(INW // c,),
        in_specs=[pl.BlockSpec((t, c), lambda j: (0, jnp.clip(j, 0, first_q - 1))),
                  pl.BlockSpec((c, t), lambda j: (jnp.clip(j - first_q, 0, first_kv - first_q - 1), 0)),
                  pl.BlockSpec((c, t), lambda j: (jnp.clip(j - first_kv, 0, first_gate - first_kv - 1), 0)),
                  pl.BlockSpec((t, c), lambda j: (0, jnp.clip(j - first_gate, 0, INW // c - first_gate - 1))),
                  _resident((t, D))],
        out_specs=pl.BlockSpec((c, D), lambda j: (j, 0)),
        out_shape=jax.ShapeDtypeStruct((INW, D), BF),
        compiler_params=_params(1), name="mix_dw_in")(duc, dq_t, dkv_t, dgp, hm)


def _mix_proj(x, g, w_t):
    t = x.shape[0]
    tm = min(ROW_TILE_WIDE, t)

    def body(x_ref, g_ref, w_ref, hm_ref, uc_ref, gp_ref, qkv_ref):
        r, xh = _rms_stats(x_ref[...])
        hm = (xh * g_ref[...]).astype(BF)
        hm_ref[...] = hm
        uc_ref[...] = _dot_nt(hm, w_ref[R_CONV[0]:R_CONV[1], :]).astype(BF)
        gp_ref[...] = _dot_nt(hm, w_ref[R_GATE[0]:R_GATE[1], :]).astype(BF)
        qkv_ref[...] = _dot_nt(w_ref[R_QKV[0]:R_QKV[1], :], hm).astype(BF)

    return pl.pallas_call(
        body, grid=(t // tm,),
        in_specs=[_row_tile(tm, D), _resident((1, D)), _resident((INW, D))],
        out_specs=[_row_tile(tm, D), _row_tile(tm, 2 * D), _row_tile(tm, 2 * D), pl.BlockSpec((1536, tm), lambda i: (0, i))],
        out_shape=[jax.ShapeDtypeStruct((t, D), BF), jax.ShapeDtypeStruct((t, 2 * D), BF),
                   jax.ShapeDtypeStruct((t, 2 * D), BF), jax.ShapeDtypeStruct((1536, t), BF)],
        compiler_params=_params(1), name="mix_proj")(x, g, w_t)


CONV_HALO = 32
CONV_LEAD = CONV_HALO - (CW - 1)


def _glu(uc):
    uc = uc.astype(F32)
    return uc[:, :D] * _sig(uc[:, D:])


def _ln_stats(zc):
    mu = jnp.mean(zc, axis=-1, keepdims=True)
    zm = zc - mu
    r = lax.rsqrt(jnp.mean(zm * zm, axis=-1, keepdims=True) + EPS)
    return r, zm * r


CONV_SHIFTS = 8
CONV_CHUNK = 32


def _store_shifted(buf, rows):
    for b in range(1, CONV_SHIFTS):
        buf[b, 0:rows - 8, :] = buf[0, pl.ds(b, rows - 8), :]


def _conv_fwd(uc, dwk, dwb, lng, lnb, swap=None):
    t = uc.shape[0]
    tm = min(512, t)
    per = tm // CONV_HALO
    ext = tm + CONV_HALO

    def body(cur_ref, prev_ref, k_ref, kb_ref, g_ref, b_ref, o_ref, zc_ref, zsh):
        i = pl.program_id(0)
        zsh[0, 0:CONV_HALO, :] = _glu(prev_ref[...]) * (i > 0).astype(F32)
        zsh[0, CONV_HALO:, :] = _glu(cur_ref[...])
        _store_shifted(zsh, ext)

        def chunk(ci, carry):
            r0 = pl.multiple_of(ci * CONV_CHUNK, CONV_CHUNK)
            acc = jnp.zeros((CONV_CHUNK, D), F32) + kb_ref[...]
            for w in range(CW):
                a, b = divmod(CONV_LEAD + w, 8)
                acc = acc + k_ref[w:w + 1, :] * zsh[b, pl.ds(r0 + 8 * a, CONV_CHUNK), :]
            zc_ref[pl.ds(r0, CONV_CHUNK), :] = acc
            return carry

        lax.fori_loop(0, tm // CONV_CHUNK, chunk, 0)
        r, xh = _ln_stats(zc_ref[...])
        y = xh * g_ref[...] + b_ref[...]
        o_ref[...] = (y * _sig(y)).astype(BF)

    kw = dict(
        grid=(t // tm,),
        in_specs=[_row_tile(tm, 2 * D),
                  pl.BlockSpec((CONV_HALO, 2 * D), lambda i: (jnp.maximum(i * per - 1, 0), 0)),
                  _resident((CWP, D)), _resident((1, D)), _resident((1, D)), _resident((1, D))],
        out_specs=[_row_tile(tm, D), _row_tile(tm, D)],
        out_shape=[jax.ShapeDtypeStruct((t, D), BF), jax.ShapeDtypeStruct((t, D), F32)],
        scratch_shapes=[pltpu.VMEM((CONV_SHIFTS, ext, D), F32)],
        compiler_params=_params(1), name="conv_fwd")
    args = (uc, uc, dwk, dwb, lng, lnb)
    return pl.pallas_call(body, **kw)(*args) if swap is None else _call_with_swap(body, args, swap, **kw)


def _conv_bwd(uc, zc, dzs, dwk, lng, lnb):
    t = uc.shape[0]
    tm = min(ROW_TILE_WIDE, t)
    per = tm // CONV_HALO
    n_tiles = t // tm
    ext = tm + CONV_HALO
    last_block = t // CONV_HALO - 1

    def body(cur_ref, zc_ref, zcn_ref, dz_ref, dzn_ref, k_ref, g_ref, b_ref,
             duc_ref, dk_ref, dkb_ref, dg_ref, db_ref, dsh, dk8, z_scr):
        i = pl.program_id(0)

        @pl.when(i == 0)
        def _():
            dk8[...] = jnp.zeros_like(dk8)
            dkb_ref[...] = jnp.zeros_like(dkb_ref)
            dg_ref[...] = jnp.zeros_like(dg_ref)
            db_ref[...] = jnp.zeros_like(db_ref)

        has_next = (i < n_tiles - 1).astype(F32)
        z_scr[...] = _glu(cur_ref[...])
        gain = g_ref[...]

        def ln_silu_bwd(zc, dzs, live):
            r, xh = _ln_stats(zc)
            y = xh * gain + b_ref[...]
            sy = _sig(y)
            dy = dzs * (sy * (1.0 + y * (1.0 - sy))) * live
            dxh = dy * gain
            dzc = r * (dxh - jnp.mean(dxh, axis=-1, keepdims=True) - xh * jnp.mean(dxh * xh, axis=-1, keepdims=True))
            return dzc, dy, xh

        dzc, dy, xh = ln_silu_bwd(zc_ref[...], dz_ref[...], 1.0)
        dsh[0, 0:tm, :] = dzc
        dg_ref[...] += jnp.sum(dy * xh, axis=0, keepdims=True)
        db_ref[...] += jnp.sum(dy, axis=0, keepdims=True)
        dkb_ref[...] += jnp.sum(dzc, axis=0, keepdims=True)
        dsh[0, tm:, :] = ln_silu_bwd(zcn_ref[...], dzn_ref[...], has_next)[0]
        _store_shifted(dsh, ext)

        def chunk(ci, carry):
            r0 = pl.multiple_of(ci * CONV_CHUNK, CONV_CHUNK)
            z_c = z_scr[pl.ds(r0, CONV_CHUNK), :]
            dz = jnp.zeros((CONV_CHUNK, D), F32)
            for w in range(CW):
                a, b = divmod(CW - 1 - w, 8)
                window = dsh[b, pl.ds(r0 + 8 * a, CONV_CHUNK), :]
                dz = dz + k_ref[w:w + 1, :] * window
                prod = z_c * window
                part = prod[0:8, :]
                for j in range(1, CONV_CHUNK // 8):
                    part = part + prod[8 * j:8 * j + 8, :]
                dk8[w] += part
            ucc = cur_ref[pl.ds(r0, CONV_CHUNK), :].astype(F32)
            sg = _sig(ucc[:, D:])
            duc_ref[pl.ds(r0, CONV_CHUNK), 0:D] = (dz * sg).astype(BF)
            duc_ref[pl.ds(r0, CONV_CHUNK), D:2 * D] = (dz * ucc[:, :D] * sg * (1.0 - sg)).astype(BF)
            return carry

        lax.fori_loop(0, tm // CONV_CHUNK, chunk, 0)

        @pl.when(i == n_tiles - 1)
        def _():
            dk_ref[...] = jnp.sum(dk8[...], axis=1)

    vec = pl.BlockSpec((1, D), lambda i: (0, 0))
    next_halo = pl.BlockSpec((CONV_HALO, D), lambda i: (jnp.minimum((i + 1) * per, last_block), 0))
    return pl.pallas_call(
        body, grid=(n_tiles,),
        in_specs=[_row_tile(tm, 2 * D), _row_tile(tm, D), next_halo, _row_tile(tm, D), next_halo,
                  _resident((CWP, D)), _resident((1, D)), _resident((1, D))],
        out_specs=[_row_tile(tm, 2 * D), pl.BlockSpec((CWP, D), lambda i: (0, 0)), vec, vec, vec],
        out_shape=[jax.ShapeDtypeStruct((t, 2 * D), BF), jax.ShapeDtypeStruct((CWP, D), F32),
                   jax.ShapeDtypeStruct((1, D), F32), jax.ShapeDtypeStruct((1, D), F32), jax.ShapeDtypeStruct((1, D), F32)],
        scratch_shapes=[pltpu.VMEM((CONV_SHIFTS, ext, D), F32), pltpu.VMEM((CWP, 8, D), F32), pltpu.VMEM((tm, D), F32)],
        compiler_params=_params(1), name="conv_bwd")(uc, zc, zc, dzs, dzs, dwk, lng, lnb)


def _norm_rows(xt, g):
    r = lax.rsqrt(jnp.mean(xt * xt, axis=0, keepdims=True) + EPS)
    xh = xt * r
    return xh * g, r, xh


ATT_TQ = 1024


def _attn_specs(t, tq):
    per = tq // BLK
    return [pl.BlockSpec((1536, tq), lambda i: (0, i)),
            pl.BlockSpec((512, BLK), lambda i: (2, jnp.maximum(i * per - 1, 0))),
            _resident((HD, 1)), _resident((HD, 1)), _resident((NKV, 1, GRP * BLK)),
            _resident((2, NKV, 2 * BLK, GRP * BLK))]


def _attn_window(hk, sb, qkv_ref, halo_ref, kn_cur, kn_halo):
    v0 = D + NKV * HD + hk * HD
    if sb == 0:
        k_prev = kn_halo[hk]
        v_prev = halo_ref[NKV * HD + hk * HD:NKV * HD + (hk + 1) * HD, :]
    else:
        k_prev = kn_cur[hk][:, (sb - 1) * BLK:sb * BLK]
        v_prev = qkv_ref[v0:v0 + HD, (sb - 1) * BLK:sb * BLK]
    kw = jnp.concatenate([k_prev, kn_cur[hk][:, sb * BLK:(sb + 1) * BLK]], axis=1).astype(BF)
    vw = jnp.concatenate([v_prev, qkv_ref[v0:v0 + HD, sb * BLK:(sb + 1) * BLK]], axis=1)
    return kw, vw


def _attn_probs(kw, qc, bias, sink):
    st = _dot_tn(kw, qc) + bias
    m = jnp.maximum(jnp.max(st, axis=0, keepdims=True), sink)
    p = jnp.exp(st - m)
    e_sink = jnp.exp(sink - m)
    inv = 1.0 / (jnp.sum(p, axis=0, keepdims=True) + e_sink)
    return p * inv, e_sink * inv


def _attn_fwd(qkv_t, qg, kg, sink_rows, bias_t):
    t = qkv_t.shape[1]
    tq = min(ATT_TQ, t)
    n_sub = tq // BLK

    def body(qkv_ref, halo_ref, qg_ref, kg_ref, sink_ref, bias_ref, o_ref, p_ref, ps_ref):
        i = pl.program_id(0)
        first = (i == 0).astype(jnp.int32)
        kgain = kg_ref[...]
        qgain = qg_ref[...]
        kn_cur = [_norm_rows(qkv_ref[D + h * HD:D + (h + 1) * HD, :].astype(F32), kgain)[0] for h in range(NKV)]
        kn_halo = [_norm_rows(halo_ref[h * HD:(h + 1) * HD, :].astype(F32), kgain)[0] for h in range(NKV)]
        for hk in range(NKV):
            for sb in range(n_sub):
                cols = slice(sb * BLK, (sb + 1) * BLK)
                kw, vw = _attn_window(hk, sb, qkv_ref, halo_ref, kn_cur, kn_halo)
                qc = jnp.concatenate(
                    [_norm_rows(qkv_ref[(GRP * hk + g) * HD:(GRP * hk + g + 1) * HD, cols].astype(F32), qgain)[0] * QK_SCALE
                     for g in range(GRP)], axis=1).astype(BF)
                bias = bias_ref[first, hk] if sb == 0 else bias_ref[0, hk]
                p, p_sink = _attn_probs(kw, qc, bias, sink_ref[hk])
                p = p.astype(BF)
                p_ref[sb, hk] = p
                ps_ref[sb, hk] = p_sink
                o = _dot(vw, p)
                for g in range(GRP):
                    head = GRP * hk + g
                    o_ref[head * HD:(head + 1) * HD, cols] = o[:, g * BLK:(g + 1) * BLK].astype(BF)

    return pl.pallas_call(
        body, grid=(t // tq,),
        in_specs=_attn_specs(t, tq),
        out_specs=[pl.BlockSpec((D, tq), lambda i: (0, i)),
                   pl.BlockSpec((n_sub, NKV, 2 * BLK, GRP * BLK), lambda i: (i, 0, 0, 0)),
                   pl.BlockSpec((n_sub, NKV, 1, GRP * BLK), lambda i: (i, 0, 0, 0))],
        out_shape=[jax.ShapeDtypeStruct((D, t), BF), jax.ShapeDtypeStruct((t // BLK, NKV, 2 * BLK, GRP * BLK), BF),
                   jax.ShapeDtypeStruct((t // BLK, NKV, 1, GRP * BLK), F32)],
        compiler_params=_params(1), name="attn_fwd")(qkv_t, qkv_t, qg, kg, sink_rows, bias_t)


def _attn_bwd(qkv_t, do_t, probs, sink_probs, qg, kg, onehot_t, deps=()):
    t = qkv_t.shape[1]
    tq = min(ATT_TQ, t)
    n_sub = tq // BLK
    n_tiles = t // tq

    def body(qkv_ref, halo_ref, do_ref, p_ref, ps_ref, qg_ref, kg_ref, oh_ref,
             dq_ref, ckv_ref, dqg_ref, dsink_ref, dbias_ref, qg_scr, sink_scr, ds_scr):
        i = pl.program_id(0)

        @pl.when(i == 0)
        def _():
            qg_scr[...] = jnp.zeros_like(qg_scr)
            sink_scr[...] = jnp.zeros_like(sink_scr)
            ds_scr[...] = jnp.zeros_like(ds_scr)

        kgain = kg_ref[...]
        qgain = qg_ref[...]
        kn_cur = [_norm_rows(qkv_ref[D + h * HD:D + (h + 1) * HD, :].astype(F32), kgain)[0] for h in range(NKV)]
        kn_halo = [_norm_rows(halo_ref[h * HD:(h + 1) * HD, :].astype(F32), kgain)[0] for h in range(NKV)]
        dqg = jnp.zeros((HD, BLK), F32)
        for hk in range(NKV):
            for sb in range(n_sub):
                cols = slice(sb * BLK, (sb + 1) * BLK)
                kw, vw = _attn_window(hk, sb, qkv_ref, halo_ref, kn_cur, kn_halo)
                qn, qr, qh = [], [], []
                for g in range(GRP):
                    head = GRP * hk + g
                    n_, r_, h_ = _norm_rows(qkv_ref[head * HD:(head + 1) * HD, cols].astype(F32), qgain)
                    qn.append(n_)
                    qr.append(r_)
                    qh.append(h_)
                qc = (jnp.concatenate(qn, axis=1) * QK_SCALE).astype(BF)
                p_bf = p_ref[sb, hk]
                p = p_bf.astype(F32)
                doc = jnp.concatenate([do_ref[(GRP * hk + g) * HD:(GRP * hk + g + 1) * HD, cols] for g in range(GRP)], axis=1)
                dp = _dot_tn(vw, doc)
                delta = jnp.sum(p * dp, axis=0, keepdims=True)
                ds = p * (dp - delta)
                sink_scr[hk] += -(ps_ref[sb, hk] * delta)
                ds_scr[hk] += ds
                dsb = ds.astype(BF)
                dqc = _dot(kw, dsb) * QK_SCALE
                ckv_ref[sb, hk * HD:(hk + 1) * HD, :] = _dot_nt(qc, dsb)
                ckv_ref[sb, NKV * HD + hk * HD:NKV * HD + (hk + 1) * HD, :] = _dot_nt(doc, p_bf)
                for g in range(GRP):
                    head = GRP * hk + g
                    dqn = dqc[:, g * BLK:(g + 1) * BLK]
                    dqh = dqn * qgain
                    dq = qr[g] * (dqh - qh[g] * jnp.mean(dqh * qh[g], axis=0, keepdims=True))
                    dq_ref[head * HD:(head + 1) * HD, cols] = dq.astype(BF)
                    dqg = dqg + dqn * qh[g]
        qg_scr[...] += dqg

        @pl.when(i == n_tiles - 1)
        def _():
            dqg_ref[...] = jnp.sum(qg_scr[...], axis=1, keepdims=True)
            dsink_ref[...] = _group_lane_sums(sink_scr[:, 0, :])

            def bucket(b, carry):
                oh = jnp.concatenate([oh_ref[b]] * GRP, axis=1)
                dbias_ref[b] = _group_lane_sums(jnp.sum(ds_scr[...] * oh[None], axis=1))
                return carry

            lax.fori_loop(0, NBUCKET, bucket, 0)

    return _call(
        body, deps, (qkv_t, qkv_t, do_t, probs, sink_probs, qg, kg, onehot_t), grid=(n_tiles,),
        in_specs=_attn_specs(t, tq)[:2] + [pl.BlockSpec((D, tq), lambda i: (0, i)),
                                           pl.BlockSpec((n_sub, NKV, 2 * BLK, GRP * BLK), lambda i: (i, 0, 0, 0)),
                                           pl.BlockSpec((n_sub, NKV, 1, GRP * BLK), lambda i: (i, 0, 0, 0))]
        + _attn_specs(t, tq)[2:4] + [_resident((NBUCKET, 2 * BLK, BLK))],
        out_specs=[pl.BlockSpec((D, tq), lambda i: (0, i)),
                   pl.BlockSpec((n_sub, 2 * NKV * HD, 2 * BLK), lambda i: (i, 0, 0)),
                   pl.BlockSpec((HD, 1), lambda i: (0, 0)),
                   pl.BlockSpec((NKV, BLK), lambda i: (0, 0)),
                   pl.BlockSpec((NBUCKET, NKV, BLK), lambda i: (0, 0, 0))],
        out_shape=[jax.ShapeDtypeStruct((D, t), BF),
                   jax.ShapeDtypeStruct((t // BLK, 2 * NKV * HD, 2 * BLK), F32),
                   jax.ShapeDtypeStruct((HD, 1), F32),
                   jax.ShapeDtypeStruct((NKV, BLK), F32),
                   jax.ShapeDtypeStruct((NBUCKET, NKV, BLK), F32)],
        scratch_shapes=[pltpu.VMEM((HD, BLK), F32), pltpu.VMEM((NKV, 1, GRP * BLK), F32),
                        pltpu.VMEM((NKV, 2 * BLK, GRP * BLK), F32)],
        compiler_params=_params(1), name="attn_bwd")


def _kv_combine_tile(c_ref, cn_ref, has_next, k_ref, kgain, o_ref):
    rows = NKV * HD
    per = c_ref.shape[0]
    dkg = jnp.zeros((HD, BLK), F32)
    for s in range(per):
        cols = slice(s * BLK, (s + 1) * BLK)
        after = c_ref[s + 1, :, :BLK] if s + 1 < per else cn_ref[0, :, :BLK] * has_next
        d = c_ref[s, :, BLK:] + after
        o_ref[rows:, cols] = d[rows:, :].astype(BF)
        for h in range(NKV):
            _, r, kh = _norm_rows(k_ref[h * HD:(h + 1) * HD, cols].astype(F32), kgain)
            dkn = d[h * HD:(h + 1) * HD, :]
            dkh = dkn * kgain
            o_ref[h * HD:(h + 1) * HD, cols] = (r * (dkh - kh * jnp.mean(dkh * kh, axis=0, keepdims=True))).astype(BF)
            dkg = dkg + dkn * kh
    return dkg


def _group_lane_sums(v):
    lane_group = lax.broadcasted_iota(jnp.int32, (1, GRP * BLK), 1) // BLK
    col = lax.broadcasted_iota(jnp.int32, (1, BLK), 1)
    out = jnp.zeros((v.shape[0], BLK), F32)
    for g in range(GRP):
        s = jnp.sum(jnp.where(lane_group == g, v, 0.0), axis=1, keepdims=True)
        out = jnp.where(col == g, s, out)
    return out


def _mix_out(zs, o_t, gp, x, w_cp, w_o, w_out):
    t = x.shape[0]
    tm = min(ROW_TILE_WIDE, t)

    def body(zs_ref, ot_ref, gp_ref, x_ref, wcp_ref, wo_ref, wout_ref, xo_ref, a_ref, b_ref, m_ref):
        a = _dot(zs_ref[...], wcp_ref[...])
        b = _dot_tn(ot_ref[...], wo_ref[...])
        a_ref[...] = a.astype(BF)
        b_ref[...] = b.astype(BF)
        merged = (_sig(gp_ref[:, :D].astype(F32)) * a + _sig(gp_ref[:, D:].astype(F32)) * b).astype(BF)
        m_ref[...] = merged
        xo_ref[...] = x_ref[...] + _dot(merged, wout_ref[...])

    return pl.pallas_call(
        body, grid=(t // tm,),
        in_specs=[_row_tile(tm, D), pl.BlockSpec((D, tm), lambda i: (0, i)), _row_tile(tm, 2 * D), _row_tile(tm, D),
                  _resident((D, D)), _resident((D, D)), _resident((D, D))],
        out_specs=[_row_tile(tm, D)] * 4,
        out_shape=[jax.ShapeDtypeStruct((t, D), F32)] + [jax.ShapeDtypeStruct((t, D), BF)] * 3,
        compiler_params=_params(1), name="mix_out")(zs, o_t, gp, x, w_cp, w_o, w_out)


def _mix_out_bwd(dx, a, b, gp, w_cp, w_o, w_out, deps=()):
    t = dx.shape[0]
    tm = min(ROW_TILE_WIDE, t)

    def body(dx_ref, a_ref, b_ref, gp_ref, wcp_ref, wo_ref, wout_ref, dzs_ref, dot_ref, dgp_ref, da_ref, db_ref, dxb_ref):
        dxb = dx_ref[...].astype(BF)
        dxb_ref[...] = dxb
        dm = _dot_nt(dxb, wout_ref[...])
        gc = _sig(gp_ref[:, :D].astype(F32))
        ga = _sig(gp_ref[:, D:].astype(F32))
        da = (dm * gc).astype(BF)
        db = (dm * ga).astype(BF)
        da_ref[...] = da
        db_ref[...] = db
        dgp_ref[:, :D] = (dm * a_ref[...].astype(F32) * gc * (1.0 - gc)).astype(BF)
        dgp_ref[:, D:] = (dm * b_ref[...].astype(F32) * ga * (1.0 - ga)).astype(BF)
        dzs_ref[...] = _dot_nt(da, wcp_ref[...])
        dot_ref[...] = _dot_nt(wo_ref[...], db).astype(BF)

    return _call(
        body, deps, (dx, a, b, gp, w_cp, w_o, w_out), grid=(t // tm,),
        in_specs=[_row_tile(tm, D), _row_tile(tm, D), _row_tile(tm, D), _row_tile(tm, 2 * D),
                  _resident((D, D)), _resident((D, D)), _resident((D, D))],
        out_specs=[_row_tile(tm, D), pl.BlockSpec((D, tm), lambda i: (0, i)), _row_tile(tm, 2 * D),
                   _row_tile(tm, D), _row_tile(tm, D), _row_tile(tm, D)],
        out_shape=[jax.ShapeDtypeStruct((t, D), F32), jax.ShapeDtypeStruct((D, t), BF), jax.ShapeDtypeStruct((t, 2 * D), BF),
                   jax.ShapeDtypeStruct((t, D), BF), jax.ShapeDtypeStruct((t, D), BF), jax.ShapeDtypeStruct((t, D), BF)],
        compiler_params=_params(1), name="mix_out_bwd")


def _mix_proj_bwd(dxo, duc, dq_t, ckv, qkv_t, kg, dgp, x, g, w_t):
    t = x.shape[0]
    tm = min(ROW_TILE_WIDE, t)
    per = tm // BLK
    steps = t // tm
    kv_rows = 2 * NKV * HD

    def body(dxo_ref, duc_ref, dq_ref, c_ref, cn_ref, k_ref, kg_ref, dgp_ref, x_ref, g_ref, w_ref,
             dx_ref, dg_ref, dkv_ref, dkg_ref, kg_scr):
        i = pl.program_id(0)

        @pl.when(i == 0)
        def _():
            dg_ref[...] = jnp.zeros_like(dg_ref)
            kg_scr[...] = jnp.zeros_like(kg_scr)

        kg_scr[...] += _kv_combine_tile(c_ref, cn_ref, (i < steps - 1).astype(F32), k_ref, kg_ref[...], dkv_ref)
        dn = _dot(duc_ref[...], w_ref[R_CONV[0]:R_CONV[1], :])
        dn = dn + _dot(dgp_ref[...], w_ref[R_GATE[0]:R_GATE[1], :])
        dn = dn + _dot_tn(dq_ref[...], w_ref[R_Q[0]:R_Q[1], :])
        dn = dn + _dot_tn(dkv_ref[...], w_ref[R_KV[0]:R_KV[1], :])
        dx, dg = _rms_bwd(dn, x_ref[...], g_ref[...])
        dx_ref[...] = dxo_ref[...] + dx
        dg_ref[...] += dg

        @pl.when(i == steps - 1)
        def _():
            dkg_ref[...] = jnp.sum(kg_scr[...], axis=1, keepdims=True)

    return pl.pallas_call(
        body, grid=(steps,),
        in_specs=[_row_tile(tm, D), _row_tile(tm, 2 * D), pl.BlockSpec((D, tm), lambda i: (0, i)),
                  pl.BlockSpec((per, kv_rows, 2 * BLK), lambda i: (i, 0, 0)),
                  pl.BlockSpec((1, kv_rows, 2 * BLK), lambda i: (jnp.minimum((i + 1) * per, t // BLK - 1), 0, 0)),
                  pl.BlockSpec((NKV * HD, tm), lambda i: (D // (NKV * HD), i)), _resident((HD, 1)),
                  _row_tile(tm, 2 * D), _row_tile(tm, D), _resident((1, D)), _resident((INW, D))],
        out_specs=[_row_tile(tm, D), pl.BlockSpec((1, D), lambda i: (0, 0)), pl.BlockSpec((kv_rows, tm), lambda i: (0, i)),
                   pl.BlockSpec((HD, 1), lambda i: (0, 0))],
        out_shape=[jax.ShapeDtypeStruct((t, D), F32), jax.ShapeDtypeStruct((1, D), F32),
                   jax.ShapeDtypeStruct((kv_rows, t), BF), jax.ShapeDtypeStruct((HD, 1), F32)],
        scratch_shapes=[pltpu.VMEM((HD, BLK), F32)],
        compiler_params=_params(1), name="mix_proj_bwd")(dxo, duc, dq_t, ckv, ckv, qkv_t, kg, dgp, x, g, w_t)


def _attention_tables():
    kj = np.arange(2 * BLK)[:, None]
    qi = np.arange(BLK)[None, :]
    dist = qi + BLK - kj
    in_win = (dist >= 0) & (dist < BLK)
    dpos = np.maximum(dist, 0)
    max_exact = NBUCKET // 2
    dfl = np.maximum(dpos, 1).astype(np.float32)
    large = max_exact + (np.log(dfl / np.float32(max_exact)) / np.float32(math.log(BLK / max_exact))
                         * np.float32(NBUCKET - max_exact)).astype(np.int32)
    large = np.minimum(large, NBUCKET - 1)
    bucket = np.where(dpos < max_exact, dpos, large)
    onehot = (bucket[None] == np.arange(NBUCKET)[:, None, None]).astype(np.float32)
    mask = in_win.astype(np.float32)
    mask_first = mask * (kj >= BLK)
    masks = np.stack([np.tile(mask, (1, GRP)), np.tile(mask_first, (1, GRP))])
    return onehot, masks


def _bias_table(rel_bias, onehot):
    tab = jnp.einsum("bkq,bh->hkq", onehot, rel_bias, precision=lax.Precision.HIGHEST)
    tab = tab.reshape(NKV, GRP, 2 * BLK, BLK)
    return jnp.transpose(tab, (0, 2, 1, 3)).reshape(NKV, 2 * BLK, GRP * BLK)


def _local_step(x, target, vec, ffn1_up, weights_of, wgrad, wgrads, grads_done, small_done):
    onehot_np, masks_np = _attention_tables()
    onehot = jnp.asarray(onehot_np)
    masks = jnp.asarray(masks_np)
    bias_t = jnp.where(masks[:, None] > 0.5, _bias_table(vec["rel_bias"], onehot)[None], NEG)
    sink_rows = jnp.repeat(vec["attn_sinks"].reshape(NKV, 1, GRP), BLK, axis=2)
    qg = vec["q_norm"].reshape(HD, 1)
    kg = vec["k_norm"].reshape(HD, 1)
    g1 = vec["ffn1_norm"].reshape(1, D)
    gm = vec["mix_norm"].reshape(1, D)
    g2 = vec["ffn2_norm"].reshape(1, D)
    dwb = vec["conv_dw_bias"].reshape(1, D)
    lng = vec["conv_ln_g"].reshape(1, D)
    lnb = vec["conv_ln_b"].reshape(1, D)

    n1, u1, w1 = ffn1_up(x, g1, (bias_t, sink_rows))
    x1, = _ffn_down(x, u1, w1["ffn1_w_out"], "ffn1_down_first", part=(0, 2))
    (x1,), wm = weights_of("mix_proj", (x1,), during=functools.partial(
        _ffn_down, x, u1, w1["ffn1_w_out"], "ffn1_down", part=(1, 2), out=x1))
    dwk = jnp.pad(wm["conv_dw_kernel"], ((0, CWP - CW), (0, 0)))
    hm, uc, gp, qkv_t = _mix_proj(x1, gm, wm["w_in"])
    (zs, zc), merge = weights_of("mix_merge", (uc,), during=functools.partial(_conv_fwd, uc, dwk, dwb, lng, lnb))
    wm.update(merge)
    o_t, probs, sink_probs = _attn_fwd(qkv_t, qg, kg, sink_rows, bias_t)
    x2, a, b, merged = _mix_out(zs, o_t, gp, x1, wm["conv_w_proj"], wm["attn_w_o"], wm["w_out"])
    w2 = weights_of("ffn2", (x2,))
    gv = {}
    n2, du2, h2, dy2, dx2, sq, gv["ffn2_norm"] = _ffn_last(x2, target, g2, w2["ffn2_w_in"], w2["ffn2_w_out"], "ffn2")

    deps = grads_done("ffn2", {"ffn2_w_in": wgrad(du2, n2, "ffn2_dw_in", False),
                               "ffn2_w_out": wgrad(h2, dy2, "ffn2_dw_out", False)})

    dzs, do_t, dgp, da, db, dx2b = _mix_out_bwd(dx2, a, b, gp, wm["conv_w_proj"], wm["attn_w_o"], wm["w_out"], deps=deps)
    grads = wgrads([(merged, dx2b, False), (zs, da, False), (o_t, db, True)], "mix_dw_merge")
    deps = grads_done("mix_out", dict(zip(("w_out", "conv_w_proj", "attn_w_o"), grads)))

    dq_t, ckv, dqg, dsink, dbias = _attn_bwd(qkv_t, do_t, probs, sink_probs, qg, kg, onehot, deps=deps)
    gv["q_norm"] = dqg.reshape(HD)
    gv["attn_sinks"] = dsink[:, :GRP].reshape(NQ)
    gv["rel_bias"] = dbias[:, :, :GRP].reshape(NBUCKET, NQ)

    duc, dk_conv, gv["conv_dw_bias"], gv["conv_ln_g"], gv["conv_ln_b"] = _conv_bwd(uc, zc, dzs, dwk, lng, lnb)
    gv["conv_dw_kernel"] = dk_conv[:CW]

    dx1, gv["mix_norm"], dkv_t, dkg = _mix_proj_bwd(dx2, duc, dq_t, ckv, qkv_t, kg, dgp, x1, gm, wm["w_in"])
    gv["k_norm"] = dkg.reshape(HD)
    deps = grads_done("mix_in", {"w_in": _wgrad_mix(duc, dq_t, dkv_t, dgp, hm)})

    dx0, du1, h1, dy1, gv["ffn1_norm"] = _ffn_bwd(dx1, x, g1, u1, w1["ffn1_w_in"], w1["ffn1_w_out"], "ffn1_bwd", deps=deps)
    for k in ("ffn1_norm", "mix_norm", "ffn2_norm", "conv_dw_bias", "conv_ln_g", "conv_ln_b"):
        gv[k] = gv[k].reshape(D)
    deps = small_done(gv, sq)
    deps = grads_done("ffn1_in", {"ffn1_w_in": wgrad(du1, n1, "ffn1_dw_in", False, deps)})
    grads_done("ffn1_out", {"ffn1_w_out": wgrad(h1, dy1, "ffn1_dw_out", False, deps)})
    return dx0


MESH_ID = pl.DeviceIdType.MESH


def _position():
    return lax.axis_index("x"), lax.axis_index("y"), lax.axis_index("c")


def _shard_rows(ref, index, rows):
    return ref.at[pl.ds(pl.multiple_of(index * rows, 16), rows), :]


def _prep(weights, taps, me, name, deps=(), swap=None):
    n = len(weights)
    n_deps = len(deps)
    with_taps = taps is not None

    def body(me_ref, *refs):
        refs = refs[n_deps:]
        ins, outs = refs[:len(refs) // 2], refs[len(refs) // 2:]
        for k in range(n):
            outs[k][...] = ins[k][...].astype(BF)
        if with_taps:
            outs[n][0:CW, :] = ins[n][...]
            outs[n][CW:, :] = jnp.zeros((CWP - CW, BLK), F32)

    shard_shapes = [w.shape for w in weights] + [(CWP, BLK)] * with_taps
    dtypes = [BF] * n + [F32] * with_taps
    ins = list(weights) + [taps] * with_taps
    in_specs = [ANY] * n_deps + [pl.BlockSpec(a.shape, lambda i, m: (0, 0), pipeline_mode=pl.Buffered(1)) for a in ins]
    out_specs = [pl.BlockSpec(s, lambda i, m: (m[0], 0)) for s in shard_shapes]
    out_shape = [jax.ShapeDtypeStruct((N_DEV * s[0], s[1]), d) for s, d in zip(shard_shapes, dtypes)]
    if swap is not None:
        return _call_with_swap(body, (*deps, *ins), swap, prefetch=(me,), grid=(1,), in_specs=in_specs, out_specs=out_specs,
                               out_shape=out_shape, scratch_shapes=[], compiler_params=_params(1), name=name)
    return pl.pallas_call(
        body,
        grid_spec=pltpu.PrefetchScalarGridSpec(num_scalar_prefetch=1, grid=(1,), in_specs=in_specs, out_specs=out_specs),
        out_shape=out_shape, compiler_params=_params(1), name=name)(me, *deps, *ins)


HBM = pl.BlockSpec(memory_space=pltpu.HBM)
SEM = pl.BlockSpec(memory_space=pltpu.SEMAPHORE)
DATAFLOW = pltpu.SideEffectType.DATAFLOW_SIDE_EFFECTING
TOKEN = jax.ShapeDtypeStruct((8, 128), F32)


def _in_hbm(x):
    return pltpu.with_memory_space_constraint(x, pltpu.HBM)


def _hbm_like(arrays):
    return [pltpu.HBM(a.shape, a.dtype) for a in arrays]


def _other_chips(x, y):
    return [(1 - x, y), (x, 1 - y), (1 - x, 1 - y)]


def _device_index(chip, c):
    return 4 * chip[0] + 2 * chip[1] + c


def _chip_index(chip):
    return 2 * chip[0] + chip[1]


class _Exchange:
    def __init__(self, gather, all_cores=False):
        self.gather = gather
        self.all_cores = all_cores
        self.n_peers = N_DEV - 1 if all_cores else 3

    def peers(self, x, y, c):
        if self.all_cores:
            return [(x ^ (k >> 2), y ^ ((k >> 1) & 1), c ^ (k & 1)) for k in range(1, N_DEV)]
        return [(*chip, c) for chip in _other_chips(x, y)]

    def sent(self, x, y, c, peer):
        return _device_index((x, y), c) if self.gather else _chip_index(peer[:2])

    def lands_at(self, x, y, c):
        return _device_index((x, y), c) if self.gather else _chip_index((x, y))

    def arrives_at(self, peer):
        return _device_index(peer[:2], peer[2]) if self.gather else _chip_index(peer[:2])


def _ici_copies_start(sets, sources, landings, exchanges, name, deps=()):
    n = len(landings)
    arrays = (list(sources) if sources is not None else []) + list(landings)
    first_land = len(arrays) - n
    n_sets = len(sets)
    n_deps = len(deps)

    def body(*refs):
        refs = refs[n_deps:]
        src, land = refs[:n], refs[first_land:first_land + n]
        sems = refs[len(arrays):len(arrays) + 2 * n_sets]
        token = refs[-1]
        x, y, c = _position()
        for s, (members, exchange) in enumerate(zip(sets, exchanges)):
            for slot, (k, rows) in enumerate(members):
                for j, peer in enumerate(exchange.peers(x, y, c)):
                    at = exchange.n_peers * slot + j
                    pltpu.make_async_remote_copy(
                        src_ref=_shard_rows(src[k], exchange.sent(x, y, c, peer), rows),
                        dst_ref=_shard_rows(land[k], exchange.lands_at(x, y, c), rows),
                        send_sem=sems[2 * s].at[at], recv_sem=sems[2 * s + 1].at[at],
                        device_id=peer, device_id_type=MESH_ID).start()
        token[...] = jnp.zeros_like(token)

    sem_shapes = []
    for members, exchange in zip(sets, exchanges):
        sem_shapes += [pltpu.SemaphoreType.DMA((exchange.n_peers * len(members),))] * 2
    out = pl.pallas_call(
        body, name=name,
        out_shape=sem_shapes + _hbm_like(arrays) + [TOKEN],
        in_specs=[ANY] * n_deps + [HBM] * len(arrays),
        out_specs=[SEM] * (2 * n_sets) + [HBM] * len(arrays) + [pl.BlockSpec(memory_space=pltpu.VMEM)],
        input_output_aliases={n_deps + i: 2 * n_sets + i for i in range(len(arrays))},
        compiler_params=pltpu.CompilerParams(has_side_effects=DATAFLOW),
    )(*deps, *[_in_hbm(a) for a in arrays])
    sems = [(out[2 * s], out[2 * s + 1]) for s in range(n_sets)]
    thru = list(out[2 * n_sets:2 * n_sets + len(arrays)])
    return sems, (thru[:first_land] if sources is not None else None), thru[first_land:], out[-1]


def _ici_copies_wait(sems, members, sources, landings, exchange, after, name):
    n = len(landings)
    arrays = (list(sources) if sources is not None else []) + list(landings)
    first_land = len(arrays) - n

    def body(*refs):
        src, land = refs[:n], refs[first_land:first_land + n]
        send_sems, recv_sems = refs[len(arrays)], refs[len(arrays) + 1]
        x, y, c = _position()
        for slot, rows in enumerate(members):
            for j, peer in enumerate(exchange.peers(x, y, c)):
                at = exchange.n_peers * slot + j
                cp = pltpu.make_async_remote_copy(
                    src_ref=_shard_rows(src[slot], exchange.sent(x, y, c, peer), rows),
                    dst_ref=_shard_rows(land[slot], exchange.arrives_at(peer), rows),
                    send_sem=send_sems.at[at], recv_sem=recv_sems.at[at], device_id=peer, device_id_type=MESH_ID)
                cp.wait_send()
                cp.wait_recv()

    out = pl.pallas_call(
        body, name=name, out_shape=_hbm_like(arrays),
        in_specs=[HBM] * len(arrays) + [SEM, SEM] + [ANY] * len(after), out_specs=[HBM] * len(arrays),
        input_output_aliases={i: i for i in range(len(arrays))},
        compiler_params=pltpu.CompilerParams(has_side_effects=DATAFLOW),
    )(*arrays, sems[0], sems[1], *after)
    return list(out[first_land:])


def _swap_copies(land, rows, which, send_sems, recv_sems):
    x, y, c = _position()
    chips = [([(x, y)] + _other_chips(x, y))[j] for j in which]
    sends, recvs = [], []
    for k in range(len(land)):
        for j, chip in enumerate(chips):
            for copies, core in ((sends, c), (recvs, 1 - c)):
                block = _shard_rows(land[k], _device_index(chip, core), rows[k])
                copies.append(pltpu.make_async_remote_copy(
                    src_ref=block, dst_ref=block, send_sem=send_sems.at[k, j], recv_sem=recv_sems.at[k, j],
                    device_id=(x, y, 1 - c), device_id_type=MESH_ID))
    return sends, recvs


def _d2d_gather(buffers, rows, name, which=(0, 1, 2, 3), deps=()):
    n = len(buffers)
    n_deps = len(deps)

    def body(*refs):
        sends, recvs = _swap_copies(refs[n_deps + n:n_deps + 2 * n], rows, which, *refs[n_deps + 2 * n:])
        for cp in sends:
            cp.start()
        for cp in recvs:
            cp.wait_recv()
        for cp in sends:
            cp.wait_send()

    return pl.pallas_call(
        body, name=name, out_shape=[jax.ShapeDtypeStruct(a.shape, a.dtype) for a in buffers],
        in_specs=[ANY] * (n_deps + n), out_specs=[ANY] * n, input_output_aliases={n_deps + i: i for i in range(n)},
        scratch_shapes=[pltpu.SemaphoreType.DMA((n, len(which))), pltpu.SemaphoreType.DMA((n, len(which)))],
    )(*deps, *buffers)


def _call_with_swap(body, args, swap, prefetch=(), **kw):
    buffers, rows, *chips = swap
    which = chips[0] if chips else (0, 1, 2, 3)
    n, n_pre, n_in, n_out = len(buffers), len(prefetch), len(args), len(kw["out_shape"])
    n_scratch = len(kw["scratch_shapes"])
    grid = kw["grid"]

    def at_step(last):
        hit = [pl.program_id(a) == (extent - 1 if last else 0) for a, extent in enumerate(grid)]
        return functools.reduce(jnp.logical_and, hit)

    def hosted(*refs):
        pre, ins, refs = refs[:n_pre], refs[n_pre:n_pre + n_in], refs[n_pre + n_in + n:]
        outs, land, scratch = refs[:n_out], refs[n_out:n_out + n], refs[n_out + n:n_out + n + n_scratch]
        sends, recvs = _swap_copies(land, rows, which, *refs[n_out + n + n_scratch:])

        @pl.when(at_step(False))
        def _():
            for cp in sends:
                cp.start()

        body(*pre, *ins, *outs, *scratch)

        @pl.when(at_step(True))
        def _():
            for cp in recvs:
                cp.wait_recv()
            for cp in sends:
                cp.wait_send()

    sem_shape = pltpu.SemaphoreType.DMA((n, len(which)))
    aliases = {**kw.get("input_output_aliases", {}), **{n_in + i: n_out + i for i in range(n)}}
    out = pl.pallas_call(
        hosted,
        grid_spec=pltpu.PrefetchScalarGridSpec(
            num_scalar_prefetch=n_pre, grid=grid, in_specs=kw["in_specs"] + [ANY] * n, out_specs=kw["out_specs"] + [ANY] * n,
            scratch_shapes=kw["scratch_shapes"] + [sem_shape, sem_shape]),
        out_shape=kw["out_shape"] + [jax.ShapeDtypeStruct(a.shape, a.dtype) for a in buffers],
        input_output_aliases={n_pre + i: o for i, o in aliases.items()},
        compiler_params=kw["compiler_params"], name=kw["name"])(*prefetch, *args, *buffers)
    return out[:n_out], out[n_out:]


def _pair_exchange_add(grad, place, name):
    r = grad.shape[0] // N_DEV
    n_chips = N_DEV // 2

    def body(place_ref, g_hbm, kept_ref, part_ref, land_ref, inbox, send_sems, recv_sems):
        q = pl.program_id(0)
        x, y, c = _position()
        copies = [pltpu.make_async_remote_copy(
            src_ref=_shard_rows(g_hbm, 2 * i + 1 - c, r), dst_ref=inbox.at[i], send_sem=send_sems.at[i],
            recv_sem=recv_sems.at[i], device_id=(x, y, 1 - c), device_id_type=MESH_ID) for i in range(n_chips)]

        @pl.when(q == 0)
        def _():
            for cp in copies:
                cp.start()

        for i, cp in enumerate(copies):
            @pl.when(q == i)
            def _(cp=cp):
                cp.wait_recv()

        total = (kept_ref[...].astype(F32) + inbox[q].astype(F32)).astype(BF)
        part_ref[...] = total

        @pl.when(q == place_ref[1])
        def _():
            land_ref[...] = total

        @pl.when(q == n_chips - 1)
        def _():
            for cp in copies:
                cp.wait_send()

    return pl.pallas_call(
        body,
        grid_spec=pltpu.PrefetchScalarGridSpec(
            num_scalar_prefetch=1, grid=(n_chips,),
            in_specs=[ANY, pl.BlockSpec((r, D), lambda q, p: (2 * q + p[0], 0))],
            out_specs=[pl.BlockSpec((r, D), lambda q, p: (q, 0)), pl.BlockSpec((r, D), lambda q, p: (p[1], 0))],
            scratch_shapes=[pltpu.VMEM((n_chips, r, D), BF), pltpu.SemaphoreType.DMA((n_chips,)),
                            pltpu.SemaphoreType.DMA((n_chips,))]),
        out_shape=[jax.ShapeDtypeStruct((n_chips * r, D), BF)] * 2,
        compiler_params=_params(1), name=name)(place, grad, grad)


def _wgrad_pair_sum(lhs, rhs, place, name, *, lhs_is_transposed, deps=()):
    t = rhs.shape[0]
    n = lhs.shape[0] if lhs_is_transposed else lhs.shape[1]
    r = n // N_DEV
    n_chips = N_DEV // 2
    per = 1 if (2 * r) % BLK == 0 else 2
    steps = n_chips // per
    n_deps = len(deps)
    in_vmem = r <= WGRAD_SUM_MAX_ROWS

    def body(place_ref, *refs):
        if in_vmem:
            l_ref, r_ref, part_ref, land_ref, res, inbox, send_sems, recv_sems = refs[n_deps:]
        else:
            l_ref, r_ref, part_ref, land_ref, inbox, res, staged, send_sems, recv_sems, stage_sem = refs[n_deps:]
        q = pl.program_id(0)
        slot = q % 2
        x, y, c = _position()

        def send(step, buf, i):
            return pltpu.make_async_remote_copy(
                src_ref=res.at[buf, pl.ds(pl.multiple_of((2 * i + 1 - c) * r, 16), r), :], dst_ref=inbox.at[step * per + i],
                send_sem=send_sems.at[buf, i], recv_sem=recv_sems.at[step * per + i],
                device_id=(x, y, 1 - c), device_id_type=MESH_ID)

        @pl.when(q < steps)
        def _():
            @pl.when(q >= 2)
            def _():
                for i in range(per):
                    send(q - 2, slot, i).wait_send()

            if lhs_is_transposed:
                res[slot] = _dot(l_ref[...], r_ref[...]).astype(BF)
            else:
                res[slot] = _dot_tn(l_ref[...], r_ref[...]).astype(BF)
            for i in range(per):
                send(q, slot, i).start()

        @pl.when(q >= 1)
        def _():
            for i in range(per):
                chip = (q - 1) * per + i
                send(q - 1, 1 - slot, i).wait_recv()
                kept = res[1 - slot, pl.ds(pl.multiple_of((2 * i + c) * r, 16), r), :]
                if in_vmem:
                    theirs = inbox[chip]
                else:
                    stage = pltpu.make_async_copy(inbox.at[chip], staged, stage_sem)
                    stage.start()
                    stage.wait()
                    theirs = staged[...]
                total = (kept.astype(F32) + theirs.astype(F32)).astype(BF)
                part_ref[i * r:(i + 1) * r, :] = total

                @pl.when(chip == place_ref[1])
                def _():
                    land_ref[...] = total

        @pl.when(q == steps)
        def _():
            for i in range(per):
                if steps > 1:
                    send(q - 2, slot, i).wait_send()
                send(q - 1, 1 - slot, i).wait_send()

    width = 2 * r * per
    last = steps - 1
    if lhs_is_transposed:
        lhs_spec = pl.BlockSpec((width, t), lambda q, p: (jnp.minimum(q, last), 0))
    else:
        lhs_spec = pl.BlockSpec((t, width), lambda q, p: (0, jnp.minimum(q, last)))
    sems = [pltpu.SemaphoreType.DMA((2, per)), pltpu.SemaphoreType.DMA((n_chips,))]
    inbox_shape = (n_chips, r, D)
    if in_vmem:
        extra_specs, extra_shapes = [], []
        scratch = [pltpu.VMEM((2, width, D), BF), pltpu.VMEM(inbox_shape, BF)] + sems
    else:
        extra_specs, extra_shapes = [ANY], [jax.ShapeDtypeStruct(inbox_shape, BF)]
        scratch = [pltpu.VMEM((2, width, D), BF), pltpu.VMEM((r, D), BF)] + sems + [pltpu.SemaphoreType.DMA(())]
    out = pl.pallas_call(
        body,
        grid_spec=pltpu.PrefetchScalarGridSpec(
            num_scalar_prefetch=1, grid=(steps + 1,),
            in_specs=[ANY] * n_deps + [lhs_spec, pl.BlockSpec((t, D), lambda q, p: (0, 0), pipeline_mode=pl.Buffered(1))],
            out_specs=[pl.BlockSpec((per * r, D), lambda q, p: (jnp.maximum(q - 1, 0), 0)),
                       pl.BlockSpec((r, D), lambda q, p: (p[1], 0))] + extra_specs,
            scratch_shapes=scratch),
        out_shape=[jax.ShapeDtypeStruct((n // 2, D), BF)] * 2 + extra_shapes,
        compiler_params=_params(1), name=name)(place, *deps, lhs, rhs)
    return out[:2]


def _wgrad_pair_sum_many(items, place, name, deps=()):
    m = len(items)
    t = items[0][1].shape[0]
    n = items[0][0].shape[0] if items[0][2] else items[0][0].shape[1]
    r = n // N_DEV
    assert (2 * r) % BLK == 0 and r <= WGRAD_SUM_MAX_ROWS
    steps = N_DEV // 2
    chunks = m * steps
    n_deps = len(deps)

    def body(place_ref, *refs):
        refs = refs[n_deps:]
        l_refs, r_first, r_later = refs[:m], refs[m], refs[m + 1:2 * m]
        parts, lands = refs[2 * m:3 * m], refs[3 * m:4 * m]
        res, inbox, r_scr, send_sems, recv_sems, fetch_sems = refs[4 * m:]
        g = pl.program_id(0)
        slot = g % 2
        x, y, c = _position()

        def send(chunk, buf):
            return pltpu.make_async_remote_copy(
                src_ref=res.at[buf, pl.ds(pl.multiple_of((1 - c) * r, 16), r), :], dst_ref=inbox.at[chunk],
                send_sem=send_sems.at[buf], recv_sem=recv_sems.at[chunk], device_id=(x, y, 1 - c), device_id_type=MESH_ID)

        def fetch(k):
            return pltpu.make_async_copy(r_later[k - 1], r_scr.at[k - 1], fetch_sems.at[k - 1])

        @pl.when(g == 0)
        def _():
            for k in range(1, m):
                fetch(k).start()

        @pl.when(g < chunks)
        def _():
            @pl.when(g >= 2)
            def _():
                send(g - 2, slot).wait_send()

            for k, (_, _, transposed) in enumerate(items):
                @pl.when(g // steps == k)
                def _(k=k, transposed=transposed):
                    if k > 0:
                        @pl.when(g == k * steps)
                        def _():
                            fetch(k).wait()
                    rhs = r_first[...] if k == 0 else r_scr[k - 1]
                    res[slot] = (_dot(l_refs[k][...], rhs) if transposed else _dot_tn(l_refs[k][...], rhs)).astype(BF)

            send(g, slot).start()

        @pl.when(g >= 1)
        def _():
            chunk = g - 1
            send(chunk, 1 - slot).wait_recv()
            kept = res[1 - slot, pl.ds(pl.multiple_of(c * r, 16), r), :]
            total = (kept.astype(F32) + inbox[chunk].astype(F32)).astype(BF)
            for k in range(m):
                @pl.when(chunk // steps == k)
                def _(k=k):
                    parts[k][...] = total

                    @pl.when(chunk % steps == place_ref[1])
                    def _():
                        lands[k][...] = total

        @pl.when(g == chunks)
        def _():
            send(g - 2, slot).wait_send()
            send(g - 1, 1 - slot).wait_send()

    def own_steps(k):
        return lambda g: jnp.clip(g - k * steps, 0, steps - 1)

    lhs_specs = []
    for k, (lhs, _, transposed) in enumerate(items):
        at = own_steps(k)
        lhs_specs.append(pl.BlockSpec((2 * r, t), lambda g, p, at=at: (at(g), 0)) if transposed
                         else pl.BlockSpec((t, 2 * r), lambda g, p, at=at: (0, at(g))))
    out = pl.pallas_call(
        body,
        grid_spec=pltpu.PrefetchScalarGridSpec(
            num_scalar_prefetch=1, grid=(chunks + 1,),
            in_specs=[ANY] * n_deps + lhs_specs
            + [pl.BlockSpec((t, D), lambda g, p: (0, 0), pipeline_mode=pl.Buffered(1))] + [ANY] * (m - 1),
            out_specs=[pl.BlockSpec((r, D), lambda g, p, at=own_steps(k): (at(g - 1), 0)) for k in range(m)]
            + [pl.BlockSpec((r, D), lambda g, p: (p[1], 0))] * m,
            scratch_shapes=[pltpu.VMEM((2, 2 * r, D), BF), pltpu.VMEM((chunks, r, D), BF), pltpu.VMEM((m - 1, t, D), BF),
                            pltpu.SemaphoreType.DMA((2,)), pltpu.SemaphoreType.DMA((chunks,)),
                            pltpu.SemaphoreType.DMA((m - 1,))]),
        out_shape=[jax.ShapeDtypeStruct((n // 2, D), BF)] * (2 * m),
        compiler_params=_params(1), name=name)(place, *deps, *[i[0] for i in items], *[i[1] for i in items])
    return [(out[k], out[m + k]) for k in range(m)]


def _sum_blocks(gathered, rows):
    def body(b_ref, o_ref):
        acc = b_ref[0:rows, :]
        for d in range(1, N_DEV):
            acc = acc + b_ref[d * rows:(d + 1) * rows, :]
        o_ref[...] = acc

    return pl.pallas_call(body, out_shape=jax.ShapeDtypeStruct((rows, D), F32), name="small_sum")(gathered)


def _adamw_math(w, g, m, v):
    m = ADAM_B1 * m + (1.0 - ADAM_B1) * g
    v = ADAM_B2 * v + (1.0 - ADAM_B2) * (g * g)
    m_hat = m / (1.0 - ADAM_B1 ** ADAM_STEP)
    v_hat = v / (1.0 - ADAM_B2 ** ADAM_STEP)
    delta = -ADAM_LR * (m_hat / (jnp.sqrt(v_hat) + ADAM_EPS) + ADAM_WD * w)
    return delta, m, v


def _sum_partials(blocks):
    g = blocks[0].astype(F32)
    for blk in blocks[1:]:
        g = g + blk.astype(F32)
    return g


ADAMW_MAX_ROWS = 352


def _reduce_adamw(items, name):
    n = len(items)
    per = -(-max(w.shape[0] for _, w, _, _ in items) // ADAMW_MAX_ROWS)

    def body(*refs):
        for k in range(n):
            r0, r1, r2, r3, w_ref, m_ref, v_ref = refs[7 * k:7 * k + 7]
            g_ref, d_ref, nm_ref, nv_ref = refs[7 * n + 4 * k:7 * n + 4 * k + 4]
            g = _sum_partials([r0[...], r1[...], r2[...], r3[...]])
            g_ref[...] = g
            d_ref[...], nm_ref[...], nv_ref[...] = _adamw_math(w_ref[...], g, m_ref[...], v_ref[...])

    in_specs, out_specs, out_shape, args = [], [], [], []
    for landed, w, m, v in items:
        tr = w.shape[0] // per
        assert tr * per == w.shape[0] and tr % 16 == 0
        tile = _row_tile(tr, D)
        in_specs += [pl.BlockSpec((tr, D), lambda i, q=q: (q * per + i, 0)) for q in range(4)] + [tile] * 3
        out_specs += [tile] * 4
        out_shape += [jax.ShapeDtypeStruct(w.shape, F32)] * 4
        args += [landed] * 4 + [w, m, v]
    out = pl.pallas_call(body, grid=(per,), in_specs=in_specs, out_specs=out_specs, out_shape=out_shape,
                         compiler_params=_params(1), name=name)(*args)
    return [out[4 * k:4 * k + 4] for k in range(n)]


def _adamw_small(w, g, m, v, name):
    def body(w_ref, g_ref, m_ref, v_ref, d_ref, nm_ref, nv_ref):
        d_ref[...], nm_ref[...], nv_ref[...] = _adamw_math(w_ref[...], g_ref[...], m_ref[...], v_ref[...])

    return pl.pallas_call(body, out_shape=[jax.ShapeDtypeStruct(w.shape, F32)] * 3, name=name)(w, g, m, v)


WEIGHTS = ("ffn1_norm", "ffn1_w_in", "ffn1_w_out", "mix_norm", "w_in", "conv_dw_kernel", "conv_dw_bias", "conv_ln_g",
           "conv_ln_b", "conv_w_proj", "q_norm", "k_norm", "attn_sinks", "rel_bias", "attn_w_o", "w_out", "ffn2_norm",
           "ffn2_w_in", "ffn2_w_out")
MATRICES = ("ffn1_w_in", "ffn1_w_out", "w_in", "conv_w_proj", "attn_w_o", "w_out", "ffn2_w_in", "ffn2_w_out")
COLUMN_SHARDED = ("ffn1_w_in", "w_in", "ffn2_w_in")
ROW_VECTORS = ("ffn1_norm", "mix_norm", "conv_dw_bias", "conv_ln_g", "conv_ln_b", "ffn2_norm")
PACKED = (("q_norm", HD), ("k_norm", HD), ("attn_sinks", NQ), ("rel_bias", NBUCKET * NQ))
GATHER = _Exchange(gather=True)
GATHER_ALL = _Exchange(gather=True, all_cores=True)
SCATTER = _Exchange(gather=False)
FIRST = "ffn1_w_in"
GATHER_STAGES = ("ffn1_out", "mix_proj", "mix_merge", "ffn2")
STAGE_GATHER = {"ffn1_out": GATHER, "mix_proj": GATHER, "mix_merge": GATHER, "ffn2": GATHER_ALL}
STAGE_MEMBERS = {"ffn1_out": ("ffn1_w_out",),
                 "mix_proj": ("w_in", "taps"), "mix_merge": ("conv_w_proj", "attn_w_o", "w_out"),
                 "ffn2": ("ffn2_w_in", "ffn2_w_out")}
ROW_PACKED = len(ROW_VECTORS)
ROW_LOSS = ROW_PACKED + 1
ROW_TAPS = 8
PAYLOAD_ROWS = 48


def _pack_small(values, last_row):
    packed = jnp.concatenate([values[k].reshape(-1) for k, _ in PACKED])
    packed = jnp.pad(packed, (0, D - packed.shape[0])).reshape(1, D)
    return jnp.concatenate([values[k].reshape(1, D) for k in ROW_VECTORS] + [packed, last_row], axis=0)


def _unpack_small(rows):
    out = {k: rows[i] for i, k in enumerate(ROW_VECTORS)}
    at = 0
    for k, size in PACKED:
        out[k] = rows[ROW_PACKED, at:at + size]
        at += size
    out["rel_bias"] = out["rel_bias"].reshape(NBUCKET, NQ)
    return out


def kernel(x, ffn1_norm, ffn1_w_in, ffn1_w_out, mix_norm, w_in, conv_dw_kernel, conv_dw_bias, conv_ln_g, conv_ln_b, conv_w_proj, q_norm, k_norm, attn_sinks, rel_bias, attn_w_o, w_out, ffn2_norm, ffn2_w_in, ffn2_w_out, loss_target, m_ffn1_norm, m_ffn1_w_in, m_ffn1_w_out, m_mix_norm, m_w_in, m_conv_dw_kernel, m_conv_dw_bias, m_conv_ln_g, m_conv_ln_b, m_conv_w_proj, m_q_norm, m_k_norm, m_attn_sinks, m_rel_bias, m_attn_w_o, m_w_out, m_ffn2_norm, m_ffn2_w_in, m_ffn2_w_out, v_ffn1_norm, v_ffn1_w_in, v_ffn1_w_out, v_mix_norm, v_w_in, v_conv_dw_kernel, v_conv_dw_bias, v_conv_ln_g, v_conv_ln_b, v_conv_w_proj, v_q_norm, v_k_norm, v_attn_sinks, v_rel_bias, v_attn_w_o, v_w_out, v_ffn2_norm, v_ffn2_w_in, v_ffn2_w_out):
    w = dict(ffn1_norm=ffn1_norm, ffn1_w_in=ffn1_w_in, ffn1_w_out=ffn1_w_out, mix_norm=mix_norm, w_in=w_in,
             conv_dw_kernel=conv_dw_kernel, conv_dw_bias=conv_dw_bias, conv_ln_g=conv_ln_g, conv_ln_b=conv_ln_b,
             conv_w_proj=conv_w_proj, q_norm=q_norm, k_norm=k_norm, attn_sinks=attn_sinks, rel_bias=rel_bias,
             attn_w_o=attn_w_o, w_out=w_out, ffn2_norm=ffn2_norm, ffn2_w_in=ffn2_w_in, ffn2_w_out=ffn2_w_out)
    m = dict(ffn1_norm=m_ffn1_norm, ffn1_w_in=m_ffn1_w_in, ffn1_w_out=m_ffn1_w_out, mix_norm=m_mix_norm, w_in=m_w_in,
             conv_dw_kernel=m_conv_dw_kernel, conv_dw_bias=m_conv_dw_bias, conv_ln_g=m_conv_ln_g, conv_ln_b=m_conv_ln_b,
             conv_w_proj=m_conv_w_proj, q_norm=m_q_norm, k_norm=m_k_norm, attn_sinks=m_attn_sinks, rel_bias=m_rel_bias,
             attn_w_o=m_attn_w_o, w_out=m_w_out, ffn2_norm=m_ffn2_norm, ffn2_w_in=m_ffn2_w_in, ffn2_w_out=m_ffn2_w_out)
    v = dict(ffn1_norm=v_ffn1_norm, ffn1_w_in=v_ffn1_w_in, ffn1_w_out=v_ffn1_w_out, mix_norm=v_mix_norm, w_in=v_w_in,
             conv_dw_kernel=v_conv_dw_kernel, conv_dw_bias=v_conv_dw_bias, conv_ln_g=v_conv_ln_g, conv_ln_b=v_conv_ln_b,
             conv_w_proj=v_conv_w_proj, q_norm=v_q_norm, k_norm=v_k_norm, attn_sinks=v_attn_sinks, rel_bias=v_rel_bias,
             attn_w_o=v_attn_w_o, w_out=v_w_out, ffn2_norm=v_ffn2_norm, ffn2_w_in=v_ffn2_w_in, ffn2_w_out=v_ffn2_w_out)
    px, py, pc = _position()
    me = 4 * px + 2 * py + pc
    place = jnp.stack([pc, 2 * px + py]).astype(jnp.int32)

    rows_of = lambda k, a: a.T if k in COLUMN_SHARDED else a
    me1 = me.astype(jnp.int32).reshape(1)
    rest = tuple(k for k in MATRICES if k != FIRST)
    shard_rows = dict({k: rows_of(k, w[k]).shape[0] for k in MATRICES}, taps=CWP)
    sems_first, _, thru_first, token = _ici_copies_start(
        [[(0, shard_rows[FIRST])]], None, _prep([rows_of(FIRST, w[FIRST])], None, me1, "prep_first"), [GATHER],
        "gather_start_first")
    prepped, mine = _prep([rows_of(k, w[k]) for k in rest], conv_dw_kernel, me1, "prep", deps=[token],
                          swap=(thru_first, [shard_rows[FIRST]], (0,)))
    buffers = dict(zip(rest + ("taps",), prepped))
    landings, sets = [], []
    for stage in GATHER_STAGES:
        sets.append([(len(landings) + i, shard_rows[k]) for i, k in enumerate(STAGE_MEMBERS[stage])])
        landings += list(STAGE_MEMBERS[stage])
    sems, _, land_thru, started = _ici_copies_start(sets, None, [buffers[k] for k in landings],
                                                    [STAGE_GATHER[s] for s in GATHER_STAGES], "gather_start")

    packed = [_pack_small(a, jnp.zeros((1, D), F32)) for a in (w, m, v)]

    def ffn1_up(x, g, after):
        chips = jnp.stack([_chip_index(chip) for chip in [(px, py)] + _other_chips(px, py)]).astype(jnp.int32)
        rows = [shard_rows[FIRST]]
        n, u = _ffn_up_blocks(x, g, None, mine[0], chips[:1], None, "ffn1_up_mine", deps=[started])
        landed = _ici_copies_wait(sems_first[0], rows, None, mine, GATHER, [u, *after, *packed], "gather_wait_first")
        w_in_t, = _d2d_gather(landed, rows, "gather_d2d_first", which=(1, 2, 3))
        n, u = _ffn_up_blocks(None, None, n, w_in_t, chips[1:3], u, "ffn1_up_next")
        (n, u), w1 = weights_of("ffn1_out", (u,), during=functools.partial(
            _ffn_up_blocks, None, None, n, w_in_t, chips[3:], u, "ffn1_up"))
        return n, u, dict(w1, ffn1_w_in=w_in_t)

    def weights_of(stage, after, during=None):
        s = GATHER_STAGES.index(stage)
        rows = [r for _, r in sets[s]]
        landed = _ici_copies_wait(sems[s], rows, None, [land_thru[k] for k, _ in sets[s]], STAGE_GATHER[stage],
                                  list(after), "gather_wait_" + stage)
        if during is not None:
            results, landed = during(swap=(landed, rows))
        elif not STAGE_GATHER[stage].all_cores:
            landed = _d2d_gather(landed, rows, "gather_d2d_" + stage)
        out = dict(zip(STAGE_MEMBERS[stage], landed))
        if "taps" in out:
            taps = out.pop("taps")
            out["conv_dw_kernel"] = jnp.transpose(taps.reshape(N_DEV, CWP, BLK), (1, 0, 2)).reshape(CWP, D)[:CW]
        return out if during is None else (results, out)

    in_flight = []

    def wgrad(lhs, rhs, name, lhs_is_transposed, deps=()):
        return ("summed",) + tuple(_wgrad_pair_sum(lhs, rhs, place, name, lhs_is_transposed=lhs_is_transposed, deps=deps))

    def wgrads(items, name):
        return [("summed",) + pair for pair in _wgrad_pair_sum_many(items, place, name)]

    def grads_done(stage, grads):
        names = list(grads)
        added = []
        for k in names:
            if not isinstance(grads[k], tuple):
                added.append(_pair_exchange_add(grads[k], place, "pair_add_" + k))
            else:
                added.append(grads[k][1:])
        partials = [p for p, _ in added]
        members = [(i, p.shape[0] // 4) for i, p in enumerate(partials)]
        sem, p_thru, l_thru, token = _ici_copies_start([members], partials, [l for _, l in added], [SCATTER],
                                                       "scatter_start_" + stage)
        in_flight.append((stage, names, sem[0], p_thru, l_thru, token))
        return [token]

    small = []

    def small_done(gv, sq):
        payload = jnp.concatenate([_pack_small(gv, sq), jnp.pad(gv["conv_dw_kernel"], ((0, PAYLOAD_ROWS - ROW_TAPS - CW), (0, 0)))],
                                  axis=0)
        mine = lax.dynamic_update_slice_in_dim(lax.empty((N_DEV * PAYLOAD_ROWS, D), F32), payload, me * PAYLOAD_ROWS, axis=0)
        sems, _, thru, token = _ici_copies_start([[(0, PAYLOAD_ROWS)]], None, [mine], [GATHER_ALL], "small_start")
        small.append((sems[0], thru))
        return [token]

    vec = {k: w[k] for k in WEIGHTS if k not in MATRICES and k != "conv_dw_kernel"}
    dx0 = _local_step(x[0], loss_target[0], vec, ffn1_up, weights_of, wgrad, wgrads, grads_done, small_done)
    gathered, = _ici_copies_wait(small[0][0], [PAYLOAD_ROWS], None, small[0][1], GATHER_ALL, [in_flight[-1][-1]], "small_wait")
    total = _sum_blocks(gathered, PAYLOAD_ROWS)
    loss = (0.5 / D) * jnp.sum(total[ROW_LOSS])

    grads, delta, new_m, new_v = {}, {}, {}, {}
    after, pending = [total], []
    for stage, names, sem, p_thru, l_thru, _ in in_flight:
        landed = _ici_copies_wait(sem, [p.shape[0] // 4 for p in p_thru], p_thru, l_thru, SCATTER, after,
                                  "scatter_wait_" + stage)
        pending += zip(names, landed)
        after = list(landed)
        if stage == in_flight[-2][0]:
            continue
        outs = _reduce_adamw([(buf, rows_of(k, w[k]), rows_of(k, m[k]), rows_of(k, v[k])) for k, buf in pending],
                             "adamw_" + stage)
        for (k, _), out in zip(pending, outs):
            grads[k], delta[k], new_m[k], new_v[k] = [rows_of(k, a) for a in out]
        after, pending = [out[1] for out in outs], []
    d8, m8, v8 = _adamw_small(packed[0], total[:ROW_TAPS], packed[1], packed[2], "adamw_small")
    grads.update(_unpack_small(total[:ROW_TAPS]))
    delta.update(_unpack_small(d8))
    new_m.update(_unpack_small(m8))
    new_v.update(_unpack_small(v8))
    k = "conv_dw_kernel"
    grads[k] = lax.dynamic_slice_in_dim(total[ROW_TAPS:ROW_TAPS + CW], me * BLK, BLK, axis=1)
    delta[k], new_m[k], new_v[k] = _adamw_small(w[k], grads[k], m[k], v[k], "adamw_taps")

    return (loss, dx0[None], *[grads[k] for k in WEIGHTS], *[delta[k] for k in WEIGHTS],
            *[new_m[k] for k in WEIGHTS], *[new_v[k] for k in WEIGHTS])
```
